```python
import jax, jax.numpy as jnp
from jax import lax
import numpy as np

D_MODEL = 1024
BATCH = 8
SEQ = 4096
DEPTH = 2

N_MIXERS = 2
ATTN_PAIRS = ((128, 1), (512, 4), (2048, 16))
N_ATTN_GROUPS = len(ATTN_PAIRS)
HEADS_PER_GROUP = 8
HEAD_DIM = D_MODEL // HEADS_PER_GROUP
ATTN_WIDTH = HEADS_PER_GROUP * HEAD_DIM
QKV_WIDTH = N_ATTN_GROUPS * 3 * ATTN_WIDTH
N_ALIBI_HEADS = N_ATTN_GROUPS * HEADS_PER_GROUP
Q_BLOCK = 128
POOL_WINDOWS = (2, 4, 8, 16)
POOL_GROUPS = len(POOL_WINDOWS)
POOL_GROUP_DIM = D_MODEL // POOL_GROUPS
D_FF = ((8 * D_MODEL + 3 * 256 - 1) // (3 * 256)) * 256
N_ATTN_LAYERS = (DEPTH + 1) // 2
N_POOL_LAYERS = DEPTH // 2
RMS_EPS = 1e-6

kernel_name = "dilated_attn_pool_hybrid_trunk"


def rmsnorm(x, g):
    xf = x.astype(jnp.float32)
    y = xf * lax.rsqrt(jnp.mean(xf * xf, axis=-1, keepdims=True) + RMS_EPS)
    return (y * g.astype(jnp.float32)).astype(x.dtype)


def alibi_slopes():
    n = N_ALIBI_HEADS
    return jnp.exp2(-8.0 * jnp.arange(1, n + 1, dtype=jnp.float32) / n)


def dilated_window_attention(q, k, v, window, dil, slopes):
    B, S, H, D = q.shape
    L = S // dil
    w_sub = window // dil
    nb = -(-L // Q_BLOCK)
    Lp = nb * Q_BLOCK
    pad = Lp - L
    Bd = B * dil

    def fold(a):
        return a.reshape(B, L, dil, H, D).transpose(0, 2, 1, 3, 4).reshape(Bd, L, H, D)

    def band(a):
        a = jnp.pad(a, ((0, 0), (Q_BLOCK, pad), (0, 0), (0, 0))).reshape(Bd, nb + 1, Q_BLOCK, H, D)
        return jnp.concatenate([a[:, :-1], a[:, 1:]], axis=2)

    qb = jnp.pad(fold(q), ((0, 0), (0, pad), (0, 0), (0, 0))).reshape(Bd, nb, Q_BLOCK, H, D)
    kb = band(fold(k))
    vb = band(fold(v))

    s = jnp.einsum('bnqhd,bnkhd->bnhqk', qb, kb) * (D ** -0.5)
    qi = jnp.arange(Q_BLOCK)[:, None]
    ki = jnp.arange(2 * Q_BLOCK)[None, :]
    delta = Q_BLOCK + qi - ki
    blk = jnp.arange(nb)[:, None, None]
    key_idx = blk * Q_BLOCK + ki[None] - Q_BLOCK
    valid = (delta >= 0)[None] & (delta <= w_sub)[None] & (key_idx >= 0)
    bias = -slopes[:, None, None] * (delta * dil).astype(jnp.float32)[None]
    s = s + bias[None, None]
    s = jnp.where(valid[None, :, None], s, -jnp.inf)
    m = jnp.max(s, axis=-1, keepdims=True)
    p = jnp.exp(s - m)
    den = jnp.sum(p, axis=-1, keepdims=True)
    o = jnp.einsum('bnhqk,bnkhd->bnqhd', p / den, vb)
    lse = (m + jnp.log(den))[..., 0].transpose(0, 1, 3, 2)

    o = o.reshape(Bd, Lp, H, D)[:, :L]
    o = o.reshape(B, dil, L, H, D).transpose(0, 2, 1, 3, 4).reshape(B, S, H, D)
    lse = lse.reshape(Bd, Lp, H)[:, :L]
    lse = lse.reshape(B, dil, L, H).transpose(0, 2, 1, 3).reshape(B, S, H)
    return o, lse


def dilated_attention_mixer(h, w_qkv, w_out):
    B, S, _ = h.shape
    qkv = (h @ w_qkv).astype(jnp.float32).reshape(B, S, N_ATTN_GROUPS, 3, HEADS_PER_GROUP, HEAD_DIM)
    slopes = alibi_slopes().reshape(N_ATTN_GROUPS, HEADS_PER_GROUP)
    outs, lses = [], []
    for g, (window, dil) in enumerate(ATTN_PAIRS):
        o, lse = dilated_window_attention(qkv[:, :, g, 0], qkv[:, :, g, 1], qkv[:, :, g, 2], window, dil, slopes[g])
        outs.append(o)
        lses.append(lse)
    o = jnp.stack(outs, axis=0)
    wts = jax.nn.softmax(jnp.stack(lses, axis=0), axis=0)
    o = jnp.sum(wts[..., None] * o, axis=0).reshape(B, S, ATTN_WIDTH)
    return o.astype(h.dtype) @ w_out


def trailing_mean(u, w):
    B, S, C = u.shape
    c = jnp.cumsum(u, axis=1)
    shifted = jnp.concatenate([jnp.zeros((B, w, C), u.dtype), c[:, :S - w]], axis=1)
    count = jnp.minimum(jnp.arange(1, S + 1), w).astype(jnp.float32)[None, :, None]
    return (c - shifted) / count


def pooling_mixer(h, w_in, w_group, scale):
    B, S, _ = h.shape
    u = (h @ w_in).astype(jnp.float32).reshape(B, S, POOL_GROUPS, POOL_GROUP_DIM)
    ys = [trailing_mean(u[:, :, g], w) - u[:, :, g] for g, w in enumerate(POOL_WINDOWS)]
    y = jnp.stack(ys, axis=2)
    y = jnp.einsum('bsgc,gcd->bsgd', y, w_group.astype(jnp.float32)).reshape(B, S, D_MODEL)
    return (y * scale.astype(jnp.float32)).astype(h.dtype)


def swiglu_ffn(h, w_gate_up, w_down):
    gu = h @ w_gate_up
    gate, up = gu[..., :D_FF], gu[..., D_FF:]
    return (jax.nn.silu(gate) * up) @ w_down


def _fwd_setup_inputs(seed: int = 0) -> dict:
    key = jax.random.key(seed)
    ks = jax.random.split(key, 13)
    f32 = jnp.float32
    nA, nP = N_ATTN_LAYERS, N_POOL_LAYERS
    return {
        "x": jax.random.normal(ks[0], (BATCH, SEQ, D_MODEL), f32),
        "attn_norm": 1.0 + 0.02 * jax.random.normal(ks[1], (nA, D_MODEL), f32),
        "w_qkv": jax.random.normal(ks[2], (nA, D_MODEL, QKV_WIDTH), f32) * D_MODEL ** -0.5,
        "w_attn_out": jax.random.normal(ks[3], (nA, ATTN_WIDTH, D_MODEL), f32) * ATTN_WIDTH ** -0.5,
        "pool_norm": 1.0 + 0.02 * jax.random.normal(ks[4], (nP, D_MODEL), f32),
        "w_pool_in": jax.random.normal(ks[5], (nP, D_MODEL, D_MODEL), f32) * D_MODEL ** -0.5,
        "w_pool_group": jax.random.normal(ks[6], (nP, POOL_GROUPS, POOL_GROUP_DIM, POOL_GROUP_DIM), f32) * POOL_GROUP_DIM ** -0.5,
        "pool_scale": 0.5 + 0.1 * jax.random.normal(ks[7], (nP, D_MODEL), f32),
        "ffn_norm": 1.0 + 0.02 * jax.random.normal(ks[8], (DEPTH, D_MODEL), f32),
        "w_ffn_gate_up": jax.random.normal(ks[9], (DEPTH, D_MODEL, 2 * D_FF), f32) * D_MODEL ** -0.5,
        "w_ffn_down": jax.random.normal(ks[10], (DEPTH, D_FF, D_MODEL), f32) * D_FF ** -0.5,
        "final_norm": 1.0 + 0.02 * jax.random.normal(ks[11], (D_MODEL,), f32),
    }


def _fwd_reference(x, attn_norm, w_qkv, w_attn_out, pool_norm, w_pool_in, w_pool_group, pool_scale,
              ffn_norm, w_ffn_gate_up, w_ffn_down, final_norm):
    for i in range(DEPTH):
        j = i // N_MIXERS
        if i % N_MIXERS == 0:
            x = x + dilated_attention_mixer(rmsnorm(x, attn_norm[j]), w_qkv[j], w_attn_out[j])
        else:
            x = x + pooling_mixer(rmsnorm(x, pool_norm[j]), w_pool_in[j], w_pool_group[j], pool_scale[j])
        x = x + swiglu_ffn(rmsnorm(x, ffn_norm[i]), w_ffn_gate_up[i], w_ffn_down[i])
    return rmsnorm(x, final_norm)


import jax as _jax
import jax.numpy as _jnp

TWIN_FORMAT = 'train_step'
FWD_PARAMS = ['x', 'attn_norm', 'w_qkv', 'w_attn_out', 'pool_norm', 'w_pool_in', 'w_pool_group', 'pool_scale', 'ffn_norm', 'w_ffn_gate_up', 'w_ffn_down', 'final_norm']
TWIN_WEIGHTS = ['attn_norm', 'w_qkv', 'w_attn_out', 'pool_norm', 'w_pool_in', 'w_pool_group', 'pool_scale', 'ffn_norm', 'w_ffn_gate_up', 'w_ffn_down', 'final_norm']
TWIN_DIFF_INPUT = 'x'
TWIN_INPUTS = ['x', 'attn_norm', 'w_qkv', 'w_attn_out', 'pool_norm', 'w_pool_in', 'w_pool_group', 'pool_scale', 'ffn_norm', 'w_ffn_gate_up', 'w_ffn_down', 'final_norm', 'loss_target', 'm_attn_norm', 'm_w_qkv', 'm_w_attn_out', 'm_pool_norm', 'm_w_pool_in', 'm_w_pool_group', 'm_pool_scale', 'm_ffn_norm', 'm_w_ffn_gate_up', 'm_w_ffn_down', 'm_final_norm', 'v_attn_norm', 'v_w_qkv', 'v_w_attn_out', 'v_pool_norm', 'v_w_pool_in', 'v_w_pool_group', 'v_pool_scale', 'v_ffn_norm', 'v_w_ffn_gate_up', 'v_w_ffn_down', 'v_final_norm']
TWIN_OUTPUTS = ['loss', 'grad_x', 'grad_attn_norm', 'grad_w_qkv', 'grad_w_attn_out', 'grad_pool_norm', 'grad_w_pool_in', 'grad_w_pool_group', 'grad_pool_scale', 'grad_ffn_norm', 'grad_w_ffn_gate_up', 'grad_w_ffn_down', 'grad_final_norm', 'delta_attn_norm', 'delta_w_qkv', 'delta_w_attn_out', 'delta_pool_norm', 'delta_w_pool_in', 'delta_w_pool_group', 'delta_pool_scale', 'delta_ffn_norm', 'delta_w_ffn_gate_up', 'delta_w_ffn_down', 'delta_final_norm', 'new_m_attn_norm', 'new_m_w_qkv', 'new_m_w_attn_out', 'new_m_pool_norm', 'new_m_w_pool_in', 'new_m_w_pool_group', 'new_m_pool_scale', 'new_m_ffn_norm', 'new_m_w_ffn_gate_up', 'new_m_w_ffn_down', 'new_m_final_norm', 'new_v_attn_norm', 'new_v_w_qkv', 'new_v_w_attn_out', 'new_v_pool_norm', 'new_v_w_pool_in', 'new_v_w_pool_group', 'new_v_pool_scale', 'new_v_ffn_norm', 'new_v_w_ffn_gate_up', 'new_v_w_ffn_down', 'new_v_final_norm']
TWIN_LEAF_KINDS = {'loss': 'loss', 'grad_x': 'grad_x', 'grad_attn_norm': 'grad_w', 'grad_w_qkv': 'grad_w', 'grad_w_attn_out': 'grad_w', 'grad_pool_norm': 'grad_w', 'grad_w_pool_in': 'grad_w', 'grad_w_pool_group': 'grad_w', 'grad_pool_scale': 'grad_w', 'grad_ffn_norm': 'grad_w', 'grad_w_ffn_gate_up': 'grad_w', 'grad_w_ffn_down': 'grad_w', 'grad_final_norm': 'grad_w', 'delta_attn_norm': 'delta_w', 'delta_w_qkv': 'delta_w', 'delta_w_attn_out': 'delta_w', 'delta_pool_norm': 'delta_w', 'delta_w_pool_in': 'delta_w', 'delta_w_pool_group': 'delta_w', 'delta_pool_scale': 'delta_w', 'delta_ffn_norm': 'delta_w', 'delta_w_ffn_gate_up': 'delta_w', 'delta_w_ffn_down': 'delta_w', 'delta_final_norm': 'delta_w', 'new_m_attn_norm': 'new_m', 'new_m_w_qkv': 'new_m', 'new_m_w_attn_out': 'new_m', 'new_m_pool_norm': 'new_m', 'new_m_w_pool_in': 'new_m', 'new_m_w_pool_group': 'new_m', 'new_m_pool_scale': 'new_m', 'new_m_ffn_norm': 'new_m', 'new_m_w_ffn_gate_up': 'new_m', 'new_m_w_ffn_down': 'new_m', 'new_m_final_norm': 'new_m', 'new_v_attn_norm': 'new_v', 'new_v_w_qkv': 'new_v', 'new_v_w_attn_out': 'new_v', 'new_v_pool_norm': 'new_v', 'new_v_w_pool_in': 'new_v', 'new_v_w_pool_group': 'new_v', 'new_v_pool_scale': 'new_v', 'new_v_ffn_norm': 'new_v', 'new_v_w_ffn_gate_up': 'new_v', 'new_v_w_ffn_down': 'new_v', 'new_v_final_norm': 'new_v'}


def _forward(args):
    return _fwd_reference(*[args[k] for k in FWD_PARAMS])


def _output_shape():
    out = _jax.eval_shape(lambda: _forward(_fwd_setup_inputs(0)))
    return out.shape, out.dtype

N_MICROBATCH = 1
ADAM_LR = 0.001
ADAM_B1 = 0.9
ADAM_B2 = 0.999
ADAM_EPS = 1e-08
ADAM_WD = 0.01
ADAM_STEP = 10
PER_EXAMPLE_BATCH_AXIS = {'x': 0, 'loss_target': 0}
SHARED_INPUTS = []
_WEIGHT_DTYPES = {'attn_norm': _jnp.float32, 'w_qkv': _jnp.float32, 'w_attn_out': _jnp.float32, 'pool_norm': _jnp.float32, 'w_pool_in': _jnp.float32, 'w_pool_group': _jnp.float32, 'pool_scale': _jnp.float32, 'ffn_norm': _jnp.float32, 'w_ffn_gate_up': _jnp.float32, 'w_ffn_down': _jnp.float32, 'final_norm': _jnp.float32}
MOMENT_SCALE = {'attn_norm': 1.061452e-01, 'w_qkv': 3.535737e-02, 'w_attn_out': 6.895835e-02, 'pool_norm': 6.680790e-02, 'w_pool_in': 6.768704e-02, 'w_pool_group': 6.743902e-02, 'pool_scale': 1.826029e-01, 'ffn_norm': 1.309985e-01, 'w_ffn_gate_up': 5.319807e-02, 'w_ffn_down': 8.691447e-02, 'final_norm': 3.205909e+01}


def _to_microbatches(a, axis):
    t = _jnp.moveaxis(a, axis, 0)
    t = t.reshape((N_MICROBATCH, t.shape[0] // N_MICROBATCH) + t.shape[1:])
    return _jnp.moveaxis(t, 1, axis + 1)


def setup_inputs(seed: int = 0) -> dict:
    inp = _fwd_setup_inputs(seed)
    key = _jax.random.fold_in(_jax.random.key(seed), 7919)
    shape, _ = _output_shape()
    out = dict(inp)
    out["loss_target"] = _jax.random.normal(_jax.random.fold_in(key, 0), shape, _jnp.float32)
    for i, name in enumerate(TWIN_WEIGHTS):
        w = inp[name].astype(_jnp.float32)
        if MOMENT_SCALE is None:
            s = _jnp.sqrt(_jnp.mean(_jnp.square(w)) + 1e-30)
        else:
            s = MOMENT_SCALE[name]
        km, kv = _jax.random.split(_jax.random.fold_in(key, i + 1))
        out[name] = w
        out["m_" + name] = s * _jax.random.normal(km, w.shape, _jnp.float32)
        out["v_" + name] = (s * s) * _jax.random.uniform(kv, w.shape, _jnp.float32, 0.5, 1.5)
    if N_MICROBATCH > 1:
        for name, axis in PER_EXAMPLE_BATCH_AXIS.items():
            out[name] = _to_microbatches(out[name], axis)
    return {'x': out['x'], 'attn_norm': out['attn_norm'], 'w_qkv': out['w_qkv'], 'w_attn_out': out['w_attn_out'], 'pool_norm': out['pool_norm'], 'w_pool_in': out['w_pool_in'], 'w_pool_group': out['w_pool_group'], 'pool_scale': out['pool_scale'], 'ffn_norm': out['ffn_norm'], 'w_ffn_gate_up': out['w_ffn_gate_up'], 'w_ffn_down': out['w_ffn_down'], 'final_norm': out['final_norm'], 'loss_target': out['loss_target'], 'm_attn_norm': out['m_attn_norm'], 'm_w_qkv': out['m_w_qkv'], 'm_w_attn_out': out['m_w_attn_out'], 'm_pool_norm': out['m_pool_norm'], 'm_w_pool_in': out['m_w_pool_in'], 'm_w_pool_group': out['m_w_pool_group'], 'm_pool_scale': out['m_pool_scale'], 'm_ffn_norm': out['m_ffn_norm'], 'm_w_ffn_gate_up': out['m_w_ffn_gate_up'], 'm_w_ffn_down': out['m_w_ffn_down'], 'm_final_norm': out['m_final_norm'], 'v_attn_norm': out['v_attn_norm'], 'v_w_qkv': out['v_w_qkv'], 'v_w_attn_out': out['v_w_attn_out'], 'v_pool_norm': out['v_pool_norm'], 'v_w_pool_in': out['v_w_pool_in'], 'v_w_pool_group': out['v_w_pool_group'], 'v_pool_scale': out['v_pool_scale'], 'v_ffn_norm': out['v_ffn_norm'], 'v_w_ffn_gate_up': out['v_w_ffn_gate_up'], 'v_w_ffn_down': out['v_w_ffn_down'], 'v_final_norm': out['v_final_norm']}


def _loss(weights, diff, rest, loss_target):
    with _jax.named_scope("forward"):
        args = {**rest, TWIN_DIFF_INPUT: diff, **{k: w.astype(_WEIGHT_DTYPES[k]) for k, w in weights.items()}}
        y = _forward(args)
    with _jax.named_scope("loss_head"):
        err = _jnp.square(y.astype(_jnp.float32) - loss_target)
        return 0.5 * _jnp.sum(_jnp.mean(err, axis=-1)) if err.ndim else 0.5 * err


def _adamw(w, g, m, v):
    m = ADAM_B1 * m + (1.0 - ADAM_B1) * g
    v = ADAM_B2 * v + (1.0 - ADAM_B2) * _jnp.square(g)
    m_hat = m / (1.0 - ADAM_B1 ** ADAM_STEP)
    v_hat = v / (1.0 - ADAM_B2 ** ADAM_STEP)
    delta = -ADAM_LR * (m_hat / (_jnp.sqrt(v_hat) + ADAM_EPS) + ADAM_WD * w)
    return delta, m, v


def reference(x, attn_norm, w_qkv, w_attn_out, pool_norm, w_pool_in, w_pool_group, pool_scale, ffn_norm, w_ffn_gate_up, w_ffn_down, final_norm, loss_target, m_attn_norm, m_w_qkv, m_w_attn_out, m_pool_norm, m_w_pool_in, m_w_pool_group, m_pool_scale, m_ffn_norm, m_w_ffn_gate_up, m_w_ffn_down, m_final_norm, v_attn_norm, v_w_qkv, v_w_attn_out, v_pool_norm, v_w_pool_in, v_w_pool_group, v_pool_scale, v_ffn_norm, v_w_ffn_gate_up, v_w_ffn_down, v_final_norm):
    given = dict(x=x, attn_norm=attn_norm, w_qkv=w_qkv, w_attn_out=w_attn_out, pool_norm=pool_norm, w_pool_in=w_pool_in, w_pool_group=w_pool_group, pool_scale=pool_scale, ffn_norm=ffn_norm, w_ffn_gate_up=w_ffn_gate_up, w_ffn_down=w_ffn_down, final_norm=final_norm, loss_target=loss_target, m_attn_norm=m_attn_norm, m_w_qkv=m_w_qkv, m_w_attn_out=m_w_attn_out, m_pool_norm=m_pool_norm, m_w_pool_in=m_w_pool_in, m_w_pool_group=m_w_pool_group, m_pool_scale=m_pool_scale, m_ffn_norm=m_ffn_norm, m_w_ffn_gate_up=m_w_ffn_gate_up, m_w_ffn_down=m_w_ffn_down, m_final_norm=m_final_norm, v_attn_norm=v_attn_norm, v_w_qkv=v_w_qkv, v_w_attn_out=v_w_attn_out, v_pool_norm=v_pool_norm, v_w_pool_in=v_w_pool_in, v_w_pool_group=v_w_pool_group, v_pool_scale=v_pool_scale, v_ffn_norm=v_ffn_norm, v_w_ffn_gate_up=v_w_ffn_gate_up, v_w_ffn_down=v_w_ffn_down, v_final_norm=v_final_norm)
    weights = {n: given[n] for n in TWIN_WEIGHTS}
    shared = {n: given[n] for n in SHARED_INPUTS}
    per_example = {n: given[n] for n in ['x']}
    grad_fn = _jax.value_and_grad(_loss, argnums=(0, 1))

    def one_microbatch(ex, loss_target):
        ex = dict(ex)
        diff = ex.pop(TWIN_DIFF_INPUT)
        return grad_fn(weights, diff, {**shared, **ex}, loss_target)

    if N_MICROBATCH == 1:
        loss, (grad_w, grad_x) = one_microbatch(per_example, given["loss_target"])
    else:
        def body(carry, xs):
            loss_sum, grad_sum = carry
            l_k, (gw_k, gx_k) = one_microbatch(xs[0], xs[1])
            with _jax.named_scope("update"):
                return (loss_sum + l_k, _jax.tree.map(_jnp.add, grad_sum, gw_k)), gx_k

        init = (_jnp.zeros((), _jnp.float32), _jax.tree.map(_jnp.zeros_like, weights))
        (loss, grad_w), grad_x = _jax.lax.scan(body, init, (per_example, given["loss_target"]))
    with _jax.named_scope("update"):
        delta_w, new_m, new_v = {}, {}, {}
        for n in TWIN_WEIGHTS:
            delta_w[n], new_m[n], new_v[n] = _adamw(weights[n], grad_w[n], given["m_" + n], given["v_" + n])
    return (loss, grad_x, *[grad_w[n] for n in TWIN_WEIGHTS], *[delta_w[n] for n in TWIN_WEIGHTS],
            *[new_m[n] for n in TWIN_WEIGHTS], *[new_v[n] for n in TWIN_WEIGHTS])
```

```python
import jax
import jax.numpy as jnp
from jax import lax
from jax.experimental import pallas as pl
from jax.experimental.pallas import tpu as pltpu

F32 = jnp.float32
BF16 = jnp.bfloat16

D = 1024
NDEV = 8
HEADS = 8
HD = 128
QB = 128
NGROUPS = 3
DILS = (1, 4, 16)
DFF = 2816
HCH = 1408
POOL_G = 4
PGD = 256
RMS_EPS = 1e-6
NEG = -1e30

ADAM_LR = 0.001
ADAM_B1 = 0.9
ADAM_B2 = 0.999
ADAM_EPS = 1e-08
ADAM_WD = 0.01
ADAM_STEP = 10

VMEM_LIMIT = 52 * 1024 * 1024
MESH_AXES = ("x", "y", "c")

SECTIONS = (("qkv", 1152), ("wo", 128), ("wpi", 128), ("gu0", 704), ("gu1", 704),
            ("d0", 352), ("d1", 352), ("pg", 32))
LOC_OFF = {}
GLB_OFF = {}
_o = 0
for _n, _r in SECTIONS:
    LOC_OFF[_n] = _o
    GLB_OFF[_n] = _o * NDEV
    _o += _r
PACK_ROWS = _o
GLB_ROWS = PACK_ROWS * NDEV
SEC_ROWS = dict(SECTIONS)


def _cparams(n_grid):
    return pltpu.CompilerParams(dimension_semantics=("arbitrary",) * n_grid, vmem_limit_bytes=VMEM_LIMIT)


def _shard_pos(name, dev):
    n = SEC_ROWS[name]
    if name in ("gu0", "gu1"):
        return ((dev % 4) // 2) * (2 * HCH) + (dev // 4) * HCH + (dev % 2) * n
    return dev * n


def _mm(a, b, *, mode, M, N, K, tm, tn, tk, out_dtype, name, nbat=1, a_off=(0, 0), b_off=(0, 0),
        a_boff=(0, 0), b_boff=(0, 0), res=None, out_2d_rows=None):
    nm, nn, nk = M // tm, N // tn, K // tk
    assert nm * tm == M and nn * tn == N and nk * tk == K
    if mode == "nn":
        a_bs, b_bs = (tm, tk), (tk, tn)
        a_ix = lambda i, j, k: (i, k)
        b_ix = lambda i, j, k: (k, j)
        dims = (((1,), (0,)), ((), ()))
    elif mode == "nt":
        a_bs, b_bs = (tm, tk), (tn, tk)
        a_ix = lambda i, j, k: (i, k)
        b_ix = lambda i, j, k: (j, k)
        dims = (((1,), (1,)), ((), ()))
    else:
        a_bs, b_bs = (tk, tm), (tk, tn)
        a_ix = lambda i, j, k: (k, i)
        b_ix = lambda i, j, k: (k, j)
        dims = (((0,), (0,)), ((), ()))

    def spec(arr, bs, ix, off, boff):
        if arr.ndim == 3:
            return pl.BlockSpec((None,) + bs, lambda bb, i, j, k: (bb,) + ix(i, j, k))

        def im(bb, i, j, k):
            r, c = ix(i, j, k)
            return (r + off[0] + bb * boff[0], c + off[1] + bb * boff[1])
        return pl.BlockSpec(bs, im)

    in_specs = [spec(a, a_bs, a_ix, a_off, a_boff), spec(b, b_bs, b_ix, b_off, b_boff)]
    args = [a, b]
    if res is not None:
        in_specs.append(pl.BlockSpec((tm, tn), lambda bb, i, j, k: (i, j)))
        args.append(res)
    if nbat > 1 and out_2d_rows is None:
        out_shape = jax.ShapeDtypeStruct((nbat, M, N), out_dtype)
        out_spec = pl.BlockSpec((None, tm, tn), lambda bb, i, j, k: (bb, i, j))
    else:
        rows = M if out_2d_rows is None else out_2d_rows
        out_shape = jax.ShapeDtypeStruct((rows, N), out_dtype)
        out_spec = pl.BlockSpec((tm, tn), lambda bb, i, j, k: (i + bb * nm, j))
    has_res = res is not None

    def kern(*refs):
        a_ref, b_ref = refs[0], refs[1]
        res_ref = refs[2] if has_res else None
        o_ref = refs[3] if has_res else refs[2]
        av = a_ref[...]
        bv = b_ref[...]
        if av.dtype != BF16:
            av = av.astype(BF16)
        if bv.dtype != BF16:
            bv = bv.astype(BF16)
        part = lax.dot_general(av, bv, dims, preferred_element_type=F32)

        def write(val):
            if has_res:
                val = val + res_ref[...]
            o_ref[...] = val.astype(out_dtype)

        if nk == 1:
            write(part)
        else:
            acc_ref = refs[-1]
            k = pl.program_id(3)

            @pl.when(k == 0)
            def _():
                acc_ref[...] = part

            @pl.when(k > 0)
            def _():
                acc_ref[...] += part

            @pl.when(k == nk - 1)
            def _():
                write(acc_ref[...])

    scratch = [pltpu.VMEM((tm, tn), F32)] if nk > 1 else []
    return pl.pallas_call(
        kern, grid=(nbat, nm, nn, nk), in_specs=in_specs, out_specs=out_spec, out_shape=out_shape,
        scratch_shapes=scratch, compiler_params=_cparams(4), name=name)(*args)


def _rms_fwd(x, g, name):
    S = x.shape[0]
    tr = 512

    def kern(x_ref, g_ref, h_ref):
        xv = x_ref[...]
        r = lax.rsqrt(jnp.mean(xv * xv, axis=-1, keepdims=True) + RMS_EPS)
        h_ref[...] = (xv * r * g_ref[...]).astype(BF16)

    return pl.pallas_call(
        kern, grid=(S // tr,),
        in_specs=[pl.BlockSpec((tr, D), lambda i: (i, 0)), pl.BlockSpec((1, D), lambda i: (0, 0))],
        out_specs=pl.BlockSpec((tr, D), lambda i: (i, 0)),
        out_shape=jax.ShapeDtypeStruct((S, D), BF16), compiler_params=_cparams(1), name=name)(x, g)


def _rms_bwd(dh, x, g, dres, name):
    S = x.shape[0]
    tr = 512

    def kern(dh_ref, x_ref, g_ref, dres_ref, dx_ref, dg_ref):
        i = pl.program_id(0)
        xv = x_ref[...]
        dhv = dh_ref[...].astype(F32)
        r = lax.rsqrt(jnp.mean(xv * xv, axis=-1, keepdims=True) + RMS_EPS)
        xhat = xv * r
        gy = dhv * g_ref[...]
        dx_ref[...] = dres_ref[...] + r * (gy - xhat * jnp.mean(gy * xhat, axis=-1, keepdims=True))
        part = jnp.sum(dhv * xhat, axis=0, keepdims=True)

        @pl.when(i == 0)
        def _():
            dg_ref[...] = part

        @pl.when(i > 0)
        def _():
            dg_ref[...] += part

    row = pl.BlockSpec((tr, D), lambda i: (i, 0))
    vec = pl.BlockSpec((1, D), lambda i: (0, 0))
    return pl.pallas_call(
        kern, grid=(S // tr,), in_specs=[row, row, vec, row], out_specs=[row, vec],
        out_shape=[jax.ShapeDtypeStruct((S, D), F32), jax.ShapeDtypeStruct((1, D), F32)],
        compiler_params=_cparams(1), name=name)(dh, x, g, dres)


def _loss_head(x, g, tgt):
    S = x.shape[0]
    tr = 512

    def kern(x_ref, g_ref, t_ref, dx_ref, dg_ref, ls_ref):
        i = pl.program_id(0)
        xv = x_ref[...]
        gv = g_ref[...]
        r = lax.rsqrt(jnp.mean(xv * xv, axis=-1, keepdims=True) + RMS_EPS)
        xhat = xv * r
        e = xhat * gv - t_ref[...]
        dy = e * (1.0 / D)
        gy = dy * gv
        dx_ref[...] = r * (gy - xhat * jnp.mean(gy * xhat, axis=-1, keepdims=True))
        dgp = jnp.sum(dy * xhat, axis=0, keepdims=True)
        lsp = jnp.sum(e * e, axis=0, keepdims=True)

        @pl.when(i == 0)
        def _():
            dg_ref[...] = dgp
            ls_ref[...] = lsp

        @pl.when(i > 0)
        def _():
            dg_ref[...] += dgp
            ls_ref[...] += lsp

    row = pl.BlockSpec((tr, D), lambda i: (i, 0))
    vec = pl.BlockSpec((1, D), lambda i: (0, 0))
    return pl.pallas_call(
        kern, grid=(S // tr,), in_specs=[row, vec, row], out_specs=[row, vec, vec],
        out_shape=[jax.ShapeDtypeStruct((S, D), F32), jax.ShapeDtypeStruct((1, D), F32),
                   jax.ShapeDtypeStruct((1, D), F32)],
        compiler_params=_cparams(1), name="loss_head")(x, g, tgt)


def _ffn_up(h, G, sec, name):
    S = h.shape[0]
    tm = 512
    nj = DFF // HCH
    boff = GLB_OFF[sec] // (2 * HCH)

    def kern(h_ref, w_ref, gu_ref, act_ref):
        gu = lax.dot_general(h_ref[...], w_ref[...], (((1,), (1,)), ((), ())), preferred_element_type=F32)
        gu_ref[...] = gu.astype(BF16)
        gate = gu[:, :HCH]
        up = gu[:, HCH:]
        act_ref[...] = (gate * jax.nn.sigmoid(gate) * up).astype(BF16)

    return pl.pallas_call(
        kern, grid=(nj, S // tm),
        in_specs=[pl.BlockSpec((tm, D), lambda j, i: (i, 0)),
                  pl.BlockSpec((2 * HCH, D), lambda j, i: (boff + j, 0))],
        out_specs=[pl.BlockSpec((tm, 2 * HCH), lambda j, i: (i, j)),
                   pl.BlockSpec((tm, HCH), lambda j, i: (i, j))],
        out_shape=[jax.ShapeDtypeStruct((S, 2 * DFF), BF16), jax.ShapeDtypeStruct((S, DFF), BF16)],
        compiler_params=_cparams(2), name=name)(h, G)


def _ffn_down_bwd(dx, G, sec, gu, name):
    S = dx.shape[0]
    tm = 512
    nj = DFF // HCH
    boff = GLB_OFF[sec] // HCH

    def kern(dx_ref, w_ref, gu_ref, o_ref):
        dact = lax.dot_general(dx_ref[...].astype(BF16), w_ref[...], (((1,), (1,)), ((), ())),
                               preferred_element_type=F32)
        gate = gu_ref[:, :HCH].astype(F32)
        up = gu_ref[:, HCH:].astype(F32)
        sig = jax.nn.sigmoid(gate)
        silu = gate * sig
        o_ref[:, :HCH] = (dact * up * (sig * (1.0 + gate * (1.0 - sig)))).astype(BF16)
        o_ref[:, HCH:] = (dact * silu).astype(BF16)

    return pl.pallas_call(
        kern, grid=(nj, S // tm),
        in_specs=[pl.BlockSpec((tm, D), lambda j, i: (i, 0)),
                  pl.BlockSpec((HCH, D), lambda j, i: (boff + j, 0)),
                  pl.BlockSpec((tm, 2 * HCH), lambda j, i: (i, j))],
        out_specs=pl.BlockSpec((tm, 2 * HCH), lambda j, i: (i, j)),
        out_shape=jax.ShapeDtypeStruct((S, 2 * DFF), BF16),
        compiler_params=_cparams(2), name=name)(dx, G, gu)


def _trail(u, *, backward, name):
    S = u.shape[0]

    def kern(u_ref, o_ref):
        g = pl.program_id(0)
        uv = u_ref[...].astype(F32)
        row = lax.broadcasted_iota(jnp.int32, uv.shape, 0)
        win = jnp.left_shift(jnp.int32(2), g)
        cnt = jnp.minimum(row + 1, win).astype(F32)
        s = uv / cnt if backward else uv
        levels = []
        for k in (1, 2, 4, 8):
            if backward:
                sh = jnp.where(row < S - k, pltpu.roll(s, S - k, 0), 0.0)
            else:
                sh = jnp.where(row >= k, pltpu.roll(s, k, 0), 0.0)
            s = s + sh
            levels.append(s)
        sel = jnp.where(g == 0, levels[0], jnp.where(g == 1, levels[1], jnp.where(g == 2, levels[2], levels[3])))
        if backward:
            o_ref[...] = (sel - uv).astype(BF16)
        else:
            o_ref[...] = (sel / cnt - uv).astype(BF16)

    blk = pl.BlockSpec((S, PGD), lambda g: (0, g))
    return pl.pallas_call(
        kern, grid=(POOL_G,), in_specs=[blk], out_specs=blk,
        out_shape=jax.ShapeDtypeStruct((S, D), BF16), compiler_params=_cparams(1), name=name)(u)


def _pool_out(yd, G, scale, xres):
    S = yd.shape[0]
    tm = 1024
    roff = GLB_OFF["pg"] // PGD

    def kern(y_ref, w_ref, s_ref, x_ref, o_ref):
        z = jnp.dot(y_ref[...], w_ref[...], preferred_element_type=F32)
        o_ref[...] = x_ref[...] + z * s_ref[...]

    tile = pl.BlockSpec((tm, PGD), lambda i, g: (i, g))
    return pl.pallas_call(
        kern, grid=(S // tm, POOL_G),
        in_specs=[tile, pl.BlockSpec((PGD, PGD), lambda i, g: (roff, g)),
                  pl.BlockSpec((1, PGD), lambda i, g: (0, g)), tile],
        out_specs=tile, out_shape=jax.ShapeDtypeStruct((S, D), F32),
        compiler_params=_cparams(2), name="pool_out")(yd, G, scale, xres)


def _pool_out_bwd(dz, yd, G, scale):
    S = yd.shape[0]
    tm = 1024
    ni = S // tm
    roff = GLB_OFF["pg"] // PGD

    def kern(dz_ref, y_ref, w_ref, s_ref, dy_ref, ds_ref, dw_ref, acc_ref):
        i = pl.program_id(1)
        dzv = dz_ref[...]
        yv = y_ref[...]
        wv = w_ref[...]
        zraw = jnp.dot(yv, wv, preferred_element_type=F32)
        dsp = jnp.sum(dzv * zraw, axis=0, keepdims=True)
        dzr = (dzv * s_ref[...]).astype(BF16)
        dy_ref[...] = lax.dot_general(dzr, wv, (((1,), (1,)), ((), ())), preferred_element_type=F32)
        dwp = lax.dot_general(yv, dzr, (((0,), (0,)), ((), ())), preferred_element_type=F32)

        @pl.when(i == 0)
        def _():
            ds_ref[...] = dsp
            acc_ref[...] = dwp

        @pl.when(i > 0)
        def _():
            ds_ref[...] += dsp
            acc_ref[...] += dwp

        @pl.when(i == ni - 1)
        def _():
            dw_ref[...] = acc_ref[...].astype(BF16)

    tile = pl.BlockSpec((tm, PGD), lambda g, i: (i, g))
    return pl.pallas_call(
        kern, grid=(POOL_G, ni),
        in_specs=[tile, tile, pl.BlockSpec((PGD, PGD), lambda g, i: (roff, g)),
                  pl.BlockSpec((1, PGD), lambda g, i: (0, g))],
        out_specs=[tile, pl.BlockSpec((1, PGD), lambda g, i: (0, g)),
                   pl.BlockSpec((PGD, PGD), lambda g, i: (0, g))],
        out_shape=[jax.ShapeDtypeStruct((S, D), F32), jax.ShapeDtypeStruct((1, D), F32),
                   jax.ShapeDtypeStruct((PGD, D), BF16)],
        scratch_shapes=[pltpu.VMEM((PGD, PGD), F32)],
        compiler_params=_cparams(2), name="pool_out_bwd")(dz, yd, G, scale)


def _bias_table():
    qi = jnp.arange(QB)[:, None]
    ki = jnp.arange(2 * QB)[None, :]
    delta = QB + qi - ki
    inband = (delta >= 0) & (delta <= QB)
    n = NGROUPS * HEADS
    slopes = jnp.exp2(-8.0 * jnp.arange(1, n + 1, dtype=F32) / n).reshape(NGROUPS, HEADS)
    dil = jnp.asarray(DILS, F32)
    bias = -slopes[:, :, None, None] * (delta.astype(F32)[None, None] * dil[:, None, None, None])
    return jnp.where(inband[None, None], bias, NEG)


def _group_scalars(g, b, nblk):
    nb = jnp.right_shift(jnp.int32(nblk), 2 * g)
    has_prev = jnp.bitwise_and(b, nb - 1) != 0
    return has_prev


def _attn_fwd(qkv_f, bias):
    S = qkv_f.shape[1]
    nblk = S // QB
    scale = HD ** -0.5

    def kern(q_ref, kc_ref, kp_ref, vc_ref, vp_ref, b_ref, o_ref, l_ref):
        g = pl.program_id(0)
        b = pl.program_id(1)
        has_prev = _group_scalars(g, b, nblk)
        col = lax.broadcasted_iota(jnp.int32, (QB, 2 * QB), 1)
        dead = jnp.logical_and(col < QB, jnp.logical_not(has_prev))
        lane = lax.broadcasted_iota(jnp.int32, (QB, HD), 1)
        lse_all = jnp.zeros((QB, HD), F32)
        for h in range(HEADS):
            sl = slice(h * HD, (h + 1) * HD)
            kk = jnp.concatenate([kp_ref[:, sl], kc_ref[:, sl]], axis=0)
            vv = jnp.concatenate([vp_ref[:, sl], vc_ref[:, sl]], axis=0)
            s = lax.dot_general(q_ref[:, sl], kk, (((1,), (1,)), ((), ())), preferred_element_type=F32)
            s = s * scale + b_ref[h]
            s = jnp.where(dead, NEG, s)
            m = jnp.max(s, axis=-1, keepdims=True)
            p = jnp.exp(s - m)
            den = jnp.sum(p, axis=-1, keepdims=True)
            o = jnp.dot(p.astype(BF16), vv, preferred_element_type=F32) / den
            o_ref[:, sl] = o.astype(BF16)
            lse_all = jnp.where(lane == h, m + jnp.log(den), lse_all)
        l_ref[...] = lse_all

    def blk(colblk, prev):
        if prev:
            return pl.BlockSpec((None, QB, D), lambda g, b: (g, jnp.maximum(b - 1, 0), colblk))
        return pl.BlockSpec((None, QB, D), lambda g, b: (g, b, colblk))

    return pl.pallas_call(
        kern, grid=(NGROUPS, nblk),
        in_specs=[blk(0, False), blk(1, False), blk(1, True), blk(2, False), blk(2, True),
                  pl.BlockSpec((None, HEADS, QB, 2 * QB), lambda g, b: (g, 0, 0, 0))],
        out_specs=[pl.BlockSpec((None, QB, D), lambda g, b: (g, b, 0)),
                   pl.BlockSpec((None, QB, HD), lambda g, b: (g, b, 0))],
        out_shape=[jax.ShapeDtypeStruct((NGROUPS, S, D), BF16), jax.ShapeDtypeStruct((NGROUPS, S, HD), F32)],
        compiler_params=_cparams(2), name="attn_fwd")(qkv_f, qkv_f, qkv_f, qkv_f, qkv_f, bias)


def _attn_merge(o_n, lse_n):
    S = o_n.shape[1]
    tm = 512

    def kern(o_ref, l_ref, om_ref, lm_ref):
        l0, l1, l2 = l_ref[0], l_ref[1], l_ref[2]
        m = jnp.maximum(jnp.maximum(l0, l1), l2)
        e0, e1, e2 = jnp.exp(l0 - m), jnp.exp(l1 - m), jnp.exp(l2 - m)
        tot = e0 + e1 + e2
        lm_ref[...] = m + jnp.log(tot)
        w0, w1, w2 = e0 / tot, e1 / tot, e2 / tot
        for h in range(HEADS):
            sl = slice(h * HD, (h + 1) * HD)
            acc = (w0[:, h:h + 1] * o_ref[0, :, sl].astype(F32) + w1[:, h:h + 1] * o_ref[1, :, sl].astype(F32)
                   + w2[:, h:h + 1] * o_ref[2, :, sl].astype(F32))
            om_ref[:, sl] = acc.astype(BF16)

    return pl.pallas_call(
        kern, grid=(S // tm,),
        in_specs=[pl.BlockSpec((NGROUPS, tm, D), lambda i: (0, i, 0)),
                  pl.BlockSpec((NGROUPS, tm, HD), lambda i: (0, i, 0))],
        out_specs=[pl.BlockSpec((tm, D), lambda i: (i, 0)), pl.BlockSpec((tm, HD), lambda i: (i, 0))],
        out_shape=[jax.ShapeDtypeStruct((S, D), BF16), jax.ShapeDtypeStruct((S, HD), F32)],
        compiler_params=_cparams(1), name="attn_merge")(o_n, lse_n)


def _attn_delta(do, o):
    S = o.shape[0]
    tm = 512

    def kern(do_ref, o_ref, d_ref):
        lane = lax.broadcasted_iota(jnp.int32, (tm, HD), 1)
        acc = jnp.zeros((tm, HD), F32)
        for h in range(HEADS):
            sl = slice(h * HD, (h + 1) * HD)
            prod = do_ref[:, sl].astype(F32) * o_ref[:, sl].astype(F32)
            acc = jnp.where(lane == h, jnp.sum(prod, axis=-1, keepdims=True), acc)
        d_ref[...] = acc

    row = pl.BlockSpec((tm, D), lambda i: (i, 0))
    return pl.pallas_call(
        kern, grid=(S // tm,), in_specs=[row, row], out_specs=pl.BlockSpec((tm, HD), lambda i: (i, 0)),
        out_shape=jax.ShapeDtypeStruct((S, HD), F32), compiler_params=_cparams(1), name="attn_delta")(do, o)


def _attn_bwd(qkv_f, do_f, lse_f, delta_f, bias):
    S = qkv_f.shape[1]
    nblk = S // QB
    scale = HD ** -0.5

    def kern(q_ref, kc_ref, kp_ref, vc_ref, vp_ref, do_ref, l_ref, d_ref, b_ref, out_ref, dq_c, dk_c, dv_c):
        g = pl.program_id(0)
        b = pl.program_id(1)

        @pl.when(b == 0)
        def _():
            dq_c[...] = jnp.zeros_like(dq_c)
            dk_c[...] = jnp.zeros_like(dk_c)
            dv_c[...] = jnp.zeros_like(dv_c)

        @pl.when(b == nblk)
        def _():
            out_ref[:, 0:D] = dq_c[...].astype(BF16)
            out_ref[:, D:2 * D] = dk_c[...].astype(BF16)
            out_ref[:, 2 * D:3 * D] = dv_c[...].astype(BF16)

        @pl.when(b < nblk)
        def _():
            has_prev = _group_scalars(g, b, nblk)
            col = lax.broadcasted_iota(jnp.int32, (QB, 2 * QB), 1)
            dead = jnp.logical_and(col < QB, jnp.logical_not(has_prev))
            out_ref[:, 0:D] = dq_c[...].astype(BF16)
            lv = l_ref[...]
            dv_ = d_ref[...]
            for h in range(HEADS):
                sl = slice(h * HD, (h + 1) * HD)
                qh = q_ref[:, sl]
                doh = do_ref[:, sl]
                kk = jnp.concatenate([kp_ref[:, sl], kc_ref[:, sl]], axis=0)
                vv = jnp.concatenate([vp_ref[:, sl], vc_ref[:, sl]], axis=0)
                s = lax.dot_general(qh, kk, (((1,), (1,)), ((), ())), preferred_element_type=F32)
                s = s * scale + b_ref[h]
                s = jnp.where(dead, NEG, s)
                p = jnp.exp(s - lv[:, h:h + 1])
                dp = lax.dot_general(doh, vv, (((1,), (1,)), ((), ())), preferred_element_type=F32)
                ds = (p * (dp - dv_[:, h:h + 1]) * scale).astype(BF16)
                pb = p.astype(BF16)
                dq_c[:, sl] = jnp.dot(ds, kk, preferred_element_type=F32)
                dkk = lax.dot_general(ds, qh, (((0,), (0,)), ((), ())), preferred_element_type=F32)
                dvv = lax.dot_general(pb, doh, (((0,), (0,)), ((), ())), preferred_element_type=F32)
                out_ref[:, D + h * HD:D + (h + 1) * HD] = (dk_c[:, sl] + dkk[:QB]).astype(BF16)
                out_ref[:, 2 * D + h * HD:2 * D + (h + 1) * HD] = (dv_c[:, sl] + dvv[:QB]).astype(BF16)
                dk_c[:, sl] = dkk[QB:]
                dv_c[:, sl] = dvv[QB:]

    last = nblk - 1

    def blk(colblk, prev):
        if prev:
            return pl.BlockSpec((None, QB, D), lambda g, b: (g, jnp.maximum(jnp.minimum(b, last) - 1, 0), colblk))
        return pl.BlockSpec((None, QB, D), lambda g, b: (g, jnp.minimum(b, last), colblk))

    stat = pl.BlockSpec((None, QB, HD), lambda g, b: (g, jnp.minimum(b, last), 0))
    return pl.pallas_call(
        kern, grid=(NGROUPS, nblk + 1),
        in_specs=[blk(0, False), blk(1, False), blk(1, True), blk(2, False), blk(2, True),
                  pl.BlockSpec((None, QB, D), lambda g, b: (g, jnp.minimum(b, last), 0)), stat, stat,
                  pl.BlockSpec((None, HEADS, QB, 2 * QB), lambda g, b: (g, 0, 0, 0))],
        out_specs=pl.BlockSpec((None, QB, 3 * D), lambda g, b: (g, jnp.maximum(b - 1, 0), 0)),
        out_shape=jax.ShapeDtypeStruct((NGROUPS, S, 3 * D), BF16),
        scratch_shapes=[pltpu.VMEM((QB, D), F32), pltpu.VMEM((QB, D), F32), pltpu.VMEM((QB, D), F32)],
        compiler_params=_cparams(2), name="attn_bwd")(qkv_f, qkv_f, qkv_f, qkv_f, qkv_f, do_f, lse_f, delta_f, bias)


def _fold(a, dil):
    if dil == 1:
        return a
    S, C = a.shape
    return a.reshape(S // dil, dil, C).transpose(1, 0, 2).reshape(S, C)


def _unfold(a, dil):
    if dil == 1:
        return a
    S, C = a.shape
    return a.reshape(dil, S // dil, C).transpose(1, 0, 2).reshape(S, C)


def _local_step(x, tgt, G, attn_norm, ffn_norm, final_norm, pool_norm, pool_scale):
    S = x.shape[0]
    bias = _bias_table()
    g_attn = attn_norm.reshape(1, D)
    g_f0 = ffn_norm[0:1]
    g_f1 = ffn_norm[1:2]
    g_fin = final_norm.reshape(1, D)

    def ffn_fwd(xin, gain, l):
        h = _rms_fwd(xin, gain, f"rms_ffn{l}")
        gu, act = _ffn_up(h, G, f"gu{l}", f"ffn_up{l}")
        xo = _mm(act, G, mode="nn", M=S, N=D, K=DFF, tm=512, tn=D, tk=DFF, out_dtype=F32,
                 b_off=(GLB_OFF[f"d{l}"] // DFF, 0), res=xin, name=f"ffn_down{l}")
        return h, gu, act, xo

    def ffn_bwd(dxo, xin, gain, h, gu, act, l):
        dgu = _ffn_down_bwd(dxo, G, f"d{l}", gu, f"ffn_down_bwd{l}")
        gw_d = _mm(act, dxo, mode="tn", M=DFF, N=D, K=S, tm=HCH, tn=D, tk=1024, out_dtype=BF16, name=f"gw_d{l}")
        dh = _mm(dgu, G, mode="nn", M=S, N=D, K=2 * DFF, tm=1024, tn=D, tk=HCH, out_dtype=F32,
                 b_off=(GLB_OFF[f"gu{l}"] // HCH, 0), name=f"ffn_up_bwd{l}")
        gw_gu = _mm(dgu, h, mode="tn", M=2 * DFF, N=D, K=S, tm=HCH, tn=D, tk=1024, out_dtype=BF16, name=f"gw_gu{l}")
        dxin, dgain = _rms_bwd(dh, xin, gain, dxo, f"rms_ffn_bwd{l}")
        return dxin, dgain, gw_gu, gw_d

    h0 = _rms_fwd(x, g_attn, "rms_attn")
    hf = jnp.stack([_fold(h0, d) for d in DILS])
    qkv_f = _mm(hf, G, mode="nt", nbat=NGROUPS, M=S, N=3 * D, K=D, tm=1024, tn=1024, tk=D, out_dtype=BF16,
                b_off=(GLB_OFF["qkv"] // 1024, 0), b_boff=(3, 0), name="qkv_proj")
    o_f, lse_f = _attn_fwd(qkv_f, bias)
    o_n = jnp.stack([_unfold(o_f[g], d) for g, d in enumerate(DILS)])
    lse_n = jnp.stack([_unfold(lse_f[g], d) for g, d in enumerate(DILS)])
    o, lse = _attn_merge(o_n, lse_n)
    x1 = _mm(o, G, mode="nn", M=S, N=D, K=D, tm=1024, tn=D, tk=D, out_dtype=F32,
             b_off=(GLB_OFF["wo"] // D, 0), res=x, name="attn_out")
    h1, gu0, act0, x2 = ffn_fwd(x1, g_f0, 0)

    h2 = _rms_fwd(x2, pool_norm, "rms_pool")
    u = _mm(h2, G, mode="nn", M=S, N=D, K=D, tm=1024, tn=D, tk=D, out_dtype=F32,
            b_off=(GLB_OFF["wpi"] // D, 0), name="pool_in")
    yd = _trail(u, backward=False, name="trail_fwd")
    x3 = _pool_out(yd, G, pool_scale, x2)
    h3, gu1, act1, x4 = ffn_fwd(x3, g_f1, 1)

    dx4, d_fin, lossvec = _loss_head(x4, g_fin, tgt)
    loss = 0.5 * jnp.sum(lossvec) / D

    dx3, d_f1, gw_gu1, gw_d1 = ffn_bwd(dx4, x3, g_f1, h3, gu1, act1, 1)
    dyd, d_scale, gw_pg = _pool_out_bwd(dx3, yd, G, pool_scale)
    du = _trail(dyd, backward=True, name="trail_bwd")
    dh2 = _mm(du, G, mode="nt", M=S, N=D, K=D, tm=1024, tn=D, tk=D, out_dtype=F32,
              b_off=(GLB_OFF["wpi"] // D, 0), name="pool_in_bwd")
    gw_pi = _mm(h2, du, mode="tn", M=D, N=D, K=S, tm=D, tn=D, tk=1024, out_dtype=BF16, name="gw_pi")
    dx2, d_pool = _rms_bwd(dh2, x2, pool_norm, dx3, "rms_pool_bwd")
    dx1, d_f0, gw_gu0, gw_d0 = ffn_bwd(dx2, x1, g_f0, h1, gu0, act0, 0)

    gw_o = _mm(o, dx1, mode="tn", M=D, N=D, K=S, tm=D, tn=D, tk=1024, out_dtype=BF16, name="gw_o")
    do = _mm(dx1, G, mode="nt", M=S, N=D, K=D, tm=1024, tn=D, tk=D, out_dtype=BF16,
             b_off=(GLB_OFF["wo"] // D, 0), name="attn_out_bwd")
    delta = _attn_delta(do, o)
    do_f = jnp.stack([_fold(do, d) for d in DILS])
    lse_ff = jnp.stack([_fold(lse, d) for d in DILS])
    delta_f = jnp.stack([_fold(delta, d) for d in DILS])
    dqkv_f = _attn_bwd(qkv_f, do_f, lse_ff, delta_f, bias)
    dh0_f = _mm(dqkv_f, G, mode="nn", nbat=NGROUPS, M=S, N=D, K=3 * D, tm=1024, tn=D, tk=1024, out_dtype=F32,
                b_off=(GLB_OFF["qkv"] // 1024, 0), b_boff=(3, 0), name="qkv_proj_bwd")
    gw_qkv = _mm(dqkv_f, hf, mode="tn", nbat=NGROUPS, M=3 * D, N=D, K=S, tm=1024, tn=D, tk=1024, out_dtype=BF16,
                 out_2d_rows=NGROUPS * 3 * D, name="gw_qkv")
    dh0 = _unfold(dh0_f[0], DILS[0]) + _unfold(dh0_f[1], DILS[1]) + _unfold(dh0_f[2], DILS[2])
    grad_x, d_attn = _rms_bwd(dh0, x, g_attn, dx1, "rms_attn_bwd")

    gws = {"qkv": gw_qkv, "wo": gw_o, "wpi": gw_pi, "gu0": gw_gu0, "gu1": gw_gu1, "d0": gw_d0, "d1": gw_d1,
           "pg": gw_pg}
    vec = jnp.concatenate([d_attn, d_f0, d_f1, d_fin, d_pool, d_scale, jnp.zeros((2, D), F32)], axis=0)
    return loss, grad_x, gws, vec


def _mesh_pos():
    x, y, c = lax.axis_index("x"), lax.axis_index("y"), lax.axis_index("c")
    return x, y, c, 4 * x + 2 * y + c


def _peer(x, y, c, k):
    kx, ky, kc = (k >> 2) & 1, (k >> 1) & 1, k & 1
    px = 1 - x if kx else x
    py = 1 - y if ky else y
    pc = 1 - c if kc else c
    return (px, py, pc), 4 * px + 2 * py + pc


ANY = pl.BlockSpec(memory_space=pl.ANY)


def _all_gather(P, vsh):
    nsec = len(SECTIONS)

    def kern(p_ref, v_ref, g_ref, vg_ref, send, recv, vsend, vrecv, lsem):
        x, y, c, me = _mesh_pos()

        def src(name):
            return p_ref.at[pl.ds(LOC_OFF[name], SEC_ROWS[name])]

        def dst(name):
            row = pl.multiple_of(GLB_OFF[name] + _shard_pos(name, me), 16)
            return g_ref.at[pl.ds(row, SEC_ROWS[name])]

        local = [pltpu.make_async_copy(src(n), dst(n), lsem.at[s]) for s, (n, _) in enumerate(SECTIONS)]
        local.append(pltpu.make_async_copy(v_ref, vg_ref.at[me], lsem.at[nsec]))
        for cp in local:
            cp.start()
        for k in range(1, NDEV):
            peer, _ = _peer(x, y, c, k)
            for n, _ in SECTIONS:
                pltpu.make_async_remote_copy(src_ref=src(n), dst_ref=dst(n), send_sem=send.at[k - 1],
                                             recv_sem=recv.at[k - 1], device_id=peer,
                                             device_id_type=pl.DeviceIdType.MESH).start()
            pltpu.make_async_remote_copy(src_ref=v_ref, dst_ref=vg_ref.at[me], send_sem=vsend.at[k - 1],
                                         recv_sem=vrecv.at[k - 1], device_id=peer,
                                         device_id_type=pl.DeviceIdType.MESH).start()
        for k in range(1, NDEV):
            peer, _ = _peer(x, y, c, k)
            whole = pltpu.make_async_remote_copy(src_ref=p_ref, dst_ref=g_ref.at[pl.ds(0, PACK_ROWS)],
                                                 send_sem=send.at[k - 1], recv_sem=recv.at[k - 1], device_id=peer,
                                                 device_id_type=pl.DeviceIdType.MESH)
            whole.wait_recv()
            whole.wait_send()
            vcp = pltpu.make_async_remote_copy(src_ref=v_ref, dst_ref=vg_ref.at[me], send_sem=vsend.at[k - 1],
                                               recv_sem=vrecv.at[k - 1], device_id=peer,
                                               device_id_type=pl.DeviceIdType.MESH)
            vcp.wait_recv()
            vcp.wait_send()
        for cp in local:
            cp.wait()

    return pl.pallas_call(
        kern, in_specs=[ANY, ANY], out_specs=[ANY, ANY],
        out_shape=[jax.ShapeDtypeStruct((GLB_ROWS, D), BF16), jax.ShapeDtypeStruct((NDEV, 8, 128), F32)],
        scratch_shapes=[pltpu.SemaphoreType.DMA((NDEV - 1,)), pltpu.SemaphoreType.DMA((NDEV - 1,)),
                        pltpu.SemaphoreType.DMA((NDEV - 1,)), pltpu.SemaphoreType.DMA((NDEV - 1,)),
                        pltpu.SemaphoreType.DMA((nsec + 1,))],
        name="all_gather_weights")(P, vsh)


def _grad_exchange(gws, vec):
    names = [n for n, _ in SECTIONS]
    nsec = len(names)

    def kern(*refs):
        g_refs = dict(zip(names, refs[:nsec]))
        v_ref = refs[nsec]
        r_ref, vr_ref = refs[nsec + 1], refs[nsec + 2]
        send, recv, vsend, vrecv, lsem = refs[nsec + 3:]
        x, y, c, me = _mesh_pos()

        def src(name, dev):
            row = pl.multiple_of(_shard_pos(name, dev), 16)
            return g_refs[name].at[pl.ds(row, SEC_ROWS[name])]

        def dst(name):
            return r_ref.at[me, pl.ds(LOC_OFF[name], SEC_ROWS[name])]

        local = [pltpu.make_async_copy(src(n, me), dst(n), lsem.at[s]) for s, n in enumerate(names)]
        local.append(pltpu.make_async_copy(v_ref, vr_ref.at[me], lsem.at[nsec]))
        for cp in local:
            cp.start()
        for k in range(1, NDEV):
            peer, pid = _peer(x, y, c, k)
            for n in names:
                pltpu.make_async_remote_copy(src_ref=src(n, pid), dst_ref=dst(n), send_sem=send.at[k - 1],
                                             recv_sem=recv.at[k - 1], device_id=peer,
                                             device_id_type=pl.DeviceIdType.MESH).start()
            pltpu.make_async_remote_copy(src_ref=v_ref, dst_ref=vr_ref.at[me], send_sem=vsend.at[k - 1],
                                         recv_sem=vrecv.at[k - 1], device_id=peer,
                                         device_id_type=pl.DeviceIdType.MESH).start()
        for k in range(1, NDEV):
            peer, _ = _peer(x, y, c, k)
            whole = pltpu.make_async_remote_copy(src_ref=r_ref.at[0], dst_ref=r_ref.at[1], send_sem=send.at[k - 1],
                                                 recv_sem=recv.at[k - 1], device_id=peer,
                                                 device_id_type=pl.DeviceIdType.MESH)
            whole.wait_recv()
            whole.wait_send()
            vcp = pltpu.make_async_remote_copy(src_ref=v_ref, dst_ref=vr_ref.at[me], send_sem=vsend.at[k - 1],
                                               recv_sem=vrecv.at[k - 1], device_id=peer,
                                               device_id_type=pl.DeviceIdType.MESH)
            vcp.wait_recv()
            vcp.wait_send()
        for cp in local:
            cp.wait()

    return pl.pallas_call(
        kern, in_specs=[ANY] * (nsec + 1), out_specs=[ANY, ANY],
        out_shape=[jax.ShapeDtypeStruct((NDEV, PACK_ROWS, D), BF16), jax.ShapeDtypeStruct((NDEV, 8, D), F32)],
        scratch_shapes=[pltpu.SemaphoreType.DMA((NDEV - 1,)), pltpu.SemaphoreType.DMA((NDEV - 1,)),
                        pltpu.SemaphoreType.DMA((NDEV - 1,)), pltpu.SemaphoreType.DMA((NDEV - 1,)),
                        pltpu.SemaphoreType.DMA((nsec + 1,))],
        name="grad_exchange")(*[gws[n] for n in names], vec)


def _adamw(R, w, m, v, *, rows, tr, name):
    c1 = 1.0 / (1.0 - ADAM_B1 ** ADAM_STEP)
    c2 = 1.0 / (1.0 - ADAM_B2 ** ADAM_STEP)

    def kern(r_ref, w_ref, m_ref, v_ref, g_out, d_out, m_out, v_out):
        g = r_ref[0].astype(F32)
        for dev in range(1, NDEV):
            g = g + r_ref[dev].astype(F32)
        mn = ADAM_B1 * m_ref[...] + (1.0 - ADAM_B1) * g
        vn = ADAM_B2 * v_ref[...] + (1.0 - ADAM_B2) * (g * g)
        g_out[...] = g
        m_out[...] = mn
        v_out[...] = vn
        d_out[...] = -ADAM_LR * ((mn * c1) / (jnp.sqrt(vn * c2) + ADAM_EPS) + ADAM_WD * w_ref[...])

    tile = pl.BlockSpec((tr, D), lambda i: (i, 0))
    shp = jax.ShapeDtypeStruct((rows, D), F32)
    return pl.pallas_call(
        kern, grid=(rows // tr,),
        in_specs=[pl.BlockSpec((NDEV, tr, D), lambda i: (0, i, 0)), tile, tile, tile],
        out_specs=[tile] * 4, out_shape=[shp] * 4, compiler_params=_cparams(1), name=name)(R, w, m, v)


def _pack(w_qkv, w_attn_out, w_pool_in, w_pool_group, w_ffn_gate_up, w_ffn_down):
    pg = w_pool_group[0].transpose(1, 0, 2).reshape(SEC_ROWS["pg"], D)
    return jnp.concatenate([
        w_qkv[0].T, w_attn_out[0], w_pool_in[0], w_ffn_gate_up[0].T, w_ffn_gate_up[1].T,
        w_ffn_down[0], w_ffn_down[1], pg], axis=0)


def _unpack(p):
    def sec(n):
        return p[LOC_OFF[n]:LOC_OFF[n] + SEC_ROWS[n]]
    w_qkv = sec("qkv").T[None]
    w_attn_out = sec("wo")[None]
    w_pool_in = sec("wpi")[None]
    w_pool_group = sec("pg").reshape(SEC_ROWS["pg"], POOL_G, PGD).transpose(1, 0, 2)[None]
    w_gu = jnp.stack([sec("gu0").T, sec("gu1").T])
    w_d = jnp.stack([sec("d0"), sec("d1")])
    return w_qkv, w_attn_out, w_pool_in, w_pool_group, w_gu, w_d


def _vec_pack(attn_norm, ffn_norm, final_norm, pool_norm_sh, pool_scale_sh, me):
    def place(sh):
        return lax.dynamic_update_slice(jnp.zeros((1, D), F32), sh, (0, me * 128))
    return jnp.concatenate([attn_norm, ffn_norm, final_norm.reshape(1, D), place(pool_norm_sh),
                            place(pool_scale_sh), jnp.zeros((2, D), F32)], axis=0)


def _vec_unpack(p, me):
    def take(r):
        return lax.dynamic_slice(p[r:r + 1], (0, me * 128), (1, 128))
    return p[0:1], p[1:3], p[3], take(4), take(5)


def kernel(x, attn_norm, w_qkv, w_attn_out, pool_norm, w_pool_in, w_pool_group, pool_scale, ffn_norm, w_ffn_gate_up, w_ffn_down, final_norm, loss_target, m_attn_norm, m_w_qkv, m_w_attn_out, m_pool_norm, m_w_pool_in, m_w_pool_group, m_pool_scale, m_ffn_norm, m_w_ffn_gate_up, m_w_ffn_down, m_final_norm, v_attn_norm, v_w_qkv, v_w_attn_out, v_pool_norm, v_w_pool_in, v_w_pool_group, v_pool_scale, v_ffn_norm, v_w_ffn_gate_up, v_w_ffn_down, v_final_norm):
    me = 4 * lax.axis_index("x") + 2 * lax.axis_index("y") + lax.axis_index("c")

    pw = _pack(w_qkv, w_attn_out, w_pool_in, w_pool_group, w_ffn_gate_up, w_ffn_down)
    pm = _pack(m_w_qkv, m_w_attn_out, m_w_pool_in, m_w_pool_group, m_w_ffn_gate_up, m_w_ffn_down)
    pv = _pack(v_w_qkv, v_w_attn_out, v_w_pool_in, v_w_pool_group, v_w_ffn_gate_up, v_w_ffn_down)
    vsh = jnp.concatenate([pool_norm, pool_scale, jnp.zeros((6, 128), F32)], axis=0)

    G, vg = _all_gather(pw.astype(BF16), vsh)
    pool_norm_full = vg[:, 0, :].reshape(1, D)
    pool_scale_full = vg[:, 1, :].reshape(1, D)

    loss, grad_x, gws, vec = _local_step(x[0], loss_target[0], G, attn_norm, ffn_norm, final_norm,
                                         pool_norm_full, pool_scale_full)
    loss = lax.psum(loss, MESH_AXES)

    R, VR = _grad_exchange(gws, vec)
    g_p, d_p, m_p, v_p = _adamw(R, pw, pm, pv, rows=PACK_ROWS, tr=96, name="adamw")
    vw = _vec_pack(attn_norm, ffn_norm, final_norm, pool_norm, pool_scale, me)
    vm = _vec_pack(m_attn_norm, m_ffn_norm, m_final_norm, m_pool_norm, m_pool_scale, me)
    vv = _vec_pack(v_attn_norm, v_ffn_norm, v_final_norm, v_pool_norm, v_pool_scale, me)
    g_v, d_v, m_v, v_v = _adamw(VR, vw, vm, vv, rows=8, tr=8, name="adamw_vec")

    outs = []
    for p, pvec in ((g_p, g_v), (d_p, d_v), (m_p, m_v), (v_p, v_v)):
        q, o, pi, pg, gu, dn = _unpack(p)
        an, fn, fin, pn, ps = _vec_unpack(pvec, me)
        outs.append((an, q, o, pn, pi, pg, ps, fn, gu, dn, fin))
    return (loss, grad_x[None]) + outs[0] + outs[1] + outs[2] + outs[3]
```

```python
import jax
import jax.numpy as jnp
from jax import lax
from jax.experimental import pallas as pl
from jax.experimental.pallas import tpu as pltpu

F32 = jnp.float32
BF16 = jnp.bfloat16

D = 1024
NDEV = 8
HEADS = 8
HD = 128
QB = 128
NGROUPS = 3
DILS = (1, 4, 16)
DFF = 2816
HCH = 1408
POOL_G = 4
PGD = 256
RMS_EPS = 1e-6
NEG = -1e30

ADAM_LR = 0.001
ADAM_B1 = 0.9
ADAM_B2 = 0.999
ADAM_EPS = 1e-08
ADAM_WD = 0.01
ADAM_STEP = 10

VMEM_LIMIT = 52 * 1024 * 1024
MESH_AXES = ("x", "y", "c")

SECTIONS = (("qkv", 1152), ("wo", 128), ("wpi", 128), ("gu0", 704), ("gu1", 704),
            ("d0", 352), ("d1", 352), ("pg", 32))
LOC_OFF = {}
GLB_OFF = {}
_o = 0
for _n, _r in SECTIONS:
    LOC_OFF[_n] = _o
    GLB_OFF[_n] = _o * NDEV
    _o += _r
PACK_ROWS = _o
GLB_ROWS = PACK_ROWS * NDEV
SEC_ROWS = dict(SECTIONS)


def _cparams(n_grid):
    return pltpu.CompilerParams(dimension_semantics=("arbitrary",) * n_grid, vmem_limit_bytes=VMEM_LIMIT)


def _shard_pos(name, dev):
    n = SEC_ROWS[name]
    if name in ("gu0", "gu1"):
        return ((dev % 4) // 2) * (2 * HCH) + (dev // 4) * HCH + (dev % 2) * n
    return dev * n


def _mm(a, b, *, mode, M, N, K, tm, tn, tk, out_dtype, name, nbat=1, a_off=(0, 0), b_off=(0, 0),
        a_boff=(0, 0), b_boff=(0, 0), res=None, out_2d_rows=None):
    nm, nn, nk = M // tm, N // tn, K // tk
    assert nm * tm == M and nn * tn == N and nk * tk == K
    if mode == "nn":
        a_bs, b_bs = (tm, tk), (tk, tn)
        a_ix = lambda i, j, k: (i, k)
        b_ix = lambda i, j, k: (k, j)
        dims = (((1,), (0,)), ((), ()))
    elif mode == "nt":
        a_bs, b_bs = (tm, tk), (tn, tk)
        a_ix = lambda i, j, k: (i, k)
        b_ix = lambda i, j, k: (j, k)
        dims = (((1,), (1,)), ((), ()))
    else:
        a_bs, b_bs = (tk, tm), (tk, tn)
        a_ix = lambda i, j, k: (k, i)
        b_ix = lambda i, j, k: (k, j)
        dims = (((0,), (0,)), ((), ()))

    def spec(arr, bs, ix, off, boff):
        if arr.ndim == 3:
            return pl.BlockSpec((None,) + bs, lambda bb, i, j, k: (bb,) + ix(i, j, k))

        def im(bb, i, j, k):
            r, c = ix(i, j, k)
            return (r + off[0] + bb * boff[0], c + off[1] + bb * boff[1])
        return pl.BlockSpec(bs, im)

    in_specs = [spec(a, a_bs, a_ix, a_off, a_boff), spec(b, b_bs, b_ix, b_off, b_boff)]
    args = [a, b]
    if res is not None:
        in_specs.append(pl.BlockSpec((tm, tn), lambda bb, i, j, k: (i, j)))
        args.append(res)
    if nbat > 1 and out_2d_rows is None:
        out_shape = jax.ShapeDtypeStruct((nbat, M, N), out_dtype)
        out_spec = pl.BlockSpec((None, tm, tn), lambda bb, i, j, k: (bb, i, j))
    else:
        rows = M if out_2d_rows is None else out_2d_rows
        out_shape = jax.ShapeDtypeStruct((rows, N), out_dtype)
        out_spec = pl.BlockSpec((tm, tn), lambda bb, i, j, k: (i + bb * nm, j))
    has_res = res is not None

    def kern(*refs):
        a_ref, b_ref = refs[0], refs[1]
        res_ref = refs[2] if has_res else None
        o_ref = refs[3] if has_res else refs[2]
        av = a_ref[...]
        bv = b_ref[...]
        if av.dtype != BF16:
            av = av.astype(BF16)
        if bv.dtype != BF16:
            bv = bv.astype(BF16)
        part = lax.dot_general(av, bv, dims, preferred_element_type=F32)

        def write(val):
            if has_res:
                val = val + res_ref[...]
            o_ref[...] = val.astype(out_dtype)

        if nk == 1:
            write(part)
        else:
            acc_ref = refs[-1]
            k = pl.program_id(3)

            @pl.when(k == 0)
            def _():
                acc_ref[...] = part

            @pl.when(k > 0)
            def _():
                acc_ref[...] += part

            @pl.when(k == nk - 1)
            def _():
                write(acc_ref[...])

    scratch = [pltpu.VMEM((tm, tn), F32)] if nk > 1 else []
    return pl.pallas_call(
        kern, grid=(nbat, nm, nn, nk), in_specs=in_specs, out_specs=out_spec, out_shape=out_shape,
        scratch_shapes=scratch, compiler_params=_cparams(4), name=name)(*args)


def _rms_fwd(x, g, name):
    S = x.shape[0]
    tr = 512

    def kern(x_ref, g_ref, h_ref):
        xv = x_ref[...]
        r = lax.rsqrt(jnp.mean(xv * xv, axis=-1, keepdims=True) + RMS_EPS)
        h_ref[...] = (xv * r * g_ref[...]).astype(BF16)

    return pl.pallas_call(
        kern, grid=(S // tr,),
        in_specs=[pl.BlockSpec((tr, D), lambda i: (i, 0)), pl.BlockSpec((1, D), lambda i: (0, 0))],
        out_specs=pl.BlockSpec((tr, D), lambda i: (i, 0)),
        out_shape=jax.ShapeDtypeStruct((S, D), BF16), compiler_params=_cparams(1), name=name)(x, g)


def _rms_bwd(dh, x, g, dres, name):
    S = x.shape[0]
    tr = 512

    def kern(dh_ref, x_ref, g_ref, dres_ref, dx_ref, dg_ref):
        i = pl.program_id(0)
        xv = x_ref[...]
        dhv = dh_ref[...].astype(F32)
        r = lax.rsqrt(jnp.mean(xv * xv, axis=-1, keepdims=True) + RMS_EPS)
        xhat = xv * r
        gy = dhv * g_ref[...]
        dx_ref[...] = dres_ref[...] + r * (gy - xhat * jnp.mean(gy * xhat, axis=-1, keepdims=True))
        part = jnp.sum(dhv * xhat, axis=0, keepdims=True)

        @pl.when(i == 0)
        def _():
            dg_ref[...] = part

        @pl.when(i > 0)
        def _():
            dg_ref[...] += part

    row = pl.BlockSpec((tr, D), lambda i: (i, 0))
    vec = pl.BlockSpec((1, D), lambda i: (0, 0))
    return pl.pallas_call(
        kern, grid=(S // tr,), in_specs=[row, row, vec, row], out_specs=[row, vec],
        out_shape=[jax.ShapeDtypeStruct((S, D), F32), jax.ShapeDtypeStruct((1, D), F32)],
        compiler_params=_cparams(1), name=name)(dh, x, g, dres)


def _loss_head(x, g, tgt):
    S = x.shape[0]
    tr = 512

    def kern(x_ref, g_ref, t_ref, dx_ref, dg_ref, ls_ref):
        i = pl.program_id(0)
        xv = x_ref[...]
        gv = g_ref[...]
        r = lax.rsqrt(jnp.mean(xv * xv, axis=-1, keepdims=True) + RMS_EPS)
        xhat = xv * r
        e = xhat * gv - t_ref[...]
        dy = e * (1.0 / D)
        gy = dy * gv
        dx_ref[...] = r * (gy - xhat * jnp.mean(gy * xhat, axis=-1, keepdims=True))
        dgp = jnp.sum(dy * xhat, axis=0, keepdims=True)
        lsp = jnp.sum(e * e, axis=0, keepdims=True)

        @pl.when(i == 0)
        def _():
            dg_ref[...] = dgp
            ls_ref[...] = lsp

        @pl.when(i > 0)
        def _():
            dg_ref[...] += dgp
            ls_ref[...] += lsp

    row = pl.BlockSpec((tr, D), lambda i: (i, 0))
    vec = pl.BlockSpec((1, D), lambda i: (0, 0))
    return pl.pallas_call(
        kern, grid=(S // tr,), in_specs=[row, vec, row], out_specs=[row, vec, vec],
        out_shape=[jax.ShapeDtypeStruct((S, D), F32), jax.ShapeDtypeStruct((1, D), F32),
                   jax.ShapeDtypeStruct((1, D), F32)],
        compiler_params=_cparams(1), name="loss_head")(x, g, tgt)


def _ffn_up(h, G, name):
    S = h.shape[0]
    tm = 512
    nj = DFF // HCH

    def kern(h_ref, w_ref, gu_ref, act_ref):
        gu = lax.dot_general(h_ref[...], w_ref[...], (((1,), (1,)), ((), ())), preferred_element_type=F32)
        gu_ref[...] = gu.astype(BF16)
        gate = gu[:, :HCH]
        up = gu[:, HCH:]
        act_ref[...] = (gate * jax.nn.sigmoid(gate) * up).astype(BF16)

    return pl.pallas_call(
        kern, grid=(nj, S // tm),
        in_specs=[pl.BlockSpec((tm, D), lambda j, i: (i, 0)),
                  pl.BlockSpec((2 * HCH, D), lambda j, i: (j, 0))],
        out_specs=[pl.BlockSpec((tm, 2 * HCH), lambda j, i: (i, j)),
                   pl.BlockSpec((tm, HCH), lambda j, i: (i, j))],
        out_shape=[jax.ShapeDtypeStruct((S, 2 * DFF), BF16), jax.ShapeDtypeStruct((S, DFF), BF16)],
        compiler_params=_cparams(2), name=name)(h, G)


def _ffn_down_bwd(dx, G, gu, name):
    S = dx.shape[0]
    tm = 512
    nj = DFF // HCH

    def kern(dx_ref, w_ref, gu_ref, o_ref):
        dact = lax.dot_general(dx_ref[...].astype(BF16), w_ref[...], (((1,), (1,)), ((), ())),
                               preferred_element_type=F32)
        gate = gu_ref[:, :HCH].astype(F32)
        up = gu_ref[:, HCH:].astype(F32)
        sig = jax.nn.sigmoid(gate)
        silu = gate * sig
        o_ref[:, :HCH] = (dact * up * (sig * (1.0 + gate * (1.0 - sig)))).astype(BF16)
        o_ref[:, HCH:] = (dact * silu).astype(BF16)

    return pl.pallas_call(
        kern, grid=(nj, S // tm),
        in_specs=[pl.BlockSpec((tm, D), lambda j, i: (i, 0)),
                  pl.BlockSpec((HCH, D), lambda j, i: (j, 0)),
                  pl.BlockSpec((tm, 2 * HCH), lambda j, i: (i, j))],
        out_specs=pl.BlockSpec((tm, 2 * HCH), lambda j, i: (i, j)),
        out_shape=jax.ShapeDtypeStruct((S, 2 * DFF), BF16),
        compiler_params=_cparams(2), name=name)(dx, G, gu)


def _trail(u, *, backward, name):
    S = u.shape[0]

    def kern(u_ref, o_ref):
        g = pl.program_id(0)
        uv = u_ref[...].astype(F32)
        row = lax.broadcasted_iota(jnp.int32, uv.shape, 0)
        win = jnp.left_shift(jnp.int32(2), g)
        cnt = jnp.minimum(row + 1, win).astype(F32)
        s = uv / cnt if backward else uv
        levels = []
        for k in (1, 2, 4, 8):
            if backward:
                sh = jnp.where(row < S - k, pltpu.roll(s, S - k, 0), 0.0)
            else:
                sh = jnp.where(row >= k, pltpu.roll(s, k, 0), 0.0)
            s = s + sh
            levels.append(s)
        sel = jnp.where(g == 0, levels[0], jnp.where(g == 1, levels[1], jnp.where(g == 2, levels[2], levels[3])))
        if backward:
            o_ref[...] = (sel - uv).astype(BF16)
        else:
            o_ref[...] = (sel / cnt - uv).astype(BF16)

    blk = pl.BlockSpec((S, PGD), lambda g: (0, g))
    return pl.pallas_call(
        kern, grid=(POOL_G,), in_specs=[blk], out_specs=blk,
        out_shape=jax.ShapeDtypeStruct((S, D), BF16), compiler_params=_cparams(1), name=name)(u)


def _pool_out(yd, G, scale, xres):
    S = yd.shape[0]
    tm = 1024

    def kern(y_ref, w_ref, s_ref, x_ref, o_ref):
        z = jnp.dot(y_ref[...], w_ref[...], preferred_element_type=F32)
        o_ref[...] = x_ref[...] + z * s_ref[...]

    tile = pl.BlockSpec((tm, PGD), lambda i, g: (i, g))
    return pl.pallas_call(
        kern, grid=(S // tm, POOL_G),
        in_specs=[tile, pl.BlockSpec((PGD, PGD), lambda i, g: (0, g)),
                  pl.BlockSpec((1, PGD), lambda i, g: (0, g)), tile],
        out_specs=tile, out_shape=jax.ShapeDtypeStruct((S, D), F32),
        compiler_params=_cparams(2), name="pool_out")(yd, G, scale, xres)


def _pool_out_bwd(dz, yd, G, scale):
    S = yd.shape[0]
    tm = 1024
    ni = S // tm

    def kern(dz_ref, y_ref, w_ref, s_ref, dy_ref, ds_ref, dw_ref, acc_ref):
        i = pl.program_id(1)
        dzv = dz_ref[...]
        yv = y_ref[...]
        wv = w_ref[...]
        zraw = jnp.dot(yv, wv, preferred_element_type=F32)
        dsp = jnp.sum(dzv * zraw, axis=0, keepdims=True)
        dzr = (dzv * s_ref[...]).astype(BF16)
        dy_ref[...] = lax.dot_general(dzr, wv, (((1,), (1,)), ((), ())), preferred_element_type=F32)
        dwp = lax.dot_general(yv, dzr, (((0,), (0,)), ((), ())), preferred_element_type=F32)

        @pl.when(i == 0)
        def _():
            ds_ref[...] = dsp
            acc_ref[...] = dwp

        @pl.when(i > 0)
        def _():
            ds_ref[...] += dsp
            acc_ref[...] += dwp

        @pl.when(i == ni - 1)
        def _():
            dw_ref[...] = acc_ref[...].astype(BF16)

    tile = pl.BlockSpec((tm, PGD), lambda g, i: (i, g))
    return pl.pallas_call(
        kern, grid=(POOL_G, ni),
        in_specs=[tile, tile, pl.BlockSpec((PGD, PGD), lambda g, i: (0, g)),
                  pl.BlockSpec((1, PGD), lambda g, i: (0, g))],
        out_specs=[tile, pl.BlockSpec((1, PGD), lambda g, i: (0, g)),
                   pl.BlockSpec((PGD, PGD), lambda g, i: (0, g))],
        out_shape=[jax.ShapeDtypeStruct((S, D), F32), jax.ShapeDtypeStruct((1, D), F32),
                   jax.ShapeDtypeStruct((PGD, D), BF16)],
        scratch_shapes=[pltpu.VMEM((PGD, PGD), F32)],
        compiler_params=_cparams(2), name="pool_out_bwd")(dz, yd, G, scale)


def _bias_table():
    qi = jnp.arange(QB)[:, None]
    ki = jnp.arange(2 * QB)[None, :]
    delta = QB + qi - ki
    inband = (delta >= 0) & (delta <= QB)
    n = NGROUPS * HEADS
    slopes = jnp.exp2(-8.0 * jnp.arange(1, n + 1, dtype=F32) / n).reshape(NGROUPS, HEADS)
    dil = jnp.asarray(DILS, F32)
    bias = -slopes[:, :, None, None] * (delta.astype(F32)[None, None] * dil[:, None, None, None])
    return jnp.where(inband[None, None], bias, NEG)


def _group_scalars(g, b, nblk):
    nb = jnp.right_shift(jnp.int32(nblk), 2 * g)
    has_prev = jnp.bitwise_and(b, nb - 1) != 0
    return has_prev


def _attn_fwd(qkv_f, bias):
    S = qkv_f.shape[1]
    nblk = S // QB
    scale = HD ** -0.5

    def kern(q_ref, kc_ref, kp_ref, vc_ref, vp_ref, b_ref, o_ref, l_ref):
        g = pl.program_id(0)
        b = pl.program_id(1)
        has_prev = _group_scalars(g, b, nblk)
        col = lax.broadcasted_iota(jnp.int32, (QB, 2 * QB), 1)
        dead = jnp.logical_and(col < QB, jnp.logical_not(has_prev))
        lane = lax.broadcasted_iota(jnp.int32, (QB, HD), 1)
        lse_all = jnp.zeros((QB, HD), F32)
        for h in range(HEADS):
            sl = slice(h * HD, (h + 1) * HD)
            kk = jnp.concatenate([kp_ref[:, sl], kc_ref[:, sl]], axis=0)
            vv = jnp.concatenate([vp_ref[:, sl], vc_ref[:, sl]], axis=0)
            s = lax.dot_general(q_ref[:, sl], kk, (((1,), (1,)), ((), ())), preferred_element_type=F32)
            s = s * scale + b_ref[h]
            s = jnp.where(dead, NEG, s)
            m = jnp.max(s, axis=-1, keepdims=True)
            p = jnp.exp(s - m)
            den = jnp.sum(p, axis=-1, keepdims=True)
            o = jnp.dot(p.astype(BF16), vv, preferred_element_type=F32) / den
            o_ref[:, sl] = o.astype(BF16)
            lse_all = jnp.where(lane == h, m + jnp.log(den), lse_all)
        l_ref[...] = lse_all

    def blk(colblk, prev):
        if prev:
            return pl.BlockSpec((None, QB, D), lambda g, b: (g, jnp.maximum(b - 1, 0), colblk))
        return pl.BlockSpec((None, QB, D), lambda g, b: (g, b, colblk))

    return pl.pallas_call(
        kern, grid=(NGROUPS, nblk),
        in_specs=[blk(0, False), blk(1, False), blk(1, True), blk(2, False), blk(2, True),
                  pl.BlockSpec((None, HEADS, QB, 2 * QB), lambda g, b: (g, 0, 0, 0))],
        out_specs=[pl.BlockSpec((None, QB, D), lambda g, b: (g, b, 0)),
                   pl.BlockSpec((None, QB, HD), lambda g, b: (g, b, 0))],
        out_shape=[jax.ShapeDtypeStruct((NGROUPS, S, D), BF16), jax.ShapeDtypeStruct((NGROUPS, S, HD), F32)],
        compiler_params=_cparams(2), name="attn_fwd")(qkv_f, qkv_f, qkv_f, qkv_f, qkv_f, bias)


def _attn_merge(o_n, lse_n):
    S = o_n.shape[1]
    tm = 512

    def kern(o_ref, l_ref, om_ref, lm_ref):
        l0, l1, l2 = l_ref[0], l_ref[1], l_ref[2]
        m = jnp.maximum(jnp.maximum(l0, l1), l2)
        e0, e1, e2 = jnp.exp(l0 - m), jnp.exp(l1 - m), jnp.exp(l2 - m)
        tot = e0 + e1 + e2
        lm_ref[...] = m + jnp.log(tot)
        w0, w1, w2 = e0 / tot, e1 / tot, e2 / tot
        for h in range(HEADS):
            sl = slice(h * HD, (h + 1) * HD)
            acc = (w0[:, h:h + 1] * o_ref[0, :, sl].astype(F32) + w1[:, h:h + 1] * o_ref[1, :, sl].astype(F32)
                   + w2[:, h:h + 1] * o_ref[2, :, sl].astype(F32))
            om_ref[:, sl] = acc.astype(BF16)

    return pl.pallas_call(
        kern, grid=(S // tm,),
        in_specs=[pl.BlockSpec((NGROUPS, tm, D), lambda i: (0, i, 0)),
                  pl.BlockSpec((NGROUPS, tm, HD), lambda i: (0, i, 0))],
        out_specs=[pl.BlockSpec((tm, D), lambda i: (i, 0)), pl.BlockSpec((tm, HD), lambda i: (i, 0))],
        out_shape=[jax.ShapeDtypeStruct((S, D), BF16), jax.ShapeDtypeStruct((S, HD), F32)],
        compiler_params=_cparams(1), name="attn_merge")(o_n, lse_n)


def _attn_delta(do, o):
    S = o.shape[0]
    tm = 512

    def kern(do_ref, o_ref, d_ref):
        lane = lax.broadcasted_iota(jnp.int32, (tm, HD), 1)
        acc = jnp.zeros((tm, HD), F32)
        for h in range(HEADS):
            sl = slice(h * HD, (h + 1) * HD)
            prod = do_ref[:, sl].astype(F32) * o_ref[:, sl].astype(F32)
            acc = jnp.where(lane == h, jnp.sum(prod, axis=-1, keepdims=True), acc)
        d_ref[...] = acc

    row = pl.BlockSpec((tm, D), lambda i: (i, 0))
    return pl.pallas_call(
        kern, grid=(S // tm,), in_specs=[row, row], out_specs=pl.BlockSpec((tm, HD), lambda i: (i, 0)),
        out_shape=jax.ShapeDtypeStruct((S, HD), F32), compiler_params=_cparams(1), name="attn_delta")(do, o)


def _attn_bwd(qkv_f, do_f, lse_f, delta_f, bias):
    S = qkv_f.shape[1]
    nblk = S // QB
    scale = HD ** -0.5

    def kern(q_ref, kc_ref, kp_ref, vc_ref, vp_ref, do_ref, l_ref, d_ref, b_ref, out_ref, dq_c, dk_c, dv_c):
        g = pl.program_id(0)
        b = pl.program_id(1)

        @pl.when(b == 0)
        def _():
            dq_c[...] = jnp.zeros_like(dq_c)
            dk_c[...] = jnp.zeros_like(dk_c)
            dv_c[...] = jnp.zeros_like(dv_c)

        @pl.when(b == nblk)
        def _():
            out_ref[:, 0:D] = dq_c[...].astype(BF16)
            out_ref[:, D:2 * D] = dk_c[...].astype(BF16)
            out_ref[:, 2 * D:3 * D] = dv_c[...].astype(BF16)

        @pl.when(b < nblk)
        def _():
            has_prev = _group_scalars(g, b, nblk)
            col = lax.broadcasted_iota(jnp.int32, (QB, 2 * QB), 1)
            dead = jnp.logical_and(col < QB, jnp.logical_not(has_prev))
            out_ref[:, 0:D] = dq_c[...].astype(BF16)
            lv = l_ref[...]
            dv_ = d_ref[...]
            for h in range(HEADS):
                sl = slice(h * HD, (h + 1) * HD)
                qh = q_ref[:, sl]
                doh = do_ref[:, sl]
                kk = jnp.concatenate([kp_ref[:, sl], kc_ref[:, sl]], axis=0)
                vv = jnp.concatenate([vp_ref[:, sl], vc_ref[:, sl]], axis=0)
                s = lax.dot_general(qh, kk, (((1,), (1,)), ((), ())), preferred_element_type=F32)
                s = s * scale + b_ref[h]
                s = jnp.where(dead, NEG, s)
                p = jnp.exp(s - lv[:, h:h + 1])
                dp = lax.dot_general(doh, vv, (((1,), (1,)), ((), ())), preferred_element_type=F32)
                ds = (p * (dp - dv_[:, h:h + 1]) * scale).astype(BF16)
                pb = p.astype(BF16)
                dq_c[:, sl] = jnp.dot(ds, kk, preferred_element_type=F32)
                dkk = lax.dot_general(ds, qh, (((0,), (0,)), ((), ())), preferred_element_type=F32)
                dvv = lax.dot_general(pb, doh, (((0,), (0,)), ((), ())), preferred_element_type=F32)
                out_ref[:, D + h * HD:D + (h + 1) * HD] = (dk_c[:, sl] + dkk[:QB]).astype(BF16)
                out_ref[:, 2 * D + h * HD:2 * D + (h + 1) * HD] = (dv_c[:, sl] + dvv[:QB]).astype(BF16)
                dk_c[:, sl] = dkk[QB:]
                dv_c[:, sl] = dvv[QB:]

    last = nblk - 1

    def blk(colblk, prev):
        if prev:
            return pl.BlockSpec((None, QB, D), lambda g, b: (g, jnp.maximum(jnp.minimum(b, last) - 1, 0), colblk))
        return pl.BlockSpec((None, QB, D), lambda g, b: (g, jnp.minimum(b, last), colblk))

    stat = pl.BlockSpec((None, QB, HD), lambda g, b: (g, jnp.minimum(b, last), 0))
    return pl.pallas_call(
        kern, grid=(NGROUPS, nblk + 1),
        in_specs=[blk(0, False), blk(1, False), blk(1, True), blk(2, False), blk(2, True),
                  pl.BlockSpec((None, QB, D), lambda g, b: (g, jnp.minimum(b, last), 0)), stat, stat,
                  pl.BlockSpec((None, HEADS, QB, 2 * QB), lambda g, b: (g, 0, 0, 0))],
        out_specs=pl.BlockSpec((None, QB, 3 * D), lambda g, b: (g, jnp.maximum(b - 1, 0), 0)),
        out_shape=jax.ShapeDtypeStruct((NGROUPS, S, 3 * D), BF16),
        scratch_shapes=[pltpu.VMEM((QB, D), F32), pltpu.VMEM((QB, D), F32), pltpu.VMEM((QB, D), F32)],
        compiler_params=_cparams(2), name="attn_bwd")(qkv_f, qkv_f, qkv_f, qkv_f, qkv_f, do_f, lse_f, delta_f, bias)


def _fold(a, dil):
    if dil == 1:
        return a
    S, C = a.shape
    return a.reshape(S // dil, dil, C).transpose(1, 0, 2).reshape(S, C)


def _unfold(a, dil):
    if dil == 1:
        return a
    S, C = a.shape
    return a.reshape(dil, S // dil, C).transpose(1, 0, 2).reshape(S, C)


def _after(a, token):
    return lax.optimization_barrier((a, token))[0]


def _local_step(x, tgt, comm, attn_norm, ffn_norm, final_norm, pool_norm, pool_scale):
    S = x.shape[0]
    bias = _bias_table()
    g_attn = attn_norm.reshape(1, D)
    g_f0 = ffn_norm[0:1]
    g_f1 = ffn_norm[1:2]
    g_fin = final_norm.reshape(1, D)
    W = {}

    def ffn_fwd(xin, gain, l):
        h = _rms_fwd(xin, gain, f"rms_ffn{l}")
        gu, act = _ffn_up(h, W[f"gu{l}"], f"ffn_up{l}")
        xo = _mm(act, W[f"d{l}"], mode="nn", M=S, N=D, K=DFF, tm=512, tn=D, tk=DFF, out_dtype=F32,
                 res=xin, name=f"ffn_down{l}")
        return h, gu, act, xo

    def ffn_bwd(dxo, xin, gain, h, gu, act, l, rs_group):
        dgu = _ffn_down_bwd(dxo, W[f"d{l}"], gu, f"ffn_down_bwd{l}")
        gw_d = _mm(act, dxo, mode="tn", M=DFF, N=D, K=S, tm=HCH, tn=D, tk=1024, out_dtype=BF16, name=f"gw_d{l}")
        gw_gu = _mm(dgu, h, mode="tn", M=2 * DFF, N=D, K=S, tm=HCH, tn=D, tk=1024, out_dtype=BF16, name=f"gw_gu{l}")
        token = comm.send_grads(rs_group, {f"d{l}": gw_d, f"gu{l}": gw_gu})
        dh = _mm(_after(dgu, token), W[f"gu{l}"], mode="nn", M=S, N=D, K=2 * DFF, tm=1024, tn=D, tk=HCH,
                 out_dtype=F32, name=f"ffn_up_bwd{l}")
        dxin, dgain = _rms_bwd(dh, xin, gain, dxo, f"rms_ffn_bwd{l}")
        return dxin, dgain

    h0 = _rms_fwd(x, g_attn, "rms_attn")
    hf = jnp.stack([_fold(h0, d) for d in DILS])
    W.update(comm.weights(0, hf))
    qkv_f = _mm(hf, W["qkv"], mode="nt", nbat=NGROUPS, M=S, N=3 * D, K=D, tm=1024, tn=1024, tk=D, out_dtype=BF16,
                b_boff=(3, 0), name="qkv_proj")
    o_f, lse_f = _attn_fwd(qkv_f, bias)
    o_n = jnp.stack([_unfold(o_f[g], d) for g, d in enumerate(DILS)])
    lse_n = jnp.stack([_unfold(lse_f[g], d) for g, d in enumerate(DILS)])
    o, lse = _attn_merge(o_n, lse_n)
    W.update(comm.weights(1, o))
    x1 = _mm(o, W["wo"], mode="nn", M=S, N=D, K=D, tm=1024, tn=D, tk=D, out_dtype=F32, res=x, name="attn_out")
    h1, gu0, act0, x2 = ffn_fwd(x1, g_f0, 0)

    W.update(comm.weights(2, x2))
    h2 = _rms_fwd(x2, pool_norm, "rms_pool")
    u = _mm(h2, W["wpi"], mode="nn", M=S, N=D, K=D, tm=1024, tn=D, tk=D, out_dtype=F32, name="pool_in")
    yd = _trail(u, backward=False, name="trail_fwd")
    x3 = _pool_out(yd, W["pg"], pool_scale, x2)
    h3, gu1, act1, x4 = ffn_fwd(x3, g_f1, 1)

    dx4, d_fin, lossvec = _loss_head(x4, g_fin, tgt)
    loss = 0.5 * jnp.sum(lossvec) / D

    dx3, d_f1 = ffn_bwd(dx4, x3, g_f1, h3, gu1, act1, 1, 0)
    dyd, d_scale, gw_pg = _pool_out_bwd(dx3, yd, W["pg"], pool_scale)
    du = _trail(dyd, backward=True, name="trail_bwd")
    gw_pi = _mm(h2, du, mode="tn", M=D, N=D, K=S, tm=D, tn=D, tk=1024, out_dtype=BF16, name="gw_pi")
    token = comm.send_grads(1, {"pg": gw_pg, "wpi": gw_pi})
    dh2 = _mm(_after(du, token), W["wpi"], mode="nt", M=S, N=D, K=D, tm=1024, tn=D, tk=D, out_dtype=F32,
              name="pool_in_bwd")
    dx2, d_pool = _rms_bwd(dh2, x2, pool_norm, dx3, "rms_pool_bwd")
    dx1, d_f0 = ffn_bwd(dx2, x1, g_f0, h1, gu0, act0, 0, 2)

    gw_o = _mm(o, dx1, mode="tn", M=D, N=D, K=S, tm=D, tn=D, tk=1024, out_dtype=BF16, name="gw_o")
    do = _mm(dx1, W["wo"], mode="nt", M=S, N=D, K=D, tm=1024, tn=D, tk=D, out_dtype=BF16, name="attn_out_bwd")
    delta = _attn_delta(do, o)
    do_f = jnp.stack([_fold(do, d) for d in DILS])
    lse_ff = jnp.stack([_fold(lse, d) for d in DILS])
    delta_f = jnp.stack([_fold(delta, d) for d in DILS])
    dqkv_f = _attn_bwd(qkv_f, do_f, lse_ff, delta_f, bias)
    gw_qkv = _mm(dqkv_f, hf, mode="tn", nbat=NGROUPS, M=3 * D, N=D, K=S, tm=1024, tn=D, tk=1024, out_dtype=BF16,
                 out_2d_rows=NGROUPS * 3 * D, name="gw_qkv")
    token = comm.send_grads(3, {"wo": gw_o, "qkv": gw_qkv})
    dh0_f = _mm(_after(dqkv_f, token), W["qkv"], mode="nn", nbat=NGROUPS, M=S, N=D, K=3 * D, tm=1024, tn=D, tk=1024,
                out_dtype=F32, b_boff=(3, 0), name="qkv_proj_bwd")
    dh0 = _unfold(dh0_f[0], DILS[0]) + _unfold(dh0_f[1], DILS[1]) + _unfold(dh0_f[2], DILS[2])
    grad_x, d_attn = _rms_bwd(dh0, x, g_attn, dx1, "rms_attn_bwd")

    vec = jnp.concatenate([d_attn, d_f0, d_f1, d_fin, d_pool, d_scale, jnp.zeros((2, D), F32)], axis=0)
    return loss, grad_x, vec


def _mesh_pos():
    x, y, c = lax.axis_index("x"), lax.axis_index("y"), lax.axis_index("c")
    return x, y, c, 4 * x + 2 * y + c


def _peer(x, y, c, k):
    kx, ky, kc = (k >> 2) & 1, (k >> 1) & 1, k & 1
    px = 1 - x if kx else x
    py = 1 - y if ky else y
    pc = 1 - c if kc else c
    return (px, py, pc), 4 * px + 2 * py + pc


ANY = pl.BlockSpec(memory_space=pl.ANY)


HBM = pl.BlockSpec(memory_space=pltpu.HBM)
SEMS = pl.BlockSpec(memory_space=pltpu.SEMAPHORE)
EFFECT = pltpu.SideEffectType.DATAFLOW_SIDE_EFFECTING
NPEER = NDEV - 1

AG_GROUPS = (("qkv",), ("wo", "gu0", "d0"), ("wpi", "pg", "gu1", "d1"))
AG_ORDER = tuple(n for grp in AG_GROUPS for n in grp)
RS_GROUPS = (("d1", "gu1"), ("pg", "wpi"), ("d0", "gu0"), ("wo", "qkv"))


def _hbm(a):
    return pltpu.with_memory_space_constraint(a, pltpu.HBM)


def _remote(src, dst, send, recv, peer):
    return pltpu.make_async_remote_copy(src_ref=src, dst_ref=dst, send_sem=send, recv_sem=recv, device_id=peer,
                                        device_id_type=pl.DeviceIdType.MESH)


def _bcast_all(v, name):
    W = v.shape[1]

    def kern(v_ref, o_ref, send, recv, lsem):
        x, y, c, me = _mesh_pos()
        own = pltpu.make_async_copy(v_ref, o_ref.at[me], lsem)
        own.start()
        cps = [_remote(v_ref, o_ref.at[me], send.at[k - 1], recv.at[k - 1], _peer(x, y, c, k)[0])
               for k in range(1, NDEV)]
        for cp in cps:
            cp.start()
        for cp in cps:
            cp.wait_recv()
            cp.wait_send()
        own.wait()

    return pl.pallas_call(
        kern, in_specs=[ANY], out_specs=ANY, out_shape=jax.ShapeDtypeStruct((NDEV, 8, W), F32),
        scratch_shapes=[pltpu.SemaphoreType.DMA((NPEER,)), pltpu.SemaphoreType.DMA((NPEER,)),
                        pltpu.SemaphoreType.DMA(())],
        name=name)(v)


def _place_own(srcs, src_rows, dst_rows, out_shapes, name):
    n = len(srcs)

    def kern(*refs):
        ins, outs, sem = refs[:n], refs[n:2 * n], refs[2 * n]
        _, _, _, me = _mesh_pos()
        cps = [pltpu.make_async_copy(src_rows(j, ins[j], me), dst_rows(j, outs[j], me), sem.at[j]) for j in range(n)]
        for cp in cps:
            cp.start()
        for cp in cps:
            cp.wait()

    return pl.pallas_call(
        kern, in_specs=[ANY] * n, out_specs=[ANY] * n, out_shape=out_shapes,
        scratch_shapes=[pltpu.SemaphoreType.DMA((n,))], name=name)(*srcs)


def _split_start(srcs, src_of, lands, copy_refs, name):
    ns, n = len(srcs), len(lands)

    def body(*refs):
        ins, land = refs[:ns], refs[ns:ns + n]
        send, recv = refs[ns + n], refs[ns + n + 1]
        token = refs[-1]
        x, y, c, me = _mesh_pos()
        for j in range(n):
            for k in range(1, NDEV):
                peer, pid = _peer(x, y, c, k)
                src, dst = copy_refs(j, ins[src_of[j]], land[j], me, pid)
                _remote(src, dst, send.at[j * NPEER + k - 1], recv.at[j * NPEER + k - 1], peer).start()
        token[...] = jnp.zeros_like(token)

    outs = pl.pallas_call(
        body, name=name,
        out_shape=(pltpu.SemaphoreType.DMA((n * NPEER,)), pltpu.SemaphoreType.DMA((n * NPEER,)))
        + tuple(pltpu.HBM(a.shape, a.dtype) for a in srcs) + tuple(pltpu.HBM(a.shape, a.dtype) for a in lands)
        + (jax.ShapeDtypeStruct((8, 128), F32),),
        in_specs=(HBM,) * (ns + n),
        out_specs=(SEMS, SEMS) + (HBM,) * (ns + n) + (pl.BlockSpec(memory_space=pltpu.VMEM),),
        input_output_aliases={i: 2 + i for i in range(ns + n)},
        compiler_params=pltpu.CompilerParams(has_side_effects=EFFECT),
    )(*[_hbm(a) for a in srcs], *[_hbm(a) for a in lands])
    return outs[0], outs[1], list(outs[2:2 + ns]), list(outs[2 + ns:2 + ns + n]), outs[-1]


def _split_wait(srcs, src_of, lands, send, recv, sem_rows, wait_refs, after, name):
    ns, n = len(srcs), len(lands)

    def body(*refs):
        ins, land = refs[:ns], refs[ns:ns + n]
        send_ref, recv_ref = refs[ns + n], refs[ns + n + 1]
        x, y, c, me = _mesh_pos()
        for j in range(n):
            for k in range(1, NDEV):
                peer, _ = _peer(x, y, c, k)
                src, dst = wait_refs(j, ins[src_of[j]], land[j])
                sem = sem_rows[j] * NPEER + k - 1
                cp = _remote(src, dst, send_ref.at[sem], recv_ref.at[sem], peer)
                cp.wait_send()
                cp.wait_recv()

    outs = pl.pallas_call(
        body, name=name,
        out_shape=tuple(pltpu.HBM(a.shape, a.dtype) for a in srcs) + tuple(pltpu.HBM(a.shape, a.dtype) for a in lands),
        in_specs=(HBM,) * (ns + n) + (SEMS, SEMS, ANY),
        out_specs=(HBM,) * (ns + n),
        input_output_aliases={i: i for i in range(ns + n)},
        compiler_params=pltpu.CompilerParams(has_side_effects=EFFECT),
    )(*srcs, *lands, send, recv, after)
    return list(outs[:ns]), list(outs[ns:])


class _Comm:
    def __init__(self, pack_bf16):
        names = AG_ORDER
        rows = [SEC_ROWS[n] for n in names]
        lands = _place_own(
            [pack_bf16] * len(names),
            lambda j, ref, me: ref.at[pl.ds(LOC_OFF[names[j]], rows[j])],
            lambda j, ref, me: ref.at[pl.ds(pl.multiple_of(_shard_pos(names[j], me), 16), rows[j])],
            [jax.ShapeDtypeStruct((NDEV * r, D), BF16) for r in rows], "ag_place_own")

        def copy_refs(j, src, land, me, pid):
            return (src.at[pl.ds(LOC_OFF[names[j]], rows[j])],
                    land.at[pl.ds(pl.multiple_of(_shard_pos(names[j], me), 16), rows[j])])

        self.ag_send, self.ag_recv, src, lands, _ = _split_start(
            [pack_bf16], [0] * len(names), lands, copy_refs, "ag_start")
        self.ag_src = src[0]
        self.ag_land = dict(zip(names, lands))
        self.rs = []

    def weights(self, group, after):
        names = AG_GROUPS[group]
        idx = [AG_ORDER.index(n) for n in names]
        rows = [SEC_ROWS[n] for n in names]

        def wait_refs(j, src, land):
            return src.at[pl.ds(LOC_OFF[names[j]], rows[j])], land.at[pl.ds(0, rows[j])]

        src, lands = _split_wait([self.ag_src], [0] * len(names), [self.ag_land[n] for n in names], self.ag_send,
                                 self.ag_recv, idx, wait_refs, after, f"ag_wait{group}")
        self.ag_src = src[0]
        return dict(zip(names, lands))

    def send_grads(self, group, gws):
        names = RS_GROUPS[group]
        rows = [SEC_ROWS[n] for n in names]
        grads = [gws[n] for n in names]
        lands = _place_own(
            grads,
            lambda j, ref, me: ref.at[pl.ds(pl.multiple_of(_shard_pos(names[j], me), 16), rows[j])],
            lambda j, ref, me: ref.at[me],
            [jax.ShapeDtypeStruct((NDEV, r, D), BF16) for r in rows], f"rs_place_own{group}")

        def copy_refs(j, src, land, me, pid):
            return src.at[pl.ds(pl.multiple_of(_shard_pos(names[j], pid), 16), rows[j])], land.at[me]

        send, recv, srcs, lands, token = _split_start(grads, list(range(len(names))), lands, copy_refs,
                                                      f"rs_start{group}")
        self.rs.append((names, rows, send, recv, srcs, lands))
        return token

    def received(self, group, after):
        names, rows, send, recv, srcs, lands = self.rs[group]

        def wait_refs(j, src, land):
            return src.at[pl.ds(0, rows[j])], land.at[0]

        _, lands = _split_wait(srcs, list(range(len(names))), lands, send, recv, list(range(len(names))), wait_refs,
                               after, f"rs_wait{group}")
        return dict(zip(names, lands))


def _adamw(R, w, m, v, *, rows, tr, name, off=0):
    c1 = 1.0 / (1.0 - ADAM_B1 ** ADAM_STEP)
    c2 = 1.0 / (1.0 - ADAM_B2 ** ADAM_STEP)

    def kern(r_ref, w_ref, m_ref, v_ref, g_out, d_out, m_out, v_out):
        g = r_ref[0].astype(F32)
        for dev in range(1, NDEV):
            g = g + r_ref[dev].astype(F32)
        mn = ADAM_B1 * m_ref[...] + (1.0 - ADAM_B1) * g
        vn = ADAM_B2 * v_ref[...] + (1.0 - ADAM_B2) * (g * g)
        g_out[...] = g
        m_out[...] = mn
        v_out[...] = vn
        d_out[...] = -ADAM_LR * ((mn * c1) / (jnp.sqrt(vn * c2) + ADAM_EPS) + ADAM_WD * w_ref[...])

    tile = pl.BlockSpec((tr, D), lambda i: (i, 0))
    wtile = pl.BlockSpec((tr, D), lambda i: (i + off, 0))
    shp = jax.ShapeDtypeStruct((rows, D), F32)
    return pl.pallas_call(
        kern, grid=(rows // tr,),
        in_specs=[pl.BlockSpec((NDEV, tr, D), lambda i: (0, i, 0)), wtile, wtile, wtile],
        out_specs=[tile] * 4, out_shape=[shp] * 4, compiler_params=_cparams(1), name=name)(R, w, m, v)


ADAM_TILE = {"qkv": 384, "wo": 128, "wpi": 128, "gu0": 352, "gu1": 352, "d0": 352, "d1": 352, "pg": 32}


def _pack(w_qkv, w_attn_out, w_pool_in, w_pool_group, w_ffn_gate_up, w_ffn_down):
    pg = w_pool_group[0].transpose(1, 0, 2).reshape(SEC_ROWS["pg"], D)
    return jnp.concatenate([
        w_qkv[0].T, w_attn_out[0], w_pool_in[0], w_ffn_gate_up[0].T, w_ffn_gate_up[1].T,
        w_ffn_down[0], w_ffn_down[1], pg], axis=0)


def _unpack(p):
    def sec(n):
        return p[n]
    w_qkv = sec("qkv").T[None]
    w_attn_out = sec("wo")[None]
    w_pool_in = sec("wpi")[None]
    w_pool_group = sec("pg").reshape(SEC_ROWS["pg"], POOL_G, PGD).transpose(1, 0, 2)[None]
    w_gu = jnp.stack([sec("gu0").T, sec("gu1").T])
    w_d = jnp.stack([sec("d0"), sec("d1")])
    return w_qkv, w_attn_out, w_pool_in, w_pool_group, w_gu, w_d


def _vec_pack(attn_norm, ffn_norm, final_norm, pool_norm_sh, pool_scale_sh, me):
    def place(sh):
        return lax.dynamic_update_slice(jnp.zeros((1, D), F32), sh, (0, me * 128))
    return jnp.concatenate([attn_norm, ffn_norm, final_norm.reshape(1, D), place(pool_norm_sh),
                            place(pool_scale_sh), jnp.zeros((2, D), F32)], axis=0)


def _vec_unpack(p, me):
    def take(r):
        return lax.dynamic_slice(p[r:r + 1], (0, me * 128), (1, 128))
    return p[0:1], p[1:3], p[3], take(4), take(5)


def kernel(x, attn_norm, w_qkv, w_attn_out, pool_norm, w_pool_in, w_pool_group, pool_scale, ffn_norm, w_ffn_gate_up, w_ffn_down, final_norm, loss_target, m_attn_norm, m_w_qkv, m_w_attn_out, m_pool_norm, m_w_pool_in, m_w_pool_group, m_pool_scale, m_ffn_norm, m_w_ffn_gate_up, m_w_ffn_down, m_final_norm, v_attn_norm, v_w_qkv, v_w_attn_out, v_pool_norm, v_w_pool_in, v_w_pool_group, v_pool_scale, v_ffn_norm, v_w_ffn_gate_up, v_w_ffn_down, v_final_norm):
    me = 4 * lax.axis_index("x") + 2 * lax.axis_index("y") + lax.axis_index("c")

    pw = _pack(w_qkv, w_attn_out, w_pool_in, w_pool_group, w_ffn_gate_up, w_ffn_down)
    pm = _pack(m_w_qkv, m_w_attn_out, m_w_pool_in, m_w_pool_group, m_w_ffn_gate_up, m_w_ffn_down)
    pv = _pack(v_w_qkv, v_w_attn_out, v_w_pool_in, v_w_pool_group, v_w_ffn_gate_up, v_w_ffn_down)
    vsh = jnp.concatenate([pool_norm, pool_scale, jnp.zeros((6, 128), F32)], axis=0)

    comm = _Comm(pw.astype(BF16))
    vg = _bcast_all(vsh, "gather_pool_vectors")
    pool_norm_full = vg[:, 0, :].reshape(1, D)
    pool_scale_full = vg[:, 1, :].reshape(1, D)

    loss, grad_x, vec = _local_step(x[0], loss_target[0], comm, attn_norm, ffn_norm, final_norm,
                                    pool_norm_full, pool_scale_full)
    loss = lax.psum(loss, MESH_AXES)

    VR = _bcast_all(vec, "exchange_vector_grads")
    vw = _vec_pack(attn_norm, ffn_norm, final_norm, pool_norm, pool_scale, me)
    vm = _vec_pack(m_attn_norm, m_ffn_norm, m_final_norm, m_pool_norm, m_pool_scale, me)
    vv = _vec_pack(v_attn_norm, v_ffn_norm, v_final_norm, v_pool_norm, v_pool_scale, me)
    vec_out = _adamw(VR, vw, vm, vv, rows=8, tr=8, name="adamw_vec")

    sec_out = [{}, {}, {}, {}]
    after = vec_out[0]
    for group in range(len(RS_GROUPS)):
        for n, R in comm.received(group, after).items():
            tr = ADAM_TILE[n]
            res = _adamw(R, pw, pm, pv, rows=SEC_ROWS[n], tr=tr, off=LOC_OFF[n] // tr, name=f"adamw_{n}")
            for kind in range(4):
                sec_out[kind][n] = res[kind]
            after = res[0]

    outs = []
    for kind in range(4):
        q, o, pi, pg, gu, dn = _unpack(sec_out[kind])
        an, fn, fin, pn, ps = _vec_unpack(vec_out[kind], me)
        outs.append((an, q, o, pn, pi, pg, ps, fn, gu, dn, fin))
    return (loss, grad_x[None]) + outs[0] + outs[1] + outs[2] + outs[3]
```

```python
import jax
import jax.numpy as jnp
from jax import lax
from jax.experimental import pallas as pl
from jax.experimental.pallas import tpu as pltpu

F32 = jnp.float32
BF16 = jnp.bfloat16

D = 1024
NDEV = 8
HEADS = 8
HD = 128
QB = 128
NGROUPS = 3
DILS = (1, 4, 16)
DFF = 2816
HCH = 1408
POOL_G = 4
PGD = 256
RMS_EPS = 1e-6
NEG = -1e30

ADAM_LR = 0.001
ADAM_B1 = 0.9
ADAM_B2 = 0.999
ADAM_EPS = 1e-08
ADAM_WD = 0.01
ADAM_STEP = 10

VMEM_LIMIT = 52 * 1024 * 1024
MESH_AXES = ("x", "y", "c")

SECTIONS = (("qkv", 1152), ("wo", 128), ("wpi", 128), ("gu0", 704), ("gu1", 704),
            ("d0", 352), ("d1", 352), ("pg", 32))
LOC_OFF = {}
GLB_OFF = {}
_o = 0
for _n, _r in SECTIONS:
    LOC_OFF[_n] = _o
    GLB_OFF[_n] = _o * NDEV
    _o += _r
PACK_ROWS = _o
GLB_ROWS = PACK_ROWS * NDEV
SEC_ROWS = dict(SECTIONS)


def _cparams(n_grid):
    return pltpu.CompilerParams(dimension_semantics=("arbitrary",) * n_grid, vmem_limit_bytes=VMEM_LIMIT)


def _shard_pos(name, dev):
    n = SEC_ROWS[name]
    if name in ("gu0", "gu1"):
        return ((dev % 4) // 2) * (2 * HCH) + (dev // 4) * HCH + (dev % 2) * n
    return dev * n


def _mm(a, b, *, mode, M, N, K, tm, tn, tk, out_dtype, name, nbat=1, a_off=(0, 0), b_off=(0, 0),
        a_boff=(0, 0), b_boff=(0, 0), res=None, out_2d_rows=None):
    nm, nn, nk = M // tm, N // tn, K // tk
    assert nm * tm == M and nn * tn == N and nk * tk == K
    if mode == "nn":
        a_bs, b_bs = (tm, tk), (tk, tn)
        a_ix = lambda i, j, k: (i, k)
        b_ix = lambda i, j, k: (k, j)
        dims = (((1,), (0,)), ((), ()))
    elif mode == "nt":
        a_bs, b_bs = (tm, tk), (tn, tk)
        a_ix = lambda i, j, k: (i, k)
        b_ix = lambda i, j, k: (j, k)
        dims = (((1,), (1,)), ((), ()))
    else:
        a_bs, b_bs = (tk, tm), (tk, tn)
        a_ix = lambda i, j, k: (k, i)
        b_ix = lambda i, j, k: (k, j)
        dims = (((0,), (0,)), ((), ()))

    def spec(arr, bs, ix, off, boff):
        if arr.ndim == 3:
            return pl.BlockSpec((None,) + bs, lambda bb, i, j, k: (bb,) + ix(i, j, k))

        def im(bb, i, j, k):
            r, c = ix(i, j, k)
            return (r + off[0] + bb * boff[0], c + off[1] + bb * boff[1])
        return pl.BlockSpec(bs, im)

    in_specs = [spec(a, a_bs, a_ix, a_off, a_boff), spec(b, b_bs, b_ix, b_off, b_boff)]
    args = [a, b]
    if res is not None:
        in_specs.append(pl.BlockSpec((tm, tn), lambda bb, i, j, k: (i, j)))
        args.append(res)
    if nbat > 1 and out_2d_rows is None:
        out_shape = jax.ShapeDtypeStruct((nbat, M, N), out_dtype)
        out_spec = pl.BlockSpec((None, tm, tn), lambda bb, i, j, k: (bb, i, j))
    else:
        rows = M if out_2d_rows is None else out_2d_rows
        out_shape = jax.ShapeDtypeStruct((rows, N), out_dtype)
        out_spec = pl.BlockSpec((tm, tn), lambda bb, i, j, k: (i + bb * nm, j))
    has_res = res is not None

    def kern(*refs):
        a_ref, b_ref = refs[0], refs[1]
        res_ref = refs[2] if has_res else None
        o_ref = refs[3] if has_res else refs[2]
        av = a_ref[...]
        bv = b_ref[...]
        if av.dtype != BF16:
            av = av.astype(BF16)
        if bv.dtype != BF16:
            bv = bv.astype(BF16)
        part = lax.dot_general(av, bv, dims, preferred_element_type=F32)

        def write(val):
            if has_res:
                val = val + res_ref[...]
            o_ref[...] = val.astype(out_dtype)

        if nk == 1:
            write(part)
        else:
            acc_ref = refs[-1]
            k = pl.program_id(3)

            @pl.when(k == 0)
            def _():
                acc_ref[...] = part

            @pl.when(k > 0)
            def _():
                acc_ref[...] += part

            @pl.when(k == nk - 1)
            def _():
                write(acc_ref[...])

    scratch = [pltpu.VMEM((tm, tn), F32)] if nk > 1 else []
    return pl.pallas_call(
        kern, grid=(nbat, nm, nn, nk), in_specs=in_specs, out_specs=out_spec, out_shape=out_shape,
        scratch_shapes=scratch, compiler_params=_cparams(4), name=name)(*args)


def _rms_fwd(x, g, name):
    S = x.shape[0]
    tr = 512

    def kern(x_ref, g_ref, h_ref):
        xv = x_ref[...]
        r = lax.rsqrt(jnp.mean(xv * xv, axis=-1, keepdims=True) + RMS_EPS)
        h_ref[...] = (xv * r * g_ref[...]).astype(BF16)

    return pl.pallas_call(
        kern, grid=(S // tr,),
        in_specs=[pl.BlockSpec((tr, D), lambda i: (i, 0)), pl.BlockSpec((1, D), lambda i: (0, 0))],
        out_specs=pl.BlockSpec((tr, D), lambda i: (i, 0)),
        out_shape=jax.ShapeDtypeStruct((S, D), BF16), compiler_params=_cparams(1), name=name)(x, g)


def _rms_bwd(dh, x, g, dres, name):
    S = x.shape[0]
    tr = 512

    def kern(dh_ref, x_ref, g_ref, dres_ref, dx_ref, dg_ref):
        i = pl.program_id(0)
        xv = x_ref[...]
        dhv = dh_ref[...].astype(F32)
        r = lax.rsqrt(jnp.mean(xv * xv, axis=-1, keepdims=True) + RMS_EPS)
        xhat = xv * r
        gy = dhv * g_ref[...]
        dx_ref[...] = dres_ref[...] + r * (gy - xhat * jnp.mean(gy * xhat, axis=-1, keepdims=True))
        part = jnp.sum(dhv * xhat, axis=0, keepdims=True)

        @pl.when(i == 0)
        def _():
            dg_ref[...] = part

        @pl.when(i > 0)
        def _():
            dg_ref[...] += part

    row = pl.BlockSpec((tr, D), lambda i: (i, 0))
    vec = pl.BlockSpec((1, D), lambda i: (0, 0))
    return pl.pallas_call(
        kern, grid=(S // tr,), in_specs=[row, row, vec, row], out_specs=[row, vec],
        out_shape=[jax.ShapeDtypeStruct((S, D), F32), jax.ShapeDtypeStruct((1, D), F32)],
        compiler_params=_cparams(1), name=name)(dh, x, g, dres)


def _loss_head(x, g, tgt):
    S = x.shape[0]
    tr = 512

    def kern(x_ref, g_ref, t_ref, dx_ref, dg_ref, ls_ref):
        i = pl.program_id(0)
        xv = x_ref[...]
        gv = g_ref[...]
        r = lax.rsqrt(jnp.mean(xv * xv, axis=-1, keepdims=True) + RMS_EPS)
        xhat = xv * r
        e = xhat * gv - t_ref[...]
        dy = e * (1.0 / D)
        gy = dy * gv
        dx_ref[...] = r * (gy - xhat * jnp.mean(gy * xhat, axis=-1, keepdims=True))
        dgp = jnp.sum(dy * xhat, axis=0, keepdims=True)
        lsp = jnp.sum(e * e, axis=0, keepdims=True)

        @pl.when(i == 0)
        def _():
            dg_ref[...] = dgp
            ls_ref[...] = lsp

        @pl.when(i > 0)
        def _():
            dg_ref[...] += dgp
            ls_ref[...] += lsp

    row = pl.BlockSpec((tr, D), lambda i: (i, 0))
    vec = pl.BlockSpec((1, D), lambda i: (0, 0))
    return pl.pallas_call(
        kern, grid=(S // tr,), in_specs=[row, vec, row], out_specs=[row, vec, vec],
        out_shape=[jax.ShapeDtypeStruct((S, D), F32), jax.ShapeDtypeStruct((1, D), F32),
                   jax.ShapeDtypeStruct((1, D), F32)],
        compiler_params=_cparams(1), name="loss_head")(x, g, tgt)


def _ffn_up(h, G, name):
    S = h.shape[0]
    tm = 512
    nj = DFF // HCH

    def kern(h_ref, w_ref, gu_ref, act_ref):
        gu = lax.dot_general(h_ref[...], w_ref[...], (((1,), (1,)), ((), ())), preferred_element_type=F32)
        gu_ref[...] = gu.astype(BF16)
        gate = gu[:, :HCH]
        up = gu[:, HCH:]
        act_ref[...] = (gate * jax.nn.sigmoid(gate) * up).astype(BF16)

    return pl.pallas_call(
        kern, grid=(nj, S // tm),
        in_specs=[pl.BlockSpec((tm, D), lambda j, i: (i, 0)),
                  pl.BlockSpec((2 * HCH, D), lambda j, i: (j, 0))],
        out_specs=[pl.BlockSpec((tm, 2 * HCH), lambda j, i: (i, j)),
                   pl.BlockSpec((tm, HCH), lambda j, i: (i, j))],
        out_shape=[jax.ShapeDtypeStruct((S, 2 * DFF), BF16), jax.ShapeDtypeStruct((S, DFF), BF16)],
        compiler_params=_cparams(2), name=name)(h, G)


def _ffn_down_bwd(dx, G, gu, name):
    S = dx.shape[0]
    tm = 512
    nj = DFF // HCH

    def kern(dx_ref, w_ref, gu_ref, o_ref):
        dact = lax.dot_general(dx_ref[...].astype(BF16), w_ref[...], (((1,), (1,)), ((), ())),
                               preferred_element_type=F32)
        gate = gu_ref[:, :HCH].astype(F32)
        up = gu_ref[:, HCH:].astype(F32)
        sig = jax.nn.sigmoid(gate)
        silu = gate * sig
        o_ref[:, :HCH] = (dact * up * (sig * (1.0 + gate * (1.0 - sig)))).astype(BF16)
        o_ref[:, HCH:] = (dact * silu).astype(BF16)

    return pl.pallas_call(
        kern, grid=(nj, S // tm),
        in_specs=[pl.BlockSpec((tm, D), lambda j, i: (i, 0)),
                  pl.BlockSpec((HCH, D), lambda j, i: (j, 0)),
                  pl.BlockSpec((tm, 2 * HCH), lambda j, i: (i, j))],
        out_specs=pl.BlockSpec((tm, 2 * HCH), lambda j, i: (i, j)),
        out_shape=jax.ShapeDtypeStruct((S, 2 * DFF), BF16),
        compiler_params=_cparams(2), name=name)(dx, G, gu)


def _trail(u, *, backward, name):
    S = u.shape[0]

    def kern(u_ref, o_ref):
        g = pl.program_id(0)
        uv = u_ref[...].astype(F32)
        row = lax.broadcasted_iota(jnp.int32, uv.shape, 0)
        win = jnp.left_shift(jnp.int32(2), g)
        cnt = jnp.minimum(row + 1, win).astype(F32)
        s = uv / cnt if backward else uv
        levels = []
        for k in (1, 2, 4, 8):
            if backward:
                sh = jnp.where(row < S - k, pltpu.roll(s, S - k, 0), 0.0)
            else:
                sh = jnp.where(row >= k, pltpu.roll(s, k, 0), 0.0)
            s = s + sh
            levels.append(s)
        sel = jnp.where(g == 0, levels[0], jnp.where(g == 1, levels[1], jnp.where(g == 2, levels[2], levels[3])))
        if backward:
            o_ref[...] = (sel - uv).astype(BF16)
        else:
            o_ref[...] = (sel / cnt - uv).astype(BF16)

    blk = pl.BlockSpec((S, PGD), lambda g: (0, g))
    return pl.pallas_call(
        kern, grid=(POOL_G,), in_specs=[blk], out_specs=blk,
        out_shape=jax.ShapeDtypeStruct((S, D), BF16), compiler_params=_cparams(1), name=name)(u)


def _pool_out(yd, G, scale, xres):
    S = yd.shape[0]
    tm = 1024

    def kern(y_ref, w_ref, s_ref, x_ref, o_ref):
        z = jnp.dot(y_ref[...], w_ref[...], preferred_element_type=F32)
        o_ref[...] = x_ref[...] + z * s_ref[...]

    tile = pl.BlockSpec((tm, PGD), lambda i, g: (i, g))
    return pl.pallas_call(
        kern, grid=(S // tm, POOL_G),
        in_specs=[tile, pl.BlockSpec((PGD, PGD), lambda i, g: (0, g)),
                  pl.BlockSpec((1, PGD), lambda i, g: (0, g)), tile],
        out_specs=tile, out_shape=jax.ShapeDtypeStruct((S, D), F32),
        compiler_params=_cparams(2), name="pool_out")(yd, G, scale, xres)


def _pool_out_bwd(dz, yd, G, scale):
    S = yd.shape[0]
    tm = 1024
    ni = S // tm

    def kern(dz_ref, y_ref, w_ref, s_ref, dy_ref, ds_ref, dw_ref, acc_ref):
        i = pl.program_id(1)
        dzv = dz_ref[...]
        yv = y_ref[...]
        wv = w_ref[...]
        zraw = jnp.dot(yv, wv, preferred_element_type=F32)
        dsp = jnp.sum(dzv * zraw, axis=0, keepdims=True)
        dzr = (dzv * s_ref[...]).astype(BF16)
        dy_ref[...] = lax.dot_general(dzr, wv, (((1,), (1,)), ((), ())), preferred_element_type=F32)
        dwp = lax.dot_general(yv, dzr, (((0,), (0,)), ((), ())), preferred_element_type=F32)

        @pl.when(i == 0)
        def _():
            ds_ref[...] = dsp
            acc_ref[...] = dwp

        @pl.when(i > 0)
        def _():
            ds_ref[...] += dsp
            acc_ref[...] += dwp

        @pl.when(i == ni - 1)
        def _():
            dw_ref[...] = acc_ref[...].astype(BF16)

    tile = pl.BlockSpec((tm, PGD), lambda g, i: (i, g))
    return pl.pallas_call(
        kern, grid=(POOL_G, ni),
        in_specs=[tile, tile, pl.BlockSpec((PGD, PGD), lambda g, i: (0, g)),
                  pl.BlockSpec((1, PGD), lambda g, i: (0, g))],
        out_specs=[tile, pl.BlockSpec((1, PGD), lambda g, i: (0, g)),
                   pl.BlockSpec((PGD, PGD), lambda g, i: (0, g))],
        out_shape=[jax.ShapeDtypeStruct((S, D), F32), jax.ShapeDtypeStruct((1, D), F32),
                   jax.ShapeDtypeStruct((PGD, D), BF16)],
        scratch_shapes=[pltpu.VMEM((PGD, PGD), F32)],
        compiler_params=_cparams(2), name="pool_out_bwd")(dz, yd, G, scale)


def _bias_table():
    qi = jnp.arange(QB)[:, None]
    ki = jnp.arange(2 * QB)[None, :]
    delta = QB + qi - ki
    inband = (delta >= 0) & (delta <= QB)
    n = NGROUPS * HEADS
    slopes = jnp.exp2(-8.0 * jnp.arange(1, n + 1, dtype=F32) / n).reshape(NGROUPS, HEADS)
    dil = jnp.asarray(DILS, F32)
    bias = -slopes[:, :, None, None] * (delta.astype(F32)[None, None] * dil[:, None, None, None])
    return jnp.where(inband[None, None], bias, NEG)


def _group_scalars(g, b, nblk):
    nb = jnp.right_shift(jnp.int32(nblk), 2 * g)
    has_prev = jnp.bitwise_and(b, nb - 1) != 0
    return has_prev


def _attn_fwd(qkv_f, bias):
    S = qkv_f.shape[1]
    nblk = S // QB
    scale = HD ** -0.5

    def kern(q_ref, kc_ref, kp_ref, vc_ref, vp_ref, b_ref, o_ref, l_ref):
        g = pl.program_id(0)
        b = pl.program_id(1)
        has_prev = _group_scalars(g, b, nblk)
        col = lax.broadcasted_iota(jnp.int32, (QB, 2 * QB), 1)
        dead = jnp.logical_and(col < QB, jnp.logical_not(has_prev))
        lane = lax.broadcasted_iota(jnp.int32, (QB, HD), 1)
        lse_all = jnp.zeros((QB, HD), F32)
        for h in range(HEADS):
            sl = slice(h * HD, (h + 1) * HD)
            kk = jnp.concatenate([kp_ref[:, sl], kc_ref[:, sl]], axis=0)
            vv = jnp.concatenate([vp_ref[:, sl], vc_ref[:, sl]], axis=0)
            s = lax.dot_general(q_ref[:, sl], kk, (((1,), (1,)), ((), ())), preferred_element_type=F32)
            s = s * scale + b_ref[h]
            s = jnp.where(dead, NEG, s)
            m = jnp.max(s, axis=-1, keepdims=True)
            p = jnp.exp(s - m)
            den = jnp.sum(p, axis=-1, keepdims=True)
            o = jnp.dot(p.astype(BF16), vv, preferred_element_type=F32) / den
            o_ref[:, sl] = o.astype(BF16)
            lse_all = jnp.where(lane == h, m + jnp.log(den), lse_all)
        l_ref[...] = lse_all

    def blk(colblk, prev):
        if prev:
            return pl.BlockSpec((None, QB, D), lambda g, b: (g, jnp.maximum(b - 1, 0), colblk))
        return pl.BlockSpec((None, QB, D), lambda g, b: (g, b, colblk))

    return pl.pallas_call(
        kern, grid=(NGROUPS, nblk),
        in_specs=[blk(0, False), blk(1, False), blk(1, True), blk(2, False), blk(2, True),
                  pl.BlockSpec((None, HEADS, QB, 2 * QB), lambda g, b: (g, 0, 0, 0))],
        out_specs=[pl.BlockSpec((None, QB, D), lambda g, b: (g, b, 0)),
                   pl.BlockSpec((None, QB, HD), lambda g, b: (g, b, 0))],
        out_shape=[jax.ShapeDtypeStruct((NGROUPS, S, D), BF16), jax.ShapeDtypeStruct((NGROUPS, S, HD), F32)],
        compiler_params=_cparams(2), name="attn_fwd")(qkv_f, qkv_f, qkv_f, qkv_f, qkv_f, bias)


def _attn_merge(o_n, lse_n):
    S = o_n.shape[1]
    tm = 512

    def kern(o_ref, l_ref, om_ref, lm_ref):
        l0, l1, l2 = l_ref[0], l_ref[1], l_ref[2]
        m = jnp.maximum(jnp.maximum(l0, l1), l2)
        e0, e1, e2 = jnp.exp(l0 - m), jnp.exp(l1 - m), jnp.exp(l2 - m)
        tot = e0 + e1 + e2
        lm_ref[...] = m + jnp.log(tot)
        w0, w1, w2 = e0 / tot, e1 / tot, e2 / tot
        for h in range(HEADS):
            sl = slice(h * HD, (h + 1) * HD)
            acc = (w0[:, h:h + 1] * o_ref[0, :, sl].astype(F32) + w1[:, h:h + 1] * o_ref[1, :, sl].astype(F32)
                   + w2[:, h:h + 1] * o_ref[2, :, sl].astype(F32))
            om_ref[:, sl] = acc.astype(BF16)

    return pl.pallas_call(
        kern, grid=(S // tm,),
        in_specs=[pl.BlockSpec((NGROUPS, tm, D), lambda i: (0, i, 0)),
                  pl.BlockSpec((NGROUPS, tm, HD), lambda i: (0, i, 0))],
        out_specs=[pl.BlockSpec((tm, D), lambda i: (i, 0)), pl.BlockSpec((tm, HD), lambda i: (i, 0))],
        out_shape=[jax.ShapeDtypeStruct((S, D), BF16), jax.ShapeDtypeStruct((S, HD), F32)],
        compiler_params=_cparams(1), name="attn_merge")(o_n, lse_n)


def _attn_delta(do, o):
    S = o.shape[0]
    tm = 512

    def kern(do_ref, o_ref, d_ref):
        lane = lax.broadcasted_iota(jnp.int32, (tm, HD), 1)
        acc = jnp.zeros((tm, HD), F32)
        for h in range(HEADS):
            sl = slice(h * HD, (h + 1) * HD)
            prod = do_ref[:, sl].astype(F32) * o_ref[:, sl].astype(F32)
            acc = jnp.where(lane == h, jnp.sum(prod, axis=-1, keepdims=True), acc)
        d_ref[...] = acc

    row = pl.BlockSpec((tm, D), lambda i: (i, 0))
    return pl.pallas_call(
        kern, grid=(S // tm,), in_specs=[row, row], out_specs=pl.BlockSpec((tm, HD), lambda i: (i, 0)),
        out_shape=jax.ShapeDtypeStruct((S, HD), F32), compiler_params=_cparams(1), name="attn_delta")(do, o)


def _attn_bwd(qkv_f, do_f, lse_f, delta_f, bias):
    S = qkv_f.shape[1]
    nblk = S // QB
    scale = HD ** -0.5

    def kern(q_ref, kc_ref, kp_ref, vc_ref, vp_ref, do_ref, l_ref, d_ref, b_ref, out_ref, dq_c, dk_c, dv_c):
        g = pl.program_id(0)
        b = pl.program_id(1)

        @pl.when(b == 0)
        def _():
            dq_c[...] = jnp.zeros_like(dq_c)
            dk_c[...] = jnp.zeros_like(dk_c)
            dv_c[...] = jnp.zeros_like(dv_c)

        @pl.when(b == nblk)
        def _():
            out_ref[:, 0:D] = dq_c[...].astype(BF16)
            out_ref[:, D:2 * D] = dk_c[...].astype(BF16)
            out_ref[:, 2 * D:3 * D] = dv_c[...].astype(BF16)

        @pl.when(b < nblk)
        def _():
            has_prev = _group_scalars(g, b, nblk)
            col = lax.broadcasted_iota(jnp.int32, (QB, 2 * QB), 1)
            dead = jnp.logical_and(col < QB, jnp.logical_not(has_prev))
            out_ref[:, 0:D] = dq_c[...].astype(BF16)
            lv = l_ref[...]
            dv_ = d_ref[...]
            for h in range(HEADS):
                sl = slice(h * HD, (h + 1) * HD)
                qh = q_ref[:, sl]
                doh = do_ref[:, sl]
                kk = jnp.concatenate([kp_ref[:, sl], kc_ref[:, sl]], axis=0)
                vv = jnp.concatenate([vp_ref[:, sl], vc_ref[:, sl]], axis=0)
                s = lax.dot_general(qh, kk, (((1,), (1,)), ((), ())), preferred_element_type=F32)
                s = s * scale + b_ref[h]
                s = jnp.where(dead, NEG, s)
                p = jnp.exp(s - lv[:, h:h + 1])
                dp = lax.dot_general(doh, vv, (((1,), (1,)), ((), ())), preferred_element_type=F32)
                ds = (p * (dp - dv_[:, h:h + 1]) * scale).astype(BF16)
                pb = p.astype(BF16)
                dq_c[:, sl] = jnp.dot(ds, kk, preferred_element_type=F32)
                dkk = lax.dot_general(ds, qh, (((0,), (0,)), ((), ())), preferred_element_type=F32)
                dvv = lax.dot_general(pb, doh, (((0,), (0,)), ((), ())), preferred_element_type=F32)
                out_ref[:, D + h * HD:D + (h + 1) * HD] = (dk_c[:, sl] + dkk[:QB]).astype(BF16)
                out_ref[:, 2 * D + h * HD:2 * D + (h + 1) * HD] = (dv_c[:, sl] + dvv[:QB]).astype(BF16)
                dk_c[:, sl] = dkk[QB:]
                dv_c[:, sl] = dvv[QB:]

    last = nblk - 1

    def blk(colblk, prev):
        if prev:
            return pl.BlockSpec((None, QB, D), lambda g, b: (g, jnp.maximum(jnp.minimum(b, last) - 1, 0), colblk))
        return pl.BlockSpec((None, QB, D), lambda g, b: (g, jnp.minimum(b, last), colblk))

    stat = pl.BlockSpec((None, QB, HD), lambda g, b: (g, jnp.minimum(b, last), 0))
    return pl.pallas_call(
        kern, grid=(NGROUPS, nblk + 1),
        in_specs=[blk(0, False), blk(1, False), blk(1, True), blk(2, False), blk(2, True),
                  pl.BlockSpec((None, QB, D), lambda g, b: (g, jnp.minimum(b, last), 0)), stat, stat,
                  pl.BlockSpec((None, HEADS, QB, 2 * QB), lambda g, b: (g, 0, 0, 0))],
        out_specs=pl.BlockSpec((None, QB, 3 * D), lambda g, b: (g, jnp.maximum(b - 1, 0), 0)),
        out_shape=jax.ShapeDtypeStruct((NGROUPS, S, 3 * D), BF16),
        scratch_shapes=[pltpu.VMEM((QB, D), F32), pltpu.VMEM((QB, D), F32), pltpu.VMEM((QB, D), F32)],
        compiler_params=_cparams(2), name="attn_bwd")(qkv_f, qkv_f, qkv_f, qkv_f, qkv_f, do_f, lse_f, delta_f, bias)


def _fold(a, dil):
    if dil == 1:
        return a
    S, C = a.shape
    return a.reshape(S // dil, dil, C).transpose(1, 0, 2).reshape(S, C)


def _unfold(a, dil):
    if dil == 1:
        return a
    S, C = a.shape
    return a.reshape(dil, S // dil, C).transpose(1, 0, 2).reshape(S, C)


def _after(a, token):
    return lax.optimization_barrier((a, token))[0]


def _local_step(x, tgt, comm, attn_norm, ffn_norm, final_norm, pool_norm, pool_scale):
    S = x.shape[0]
    bias = _bias_table()
    g_attn = attn_norm.reshape(1, D)
    g_f0 = ffn_norm[0:1]
    g_f1 = ffn_norm[1:2]
    g_fin = final_norm.reshape(1, D)
    W = {}

    def ffn_fwd(xin, gain, l):
        h = _rms_fwd(xin, gain, f"rms_ffn{l}")
        gu, act = _ffn_up(h, W[f"gu{l}"], f"ffn_up{l}")
        xo = _mm(act, W[f"d{l}"], mode="nn", M=S, N=D, K=DFF, tm=512, tn=D, tk=DFF, out_dtype=F32,
                 res=xin, name=f"ffn_down{l}")
        return h, gu, act, xo

    def ffn_bwd(dxo, xin, gain, h, gu, act, l, rs_group):
        dgu = _ffn_down_bwd(dxo, W[f"d{l}"], gu, f"ffn_down_bwd{l}")
        gw_d = _mm(act, dxo, mode="tn", M=DFF, N=D, K=S, tm=HCH, tn=D, tk=1024, out_dtype=BF16, name=f"gw_d{l}")
        gw_gu = _mm(dgu, h, mode="tn", M=2 * DFF, N=D, K=S, tm=HCH, tn=D, tk=1024, out_dtype=BF16, name=f"gw_gu{l}")
        token = comm.send_grads(rs_group, {f"d{l}": gw_d, f"gu{l}": gw_gu})
        dh = _mm(_after(dgu, token), W[f"gu{l}"], mode="nn", M=S, N=D, K=2 * DFF, tm=1024, tn=D, tk=HCH,
                 out_dtype=F32, name=f"ffn_up_bwd{l}")
        dxin, dgain = _rms_bwd(dh, xin, gain, dxo, f"rms_ffn_bwd{l}")
        return dxin, dgain

    h0 = _rms_fwd(x, g_attn, "rms_attn")
    hf = jnp.stack([_fold(h0, d) for d in DILS])
    W.update(comm.weights(0, hf))
    qkv_f = _mm(hf, W["qkv"], mode="nt", nbat=NGROUPS, M=S, N=3 * D, K=D, tm=1024, tn=1024, tk=D, out_dtype=BF16,
                b_boff=(3, 0), name="qkv_proj")
    o_f, lse_f = _attn_fwd(qkv_f, bias)
    o_n = jnp.stack([_unfold(o_f[g], d) for g, d in enumerate(DILS)])
    lse_n = jnp.stack([_unfold(lse_f[g], d) for g, d in enumerate(DILS)])
    o, lse = _attn_merge(o_n, lse_n)
    W.update(comm.weights(1, o))
    x1 = _mm(o, W["wo"], mode="nn", M=S, N=D, K=D, tm=1024, tn=D, tk=D, out_dtype=F32, res=x, name="attn_out")
    h1, gu0, act0, x2 = ffn_fwd(x1, g_f0, 0)

    W.update(comm.weights(2, x2))
    h2 = _rms_fwd(x2, pool_norm, "rms_pool")
    u = _mm(h2, W["wpi"], mode="nn", M=S, N=D, K=D, tm=1024, tn=D, tk=D, out_dtype=F32, name="pool_in")
    yd = _trail(u, backward=False, name="trail_fwd")
    x3 = _pool_out(yd, W["pg"], pool_scale, x2)
    h3, gu1, act1, x4 = ffn_fwd(x3, g_f1, 1)

    dx4, d_fin, lossvec = _loss_head(x4, g_fin, tgt)
    loss = 0.5 * jnp.sum(lossvec) / D

    dx3, d_f1 = ffn_bwd(dx4, x3, g_f1, h3, gu1, act1, 1, 0)
    dyd, d_scale, gw_pg = _pool_out_bwd(dx3, yd, W["pg"], pool_scale)
    du = _trail(dyd, backward=True, name="trail_bwd")
    gw_pi = _mm(h2, du, mode="tn", M=D, N=D, K=S, tm=D, tn=D, tk=1024, out_dtype=BF16, name="gw_pi")
    token = comm.send_grads(1, {"pg": gw_pg, "wpi": gw_pi})
    dh2 = _mm(_after(du, token), W["wpi"], mode="nt", M=S, N=D, K=D, tm=1024, tn=D, tk=D, out_dtype=F32,
              name="pool_in_bwd")
    dx2, d_pool = _rms_bwd(dh2, x2, pool_norm, dx3, "rms_pool_bwd")
    dx1, d_f0 = ffn_bwd(dx2, x1, g_f0, h1, gu0, act0, 0, 2)

    gw_o = _mm(o, dx1, mode="tn", M=D, N=D, K=S, tm=D, tn=D, tk=1024, out_dtype=BF16, name="gw_o")
    do = _mm(dx1, W["wo"], mode="nt", M=S, N=D, K=D, tm=1024, tn=D, tk=D, out_dtype=BF16, name="attn_out_bwd")
    delta = _attn_delta(do, o)
    do_f = jnp.stack([_fold(do, d) for d in DILS])
    lse_ff = jnp.stack([_fold(lse, d) for d in DILS])
    delta_f = jnp.stack([_fold(delta, d) for d in DILS])
    dqkv_f = _attn_bwd(qkv_f, do_f, lse_ff, delta_f, bias)
    gw_qkv = _mm(dqkv_f, hf, mode="tn", nbat=NGROUPS, M=3 * D, N=D, K=S, tm=1024, tn=D, tk=1024, out_dtype=BF16,
                 out_2d_rows=NGROUPS * 3 * D, name="gw_qkv")
    token = comm.send_grads(3, {"wo": gw_o, "qkv": gw_qkv})
    dh0_f = _mm(_after(dqkv_f, token), W["qkv"], mode="nn", nbat=NGROUPS, M=S, N=D, K=3 * D, tm=1024, tn=D, tk=1024,
                out_dtype=F32, b_boff=(3, 0), name="qkv_proj_bwd")
    dh0 = _unfold(dh0_f[0], DILS[0]) + _unfold(dh0_f[1], DILS[1]) + _unfold(dh0_f[2], DILS[2])
    grad_x, d_attn = _rms_bwd(dh0, x, g_attn, dx1, "rms_attn_bwd")

    vec = jnp.concatenate([d_attn, d_f0, d_f1, d_fin, d_pool, d_scale, jnp.zeros((2, D), F32)], axis=0)
    return loss, grad_x, vec


def _mesh_pos():
    x, y, c = lax.axis_index("x"), lax.axis_index("y"), lax.axis_index("c")
    return x, y, c, 4 * x + 2 * y + c


def _peer(x, y, c, k):
    kx, ky, kc = (k >> 2) & 1, (k >> 1) & 1, k & 1
    px = 1 - x if kx else x
    py = 1 - y if ky else y
    pc = 1 - c if kc else c
    return (px, py, pc), 4 * px + 2 * py + pc


ANY = pl.BlockSpec(memory_space=pl.ANY)


HBM = pl.BlockSpec(memory_space=pltpu.HBM)
SEMS = pl.BlockSpec(memory_space=pltpu.SEMAPHORE)
EFFECT = pltpu.SideEffectType.DATAFLOW_SIDE_EFFECTING
NPEER = NDEV - 1

AG_GROUPS = (("qkv",), ("wo", "gu0", "d0"), ("wpi", "pg", "gu1", "d1"))
AG_ORDER = tuple(n for grp in AG_GROUPS for n in grp)
RS_GROUPS = (("d1", "gu1"), ("pg", "wpi"), ("d0", "gu0"), ("wo", "qkv"))


def _hbm(a):
    return pltpu.with_memory_space_constraint(a, pltpu.HBM)


def _remote(src, dst, send, recv, peer):
    return pltpu.make_async_remote_copy(src_ref=src, dst_ref=dst, send_sem=send, recv_sem=recv, device_id=peer,
                                        device_id_type=pl.DeviceIdType.MESH)


def _bcast_all(v, name):
    W = v.shape[1]

    def kern(v_ref, o_ref, send, recv, lsem):
        x, y, c, me = _mesh_pos()
        own = pltpu.make_async_copy(v_ref, o_ref.at[me], lsem)
        own.start()
        cps = [_remote(v_ref, o_ref.at[me], send.at[k - 1], recv.at[k - 1], _peer(x, y, c, k)[0])
               for k in range(1, NDEV)]
        for cp in cps:
            cp.start()
        for cp in cps:
            cp.wait_recv()
            cp.wait_send()
        own.wait()

    return pl.pallas_call(
        kern, in_specs=[ANY], out_specs=ANY, out_shape=jax.ShapeDtypeStruct((NDEV, 8, W), F32),
        scratch_shapes=[pltpu.SemaphoreType.DMA((NPEER,)), pltpu.SemaphoreType.DMA((NPEER,)),
                        pltpu.SemaphoreType.DMA(())],
        name=name)(v)


def _split_start(srcs, src_of, lands, copy_refs, name):
    ns, n = len(srcs), len(lands)

    def body(*refs):
        ins, land = refs[:ns], refs[ns:ns + n]
        send, recv = refs[ns + n], refs[ns + n + 1]
        token = refs[-1]
        x, y, c, me = _mesh_pos()
        for j in range(n):
            for k in range(1, NDEV):
                peer, pid = _peer(x, y, c, k)
                src, dst = copy_refs(j, (land[j] if src_of[j] is None else ins[src_of[j]]), land[j], me, pid)
                _remote(src, dst, send.at[j * NPEER + k - 1], recv.at[j * NPEER + k - 1], peer).start()
        token[...] = jnp.zeros_like(token)

    outs = pl.pallas_call(
        body, name=name,
        out_shape=(pltpu.SemaphoreType.DMA((n * NPEER,)), pltpu.SemaphoreType.DMA((n * NPEER,)))
        + tuple(pltpu.HBM(a.shape, a.dtype) for a in srcs) + tuple(pltpu.HBM(a.shape, a.dtype) for a in lands)
        + (jax.ShapeDtypeStruct((8, 128), F32),),
        in_specs=(HBM,) * (ns + n),
        out_specs=(SEMS, SEMS) + (HBM,) * (ns + n) + (pl.BlockSpec(memory_space=pltpu.VMEM),),
        input_output_aliases={i: 2 + i for i in range(ns + n)},
        compiler_params=pltpu.CompilerParams(has_side_effects=EFFECT),
    )(*[_hbm(a) for a in srcs], *[_hbm(a) for a in lands])
    return outs[0], outs[1], list(outs[2:2 + ns]), list(outs[2 + ns:2 + ns + n]), outs[-1]


def _split_wait(srcs, src_of, lands, send, recv, sem_rows, wait_refs, after, name):
    ns, n = len(srcs), len(lands)

    def body(*refs):
        ins, land = refs[:ns], refs[ns:ns + n]
        send_ref, recv_ref = refs[ns + n], refs[ns + n + 1]
        x, y, c, me = _mesh_pos()
        for j in range(n):
            for k in range(1, NDEV):
                peer, _ = _peer(x, y, c, k)
                src, dst = wait_refs(j, (land[j] if src_of[j] is None else ins[src_of[j]]), land[j])
                sem = sem_rows[j] * NPEER + k - 1
                cp = _remote(src, dst, send_ref.at[sem], recv_ref.at[sem], peer)
                cp.wait_send()
                cp.wait_recv()

    outs = pl.pallas_call(
        body, name=name,
        out_shape=tuple(pltpu.HBM(a.shape, a.dtype) for a in srcs) + tuple(pltpu.HBM(a.shape, a.dtype) for a in lands),
        in_specs=(HBM,) * (ns + n) + (SEMS, SEMS, ANY),
        out_specs=(HBM,) * (ns + n),
        input_output_aliases={i: i for i in range(ns + n)},
        compiler_params=pltpu.CompilerParams(has_side_effects=EFFECT),
    )(*srcs, *lands, send, recv, after)
    return list(outs[:ns]), list(outs[ns:])


class _Comm:
    def __init__(self, shards, me):
        names = AG_ORDER
        rows = [SEC_ROWS[n] for n in names]
        self.me = me
        lands = [lax.dynamic_update_slice(lax.empty((NDEV * r, D), BF16), shards[n], (_shard_pos(n, me), 0))
                 for n, r in zip(names, rows)]

        def copy_refs(j, src, land, me, pid):
            own = land.at[pl.ds(pl.multiple_of(_shard_pos(names[j], me), 16), rows[j])]
            return own, own

        self.ag_send, self.ag_recv, _, lands, _ = _split_start([], [None] * len(names), lands, copy_refs, "ag_start")
        self.ag_land = dict(zip(names, lands))
        self.rs = []

    def weights(self, group, after):
        names = AG_GROUPS[group]
        idx = [AG_ORDER.index(n) for n in names]
        rows = [SEC_ROWS[n] for n in names]

        def wait_refs(j, src, land):
            return land.at[pl.ds(0, rows[j])], land.at[pl.ds(0, rows[j])]

        _, lands = _split_wait([], [None] * len(names), [self.ag_land[n] for n in names], self.ag_send,
                               self.ag_recv, idx, wait_refs, after, f"ag_wait{group}")
        return dict(zip(names, lands))

    def send_grads(self, group, gws):
        names = RS_GROUPS[group]
        rows = [SEC_ROWS[n] for n in names]
        grads = [gws[n] for n in names]
        me = self.me
        lands = [lax.dynamic_update_slice(
            lax.empty((NDEV, r, D), BF16),
            lax.dynamic_slice(g, (_shard_pos(n, me), 0), (r, D))[None], (me, 0, 0))
            for n, r, g in zip(names, rows, grads)]

        def copy_refs(j, src, land, me, pid):
            return src.at[pl.ds(pl.multiple_of(_shard_pos(names[j], pid), 16), rows[j])], land.at[me]

        send, recv, srcs, lands, token = _split_start(grads, list(range(len(names))), lands, copy_refs,
                                                      f"rs_start{group}")
        self.rs.append((names, rows, send, recv, srcs, lands))
        return token

    def received(self, group, after):
        names, rows, send, recv, srcs, lands = self.rs[group]

        def wait_refs(j, src, land):
            return src.at[pl.ds(0, rows[j])], land.at[0]

        _, lands = _split_wait(srcs, list(range(len(names))), lands, send, recv, list(range(len(names))), wait_refs,
                               after, f"rs_wait{group}")
        return dict(zip(names, lands))


def _adamw(R, w, m, v, *, rows, tr, name, off=0):
    c1 = 1.0 / (1.0 - ADAM_B1 ** ADAM_STEP)
    c2 = 1.0 / (1.0 - ADAM_B2 ** ADAM_STEP)

    def kern(r_ref, w_ref, m_ref, v_ref, g_out, d_out, m_out, v_out):
        g = r_ref[0].astype(F32)
        for dev in range(1, NDEV):
            g = g + r_ref[dev].astype(F32)
        mn = ADAM_B1 * m_ref[...] + (1.0 - ADAM_B1) * g
        vn = ADAM_B2 * v_ref[...] + (1.0 - ADAM_B2) * (g * g)
        g_out[...] = g
        m_out[...] = mn
        v_out[...] = vn
        d_out[...] = -ADAM_LR * ((mn * c1) / (jnp.sqrt(vn * c2) + ADAM_EPS) + ADAM_WD * w_ref[...])

    tile = pl.BlockSpec((tr, D), lambda i: (i, 0))
    wtile = pl.BlockSpec((tr, D), lambda i: (i + off, 0))
    shp = jax.ShapeDtypeStruct((rows, D), F32)
    return pl.pallas_call(
        kern, grid=(rows // tr,),
        in_specs=[pl.BlockSpec((NDEV, tr, D), lambda i: (0, i, 0)), wtile, wtile, wtile],
        out_specs=[tile] * 4, out_shape=[shp] * 4, compiler_params=_cparams(1), name=name)(R, w, m, v)


ADAM_TILE = {"qkv": 384, "wo": 128, "wpi": 128, "gu0": 352, "gu1": 352, "d0": 352, "d1": 352, "pg": 32}


def _pack(w_qkv, w_attn_out, w_pool_in, w_pool_group, w_ffn_gate_up, w_ffn_down):
    pg = w_pool_group[0].transpose(1, 0, 2).reshape(SEC_ROWS["pg"], D)
    return jnp.concatenate([
        w_qkv[0].T, w_attn_out[0], w_pool_in[0], w_ffn_gate_up[0].T, w_ffn_gate_up[1].T,
        w_ffn_down[0], w_ffn_down[1], pg], axis=0)


def _unpack(p):
    def sec(n):
        return p[n]
    w_qkv = sec("qkv").T[None]
    w_attn_out = sec("wo")[None]
    w_pool_in = sec("wpi")[None]
    w_pool_group = sec("pg").reshape(SEC_ROWS["pg"], POOL_G, PGD).transpose(1, 0, 2)[None]
    w_gu = jnp.stack([sec("gu0").T, sec("gu1").T])
    w_d = jnp.stack([sec("d0"), sec("d1")])
    return w_qkv, w_attn_out, w_pool_in, w_pool_group, w_gu, w_d


def _vec_pack(attn_norm, ffn_norm, final_norm, pool_norm_sh, pool_scale_sh, me):
    def place(sh):
        return lax.dynamic_update_slice(jnp.zeros((1, D), F32), sh, (0, me * 128))
    return jnp.concatenate([attn_norm, ffn_norm, final_norm.reshape(1, D), place(pool_norm_sh),
                            place(pool_scale_sh), jnp.zeros((2, D), F32)], axis=0)


def _vec_unpack(p, me):
    def take(r):
        return lax.dynamic_slice(p[r:r + 1], (0, me * 128), (1, 128))
    return p[0:1], p[1:3], p[3], take(4), take(5)


def kernel(x, attn_norm, w_qkv, w_attn_out, pool_norm, w_pool_in, w_pool_group, pool_scale, ffn_norm, w_ffn_gate_up, w_ffn_down, final_norm, loss_target, m_attn_norm, m_w_qkv, m_w_attn_out, m_pool_norm, m_w_pool_in, m_w_pool_group, m_pool_scale, m_ffn_norm, m_w_ffn_gate_up, m_w_ffn_down, m_final_norm, v_attn_norm, v_w_qkv, v_w_attn_out, v_pool_norm, v_w_pool_in, v_w_pool_group, v_pool_scale, v_ffn_norm, v_w_ffn_gate_up, v_w_ffn_down, v_final_norm):
    me = 4 * lax.axis_index("x") + 2 * lax.axis_index("y") + lax.axis_index("c")

    pw = _pack(w_qkv, w_attn_out, w_pool_in, w_pool_group, w_ffn_gate_up, w_ffn_down)
    pm = _pack(m_w_qkv, m_w_attn_out, m_w_pool_in, m_w_pool_group, m_w_ffn_gate_up, m_w_ffn_down)
    pv = _pack(v_w_qkv, v_w_attn_out, v_w_pool_in, v_w_pool_group, v_w_ffn_gate_up, v_w_ffn_down)
    vsh = jnp.concatenate([pool_norm, pool_scale, jnp.zeros((6, 128), F32)], axis=0)

    comm = _Comm({n: pw[LOC_OFF[n]:LOC_OFF[n] + r].astype(BF16) for n, r in SECTIONS}, me)
    vg = _bcast_all(vsh, "gather_pool_vectors")
    pool_norm_full = vg[:, 0, :].reshape(1, D)
    pool_scale_full = vg[:, 1, :].reshape(1, D)

    loss, grad_x, vec = _local_step(x[0], loss_target[0], comm, attn_norm, ffn_norm, final_norm,
                                    pool_norm_full, pool_scale_full)
    loss = lax.psum(loss, MESH_AXES)

    VR = _bcast_all(vec, "exchange_vector_grads")
    vw = _vec_pack(attn_norm, ffn_norm, final_norm, pool_norm, pool_scale, me)
    vm = _vec_pack(m_attn_norm, m_ffn_norm, m_final_norm, m_pool_norm, m_pool_scale, me)
    vv = _vec_pack(v_attn_norm, v_ffn_norm, v_final_norm, v_pool_norm, v_pool_scale, me)
    vec_out = _adamw(VR, vw, vm, vv, rows=8, tr=8, name="adamw_vec")

    sec_out = [{}, {}, {}, {}]
    after = vec_out[0]
    for group in range(len(RS_GROUPS)):
        for n, R in comm.received(group, after).items():
            tr = ADAM_TILE[n]
            res = _adamw(R, pw, pm, pv, rows=SEC_ROWS[n], tr=tr, off=LOC_OFF[n] // tr, name=f"adamw_{n}")
            for kind in range(4):
                sec_out[kind][n] = res[kind]
            after = res[0]

    outs = []
    for kind in range(4):
        q, o, pi, pg, gu, dn = _unpack(sec_out[kind])
        an, fn, fin, pn, ps = _vec_unpack(vec_out[kind], me)
        outs.append((an, q, o, pn, pi, pg, ps, fn, gu, dn, fin))
    return (loss, grad_x[None]) + outs[0] + outs[1] + outs[2] + outs[3]
```

```python
import jax
import jax.numpy as jnp
from jax import lax
from jax.experimental import pallas as pl
from jax.experimental.pallas import tpu as pltpu

F32 = jnp.float32
BF16 = jnp.bfloat16

D = 1024
NDEV = 8
HEADS = 8
HD = 128
QB = 128
NGROUPS = 3
DILS = (1, 4, 16)
DFF = 2816
HCH = 1408
POOL_G = 4
PGD = 256
RMS_EPS = 1e-6
NEG = -1e30

ADAM_LR = 0.001
ADAM_B1 = 0.9
ADAM_B2 = 0.999
ADAM_EPS = 1e-08
ADAM_WD = 0.01
ADAM_STEP = 10

VMEM_LIMIT = 52 * 1024 * 1024
MESH_AXES = ("x", "y", "c")

SECTIONS = (("qkv", 1152), ("wo", 128), ("wpi", 128), ("gu0", 704), ("gu1", 704),
            ("d0", 352), ("d1", 352), ("pg", 32))
LOC_OFF = {}
GLB_OFF = {}
_o = 0
for _n, _r in SECTIONS:
    LOC_OFF[_n] = _o
    GLB_OFF[_n] = _o * NDEV
    _o += _r
PACK_ROWS = _o
GLB_ROWS = PACK_ROWS * NDEV
SEC_ROWS = dict(SECTIONS)


def _cparams(n_grid):
    return pltpu.CompilerParams(dimension_semantics=("arbitrary",) * n_grid, vmem_limit_bytes=VMEM_LIMIT)


def _shard_pos(name, dev):
    n = SEC_ROWS[name]
    if name in ("gu0", "gu1"):
        return ((dev % 4) // 2) * (2 * HCH) + (dev // 4) * HCH + (dev % 2) * n
    return dev * n


def _mm(a, b, *, mode, M, N, K, tm, tn, tk, out_dtype, name, nbat=1, a_off=(0, 0), b_off=(0, 0),
        a_boff=(0, 0), b_boff=(0, 0), res=None, out_2d_rows=None, deps=()):
    nm, nn, nk = M // tm, N // tn, K // tk
    assert nm * tm == M and nn * tn == N and nk * tk == K
    if mode == "nn":
        a_bs, b_bs = (tm, tk), (tk, tn)
        a_ix = lambda i, j, k: (i, k)
        b_ix = lambda i, j, k: (k, j)
        dims = (((1,), (0,)), ((), ()))
    elif mode == "nt":
        a_bs, b_bs = (tm, tk), (tn, tk)
        a_ix = lambda i, j, k: (i, k)
        b_ix = lambda i, j, k: (j, k)
        dims = (((1,), (1,)), ((), ()))
    else:
        a_bs, b_bs = (tk, tm), (tk, tn)
        a_ix = lambda i, j, k: (k, i)
        b_ix = lambda i, j, k: (k, j)
        dims = (((0,), (0,)), ((), ()))

    def spec(arr, bs, ix, off, boff):
        if arr.ndim == 3:
            return pl.BlockSpec((None,) + bs, lambda bb, i, j, k: (bb,) + ix(i, j, k))

        def im(bb, i, j, k):
            r, c = ix(i, j, k)
            return (r + off[0] + bb * boff[0], c + off[1] + bb * boff[1])
        return pl.BlockSpec(bs, im)

    in_specs = [spec(a, a_bs, a_ix, a_off, a_boff), spec(b, b_bs, b_ix, b_off, b_boff)]
    args = [a, b]
    if res is not None:
        in_specs.append(pl.BlockSpec((tm, tn), lambda bb, i, j, k: (i, j)))
        args.append(res)
    if nbat > 1 and out_2d_rows is None:
        out_shape = jax.ShapeDtypeStruct((nbat, M, N), out_dtype)
        out_spec = pl.BlockSpec((None, tm, tn), lambda bb, i, j, k: (bb, i, j))
    else:
        rows = M if out_2d_rows is None else out_2d_rows
        out_shape = jax.ShapeDtypeStruct((rows, N), out_dtype)
        out_spec = pl.BlockSpec((tm, tn), lambda bb, i, j, k: (i + bb * nm, j))
    has_res = res is not None
    for dep in deps:
        in_specs.append(pl.BlockSpec(memory_space=pl.ANY))
        args.append(dep)
    o_pos = 2 + int(has_res) + len(deps)

    def kern(*refs):
        a_ref, b_ref = refs[0], refs[1]
        res_ref = refs[2] if has_res else None
        o_ref = refs[o_pos]
        av = a_ref[...]
        bv = b_ref[...]
        if av.dtype != BF16:
            av = av.astype(BF16)
        if bv.dtype != BF16:
            bv = bv.astype(BF16)
        part = lax.dot_general(av, bv, dims, preferred_element_type=F32)

        def write(val):
            if has_res:
                val = val + res_ref[...]
            o_ref[...] = val.astype(out_dtype)

        if nk == 1:
            write(part)
        else:
            acc_ref = refs[-1]
            k = pl.program_id(3)

            @pl.when(k == 0)
            def _():
                acc_ref[...] = part

            @pl.when(k > 0)
            def _():
                acc_ref[...] += part

            @pl.when(k == nk - 1)
            def _():
                write(acc_ref[...])

    scratch = [pltpu.VMEM((tm, tn), F32)] if nk > 1 else []
    return pl.pallas_call(
        kern, grid=(nbat, nm, nn, nk), in_specs=in_specs, out_specs=out_spec, out_shape=out_shape,
        scratch_shapes=scratch, compiler_params=_cparams(4), name=name)(*args)


def _rms_fwd(x, g, name, deps=()):
    S = x.shape[0]
    tr = 512

    def kern(x_ref, g_ref, *rest):
        h_ref = rest[-1]
        xv = x_ref[...]
        r = lax.rsqrt(jnp.mean(xv * xv, axis=-1, keepdims=True) + RMS_EPS)
        h_ref[...] = (xv * r * g_ref[...]).astype(BF16)

    return pl.pallas_call(
        kern, grid=(S // tr,),
        in_specs=[pl.BlockSpec((tr, D), lambda i: (i, 0)), pl.BlockSpec((1, D), lambda i: (0, 0))]
        + [pl.BlockSpec(memory_space=pl.ANY)] * len(deps),
        out_specs=pl.BlockSpec((tr, D), lambda i: (i, 0)),
        out_shape=jax.ShapeDtypeStruct((S, D), BF16), compiler_params=_cparams(1), name=name)(x, g, *deps)


def _rms_bwd(dh, x, g, dres, name):
    S = x.shape[0]
    tr = 512

    def kern(dh_ref, x_ref, g_ref, dres_ref, dx_ref, dg_ref):
        i = pl.program_id(0)
        xv = x_ref[...]
        dhv = dh_ref[...].astype(F32)
        r = lax.rsqrt(jnp.mean(xv * xv, axis=-1, keepdims=True) + RMS_EPS)
        xhat = xv * r
        gy = dhv * g_ref[...]
        dx_ref[...] = dres_ref[...] + r * (gy - xhat * jnp.mean(gy * xhat, axis=-1, keepdims=True))
        part = jnp.sum(dhv * xhat, axis=0, keepdims=True)

        @pl.when(i == 0)
        def _():
            dg_ref[...] = part

        @pl.when(i > 0)
        def _():
            dg_ref[...] += part

    row = pl.BlockSpec((tr, D), lambda i: (i, 0))
    vec = pl.BlockSpec((1, D), lambda i: (0, 0))
    return pl.pallas_call(
        kern, grid=(S // tr,), in_specs=[row, row, vec, row], out_specs=[row, vec],
        out_shape=[jax.ShapeDtypeStruct((S, D), F32), jax.ShapeDtypeStruct((1, D), F32)],
        compiler_params=_cparams(1), name=name)(dh, x, g, dres)


def _loss_head(x, g, tgt):
    S = x.shape[0]
    tr = 512

    def kern(x_ref, g_ref, t_ref, dx_ref, dg_ref, ls_ref):
        i = pl.program_id(0)
        xv = x_ref[...]
        gv = g_ref[...]
        r = lax.rsqrt(jnp.mean(xv * xv, axis=-1, keepdims=True) + RMS_EPS)
        xhat = xv * r
        e = xhat * gv - t_ref[...]
        dy = e * (1.0 / D)
        gy = dy * gv
        dx_ref[...] = r * (gy - xhat * jnp.mean(gy * xhat, axis=-1, keepdims=True))
        dgp = jnp.sum(dy * xhat, axis=0, keepdims=True)
        lsp = jnp.sum(e * e, axis=0, keepdims=True)

        @pl.when(i == 0)
        def _():
            dg_ref[...] = dgp
            ls_ref[...] = lsp

        @pl.when(i > 0)
        def _():
            dg_ref[...] += dgp
            ls_ref[...] += lsp

    row = pl.BlockSpec((tr, D), lambda i: (i, 0))
    vec = pl.BlockSpec((1, D), lambda i: (0, 0))
    return pl.pallas_call(
        kern, grid=(S // tr,), in_specs=[row, vec, row], out_specs=[row, vec, vec],
        out_shape=[jax.ShapeDtypeStruct((S, D), F32), jax.ShapeDtypeStruct((1, D), F32),
                   jax.ShapeDtypeStruct((1, D), F32)],
        compiler_params=_cparams(1), name="loss_head")(x, g, tgt)


def _ffn_up(h, G, name):
    S = h.shape[0]
    tm = 512
    nj = DFF // HCH

    def kern(h_ref, w_ref, gu_ref, act_ref):
        gu = lax.dot_general(h_ref[...], w_ref[...], (((1,), (1,)), ((), ())), preferred_element_type=F32)
        gu_ref[...] = gu.astype(BF16)
        gate = gu[:, :HCH]
        up = gu[:, HCH:]
        act_ref[...] = (gate * jax.nn.sigmoid(gate) * up).astype(BF16)

    return pl.pallas_call(
        kern, grid=(nj, S // tm),
        in_specs=[pl.BlockSpec((tm, D), lambda j, i: (i, 0)),
                  pl.BlockSpec((2 * HCH, D), lambda j, i: (j, 0))],
        out_specs=[pl.BlockSpec((tm, 2 * HCH), lambda j, i: (i, j)),
                   pl.BlockSpec((tm, HCH), lambda j, i: (i, j))],
        out_shape=[jax.ShapeDtypeStruct((S, 2 * DFF), BF16), jax.ShapeDtypeStruct((S, DFF), BF16)],
        compiler_params=_cparams(2), name=name)(h, G)


def _ffn_down_bwd(dx, G, gu, name):
    S = dx.shape[0]
    tm = 512
    nj = DFF // HCH

    def kern(dx_ref, w_ref, gu_ref, o_ref):
        dact = lax.dot_general(dx_ref[...].astype(BF16), w_ref[...], (((1,), (1,)), ((), ())),
                               preferred_element_type=F32)
        gate = gu_ref[:, :HCH].astype(F32)
        up = gu_ref[:, HCH:].astype(F32)
        sig = jax.nn.sigmoid(gate)
        silu = gate * sig
        o_ref[:, :HCH] = (dact * up * (sig * (1.0 + gate * (1.0 - sig)))).astype(BF16)
        o_ref[:, HCH:] = (dact * silu).astype(BF16)

    return pl.pallas_call(
        kern, grid=(nj, S // tm),
        in_specs=[pl.BlockSpec((tm, D), lambda j, i: (i, 0)),
                  pl.BlockSpec((HCH, D), lambda j, i: (j, 0)),
                  pl.BlockSpec((tm, 2 * HCH), lambda j, i: (i, j))],
        out_specs=pl.BlockSpec((tm, 2 * HCH), lambda j, i: (i, j)),
        out_shape=jax.ShapeDtypeStruct((S, 2 * DFF), BF16),
        compiler_params=_cparams(2), name=name)(dx, G, gu)


def _trail(u, *, backward, name):
    S = u.shape[0]

    def kern(u_ref, o_ref):
        g = pl.program_id(0)
        uv = u_ref[...].astype(F32)
        row = lax.broadcasted_iota(jnp.int32, uv.shape, 0)
        win = jnp.left_shift(jnp.int32(2), g)
        cnt = jnp.minimum(row + 1, win).astype(F32)
        s = uv / cnt if backward else uv
        levels = []
        for k in (1, 2, 4, 8):
            if backward:
                sh = jnp.where(row < S - k, pltpu.roll(s, S - k, 0), 0.0)
            else:
                sh = jnp.where(row >= k, pltpu.roll(s, k, 0), 0.0)
            s = s + sh
            levels.append(s)
        sel = jnp.where(g == 0, levels[0], jnp.where(g == 1, levels[1], jnp.where(g == 2, levels[2], levels[3])))
        if backward:
            o_ref[...] = (sel - uv).astype(BF16)
        else:
            o_ref[...] = (sel / cnt - uv).astype(BF16)

    blk = pl.BlockSpec((S, PGD), lambda g: (0, g))
    return pl.pallas_call(
        kern, grid=(POOL_G,), in_specs=[blk], out_specs=blk,
        out_shape=jax.ShapeDtypeStruct((S, D), BF16), compiler_params=_cparams(1), name=name)(u)


def _pool_out(yd, G, scale, xres):
    S = yd.shape[0]
    tm = 1024

    def kern(y_ref, w_ref, s_ref, x_ref, o_ref):
        z = jnp.dot(y_ref[...], w_ref[...], preferred_element_type=F32)
        o_ref[...] = x_ref[...] + z * s_ref[...]

    tile = pl.BlockSpec((tm, PGD), lambda i, g: (i, g))
    return pl.pallas_call(
        kern, grid=(S // tm, POOL_G),
        in_specs=[tile, pl.BlockSpec((PGD, PGD), lambda i, g: (0, g)),
                  pl.BlockSpec((1, PGD), lambda i, g: (0, g)), tile],
        out_specs=tile, out_shape=jax.ShapeDtypeStruct((S, D), F32),
        compiler_params=_cparams(2), name="pool_out")(yd, G, scale, xres)


def _pool_out_bwd(dz, yd, G, scale):
    S = yd.shape[0]
    tm = 1024
    ni = S // tm

    def kern(dz_ref, y_ref, w_ref, s_ref, dy_ref, ds_ref, dw_ref, acc_ref):
        i = pl.program_id(1)
        dzv = dz_ref[...]
        yv = y_ref[...]
        wv = w_ref[...]
        zraw = jnp.dot(yv, wv, preferred_element_type=F32)
        dsp = jnp.sum(dzv * zraw, axis=0, keepdims=True)
        dzr = (dzv * s_ref[...]).astype(BF16)
        dy_ref[...] = lax.dot_general(dzr, wv, (((1,), (1,)), ((), ())), preferred_element_type=F32)
        dwp = lax.dot_general(yv, dzr, (((0,), (0,)), ((), ())), preferred_element_type=F32)

        @pl.when(i == 0)
        def _():
            ds_ref[...] = dsp
            acc_ref[...] = dwp

        @pl.when(i > 0)
        def _():
            ds_ref[...] += dsp
            acc_ref[...] += dwp

        @pl.when(i == ni - 1)
        def _():
            dw_ref[...] = acc_ref[...].astype(BF16)

    tile = pl.BlockSpec((tm, PGD), lambda g, i: (i, g))
    return pl.pallas_call(
        kern, grid=(POOL_G, ni),
        in_specs=[tile, tile, pl.BlockSpec((PGD, PGD), lambda g, i: (0, g)),
                  pl.BlockSpec((1, PGD), lambda g, i: (0, g))],
        out_specs=[tile, pl.BlockSpec((1, PGD), lambda g, i: (0, g)),
                   pl.BlockSpec((PGD, PGD), lambda g, i: (0, g))],
        out_shape=[jax.ShapeDtypeStruct((S, D), F32), jax.ShapeDtypeStruct((1, D), F32),
                   jax.ShapeDtypeStruct((PGD, D), BF16)],
        scratch_shapes=[pltpu.VMEM((PGD, PGD), F32)],
        compiler_params=_cparams(2), name="pool_out_bwd")(dz, yd, G, scale)


def _bias_table():
    qi = jnp.arange(QB)[:, None]
    ki = jnp.arange(2 * QB)[None, :]
    delta = QB + qi - ki
    inband = (delta >= 0) & (delta <= QB)
    n = NGROUPS * HEADS
    slopes = jnp.exp2(-8.0 * jnp.arange(1, n + 1, dtype=F32) / n).reshape(NGROUPS, HEADS)
    dil = jnp.asarray(DILS, F32)
    bias = -slopes[:, :, None, None] * (delta.astype(F32)[None, None] * dil[:, None, None, None])
    return jnp.where(inband[None, None], bias, NEG)


def _group_scalars(g, b, nblk):
    nb = jnp.right_shift(jnp.int32(nblk), 2 * g)
    has_prev = jnp.bitwise_and(b, nb - 1) != 0
    return has_prev


def _attn_fwd(qkv_f, bias):
    S = qkv_f.shape[1]
    nblk = S // QB
    scale = HD ** -0.5

    def kern(q_ref, kc_ref, kp_ref, vc_ref, vp_ref, b_ref, o_ref, l_ref):
        g = pl.program_id(0)
        b = pl.program_id(1)
        has_prev = _group_scalars(g, b, nblk)
        col = lax.broadcasted_iota(jnp.int32, (QB, 2 * QB), 1)
        dead = jnp.logical_and(col < QB, jnp.logical_not(has_prev))
        lane = lax.broadcasted_iota(jnp.int32, (QB, HD), 1)
        lse_all = jnp.zeros((QB, HD), F32)
        for h in range(HEADS):
            sl = slice(h * HD, (h + 1) * HD)
            kk = jnp.concatenate([kp_ref[:, sl], kc_ref[:, sl]], axis=0)
            vv = jnp.concatenate([vp_ref[:, sl], vc_ref[:, sl]], axis=0)
            s = lax.dot_general(q_ref[:, sl], kk, (((1,), (1,)), ((), ())), preferred_element_type=F32)
            s = s * scale + b_ref[h]
            s = jnp.where(dead, NEG, s)
            m = jnp.max(s, axis=-1, keepdims=True)
            p = jnp.exp(s - m)
            den = jnp.sum(p, axis=-1, keepdims=True)
            o = jnp.dot(p.astype(BF16), vv, preferred_element_type=F32) / den
            o_ref[:, sl] = o.astype(BF16)
            lse_all = jnp.where(lane == h, m + jnp.log(den), lse_all)
        l_ref[...] = lse_all

    def blk(colblk, prev):
        if prev:
            return pl.BlockSpec((None, QB, D), lambda g, b: (g, jnp.maximum(b - 1, 0), colblk))
        return pl.BlockSpec((None, QB, D), lambda g, b: (g, b, colblk))

    return pl.pallas_call(
        kern, grid=(NGROUPS, nblk),
        in_specs=[blk(0, False), blk(1, False), blk(1, True), blk(2, False), blk(2, True),
                  pl.BlockSpec((None, HEADS, QB, 2 * QB), lambda g, b: (g, 0, 0, 0))],
        out_specs=[pl.BlockSpec((None, QB, D), lambda g, b: (g, b, 0)),
                   pl.BlockSpec((None, QB, HD), lambda g, b: (g, b, 0))],
        out_shape=[jax.ShapeDtypeStruct((NGROUPS, S, D), BF16), jax.ShapeDtypeStruct((NGROUPS, S, HD), F32)],
        compiler_params=_cparams(2), name="attn_fwd")(qkv_f, qkv_f, qkv_f, qkv_f, qkv_f, bias)


def _attn_merge(o_n, lse_n):
    S = o_n.shape[1]
    tm = 512

    def kern(o_ref, l_ref, om_ref, lm_ref):
        l0, l1, l2 = l_ref[0], l_ref[1], l_ref[2]
        m = jnp.maximum(jnp.maximum(l0, l1), l2)
        e0, e1, e2 = jnp.exp(l0 - m), jnp.exp(l1 - m), jnp.exp(l2 - m)
        tot = e0 + e1 + e2
        lm_ref[...] = m + jnp.log(tot)
        w0, w1, w2 = e0 / tot, e1 / tot, e2 / tot
        for h in range(HEADS):
            sl = slice(h * HD, (h + 1) * HD)
            acc = (w0[:, h:h + 1] * o_ref[0, :, sl].astype(F32) + w1[:, h:h + 1] * o_ref[1, :, sl].astype(F32)
                   + w2[:, h:h + 1] * o_ref[2, :, sl].astype(F32))
            om_ref[:, sl] = acc.astype(BF16)

    return pl.pallas_call(
        kern, grid=(S // tm,),
        in_specs=[pl.BlockSpec((NGROUPS, tm, D), lambda i: (0, i, 0)),
                  pl.BlockSpec((NGROUPS, tm, HD), lambda i: (0, i, 0))],
        out_specs=[pl.BlockSpec((tm, D), lambda i: (i, 0)), pl.BlockSpec((tm, HD), lambda i: (i, 0))],
        out_shape=[jax.ShapeDtypeStruct((S, D), BF16), jax.ShapeDtypeStruct((S, HD), F32)],
        compiler_params=_cparams(1), name="attn_merge")(o_n, lse_n)


def _attn_delta(do, o):
    S = o.shape[0]
    tm = 512

    def kern(do_ref, o_ref, d_ref):
        lane = lax.broadcasted_iota(jnp.int32, (tm, HD), 1)
        acc = jnp.zeros((tm, HD), F32)
        for h in range(HEADS):
            sl = slice(h * HD, (h + 1) * HD)
            prod = do_ref[:, sl].astype(F32) * o_ref[:, sl].astype(F32)
            acc = jnp.where(lane == h, jnp.sum(prod, axis=-1, keepdims=True), acc)
        d_ref[...] = acc

    row = pl.BlockSpec((tm, D), lambda i: (i, 0))
    return pl.pallas_call(
        kern, grid=(S // tm,), in_specs=[row, row], out_specs=pl.BlockSpec((tm, HD), lambda i: (i, 0)),
        out_shape=jax.ShapeDtypeStruct((S, HD), F32), compiler_params=_cparams(1), name="attn_delta")(do, o)


def _attn_bwd(qkv_f, do_f, lse_f, delta_f, bias):
    S = qkv_f.shape[1]
    nblk = S // QB
    scale = HD ** -0.5

    def kern(q_ref, kc_ref, kp_ref, vc_ref, vp_ref, do_ref, l_ref, d_ref, b_ref, out_ref, dq_c, dk_c, dv_c):
        g = pl.program_id(0)
        b = pl.program_id(1)

        @pl.when(b == 0)
        def _():
            dq_c[...] = jnp.zeros_like(dq_c)
            dk_c[...] = jnp.zeros_like(dk_c)
            dv_c[...] = jnp.zeros_like(dv_c)

        @pl.when(b == nblk)
        def _():
            out_ref[:, 0:D] = dq_c[...].astype(BF16)
            out_ref[:, D:2 * D] = dk_c[...].astype(BF16)
            out_ref[:, 2 * D:3 * D] = dv_c[...].astype(BF16)

        @pl.when(b < nblk)
        def _():
            has_prev = _group_scalars(g, b, nblk)
            col = lax.broadcasted_iota(jnp.int32, (QB, 2 * QB), 1)
            dead = jnp.logical_and(col < QB, jnp.logical_not(has_prev))
            out_ref[:, 0:D] = dq_c[...].astype(BF16)
            lv = l_ref[...]
            dv_ = d_ref[...]
            for h in range(HEADS):
                sl = slice(h * HD, (h + 1) * HD)
                qh = q_ref[:, sl]
                doh = do_ref[:, sl]
                kk = jnp.concatenate([kp_ref[:, sl], kc_ref[:, sl]], axis=0)
                vv = jnp.concatenate([vp_ref[:, sl], vc_ref[:, sl]], axis=0)
                s = lax.dot_general(qh, kk, (((1,), (1,)), ((), ())), preferred_element_type=F32)
                s = s * scale + b_ref[h]
                s = jnp.where(dead, NEG, s)
                p = jnp.exp(s - lv[:, h:h + 1])
                dp = lax.dot_general(doh, vv, (((1,), (1,)), ((), ())), preferred_element_type=F32)
                ds = (p * (dp - dv_[:, h:h + 1]) * scale).astype(BF16)
                pb = p.astype(BF16)
                dq_c[:, sl] = jnp.dot(ds, kk, preferred_element_type=F32)
                dkk = lax.dot_general(ds, qh, (((0,), (0,)), ((), ())), preferred_element_type=F32)
                dvv = lax.dot_general(pb, doh, (((0,), (0,)), ((), ())), preferred_element_type=F32)
                out_ref[:, D + h * HD:D + (h + 1) * HD] = (dk_c[:, sl] + dkk[:QB]).astype(BF16)
                out_ref[:, 2 * D + h * HD:2 * D + (h + 1) * HD] = (dv_c[:, sl] + dvv[:QB]).astype(BF16)
                dk_c[:, sl] = dkk[QB:]
                dv_c[:, sl] = dvv[QB:]

    last = nblk - 1

    def blk(colblk, prev):
        if prev:
            return pl.BlockSpec((None, QB, D), lambda g, b: (g, jnp.maximum(jnp.minimum(b, last) - 1, 0), colblk))
        return pl.BlockSpec((None, QB, D), lambda g, b: (g, jnp.minimum(b, last), colblk))

    stat = pl.BlockSpec((None, QB, HD), lambda g, b: (g, jnp.minimum(b, last), 0))
    return pl.pallas_call(
        kern, grid=(NGROUPS, nblk + 1),
        in_specs=[blk(0, False), blk(1, False), blk(1, True), blk(2, False), blk(2, True),
                  pl.BlockSpec((None, QB, D), lambda g, b: (g, jnp.minimum(b, last), 0)), stat, stat,
                  pl.BlockSpec((None, HEADS, QB, 2 * QB), lambda g, b: (g, 0, 0, 0))],
        out_specs=pl.BlockSpec((None, QB, 3 * D), lambda g, b: (g, jnp.maximum(b - 1, 0), 0)),
        out_shape=jax.ShapeDtypeStruct((NGROUPS, S, 3 * D), BF16),
        scratch_shapes=[pltpu.VMEM((QB, D), F32), pltpu.VMEM((QB, D), F32), pltpu.VMEM((QB, D), F32)],
        compiler_params=_cparams(2), name="attn_bwd")(qkv_f, qkv_f, qkv_f, qkv_f, qkv_f, do_f, lse_f, delta_f, bias)


def _fold(a, dil):
    if dil == 1:
        return a
    S, C = a.shape
    return a.reshape(S // dil, dil, C).transpose(1, 0, 2).reshape(S, C)


def _unfold(a, dil):
    if dil == 1:
        return a
    S, C = a.shape
    return a.reshape(dil, S // dil, C).transpose(1, 0, 2).reshape(S, C)


def _local_step(x, tgt, comm, attn_norm, ffn_norm, final_norm, pool_norm, pool_scale):
    S = x.shape[0]
    bias = _bias_table()
    g_attn = attn_norm.reshape(1, D)
    g_f0 = ffn_norm[0:1]
    g_f1 = ffn_norm[1:2]
    g_fin = final_norm.reshape(1, D)
    W = {}

    def ffn_fwd(xin, gain, l):
        h = _rms_fwd(xin, gain, f"rms_ffn{l}")
        gu, act = _ffn_up(h, W[f"gu{l}"], f"ffn_up{l}")
        xo = _mm(act, W[f"d{l}"], mode="nn", M=S, N=D, K=DFF, tm=512, tn=D, tk=DFF, out_dtype=F32,
                 res=xin, name=f"ffn_down{l}")
        return h, gu, act, xo

    def ffn_bwd(dxo, xin, gain, h, gu, act, l, rs_group):
        dgu = _ffn_down_bwd(dxo, W[f"d{l}"], gu, f"ffn_down_bwd{l}")
        gw_d = _mm(act, dxo, mode="tn", M=DFF, N=D, K=S, tm=HCH, tn=D, tk=1024, out_dtype=BF16, name=f"gw_d{l}")
        gw_gu = _mm(dgu, h, mode="tn", M=2 * DFF, N=D, K=S, tm=HCH, tn=D, tk=1024, out_dtype=BF16, name=f"gw_gu{l}")
        token = comm.send_grads(rs_group, {f"d{l}": gw_d, f"gu{l}": gw_gu})
        dh = _mm(dgu, W[f"gu{l}"], mode="nn", M=S, N=D, K=2 * DFF, tm=1024, tn=D, tk=HCH,
                 out_dtype=F32, deps=(token,), name=f"ffn_up_bwd{l}")
        dxin, dgain = _rms_bwd(dh, xin, gain, dxo, f"rms_ffn_bwd{l}")
        return dxin, dgain

    h0 = _rms_fwd(x, g_attn, "rms_attn", deps=(comm.ag_token,))
    hf = jnp.stack([_fold(h0, d) for d in DILS])
    W.update(comm.weights(0, hf))
    qkv_f = _mm(hf, W["qkv"], mode="nt", nbat=NGROUPS, M=S, N=3 * D, K=D, tm=1024, tn=1024, tk=D, out_dtype=BF16,
                b_boff=(3, 0), name="qkv_proj")
    o_f, lse_f = _attn_fwd(qkv_f, bias)
    o_n = jnp.stack([_unfold(o_f[g], d) for g, d in enumerate(DILS)])
    lse_n = jnp.stack([_unfold(lse_f[g], d) for g, d in enumerate(DILS)])
    o, lse = _attn_merge(o_n, lse_n)
    W.update(comm.weights(1, o))
    x1 = _mm(o, W["wo"], mode="nn", M=S, N=D, K=D, tm=1024, tn=D, tk=D, out_dtype=F32, res=x, name="attn_out")
    h1, gu0, act0, x2 = ffn_fwd(x1, g_f0, 0)

    W.update(comm.weights(2, x2))
    h2 = _rms_fwd(x2, pool_norm, "rms_pool")
    u = _mm(h2, W["wpi"], mode="nn", M=S, N=D, K=D, tm=1024, tn=D, tk=D, out_dtype=F32, name="pool_in")
    yd = _trail(u, backward=False, name="trail_fwd")
    x3 = _pool_out(yd, W["pg"], pool_scale, x2)
    h3, gu1, act1, x4 = ffn_fwd(x3, g_f1, 1)

    dx4, d_fin, lossvec = _loss_head(x4, g_fin, tgt)
    loss = 0.5 * jnp.sum(lossvec) / D

    dx3, d_f1 = ffn_bwd(dx4, x3, g_f1, h3, gu1, act1, 1, 0)
    dyd, d_scale, gw_pg = _pool_out_bwd(dx3, yd, W["pg"], pool_scale)
    du = _trail(dyd, backward=True, name="trail_bwd")
    gw_pi = _mm(h2, du, mode="tn", M=D, N=D, K=S, tm=D, tn=D, tk=1024, out_dtype=BF16, name="gw_pi")
    token = comm.send_grads(1, {"pg": gw_pg, "wpi": gw_pi})
    dh2 = _mm(du, W["wpi"], mode="nt", M=S, N=D, K=D, tm=1024, tn=D, tk=D, out_dtype=F32,
              deps=(token,), name="pool_in_bwd")
    dx2, d_pool = _rms_bwd(dh2, x2, pool_norm, dx3, "rms_pool_bwd")
    dx1, d_f0 = ffn_bwd(dx2, x1, g_f0, h1, gu0, act0, 0, 2)

    gw_o = _mm(o, dx1, mode="tn", M=D, N=D, K=S, tm=D, tn=D, tk=1024, out_dtype=BF16, name="gw_o")
    do = _mm(dx1, W["wo"], mode="nt", M=S, N=D, K=D, tm=1024, tn=D, tk=D, out_dtype=BF16, name="attn_out_bwd")
    delta = _attn_delta(do, o)
    do_f = jnp.stack([_fold(do, d) for d in DILS])
    lse_ff = jnp.stack([_fold(lse, d) for d in DILS])
    delta_f = jnp.stack([_fold(delta, d) for d in DILS])
    dqkv_f = _attn_bwd(qkv_f, do_f, lse_ff, delta_f, bias)
    gw_qkv = _mm(dqkv_f, hf, mode="tn", nbat=NGROUPS, M=3 * D, N=D, K=S, tm=1024, tn=D, tk=1024, out_dtype=BF16,
                 out_2d_rows=NGROUPS * 3 * D, name="gw_qkv")
    token = comm.send_grads(3, {"wo": gw_o, "qkv": gw_qkv})
    dh0_f = _mm(dqkv_f, W["qkv"], mode="nn", nbat=NGROUPS, M=S, N=D, K=3 * D, tm=1024, tn=D, tk=1024,
                out_dtype=F32, b_boff=(3, 0), deps=(token,), name="qkv_proj_bwd")
    dh0 = _unfold(dh0_f[0], DILS[0]) + _unfold(dh0_f[1], DILS[1]) + _unfold(dh0_f[2], DILS[2])
    grad_x, d_attn = _rms_bwd(dh0, x, g_attn, dx1, "rms_attn_bwd")

    vec = jnp.concatenate([d_attn, d_f0, d_f1, d_fin, d_pool, d_scale, jnp.zeros((2, D), F32)], axis=0)
    return loss, grad_x, vec


def _mesh_pos():
    x, y, c = lax.axis_index("x"), lax.axis_index("y"), lax.axis_index("c")
    return x, y, c, 4 * x + 2 * y + c


def _peer(x, y, c, k):
    kx, ky, kc = (k >> 2) & 1, (k >> 1) & 1, k & 1
    px = 1 - x if kx else x
    py = 1 - y if ky else y
    pc = 1 - c if kc else c
    return (px, py, pc), 4 * px + 2 * py + pc


ANY = pl.BlockSpec(memory_space=pl.ANY)


HBM = pl.BlockSpec(memory_space=pltpu.HBM)
SEMS = pl.BlockSpec(memory_space=pltpu.SEMAPHORE)
EFFECT = pltpu.SideEffectType.DATAFLOW_SIDE_EFFECTING
NPEER = NDEV - 1

AG_GROUPS = (("qkv",), ("wo", "gu0", "d0"), ("wpi", "pg", "gu1", "d1"))
AG_ORDER = tuple(n for grp in AG_GROUPS for n in grp)
RS_GROUPS = (("d1", "gu1"), ("pg", "wpi"), ("d0", "gu0"), ("wo", "qkv"))


def _hbm(a):
    return pltpu.with_memory_space_constraint(a, pltpu.HBM)


def _remote(src, dst, send, recv, peer):
    return pltpu.make_async_remote_copy(src_ref=src, dst_ref=dst, send_sem=send, recv_sem=recv, device_id=peer,
                                        device_id_type=pl.DeviceIdType.MESH)


def _bcast_all(v, name, deps=()):
    W = v.shape[1]
    nd = len(deps)

    def kern(v_ref, *rest):
        o_ref, send, recv, lsem = rest[nd:]
        x, y, c, me = _mesh_pos()
        own = pltpu.make_async_copy(v_ref, o_ref.at[me], lsem)
        own.start()
        cps = [_remote(v_ref, o_ref.at[me], send.at[k - 1], recv.at[k - 1], _peer(x, y, c, k)[0])
               for k in range(1, NDEV)]
        for cp in cps:
            cp.start()
        for cp in cps:
            cp.wait_recv()
            cp.wait_send()
        own.wait()

    return pl.pallas_call(
        kern, in_specs=[ANY] * (1 + nd), out_specs=ANY, out_shape=jax.ShapeDtypeStruct((NDEV, 8, W), F32),
        scratch_shapes=[pltpu.SemaphoreType.DMA((NPEER,)), pltpu.SemaphoreType.DMA((NPEER,)),
                        pltpu.SemaphoreType.DMA(())],
        name=name)(v, *deps)


def _split_start(srcs, src_of, lands, copy_refs, name):
    ns, n = len(srcs), len(lands)

    def body(*refs):
        ins, land = refs[:ns], refs[ns:ns + n]
        send, recv = refs[ns + n], refs[ns + n + 1]
        token = refs[-1]
        x, y, c, me = _mesh_pos()
        for j in range(n):
            for k in range(1, NDEV):
                peer, pid = _peer(x, y, c, k)
                src, dst = copy_refs(j, (land[j] if src_of[j] is None else ins[src_of[j]]), land[j], me, pid)
                _remote(src, dst, send.at[j * NPEER + k - 1], recv.at[j * NPEER + k - 1], peer).start()
        token[...] = jnp.zeros_like(token)

    outs = pl.pallas_call(
        body, name=name,
        out_shape=(pltpu.SemaphoreType.DMA((n * NPEER,)), pltpu.SemaphoreType.DMA((n * NPEER,)))
        + tuple(pltpu.HBM(a.shape, a.dtype) for a in srcs) + tuple(pltpu.HBM(a.shape, a.dtype) for a in lands)
        + (jax.ShapeDtypeStruct((8, 128), F32),),
        in_specs=(HBM,) * (ns + n),
        out_specs=(SEMS, SEMS) + (HBM,) * (ns + n) + (pl.BlockSpec(memory_space=pltpu.VMEM),),
        input_output_aliases={i: 2 + i for i in range(ns + n)},
        compiler_params=pltpu.CompilerParams(has_side_effects=EFFECT),
    )(*[_hbm(a) for a in srcs], *[_hbm(a) for a in lands])
    return outs[0], outs[1], list(outs[2:2 + ns]), list(outs[2 + ns:2 + ns + n]), outs[-1]


def _split_wait(srcs, src_of, lands, send, recv, sem_rows, wait_refs, after, name):
    ns, n = len(srcs), len(lands)

    def body(*refs):
        ins, land = refs[:ns], refs[ns:ns + n]
        send_ref, recv_ref = refs[ns + n], refs[ns + n + 1]
        x, y, c, me = _mesh_pos()
        for j in range(n):
            for k in range(1, NDEV):
                peer, _ = _peer(x, y, c, k)
                src, dst = wait_refs(j, (land[j] if src_of[j] is None else ins[src_of[j]]), land[j])
                sem = sem_rows[j] * NPEER + k - 1
                cp = _remote(src, dst, send_ref.at[sem], recv_ref.at[sem], peer)
                cp.wait_send()
                cp.wait_recv()

    outs = pl.pallas_call(
        body, name=name,
        out_shape=tuple(pltpu.HBM(a.shape, a.dtype) for a in srcs) + tuple(pltpu.HBM(a.shape, a.dtype) for a in lands),
        in_specs=(HBM,) * (ns + n) + (SEMS, SEMS, ANY),
        out_specs=(HBM,) * (ns + n),
        input_output_aliases={i: i for i in range(ns + n)},
        compiler_params=pltpu.CompilerParams(has_side_effects=EFFECT),
    )(*srcs, *lands, send, recv, after)
    return list(outs[:ns]), list(outs[ns:])


class _Comm:
    def __init__(self, shards, me):
        names = AG_ORDER
        rows = [SEC_ROWS[n] for n in names]
        self.me = me
        lands = [lax.dynamic_update_slice(lax.empty((NDEV * r, D), BF16), shards[n], (_shard_pos(n, me), 0))
                 for n, r in zip(names, rows)]

        def copy_refs(j, src, land, me, pid):
            own = land.at[pl.ds(pl.multiple_of(_shard_pos(names[j], me), 16), rows[j])]
            return own, own

        self.ag_send, self.ag_recv, _, lands, self.ag_token = _split_start(
            [], [None] * len(names), lands, copy_refs, "ag_start")
        self.ag_land = dict(zip(names, lands))
        self.rs = []

    def weights(self, group, after):
        names = AG_GROUPS[group]
        idx = [AG_ORDER.index(n) for n in names]
        rows = [SEC_ROWS[n] for n in names]

        def wait_refs(j, src, land):
            return land.at[pl.ds(0, rows[j])], land.at[pl.ds(0, rows[j])]

        _, lands = _split_wait([], [None] * len(names), [self.ag_land[n] for n in names], self.ag_send,
                               self.ag_recv, idx, wait_refs, after, f"ag_wait{group}")
        return dict(zip(names, lands))

    def send_grads(self, group, gws):
        names = RS_GROUPS[group]
        rows = [SEC_ROWS[n] for n in names]
        grads = [gws[n] for n in names]
        me = self.me
        lands = [lax.dynamic_update_slice(
            lax.empty((NDEV, r, D), BF16),
            lax.dynamic_slice(g, (_shard_pos(n, me), 0), (r, D))[None], (me, 0, 0))
            for n, r, g in zip(names, rows, grads)]

        def copy_refs(j, src, land, me, pid):
            return src.at[pl.ds(pl.multiple_of(_shard_pos(names[j], pid), 16), rows[j])], land.at[me]

        send, recv, srcs, lands, token = _split_start(grads, list(range(len(names))), lands, copy_refs,
                                                      f"rs_start{group}")
        self.rs.append((names, rows, send, recv, srcs, lands))
        return token

    def received(self, group, after):
        names, rows, send, recv, srcs, lands = self.rs[group]

        def wait_refs(j, src, land):
            return src.at[pl.ds(0, rows[j])], land.at[0]

        _, lands = _split_wait(srcs, list(range(len(names))), lands, send, recv, list(range(len(names))), wait_refs,
                               after, f"rs_wait{group}")
        return dict(zip(names, lands))


def _adamw(R, w, m, v, *, rows, tr, name, off=0):
    c1 = 1.0 / (1.0 - ADAM_B1 ** ADAM_STEP)
    c2 = 1.0 / (1.0 - ADAM_B2 ** ADAM_STEP)

    def kern(r_ref, w_ref, m_ref, v_ref, g_out, d_out, m_out, v_out):
        g = r_ref[0].astype(F32)
        for dev in range(1, NDEV):
            g = g + r_ref[dev].astype(F32)
        mn = ADAM_B1 * m_ref[...] + (1.0 - ADAM_B1) * g
        vn = ADAM_B2 * v_ref[...] + (1.0 - ADAM_B2) * (g * g)
        g_out[...] = g
        m_out[...] = mn
        v_out[...] = vn
        d_out[...] = -ADAM_LR * ((mn * c1) / (jnp.sqrt(vn * c2) + ADAM_EPS) + ADAM_WD * w_ref[...])

    tile = pl.BlockSpec((tr, D), lambda i: (i, 0))
    wtile = pl.BlockSpec((tr, D), lambda i: (i + off, 0))
    shp = jax.ShapeDtypeStruct((rows, D), F32)
    return pl.pallas_call(
        kern, grid=(rows // tr,),
        in_specs=[pl.BlockSpec((NDEV, tr, D), lambda i: (0, i, 0)), wtile, wtile, wtile],
        out_specs=[tile] * 4, out_shape=[shp] * 4, compiler_params=_cparams(1), name=name)(R, w, m, v)


ADAM_TILE = {"qkv": 384, "wo": 128, "wpi": 128, "gu0": 352, "gu1": 352, "d0": 352, "d1": 352, "pg": 32}


def _pack_sections(w_qkv, w_attn_out, w_pool_in, w_pool_group, w_ffn_gate_up, w_ffn_down):
    pg = w_pool_group[0].transpose(1, 0, 2).reshape(SEC_ROWS["pg"], D)
    return {"qkv": w_qkv[0].T, "wo": w_attn_out[0], "wpi": w_pool_in[0], "gu0": w_ffn_gate_up[0].T,
            "gu1": w_ffn_gate_up[1].T, "d0": w_ffn_down[0], "d1": w_ffn_down[1], "pg": pg}


def _unpack(p):
    def sec(n):
        return p[n]
    w_qkv = sec("qkv").T[None]
    w_attn_out = sec("wo")[None]
    w_pool_in = sec("wpi")[None]
    w_pool_group = sec("pg").reshape(SEC_ROWS["pg"], POOL_G, PGD).transpose(1, 0, 2)[None]
    w_gu = jnp.stack([sec("gu0").T, sec("gu1").T])
    w_d = jnp.stack([sec("d0"), sec("d1")])
    return w_qkv, w_attn_out, w_pool_in, w_pool_group, w_gu, w_d


def _vec_pack(attn_norm, ffn_norm, final_norm, pool_norm_sh, pool_scale_sh, me):
    def place(sh):
        return lax.dynamic_update_slice(jnp.zeros((1, D), F32), sh, (0, me * 128))
    return jnp.concatenate([attn_norm, ffn_norm, final_norm.reshape(1, D), place(pool_norm_sh),
                            place(pool_scale_sh), jnp.zeros((2, D), F32)], axis=0)


def _vec_unpack(p, me):
    def take(r):
        return lax.dynamic_slice(p[r:r + 1], (0, me * 128), (1, 128))
    return p[0:1], p[1:3], p[3], take(4), take(5)


def kernel(x, attn_norm, w_qkv, w_attn_out, pool_norm, w_pool_in, w_pool_group, pool_scale, ffn_norm, w_ffn_gate_up, w_ffn_down, final_norm, loss_target, m_attn_norm, m_w_qkv, m_w_attn_out, m_pool_norm, m_w_pool_in, m_w_pool_group, m_pool_scale, m_ffn_norm, m_w_ffn_gate_up, m_w_ffn_down, m_final_norm, v_attn_norm, v_w_qkv, v_w_attn_out, v_pool_norm, v_w_pool_in, v_w_pool_group, v_pool_scale, v_ffn_norm, v_w_ffn_gate_up, v_w_ffn_down, v_final_norm):
    me = 4 * lax.axis_index("x") + 2 * lax.axis_index("y") + lax.axis_index("c")

    pw = _pack_sections(w_qkv, w_attn_out, w_pool_in, w_pool_group, w_ffn_gate_up, w_ffn_down)
    pm = _pack_sections(m_w_qkv, m_w_attn_out, m_w_pool_in, m_w_pool_group, m_w_ffn_gate_up, m_w_ffn_down)
    pv = _pack_sections(v_w_qkv, v_w_attn_out, v_w_pool_in, v_w_pool_group, v_w_ffn_gate_up, v_w_ffn_down)
    vsh = jnp.concatenate([pool_norm, pool_scale, jnp.zeros((6, 128), F32)], axis=0)

    comm = _Comm({n: pw[n].astype(BF16) for n, _ in SECTIONS}, me)
    vg = _bcast_all(vsh, "gather_pool_vectors")
    pool_norm_full = vg[:, 0, :].reshape(1, D)
    pool_scale_full = vg[:, 1, :].reshape(1, D)

    loss, grad_x, vec = _local_step(x[0], loss_target[0], comm, attn_norm, ffn_norm, final_norm,
                                    pool_norm_full, pool_scale_full)
    loss = lax.psum(loss, MESH_AXES)

    vw = _vec_pack(attn_norm, ffn_norm, final_norm, pool_norm, pool_scale, me)
    vm = _vec_pack(m_attn_norm, m_ffn_norm, m_final_norm, m_pool_norm, m_pool_scale, me)
    vv = _vec_pack(v_attn_norm, v_ffn_norm, v_final_norm, v_pool_norm, v_pool_scale, me)

    sec_out = [{}, {}, {}, {}]
    vec_out = None
    after = grad_x
    for group in range(len(RS_GROUPS)):
        if group == len(RS_GROUPS) - 1:
            VR = _bcast_all(vec, "exchange_vector_grads", deps=(after,))
            vec_out = _adamw(VR, vw, vm, vv, rows=8, tr=8, name="adamw_vec")
            after = vec_out[0]
        for n, R in comm.received(group, after).items():
            tr = ADAM_TILE[n]
            res = _adamw(R, pw[n], pm[n], pv[n], rows=SEC_ROWS[n], tr=tr, name=f"adamw_{n}")
            for kind in range(4):
                sec_out[kind][n] = res[kind]
            after = res[0]

    outs = []
    for kind in range(4):
        q, o, pi, pg, gu, dn = _unpack(sec_out[kind])
        an, fn, fin, pn, ps = _vec_unpack(vec_out[kind], me)
        outs.append((an, q, o, pn, pi, pg, ps, fn, gu, dn, fin))
    return (loss, grad_x[None]) + outs[0] + outs[1] + outs[2] + outs[3]
```

```python
import jax
import jax.numpy as jnp
from jax import lax
from jax.experimental import pallas as pl
from jax.experimental.pallas import tpu as pltpu

F32 = jnp.float32
BF16 = jnp.bfloat16

D = 1024
NDEV = 8
HEADS = 8
HD = 128
QB = 128
NGROUPS = 3
DILS = (1, 4, 16)
DFF = 2816
HCH = 1408
POOL_G = 4
PGD = 256
RMS_EPS = 1e-6
NEG = -1e30

ADAM_LR = 0.001
ADAM_B1 = 0.9
ADAM_B2 = 0.999
ADAM_EPS = 1e-08
ADAM_WD = 0.01
ADAM_STEP = 10

VMEM_LIMIT = 52 * 1024 * 1024

SECTIONS = (("qkv", 1152), ("wo", 128), ("wpi", 128), ("gu0", 704), ("gu1", 704),
            ("d0", 352), ("d1", 352), ("pg", 32))
LOC_OFF = {}
GLB_OFF = {}
_o = 0
for _n, _r in SECTIONS:
    LOC_OFF[_n] = _o
    GLB_OFF[_n] = _o * NDEV
    _o += _r
PACK_ROWS = _o
GLB_ROWS = PACK_ROWS * NDEV
SEC_ROWS = dict(SECTIONS)


def _cparams(n_grid):
    return pltpu.CompilerParams(dimension_semantics=("arbitrary",) * n_grid, vmem_limit_bytes=VMEM_LIMIT)


def _shard_pos(name, dev):
    n = SEC_ROWS[name]
    if name in ("gu0", "gu1"):
        return ((dev % 4) // 2) * (2 * HCH) + (dev // 4) * HCH + (dev % 2) * n
    return dev * n


def _mm(a, b, *, mode, M, N, K, tm, tn, tk, out_dtype, name, nbat=1, a_off=(0, 0), b_off=(0, 0),
        a_boff=(0, 0), b_boff=(0, 0), res=None, out_2d_rows=None, deps=()):
    nm, nn, nk = M // tm, N // tn, K // tk
    assert nm * tm == M and nn * tn == N and nk * tk == K
    if mode == "nn":
        a_bs, b_bs = (tm, tk), (tk, tn)
        a_ix = lambda i, j, k: (i, k)
        b_ix = lambda i, j, k: (k, j)
        dims = (((1,), (0,)), ((), ()))
    elif mode == "nt":
        a_bs, b_bs = (tm, tk), (tn, tk)
        a_ix = lambda i, j, k: (i, k)
        b_ix = lambda i, j, k: (j, k)
        dims = (((1,), (1,)), ((), ()))
    else:
        a_bs, b_bs = (tk, tm), (tk, tn)
        a_ix = lambda i, j, k: (k, i)
        b_ix = lambda i, j, k: (k, j)
        dims = (((0,), (0,)), ((), ()))

    def spec(arr, bs, ix, off, boff):
        if arr.ndim == 3:
            return pl.BlockSpec((None,) + bs, lambda bb, i, j, k: (bb,) + ix(i, j, k))

        def im(bb, i, j, k):
            r, c = ix(i, j, k)
            return (r + off[0] + bb * boff[0], c + off[1] + bb * boff[1])
        return pl.BlockSpec(bs, im)

    in_specs = [spec(a, a_bs, a_ix, a_off, a_boff), spec(b, b_bs, b_ix, b_off, b_boff)]
    args = [a, b]
    if res is not None:
        in_specs.append(pl.BlockSpec((tm, tn), lambda bb, i, j, k: (i, j)))
        args.append(res)
    if nbat > 1 and out_2d_rows is None:
        out_shape = jax.ShapeDtypeStruct((nbat, M, N), out_dtype)
        out_spec = pl.BlockSpec((None, tm, tn), lambda bb, i, j, k: (bb, i, j))
    else:
        rows = M if out_2d_rows is None else out_2d_rows
        out_shape = jax.ShapeDtypeStruct((rows, N), out_dtype)
        out_spec = pl.BlockSpec((tm, tn), lambda bb, i, j, k: (i + bb * nm, j))
    has_res = res is not None
    for dep in deps:
        in_specs.append(pl.BlockSpec(memory_space=pl.ANY))
        args.append(dep)
    o_pos = 2 + int(has_res) + len(deps)

    def kern(*refs):
        a_ref, b_ref = refs[0], refs[1]
        res_ref = refs[2] if has_res else None
        o_ref = refs[o_pos]
        av = a_ref[...]
        bv = b_ref[...]
        if av.dtype != BF16:
            av = av.astype(BF16)
        if bv.dtype != BF16:
            bv = bv.astype(BF16)
        part = lax.dot_general(av, bv, dims, preferred_element_type=F32)

        def write(val):
            if has_res:
                val = val + res_ref[...]
            o_ref[...] = val.astype(out_dtype)

        if nk == 1:
            write(part)
        else:
            acc_ref = refs[-1]
            k = pl.program_id(3)

            @pl.when(k == 0)
            def _():
                acc_ref[...] = part

            @pl.when(k > 0)
            def _():
                acc_ref[...] += part

            @pl.when(k == nk - 1)
            def _():
                write(acc_ref[...])

    scratch = [pltpu.VMEM((tm, tn), F32)] if nk > 1 else []
    return pl.pallas_call(
        kern, grid=(nbat, nm, nn, nk), in_specs=in_specs, out_specs=out_spec, out_shape=out_shape,
        scratch_shapes=scratch, compiler_params=_cparams(4), name=name)(*args)


def _rms_fwd(x, g, name, deps=()):
    S = x.shape[0]
    tr = 512

    def kern(x_ref, g_ref, *rest):
        h_ref = rest[-1]
        xv = x_ref[...]
        r = lax.rsqrt(jnp.mean(xv * xv, axis=-1, keepdims=True) + RMS_EPS)
        h_ref[...] = (xv * r * g_ref[...]).astype(BF16)

    return pl.pallas_call(
        kern, grid=(S // tr,),
        in_specs=[pl.BlockSpec((tr, D), lambda i: (i, 0)), pl.BlockSpec((1, D), lambda i: (0, 0))]
        + [pl.BlockSpec(memory_space=pl.ANY)] * len(deps),
        out_specs=pl.BlockSpec((tr, D), lambda i: (i, 0)),
        out_shape=jax.ShapeDtypeStruct((S, D), BF16), compiler_params=_cparams(1), name=name)(x, g, *deps)


def _rms_bwd(dh, x, g, dres, name):
    S = x.shape[0]
    tr = 512

    def kern(dh_ref, x_ref, g_ref, dres_ref, dx_ref, dg_ref):
        i = pl.program_id(0)
        xv = x_ref[...]
        dhv = dh_ref[...].astype(F32)
        r = lax.rsqrt(jnp.mean(xv * xv, axis=-1, keepdims=True) + RMS_EPS)
        xhat = xv * r
        gy = dhv * g_ref[...]
        dx_ref[...] = dres_ref[...] + r * (gy - xhat * jnp.mean(gy * xhat, axis=-1, keepdims=True))
        part = jnp.sum(dhv * xhat, axis=0, keepdims=True)

        @pl.when(i == 0)
        def _():
            dg_ref[...] = part

        @pl.when(i > 0)
        def _():
            dg_ref[...] += part

    row = pl.BlockSpec((tr, D), lambda i: (i, 0))
    vec = pl.BlockSpec((1, D), lambda i: (0, 0))
    return pl.pallas_call(
        kern, grid=(S // tr,), in_specs=[row, row, vec, row], out_specs=[row, vec],
        out_shape=[jax.ShapeDtypeStruct((S, D), F32), jax.ShapeDtypeStruct((1, D), F32)],
        compiler_params=_cparams(1), name=name)(dh, x, g, dres)


def _loss_head(x, g, tgt):
    S = x.shape[0]
    tr = 512

    def kern(x_ref, g_ref, t_ref, dx_ref, dg_ref, ls_ref):
        i = pl.program_id(0)
        xv = x_ref[...]
        gv = g_ref[...]
        r = lax.rsqrt(jnp.mean(xv * xv, axis=-1, keepdims=True) + RMS_EPS)
        xhat = xv * r
        e = xhat * gv - t_ref[...]
        dy = e * (1.0 / D)
        gy = dy * gv
        dx_ref[...] = r * (gy - xhat * jnp.mean(gy * xhat, axis=-1, keepdims=True))
        dgp = jnp.sum(dy * xhat, axis=0, keepdims=True)
        lsp = jnp.sum(e * e, axis=0, keepdims=True)

        @pl.when(i == 0)
        def _():
            dg_ref[...] = dgp
            ls_ref[...] = lsp

        @pl.when(i > 0)
        def _():
            dg_ref[...] += dgp
            ls_ref[...] += lsp

    row = pl.BlockSpec((tr, D), lambda i: (i, 0))
    vec = pl.BlockSpec((1, D), lambda i: (0, 0))
    return pl.pallas_call(
        kern, grid=(S // tr,), in_specs=[row, vec, row], out_specs=[row, vec, vec],
        out_shape=[jax.ShapeDtypeStruct((S, D), F32), jax.ShapeDtypeStruct((1, D), F32),
                   jax.ShapeDtypeStruct((1, D), F32)],
        compiler_params=_cparams(1), name="loss_head")(x, g, tgt)


def _ffn_up(h, G, name):
    S = h.shape[0]
    tm = 512
    nj = DFF // HCH

    def kern(h_ref, w_ref, gu_ref, act_ref):
        gu = lax.dot_general(h_ref[...], w_ref[...], (((1,), (1,)), ((), ())), preferred_element_type=F32)
        gu_ref[...] = gu.astype(BF16)
        gate = gu[:, :HCH]
        up = gu[:, HCH:]
        act_ref[...] = (gate * jax.nn.sigmoid(gate) * up).astype(BF16)

    return pl.pallas_call(
        kern, grid=(nj, S // tm),
        in_specs=[pl.BlockSpec((tm, D), lambda j, i: (i, 0)),
                  pl.BlockSpec((2 * HCH, D), lambda j, i: (j, 0))],
        out_specs=[pl.BlockSpec((tm, 2 * HCH), lambda j, i: (i, j)),
                   pl.BlockSpec((tm, HCH), lambda j, i: (i, j))],
        out_shape=[jax.ShapeDtypeStruct((S, 2 * DFF), BF16), jax.ShapeDtypeStruct((S, DFF), BF16)],
        compiler_params=_cparams(2), name=name)(h, G)


def _ffn_down_bwd(dx, G, gu, name):
    S = dx.shape[0]
    tm = 512
    nj = DFF // HCH

    def kern(dx_ref, w_ref, gu_ref, o_ref):
        dact = lax.dot_general(dx_ref[...].astype(BF16), w_ref[...], (((1,), (1,)), ((), ())),
                               preferred_element_type=F32)
        gate = gu_ref[:, :HCH].astype(F32)
        up = gu_ref[:, HCH:].astype(F32)
        sig = jax.nn.sigmoid(gate)
        silu = gate * sig
        o_ref[:, :HCH] = (dact * up * (sig * (1.0 + gate * (1.0 - sig)))).astype(BF16)
        o_ref[:, HCH:] = (dact * silu).astype(BF16)

    return pl.pallas_call(
        kern, grid=(nj, S // tm),
        in_specs=[pl.BlockSpec((tm, D), lambda j, i: (i, 0)),
                  pl.BlockSpec((HCH, D), lambda j, i: (j, 0)),
                  pl.BlockSpec((tm, 2 * HCH), lambda j, i: (i, j))],
        out_specs=pl.BlockSpec((tm, 2 * HCH), lambda j, i: (i, j)),
        out_shape=jax.ShapeDtypeStruct((S, 2 * DFF), BF16),
        compiler_params=_cparams(2), name=name)(dx, G, gu)


def _trail(u, *, backward, name):
    S = u.shape[0]

    def kern(u_ref, o_ref):
        g = pl.program_id(0)
        uv = u_ref[...].astype(F32)
        row = lax.broadcasted_iota(jnp.int32, uv.shape, 0)
        win = jnp.left_shift(jnp.int32(2), g)
        cnt = jnp.minimum(row + 1, win).astype(F32)
        s = uv / cnt if backward else uv
        levels = []
        for k in (1, 2, 4, 8):
            if backward:
                sh = jnp.where(row < S - k, pltpu.roll(s, S - k, 0), 0.0)
            else:
                sh = jnp.where(row >= k, pltpu.roll(s, k, 0), 0.0)
            s = s + sh
            levels.append(s)
        sel = jnp.where(g == 0, levels[0], jnp.where(g == 1, levels[1], jnp.where(g == 2, levels[2], levels[3])))
        if backward:
            o_ref[...] = (sel - uv).astype(BF16)
        else:
            o_ref[...] = (sel / cnt - uv).astype(BF16)

    blk = pl.BlockSpec((S, PGD), lambda g: (0, g))
    return pl.pallas_call(
        kern, grid=(POOL_G,), in_specs=[blk], out_specs=blk,
        out_shape=jax.ShapeDtypeStruct((S, D), BF16), compiler_params=_cparams(1), name=name)(u)


def _pool_out(yd, G, scale, xres):
    S = yd.shape[0]
    tm = 1024

    def kern(y_ref, w_ref, s_ref, x_ref, o_ref):
        z = jnp.dot(y_ref[...], w_ref[...], preferred_element_type=F32)
        o_ref[...] = x_ref[...] + z * s_ref[...]

    tile = pl.BlockSpec((tm, PGD), lambda i, g: (i, g))
    return pl.pallas_call(
        kern, grid=(S // tm, POOL_G),
        in_specs=[tile, pl.BlockSpec((PGD, PGD), lambda i, g: (0, g)),
                  pl.BlockSpec((1, PGD), lambda i, g: (0, g)), tile],
        out_specs=tile, out_shape=jax.ShapeDtypeStruct((S, D), F32),
        compiler_params=_cparams(2), name="pool_out")(yd, G, scale, xres)


def _pool_out_bwd(dz, yd, G, scale):
    S = yd.shape[0]
    tm = 1024
    ni = S // tm

    def kern(dz_ref, y_ref, w_ref, s_ref, dy_ref, ds_ref, dw_ref, acc_ref):
        i = pl.program_id(1)
        dzv = dz_ref[...]
        yv = y_ref[...]
        wv = w_ref[...]
        zraw = jnp.dot(yv, wv, preferred_element_type=F32)
        dsp = jnp.sum(dzv * zraw, axis=0, keepdims=True)
        dzr = (dzv * s_ref[...]).astype(BF16)
        dy_ref[...] = lax.dot_general(dzr, wv, (((1,), (1,)), ((), ())), preferred_element_type=F32)
        dwp = lax.dot_general(yv, dzr, (((0,), (0,)), ((), ())), preferred_element_type=F32)

        @pl.when(i == 0)
        def _():
            ds_ref[...] = dsp
            acc_ref[...] = dwp

        @pl.when(i > 0)
        def _():
            ds_ref[...] += dsp
            acc_ref[...] += dwp

        @pl.when(i == ni - 1)
        def _():
            dw_ref[...] = acc_ref[...].astype(BF16)

    tile = pl.BlockSpec((tm, PGD), lambda g, i: (i, g))
    return pl.pallas_call(
        kern, grid=(POOL_G, ni),
        in_specs=[tile, tile, pl.BlockSpec((PGD, PGD), lambda g, i: (0, g)),
                  pl.BlockSpec((1, PGD), lambda g, i: (0, g))],
        out_specs=[tile, pl.BlockSpec((1, PGD), lambda g, i: (0, g)),
                   pl.BlockSpec((PGD, PGD), lambda g, i: (0, g))],
        out_shape=[jax.ShapeDtypeStruct((S, D), F32), jax.ShapeDtypeStruct((1, D), F32),
                   jax.ShapeDtypeStruct((PGD, D), BF16)],
        scratch_shapes=[pltpu.VMEM((PGD, PGD), F32)],
        compiler_params=_cparams(2), name="pool_out_bwd")(dz, yd, G, scale)


def _bias_table():
    qi = jnp.arange(QB)[:, None]
    ki = jnp.arange(2 * QB)[None, :]
    delta = QB + qi - ki
    inband = (delta >= 0) & (delta <= QB)
    n = NGROUPS * HEADS
    slopes = jnp.exp2(-8.0 * jnp.arange(1, n + 1, dtype=F32) / n).reshape(NGROUPS, HEADS)
    dil = jnp.asarray(DILS, F32)
    bias = -slopes[:, :, None, None] * (delta.astype(F32)[None, None] * dil[:, None, None, None])
    return jnp.where(inband[None, None], bias, NEG)


def _group_scalars(g, b, nblk):
    nb = jnp.right_shift(jnp.int32(nblk), 2 * g)
    has_prev = jnp.bitwise_and(b, nb - 1) != 0
    return has_prev


def _attn_fwd(qkv_f, bias):
    S = qkv_f.shape[1]
    nblk = S // QB
    scale = HD ** -0.5

    def kern(q_ref, kc_ref, kp_ref, vc_ref, vp_ref, b_ref, o_ref, l_ref):
        g = pl.program_id(0)
        b = pl.program_id(1)
        has_prev = _group_scalars(g, b, nblk)
        col = lax.broadcasted_iota(jnp.int32, (QB, 2 * QB), 1)
        dead = jnp.logical_and(col < QB, jnp.logical_not(has_prev))
        lane = lax.broadcasted_iota(jnp.int32, (QB, HD), 1)
        lse_all = jnp.zeros((QB, HD), F32)
        for h in range(HEADS):
            sl = slice(h * HD, (h + 1) * HD)
            kk = jnp.concatenate([kp_ref[:, sl], kc_ref[:, sl]], axis=0)
            vv = jnp.concatenate([vp_ref[:, sl], vc_ref[:, sl]], axis=0)
            s = lax.dot_general(q_ref[:, sl], kk, (((1,), (1,)), ((), ())), preferred_element_type=F32)
            s = s * scale + b_ref[h]
            s = jnp.where(dead, NEG, s)
            m = jnp.max(s, axis=-1, keepdims=True)
            p = jnp.exp(s - m)
            den = jnp.sum(p, axis=-1, keepdims=True)
            o = jnp.dot(p.astype(BF16), vv, preferred_element_type=F32) / den
            o_ref[:, sl] = o.astype(BF16)
            lse_all = jnp.where(lane == h, m + jnp.log(den), lse_all)
        l_ref[...] = lse_all

    def blk(colblk, prev):
        if prev:
            return pl.BlockSpec((None, QB, D), lambda g, b: (g, jnp.maximum(b - 1, 0), colblk))
        return pl.BlockSpec((None, QB, D), lambda g, b: (g, b, colblk))

    return pl.pallas_call(
        kern, grid=(NGROUPS, nblk),
        in_specs=[blk(0, False), blk(1, False), blk(1, True), blk(2, False), blk(2, True),
                  pl.BlockSpec((None, HEADS, QB, 2 * QB), lambda g, b: (g, 0, 0, 0))],
        out_specs=[pl.BlockSpec((None, QB, D), lambda g, b: (g, b, 0)),
                   pl.BlockSpec((None, QB, HD), lambda g, b: (g, b, 0))],
        out_shape=[jax.ShapeDtypeStruct((NGROUPS, S, D), BF16), jax.ShapeDtypeStruct((NGROUPS, S, HD), F32)],
        compiler_params=_cparams(2), name="attn_fwd")(qkv_f, qkv_f, qkv_f, qkv_f, qkv_f, bias)


def _attn_merge(o_n, lse_n):
    S = o_n.shape[1]
    tm = 512

    def kern(o_ref, l_ref, om_ref, lm_ref):
        l0, l1, l2 = l_ref[0], l_ref[1], l_ref[2]
        m = jnp.maximum(jnp.maximum(l0, l1), l2)
        e0, e1, e2 = jnp.exp(l0 - m), jnp.exp(l1 - m), jnp.exp(l2 - m)
        tot = e0 + e1 + e2
        lm_ref[...] = m + jnp.log(tot)
        w0, w1, w2 = e0 / tot, e1 / tot, e2 / tot
        for h in range(HEADS):
            sl = slice(h * HD, (h + 1) * HD)
            acc = (w0[:, h:h + 1] * o_ref[0, :, sl].astype(F32) + w1[:, h:h + 1] * o_ref[1, :, sl].astype(F32)
                   + w2[:, h:h + 1] * o_ref[2, :, sl].astype(F32))
            om_ref[:, sl] = acc.astype(BF16)

    return pl.pallas_call(
        kern, grid=(S // tm,),
        in_specs=[pl.BlockSpec((NGROUPS, tm, D), lambda i: (0, i, 0)),
                  pl.BlockSpec((NGROUPS, tm, HD), lambda i: (0, i, 0))],
        out_specs=[pl.BlockSpec((tm, D), lambda i: (i, 0)), pl.BlockSpec((tm, HD), lambda i: (i, 0))],
        out_shape=[jax.ShapeDtypeStruct((S, D), BF16), jax.ShapeDtypeStruct((S, HD), F32)],
        compiler_params=_cparams(1), name="attn_merge")(o_n, lse_n)


def _attn_delta(do, o):
    S = o.shape[0]
    tm = 512

    def kern(do_ref, o_ref, d_ref):
        lane = lax.broadcasted_iota(jnp.int32, (tm, HD), 1)
        acc = jnp.zeros((tm, HD), F32)
        for h in range(HEADS):
            sl = slice(h * HD, (h + 1) * HD)
            prod = do_ref[:, sl].astype(F32) * o_ref[:, sl].astype(F32)
            acc = jnp.where(lane == h, jnp.sum(prod, axis=-1, keepdims=True), acc)
        d_ref[...] = acc

    row = pl.BlockSpec((tm, D), lambda i: (i, 0))
    return pl.pallas_call(
        kern, grid=(S // tm,), in_specs=[row, row], out_specs=pl.BlockSpec((tm, HD), lambda i: (i, 0)),
        out_shape=jax.ShapeDtypeStruct((S, HD), F32), compiler_params=_cparams(1), name="attn_delta")(do, o)


def _attn_bwd(qkv_f, do_f, lse_f, delta_f, bias):
    S = qkv_f.shape[1]
    nblk = S // QB
    scale = HD ** -0.5

    def kern(q_ref, kc_ref, kp_ref, vc_ref, vp_ref, do_ref, l_ref, d_ref, b_ref, out_ref, dq_c, dk_c, dv_c):
        g = pl.program_id(0)
        b = pl.program_id(1)

        @pl.when(b == 0)
        def _():
            dq_c[...] = jnp.zeros_like(dq_c)
            dk_c[...] = jnp.zeros_like(dk_c)
            dv_c[...] = jnp.zeros_like(dv_c)

        @pl.when(b == nblk)
        def _():
            out_ref[:, 0:D] = dq_c[...].astype(BF16)
            out_ref[:, D:2 * D] = dk_c[...].astype(BF16)
            out_ref[:, 2 * D:3 * D] = dv_c[...].astype(BF16)

        @pl.when(b < nblk)
        def _():
            has_prev = _group_scalars(g, b, nblk)
            col = lax.broadcasted_iota(jnp.int32, (QB, 2 * QB), 1)
            dead = jnp.logical_and(col < QB, jnp.logical_not(has_prev))
            out_ref[:, 0:D] = dq_c[...].astype(BF16)
            lv = l_ref[...]
            dv_ = d_ref[...]
            for h in range(HEADS):
                sl = slice(h * HD, (h + 1) * HD)
                qh = q_ref[:, sl]
                doh = do_ref[:, sl]
                kk = jnp.concatenate([kp_ref[:, sl], kc_ref[:, sl]], axis=0)
                vv = jnp.concatenate([vp_ref[:, sl], vc_ref[:, sl]], axis=0)
                s = lax.dot_general(qh, kk, (((1,), (1,)), ((), ())), preferred_element_type=F32)
                s = s * scale + b_ref[h]
                s = jnp.where(dead, NEG, s)
                p = jnp.exp(s - lv[:, h:h + 1])
                dp = lax.dot_general(doh, vv, (((1,), (1,)), ((), ())), preferred_element_type=F32)
                ds = (p * (dp - dv_[:, h:h + 1]) * scale).astype(BF16)
                pb = p.astype(BF16)
                dq_c[:, sl] = jnp.dot(ds, kk, preferred_element_type=F32)
                dkk = lax.dot_general(ds, qh, (((0,), (0,)), ((), ())), preferred_element_type=F32)
                dvv = lax.dot_general(pb, doh, (((0,), (0,)), ((), ())), preferred_element_type=F32)
                out_ref[:, D + h * HD:D + (h + 1) * HD] = (dk_c[:, sl] + dkk[:QB]).astype(BF16)
                out_ref[:, 2 * D + h * HD:2 * D + (h + 1) * HD] = (dv_c[:, sl] + dvv[:QB]).astype(BF16)
                dk_c[:, sl] = dkk[QB:]
                dv_c[:, sl] = dvv[QB:]

    last = nblk - 1

    def blk(colblk, prev):
        if prev:
            return pl.BlockSpec((None, QB, D), lambda g, b: (g, jnp.maximum(jnp.minimum(b, last) - 1, 0), colblk))
        return pl.BlockSpec((None, QB, D), lambda g, b: (g, jnp.minimum(b, last), colblk))

    stat = pl.BlockSpec((None, QB, HD), lambda g, b: (g, jnp.minimum(b, last), 0))
    return pl.pallas_call(
        kern, grid=(NGROUPS, nblk + 1),
        in_specs=[blk(0, False), blk(1, False), blk(1, True), blk(2, False), blk(2, True),
                  pl.BlockSpec((None, QB, D), lambda g, b: (g, jnp.minimum(b, last), 0)), stat, stat,
                  pl.BlockSpec((None, HEADS, QB, 2 * QB), lambda g, b: (g, 0, 0, 0))],
        out_specs=pl.BlockSpec((None, QB, 3 * D), lambda g, b: (g, jnp.maximum(b - 1, 0), 0)),
        out_shape=jax.ShapeDtypeStruct((NGROUPS, S, 3 * D), BF16),
        scratch_shapes=[pltpu.VMEM((QB, D), F32), pltpu.VMEM((QB, D), F32), pltpu.VMEM((QB, D), F32)],
        compiler_params=_cparams(2), name="attn_bwd")(qkv_f, qkv_f, qkv_f, qkv_f, qkv_f, do_f, lse_f, delta_f, bias)


def _fold(a, dil):
    if dil == 1:
        return a
    S, C = a.shape
    return a.reshape(S // dil, dil, C).transpose(1, 0, 2).reshape(S, C)


def _unfold(a, dil):
    if dil == 1:
        return a
    S, C = a.shape
    return a.reshape(dil, S // dil, C).transpose(1, 0, 2).reshape(S, C)


def _local_step(x, tgt, comm, attn_norm, ffn_norm, final_norm, pool_norm, pool_scale):
    S = x.shape[0]
    bias = _bias_table()
    g_attn = attn_norm.reshape(1, D)
    g_f0 = ffn_norm[0:1]
    g_f1 = ffn_norm[1:2]
    g_fin = final_norm.reshape(1, D)
    W = {}

    def ffn_fwd(xin, gain, l):
        h = _rms_fwd(xin, gain, f"rms_ffn{l}")
        gu, act = _ffn_up(h, W[f"gu{l}"], f"ffn_up{l}")
        xo = _mm(act, W[f"d{l}"], mode="nn", M=S, N=D, K=DFF, tm=512, tn=D, tk=DFF, out_dtype=F32,
                 res=xin, name=f"ffn_down{l}")
        return h, gu, act, xo

    def ffn_bwd(dxo, xin, gain, h, gu, act, l, rs_group):
        dgu = _ffn_down_bwd(dxo, W[f"d{l}"], gu, f"ffn_down_bwd{l}")
        gw_d = _mm(act, dxo, mode="tn", M=DFF, N=D, K=S, tm=HCH, tn=D, tk=1024, out_dtype=BF16, name=f"gw_d{l}")
        gw_gu = _mm(dgu, h, mode="tn", M=2 * DFF, N=D, K=S, tm=HCH, tn=D, tk=1024, out_dtype=BF16, name=f"gw_gu{l}")
        token = comm.send_grads(rs_group, {f"d{l}": gw_d, f"gu{l}": gw_gu})
        dh = _mm(dgu, W[f"gu{l}"], mode="nn", M=S, N=D, K=2 * DFF, tm=1024, tn=D, tk=HCH,
                 out_dtype=F32, deps=(token,), name=f"ffn_up_bwd{l}")
        dxin, dgain = _rms_bwd(dh, xin, gain, dxo, f"rms_ffn_bwd{l}")
        return dxin, dgain

    h0 = _rms_fwd(x, g_attn, "rms_attn", deps=(comm.ag_token,))
    hf = jnp.stack([_fold(h0, d) for d in DILS])
    W.update(comm.weights(0, hf))
    qkv_f = _mm(hf, W["qkv"], mode="nt", nbat=NGROUPS, M=S, N=3 * D, K=D, tm=1024, tn=1024, tk=D, out_dtype=BF16,
                b_boff=(3, 0), name="qkv_proj")
    o_f, lse_f = _attn_fwd(qkv_f, bias)
    o_n = jnp.stack([_unfold(o_f[g], d) for g, d in enumerate(DILS)])
    lse_n = jnp.stack([_unfold(lse_f[g], d) for g, d in enumerate(DILS)])
    o, lse = _attn_merge(o_n, lse_n)
    W.update(comm.weights(1, o))
    x1 = _mm(o, W["wo"], mode="nn", M=S, N=D, K=D, tm=1024, tn=D, tk=D, out_dtype=F32, res=x, name="attn_out")
    h1, gu0, act0, x2 = ffn_fwd(x1, g_f0, 0)

    W.update(comm.weights(2, x2))
    h2 = _rms_fwd(x2, pool_norm, "rms_pool")
    u = _mm(h2, W["wpi"], mode="nn", M=S, N=D, K=D, tm=1024, tn=D, tk=D, out_dtype=F32, name="pool_in")
    yd = _trail(u, backward=False, name="trail_fwd")
    x3 = _pool_out(yd, W["pg"], pool_scale, x2)
    h3, gu1, act1, x4 = ffn_fwd(x3, g_f1, 1)

    dx4, d_fin, lossvec = _loss_head(x4, g_fin, tgt)

    dx3, d_f1 = ffn_bwd(dx4, x3, g_f1, h3, gu1, act1, 1, 0)
    dyd, d_scale, gw_pg = _pool_out_bwd(dx3, yd, W["pg"], pool_scale)
    du = _trail(dyd, backward=True, name="trail_bwd")
    gw_pi = _mm(h2, du, mode="tn", M=D, N=D, K=S, tm=D, tn=D, tk=1024, out_dtype=BF16, name="gw_pi")
    token = comm.send_grads(1, {"pg": gw_pg, "wpi": gw_pi})
    dh2 = _mm(du, W["wpi"], mode="nt", M=S, N=D, K=D, tm=1024, tn=D, tk=D, out_dtype=F32,
              deps=(token,), name="pool_in_bwd")
    dx2, d_pool = _rms_bwd(dh2, x2, pool_norm, dx3, "rms_pool_bwd")
    dx1, d_f0 = ffn_bwd(dx2, x1, g_f0, h1, gu0, act0, 0, 2)

    gw_o = _mm(o, dx1, mode="tn", M=D, N=D, K=S, tm=D, tn=D, tk=1024, out_dtype=BF16, name="gw_o")
    do = _mm(dx1, W["wo"], mode="nt", M=S, N=D, K=D, tm=1024, tn=D, tk=D, out_dtype=BF16, name="attn_out_bwd")
    delta = _attn_delta(do, o)
    do_f = jnp.stack([_fold(do, d) for d in DILS])
    lse_ff = jnp.stack([_fold(lse, d) for d in DILS])
    delta_f = jnp.stack([_fold(delta, d) for d in DILS])
    dqkv_f = _attn_bwd(qkv_f, do_f, lse_ff, delta_f, bias)
    gw_qkv = _mm(dqkv_f, hf, mode="tn", nbat=NGROUPS, M=3 * D, N=D, K=S, tm=1024, tn=D, tk=1024, out_dtype=BF16,
                 out_2d_rows=NGROUPS * 3 * D, name="gw_qkv")
    token = comm.send_grads(3, {"wo": gw_o, "qkv": gw_qkv})
    dh0_f = _mm(dqkv_f, W["qkv"], mode="nn", nbat=NGROUPS, M=S, N=D, K=3 * D, tm=1024, tn=D, tk=1024,
                out_dtype=F32, b_boff=(3, 0), deps=(token,), name="qkv_proj_bwd")
    dh0 = _unfold(dh0_f[0], DILS[0]) + _unfold(dh0_f[1], DILS[1]) + _unfold(dh0_f[2], DILS[2])
    grad_x, d_attn = _rms_bwd(dh0, x, g_attn, dx1, "rms_attn_bwd")

    vec = jnp.concatenate([d_attn, d_f0, d_f1, d_fin, d_pool, d_scale, lossvec, jnp.zeros((1, D), F32)], axis=0)
    return grad_x, vec


def _mesh_pos():
    x, y, c = lax.axis_index("x"), lax.axis_index("y"), lax.axis_index("c")
    return x, y, c, 4 * x + 2 * y + c


def _peer(x, y, c, k):
    kx, ky, kc = (k >> 2) & 1, (k >> 1) & 1, k & 1
    px = 1 - x if kx else x
    py = 1 - y if ky else y
    pc = 1 - c if kc else c
    return (px, py, pc), 4 * px + 2 * py + pc


ANY = pl.BlockSpec(memory_space=pl.ANY)


HBM = pl.BlockSpec(memory_space=pltpu.HBM)
SEMS = pl.BlockSpec(memory_space=pltpu.SEMAPHORE)
EFFECT = pltpu.SideEffectType.DATAFLOW_SIDE_EFFECTING
NPEER = NDEV - 1

AG_GROUPS = (("qkv",), ("wo", "gu0", "d0"), ("wpi", "pg", "gu1", "d1"))
AG_ORDER = tuple(n for grp in AG_GROUPS for n in grp)
RS_GROUPS = (("d1", "gu1"), ("pg", "wpi"), ("d0", "gu0"), ("wo", "qkv"))


def _hbm(a):
    return pltpu.with_memory_space_constraint(a, pltpu.HBM)


def _remote(src, dst, send, recv, peer):
    return pltpu.make_async_remote_copy(src_ref=src, dst_ref=dst, send_sem=send, recv_sem=recv, device_id=peer,
                                        device_id_type=pl.DeviceIdType.MESH)


def _bcast_all(v, name, deps=()):
    W = v.shape[1]
    nd = len(deps)

    def kern(v_ref, *rest):
        o_ref, send, recv, lsem = rest[nd:]
        x, y, c, me = _mesh_pos()
        own = pltpu.make_async_copy(v_ref, o_ref.at[me], lsem)
        own.start()
        cps = [_remote(v_ref, o_ref.at[me], send.at[k - 1], recv.at[k - 1], _peer(x, y, c, k)[0])
               for k in range(1, NDEV)]
        for cp in cps:
            cp.start()
        for cp in cps:
            cp.wait_recv()
            cp.wait_send()
        own.wait()

    return pl.pallas_call(
        kern, in_specs=[ANY] * (1 + nd), out_specs=ANY, out_shape=jax.ShapeDtypeStruct((NDEV, 8, W), F32),
        scratch_shapes=[pltpu.SemaphoreType.DMA((NPEER,)), pltpu.SemaphoreType.DMA((NPEER,)),
                        pltpu.SemaphoreType.DMA(())],
        name=name)(v, *deps)


def _split_start(srcs, src_of, lands, copy_refs, name, deps=()):
    ns, n, nd = len(srcs), len(lands), len(deps)

    def body(*refs):
        ins, land = refs[:ns], refs[ns:ns + n]
        send, recv = refs[ns + n + nd], refs[ns + n + nd + 1]
        token = refs[-1]
        x, y, c, me = _mesh_pos()
        for j in range(n):
            for k in range(1, NDEV):
                peer, pid = _peer(x, y, c, k)
                src, dst = copy_refs(j, (land[j] if src_of[j] is None else ins[src_of[j]]), land[j], me, pid)
                _remote(src, dst, send.at[j * NPEER + k - 1], recv.at[j * NPEER + k - 1], peer).start()
        token[...] = jnp.zeros_like(token)

    outs = pl.pallas_call(
        body, name=name,
        out_shape=(pltpu.SemaphoreType.DMA((n * NPEER,)), pltpu.SemaphoreType.DMA((n * NPEER,)))
        + tuple(pltpu.HBM(a.shape, a.dtype) for a in srcs) + tuple(pltpu.HBM(a.shape, a.dtype) for a in lands)
        + (jax.ShapeDtypeStruct((8, 128), F32),),
        in_specs=(HBM,) * (ns + n) + (ANY,) * nd,
        out_specs=(SEMS, SEMS) + (HBM,) * (ns + n) + (pl.BlockSpec(memory_space=pltpu.VMEM),),
        input_output_aliases={i: 2 + i for i in range(ns + n)},
        compiler_params=pltpu.CompilerParams(has_side_effects=EFFECT),
    )(*[_hbm(a) for a in srcs], *[_hbm(a) for a in lands], *deps)
    return outs[0], outs[1], list(outs[2:2 + ns]), list(outs[2 + ns:2 + ns + n]), outs[-1]


def _split_wait(srcs, src_of, lands, send, recv, sem_rows, wait_refs, after, name):
    ns, n = len(srcs), len(lands)

    def body(*refs):
        ins, land = refs[:ns], refs[ns:ns + n]
        send_ref, recv_ref = refs[ns + n], refs[ns + n + 1]
        x, y, c, me = _mesh_pos()
        for j in range(n):
            for k in range(1, NDEV):
                peer, _ = _peer(x, y, c, k)
                src, dst = wait_refs(j, (land[j] if src_of[j] is None else ins[src_of[j]]), land[j])
                sem = sem_rows[j] * NPEER + k - 1
                cp = _remote(src, dst, send_ref.at[sem], recv_ref.at[sem], peer)
                cp.wait_send()
                cp.wait_recv()

    outs = pl.pallas_call(
        body, name=name,
        out_shape=tuple(pltpu.HBM(a.shape, a.dtype) for a in srcs) + tuple(pltpu.HBM(a.shape, a.dtype) for a in lands),
        in_specs=(HBM,) * (ns + n) + (SEMS, SEMS, ANY),
        out_specs=(HBM,) * (ns + n),
        input_output_aliases={i: i for i in range(ns + n)},
        compiler_params=pltpu.CompilerParams(has_side_effects=EFFECT),
    )(*srcs, *lands, send, recv, after)
    return list(outs[:ns]), list(outs[ns:])


class _Comm:
    def __init__(self, shards, me, deps=()):
        names = AG_ORDER
        rows = [SEC_ROWS[n] for n in names]
        self.me = me
        lands = [lax.dynamic_update_slice(lax.empty((NDEV * r, D), BF16), shards[n], (_shard_pos(n, me), 0))
                 for n, r in zip(names, rows)]

        def copy_refs(j, src, land, me, pid):
            own = land.at[pl.ds(pl.multiple_of(_shard_pos(names[j], me), 16), rows[j])]
            return own, own

        self.ag_send, self.ag_recv, _, lands, self.ag_token = _split_start(
            [], [None] * len(names), lands, copy_refs, "ag_start", deps=deps)
        self.ag_land = dict(zip(names, lands))
        self.rs = []

    def weights(self, group, after):
        names = AG_GROUPS[group]
        idx = [AG_ORDER.index(n) for n in names]
        rows = [SEC_ROWS[n] for n in names]

        def wait_refs(j, src, land):
            return land.at[pl.ds(0, rows[j])], land.at[pl.ds(0, rows[j])]

        _, lands = _split_wait([], [None] * len(names), [self.ag_land[n] for n in names], self.ag_send,
                               self.ag_recv, idx, wait_refs, after, f"ag_wait{group}")
        return dict(zip(names, lands))

    def send_grads(self, group, gws):
        names = RS_GROUPS[group]
        rows = [SEC_ROWS[n] for n in names]
        grads = [gws[n] for n in names]
        me = self.me
        lands = [lax.dynamic_update_slice(
            lax.empty((NDEV, r, D), BF16),
            lax.dynamic_slice(g, (_shard_pos(n, me), 0), (r, D))[None], (me, 0, 0))
            for n, r, g in zip(names, rows, grads)]

        def copy_refs(j, src, land, me, pid):
            return src.at[pl.ds(pl.multiple_of(_shard_pos(names[j], pid), 16), rows[j])], land.at[me]

        send, recv, srcs, lands, token = _split_start(grads, list(range(len(names))), lands, copy_refs,
                                                      f"rs_start{group}")
        self.rs.append((names, rows, send, recv, srcs, lands))
        return token

    def received(self, group, after):
        names, rows, send, recv, srcs, lands = self.rs[group]

        def wait_refs(j, src, land):
            return src.at[pl.ds(0, rows[j])], land.at[0]

        _, lands = _split_wait(srcs, list(range(len(names))), lands, send, recv, list(range(len(names))), wait_refs,
                               after, f"rs_wait{group}")
        return dict(zip(names, lands))


def _adamw(R, w, m, v, *, rows, tr, name, off=0):
    c1 = 1.0 / (1.0 - ADAM_B1 ** ADAM_STEP)
    c2 = 1.0 / (1.0 - ADAM_B2 ** ADAM_STEP)

    def kern(r_ref, w_ref, m_ref, v_ref, g_out, d_out, m_out, v_out):
        g = r_ref[0].astype(F32)
        for dev in range(1, NDEV):
            g = g + r_ref[dev].astype(F32)
        mn = ADAM_B1 * m_ref[...] + (1.0 - ADAM_B1) * g
        vn = ADAM_B2 * v_ref[...] + (1.0 - ADAM_B2) * (g * g)
        g_out[...] = g
        m_out[...] = mn
        v_out[...] = vn
        d_out[...] = -ADAM_LR * ((mn * c1) / (jnp.sqrt(vn * c2) + ADAM_EPS) + ADAM_WD * w_ref[...])

    tile = pl.BlockSpec((tr, D), lambda i: (i, 0))
    wtile = pl.BlockSpec((tr, D), lambda i: (i + off, 0))
    shp = jax.ShapeDtypeStruct((rows, D), F32)
    return pl.pallas_call(
        kern, grid=(rows // tr,),
        in_specs=[pl.BlockSpec((NDEV, tr, D), lambda i: (0, i, 0)), wtile, wtile, wtile],
        out_specs=[tile] * 4, out_shape=[shp] * 4, compiler_params=_cparams(1), name=name)(R, w, m, v)


ADAM_TILE = {"qkv": 384, "wo": 128, "wpi": 128, "gu0": 352, "gu1": 352, "d0": 352, "d1": 352, "pg": 32}


def _pack_sections(w_qkv, w_attn_out, w_pool_in, w_pool_group, w_ffn_gate_up, w_ffn_down):
    pg = w_pool_group[0].transpose(1, 0, 2).reshape(SEC_ROWS["pg"], D)
    return {"qkv": w_qkv[0].T, "wo": w_attn_out[0], "wpi": w_pool_in[0], "gu0": w_ffn_gate_up[0].T,
            "gu1": w_ffn_gate_up[1].T, "d0": w_ffn_down[0], "d1": w_ffn_down[1], "pg": pg}


def _unpack(p):
    def sec(n):
        return p[n]
    w_qkv = sec("qkv").T[None]
    w_attn_out = sec("wo")[None]
    w_pool_in = sec("wpi")[None]
    w_pool_group = sec("pg").reshape(SEC_ROWS["pg"], POOL_G, PGD).transpose(1, 0, 2)[None]
    w_gu = jnp.stack([sec("gu0").T, sec("gu1").T])
    w_d = jnp.stack([sec("d0"), sec("d1")])
    return w_qkv, w_attn_out, w_pool_in, w_pool_group, w_gu, w_d


def _vec_pack(attn_norm, ffn_norm, final_norm, pool_norm_sh, pool_scale_sh, me):
    def place(sh):
        return lax.dynamic_update_slice(jnp.zeros((1, D), F32), sh, (0, me * 128))
    return jnp.concatenate([attn_norm, ffn_norm, final_norm.reshape(1, D), place(pool_norm_sh),
                            place(pool_scale_sh), jnp.zeros((2, D), F32)], axis=0)


def _vec_unpack(p, me):
    def take(r):
        return lax.dynamic_slice(p[r:r + 1], (0, me * 128), (1, 128))
    return p[0:1], p[1:3], p[3], take(4), take(5)


def kernel(x, attn_norm, w_qkv, w_attn_out, pool_norm, w_pool_in, w_pool_group, pool_scale, ffn_norm, w_ffn_gate_up, w_ffn_down, final_norm, loss_target, m_attn_norm, m_w_qkv, m_w_attn_out, m_pool_norm, m_w_pool_in, m_w_pool_group, m_pool_scale, m_ffn_norm, m_w_ffn_gate_up, m_w_ffn_down, m_final_norm, v_attn_norm, v_w_qkv, v_w_attn_out, v_pool_norm, v_w_pool_in, v_w_pool_group, v_pool_scale, v_ffn_norm, v_w_ffn_gate_up, v_w_ffn_down, v_final_norm):
    me = 4 * lax.axis_index("x") + 2 * lax.axis_index("y") + lax.axis_index("c")

    pw = _pack_sections(w_qkv, w_attn_out, w_pool_in, w_pool_group, w_ffn_gate_up, w_ffn_down)
    pm = _pack_sections(m_w_qkv, m_w_attn_out, m_w_pool_in, m_w_pool_group, m_w_ffn_gate_up, m_w_ffn_down)
    pv = _pack_sections(v_w_qkv, v_w_attn_out, v_w_pool_in, v_w_pool_group, v_w_ffn_gate_up, v_w_ffn_down)
    vsh = jnp.concatenate([pool_norm, pool_scale, jnp.zeros((6, 128), F32)], axis=0)

    vg = _bcast_all(vsh, "gather_pool_vectors")
    comm = _Comm({n: pw[n].astype(BF16) for n, _ in SECTIONS}, me, deps=(vg,))
    pool_norm_full = vg[:, 0, :].reshape(1, D)
    pool_scale_full = vg[:, 1, :].reshape(1, D)

    grad_x, vec = _local_step(x[0], loss_target[0], comm, attn_norm, ffn_norm, final_norm,
                              pool_norm_full, pool_scale_full)

    vw = _vec_pack(attn_norm, ffn_norm, final_norm, pool_norm, pool_scale, me)
    vm = _vec_pack(m_attn_norm, m_ffn_norm, m_final_norm, m_pool_norm, m_pool_scale, me)
    vv = _vec_pack(v_attn_norm, v_ffn_norm, v_final_norm, v_pool_norm, v_pool_scale, me)

    sec_out = [{}, {}, {}, {}]
    vec_out = None
    after = grad_x
    for group in range(len(RS_GROUPS)):
        if group == len(RS_GROUPS) - 1:
            VR = _bcast_all(vec, "exchange_vector_grads", deps=(after,))
            vec_out = _adamw(VR, vw, vm, vv, rows=8, tr=8, name="adamw_vec")
            after = vec_out[0]
        for n, R in comm.received(group, after).items():
            tr = ADAM_TILE[n]
            res = _adamw(R, pw[n], pm[n], pv[n], rows=SEC_ROWS[n], tr=tr, name=f"adamw_{n}")
            for kind in range(4):
                sec_out[kind][n] = res[kind]
            after = res[0]

    outs = []
    for kind in range(4):
        q, o, pi, pg, gu, dn = _unpack(sec_out[kind])
        an, fn, fin, pn, ps = _vec_unpack(vec_out[kind], me)
        outs.append((an, q, o, pn, pi, pg, ps, fn, gu, dn, fin))
    loss = 0.5 * jnp.sum(vec_out[0][6]) / D
    return (loss, grad_x[None]) + outs[0] + outs[1] + outs[2] + outs[3]
```

```python
import jax
import jax.numpy as jnp
from jax import lax
from jax.experimental import pallas as pl
from jax.experimental.pallas import tpu as pltpu

F32 = jnp.float32
BF16 = jnp.bfloat16

D = 1024
NDEV = 8
HEADS = 8
HD = 128
QB = 128
NGROUPS = 3
DILS = (1, 4, 16)
DFF = 2816
HCH = 1408
POOL_G = 4
PGD = 256
RMS_EPS = 1e-6
NEG = -1e30

ADAM_LR = 0.001
ADAM_B1 = 0.9
ADAM_B2 = 0.999
ADAM_EPS = 1e-08
ADAM_WD = 0.01
ADAM_STEP = 10

VMEM_LIMIT = 52 * 1024 * 1024

SECTIONS = (("qkv", 1152), ("wo", 128), ("wpi", 128), ("gu0", 704), ("gu1", 704),
            ("d0", 352), ("d1", 352), ("pg", 32))
LOC_OFF = {}
GLB_OFF = {}
_o = 0
for _n, _r in SECTIONS:
    LOC_OFF[_n] = _o
    GLB_OFF[_n] = _o * NDEV
    _o += _r
PACK_ROWS = _o
GLB_ROWS = PACK_ROWS * NDEV
SEC_ROWS = dict(SECTIONS)


def _cparams(n_grid):
    return pltpu.CompilerParams(dimension_semantics=("arbitrary",) * n_grid, vmem_limit_bytes=VMEM_LIMIT)


def _shard_pos(name, dev):
    n = SEC_ROWS[name]
    if name in ("gu0", "gu1"):
        return ((dev % 4) // 2) * (2 * HCH) + (dev // 4) * HCH + (dev % 2) * n
    return dev * n


def _mm(a, b, *, mode, M, N, K, tm, tn, tk, out_dtype, name, a_off=(0, 0), b_off=(0, 0), res=None,
        out_rows=None, out_off=0, out_prev=None, deps=()):
    nm, nn, nk = M // tm, N // tn, K // tk
    assert nm * tm == M and nn * tn == N and nk * tk == K
    if mode == "nn":
        a_bs, b_bs = (tm, tk), (tk, tn)
        a_ix = lambda i, j, k: (i, k)
        b_ix = lambda i, j, k: (k, j)
        dims = (((1,), (0,)), ((), ()))
    elif mode == "nt":
        a_bs, b_bs = (tm, tk), (tn, tk)
        a_ix = lambda i, j, k: (i, k)
        b_ix = lambda i, j, k: (j, k)
        dims = (((1,), (1,)), ((), ()))
    else:
        a_bs, b_bs = (tk, tm), (tk, tn)
        a_ix = lambda i, j, k: (k, i)
        b_ix = lambda i, j, k: (k, j)
        dims = (((0,), (0,)), ((), ()))

    def spec(bs, ix, off):
        def im(i, j, k):
            r, c = ix(i, j, k)
            return (r + off[0], c + off[1])
        return pl.BlockSpec(bs, im)

    in_specs = [spec(a_bs, a_ix, a_off), spec(b_bs, b_ix, b_off)]
    args = [a, b]
    if res is not None:
        in_specs.append(pl.BlockSpec((tm, tn), lambda i, j, k: (i, j)))
        args.append(res)
    out_shape = jax.ShapeDtypeStruct((M if out_rows is None else out_rows, N), out_dtype)
    out_spec = pl.BlockSpec((tm, tn), lambda i, j, k: (i + out_off, j))
    has_res = res is not None
    extra = list(deps) + ([out_prev] if out_prev is not None else [])
    for dep in extra:
        in_specs.append(pl.BlockSpec(memory_space=pl.ANY))
        args.append(dep)
    o_pos = 2 + int(has_res) + len(extra)
    aliases = {len(args) - 1: 0} if out_prev is not None else {}

    def kern(*refs):
        a_ref, b_ref = refs[0], refs[1]
        res_ref = refs[2] if has_res else None
        o_ref = refs[o_pos]
        av = a_ref[...]
        bv = b_ref[...]
        if av.dtype != BF16:
            av = av.astype(BF16)
        if bv.dtype != BF16:
            bv = bv.astype(BF16)
        part = lax.dot_general(av, bv, dims, preferred_element_type=F32)

        def write(val):
            if has_res:
                val = val + res_ref[...]
            o_ref[...] = val.astype(out_dtype)

        if nk == 1:
            write(part)
        else:
            acc_ref = refs[-1]
            k = pl.program_id(2)

            @pl.when(k == 0)
            def _():
                acc_ref[...] = part

            @pl.when(k > 0)
            def _():
                acc_ref[...] += part

            @pl.when(k == nk - 1)
            def _():
                write(acc_ref[...])

    scratch = [pltpu.VMEM((tm, tn), F32)] if nk > 1 else []
    return pl.pallas_call(
        kern, grid=(nm, nn, nk), in_specs=in_specs, out_specs=out_spec, out_shape=out_shape,
        scratch_shapes=scratch, input_output_aliases=aliases, compiler_params=_cparams(3), name=name)(*args)


def _rms_fwd(x, g, name, deps=()):
    S = x.shape[0]
    tr = 512

    def kern(x_ref, g_ref, *rest):
        h_ref = rest[-1]
        xv = x_ref[...]
        r = lax.rsqrt(jnp.mean(xv * xv, axis=-1, keepdims=True) + RMS_EPS)
        h_ref[...] = (xv * r * g_ref[...]).astype(BF16)

    return pl.pallas_call(
        kern, grid=(S // tr,),
        in_specs=[pl.BlockSpec((tr, D), lambda i: (i, 0)), pl.BlockSpec((1, D), lambda i: (0, 0))]
        + [pl.BlockSpec(memory_space=pl.ANY)] * len(deps),
        out_specs=pl.BlockSpec((tr, D), lambda i: (i, 0)),
        out_shape=jax.ShapeDtypeStruct((S, D), BF16), compiler_params=_cparams(1), name=name)(x, g, *deps)


def _chunks_put(scr, val):
    for c in range(scr.shape[0]):
        scr[c] = val[:, c * 128:(c + 1) * 128]


def _chunks_get(scr):
    return jnp.concatenate([scr[c] for c in range(scr.shape[0])], axis=1)


def _chunks_rows(scr, r, n, dil):
    return jnp.concatenate([scr.at[c][pl.ds(r, n, stride=dil), :] for c in range(scr.shape[0])], axis=1)


def _chunks_add_rows(scr, val, r, n, dil, accumulate):
    for c in range(scr.shape[0]):
        rows = pl.ds(r, n, stride=dil)
        piece = val[:, c * 128:(c + 1) * 128]
        tile = scr.at[c]
        tile[rows, :] = tile[rows, :] + piece if accumulate else piece


def _rms_fwd_folded(x, g, name, deps=()):
    S = x.shape[0]
    tr = 512
    dils = DILS[1:]

    def kern(x_ref, g_ref, *rest):
        outs, scr = rest[len(deps):-1], rest[-1]
        xv = x_ref[...]
        r = lax.rsqrt(jnp.mean(xv * xv, axis=-1, keepdims=True) + RMS_EPS)
        h = (xv * r * g_ref[...]).astype(BF16)
        outs[0][...] = h
        _chunks_put(scr, h.astype(F32))
        for o_ref, dil in zip(outs[1:], dils):
            for res in range(dil):
                o_ref[res] = _chunks_rows(scr, res, tr // dil, dil).astype(BF16)

    return pl.pallas_call(
        kern, grid=(S // tr,),
        in_specs=[pl.BlockSpec((tr, D), lambda i: (i, 0)), pl.BlockSpec((1, D), lambda i: (0, 0))]
        + [pl.BlockSpec(memory_space=pl.ANY)] * len(deps),
        out_specs=[pl.BlockSpec((tr, D), lambda i: (i, 0))]
        + [pl.BlockSpec((dil, tr // dil, D), lambda i: (0, i, 0)) for dil in dils],
        out_shape=[jax.ShapeDtypeStruct((S, D), BF16)]
        + [jax.ShapeDtypeStruct((dil, S // dil, D), BF16) for dil in dils],
        scratch_shapes=[pltpu.VMEM((D // 128, tr, 128), F32)],
        compiler_params=_cparams(1), name=name)(x, g, *deps)


def _rms_bwd(dh, x, g, dres, name, folded=()):
    S = x.shape[0]
    tr = 512
    nf = len(folded)

    def kern(dh_ref, *rest):
        f_refs = rest[:nf]
        x_ref, g_ref, dres_ref, dx_ref, dg_ref = rest[nf:nf + 5]
        i = pl.program_id(0)
        xv = x_ref[...]
        if nf:
            acc_ref = rest[nf + 5]
            _chunks_put(acc_ref, dh_ref[...].astype(F32))
            for f_ref in f_refs:
                dil = f_ref.shape[0]
                for res in range(dil):
                    _chunks_add_rows(acc_ref, f_ref[res], res, tr // dil, dil, True)
            dhv = _chunks_get(acc_ref)
        else:
            dhv = dh_ref[...].astype(F32)
        r = lax.rsqrt(jnp.mean(xv * xv, axis=-1, keepdims=True) + RMS_EPS)
        xhat = xv * r
        gy = dhv * g_ref[...]
        dx_ref[...] = dres_ref[...] + r * (gy - xhat * jnp.mean(gy * xhat, axis=-1, keepdims=True))
        part = jnp.sum(dhv * xhat, axis=0, keepdims=True)

        @pl.when(i == 0)
        def _():
            dg_ref[...] = part

        @pl.when(i > 0)
        def _():
            dg_ref[...] += part

    row = pl.BlockSpec((tr, D), lambda i: (i, 0))
    vec = pl.BlockSpec((1, D), lambda i: (0, 0))
    fspecs = [pl.BlockSpec((f.shape[0], tr // f.shape[0], D), lambda i: (0, i, 0)) for f in folded]
    return pl.pallas_call(
        kern, grid=(S // tr,), in_specs=[row] + fspecs + [row, vec, row], out_specs=[row, vec],
        out_shape=[jax.ShapeDtypeStruct((S, D), F32), jax.ShapeDtypeStruct((1, D), F32)],
        scratch_shapes=[pltpu.VMEM((D // 128, tr, 128), F32)] if nf else [],
        compiler_params=_cparams(1), name=name)(dh, *folded, x, g, dres)


def _loss_head(x, g, tgt):
    S = x.shape[0]
    tr = 512

    def kern(x_ref, g_ref, t_ref, dx_ref, dg_ref, ls_ref):
        i = pl.program_id(0)
        xv = x_ref[...]
        gv = g_ref[...]
        r = lax.rsqrt(jnp.mean(xv * xv, axis=-1, keepdims=True) + RMS_EPS)
        xhat = xv * r
        e = xhat * gv - t_ref[...]
        dy = e * (1.0 / D)
        gy = dy * gv
        dx_ref[...] = r * (gy - xhat * jnp.mean(gy * xhat, axis=-1, keepdims=True))
        dgp = jnp.sum(dy * xhat, axis=0, keepdims=True)
        lsp = jnp.sum(e * e, axis=0, keepdims=True)

        @pl.when(i == 0)
        def _():
            dg_ref[...] = dgp
            ls_ref[...] = lsp

        @pl.when(i > 0)
        def _():
            dg_ref[...] += dgp
            ls_ref[...] += lsp

    row = pl.BlockSpec((tr, D), lambda i: (i, 0))
    vec = pl.BlockSpec((1, D), lambda i: (0, 0))
    return pl.pallas_call(
        kern, grid=(S // tr,), in_specs=[row, vec, row], out_specs=[row, vec, vec],
        out_shape=[jax.ShapeDtypeStruct((S, D), F32), jax.ShapeDtypeStruct((1, D), F32),
                   jax.ShapeDtypeStruct((1, D), F32)],
        compiler_params=_cparams(1), name="loss_head")(x, g, tgt)


def _ffn_up(h, G, name):
    S = h.shape[0]
    tm = 512
    nj = DFF // HCH

    def kern(h_ref, w_ref, gu_ref, act_ref):
        gu = lax.dot_general(h_ref[...], w_ref[...], (((1,), (1,)), ((), ())), preferred_element_type=F32)
        gu_ref[...] = gu.astype(BF16)
        gate = gu[:, :HCH]
        up = gu[:, HCH:]
        act_ref[...] = (gate * jax.nn.sigmoid(gate) * up).astype(BF16)

    return pl.pallas_call(
        kern, grid=(nj, S // tm),
        in_specs=[pl.BlockSpec((tm, D), lambda j, i: (i, 0)),
                  pl.BlockSpec((2 * HCH, D), lambda j, i: (j, 0))],
        out_specs=[pl.BlockSpec((tm, 2 * HCH), lambda j, i: (i, j)),
                   pl.BlockSpec((tm, HCH), lambda j, i: (i, j))],
        out_shape=[jax.ShapeDtypeStruct((S, 2 * DFF), BF16), jax.ShapeDtypeStruct((S, DFF), BF16)],
        compiler_params=_cparams(2), name=name)(h, G)


def _ffn_down_bwd(dx, G, gu, name):
    S = dx.shape[0]
    tm = 512
    nj = DFF // HCH

    def kern(dx_ref, w_ref, gu_ref, o_ref):
        dact = lax.dot_general(dx_ref[...].astype(BF16), w_ref[...], (((1,), (1,)), ((), ())),
                               preferred_element_type=F32)
        gate = gu_ref[:, :HCH].astype(F32)
        up = gu_ref[:, HCH:].astype(F32)
        sig = jax.nn.sigmoid(gate)
        silu = gate * sig
        o_ref[:, :HCH] = (dact * up * (sig * (1.0 + gate * (1.0 - sig)))).astype(BF16)
        o_ref[:, HCH:] = (dact * silu).astype(BF16)

    return pl.pallas_call(
        kern, grid=(nj, S // tm),
        in_specs=[pl.BlockSpec((tm, D), lambda j, i: (i, 0)),
                  pl.BlockSpec((HCH, D), lambda j, i: (j, 0)),
                  pl.BlockSpec((tm, 2 * HCH), lambda j, i: (i, j))],
        out_specs=pl.BlockSpec((tm, 2 * HCH), lambda j, i: (i, j)),
        out_shape=jax.ShapeDtypeStruct((S, 2 * DFF), BF16),
        compiler_params=_cparams(2), name=name)(dx, G, gu)


def _trail(u, *, backward, name):
    S = u.shape[0]

    def kern(u_ref, o_ref):
        g = pl.program_id(0)
        uv = u_ref[...].astype(F32)
        row = lax.broadcasted_iota(jnp.int32, uv.shape, 0)
        win = jnp.left_shift(jnp.int32(2), g)
        cnt = jnp.minimum(row + 1, win).astype(F32)
        s = uv / cnt if backward else uv
        levels = []
        for k in (1, 2, 4, 8):
            if backward:
                sh = jnp.where(row < S - k, pltpu.roll(s, S - k, 0), 0.0)
            else:
                sh = jnp.where(row >= k, pltpu.roll(s, k, 0), 0.0)
            s = s + sh
            levels.append(s)
        sel = jnp.where(g == 0, levels[0], jnp.where(g == 1, levels[1], jnp.where(g == 2, levels[2], levels[3])))
        if backward:
            o_ref[...] = (sel - uv).astype(BF16)
        else:
            o_ref[...] = (sel / cnt - uv).astype(BF16)

    blk = pl.BlockSpec((S, PGD), lambda g: (0, g))
    return pl.pallas_call(
        kern, grid=(POOL_G,), in_specs=[blk], out_specs=blk,
        out_shape=jax.ShapeDtypeStruct((S, D), BF16), compiler_params=_cparams(1), name=name)(u)


def _pool_out(yd, G, scale, xres):
    S = yd.shape[0]
    tm = 1024

    def kern(y_ref, w_ref, s_ref, x_ref, o_ref):
        z = jnp.dot(y_ref[...], w_ref[...], preferred_element_type=F32)
        o_ref[...] = x_ref[...] + z * s_ref[...]

    tile = pl.BlockSpec((tm, PGD), lambda i, g: (i, g))
    return pl.pallas_call(
        kern, grid=(S // tm, POOL_G),
        in_specs=[tile, pl.BlockSpec((PGD, PGD), lambda i, g: (0, g)),
                  pl.BlockSpec((1, PGD), lambda i, g: (0, g)), tile],
        out_specs=tile, out_shape=jax.ShapeDtypeStruct((S, D), F32),
        compiler_params=_cparams(2), name="pool_out")(yd, G, scale, xres)


def _pool_out_bwd(dz, yd, G, scale):
    S = yd.shape[0]
    tm = 1024
    ni = S // tm

    def kern(dz_ref, y_ref, w_ref, s_ref, dy_ref, ds_ref, dw_ref, acc_ref):
        i = pl.program_id(1)
        dzv = dz_ref[...]
        yv = y_ref[...]
        wv = w_ref[...]
        zraw = jnp.dot(yv, wv, preferred_element_type=F32)
        dsp = jnp.sum(dzv * zraw, axis=0, keepdims=True)
        dzr = (dzv * s_ref[...]).astype(BF16)
        dy_ref[...] = lax.dot_general(dzr, wv, (((1,), (1,)), ((), ())), preferred_element_type=F32)
        dwp = lax.dot_general(yv, dzr, (((0,), (0,)), ((), ())), preferred_element_type=F32)

        @pl.when(i == 0)
        def _():
            ds_ref[...] = dsp
            acc_ref[...] = dwp

        @pl.when(i > 0)
        def _():
            ds_ref[...] += dsp
            acc_ref[...] += dwp

        @pl.when(i == ni - 1)
        def _():
            dw_ref[...] = acc_ref[...].astype(BF16)

    tile = pl.BlockSpec((tm, PGD), lambda g, i: (i, g))
    return pl.pallas_call(
        kern, grid=(POOL_G, ni),
        in_specs=[tile, tile, pl.BlockSpec((PGD, PGD), lambda g, i: (0, g)),
                  pl.BlockSpec((1, PGD), lambda g, i: (0, g))],
        out_specs=[tile, pl.BlockSpec((1, PGD), lambda g, i: (0, g)),
                   pl.BlockSpec((PGD, PGD), lambda g, i: (0, g))],
        out_shape=[jax.ShapeDtypeStruct((S, D), F32), jax.ShapeDtypeStruct((1, D), F32),
                   jax.ShapeDtypeStruct((PGD, D), BF16)],
        scratch_shapes=[pltpu.VMEM((PGD, PGD), F32)],
        compiler_params=_cparams(2), name="pool_out_bwd")(dz, yd, G, scale)


def _bias_table():
    qi = jnp.arange(QB)[:, None]
    ki = jnp.arange(2 * QB)[None, :]
    delta = QB + qi - ki
    inband = (delta >= 0) & (delta <= QB)
    n = NGROUPS * HEADS
    slopes = jnp.exp2(-8.0 * jnp.arange(1, n + 1, dtype=F32) / n).reshape(NGROUPS, HEADS)
    dil = jnp.asarray(DILS, F32)
    bias = -slopes[:, :, None, None] * (delta.astype(F32)[None, None] * dil[:, None, None, None])
    return jnp.where(inband[None, None], bias, NEG)


def _attn_fwd(qkv_f, bias, nb, name):
    S = qkv_f.shape[0]
    nblk = S // QB
    scale = HD ** -0.5

    def kern(q_ref, kc_ref, kp_ref, vc_ref, vp_ref, b_ref, o_ref, l_ref):
        b = pl.program_id(0)
        has_prev = jnp.bitwise_and(b, nb - 1) != 0
        col = lax.broadcasted_iota(jnp.int32, (QB, 2 * QB), 1)
        dead = jnp.logical_and(col < QB, jnp.logical_not(has_prev))
        lane = lax.broadcasted_iota(jnp.int32, (QB, HD), 1)
        lse_all = jnp.zeros((QB, HD), F32)
        for h in range(HEADS):
            sl = slice(h * HD, (h + 1) * HD)
            kk = jnp.concatenate([kp_ref[:, sl], kc_ref[:, sl]], axis=0)
            vv = jnp.concatenate([vp_ref[:, sl], vc_ref[:, sl]], axis=0)
            s = lax.dot_general(q_ref[:, sl], kk, (((1,), (1,)), ((), ())), preferred_element_type=F32)
            s = s * scale + b_ref[h]
            s = jnp.where(dead, NEG, s)
            m = jnp.max(s, axis=-1, keepdims=True)
            p = jnp.exp(s - m)
            den = jnp.sum(p, axis=-1, keepdims=True)
            o = jnp.dot(p.astype(BF16), vv, preferred_element_type=F32) / den
            o_ref[:, sl] = o.astype(BF16)
            lse_all = jnp.where(lane == h, m + jnp.log(den), lse_all)
        l_ref[...] = lse_all

    def blk(colblk, prev):
        if prev:
            return pl.BlockSpec((QB, D), lambda b: (jnp.maximum(b - 1, 0), colblk))
        return pl.BlockSpec((QB, D), lambda b: (b, colblk))

    return pl.pallas_call(
        kern, grid=(nblk,),
        in_specs=[blk(0, False), blk(1, False), blk(1, True), blk(2, False), blk(2, True),
                  pl.BlockSpec((HEADS, QB, 2 * QB), lambda b: (0, 0, 0))],
        out_specs=[pl.BlockSpec((QB, D), lambda b: (b, 0)), pl.BlockSpec((QB, HD), lambda b: (b, 0))],
        out_shape=[jax.ShapeDtypeStruct((S, D), BF16), jax.ShapeDtypeStruct((S, HD), F32)],
        compiler_params=_cparams(1), name=name)(qkv_f, qkv_f, qkv_f, qkv_f, qkv_f, bias)


def _natural(ref, scr, tm):
    dil = ref.shape[0]
    for res in range(dil):
        _chunks_add_rows(scr, ref[res].astype(F32), res, tm // dil, dil, False)
    return _chunks_get(scr)


def _attn_merge(os, lses):
    S = os[0].shape[0]
    tm = 512

    def kern(o0, o1, o2, l0, l1, l2, om_ref, lm_ref, ls1, ls2, os1, os2):
        la = l0[...]
        lb = _natural(l1, ls1, tm)
        lc = _natural(l2, ls2, tm)
        m = jnp.maximum(jnp.maximum(la, lb), lc)
        e0, e1, e2 = jnp.exp(la - m), jnp.exp(lb - m), jnp.exp(lc - m)
        tot = e0 + e1 + e2
        lm_ref[...] = m + jnp.log(tot)
        w0, w1, w2 = e0 / tot, e1 / tot, e2 / tot
        for res in range(o1.shape[0]):
            _chunks_add_rows(os1, o1[res].astype(F32), res, tm // o1.shape[0], o1.shape[0], False)
        for res in range(o2.shape[0]):
            _chunks_add_rows(os2, o2[res].astype(F32), res, tm // o2.shape[0], o2.shape[0], False)
        for h in range(HEADS):
            sl = slice(h * HD, (h + 1) * HD)
            acc = w0[:, h:h + 1] * o0[:, sl].astype(F32) + w1[:, h:h + 1] * os1[h] + w2[:, h:h + 1] * os2[h]
            om_ref[:, sl] = acc.astype(BF16)

    def spec(a, c):
        if a.ndim == 2:
            return pl.BlockSpec((tm, c), lambda i: (i, 0))
        return pl.BlockSpec((a.shape[0], tm // a.shape[0], c), lambda i: (0, i, 0))

    return pl.pallas_call(
        kern, grid=(S // tm,),
        in_specs=[spec(a, D) for a in os] + [spec(a, HD) for a in lses],
        out_specs=[pl.BlockSpec((tm, D), lambda i: (i, 0)), pl.BlockSpec((tm, HD), lambda i: (i, 0))],
        out_shape=[jax.ShapeDtypeStruct((S, D), BF16), jax.ShapeDtypeStruct((S, HD), F32)],
        scratch_shapes=[pltpu.VMEM((1, tm, HD), F32), pltpu.VMEM((1, tm, HD), F32),
                        pltpu.VMEM((HEADS, tm, HD), F32), pltpu.VMEM((HEADS, tm, HD), F32)],
        compiler_params=_cparams(1), name="attn_merge")(*os, *lses)


def _attn_bwd_prep(do, o, lse):
    S = o.shape[0]
    tm = 512
    dils = DILS[1:]

    def kern(do_ref, o_ref, l_ref, *rest):
        do_outs, l_outs, d_outs = rest[0:3], rest[3:5], rest[5:8]
        do_scr, l_scr, d_scr = rest[8:11]
        lane = lax.broadcasted_iota(jnp.int32, (tm, HD), 1)
        acc = jnp.zeros((tm, HD), F32)
        for h in range(HEADS):
            sl = slice(h * HD, (h + 1) * HD)
            prod = do_ref[:, sl] * o_ref[:, sl].astype(F32)
            acc = jnp.where(lane == h, jnp.sum(prod, axis=-1, keepdims=True), acc)
        d_scr[0] = acc
        l_scr[0] = l_ref[...]
        _chunks_put(do_scr, do_ref[...])
        do_outs[0][...] = do_ref[...].astype(BF16)
        d_outs[0][...] = acc
        for j, dil in enumerate(dils):
            for res in range(dil):
                n = tm // dil
                do_outs[1 + j][res] = _chunks_rows(do_scr, res, n, dil).astype(BF16)
                l_outs[j][res] = _chunks_rows(l_scr, res, n, dil)
                d_outs[1 + j][res] = _chunks_rows(d_scr, res, n, dil)

    def nat(c):
        return pl.BlockSpec((tm, c), lambda i: (i, 0))

    def fol(dil, c):
        return pl.BlockSpec((dil, tm // dil, c), lambda i: (0, i, 0))

    def shapes(c, dt, with_natural):
        first = [jax.ShapeDtypeStruct((S, c), dt)] if with_natural else []
        return first + [jax.ShapeDtypeStruct((dil, S // dil, c), dt) for dil in dils]

    outs = pl.pallas_call(
        kern, grid=(S // tm,), in_specs=[nat(D), nat(D), nat(HD)],
        out_specs=[nat(D)] + [fol(dil, D) for dil in dils] + [fol(dil, HD) for dil in dils]
        + [nat(HD)] + [fol(dil, HD) for dil in dils],
        out_shape=shapes(D, BF16, True) + shapes(HD, F32, False) + shapes(HD, F32, True),
        scratch_shapes=[pltpu.VMEM((HEADS, tm, HD), F32), pltpu.VMEM((1, tm, HD), F32), pltpu.VMEM((1, tm, HD), F32)],
        compiler_params=_cparams(1), name="attn_bwd_prep")(do, o, lse)
    return outs[0:3], [lse] + list(outs[3:5]), outs[5:8]


def _attn_bwd(qkv_f, do_f, lse_f, delta_f, bias, nb, name):
    S = qkv_f.shape[0]
    nblk = S // QB
    scale = HD ** -0.5

    def kern(q_ref, kc_ref, kp_ref, vc_ref, vp_ref, do_ref, l_ref, d_ref, b_ref, out_ref, dq_c, dk_c, dv_c):
        b = pl.program_id(0)

        @pl.when(b == 0)
        def _():
            dq_c[...] = jnp.zeros_like(dq_c)
            dk_c[...] = jnp.zeros_like(dk_c)
            dv_c[...] = jnp.zeros_like(dv_c)

        @pl.when(b == nblk)
        def _():
            out_ref[:, 0:D] = dq_c[...].astype(BF16)
            out_ref[:, D:2 * D] = dk_c[...].astype(BF16)
            out_ref[:, 2 * D:3 * D] = dv_c[...].astype(BF16)

        @pl.when(b < nblk)
        def _():
            has_prev = jnp.bitwise_and(b, nb - 1) != 0
            col = lax.broadcasted_iota(jnp.int32, (QB, 2 * QB), 1)
            dead = jnp.logical_and(col < QB, jnp.logical_not(has_prev))
            out_ref[:, 0:D] = dq_c[...].astype(BF16)
            lv = l_ref[...]
            dv_ = d_ref[...]
            for h in range(HEADS):
                sl = slice(h * HD, (h + 1) * HD)
                qh = q_ref[:, sl]
                doh = do_ref[:, sl]
                kk = jnp.concatenate([kp_ref[:, sl], kc_ref[:, sl]], axis=0)
                vv = jnp.concatenate([vp_ref[:, sl], vc_ref[:, sl]], axis=0)
                s = lax.dot_general(qh, kk, (((1,), (1,)), ((), ())), preferred_element_type=F32)
                s = s * scale + b_ref[h]
                s = jnp.where(dead, NEG, s)
                p = jnp.exp(s - lv[:, h:h + 1])
                dp = lax.dot_general(doh, vv, (((1,), (1,)), ((), ())), preferred_element_type=F32)
                ds = (p * (dp - dv_[:, h:h + 1]) * scale).astype(BF16)
                pb = p.astype(BF16)
                dq_c[:, sl] = jnp.dot(ds, kk, preferred_element_type=F32)
                dkk = lax.dot_general(ds, qh, (((0,), (0,)), ((), ())), preferred_element_type=F32)
                dvv = lax.dot_general(pb, doh, (((0,), (0,)), ((), ())), preferred_element_type=F32)
                out_ref[:, D + h * HD:D + (h + 1) * HD] = (dk_c[:, sl] + dkk[:QB]).astype(BF16)
                out_ref[:, 2 * D + h * HD:2 * D + (h + 1) * HD] = (dv_c[:, sl] + dvv[:QB]).astype(BF16)
                dk_c[:, sl] = dkk[QB:]
                dv_c[:, sl] = dvv[QB:]

    last = nblk - 1

    def blk(colblk, prev):
        if prev:
            return pl.BlockSpec((QB, D), lambda b: (jnp.maximum(jnp.minimum(b, last) - 1, 0), colblk))
        return pl.BlockSpec((QB, D), lambda b: (jnp.minimum(b, last), colblk))

    stat = pl.BlockSpec((QB, HD), lambda b: (jnp.minimum(b, last), 0))
    return pl.pallas_call(
        kern, grid=(nblk + 1,),
        in_specs=[blk(0, False), blk(1, False), blk(1, True), blk(2, False), blk(2, True),
                  pl.BlockSpec((QB, D), lambda b: (jnp.minimum(b, last), 0)), stat, stat,
                  pl.BlockSpec((HEADS, QB, 2 * QB), lambda b: (0, 0, 0))],
        out_specs=pl.BlockSpec((QB, 3 * D), lambda b: (jnp.maximum(b - 1, 0), 0)),
        out_shape=jax.ShapeDtypeStruct((S, 3 * D), BF16),
        scratch_shapes=[pltpu.VMEM((QB, D), F32), pltpu.VMEM((QB, D), F32), pltpu.VMEM((QB, D), F32)],
        compiler_params=_cparams(1), name=name)(qkv_f, qkv_f, qkv_f, qkv_f, qkv_f, do_f, lse_f, delta_f, bias)


def _local_step(x, tgt, comm, attn_norm, ffn_norm, final_norm, pool_norm, pool_scale):
    S = x.shape[0]
    bias = _bias_table()
    g_attn = attn_norm.reshape(1, D)
    g_f0 = ffn_norm[0:1]
    g_f1 = ffn_norm[1:2]
    g_fin = final_norm.reshape(1, D)
    W = {}

    def ffn_fwd(xin, gain, l):
        h = _rms_fwd(xin, gain, f"rms_ffn{l}")
        gu, act = _ffn_up(h, W[f"gu{l}"], f"ffn_up{l}")
        xo = _mm(act, W[f"d{l}"], mode="nn", M=S, N=D, K=DFF, tm=512, tn=D, tk=DFF, out_dtype=F32,
                 res=xin, name=f"ffn_down{l}")
        return h, gu, act, xo

    def ffn_bwd(dxo, xin, gain, h, gu, act, l, rs_group):
        dgu = _ffn_down_bwd(dxo, W[f"d{l}"], gu, f"ffn_down_bwd{l}")
        gw_d = _mm(act, dxo, mode="tn", M=DFF, N=D, K=S, tm=HCH, tn=D, tk=1024, out_dtype=BF16, name=f"gw_d{l}")
        gw_gu = _mm(dgu, h, mode="tn", M=2 * DFF, N=D, K=S, tm=HCH, tn=D, tk=1024, out_dtype=BF16, name=f"gw_gu{l}")
        token = comm.send_grads(rs_group, {f"d{l}": gw_d, f"gu{l}": gw_gu})
        dh = _mm(dgu, W[f"gu{l}"], mode="nn", M=S, N=D, K=2 * DFF, tm=1024, tn=D, tk=HCH,
                 out_dtype=F32, deps=(token,), name=f"ffn_up_bwd{l}")
        dxin, dgain = _rms_bwd(dh, xin, gain, dxo, f"rms_ffn_bwd{l}")
        return dxin, dgain

    nbs = [S // QB // dil for dil in DILS]
    hf = _rms_fwd_folded(x, g_attn, "rms_attn", deps=(comm.ag_token,))
    hf = [h.reshape(S, D) for h in hf]
    W.update(comm.weights(0, hf[0]))
    qkv_f, o_f, lse_f = [], [], []
    for g, dil in enumerate(DILS):
        qkv_f.append(_mm(hf[g], W["qkv"], mode="nt", M=S, N=3 * D, K=D, tm=1024, tn=1024, tk=D, out_dtype=BF16,
                         b_off=(3 * g, 0), name=f"qkv_proj{g}"))
        og, lg = _attn_fwd(qkv_f[g], bias[g], nbs[g], f"attn_fwd{g}")
        o_f.append(og if dil == 1 else og.reshape(dil, S // dil, D))
        lse_f.append(lg if dil == 1 else lg.reshape(dil, S // dil, HD))
    o, lse = _attn_merge(o_f, lse_f)
    W.update(comm.weights(1, o))
    x1 = _mm(o, W["wo"], mode="nn", M=S, N=D, K=D, tm=1024, tn=D, tk=D, out_dtype=F32, res=x, name="attn_out")
    h1, gu0, act0, x2 = ffn_fwd(x1, g_f0, 0)

    W.update(comm.weights(2, x2))
    h2 = _rms_fwd(x2, pool_norm, "rms_pool")
    u = _mm(h2, W["wpi"], mode="nn", M=S, N=D, K=D, tm=1024, tn=D, tk=D, out_dtype=F32, name="pool_in")
    yd = _trail(u, backward=False, name="trail_fwd")
    x3 = _pool_out(yd, W["pg"], pool_scale, x2)
    h3, gu1, act1, x4 = ffn_fwd(x3, g_f1, 1)

    dx4, d_fin, lossvec = _loss_head(x4, g_fin, tgt)

    dx3, d_f1 = ffn_bwd(dx4, x3, g_f1, h3, gu1, act1, 1, 0)
    dyd, d_scale, gw_pg = _pool_out_bwd(dx3, yd, W["pg"], pool_scale)
    du = _trail(dyd, backward=True, name="trail_bwd")
    gw_pi = _mm(h2, du, mode="tn", M=D, N=D, K=S, tm=D, tn=D, tk=1024, out_dtype=BF16, name="gw_pi")
    token = comm.send_grads(1, {"pg": gw_pg, "wpi": gw_pi})
    dh2 = _mm(du, W["wpi"], mode="nt", M=S, N=D, K=D, tm=1024, tn=D, tk=D, out_dtype=F32,
              deps=(token,), name="pool_in_bwd")
    dx2, d_pool = _rms_bwd(dh2, x2, pool_norm, dx3, "rms_pool_bwd")
    dx1, d_f0 = ffn_bwd(dx2, x1, g_f0, h1, gu0, act0, 0, 2)

    gw_o = _mm(o, dx1, mode="tn", M=D, N=D, K=S, tm=D, tn=D, tk=1024, out_dtype=BF16, name="gw_o")
    do = _mm(dx1, W["wo"], mode="nt", M=S, N=D, K=D, tm=1024, tn=D, tk=D, out_dtype=F32, name="attn_out_bwd")
    do_f, lse_ff, delta_f = _attn_bwd_prep(do, o, lse)
    dqkv_f, gw_qkv = [], None
    for g in range(NGROUPS):
        dqkv_f.append(_attn_bwd(qkv_f[g], do_f[g].reshape(S, D), lse_ff[g].reshape(S, HD),
                                delta_f[g].reshape(S, HD), bias[g], nbs[g], f"attn_bwd{g}"))
        gw_qkv = _mm(dqkv_f[g], hf[g], mode="tn", M=3 * D, N=D, K=S, tm=1024, tn=D, tk=1024, out_dtype=BF16,
                     out_rows=NGROUPS * 3 * D, out_off=3 * g, out_prev=gw_qkv, name=f"gw_qkv{g}")
    token = comm.send_grads(3, {"wo": gw_o, "qkv": gw_qkv})
    dh0_f = [_mm(dqkv_f[g], W["qkv"], mode="nn", M=S, N=D, K=3 * D, tm=1024, tn=D, tk=1024, out_dtype=F32,
                 b_off=(3 * g, 0), deps=(token,), name=f"qkv_proj_bwd{g}") for g in range(NGROUPS)]
    folded = [dh0_f[g].reshape(dil, S // dil, D) for g, dil in enumerate(DILS) if dil > 1]
    grad_x, d_attn = _rms_bwd(dh0_f[0], x, g_attn, dx1, "rms_attn_bwd", folded=folded)

    vec = jnp.concatenate([d_attn, d_f0, d_f1, d_fin, d_pool, d_scale, lossvec, jnp.zeros((1, D), F32)], axis=0)
    return grad_x, vec


def _mesh_pos():
    x, y, c = lax.axis_index("x"), lax.axis_index("y"), lax.axis_index("c")
    return x, y, c, 4 * x + 2 * y + c


def _peer(x, y, c, k):
    kx, ky, kc = (k >> 2) & 1, (k >> 1) & 1, k & 1
    px = 1 - x if kx else x
    py = 1 - y if ky else y
    pc = 1 - c if kc else c
    return (px, py, pc), 4 * px + 2 * py + pc


ANY = pl.BlockSpec(memory_space=pl.ANY)


HBM = pl.BlockSpec(memory_space=pltpu.HBM)
SEMS = pl.BlockSpec(memory_space=pltpu.SEMAPHORE)
EFFECT = pltpu.SideEffectType.DATAFLOW_SIDE_EFFECTING
NPEER = NDEV - 1

AG_GROUPS = (("qkv",), ("wo", "gu0", "d0"), ("wpi", "pg", "gu1", "d1"))
AG_ORDER = tuple(n for grp in AG_GROUPS for n in grp)
RS_GROUPS = (("d1", "gu1"), ("pg", "wpi"), ("d0", "gu0"), ("wo", "qkv"))


def _hbm(a):
    return pltpu.with_memory_space_constraint(a, pltpu.HBM)


def _remote(src, dst, send, recv, peer):
    return pltpu.make_async_remote_copy(src_ref=src, dst_ref=dst, send_sem=send, recv_sem=recv, device_id=peer,
                                        device_id_type=pl.DeviceIdType.MESH)


def _bcast_all(v, name, deps=()):
    W = v.shape[1]
    nd = len(deps)

    def kern(v_ref, *rest):
        o_ref, send, recv, lsem = rest[nd:]
        x, y, c, me = _mesh_pos()
        own = pltpu.make_async_copy(v_ref, o_ref.at[me], lsem)
        own.start()
        cps = [_remote(v_ref, o_ref.at[me], send.at[k - 1], recv.at[k - 1], _peer(x, y, c, k)[0])
               for k in range(1, NDEV)]
        for cp in cps:
            cp.start()
        for cp in cps:
            cp.wait_recv()
            cp.wait_send()
        own.wait()

    return pl.pallas_call(
        kern, in_specs=[ANY] * (1 + nd), out_specs=ANY, out_shape=jax.ShapeDtypeStruct((NDEV, 8, W), F32),
        scratch_shapes=[pltpu.SemaphoreType.DMA((NPEER,)), pltpu.SemaphoreType.DMA((NPEER,)),
                        pltpu.SemaphoreType.DMA(())],
        name=name)(v, *deps)


def _split_start(srcs, src_of, lands, copy_refs, name, deps=()):
    ns, n, nd = len(srcs), len(lands), len(deps)

    def body(*refs):
        ins, land = refs[:ns], refs[ns:ns + n]
        send, recv = refs[ns + n + nd], refs[ns + n + nd + 1]
        token = refs[-1]
        x, y, c, me = _mesh_pos()
        for j in range(n):
            for k in range(1, NDEV):
                peer, pid = _peer(x, y, c, k)
                src, dst = copy_refs(j, (land[j] if src_of[j] is None else ins[src_of[j]]), land[j], me, pid)
                _remote(src, dst, send.at[j * NPEER + k - 1], recv.at[j * NPEER + k - 1], peer).start()
        token[...] = jnp.zeros_like(token)

    outs = pl.pallas_call(
        body, name=name,
        out_shape=(pltpu.SemaphoreType.DMA((n * NPEER,)), pltpu.SemaphoreType.DMA((n * NPEER,)))
        + tuple(pltpu.HBM(a.shape, a.dtype) for a in srcs) + tuple(pltpu.HBM(a.shape, a.dtype) for a in lands)
        + (jax.ShapeDtypeStruct((8, 128), F32),),
        in_specs=(HBM,) * (ns + n) + (ANY,) * nd,
        out_specs=(SEMS, SEMS) + (HBM,) * (ns + n) + (pl.BlockSpec(memory_space=pltpu.VMEM),),
        input_output_aliases={i: 2 + i for i in range(ns + n)},
        compiler_params=pltpu.CompilerParams(has_side_effects=EFFECT),
    )(*[_hbm(a) for a in srcs], *[_hbm(a) for a in lands], *deps)
    return outs[0], outs[1], list(outs[2:2 + ns]), list(outs[2 + ns:2 + ns + n]), outs[-1]


def _split_wait(srcs, src_of, lands, send, recv, sem_rows, wait_refs, after, name):
    ns, n = len(srcs), len(lands)

    def body(*refs):
        ins, land = refs[:ns], refs[ns:ns + n]
        send_ref, recv_ref = refs[ns + n], refs[ns + n + 1]
        x, y, c, me = _mesh_pos()
        for j in range(n):
            for k in range(1, NDEV):
                peer, _ = _peer(x, y, c, k)
                src, dst = wait_refs(j, (land[j] if src_of[j] is None else ins[src_of[j]]), land[j])
                sem = sem_rows[j] * NPEER + k - 1
                cp = _remote(src, dst, send_ref.at[sem], recv_ref.at[sem], peer)
                cp.wait_send()
                cp.wait_recv()

    outs = pl.pallas_call(
        body, name=name,
        out_shape=tuple(pltpu.HBM(a.shape, a.dtype) for a in srcs) + tuple(pltpu.HBM(a.shape, a.dtype) for a in lands),
        in_specs=(HBM,) * (ns + n) + (SEMS, SEMS, ANY),
        out_specs=(HBM,) * (ns + n),
        input_output_aliases={i: i for i in range(ns + n)},
        compiler_params=pltpu.CompilerParams(has_side_effects=EFFECT),
    )(*srcs, *lands, send, recv, after)
    return list(outs[:ns]), list(outs[ns:])


class _Comm:
    def __init__(self, shards, me, deps=()):
        names = AG_ORDER
        rows = [SEC_ROWS[n] for n in names]
        self.me = me
        lands = [lax.dynamic_update_slice(lax.empty((NDEV * r, D), BF16), shards[n], (_shard_pos(n, me), 0))
                 for n, r in zip(names, rows)]

        def copy_refs(j, src, land, me, pid):
            own = land.at[pl.ds(pl.multiple_of(_shard_pos(names[j], me), 16), rows[j])]
            return own, own

        self.ag_send, self.ag_recv, _, lands, self.ag_token = _split_start(
            [], [None] * len(names), lands, copy_refs, "ag_start", deps=deps)
        self.ag_land = dict(zip(names, lands))
        self.rs = []

    def weights(self, group, after):
        names = AG_GROUPS[group]
        idx = [AG_ORDER.index(n) for n in names]
        rows = [SEC_ROWS[n] for n in names]

        def wait_refs(j, src, land):
            return land.at[pl.ds(0, rows[j])], land.at[pl.ds(0, rows[j])]

        _, lands = _split_wait([], [None] * len(names), [self.ag_land[n] for n in names], self.ag_send,
                               self.ag_recv, idx, wait_refs, after, f"ag_wait{group}")
        return dict(zip(names, lands))

    def send_grads(self, group, gws):
        names = RS_GROUPS[group]
        rows = [SEC_ROWS[n] for n in names]
        grads = [gws[n] for n in names]
        me = self.me
        lands = [lax.dynamic_update_slice(
            lax.empty((NDEV, r, D), BF16),
            lax.dynamic_slice(g, (_shard_pos(n, me), 0), (r, D))[None], (me, 0, 0))
            for n, r, g in zip(names, rows, grads)]

        def copy_refs(j, src, land, me, pid):
            return src.at[pl.ds(pl.multiple_of(_shard_pos(names[j], pid), 16), rows[j])], land.at[me]

        send, recv, srcs, lands, token = _split_start(grads, list(range(len(names))), lands, copy_refs,
                                                      f"rs_start{group}")
        self.rs.append((names, rows, send, recv, srcs, lands))
        return token

    def received(self, group, after):
        names, rows, send, recv, srcs, lands = self.rs[group]

        def wait_refs(j, src, land):
            return src.at[pl.ds(0, rows[j])], land.at[0]

        _, lands = _split_wait(srcs, list(range(len(names))), lands, send, recv, list(range(len(names))), wait_refs,
                               after, f"rs_wait{group}")
        return dict(zip(names, lands))


def _adamw(R, w, m, v, *, rows, tr, name, off=0):
    c1 = 1.0 / (1.0 - ADAM_B1 ** ADAM_STEP)
    c2 = 1.0 / (1.0 - ADAM_B2 ** ADAM_STEP)

    def kern(r_ref, w_ref, m_ref, v_ref, g_out, d_out, m_out, v_out):
        g = r_ref[0].astype(F32)
        for dev in range(1, NDEV):
            g = g + r_ref[dev].astype(F32)
        mn = ADAM_B1 * m_ref[...] + (1.0 - ADAM_B1) * g
        vn = ADAM_B2 * v_ref[...] + (1.0 - ADAM_B2) * (g * g)
        g_out[...] = g
        m_out[...] = mn
        v_out[...] = vn
        d_out[...] = -ADAM_LR * ((mn * c1) / (jnp.sqrt(vn * c2) + ADAM_EPS) + ADAM_WD * w_ref[...])

    tile = pl.BlockSpec((tr, D), lambda i: (i, 0))
    wtile = pl.BlockSpec((tr, D), lambda i: (i + off, 0))
    shp = jax.ShapeDtypeStruct((rows, D), F32)
    return pl.pallas_call(
        kern, grid=(rows // tr,),
        in_specs=[pl.BlockSpec((NDEV, tr, D), lambda i: (0, i, 0)), wtile, wtile, wtile],
        out_specs=[tile] * 4, out_shape=[shp] * 4, compiler_params=_cparams(1), name=name)(R, w, m, v)


ADAM_TILE = {"qkv": 384, "wo": 128, "wpi": 128, "gu0": 352, "gu1": 352, "d0": 352, "d1": 352, "pg": 32}


def _pack_sections(w_qkv, w_attn_out, w_pool_in, w_pool_group, w_ffn_gate_up, w_ffn_down):
    pg = w_pool_group[0].transpose(1, 0, 2).reshape(SEC_ROWS["pg"], D)
    return {"qkv": w_qkv[0].T, "wo": w_attn_out[0], "wpi": w_pool_in[0], "gu0": w_ffn_gate_up[0].T,
            "gu1": w_ffn_gate_up[1].T, "d0": w_ffn_down[0], "d1": w_ffn_down[1], "pg": pg}


def _unpack(p):
    def sec(n):
        return p[n]
    w_qkv = sec("qkv").T[None]
    w_attn_out = sec("wo")[None]
    w_pool_in = sec("wpi")[None]
    w_pool_group = sec("pg").reshape(SEC_ROWS["pg"], POOL_G, PGD).transpose(1, 0, 2)[None]
    w_gu = jnp.stack([sec("gu0").T, sec("gu1").T])
    w_d = jnp.stack([sec("d0"), sec("d1")])
    return w_qkv, w_attn_out, w_pool_in, w_pool_group, w_gu, w_d


def _vec_pack(attn_norm, ffn_norm, final_norm, pool_norm_sh, pool_scale_sh, me):
    def place(sh):
        return lax.dynamic_update_slice(jnp.zeros((1, D), F32), sh, (0, me * 128))
    return jnp.concatenate([attn_norm, ffn_norm, final_norm.reshape(1, D), place(pool_norm_sh),
                            place(pool_scale_sh), jnp.zeros((2, D), F32)], axis=0)


def _vec_unpack(p, me):
    def take(r):
        return lax.dynamic_slice(p[r:r + 1], (0, me * 128), (1, 128))
    return p[0:1], p[1:3], p[3], take(4), take(5)


def kernel(x, attn_norm, w_qkv, w_attn_out, pool_norm, w_pool_in, w_pool_group, pool_scale, ffn_norm, w_ffn_gate_up, w_ffn_down, final_norm, loss_target, m_attn_norm, m_w_qkv, m_w_attn_out, m_pool_norm, m_w_pool_in, m_w_pool_group, m_pool_scale, m_ffn_norm, m_w_ffn_gate_up, m_w_ffn_down, m_final_norm, v_attn_norm, v_w_qkv, v_w_attn_out, v_pool_norm, v_w_pool_in, v_w_pool_group, v_pool_scale, v_ffn_norm, v_w_ffn_gate_up, v_w_ffn_down, v_final_norm):
    me = 4 * lax.axis_index("x") + 2 * lax.axis_index("y") + lax.axis_index("c")

    pw = _pack_sections(w_qkv, w_attn_out, w_pool_in, w_pool_group, w_ffn_gate_up, w_ffn_down)
    pm = _pack_sections(m_w_qkv, m_w_attn_out, m_w_pool_in, m_w_pool_group, m_w_ffn_gate_up, m_w_ffn_down)
    pv = _pack_sections(v_w_qkv, v_w_attn_out, v_w_pool_in, v_w_pool_group, v_w_ffn_gate_up, v_w_ffn_down)
    vsh = jnp.concatenate([pool_norm, pool_scale, jnp.zeros((6, 128), F32)], axis=0)

    vg = _bcast_all(vsh, "gather_pool_vectors")
    comm = _Comm({n: pw[n].astype(BF16) for n, _ in SECTIONS}, me, deps=(vg,))
    pool_norm_full = vg[:, 0, :].reshape(1, D)
    pool_scale_full = vg[:, 1, :].reshape(1, D)

    grad_x, vec = _local_step(x[0], loss_target[0], comm, attn_norm, ffn_norm, final_norm,
                              pool_norm_full, pool_scale_full)

    vw = _vec_pack(attn_norm, ffn_norm, final_norm, pool_norm, pool_scale, me)
    vm = _vec_pack(m_attn_norm, m_ffn_norm, m_final_norm, m_pool_norm, m_pool_scale, me)
    vv = _vec_pack(v_attn_norm, v_ffn_norm, v_final_norm, v_pool_norm, v_pool_scale, me)

    sec_out = [{}, {}, {}, {}]
    vec_out = None
    after = grad_x
    for group in range(len(RS_GROUPS)):
        if group == len(RS_GROUPS) - 1:
            VR = _bcast_all(vec, "exchange_vector_grads", deps=(after,))
            vec_out = _adamw(VR, vw, vm, vv, rows=8, tr=8, name="adamw_vec")
            after = vec_out[0]
        for n, R in comm.received(group, after).items():
            tr = ADAM_TILE[n]
            res = _adamw(R, pw[n], pm[n], pv[n], rows=SEC_ROWS[n], tr=tr, name=f"adamw_{n}")
            for kind in range(4):
                sec_out[kind][n] = res[kind]
            after = res[0]

    outs = []
    for kind in range(4):
        q, o, pi, pg, gu, dn = _unpack(sec_out[kind])
        an, fn, fin, pn, ps = _vec_unpack(vec_out[kind], me)
        outs.append((an, q, o, pn, pi, pg, ps, fn, gu, dn, fin))
    loss = 0.5 * jnp.sum(vec_out[0][6]) / D
    return (loss, grad_x[None]) + outs[0] + outs[1] + outs[2] + outs[3]
```

```python
import jax
import jax.numpy as jnp
from jax import lax
from jax.experimental import pallas as pl
from jax.experimental.pallas import tpu as pltpu

F32 = jnp.float32
BF16 = jnp.bfloat16

D = 1024
NDEV = 8
HEADS = 8
HD = 128
QB = 128
NGROUPS = 3
DILS = (1, 4, 16)
DFF = 2816
HCH = 1408
POOL_G = 4
PGD = 256
RMS_EPS = 1e-6
NEG = -1e30

ADAM_LR = 0.001
ADAM_B1 = 0.9
ADAM_B2 = 0.999
ADAM_EPS = 1e-08
ADAM_WD = 0.01
ADAM_STEP = 10

VMEM_LIMIT = 52 * 1024 * 1024

SECTIONS = (("qkv", 1152), ("wo", 128), ("wpi", 128), ("gu0", 704), ("gu1", 704),
            ("d0", 352), ("d1", 352), ("pg", 32))
LOC_OFF = {}
GLB_OFF = {}
_o = 0
for _n, _r in SECTIONS:
    LOC_OFF[_n] = _o
    GLB_OFF[_n] = _o * NDEV
    _o += _r
PACK_ROWS = _o
GLB_ROWS = PACK_ROWS * NDEV
SEC_ROWS = dict(SECTIONS)


def _cparams(n_grid):
    return pltpu.CompilerParams(dimension_semantics=("arbitrary",) * n_grid, vmem_limit_bytes=VMEM_LIMIT)


def _shard_pos(name, dev):
    n = SEC_ROWS[name]
    if name in ("gu0", "gu1"):
        return ((dev % 4) // 2) * (2 * HCH) + (dev // 4) * HCH + (dev % 2) * n
    return dev * n


def _mm(a, b, *, mode, M, N, K, tm, tn, tk, out_dtype, name, a_off=(0, 0), b_off=(0, 0), res=None,
        out_rows=None, out_off=0, out_prev=None, deps=()):
    nm, nn, nk = M // tm, N // tn, K // tk
    assert nm * tm == M and nn * tn == N and nk * tk == K
    if mode == "nn":
        a_bs, b_bs = (tm, tk), (tk, tn)
        a_ix = lambda i, j, k: (i, k)
        b_ix = lambda i, j, k: (k, j)
        dims = (((1,), (0,)), ((), ()))
    elif mode == "nt":
        a_bs, b_bs = (tm, tk), (tn, tk)
        a_ix = lambda i, j, k: (i, k)
        b_ix = lambda i, j, k: (j, k)
        dims = (((1,), (1,)), ((), ()))
    else:
        a_bs, b_bs = (tk, tm), (tk, tn)
        a_ix = lambda i, j, k: (k, i)
        b_ix = lambda i, j, k: (k, j)
        dims = (((0,), (0,)), ((), ()))

    def spec(bs, ix, off):
        def im(i, j, k):
            r, c = ix(i, j, k)
            return (r + off[0], c + off[1])
        return pl.BlockSpec(bs, im)

    in_specs = [spec(a_bs, a_ix, a_off), spec(b_bs, b_ix, b_off)]
    args = [a, b]
    if res is not None:
        in_specs.append(pl.BlockSpec((tm, tn), lambda i, j, k: (i, j)))
        args.append(res)
    out_shape = jax.ShapeDtypeStruct((M if out_rows is None else out_rows, N), out_dtype)
    out_spec = pl.BlockSpec((tm, tn), lambda i, j, k: (i + out_off, j))
    has_res = res is not None
    extra = list(deps) + ([out_prev] if out_prev is not None else [])
    for dep in extra:
        in_specs.append(pl.BlockSpec(memory_space=pl.ANY))
        args.append(dep)
    o_pos = 2 + int(has_res) + len(extra)
    aliases = {len(args) - 1: 0} if out_prev is not None else {}

    def kern(*refs):
        a_ref, b_ref = refs[0], refs[1]
        res_ref = refs[2] if has_res else None
        o_ref = refs[o_pos]
        av = a_ref[...]
        bv = b_ref[...]
        if av.dtype != BF16:
            av = av.astype(BF16)
        if bv.dtype != BF16:
            bv = bv.astype(BF16)
        part = lax.dot_general(av, bv, dims, preferred_element_type=F32)

        def write(val):
            if has_res:
                val = val + res_ref[...]
            o_ref[...] = val.astype(out_dtype)

        if nk == 1:
            write(part)
        else:
            acc_ref = refs[-1]
            k = pl.program_id(2)

            @pl.when(k == 0)
            def _():
                acc_ref[...] = part

            @pl.when(k > 0)
            def _():
                acc_ref[...] += part

            @pl.when(k == nk - 1)
            def _():
                write(acc_ref[...])

    scratch = [pltpu.VMEM((tm, tn), F32)] if nk > 1 else []
    return pl.pallas_call(
        kern, grid=(nm, nn, nk), in_specs=in_specs, out_specs=out_spec, out_shape=out_shape,
        scratch_shapes=scratch, input_output_aliases=aliases, compiler_params=_cparams(3), name=name)(*args)


def _rms_fwd(x, g, name, deps=()):
    S = x.shape[0]
    tr = 512

    def kern(x_ref, g_ref, *rest):
        h_ref = rest[-1]
        xv = x_ref[...]
        r = lax.rsqrt(jnp.mean(xv * xv, axis=-1, keepdims=True) + RMS_EPS)
        h_ref[...] = (xv * r * g_ref[...]).astype(BF16)

    return pl.pallas_call(
        kern, grid=(S // tr,),
        in_specs=[pl.BlockSpec((tr, D), lambda i: (i, 0)), pl.BlockSpec((1, D), lambda i: (0, 0))]
        + [pl.BlockSpec(memory_space=pl.ANY)] * len(deps),
        out_specs=pl.BlockSpec((tr, D), lambda i: (i, 0)),
        out_shape=jax.ShapeDtypeStruct((S, D), BF16), compiler_params=_cparams(1), name=name)(x, g, *deps)


def _chunks_put(scr, val):
    for c in range(scr.shape[0]):
        scr[c] = val[:, c * 128:(c + 1) * 128]


def _chunks_get(scr):
    return jnp.concatenate([scr[c] for c in range(scr.shape[0])], axis=1)


def _chunks_rows(scr, r, n, dil):
    return jnp.concatenate([scr.at[c][pl.ds(r, n, stride=dil), :] for c in range(scr.shape[0])], axis=1)


def _chunks_add_rows(scr, val, r, n, dil, accumulate):
    for c in range(scr.shape[0]):
        rows = pl.ds(r, n, stride=dil)
        piece = val[:, c * 128:(c + 1) * 128]
        tile = scr.at[c]
        tile[rows, :] = tile[rows, :] + piece if accumulate else piece


def _rms_fwd_folded(x, g, name, deps=()):
    S = x.shape[0]
    tr = 512
    dils = DILS[1:]

    def kern(x_ref, g_ref, *rest):
        outs, scr = rest[len(deps):-1], rest[-1]
        xv = x_ref[...]
        r = lax.rsqrt(jnp.mean(xv * xv, axis=-1, keepdims=True) + RMS_EPS)
        h = (xv * r * g_ref[...]).astype(BF16)
        outs[0][...] = h
        _chunks_put(scr, h.astype(F32))
        for o_ref, dil in zip(outs[1:], dils):
            for res in range(dil):
                o_ref[res] = _chunks_rows(scr, res, tr // dil, dil).astype(BF16)

    return pl.pallas_call(
        kern, grid=(S // tr,),
        in_specs=[pl.BlockSpec((tr, D), lambda i: (i, 0)), pl.BlockSpec((1, D), lambda i: (0, 0))]
        + [pl.BlockSpec(memory_space=pl.ANY)] * len(deps),
        out_specs=[pl.BlockSpec((tr, D), lambda i: (i, 0))]
        + [pl.BlockSpec((dil, tr // dil, D), lambda i: (0, i, 0)) for dil in dils],
        out_shape=[jax.ShapeDtypeStruct((S, D), BF16)]
        + [jax.ShapeDtypeStruct((dil, S // dil, D), BF16) for dil in dils],
        scratch_shapes=[pltpu.VMEM((D // 128, tr, 128), F32)],
        compiler_params=_cparams(1), name=name)(x, g, *deps)


def _rms_bwd(dh, x, g, dres, name, folded=()):
    S = x.shape[0]
    tr = 512
    nf = len(folded)

    def kern(dh_ref, *rest):
        f_refs = rest[:nf]
        x_ref, g_ref, dres_ref, dx_ref, dg_ref = rest[nf:nf + 5]
        i = pl.program_id(0)
        xv = x_ref[...]
        if nf:
            acc_ref = rest[nf + 5]
            _chunks_put(acc_ref, dh_ref[...].astype(F32))
            for f_ref in f_refs:
                dil = f_ref.shape[0]
                for res in range(dil):
                    _chunks_add_rows(acc_ref, f_ref[res], res, tr // dil, dil, True)
            dhv = _chunks_get(acc_ref)
        else:
            dhv = dh_ref[...].astype(F32)
        r = lax.rsqrt(jnp.mean(xv * xv, axis=-1, keepdims=True) + RMS_EPS)
        xhat = xv * r
        gy = dhv * g_ref[...]
        dx_ref[...] = dres_ref[...] + r * (gy - xhat * jnp.mean(gy * xhat, axis=-1, keepdims=True))
        part = jnp.sum(dhv * xhat, axis=0, keepdims=True)

        @pl.when(i == 0)
        def _():
            dg_ref[...] = part

        @pl.when(i > 0)
        def _():
            dg_ref[...] += part

    row = pl.BlockSpec((tr, D), lambda i: (i, 0))
    vec = pl.BlockSpec((1, D), lambda i: (0, 0))
    fspecs = [pl.BlockSpec((f.shape[0], tr // f.shape[0], D), lambda i: (0, i, 0)) for f in folded]
    return pl.pallas_call(
        kern, grid=(S // tr,), in_specs=[row] + fspecs + [row, vec, row], out_specs=[row, vec],
        out_shape=[jax.ShapeDtypeStruct((S, D), F32), jax.ShapeDtypeStruct((1, D), F32)],
        scratch_shapes=[pltpu.VMEM((D // 128, tr, 128), F32)] if nf else [],
        compiler_params=_cparams(1), name=name)(dh, *folded, x, g, dres)


def _loss_head(x, g, tgt):
    S = x.shape[0]
    tr = 512

    def kern(x_ref, g_ref, t_ref, dx_ref, dg_ref, ls_ref):
        i = pl.program_id(0)
        xv = x_ref[...]
        gv = g_ref[...]
        r = lax.rsqrt(jnp.mean(xv * xv, axis=-1, keepdims=True) + RMS_EPS)
        xhat = xv * r
        e = xhat * gv - t_ref[...]
        dy = e * (1.0 / D)
        gy = dy * gv
        dx_ref[...] = r * (gy - xhat * jnp.mean(gy * xhat, axis=-1, keepdims=True))
        dgp = jnp.sum(dy * xhat, axis=0, keepdims=True)
        lsp = jnp.sum(e * e, axis=0, keepdims=True)

        @pl.when(i == 0)
        def _():
            dg_ref[...] = dgp
            ls_ref[...] = lsp

        @pl.when(i > 0)
        def _():
            dg_ref[...] += dgp
            ls_ref[...] += lsp

    row = pl.BlockSpec((tr, D), lambda i: (i, 0))
    vec = pl.BlockSpec((1, D), lambda i: (0, 0))
    return pl.pallas_call(
        kern, grid=(S // tr,), in_specs=[row, vec, row], out_specs=[row, vec, vec],
        out_shape=[jax.ShapeDtypeStruct((S, D), F32), jax.ShapeDtypeStruct((1, D), F32),
                   jax.ShapeDtypeStruct((1, D), F32)],
        compiler_params=_cparams(1), name="loss_head")(x, g, tgt)


def _ffn_up(h, G, name):
    S = h.shape[0]
    tm = 512
    nj = DFF // HCH

    def kern(h_ref, w_ref, gu_ref, act_ref):
        gu = lax.dot_general(h_ref[...], w_ref[...], (((1,), (1,)), ((), ())), preferred_element_type=F32)
        gu_ref[...] = gu.astype(BF16)
        gate = gu[:, :HCH]
        up = gu[:, HCH:]
        act_ref[...] = (gate * jax.nn.sigmoid(gate) * up).astype(BF16)

    return pl.pallas_call(
        kern, grid=(nj, S // tm),
        in_specs=[pl.BlockSpec((tm, D), lambda j, i: (i, 0)),
                  pl.BlockSpec((2 * HCH, D), lambda j, i: (j, 0))],
        out_specs=[pl.BlockSpec((tm, 2 * HCH), lambda j, i: (i, j)),
                   pl.BlockSpec((tm, HCH), lambda j, i: (i, j))],
        out_shape=[jax.ShapeDtypeStruct((S, 2 * DFF), BF16), jax.ShapeDtypeStruct((S, DFF), BF16)],
        compiler_params=_cparams(2), name=name)(h, G)


def _ffn_down_bwd(dx, G, gu, name):
    S = dx.shape[0]
    tm = 512
    nj = DFF // HCH

    def kern(dx_ref, w_ref, gu_ref, o_ref):
        dact = lax.dot_general(dx_ref[...].astype(BF16), w_ref[...], (((1,), (1,)), ((), ())),
                               preferred_element_type=F32)
        gate = gu_ref[:, :HCH].astype(F32)
        up = gu_ref[:, HCH:].astype(F32)
        sig = jax.nn.sigmoid(gate)
        silu = gate * sig
        o_ref[:, :HCH] = (dact * up * (sig * (1.0 + gate * (1.0 - sig)))).astype(BF16)
        o_ref[:, HCH:] = (dact * silu).astype(BF16)

    return pl.pallas_call(
        kern, grid=(nj, S // tm),
        in_specs=[pl.BlockSpec((tm, D), lambda j, i: (i, 0)),
                  pl.BlockSpec((HCH, D), lambda j, i: (j, 0)),
                  pl.BlockSpec((tm, 2 * HCH), lambda j, i: (i, j))],
        out_specs=pl.BlockSpec((tm, 2 * HCH), lambda j, i: (i, j)),
        out_shape=jax.ShapeDtypeStruct((S, 2 * DFF), BF16),
        compiler_params=_cparams(2), name=name)(dx, G, gu)


def _trail(u, *, backward, name):
    S = u.shape[0]

    def kern(u_ref, o_ref):
        g = pl.program_id(0)
        uv = u_ref[...].astype(F32)
        row = lax.broadcasted_iota(jnp.int32, uv.shape, 0)
        win = jnp.left_shift(jnp.int32(2), g)
        cnt = jnp.minimum(row + 1, win).astype(F32)
        s = uv / cnt if backward else uv
        levels = []
        for k in (1, 2, 4, 8):
            if backward:
                sh = jnp.where(row < S - k, pltpu.roll(s, S - k, 0), 0.0)
            else:
                sh = jnp.where(row >= k, pltpu.roll(s, k, 0), 0.0)
            s = s + sh
            levels.append(s)
        sel = jnp.where(g == 0, levels[0], jnp.where(g == 1, levels[1], jnp.where(g == 2, levels[2], levels[3])))
        if backward:
            o_ref[...] = (sel - uv).astype(BF16)
        else:
            o_ref[...] = (sel / cnt - uv).astype(BF16)

    blk = pl.BlockSpec((S, PGD), lambda g: (0, g))
    return pl.pallas_call(
        kern, grid=(POOL_G,), in_specs=[blk], out_specs=blk,
        out_shape=jax.ShapeDtypeStruct((S, D), BF16), compiler_params=_cparams(1), name=name)(u)


def _pool_out(yd, G, scale, xres):
    S = yd.shape[0]
    tm = 1024

    def kern(y_ref, w_ref, s_ref, x_ref, o_ref):
        z = jnp.dot(y_ref[...], w_ref[...], preferred_element_type=F32)
        o_ref[...] = x_ref[...] + z * s_ref[...]

    tile = pl.BlockSpec((tm, PGD), lambda i, g: (i, g))
    return pl.pallas_call(
        kern, grid=(S // tm, POOL_G),
        in_specs=[tile, pl.BlockSpec((PGD, PGD), lambda i, g: (0, g)),
                  pl.BlockSpec((1, PGD), lambda i, g: (0, g)), tile],
        out_specs=tile, out_shape=jax.ShapeDtypeStruct((S, D), F32),
        compiler_params=_cparams(2), name="pool_out")(yd, G, scale, xres)


def _pool_out_bwd(dz, yd, G, scale):
    S = yd.shape[0]
    tm = 1024
    ni = S // tm

    def kern(dz_ref, y_ref, w_ref, s_ref, dy_ref, ds_ref, dw_ref, acc_ref):
        i = pl.program_id(1)
        dzv = dz_ref[...]
        yv = y_ref[...]
        wv = w_ref[...]
        zraw = jnp.dot(yv, wv, preferred_element_type=F32)
        dsp = jnp.sum(dzv * zraw, axis=0, keepdims=True)
        dzr = (dzv * s_ref[...]).astype(BF16)
        dy_ref[...] = lax.dot_general(dzr, wv, (((1,), (1,)), ((), ())), preferred_element_type=F32)
        dwp = lax.dot_general(yv, dzr, (((0,), (0,)), ((), ())), preferred_element_type=F32)

        @pl.when(i == 0)
        def _():
            ds_ref[...] = dsp
            acc_ref[...] = dwp

        @pl.when(i > 0)
        def _():
            ds_ref[...] += dsp
            acc_ref[...] += dwp

        @pl.when(i == ni - 1)
        def _():
            dw_ref[...] = acc_ref[...].astype(BF16)

    tile = pl.BlockSpec((tm, PGD), lambda g, i: (i, g))
    return pl.pallas_call(
        kern, grid=(POOL_G, ni),
        in_specs=[tile, tile, pl.BlockSpec((PGD, PGD), lambda g, i: (0, g)),
                  pl.BlockSpec((1, PGD), lambda g, i: (0, g))],
        out_specs=[tile, pl.BlockSpec((1, PGD), lambda g, i: (0, g)),
                   pl.BlockSpec((PGD, PGD), lambda g, i: (0, g))],
        out_shape=[jax.ShapeDtypeStruct((S, D), F32), jax.ShapeDtypeStruct((1, D), F32),
                   jax.ShapeDtypeStruct((PGD, D), BF16)],
        scratch_shapes=[pltpu.VMEM((PGD, PGD), F32)],
        compiler_params=_cparams(2), name="pool_out_bwd")(dz, yd, G, scale)


def _bias_table():
    qi = jnp.arange(QB)[:, None]
    ki = jnp.arange(2 * QB)[None, :]
    delta = QB + qi - ki
    inband = (delta >= 0) & (delta <= QB)
    n = NGROUPS * HEADS
    slopes = jnp.exp2(-8.0 * jnp.arange(1, n + 1, dtype=F32) / n).reshape(NGROUPS, HEADS)
    dil = jnp.asarray(DILS, F32)
    bias = -slopes[:, :, None, None] * (delta.astype(F32)[None, None] * dil[:, None, None, None])
    return jnp.where(inband[None, None], bias, NEG)


def _attn_fwd(qkv_f, bias, nb, name):
    S = qkv_f.shape[0]
    nblk = S // QB
    scale = HD ** -0.5

    def kern(q_ref, kc_ref, kp_ref, vc_ref, vp_ref, b_ref, o_ref, l_ref, s_scr, p_scr, r_scr):
        b = pl.program_id(0)
        has_prev = jnp.bitwise_and(b, nb - 1) != 0
        col = lax.broadcasted_iota(jnp.int32, (QB, 2 * QB), 1)
        dead = jnp.logical_and(col < QB, jnp.logical_not(has_prev))
        lane = lax.broadcasted_iota(jnp.int32, (QB, HD), 1)
        lse_all = jnp.zeros((QB, HD), F32)
        for h in range(HEADS):
            sl = slice(h * HD, (h + 1) * HD)
            kk = jnp.concatenate([kp_ref[:, sl], kc_ref[:, sl]], axis=0)
            s_scr[h] = lax.dot_general(q_ref[:, sl], kk, (((1,), (1,)), ((), ())), preferred_element_type=F32)
        for h in range(HEADS):
            s = s_scr[h] * scale + b_ref[h]
            s = jnp.where(dead, NEG, s)
            m = jnp.max(s, axis=-1, keepdims=True)
            p = jnp.exp(s - m)
            den = jnp.sum(p, axis=-1, keepdims=True)
            p_scr[h] = p.astype(BF16)
            r_scr[h] = jnp.broadcast_to(1.0 / den, (QB, HD))
            lse_all = jnp.where(lane == h, m + jnp.log(den), lse_all)
        for h in range(HEADS):
            sl = slice(h * HD, (h + 1) * HD)
            vv = jnp.concatenate([vp_ref[:, sl], vc_ref[:, sl]], axis=0)
            o = jnp.dot(p_scr[h], vv, preferred_element_type=F32) * r_scr[h]
            o_ref[:, sl] = o.astype(BF16)
        l_ref[...] = lse_all

    def blk(colblk, prev):
        if prev:
            return pl.BlockSpec((QB, D), lambda b: (jnp.maximum(b - 1, 0), colblk))
        return pl.BlockSpec((QB, D), lambda b: (b, colblk))

    return pl.pallas_call(
        kern, grid=(nblk,),
        in_specs=[blk(0, False), blk(1, False), blk(1, True), blk(2, False), blk(2, True),
                  pl.BlockSpec((HEADS, QB, 2 * QB), lambda b: (0, 0, 0))],
        out_specs=[pl.BlockSpec((QB, D), lambda b: (b, 0)), pl.BlockSpec((QB, HD), lambda b: (b, 0))],
        out_shape=[jax.ShapeDtypeStruct((S, D), BF16), jax.ShapeDtypeStruct((S, HD), F32)],
        scratch_shapes=[pltpu.VMEM((HEADS, QB, 2 * QB), F32), pltpu.VMEM((HEADS, QB, 2 * QB), BF16),
                        pltpu.VMEM((HEADS, QB, HD), F32)],
        compiler_params=_cparams(1), name=name)(qkv_f, qkv_f, qkv_f, qkv_f, qkv_f, bias)


def _natural(ref, scr, tm):
    dil = ref.shape[0]
    for res in range(dil):
        _chunks_add_rows(scr, ref[res].astype(F32), res, tm // dil, dil, False)
    return _chunks_get(scr)


def _attn_merge(os, lses):
    S = os[0].shape[0]
    tm = 512

    def kern(o0, o1, o2, l0, l1, l2, om_ref, lm_ref, ls1, ls2, os1, os2):
        la = l0[...]
        lb = _natural(l1, ls1, tm)
        lc = _natural(l2, ls2, tm)
        m = jnp.maximum(jnp.maximum(la, lb), lc)
        e0, e1, e2 = jnp.exp(la - m), jnp.exp(lb - m), jnp.exp(lc - m)
        tot = e0 + e1 + e2
        lm_ref[...] = m + jnp.log(tot)
        w0, w1, w2 = e0 / tot, e1 / tot, e2 / tot
        for res in range(o1.shape[0]):
            _chunks_add_rows(os1, o1[res].astype(F32), res, tm // o1.shape[0], o1.shape[0], False)
        for res in range(o2.shape[0]):
            _chunks_add_rows(os2, o2[res].astype(F32), res, tm // o2.shape[0], o2.shape[0], False)
        for h in range(HEADS):
            sl = slice(h * HD, (h + 1) * HD)
            acc = w0[:, h:h + 1] * o0[:, sl].astype(F32) + w1[:, h:h + 1] * os1[h] + w2[:, h:h + 1] * os2[h]
            om_ref[:, sl] = acc.astype(BF16)

    def spec(a, c):
        if a.ndim == 2:
            return pl.BlockSpec((tm, c), lambda i: (i, 0))
        return pl.BlockSpec((a.shape[0], tm // a.shape[0], c), lambda i: (0, i, 0))

    return pl.pallas_call(
        kern, grid=(S // tm,),
        in_specs=[spec(a, D) for a in os] + [spec(a, HD) for a in lses],
        out_specs=[pl.BlockSpec((tm, D), lambda i: (i, 0)), pl.BlockSpec((tm, HD), lambda i: (i, 0))],
        out_shape=[jax.ShapeDtypeStruct((S, D), BF16), jax.ShapeDtypeStruct((S, HD), F32)],
        scratch_shapes=[pltpu.VMEM((1, tm, HD), F32), pltpu.VMEM((1, tm, HD), F32),
                        pltpu.VMEM((HEADS, tm, HD), F32), pltpu.VMEM((HEADS, tm, HD), F32)],
        compiler_params=_cparams(1), name="attn_merge")(*os, *lses)


def _attn_bwd_prep(do, o, lse):
    S = o.shape[0]
    tm = 512
    dils = DILS[1:]

    def kern(do_ref, o_ref, l_ref, *rest):
        do_outs, l_outs, d_outs = rest[0:3], rest[3:5], rest[5:8]
        do_scr, l_scr, d_scr = rest[8:11]
        lane = lax.broadcasted_iota(jnp.int32, (tm, HD), 1)
        acc = jnp.zeros((tm, HD), F32)
        for h in range(HEADS):
            sl = slice(h * HD, (h + 1) * HD)
            prod = do_ref[:, sl] * o_ref[:, sl].astype(F32)
            acc = jnp.where(lane == h, jnp.sum(prod, axis=-1, keepdims=True), acc)
        d_scr[0] = acc
        l_scr[0] = l_ref[...]
        _chunks_put(do_scr, do_ref[...])
        do_outs[0][...] = do_ref[...].astype(BF16)
        d_outs[0][...] = acc
        for j, dil in enumerate(dils):
            for res in range(dil):
                n = tm // dil
                do_outs[1 + j][res] = _chunks_rows(do_scr, res, n, dil).astype(BF16)
                l_outs[j][res] = _chunks_rows(l_scr, res, n, dil)
                d_outs[1 + j][res] = _chunks_rows(d_scr, res, n, dil)

    def nat(c):
        return pl.BlockSpec((tm, c), lambda i: (i, 0))

    def fol(dil, c):
        return pl.BlockSpec((dil, tm // dil, c), lambda i: (0, i, 0))

    def shapes(c, dt, with_natural):
        first = [jax.ShapeDtypeStruct((S, c), dt)] if with_natural else []
        return first + [jax.ShapeDtypeStruct((dil, S // dil, c), dt) for dil in dils]

    outs = pl.pallas_call(
        kern, grid=(S // tm,), in_specs=[nat(D), nat(D), nat(HD)],
        out_specs=[nat(D)] + [fol(dil, D) for dil in dils] + [fol(dil, HD) for dil in dils]
        + [nat(HD)] + [fol(dil, HD) for dil in dils],
        out_shape=shapes(D, BF16, True) + shapes(HD, F32, False) + shapes(HD, F32, True),
        scratch_shapes=[pltpu.VMEM((HEADS, tm, HD), F32), pltpu.VMEM((1, tm, HD), F32), pltpu.VMEM((1, tm, HD), F32)],
        compiler_params=_cparams(1), name="attn_bwd_prep")(do, o, lse)
    return outs[0:3], [lse] + list(outs[3:5]), outs[5:8]


def _attn_bwd(qkv_f, do_f, lse_f, delta_f, bias, nb, name):
    S = qkv_f.shape[0]
    nblk = S // QB
    scale = HD ** -0.5

    def kern(q_ref, kc_ref, kp_ref, vc_ref, vp_ref, do_ref, l_ref, d_ref, b_ref, out_ref, dq_c, dk_c, dv_c,
             s_scr, dp_scr, ds_scr, p_scr):
        b = pl.program_id(0)

        @pl.when(b == 0)
        def _():
            dq_c[...] = jnp.zeros_like(dq_c)
            dk_c[...] = jnp.zeros_like(dk_c)
            dv_c[...] = jnp.zeros_like(dv_c)

        @pl.when(b == nblk)
        def _():
            out_ref[:, 0:D] = dq_c[...].astype(BF16)
            out_ref[:, D:2 * D] = dk_c[...].astype(BF16)
            out_ref[:, 2 * D:3 * D] = dv_c[...].astype(BF16)

        @pl.when(b < nblk)
        def _():
            has_prev = jnp.bitwise_and(b, nb - 1) != 0
            col = lax.broadcasted_iota(jnp.int32, (QB, 2 * QB), 1)
            dead = jnp.logical_and(col < QB, jnp.logical_not(has_prev))
            out_ref[:, 0:D] = dq_c[...].astype(BF16)
            lv = l_ref[...]
            dv_ = d_ref[...]
            for h in range(HEADS):
                sl = slice(h * HD, (h + 1) * HD)
                kk = jnp.concatenate([kp_ref[:, sl], kc_ref[:, sl]], axis=0)
                vv = jnp.concatenate([vp_ref[:, sl], vc_ref[:, sl]], axis=0)
                s_scr[h] = lax.dot_general(q_ref[:, sl], kk, (((1,), (1,)), ((), ())), preferred_element_type=F32)
                dp_scr[h] = lax.dot_general(do_ref[:, sl], vv, (((1,), (1,)), ((), ())),
                                            preferred_element_type=F32)
            for h in range(HEADS):
                s = s_scr[h] * scale + b_ref[h]
                s = jnp.where(dead, NEG, s)
                p = jnp.exp(s - lv[:, h:h + 1])
                ds_scr[h] = (p * (dp_scr[h] - dv_[:, h:h + 1]) * scale).astype(BF16)
                p_scr[h] = p.astype(BF16)
            for h in range(HEADS):
                sl = slice(h * HD, (h + 1) * HD)
                kk = jnp.concatenate([kp_ref[:, sl], kc_ref[:, sl]], axis=0)
                ds = ds_scr[h]
                dq_c[:, sl] = jnp.dot(ds, kk, preferred_element_type=F32)
                dkk = lax.dot_general(ds, q_ref[:, sl], (((0,), (0,)), ((), ())), preferred_element_type=F32)
                dvv = lax.dot_general(p_scr[h], do_ref[:, sl], (((0,), (0,)), ((), ())),
                                      preferred_element_type=F32)
                out_ref[:, D + h * HD:D + (h + 1) * HD] = (dk_c[:, sl] + dkk[:QB]).astype(BF16)
                out_ref[:, 2 * D + h * HD:2 * D + (h + 1) * HD] = (dv_c[:, sl] + dvv[:QB]).astype(BF16)
                dk_c[:, sl] = dkk[QB:]
                dv_c[:, sl] = dvv[QB:]

    last = nblk - 1

    def blk(colblk, prev):
        if prev:
            return pl.BlockSpec((QB, D), lambda b: (jnp.maximum(jnp.minimum(b, last) - 1, 0), colblk))
        return pl.BlockSpec((QB, D), lambda b: (jnp.minimum(b, last), colblk))

    stat = pl.BlockSpec((QB, HD), lambda b: (jnp.minimum(b, last), 0))
    return pl.pallas_call(
        kern, grid=(nblk + 1,),
        in_specs=[blk(0, False), blk(1, False), blk(1, True), blk(2, False), blk(2, True),
                  pl.BlockSpec((QB, D), lambda b: (jnp.minimum(b, last), 0)), stat, stat,
                  pl.BlockSpec((HEADS, QB, 2 * QB), lambda b: (0, 0, 0))],
        out_specs=pl.BlockSpec((QB, 3 * D), lambda b: (jnp.maximum(b - 1, 0), 0)),
        out_shape=jax.ShapeDtypeStruct((S, 3 * D), BF16),
        scratch_shapes=[pltpu.VMEM((QB, D), F32), pltpu.VMEM((QB, D), F32), pltpu.VMEM((QB, D), F32),
                        pltpu.VMEM((HEADS, QB, 2 * QB), F32), pltpu.VMEM((HEADS, QB, 2 * QB), F32),
                        pltpu.VMEM((HEADS, QB, 2 * QB), BF16), pltpu.VMEM((HEADS, QB, 2 * QB), BF16)],
        compiler_params=_cparams(1), name=name)(qkv_f, qkv_f, qkv_f, qkv_f, qkv_f, do_f, lse_f, delta_f, bias)


def _local_step(x, tgt, comm, attn_norm, ffn_norm, final_norm, pool_norm, pool_scale):
    S = x.shape[0]
    bias = _bias_table()
    g_attn = attn_norm.reshape(1, D)
    g_f0 = ffn_norm[0:1]
    g_f1 = ffn_norm[1:2]
    g_fin = final_norm.reshape(1, D)
    W = {}

    def ffn_fwd(xin, gain, l):
        h = _rms_fwd(xin, gain, f"rms_ffn{l}")
        gu, act = _ffn_up(h, W[f"gu{l}"], f"ffn_up{l}")
        xo = _mm(act, W[f"d{l}"], mode="nn", M=S, N=D, K=DFF, tm=512, tn=D, tk=DFF, out_dtype=F32,
                 res=xin, name=f"ffn_down{l}")
        return h, gu, act, xo

    def ffn_bwd(dxo, xin, gain, h, gu, act, l, rs_group):
        dgu = _ffn_down_bwd(dxo, W[f"d{l}"], gu, f"ffn_down_bwd{l}")
        gw_d = _mm(act, dxo, mode="tn", M=DFF, N=D, K=S, tm=HCH, tn=D, tk=1024, out_dtype=BF16, name=f"gw_d{l}")
        gw_gu = _mm(dgu, h, mode="tn", M=2 * DFF, N=D, K=S, tm=HCH, tn=D, tk=1024, out_dtype=BF16, name=f"gw_gu{l}")
        token = comm.send_grads(rs_group, {f"d{l}": gw_d, f"gu{l}": gw_gu})
        dh = _mm(dgu, W[f"gu{l}"], mode="nn", M=S, N=D, K=2 * DFF, tm=1024, tn=D, tk=HCH,
                 out_dtype=F32, deps=(token,), name=f"ffn_up_bwd{l}")
        dxin, dgain = _rms_bwd(dh, xin, gain, dxo, f"rms_ffn_bwd{l}")
        return dxin, dgain

    nbs = [S // QB // dil for dil in DILS]
    hf = _rms_fwd_folded(x, g_attn, "rms_attn", deps=(comm.ag_token,))
    hf = [h.reshape(S, D) for h in hf]
    W.update(comm.weights(0, hf[0]))
    qkv_f, o_f, lse_f = [], [], []
    for g, dil in enumerate(DILS):
        qkv_f.append(_mm(hf[g], W["qkv"], mode="nt", M=S, N=3 * D, K=D, tm=1024, tn=1024, tk=D, out_dtype=BF16,
                         b_off=(3 * g, 0), name=f"qkv_proj{g}"))
        og, lg = _attn_fwd(qkv_f[g], bias[g], nbs[g], f"attn_fwd{g}")
        o_f.append(og if dil == 1 else og.reshape(dil, S // dil, D))
        lse_f.append(lg if dil == 1 else lg.reshape(dil, S // dil, HD))
    o, lse = _attn_merge(o_f, lse_f)
    W.update(comm.weights(1, o))
    x1 = _mm(o, W["wo"], mode="nn", M=S, N=D, K=D, tm=1024, tn=D, tk=D, out_dtype=F32, res=x, name="attn_out")
    h1, gu0, act0, x2 = ffn_fwd(x1, g_f0, 0)

    W.update(comm.weights(2, x2))
    h2 = _rms_fwd(x2, pool_norm, "rms_pool")
    u = _mm(h2, W["wpi"], mode="nn", M=S, N=D, K=D, tm=1024, tn=D, tk=D, out_dtype=F32, name="pool_in")
    yd = _trail(u, backward=False, name="trail_fwd")
    x3 = _pool_out(yd, W["pg"], pool_scale, x2)
    h3, gu1, act1, x4 = ffn_fwd(x3, g_f1, 1)

    dx4, d_fin, lossvec = _loss_head(x4, g_fin, tgt)

    dx3, d_f1 = ffn_bwd(dx4, x3, g_f1, h3, gu1, act1, 1, 0)
    dyd, d_scale, gw_pg = _pool_out_bwd(dx3, yd, W["pg"], pool_scale)
    du = _trail(dyd, backward=True, name="trail_bwd")
    gw_pi = _mm(h2, du, mode="tn", M=D, N=D, K=S, tm=D, tn=D, tk=1024, out_dtype=BF16, name="gw_pi")
    token = comm.send_grads(1, {"pg": gw_pg, "wpi": gw_pi})
    dh2 = _mm(du, W["wpi"], mode="nt", M=S, N=D, K=D, tm=1024, tn=D, tk=D, out_dtype=F32,
              deps=(token,), name="pool_in_bwd")
    dx2, d_pool = _rms_bwd(dh2, x2, pool_norm, dx3, "rms_pool_bwd")
    dx1, d_f0 = ffn_bwd(dx2, x1, g_f0, h1, gu0, act0, 0, 2)

    gw_o = _mm(o, dx1, mode="tn", M=D, N=D, K=S, tm=D, tn=D, tk=1024, out_dtype=BF16, name="gw_o")
    do = _mm(dx1, W["wo"], mode="nt", M=S, N=D, K=D, tm=1024, tn=D, tk=D, out_dtype=F32, name="attn_out_bwd")
    do_f, lse_ff, delta_f = _attn_bwd_prep(do, o, lse)
    dqkv_f, gw_qkv = [], None
    for g in range(NGROUPS):
        dqkv_f.append(_attn_bwd(qkv_f[g], do_f[g].reshape(S, D), lse_ff[g].reshape(S, HD),
                                delta_f[g].reshape(S, HD), bias[g], nbs[g], f"attn_bwd{g}"))
        gw_qkv = _mm(dqkv_f[g], hf[g], mode="tn", M=3 * D, N=D, K=S, tm=1024, tn=D, tk=1024, out_dtype=BF16,
                     out_rows=NGROUPS * 3 * D, out_off=3 * g, out_prev=gw_qkv, name=f"gw_qkv{g}")
    token = comm.send_grads(3, {"wo": gw_o, "qkv": gw_qkv})
    dh0_f = [_mm(dqkv_f[g], W["qkv"], mode="nn", M=S, N=D, K=3 * D, tm=1024, tn=D, tk=1024, out_dtype=F32,
                 b_off=(3 * g, 0), deps=(token,), name=f"qkv_proj_bwd{g}") for g in range(NGROUPS)]
    folded = [dh0_f[g].reshape(dil, S // dil, D) for g, dil in enumerate(DILS) if dil > 1]
    grad_x, d_attn = _rms_bwd(dh0_f[0], x, g_attn, dx1, "rms_attn_bwd", folded=folded)

    vec = jnp.concatenate([d_attn, d_f0, d_f1, d_fin, d_pool, d_scale, lossvec, jnp.zeros((1, D), F32)], axis=0)
    return grad_x, vec


def _mesh_pos():
    x, y, c = lax.axis_index("x"), lax.axis_index("y"), lax.axis_index("c")
    return x, y, c, 4 * x + 2 * y + c


def _peer(x, y, c, k):
    kx, ky, kc = (k >> 2) & 1, (k >> 1) & 1, k & 1
    px = 1 - x if kx else x
    py = 1 - y if ky else y
    pc = 1 - c if kc else c
    return (px, py, pc), 4 * px + 2 * py + pc


ANY = pl.BlockSpec(memory_space=pl.ANY)


HBM = pl.BlockSpec(memory_space=pltpu.HBM)
SEMS = pl.BlockSpec(memory_space=pltpu.SEMAPHORE)
EFFECT = pltpu.SideEffectType.DATAFLOW_SIDE_EFFECTING
NPEER = NDEV - 1

AG_GROUPS = (("qkv",), ("wo", "gu0", "d0"), ("wpi", "pg", "gu1", "d1"))
AG_ORDER = tuple(n for grp in AG_GROUPS for n in grp)
RS_GROUPS = (("d1", "gu1"), ("pg", "wpi"), ("d0", "gu0"), ("wo", "qkv"))


def _hbm(a):
    return pltpu.with_memory_space_constraint(a, pltpu.HBM)


def _remote(src, dst, send, recv, peer):
    return pltpu.make_async_remote_copy(src_ref=src, dst_ref=dst, send_sem=send, recv_sem=recv, device_id=peer,
                                        device_id_type=pl.DeviceIdType.MESH)


def _bcast_all(v, name, deps=()):
    W = v.shape[1]
    nd = len(deps)

    def kern(v_ref, *rest):
        o_ref, send, recv, lsem = rest[nd:]
        x, y, c, me = _mesh_pos()
        own = pltpu.make_async_copy(v_ref, o_ref.at[me], lsem)
        own.start()
        cps = [_remote(v_ref, o_ref.at[me], send.at[k - 1], recv.at[k - 1], _peer(x, y, c, k)[0])
               for k in range(1, NDEV)]
        for cp in cps:
            cp.start()
        for cp in cps:
            cp.wait_recv()
            cp.wait_send()
        own.wait()

    return pl.pallas_call(
        kern, in_specs=[ANY] * (1 + nd), out_specs=ANY, out_shape=jax.ShapeDtypeStruct((NDEV, 8, W), F32),
        scratch_shapes=[pltpu.SemaphoreType.DMA((NPEER,)), pltpu.SemaphoreType.DMA((NPEER,)),
                        pltpu.SemaphoreType.DMA(())],
        name=name)(v, *deps)


def _split_start(srcs, src_of, lands, copy_refs, name, deps=()):
    ns, n, nd = len(srcs), len(lands), len(deps)

    def body(*refs):
        ins, land = refs[:ns], refs[ns:ns + n]
        send, recv = refs[ns + n + nd], refs[ns + n + nd + 1]
        token = refs[-1]
        x, y, c, me = _mesh_pos()
        for j in range(n):
            for k in range(1, NDEV):
                peer, pid = _peer(x, y, c, k)
                src, dst = copy_refs(j, (land[j] if src_of[j] is None else ins[src_of[j]]), land[j], me, pid)
                _remote(src, dst, send.at[j * NPEER + k - 1], recv.at[j * NPEER + k - 1], peer).start()
        token[...] = jnp.zeros_like(token)

    outs = pl.pallas_call(
        body, name=name,
        out_shape=(pltpu.SemaphoreType.DMA((n * NPEER,)), pltpu.SemaphoreType.DMA((n * NPEER,)))
        + tuple(pltpu.HBM(a.shape, a.dtype) for a in srcs) + tuple(pltpu.HBM(a.shape, a.dtype) for a in lands)
        + (jax.ShapeDtypeStruct((8, 128), F32),),
        in_specs=(HBM,) * (ns + n) + (ANY,) * nd,
        out_specs=(SEMS, SEMS) + (HBM,) * (ns + n) + (pl.BlockSpec(memory_space=pltpu.VMEM),),
        input_output_aliases={i: 2 + i for i in range(ns + n)},
        compiler_params=pltpu.CompilerParams(has_side_effects=EFFECT),
    )(*[_hbm(a) for a in srcs], *[_hbm(a) for a in lands], *deps)
    return outs[0], outs[1], list(outs[2:2 + ns]), list(outs[2 + ns:2 + ns + n]), outs[-1]


def _split_wait(srcs, src_of, lands, send, recv, sem_rows, wait_refs, after, name):
    ns, n = len(srcs), len(lands)

    def body(*refs):
        ins, land = refs[:ns], refs[ns:ns + n]
        send_ref, recv_ref = refs[ns + n], refs[ns + n + 1]
        x, y, c, me = _mesh_pos()
        for j in range(n):
            for k in range(1, NDEV):
                peer, _ = _peer(x, y, c, k)
                src, dst = wait_refs(j, (land[j] if src_of[j] is None else ins[src_of[j]]), land[j])
                sem = sem_rows[j] * NPEER + k - 1
                cp = _remote(src, dst, send_ref.at[sem], recv_ref.at[sem], peer)
                cp.wait_send()
                cp.wait_recv()

    outs = pl.pallas_call(
        body, name=name,
        out_shape=tuple(pltpu.HBM(a.shape, a.dtype) for a in srcs) + tuple(pltpu.HBM(a.shape, a.dtype) for a in lands),
        in_specs=(HBM,) * (ns + n) + (SEMS, SEMS, ANY),
        out_specs=(HBM,) * (ns + n),
        input_output_aliases={i: i for i in range(ns + n)},
        compiler_params=pltpu.CompilerParams(has_side_effects=EFFECT),
    )(*srcs, *lands, send, recv, after)
    return list(outs[:ns]), list(outs[ns:])


class _Comm:
    def __init__(self, shards, me, deps=()):
        names = AG_ORDER
        rows = [SEC_ROWS[n] for n in names]
        self.me = me
        lands = [lax.dynamic_update_slice(lax.empty((NDEV * r, D), BF16), shards[n], (_shard_pos(n, me), 0))
                 for n, r in zip(names, rows)]

        def copy_refs(j, src, land, me, pid):
            own = land.at[pl.ds(pl.multiple_of(_shard_pos(names[j], me), 16), rows[j])]
            return own, own

        self.ag_send, self.ag_recv, _, lands, self.ag_token = _split_start(
            [], [None] * len(names), lands, copy_refs, "ag_start", deps=deps)
        self.ag_land = dict(zip(names, lands))
        self.rs = []

    def weights(self, group, after):
        names = AG_GROUPS[group]
        idx = [AG_ORDER.index(n) for n in names]
        rows = [SEC_ROWS[n] for n in names]

        def wait_refs(j, src, land):
            return land.at[pl.ds(0, rows[j])], land.at[pl.ds(0, rows[j])]

        _, lands = _split_wait([], [None] * len(names), [self.ag_land[n] for n in names], self.ag_send,
                               self.ag_recv, idx, wait_refs, after, f"ag_wait{group}")
        return dict(zip(names, lands))

    def send_grads(self, group, gws):
        names = RS_GROUPS[group]
        rows = [SEC_ROWS[n] for n in names]
        grads = [gws[n] for n in names]
        me = self.me
        lands = [lax.dynamic_update_slice(
            lax.empty((NDEV, r, D), BF16),
            lax.dynamic_slice(g, (_shard_pos(n, me), 0), (r, D))[None], (me, 0, 0))
            for n, r, g in zip(names, rows, grads)]

        def copy_refs(j, src, land, me, pid):
            return src.at[pl.ds(pl.multiple_of(_shard_pos(names[j], pid), 16), rows[j])], land.at[me]

        send, recv, srcs, lands, token = _split_start(grads, list(range(len(names))), lands, copy_refs,
                                                      f"rs_start{group}")
        self.rs.append((names, rows, send, recv, srcs, lands))
        return token

    def received(self, group, after):
        names, rows, send, recv, srcs, lands = self.rs[group]

        def wait_refs(j, src, land):
            return src.at[pl.ds(0, rows[j])], land.at[0]

        _, lands = _split_wait(srcs, list(range(len(names))), lands, send, recv, list(range(len(names))), wait_refs,
                               after, f"rs_wait{group}")
        return dict(zip(names, lands))


def _adamw(R, w, m, v, *, rows, tr, name, off=0):
    c1 = 1.0 / (1.0 - ADAM_B1 ** ADAM_STEP)
    c2 = 1.0 / (1.0 - ADAM_B2 ** ADAM_STEP)

    def kern(r_ref, w_ref, m_ref, v_ref, g_out, d_out, m_out, v_out):
        g = r_ref[0].astype(F32)
        for dev in range(1, NDEV):
            g = g + r_ref[dev].astype(F32)
        mn = ADAM_B1 * m_ref[...] + (1.0 - ADAM_B1) * g
        vn = ADAM_B2 * v_ref[...] + (1.0 - ADAM_B2) * (g * g)
        g_out[...] = g
        m_out[...] = mn
        v_out[...] = vn
        d_out[...] = -ADAM_LR * ((mn * c1) / (jnp.sqrt(vn * c2) + ADAM_EPS) + ADAM_WD * w_ref[...])

    tile = pl.BlockSpec((tr, D), lambda i: (i, 0))
    wtile = pl.BlockSpec((tr, D), lambda i: (i + off, 0))
    shp = jax.ShapeDtypeStruct((rows, D), F32)
    return pl.pallas_call(
        kern, grid=(rows // tr,),
        in_specs=[pl.BlockSpec((NDEV, tr, D), lambda i: (0, i, 0)), wtile, wtile, wtile],
        out_specs=[tile] * 4, out_shape=[shp] * 4, compiler_params=_cparams(1), name=name)(R, w, m, v)


ADAM_TILE = {"qkv": 384, "wo": 128, "wpi": 128, "gu0": 352, "gu1": 352, "d0": 352, "d1": 352, "pg": 32}


def _pack_sections(w_qkv, w_attn_out, w_pool_in, w_pool_group, w_ffn_gate_up, w_ffn_down):
    pg = w_pool_group[0].transpose(1, 0, 2).reshape(SEC_ROWS["pg"], D)
    return {"qkv": w_qkv[0].T, "wo": w_attn_out[0], "wpi": w_pool_in[0], "gu0": w_ffn_gate_up[0].T,
            "gu1": w_ffn_gate_up[1].T, "d0": w_ffn_down[0], "d1": w_ffn_down[1], "pg": pg}


def _unpack(p):
    def sec(n):
        return p[n]
    w_qkv = sec("qkv").T[None]
    w_attn_out = sec("wo")[None]
    w_pool_in = sec("wpi")[None]
    w_pool_group = sec("pg").reshape(SEC_ROWS["pg"], POOL_G, PGD).transpose(1, 0, 2)[None]
    w_gu = jnp.stack([sec("gu0").T, sec("gu1").T])
    w_d = jnp.stack([sec("d0"), sec("d1")])
    return w_qkv, w_attn_out, w_pool_in, w_pool_group, w_gu, w_d


def _vec_pack(attn_norm, ffn_norm, final_norm, pool_norm_sh, pool_scale_sh, me):
    def place(sh):
        return lax.dynamic_update_slice(jnp.zeros((1, D), F32), sh, (0, me * 128))
    return jnp.concatenate([attn_norm, ffn_norm, final_norm.reshape(1, D), place(pool_norm_sh),
                            place(pool_scale_sh), jnp.zeros((2, D), F32)], axis=0)


def _vec_unpack(p, me):
    def take(r):
        return lax.dynamic_slice(p[r:r + 1], (0, me * 128), (1, 128))
    return p[0:1], p[1:3], p[3], take(4), take(5)


def kernel(x, attn_norm, w_qkv, w_attn_out, pool_norm, w_pool_in, w_pool_group, pool_scale, ffn_norm, w_ffn_gate_up, w_ffn_down, final_norm, loss_target, m_attn_norm, m_w_qkv, m_w_attn_out, m_pool_norm, m_w_pool_in, m_w_pool_group, m_pool_scale, m_ffn_norm, m_w_ffn_gate_up, m_w_ffn_down, m_final_norm, v_attn_norm, v_w_qkv, v_w_attn_out, v_pool_norm, v_w_pool_in, v_w_pool_group, v_pool_scale, v_ffn_norm, v_w_ffn_gate_up, v_w_ffn_down, v_final_norm):
    me = 4 * lax.axis_index("x") + 2 * lax.axis_index("y") + lax.axis_index("c")

    pw = _pack_sections(w_qkv, w_attn_out, w_pool_in, w_pool_group, w_ffn_gate_up, w_ffn_down)
    pm = _pack_sections(m_w_qkv, m_w_attn_out, m_w_pool_in, m_w_pool_group, m_w_ffn_gate_up, m_w_ffn_down)
    pv = _pack_sections(v_w_qkv, v_w_attn_out, v_w_pool_in, v_w_pool_group, v_w_ffn_gate_up, v_w_ffn_down)
    vsh = jnp.concatenate([pool_norm, pool_scale, jnp.zeros((6, 128), F32)], axis=0)

    vg = _bcast_all(vsh, "gather_pool_vectors")
    comm = _Comm({n: pw[n].astype(BF16) for n, _ in SECTIONS}, me, deps=(vg,))
    pool_norm_full = vg[:, 0, :].reshape(1, D)
    pool_scale_full = vg[:, 1, :].reshape(1, D)

    grad_x, vec = _local_step(x[0], loss_target[0], comm, attn_norm, ffn_norm, final_norm,
                              pool_norm_full, pool_scale_full)

    vw = _vec_pack(attn_norm, ffn_norm, final_norm, pool_norm, pool_scale, me)
    vm = _vec_pack(m_attn_norm, m_ffn_norm, m_final_norm, m_pool_norm, m_pool_scale, me)
    vv = _vec_pack(v_attn_norm, v_ffn_norm, v_final_norm, v_pool_norm, v_pool_scale, me)

    sec_out = [{}, {}, {}, {}]
    vec_out = None
    after = grad_x
    for group in range(len(RS_GROUPS)):
        if group == len(RS_GROUPS) - 1:
            VR = _bcast_all(vec, "exchange_vector_grads", deps=(after,))
            vec_out = _adamw(VR, vw, vm, vv, rows=8, tr=8, name="adamw_vec")
            after = vec_out[0]
        for n, R in comm.received(group, after).items():
            tr = ADAM_TILE[n]
            res = _adamw(R, pw[n], pm[n], pv[n], rows=SEC_ROWS[n], tr=tr, name=f"adamw_{n}")
            for kind in range(4):
                sec_out[kind][n] = res[kind]
            after = res[0]

    outs = []
    for kind in range(4):
        q, o, pi, pg, gu, dn = _unpack(sec_out[kind])
        an, fn, fin, pn, ps = _vec_unpack(vec_out[kind], me)
        outs.append((an, q, o, pn, pi, pg, ps, fn, gu, dn, fin))
    loss = 0.5 * jnp.sum(vec_out[0][6]) / D
    return (loss, grad_x[None]) + outs[0] + outs[1] + outs[2] + outs[3]
```

```python
import jax
import jax.numpy as jnp
from jax import lax
from jax.experimental import pallas as pl
from jax.experimental.pallas import tpu as pltpu

F32 = jnp.float32
BF16 = jnp.bfloat16

D = 1024
NDEV = 8
HEADS = 8
HD = 128
QB = 128
NGROUPS = 3
DILS = (1, 4, 16)
DFF = 2816
HCH = 1408
POOL_G = 4
PGD = 256
RMS_EPS = 1e-6
NEG = -1e30

ADAM_LR = 0.001
ADAM_B1 = 0.9
ADAM_B2 = 0.999
ADAM_EPS = 1e-08
ADAM_WD = 0.01
ADAM_STEP = 10

VMEM_LIMIT = 52 * 1024 * 1024

SECTIONS = (("qkv", 1152), ("wo", 128), ("wpi", 128), ("gu0", 704), ("gu1", 704),
            ("d0", 352), ("d1", 352), ("pg", 32))
LOC_OFF = {}
GLB_OFF = {}
_o = 0
for _n, _r in SECTIONS:
    LOC_OFF[_n] = _o
    GLB_OFF[_n] = _o * NDEV
    _o += _r
PACK_ROWS = _o
GLB_ROWS = PACK_ROWS * NDEV
SEC_ROWS = dict(SECTIONS)


def _cparams(n_grid):
    return pltpu.CompilerParams(dimension_semantics=("arbitrary",) * n_grid, vmem_limit_bytes=VMEM_LIMIT)


def _shard_pos(name, dev):
    n = SEC_ROWS[name]
    if name in ("gu0", "gu1"):
        return ((dev % 4) // 2) * (2 * HCH) + (dev // 4) * HCH + (dev % 2) * n
    return dev * n


def _mm(a, b, *, mode, M, N, K, tm, tn, tk, out_dtype, name, a_off=(0, 0), b_off=(0, 0), res=None,
        out_rows=None, out_off=0, out_prev=None, deps=()):
    nm, nn, nk = M // tm, N // tn, K // tk
    assert nm * tm == M and nn * tn == N and nk * tk == K
    if mode == "nn":
        a_bs, b_bs = (tm, tk), (tk, tn)
        a_ix = lambda i, j, k: (i, k)
        b_ix = lambda i, j, k: (k, j)
        dims = (((1,), (0,)), ((), ()))
    elif mode == "nt":
        a_bs, b_bs = (tm, tk), (tn, tk)
        a_ix = lambda i, j, k: (i, k)
        b_ix = lambda i, j, k: (j, k)
        dims = (((1,), (1,)), ((), ()))
    else:
        a_bs, b_bs = (tk, tm), (tk, tn)
        a_ix = lambda i, j, k: (k, i)
        b_ix = lambda i, j, k: (k, j)
        dims = (((0,), (0,)), ((), ()))

    def spec(bs, ix, off):
        def im(i, j, k):
            r, c = ix(i, j, k)
            return (r + off[0], c + off[1])
        return pl.BlockSpec(bs, im)

    in_specs = [spec(a_bs, a_ix, a_off), spec(b_bs, b_ix, b_off)]
    args = [a, b]
    if res is not None:
        in_specs.append(pl.BlockSpec((tm, tn), lambda i, j, k: (i, j)))
        args.append(res)
    out_shape = jax.ShapeDtypeStruct((M if out_rows is None else out_rows, N), out_dtype)
    out_spec = pl.BlockSpec((tm, tn), lambda i, j, k: (i + out_off, j))
    has_res = res is not None
    extra = list(deps) + ([out_prev] if out_prev is not None else [])
    for dep in extra:
        in_specs.append(pl.BlockSpec(memory_space=pl.ANY))
        args.append(dep)
    o_pos = 2 + int(has_res) + len(extra)
    aliases = {len(args) - 1: 0} if out_prev is not None else {}

    def kern(*refs):
        a_ref, b_ref = refs[0], refs[1]
        res_ref = refs[2] if has_res else None
        o_ref = refs[o_pos]
        av = a_ref[...]
        bv = b_ref[...]
        if av.dtype != BF16:
            av = av.astype(BF16)
        if bv.dtype != BF16:
            bv = bv.astype(BF16)
        part = lax.dot_general(av, bv, dims, preferred_element_type=F32)

        def write(val):
            if has_res:
                val = val + res_ref[...]
            o_ref[...] = val.astype(out_dtype)

        if nk == 1:
            write(part)
        else:
            acc_ref = refs[-1]
            k = pl.program_id(2)

            @pl.when(k == 0)
            def _():
                acc_ref[...] = part

            @pl.when(k > 0)
            def _():
                acc_ref[...] += part

            @pl.when(k == nk - 1)
            def _():
                write(acc_ref[...])

    scratch = [pltpu.VMEM((tm, tn), F32)] if nk > 1 else []
    return pl.pallas_call(
        kern, grid=(nm, nn, nk), in_specs=in_specs, out_specs=out_spec, out_shape=out_shape,
        scratch_shapes=scratch, input_output_aliases=aliases, compiler_params=_cparams(3), name=name)(*args)


def _rms_fwd(x, g, name, deps=()):
    S = x.shape[0]
    tr = 512

    def kern(x_ref, g_ref, *rest):
        h_ref = rest[-1]
        xv = x_ref[...]
        r = lax.rsqrt(jnp.mean(xv * xv, axis=-1, keepdims=True) + RMS_EPS)
        h_ref[...] = (xv * r * g_ref[...]).astype(BF16)

    return pl.pallas_call(
        kern, grid=(S // tr,),
        in_specs=[pl.BlockSpec((tr, D), lambda i: (i, 0)), pl.BlockSpec((1, D), lambda i: (0, 0))]
        + [pl.BlockSpec(memory_space=pl.ANY)] * len(deps),
        out_specs=pl.BlockSpec((tr, D), lambda i: (i, 0)),
        out_shape=jax.ShapeDtypeStruct((S, D), BF16), compiler_params=_cparams(1), name=name)(x, g, *deps)


def _chunks_put(scr, val):
    for c in range(scr.shape[0]):
        scr[c] = val[:, c * 128:(c + 1) * 128]


def _chunks_get(scr):
    return jnp.concatenate([scr[c] for c in range(scr.shape[0])], axis=1)


def _chunks_rows(scr, r, n, dil):
    return jnp.concatenate([scr.at[c][pl.ds(r, n, stride=dil), :] for c in range(scr.shape[0])], axis=1)


def _chunks_add_rows(scr, val, r, n, dil, accumulate):
    for c in range(scr.shape[0]):
        rows = pl.ds(r, n, stride=dil)
        piece = val[:, c * 128:(c + 1) * 128]
        tile = scr.at[c]
        tile[rows, :] = tile[rows, :] + piece if accumulate else piece


def _rms_fwd_folded(x, g, name, deps=()):
    S = x.shape[0]
    tr = 512
    dils = DILS[1:]

    def kern(x_ref, g_ref, *rest):
        outs, scr = rest[len(deps):-1], rest[-1]
        xv = x_ref[...]
        r = lax.rsqrt(jnp.mean(xv * xv, axis=-1, keepdims=True) + RMS_EPS)
        h = (xv * r * g_ref[...]).astype(BF16)
        outs[0][...] = h
        _chunks_put(scr, h.astype(F32))
        for o_ref, dil in zip(outs[1:], dils):
            for res in range(dil):
                o_ref[res] = _chunks_rows(scr, res, tr // dil, dil).astype(BF16)

    return pl.pallas_call(
        kern, grid=(S // tr,),
        in_specs=[pl.BlockSpec((tr, D), lambda i: (i, 0)), pl.BlockSpec((1, D), lambda i: (0, 0))]
        + [pl.BlockSpec(memory_space=pl.ANY)] * len(deps),
        out_specs=[pl.BlockSpec((tr, D), lambda i: (i, 0))]
        + [pl.BlockSpec((dil, tr // dil, D), lambda i: (0, i, 0)) for dil in dils],
        out_shape=[jax.ShapeDtypeStruct((S, D), BF16)]
        + [jax.ShapeDtypeStruct((dil, S // dil, D), BF16) for dil in dils],
        scratch_shapes=[pltpu.VMEM((D // 128, tr, 128), F32)],
        compiler_params=_cparams(1), name=name)(x, g, *deps)


def _rms_bwd(dh, x, g, dres, name, folded=()):
    S = x.shape[0]
    tr = 512
    nf = len(folded)

    def kern(dh_ref, *rest):
        f_refs = rest[:nf]
        x_ref, g_ref, dres_ref, dx_ref, dg_ref = rest[nf:nf + 5]
        i = pl.program_id(0)
        xv = x_ref[...]
        if nf:
            acc_ref = rest[nf + 5]
            _chunks_put(acc_ref, dh_ref[...].astype(F32))
            for f_ref in f_refs:
                dil = f_ref.shape[0]
                for res in range(dil):
                    _chunks_add_rows(acc_ref, f_ref[res], res, tr // dil, dil, True)
            dhv = _chunks_get(acc_ref)
        else:
            dhv = dh_ref[...].astype(F32)
        r = lax.rsqrt(jnp.mean(xv * xv, axis=-1, keepdims=True) + RMS_EPS)
        xhat = xv * r
        gy = dhv * g_ref[...]
        dx_ref[...] = dres_ref[...] + r * (gy - xhat * jnp.mean(gy * xhat, axis=-1, keepdims=True))
        part = jnp.sum(dhv * xhat, axis=0, keepdims=True)

        @pl.when(i == 0)
        def _():
            dg_ref[...] = part

        @pl.when(i > 0)
        def _():
            dg_ref[...] += part

    row = pl.BlockSpec((tr, D), lambda i: (i, 0))
    vec = pl.BlockSpec((1, D), lambda i: (0, 0))
    fspecs = [pl.BlockSpec((f.shape[0], tr // f.shape[0], D), lambda i: (0, i, 0)) for f in folded]
    return pl.pallas_call(
        kern, grid=(S // tr,), in_specs=[row] + fspecs + [row, vec, row], out_specs=[row, vec],
        out_shape=[jax.ShapeDtypeStruct((S, D), F32), jax.ShapeDtypeStruct((1, D), F32)],
        scratch_shapes=[pltpu.VMEM((D // 128, tr, 128), F32)] if nf else [],
        compiler_params=_cparams(1), name=name)(dh, *folded, x, g, dres)


def _loss_head(x, g, tgt):
    S = x.shape[0]
    tr = 512

    def kern(x_ref, g_ref, t_ref, dx_ref, dg_ref, ls_ref):
        i = pl.program_id(0)
        xv = x_ref[...]
        gv = g_ref[...]
        r = lax.rsqrt(jnp.mean(xv * xv, axis=-1, keepdims=True) + RMS_EPS)
        xhat = xv * r
        e = xhat * gv - t_ref[...]
        dy = e * (1.0 / D)
        gy = dy * gv
        dx_ref[...] = r * (gy - xhat * jnp.mean(gy * xhat, axis=-1, keepdims=True))
        dgp = jnp.sum(dy * xhat, axis=0, keepdims=True)
        lsp = jnp.sum(e * e, axis=0, keepdims=True)

        @pl.when(i == 0)
        def _():
            dg_ref[...] = dgp
            ls_ref[...] = lsp

        @pl.when(i > 0)
        def _():
            dg_ref[...] += dgp
            ls_ref[...] += lsp

    row = pl.BlockSpec((tr, D), lambda i: (i, 0))
    vec = pl.BlockSpec((1, D), lambda i: (0, 0))
    return pl.pallas_call(
        kern, grid=(S // tr,), in_specs=[row, vec, row], out_specs=[row, vec, vec],
        out_shape=[jax.ShapeDtypeStruct((S, D), F32), jax.ShapeDtypeStruct((1, D), F32),
                   jax.ShapeDtypeStruct((1, D), F32)],
        compiler_params=_cparams(1), name="loss_head")(x, g, tgt)


def _ffn_up(h, G, name):
    S = h.shape[0]
    tm = 512
    nj = DFF // HCH

    def kern(h_ref, w_ref, gu_ref, act_ref):
        gu = lax.dot_general(h_ref[...], w_ref[...], (((1,), (1,)), ((), ())), preferred_element_type=F32)
        gu_ref[...] = gu.astype(BF16)
        gate = gu[:, :HCH]
        up = gu[:, HCH:]
        act_ref[...] = (gate * jax.nn.sigmoid(gate) * up).astype(BF16)

    return pl.pallas_call(
        kern, grid=(nj, S // tm),
        in_specs=[pl.BlockSpec((tm, D), lambda j, i: (i, 0)),
                  pl.BlockSpec((2 * HCH, D), lambda j, i: (j, 0))],
        out_specs=[pl.BlockSpec((tm, 2 * HCH), lambda j, i: (i, j)),
                   pl.BlockSpec((tm, HCH), lambda j, i: (i, j))],
        out_shape=[jax.ShapeDtypeStruct((S, 2 * DFF), BF16), jax.ShapeDtypeStruct((S, DFF), BF16)],
        compiler_params=_cparams(2), name=name)(h, G)


def _ffn_down_bwd(dx, G, gu, name):
    S = dx.shape[0]
    tm = 512
    nj = DFF // HCH

    def kern(dx_ref, w_ref, gu_ref, o_ref):
        dact = lax.dot_general(dx_ref[...].astype(BF16), w_ref[...], (((1,), (1,)), ((), ())),
                               preferred_element_type=F32)
        gate = gu_ref[:, :HCH].astype(F32)
        up = gu_ref[:, HCH:].astype(F32)
        sig = jax.nn.sigmoid(gate)
        silu = gate * sig
        o_ref[:, :HCH] = (dact * up * (sig * (1.0 + gate * (1.0 - sig)))).astype(BF16)
        o_ref[:, HCH:] = (dact * silu).astype(BF16)

    return pl.pallas_call(
        kern, grid=(nj, S // tm),
        in_specs=[pl.BlockSpec((tm, D), lambda j, i: (i, 0)),
                  pl.BlockSpec((HCH, D), lambda j, i: (j, 0)),
                  pl.BlockSpec((tm, 2 * HCH), lambda j, i: (i, j))],
        out_specs=pl.BlockSpec((tm, 2 * HCH), lambda j, i: (i, j)),
        out_shape=jax.ShapeDtypeStruct((S, 2 * DFF), BF16),
        compiler_params=_cparams(2), name=name)(dx, G, gu)


def _trail(u, *, backward, name):
    S = u.shape[0]

    def kern(u_ref, o_ref):
        g = pl.program_id(0)
        uv = u_ref[...].astype(F32)
        row = lax.broadcasted_iota(jnp.int32, uv.shape, 0)
        win = jnp.left_shift(jnp.int32(2), g)
        cnt = jnp.minimum(row + 1, win).astype(F32)
        s = uv / cnt if backward else uv
        levels = []
        for k in (1, 2, 4, 8):
            if backward:
                sh = jnp.where(row < S - k, pltpu.roll(s, S - k, 0), 0.0)
            else:
                sh = jnp.where(row >= k, pltpu.roll(s, k, 0), 0.0)
            s = s + sh
            levels.append(s)
        sel = jnp.where(g == 0, levels[0], jnp.where(g == 1, levels[1], jnp.where(g == 2, levels[2], levels[3])))
        if backward:
            o_ref[...] = (sel - uv).astype(BF16)
        else:
            o_ref[...] = (sel / cnt - uv).astype(BF16)

    blk = pl.BlockSpec((S, PGD), lambda g: (0, g))
    return pl.pallas_call(
        kern, grid=(POOL_G,), in_specs=[blk], out_specs=blk,
        out_shape=jax.ShapeDtypeStruct((S, D), BF16), compiler_params=_cparams(1), name=name)(u)


def _pool_out(yd, G, scale, xres):
    S = yd.shape[0]
    tm = 1024

    def kern(y_ref, w_ref, s_ref, x_ref, o_ref):
        z = jnp.dot(y_ref[...], w_ref[...], preferred_element_type=F32)
        o_ref[...] = x_ref[...] + z * s_ref[...]

    tile = pl.BlockSpec((tm, PGD), lambda i, g: (i, g))
    return pl.pallas_call(
        kern, grid=(S // tm, POOL_G),
        in_specs=[tile, pl.BlockSpec((PGD, PGD), lambda i, g: (0, g)),
                  pl.BlockSpec((1, PGD), lambda i, g: (0, g)), tile],
        out_specs=tile, out_shape=jax.ShapeDtypeStruct((S, D), F32),
        compiler_params=_cparams(2), name="pool_out")(yd, G, scale, xres)


def _pool_out_bwd(dz, yd, G, scale):
    S = yd.shape[0]
    tm = 1024
    ni = S // tm

    def kern(dz_ref, y_ref, w_ref, s_ref, dy_ref, ds_ref, dw_ref, acc_ref):
        i = pl.program_id(1)
        dzv = dz_ref[...]
        yv = y_ref[...]
        wv = w_ref[...]
        zraw = jnp.dot(yv, wv, preferred_element_type=F32)
        dsp = jnp.sum(dzv * zraw, axis=0, keepdims=True)
        dzr = (dzv * s_ref[...]).astype(BF16)
        dy_ref[...] = lax.dot_general(dzr, wv, (((1,), (1,)), ((), ())), preferred_element_type=F32)
        dwp = lax.dot_general(yv, dzr, (((0,), (0,)), ((), ())), preferred_element_type=F32)

        @pl.when(i == 0)
        def _():
            ds_ref[...] = dsp
            acc_ref[...] = dwp

        @pl.when(i > 0)
        def _():
            ds_ref[...] += dsp
            acc_ref[...] += dwp

        @pl.when(i == ni - 1)
        def _():
            dw_ref[...] = acc_ref[...].astype(BF16)

    tile = pl.BlockSpec((tm, PGD), lambda g, i: (i, g))
    return pl.pallas_call(
        kern, grid=(POOL_G, ni),
        in_specs=[tile, tile, pl.BlockSpec((PGD, PGD), lambda g, i: (0, g)),
                  pl.BlockSpec((1, PGD), lambda g, i: (0, g))],
        out_specs=[tile, pl.BlockSpec((1, PGD), lambda g, i: (0, g)),
                   pl.BlockSpec((PGD, PGD), lambda g, i: (0, g))],
        out_shape=[jax.ShapeDtypeStruct((S, D), F32), jax.ShapeDtypeStruct((1, D), F32),
                   jax.ShapeDtypeStruct((PGD, D), BF16)],
        scratch_shapes=[pltpu.VMEM((PGD, PGD), F32)],
        compiler_params=_cparams(2), name="pool_out_bwd")(dz, yd, G, scale)


def _bias_table():
    qi = jnp.arange(QB)[:, None]
    ki = jnp.arange(2 * QB)[None, :]
    delta = QB + qi - ki
    inband = (delta >= 0) & (delta <= QB)
    n = NGROUPS * HEADS
    slopes = jnp.exp2(-8.0 * jnp.arange(1, n + 1, dtype=F32) / n).reshape(NGROUPS, HEADS)
    dil = jnp.asarray(DILS, F32)
    bias = -slopes[:, :, None, None] * (delta.astype(F32)[None, None] * dil[:, None, None, None])
    return jnp.where(inband[None, None], bias, NEG)


def _attn_fwd(qkv_f, bias, nb, name):
    S = qkv_f.shape[0]
    nblk = S // QB
    scale = HD ** -0.5

    def kern(q_ref, kc_ref, kp_ref, vc_ref, vp_ref, b_ref, o_ref, l_ref, s_scr, p_scr, r_scr):
        b = pl.program_id(0)
        has_prev = jnp.bitwise_and(b, nb - 1) != 0
        col = lax.broadcasted_iota(jnp.int32, (QB, 2 * QB), 1)
        dead = jnp.logical_and(col < QB, jnp.logical_not(has_prev))
        lane = lax.broadcasted_iota(jnp.int32, (QB, HD), 1)
        lse_all = jnp.zeros((QB, HD), F32)
        for h in range(HEADS):
            sl = slice(h * HD, (h + 1) * HD)
            kk = jnp.concatenate([kp_ref[:, sl], kc_ref[:, sl]], axis=0)
            s_scr[h] = lax.dot_general(q_ref[:, sl], kk, (((1,), (1,)), ((), ())), preferred_element_type=F32)
        for h in range(HEADS):
            s = s_scr[h] * scale + b_ref[h]
            s = jnp.where(dead, NEG, s)
            m = jnp.max(s, axis=-1, keepdims=True)
            p = jnp.exp(s - m)
            den = jnp.sum(p, axis=-1, keepdims=True)
            p_scr[h] = p.astype(BF16)
            r_scr[h] = jnp.broadcast_to(1.0 / den, (QB, HD))
            lse_all = jnp.where(lane == h, m + jnp.log(den), lse_all)
        for h in range(HEADS):
            sl = slice(h * HD, (h + 1) * HD)
            vv = jnp.concatenate([vp_ref[:, sl], vc_ref[:, sl]], axis=0)
            o = jnp.dot(p_scr[h], vv, preferred_element_type=F32) * r_scr[h]
            o_ref[:, sl] = o.astype(BF16)
        l_ref[...] = lse_all

    def blk(colblk, prev):
        if prev:
            return pl.BlockSpec((QB, D), lambda b: (jnp.maximum(b - 1, 0), colblk))
        return pl.BlockSpec((QB, D), lambda b: (b, colblk))

    return pl.pallas_call(
        kern, grid=(nblk,),
        in_specs=[blk(0, False), blk(1, False), blk(1, True), blk(2, False), blk(2, True),
                  pl.BlockSpec((HEADS, QB, 2 * QB), lambda b: (0, 0, 0))],
        out_specs=[pl.BlockSpec((QB, D), lambda b: (b, 0)), pl.BlockSpec((QB, HD), lambda b: (b, 0))],
        out_shape=[jax.ShapeDtypeStruct((S, D), BF16), jax.ShapeDtypeStruct((S, HD), F32)],
        scratch_shapes=[pltpu.VMEM((HEADS, QB, 2 * QB), F32), pltpu.VMEM((HEADS, QB, 2 * QB), BF16),
                        pltpu.VMEM((HEADS, QB, HD), F32)],
        compiler_params=_cparams(1), name=name)(qkv_f, qkv_f, qkv_f, qkv_f, qkv_f, bias)


def _natural(ref, scr, tm):
    dil = ref.shape[0]
    for res in range(dil):
        _chunks_add_rows(scr, ref[res].astype(F32), res, tm // dil, dil, False)
    return _chunks_get(scr)


def _attn_merge(os, lses):
    S = os[0].shape[0]
    tm = 512

    def kern(o0, o1, o2, l0, l1, l2, om_ref, lm_ref, ls1, ls2, os1, os2):
        la = l0[...]
        lb = _natural(l1, ls1, tm)
        lc = _natural(l2, ls2, tm)
        m = jnp.maximum(jnp.maximum(la, lb), lc)
        e0, e1, e2 = jnp.exp(la - m), jnp.exp(lb - m), jnp.exp(lc - m)
        tot = e0 + e1 + e2
        lm_ref[...] = m + jnp.log(tot)
        w0, w1, w2 = e0 / tot, e1 / tot, e2 / tot
        for res in range(o1.shape[0]):
            _chunks_add_rows(os1, o1[res].astype(F32), res, tm // o1.shape[0], o1.shape[0], False)
        for res in range(o2.shape[0]):
            _chunks_add_rows(os2, o2[res].astype(F32), res, tm // o2.shape[0], o2.shape[0], False)
        for h in range(HEADS):
            sl = slice(h * HD, (h + 1) * HD)
            acc = w0[:, h:h + 1] * o0[:, sl].astype(F32) + w1[:, h:h + 1] * os1[h] + w2[:, h:h + 1] * os2[h]
            om_ref[:, sl] = acc.astype(BF16)

    def spec(a, c):
        if a.ndim == 2:
            return pl.BlockSpec((tm, c), lambda i: (i, 0))
        return pl.BlockSpec((a.shape[0], tm // a.shape[0], c), lambda i: (0, i, 0))

    return pl.pallas_call(
        kern, grid=(S // tm,),
        in_specs=[spec(a, D) for a in os] + [spec(a, HD) for a in lses],
        out_specs=[pl.BlockSpec((tm, D), lambda i: (i, 0)), pl.BlockSpec((tm, HD), lambda i: (i, 0))],
        out_shape=[jax.ShapeDtypeStruct((S, D), BF16), jax.ShapeDtypeStruct((S, HD), F32)],
        scratch_shapes=[pltpu.VMEM((1, tm, HD), F32), pltpu.VMEM((1, tm, HD), F32),
                        pltpu.VMEM((HEADS, tm, HD), F32), pltpu.VMEM((HEADS, tm, HD), F32)],
        compiler_params=_cparams(1), name="attn_merge")(*os, *lses)


def _attn_bwd_prep(do, o, lse):
    S = o.shape[0]
    tm = 512
    dils = DILS[1:]

    def kern(do_ref, o_ref, l_ref, *rest):
        do_outs, l_outs, d_outs = rest[0:3], rest[3:5], rest[5:8]
        do_scr, l_scr, d_scr = rest[8:11]
        lane = lax.broadcasted_iota(jnp.int32, (tm, HD), 1)
        acc = jnp.zeros((tm, HD), F32)
        for h in range(HEADS):
            sl = slice(h * HD, (h + 1) * HD)
            prod = do_ref[:, sl] * o_ref[:, sl].astype(F32)
            acc = jnp.where(lane == h, jnp.sum(prod, axis=-1, keepdims=True), acc)
        d_scr[0] = acc
        l_scr[0] = l_ref[...]
        _chunks_put(do_scr, do_ref[...])
        do_outs[0][...] = do_ref[...].astype(BF16)
        d_outs[0][...] = acc
        for j, dil in enumerate(dils):
            for res in range(dil):
                n = tm // dil
                do_outs[1 + j][res] = _chunks_rows(do_scr, res, n, dil).astype(BF16)
                l_outs[j][res] = _chunks_rows(l_scr, res, n, dil)
                d_outs[1 + j][res] = _chunks_rows(d_scr, res, n, dil)

    def nat(c):
        return pl.BlockSpec((tm, c), lambda i: (i, 0))

    def fol(dil, c):
        return pl.BlockSpec((dil, tm // dil, c), lambda i: (0, i, 0))

    def shapes(c, dt, with_natural):
        first = [jax.ShapeDtypeStruct((S, c), dt)] if with_natural else []
        return first + [jax.ShapeDtypeStruct((dil, S // dil, c), dt) for dil in dils]

    outs = pl.pallas_call(
        kern, grid=(S // tm,), in_specs=[nat(D), nat(D), nat(HD)],
        out_specs=[nat(D)] + [fol(dil, D) for dil in dils] + [fol(dil, HD) for dil in dils]
        + [nat(HD)] + [fol(dil, HD) for dil in dils],
        out_shape=shapes(D, BF16, True) + shapes(HD, F32, False) + shapes(HD, F32, True),
        scratch_shapes=[pltpu.VMEM((HEADS, tm, HD), F32), pltpu.VMEM((1, tm, HD), F32), pltpu.VMEM((1, tm, HD), F32)],
        compiler_params=_cparams(1), name="attn_bwd_prep")(do, o, lse)
    return outs[0:3], [lse] + list(outs[3:5]), outs[5:8]


def _attn_bwd(qkv_f, do_f, lse_f, delta_f, bias, nb, name):
    S = qkv_f.shape[0]
    nblk = S // QB
    scale = HD ** -0.5

    def kern(q_ref, kc_ref, kp_ref, vc_ref, vp_ref, do_ref, l_ref, d_ref, b_ref, out_ref, dq_c, dk_c, dv_c,
             s_scr, dp_scr, ds_scr, p_scr):
        b = pl.program_id(0)

        @pl.when(b == 0)
        def _():
            dq_c[...] = jnp.zeros_like(dq_c)
            dk_c[...] = jnp.zeros_like(dk_c)
            dv_c[...] = jnp.zeros_like(dv_c)

        @pl.when(b == nblk)
        def _():
            out_ref[:, 0:D] = dq_c[...].astype(BF16)
            out_ref[:, D:2 * D] = dk_c[...].astype(BF16)
            out_ref[:, 2 * D:3 * D] = dv_c[...].astype(BF16)

        @pl.when(b < nblk)
        def _():
            has_prev = jnp.bitwise_and(b, nb - 1) != 0
            col = lax.broadcasted_iota(jnp.int32, (QB, 2 * QB), 1)
            dead = jnp.logical_and(col < QB, jnp.logical_not(has_prev))
            out_ref[:, 0:D] = dq_c[...].astype(BF16)
            lv = l_ref[...]
            dv_ = d_ref[...]
            for h in range(HEADS):
                sl = slice(h * HD, (h + 1) * HD)
                kk = jnp.concatenate([kp_ref[:, sl], kc_ref[:, sl]], axis=0)
                vv = jnp.concatenate([vp_ref[:, sl], vc_ref[:, sl]], axis=0)
                s_scr[h] = lax.dot_general(q_ref[:, sl], kk, (((1,), (1,)), ((), ())), preferred_element_type=F32)
                dp_scr[h] = lax.dot_general(do_ref[:, sl], vv, (((1,), (1,)), ((), ())),
                                            preferred_element_type=F32)
            for h in range(HEADS):
                s = s_scr[h] * scale + b_ref[h]
                s = jnp.where(dead, NEG, s)
                p = jnp.exp(s - lv[:, h:h + 1])
                ds_scr[h] = (p * (dp_scr[h] - dv_[:, h:h + 1]) * scale).astype(BF16)
                p_scr[h] = p.astype(BF16)
            for h in range(HEADS):
                sl = slice(h * HD, (h + 1) * HD)
                kk = jnp.concatenate([kp_ref[:, sl], kc_ref[:, sl]], axis=0)
                ds = ds_scr[h]
                dq_c[:, sl] = jnp.dot(ds, kk, preferred_element_type=F32)
                dkk = lax.dot_general(ds, q_ref[:, sl], (((0,), (0,)), ((), ())), preferred_element_type=F32)
                dvv = lax.dot_general(p_scr[h], do_ref[:, sl], (((0,), (0,)), ((), ())),
                                      preferred_element_type=F32)
                out_ref[:, D + h * HD:D + (h + 1) * HD] = (dk_c[:, sl] + dkk[:QB]).astype(BF16)
                out_ref[:, 2 * D + h * HD:2 * D + (h + 1) * HD] = (dv_c[:, sl] + dvv[:QB]).astype(BF16)
                dk_c[:, sl] = dkk[QB:]
                dv_c[:, sl] = dvv[QB:]

    last = nblk - 1

    def blk(colblk, prev):
        if prev:
            return pl.BlockSpec((QB, D), lambda b: (jnp.maximum(jnp.minimum(b, last) - 1, 0), colblk))
        return pl.BlockSpec((QB, D), lambda b: (jnp.minimum(b, last), colblk))

    stat = pl.BlockSpec((QB, HD), lambda b: (jnp.minimum(b, last), 0))
    return pl.pallas_call(
        kern, grid=(nblk + 1,),
        in_specs=[blk(0, False), blk(1, False), blk(1, True), blk(2, False), blk(2, True),
                  pl.BlockSpec((QB, D), lambda b: (jnp.minimum(b, last), 0)), stat, stat,
                  pl.BlockSpec((HEADS, QB, 2 * QB), lambda b: (0, 0, 0))],
        out_specs=pl.BlockSpec((QB, 3 * D), lambda b: (jnp.maximum(b - 1, 0), 0)),
        out_shape=jax.ShapeDtypeStruct((S, 3 * D), BF16),
        scratch_shapes=[pltpu.VMEM((QB, D), F32), pltpu.VMEM((QB, D), F32), pltpu.VMEM((QB, D), F32),
                        pltpu.VMEM((HEADS, QB, 2 * QB), F32), pltpu.VMEM((HEADS, QB, 2 * QB), F32),
                        pltpu.VMEM((HEADS, QB, 2 * QB), BF16), pltpu.VMEM((HEADS, QB, 2 * QB), BF16)],
        compiler_params=_cparams(1), name=name)(qkv_f, qkv_f, qkv_f, qkv_f, qkv_f, do_f, lse_f, delta_f, bias)


def _local_step(x, tgt, comm, attn_norm, ffn_norm, final_norm, pool_norm, pool_scale):
    S = x.shape[0]
    bias = _bias_table()
    g_attn = attn_norm.reshape(1, D)
    g_f0 = ffn_norm[0:1]
    g_f1 = ffn_norm[1:2]
    g_fin = final_norm.reshape(1, D)
    W = {}

    def ffn_fwd(xin, gain, l):
        h = _rms_fwd(xin, gain, f"rms_ffn{l}")
        gu, act = _ffn_up(h, W[f"gu{l}"], f"ffn_up{l}")
        xo = _mm(act, W[f"d{l}"], mode="nn", M=S, N=D, K=DFF, tm=512, tn=D, tk=DFF, out_dtype=F32,
                 res=xin, name=f"ffn_down{l}")
        return h, gu, act, xo

    def ffn_bwd(dxo, xin, gain, h, gu, act, l, rs_group):
        dgu = _ffn_down_bwd(dxo, W[f"d{l}"], gu, f"ffn_down_bwd{l}")
        gw_d = _mm(act, dxo, mode="tn", M=DFF, N=D, K=S, tm=HCH, tn=D, tk=1024, out_dtype=BF16, name=f"gw_d{l}")
        gw_gu = _mm(dgu, h, mode="tn", M=2 * DFF, N=D, K=S, tm=HCH, tn=D, tk=1024, out_dtype=BF16, name=f"gw_gu{l}")
        token = comm.send_grads(rs_group, {f"d{l}": gw_d, f"gu{l}": gw_gu})
        dh = _mm(dgu, W[f"gu{l}"], mode="nn", M=S, N=D, K=2 * DFF, tm=1024, tn=D, tk=HCH,
                 out_dtype=F32, deps=(token,), name=f"ffn_up_bwd{l}")
        dxin, dgain = _rms_bwd(dh, xin, gain, dxo, f"rms_ffn_bwd{l}")
        return dxin, dgain

    nbs = [S // QB // dil for dil in DILS]
    hf = _rms_fwd_folded(x, g_attn, "rms_attn", deps=(comm.ag_token,))
    hf = [h.reshape(S, D) for h in hf]
    W.update(comm.weights(0, hf[0]))
    qkv_f, o_f, lse_f = [], [], []
    for g, dil in enumerate(DILS):
        qkv_f.append(_mm(hf[g], W["qkv"], mode="nt", M=S, N=3 * D, K=D, tm=1024, tn=1024, tk=D, out_dtype=BF16,
                         b_off=(3 * g, 0), name=f"qkv_proj{g}"))
        og, lg = _attn_fwd(qkv_f[g], bias[g], nbs[g], f"attn_fwd{g}")
        o_f.append(og if dil == 1 else og.reshape(dil, S // dil, D))
        lse_f.append(lg if dil == 1 else lg.reshape(dil, S // dil, HD))
    o, lse = _attn_merge(o_f, lse_f)
    W.update(comm.weights(1, o))
    x1 = _mm(o, W["wo"], mode="nn", M=S, N=D, K=D, tm=1024, tn=D, tk=D, out_dtype=F32, res=x, name="attn_out")
    h1, gu0, act0, x2 = ffn_fwd(x1, g_f0, 0)

    W.update(comm.weights(2, x2))
    h2 = _rms_fwd(x2, pool_norm, "rms_pool")
    u = _mm(h2, W["wpi"], mode="nn", M=S, N=D, K=D, tm=1024, tn=D, tk=D, out_dtype=F32, name="pool_in")
    yd = _trail(u, backward=False, name="trail_fwd")
    x3 = _pool_out(yd, W["pg"], pool_scale, x2)
    h3, gu1, act1, x4 = ffn_fwd(x3, g_f1, 1)

    dx4, d_fin, lossvec = _loss_head(x4, g_fin, tgt)

    dx3, d_f1 = ffn_bwd(dx4, x3, g_f1, h3, gu1, act1, 1, 0)
    dyd, d_scale, gw_pg = _pool_out_bwd(dx3, yd, W["pg"], pool_scale)
    du = _trail(dyd, backward=True, name="trail_bwd")
    gw_pi = _mm(h2, du, mode="tn", M=D, N=D, K=S, tm=D, tn=D, tk=1024, out_dtype=BF16, name="gw_pi")
    token = comm.send_grads(1, {"pg": gw_pg, "wpi": gw_pi})
    dh2 = _mm(du, W["wpi"], mode="nt", M=S, N=D, K=D, tm=1024, tn=D, tk=D, out_dtype=F32,
              deps=(token,), name="pool_in_bwd")
    dx2, d_pool = _rms_bwd(dh2, x2, pool_norm, dx3, "rms_pool_bwd")
    dx1, d_f0 = ffn_bwd(dx2, x1, g_f0, h1, gu0, act0, 0, 2)

    gw_o = _mm(o, dx1, mode="tn", M=D, N=D, K=S, tm=D, tn=D, tk=1024, out_dtype=BF16, name="gw_o")
    do = _mm(dx1, W["wo"], mode="nt", M=S, N=D, K=D, tm=1024, tn=D, tk=D, out_dtype=F32, name="attn_out_bwd")
    do_f, lse_ff, delta_f = _attn_bwd_prep(do, o, lse)
    dqkv_f, gw_qkv = [], None
    for g in range(NGROUPS):
        dqkv_f.append(_attn_bwd(qkv_f[g], do_f[g].reshape(S, D), lse_ff[g].reshape(S, HD),
                                delta_f[g].reshape(S, HD), bias[g], nbs[g], f"attn_bwd{g}"))
        gw_qkv = _mm(dqkv_f[g], hf[g], mode="tn", M=3 * D, N=D, K=S, tm=1024, tn=D, tk=1024, out_dtype=BF16,
                     out_rows=NGROUPS * 3 * D, out_off=3 * g, out_prev=gw_qkv, name=f"gw_qkv{g}")
    token = comm.send_grads(3, {"wo": gw_o, "qkv": gw_qkv})
    dh0_f = [_mm(dqkv_f[g], W["qkv"], mode="nn", M=S, N=D, K=3 * D, tm=1024, tn=D, tk=1024, out_dtype=F32,
                 b_off=(3 * g, 0), deps=(token,), name=f"qkv_proj_bwd{g}") for g in range(NGROUPS)]
    folded = [dh0_f[g].reshape(dil, S // dil, D) for g, dil in enumerate(DILS) if dil > 1]
    grad_x, d_attn = _rms_bwd(dh0_f[0], x, g_attn, dx1, "rms_attn_bwd", folded=folded)

    vec = jnp.concatenate([d_attn, d_f0, d_f1, d_fin, d_pool, d_scale, lossvec, jnp.zeros((1, D), F32)], axis=0)
    return grad_x, vec


def _mesh_pos():
    x, y, c = lax.axis_index("x"), lax.axis_index("y"), lax.axis_index("c")
    return x, y, c, 4 * x + 2 * y + c


def _peer(x, y, c, k):
    kx, ky, kc = (k >> 2) & 1, (k >> 1) & 1, k & 1
    px = 1 - x if kx else x
    py = 1 - y if ky else y
    pc = 1 - c if kc else c
    return (px, py, pc), 4 * px + 2 * py + pc


ANY = pl.BlockSpec(memory_space=pl.ANY)


HBM = pl.BlockSpec(memory_space=pltpu.HBM)
SEMS = pl.BlockSpec(memory_space=pltpu.SEMAPHORE)
EFFECT = pltpu.SideEffectType.DATAFLOW_SIDE_EFFECTING
NPEER = NDEV - 1

AG_GROUPS = (("qkv",), ("wo", "gu0", "d0"), ("wpi", "pg", "gu1", "d1"))
AG_ORDER = tuple(n for grp in AG_GROUPS for n in grp)
RS_GROUPS = (("d1", "gu1"), ("pg", "wpi"), ("d0", "gu0"), ("wo", "qkv"))


def _hbm(a):
    return pltpu.with_memory_space_constraint(a, pltpu.HBM)


def _remote(src, dst, send, recv, peer):
    return pltpu.make_async_remote_copy(src_ref=src, dst_ref=dst, send_sem=send, recv_sem=recv, device_id=peer,
                                        device_id_type=pl.DeviceIdType.MESH)


def _bcast_all(v, name, deps=()):
    W = v.shape[1]
    nd = len(deps)

    def kern(v_ref, *rest):
        o_ref, send, recv, lsem = rest[nd:]
        x, y, c, me = _mesh_pos()
        own = pltpu.make_async_copy(v_ref, o_ref.at[me], lsem)
        own.start()
        cps = [_remote(v_ref, o_ref.at[me], send.at[k - 1], recv.at[k - 1], _peer(x, y, c, k)[0])
               for k in range(1, NDEV)]
        for cp in cps:
            cp.start()
        for cp in cps:
            cp.wait_recv()
            cp.wait_send()
        own.wait()

    return pl.pallas_call(
        kern, in_specs=[ANY] * (1 + nd), out_specs=ANY, out_shape=jax.ShapeDtypeStruct((NDEV, 8, W), F32),
        scratch_shapes=[pltpu.SemaphoreType.DMA((NPEER,)), pltpu.SemaphoreType.DMA((NPEER,)),
                        pltpu.SemaphoreType.DMA(())],
        name=name)(v, *deps)


ALL_KS = tuple(range(1, NDEV))
AG_KS1 = (1, 2, 4, 6)
AG_KS2 = (2, 4, 6)


def _split_start(srcs, src_of, lands, copy_refs, name, deps=(), ks=ALL_KS, to=None):
    ns, n, nd, nk = len(srcs), len(lands), len(deps), len(ks)

    def body(*refs):
        ins, land = refs[:ns], refs[ns:ns + n]
        send, recv = refs[ns + n + nd], refs[ns + n + nd + 1]
        token = refs[-1]
        x, y, c, me = _mesh_pos()
        for j in range(n):
            for i, k in enumerate(ks):
                _, pid = _peer(x, y, c, k)
                dest, _ = _peer(x, y, c, k if to is None else to)
                src, dst = copy_refs(j, (land[j] if src_of[j] is None else ins[src_of[j]]), land[j], me, pid)
                _remote(src, dst, send.at[j * nk + i], recv.at[j * nk + i], dest).start()
        token[...] = jnp.zeros_like(token)

    outs = pl.pallas_call(
        body, name=name,
        out_shape=(pltpu.SemaphoreType.DMA((n * nk,)), pltpu.SemaphoreType.DMA((n * nk,)))
        + tuple(pltpu.HBM(a.shape, a.dtype) for a in srcs) + tuple(pltpu.HBM(a.shape, a.dtype) for a in lands)
        + (jax.ShapeDtypeStruct((8, 128), F32),),
        in_specs=(HBM,) * (ns + n) + (ANY,) * nd,
        out_specs=(SEMS, SEMS) + (HBM,) * (ns + n) + (pl.BlockSpec(memory_space=pltpu.VMEM),),
        input_output_aliases={i: 2 + i for i in range(ns + n)},
        compiler_params=pltpu.CompilerParams(has_side_effects=EFFECT),
    )(*[_hbm(a) for a in srcs], *[_hbm(a) for a in lands], *deps)
    return outs[0], outs[1], list(outs[2:2 + ns]), list(outs[2 + ns:2 + ns + n]), outs[-1]


def _split_wait(srcs, src_of, lands, send, recv, sem_rows, wait_refs, after, name, ks=ALL_KS):
    ns, n, nk = len(srcs), len(lands), len(ks)

    def body(*refs):
        ins, land = refs[:ns], refs[ns:ns + n]
        send_ref, recv_ref = refs[ns + n], refs[ns + n + 1]
        x, y, c, me = _mesh_pos()
        for j in range(n):
            for i, k in enumerate(ks):
                peer, _ = _peer(x, y, c, k)
                src, dst = wait_refs(j, (land[j] if src_of[j] is None else ins[src_of[j]]), land[j])
                sem = sem_rows[j] * nk + i
                cp = _remote(src, dst, send_ref.at[sem], recv_ref.at[sem], peer)
                cp.wait_send()
                cp.wait_recv()

    outs = pl.pallas_call(
        body, name=name,
        out_shape=tuple(pltpu.HBM(a.shape, a.dtype) for a in srcs) + tuple(pltpu.HBM(a.shape, a.dtype) for a in lands),
        in_specs=(HBM,) * (ns + n) + (SEMS, SEMS, ANY),
        out_specs=(HBM,) * (ns + n),
        input_output_aliases={i: i for i in range(ns + n)},
        compiler_params=pltpu.CompilerParams(has_side_effects=EFFECT),
    )(*srcs, *lands, send, recv, after)
    return list(outs[:ns]), list(outs[ns:])


class _Comm:
    def __init__(self, shards, me, deps=()):
        names = AG_ORDER
        rows = [SEC_ROWS[n] for n in names]
        self.me = me
        lands = [lax.dynamic_update_slice(lax.empty((NDEV * r, D), BF16), shards[n], (_shard_pos(n, me), 0))
                 for n, r in zip(names, rows)]

        def copy_refs(j, src, land, me, pid):
            own = land.at[pl.ds(pl.multiple_of(_shard_pos(names[j], me), 16), rows[j])]
            return own, own

        self.ag_send, self.ag_recv, _, lands, self.ag_token = _split_start(
            [], [None] * len(names), lands, copy_refs, "ag_start", deps=deps, ks=AG_KS1)
        self.ag_land = dict(zip(names, lands))
        self.rs = []

    def weights(self, group, after):
        names = AG_GROUPS[group]
        idx = [AG_ORDER.index(n) for n in names]
        rows = [SEC_ROWS[n] for n in names]
        none = [None] * len(names)

        def wait_refs(j, src, land):
            return land.at[pl.ds(0, rows[j])], land.at[pl.ds(0, rows[j])]

        _, lands = _split_wait([], none, [self.ag_land[n] for n in names], self.ag_send, self.ag_recv, idx,
                               wait_refs, after, f"ag_wait{group}", ks=AG_KS1)

        def copy_refs(j, src, land, me, pid):
            theirs = land.at[pl.ds(pl.multiple_of(_shard_pos(names[j], pid), 16), rows[j])]
            return theirs, theirs

        send, recv, _, lands, token = _split_start([], none, lands, copy_refs, f"ag_pass{group}", ks=AG_KS2, to=1)
        _, lands = _split_wait([], none, lands, send, recv, list(range(len(names))), wait_refs, token,
                               f"ag_pass_wait{group}", ks=AG_KS2)
        return dict(zip(names, lands))

    def send_grads(self, group, gws):
        names = RS_GROUPS[group]
        rows = [SEC_ROWS[n] for n in names]
        grads = [gws[n] for n in names]
        me = self.me
        lands = [lax.dynamic_update_slice(
            lax.empty((NDEV, r, D), BF16),
            lax.dynamic_slice(g, (_shard_pos(n, me), 0), (r, D))[None], (me, 0, 0))
            for n, r, g in zip(names, rows, grads)]

        def copy_refs(j, src, land, me, pid):
            return src.at[pl.ds(pl.multiple_of(_shard_pos(names[j], pid), 16), rows[j])], land.at[me]

        send, recv, srcs, lands, token = _split_start(grads, list(range(len(names))), lands, copy_refs,
                                                      f"rs_start{group}")
        self.rs.append((names, rows, send, recv, srcs, lands))
        return token

    def received(self, group, after):
        names, rows, send, recv, srcs, lands = self.rs[group]

        def wait_refs(j, src, land):
            return src.at[pl.ds(0, rows[j])], land.at[0]

        _, lands = _split_wait(srcs, list(range(len(names))), lands, send, recv, list(range(len(names))), wait_refs,
                               after, f"rs_wait{group}")
        return dict(zip(names, lands))


def _sum_contributions(r_ref):
    g = r_ref[0].astype(F32)
    for dev in range(1, NDEV):
        g = g + r_ref[dev].astype(F32)
    return g


def _adam_math(g, w, m, v):
    c1 = 1.0 / (1.0 - ADAM_B1 ** ADAM_STEP)
    c2 = 1.0 / (1.0 - ADAM_B2 ** ADAM_STEP)
    mn = ADAM_B1 * m + (1.0 - ADAM_B1) * g
    vn = ADAM_B2 * v + (1.0 - ADAM_B2) * (g * g)
    return -ADAM_LR * ((mn * c1) / (jnp.sqrt(vn * c2) + ADAM_EPS) + ADAM_WD * w), mn, vn


def _adamw(R, w, m, v, *, tr, name):
    rows, C = w.shape

    def kern(r_ref, w_ref, m_ref, v_ref, g_out, d_out, m_out, v_out):
        g = _sum_contributions(r_ref)
        g_out[...] = g
        d_out[...], m_out[...], v_out[...] = _adam_math(g, w_ref[...], m_ref[...], v_ref[...])

    tile = pl.BlockSpec((tr, C), lambda i: (i, 0))
    shp = jax.ShapeDtypeStruct((rows, C), F32)
    return pl.pallas_call(
        kern, grid=(rows // tr,),
        in_specs=[pl.BlockSpec((NDEV, tr, C), lambda i: (0, i, 0)), tile, tile, tile],
        out_specs=[tile] * 4, out_shape=[shp] * 4, compiler_params=_cparams(1), name=name)(R, w, m, v)


def _adamw_pool_group(R, w, m, v):
    rows = SEC_ROWS["pg"]

    def kern(r_ref, w_ref, m_ref, v_ref, g_out, d_out, m_out, v_out):
        g = _sum_contributions(r_ref)
        g_out[0] = g
        d_out[0], m_out[0], v_out[0] = _adam_math(g, w_ref[0], m_ref[0], v_ref[0])

    blk = pl.BlockSpec((1, rows, PGD), lambda i: (i, 0, 0))
    shp = jax.ShapeDtypeStruct((POOL_G, rows, PGD), F32)
    return pl.pallas_call(
        kern, grid=(POOL_G,),
        in_specs=[pl.BlockSpec((NDEV, rows, PGD), lambda i: (0, 0, i)), blk, blk, blk],
        out_specs=[blk] * 4, out_shape=[shp] * 4, compiler_params=_cparams(1), name="adamw_pg")(R, w, m, v)


def _grad_sum_t(R, name):
    rows = R.shape[1]
    tr = 128 if rows % 128 == 0 else rows

    def kern(r_ref, o_ref):
        o_ref[...] = _sum_contributions(r_ref).T

    return pl.pallas_call(
        kern, grid=(rows // tr,), in_specs=[pl.BlockSpec((NDEV, tr, D), lambda i: (0, i, 0))],
        out_specs=pl.BlockSpec((D, tr), lambda i: (0, i)), out_shape=jax.ShapeDtypeStruct((D, rows), F32),
        compiler_params=_cparams(1), name=name)(R)


def _adam_plain(g, w, m, v, *, tr, name):
    rows, C = w.shape

    def kern(g_ref, w_ref, m_ref, v_ref, d_out, m_out, v_out):
        d_out[...], m_out[...], v_out[...] = _adam_math(g_ref[...], w_ref[...], m_ref[...], v_ref[...])

    tile = pl.BlockSpec((tr, C), lambda i: (i, 0))
    shp = jax.ShapeDtypeStruct((rows, C), F32)
    return pl.pallas_call(
        kern, grid=(rows // tr,), in_specs=[tile] * 4, out_specs=[tile] * 3, out_shape=[shp] * 3,
        compiler_params=_cparams(1), name=name)(g, w, m, v)


def _pack_sections(w_qkv, w_attn_out, w_pool_in, w_pool_group, w_ffn_gate_up, w_ffn_down):
    pg = w_pool_group[0].transpose(1, 0, 2).reshape(SEC_ROWS["pg"], D)
    return {"qkv": w_qkv[0].T, "wo": w_attn_out[0], "wpi": w_pool_in[0], "gu0": w_ffn_gate_up[0].T,
            "gu1": w_ffn_gate_up[1].T, "d0": w_ffn_down[0], "d1": w_ffn_down[1], "pg": pg}


def _vec_pack(attn_norm, ffn_norm, final_norm, pool_norm_sh, pool_scale_sh, me):
    def place(sh):
        return lax.dynamic_update_slice(jnp.zeros((1, D), F32), sh, (0, me * 128))
    return jnp.concatenate([attn_norm, ffn_norm, final_norm.reshape(1, D), place(pool_norm_sh),
                            place(pool_scale_sh), jnp.zeros((2, D), F32)], axis=0)


def _vec_unpack(p, me):
    def take(r):
        return lax.dynamic_slice(p[r:r + 1], (0, me * 128), (1, 128))
    return p[0:1], p[1:3], p[3], take(4), take(5)


def kernel(x, attn_norm, w_qkv, w_attn_out, pool_norm, w_pool_in, w_pool_group, pool_scale, ffn_norm, w_ffn_gate_up, w_ffn_down, final_norm, loss_target, m_attn_norm, m_w_qkv, m_w_attn_out, m_pool_norm, m_w_pool_in, m_w_pool_group, m_pool_scale, m_ffn_norm, m_w_ffn_gate_up, m_w_ffn_down, m_final_norm, v_attn_norm, v_w_qkv, v_w_attn_out, v_pool_norm, v_w_pool_in, v_w_pool_group, v_pool_scale, v_ffn_norm, v_w_ffn_gate_up, v_w_ffn_down, v_final_norm):
    me = 4 * lax.axis_index("x") + 2 * lax.axis_index("y") + lax.axis_index("c")

    pw = _pack_sections(w_qkv, w_attn_out, w_pool_in, w_pool_group, w_ffn_gate_up, w_ffn_down)
    vsh = jnp.concatenate([pool_norm, pool_scale, jnp.zeros((6, 128), F32)], axis=0)

    vg = _bcast_all(vsh, "gather_pool_vectors")
    comm = _Comm({n: pw[n].astype(BF16) for n, _ in SECTIONS}, me, deps=(vg,))
    pool_norm_full = vg[:, 0, :].reshape(1, D)
    pool_scale_full = vg[:, 1, :].reshape(1, D)

    grad_x, vec = _local_step(x[0], loss_target[0], comm, attn_norm, ffn_norm, final_norm,
                              pool_norm_full, pool_scale_full)

    vw = _vec_pack(attn_norm, ffn_norm, final_norm, pool_norm, pool_scale, me)
    vm = _vec_pack(m_attn_norm, m_ffn_norm, m_final_norm, m_pool_norm, m_pool_scale, me)
    vv = _vec_pack(v_attn_norm, v_ffn_norm, v_final_norm, v_pool_norm, v_pool_scale, me)

    gu_shape = w_ffn_gate_up.shape
    res = {}
    gu_t, d_res = {}, {}
    vec_out = None
    after = grad_x
    for group in range(len(RS_GROUPS)):
        if group == len(RS_GROUPS) - 1:
            VR = _bcast_all(vec, "exchange_vector_grads", deps=(after,))
            vec_out = _adamw(VR, vw, vm, vv, tr=8, name="adamw_vec")
            after = vec_out[0]
        for n, R in comm.received(group, after).items():
            if n in ("d0", "d1"):
                l = int(n[1])
                d_res[l] = _adamw(R, w_ffn_down[l], m_w_ffn_down[l], v_w_ffn_down[l], tr=352, name=f"adamw_{n}")
                after = d_res[l][0]
            elif n in ("gu0", "gu1"):
                gu_t[int(n[2])] = after = _grad_sum_t(R, f"grad_sum_{n}")
            elif n == "pg":
                out = _adamw_pool_group(R, w_pool_group[0], m_w_pool_group[0], v_w_pool_group[0])
                res["pg"] = tuple(a[None] for a in out)
                after = out[0]
            elif n in ("wo", "wpi"):
                w, m, v = ((w_attn_out, m_w_attn_out, v_w_attn_out) if n == "wo"
                           else (w_pool_in, m_w_pool_in, v_w_pool_in))
                res[n] = _adamw(R, w[0], m[0], v[0], tr=128, name=f"adamw_{n}")
                res[n] = tuple(a[None] for a in res[n])
                after = res[n][0]
            else:
                g = _grad_sum_t(R, "grad_sum_qkv")
                out = _adam_plain(g, w_qkv[0], m_w_qkv[0], v_w_qkv[0], tr=256, name="adamw_qkv")
                res["qkv"] = tuple(a[None] for a in (g,) + tuple(out))
                after = out[0]
    g_gu = jnp.stack([gu_t[0], gu_t[1]])
    flat = (2 * D, gu_shape[2])
    out = _adam_plain(g_gu.reshape(flat), w_ffn_gate_up.reshape(flat), m_w_ffn_gate_up.reshape(flat),
                      v_w_ffn_gate_up.reshape(flat), tr=256, name="adamw_gu")
    res["gu"] = (g_gu,) + tuple(a.reshape(gu_shape) for a in out)
    res["d"] = tuple(jnp.stack([d_res[0][kind], d_res[1][kind]]) for kind in range(4))

    outs = []
    for kind in range(4):
        an, fn, fin, pn, ps = _vec_unpack(vec_out[kind], me)
        outs.append((an, res["qkv"][kind], res["wo"][kind], pn, res["wpi"][kind], res["pg"][kind], ps, fn,
                     res["gu"][kind], res["d"][kind], fin))
    loss = 0.5 * jnp.sum(vec_out[0][6]) / D
    return (loss, grad_x[None]) + outs[0] + outs[1] + outs[2] + outs[3]
```

```python
import jax
import jax.numpy as jnp
from jax import lax
from jax.experimental import pallas as pl
from jax.experimental.pallas import tpu as pltpu

F32 = jnp.float32
BF16 = jnp.bfloat16

D = 1024
NDEV = 8
HEADS = 8
HD = 128
QB = 128
NGROUPS = 3
DILS = (1, 4, 16)
DFF = 2816
HCH = 1408
POOL_G = 4
PGD = 256
RMS_EPS = 1e-6
NEG = -1e30

ADAM_LR = 0.001
ADAM_B1 = 0.9
ADAM_B2 = 0.999
ADAM_EPS = 1e-08
ADAM_WD = 0.01
ADAM_STEP = 10

VMEM_LIMIT = 52 * 1024 * 1024

SECTIONS = (("qkv", 1152), ("wo", 128), ("wpi", 128), ("gu0", 704), ("gu1", 704),
            ("d0", 352), ("d1", 352), ("pg", 32))
LOC_OFF = {}
GLB_OFF = {}
_o = 0
for _n, _r in SECTIONS:
    LOC_OFF[_n] = _o
    GLB_OFF[_n] = _o * NDEV
    _o += _r
PACK_ROWS = _o
GLB_ROWS = PACK_ROWS * NDEV
SEC_ROWS = dict(SECTIONS)


def _cparams(n_grid):
    return pltpu.CompilerParams(dimension_semantics=("arbitrary",) * n_grid, vmem_limit_bytes=VMEM_LIMIT)


def _shard_pos(name, dev):
    n = SEC_ROWS[name]
    if name in ("gu0", "gu1"):
        return ((dev % 4) // 2) * (2 * HCH) + (dev // 4) * HCH + (dev % 2) * n
    return dev * n


def _mm(a, b, *, mode, M, N, K, tm, tn, tk, out_dtype, name, a_off=(0, 0), b_off=(0, 0), res=None,
        out_rows=None, out_off=0, out_prev=None, deps=()):
    nm, nn, nk = M // tm, N // tn, K // tk
    assert nm * tm == M and nn * tn == N and nk * tk == K
    if mode == "nn":
        a_bs, b_bs = (tm, tk), (tk, tn)
        a_ix = lambda i, j, k: (i, k)
        b_ix = lambda i, j, k: (k, j)
        dims = (((1,), (0,)), ((), ()))
    elif mode == "nt":
        a_bs, b_bs = (tm, tk), (tn, tk)
        a_ix = lambda i, j, k: (i, k)
        b_ix = lambda i, j, k: (j, k)
        dims = (((1,), (1,)), ((), ()))
    else:
        a_bs, b_bs = (tk, tm), (tk, tn)
        a_ix = lambda i, j, k: (k, i)
        b_ix = lambda i, j, k: (k, j)
        dims = (((0,), (0,)), ((), ()))

    def spec(bs, ix, off):
        def im(i, j, k):
            r, c = ix(i, j, k)
            return (r + off[0], c + off[1])
        return pl.BlockSpec(bs, im)

    in_specs = [spec(a_bs, a_ix, a_off), spec(b_bs, b_ix, b_off)]
    args = [a, b]
    if res is not None:
        in_specs.append(pl.BlockSpec((tm, tn), lambda i, j, k: (i, j)))
        args.append(res)
    out_shape = jax.ShapeDtypeStruct((M if out_rows is None else out_rows, N), out_dtype)
    out_spec = pl.BlockSpec((tm, tn), lambda i, j, k: (i + out_off, j))
    has_res = res is not None
    extra = list(deps) + ([out_prev] if out_prev is not None else [])
    for dep in extra:
        in_specs.append(pl.BlockSpec(memory_space=pl.ANY))
        args.append(dep)
    o_pos = 2 + int(has_res) + len(extra)
    aliases = {len(args) - 1: 0} if out_prev is not None else {}

    def kern(*refs):
        a_ref, b_ref = refs[0], refs[1]
        res_ref = refs[2] if has_res else None
        o_ref = refs[o_pos]
        av = a_ref[...]
        bv = b_ref[...]
        if av.dtype != BF16:
            av = av.astype(BF16)
        if bv.dtype != BF16:
            bv = bv.astype(BF16)
        part = lax.dot_general(av, bv, dims, preferred_element_type=F32)

        def write(val):
            if has_res:
                val = val + res_ref[...]
            o_ref[...] = val.astype(out_dtype)

        if nk == 1:
            write(part)
        else:
            acc_ref = refs[-1]
            k = pl.program_id(2)

            @pl.when(k == 0)
            def _():
                acc_ref[...] = part

            @pl.when(k > 0)
            def _():
                acc_ref[...] += part

            @pl.when(k == nk - 1)
            def _():
                write(acc_ref[...])

    scratch = [pltpu.VMEM((tm, tn), F32)] if nk > 1 else []
    return pl.pallas_call(
        kern, grid=(nm, nn, nk), in_specs=in_specs, out_specs=out_spec, out_shape=out_shape,
        scratch_shapes=scratch, input_output_aliases=aliases, compiler_params=_cparams(3), name=name)(*args)


def _rms_fwd(x, g, name, deps=()):
    S = x.shape[0]
    tr = 512

    def kern(x_ref, g_ref, *rest):
        h_ref = rest[-1]
        xv = x_ref[...]
        r = lax.rsqrt(jnp.mean(xv * xv, axis=-1, keepdims=True) + RMS_EPS)
        h_ref[...] = (xv * r * g_ref[...]).astype(BF16)

    return pl.pallas_call(
        kern, grid=(S // tr,),
        in_specs=[pl.BlockSpec((tr, D), lambda i: (i, 0)), pl.BlockSpec((1, D), lambda i: (0, 0))]
        + [pl.BlockSpec(memory_space=pl.ANY)] * len(deps),
        out_specs=pl.BlockSpec((tr, D), lambda i: (i, 0)),
        out_shape=jax.ShapeDtypeStruct((S, D), BF16), compiler_params=_cparams(1), name=name)(x, g, *deps)


def _chunks_put(scr, val):
    for c in range(scr.shape[0]):
        scr[c] = val[:, c * 128:(c + 1) * 128]


def _chunks_get(scr):
    return jnp.concatenate([scr[c] for c in range(scr.shape[0])], axis=1)


def _chunks_rows(scr, r, n, dil):
    return jnp.concatenate([scr.at[c][pl.ds(r, n, stride=dil), :] for c in range(scr.shape[0])], axis=1)


def _chunks_add_rows(scr, val, r, n, dil, accumulate):
    for c in range(scr.shape[0]):
        rows = pl.ds(r, n, stride=dil)
        piece = val[:, c * 128:(c + 1) * 128]
        tile = scr.at[c]
        tile[rows, :] = tile[rows, :] + piece if accumulate else piece


def _rms_fwd_folded(x, g, name, deps=()):
    S = x.shape[0]
    tr = 512
    dils = DILS[1:]

    def kern(x_ref, g_ref, *rest):
        outs, scr = rest[len(deps):-1], rest[-1]
        xv = x_ref[...]
        r = lax.rsqrt(jnp.mean(xv * xv, axis=-1, keepdims=True) + RMS_EPS)
        h = (xv * r * g_ref[...]).astype(BF16)
        outs[0][...] = h
        _chunks_put(scr, h.astype(F32))
        for o_ref, dil in zip(outs[1:], dils):
            for res in range(dil):
                o_ref[res] = _chunks_rows(scr, res, tr // dil, dil).astype(BF16)

    return pl.pallas_call(
        kern, grid=(S // tr,),
        in_specs=[pl.BlockSpec((tr, D), lambda i: (i, 0)), pl.BlockSpec((1, D), lambda i: (0, 0))]
        + [pl.BlockSpec(memory_space=pl.ANY)] * len(deps),
        out_specs=[pl.BlockSpec((tr, D), lambda i: (i, 0))]
        + [pl.BlockSpec((dil, tr // dil, D), lambda i: (0, i, 0)) for dil in dils],
        out_shape=[jax.ShapeDtypeStruct((S, D), BF16)]
        + [jax.ShapeDtypeStruct((dil, S // dil, D), BF16) for dil in dils],
        scratch_shapes=[pltpu.VMEM((D // 128, tr, 128), F32)],
        compiler_params=_cparams(1), name=name)(x, g, *deps)


def _rms_bwd(dh, x, g, dres, name, folded=()):
    S = x.shape[0]
    tr = 512
    nf = len(folded)

    def kern(dh_ref, *rest):
        f_refs = rest[:nf]
        x_ref, g_ref, dres_ref, dx_ref, dg_ref = rest[nf:nf + 5]
        i = pl.program_id(0)
        xv = x_ref[...]
        if nf:
            acc_ref = rest[nf + 5]
            _chunks_put(acc_ref, dh_ref[...].astype(F32))
            for f_ref in f_refs:
                dil = f_ref.shape[0]
                for res in range(dil):
                    _chunks_add_rows(acc_ref, f_ref[res], res, tr // dil, dil, True)
            dhv = _chunks_get(acc_ref)
        else:
            dhv = dh_ref[...].astype(F32)
        r = lax.rsqrt(jnp.mean(xv * xv, axis=-1, keepdims=True) + RMS_EPS)
        xhat = xv * r
        gy = dhv * g_ref[...]
        dx_ref[...] = dres_ref[...] + r * (gy - xhat * jnp.mean(gy * xhat, axis=-1, keepdims=True))
        part = jnp.sum(dhv * xhat, axis=0, keepdims=True)

        @pl.when(i == 0)
        def _():
            dg_ref[...] = part

        @pl.when(i > 0)
        def _():
            dg_ref[...] += part

    row = pl.BlockSpec((tr, D), lambda i: (i, 0))
    vec = pl.BlockSpec((1, D), lambda i: (0, 0))
    fspecs = [pl.BlockSpec((f.shape[0], tr // f.shape[0], D), lambda i: (0, i, 0)) for f in folded]
    return pl.pallas_call(
        kern, grid=(S // tr,), in_specs=[row] + fspecs + [row, vec, row], out_specs=[row, vec],
        out_shape=[jax.ShapeDtypeStruct((S, D), F32), jax.ShapeDtypeStruct((1, D), F32)],
        scratch_shapes=[pltpu.VMEM((D // 128, tr, 128), F32)] if nf else [],
        compiler_params=_cparams(1), name=name)(dh, *folded, x, g, dres)


def _loss_head(x, g, tgt):
    S = x.shape[0]
    tr = 512

    def kern(x_ref, g_ref, t_ref, dx_ref, dg_ref, ls_ref):
        i = pl.program_id(0)
        xv = x_ref[...]
        gv = g_ref[...]
        r = lax.rsqrt(jnp.mean(xv * xv, axis=-1, keepdims=True) + RMS_EPS)
        xhat = xv * r
        e = xhat * gv - t_ref[...]
        dy = e * (1.0 / D)
        gy = dy * gv
        dx_ref[...] = r * (gy - xhat * jnp.mean(gy * xhat, axis=-1, keepdims=True))
        dgp = jnp.sum(dy * xhat, axis=0, keepdims=True)
        lsp = jnp.sum(e * e, axis=0, keepdims=True)

        @pl.when(i == 0)
        def _():
            dg_ref[...] = dgp
            ls_ref[...] = lsp

        @pl.when(i > 0)
        def _():
            dg_ref[...] += dgp
            ls_ref[...] += lsp

    row = pl.BlockSpec((tr, D), lambda i: (i, 0))
    vec = pl.BlockSpec((1, D), lambda i: (0, 0))
    return pl.pallas_call(
        kern, grid=(S // tr,), in_specs=[row, vec, row], out_specs=[row, vec, vec],
        out_shape=[jax.ShapeDtypeStruct((S, D), F32), jax.ShapeDtypeStruct((1, D), F32),
                   jax.ShapeDtypeStruct((1, D), F32)],
        compiler_params=_cparams(1), name="loss_head")(x, g, tgt)


def _ffn_up(h, G, name):
    S = h.shape[0]
    tm = 512
    nj = DFF // HCH

    def kern(h_ref, w_ref, gu_ref, act_ref):
        gu = lax.dot_general(h_ref[...], w_ref[...], (((1,), (1,)), ((), ())), preferred_element_type=F32)
        gu_ref[...] = gu.astype(BF16)
        gate = gu[:, :HCH]
        up = gu[:, HCH:]
        act_ref[...] = (gate * jax.nn.sigmoid(gate) * up).astype(BF16)

    return pl.pallas_call(
        kern, grid=(nj, S // tm),
        in_specs=[pl.BlockSpec((tm, D), lambda j, i: (i, 0)),
                  pl.BlockSpec((2 * HCH, D), lambda j, i: (j, 0))],
        out_specs=[pl.BlockSpec((tm, 2 * HCH), lambda j, i: (i, j)),
                   pl.BlockSpec((tm, HCH), lambda j, i: (i, j))],
        out_shape=[jax.ShapeDtypeStruct((S, 2 * DFF), BF16), jax.ShapeDtypeStruct((S, DFF), BF16)],
        compiler_params=_cparams(2), name=name)(h, G)


def _ffn_down_bwd(dx, G, gu, name):
    S = dx.shape[0]
    tm = 512
    nj = DFF // HCH

    def kern(dx_ref, w_ref, gu_ref, o_ref):
        dact = lax.dot_general(dx_ref[...].astype(BF16), w_ref[...], (((1,), (1,)), ((), ())),
                               preferred_element_type=F32)
        gate = gu_ref[:, :HCH].astype(F32)
        up = gu_ref[:, HCH:].astype(F32)
        sig = jax.nn.sigmoid(gate)
        silu = gate * sig
        o_ref[:, :HCH] = (dact * up * (sig * (1.0 + gate * (1.0 - sig)))).astype(BF16)
        o_ref[:, HCH:] = (dact * silu).astype(BF16)

    return pl.pallas_call(
        kern, grid=(nj, S // tm),
        in_specs=[pl.BlockSpec((tm, D), lambda j, i: (i, 0)),
                  pl.BlockSpec((HCH, D), lambda j, i: (j, 0)),
                  pl.BlockSpec((tm, 2 * HCH), lambda j, i: (i, j))],
        out_specs=pl.BlockSpec((tm, 2 * HCH), lambda j, i: (i, j)),
        out_shape=jax.ShapeDtypeStruct((S, 2 * DFF), BF16),
        compiler_params=_cparams(2), name=name)(dx, G, gu)


def _trail(u, *, backward, name):
    S = u.shape[0]

    def kern(u_ref, o_ref):
        g = pl.program_id(0)
        uv = u_ref[...].astype(F32)
        row = lax.broadcasted_iota(jnp.int32, uv.shape, 0)
        win = jnp.left_shift(jnp.int32(2), g)
        cnt = jnp.minimum(row + 1, win).astype(F32)
        s = uv / cnt if backward else uv
        levels = []
        for k in (1, 2, 4, 8):
            if backward:
                sh = jnp.where(row < S - k, pltpu.roll(s, S - k, 0), 0.0)
            else:
                sh = jnp.where(row >= k, pltpu.roll(s, k, 0), 0.0)
            s = s + sh
            levels.append(s)
        sel = jnp.where(g == 0, levels[0], jnp.where(g == 1, levels[1], jnp.where(g == 2, levels[2], levels[3])))
        if backward:
            o_ref[...] = (sel - uv).astype(BF16)
        else:
            o_ref[...] = (sel / cnt - uv).astype(BF16)

    blk = pl.BlockSpec((S, PGD), lambda g: (0, g))
    return pl.pallas_call(
        kern, grid=(POOL_G,), in_specs=[blk], out_specs=blk,
        out_shape=jax.ShapeDtypeStruct((S, D), BF16), compiler_params=_cparams(1), name=name)(u)


def _pool_out(yd, G, scale, xres):
    S = yd.shape[0]
    tm = 1024

    def kern(y_ref, w_ref, s_ref, x_ref, o_ref):
        z = jnp.dot(y_ref[...], w_ref[...], preferred_element_type=F32)
        o_ref[...] = x_ref[...] + z * s_ref[...]

    tile = pl.BlockSpec((tm, PGD), lambda i, g: (i, g))
    return pl.pallas_call(
        kern, grid=(S // tm, POOL_G),
        in_specs=[tile, pl.BlockSpec((PGD, PGD), lambda i, g: (0, g)),
                  pl.BlockSpec((1, PGD), lambda i, g: (0, g)), tile],
        out_specs=tile, out_shape=jax.ShapeDtypeStruct((S, D), F32),
        compiler_params=_cparams(2), name="pool_out")(yd, G, scale, xres)


def _pool_out_bwd(dz, yd, G, scale):
    S = yd.shape[0]
    tm = 1024
    ni = S // tm

    def kern(dz_ref, y_ref, w_ref, s_ref, dy_ref, ds_ref, dw_ref, acc_ref):
        i = pl.program_id(1)
        dzv = dz_ref[...]
        yv = y_ref[...]
        wv = w_ref[...]
        zraw = jnp.dot(yv, wv, preferred_element_type=F32)
        dsp = jnp.sum(dzv * zraw, axis=0, keepdims=True)
        dzr = (dzv * s_ref[...]).astype(BF16)
        dy_ref[...] = lax.dot_general(dzr, wv, (((1,), (1,)), ((), ())), preferred_element_type=F32)
        dwp = lax.dot_general(yv, dzr, (((0,), (0,)), ((), ())), preferred_element_type=F32)

        @pl.when(i == 0)
        def _():
            ds_ref[...] = dsp
            acc_ref[...] = dwp

        @pl.when(i > 0)
        def _():
            ds_ref[...] += dsp
            acc_ref[...] += dwp

        @pl.when(i == ni - 1)
        def _():
            dw_ref[...] = acc_ref[...].astype(BF16)

    tile = pl.BlockSpec((tm, PGD), lambda g, i: (i, g))
    return pl.pallas_call(
        kern, grid=(POOL_G, ni),
        in_specs=[tile, tile, pl.BlockSpec((PGD, PGD), lambda g, i: (0, g)),
                  pl.BlockSpec((1, PGD), lambda g, i: (0, g))],
        out_specs=[tile, pl.BlockSpec((1, PGD), lambda g, i: (0, g)),
                   pl.BlockSpec((PGD, PGD), lambda g, i: (0, g))],
        out_shape=[jax.ShapeDtypeStruct((S, D), F32), jax.ShapeDtypeStruct((1, D), F32),
                   jax.ShapeDtypeStruct((PGD, D), BF16)],
        scratch_shapes=[pltpu.VMEM((PGD, PGD), F32)],
        compiler_params=_cparams(2), name="pool_out_bwd")(dz, yd, G, scale)


def _bias_table():
    qi = jnp.arange(QB)[:, None]
    ki = jnp.arange(2 * QB)[None, :]
    delta = QB + qi - ki
    inband = (delta >= 0) & (delta <= QB)
    n = NGROUPS * HEADS
    slopes = jnp.exp2(-8.0 * jnp.arange(1, n + 1, dtype=F32) / n).reshape(NGROUPS, HEADS)
    dil = jnp.asarray(DILS, F32)
    bias = -slopes[:, :, None, None] * (delta.astype(F32)[None, None] * dil[:, None, None, None])
    return jnp.where(inband[None, None], bias, NEG)


def _attn_fwd(qkv_f, bias, nb, name):
    S = qkv_f.shape[0]
    nblk = S // QB
    scale = HD ** -0.5

    def kern(q_ref, kc_ref, kp_ref, vc_ref, vp_ref, b_ref, o_ref, l_ref, s_scr, p_scr, r_scr):
        b = pl.program_id(0)
        has_prev = jnp.bitwise_and(b, nb - 1) != 0
        col = lax.broadcasted_iota(jnp.int32, (QB, 2 * QB), 1)
        dead = jnp.logical_and(col < QB, jnp.logical_not(has_prev))
        lane = lax.broadcasted_iota(jnp.int32, (QB, HD), 1)
        lse_all = jnp.zeros((QB, HD), F32)
        for h in range(HEADS):
            sl = slice(h * HD, (h + 1) * HD)
            kk = jnp.concatenate([kp_ref[:, sl], kc_ref[:, sl]], axis=0)
            s_scr[h] = lax.dot_general(q_ref[:, sl], kk, (((1,), (1,)), ((), ())), preferred_element_type=F32)
        for h in range(HEADS):
            s = s_scr[h] * scale + b_ref[h]
            s = jnp.where(dead, NEG, s)
            m = jnp.max(s, axis=-1, keepdims=True)
            p = jnp.exp(s - m)
            den = jnp.sum(p, axis=-1, keepdims=True)
            p_scr[h] = p.astype(BF16)
            r_scr[h] = jnp.broadcast_to(1.0 / den, (QB, HD))
            lse_all = jnp.where(lane == h, m + jnp.log(den), lse_all)
        for h in range(HEADS):
            sl = slice(h * HD, (h + 1) * HD)
            vv = jnp.concatenate([vp_ref[:, sl], vc_ref[:, sl]], axis=0)
            o = jnp.dot(p_scr[h], vv, preferred_element_type=F32) * r_scr[h]
            o_ref[:, sl] = o.astype(BF16)
        l_ref[...] = lse_all

    def blk(colblk, prev):
        if prev:
            return pl.BlockSpec((QB, D), lambda b: (jnp.maximum(b - 1, 0), colblk))
        return pl.BlockSpec((QB, D), lambda b: (b, colblk))

    return pl.pallas_call(
        kern, grid=(nblk,),
        in_specs=[blk(0, False), blk(1, False), blk(1, True), blk(2, False), blk(2, True),
                  pl.BlockSpec((HEADS, QB, 2 * QB), lambda b: (0, 0, 0))],
        out_specs=[pl.BlockSpec((QB, D), lambda b: (b, 0)), pl.BlockSpec((QB, HD), lambda b: (b, 0))],
        out_shape=[jax.ShapeDtypeStruct((S, D), BF16), jax.ShapeDtypeStruct((S, HD), F32)],
        scratch_shapes=[pltpu.VMEM((HEADS, QB, 2 * QB), F32), pltpu.VMEM((HEADS, QB, 2 * QB), BF16),
                        pltpu.VMEM((HEADS, QB, HD), F32)],
        compiler_params=_cparams(1), name=name)(qkv_f, qkv_f, qkv_f, qkv_f, qkv_f, bias)


def _natural(ref, scr, tm):
    dil = ref.shape[0]
    for res in range(dil):
        _chunks_add_rows(scr, ref[res].astype(F32), res, tm // dil, dil, False)
    return _chunks_get(scr)


def _attn_merge(os, lses):
    S = os[0].shape[0]
    tm = 512

    def kern(o0, o1, o2, l0, l1, l2, om_ref, lm_ref, ls1, ls2, os1, os2):
        la = l0[...]
        lb = _natural(l1, ls1, tm)
        lc = _natural(l2, ls2, tm)
        m = jnp.maximum(jnp.maximum(la, lb), lc)
        e0, e1, e2 = jnp.exp(la - m), jnp.exp(lb - m), jnp.exp(lc - m)
        tot = e0 + e1 + e2
        lm_ref[...] = m + jnp.log(tot)
        w0, w1, w2 = e0 / tot, e1 / tot, e2 / tot
        for res in range(o1.shape[0]):
            _chunks_add_rows(os1, o1[res].astype(F32), res, tm // o1.shape[0], o1.shape[0], False)
        for res in range(o2.shape[0]):
            _chunks_add_rows(os2, o2[res].astype(F32), res, tm // o2.shape[0], o2.shape[0], False)
        for h in range(HEADS):
            sl = slice(h * HD, (h + 1) * HD)
            acc = w0[:, h:h + 1] * o0[:, sl].astype(F32) + w1[:, h:h + 1] * os1[h] + w2[:, h:h + 1] * os2[h]
            om_ref[:, sl] = acc.astype(BF16)

    def spec(a, c):
        if a.ndim == 2:
            return pl.BlockSpec((tm, c), lambda i: (i, 0))
        return pl.BlockSpec((a.shape[0], tm // a.shape[0], c), lambda i: (0, i, 0))

    return pl.pallas_call(
        kern, grid=(S // tm,),
        in_specs=[spec(a, D) for a in os] + [spec(a, HD) for a in lses],
        out_specs=[pl.BlockSpec((tm, D), lambda i: (i, 0)), pl.BlockSpec((tm, HD), lambda i: (i, 0))],
        out_shape=[jax.ShapeDtypeStruct((S, D), BF16), jax.ShapeDtypeStruct((S, HD), F32)],
        scratch_shapes=[pltpu.VMEM((1, tm, HD), F32), pltpu.VMEM((1, tm, HD), F32),
                        pltpu.VMEM((HEADS, tm, HD), F32), pltpu.VMEM((HEADS, tm, HD), F32)],
        compiler_params=_cparams(1), name="attn_merge")(*os, *lses)


def _attn_bwd_prep(do, o, lse):
    S = o.shape[0]
    tm = 512
    dils = DILS[1:]

    def kern(do_ref, o_ref, l_ref, *rest):
        do_outs, l_outs, d_outs = rest[0:3], rest[3:5], rest[5:8]
        do_scr, l_scr, d_scr = rest[8:11]
        lane = lax.broadcasted_iota(jnp.int32, (tm, HD), 1)
        acc = jnp.zeros((tm, HD), F32)
        for h in range(HEADS):
            sl = slice(h * HD, (h + 1) * HD)
            prod = do_ref[:, sl] * o_ref[:, sl].astype(F32)
            acc = jnp.where(lane == h, jnp.sum(prod, axis=-1, keepdims=True), acc)
        d_scr[0] = acc
        l_scr[0] = l_ref[...]
        _chunks_put(do_scr, do_ref[...])
        do_outs[0][...] = do_ref[...].astype(BF16)
        d_outs[0][...] = acc
        for j, dil in enumerate(dils):
            for res in range(dil):
                n = tm // dil
                do_outs[1 + j][res] = _chunks_rows(do_scr, res, n, dil).astype(BF16)
                l_outs[j][res] = _chunks_rows(l_scr, res, n, dil)
                d_outs[1 + j][res] = _chunks_rows(d_scr, res, n, dil)

    def nat(c):
        return pl.BlockSpec((tm, c), lambda i: (i, 0))

    def fol(dil, c):
        return pl.BlockSpec((dil, tm // dil, c), lambda i: (0, i, 0))

    def shapes(c, dt, with_natural):
        first = [jax.ShapeDtypeStruct((S, c), dt)] if with_natural else []
        return first + [jax.ShapeDtypeStruct((dil, S // dil, c), dt) for dil in dils]

    outs = pl.pallas_call(
        kern, grid=(S // tm,), in_specs=[nat(D), nat(D), nat(HD)],
        out_specs=[nat(D)] + [fol(dil, D) for dil in dils] + [fol(dil, HD) for dil in dils]
        + [nat(HD)] + [fol(dil, HD) for dil in dils],
        out_shape=shapes(D, BF16, True) + shapes(HD, F32, False) + shapes(HD, F32, True),
        scratch_shapes=[pltpu.VMEM((HEADS, tm, HD), F32), pltpu.VMEM((1, tm, HD), F32), pltpu.VMEM((1, tm, HD), F32)],
        compiler_params=_cparams(1), name="attn_bwd_prep")(do, o, lse)
    return outs[0:3], [lse] + list(outs[3:5]), outs[5:8]


def _attn_bwd(qkv_f, do_f, lse_f, delta_f, bias, nb, name):
    S = qkv_f.shape[0]
    nblk = S // QB
    scale = HD ** -0.5

    def kern(q_ref, kc_ref, kp_ref, vc_ref, vp_ref, do_ref, l_ref, d_ref, b_ref, out_ref, dq_c, dk_c, dv_c,
             s_scr, dp_scr, ds_scr, p_scr):
        b = pl.program_id(0)

        @pl.when(b == 0)
        def _():
            dq_c[...] = jnp.zeros_like(dq_c)
            dk_c[...] = jnp.zeros_like(dk_c)
            dv_c[...] = jnp.zeros_like(dv_c)

        @pl.when(b == nblk)
        def _():
            out_ref[:, 0:D] = dq_c[...].astype(BF16)
            out_ref[:, D:2 * D] = dk_c[...].astype(BF16)
            out_ref[:, 2 * D:3 * D] = dv_c[...].astype(BF16)

        @pl.when(b < nblk)
        def _():
            has_prev = jnp.bitwise_and(b, nb - 1) != 0
            col = lax.broadcasted_iota(jnp.int32, (QB, 2 * QB), 1)
            dead = jnp.logical_and(col < QB, jnp.logical_not(has_prev))
            out_ref[:, 0:D] = dq_c[...].astype(BF16)
            lv = l_ref[...]
            dv_ = d_ref[...]
            for h in range(HEADS):
                sl = slice(h * HD, (h + 1) * HD)
                kk = jnp.concatenate([kp_ref[:, sl], kc_ref[:, sl]], axis=0)
                vv = jnp.concatenate([vp_ref[:, sl], vc_ref[:, sl]], axis=0)
                s_scr[h] = lax.dot_general(q_ref[:, sl], kk, (((1,), (1,)), ((), ())), preferred_element_type=F32)
                dp_scr[h] = lax.dot_general(do_ref[:, sl], vv, (((1,), (1,)), ((), ())),
                                            preferred_element_type=F32)
            for h in range(HEADS):
                s = s_scr[h] * scale + b_ref[h]
                s = jnp.where(dead, NEG, s)
                p = jnp.exp(s - lv[:, h:h + 1])
                ds_scr[h] = (p * (dp_scr[h] - dv_[:, h:h + 1]) * scale).astype(BF16)
                p_scr[h] = p.astype(BF16)
            for h in range(HEADS):
                sl = slice(h * HD, (h + 1) * HD)
                kk = jnp.concatenate([kp_ref[:, sl], kc_ref[:, sl]], axis=0)
                ds = ds_scr[h]
                dq_c[:, sl] = jnp.dot(ds, kk, preferred_element_type=F32)
                dkk = lax.dot_general(ds, q_ref[:, sl], (((0,), (0,)), ((), ())), preferred_element_type=F32)
                dvv = lax.dot_general(p_scr[h], do_ref[:, sl], (((0,), (0,)), ((), ())),
                                      preferred_element_type=F32)
                out_ref[:, D + h * HD:D + (h + 1) * HD] = (dk_c[:, sl] + dkk[:QB]).astype(BF16)
                out_ref[:, 2 * D + h * HD:2 * D + (h + 1) * HD] = (dv_c[:, sl] + dvv[:QB]).astype(BF16)
                dk_c[:, sl] = dkk[QB:]
                dv_c[:, sl] = dvv[QB:]

    last = nblk - 1

    def blk(colblk, prev):
        if prev:
            return pl.BlockSpec((QB, D), lambda b: (jnp.maximum(jnp.minimum(b, last) - 1, 0), colblk))
        return pl.BlockSpec((QB, D), lambda b: (jnp.minimum(b, last), colblk))

    stat = pl.BlockSpec((QB, HD), lambda b: (jnp.minimum(b, last), 0))
    return pl.pallas_call(
        kern, grid=(nblk + 1,),
        in_specs=[blk(0, False), blk(1, False), blk(1, True), blk(2, False), blk(2, True),
                  pl.BlockSpec((QB, D), lambda b: (jnp.minimum(b, last), 0)), stat, stat,
                  pl.BlockSpec((HEADS, QB, 2 * QB), lambda b: (0, 0, 0))],
        out_specs=pl.BlockSpec((QB, 3 * D), lambda b: (jnp.maximum(b - 1, 0), 0)),
        out_shape=jax.ShapeDtypeStruct((S, 3 * D), BF16),
        scratch_shapes=[pltpu.VMEM((QB, D), F32), pltpu.VMEM((QB, D), F32), pltpu.VMEM((QB, D), F32),
                        pltpu.VMEM((HEADS, QB, 2 * QB), F32), pltpu.VMEM((HEADS, QB, 2 * QB), F32),
                        pltpu.VMEM((HEADS, QB, 2 * QB), BF16), pltpu.VMEM((HEADS, QB, 2 * QB), BF16)],
        compiler_params=_cparams(1), name=name)(qkv_f, qkv_f, qkv_f, qkv_f, qkv_f, do_f, lse_f, delta_f, bias)


def _local_step(x, tgt, comm, attn_norm, ffn_norm, final_norm, pool_norm, pool_scale):
    S = x.shape[0]
    bias = _bias_table()
    g_attn = attn_norm.reshape(1, D)
    g_f0 = ffn_norm[0:1]
    g_f1 = ffn_norm[1:2]
    g_fin = final_norm.reshape(1, D)
    W = {}

    def ffn_fwd(xin, gain, l):
        h = _rms_fwd(xin, gain, f"rms_ffn{l}")
        gu, act = _ffn_up(h, W[f"gu{l}"], f"ffn_up{l}")
        xo = _mm(act, W[f"d{l}"], mode="nn", M=S, N=D, K=DFF, tm=1024, tn=D, tk=DFF, out_dtype=F32,
                 res=xin, name=f"ffn_down{l}")
        return h, gu, act, xo

    def ffn_bwd(dxo, xin, gain, h, gu, act, l, rs_group):
        dgu = _ffn_down_bwd(dxo, W[f"d{l}"], gu, f"ffn_down_bwd{l}")
        gw_d = _mm(act, dxo, mode="tn", M=DFF, N=D, K=S, tm=HCH, tn=D, tk=2048, out_dtype=BF16, name=f"gw_d{l}")
        gw_gu = _mm(dgu, h, mode="tn", M=2 * DFF, N=D, K=S, tm=HCH, tn=D, tk=2048, out_dtype=BF16, name=f"gw_gu{l}")
        token = comm.send_grads(rs_group, {f"d{l}": gw_d, f"gu{l}": gw_gu})
        dh = _mm(dgu, W[f"gu{l}"], mode="nn", M=S, N=D, K=2 * DFF, tm=512, tn=D, tk=2 * DFF,
                 out_dtype=F32, deps=(token,), name=f"ffn_up_bwd{l}")
        dxin, dgain = _rms_bwd(dh, xin, gain, dxo, f"rms_ffn_bwd{l}")
        return dxin, dgain

    nbs = [S // QB // dil for dil in DILS]
    hf = _rms_fwd_folded(x, g_attn, "rms_attn", deps=(comm.ag_token,))
    hf = [h.reshape(S, D) for h in hf]
    W.update(comm.weights(0, hf[0]))
    qkv_f, o_f, lse_f = [], [], []
    for g, dil in enumerate(DILS):
        qkv_f.append(_mm(hf[g], W["qkv"], mode="nt", M=S, N=3 * D, K=D, tm=2048, tn=1024, tk=D, out_dtype=BF16,
                         b_off=(3 * g, 0), name=f"qkv_proj{g}"))
        og, lg = _attn_fwd(qkv_f[g], bias[g], nbs[g], f"attn_fwd{g}")
        o_f.append(og if dil == 1 else og.reshape(dil, S // dil, D))
        lse_f.append(lg if dil == 1 else lg.reshape(dil, S // dil, HD))
    o, lse = _attn_merge(o_f, lse_f)
    W.update(comm.weights(1, o))
    x1 = _mm(o, W["wo"], mode="nn", M=S, N=D, K=D, tm=1024, tn=D, tk=D, out_dtype=F32, res=x, name="attn_out")
    h1, gu0, act0, x2 = ffn_fwd(x1, g_f0, 0)

    W.update(comm.weights(2, x2))
    h2 = _rms_fwd(x2, pool_norm, "rms_pool")
    u = _mm(h2, W["wpi"], mode="nn", M=S, N=D, K=D, tm=1024, tn=D, tk=D, out_dtype=F32, name="pool_in")
    yd = _trail(u, backward=False, name="trail_fwd")
    x3 = _pool_out(yd, W["pg"], pool_scale, x2)
    h3, gu1, act1, x4 = ffn_fwd(x3, g_f1, 1)

    dx4, d_fin, lossvec = _loss_head(x4, g_fin, tgt)

    dx3, d_f1 = ffn_bwd(dx4, x3, g_f1, h3, gu1, act1, 1, 0)
    dyd, d_scale, gw_pg = _pool_out_bwd(dx3, yd, W["pg"], pool_scale)
    du = _trail(dyd, backward=True, name="trail_bwd")
    gw_pi = _mm(h2, du, mode="tn", M=D, N=D, K=S, tm=D, tn=D, tk=2048, out_dtype=BF16, name="gw_pi")
    token = comm.send_grads(1, {"pg": gw_pg, "wpi": gw_pi})
    dh2 = _mm(du, W["wpi"], mode="nt", M=S, N=D, K=D, tm=1024, tn=D, tk=D, out_dtype=F32,
              deps=(token,), name="pool_in_bwd")
    dx2, d_pool = _rms_bwd(dh2, x2, pool_norm, dx3, "rms_pool_bwd")
    dx1, d_f0 = ffn_bwd(dx2, x1, g_f0, h1, gu0, act0, 0, 2)

    gw_o = _mm(o, dx1, mode="tn", M=D, N=D, K=S, tm=D, tn=D, tk=2048, out_dtype=BF16, name="gw_o")
    do = _mm(dx1, W["wo"], mode="nt", M=S, N=D, K=D, tm=1024, tn=D, tk=D, out_dtype=F32, name="attn_out_bwd")
    do_f, lse_ff, delta_f = _attn_bwd_prep(do, o, lse)
    dqkv_f, gw_qkv = [], None
    for g in range(NGROUPS):
        dqkv_f.append(_attn_bwd(qkv_f[g], do_f[g].reshape(S, D), lse_ff[g].reshape(S, HD),
                                delta_f[g].reshape(S, HD), bias[g], nbs[g], f"attn_bwd{g}"))
        gw_qkv = _mm(dqkv_f[g], hf[g], mode="tn", M=3 * D, N=D, K=S, tm=1024, tn=D, tk=2048, out_dtype=BF16,
                     out_rows=NGROUPS * 3 * D, out_off=3 * g, out_prev=gw_qkv, name=f"gw_qkv{g}")
    token = comm.send_grads(3, {"wo": gw_o, "qkv": gw_qkv})
    dh0_f = [_mm(dqkv_f[g], W["qkv"], mode="nn", M=S, N=D, K=3 * D, tm=1024, tn=D, tk=3 * D, out_dtype=F32,
                 b_off=(g, 0), deps=(token,), name=f"qkv_proj_bwd{g}") for g in range(NGROUPS)]
    folded = [dh0_f[g].reshape(dil, S // dil, D) for g, dil in enumerate(DILS) if dil > 1]
    grad_x, d_attn = _rms_bwd(dh0_f[0], x, g_attn, dx1, "rms_attn_bwd", folded=folded)

    vec = jnp.concatenate([d_attn, d_f0, d_f1, d_fin, d_pool, d_scale, lossvec, jnp.zeros((1, D), F32)], axis=0)
    return grad_x, vec


def _mesh_pos():
    x, y, c = lax.axis_index("x"), lax.axis_index("y"), lax.axis_index("c")
    return x, y, c, 4 * x + 2 * y + c


def _peer(x, y, c, k):
    kx, ky, kc = (k >> 2) & 1, (k >> 1) & 1, k & 1
    px = 1 - x if kx else x
    py = 1 - y if ky else y
    pc = 1 - c if kc else c
    return (px, py, pc), 4 * px + 2 * py + pc


ANY = pl.BlockSpec(memory_space=pl.ANY)


HBM = pl.BlockSpec(memory_space=pltpu.HBM)
SEMS = pl.BlockSpec(memory_space=pltpu.SEMAPHORE)
EFFECT = pltpu.SideEffectType.DATAFLOW_SIDE_EFFECTING
NPEER = NDEV - 1

AG_GROUPS = (("qkv",), ("wo", "gu0", "d0"), ("wpi", "pg", "gu1", "d1"))
AG_ORDER = tuple(n for grp in AG_GROUPS for n in grp)
RS_GROUPS = (("d1", "gu1"), ("pg", "wpi"), ("d0", "gu0"), ("wo", "qkv"))


def _hbm(a):
    return pltpu.with_memory_space_constraint(a, pltpu.HBM)


def _remote(src, dst, send, recv, peer):
    return pltpu.make_async_remote_copy(src_ref=src, dst_ref=dst, send_sem=send, recv_sem=recv, device_id=peer,
                                        device_id_type=pl.DeviceIdType.MESH)


def _bcast_all(v, name, deps=()):
    W = v.shape[1]
    nd = len(deps)

    def kern(v_ref, *rest):
        o_ref, send, recv, lsem = rest[nd:]
        x, y, c, me = _mesh_pos()
        own = pltpu.make_async_copy(v_ref, o_ref.at[me], lsem)
        own.start()
        cps = [_remote(v_ref, o_ref.at[me], send.at[k - 1], recv.at[k - 1], _peer(x, y, c, k)[0])
               for k in range(1, NDEV)]
        for cp in cps:
            cp.start()
        for cp in cps:
            cp.wait_recv()
            cp.wait_send()
        own.wait()

    return pl.pallas_call(
        kern, in_specs=[ANY] * (1 + nd), out_specs=ANY, out_shape=jax.ShapeDtypeStruct((NDEV, 8, W), F32),
        scratch_shapes=[pltpu.SemaphoreType.DMA((NPEER,)), pltpu.SemaphoreType.DMA((NPEER,)),
                        pltpu.SemaphoreType.DMA(())],
        name=name)(v, *deps)


ALL_KS = tuple(range(1, NDEV))
AG_KS1 = (1, 2, 4, 6)
AG_KS2 = (2, 4, 6)


def _split_start(srcs, src_of, lands, copy_refs, name, deps=(), ks=ALL_KS, to=None):
    ns, n, nd, nk = len(srcs), len(lands), len(deps), len(ks)

    def body(*refs):
        ins, land = refs[:ns], refs[ns:ns + n]
        send, recv = refs[ns + n + nd], refs[ns + n + nd + 1]
        token = refs[-1]
        x, y, c, me = _mesh_pos()
        for j in range(n):
            for i, k in enumerate(ks):
                _, pid = _peer(x, y, c, k)
                dest, _ = _peer(x, y, c, k if to is None else to)
                src, dst = copy_refs(j, (land[j] if src_of[j] is None else ins[src_of[j]]), land[j], me, pid)
                _remote(src, dst, send.at[j * nk + i], recv.at[j * nk + i], dest).start()
        token[...] = jnp.zeros_like(token)

    outs = pl.pallas_call(
        body, name=name,
        out_shape=(pltpu.SemaphoreType.DMA((n * nk,)), pltpu.SemaphoreType.DMA((n * nk,)))
        + tuple(pltpu.HBM(a.shape, a.dtype) for a in srcs) + tuple(pltpu.HBM(a.shape, a.dtype) for a in lands)
        + (jax.ShapeDtypeStruct((8, 128), F32),),
        in_specs=(HBM,) * (ns + n) + (ANY,) * nd,
        out_specs=(SEMS, SEMS) + (HBM,) * (ns + n) + (pl.BlockSpec(memory_space=pltpu.VMEM),),
        input_output_aliases={i: 2 + i for i in range(ns + n)},
        compiler_params=pltpu.CompilerParams(has_side_effects=EFFECT),
    )(*[_hbm(a) for a in srcs], *[_hbm(a) for a in lands], *deps)
    return outs[0], outs[1], list(outs[2:2 + ns]), list(outs[2 + ns:2 + ns + n]), outs[-1]


def _split_wait(srcs, src_of, lands, send, recv, sem_rows, wait_refs, after, name, ks=ALL_KS):
    ns, n, nk = len(srcs), len(lands), len(ks)

    def body(*refs):
        ins, land = refs[:ns], refs[ns:ns + n]
        send_ref, recv_ref = refs[ns + n], refs[ns + n + 1]
        x, y, c, me = _mesh_pos()
        for j in range(n):
            for i, k in enumerate(ks):
                peer, _ = _peer(x, y, c, k)
                src, dst = wait_refs(j, (land[j] if src_of[j] is None else ins[src_of[j]]), land[j])
                sem = sem_rows[j] * nk + i
                cp = _remote(src, dst, send_ref.at[sem], recv_ref.at[sem], peer)
                cp.wait_send()
                cp.wait_recv()

    outs = pl.pallas_call(
        body, name=name,
        out_shape=tuple(pltpu.HBM(a.shape, a.dtype) for a in srcs) + tuple(pltpu.HBM(a.shape, a.dtype) for a in lands),
        in_specs=(HBM,) * (ns + n) + (SEMS, SEMS, ANY),
        out_specs=(HBM,) * (ns + n),
        input_output_aliases={i: i for i in range(ns + n)},
        compiler_params=pltpu.CompilerParams(has_side_effects=EFFECT),
    )(*srcs, *lands, send, recv, after)
    return list(outs[:ns]), list(outs[ns:])


class _Comm:
    def __init__(self, shards, me, deps=()):
        names = AG_ORDER
        rows = [SEC_ROWS[n] for n in names]
        self.me = me
        lands = [lax.dynamic_update_slice(lax.empty((NDEV * r, D), BF16), shards[n], (_shard_pos(n, me), 0))
                 for n, r in zip(names, rows)]

        def copy_refs(j, src, land, me, pid):
            own = land.at[pl.ds(pl.multiple_of(_shard_pos(names[j], me), 16), rows[j])]
            return own, own

        self.ag_send, self.ag_recv, _, lands, self.ag_token = _split_start(
            [], [None] * len(names), lands, copy_refs, "ag_start", deps=deps, ks=AG_KS1)
        self.ag_land = dict(zip(names, lands))
        self.rs = []

    def weights(self, group, after):
        names = AG_GROUPS[group]
        idx = [AG_ORDER.index(n) for n in names]
        rows = [SEC_ROWS[n] for n in names]
        none = [None] * len(names)

        def wait_refs(j, src, land):
            return land.at[pl.ds(0, rows[j])], land.at[pl.ds(0, rows[j])]

        _, lands = _split_wait([], none, [self.ag_land[n] for n in names], self.ag_send, self.ag_recv, idx,
                               wait_refs, after, f"ag_wait{group}", ks=AG_KS1)

        def copy_refs(j, src, land, me, pid):
            theirs = land.at[pl.ds(pl.multiple_of(_shard_pos(names[j], pid), 16), rows[j])]
            return theirs, theirs

        send, recv, _, lands, token = _split_start([], none, lands, copy_refs, f"ag_pass{group}", ks=AG_KS2, to=1)
        _, lands = _split_wait([], none, lands, send, recv, list(range(len(names))), wait_refs, token,
                               f"ag_pass_wait{group}", ks=AG_KS2)
        return dict(zip(names, lands))

    def send_grads(self, group, gws):
        names = RS_GROUPS[group]
        rows = [SEC_ROWS[n] for n in names]
        grads = [gws[n] for n in names]
        me = self.me
        lands = [lax.dynamic_update_slice(
            lax.empty((NDEV, r, D), BF16),
            lax.dynamic_slice(g, (_shard_pos(n, me), 0), (r, D))[None], (me, 0, 0))
            for n, r, g in zip(names, rows, grads)]

        def copy_refs(j, src, land, me, pid):
            return src.at[pl.ds(pl.multiple_of(_shard_pos(names[j], pid), 16), rows[j])], land.at[me]

        send, recv, srcs, lands, token = _split_start(grads, list(range(len(names))), lands, copy_refs,
                                                      f"rs_start{group}")
        self.rs.append((names, rows, send, recv, srcs, lands))
        return token

    def received(self, group, after):
        names, rows, send, recv, srcs, lands = self.rs[group]

        def wait_refs(j, src, land):
            return src.at[pl.ds(0, rows[j])], land.at[0]

        _, lands = _split_wait(srcs, list(range(len(names))), lands, send, recv, list(range(len(names))), wait_refs,
                               after, f"rs_wait{group}")
        return dict(zip(names, lands))


def _sum_contributions(r_ref):
    g = r_ref[0].astype(F32)
    for dev in range(1, NDEV):
        g = g + r_ref[dev].astype(F32)
    return g


def _adam_math(g, w, m, v):
    c1 = 1.0 / (1.0 - ADAM_B1 ** ADAM_STEP)
    c2 = 1.0 / (1.0 - ADAM_B2 ** ADAM_STEP)
    mn = ADAM_B1 * m + (1.0 - ADAM_B1) * g
    vn = ADAM_B2 * v + (1.0 - ADAM_B2) * (g * g)
    return -ADAM_LR * ((mn * c1) / (jnp.sqrt(vn * c2) + ADAM_EPS) + ADAM_WD * w), mn, vn


def _adamw(R, w, m, v, *, tr, name):
    rows, C = w.shape

    def kern(r_ref, w_ref, m_ref, v_ref, g_out, d_out, m_out, v_out):
        g = _sum_contributions(r_ref)
        g_out[...] = g
        d_out[...], m_out[...], v_out[...] = _adam_math(g, w_ref[...], m_ref[...], v_ref[...])

    tile = pl.BlockSpec((tr, C), lambda i: (i, 0))
    shp = jax.ShapeDtypeStruct((rows, C), F32)
    return pl.pallas_call(
        kern, grid=(rows // tr,),
        in_specs=[pl.BlockSpec((NDEV, tr, C), lambda i: (0, i, 0)), tile, tile, tile],
        out_specs=[tile] * 4, out_shape=[shp] * 4, compiler_params=_cparams(1), name=name)(R, w, m, v)


def _adamw_pool_group(R, w, m, v):
    rows = SEC_ROWS["pg"]

    def kern(r_ref, w_ref, m_ref, v_ref, g_out, d_out, m_out, v_out):
        g = _sum_contributions(r_ref)
        g_out[0] = g
        d_out[0], m_out[0], v_out[0] = _adam_math(g, w_ref[0], m_ref[0], v_ref[0])

    blk = pl.BlockSpec((1, rows, PGD), lambda i: (i, 0, 0))
    shp = jax.ShapeDtypeStruct((POOL_G, rows, PGD), F32)
    return pl.pallas_call(
        kern, grid=(POOL_G,),
        in_specs=[pl.BlockSpec((NDEV, rows, PGD), lambda i: (0, 0, i)), blk, blk, blk],
        out_specs=[blk] * 4, out_shape=[shp] * 4, compiler_params=_cparams(1), name="adamw_pg")(R, w, m, v)


def _grad_sum_t(R, name):
    rows = R.shape[1]
    tr = 128 if rows % 128 == 0 else rows

    def kern(r_ref, o_ref):
        o_ref[...] = _sum_contributions(r_ref).T

    return pl.pallas_call(
        kern, grid=(rows // tr,), in_specs=[pl.BlockSpec((NDEV, tr, D), lambda i: (0, i, 0))],
        out_specs=pl.BlockSpec((D, tr), lambda i: (0, i)), out_shape=jax.ShapeDtypeStruct((D, rows), F32),
        compiler_params=_cparams(1), name=name)(R)


def _adam_plain(g, w, m, v, *, tr, name):
    rows, C = w.shape

    def kern(g_ref, w_ref, m_ref, v_ref, d_out, m_out, v_out):
        d_out[...], m_out[...], v_out[...] = _adam_math(g_ref[...], w_ref[...], m_ref[...], v_ref[...])

    tile = pl.BlockSpec((tr, C), lambda i: (i, 0))
    shp = jax.ShapeDtypeStruct((rows, C), F32)
    return pl.pallas_call(
        kern, grid=(rows // tr,), in_specs=[tile] * 4, out_specs=[tile] * 3, out_shape=[shp] * 3,
        compiler_params=_cparams(1), name=name)(g, w, m, v)


def _pack_sections(w_qkv, w_attn_out, w_pool_in, w_pool_group, w_ffn_gate_up, w_ffn_down):
    pg = w_pool_group[0].transpose(1, 0, 2).reshape(SEC_ROWS["pg"], D)
    return {"qkv": w_qkv[0].T, "wo": w_attn_out[0], "wpi": w_pool_in[0], "gu0": w_ffn_gate_up[0].T,
            "gu1": w_ffn_gate_up[1].T, "d0": w_ffn_down[0], "d1": w_ffn_down[1], "pg": pg}


def _vec_pack(attn_norm, ffn_norm, final_norm, pool_norm_sh, pool_scale_sh, me):
    def place(sh):
        return lax.dynamic_update_slice(jnp.zeros((1, D), F32), sh, (0, me * 128))
    return jnp.concatenate([attn_norm, ffn_norm, final_norm.reshape(1, D), place(pool_norm_sh),
                            place(pool_scale_sh), jnp.zeros((2, D), F32)], axis=0)


def _vec_unpack(p, me):
    def take(r):
        return lax.dynamic_slice(p[r:r + 1], (0, me * 128), (1, 128))
    return p[0:1], p[1:3], p[3], take(4), take(5)


def kernel(x, attn_norm, w_qkv, w_attn_out, pool_norm, w_pool_in, w_pool_group, pool_scale, ffn_norm, w_ffn_gate_up, w_ffn_down, final_norm, loss_target, m_attn_norm, m_w_qkv, m_w_attn_out, m_pool_norm, m_w_pool_in, m_w_pool_group, m_pool_scale, m_ffn_norm, m_w_ffn_gate_up, m_w_ffn_down, m_final_norm, v_attn_norm, v_w_qkv, v_w_attn_out, v_pool_norm, v_w_pool_in, v_w_pool_group, v_pool_scale, v_ffn_norm, v_w_ffn_gate_up, v_w_ffn_down, v_final_norm):
    me = 4 * lax.axis_index("x") + 2 * lax.axis_index("y") + lax.axis_index("c")

    pw = _pack_sections(w_qkv, w_attn_out, w_pool_in, w_pool_group, w_ffn_gate_up, w_ffn_down)
    vsh = jnp.concatenate([pool_norm, pool_scale, jnp.zeros((6, 128), F32)], axis=0)

    vg = _bcast_all(vsh, "gather_pool_vectors")
    comm = _Comm({n: pw[n].astype(BF16) for n, _ in SECTIONS}, me, deps=(vg,))
    pool_norm_full = vg[:, 0, :].reshape(1, D)
    pool_scale_full = vg[:, 1, :].reshape(1, D)

    grad_x, vec = _local_step(x[0], loss_target[0], comm, attn_norm, ffn_norm, final_norm,
                              pool_norm_full, pool_scale_full)

    vw = _vec_pack(attn_norm, ffn_norm, final_norm, pool_norm, pool_scale, me)
    vm = _vec_pack(m_attn_norm, m_ffn_norm, m_final_norm, m_pool_norm, m_pool_scale, me)
    vv = _vec_pack(v_attn_norm, v_ffn_norm, v_final_norm, v_pool_norm, v_pool_scale, me)

    gu_shape = w_ffn_gate_up.shape
    res = {}
    gu_t, d_res = {}, {}
    vec_out = None
    after = grad_x
    for group in range(len(RS_GROUPS)):
        if group == len(RS_GROUPS) - 1:
            VR = _bcast_all(vec, "exchange_vector_grads", deps=(after,))
            vec_out = _adamw(VR, vw, vm, vv, tr=8, name="adamw_vec")
            after = vec_out[0]
        for n, R in comm.received(group, after).items():
            if n in ("d0", "d1"):
                l = int(n[1])
                d_res[l] = _adamw(R, w_ffn_down[l], m_w_ffn_down[l], v_w_ffn_down[l], tr=352, name=f"adamw_{n}")
                after = d_res[l][0]
            elif n in ("gu0", "gu1"):
                gu_t[int(n[2])] = after = _grad_sum_t(R, f"grad_sum_{n}")
            elif n == "pg":
                out = _adamw_pool_group(R, w_pool_group[0], m_w_pool_group[0], v_w_pool_group[0])
                res["pg"] = tuple(a[None] for a in out)
                after = out[0]
            elif n in ("wo", "wpi"):
                w, m, v = ((w_attn_out, m_w_attn_out, v_w_attn_out) if n == "wo"
                           else (w_pool_in, m_w_pool_in, v_w_pool_in))
                res[n] = _adamw(R, w[0], m[0], v[0], tr=128, name=f"adamw_{n}")
                res[n] = tuple(a[None] for a in res[n])
                after = res[n][0]
            else:
                g = _grad_sum_t(R, "grad_sum_qkv")
                out = _adam_plain(g, w_qkv[0], m_w_qkv[0], v_w_qkv[0], tr=256, name="adamw_qkv")
                res["qkv"] = tuple(a[None] for a in (g,) + tuple(out))
                after = out[0]
    g_gu = jnp.stack([gu_t[0], gu_t[1]])
    flat = (2 * D, gu_shape[2])
    out = _adam_plain(g_gu.reshape(flat), w_ffn_gate_up.reshape(flat), m_w_ffn_gate_up.reshape(flat),
                      v_w_ffn_gate_up.reshape(flat), tr=256, name="adamw_gu")
    res["gu"] = (g_gu,) + tuple(a.reshape(gu_shape) for a in out)
    res["d"] = tuple(jnp.stack([d_res[0][kind], d_res[1][kind]]) for kind in range(4))

    outs = []
    for kind in range(4):
        an, fn, fin, pn, ps = _vec_unpack(vec_out[kind], me)
        outs.append((an, res["qkv"][kind], res["wo"][kind], pn, res["wpi"][kind], res["pg"][kind], ps, fn,
                     res["gu"][kind], res["d"][kind], fin))
    loss = 0.5 * jnp.sum(vec_out[0][6]) / D
    return (loss, grad_x[None]) + outs[0] + outs[1] + outs[2] + outs[3]
```

```python
import jax
import jax.numpy as jnp
from jax import lax
from jax.experimental import pallas as pl
from jax.experimental.pallas import tpu as pltpu

F32 = jnp.float32
BF16 = jnp.bfloat16

D = 1024
NDEV = 8
HEADS = 8
HD = 128
QB = 128
NGROUPS = 3
DILS = (1, 4, 16)
DFF = 2816
HCH = 1408
POOL_G = 4
PGD = 256
RMS_EPS = 1e-6
NEG = -1e30

ADAM_LR = 0.001
ADAM_B1 = 0.9
ADAM_B2 = 0.999
ADAM_EPS = 1e-08
ADAM_WD = 0.01
ADAM_STEP = 10

VMEM_LIMIT = 52 * 1024 * 1024

SECTIONS = (("qkv", 1152), ("wo", 128), ("wpi", 128), ("gu0", 704), ("gu1", 704),
            ("d0", 352), ("d1", 352), ("pg", 32))
LOC_OFF = {}
GLB_OFF = {}
_o = 0
for _n, _r in SECTIONS:
    LOC_OFF[_n] = _o
    GLB_OFF[_n] = _o * NDEV
    _o += _r
PACK_ROWS = _o
GLB_ROWS = PACK_ROWS * NDEV
SEC_ROWS = dict(SECTIONS)


def _cparams(n_grid):
    return pltpu.CompilerParams(dimension_semantics=("arbitrary",) * n_grid, vmem_limit_bytes=VMEM_LIMIT)


def _shard_pos(name, dev):
    n = SEC_ROWS[name]
    if name in ("gu0", "gu1"):
        return ((dev % 4) // 2) * (2 * HCH) + (dev // 4) * HCH + (dev % 2) * n
    return dev * n


def _mm(a, b, *, mode, M, N, K, tm, tn, tk, out_dtype, name, a_off=(0, 0), b_off=(0, 0), res=None,
        out_rows=None, out_off=0, out_prev=None, deps=()):
    nm, nn, nk = M // tm, N // tn, K // tk
    assert nm * tm == M and nn * tn == N and nk * tk == K
    if mode == "nn":
        a_bs, b_bs = (tm, tk), (tk, tn)
        a_ix = lambda i, j, k: (i, k)
        b_ix = lambda i, j, k: (k, j)
        dims = (((1,), (0,)), ((), ()))
    elif mode == "nt":
        a_bs, b_bs = (tm, tk), (tn, tk)
        a_ix = lambda i, j, k: (i, k)
        b_ix = lambda i, j, k: (j, k)
        dims = (((1,), (1,)), ((), ()))
    else:
        a_bs, b_bs = (tk, tm), (tk, tn)
        a_ix = lambda i, j, k: (k, i)
        b_ix = lambda i, j, k: (k, j)
        dims = (((0,), (0,)), ((), ()))

    def spec(bs, ix, off):
        def im(i, j, k):
            r, c = ix(i, j, k)
            return (r + off[0], c + off[1])
        return pl.BlockSpec(bs, im)

    in_specs = [spec(a_bs, a_ix, a_off), spec(b_bs, b_ix, b_off)]
    args = [a, b]
    if res is not None:
        in_specs.append(pl.BlockSpec((tm, tn), lambda i, j, k: (i, j)))
        args.append(res)
    out_shape = jax.ShapeDtypeStruct((M if out_rows is None else out_rows, N), out_dtype)
    out_spec = pl.BlockSpec((tm, tn), lambda i, j, k: (i + out_off, j))
    has_res = res is not None
    extra = list(deps) + ([out_prev] if out_prev is not None else [])
    for dep in extra:
        in_specs.append(pl.BlockSpec(memory_space=pl.ANY))
        args.append(dep)
    o_pos = 2 + int(has_res) + len(extra)
    aliases = {len(args) - 1: 0} if out_prev is not None else {}

    def kern(*refs):
        a_ref, b_ref = refs[0], refs[1]
        res_ref = refs[2] if has_res else None
        o_ref = refs[o_pos]
        av = a_ref[...]
        bv = b_ref[...]
        if av.dtype != BF16:
            av = av.astype(BF16)
        if bv.dtype != BF16:
            bv = bv.astype(BF16)
        part = lax.dot_general(av, bv, dims, preferred_element_type=F32)

        def write(val):
            if has_res:
                val = val + res_ref[...]
            o_ref[...] = val.astype(out_dtype)

        if nk == 1:
            write(part)
        else:
            acc_ref = refs[-1]
            k = pl.program_id(2)

            @pl.when(k == 0)
            def _():
                acc_ref[...] = part

            @pl.when(k > 0)
            def _():
                acc_ref[...] += part

            @pl.when(k == nk - 1)
            def _():
                write(acc_ref[...])

    scratch = [pltpu.VMEM((tm, tn), F32)] if nk > 1 else []
    return pl.pallas_call(
        kern, grid=(nm, nn, nk), in_specs=in_specs, out_specs=out_spec, out_shape=out_shape,
        scratch_shapes=scratch, input_output_aliases=aliases, compiler_params=_cparams(3), name=name)(*args)


def _mm_rms_bwd(a, b, x, g, dres, *, mode, M, K, tm, name, b_off=(0, 0), deps=()):
    nd = len(deps)
    b_bs = (K, D) if mode == "nn" else (D, K)
    dims = (((1,), (0,)), ((), ())) if mode == "nn" else (((1,), (1,)), ((), ()))

    def kern(a_ref, b_ref, x_ref, g_ref, dres_ref, *rest):
        dx_ref, dg_ref = rest[nd:]
        i = pl.program_id(0)
        av = a_ref[...]
        if av.dtype != BF16:
            av = av.astype(BF16)
        dhv = lax.dot_general(av, b_ref[...], dims, preferred_element_type=F32)
        xv = x_ref[...]
        r = lax.rsqrt(jnp.mean(xv * xv, axis=-1, keepdims=True) + RMS_EPS)
        xhat = xv * r
        gy = dhv * g_ref[...]
        dx_ref[...] = dres_ref[...] + r * (gy - xhat * jnp.mean(gy * xhat, axis=-1, keepdims=True))
        part = jnp.sum(dhv * xhat, axis=0, keepdims=True)

        @pl.when(i == 0)
        def _():
            dg_ref[...] = part

        @pl.when(i > 0)
        def _():
            dg_ref[...] += part

    row = pl.BlockSpec((tm, D), lambda i: (i, 0))
    vec = pl.BlockSpec((1, D), lambda i: (0, 0))
    return pl.pallas_call(
        kern, grid=(M // tm,),
        in_specs=[pl.BlockSpec((tm, K), lambda i: (i, 0)),
                  pl.BlockSpec(b_bs, lambda i: b_off, pipeline_mode=pl.Buffered(1)), row, vec, row]
        + [pl.BlockSpec(memory_space=pl.ANY)] * nd,
        out_specs=[row, vec],
        out_shape=[jax.ShapeDtypeStruct((M, D), F32), jax.ShapeDtypeStruct((1, D), F32)],
        compiler_params=_cparams(1), name=name)(a, b, x, g, dres, *deps)


def _rms_fwd(x, g, name, deps=()):
    S = x.shape[0]
    tr = 512

    def kern(x_ref, g_ref, *rest):
        h_ref = rest[-1]
        xv = x_ref[...]
        r = lax.rsqrt(jnp.mean(xv * xv, axis=-1, keepdims=True) + RMS_EPS)
        h_ref[...] = (xv * r * g_ref[...]).astype(BF16)

    return pl.pallas_call(
        kern, grid=(S // tr,),
        in_specs=[pl.BlockSpec((tr, D), lambda i: (i, 0)), pl.BlockSpec((1, D), lambda i: (0, 0))]
        + [pl.BlockSpec(memory_space=pl.ANY)] * len(deps),
        out_specs=pl.BlockSpec((tr, D), lambda i: (i, 0)),
        out_shape=jax.ShapeDtypeStruct((S, D), BF16), compiler_params=_cparams(1), name=name)(x, g, *deps)


def _chunks_put(scr, val):
    for c in range(scr.shape[0]):
        scr[c] = val[:, c * 128:(c + 1) * 128]


def _chunks_get(scr):
    return jnp.concatenate([scr[c] for c in range(scr.shape[0])], axis=1)


def _chunks_rows(scr, r, n, dil):
    return jnp.concatenate([scr.at[c][pl.ds(r, n, stride=dil), :] for c in range(scr.shape[0])], axis=1)


def _chunks_add_rows(scr, val, r, n, dil, accumulate):
    for c in range(scr.shape[0]):
        rows = pl.ds(r, n, stride=dil)
        piece = val[:, c * 128:(c + 1) * 128]
        tile = scr.at[c]
        tile[rows, :] = tile[rows, :] + piece if accumulate else piece


def _rms_fwd_folded(x, g, name, deps=()):
    S = x.shape[0]
    tr = 512
    dils = DILS[1:]

    def kern(x_ref, g_ref, *rest):
        outs, scr = rest[len(deps):-1], rest[-1]
        xv = x_ref[...]
        r = lax.rsqrt(jnp.mean(xv * xv, axis=-1, keepdims=True) + RMS_EPS)
        h = (xv * r * g_ref[...]).astype(BF16)
        outs[0][...] = h
        _chunks_put(scr, h.astype(F32))
        for o_ref, dil in zip(outs[1:], dils):
            for res in range(dil):
                o_ref[res] = _chunks_rows(scr, res, tr // dil, dil).astype(BF16)

    return pl.pallas_call(
        kern, grid=(S // tr,),
        in_specs=[pl.BlockSpec((tr, D), lambda i: (i, 0)), pl.BlockSpec((1, D), lambda i: (0, 0))]
        + [pl.BlockSpec(memory_space=pl.ANY)] * len(deps),
        out_specs=[pl.BlockSpec((tr, D), lambda i: (i, 0))]
        + [pl.BlockSpec((dil, tr // dil, D), lambda i: (0, i, 0)) for dil in dils],
        out_shape=[jax.ShapeDtypeStruct((S, D), BF16)]
        + [jax.ShapeDtypeStruct((dil, S // dil, D), BF16) for dil in dils],
        scratch_shapes=[pltpu.VMEM((D // 128, tr, 128), F32)],
        compiler_params=_cparams(1), name=name)(x, g, *deps)


def _rms_bwd(dh, x, g, dres, name, folded=()):
    S = x.shape[0]
    tr = 512
    nf = len(folded)

    def kern(dh_ref, *rest):
        f_refs = rest[:nf]
        x_ref, g_ref, dres_ref, dx_ref, dg_ref = rest[nf:nf + 5]
        i = pl.program_id(0)
        xv = x_ref[...]
        if nf:
            acc_ref = rest[nf + 5]
            _chunks_put(acc_ref, dh_ref[...].astype(F32))
            for f_ref in f_refs:
                dil = f_ref.shape[0]
                for res in range(dil):
                    _chunks_add_rows(acc_ref, f_ref[res], res, tr // dil, dil, True)
            dhv = _chunks_get(acc_ref)
        else:
            dhv = dh_ref[...].astype(F32)
        r = lax.rsqrt(jnp.mean(xv * xv, axis=-1, keepdims=True) + RMS_EPS)
        xhat = xv * r
        gy = dhv * g_ref[...]
        dx_ref[...] = dres_ref[...] + r * (gy - xhat * jnp.mean(gy * xhat, axis=-1, keepdims=True))
        part = jnp.sum(dhv * xhat, axis=0, keepdims=True)

        @pl.when(i == 0)
        def _():
            dg_ref[...] = part

        @pl.when(i > 0)
        def _():
            dg_ref[...] += part

    row = pl.BlockSpec((tr, D), lambda i: (i, 0))
    vec = pl.BlockSpec((1, D), lambda i: (0, 0))
    fspecs = [pl.BlockSpec((f.shape[0], tr // f.shape[0], D), lambda i: (0, i, 0)) for f in folded]
    return pl.pallas_call(
        kern, grid=(S // tr,), in_specs=[row] + fspecs + [row, vec, row], out_specs=[row, vec],
        out_shape=[jax.ShapeDtypeStruct((S, D), F32), jax.ShapeDtypeStruct((1, D), F32)],
        scratch_shapes=[pltpu.VMEM((D // 128, tr, 128), F32)] if nf else [],
        compiler_params=_cparams(1), name=name)(dh, *folded, x, g, dres)


def _loss_head(x, g, tgt):
    S = x.shape[0]
    tr = 512

    def kern(x_ref, g_ref, t_ref, dx_ref, dg_ref, ls_ref):
        i = pl.program_id(0)
        xv = x_ref[...]
        gv = g_ref[...]
        r = lax.rsqrt(jnp.mean(xv * xv, axis=-1, keepdims=True) + RMS_EPS)
        xhat = xv * r
        e = xhat * gv - t_ref[...]
        dy = e * (1.0 / D)
        gy = dy * gv
        dx_ref[...] = r * (gy - xhat * jnp.mean(gy * xhat, axis=-1, keepdims=True))
        dgp = jnp.sum(dy * xhat, axis=0, keepdims=True)
        lsp = jnp.sum(e * e, axis=0, keepdims=True)

        @pl.when(i == 0)
        def _():
            dg_ref[...] = dgp
            ls_ref[...] = lsp

        @pl.when(i > 0)
        def _():
            dg_ref[...] += dgp
            ls_ref[...] += lsp

    row = pl.BlockSpec((tr, D), lambda i: (i, 0))
    vec = pl.BlockSpec((1, D), lambda i: (0, 0))
    return pl.pallas_call(
        kern, grid=(S // tr,), in_specs=[row, vec, row], out_specs=[row, vec, vec],
        out_shape=[jax.ShapeDtypeStruct((S, D), F32), jax.ShapeDtypeStruct((1, D), F32),
                   jax.ShapeDtypeStruct((1, D), F32)],
        compiler_params=_cparams(1), name="loss_head")(x, g, tgt)


def _ffn_up(h, G, name):
    S = h.shape[0]
    tm = 512
    nj = DFF // HCH

    def kern(h_ref, w_ref, gu_ref, act_ref):
        gu = lax.dot_general(h_ref[...], w_ref[...], (((1,), (1,)), ((), ())), preferred_element_type=F32)
        gu_ref[...] = gu.astype(BF16)
        gate = gu[:, :HCH]
        up = gu[:, HCH:]
        act_ref[...] = (gate * jax.nn.sigmoid(gate) * up).astype(BF16)

    return pl.pallas_call(
        kern, grid=(nj, S // tm),
        in_specs=[pl.BlockSpec((tm, D), lambda j, i: (i, 0)),
                  pl.BlockSpec((2 * HCH, D), lambda j, i: (j, 0))],
        out_specs=[pl.BlockSpec((tm, 2 * HCH), lambda j, i: (i, j)),
                   pl.BlockSpec((tm, HCH), lambda j, i: (i, j))],
        out_shape=[jax.ShapeDtypeStruct((S, 2 * DFF), BF16), jax.ShapeDtypeStruct((S, DFF), BF16)],
        compiler_params=_cparams(2), name=name)(h, G)


def _ffn_down_bwd(dx, G, gu, name):
    S = dx.shape[0]
    tm = 512
    nj = DFF // HCH

    def kern(dx_ref, w_ref, gu_ref, o_ref):
        dact = lax.dot_general(dx_ref[...].astype(BF16), w_ref[...], (((1,), (1,)), ((), ())),
                               preferred_element_type=F32)
        gate = gu_ref[:, :HCH].astype(F32)
        up = gu_ref[:, HCH:].astype(F32)
        sig = jax.nn.sigmoid(gate)
        silu = gate * sig
        o_ref[:, :HCH] = (dact * up * (sig * (1.0 + gate * (1.0 - sig)))).astype(BF16)
        o_ref[:, HCH:] = (dact * silu).astype(BF16)

    return pl.pallas_call(
        kern, grid=(nj, S // tm),
        in_specs=[pl.BlockSpec((tm, D), lambda j, i: (i, 0)),
                  pl.BlockSpec((HCH, D), lambda j, i: (j, 0)),
                  pl.BlockSpec((tm, 2 * HCH), lambda j, i: (i, j))],
        out_specs=pl.BlockSpec((tm, 2 * HCH), lambda j, i: (i, j)),
        out_shape=jax.ShapeDtypeStruct((S, 2 * DFF), BF16),
        compiler_params=_cparams(2), name=name)(dx, G, gu)


def _trail(u, *, backward, name):
    S = u.shape[0]

    def kern(u_ref, o_ref):
        g = pl.program_id(0)
        uv = u_ref[...].astype(F32)
        row = lax.broadcasted_iota(jnp.int32, uv.shape, 0)
        win = jnp.left_shift(jnp.int32(2), g)
        cnt = jnp.minimum(row + 1, win).astype(F32)
        s = uv / cnt if backward else uv
        levels = []
        for k in (1, 2, 4, 8):
            if backward:
                sh = jnp.where(row < S - k, pltpu.roll(s, S - k, 0), 0.0)
            else:
                sh = jnp.where(row >= k, pltpu.roll(s, k, 0), 0.0)
            s = s + sh
            levels.append(s)
        sel = jnp.where(g == 0, levels[0], jnp.where(g == 1, levels[1], jnp.where(g == 2, levels[2], levels[3])))
        if backward:
            o_ref[...] = (sel - uv).astype(BF16)
        else:
            o_ref[...] = (sel / cnt - uv).astype(BF16)

    blk = pl.BlockSpec((S, PGD), lambda g: (0, g))
    return pl.pallas_call(
        kern, grid=(POOL_G,), in_specs=[blk], out_specs=blk,
        out_shape=jax.ShapeDtypeStruct((S, D), BF16), compiler_params=_cparams(1), name=name)(u)


def _pool_out(yd, G, scale, xres):
    S = yd.shape[0]
    tm = 1024

    def kern(y_ref, w_ref, s_ref, x_ref, o_ref):
        z = jnp.dot(y_ref[...], w_ref[...], preferred_element_type=F32)
        o_ref[...] = x_ref[...] + z * s_ref[...]

    tile = pl.BlockSpec((tm, PGD), lambda i, g: (i, g))
    return pl.pallas_call(
        kern, grid=(S // tm, POOL_G),
        in_specs=[tile, pl.BlockSpec((PGD, PGD), lambda i, g: (0, g)),
                  pl.BlockSpec((1, PGD), lambda i, g: (0, g)), tile],
        out_specs=tile, out_shape=jax.ShapeDtypeStruct((S, D), F32),
        compiler_params=_cparams(2), name="pool_out")(yd, G, scale, xres)


def _pool_out_bwd(dz, yd, G, scale):
    S = yd.shape[0]
    tm = 1024
    ni = S // tm

    def kern(dz_ref, y_ref, w_ref, s_ref, dy_ref, ds_ref, dw_ref, acc_ref):
        i = pl.program_id(1)
        dzv = dz_ref[...]
        yv = y_ref[...]
        wv = w_ref[...]
        zraw = jnp.dot(yv, wv, preferred_element_type=F32)
        dsp = jnp.sum(dzv * zraw, axis=0, keepdims=True)
        dzr = (dzv * s_ref[...]).astype(BF16)
        dy_ref[...] = lax.dot_general(dzr, wv, (((1,), (1,)), ((), ())), preferred_element_type=F32)
        dwp = lax.dot_general(yv, dzr, (((0,), (0,)), ((), ())), preferred_element_type=F32)

        @pl.when(i == 0)
        def _():
            ds_ref[...] = dsp
            acc_ref[...] = dwp

        @pl.when(i > 0)
        def _():
            ds_ref[...] += dsp
            acc_ref[...] += dwp

        @pl.when(i == ni - 1)
        def _():
            dw_ref[...] = acc_ref[...].astype(BF16)

    tile = pl.BlockSpec((tm, PGD), lambda g, i: (i, g))
    return pl.pallas_call(
        kern, grid=(POOL_G, ni),
        in_specs=[tile, tile, pl.BlockSpec((PGD, PGD), lambda g, i: (0, g)),
                  pl.BlockSpec((1, PGD), lambda g, i: (0, g))],
        out_specs=[tile, pl.BlockSpec((1, PGD), lambda g, i: (0, g)),
                   pl.BlockSpec((PGD, PGD), lambda g, i: (0, g))],
        out_shape=[jax.ShapeDtypeStruct((S, D), F32), jax.ShapeDtypeStruct((1, D), F32),
                   jax.ShapeDtypeStruct((PGD, D), BF16)],
        scratch_shapes=[pltpu.VMEM((PGD, PGD), F32)],
        compiler_params=_cparams(2), name="pool_out_bwd")(dz, yd, G, scale)


def _bias_table():
    qi = jnp.arange(QB)[:, None]
    ki = jnp.arange(2 * QB)[None, :]
    delta = QB + qi - ki
    inband = (delta >= 0) & (delta <= QB)
    n = NGROUPS * HEADS
    slopes = jnp.exp2(-8.0 * jnp.arange(1, n + 1, dtype=F32) / n).reshape(NGROUPS, HEADS)
    dil = jnp.asarray(DILS, F32)
    bias = -slopes[:, :, None, None] * (delta.astype(F32)[None, None] * dil[:, None, None, None])
    return jnp.where(inband[None, None], bias, NEG)


def _attn_fwd(qkv_f, bias, nb, name):
    S = qkv_f.shape[0]
    nblk = S // QB
    scale = HD ** -0.5

    def kern(q_ref, kc_ref, kp_ref, vc_ref, vp_ref, b_ref, o_ref, l_ref, s_scr, p_scr, r_scr):
        b = pl.program_id(0)
        has_prev = jnp.bitwise_and(b, nb - 1) != 0
        col = lax.broadcasted_iota(jnp.int32, (QB, 2 * QB), 1)
        dead = jnp.logical_and(col < QB, jnp.logical_not(has_prev))
        lane = lax.broadcasted_iota(jnp.int32, (QB, HD), 1)
        lse_all = jnp.zeros((QB, HD), F32)
        for h in range(HEADS):
            sl = slice(h * HD, (h + 1) * HD)
            kk = jnp.concatenate([kp_ref[:, sl], kc_ref[:, sl]], axis=0)
            s_scr[h] = lax.dot_general(q_ref[:, sl], kk, (((1,), (1,)), ((), ())), preferred_element_type=F32)
        for h in range(HEADS):
            s = s_scr[h] * scale + b_ref[h]
            s = jnp.where(dead, NEG, s)
            m = jnp.max(s, axis=-1, keepdims=True)
            p = jnp.exp(s - m)
            den = jnp.sum(p, axis=-1, keepdims=True)
            p_scr[h] = p.astype(BF16)
            r_scr[h] = jnp.broadcast_to(1.0 / den, (QB, HD))
            lse_all = jnp.where(lane == h, m + jnp.log(den), lse_all)
        for h in range(HEADS):
            sl = slice(h * HD, (h + 1) * HD)
            vv = jnp.concatenate([vp_ref[:, sl], vc_ref[:, sl]], axis=0)
            o = jnp.dot(p_scr[h], vv, preferred_element_type=F32) * r_scr[h]
            o_ref[:, sl] = o.astype(BF16)
        l_ref[...] = lse_all

    def blk(colblk, prev):
        if prev:
            return pl.BlockSpec((QB, D), lambda b: (jnp.maximum(b - 1, 0), colblk))
        return pl.BlockSpec((QB, D), lambda b: (b, colblk))

    return pl.pallas_call(
        kern, grid=(nblk,),
        in_specs=[blk(0, False), blk(1, False), blk(1, True), blk(2, False), blk(2, True),
                  pl.BlockSpec((HEADS, QB, 2 * QB), lambda b: (0, 0, 0))],
        out_specs=[pl.BlockSpec((QB, D), lambda b: (b, 0)), pl.BlockSpec((QB, HD), lambda b: (b, 0))],
        out_shape=[jax.ShapeDtypeStruct((S, D), BF16), jax.ShapeDtypeStruct((S, HD), F32)],
        scratch_shapes=[pltpu.VMEM((HEADS, QB, 2 * QB), F32), pltpu.VMEM((HEADS, QB, 2 * QB), BF16),
                        pltpu.VMEM((HEADS, QB, HD), F32)],
        compiler_params=_cparams(1), name=name)(qkv_f, qkv_f, qkv_f, qkv_f, qkv_f, bias)


def _natural(ref, scr, tm):
    dil = ref.shape[0]
    for res in range(dil):
        _chunks_add_rows(scr, ref[res].astype(F32), res, tm // dil, dil, False)
    return _chunks_get(scr)


def _attn_merge(os, lses):
    S = os[0].shape[0]
    tm = 512

    def kern(o0, o1, o2, l0, l1, l2, om_ref, lm_ref, ls1, ls2, os1, os2):
        la = l0[...]
        lb = _natural(l1, ls1, tm)
        lc = _natural(l2, ls2, tm)
        m = jnp.maximum(jnp.maximum(la, lb), lc)
        e0, e1, e2 = jnp.exp(la - m), jnp.exp(lb - m), jnp.exp(lc - m)
        tot = e0 + e1 + e2
        lm_ref[...] = m + jnp.log(tot)
        w0, w1, w2 = e0 / tot, e1 / tot, e2 / tot
        for res in range(o1.shape[0]):
            _chunks_add_rows(os1, o1[res].astype(F32), res, tm // o1.shape[0], o1.shape[0], False)
        for res in range(o2.shape[0]):
            _chunks_add_rows(os2, o2[res].astype(F32), res, tm // o2.shape[0], o2.shape[0], False)
        for h in range(HEADS):
            sl = slice(h * HD, (h + 1) * HD)
            acc = w0[:, h:h + 1] * o0[:, sl].astype(F32) + w1[:, h:h + 1] * os1[h] + w2[:, h:h + 1] * os2[h]
            om_ref[:, sl] = acc.astype(BF16)

    def spec(a, c):
        if a.ndim == 2:
            return pl.BlockSpec((tm, c), lambda i: (i, 0))
        return pl.BlockSpec((a.shape[0], tm // a.shape[0], c), lambda i: (0, i, 0))

    return pl.pallas_call(
        kern, grid=(S // tm,),
        in_specs=[spec(a, D) for a in os] + [spec(a, HD) for a in lses],
        out_specs=[pl.BlockSpec((tm, D), lambda i: (i, 0)), pl.BlockSpec((tm, HD), lambda i: (i, 0))],
        out_shape=[jax.ShapeDtypeStruct((S, D), BF16), jax.ShapeDtypeStruct((S, HD), F32)],
        scratch_shapes=[pltpu.VMEM((1, tm, HD), F32), pltpu.VMEM((1, tm, HD), F32),
                        pltpu.VMEM((HEADS, tm, HD), F32), pltpu.VMEM((HEADS, tm, HD), F32)],
        compiler_params=_cparams(1), name="attn_merge")(*os, *lses)


def _attn_bwd_prep(do, o, lse):
    S = o.shape[0]
    tm = 512
    dils = DILS[1:]

    def kern(do_ref, o_ref, l_ref, *rest):
        do_outs, l_outs, d_outs = rest[0:3], rest[3:5], rest[5:8]
        do_scr, l_scr, d_scr = rest[8:11]
        lane = lax.broadcasted_iota(jnp.int32, (tm, HD), 1)
        acc = jnp.zeros((tm, HD), F32)
        for h in range(HEADS):
            sl = slice(h * HD, (h + 1) * HD)
            prod = do_ref[:, sl] * o_ref[:, sl].astype(F32)
            acc = jnp.where(lane == h, jnp.sum(prod, axis=-1, keepdims=True), acc)
        d_scr[0] = acc
        l_scr[0] = l_ref[...]
        _chunks_put(do_scr, do_ref[...])
        do_outs[0][...] = do_ref[...].astype(BF16)
        d_outs[0][...] = acc
        for j, dil in enumerate(dils):
            for res in range(dil):
                n = tm // dil
                do_outs[1 + j][res] = _chunks_rows(do_scr, res, n, dil).astype(BF16)
                l_outs[j][res] = _chunks_rows(l_scr, res, n, dil)
                d_outs[1 + j][res] = _chunks_rows(d_scr, res, n, dil)

    def nat(c):
        return pl.BlockSpec((tm, c), lambda i: (i, 0))

    def fol(dil, c):
        return pl.BlockSpec((dil, tm // dil, c), lambda i: (0, i, 0))

    def shapes(c, dt, with_natural):
        first = [jax.ShapeDtypeStruct((S, c), dt)] if with_natural else []
        return first + [jax.ShapeDtypeStruct((dil, S // dil, c), dt) for dil in dils]

    outs = pl.pallas_call(
        kern, grid=(S // tm,), in_specs=[nat(D), nat(D), nat(HD)],
        out_specs=[nat(D)] + [fol(dil, D) for dil in dils] + [fol(dil, HD) for dil in dils]
        + [nat(HD)] + [fol(dil, HD) for dil in dils],
        out_shape=shapes(D, BF16, True) + shapes(HD, F32, False) + shapes(HD, F32, True),
        scratch_shapes=[pltpu.VMEM((HEADS, tm, HD), F32), pltpu.VMEM((1, tm, HD), F32), pltpu.VMEM((1, tm, HD), F32)],
        compiler_params=_cparams(1), name="attn_bwd_prep")(do, o, lse)
    return outs[0:3], [lse] + list(outs[3:5]), outs[5:8]


def _attn_bwd(qkv_f, do_f, lse_f, delta_f, bias, nb, name):
    S = qkv_f.shape[0]
    nblk = S // QB
    scale = HD ** -0.5

    def kern(q_ref, kc_ref, kp_ref, vc_ref, vp_ref, do_ref, l_ref, d_ref, b_ref, out_ref, dq_c, dk_c, dv_c,
             s_scr, dp_scr, ds_scr, p_scr):
        b = pl.program_id(0)

        @pl.when(b == 0)
        def _():
            dq_c[...] = jnp.zeros_like(dq_c)
            dk_c[...] = jnp.zeros_like(dk_c)
            dv_c[...] = jnp.zeros_like(dv_c)

        @pl.when(b == nblk)
        def _():
            out_ref[:, 0:D] = dq_c[...].astype(BF16)
            out_ref[:, D:2 * D] = dk_c[...].astype(BF16)
            out_ref[:, 2 * D:3 * D] = dv_c[...].astype(BF16)

        @pl.when(b < nblk)
        def _():
            has_prev = jnp.bitwise_and(b, nb - 1) != 0
            col = lax.broadcasted_iota(jnp.int32, (QB, 2 * QB), 1)
            dead = jnp.logical_and(col < QB, jnp.logical_not(has_prev))
            out_ref[:, 0:D] = dq_c[...].astype(BF16)
            lv = l_ref[...]
            dv_ = d_ref[...]
            for h in range(HEADS):
                sl = slice(h * HD, (h + 1) * HD)
                kk = jnp.concatenate([kp_ref[:, sl], kc_ref[:, sl]], axis=0)
                vv = jnp.concatenate([vp_ref[:, sl], vc_ref[:, sl]], axis=0)
                s_scr[h] = lax.dot_general(q_ref[:, sl], kk, (((1,), (1,)), ((), ())), preferred_element_type=F32)
                dp_scr[h] = lax.dot_general(do_ref[:, sl], vv, (((1,), (1,)), ((), ())),
                                            preferred_element_type=F32)
            for h in range(HEADS):
                s = s_scr[h] * scale + b_ref[h]
                s = jnp.where(dead, NEG, s)
                p = jnp.exp(s - lv[:, h:h + 1])
                ds_scr[h] = (p * (dp_scr[h] - dv_[:, h:h + 1]) * scale).astype(BF16)
                p_scr[h] = p.astype(BF16)
            for h in range(HEADS):
                sl = slice(h * HD, (h + 1) * HD)
                kk = jnp.concatenate([kp_ref[:, sl], kc_ref[:, sl]], axis=0)
                ds = ds_scr[h]
                dq_c[:, sl] = jnp.dot(ds, kk, preferred_element_type=F32)
                dkk = lax.dot_general(ds, q_ref[:, sl], (((0,), (0,)), ((), ())), preferred_element_type=F32)
                dvv = lax.dot_general(p_scr[h], do_ref[:, sl], (((0,), (0,)), ((), ())),
                                      preferred_element_type=F32)
                out_ref[:, D + h * HD:D + (h + 1) * HD] = (dk_c[:, sl] + dkk[:QB]).astype(BF16)
                out_ref[:, 2 * D + h * HD:2 * D + (h + 1) * HD] = (dv_c[:, sl] + dvv[:QB]).astype(BF16)
                dk_c[:, sl] = dkk[QB:]
                dv_c[:, sl] = dvv[QB:]

    last = nblk - 1

    def blk(colblk, prev):
        if prev:
            return pl.BlockSpec((QB, D), lambda b: (jnp.maximum(jnp.minimum(b, last) - 1, 0), colblk))
        return pl.BlockSpec((QB, D), lambda b: (jnp.minimum(b, last), colblk))

    stat = pl.BlockSpec((QB, HD), lambda b: (jnp.minimum(b, last), 0))
    return pl.pallas_call(
        kern, grid=(nblk + 1,),
        in_specs=[blk(0, False), blk(1, False), blk(1, True), blk(2, False), blk(2, True),
                  pl.BlockSpec((QB, D), lambda b: (jnp.minimum(b, last), 0)), stat, stat,
                  pl.BlockSpec((HEADS, QB, 2 * QB), lambda b: (0, 0, 0))],
        out_specs=pl.BlockSpec((QB, 3 * D), lambda b: (jnp.maximum(b - 1, 0), 0)),
        out_shape=jax.ShapeDtypeStruct((S, 3 * D), BF16),
        scratch_shapes=[pltpu.VMEM((QB, D), F32), pltpu.VMEM((QB, D), F32), pltpu.VMEM((QB, D), F32),
                        pltpu.VMEM((HEADS, QB, 2 * QB), F32), pltpu.VMEM((HEADS, QB, 2 * QB), F32),
                        pltpu.VMEM((HEADS, QB, 2 * QB), BF16), pltpu.VMEM((HEADS, QB, 2 * QB), BF16)],
        compiler_params=_cparams(1), name=name)(qkv_f, qkv_f, qkv_f, qkv_f, qkv_f, do_f, lse_f, delta_f, bias)


def _local_step(x, tgt, comm, attn_norm, ffn_norm, final_norm, pool_norm, pool_scale):
    S = x.shape[0]
    bias = _bias_table()
    g_attn = attn_norm.reshape(1, D)
    g_f0 = ffn_norm[0:1]
    g_f1 = ffn_norm[1:2]
    g_fin = final_norm.reshape(1, D)
    W = {}

    def ffn_fwd(xin, gain, l):
        h = _rms_fwd(xin, gain, f"rms_ffn{l}")
        gu, act = _ffn_up(h, W[f"gu{l}"], f"ffn_up{l}")
        xo = _mm(act, W[f"d{l}"], mode="nn", M=S, N=D, K=DFF, tm=1024, tn=D, tk=DFF, out_dtype=F32,
                 res=xin, name=f"ffn_down{l}")
        return h, gu, act, xo

    def ffn_bwd(dxo, xin, gain, h, gu, act, l, rs_group):
        dgu = _ffn_down_bwd(dxo, W[f"d{l}"], gu, f"ffn_down_bwd{l}")
        gw_d = _mm(act, dxo, mode="tn", M=DFF, N=D, K=S, tm=HCH, tn=D, tk=2048, out_dtype=BF16, name=f"gw_d{l}")
        gw_gu = _mm(dgu, h, mode="tn", M=2 * DFF, N=D, K=S, tm=HCH, tn=D, tk=2048, out_dtype=BF16, name=f"gw_gu{l}")
        token = comm.send_grads(rs_group, {f"d{l}": gw_d, f"gu{l}": gw_gu})
        return _mm_rms_bwd(dgu, W[f"gu{l}"], xin, gain, dxo, mode="nn", M=S, K=2 * DFF, tm=512, deps=(token,),
                           name=f"ffn_up_bwd{l}")

    nbs = [S // QB // dil for dil in DILS]
    hf = _rms_fwd_folded(x, g_attn, "rms_attn", deps=comm.ag_tokens)
    hf = [h.reshape(S, D) for h in hf]
    W.update(comm.weights(0, hf[0]))
    qkv_f, o_f, lse_f = [], [], []
    for g, dil in enumerate(DILS):
        qkv_f.append(_mm(hf[g], W["qkv"], mode="nt", M=S, N=3 * D, K=D, tm=2048, tn=1024, tk=D, out_dtype=BF16,
                         b_off=(3 * g, 0), name=f"qkv_proj{g}"))
        og, lg = _attn_fwd(qkv_f[g], bias[g], nbs[g], f"attn_fwd{g}")
        o_f.append(og if dil == 1 else og.reshape(dil, S // dil, D))
        lse_f.append(lg if dil == 1 else lg.reshape(dil, S // dil, HD))
    o, lse = _attn_merge(o_f, lse_f)
    W.update(comm.weights(1, o))
    x1 = _mm(o, W["wo"], mode="nn", M=S, N=D, K=D, tm=1024, tn=D, tk=D, out_dtype=F32, res=x, name="attn_out")
    h1, gu0, act0, x2 = ffn_fwd(x1, g_f0, 0)

    W.update(comm.weights(2, x2))
    h2 = _rms_fwd(x2, pool_norm, "rms_pool")
    u = _mm(h2, W["wpi"], mode="nn", M=S, N=D, K=D, tm=1024, tn=D, tk=D, out_dtype=F32, name="pool_in")
    yd = _trail(u, backward=False, name="trail_fwd")
    x3 = _pool_out(yd, W["pg"], pool_scale, x2)
    h3, gu1, act1, x4 = ffn_fwd(x3, g_f1, 1)

    dx4, d_fin, lossvec = _loss_head(x4, g_fin, tgt)

    dx3, d_f1 = ffn_bwd(dx4, x3, g_f1, h3, gu1, act1, 1, 0)
    dyd, d_scale, gw_pg = _pool_out_bwd(dx3, yd, W["pg"], pool_scale)
    du = _trail(dyd, backward=True, name="trail_bwd")
    gw_pi = _mm(h2, du, mode="tn", M=D, N=D, K=S, tm=D, tn=D, tk=2048, out_dtype=BF16, name="gw_pi")
    token = comm.send_grads(1, {"pg": gw_pg, "wpi": gw_pi})
    dx2, d_pool = _mm_rms_bwd(du, W["wpi"], x2, pool_norm, dx3, mode="nt", M=S, K=D, tm=1024, deps=(token,),
                              name="pool_in_bwd")
    dx1, d_f0 = ffn_bwd(dx2, x1, g_f0, h1, gu0, act0, 0, 2)

    gw_o = _mm(o, dx1, mode="tn", M=D, N=D, K=S, tm=D, tn=D, tk=2048, out_dtype=BF16, name="gw_o")
    do = _mm(dx1, W["wo"], mode="nt", M=S, N=D, K=D, tm=1024, tn=D, tk=D, out_dtype=F32, name="attn_out_bwd")
    do_f, lse_ff, delta_f = _attn_bwd_prep(do, o, lse)
    dqkv_f, gw_qkv = [], None
    for g in range(NGROUPS):
        dqkv_f.append(_attn_bwd(qkv_f[g], do_f[g].reshape(S, D), lse_ff[g].reshape(S, HD),
                                delta_f[g].reshape(S, HD), bias[g], nbs[g], f"attn_bwd{g}"))
        gw_qkv = _mm(dqkv_f[g], hf[g], mode="tn", M=3 * D, N=D, K=S, tm=1024, tn=D, tk=2048, out_dtype=BF16,
                     out_rows=NGROUPS * 3 * D, out_off=3 * g, out_prev=gw_qkv, name=f"gw_qkv{g}")
    token = comm.send_grads(3, {"wo": gw_o, "qkv": gw_qkv})
    dh0_f = [_mm(dqkv_f[g], W["qkv"], mode="nn", M=S, N=D, K=3 * D, tm=1024, tn=D, tk=3 * D, out_dtype=F32,
                 b_off=(g, 0), deps=(token,), name=f"qkv_proj_bwd{g}") for g in range(NGROUPS)]
    folded = [dh0_f[g].reshape(dil, S // dil, D) for g, dil in enumerate(DILS) if dil > 1]
    grad_x, d_attn = _rms_bwd(dh0_f[0], x, g_attn, dx1, "rms_attn_bwd", folded=folded)

    vec = jnp.concatenate([d_attn, d_f0, d_f1, d_fin, d_pool, d_scale, lossvec, jnp.zeros((1, D), F32)], axis=0)
    return grad_x, vec


def _mesh_pos():
    x, y, c = lax.axis_index("x"), lax.axis_index("y"), lax.axis_index("c")
    return x, y, c, 4 * x + 2 * y + c


def _peer(x, y, c, k):
    kx, ky, kc = (k >> 2) & 1, (k >> 1) & 1, k & 1
    px = 1 - x if kx else x
    py = 1 - y if ky else y
    pc = 1 - c if kc else c
    return (px, py, pc), 4 * px + 2 * py + pc


ANY = pl.BlockSpec(memory_space=pl.ANY)


HBM = pl.BlockSpec(memory_space=pltpu.HBM)
SEMS = pl.BlockSpec(memory_space=pltpu.SEMAPHORE)
EFFECT = pltpu.SideEffectType.DATAFLOW_SIDE_EFFECTING
NPEER = NDEV - 1

AG_GROUPS = (("qkv",), ("wo", "gu0", "d0"), ("wpi", "pg", "gu1", "d1"))
AG_ORDER = tuple(n for grp in AG_GROUPS for n in grp)
RS_GROUPS = (("d1", "gu1"), ("pg", "wpi"), ("d0", "gu0"), ("wo", "qkv"))


def _hbm(a):
    return pltpu.with_memory_space_constraint(a, pltpu.HBM)


def _remote(src, dst, send, recv, peer):
    return pltpu.make_async_remote_copy(src_ref=src, dst_ref=dst, send_sem=send, recv_sem=recv, device_id=peer,
                                        device_id_type=pl.DeviceIdType.MESH)


def _bcast_all(v, name, deps=()):
    W = v.shape[1]
    nd = len(deps)

    def kern(v_ref, *rest):
        o_ref, send, recv, lsem = rest[nd:]
        x, y, c, me = _mesh_pos()
        own = pltpu.make_async_copy(v_ref, o_ref.at[me], lsem)
        own.start()
        cps = [_remote(v_ref, o_ref.at[me], send.at[k - 1], recv.at[k - 1], _peer(x, y, c, k)[0])
               for k in range(1, NDEV)]
        for cp in cps:
            cp.start()
        for cp in cps:
            cp.wait_recv()
            cp.wait_send()
        own.wait()

    return pl.pallas_call(
        kern, in_specs=[ANY] * (1 + nd), out_specs=ANY, out_shape=jax.ShapeDtypeStruct((NDEV, 8, W), F32),
        scratch_shapes=[pltpu.SemaphoreType.DMA((NPEER,)), pltpu.SemaphoreType.DMA((NPEER,)),
                        pltpu.SemaphoreType.DMA(())],
        name=name)(v, *deps)


ALL_KS = tuple(range(1, NDEV))
AG_KS1 = (1, 2, 4, 6)
AG_KS2 = (2, 4, 6)


def _split_start(srcs, src_of, lands, copy_refs, name, deps=(), ks=ALL_KS, to=None):
    ns, n, nd, nk = len(srcs), len(lands), len(deps), len(ks)

    def body(*refs):
        ins, land = refs[:ns], refs[ns:ns + n]
        send, recv = refs[ns + n + nd], refs[ns + n + nd + 1]
        token = refs[-1]
        x, y, c, me = _mesh_pos()
        for j in range(n):
            for i, k in enumerate(ks):
                _, pid = _peer(x, y, c, k)
                dest, _ = _peer(x, y, c, k if to is None else to)
                src, dst = copy_refs(j, (land[j] if src_of[j] is None else ins[src_of[j]]), land[j], me, pid)
                _remote(src, dst, send.at[j * nk + i], recv.at[j * nk + i], dest).start()
        token[...] = jnp.zeros_like(token)

    outs = pl.pallas_call(
        body, name=name,
        out_shape=(pltpu.SemaphoreType.DMA((n * nk,)), pltpu.SemaphoreType.DMA((n * nk,)))
        + tuple(pltpu.HBM(a.shape, a.dtype) for a in srcs) + tuple(pltpu.HBM(a.shape, a.dtype) for a in lands)
        + (jax.ShapeDtypeStruct((8, 128), F32),),
        in_specs=(HBM,) * (ns + n) + (ANY,) * nd,
        out_specs=(SEMS, SEMS) + (HBM,) * (ns + n) + (pl.BlockSpec(memory_space=pltpu.VMEM),),
        input_output_aliases={i: 2 + i for i in range(ns + n)},
        compiler_params=pltpu.CompilerParams(has_side_effects=EFFECT),
    )(*[_hbm(a) for a in srcs], *[_hbm(a) for a in lands], *deps)
    return outs[0], outs[1], list(outs[2:2 + ns]), list(outs[2 + ns:2 + ns + n]), outs[-1]


def _split_wait(srcs, src_of, lands, send, recv, sem_rows, wait_refs, after, name, ks=ALL_KS):
    ns, n, nk = len(srcs), len(lands), len(ks)

    def body(*refs):
        ins, land = refs[:ns], refs[ns:ns + n]
        send_ref, recv_ref = refs[ns + n], refs[ns + n + 1]
        x, y, c, me = _mesh_pos()
        for j in range(n):
            for i, k in enumerate(ks):
                peer, _ = _peer(x, y, c, k)
                src, dst = wait_refs(j, (land[j] if src_of[j] is None else ins[src_of[j]]), land[j])
                sem = sem_rows[j] * nk + i
                cp = _remote(src, dst, send_ref.at[sem], recv_ref.at[sem], peer)
                cp.wait_send()
                cp.wait_recv()

    outs = pl.pallas_call(
        body, name=name,
        out_shape=tuple(pltpu.HBM(a.shape, a.dtype) for a in srcs) + tuple(pltpu.HBM(a.shape, a.dtype) for a in lands),
        in_specs=(HBM,) * (ns + n) + (SEMS, SEMS, ANY),
        out_specs=(HBM,) * (ns + n),
        input_output_aliases={i: i for i in range(ns + n)},
        compiler_params=pltpu.CompilerParams(has_side_effects=EFFECT),
    )(*srcs, *lands, send, recv, after)
    return list(outs[:ns]), list(outs[ns:])


class _Comm:
    def __init__(self, shards, me, deps=()):
        self.me = me
        self.ag_land, self.ag_sems, self.ag_tokens = {}, {}, ()
        self.rs = []
        for part, names in enumerate((AG_GROUPS[0], AG_ORDER[len(AG_GROUPS[0]):])):
            rows = [SEC_ROWS[n] for n in names]
            lands = [lax.dynamic_update_slice(lax.empty((NDEV * r, D), BF16), shards[n], (_shard_pos(n, me), 0))
                     for n, r in zip(names, rows)]

            def copy_refs(j, src, land, me, pid, names=names, rows=rows):
                own = land.at[pl.ds(pl.multiple_of(_shard_pos(names[j], me), 16), rows[j])]
                return own, own

            send, recv, _, lands, token = _split_start([], [None] * len(names), lands, copy_refs, f"ag_start{part}",
                                                       deps=deps, ks=AG_KS1)
            deps = (token,)
            self.ag_tokens += (token,)
            for j, n in enumerate(names):
                self.ag_land[n] = lands[j]
                self.ag_sems[n] = (send, recv, j)

    def weights(self, group, after):
        names = AG_GROUPS[group]
        send, recv = self.ag_sems[names[0]][:2]
        idx = [self.ag_sems[n][2] for n in names]
        rows = [SEC_ROWS[n] for n in names]
        none = [None] * len(names)

        def wait_refs(j, src, land):
            return land.at[pl.ds(0, rows[j])], land.at[pl.ds(0, rows[j])]

        _, lands = _split_wait([], none, [self.ag_land[n] for n in names], send, recv, idx,
                               wait_refs, after, f"ag_wait{group}", ks=AG_KS1)

        def copy_refs(j, src, land, me, pid):
            theirs = land.at[pl.ds(pl.multiple_of(_shard_pos(names[j], pid), 16), rows[j])]
            return theirs, theirs

        send, recv, _, lands, token = _split_start([], none, lands, copy_refs, f"ag_pass{group}", ks=AG_KS2, to=1)
        _, lands = _split_wait([], none, lands, send, recv, list(range(len(names))), wait_refs, token,
                               f"ag_pass_wait{group}", ks=AG_KS2)
        return dict(zip(names, lands))

    def send_grads(self, group, gws):
        names = RS_GROUPS[group]
        rows = [SEC_ROWS[n] for n in names]
        grads = [gws[n] for n in names]
        me = self.me
        lands = [lax.dynamic_update_slice(
            lax.empty((NDEV, r, D), BF16),
            lax.dynamic_slice(g, (_shard_pos(n, me), 0), (r, D))[None], (me, 0, 0))
            for n, r, g in zip(names, rows, grads)]

        def copy_refs(j, src, land, me, pid):
            return src.at[pl.ds(pl.multiple_of(_shard_pos(names[j], pid), 16), rows[j])], land.at[me]

        send, recv, srcs, lands, token = _split_start(grads, list(range(len(names))), lands, copy_refs,
                                                      f"rs_start{group}")
        self.rs.append((names, rows, send, recv, srcs, lands))
        return token

    def received(self, group, after):
        names, rows, send, recv, srcs, lands = self.rs[group]

        def wait_refs(j, src, land):
            return src.at[pl.ds(0, rows[j])], land.at[0]

        _, lands = _split_wait(srcs, list(range(len(names))), lands, send, recv, list(range(len(names))), wait_refs,
                               after, f"rs_wait{group}")
        return dict(zip(names, lands))


def _sum_contributions(r_ref):
    g = r_ref[0].astype(F32)
    for dev in range(1, NDEV):
        g = g + r_ref[dev].astype(F32)
    return g


def _adam_math(g, w, m, v):
    c1 = 1.0 / (1.0 - ADAM_B1 ** ADAM_STEP)
    c2 = 1.0 / (1.0 - ADAM_B2 ** ADAM_STEP)
    mn = ADAM_B1 * m + (1.0 - ADAM_B1) * g
    vn = ADAM_B2 * v + (1.0 - ADAM_B2) * (g * g)
    return -ADAM_LR * ((mn * c1) / (jnp.sqrt(vn * c2) + ADAM_EPS) + ADAM_WD * w), mn, vn


def _adamw(R, w, m, v, *, tr, name):
    rows, C = w.shape

    def kern(r_ref, w_ref, m_ref, v_ref, g_out, d_out, m_out, v_out):
        g = _sum_contributions(r_ref)
        g_out[...] = g
        d_out[...], m_out[...], v_out[...] = _adam_math(g, w_ref[...], m_ref[...], v_ref[...])

    tile = pl.BlockSpec((tr, C), lambda i: (i, 0))
    shp = jax.ShapeDtypeStruct((rows, C), F32)
    return pl.pallas_call(
        kern, grid=(rows // tr,),
        in_specs=[pl.BlockSpec((NDEV, tr, C), lambda i: (0, i, 0)), tile, tile, tile],
        out_specs=[tile] * 4, out_shape=[shp] * 4, compiler_params=_cparams(1), name=name)(R, w, m, v)


def _adamw_pool_group(R, w, m, v):
    rows = SEC_ROWS["pg"]

    def kern(r_ref, w_ref, m_ref, v_ref, g_out, d_out, m_out, v_out):
        g = _sum_contributions(r_ref)
        g_out[0] = g
        d_out[0], m_out[0], v_out[0] = _adam_math(g, w_ref[0], m_ref[0], v_ref[0])

    blk = pl.BlockSpec((1, rows, PGD), lambda i: (i, 0, 0))
    shp = jax.ShapeDtypeStruct((POOL_G, rows, PGD), F32)
    return pl.pallas_call(
        kern, grid=(POOL_G,),
        in_specs=[pl.BlockSpec((NDEV, rows, PGD), lambda i: (0, 0, i)), blk, blk, blk],
        out_specs=[blk] * 4, out_shape=[shp] * 4, compiler_params=_cparams(1), name="adamw_pg")(R, w, m, v)


def _grad_sum_t(R, name):
    rows = R.shape[1]
    tr = 128 if rows % 128 == 0 else rows

    def kern(r_ref, o_ref):
        o_ref[...] = _sum_contributions(r_ref).T

    return pl.pallas_call(
        kern, grid=(rows // tr,), in_specs=[pl.BlockSpec((NDEV, tr, D), lambda i: (0, i, 0))],
        out_specs=pl.BlockSpec((D, tr), lambda i: (0, i)), out_shape=jax.ShapeDtypeStruct((D, rows), F32),
        compiler_params=_cparams(1), name=name)(R)


def _adam_plain(g, w, m, v, *, tr, name):
    rows, C = w.shape

    def kern(g_ref, w_ref, m_ref, v_ref, d_out, m_out, v_out):
        d_out[...], m_out[...], v_out[...] = _adam_math(g_ref[...], w_ref[...], m_ref[...], v_ref[...])

    tile = pl.BlockSpec((tr, C), lambda i: (i, 0))
    shp = jax.ShapeDtypeStruct((rows, C), F32)
    return pl.pallas_call(
        kern, grid=(rows // tr,), in_specs=[tile] * 4, out_specs=[tile] * 3, out_shape=[shp] * 3,
        compiler_params=_cparams(1), name=name)(g, w, m, v)


def _pack_sections(w_qkv, w_attn_out, w_pool_in, w_pool_group, w_ffn_gate_up, w_ffn_down):
    pg = w_pool_group[0].transpose(1, 0, 2).reshape(SEC_ROWS["pg"], D)
    return {"qkv": w_qkv[0].T, "wo": w_attn_out[0], "wpi": w_pool_in[0], "gu0": w_ffn_gate_up[0].T,
            "gu1": w_ffn_gate_up[1].T, "d0": w_ffn_down[0], "d1": w_ffn_down[1], "pg": pg}


def _vec_pack(attn_norm, ffn_norm, final_norm, pool_norm_sh, pool_scale_sh, me):
    def place(sh):
        return lax.dynamic_update_slice(jnp.zeros((1, D), F32), sh, (0, me * 128))
    return jnp.concatenate([attn_norm, ffn_norm, final_norm.reshape(1, D), place(pool_norm_sh),
                            place(pool_scale_sh), jnp.zeros((2, D), F32)], axis=0)


def _vec_unpack(p, me):
    def take(r):
        return lax.dynamic_slice(p[r:r + 1], (0, me * 128), (1, 128))
    return p[0:1], p[1:3], p[3], take(4), take(5)


def kernel(x, attn_norm, w_qkv, w_attn_out, pool_norm, w_pool_in, w_pool_group, pool_scale, ffn_norm, w_ffn_gate_up, w_ffn_down, final_norm, loss_target, m_attn_norm, m_w_qkv, m_w_attn_out, m_pool_norm, m_w_pool_in, m_w_pool_group, m_pool_scale, m_ffn_norm, m_w_ffn_gate_up, m_w_ffn_down, m_final_norm, v_attn_norm, v_w_qkv, v_w_attn_out, v_pool_norm, v_w_pool_in, v_w_pool_group, v_pool_scale, v_ffn_norm, v_w_ffn_gate_up, v_w_ffn_down, v_final_norm):
    me = 4 * lax.axis_index("x") + 2 * lax.axis_index("y") + lax.axis_index("c")

    pw = _pack_sections(w_qkv, w_attn_out, w_pool_in, w_pool_group, w_ffn_gate_up, w_ffn_down)
    vsh = jnp.concatenate([pool_norm, pool_scale, jnp.zeros((6, 128), F32)], axis=0)

    vg = _bcast_all(vsh, "gather_pool_vectors")
    comm = _Comm({n: pw[n].astype(BF16) for n, _ in SECTIONS}, me, deps=(vg,))
    pool_norm_full = vg[:, 0, :].reshape(1, D)
    pool_scale_full = vg[:, 1, :].reshape(1, D)

    grad_x, vec = _local_step(x[0], loss_target[0], comm, attn_norm, ffn_norm, final_norm,
                              pool_norm_full, pool_scale_full)

    vw = _vec_pack(attn_norm, ffn_norm, final_norm, pool_norm, pool_scale, me)
    vm = _vec_pack(m_attn_norm, m_ffn_norm, m_final_norm, m_pool_norm, m_pool_scale, me)
    vv = _vec_pack(v_attn_norm, v_ffn_norm, v_final_norm, v_pool_norm, v_pool_scale, me)

    gu_shape = w_ffn_gate_up.shape
    res = {}
    gu_t, d_res = {}, {}
    vec_out = None
    after = grad_x
    for group in range(len(RS_GROUPS)):
        if group == len(RS_GROUPS) - 1:
            VR = _bcast_all(vec, "exchange_vector_grads", deps=(after,))
            vec_out = _adamw(VR, vw, vm, vv, tr=8, name="adamw_vec")
            after = vec_out[0]
        for n, R in comm.received(group, after).items():
            if n in ("d0", "d1"):
                l = int(n[1])
                d_res[l] = _adamw(R, w_ffn_down[l], m_w_ffn_down[l], v_w_ffn_down[l], tr=352, name=f"adamw_{n}")
                after = d_res[l][0]
            elif n in ("gu0", "gu1"):
                gu_t[int(n[2])] = after = _grad_sum_t(R, f"grad_sum_{n}")
            elif n == "pg":
                out = _adamw_pool_group(R, w_pool_group[0], m_w_pool_group[0], v_w_pool_group[0])
                res["pg"] = tuple(a[None] for a in out)
                after = out[0]
            elif n in ("wo", "wpi"):
                w, m, v = ((w_attn_out, m_w_attn_out, v_w_attn_out) if n == "wo"
                           else (w_pool_in, m_w_pool_in, v_w_pool_in))
                res[n] = _adamw(R, w[0], m[0], v[0], tr=128, name=f"adamw_{n}")
                res[n] = tuple(a[None] for a in res[n])
                after = res[n][0]
            else:
                g = _grad_sum_t(R, "grad_sum_qkv")
                out = _adam_plain(g, w_qkv[0], m_w_qkv[0], v_w_qkv[0], tr=256, name="adamw_qkv")
                res["qkv"] = tuple(a[None] for a in (g,) + tuple(out))
                after = out[0]
    g_gu = jnp.stack([gu_t[0], gu_t[1]])
    flat = (2 * D, gu_shape[2])
    out = _adam_plain(g_gu.reshape(flat), w_ffn_gate_up.reshape(flat), m_w_ffn_gate_up.reshape(flat),
                      v_w_ffn_gate_up.reshape(flat), tr=256, name="adamw_gu")
    res["gu"] = (g_gu,) + tuple(a.reshape(gu_shape) for a in out)
    res["d"] = tuple(jnp.stack([d_res[0][kind], d_res[1][kind]]) for kind in range(4))

    outs = []
    for kind in range(4):
        an, fn, fin, pn, ps = _vec_unpack(vec_out[kind], me)
        outs.append((an, res["qkv"][kind], res["wo"][kind], pn, res["wpi"][kind], res["pg"][kind], ps, fn,
                     res["gu"][kind], res["d"][kind], fin))
    loss = 0.5 * jnp.sum(vec_out[0][6]) / D
    return (loss, grad_x[None]) + outs[0] + outs[1] + outs[2] + outs[3]
```

```python
import jax
import jax.numpy as jnp
from jax import lax
from jax.experimental import pallas as pl
from jax.experimental.pallas import tpu as pltpu

F32 = jnp.float32
BF16 = jnp.bfloat16

D = 1024
NDEV = 8
HEADS = 8
HD = 128
QB = 128
NGROUPS = 3
DILS = (1, 4, 16)
DFF = 2816
HCH = 1408
POOL_G = 4
PGD = 256
RMS_EPS = 1e-6
NEG = -1e30

ADAM_LR = 0.001
ADAM_B1 = 0.9
ADAM_B2 = 0.999
ADAM_EPS = 1e-08
ADAM_WD = 0.01
ADAM_STEP = 10

VMEM_LIMIT = 52 * 1024 * 1024

SECTIONS = (("qkv", 1152), ("wo", 128), ("wpi", 128), ("gu0", 704), ("gu1", 704),
            ("d0", 352), ("d1", 352), ("pg", 32))
LOC_OFF = {}
GLB_OFF = {}
_o = 0
for _n, _r in SECTIONS:
    LOC_OFF[_n] = _o
    GLB_OFF[_n] = _o * NDEV
    _o += _r
PACK_ROWS = _o
GLB_ROWS = PACK_ROWS * NDEV
SEC_ROWS = dict(SECTIONS)


def _cparams(n_grid):
    return pltpu.CompilerParams(dimension_semantics=("arbitrary",) * n_grid, vmem_limit_bytes=VMEM_LIMIT)


def _shard_pos(name, dev):
    n = SEC_ROWS[name]
    if name in ("gu0", "gu1"):
        return ((dev % 4) // 2) * (2 * HCH) + (dev // 4) * HCH + (dev % 2) * n
    return dev * n


def _mm(a, b, *, mode, M, N, K, tm, tn, tk, out_dtype, name, a_off=(0, 0), b_off=(0, 0), res=None,
        out_rows=None, out_off=0, out_prev=None, deps=()):
    nm, nn, nk = M // tm, N // tn, K // tk
    assert nm * tm == M and nn * tn == N and nk * tk == K
    if mode == "nn":
        a_bs, b_bs = (tm, tk), (tk, tn)
        a_ix = lambda i, j, k: (i, k)
        b_ix = lambda i, j, k: (k, j)
        dims = (((1,), (0,)), ((), ()))
    elif mode == "nt":
        a_bs, b_bs = (tm, tk), (tn, tk)
        a_ix = lambda i, j, k: (i, k)
        b_ix = lambda i, j, k: (j, k)
        dims = (((1,), (1,)), ((), ()))
    else:
        a_bs, b_bs = (tk, tm), (tk, tn)
        a_ix = lambda i, j, k: (k, i)
        b_ix = lambda i, j, k: (k, j)
        dims = (((0,), (0,)), ((), ()))

    def spec(bs, ix, off):
        def im(i, j, k):
            r, c = ix(i, j, k)
            return (r + off[0], c + off[1])
        return pl.BlockSpec(bs, im)

    in_specs = [spec(a_bs, a_ix, a_off), spec(b_bs, b_ix, b_off)]
    args = [a, b]
    if res is not None:
        in_specs.append(pl.BlockSpec((tm, tn), lambda i, j, k: (i, j)))
        args.append(res)
    out_shape = jax.ShapeDtypeStruct((M if out_rows is None else out_rows, N), out_dtype)
    out_spec = pl.BlockSpec((tm, tn), lambda i, j, k: (i + out_off, j))
    has_res = res is not None
    extra = list(deps) + ([out_prev] if out_prev is not None else [])
    for dep in extra:
        in_specs.append(pl.BlockSpec(memory_space=pl.ANY))
        args.append(dep)
    o_pos = 2 + int(has_res) + len(extra)
    aliases = {len(args) - 1: 0} if out_prev is not None else {}

    def kern(*refs):
        a_ref, b_ref = refs[0], refs[1]
        res_ref = refs[2] if has_res else None
        o_ref = refs[o_pos]
        av = a_ref[...]
        bv = b_ref[...]
        if av.dtype != BF16:
            av = av.astype(BF16)
        if bv.dtype != BF16:
            bv = bv.astype(BF16)
        part = lax.dot_general(av, bv, dims, preferred_element_type=F32)

        def write(val):
            if has_res:
                val = val + res_ref[...]
            o_ref[...] = val.astype(out_dtype)

        if nk == 1:
            write(part)
        else:
            acc_ref = refs[-1]
            k = pl.program_id(2)

            @pl.when(k == 0)
            def _():
                acc_ref[...] = part

            @pl.when(k > 0)
            def _():
                acc_ref[...] += part

            @pl.when(k == nk - 1)
            def _():
                write(acc_ref[...])

    scratch = [pltpu.VMEM((tm, tn), F32)] if nk > 1 else []
    return pl.pallas_call(
        kern, grid=(nm, nn, nk), in_specs=in_specs, out_specs=out_spec, out_shape=out_shape,
        scratch_shapes=scratch, input_output_aliases=aliases, compiler_params=_cparams(3), name=name)(*args)


def _mm_rms_bwd(a, b, x, g, dres, *, mode, M, K, tm, name, b_off=(0, 0), deps=()):
    nd = len(deps)
    b_bs = (K, D) if mode == "nn" else (D, K)
    dims = (((1,), (0,)), ((), ())) if mode == "nn" else (((1,), (1,)), ((), ()))

    def kern(a_ref, b_ref, x_ref, g_ref, dres_ref, *rest):
        dx_ref, dg_ref = rest[nd:]
        i = pl.program_id(0)
        av = a_ref[...]
        if av.dtype != BF16:
            av = av.astype(BF16)
        dhv = lax.dot_general(av, b_ref[...], dims, preferred_element_type=F32)
        xv = x_ref[...]
        r = lax.rsqrt(jnp.mean(xv * xv, axis=-1, keepdims=True) + RMS_EPS)
        xhat = xv * r
        gy = dhv * g_ref[...]
        dx_ref[...] = dres_ref[...] + r * (gy - xhat * jnp.mean(gy * xhat, axis=-1, keepdims=True))
        part = jnp.sum(dhv * xhat, axis=0, keepdims=True)

        @pl.when(i == 0)
        def _():
            dg_ref[...] = part

        @pl.when(i > 0)
        def _():
            dg_ref[...] += part

    row = pl.BlockSpec((tm, D), lambda i: (i, 0))
    vec = pl.BlockSpec((1, D), lambda i: (0, 0))
    return pl.pallas_call(
        kern, grid=(M // tm,),
        in_specs=[pl.BlockSpec((tm, K), lambda i: (i, 0)),
                  pl.BlockSpec(b_bs, lambda i: b_off, pipeline_mode=pl.Buffered(1)), row, vec, row]
        + [pl.BlockSpec(memory_space=pl.ANY)] * nd,
        out_specs=[row, vec],
        out_shape=[jax.ShapeDtypeStruct((M, D), F32), jax.ShapeDtypeStruct((1, D), F32)],
        compiler_params=_cparams(1), name=name)(a, b, x, g, dres, *deps)


def _mm_res_norm(a, b, res, g, *, K, tm, name, b_off=(0, 0), tgt=None):
    M = a.shape[0]
    head = tgt is not None

    def kern(a_ref, b_ref, res_ref, g_ref, *rest):
        xv = res_ref[...] + jnp.dot(a_ref[...], b_ref[...], preferred_element_type=F32)
        gv = g_ref[...]
        r = lax.rsqrt(jnp.mean(xv * xv, axis=-1, keepdims=True) + RMS_EPS)
        xhat = xv * r
        if not head:
            xo_ref, h_ref = rest
            xo_ref[...] = xv
            h_ref[...] = (xhat * gv).astype(BF16)
            return
        t_ref, dx_ref, dg_ref, ls_ref = rest
        i = pl.program_id(0)
        e = xhat * gv - t_ref[...]
        dy = e * (1.0 / D)
        gy = dy * gv
        dx_ref[...] = r * (gy - xhat * jnp.mean(gy * xhat, axis=-1, keepdims=True))
        dgp = jnp.sum(dy * xhat, axis=0, keepdims=True)
        lsp = jnp.sum(e * e, axis=0, keepdims=True)

        @pl.when(i == 0)
        def _():
            dg_ref[...] = dgp
            ls_ref[...] = lsp

        @pl.when(i > 0)
        def _():
            dg_ref[...] += dgp
            ls_ref[...] += lsp

    row = pl.BlockSpec((tm, D), lambda i: (i, 0))
    vec = pl.BlockSpec((1, D), lambda i: (0, 0))
    in_specs = [pl.BlockSpec((tm, K), lambda i: (i, 0)),
                pl.BlockSpec((K, D), lambda i: b_off, pipeline_mode=pl.Buffered(1)), row, vec]
    if head:
        return pl.pallas_call(
            kern, grid=(M // tm,), in_specs=in_specs + [row], out_specs=[row, vec, vec],
            out_shape=[jax.ShapeDtypeStruct((M, D), F32), jax.ShapeDtypeStruct((1, D), F32),
                       jax.ShapeDtypeStruct((1, D), F32)],
            compiler_params=_cparams(1), name=name)(a, b, res, g, tgt)
    return pl.pallas_call(
        kern, grid=(M // tm,), in_specs=in_specs, out_specs=[row, row],
        out_shape=[jax.ShapeDtypeStruct((M, D), F32), jax.ShapeDtypeStruct((M, D), BF16)],
        compiler_params=_cparams(1), name=name)(a, b, res, g)


def _rms_fwd(x, g, name, deps=()):
    S = x.shape[0]
    tr = 512

    def kern(x_ref, g_ref, *rest):
        h_ref = rest[-1]
        xv = x_ref[...]
        r = lax.rsqrt(jnp.mean(xv * xv, axis=-1, keepdims=True) + RMS_EPS)
        h_ref[...] = (xv * r * g_ref[...]).astype(BF16)

    return pl.pallas_call(
        kern, grid=(S // tr,),
        in_specs=[pl.BlockSpec((tr, D), lambda i: (i, 0)), pl.BlockSpec((1, D), lambda i: (0, 0))]
        + [pl.BlockSpec(memory_space=pl.ANY)] * len(deps),
        out_specs=pl.BlockSpec((tr, D), lambda i: (i, 0)),
        out_shape=jax.ShapeDtypeStruct((S, D), BF16), compiler_params=_cparams(1), name=name)(x, g, *deps)


def _chunks_put(scr, val):
    for c in range(scr.shape[0]):
        scr[c] = val[:, c * 128:(c + 1) * 128]


def _chunks_get(scr):
    return jnp.concatenate([scr[c] for c in range(scr.shape[0])], axis=1)


def _chunks_rows(scr, r, n, dil):
    return jnp.concatenate([scr.at[c][pl.ds(r, n, stride=dil), :] for c in range(scr.shape[0])], axis=1)


def _chunks_add_rows(scr, val, r, n, dil, accumulate):
    for c in range(scr.shape[0]):
        rows = pl.ds(r, n, stride=dil)
        piece = val[:, c * 128:(c + 1) * 128]
        tile = scr.at[c]
        tile[rows, :] = tile[rows, :] + piece if accumulate else piece


def _rms_fwd_folded(x, g, name, deps=()):
    S = x.shape[0]
    tr = 512
    dils = DILS[1:]

    def kern(x_ref, g_ref, *rest):
        outs, scr = rest[len(deps):-1], rest[-1]
        xv = x_ref[...]
        r = lax.rsqrt(jnp.mean(xv * xv, axis=-1, keepdims=True) + RMS_EPS)
        h = (xv * r * g_ref[...]).astype(BF16)
        outs[0][...] = h
        _chunks_put(scr, h.astype(F32))
        for o_ref, dil in zip(outs[1:], dils):
            for res in range(dil):
                o_ref[res] = _chunks_rows(scr, res, tr // dil, dil).astype(BF16)

    return pl.pallas_call(
        kern, grid=(S // tr,),
        in_specs=[pl.BlockSpec((tr, D), lambda i: (i, 0)), pl.BlockSpec((1, D), lambda i: (0, 0))]
        + [pl.BlockSpec(memory_space=pl.ANY)] * len(deps),
        out_specs=[pl.BlockSpec((tr, D), lambda i: (i, 0))]
        + [pl.BlockSpec((dil, tr // dil, D), lambda i: (0, i, 0)) for dil in dils],
        out_shape=[jax.ShapeDtypeStruct((S, D), BF16)]
        + [jax.ShapeDtypeStruct((dil, S // dil, D), BF16) for dil in dils],
        scratch_shapes=[pltpu.VMEM((D // 128, tr, 128), F32)],
        compiler_params=_cparams(1), name=name)(x, g, *deps)


def _rms_bwd(dh, x, g, dres, name, folded=()):
    S = x.shape[0]
    tr = 512
    nf = len(folded)

    def kern(dh_ref, *rest):
        f_refs = rest[:nf]
        x_ref, g_ref, dres_ref, dx_ref, dg_ref = rest[nf:nf + 5]
        i = pl.program_id(0)
        xv = x_ref[...]
        if nf:
            acc_ref = rest[nf + 5]
            _chunks_put(acc_ref, dh_ref[...].astype(F32))
            for f_ref in f_refs:
                dil = f_ref.shape[0]
                for res in range(dil):
                    _chunks_add_rows(acc_ref, f_ref[res], res, tr // dil, dil, True)
            dhv = _chunks_get(acc_ref)
        else:
            dhv = dh_ref[...].astype(F32)
        r = lax.rsqrt(jnp.mean(xv * xv, axis=-1, keepdims=True) + RMS_EPS)
        xhat = xv * r
        gy = dhv * g_ref[...]
        dx_ref[...] = dres_ref[...] + r * (gy - xhat * jnp.mean(gy * xhat, axis=-1, keepdims=True))
        part = jnp.sum(dhv * xhat, axis=0, keepdims=True)

        @pl.when(i == 0)
        def _():
            dg_ref[...] = part

        @pl.when(i > 0)
        def _():
            dg_ref[...] += part

    row = pl.BlockSpec((tr, D), lambda i: (i, 0))
    vec = pl.BlockSpec((1, D), lambda i: (0, 0))
    fspecs = [pl.BlockSpec((f.shape[0], tr // f.shape[0], D), lambda i: (0, i, 0)) for f in folded]
    return pl.pallas_call(
        kern, grid=(S // tr,), in_specs=[row] + fspecs + [row, vec, row], out_specs=[row, vec],
        out_shape=[jax.ShapeDtypeStruct((S, D), F32), jax.ShapeDtypeStruct((1, D), F32)],
        scratch_shapes=[pltpu.VMEM((D // 128, tr, 128), F32)] if nf else [],
        compiler_params=_cparams(1), name=name)(dh, *folded, x, g, dres)


def _ffn_up(h, G, name):
    S = h.shape[0]
    tm = 512
    nj = DFF // HCH

    def kern(h_ref, w_ref, gu_ref, act_ref):
        gu = lax.dot_general(h_ref[...], w_ref[...], (((1,), (1,)), ((), ())), preferred_element_type=F32)
        gu_ref[...] = gu.astype(BF16)
        gate = gu[:, :HCH]
        up = gu[:, HCH:]
        act_ref[...] = (gate * jax.nn.sigmoid(gate) * up).astype(BF16)

    return pl.pallas_call(
        kern, grid=(nj, S // tm),
        in_specs=[pl.BlockSpec((tm, D), lambda j, i: (i, 0)),
                  pl.BlockSpec((2 * HCH, D), lambda j, i: (j, 0))],
        out_specs=[pl.BlockSpec((tm, 2 * HCH), lambda j, i: (i, j)),
                   pl.BlockSpec((tm, HCH), lambda j, i: (i, j))],
        out_shape=[jax.ShapeDtypeStruct((S, 2 * DFF), BF16), jax.ShapeDtypeStruct((S, DFF), BF16)],
        compiler_params=_cparams(2), name=name)(h, G)


def _ffn_down_bwd(dx, G, gu, name):
    S = dx.shape[0]
    tm = 512
    nj = DFF // HCH

    def kern(dx_ref, w_ref, gu_ref, o_ref):
        dact = lax.dot_general(dx_ref[...].astype(BF16), w_ref[...], (((1,), (1,)), ((), ())),
                               preferred_element_type=F32)
        gate = gu_ref[:, :HCH].astype(F32)
        up = gu_ref[:, HCH:].astype(F32)
        sig = jax.nn.sigmoid(gate)
        silu = gate * sig
        o_ref[:, :HCH] = (dact * up * (sig * (1.0 + gate * (1.0 - sig)))).astype(BF16)
        o_ref[:, HCH:] = (dact * silu).astype(BF16)

    return pl.pallas_call(
        kern, grid=(nj, S // tm),
        in_specs=[pl.BlockSpec((tm, D), lambda j, i: (i, 0)),
                  pl.BlockSpec((HCH, D), lambda j, i: (j, 0)),
                  pl.BlockSpec((tm, 2 * HCH), lambda j, i: (i, j))],
        out_specs=pl.BlockSpec((tm, 2 * HCH), lambda j, i: (i, j)),
        out_shape=jax.ShapeDtypeStruct((S, 2 * DFF), BF16),
        compiler_params=_cparams(2), name=name)(dx, G, gu)


def _trail(u, *, backward, name):
    S = u.shape[0]

    def kern(u_ref, o_ref):
        g = pl.program_id(0)
        uv = u_ref[...].astype(F32)
        row = lax.broadcasted_iota(jnp.int32, uv.shape, 0)
        win = jnp.left_shift(jnp.int32(2), g)
        cnt = jnp.minimum(row + 1, win).astype(F32)
        s = uv / cnt if backward else uv
        levels = []
        for k in (1, 2, 4, 8):
            if backward:
                sh = jnp.where(row < S - k, pltpu.roll(s, S - k, 0), 0.0)
            else:
                sh = jnp.where(row >= k, pltpu.roll(s, k, 0), 0.0)
            s = s + sh
            levels.append(s)
        sel = jnp.where(g == 0, levels[0], jnp.where(g == 1, levels[1], jnp.where(g == 2, levels[2], levels[3])))
        if backward:
            o_ref[...] = (sel - uv).astype(BF16)
        else:
            o_ref[...] = (sel / cnt - uv).astype(BF16)

    blk = pl.BlockSpec((S, PGD), lambda g: (0, g))
    return pl.pallas_call(
        kern, grid=(POOL_G,), in_specs=[blk], out_specs=blk,
        out_shape=jax.ShapeDtypeStruct((S, D), BF16), compiler_params=_cparams(1), name=name)(u)


def _pool_out(yd, G, scale, xres):
    S = yd.shape[0]
    tm = 1024

    def kern(y_ref, w_ref, s_ref, x_ref, o_ref):
        z = jnp.dot(y_ref[...], w_ref[...], preferred_element_type=F32)
        o_ref[...] = x_ref[...] + z * s_ref[...]

    tile = pl.BlockSpec((tm, PGD), lambda i, g: (i, g))
    return pl.pallas_call(
        kern, grid=(S // tm, POOL_G),
        in_specs=[tile, pl.BlockSpec((PGD, PGD), lambda i, g: (0, g)),
                  pl.BlockSpec((1, PGD), lambda i, g: (0, g)), tile],
        out_specs=tile, out_shape=jax.ShapeDtypeStruct((S, D), F32),
        compiler_params=_cparams(2), name="pool_out")(yd, G, scale, xres)


def _pool_out_bwd(dz, yd, G, scale):
    S = yd.shape[0]
    tm = 1024
    ni = S // tm

    def kern(dz_ref, y_ref, w_ref, s_ref, dy_ref, ds_ref, dw_ref, acc_ref):
        i = pl.program_id(1)
        dzv = dz_ref[...]
        yv = y_ref[...]
        wv = w_ref[...]
        zraw = jnp.dot(yv, wv, preferred_element_type=F32)
        dsp = jnp.sum(dzv * zraw, axis=0, keepdims=True)
        dzr = (dzv * s_ref[...]).astype(BF16)
        dy_ref[...] = lax.dot_general(dzr, wv, (((1,), (1,)), ((), ())), preferred_element_type=F32)
        dwp = lax.dot_general(yv, dzr, (((0,), (0,)), ((), ())), preferred_element_type=F32)

        @pl.when(i == 0)
        def _():
            ds_ref[...] = dsp
            acc_ref[...] = dwp

        @pl.when(i > 0)
        def _():
            ds_ref[...] += dsp
            acc_ref[...] += dwp

        @pl.when(i == ni - 1)
        def _():
            dw_ref[...] = acc_ref[...].astype(BF16)

    tile = pl.BlockSpec((tm, PGD), lambda g, i: (i, g))
    return pl.pallas_call(
        kern, grid=(POOL_G, ni),
        in_specs=[tile, tile, pl.BlockSpec((PGD, PGD), lambda g, i: (0, g)),
                  pl.BlockSpec((1, PGD), lambda g, i: (0, g))],
        out_specs=[tile, pl.BlockSpec((1, PGD), lambda g, i: (0, g)),
                   pl.BlockSpec((PGD, PGD), lambda g, i: (0, g))],
        out_shape=[jax.ShapeDtypeStruct((S, D), F32), jax.ShapeDtypeStruct((1, D), F32),
                   jax.ShapeDtypeStruct((PGD, D), BF16)],
        scratch_shapes=[pltpu.VMEM((PGD, PGD), F32)],
        compiler_params=_cparams(2), name="pool_out_bwd")(dz, yd, G, scale)


def _bias_table():
    qi = jnp.arange(QB)[:, None]
    ki = jnp.arange(2 * QB)[None, :]
    delta = QB + qi - ki
    inband = (delta >= 0) & (delta <= QB)
    n = NGROUPS * HEADS
    slopes = jnp.exp2(-8.0 * jnp.arange(1, n + 1, dtype=F32) / n).reshape(NGROUPS, HEADS)
    dil = jnp.asarray(DILS, F32)
    bias = -slopes[:, :, None, None] * (delta.astype(F32)[None, None] * dil[:, None, None, None])
    return jnp.where(inband[None, None], bias, NEG)


def _attn_fwd(qkv_f, bias, nb, name):
    S = qkv_f.shape[0]
    nblk = S // QB
    scale = HD ** -0.5

    def kern(q_ref, kc_ref, kp_ref, vc_ref, vp_ref, b_ref, o_ref, l_ref, s_scr, p_scr, r_scr):
        b = pl.program_id(0)
        has_prev = jnp.bitwise_and(b, nb - 1) != 0
        col = lax.broadcasted_iota(jnp.int32, (QB, 2 * QB), 1)
        dead = jnp.logical_and(col < QB, jnp.logical_not(has_prev))
        lane = lax.broadcasted_iota(jnp.int32, (QB, HD), 1)
        lse_all = jnp.zeros((QB, HD), F32)
        for h in range(HEADS):
            sl = slice(h * HD, (h + 1) * HD)
            kk = jnp.concatenate([kp_ref[:, sl], kc_ref[:, sl]], axis=0)
            s_scr[h] = lax.dot_general(q_ref[:, sl], kk, (((1,), (1,)), ((), ())), preferred_element_type=F32)
        for h in range(HEADS):
            s = s_scr[h] * scale + b_ref[h]
            s = jnp.where(dead, NEG, s)
            m = jnp.max(s, axis=-1, keepdims=True)
            p = jnp.exp(s - m)
            den = jnp.sum(p, axis=-1, keepdims=True)
            p_scr[h] = p.astype(BF16)
            r_scr[h] = jnp.broadcast_to(1.0 / den, (QB, HD))
            lse_all = jnp.where(lane == h, m + jnp.log(den), lse_all)
        for h in range(HEADS):
            sl = slice(h * HD, (h + 1) * HD)
            vv = jnp.concatenate([vp_ref[:, sl], vc_ref[:, sl]], axis=0)
            o = jnp.dot(p_scr[h], vv, preferred_element_type=F32) * r_scr[h]
            o_ref[:, sl] = o.astype(BF16)
        l_ref[...] = lse_all

    def blk(colblk, prev):
        if prev:
            return pl.BlockSpec((QB, D), lambda b: (jnp.maximum(b - 1, 0), colblk))
        return pl.BlockSpec((QB, D), lambda b: (b, colblk))

    return pl.pallas_call(
        kern, grid=(nblk,),
        in_specs=[blk(0, False), blk(1, False), blk(1, True), blk(2, False), blk(2, True),
                  pl.BlockSpec((HEADS, QB, 2 * QB), lambda b: (0, 0, 0))],
        out_specs=[pl.BlockSpec((QB, D), lambda b: (b, 0)), pl.BlockSpec((QB, HD), lambda b: (b, 0))],
        out_shape=[jax.ShapeDtypeStruct((S, D), BF16), jax.ShapeDtypeStruct((S, HD), F32)],
        scratch_shapes=[pltpu.VMEM((HEADS, QB, 2 * QB), F32), pltpu.VMEM((HEADS, QB, 2 * QB), BF16),
                        pltpu.VMEM((HEADS, QB, HD), F32)],
        compiler_params=_cparams(1), name=name)(qkv_f, qkv_f, qkv_f, qkv_f, qkv_f, bias)


def _natural(ref, scr, tm):
    dil = ref.shape[0]
    for res in range(dil):
        _chunks_add_rows(scr, ref[res].astype(F32), res, tm // dil, dil, False)
    return _chunks_get(scr)


def _attn_merge(os, lses):
    S = os[0].shape[0]
    tm = 512

    def kern(o0, o1, o2, l0, l1, l2, om_ref, lm_ref, ls1, ls2, os1, os2):
        la = l0[...]
        lb = _natural(l1, ls1, tm)
        lc = _natural(l2, ls2, tm)
        m = jnp.maximum(jnp.maximum(la, lb), lc)
        e0, e1, e2 = jnp.exp(la - m), jnp.exp(lb - m), jnp.exp(lc - m)
        tot = e0 + e1 + e2
        lm_ref[...] = m + jnp.log(tot)
        w0, w1, w2 = e0 / tot, e1 / tot, e2 / tot
        for res in range(o1.shape[0]):
            _chunks_add_rows(os1, o1[res].astype(F32), res, tm // o1.shape[0], o1.shape[0], False)
        for res in range(o2.shape[0]):
            _chunks_add_rows(os2, o2[res].astype(F32), res, tm // o2.shape[0], o2.shape[0], False)
        for h in range(HEADS):
            sl = slice(h * HD, (h + 1) * HD)
            acc = w0[:, h:h + 1] * o0[:, sl].astype(F32) + w1[:, h:h + 1] * os1[h] + w2[:, h:h + 1] * os2[h]
            om_ref[:, sl] = acc.astype(BF16)

    def spec(a, c):
        if a.ndim == 2:
            return pl.BlockSpec((tm, c), lambda i: (i, 0))
        return pl.BlockSpec((a.shape[0], tm // a.shape[0], c), lambda i: (0, i, 0))

    return pl.pallas_call(
        kern, grid=(S // tm,),
        in_specs=[spec(a, D) for a in os] + [spec(a, HD) for a in lses],
        out_specs=[pl.BlockSpec((tm, D), lambda i: (i, 0)), pl.BlockSpec((tm, HD), lambda i: (i, 0))],
        out_shape=[jax.ShapeDtypeStruct((S, D), BF16), jax.ShapeDtypeStruct((S, HD), F32)],
        scratch_shapes=[pltpu.VMEM((1, tm, HD), F32), pltpu.VMEM((1, tm, HD), F32),
                        pltpu.VMEM((HEADS, tm, HD), F32), pltpu.VMEM((HEADS, tm, HD), F32)],
        compiler_params=_cparams(1), name="attn_merge")(*os, *lses)


def _attn_bwd_prep(do, o, lse):
    S = o.shape[0]
    tm = 512
    dils = DILS[1:]

    def kern(do_ref, o_ref, l_ref, *rest):
        do_outs, l_outs, d_outs = rest[0:3], rest[3:5], rest[5:8]
        do_scr, l_scr, d_scr = rest[8:11]
        lane = lax.broadcasted_iota(jnp.int32, (tm, HD), 1)
        acc = jnp.zeros((tm, HD), F32)
        for h in range(HEADS):
            sl = slice(h * HD, (h + 1) * HD)
            prod = do_ref[:, sl] * o_ref[:, sl].astype(F32)
            acc = jnp.where(lane == h, jnp.sum(prod, axis=-1, keepdims=True), acc)
        d_scr[0] = acc
        l_scr[0] = l_ref[...]
        _chunks_put(do_scr, do_ref[...])
        do_outs[0][...] = do_ref[...].astype(BF16)
        d_outs[0][...] = acc
        for j, dil in enumerate(dils):
            for res in range(dil):
                n = tm // dil
                do_outs[1 + j][res] = _chunks_rows(do_scr, res, n, dil).astype(BF16)
                l_outs[j][res] = _chunks_rows(l_scr, res, n, dil)
                d_outs[1 + j][res] = _chunks_rows(d_scr, res, n, dil)

    def nat(c):
        return pl.BlockSpec((tm, c), lambda i: (i, 0))

    def fol(dil, c):
        return pl.BlockSpec((dil, tm // dil, c), lambda i: (0, i, 0))

    def shapes(c, dt, with_natural):
        first = [jax.ShapeDtypeStruct((S, c), dt)] if with_natural else []
        return first + [jax.ShapeDtypeStruct((dil, S // dil, c), dt) for dil in dils]

    outs = pl.pallas_call(
        kern, grid=(S // tm,), in_specs=[nat(D), nat(D), nat(HD)],
        out_specs=[nat(D)] + [fol(dil, D) for dil in dils] + [fol(dil, HD) for dil in dils]
        + [nat(HD)] + [fol(dil, HD) for dil in dils],
        out_shape=shapes(D, BF16, True) + shapes(HD, F32, False) + shapes(HD, F32, True),
        scratch_shapes=[pltpu.VMEM((HEADS, tm, HD), F32), pltpu.VMEM((1, tm, HD), F32), pltpu.VMEM((1, tm, HD), F32)],
        compiler_params=_cparams(1), name="attn_bwd_prep")(do, o, lse)
    return outs[0:3], [lse] + list(outs[3:5]), outs[5:8]


def _attn_bwd(qkv_f, do_f, lse_f, delta_f, bias, nb, name):
    S = qkv_f.shape[0]
    nblk = S // QB
    scale = HD ** -0.5

    def kern(q_ref, kc_ref, kp_ref, vc_ref, vp_ref, do_ref, l_ref, d_ref, b_ref, out_ref, dq_c, dk_c, dv_c,
             s_scr, dp_scr, ds_scr, p_scr):
        b = pl.program_id(0)

        @pl.when(b == 0)
        def _():
            dq_c[...] = jnp.zeros_like(dq_c)
            dk_c[...] = jnp.zeros_like(dk_c)
            dv_c[...] = jnp.zeros_like(dv_c)

        @pl.when(b == nblk)
        def _():
            out_ref[:, 0:D] = dq_c[...].astype(BF16)
            out_ref[:, D:2 * D] = dk_c[...].astype(BF16)
            out_ref[:, 2 * D:3 * D] = dv_c[...].astype(BF16)

        @pl.when(b < nblk)
        def _():
            has_prev = jnp.bitwise_and(b, nb - 1) != 0
            col = lax.broadcasted_iota(jnp.int32, (QB, 2 * QB), 1)
            dead = jnp.logical_and(col < QB, jnp.logical_not(has_prev))
            out_ref[:, 0:D] = dq_c[...].astype(BF16)
            lv = l_ref[...]
            dv_ = d_ref[...]
            for h in range(HEADS):
                sl = slice(h * HD, (h + 1) * HD)
                kk = jnp.concatenate([kp_ref[:, sl], kc_ref[:, sl]], axis=0)
                vv = jnp.concatenate([vp_ref[:, sl], vc_ref[:, sl]], axis=0)
                s_scr[h] = lax.dot_general(q_ref[:, sl], kk, (((1,), (1,)), ((), ())), preferred_element_type=F32)
                dp_scr[h] = lax.dot_general(do_ref[:, sl], vv, (((1,), (1,)), ((), ())),
                                            preferred_element_type=F32)
            for h in range(HEADS):
                s = s_scr[h] * scale + b_ref[h]
                s = jnp.where(dead, NEG, s)
                p = jnp.exp(s - lv[:, h:h + 1])
                ds_scr[h] = (p * (dp_scr[h] - dv_[:, h:h + 1]) * scale).astype(BF16)
                p_scr[h] = p.astype(BF16)
            for h in range(HEADS):
                sl = slice(h * HD, (h + 1) * HD)
                kk = jnp.concatenate([kp_ref[:, sl], kc_ref[:, sl]], axis=0)
                ds = ds_scr[h]
                dq_c[:, sl] = jnp.dot(ds, kk, preferred_element_type=F32)
                dkk = lax.dot_general(ds, q_ref[:, sl], (((0,), (0,)), ((), ())), preferred_element_type=F32)
                dvv = lax.dot_general(p_scr[h], do_ref[:, sl], (((0,), (0,)), ((), ())),
                                      preferred_element_type=F32)
                out_ref[:, D + h * HD:D + (h + 1) * HD] = (dk_c[:, sl] + dkk[:QB]).astype(BF16)
                out_ref[:, 2 * D + h * HD:2 * D + (h + 1) * HD] = (dv_c[:, sl] + dvv[:QB]).astype(BF16)
                dk_c[:, sl] = dkk[QB:]
                dv_c[:, sl] = dvv[QB:]

    last = nblk - 1

    def blk(colblk, prev):
        if prev:
            return pl.BlockSpec((QB, D), lambda b: (jnp.maximum(jnp.minimum(b, last) - 1, 0), colblk))
        return pl.BlockSpec((QB, D), lambda b: (jnp.minimum(b, last), colblk))

    stat = pl.BlockSpec((QB, HD), lambda b: (jnp.minimum(b, last), 0))
    return pl.pallas_call(
        kern, grid=(nblk + 1,),
        in_specs=[blk(0, False), blk(1, False), blk(1, True), blk(2, False), blk(2, True),
                  pl.BlockSpec((QB, D), lambda b: (jnp.minimum(b, last), 0)), stat, stat,
                  pl.BlockSpec((HEADS, QB, 2 * QB), lambda b: (0, 0, 0))],
        out_specs=pl.BlockSpec((QB, 3 * D), lambda b: (jnp.maximum(b - 1, 0), 0)),
        out_shape=jax.ShapeDtypeStruct((S, 3 * D), BF16),
        scratch_shapes=[pltpu.VMEM((QB, D), F32), pltpu.VMEM((QB, D), F32), pltpu.VMEM((QB, D), F32),
                        pltpu.VMEM((HEADS, QB, 2 * QB), F32), pltpu.VMEM((HEADS, QB, 2 * QB), F32),
                        pltpu.VMEM((HEADS, QB, 2 * QB), BF16), pltpu.VMEM((HEADS, QB, 2 * QB), BF16)],
        compiler_params=_cparams(1), name=name)(qkv_f, qkv_f, qkv_f, qkv_f, qkv_f, do_f, lse_f, delta_f, bias)


def _local_step(x, tgt, comm, attn_norm, ffn_norm, final_norm, pool_norm, pool_scale):
    S = x.shape[0]
    bias = _bias_table()
    g_attn = attn_norm.reshape(1, D)
    g_f0 = ffn_norm[0:1]
    g_f1 = ffn_norm[1:2]
    g_fin = final_norm.reshape(1, D)
    W = {}

    def ffn_fwd(xin, h, l, next_gain, target=None):
        gu, act = _ffn_up(h, W[f"gu{l}"], f"ffn_up{l}")
        return gu, act, _mm_res_norm(act, W[f"d{l}"], xin, next_gain, K=DFF, tm=512, tgt=target,
                                     name=f"ffn_down{l}")

    def ffn_bwd(dxo, xin, gain, h, gu, act, l, rs_group):
        dgu = _ffn_down_bwd(dxo, W[f"d{l}"], gu, f"ffn_down_bwd{l}")
        gw_d = _mm(act, dxo, mode="tn", M=DFF, N=D, K=S, tm=HCH, tn=D, tk=2048, out_dtype=BF16, name=f"gw_d{l}")
        gw_gu = _mm(dgu, h, mode="tn", M=2 * DFF, N=D, K=S, tm=HCH, tn=D, tk=2048, out_dtype=BF16, name=f"gw_gu{l}")
        token = comm.send_grads(rs_group, {f"d{l}": gw_d, f"gu{l}": gw_gu})
        return _mm_rms_bwd(dgu, W[f"gu{l}"], xin, gain, dxo, mode="nn", M=S, K=2 * DFF, tm=512, deps=(token,),
                           name=f"ffn_up_bwd{l}")

    nbs = [S // QB // dil for dil in DILS]
    hf = _rms_fwd_folded(x, g_attn, "rms_attn", deps=comm.ag_tokens)
    hf = [h.reshape(S, D) for h in hf]
    W.update(comm.weights(0, hf[0]))
    qkv_f, o_f, lse_f = [], [], []
    for g, dil in enumerate(DILS):
        qkv_f.append(_mm(hf[g], W["qkv"], mode="nt", M=S, N=3 * D, K=D, tm=2048, tn=1024, tk=D, out_dtype=BF16,
                         b_off=(3 * g, 0), name=f"qkv_proj{g}"))
        og, lg = _attn_fwd(qkv_f[g], bias[g], nbs[g], f"attn_fwd{g}")
        o_f.append(og if dil == 1 else og.reshape(dil, S // dil, D))
        lse_f.append(lg if dil == 1 else lg.reshape(dil, S // dil, HD))
    o, lse = _attn_merge(o_f, lse_f)
    W.update(comm.weights(1, o))
    x1, h1 = _mm_res_norm(o, W["wo"], x, g_f0, K=D, tm=1024, name="attn_out")
    gu0, act0, (x2, h2) = ffn_fwd(x1, h1, 0, pool_norm)

    W.update(comm.weights(2, x2))
    u = _mm(h2, W["wpi"], mode="nn", M=S, N=D, K=D, tm=1024, tn=D, tk=D, out_dtype=F32, name="pool_in")
    yd = _trail(u, backward=False, name="trail_fwd")
    x3 = _pool_out(yd, W["pg"], pool_scale, x2)
    h3 = _rms_fwd(x3, g_f1, "rms_ffn1")
    gu1, act1, (dx4, d_fin, lossvec) = ffn_fwd(x3, h3, 1, g_fin, target=tgt)

    dx3, d_f1 = ffn_bwd(dx4, x3, g_f1, h3, gu1, act1, 1, 0)
    dyd, d_scale, gw_pg = _pool_out_bwd(dx3, yd, W["pg"], pool_scale)
    du = _trail(dyd, backward=True, name="trail_bwd")
    gw_pi = _mm(h2, du, mode="tn", M=D, N=D, K=S, tm=D, tn=D, tk=2048, out_dtype=BF16, name="gw_pi")
    token = comm.send_grads(1, {"pg": gw_pg, "wpi": gw_pi})
    dx2, d_pool = _mm_rms_bwd(du, W["wpi"], x2, pool_norm, dx3, mode="nt", M=S, K=D, tm=1024, deps=(token,),
                              name="pool_in_bwd")
    dx1, d_f0 = ffn_bwd(dx2, x1, g_f0, h1, gu0, act0, 0, 2)

    gw_o = _mm(o, dx1, mode="tn", M=D, N=D, K=S, tm=D, tn=D, tk=2048, out_dtype=BF16, name="gw_o")
    do = _mm(dx1, W["wo"], mode="nt", M=S, N=D, K=D, tm=1024, tn=D, tk=D, out_dtype=F32, name="attn_out_bwd")
    do_f, lse_ff, delta_f = _attn_bwd_prep(do, o, lse)
    dqkv_f, gw_qkv = [], None
    for g in range(NGROUPS):
        dqkv_f.append(_attn_bwd(qkv_f[g], do_f[g].reshape(S, D), lse_ff[g].reshape(S, HD),
                                delta_f[g].reshape(S, HD), bias[g], nbs[g], f"attn_bwd{g}"))
        gw_qkv = _mm(dqkv_f[g], hf[g], mode="tn", M=3 * D, N=D, K=S, tm=1024, tn=D, tk=2048, out_dtype=BF16,
                     out_rows=NGROUPS * 3 * D, out_off=3 * g, out_prev=gw_qkv, name=f"gw_qkv{g}")
    token = comm.send_grads(3, {"wo": gw_o, "qkv": gw_qkv})
    dh0_f = [_mm(dqkv_f[g], W["qkv"], mode="nn", M=S, N=D, K=3 * D, tm=1024, tn=D, tk=3 * D, out_dtype=F32,
                 b_off=(g, 0), deps=(token,), name=f"qkv_proj_bwd{g}") for g in range(NGROUPS)]
    folded = [dh0_f[g].reshape(dil, S // dil, D) for g, dil in enumerate(DILS) if dil > 1]
    grad_x, d_attn = _rms_bwd(dh0_f[0], x, g_attn, dx1, "rms_attn_bwd", folded=folded)

    vec = jnp.concatenate([d_attn, d_f0, d_f1, d_fin, d_pool, d_scale, lossvec, jnp.zeros((1, D), F32)], axis=0)
    return grad_x, vec


def _mesh_pos():
    x, y, c = lax.axis_index("x"), lax.axis_index("y"), lax.axis_index("c")
    return x, y, c, 4 * x + 2 * y + c


def _peer(x, y, c, k):
    kx, ky, kc = (k >> 2) & 1, (k >> 1) & 1, k & 1
    px = 1 - x if kx else x
    py = 1 - y if ky else y
    pc = 1 - c if kc else c
    return (px, py, pc), 4 * px + 2 * py + pc


ANY = pl.BlockSpec(memory_space=pl.ANY)


HBM = pl.BlockSpec(memory_space=pltpu.HBM)
SEMS = pl.BlockSpec(memory_space=pltpu.SEMAPHORE)
EFFECT = pltpu.SideEffectType.DATAFLOW_SIDE_EFFECTING
NPEER = NDEV - 1

AG_GROUPS = (("qkv",), ("wo", "gu0", "d0"), ("wpi", "pg", "gu1", "d1"))
AG_ORDER = tuple(n for grp in AG_GROUPS for n in grp)
RS_GROUPS = (("d1", "gu1"), ("pg", "wpi"), ("d0", "gu0"), ("wo", "qkv"))


def _hbm(a):
    return pltpu.with_memory_space_constraint(a, pltpu.HBM)


def _remote(src, dst, send, recv, peer):
    return pltpu.make_async_remote_copy(src_ref=src, dst_ref=dst, send_sem=send, recv_sem=recv, device_id=peer,
                                        device_id_type=pl.DeviceIdType.MESH)


def _bcast_all(v, name, deps=()):
    W = v.shape[1]
    nd = len(deps)

    def kern(v_ref, *rest):
        o_ref, send, recv, lsem = rest[nd:]
        x, y, c, me = _mesh_pos()
        own = pltpu.make_async_copy(v_ref, o_ref.at[me], lsem)
        own.start()
        cps = [_remote(v_ref, o_ref.at[me], send.at[k - 1], recv.at[k - 1], _peer(x, y, c, k)[0])
               for k in range(1, NDEV)]
        for cp in cps:
            cp.start()
        for cp in cps:
            cp.wait_recv()
            cp.wait_send()
        own.wait()

    return pl.pallas_call(
        kern, in_specs=[ANY] * (1 + nd), out_specs=ANY, out_shape=jax.ShapeDtypeStruct((NDEV, 8, W), F32),
        scratch_shapes=[pltpu.SemaphoreType.DMA((NPEER,)), pltpu.SemaphoreType.DMA((NPEER,)),
                        pltpu.SemaphoreType.DMA(())],
        name=name)(v, *deps)


ALL_KS = tuple(range(1, NDEV))
AG_KS1 = (1, 2, 4, 6)
AG_KS2 = (2, 4, 6)


def _split_start(srcs, src_of, lands, copy_refs, name, deps=(), ks=ALL_KS, to=None):
    ns, n, nd, nk = len(srcs), len(lands), len(deps), len(ks)

    def body(*refs):
        ins, land = refs[:ns], refs[ns:ns + n]
        send, recv = refs[ns + n + nd], refs[ns + n + nd + 1]
        token = refs[-1]
        x, y, c, me = _mesh_pos()
        for j in range(n):
            for i, k in enumerate(ks):
                _, pid = _peer(x, y, c, k)
                dest, _ = _peer(x, y, c, k if to is None else to)
                src, dst = copy_refs(j, (land[j] if src_of[j] is None else ins[src_of[j]]), land[j], me, pid)
                _remote(src, dst, send.at[j * nk + i], recv.at[j * nk + i], dest).start()
        token[...] = jnp.zeros_like(token)

    outs = pl.pallas_call(
        body, name=name,
        out_shape=(pltpu.SemaphoreType.DMA((n * nk,)), pltpu.SemaphoreType.DMA((n * nk,)))
        + tuple(pltpu.HBM(a.shape, a.dtype) for a in srcs) + tuple(pltpu.HBM(a.shape, a.dtype) for a in lands)
        + (jax.ShapeDtypeStruct((8, 128), F32),),
        in_specs=(HBM,) * (ns + n) + (ANY,) * nd,
        out_specs=(SEMS, SEMS) + (HBM,) * (ns + n) + (pl.BlockSpec(memory_space=pltpu.VMEM),),
        input_output_aliases={i: 2 + i for i in range(ns + n)},
        compiler_params=pltpu.CompilerParams(has_side_effects=EFFECT),
    )(*[_hbm(a) for a in srcs], *[_hbm(a) for a in lands], *deps)
    return outs[0], outs[1], list(outs[2:2 + ns]), list(outs[2 + ns:2 + ns + n]), outs[-1]


def _split_wait(srcs, src_of, lands, send, recv, sem_rows, wait_refs, after, name, ks=ALL_KS):
    ns, n, nk = len(srcs), len(lands), len(ks)

    def body(*refs):
        ins, land = refs[:ns], refs[ns:ns + n]
        send_ref, recv_ref = refs[ns + n], refs[ns + n + 1]
        x, y, c, me = _mesh_pos()
        for j in range(n):
            for i, k in enumerate(ks):
                peer, _ = _peer(x, y, c, k)
                src, dst = wait_refs(j, (land[j] if src_of[j] is None else ins[src_of[j]]), land[j])
                sem = sem_rows[j] * nk + i
                cp = _remote(src, dst, send_ref.at[sem], recv_ref.at[sem], peer)
                cp.wait_send()
                cp.wait_recv()

    outs = pl.pallas_call(
        body, name=name,
        out_shape=tuple(pltpu.HBM(a.shape, a.dtype) for a in srcs) + tuple(pltpu.HBM(a.shape, a.dtype) for a in lands),
        in_specs=(HBM,) * (ns + n) + (SEMS, SEMS, ANY),
        out_specs=(HBM,) * (ns + n),
        input_output_aliases={i: i for i in range(ns + n)},
        compiler_params=pltpu.CompilerParams(has_side_effects=EFFECT),
    )(*srcs, *lands, send, recv, after)
    return list(outs[:ns]), list(outs[ns:])


class _Comm:
    def __init__(self, shards, me, deps=()):
        self.me = me
        self.ag_land, self.ag_sems, self.ag_tokens = {}, {}, ()
        self.rs = []
        for part, names in enumerate((AG_GROUPS[0], AG_ORDER[len(AG_GROUPS[0]):])):
            rows = [SEC_ROWS[n] for n in names]
            lands = [lax.dynamic_update_slice(lax.empty((NDEV * r, D), BF16), shards[n], (_shard_pos(n, me), 0))
                     for n, r in zip(names, rows)]

            def copy_refs(j, src, land, me, pid, names=names, rows=rows):
                own = land.at[pl.ds(pl.multiple_of(_shard_pos(names[j], me), 16), rows[j])]
                return own, own

            send, recv, _, lands, token = _split_start([], [None] * len(names), lands, copy_refs, f"ag_start{part}",
                                                       deps=deps, ks=AG_KS1)
            deps = (token,)
            self.ag_tokens += (token,)
            for j, n in enumerate(names):
                self.ag_land[n] = lands[j]
                self.ag_sems[n] = (send, recv, j)

    def weights(self, group, after):
        names = AG_GROUPS[group]
        send, recv = self.ag_sems[names[0]][:2]
        idx = [self.ag_sems[n][2] for n in names]
        rows = [SEC_ROWS[n] for n in names]
        none = [None] * len(names)

        def wait_refs(j, src, land):
            return land.at[pl.ds(0, rows[j])], land.at[pl.ds(0, rows[j])]

        _, lands = _split_wait([], none, [self.ag_land[n] for n in names], send, recv, idx,
                               wait_refs, after, f"ag_wait{group}", ks=AG_KS1)

        def copy_refs(j, src, land, me, pid):
            theirs = land.at[pl.ds(pl.multiple_of(_shard_pos(names[j], pid), 16), rows[j])]
            return theirs, theirs

        send, recv, _, lands, token = _split_start([], none, lands, copy_refs, f"ag_pass{group}", ks=AG_KS2, to=1)
        _, lands = _split_wait([], none, lands, send, recv, list(range(len(names))), wait_refs, token,
                               f"ag_pass_wait{group}", ks=AG_KS2)
        return dict(zip(names, lands))

    def send_grads(self, group, gws):
        names = RS_GROUPS[group]
        rows = [SEC_ROWS[n] for n in names]
        grads = [gws[n] for n in names]
        me = self.me
        lands = [lax.dynamic_update_slice(
            lax.empty((NDEV, r, D), BF16),
            lax.dynamic_slice(g, (_shard_pos(n, me), 0), (r, D))[None], (me, 0, 0))
            for n, r, g in zip(names, rows, grads)]

        def copy_refs(j, src, land, me, pid):
            return src.at[pl.ds(pl.multiple_of(_shard_pos(names[j], pid), 16), rows[j])], land.at[me]

        send, recv, srcs, lands, token = _split_start(grads, list(range(len(names))), lands, copy_refs,
                                                      f"rs_start{group}")
        self.rs.append((names, rows, send, recv, srcs, lands))
        return token

    def received(self, group, after):
        names, rows, send, recv, srcs, lands = self.rs[group]

        def wait_refs(j, src, land):
            return src.at[pl.ds(0, rows[j])], land.at[0]

        _, lands = _split_wait(srcs, list(range(len(names))), lands, send, recv, list(range(len(names))), wait_refs,
                               after, f"rs_wait{group}")
        return dict(zip(names, lands))


def _sum_contributions(r_ref):
    g = r_ref[0].astype(F32)
    for dev in range(1, NDEV):
        g = g + r_ref[dev].astype(F32)
    return g


def _adam_math(g, w, m, v):
    c1 = 1.0 / (1.0 - ADAM_B1 ** ADAM_STEP)
    c2 = 1.0 / (1.0 - ADAM_B2 ** ADAM_STEP)
    mn = ADAM_B1 * m + (1.0 - ADAM_B1) * g
    vn = ADAM_B2 * v + (1.0 - ADAM_B2) * (g * g)
    return -ADAM_LR * ((mn * c1) / (jnp.sqrt(vn * c2) + ADAM_EPS) + ADAM_WD * w), mn, vn


def _adamw(R, w, m, v, *, tr, name):
    rows, C = w.shape

    def kern(r_ref, w_ref, m_ref, v_ref, g_out, d_out, m_out, v_out):
        g = _sum_contributions(r_ref)
        g_out[...] = g
        d_out[...], m_out[...], v_out[...] = _adam_math(g, w_ref[...], m_ref[...], v_ref[...])

    tile = pl.BlockSpec((tr, C), lambda i: (i, 0))
    shp = jax.ShapeDtypeStruct((rows, C), F32)
    return pl.pallas_call(
        kern, grid=(rows // tr,),
        in_specs=[pl.BlockSpec((NDEV, tr, C), lambda i: (0, i, 0)), tile, tile, tile],
        out_specs=[tile] * 4, out_shape=[shp] * 4, compiler_params=_cparams(1), name=name)(R, w, m, v)


def _adamw_pool_group(R, w, m, v):
    rows = SEC_ROWS["pg"]

    def kern(r_ref, w_ref, m_ref, v_ref, g_out, d_out, m_out, v_out):
        g = _sum_contributions(r_ref)
        g_out[0] = g
        d_out[0], m_out[0], v_out[0] = _adam_math(g, w_ref[0], m_ref[0], v_ref[0])

    blk = pl.BlockSpec((1, rows, PGD), lambda i: (i, 0, 0))
    shp = jax.ShapeDtypeStruct((POOL_G, rows, PGD), F32)
    return pl.pallas_call(
        kern, grid=(POOL_G,),
        in_specs=[pl.BlockSpec((NDEV, rows, PGD), lambda i: (0, 0, i)), blk, blk, blk],
        out_specs=[blk] * 4, out_shape=[shp] * 4, compiler_params=_cparams(1), name="adamw_pg")(R, w, m, v)


def _grad_sum_t(R, name):
    rows = R.shape[1]
    tr = 128 if rows % 128 == 0 else rows

    def kern(r_ref, o_ref):
        o_ref[...] = _sum_contributions(r_ref).T

    return pl.pallas_call(
        kern, grid=(rows // tr,), in_specs=[pl.BlockSpec((NDEV, tr, D), lambda i: (0, i, 0))],
        out_specs=pl.BlockSpec((D, tr), lambda i: (0, i)), out_shape=jax.ShapeDtypeStruct((D, rows), F32),
        compiler_params=_cparams(1), name=name)(R)


def _adam_plain(g, w, m, v, *, tr, name):
    rows, C = w.shape

    def kern(g_ref, w_ref, m_ref, v_ref, d_out, m_out, v_out):
        d_out[...], m_out[...], v_out[...] = _adam_math(g_ref[...], w_ref[...], m_ref[...], v_ref[...])

    tile = pl.BlockSpec((tr, C), lambda i: (i, 0))
    shp = jax.ShapeDtypeStruct((rows, C), F32)
    return pl.pallas_call(
        kern, grid=(rows // tr,), in_specs=[tile] * 4, out_specs=[tile] * 3, out_shape=[shp] * 3,
        compiler_params=_cparams(1), name=name)(g, w, m, v)


def _pack_sections(w_qkv, w_attn_out, w_pool_in, w_pool_group, w_ffn_gate_up, w_ffn_down):
    pg = w_pool_group[0].transpose(1, 0, 2).reshape(SEC_ROWS["pg"], D)
    return {"qkv": w_qkv[0].T, "wo": w_attn_out[0], "wpi": w_pool_in[0], "gu0": w_ffn_gate_up[0].T,
            "gu1": w_ffn_gate_up[1].T, "d0": w_ffn_down[0], "d1": w_ffn_down[1], "pg": pg}


def _vec_pack(attn_norm, ffn_norm, final_norm, pool_norm_sh, pool_scale_sh, me):
    def place(sh):
        return lax.dynamic_update_slice(jnp.zeros((1, D), F32), sh, (0, me * 128))
    return jnp.concatenate([attn_norm, ffn_norm, final_norm.reshape(1, D), place(pool_norm_sh),
                            place(pool_scale_sh), jnp.zeros((2, D), F32)], axis=0)


def _vec_unpack(p, me):
    def take(r):
        return lax.dynamic_slice(p[r:r + 1], (0, me * 128), (1, 128))
    return p[0:1], p[1:3], p[3], take(4), take(5)


def kernel(x, attn_norm, w_qkv, w_attn_out, pool_norm, w_pool_in, w_pool_group, pool_scale, ffn_norm, w_ffn_gate_up, w_ffn_down, final_norm, loss_target, m_attn_norm, m_w_qkv, m_w_attn_out, m_pool_norm, m_w_pool_in, m_w_pool_group, m_pool_scale, m_ffn_norm, m_w_ffn_gate_up, m_w_ffn_down, m_final_norm, v_attn_norm, v_w_qkv, v_w_attn_out, v_pool_norm, v_w_pool_in, v_w_pool_group, v_pool_scale, v_ffn_norm, v_w_ffn_gate_up, v_w_ffn_down, v_final_norm):
    me = 4 * lax.axis_index("x") + 2 * lax.axis_index("y") + lax.axis_index("c")

    pw = _pack_sections(w_qkv, w_attn_out, w_pool_in, w_pool_group, w_ffn_gate_up, w_ffn_down)
    vsh = jnp.concatenate([pool_norm, pool_scale, jnp.zeros((6, 128), F32)], axis=0)

    vg = _bcast_all(vsh, "gather_pool_vectors")
    comm = _Comm({n: pw[n].astype(BF16) for n, _ in SECTIONS}, me, deps=(vg,))
    pool_norm_full = vg[:, 0, :].reshape(1, D)
    pool_scale_full = vg[:, 1, :].reshape(1, D)

    grad_x, vec = _local_step(x[0], loss_target[0], comm, attn_norm, ffn_norm, final_norm,
                              pool_norm_full, pool_scale_full)

    vw = _vec_pack(attn_norm, ffn_norm, final_norm, pool_norm, pool_scale, me)
    vm = _vec_pack(m_attn_norm, m_ffn_norm, m_final_norm, m_pool_norm, m_pool_scale, me)
    vv = _vec_pack(v_attn_norm, v_ffn_norm, v_final_norm, v_pool_norm, v_pool_scale, me)

    gu_shape = w_ffn_gate_up.shape
    res = {}
    gu_t, d_res = {}, {}
    vec_out = None
    after = grad_x
    for group in range(len(RS_GROUPS)):
        if group == len(RS_GROUPS) - 1:
            VR = _bcast_all(vec, "exchange_vector_grads", deps=(after,))
            vec_out = _adamw(VR, vw, vm, vv, tr=8, name="adamw_vec")
            after = vec_out[0]
        for n, R in comm.received(group, after).items():
            if n in ("d0", "d1"):
                l = int(n[1])
                d_res[l] = _adamw(R, w_ffn_down[l], m_w_ffn_down[l], v_w_ffn_down[l], tr=352, name=f"adamw_{n}")
                after = d_res[l][0]
            elif n in ("gu0", "gu1"):
                gu_t[int(n[2])] = after = _grad_sum_t(R, f"grad_sum_{n}")
            elif n == "pg":
                out = _adamw_pool_group(R, w_pool_group[0], m_w_pool_group[0], v_w_pool_group[0])
                res["pg"] = tuple(a[None] for a in out)
                after = out[0]
            elif n in ("wo", "wpi"):
                w, m, v = ((w_attn_out, m_w_attn_out, v_w_attn_out) if n == "wo"
                           else (w_pool_in, m_w_pool_in, v_w_pool_in))
                res[n] = _adamw(R, w[0], m[0], v[0], tr=128, name=f"adamw_{n}")
                res[n] = tuple(a[None] for a in res[n])
                after = res[n][0]
            else:
                g = _grad_sum_t(R, "grad_sum_qkv")
                out = _adam_plain(g, w_qkv[0], m_w_qkv[0], v_w_qkv[0], tr=256, name="adamw_qkv")
                res["qkv"] = tuple(a[None] for a in (g,) + tuple(out))
                after = out[0]
    g_gu = jnp.stack([gu_t[0], gu_t[1]])
    flat = (2 * D, gu_shape[2])
    out = _adam_plain(g_gu.reshape(flat), w_ffn_gate_up.reshape(flat), m_w_ffn_gate_up.reshape(flat),
                      v_w_ffn_gate_up.reshape(flat), tr=256, name="adamw_gu")
    res["gu"] = (g_gu,) + tuple(a.reshape(gu_shape) for a in out)
    res["d"] = tuple(jnp.stack([d_res[0][kind], d_res[1][kind]]) for kind in range(4))

    outs = []
    for kind in range(4):
        an, fn, fin, pn, ps = _vec_unpack(vec_out[kind], me)
        outs.append((an, res["qkv"][kind], res["wo"][kind], pn, res["wpi"][kind], res["pg"][kind], ps, fn,
                     res["gu"][kind], res["d"][kind], fin))
    loss = 0.5 * jnp.sum(vec_out[0][6]) / D
    return (loss, grad_x[None]) + outs[0] + outs[1] + outs[2] + outs[3]
```

```python
import jax
import jax.numpy as jnp
from jax import lax
from jax.experimental import pallas as pl
from jax.experimental.pallas import tpu as pltpu

F32 = jnp.float32
BF16 = jnp.bfloat16

D = 1024
NDEV = 8
HEADS = 8
HD = 128
QB = 128
NGROUPS = 3
DILS = (1, 4, 16)
DFF = 2816
HCH = 1408
POOL_G = 4
PGD = 256
RMS_EPS = 1e-6
NEG = -1e30

ADAM_LR = 0.001
ADAM_B1 = 0.9
ADAM_B2 = 0.999
ADAM_EPS = 1e-08
ADAM_WD = 0.01
ADAM_STEP = 10

VMEM_LIMIT = 52 * 1024 * 1024

SECTIONS = (("qkv", 1152), ("wo", 128), ("wpi", 128), ("gu0", 704), ("gu1", 704),
            ("d0", 352), ("d1", 352), ("pg", 32))
LOC_OFF = {}
GLB_OFF = {}
_o = 0
for _n, _r in SECTIONS:
    LOC_OFF[_n] = _o
    GLB_OFF[_n] = _o * NDEV
    _o += _r
PACK_ROWS = _o
GLB_ROWS = PACK_ROWS * NDEV
SEC_ROWS = dict(SECTIONS)


def _cparams(n_grid):
    return pltpu.CompilerParams(dimension_semantics=("arbitrary",) * n_grid, vmem_limit_bytes=VMEM_LIMIT)


def _shard_pos(name, dev):
    n = SEC_ROWS[name]
    if name in ("gu0", "gu1"):
        return ((dev % 4) // 2) * (2 * HCH) + (dev // 4) * HCH + (dev % 2) * n
    return dev * n


def _mm(a, b, *, mode, M, N, K, tm, tn, tk, out_dtype, name, a_off=(0, 0), b_off=(0, 0), res=None,
        out_rows=None, out_off=0, out_prev=None, deps=()):
    nm, nn, nk = M // tm, N // tn, K // tk
    assert nm * tm == M and nn * tn == N and nk * tk == K
    if mode == "nn":
        a_bs, b_bs = (tm, tk), (tk, tn)
        a_ix = lambda i, j, k: (i, k)
        b_ix = lambda i, j, k: (k, j)
        dims = (((1,), (0,)), ((), ()))
    elif mode == "nt":
        a_bs, b_bs = (tm, tk), (tn, tk)
        a_ix = lambda i, j, k: (i, k)
        b_ix = lambda i, j, k: (j, k)
        dims = (((1,), (1,)), ((), ()))
    else:
        a_bs, b_bs = (tk, tm), (tk, tn)
        a_ix = lambda i, j, k: (k, i)
        b_ix = lambda i, j, k: (k, j)
        dims = (((0,), (0,)), ((), ()))

    def spec(bs, ix, off):
        def im(i, j, k):
            r, c = ix(i, j, k)
            return (r + off[0], c + off[1])
        return pl.BlockSpec(bs, im)

    in_specs = [spec(a_bs, a_ix, a_off), spec(b_bs, b_ix, b_off)]
    args = [a, b]
    if res is not None:
        in_specs.append(pl.BlockSpec((tm, tn), lambda i, j, k: (i, j)))
        args.append(res)
    out_shape = jax.ShapeDtypeStruct((M if out_rows is None else out_rows, N), out_dtype)
    out_spec = pl.BlockSpec((tm, tn), lambda i, j, k: (i + out_off, j))
    has_res = res is not None
    extra = list(deps) + ([out_prev] if out_prev is not None else [])
    for dep in extra:
        in_specs.append(pl.BlockSpec(memory_space=pl.ANY))
        args.append(dep)
    o_pos = 2 + int(has_res) + len(extra)
    aliases = {len(args) - 1: 0} if out_prev is not None else {}

    def kern(*refs):
        a_ref, b_ref = refs[0], refs[1]
        res_ref = refs[2] if has_res else None
        o_ref = refs[o_pos]
        av = a_ref[...]
        bv = b_ref[...]
        if av.dtype != BF16:
            av = av.astype(BF16)
        if bv.dtype != BF16:
            bv = bv.astype(BF16)
        part = lax.dot_general(av, bv, dims, preferred_element_type=F32)

        def write(val):
            if has_res:
                val = val + res_ref[...]
            o_ref[...] = val.astype(out_dtype)

        if nk == 1:
            write(part)
        else:
            acc_ref = refs[-1]
            k = pl.program_id(2)

            @pl.when(k == 0)
            def _():
                acc_ref[...] = part

            @pl.when(k > 0)
            def _():
                acc_ref[...] += part

            @pl.when(k == nk - 1)
            def _():
                write(acc_ref[...])

    scratch = [pltpu.VMEM((tm, tn), F32)] if nk > 1 else []
    return pl.pallas_call(
        kern, grid=(nm, nn, nk), in_specs=in_specs, out_specs=out_spec, out_shape=out_shape,
        scratch_shapes=scratch, input_output_aliases=aliases, compiler_params=_cparams(3), name=name)(*args)


def _mm_rms_bwd(a, b, x, g, dres, *, mode, M, K, tm, name, b_off=(0, 0), deps=()):
    nd = len(deps)
    b_bs = (K, D) if mode == "nn" else (D, K)
    dims = (((1,), (0,)), ((), ())) if mode == "nn" else (((1,), (1,)), ((), ()))

    def kern(a_ref, b_ref, x_ref, g_ref, dres_ref, *rest):
        dx_ref, dg_ref = rest[nd:]
        i = pl.program_id(0)
        av = a_ref[...]
        if av.dtype != BF16:
            av = av.astype(BF16)
        dhv = lax.dot_general(av, b_ref[...], dims, preferred_element_type=F32)
        xv = x_ref[...]
        r = lax.rsqrt(jnp.mean(xv * xv, axis=-1, keepdims=True) + RMS_EPS)
        xhat = xv * r
        gy = dhv * g_ref[...]
        dx_ref[...] = dres_ref[...] + r * (gy - xhat * jnp.mean(gy * xhat, axis=-1, keepdims=True))
        part = jnp.sum(dhv * xhat, axis=0, keepdims=True)

        @pl.when(i == 0)
        def _():
            dg_ref[...] = part

        @pl.when(i > 0)
        def _():
            dg_ref[...] += part

    row = pl.BlockSpec((tm, D), lambda i: (i, 0))
    vec = pl.BlockSpec((1, D), lambda i: (0, 0))
    return pl.pallas_call(
        kern, grid=(M // tm,),
        in_specs=[pl.BlockSpec((tm, K), lambda i: (i, 0)),
                  pl.BlockSpec(b_bs, lambda i: b_off, pipeline_mode=pl.Buffered(1)), row, vec, row]
        + [pl.BlockSpec(memory_space=pl.ANY)] * nd,
        out_specs=[row, vec],
        out_shape=[jax.ShapeDtypeStruct((M, D), F32), jax.ShapeDtypeStruct((1, D), F32)],
        compiler_params=_cparams(1), name=name)(a, b, x, g, dres, *deps)


def _mm_res_norm(a, b, res, g, *, K, tm, name, b_off=(0, 0), tgt=None):
    M = a.shape[0]
    head = tgt is not None

    def kern(a_ref, b_ref, res_ref, g_ref, *rest):
        xv = res_ref[...] + jnp.dot(a_ref[...], b_ref[...], preferred_element_type=F32)
        gv = g_ref[...]
        r = lax.rsqrt(jnp.mean(xv * xv, axis=-1, keepdims=True) + RMS_EPS)
        xhat = xv * r
        if not head:
            xo_ref, h_ref = rest
            xo_ref[...] = xv
            h_ref[...] = (xhat * gv).astype(BF16)
            return
        t_ref, dx_ref, dg_ref, ls_ref = rest
        i = pl.program_id(0)
        e = xhat * gv - t_ref[...]
        dy = e * (1.0 / D)
        gy = dy * gv
        dx_ref[...] = r * (gy - xhat * jnp.mean(gy * xhat, axis=-1, keepdims=True))
        dgp = jnp.sum(dy * xhat, axis=0, keepdims=True)
        lsp = jnp.sum(e * e, axis=0, keepdims=True)

        @pl.when(i == 0)
        def _():
            dg_ref[...] = dgp
            ls_ref[...] = lsp

        @pl.when(i > 0)
        def _():
            dg_ref[...] += dgp
            ls_ref[...] += lsp

    row = pl.BlockSpec((tm, D), lambda i: (i, 0))
    vec = pl.BlockSpec((1, D), lambda i: (0, 0))
    in_specs = [pl.BlockSpec((tm, K), lambda i: (i, 0)),
                pl.BlockSpec((K, D), lambda i: b_off, pipeline_mode=pl.Buffered(1)), row, vec]
    if head:
        return pl.pallas_call(
            kern, grid=(M // tm,), in_specs=in_specs + [row], out_specs=[row, vec, vec],
            out_shape=[jax.ShapeDtypeStruct((M, D), F32), jax.ShapeDtypeStruct((1, D), F32),
                       jax.ShapeDtypeStruct((1, D), F32)],
            compiler_params=_cparams(1), name=name)(a, b, res, g, tgt)
    return pl.pallas_call(
        kern, grid=(M // tm,), in_specs=in_specs, out_specs=[row, row],
        out_shape=[jax.ShapeDtypeStruct((M, D), F32), jax.ShapeDtypeStruct((M, D), BF16)],
        compiler_params=_cparams(1), name=name)(a, b, res, g)


def _rms_fwd(x, g, name, deps=()):
    S = x.shape[0]
    tr = 512

    def kern(x_ref, g_ref, *rest):
        h_ref = rest[-1]
        xv = x_ref[...]
        r = lax.rsqrt(jnp.mean(xv * xv, axis=-1, keepdims=True) + RMS_EPS)
        h_ref[...] = (xv * r * g_ref[...]).astype(BF16)

    return pl.pallas_call(
        kern, grid=(S // tr,),
        in_specs=[pl.BlockSpec((tr, D), lambda i: (i, 0)), pl.BlockSpec((1, D), lambda i: (0, 0))]
        + [pl.BlockSpec(memory_space=pl.ANY)] * len(deps),
        out_specs=pl.BlockSpec((tr, D), lambda i: (i, 0)),
        out_shape=jax.ShapeDtypeStruct((S, D), BF16), compiler_params=_cparams(1), name=name)(x, g, *deps)


def _chunks_put(scr, val):
    for c in range(scr.shape[0]):
        scr[c] = val[:, c * 128:(c + 1) * 128]


def _chunks_get(scr):
    return jnp.concatenate([scr[c] for c in range(scr.shape[0])], axis=1)


def _chunks_rows(scr, r, n, dil):
    return jnp.concatenate([scr.at[c][pl.ds(r, n, stride=dil), :] for c in range(scr.shape[0])], axis=1)


def _chunks_add_rows(scr, val, r, n, dil, accumulate):
    for c in range(scr.shape[0]):
        rows = pl.ds(r, n, stride=dil)
        piece = val[:, c * 128:(c + 1) * 128]
        tile = scr.at[c]
        tile[rows, :] = tile[rows, :] + piece if accumulate else piece


def _rms_fwd_folded(x, g, name, deps=()):
    S = x.shape[0]
    tr = 512
    dils = DILS[1:]

    def kern(x_ref, g_ref, *rest):
        outs, scr = rest[len(deps):-1], rest[-1]
        xv = x_ref[...]
        r = lax.rsqrt(jnp.mean(xv * xv, axis=-1, keepdims=True) + RMS_EPS)
        h = (xv * r * g_ref[...]).astype(BF16)
        outs[0][...] = h
        _chunks_put(scr, h.astype(F32))
        for o_ref, dil in zip(outs[1:], dils):
            for res in range(dil):
                o_ref[res] = _chunks_rows(scr, res, tr // dil, dil).astype(BF16)

    return pl.pallas_call(
        kern, grid=(S // tr,),
        in_specs=[pl.BlockSpec((tr, D), lambda i: (i, 0)), pl.BlockSpec((1, D), lambda i: (0, 0))]
        + [pl.BlockSpec(memory_space=pl.ANY)] * len(deps),
        out_specs=[pl.BlockSpec((tr, D), lambda i: (i, 0))]
        + [pl.BlockSpec((dil, tr // dil, D), lambda i: (0, i, 0)) for dil in dils],
        out_shape=[jax.ShapeDtypeStruct((S, D), BF16)]
        + [jax.ShapeDtypeStruct((dil, S // dil, D), BF16) for dil in dils],
        scratch_shapes=[pltpu.VMEM((D // 128, tr, 128), F32)],
        compiler_params=_cparams(1), name=name)(x, g, *deps)


def _rms_bwd(dh, x, g, dres, name, folded=()):
    S = x.shape[0]
    tr = 512
    nf = len(folded)

    def kern(dh_ref, *rest):
        f_refs = rest[:nf]
        x_ref, g_ref, dres_ref, dx_ref, dg_ref = rest[nf:nf + 5]
        i = pl.program_id(0)
        xv = x_ref[...]
        if nf:
            acc_ref = rest[nf + 5]
            _chunks_put(acc_ref, dh_ref[...].astype(F32))
            for f_ref in f_refs:
                dil = f_ref.shape[0]
                for res in range(dil):
                    _chunks_add_rows(acc_ref, f_ref[res], res, tr // dil, dil, True)
            dhv = _chunks_get(acc_ref)
        else:
            dhv = dh_ref[...].astype(F32)
        r = lax.rsqrt(jnp.mean(xv * xv, axis=-1, keepdims=True) + RMS_EPS)
        xhat = xv * r
        gy = dhv * g_ref[...]
        dx_ref[...] = dres_ref[...] + r * (gy - xhat * jnp.mean(gy * xhat, axis=-1, keepdims=True))
        part = jnp.sum(dhv * xhat, axis=0, keepdims=True)

        @pl.when(i == 0)
        def _():
            dg_ref[...] = part

        @pl.when(i > 0)
        def _():
            dg_ref[...] += part

    row = pl.BlockSpec((tr, D), lambda i: (i, 0))
    vec = pl.BlockSpec((1, D), lambda i: (0, 0))
    fspecs = [pl.BlockSpec((f.shape[0], tr // f.shape[0], D), lambda i: (0, i, 0)) for f in folded]
    return pl.pallas_call(
        kern, grid=(S // tr,), in_specs=[row] + fspecs + [row, vec, row], out_specs=[row, vec],
        out_shape=[jax.ShapeDtypeStruct((S, D), F32), jax.ShapeDtypeStruct((1, D), F32)],
        scratch_shapes=[pltpu.VMEM((D // 128, tr, 128), F32)] if nf else [],
        compiler_params=_cparams(1), name=name)(dh, *folded, x, g, dres)


def _ffn_up(h, G, name):
    S = h.shape[0]
    tm = 512
    nj = DFF // HCH

    def kern(h_ref, w_ref, gu_ref, act_ref):
        gu = lax.dot_general(h_ref[...], w_ref[...], (((1,), (1,)), ((), ())), preferred_element_type=F32)
        gu_ref[...] = gu.astype(BF16)
        gate = gu[:, :HCH]
        up = gu[:, HCH:]
        act_ref[...] = (gate * jax.nn.sigmoid(gate) * up).astype(BF16)

    return pl.pallas_call(
        kern, grid=(nj, S // tm),
        in_specs=[pl.BlockSpec((tm, D), lambda j, i: (i, 0)),
                  pl.BlockSpec((2 * HCH, D), lambda j, i: (j, 0))],
        out_specs=[pl.BlockSpec((tm, 2 * HCH), lambda j, i: (i, j)),
                   pl.BlockSpec((tm, HCH), lambda j, i: (i, j))],
        out_shape=[jax.ShapeDtypeStruct((S, 2 * DFF), BF16), jax.ShapeDtypeStruct((S, DFF), BF16)],
        compiler_params=_cparams(2), name=name)(h, G)


def _ffn_down_bwd(dx, G, gu, name):
    S = dx.shape[0]
    tm = 512
    nj = DFF // HCH

    def kern(dx_ref, w_ref, gu_ref, o_ref):
        dact = lax.dot_general(dx_ref[...].astype(BF16), w_ref[...], (((1,), (1,)), ((), ())),
                               preferred_element_type=F32)
        gate = gu_ref[:, :HCH].astype(F32)
        up = gu_ref[:, HCH:].astype(F32)
        sig = jax.nn.sigmoid(gate)
        silu = gate * sig
        o_ref[:, :HCH] = (dact * up * (sig * (1.0 + gate * (1.0 - sig)))).astype(BF16)
        o_ref[:, HCH:] = (dact * silu).astype(BF16)

    return pl.pallas_call(
        kern, grid=(nj, S // tm),
        in_specs=[pl.BlockSpec((tm, D), lambda j, i: (i, 0)),
                  pl.BlockSpec((HCH, D), lambda j, i: (j, 0)),
                  pl.BlockSpec((tm, 2 * HCH), lambda j, i: (i, j))],
        out_specs=pl.BlockSpec((tm, 2 * HCH), lambda j, i: (i, j)),
        out_shape=jax.ShapeDtypeStruct((S, 2 * DFF), BF16),
        compiler_params=_cparams(2), name=name)(dx, G, gu)


def _trail(u, *, backward, name):
    S = u.shape[0]

    def kern(u_ref, o_ref):
        g = pl.program_id(0)
        for grp in range(POOL_G):
            @pl.when(g == grp)
            def _(grp=grp):
                uv = u_ref[...].astype(F32)
                row = lax.broadcasted_iota(jnp.int32, uv.shape, 0)
                cnt = jnp.minimum(row + 1, 2 << grp).astype(F32)
                s = uv / cnt if backward else uv
                for k in (1, 2, 4, 8)[:grp + 1]:
                    if backward:
                        sh = jnp.where(row < S - k, pltpu.roll(s, S - k, 0), 0.0)
                    else:
                        sh = jnp.where(row >= k, pltpu.roll(s, k, 0), 0.0)
                    s = s + sh
                if backward:
                    o_ref[...] = (s - uv).astype(BF16)
                else:
                    o_ref[...] = (s / cnt - uv).astype(BF16)

    blk = pl.BlockSpec((S, PGD), lambda g: (0, g))
    return pl.pallas_call(
        kern, grid=(POOL_G,), in_specs=[blk], out_specs=blk,
        out_shape=jax.ShapeDtypeStruct((S, D), BF16), compiler_params=_cparams(1), name=name)(u)


def _pool_out(yd, G, scale, xres):
    S = yd.shape[0]
    tm = min(S, 4096)

    def kern(y_ref, w_ref, s_ref, x_ref, o_ref):
        z = jnp.dot(y_ref[...], w_ref[...], preferred_element_type=F32)
        o_ref[...] = x_ref[...] + z * s_ref[...]

    tile = pl.BlockSpec((tm, PGD), lambda i, g: (i, g))
    return pl.pallas_call(
        kern, grid=(S // tm, POOL_G),
        in_specs=[tile, pl.BlockSpec((PGD, PGD), lambda i, g: (0, g)),
                  pl.BlockSpec((1, PGD), lambda i, g: (0, g)), tile],
        out_specs=tile, out_shape=jax.ShapeDtypeStruct((S, D), F32),
        compiler_params=_cparams(2), name="pool_out")(yd, G, scale, xres)


def _pool_out_bwd(dz, yd, G, scale):
    S = yd.shape[0]
    tm = min(S, 4096)
    ni = S // tm

    def kern(dz_ref, y_ref, w_ref, s_ref, dy_ref, ds_ref, dw_ref, acc_ref):
        i = pl.program_id(1)
        dzv = dz_ref[...]
        yv = y_ref[...]
        wv = w_ref[...]
        zraw = jnp.dot(yv, wv, preferred_element_type=F32)
        dsp = jnp.sum(dzv * zraw, axis=0, keepdims=True)
        dzr = (dzv * s_ref[...]).astype(BF16)
        dy_ref[...] = lax.dot_general(dzr, wv, (((1,), (1,)), ((), ())), preferred_element_type=F32)
        dwp = lax.dot_general(yv, dzr, (((0,), (0,)), ((), ())), preferred_element_type=F32)

        @pl.when(i == 0)
        def _():
            ds_ref[...] = dsp
            acc_ref[...] = dwp

        @pl.when(i > 0)
        def _():
            ds_ref[...] += dsp
            acc_ref[...] += dwp

        @pl.when(i == ni - 1)
        def _():
            dw_ref[...] = acc_ref[...].astype(BF16)

    tile = pl.BlockSpec((tm, PGD), lambda g, i: (i, g))
    return pl.pallas_call(
        kern, grid=(POOL_G, ni),
        in_specs=[tile, tile, pl.BlockSpec((PGD, PGD), lambda g, i: (0, g)),
                  pl.BlockSpec((1, PGD), lambda g, i: (0, g))],
        out_specs=[tile, pl.BlockSpec((1, PGD), lambda g, i: (0, g)),
                   pl.BlockSpec((PGD, PGD), lambda g, i: (0, g))],
        out_shape=[jax.ShapeDtypeStruct((S, D), F32), jax.ShapeDtypeStruct((1, D), F32),
                   jax.ShapeDtypeStruct((PGD, D), BF16)],
        scratch_shapes=[pltpu.VMEM((PGD, PGD), F32)],
        compiler_params=_cparams(2), name="pool_out_bwd")(dz, yd, G, scale)


def _bias_table():
    qi = jnp.arange(QB)[:, None]
    ki = jnp.arange(2 * QB)[None, :]
    delta = QB + qi - ki
    inband = (delta >= 0) & (delta <= QB)
    n = NGROUPS * HEADS
    slopes = jnp.exp2(-8.0 * jnp.arange(1, n + 1, dtype=F32) / n).reshape(NGROUPS, HEADS)
    dil = jnp.asarray(DILS, F32)
    bias = -slopes[:, :, None, None] * (delta.astype(F32)[None, None] * dil[:, None, None, None])
    return jnp.where(inband[None, None], bias, NEG)


def _attn_fwd(qkv_f, bias, nb, name):
    S = qkv_f.shape[0]
    nblk = S // QB
    scale = HD ** -0.5

    def kern(q_ref, kc_ref, kp_ref, vc_ref, vp_ref, b_ref, o_ref, l_ref, s_scr, p_scr, r_scr):
        b = pl.program_id(0)
        has_prev = jnp.bitwise_and(b, nb - 1) != 0
        col = lax.broadcasted_iota(jnp.int32, (QB, 2 * QB), 1)
        dead = jnp.logical_and(col < QB, jnp.logical_not(has_prev))
        lane = lax.broadcasted_iota(jnp.int32, (QB, HD), 1)
        lse_all = jnp.zeros((QB, HD), F32)
        for h in range(HEADS):
            sl = slice(h * HD, (h + 1) * HD)
            kk = jnp.concatenate([kp_ref[:, sl], kc_ref[:, sl]], axis=0)
            s_scr[h] = lax.dot_general(q_ref[:, sl], kk, (((1,), (1,)), ((), ())), preferred_element_type=F32)
        for h in range(HEADS):
            s = s_scr[h] * scale + b_ref[h]
            s = jnp.where(dead, NEG, s)
            m = jnp.max(s, axis=-1, keepdims=True)
            p = jnp.exp(s - m)
            den = jnp.sum(p, axis=-1, keepdims=True)
            p_scr[h] = p.astype(BF16)
            r_scr[h] = jnp.broadcast_to(1.0 / den, (QB, HD))
            lse_all = jnp.where(lane == h, m + jnp.log(den), lse_all)
        for h in range(HEADS):
            sl = slice(h * HD, (h + 1) * HD)
            vv = jnp.concatenate([vp_ref[:, sl], vc_ref[:, sl]], axis=0)
            o = jnp.dot(p_scr[h], vv, preferred_element_type=F32) * r_scr[h]
            o_ref[:, sl] = o.astype(BF16)
        l_ref[...] = lse_all

    def blk(colblk, prev):
        if prev:
            return pl.BlockSpec((QB, D), lambda b: (jnp.maximum(b - 1, 0), colblk))
        return pl.BlockSpec((QB, D), lambda b: (b, colblk))

    return pl.pallas_call(
        kern, grid=(nblk,),
        in_specs=[blk(0, False), blk(1, False), blk(1, True), blk(2, False), blk(2, True),
                  pl.BlockSpec((HEADS, QB, 2 * QB), lambda b: (0, 0, 0))],
        out_specs=[pl.BlockSpec((QB, D), lambda b: (b, 0)), pl.BlockSpec((QB, HD), lambda b: (b, 0))],
        out_shape=[jax.ShapeDtypeStruct((S, D), BF16), jax.ShapeDtypeStruct((S, HD), F32)],
        scratch_shapes=[pltpu.VMEM((HEADS, QB, 2 * QB), F32), pltpu.VMEM((HEADS, QB, 2 * QB), BF16),
                        pltpu.VMEM((HEADS, QB, HD), F32)],
        compiler_params=_cparams(1), name=name)(qkv_f, qkv_f, qkv_f, qkv_f, qkv_f, bias)


def _natural(ref, scr, tm):
    dil = ref.shape[0]
    for res in range(dil):
        _chunks_add_rows(scr, ref[res].astype(F32), res, tm // dil, dil, False)
    return _chunks_get(scr)


def _attn_merge(os, lses):
    S = os[0].shape[0]
    tm = 512

    def kern(o0, o1, o2, l0, l1, l2, om_ref, lm_ref, ls1, ls2, os1, os2):
        la = l0[...]
        lb = _natural(l1, ls1, tm)
        lc = _natural(l2, ls2, tm)
        m = jnp.maximum(jnp.maximum(la, lb), lc)
        e0, e1, e2 = jnp.exp(la - m), jnp.exp(lb - m), jnp.exp(lc - m)
        tot = e0 + e1 + e2
        lm_ref[...] = m + jnp.log(tot)
        w0, w1, w2 = e0 / tot, e1 / tot, e2 / tot
        for res in range(o1.shape[0]):
            _chunks_add_rows(os1, o1[res].astype(F32), res, tm // o1.shape[0], o1.shape[0], False)
        for res in range(o2.shape[0]):
            _chunks_add_rows(os2, o2[res].astype(F32), res, tm // o2.shape[0], o2.shape[0], False)
        for h in range(HEADS):
            sl = slice(h * HD, (h + 1) * HD)
            acc = w0[:, h:h + 1] * o0[:, sl].astype(F32) + w1[:, h:h + 1] * os1[h] + w2[:, h:h + 1] * os2[h]
            om_ref[:, sl] = acc.astype(BF16)

    def spec(a, c):
        if a.ndim == 2:
            return pl.BlockSpec((tm, c), lambda i: (i, 0))
        return pl.BlockSpec((a.shape[0], tm // a.shape[0], c), lambda i: (0, i, 0))

    return pl.pallas_call(
        kern, grid=(S // tm,),
        in_specs=[spec(a, D) for a in os] + [spec(a, HD) for a in lses],
        out_specs=[pl.BlockSpec((tm, D), lambda i: (i, 0)), pl.BlockSpec((tm, HD), lambda i: (i, 0))],
        out_shape=[jax.ShapeDtypeStruct((S, D), BF16), jax.ShapeDtypeStruct((S, HD), F32)],
        scratch_shapes=[pltpu.VMEM((1, tm, HD), F32), pltpu.VMEM((1, tm, HD), F32),
                        pltpu.VMEM((HEADS, tm, HD), F32), pltpu.VMEM((HEADS, tm, HD), F32)],
        compiler_params=_cparams(1), name="attn_merge")(*os, *lses)


def _attn_bwd_prep(do, o, lse):
    S = o.shape[0]
    tm = 512
    dils = DILS[1:]

    def kern(do_ref, o_ref, l_ref, *rest):
        do_outs, l_outs, d_outs = rest[0:3], rest[3:5], rest[5:8]
        do_scr, l_scr, d_scr = rest[8:11]
        lane = lax.broadcasted_iota(jnp.int32, (tm, HD), 1)
        acc = jnp.zeros((tm, HD), F32)
        for h in range(HEADS):
            sl = slice(h * HD, (h + 1) * HD)
            prod = do_ref[:, sl] * o_ref[:, sl].astype(F32)
            acc = jnp.where(lane == h, jnp.sum(prod, axis=-1, keepdims=True), acc)
        d_scr[0] = acc
        l_scr[0] = l_ref[...]
        _chunks_put(do_scr, do_ref[...])
        do_outs[0][...] = do_ref[...].astype(BF16)
        d_outs[0][...] = acc
        for j, dil in enumerate(dils):
            for res in range(dil):
                n = tm // dil
                do_outs[1 + j][res] = _chunks_rows(do_scr, res, n, dil).astype(BF16)
                l_outs[j][res] = _chunks_rows(l_scr, res, n, dil)
                d_outs[1 + j][res] = _chunks_rows(d_scr, res, n, dil)

    def nat(c):
        return pl.BlockSpec((tm, c), lambda i: (i, 0))

    def fol(dil, c):
        return pl.BlockSpec((dil, tm // dil, c), lambda i: (0, i, 0))

    def shapes(c, dt, with_natural):
        first = [jax.ShapeDtypeStruct((S, c), dt)] if with_natural else []
        return first + [jax.ShapeDtypeStruct((dil, S // dil, c), dt) for dil in dils]

    outs = pl.pallas_call(
        kern, grid=(S // tm,), in_specs=[nat(D), nat(D), nat(HD)],
        out_specs=[nat(D)] + [fol(dil, D) for dil in dils] + [fol(dil, HD) for dil in dils]
        + [nat(HD)] + [fol(dil, HD) for dil in dils],
        out_shape=shapes(D, BF16, True) + shapes(HD, F32, False) + shapes(HD, F32, True),
        scratch_shapes=[pltpu.VMEM((HEADS, tm, HD), F32), pltpu.VMEM((1, tm, HD), F32), pltpu.VMEM((1, tm, HD), F32)],
        compiler_params=_cparams(1), name="attn_bwd_prep")(do, o, lse)
    return outs[0:3], [lse] + list(outs[3:5]), outs[5:8]


def _attn_bwd(qkv_f, do_f, lse_f, delta_f, bias, nb, name):
    S = qkv_f.shape[0]
    nblk = S // QB
    scale = HD ** -0.5

    def kern(q_ref, kc_ref, kp_ref, vc_ref, vp_ref, do_ref, l_ref, d_ref, b_ref, out_ref, dq_c, dk_c, dv_c,
             s_scr, dp_scr, ds_scr, p_scr):
        b = pl.program_id(0)

        @pl.when(b == 0)
        def _():
            dq_c[...] = jnp.zeros_like(dq_c)
            dk_c[...] = jnp.zeros_like(dk_c)
            dv_c[...] = jnp.zeros_like(dv_c)

        @pl.when(b == nblk)
        def _():
            out_ref[:, 0:D] = dq_c[...].astype(BF16)
            out_ref[:, D:2 * D] = dk_c[...].astype(BF16)
            out_ref[:, 2 * D:3 * D] = dv_c[...].astype(BF16)

        @pl.when(b < nblk)
        def _():
            has_prev = jnp.bitwise_and(b, nb - 1) != 0
            col = lax.broadcasted_iota(jnp.int32, (QB, 2 * QB), 1)
            dead = jnp.logical_and(col < QB, jnp.logical_not(has_prev))
            out_ref[:, 0:D] = dq_c[...].astype(BF16)
            lv = l_ref[...]
            dv_ = d_ref[...]
            for h in range(HEADS):
                sl = slice(h * HD, (h + 1) * HD)
                kk = jnp.concatenate([kp_ref[:, sl], kc_ref[:, sl]], axis=0)
                vv = jnp.concatenate([vp_ref[:, sl], vc_ref[:, sl]], axis=0)
                s_scr[h] = lax.dot_general(q_ref[:, sl], kk, (((1,), (1,)), ((), ())), preferred_element_type=F32)
                dp_scr[h] = lax.dot_general(do_ref[:, sl], vv, (((1,), (1,)), ((), ())),
                                            preferred_element_type=F32)
            for h in range(HEADS):
                s = s_scr[h] * scale + b_ref[h]
                s = jnp.where(dead, NEG, s)
                p = jnp.exp(s - lv[:, h:h + 1])
                ds_scr[h] = (p * (dp_scr[h] - dv_[:, h:h + 1]) * scale).astype(BF16)
                p_scr[h] = p.astype(BF16)
            for h in range(HEADS):
                sl = slice(h * HD, (h + 1) * HD)
                kk = jnp.concatenate([kp_ref[:, sl], kc_ref[:, sl]], axis=0)
                ds = ds_scr[h]
                dq_c[:, sl] = jnp.dot(ds, kk, preferred_element_type=F32)
                dkk = lax.dot_general(ds, q_ref[:, sl], (((0,), (0,)), ((), ())), preferred_element_type=F32)
                dvv = lax.dot_general(p_scr[h], do_ref[:, sl], (((0,), (0,)), ((), ())),
                                      preferred_element_type=F32)
                out_ref[:, D + h * HD:D + (h + 1) * HD] = (dk_c[:, sl] + dkk[:QB]).astype(BF16)
                out_ref[:, 2 * D + h * HD:2 * D + (h + 1) * HD] = (dv_c[:, sl] + dvv[:QB]).astype(BF16)
                dk_c[:, sl] = dkk[QB:]
                dv_c[:, sl] = dvv[QB:]

    last = nblk - 1

    def blk(colblk, prev):
        if prev:
            return pl.BlockSpec((QB, D), lambda b: (jnp.maximum(jnp.minimum(b, last) - 1, 0), colblk))
        return pl.BlockSpec((QB, D), lambda b: (jnp.minimum(b, last), colblk))

    stat = pl.BlockSpec((QB, HD), lambda b: (jnp.minimum(b, last), 0))
    return pl.pallas_call(
        kern, grid=(nblk + 1,),
        in_specs=[blk(0, False), blk(1, False), blk(1, True), blk(2, False), blk(2, True),
                  pl.BlockSpec((QB, D), lambda b: (jnp.minimum(b, last), 0)), stat, stat,
                  pl.BlockSpec((HEADS, QB, 2 * QB), lambda b: (0, 0, 0))],
        out_specs=pl.BlockSpec((QB, 3 * D), lambda b: (jnp.maximum(b - 1, 0), 0)),
        out_shape=jax.ShapeDtypeStruct((S, 3 * D), BF16),
        scratch_shapes=[pltpu.VMEM((QB, D), F32), pltpu.VMEM((QB, D), F32), pltpu.VMEM((QB, D), F32),
                        pltpu.VMEM((HEADS, QB, 2 * QB), F32), pltpu.VMEM((HEADS, QB, 2 * QB), F32),
                        pltpu.VMEM((HEADS, QB, 2 * QB), BF16), pltpu.VMEM((HEADS, QB, 2 * QB), BF16)],
        compiler_params=_cparams(1), name=name)(qkv_f, qkv_f, qkv_f, qkv_f, qkv_f, do_f, lse_f, delta_f, bias)


def _local_step(x, tgt, comm, attn_norm, ffn_norm, final_norm, pool_norm, pool_scale):
    S = x.shape[0]
    bias = _bias_table()
    g_attn = attn_norm.reshape(1, D)
    g_f0 = ffn_norm[0:1]
    g_f1 = ffn_norm[1:2]
    g_fin = final_norm.reshape(1, D)
    W = {}

    def ffn_fwd(xin, h, l, next_gain, target=None):
        gu, act = _ffn_up(h, W[f"gu{l}"], f"ffn_up{l}")
        return gu, act, _mm_res_norm(act, W[f"d{l}"], xin, next_gain, K=DFF, tm=512, tgt=target,
                                     name=f"ffn_down{l}")

    def ffn_bwd(dxo, xin, gain, h, gu, act, l, rs_group):
        dgu = _ffn_down_bwd(dxo, W[f"d{l}"], gu, f"ffn_down_bwd{l}")
        gw_d = _mm(act, dxo, mode="tn", M=DFF, N=D, K=S, tm=HCH, tn=D, tk=2048, out_dtype=BF16, name=f"gw_d{l}")
        gw_gu = _mm(dgu, h, mode="tn", M=2 * DFF, N=D, K=S, tm=HCH, tn=D, tk=2048, out_dtype=BF16, name=f"gw_gu{l}")
        token = comm.send_grads(rs_group, {f"d{l}": gw_d, f"gu{l}": gw_gu})
        return _mm_rms_bwd(dgu, W[f"gu{l}"], xin, gain, dxo, mode="nn", M=S, K=2 * DFF, tm=512, deps=(token,),
                           name=f"ffn_up_bwd{l}")

    nbs = [S // QB // dil for dil in DILS]
    hf = _rms_fwd_folded(x, g_attn, "rms_attn", deps=comm.ag_tokens)
    hf = [h.reshape(S, D) for h in hf]
    W.update(comm.weights(0, hf[0]))
    qkv_f, o_f, lse_f = [], [], []
    for g, dil in enumerate(DILS):
        qkv_f.append(_mm(hf[g], W["qkv"], mode="nt", M=S, N=3 * D, K=D, tm=2048, tn=1024, tk=D, out_dtype=BF16,
                         b_off=(3 * g, 0), name=f"qkv_proj{g}"))
        og, lg = _attn_fwd(qkv_f[g], bias[g], nbs[g], f"attn_fwd{g}")
        o_f.append(og if dil == 1 else og.reshape(dil, S // dil, D))
        lse_f.append(lg if dil == 1 else lg.reshape(dil, S // dil, HD))
    o, lse = _attn_merge(o_f, lse_f)
    W.update(comm.weights(1, o))
    x1, h1 = _mm_res_norm(o, W["wo"], x, g_f0, K=D, tm=1024, name="attn_out")
    gu0, act0, (x2, h2) = ffn_fwd(x1, h1, 0, pool_norm)

    W.update(comm.weights(2, x2))
    u = _mm(h2, W["wpi"], mode="nn", M=S, N=D, K=D, tm=1024, tn=D, tk=D, out_dtype=F32, name="pool_in")
    yd = _trail(u, backward=False, name="trail_fwd")
    x3 = _pool_out(yd, W["pg"], pool_scale, x2)
    h3 = _rms_fwd(x3, g_f1, "rms_ffn1")
    gu1, act1, (dx4, d_fin, lossvec) = ffn_fwd(x3, h3, 1, g_fin, target=tgt)

    dx3, d_f1 = ffn_bwd(dx4, x3, g_f1, h3, gu1, act1, 1, 0)
    dyd, d_scale, gw_pg = _pool_out_bwd(dx3, yd, W["pg"], pool_scale)
    du = _trail(dyd, backward=True, name="trail_bwd")
    gw_pi = _mm(h2, du, mode="tn", M=D, N=D, K=S, tm=D, tn=D, tk=2048, out_dtype=BF16, name="gw_pi")
    token = comm.send_grads(1, {"pg": gw_pg, "wpi": gw_pi})
    dx2, d_pool = _mm_rms_bwd(du, W["wpi"], x2, pool_norm, dx3, mode="nt", M=S, K=D, tm=1024, deps=(token,),
                              name="pool_in_bwd")
    dx1, d_f0 = ffn_bwd(dx2, x1, g_f0, h1, gu0, act0, 0, 2)

    gw_o = _mm(o, dx1, mode="tn", M=D, N=D, K=S, tm=D, tn=D, tk=2048, out_dtype=BF16, name="gw_o")
    do = _mm(dx1, W["wo"], mode="nt", M=S, N=D, K=D, tm=1024, tn=D, tk=D, out_dtype=F32, name="attn_out_bwd")
    do_f, lse_ff, delta_f = _attn_bwd_prep(do, o, lse)
    dqkv_f, gw_qkv = [], None
    for g in range(NGROUPS):
        dqkv_f.append(_attn_bwd(qkv_f[g], do_f[g].reshape(S, D), lse_ff[g].reshape(S, HD),
                                delta_f[g].reshape(S, HD), bias[g], nbs[g], f"attn_bwd{g}"))
        gw_qkv = _mm(dqkv_f[g], hf[g], mode="tn", M=3 * D, N=D, K=S, tm=1024, tn=D, tk=2048, out_dtype=BF16,
                     out_rows=NGROUPS * 3 * D, out_off=3 * g, out_prev=gw_qkv, name=f"gw_qkv{g}")
    token = comm.send_grads(3, {"wo": gw_o, "qkv": gw_qkv})
    dh0_f = [_mm(dqkv_f[g], W["qkv"], mode="nn", M=S, N=D, K=3 * D, tm=1024, tn=D, tk=3 * D, out_dtype=F32,
                 b_off=(g, 0), deps=(token,), name=f"qkv_proj_bwd{g}") for g in range(NGROUPS)]
    folded = [dh0_f[g].reshape(dil, S // dil, D) for g, dil in enumerate(DILS) if dil > 1]
    grad_x, d_attn = _rms_bwd(dh0_f[0], x, g_attn, dx1, "rms_attn_bwd", folded=folded)

    vec = jnp.concatenate([d_attn, d_f0, d_f1, d_fin, d_pool, d_scale, lossvec, jnp.zeros((1, D), F32)], axis=0)
    return grad_x, vec


def _mesh_pos():
    x, y, c = lax.axis_index("x"), lax.axis_index("y"), lax.axis_index("c")
    return x, y, c, 4 * x + 2 * y + c


def _peer(x, y, c, k):
    kx, ky, kc = (k >> 2) & 1, (k >> 1) & 1, k & 1
    px = 1 - x if kx else x
    py = 1 - y if ky else y
    pc = 1 - c if kc else c
    return (px, py, pc), 4 * px + 2 * py + pc


ANY = pl.BlockSpec(memory_space=pl.ANY)


HBM = pl.BlockSpec(memory_space=pltpu.HBM)
SEMS = pl.BlockSpec(memory_space=pltpu.SEMAPHORE)
EFFECT = pltpu.SideEffectType.DATAFLOW_SIDE_EFFECTING
NPEER = NDEV - 1

AG_GROUPS = (("qkv",), ("wo", "gu0", "d0"), ("wpi", "pg", "gu1", "d1"))
AG_ORDER = tuple(n for grp in AG_GROUPS for n in grp)
RS_GROUPS = (("d1", "gu1"), ("pg", "wpi"), ("d0", "gu0"), ("wo", "qkv"))


def _hbm(a):
    return pltpu.with_memory_space_constraint(a, pltpu.HBM)


def _remote(src, dst, send, recv, peer):
    return pltpu.make_async_remote_copy(src_ref=src, dst_ref=dst, send_sem=send, recv_sem=recv, device_id=peer,
                                        device_id_type=pl.DeviceIdType.MESH)


def _bcast_all(v, name, deps=()):
    W = v.shape[1]
    nd = len(deps)

    def kern(v_ref, *rest):
        o_ref, send, recv, lsem = rest[nd:]
        x, y, c, me = _mesh_pos()
        own = pltpu.make_async_copy(v_ref, o_ref.at[me], lsem)
        own.start()
        cps = [_remote(v_ref, o_ref.at[me], send.at[k - 1], recv.at[k - 1], _peer(x, y, c, k)[0])
               for k in range(1, NDEV)]
        for cp in cps:
            cp.start()
        for cp in cps:
            cp.wait_recv()
            cp.wait_send()
        own.wait()

    return pl.pallas_call(
        kern, in_specs=[ANY] * (1 + nd), out_specs=ANY, out_shape=jax.ShapeDtypeStruct((NDEV, 8, W), F32),
        scratch_shapes=[pltpu.SemaphoreType.DMA((NPEER,)), pltpu.SemaphoreType.DMA((NPEER,)),
                        pltpu.SemaphoreType.DMA(())],
        name=name)(v, *deps)


ALL_KS = tuple(range(1, NDEV))
AG_KS1 = (1, 2, 4, 6)
AG_KS2 = (2, 4, 6)


def _split_start(srcs, src_of, lands, copy_refs, name, deps=(), ks=ALL_KS, to=None):
    ns, n, nd, nk = len(srcs), len(lands), len(deps), len(ks)

    def body(*refs):
        ins, land = refs[:ns], refs[ns:ns + n]
        send, recv = refs[ns + n + nd], refs[ns + n + nd + 1]
        token = refs[-1]
        x, y, c, me = _mesh_pos()
        for j in range(n):
            for i, k in enumerate(ks):
                _, pid = _peer(x, y, c, k)
                dest, _ = _peer(x, y, c, k if to is None else to)
                src, dst = copy_refs(j, (land[j] if src_of[j] is None else ins[src_of[j]]), land[j], me, pid)
                _remote(src, dst, send.at[j * nk + i], recv.at[j * nk + i], dest).start()
        token[...] = jnp.zeros_like(token)

    outs = pl.pallas_call(
        body, name=name,
        out_shape=(pltpu.SemaphoreType.DMA((n * nk,)), pltpu.SemaphoreType.DMA((n * nk,)))
        + tuple(pltpu.HBM(a.shape, a.dtype) for a in srcs) + tuple(pltpu.HBM(a.shape, a.dtype) for a in lands)
        + (jax.ShapeDtypeStruct((8, 128), F32),),
        in_specs=(HBM,) * (ns + n) + (ANY,) * nd,
        out_specs=(SEMS, SEMS) + (HBM,) * (ns + n) + (pl.BlockSpec(memory_space=pltpu.VMEM),),
        input_output_aliases={i: 2 + i for i in range(ns + n)},
        compiler_params=pltpu.CompilerParams(has_side_effects=EFFECT),
    )(*[_hbm(a) for a in srcs], *[_hbm(a) for a in lands], *deps)
    return outs[0], outs[1], list(outs[2:2 + ns]), list(outs[2 + ns:2 + ns + n]), outs[-1]


def _split_wait(srcs, src_of, lands, send, recv, sem_rows, wait_refs, after, name, ks=ALL_KS):
    ns, n, nk = len(srcs), len(lands), len(ks)

    def body(*refs):
        ins, land = refs[:ns], refs[ns:ns + n]
        send_ref, recv_ref = refs[ns + n], refs[ns + n + 1]
        x, y, c, me = _mesh_pos()
        for j in range(n):
            for i, k in enumerate(ks):
                peer, _ = _peer(x, y, c, k)
                src, dst = wait_refs(j, (land[j] if src_of[j] is None else ins[src_of[j]]), land[j])
                sem = sem_rows[j] * nk + i
                cp = _remote(src, dst, send_ref.at[sem], recv_ref.at[sem], peer)
                cp.wait_send()
                cp.wait_recv()

    outs = pl.pallas_call(
        body, name=name,
        out_shape=tuple(pltpu.HBM(a.shape, a.dtype) for a in srcs) + tuple(pltpu.HBM(a.shape, a.dtype) for a in lands),
        in_specs=(HBM,) * (ns + n) + (SEMS, SEMS, ANY),
        out_specs=(HBM,) * (ns + n),
        input_output_aliases={i: i for i in range(ns + n)},
        compiler_params=pltpu.CompilerParams(has_side_effects=EFFECT),
    )(*srcs, *lands, send, recv, after)
    return list(outs[:ns]), list(outs[ns:])


class _Comm:
    def __init__(self, shards, me, deps=()):
        self.me = me
        self.ag_land, self.ag_sems, self.ag_tokens = {}, {}, ()
        self.rs = []
        for part, names in enumerate((AG_GROUPS[0], AG_ORDER[len(AG_GROUPS[0]):])):
            rows = [SEC_ROWS[n] for n in names]
            lands = [lax.dynamic_update_slice(lax.empty((NDEV * r, D), BF16), shards[n], (_shard_pos(n, me), 0))
                     for n, r in zip(names, rows)]

            def copy_refs(j, src, land, me, pid, names=names, rows=rows):
                own = land.at[pl.ds(pl.multiple_of(_shard_pos(names[j], me), 16), rows[j])]
                return own, own

            send, recv, _, lands, token = _split_start([], [None] * len(names), lands, copy_refs, f"ag_start{part}",
                                                       deps=deps, ks=AG_KS1)
            deps = (token,)
            self.ag_tokens += (token,)
            for j, n in enumerate(names):
                self.ag_land[n] = lands[j]
                self.ag_sems[n] = (send, recv, j)

    def weights(self, group, after):
        names = AG_GROUPS[group]
        send, recv = self.ag_sems[names[0]][:2]
        idx = [self.ag_sems[n][2] for n in names]
        rows = [SEC_ROWS[n] for n in names]
        none = [None] * len(names)

        def wait_refs(j, src, land):
            return land.at[pl.ds(0, rows[j])], land.at[pl.ds(0, rows[j])]

        _, lands = _split_wait([], none, [self.ag_land[n] for n in names], send, recv, idx,
                               wait_refs, after, f"ag_wait{group}", ks=AG_KS1)

        def copy_refs(j, src, land, me, pid):
            theirs = land.at[pl.ds(pl.multiple_of(_shard_pos(names[j], pid), 16), rows[j])]
            return theirs, theirs

        send, recv, _, lands, token = _split_start([], none, lands, copy_refs, f"ag_pass{group}", ks=AG_KS2, to=1)
        _, lands = _split_wait([], none, lands, send, recv, list(range(len(names))), wait_refs, token,
                               f"ag_pass_wait{group}", ks=AG_KS2)
        return dict(zip(names, lands))

    def send_grads(self, group, gws):
        names = RS_GROUPS[group]
        rows = [SEC_ROWS[n] for n in names]
        grads = [gws[n] for n in names]
        me = self.me
        lands = [lax.dynamic_update_slice(
            lax.empty((NDEV, r, D), BF16),
            lax.dynamic_slice(g, (_shard_pos(n, me), 0), (r, D))[None], (me, 0, 0))
            for n, r, g in zip(names, rows, grads)]

        def copy_refs(j, src, land, me, pid):
            return src.at[pl.ds(pl.multiple_of(_shard_pos(names[j], pid), 16), rows[j])], land.at[me]

        send, recv, srcs, lands, token = _split_start(grads, list(range(len(names))), lands, copy_refs,
                                                      f"rs_start{group}")
        self.rs.append((names, rows, send, recv, srcs, lands))
        return token

    def received(self, group, after):
        names, rows, send, recv, srcs, lands = self.rs[group]

        def wait_refs(j, src, land):
            return src.at[pl.ds(0, rows[j])], land.at[0]

        _, lands = _split_wait(srcs, list(range(len(names))), lands, send, recv, list(range(len(names))), wait_refs,
                               after, f"rs_wait{group}")
        return dict(zip(names, lands))


def _sum_contributions(r_ref):
    g = r_ref[0].astype(F32)
    for dev in range(1, NDEV):
        g = g + r_ref[dev].astype(F32)
    return g


def _adam_math(g, w, m, v):
    c1 = 1.0 / (1.0 - ADAM_B1 ** ADAM_STEP)
    c2 = 1.0 / (1.0 - ADAM_B2 ** ADAM_STEP)
    mn = ADAM_B1 * m + (1.0 - ADAM_B1) * g
    vn = ADAM_B2 * v + (1.0 - ADAM_B2) * (g * g)
    return -ADAM_LR * ((mn * c1) / (jnp.sqrt(vn * c2) + ADAM_EPS) + ADAM_WD * w), mn, vn


def _adamw(R, w, m, v, *, tr, name, layer=None, prev=None):
    rows, C = w.shape[-2:]
    nprev = 0 if prev is None else 4

    def kern(r_ref, w_ref, m_ref, v_ref, *rest):
        g_out, d_out, m_out, v_out = rest[nprev:]
        g = _sum_contributions(r_ref)
        g_out[...] = g
        d_out[...], m_out[...], v_out[...] = _adam_math(g, w_ref[...], m_ref[...], v_ref[...])

    if layer is None:
        tile = pl.BlockSpec((tr, C), lambda i: (i, 0))
    else:
        tile = pl.BlockSpec((None, tr, C), lambda i: (layer, i, 0))
    shp = jax.ShapeDtypeStruct(w.shape, F32)
    return pl.pallas_call(
        kern, grid=(rows // tr,),
        in_specs=[pl.BlockSpec((NDEV, tr, C), lambda i: (0, i, 0)), tile, tile, tile]
        + [pl.BlockSpec(memory_space=pl.ANY)] * nprev,
        out_specs=[tile] * 4, out_shape=[shp] * 4,
        input_output_aliases={4 + k: k for k in range(nprev)},
        compiler_params=_cparams(1), name=name)(R, w, m, v, *(prev or ()))


def _adamw_pool_group(R, w, m, v):
    rows = SEC_ROWS["pg"]

    def kern(r_ref, w_ref, m_ref, v_ref, g_out, d_out, m_out, v_out):
        g = _sum_contributions(r_ref)
        g_out[0] = g
        d_out[0], m_out[0], v_out[0] = _adam_math(g, w_ref[0], m_ref[0], v_ref[0])

    blk = pl.BlockSpec((1, rows, PGD), lambda i: (i, 0, 0))
    shp = jax.ShapeDtypeStruct((POOL_G, rows, PGD), F32)
    return pl.pallas_call(
        kern, grid=(POOL_G,),
        in_specs=[pl.BlockSpec((NDEV, rows, PGD), lambda i: (0, 0, i)), blk, blk, blk],
        out_specs=[blk] * 4, out_shape=[shp] * 4, compiler_params=_cparams(1), name="adamw_pg")(R, w, m, v)


def _grad_sum_t(R, name, layers=None, layer=0, prev=None):
    rows = R.shape[1]
    tr = 128 if rows % 128 == 0 else rows

    def kern(r_ref, *rest):
        rest[-1][...] = _sum_contributions(r_ref).T

    if layers is None:
        out_spec = pl.BlockSpec((D, tr), lambda i: (0, i))
        out_shape = jax.ShapeDtypeStruct((D, rows), F32)
    else:
        out_spec = pl.BlockSpec((None, D, tr), lambda i: (layer, 0, i))
        out_shape = jax.ShapeDtypeStruct((layers, D, rows), F32)
    extra = [] if prev is None else [prev]
    return pl.pallas_call(
        kern, grid=(rows // tr,),
        in_specs=[pl.BlockSpec((NDEV, tr, D), lambda i: (0, i, 0))] + [pl.BlockSpec(memory_space=pl.ANY)] * len(extra),
        out_specs=out_spec, out_shape=out_shape, input_output_aliases={1: 0} if extra else {},
        compiler_params=_cparams(1), name=name)(R, *extra)


def _adam_plain(g, w, m, v, *, tr, name):
    rows, C = w.shape

    def kern(g_ref, w_ref, m_ref, v_ref, d_out, m_out, v_out):
        d_out[...], m_out[...], v_out[...] = _adam_math(g_ref[...], w_ref[...], m_ref[...], v_ref[...])

    tile = pl.BlockSpec((tr, C), lambda i: (i, 0))
    shp = jax.ShapeDtypeStruct((rows, C), F32)
    return pl.pallas_call(
        kern, grid=(rows // tr,), in_specs=[tile] * 4, out_specs=[tile] * 3, out_shape=[shp] * 3,
        compiler_params=_cparams(1), name=name)(g, w, m, v)


def _pack_sections(w_qkv, w_attn_out, w_pool_in, w_pool_group, w_ffn_gate_up, w_ffn_down):
    pg = w_pool_group[0].transpose(1, 0, 2).reshape(SEC_ROWS["pg"], D)
    return {"qkv": w_qkv[0].T, "wo": w_attn_out[0], "wpi": w_pool_in[0], "gu0": w_ffn_gate_up[0].T,
            "gu1": w_ffn_gate_up[1].T, "d0": w_ffn_down[0], "d1": w_ffn_down[1], "pg": pg}


def _vec_pack(attn_norm, ffn_norm, final_norm, pool_norm_sh, pool_scale_sh, me):
    def place(sh):
        return lax.dynamic_update_slice(jnp.zeros((1, D), F32), sh, (0, me * 128))
    return jnp.concatenate([attn_norm, ffn_norm, final_norm.reshape(1, D), place(pool_norm_sh),
                            place(pool_scale_sh), jnp.zeros((2, D), F32)], axis=0)


def _vec_unpack(p, me):
    def take(r):
        return lax.dynamic_slice(p[r:r + 1], (0, me * 128), (1, 128))
    return p[0:1], p[1:3], p[3], take(4), take(5)


def kernel(x, attn_norm, w_qkv, w_attn_out, pool_norm, w_pool_in, w_pool_group, pool_scale, ffn_norm, w_ffn_gate_up, w_ffn_down, final_norm, loss_target, m_attn_norm, m_w_qkv, m_w_attn_out, m_pool_norm, m_w_pool_in, m_w_pool_group, m_pool_scale, m_ffn_norm, m_w_ffn_gate_up, m_w_ffn_down, m_final_norm, v_attn_norm, v_w_qkv, v_w_attn_out, v_pool_norm, v_w_pool_in, v_w_pool_group, v_pool_scale, v_ffn_norm, v_w_ffn_gate_up, v_w_ffn_down, v_final_norm):
    me = 4 * lax.axis_index("x") + 2 * lax.axis_index("y") + lax.axis_index("c")

    pw = _pack_sections(w_qkv, w_attn_out, w_pool_in, w_pool_group, w_ffn_gate_up, w_ffn_down)
    vsh = jnp.concatenate([pool_norm, pool_scale, jnp.zeros((6, 128), F32)], axis=0)

    vg = _bcast_all(vsh, "gather_pool_vectors")
    comm = _Comm({n: pw[n].astype(BF16) for n, _ in SECTIONS}, me, deps=(vg,))
    pool_norm_full = vg[:, 0, :].reshape(1, D)
    pool_scale_full = vg[:, 1, :].reshape(1, D)

    grad_x, vec = _local_step(x[0], loss_target[0], comm, attn_norm, ffn_norm, final_norm,
                              pool_norm_full, pool_scale_full)

    vw = _vec_pack(attn_norm, ffn_norm, final_norm, pool_norm, pool_scale, me)
    vm = _vec_pack(m_attn_norm, m_ffn_norm, m_final_norm, m_pool_norm, m_pool_scale, me)
    vv = _vec_pack(v_attn_norm, v_ffn_norm, v_final_norm, v_pool_norm, v_pool_scale, me)

    gu_shape = w_ffn_gate_up.shape
    res = {}
    g_gu, d_res = None, None
    vec_out = None
    after = grad_x
    for group in range(len(RS_GROUPS)):
        if group == len(RS_GROUPS) - 1:
            VR = _bcast_all(vec, "exchange_vector_grads", deps=(after,))
            vec_out = _adamw(VR, vw, vm, vv, tr=8, name="adamw_vec")
            after = vec_out[0]
        for n, R in comm.received(group, after).items():
            if n in ("d0", "d1"):
                d_res = _adamw(R, w_ffn_down, m_w_ffn_down, v_w_ffn_down, tr=352, name=f"adamw_{n}",
                               layer=int(n[1]), prev=d_res)
                after = d_res[0]
            elif n in ("gu0", "gu1"):
                g_gu = after = _grad_sum_t(R, f"grad_sum_{n}", layers=gu_shape[0], layer=int(n[2]), prev=g_gu)
            elif n == "pg":
                out = _adamw_pool_group(R, w_pool_group[0], m_w_pool_group[0], v_w_pool_group[0])
                res["pg"] = tuple(a[None] for a in out)
                after = out[0]
            elif n in ("wo", "wpi"):
                w, m, v = ((w_attn_out, m_w_attn_out, v_w_attn_out) if n == "wo"
                           else (w_pool_in, m_w_pool_in, v_w_pool_in))
                res[n] = _adamw(R, w[0], m[0], v[0], tr=128, name=f"adamw_{n}")
                res[n] = tuple(a[None] for a in res[n])
                after = res[n][0]
            else:
                g = _grad_sum_t(R, "grad_sum_qkv")
                out = _adam_plain(g, w_qkv[0], m_w_qkv[0], v_w_qkv[0], tr=256, name="adamw_qkv")
                res["qkv"] = tuple(a[None] for a in (g,) + tuple(out))
                after = out[0]
    flat = (2 * D, gu_shape[2])
    out = _adam_plain(g_gu.reshape(flat), w_ffn_gate_up.reshape(flat), m_w_ffn_gate_up.reshape(flat),
                      v_w_ffn_gate_up.reshape(flat), tr=256, name="adamw_gu")
    res["gu"] = (g_gu,) + tuple(a.reshape(gu_shape) for a in out)
    res["d"] = tuple(d_res)

    outs = []
    for kind in range(4):
        an, fn, fin, pn, ps = _vec_unpack(vec_out[kind], me)
        outs.append((an, res["qkv"][kind], res["wo"][kind], pn, res["wpi"][kind], res["pg"][kind], ps, fn,
                     res["gu"][kind], res["d"][kind], fin))
    loss = 0.5 * jnp.sum(vec_out[0][6]) / D
    return (loss, grad_x[None]) + outs[0] + outs[1] + outs[2] + outs[3]
```

```python
import jax
import jax.numpy as jnp
from jax import lax
from jax.experimental import pallas as pl
from jax.experimental.pallas import tpu as pltpu

F32 = jnp.float32
BF16 = jnp.bfloat16

D = 1024
NDEV = 8
HEADS = 8
HD = 128
QB = 128
NGROUPS = 3
DILS = (1, 4, 16)
DFF = 2816
HCH = 1408
POOL_G = 4
PGD = 256
RMS_EPS = 1e-6
NEG = -1e30

ADAM_LR = 0.001
ADAM_B1 = 0.9
ADAM_B2 = 0.999
ADAM_EPS = 1e-08
ADAM_WD = 0.01
ADAM_STEP = 10

VMEM_LIMIT = 52 * 1024 * 1024

SECTIONS = (("qkv", 1152), ("wo", 128), ("wpi", 128), ("gu0", 704), ("gu1", 704),
            ("d0", 352), ("d1", 352), ("pg", 32))
LOC_OFF = {}
GLB_OFF = {}
_o = 0
for _n, _r in SECTIONS:
    LOC_OFF[_n] = _o
    GLB_OFF[_n] = _o * NDEV
    _o += _r
PACK_ROWS = _o
GLB_ROWS = PACK_ROWS * NDEV
SEC_ROWS = dict(SECTIONS)


def _cparams(n_grid):
    return pltpu.CompilerParams(dimension_semantics=("arbitrary",) * n_grid, vmem_limit_bytes=VMEM_LIMIT)


def _shard_pos(name, dev):
    n = SEC_ROWS[name]
    if name in ("gu0", "gu1"):
        return ((dev % 4) // 2) * (2 * HCH) + (dev // 4) * HCH + (dev % 2) * n
    return dev * n


def _mm(a, b, *, mode, M, N, K, tm, tn, tk, out_dtype, name, a_off=(0, 0), b_off=(0, 0), res=None,
        out_rows=None, out_off=0, out_prev=None, deps=()):
    nm, nn, nk = M // tm, N // tn, K // tk
    assert nm * tm == M and nn * tn == N and nk * tk == K
    if mode == "nn":
        a_bs, b_bs = (tm, tk), (tk, tn)
        a_ix = lambda i, j, k: (i, k)
        b_ix = lambda i, j, k: (k, j)
        dims = (((1,), (0,)), ((), ()))
    elif mode == "nt":
        a_bs, b_bs = (tm, tk), (tn, tk)
        a_ix = lambda i, j, k: (i, k)
        b_ix = lambda i, j, k: (j, k)
        dims = (((1,), (1,)), ((), ()))
    else:
        a_bs, b_bs = (tk, tm), (tk, tn)
        a_ix = lambda i, j, k: (k, i)
        b_ix = lambda i, j, k: (k, j)
        dims = (((0,), (0,)), ((), ()))

    def spec(bs, ix, off):
        def im(i, j, k):
            r, c = ix(i, j, k)
            return (r + off[0], c + off[1])
        return pl.BlockSpec(bs, im)

    in_specs = [spec(a_bs, a_ix, a_off), spec(b_bs, b_ix, b_off)]
    args = [a, b]
    if res is not None:
        in_specs.append(pl.BlockSpec((tm, tn), lambda i, j, k: (i, j)))
        args.append(res)
    out_shape = jax.ShapeDtypeStruct((M if out_rows is None else out_rows, N), out_dtype)
    out_spec = pl.BlockSpec((tm, tn), lambda i, j, k: (i + out_off, j))
    has_res = res is not None
    extra = list(deps) + ([out_prev] if out_prev is not None else [])
    for dep in extra:
        in_specs.append(pl.BlockSpec(memory_space=pl.ANY))
        args.append(dep)
    o_pos = 2 + int(has_res) + len(extra)
    aliases = {len(args) - 1: 0} if out_prev is not None else {}

    def kern(*refs):
        a_ref, b_ref = refs[0], refs[1]
        res_ref = refs[2] if has_res else None
        o_ref = refs[o_pos]
        av = a_ref[...]
        bv = b_ref[...]
        if av.dtype != BF16:
            av = av.astype(BF16)
        if bv.dtype != BF16:
            bv = bv.astype(BF16)
        part = lax.dot_general(av, bv, dims, preferred_element_type=F32)

        def write(val):
            if has_res:
                val = val + res_ref[...]
            o_ref[...] = val.astype(out_dtype)

        if nk == 1:
            write(part)
        else:
            acc_ref = refs[-1]
            k = pl.program_id(2)

            @pl.when(k == 0)
            def _():
                acc_ref[...] = part

            @pl.when(k > 0)
            def _():
                acc_ref[...] += part

            @pl.when(k == nk - 1)
            def _():
                write(acc_ref[...])

    scratch = [pltpu.VMEM((tm, tn), F32)] if nk > 1 else []
    return pl.pallas_call(
        kern, grid=(nm, nn, nk), in_specs=in_specs, out_specs=out_spec, out_shape=out_shape,
        scratch_shapes=scratch, input_output_aliases=aliases, compiler_params=_cparams(3), name=name)(*args)


def _mm_rms_bwd(a, b, x, g, dres, *, mode, M, K, tm, name, b_off=(0, 0), deps=()):
    nd = len(deps)
    b_bs = (K, D) if mode == "nn" else (D, K)
    dims = (((1,), (0,)), ((), ())) if mode == "nn" else (((1,), (1,)), ((), ()))

    def kern(a_ref, b_ref, x_ref, g_ref, dres_ref, *rest):
        dx_ref, dg_ref = rest[nd:]
        i = pl.program_id(0)
        av = a_ref[...]
        if av.dtype != BF16:
            av = av.astype(BF16)
        dhv = lax.dot_general(av, b_ref[...], dims, preferred_element_type=F32)
        xv = x_ref[...]
        r = lax.rsqrt(jnp.mean(xv * xv, axis=-1, keepdims=True) + RMS_EPS)
        xhat = xv * r
        gy = dhv * g_ref[...]
        dx_ref[...] = dres_ref[...] + r * (gy - xhat * jnp.mean(gy * xhat, axis=-1, keepdims=True))
        part = jnp.sum(dhv * xhat, axis=0, keepdims=True)

        @pl.when(i == 0)
        def _():
            dg_ref[...] = part

        @pl.when(i > 0)
        def _():
            dg_ref[...] += part

    row = pl.BlockSpec((tm, D), lambda i: (i, 0))
    vec = pl.BlockSpec((1, D), lambda i: (0, 0))
    return pl.pallas_call(
        kern, grid=(M // tm,),
        in_specs=[pl.BlockSpec((tm, K), lambda i: (i, 0)),
                  pl.BlockSpec(b_bs, lambda i: b_off, pipeline_mode=pl.Buffered(1)), row, vec, row]
        + [pl.BlockSpec(memory_space=pl.ANY)] * nd,
        out_specs=[row, vec],
        out_shape=[jax.ShapeDtypeStruct((M, D), F32), jax.ShapeDtypeStruct((1, D), F32)],
        compiler_params=_cparams(1), name=name)(a, b, x, g, dres, *deps)


def _mm_res_norm(a, b, res, g, *, K, tm, name, b_off=(0, 0), tgt=None):
    M = a.shape[0]
    head = tgt is not None

    def kern(a_ref, b_ref, res_ref, g_ref, *rest):
        xv = res_ref[...] + jnp.dot(a_ref[...], b_ref[...], preferred_element_type=F32)
        gv = g_ref[...]
        r = lax.rsqrt(jnp.mean(xv * xv, axis=-1, keepdims=True) + RMS_EPS)
        xhat = xv * r
        if not head:
            xo_ref, h_ref = rest
            xo_ref[...] = xv
            h_ref[...] = (xhat * gv).astype(BF16)
            return
        t_ref, dx_ref, dg_ref, ls_ref = rest
        i = pl.program_id(0)
        e = xhat * gv - t_ref[...]
        dy = e * (1.0 / D)
        gy = dy * gv
        dx_ref[...] = r * (gy - xhat * jnp.mean(gy * xhat, axis=-1, keepdims=True))
        dgp = jnp.sum(dy * xhat, axis=0, keepdims=True)
        lsp = jnp.sum(e * e, axis=0, keepdims=True)

        @pl.when(i == 0)
        def _():
            dg_ref[...] = dgp
            ls_ref[...] = lsp

        @pl.when(i > 0)
        def _():
            dg_ref[...] += dgp
            ls_ref[...] += lsp

    row = pl.BlockSpec((tm, D), lambda i: (i, 0))
    vec = pl.BlockSpec((1, D), lambda i: (0, 0))
    in_specs = [pl.BlockSpec((tm, K), lambda i: (i, 0)),
                pl.BlockSpec((K, D), lambda i: b_off, pipeline_mode=pl.Buffered(1)), row, vec]
    if head:
        return pl.pallas_call(
            kern, grid=(M // tm,), in_specs=in_specs + [row], out_specs=[row, vec, vec],
            out_shape=[jax.ShapeDtypeStruct((M, D), F32), jax.ShapeDtypeStruct((1, D), F32),
                       jax.ShapeDtypeStruct((1, D), F32)],
            compiler_params=_cparams(1), name=name)(a, b, res, g, tgt)
    return pl.pallas_call(
        kern, grid=(M // tm,), in_specs=in_specs, out_specs=[row, row],
        out_shape=[jax.ShapeDtypeStruct((M, D), F32), jax.ShapeDtypeStruct((M, D), BF16)],
        compiler_params=_cparams(1), name=name)(a, b, res, g)


def _rms_fwd(x, g, name, deps=()):
    S = x.shape[0]
    tr = 512

    def kern(x_ref, g_ref, *rest):
        h_ref = rest[-1]
        xv = x_ref[...]
        r = lax.rsqrt(jnp.mean(xv * xv, axis=-1, keepdims=True) + RMS_EPS)
        h_ref[...] = (xv * r * g_ref[...]).astype(BF16)

    return pl.pallas_call(
        kern, grid=(S // tr,),
        in_specs=[pl.BlockSpec((tr, D), lambda i: (i, 0)), pl.BlockSpec((1, D), lambda i: (0, 0))]
        + [pl.BlockSpec(memory_space=pl.ANY)] * len(deps),
        out_specs=pl.BlockSpec((tr, D), lambda i: (i, 0)),
        out_shape=jax.ShapeDtypeStruct((S, D), BF16), compiler_params=_cparams(1), name=name)(x, g, *deps)


def _chunks_put(scr, val):
    for c in range(scr.shape[0]):
        scr[c] = val[:, c * 128:(c + 1) * 128]


def _chunks_get(scr):
    return jnp.concatenate([scr[c] for c in range(scr.shape[0])], axis=1)


def _chunks_rows(scr, r, n, dil):
    return jnp.concatenate([scr.at[c][pl.ds(r, n, stride=dil), :] for c in range(scr.shape[0])], axis=1)


def _chunks_add_rows(scr, val, r, n, dil, accumulate):
    for c in range(scr.shape[0]):
        rows = pl.ds(r, n, stride=dil)
        piece = val[:, c * 128:(c + 1) * 128]
        tile = scr.at[c]
        tile[rows, :] = tile[rows, :] + piece if accumulate else piece


def _rms_fwd_folded(x, g, name, deps=()):
    S = x.shape[0]
    tr = 512
    dils = DILS[1:]

    def kern(x_ref, g_ref, *rest):
        outs, scr = rest[len(deps):-1], rest[-1]
        xv = x_ref[...]
        r = lax.rsqrt(jnp.mean(xv * xv, axis=-1, keepdims=True) + RMS_EPS)
        h = (xv * r * g_ref[...]).astype(BF16)
        outs[0][...] = h
        _chunks_put(scr, h.astype(F32))
        for o_ref, dil in zip(outs[1:], dils):
            for res in range(dil):
                o_ref[res] = _chunks_rows(scr, res, tr // dil, dil).astype(BF16)

    return pl.pallas_call(
        kern, grid=(S // tr,),
        in_specs=[pl.BlockSpec((tr, D), lambda i: (i, 0)), pl.BlockSpec((1, D), lambda i: (0, 0))]
        + [pl.BlockSpec(memory_space=pl.ANY)] * len(deps),
        out_specs=[pl.BlockSpec((tr, D), lambda i: (i, 0))]
        + [pl.BlockSpec((dil, tr // dil, D), lambda i: (0, i, 0)) for dil in dils],
        out_shape=[jax.ShapeDtypeStruct((S, D), BF16)]
        + [jax.ShapeDtypeStruct((dil, S // dil, D), BF16) for dil in dils],
        scratch_shapes=[pltpu.VMEM((D // 128, tr, 128), F32)],
        compiler_params=_cparams(1), name=name)(x, g, *deps)


def _rms_bwd(dh, x, g, dres, name, folded=()):
    S = x.shape[0]
    tr = 512
    nf = len(folded)

    def kern(dh_ref, *rest):
        f_refs = rest[:nf]
        x_ref, g_ref, dres_ref, dx_ref, dg_ref = rest[nf:nf + 5]
        i = pl.program_id(0)
        xv = x_ref[...]
        if nf:
            acc_ref = rest[nf + 5]
            _chunks_put(acc_ref, dh_ref[...].astype(F32))
            for f_ref in f_refs:
                dil = f_ref.shape[0]
                for res in range(dil):
                    _chunks_add_rows(acc_ref, f_ref[res], res, tr // dil, dil, True)
            dhv = _chunks_get(acc_ref)
        else:
            dhv = dh_ref[...].astype(F32)
        r = lax.rsqrt(jnp.mean(xv * xv, axis=-1, keepdims=True) + RMS_EPS)
        xhat = xv * r
        gy = dhv * g_ref[...]
        dx_ref[...] = dres_ref[...] + r * (gy - xhat * jnp.mean(gy * xhat, axis=-1, keepdims=True))
        part = jnp.sum(dhv * xhat, axis=0, keepdims=True)

        @pl.when(i == 0)
        def _():
            dg_ref[...] = part

        @pl.when(i > 0)
        def _():
            dg_ref[...] += part

    row = pl.BlockSpec((tr, D), lambda i: (i, 0))
    vec = pl.BlockSpec((1, D), lambda i: (0, 0))
    fspecs = [pl.BlockSpec((f.shape[0], tr // f.shape[0], D), lambda i: (0, i, 0)) for f in folded]
    return pl.pallas_call(
        kern, grid=(S // tr,), in_specs=[row] + fspecs + [row, vec, row], out_specs=[row, vec],
        out_shape=[jax.ShapeDtypeStruct((S, D), F32), jax.ShapeDtypeStruct((1, D), F32)],
        scratch_shapes=[pltpu.VMEM((D // 128, tr, 128), F32)] if nf else [],
        compiler_params=_cparams(1), name=name)(dh, *folded, x, g, dres)


def _ffn_up(h, G, name):
    S = h.shape[0]
    tm = 512
    nj = DFF // HCH

    def kern(h_ref, w_ref, gu_ref, act_ref):
        gu = lax.dot_general(h_ref[...], w_ref[...], (((1,), (1,)), ((), ())), preferred_element_type=F32)
        gu_ref[...] = gu.astype(BF16)
        gate = gu[:, :HCH]
        up = gu[:, HCH:]
        act_ref[...] = (gate * jax.nn.sigmoid(gate) * up).astype(BF16)

    return pl.pallas_call(
        kern, grid=(nj, S // tm),
        in_specs=[pl.BlockSpec((tm, D), lambda j, i: (i, 0)),
                  pl.BlockSpec((2 * HCH, D), lambda j, i: (j, 0))],
        out_specs=[pl.BlockSpec((tm, 2 * HCH), lambda j, i: (i, j)),
                   pl.BlockSpec((tm, HCH), lambda j, i: (i, j))],
        out_shape=[jax.ShapeDtypeStruct((S, 2 * DFF), BF16), jax.ShapeDtypeStruct((S, DFF), BF16)],
        compiler_params=_cparams(2), name=name)(h, G)


def _ffn_down_bwd(dx, G, gu, name):
    S = dx.shape[0]
    tm = 512
    nj = DFF // HCH

    def kern(dx_ref, w_ref, gu_ref, o_ref):
        dact = lax.dot_general(dx_ref[...].astype(BF16), w_ref[...], (((1,), (1,)), ((), ())),
                               preferred_element_type=F32)
        gate = gu_ref[:, :HCH].astype(F32)
        up = gu_ref[:, HCH:].astype(F32)
        sig = jax.nn.sigmoid(gate)
        silu = gate * sig
        o_ref[:, :HCH] = (dact * up * (sig * (1.0 + gate * (1.0 - sig)))).astype(BF16)
        o_ref[:, HCH:] = (dact * silu).astype(BF16)

    return pl.pallas_call(
        kern, grid=(nj, S // tm),
        in_specs=[pl.BlockSpec((tm, D), lambda j, i: (i, 0)),
                  pl.BlockSpec((HCH, D), lambda j, i: (j, 0)),
                  pl.BlockSpec((tm, 2 * HCH), lambda j, i: (i, j))],
        out_specs=pl.BlockSpec((tm, 2 * HCH), lambda j, i: (i, j)),
        out_shape=jax.ShapeDtypeStruct((S, 2 * DFF), BF16),
        compiler_params=_cparams(2), name=name)(dx, G, gu)


def _trail(u, *, backward, name):
    S = u.shape[0]

    def kern(u_ref, o_ref):
        g = pl.program_id(0)
        for grp in range(POOL_G):
            @pl.when(g == grp)
            def _(grp=grp):
                uv = u_ref[...].astype(F32)
                row = lax.broadcasted_iota(jnp.int32, uv.shape, 0)
                cnt = jnp.minimum(row + 1, 2 << grp).astype(F32)
                s = uv / cnt if backward else uv
                for k in (1, 2, 4, 8)[:grp + 1]:
                    if backward:
                        sh = jnp.where(row < S - k, pltpu.roll(s, S - k, 0), 0.0)
                    else:
                        sh = jnp.where(row >= k, pltpu.roll(s, k, 0), 0.0)
                    s = s + sh
                if backward:
                    o_ref[...] = (s - uv).astype(BF16)
                else:
                    o_ref[...] = (s / cnt - uv).astype(BF16)

    blk = pl.BlockSpec((S, PGD), lambda g: (0, g))
    return pl.pallas_call(
        kern, grid=(POOL_G,), in_specs=[blk], out_specs=blk,
        out_shape=jax.ShapeDtypeStruct((S, D), BF16), compiler_params=_cparams(1), name=name)(u)


def _pool_out(yd, G, scale, xres):
    S = yd.shape[0]
    tm = min(S, 4096)

    def kern(y_ref, w_ref, s_ref, x_ref, o_ref):
        z = jnp.dot(y_ref[...], w_ref[...], preferred_element_type=F32)
        o_ref[...] = x_ref[...] + z * s_ref[...]

    tile = pl.BlockSpec((tm, PGD), lambda i, g: (i, g))
    return pl.pallas_call(
        kern, grid=(S // tm, POOL_G),
        in_specs=[tile, pl.BlockSpec((PGD, PGD), lambda i, g: (0, g)),
                  pl.BlockSpec((1, PGD), lambda i, g: (0, g)), tile],
        out_specs=tile, out_shape=jax.ShapeDtypeStruct((S, D), F32),
        compiler_params=_cparams(2), name="pool_out")(yd, G, scale, xres)


def _pool_out_bwd(dz, yd, G, scale):
    S = yd.shape[0]
    tm = min(S, 4096)
    ni = S // tm

    def kern(dz_ref, y_ref, w_ref, s_ref, dy_ref, ds_ref, dw_ref, acc_ref):
        i = pl.program_id(1)
        dzv = dz_ref[...]
        yv = y_ref[...]
        wv = w_ref[...]
        zraw = jnp.dot(yv, wv, preferred_element_type=F32)
        dsp = jnp.sum(dzv * zraw, axis=0, keepdims=True)
        dzr = (dzv * s_ref[...]).astype(BF16)
        dy_ref[...] = lax.dot_general(dzr, wv, (((1,), (1,)), ((), ())), preferred_element_type=F32)
        dwp = lax.dot_general(yv, dzr, (((0,), (0,)), ((), ())), preferred_element_type=F32)

        @pl.when(i == 0)
        def _():
            ds_ref[...] = dsp
            acc_ref[...] = dwp

        @pl.when(i > 0)
        def _():
            ds_ref[...] += dsp
            acc_ref[...] += dwp

        @pl.when(i == ni - 1)
        def _():
            dw_ref[...] = acc_ref[...].astype(BF16)

    tile = pl.BlockSpec((tm, PGD), lambda g, i: (i, g))
    return pl.pallas_call(
        kern, grid=(POOL_G, ni),
        in_specs=[tile, tile, pl.BlockSpec((PGD, PGD), lambda g, i: (0, g)),
                  pl.BlockSpec((1, PGD), lambda g, i: (0, g))],
        out_specs=[tile, pl.BlockSpec((1, PGD), lambda g, i: (0, g)),
                   pl.BlockSpec((PGD, PGD), lambda g, i: (0, g))],
        out_shape=[jax.ShapeDtypeStruct((S, D), F32), jax.ShapeDtypeStruct((1, D), F32),
                   jax.ShapeDtypeStruct((PGD, D), BF16)],
        scratch_shapes=[pltpu.VMEM((PGD, PGD), F32)],
        compiler_params=_cparams(2), name="pool_out_bwd")(dz, yd, G, scale)


def _bias_table():
    qi = jnp.arange(QB)[:, None]
    ki = jnp.arange(2 * QB)[None, :]
    delta = QB + qi - ki
    inband = (delta >= 0) & (delta <= QB)
    n = NGROUPS * HEADS
    slopes = jnp.exp2(-8.0 * jnp.arange(1, n + 1, dtype=F32) / n).reshape(NGROUPS, HEADS)
    dil = jnp.asarray(DILS, F32)
    bias = -slopes[:, :, None, None] * (delta.astype(F32)[None, None] * dil[:, None, None, None])
    return jnp.where(inband[None, None], bias, NEG)


def _attn_fwd(qkv_f, bias, nb, name):
    S = qkv_f.shape[0]
    nblk = S // QB
    scale = HD ** -0.5

    def kern(q_ref, kc_ref, kp_ref, vc_ref, vp_ref, b_ref, o_ref, l_ref, s_scr, p_scr, r_scr):
        b = pl.program_id(0)
        has_prev = jnp.bitwise_and(b, nb - 1) != 0
        col = lax.broadcasted_iota(jnp.int32, (QB, 2 * QB), 1)
        dead = jnp.logical_and(col < QB, jnp.logical_not(has_prev))
        lane = lax.broadcasted_iota(jnp.int32, (QB, HD), 1)
        lse_all = jnp.zeros((QB, HD), F32)
        for h in range(HEADS):
            sl = slice(h * HD, (h + 1) * HD)
            kk = jnp.concatenate([kp_ref[:, sl], kc_ref[:, sl]], axis=0)
            s_scr[h] = lax.dot_general(q_ref[:, sl], kk, (((1,), (1,)), ((), ())), preferred_element_type=F32)
        for h in range(HEADS):
            s = s_scr[h] * scale + b_ref[h]
            s = jnp.where(dead, NEG, s)
            m = jnp.max(s, axis=-1, keepdims=True)
            p = jnp.exp(s - m)
            den = jnp.sum(p, axis=-1, keepdims=True)
            p_scr[h] = p.astype(BF16)
            r_scr[h] = jnp.broadcast_to(1.0 / den, (QB, HD))
            lse_all = jnp.where(lane == h, m + jnp.log(den), lse_all)
        for h in range(HEADS):
            sl = slice(h * HD, (h + 1) * HD)
            vv = jnp.concatenate([vp_ref[:, sl], vc_ref[:, sl]], axis=0)
            o = jnp.dot(p_scr[h], vv, preferred_element_type=F32) * r_scr[h]
            o_ref[:, sl] = o.astype(BF16)
        l_ref[...] = lse_all

    def blk(colblk, prev):
        if prev:
            return pl.BlockSpec((QB, D), lambda b: (jnp.maximum(b - 1, 0), colblk))
        return pl.BlockSpec((QB, D), lambda b: (b, colblk))

    return pl.pallas_call(
        kern, grid=(nblk,),
        in_specs=[blk(0, False), blk(1, False), blk(1, True), blk(2, False), blk(2, True),
                  pl.BlockSpec((HEADS, QB, 2 * QB), lambda b: (0, 0, 0))],
        out_specs=[pl.BlockSpec((QB, D), lambda b: (b, 0)), pl.BlockSpec((QB, HD), lambda b: (b, 0))],
        out_shape=[jax.ShapeDtypeStruct((S, D), BF16), jax.ShapeDtypeStruct((S, HD), F32)],
        scratch_shapes=[pltpu.VMEM((HEADS, QB, 2 * QB), F32), pltpu.VMEM((HEADS, QB, 2 * QB), BF16),
                        pltpu.VMEM((HEADS, QB, HD), F32)],
        compiler_params=_cparams(1), name=name)(qkv_f, qkv_f, qkv_f, qkv_f, qkv_f, bias)


def _natural(ref, scr, tm):
    dil = ref.shape[0]
    for res in range(dil):
        _chunks_add_rows(scr, ref[res].astype(F32), res, tm // dil, dil, False)
    return _chunks_get(scr)


def _attn_merge(os, lses):
    S = os[0].shape[0]
    tm = 512

    def kern(o0, o1, o2, l0, l1, l2, om_ref, lm_ref, ls1, ls2, os1, os2):
        la = l0[...]
        lb = _natural(l1, ls1, tm)
        lc = _natural(l2, ls2, tm)
        m = jnp.maximum(jnp.maximum(la, lb), lc)
        e0, e1, e2 = jnp.exp(la - m), jnp.exp(lb - m), jnp.exp(lc - m)
        tot = e0 + e1 + e2
        lm_ref[...] = m + jnp.log(tot)
        w0, w1, w2 = e0 / tot, e1 / tot, e2 / tot
        for res in range(o1.shape[0]):
            _chunks_add_rows(os1, o1[res].astype(F32), res, tm // o1.shape[0], o1.shape[0], False)
        for res in range(o2.shape[0]):
            _chunks_add_rows(os2, o2[res].astype(F32), res, tm // o2.shape[0], o2.shape[0], False)
        for h in range(HEADS):
            sl = slice(h * HD, (h + 1) * HD)
            acc = w0[:, h:h + 1] * o0[:, sl].astype(F32) + w1[:, h:h + 1] * os1[h] + w2[:, h:h + 1] * os2[h]
            om_ref[:, sl] = acc.astype(BF16)

    def spec(a, c):
        if a.ndim == 2:
            return pl.BlockSpec((tm, c), lambda i: (i, 0))
        return pl.BlockSpec((a.shape[0], tm // a.shape[0], c), lambda i: (0, i, 0))

    return pl.pallas_call(
        kern, grid=(S // tm,),
        in_specs=[spec(a, D) for a in os] + [spec(a, HD) for a in lses],
        out_specs=[pl.BlockSpec((tm, D), lambda i: (i, 0)), pl.BlockSpec((tm, HD), lambda i: (i, 0))],
        out_shape=[jax.ShapeDtypeStruct((S, D), BF16), jax.ShapeDtypeStruct((S, HD), F32)],
        scratch_shapes=[pltpu.VMEM((1, tm, HD), F32), pltpu.VMEM((1, tm, HD), F32),
                        pltpu.VMEM((HEADS, tm, HD), F32), pltpu.VMEM((HEADS, tm, HD), F32)],
        compiler_params=_cparams(1), name="attn_merge")(*os, *lses)


def _attn_bwd_prep(do, o, lse):
    S = o.shape[0]
    tm = 512
    dils = DILS[1:]

    def kern(do_ref, o_ref, l_ref, *rest):
        do_outs, l_outs, d_outs = rest[0:3], rest[3:5], rest[5:8]
        do_scr, l_scr, d_scr = rest[8:11]
        lane = lax.broadcasted_iota(jnp.int32, (tm, HD), 1)
        acc = jnp.zeros((tm, HD), F32)
        for h in range(HEADS):
            sl = slice(h * HD, (h + 1) * HD)
            prod = do_ref[:, sl] * o_ref[:, sl].astype(F32)
            acc = jnp.where(lane == h, jnp.sum(prod, axis=-1, keepdims=True), acc)
        d_scr[0] = acc
        l_scr[0] = l_ref[...]
        _chunks_put(do_scr, do_ref[...])
        do_outs[0][...] = do_ref[...].astype(BF16)
        d_outs[0][...] = acc
        for j, dil in enumerate(dils):
            for res in range(dil):
                n = tm // dil
                do_outs[1 + j][res] = _chunks_rows(do_scr, res, n, dil).astype(BF16)
                l_outs[j][res] = _chunks_rows(l_scr, res, n, dil)
                d_outs[1 + j][res] = _chunks_rows(d_scr, res, n, dil)

    def nat(c):
        return pl.BlockSpec((tm, c), lambda i: (i, 0))

    def fol(dil, c):
        return pl.BlockSpec((dil, tm // dil, c), lambda i: (0, i, 0))

    def shapes(c, dt, with_natural):
        first = [jax.ShapeDtypeStruct((S, c), dt)] if with_natural else []
        return first + [jax.ShapeDtypeStruct((dil, S // dil, c), dt) for dil in dils]

    outs = pl.pallas_call(
        kern, grid=(S // tm,), in_specs=[nat(D), nat(D), nat(HD)],
        out_specs=[nat(D)] + [fol(dil, D) for dil in dils] + [fol(dil, HD) for dil in dils]
        + [nat(HD)] + [fol(dil, HD) for dil in dils],
        out_shape=shapes(D, BF16, True) + shapes(HD, F32, False) + shapes(HD, F32, True),
        scratch_shapes=[pltpu.VMEM((HEADS, tm, HD), F32), pltpu.VMEM((1, tm, HD), F32), pltpu.VMEM((1, tm, HD), F32)],
        compiler_params=_cparams(1), name="attn_bwd_prep")(do, o, lse)
    return outs[0:3], [lse] + list(outs[3:5]), outs[5:8]


def _attn_bwd(qkv_f, do_f, lse_f, delta_f, bias, nb, name):
    S = qkv_f.shape[0]
    nblk = S // QB
    scale = HD ** -0.5

    def kern(q_ref, kc_ref, kp_ref, vc_ref, vp_ref, do_ref, l_ref, d_ref, b_ref, out_ref, dq_c, dk_c, dv_c,
             s_scr, dp_scr, ds_scr, p_scr):
        b = pl.program_id(0)

        @pl.when(b == 0)
        def _():
            dq_c[...] = jnp.zeros_like(dq_c)
            dk_c[...] = jnp.zeros_like(dk_c)
            dv_c[...] = jnp.zeros_like(dv_c)

        @pl.when(b == nblk)
        def _():
            out_ref[:, 0:D] = dq_c[...].astype(BF16)
            out_ref[:, D:2 * D] = dk_c[...].astype(BF16)
            out_ref[:, 2 * D:3 * D] = dv_c[...].astype(BF16)

        @pl.when(b < nblk)
        def _():
            has_prev = jnp.bitwise_and(b, nb - 1) != 0
            col = lax.broadcasted_iota(jnp.int32, (QB, 2 * QB), 1)
            dead = jnp.logical_and(col < QB, jnp.logical_not(has_prev))
            out_ref[:, 0:D] = dq_c[...].astype(BF16)
            lv = l_ref[...]
            dv_ = d_ref[...]
            for h in range(HEADS):
                sl = slice(h * HD, (h + 1) * HD)
                kk = jnp.concatenate([kp_ref[:, sl], kc_ref[:, sl]], axis=0)
                vv = jnp.concatenate([vp_ref[:, sl], vc_ref[:, sl]], axis=0)
                s_scr[h] = lax.dot_general(q_ref[:, sl], kk, (((1,), (1,)), ((), ())), preferred_element_type=F32)
                dp_scr[h] = lax.dot_general(do_ref[:, sl], vv, (((1,), (1,)), ((), ())),
                                            preferred_element_type=F32)
            for h in range(HEADS):
                s = s_scr[h] * scale + b_ref[h]
                s = jnp.where(dead, NEG, s)
                p = jnp.exp(s - lv[:, h:h + 1])
                ds_scr[h] = (p * (dp_scr[h] - dv_[:, h:h + 1]) * scale).astype(BF16)
                p_scr[h] = p.astype(BF16)
            for h in range(HEADS):
                sl = slice(h * HD, (h + 1) * HD)
                kk = jnp.concatenate([kp_ref[:, sl], kc_ref[:, sl]], axis=0)
                ds = ds_scr[h]
                dq_c[:, sl] = jnp.dot(ds, kk, preferred_element_type=F32)
                dkk = lax.dot_general(ds, q_ref[:, sl], (((0,), (0,)), ((), ())), preferred_element_type=F32)
                dvv = lax.dot_general(p_scr[h], do_ref[:, sl], (((0,), (0,)), ((), ())),
                                      preferred_element_type=F32)
                out_ref[:, D + h * HD:D + (h + 1) * HD] = (dk_c[:, sl] + dkk[:QB]).astype(BF16)
                out_ref[:, 2 * D + h * HD:2 * D + (h + 1) * HD] = (dv_c[:, sl] + dvv[:QB]).astype(BF16)
                dk_c[:, sl] = dkk[QB:]
                dv_c[:, sl] = dvv[QB:]

    last = nblk - 1

    def blk(colblk, prev):
        if prev:
            return pl.BlockSpec((QB, D), lambda b: (jnp.maximum(jnp.minimum(b, last) - 1, 0), colblk))
        return pl.BlockSpec((QB, D), lambda b: (jnp.minimum(b, last), colblk))

    stat = pl.BlockSpec((QB, HD), lambda b: (jnp.minimum(b, last), 0))
    return pl.pallas_call(
        kern, grid=(nblk + 1,),
        in_specs=[blk(0, False), blk(1, False), blk(1, True), blk(2, False), blk(2, True),
                  pl.BlockSpec((QB, D), lambda b: (jnp.minimum(b, last), 0)), stat, stat,
                  pl.BlockSpec((HEADS, QB, 2 * QB), lambda b: (0, 0, 0))],
        out_specs=pl.BlockSpec((QB, 3 * D), lambda b: (jnp.maximum(b - 1, 0), 0)),
        out_shape=jax.ShapeDtypeStruct((S, 3 * D), BF16),
        scratch_shapes=[pltpu.VMEM((QB, D), F32), pltpu.VMEM((QB, D), F32), pltpu.VMEM((QB, D), F32),
                        pltpu.VMEM((HEADS, QB, 2 * QB), F32), pltpu.VMEM((HEADS, QB, 2 * QB), F32),
                        pltpu.VMEM((HEADS, QB, 2 * QB), BF16), pltpu.VMEM((HEADS, QB, 2 * QB), BF16)],
        compiler_params=_cparams(1), name=name)(qkv_f, qkv_f, qkv_f, qkv_f, qkv_f, do_f, lse_f, delta_f, bias)


def _local_step(x, tgt, comm, attn_norm, ffn_norm, final_norm, pool_norm, pool_scale):
    S = x.shape[0]
    bias = _bias_table()
    g_attn = attn_norm.reshape(1, D)
    g_f0 = ffn_norm[0:1]
    g_f1 = ffn_norm[1:2]
    g_fin = final_norm.reshape(1, D)
    W = {}

    def ffn_fwd(xin, h, l, next_gain, target=None):
        gu, act = _ffn_up(h, W[f"gu{l}"], f"ffn_up{l}")
        return gu, act, _mm_res_norm(act, W[f"d{l}"], xin, next_gain, K=DFF, tm=512, tgt=target,
                                     name=f"ffn_down{l}")

    def ffn_bwd(dxo, xin, gain, h, gu, act, l, rs_group):
        dgu = _ffn_down_bwd(dxo, W[f"d{l}"], gu, f"ffn_down_bwd{l}")
        gw_d = _mm(act, dxo, mode="tn", M=DFF, N=D, K=S, tm=HCH, tn=D, tk=2048, out_dtype=BF16, name=f"gw_d{l}")
        gw_gu = _mm(dgu, h, mode="tn", M=2 * DFF, N=D, K=S, tm=HCH, tn=D, tk=2048, out_dtype=BF16, name=f"gw_gu{l}")
        token = comm.send_grads(rs_group, {f"d{l}": gw_d, f"gu{l}": gw_gu})
        return _mm_rms_bwd(dgu, W[f"gu{l}"], xin, gain, dxo, mode="nn", M=S, K=2 * DFF, tm=512, deps=(token,),
                           name=f"ffn_up_bwd{l}")

    nbs = [S // QB // dil for dil in DILS]
    hf = _rms_fwd_folded(x, g_attn, "rms_attn", deps=comm.ag_tokens)
    hf = [h.reshape(S, D) for h in hf]
    W.update(comm.weights(0, hf[0]))
    qkv_f, o_f, lse_f = [], [], []
    for g, dil in enumerate(DILS):
        qkv_f.append(_mm(hf[g], W["qkv"], mode="nt", M=S, N=3 * D, K=D, tm=2048, tn=1024, tk=D, out_dtype=BF16,
                         b_off=(3 * g, 0), name=f"qkv_proj{g}"))
        og, lg = _attn_fwd(qkv_f[g], bias[g], nbs[g], f"attn_fwd{g}")
        o_f.append(og if dil == 1 else og.reshape(dil, S // dil, D))
        lse_f.append(lg if dil == 1 else lg.reshape(dil, S // dil, HD))
    o, lse = _attn_merge(o_f, lse_f)
    W.update(comm.weights(1, o))
    x1, h1 = _mm_res_norm(o, W["wo"], x, g_f0, K=D, tm=1024, name="attn_out")
    gu0, act0, (x2, h2) = ffn_fwd(x1, h1, 0, pool_norm)

    W.update(comm.weights(2, x2))
    u = _mm(h2, W["wpi"], mode="nn", M=S, N=D, K=D, tm=1024, tn=D, tk=D, out_dtype=F32, name="pool_in")
    yd = _trail(u, backward=False, name="trail_fwd")
    x3 = _pool_out(yd, W["pg"], pool_scale, x2)
    h3 = _rms_fwd(x3, g_f1, "rms_ffn1")
    gu1, act1, (dx4, d_fin, lossvec) = ffn_fwd(x3, h3, 1, g_fin, target=tgt)

    dx3, d_f1 = ffn_bwd(dx4, x3, g_f1, h3, gu1, act1, 1, 0)
    dyd, d_scale, gw_pg = _pool_out_bwd(dx3, yd, W["pg"], pool_scale)
    du = _trail(dyd, backward=True, name="trail_bwd")
    gw_pi = _mm(h2, du, mode="tn", M=D, N=D, K=S, tm=D, tn=D, tk=2048, out_dtype=BF16, name="gw_pi")
    token = comm.send_grads(1, {"pg": gw_pg, "wpi": gw_pi})
    dx2, d_pool = _mm_rms_bwd(du, W["wpi"], x2, pool_norm, dx3, mode="nt", M=S, K=D, tm=1024, deps=(token,),
                              name="pool_in_bwd")
    dx1, d_f0 = ffn_bwd(dx2, x1, g_f0, h1, gu0, act0, 0, 2)

    gw_o = _mm(o, dx1, mode="tn", M=D, N=D, K=S, tm=D, tn=D, tk=2048, out_dtype=BF16, name="gw_o")
    do = _mm(dx1, W["wo"], mode="nt", M=S, N=D, K=D, tm=1024, tn=D, tk=D, out_dtype=F32, name="attn_out_bwd")
    do_f, lse_ff, delta_f = _attn_bwd_prep(do, o, lse)
    dqkv_f, gw_qkv = [], None
    for g in range(NGROUPS):
        dqkv_f.append(_attn_bwd(qkv_f[g], do_f[g].reshape(S, D), lse_ff[g].reshape(S, HD),
                                delta_f[g].reshape(S, HD), bias[g], nbs[g], f"attn_bwd{g}"))
        gw_qkv = _mm(dqkv_f[g], hf[g], mode="tn", M=3 * D, N=D, K=S, tm=1024, tn=D, tk=2048, out_dtype=BF16,
                     out_rows=NGROUPS * 3 * D, out_off=3 * g, out_prev=gw_qkv, name=f"gw_qkv{g}")
    token = comm.send_grads(3, {"wo": gw_o, "qkv": gw_qkv})
    dh0_f = [_mm(dqkv_f[g], W["qkv"], mode="nn", M=S, N=D, K=3 * D, tm=1024, tn=D, tk=3 * D, out_dtype=F32,
                 b_off=(g, 0), deps=(token,), name=f"qkv_proj_bwd{g}") for g in range(NGROUPS)]
    folded = [dh0_f[g].reshape(dil, S // dil, D) for g, dil in enumerate(DILS) if dil > 1]
    grad_x, d_attn = _rms_bwd(dh0_f[0], x, g_attn, dx1, "rms_attn_bwd", folded=folded)

    vec = jnp.concatenate([d_attn, d_f0, d_f1, d_fin, d_pool, d_scale, lossvec, jnp.zeros((1, D), F32)], axis=0)
    return grad_x, vec


def _mesh_pos():
    x, y, c = lax.axis_index("x"), lax.axis_index("y"), lax.axis_index("c")
    return x, y, c, 4 * x + 2 * y + c


def _peer(x, y, c, k):
    kx, ky, kc = (k >> 2) & 1, (k >> 1) & 1, k & 1
    px = 1 - x if kx else x
    py = 1 - y if ky else y
    pc = 1 - c if kc else c
    return (px, py, pc), 4 * px + 2 * py + pc


ANY = pl.BlockSpec(memory_space=pl.ANY)


HBM = pl.BlockSpec(memory_space=pltpu.HBM)
SEMS = pl.BlockSpec(memory_space=pltpu.SEMAPHORE)
EFFECT = pltpu.SideEffectType.DATAFLOW_SIDE_EFFECTING
NPEER = NDEV - 1

AG_GROUPS = (("qkv",), ("wo", "gu0", "d0"), ("wpi", "pg", "gu1", "d1"))
AG_ORDER = tuple(n for grp in AG_GROUPS for n in grp)
RS_GROUPS = (("d1", "gu1"), ("pg", "wpi"), ("d0", "gu0"), ("wo", "qkv"))


def _hbm(a):
    return pltpu.with_memory_space_constraint(a, pltpu.HBM)


def _remote(src, dst, send, recv, peer):
    return pltpu.make_async_remote_copy(src_ref=src, dst_ref=dst, send_sem=send, recv_sem=recv, device_id=peer,
                                        device_id_type=pl.DeviceIdType.MESH)


def _bcast_all(v, name, deps=()):
    W = v.shape[1]
    nd = len(deps)

    def kern(v_ref, *rest):
        o_ref, send, recv, lsem = rest[nd:]
        x, y, c, me = _mesh_pos()
        own = pltpu.make_async_copy(v_ref, o_ref.at[me], lsem)
        own.start()
        cps = [_remote(v_ref, o_ref.at[me], send.at[k - 1], recv.at[k - 1], _peer(x, y, c, k)[0])
               for k in range(1, NDEV)]
        for cp in cps:
            cp.start()
        for cp in cps:
            cp.wait_recv()
            cp.wait_send()
        own.wait()

    return pl.pallas_call(
        kern, in_specs=[ANY] * (1 + nd), out_specs=ANY, out_shape=jax.ShapeDtypeStruct((NDEV, 8, W), F32),
        scratch_shapes=[pltpu.SemaphoreType.DMA((NPEER,)), pltpu.SemaphoreType.DMA((NPEER,)),
                        pltpu.SemaphoreType.DMA(())],
        name=name)(v, *deps)


ALL_KS = tuple(range(1, NDEV))
AG_KS1 = (1, 2, 4, 6)
AG_KS2 = (2, 4, 6)


def _split_start(srcs, src_of, lands, copy_refs, name, deps=(), ks=ALL_KS, to=None):
    ns, n, nd, nk = len(srcs), len(lands), len(deps), len(ks)

    def body(*refs):
        ins, land = refs[:ns], refs[ns:ns + n]
        send, recv = refs[ns + n + nd], refs[ns + n + nd + 1]
        token = refs[-1]
        x, y, c, me = _mesh_pos()
        for j in range(n):
            for i, k in enumerate(ks):
                _, pid = _peer(x, y, c, k)
                dest, _ = _peer(x, y, c, k if to is None else to)
                src, dst = copy_refs(j, (land[j] if src_of[j] is None else ins[src_of[j]]), land[j], me, pid)
                _remote(src, dst, send.at[j * nk + i], recv.at[j * nk + i], dest).start()
        token[...] = jnp.zeros_like(token)

    outs = pl.pallas_call(
        body, name=name,
        out_shape=(pltpu.SemaphoreType.DMA((n * nk,)), pltpu.SemaphoreType.DMA((n * nk,)))
        + tuple(pltpu.HBM(a.shape, a.dtype) for a in srcs) + tuple(pltpu.HBM(a.shape, a.dtype) for a in lands)
        + (jax.ShapeDtypeStruct((8, 128), F32),),
        in_specs=(HBM,) * (ns + n) + (ANY,) * nd,
        out_specs=(SEMS, SEMS) + (HBM,) * (ns + n) + (pl.BlockSpec(memory_space=pltpu.VMEM),),
        input_output_aliases={i: 2 + i for i in range(ns + n)},
        compiler_params=pltpu.CompilerParams(has_side_effects=EFFECT),
    )(*[_hbm(a) for a in srcs], *[_hbm(a) for a in lands], *deps)
    return outs[0], outs[1], list(outs[2:2 + ns]), list(outs[2 + ns:2 + ns + n]), outs[-1]


def _split_wait(srcs, src_of, lands, send, recv, sem_rows, wait_refs, after, name, ks=ALL_KS):
    ns, n, nk = len(srcs), len(lands), len(ks)

    def body(*refs):
        ins, land = refs[:ns], refs[ns:ns + n]
        send_ref, recv_ref = refs[ns + n], refs[ns + n + 1]
        x, y, c, me = _mesh_pos()
        for j in range(n):
            for i, k in enumerate(ks):
                peer, _ = _peer(x, y, c, k)
                src, dst = wait_refs(j, (land[j] if src_of[j] is None else ins[src_of[j]]), land[j])
                sem = sem_rows[j] * nk + i
                cp = _remote(src, dst, send_ref.at[sem], recv_ref.at[sem], peer)
                cp.wait_send()
                cp.wait_recv()

    outs = pl.pallas_call(
        body, name=name,
        out_shape=tuple(pltpu.HBM(a.shape, a.dtype) for a in srcs) + tuple(pltpu.HBM(a.shape, a.dtype) for a in lands),
        in_specs=(HBM,) * (ns + n) + (SEMS, SEMS, ANY),
        out_specs=(HBM,) * (ns + n),
        input_output_aliases={i: i for i in range(ns + n)},
        compiler_params=pltpu.CompilerParams(has_side_effects=EFFECT),
    )(*srcs, *lands, send, recv, after)
    return list(outs[:ns]), list(outs[ns:])


class _Comm:
    def __init__(self, shards, me, deps=()):
        self.me = me
        self.ag_land, self.ag_sems, self.ag_tokens = {}, {}, ()
        self.rs = []
        for part, names in enumerate((AG_GROUPS[0], AG_ORDER[len(AG_GROUPS[0]):])):
            rows = [SEC_ROWS[n] for n in names]
            lands = [lax.dynamic_update_slice(lax.empty((NDEV * r, D), BF16), shards[n], (_shard_pos(n, me), 0))
                     for n, r in zip(names, rows)]

            def copy_refs(j, src, land, me, pid, names=names, rows=rows):
                own = land.at[pl.ds(pl.multiple_of(_shard_pos(names[j], me), 16), rows[j])]
                return own, own

            send, recv, _, lands, token = _split_start([], [None] * len(names), lands, copy_refs, f"ag_start{part}",
                                                       deps=deps, ks=AG_KS1)
            deps = (token,)
            self.ag_tokens += (token,)
            for j, n in enumerate(names):
                self.ag_land[n] = lands[j]
                self.ag_sems[n] = (send, recv, j)

    def weights(self, group, after):
        names = AG_GROUPS[group]
        send, recv = self.ag_sems[names[0]][:2]
        idx = [self.ag_sems[n][2] for n in names]
        rows = [SEC_ROWS[n] for n in names]
        none = [None] * len(names)

        def wait_refs(j, src, land):
            return land.at[pl.ds(0, rows[j])], land.at[pl.ds(0, rows[j])]

        _, lands = _split_wait([], none, [self.ag_land[n] for n in names], send, recv, idx,
                               wait_refs, after, f"ag_wait{group}", ks=AG_KS1)

        def copy_refs(j, src, land, me, pid):
            theirs = land.at[pl.ds(pl.multiple_of(_shard_pos(names[j], pid), 16), rows[j])]
            return theirs, theirs

        send, recv, _, lands, token = _split_start([], none, lands, copy_refs, f"ag_pass{group}", ks=AG_KS2, to=1)
        _, lands = _split_wait([], none, lands, send, recv, list(range(len(names))), wait_refs, token,
                               f"ag_pass_wait{group}", ks=AG_KS2)
        return dict(zip(names, lands))

    def send_grads(self, group, gws):
        names = RS_GROUPS[group]
        rows = [SEC_ROWS[n] for n in names]
        grads = [gws[n] for n in names]
        me = self.me
        lands = [lax.dynamic_update_slice(
            lax.empty((NDEV, r, D), BF16),
            lax.dynamic_slice(g, (_shard_pos(n, me), 0), (r, D))[None], (me, 0, 0))
            for n, r, g in zip(names, rows, grads)]

        def copy_refs(j, src, land, me, pid):
            return src.at[pl.ds(pl.multiple_of(_shard_pos(names[j], pid), 16), rows[j])], land.at[me]

        send, recv, srcs, lands, token = _split_start(grads, list(range(len(names))), lands, copy_refs,
                                                      f"rs_start{group}")
        self.rs.append((names, rows, send, recv, srcs, lands))
        return token

    def received(self, group, after):
        names, rows, send, recv, srcs, lands = self.rs[group]

        def wait_refs(j, src, land):
            return src.at[pl.ds(0, rows[j])], land.at[0]

        _, lands = _split_wait(srcs, list(range(len(names))), lands, send, recv, list(range(len(names))), wait_refs,
                               after, f"rs_wait{group}")
        return dict(zip(names, lands))


def _sum_contributions(r_ref):
    g = r_ref[0].astype(F32)
    for dev in range(1, NDEV):
        g = g + r_ref[dev].astype(F32)
    return g


def _adam_math(g, w, m, v):
    c1 = 1.0 / (1.0 - ADAM_B1 ** ADAM_STEP)
    c2 = 1.0 / (1.0 - ADAM_B2 ** ADAM_STEP)
    mn = ADAM_B1 * m + (1.0 - ADAM_B1) * g
    vn = ADAM_B2 * v + (1.0 - ADAM_B2) * (g * g)
    return -ADAM_LR * ((mn * c1) / (jnp.sqrt(vn * c2) + ADAM_EPS) + ADAM_WD * w), mn, vn


def _adamw(R, w, m, v, *, tr, name, layer=None, prev=None):
    rows, C = w.shape[-2:]
    nprev = 0 if prev is None else 4

    def kern(r_ref, w_ref, m_ref, v_ref, *rest):
        g_out, d_out, m_out, v_out = rest[nprev:]
        g = _sum_contributions(r_ref)
        g_out[...] = g
        d_out[...], m_out[...], v_out[...] = _adam_math(g, w_ref[...], m_ref[...], v_ref[...])

    if layer is None:
        tile = pl.BlockSpec((tr, C), lambda i: (i, 0))
    else:
        tile = pl.BlockSpec((None, tr, C), lambda i: (layer, i, 0))
    shp = jax.ShapeDtypeStruct(w.shape, F32)
    return pl.pallas_call(
        kern, grid=(rows // tr,),
        in_specs=[pl.BlockSpec((NDEV, tr, C), lambda i: (0, i, 0)), tile, tile, tile]
        + [pl.BlockSpec(memory_space=pl.ANY)] * nprev,
        out_specs=[tile] * 4, out_shape=[shp] * 4,
        input_output_aliases={4 + k: k for k in range(nprev)},
        compiler_params=_cparams(1), name=name)(R, w, m, v, *(prev or ()))


def _adamw_pool_group(R, w, m, v):
    rows = SEC_ROWS["pg"]

    def kern(r_ref, w_ref, m_ref, v_ref, g_out, d_out, m_out, v_out):
        g = _sum_contributions(r_ref)
        g_out[0] = g
        d_out[0], m_out[0], v_out[0] = _adam_math(g, w_ref[0], m_ref[0], v_ref[0])

    blk = pl.BlockSpec((1, rows, PGD), lambda i: (i, 0, 0))
    shp = jax.ShapeDtypeStruct((POOL_G, rows, PGD), F32)
    return pl.pallas_call(
        kern, grid=(POOL_G,),
        in_specs=[pl.BlockSpec((NDEV, rows, PGD), lambda i: (0, 0, i)), blk, blk, blk],
        out_specs=[blk] * 4, out_shape=[shp] * 4, compiler_params=_cparams(1), name="adamw_pg")(R, w, m, v)


def _grad_sum_t(R, name):
    rows = R.shape[1]
    tr = 128

    def kern(r_ref, o_ref):
        o_ref[...] = _sum_contributions(r_ref).T

    return pl.pallas_call(
        kern, grid=(rows // tr,), in_specs=[pl.BlockSpec((NDEV, tr, D), lambda i: (0, i, 0))],
        out_specs=pl.BlockSpec((D, tr), lambda i: (0, i)), out_shape=jax.ShapeDtypeStruct((D, rows), F32),
        compiler_params=_cparams(1), name=name)(R)


def _adam_plain(g, w, m, v, *, tr, name):
    rows, C = w.shape

    def kern(g_ref, w_ref, m_ref, v_ref, d_out, m_out, v_out):
        d_out[...], m_out[...], v_out[...] = _adam_math(g_ref[...], w_ref[...], m_ref[...], v_ref[...])

    tile = pl.BlockSpec((tr, C), lambda i: (i, 0))
    shp = jax.ShapeDtypeStruct((rows, C), F32)
    return pl.pallas_call(
        kern, grid=(rows // tr,), in_specs=[tile] * 4, out_specs=[tile] * 3, out_shape=[shp] * 3,
        compiler_params=_cparams(1), name=name)(g, w, m, v)


def _pack_sections(w_qkv, w_attn_out, w_pool_in, w_pool_group, w_ffn_gate_up, w_ffn_down):
    pg = w_pool_group[0].transpose(1, 0, 2).reshape(SEC_ROWS["pg"], D)
    return {"qkv": w_qkv[0].T, "wo": w_attn_out[0], "wpi": w_pool_in[0], "gu0": w_ffn_gate_up[0].T,
            "gu1": w_ffn_gate_up[1].T, "d0": w_ffn_down[0], "d1": w_ffn_down[1], "pg": pg}


def _vec_pack(attn_norm, ffn_norm, final_norm, pool_norm_sh, pool_scale_sh, me):
    def place(sh):
        return lax.dynamic_update_slice(jnp.zeros((1, D), F32), sh, (0, me * 128))
    return jnp.concatenate([attn_norm, ffn_norm, final_norm.reshape(1, D), place(pool_norm_sh),
                            place(pool_scale_sh), jnp.zeros((2, D), F32)], axis=0)


def _vec_unpack(p, me):
    def take(r):
        return lax.dynamic_slice(p[r:r + 1], (0, me * 128), (1, 128))
    return p[0:1], p[1:3], p[3], take(4), take(5)


def kernel(x, attn_norm, w_qkv, w_attn_out, pool_norm, w_pool_in, w_pool_group, pool_scale, ffn_norm, w_ffn_gate_up, w_ffn_down, final_norm, loss_target, m_attn_norm, m_w_qkv, m_w_attn_out, m_pool_norm, m_w_pool_in, m_w_pool_group, m_pool_scale, m_ffn_norm, m_w_ffn_gate_up, m_w_ffn_down, m_final_norm, v_attn_norm, v_w_qkv, v_w_attn_out, v_pool_norm, v_w_pool_in, v_w_pool_group, v_pool_scale, v_ffn_norm, v_w_ffn_gate_up, v_w_ffn_down, v_final_norm):
    me = 4 * lax.axis_index("x") + 2 * lax.axis_index("y") + lax.axis_index("c")

    pw = _pack_sections(w_qkv, w_attn_out, w_pool_in, w_pool_group, w_ffn_gate_up, w_ffn_down)
    vsh = jnp.concatenate([pool_norm, pool_scale, jnp.zeros((6, 128), F32)], axis=0)

    vg = _bcast_all(vsh, "gather_pool_vectors")
    comm = _Comm({n: pw[n].astype(BF16) for n, _ in SECTIONS}, me, deps=(vg,))
    pool_norm_full = vg[:, 0, :].reshape(1, D)
    pool_scale_full = vg[:, 1, :].reshape(1, D)

    grad_x, vec = _local_step(x[0], loss_target[0], comm, attn_norm, ffn_norm, final_norm,
                              pool_norm_full, pool_scale_full)

    vw = _vec_pack(attn_norm, ffn_norm, final_norm, pool_norm, pool_scale, me)
    vm = _vec_pack(m_attn_norm, m_ffn_norm, m_final_norm, m_pool_norm, m_pool_scale, me)
    vv = _vec_pack(v_attn_norm, v_ffn_norm, v_final_norm, v_pool_norm, v_pool_scale, me)

    gu_t = [jnp.swapaxes(a, 1, 2) for a in (w_ffn_gate_up, m_w_ffn_gate_up, v_w_ffn_gate_up)]
    res = {}
    gu_res, d_res = None, None
    vec_out = None
    after = grad_x
    for group in range(len(RS_GROUPS)):
        if group == len(RS_GROUPS) - 1:
            VR = _bcast_all(vec, "exchange_vector_grads", deps=(after,))
            vec_out = _adamw(VR, vw, vm, vv, tr=8, name="adamw_vec")
            after = vec_out[0]
        for n, R in comm.received(group, after).items():
            if n in ("d0", "d1"):
                d_res = _adamw(R, w_ffn_down, m_w_ffn_down, v_w_ffn_down, tr=352, name=f"adamw_{n}",
                               layer=int(n[1]), prev=d_res)
                after = d_res[0]
            elif n in ("gu0", "gu1"):
                gu_res = _adamw(R, *gu_t, tr=352, name=f"adamw_{n}", layer=int(n[2]), prev=gu_res)
                after = gu_res[0]
            elif n == "pg":
                out = _adamw_pool_group(R, w_pool_group[0], m_w_pool_group[0], v_w_pool_group[0])
                res["pg"] = tuple(a[None] for a in out)
                after = out[0]
            elif n in ("wo", "wpi"):
                w, m, v = ((w_attn_out, m_w_attn_out, v_w_attn_out) if n == "wo"
                           else (w_pool_in, m_w_pool_in, v_w_pool_in))
                res[n] = _adamw(R, w[0], m[0], v[0], tr=128, name=f"adamw_{n}")
                res[n] = tuple(a[None] for a in res[n])
                after = res[n][0]
            else:
                g = _grad_sum_t(R, "grad_sum_qkv")
                out = _adam_plain(g, w_qkv[0], m_w_qkv[0], v_w_qkv[0], tr=256, name="adamw_qkv")
                res["qkv"] = tuple(a[None] for a in (g,) + tuple(out))
                after = out[0]
    res["gu"] = tuple(jnp.swapaxes(a, 1, 2) for a in gu_res)
    res["d"] = tuple(d_res)

    outs = []
    for kind in range(4):
        an, fn, fin, pn, ps = _vec_unpack(vec_out[kind], me)
        outs.append((an, res["qkv"][kind], res["wo"][kind], pn, res["wpi"][kind], res["pg"][kind], ps, fn,
                     res["gu"][kind], res["d"][kind], fin))
    loss = 0.5 * jnp.sum(vec_out[0][6]) / D
    return (loss, grad_x[None]) + outs[0] + outs[1] + outs[2] + outs[3]
```

```python
import jax
import jax.numpy as jnp
from jax import lax
from jax.experimental import pallas as pl
from jax.experimental.pallas import tpu as pltpu

F32 = jnp.float32
BF16 = jnp.bfloat16

D = 1024
NDEV = 8
HEADS = 8
HD = 128
QB = 128
NGROUPS = 3
DILS = (1, 4, 16)
DFF = 2816
HCH = 1408
POOL_G = 4
PGD = 256
RMS_EPS = 1e-6
NEG = -1e30

ADAM_LR = 0.001
ADAM_B1 = 0.9
ADAM_B2 = 0.999
ADAM_EPS = 1e-08
ADAM_WD = 0.01
ADAM_STEP = 10

VMEM_LIMIT = 52 * 1024 * 1024

SECTIONS = (("qkv", 1152), ("wo", 128), ("wpi", 128), ("gu0", 704), ("gu1", 704),
            ("d0", 352), ("d1", 352), ("pg", 32))
LOC_OFF = {}
GLB_OFF = {}
_o = 0
for _n, _r in SECTIONS:
    LOC_OFF[_n] = _o
    GLB_OFF[_n] = _o * NDEV
    _o += _r
PACK_ROWS = _o
GLB_ROWS = PACK_ROWS * NDEV
SEC_ROWS = dict(SECTIONS)


def _cparams(n_grid):
    return pltpu.CompilerParams(dimension_semantics=("arbitrary",) * n_grid, vmem_limit_bytes=VMEM_LIMIT)


def _shard_pos(name, dev):
    n = SEC_ROWS[name]
    if name in ("gu0", "gu1"):
        return ((dev % 4) // 2) * (2 * HCH) + (dev // 4) * HCH + (dev % 2) * n
    return dev * n


def _mm(a, b, *, mode, M, N, K, tm, tn, tk, out_dtype, name, a_off=(0, 0), b_off=(0, 0), res=None,
        out_rows=None, out_off=0, out_prev=None, deps=()):
    nm, nn, nk = M // tm, N // tn, K // tk
    assert nm * tm == M and nn * tn == N and nk * tk == K
    if mode == "nn":
        a_bs, b_bs = (tm, tk), (tk, tn)
        a_ix = lambda i, j, k: (i, k)
        b_ix = lambda i, j, k: (k, j)
        dims = (((1,), (0,)), ((), ()))
    elif mode == "nt":
        a_bs, b_bs = (tm, tk), (tn, tk)
        a_ix = lambda i, j, k: (i, k)
        b_ix = lambda i, j, k: (j, k)
        dims = (((1,), (1,)), ((), ()))
    else:
        a_bs, b_bs = (tk, tm), (tk, tn)
        a_ix = lambda i, j, k: (k, i)
        b_ix = lambda i, j, k: (k, j)
        dims = (((0,), (0,)), ((), ()))

    def spec(bs, ix, off):
        def im(i, j, k):
            r, c = ix(i, j, k)
            return (r + off[0], c + off[1])
        return pl.BlockSpec(bs, im)

    in_specs = [spec(a_bs, a_ix, a_off), spec(b_bs, b_ix, b_off)]
    args = [a, b]
    if res is not None:
        in_specs.append(pl.BlockSpec((tm, tn), lambda i, j, k: (i, j)))
        args.append(res)
    out_shape = jax.ShapeDtypeStruct((M if out_rows is None else out_rows, N), out_dtype)
    out_spec = pl.BlockSpec((tm, tn), lambda i, j, k: (i + out_off, j))
    has_res = res is not None
    extra = list(deps) + ([out_prev] if out_prev is not None else [])
    for dep in extra:
        in_specs.append(pl.BlockSpec(memory_space=pl.ANY))
        args.append(dep)
    o_pos = 2 + int(has_res) + len(extra)
    aliases = {len(args) - 1: 0} if out_prev is not None else {}

    def kern(*refs):
        a_ref, b_ref = refs[0], refs[1]
        res_ref = refs[2] if has_res else None
        o_ref = refs[o_pos]
        av = a_ref[...]
        bv = b_ref[...]
        if av.dtype != BF16:
            av = av.astype(BF16)
        if bv.dtype != BF16:
            bv = bv.astype(BF16)
        part = lax.dot_general(av, bv, dims, preferred_element_type=F32)

        def write(val):
            if has_res:
                val = val + res_ref[...]
            o_ref[...] = val.astype(out_dtype)

        if nk == 1:
            write(part)
        else:
            acc_ref = refs[-1]
            k = pl.program_id(2)

            @pl.when(k == 0)
            def _():
                acc_ref[...] = part

            @pl.when(k > 0)
            def _():
                acc_ref[...] += part

            @pl.when(k == nk - 1)
            def _():
                write(acc_ref[...])

    scratch = [pltpu.VMEM((tm, tn), F32)] if nk > 1 else []
    return pl.pallas_call(
        kern, grid=(nm, nn, nk), in_specs=in_specs, out_specs=out_spec, out_shape=out_shape,
        scratch_shapes=scratch, input_output_aliases=aliases, compiler_params=_cparams(3), name=name)(*args)


def _mm_rms_bwd(a, b, x, g, dres, *, mode, M, K, tm, name, b_off=(0, 0), deps=()):
    nd = len(deps)
    b_bs = (K, D) if mode == "nn" else (D, K)
    dims = (((1,), (0,)), ((), ())) if mode == "nn" else (((1,), (1,)), ((), ()))

    def kern(a_ref, b_ref, x_ref, g_ref, dres_ref, *rest):
        dx_ref, dg_ref = rest[nd:]
        i = pl.program_id(0)
        av = a_ref[...]
        if av.dtype != BF16:
            av = av.astype(BF16)
        dhv = lax.dot_general(av, b_ref[...], dims, preferred_element_type=F32)
        xv = x_ref[...]
        r = lax.rsqrt(jnp.mean(xv * xv, axis=-1, keepdims=True) + RMS_EPS)
        xhat = xv * r
        gy = dhv * g_ref[...]
        dx_ref[...] = dres_ref[...] + r * (gy - xhat * jnp.mean(gy * xhat, axis=-1, keepdims=True))
        part = jnp.sum(dhv * xhat, axis=0, keepdims=True)

        @pl.when(i == 0)
        def _():
            dg_ref[...] = part

        @pl.when(i > 0)
        def _():
            dg_ref[...] += part

    row = pl.BlockSpec((tm, D), lambda i: (i, 0))
    vec = pl.BlockSpec((1, D), lambda i: (0, 0))
    return pl.pallas_call(
        kern, grid=(M // tm,),
        in_specs=[pl.BlockSpec((tm, K), lambda i: (i, 0)),
                  pl.BlockSpec(b_bs, lambda i: b_off, pipeline_mode=pl.Buffered(1)), row, vec, row]
        + [pl.BlockSpec(memory_space=pl.ANY)] * nd,
        out_specs=[row, vec],
        out_shape=[jax.ShapeDtypeStruct((M, D), F32), jax.ShapeDtypeStruct((1, D), F32)],
        compiler_params=_cparams(1), name=name)(a, b, x, g, dres, *deps)


def _mm_res_norm(a, b, res, g, *, K, tm, name, b_off=(0, 0), tgt=None):
    M = a.shape[0]
    head = tgt is not None

    def kern(a_ref, b_ref, res_ref, g_ref, *rest):
        xv = res_ref[...] + jnp.dot(a_ref[...], b_ref[...], preferred_element_type=F32)
        gv = g_ref[...]
        r = lax.rsqrt(jnp.mean(xv * xv, axis=-1, keepdims=True) + RMS_EPS)
        xhat = xv * r
        if not head:
            xo_ref, h_ref = rest
            xo_ref[...] = xv
            h_ref[...] = (xhat * gv).astype(BF16)
            return
        t_ref, dx_ref, dg_ref, ls_ref = rest
        i = pl.program_id(0)
        e = xhat * gv - t_ref[...]
        dy = e * (1.0 / D)
        gy = dy * gv
        dx_ref[...] = r * (gy - xhat * jnp.mean(gy * xhat, axis=-1, keepdims=True))
        dgp = jnp.sum(dy * xhat, axis=0, keepdims=True)
        lsp = jnp.sum(e * e, axis=0, keepdims=True)

        @pl.when(i == 0)
        def _():
            dg_ref[...] = dgp
            ls_ref[...] = lsp

        @pl.when(i > 0)
        def _():
            dg_ref[...] += dgp
            ls_ref[...] += lsp

    row = pl.BlockSpec((tm, D), lambda i: (i, 0))
    vec = pl.BlockSpec((1, D), lambda i: (0, 0))
    in_specs = [pl.BlockSpec((tm, K), lambda i: (i, 0)),
                pl.BlockSpec((K, D), lambda i: b_off, pipeline_mode=pl.Buffered(1)), row, vec]
    if head:
        return pl.pallas_call(
            kern, grid=(M // tm,), in_specs=in_specs + [row], out_specs=[row, vec, vec],
            out_shape=[jax.ShapeDtypeStruct((M, D), F32), jax.ShapeDtypeStruct((1, D), F32),
                       jax.ShapeDtypeStruct((1, D), F32)],
            compiler_params=_cparams(1), name=name)(a, b, res, g, tgt)
    return pl.pallas_call(
        kern, grid=(M // tm,), in_specs=in_specs, out_specs=[row, row],
        out_shape=[jax.ShapeDtypeStruct((M, D), F32), jax.ShapeDtypeStruct((M, D), BF16)],
        compiler_params=_cparams(1), name=name)(a, b, res, g)


def _rms_fwd(x, g, name, deps=()):
    S = x.shape[0]
    tr = 512

    def kern(x_ref, g_ref, *rest):
        h_ref = rest[-1]
        xv = x_ref[...]
        r = lax.rsqrt(jnp.mean(xv * xv, axis=-1, keepdims=True) + RMS_EPS)
        h_ref[...] = (xv * r * g_ref[...]).astype(BF16)

    return pl.pallas_call(
        kern, grid=(S // tr,),
        in_specs=[pl.BlockSpec((tr, D), lambda i: (i, 0)), pl.BlockSpec((1, D), lambda i: (0, 0))]
        + [pl.BlockSpec(memory_space=pl.ANY)] * len(deps),
        out_specs=pl.BlockSpec((tr, D), lambda i: (i, 0)),
        out_shape=jax.ShapeDtypeStruct((S, D), BF16), compiler_params=_cparams(1), name=name)(x, g, *deps)


def _chunks_put(scr, val):
    for c in range(scr.shape[0]):
        scr[c] = val[:, c * 128:(c + 1) * 128]


def _chunks_get(scr):
    return jnp.concatenate([scr[c] for c in range(scr.shape[0])], axis=1)


def _chunks_rows(scr, r, n, dil):
    return jnp.concatenate([scr.at[c][pl.ds(r, n, stride=dil), :] for c in range(scr.shape[0])], axis=1)


def _chunks_add_rows(scr, val, r, n, dil, accumulate):
    for c in range(scr.shape[0]):
        rows = pl.ds(r, n, stride=dil)
        piece = val[:, c * 128:(c + 1) * 128]
        tile = scr.at[c]
        tile[rows, :] = tile[rows, :] + piece if accumulate else piece


def _rms_fwd_folded(x, g, name, deps=()):
    S = x.shape[0]
    tr = 512
    dils = DILS[1:]

    def kern(x_ref, g_ref, *rest):
        outs, scr = rest[len(deps):-1], rest[-1]
        xv = x_ref[...]
        r = lax.rsqrt(jnp.mean(xv * xv, axis=-1, keepdims=True) + RMS_EPS)
        h = (xv * r * g_ref[...]).astype(BF16)
        outs[0][...] = h
        _chunks_put(scr, h.astype(F32))
        for o_ref, dil in zip(outs[1:], dils):
            for res in range(dil):
                o_ref[res] = _chunks_rows(scr, res, tr // dil, dil).astype(BF16)

    return pl.pallas_call(
        kern, grid=(S // tr,),
        in_specs=[pl.BlockSpec((tr, D), lambda i: (i, 0)), pl.BlockSpec((1, D), lambda i: (0, 0))]
        + [pl.BlockSpec(memory_space=pl.ANY)] * len(deps),
        out_specs=[pl.BlockSpec((tr, D), lambda i: (i, 0))]
        + [pl.BlockSpec((dil, tr // dil, D), lambda i: (0, i, 0)) for dil in dils],
        out_shape=[jax.ShapeDtypeStruct((S, D), BF16)]
        + [jax.ShapeDtypeStruct((dil, S // dil, D), BF16) for dil in dils],
        scratch_shapes=[pltpu.VMEM((D // 128, tr, 128), F32)],
        compiler_params=_cparams(1), name=name)(x, g, *deps)


def _rms_bwd(dh, x, g, dres, name, folded=()):
    S = x.shape[0]
    tr = 512
    nf = len(folded)

    def kern(dh_ref, *rest):
        f_refs = rest[:nf]
        x_ref, g_ref, dres_ref, dx_ref, dg_ref = rest[nf:nf + 5]
        i = pl.program_id(0)
        xv = x_ref[...]
        if nf:
            acc_ref = rest[nf + 5]
            _chunks_put(acc_ref, dh_ref[...].astype(F32))
            for f_ref in f_refs:
                dil = f_ref.shape[0]
                for res in range(dil):
                    _chunks_add_rows(acc_ref, f_ref[res], res, tr // dil, dil, True)
            dhv = _chunks_get(acc_ref)
        else:
            dhv = dh_ref[...].astype(F32)
        r = lax.rsqrt(jnp.mean(xv * xv, axis=-1, keepdims=True) + RMS_EPS)
        xhat = xv * r
        gy = dhv * g_ref[...]
        dx_ref[...] = dres_ref[...] + r * (gy - xhat * jnp.mean(gy * xhat, axis=-1, keepdims=True))
        part = jnp.sum(dhv * xhat, axis=0, keepdims=True)

        @pl.when(i == 0)
        def _():
            dg_ref[...] = part

        @pl.when(i > 0)
        def _():
            dg_ref[...] += part

    row = pl.BlockSpec((tr, D), lambda i: (i, 0))
    vec = pl.BlockSpec((1, D), lambda i: (0, 0))
    fspecs = [pl.BlockSpec((f.shape[0], tr // f.shape[0], D), lambda i: (0, i, 0)) for f in folded]
    return pl.pallas_call(
        kern, grid=(S // tr,), in_specs=[row] + fspecs + [row, vec, row], out_specs=[row, vec],
        out_shape=[jax.ShapeDtypeStruct((S, D), F32), jax.ShapeDtypeStruct((1, D), F32)],
        scratch_shapes=[pltpu.VMEM((D // 128, tr, 128), F32)] if nf else [],
        compiler_params=_cparams(1), name=name)(dh, *folded, x, g, dres)


def _ffn_up(h, G, name):
    S = h.shape[0]
    tm = 512
    nj = DFF // HCH

    def kern(h_ref, w_ref, gu_ref, act_ref):
        gu = lax.dot_general(h_ref[...], w_ref[...], (((1,), (1,)), ((), ())), preferred_element_type=F32)
        gu_ref[...] = gu.astype(BF16)
        gate = gu[:, :HCH]
        up = gu[:, HCH:]
        act_ref[...] = (gate * jax.nn.sigmoid(gate) * up).astype(BF16)

    return pl.pallas_call(
        kern, grid=(nj, S // tm),
        in_specs=[pl.BlockSpec((tm, D), lambda j, i: (i, 0)),
                  pl.BlockSpec((2 * HCH, D), lambda j, i: (j, 0))],
        out_specs=[pl.BlockSpec((tm, 2 * HCH), lambda j, i: (i, j)),
                   pl.BlockSpec((tm, HCH), lambda j, i: (i, j))],
        out_shape=[jax.ShapeDtypeStruct((S, 2 * DFF), BF16), jax.ShapeDtypeStruct((S, DFF), BF16)],
        compiler_params=_cparams(2), name=name)(h, G)


def _ffn_down_bwd(dx, G, gu, name):
    S = dx.shape[0]
    tm = 512
    nj = DFF // HCH

    def kern(dx_ref, w_ref, gu_ref, o_ref):
        dact = lax.dot_general(dx_ref[...].astype(BF16), w_ref[...], (((1,), (1,)), ((), ())),
                               preferred_element_type=F32)
        gate = gu_ref[:, :HCH].astype(F32)
        up = gu_ref[:, HCH:].astype(F32)
        sig = jax.nn.sigmoid(gate)
        silu = gate * sig
        o_ref[:, :HCH] = (dact * up * (sig * (1.0 + gate * (1.0 - sig)))).astype(BF16)
        o_ref[:, HCH:] = (dact * silu).astype(BF16)

    return pl.pallas_call(
        kern, grid=(nj, S // tm),
        in_specs=[pl.BlockSpec((tm, D), lambda j, i: (i, 0)),
                  pl.BlockSpec((HCH, D), lambda j, i: (j, 0)),
                  pl.BlockSpec((tm, 2 * HCH), lambda j, i: (i, j))],
        out_specs=pl.BlockSpec((tm, 2 * HCH), lambda j, i: (i, j)),
        out_shape=jax.ShapeDtypeStruct((S, 2 * DFF), BF16),
        compiler_params=_cparams(2), name=name)(dx, G, gu)


def _trail(u, *, backward, name):
    S = u.shape[0]

    def kern(u_ref, o_ref):
        g = pl.program_id(0)
        for grp in range(POOL_G):
            @pl.when(g == grp)
            def _(grp=grp):
                uv = u_ref[...].astype(F32)
                row = lax.broadcasted_iota(jnp.int32, uv.shape, 0)
                cnt = jnp.minimum(row + 1, 2 << grp).astype(F32)
                s = uv / cnt if backward else uv
                for k in (1, 2, 4, 8)[:grp + 1]:
                    if backward:
                        sh = jnp.where(row < S - k, pltpu.roll(s, S - k, 0), 0.0)
                    else:
                        sh = jnp.where(row >= k, pltpu.roll(s, k, 0), 0.0)
                    s = s + sh
                if backward:
                    o_ref[...] = (s - uv).astype(BF16)
                else:
                    o_ref[...] = (s / cnt - uv).astype(BF16)

    blk = pl.BlockSpec((S, PGD), lambda g: (0, g))
    return pl.pallas_call(
        kern, grid=(POOL_G,), in_specs=[blk], out_specs=blk,
        out_shape=jax.ShapeDtypeStruct((S, D), BF16), compiler_params=_cparams(1), name=name)(u)


def _pool_out(yd, G, scale, xres):
    S = yd.shape[0]
    tm = min(S, 4096)

    def kern(y_ref, w_ref, s_ref, x_ref, o_ref):
        z = jnp.dot(y_ref[...], w_ref[...], preferred_element_type=F32)
        o_ref[...] = x_ref[...] + z * s_ref[...]

    tile = pl.BlockSpec((tm, PGD), lambda i, g: (i, g))
    return pl.pallas_call(
        kern, grid=(S // tm, POOL_G),
        in_specs=[tile, pl.BlockSpec((PGD, PGD), lambda i, g: (0, g)),
                  pl.BlockSpec((1, PGD), lambda i, g: (0, g)), tile],
        out_specs=tile, out_shape=jax.ShapeDtypeStruct((S, D), F32),
        compiler_params=_cparams(2), name="pool_out")(yd, G, scale, xres)


def _pool_out_bwd(dz, yd, G, scale):
    S = yd.shape[0]
    tm = min(S, 4096)
    ni = S // tm

    def kern(dz_ref, y_ref, w_ref, s_ref, dy_ref, ds_ref, dw_ref, acc_ref):
        i = pl.program_id(1)
        dzv = dz_ref[...]
        yv = y_ref[...]
        wv = w_ref[...]
        zraw = jnp.dot(yv, wv, preferred_element_type=F32)
        dsp = jnp.sum(dzv * zraw, axis=0, keepdims=True)
        dzr = (dzv * s_ref[...]).astype(BF16)
        dy_ref[...] = lax.dot_general(dzr, wv, (((1,), (1,)), ((), ())), preferred_element_type=F32)
        dwp = lax.dot_general(yv, dzr, (((0,), (0,)), ((), ())), preferred_element_type=F32)

        @pl.when(i == 0)
        def _():
            ds_ref[...] = dsp
            acc_ref[...] = dwp

        @pl.when(i > 0)
        def _():
            ds_ref[...] += dsp
            acc_ref[...] += dwp

        @pl.when(i == ni - 1)
        def _():
            dw_ref[...] = acc_ref[...].astype(BF16)

    tile = pl.BlockSpec((tm, PGD), lambda g, i: (i, g))
    return pl.pallas_call(
        kern, grid=(POOL_G, ni),
        in_specs=[tile, tile, pl.BlockSpec((PGD, PGD), lambda g, i: (0, g)),
                  pl.BlockSpec((1, PGD), lambda g, i: (0, g))],
        out_specs=[tile, pl.BlockSpec((1, PGD), lambda g, i: (0, g)),
                   pl.BlockSpec((PGD, PGD), lambda g, i: (0, g))],
        out_shape=[jax.ShapeDtypeStruct((S, D), F32), jax.ShapeDtypeStruct((1, D), F32),
                   jax.ShapeDtypeStruct((PGD, D), BF16)],
        scratch_shapes=[pltpu.VMEM((PGD, PGD), F32)],
        compiler_params=_cparams(2), name="pool_out_bwd")(dz, yd, G, scale)


def _bias_table():
    qi = jnp.arange(QB)[:, None]
    ki = jnp.arange(2 * QB)[None, :]
    delta = QB + qi - ki
    inband = (delta >= 0) & (delta <= QB)
    n = NGROUPS * HEADS
    slopes = jnp.exp2(-8.0 * jnp.arange(1, n + 1, dtype=F32) / n).reshape(NGROUPS, HEADS)
    dil = jnp.asarray(DILS, F32)
    bias = -slopes[:, :, None, None] * (delta.astype(F32)[None, None] * dil[:, None, None, None])
    return jnp.where(inband[None, None], bias, NEG)


def _attn_fwd(qkv_f, bias, nb, name):
    S = qkv_f.shape[0]
    nblk = S // QB
    scale = HD ** -0.5

    def kern(q_ref, kc_ref, kp_ref, vc_ref, vp_ref, b_ref, o_ref, l_ref, s_scr, p_scr, r_scr):
        b = pl.program_id(0)
        has_prev = jnp.bitwise_and(b, nb - 1) != 0
        col = lax.broadcasted_iota(jnp.int32, (QB, 2 * QB), 1)
        dead = jnp.logical_and(col < QB, jnp.logical_not(has_prev))
        lane = lax.broadcasted_iota(jnp.int32, (QB, HD), 1)
        lse_all = jnp.zeros((QB, HD), F32)
        for h in range(HEADS):
            sl = slice(h * HD, (h + 1) * HD)
            kk = jnp.concatenate([kp_ref[:, sl], kc_ref[:, sl]], axis=0)
            s_scr[h] = lax.dot_general(q_ref[:, sl], kk, (((1,), (1,)), ((), ())), preferred_element_type=F32)
        for h in range(HEADS):
            s = s_scr[h] * scale + b_ref[h]
            s = jnp.where(dead, NEG, s)
            m = jnp.max(s, axis=-1, keepdims=True)
            p = jnp.exp(s - m)
            den = jnp.sum(p, axis=-1, keepdims=True)
            p_scr[h] = p.astype(BF16)
            r_scr[h] = jnp.broadcast_to(1.0 / den, (QB, HD))
            lse_all = jnp.where(lane == h, m + jnp.log(den), lse_all)
        for h in range(HEADS):
            sl = slice(h * HD, (h + 1) * HD)
            vv = jnp.concatenate([vp_ref[:, sl], vc_ref[:, sl]], axis=0)
            o = jnp.dot(p_scr[h], vv, preferred_element_type=F32) * r_scr[h]
            o_ref[:, sl] = o.astype(BF16)
        l_ref[...] = lse_all

    def blk(colblk, prev):
        if prev:
            return pl.BlockSpec((QB, D), lambda b: (jnp.maximum(b - 1, 0), colblk))
        return pl.BlockSpec((QB, D), lambda b: (b, colblk))

    return pl.pallas_call(
        kern, grid=(nblk,),
        in_specs=[blk(0, False), blk(1, False), blk(1, True), blk(2, False), blk(2, True),
                  pl.BlockSpec((HEADS, QB, 2 * QB), lambda b: (0, 0, 0))],
        out_specs=[pl.BlockSpec((QB, D), lambda b: (b, 0)), pl.BlockSpec((QB, HD), lambda b: (b, 0))],
        out_shape=[jax.ShapeDtypeStruct((S, D), BF16), jax.ShapeDtypeStruct((S, HD), F32)],
        scratch_shapes=[pltpu.VMEM((HEADS, QB, 2 * QB), F32), pltpu.VMEM((HEADS, QB, 2 * QB), BF16),
                        pltpu.VMEM((HEADS, QB, HD), F32)],
        compiler_params=_cparams(1), name=name)(qkv_f, qkv_f, qkv_f, qkv_f, qkv_f, bias)


def _natural(ref, scr, tm):
    dil = ref.shape[0]
    for res in range(dil):
        _chunks_add_rows(scr, ref[res].astype(F32), res, tm // dil, dil, False)
    return _chunks_get(scr)


def _attn_merge(os, lses):
    S = os[0].shape[0]
    tm = 512

    def kern(o0, o1, o2, l0, l1, l2, om_ref, lm_ref, ls1, ls2, os1, os2):
        la = l0[...]
        lb = _natural(l1, ls1, tm)
        lc = _natural(l2, ls2, tm)
        m = jnp.maximum(jnp.maximum(la, lb), lc)
        e0, e1, e2 = jnp.exp(la - m), jnp.exp(lb - m), jnp.exp(lc - m)
        tot = e0 + e1 + e2
        lm_ref[...] = m + jnp.log(tot)
        w0, w1, w2 = e0 / tot, e1 / tot, e2 / tot
        for res in range(o1.shape[0]):
            _chunks_add_rows(os1, o1[res].astype(F32), res, tm // o1.shape[0], o1.shape[0], False)
        for res in range(o2.shape[0]):
            _chunks_add_rows(os2, o2[res].astype(F32), res, tm // o2.shape[0], o2.shape[0], False)
        for h in range(HEADS):
            sl = slice(h * HD, (h + 1) * HD)
            acc = w0[:, h:h + 1] * o0[:, sl].astype(F32) + w1[:, h:h + 1] * os1[h] + w2[:, h:h + 1] * os2[h]
            om_ref[:, sl] = acc.astype(BF16)

    def spec(a, c):
        if a.ndim == 2:
            return pl.BlockSpec((tm, c), lambda i: (i, 0))
        return pl.BlockSpec((a.shape[0], tm // a.shape[0], c), lambda i: (0, i, 0))

    return pl.pallas_call(
        kern, grid=(S // tm,),
        in_specs=[spec(a, D) for a in os] + [spec(a, HD) for a in lses],
        out_specs=[pl.BlockSpec((tm, D), lambda i: (i, 0)), pl.BlockSpec((tm, HD), lambda i: (i, 0))],
        out_shape=[jax.ShapeDtypeStruct((S, D), BF16), jax.ShapeDtypeStruct((S, HD), F32)],
        scratch_shapes=[pltpu.VMEM((1, tm, HD), F32), pltpu.VMEM((1, tm, HD), F32),
                        pltpu.VMEM((HEADS, tm, HD), F32), pltpu.VMEM((HEADS, tm, HD), F32)],
        compiler_params=_cparams(1), name="attn_merge")(*os, *lses)


def _attn_bwd_prep(do, o, lse):
    S = o.shape[0]
    tm = 512
    dils = DILS[1:]

    def kern(do_ref, o_ref, l_ref, *rest):
        do_outs, l_outs, d_outs = rest[0:3], rest[3:5], rest[5:8]
        do_scr, l_scr, d_scr = rest[8:11]
        lane = lax.broadcasted_iota(jnp.int32, (tm, HD), 1)
        acc = jnp.zeros((tm, HD), F32)
        for h in range(HEADS):
            sl = slice(h * HD, (h + 1) * HD)
            prod = do_ref[:, sl] * o_ref[:, sl].astype(F32)
            acc = jnp.where(lane == h, jnp.sum(prod, axis=-1, keepdims=True), acc)
        d_scr[0] = acc
        l_scr[0] = l_ref[...]
        _chunks_put(do_scr, do_ref[...])
        do_outs[0][...] = do_ref[...].astype(BF16)
        d_outs[0][...] = acc
        for j, dil in enumerate(dils):
            for res in range(dil):
                n = tm // dil
                do_outs[1 + j][res] = _chunks_rows(do_scr, res, n, dil).astype(BF16)
                l_outs[j][res] = _chunks_rows(l_scr, res, n, dil)
                d_outs[1 + j][res] = _chunks_rows(d_scr, res, n, dil)

    def nat(c):
        return pl.BlockSpec((tm, c), lambda i: (i, 0))

    def fol(dil, c):
        return pl.BlockSpec((dil, tm // dil, c), lambda i: (0, i, 0))

    def shapes(c, dt, with_natural):
        first = [jax.ShapeDtypeStruct((S, c), dt)] if with_natural else []
        return first + [jax.ShapeDtypeStruct((dil, S // dil, c), dt) for dil in dils]

    outs = pl.pallas_call(
        kern, grid=(S // tm,), in_specs=[nat(D), nat(D), nat(HD)],
        out_specs=[nat(D)] + [fol(dil, D) for dil in dils] + [fol(dil, HD) for dil in dils]
        + [nat(HD)] + [fol(dil, HD) for dil in dils],
        out_shape=shapes(D, BF16, True) + shapes(HD, F32, False) + shapes(HD, F32, True),
        scratch_shapes=[pltpu.VMEM((HEADS, tm, HD), F32), pltpu.VMEM((1, tm, HD), F32), pltpu.VMEM((1, tm, HD), F32)],
        compiler_params=_cparams(1), name="attn_bwd_prep")(do, o, lse)
    return outs[0:3], [lse] + list(outs[3:5]), outs[5:8]


def _attn_bwd(qkv_f, do_f, lse_f, delta_f, bias, nb, name):
    S = qkv_f.shape[0]
    nblk = S // QB
    scale = HD ** -0.5

    def kern(q_ref, kc_ref, kp_ref, vc_ref, vp_ref, do_ref, l_ref, d_ref, b_ref, out_ref, dq_c, dk_c, dv_c,
             s_scr, dp_scr, ds_scr, p_scr):
        b = pl.program_id(0)

        @pl.when(b == 0)
        def _():
            dq_c[...] = jnp.zeros_like(dq_c)
            dk_c[...] = jnp.zeros_like(dk_c)
            dv_c[...] = jnp.zeros_like(dv_c)

        @pl.when(b == nblk)
        def _():
            out_ref[:, 0:D] = dq_c[...].astype(BF16)
            out_ref[:, D:2 * D] = dk_c[...].astype(BF16)
            out_ref[:, 2 * D:3 * D] = dv_c[...].astype(BF16)

        @pl.when(b < nblk)
        def _():
            has_prev = jnp.bitwise_and(b, nb - 1) != 0
            col = lax.broadcasted_iota(jnp.int32, (QB, 2 * QB), 1)
            dead = jnp.logical_and(col < QB, jnp.logical_not(has_prev))
            out_ref[:, 0:D] = dq_c[...].astype(BF16)
            lv = l_ref[...]
            dv_ = d_ref[...]
            for h in range(HEADS):
                sl = slice(h * HD, (h + 1) * HD)
                kk = jnp.concatenate([kp_ref[:, sl], kc_ref[:, sl]], axis=0)
                vv = jnp.concatenate([vp_ref[:, sl], vc_ref[:, sl]], axis=0)
                s_scr[h] = lax.dot_general(q_ref[:, sl], kk, (((1,), (1,)), ((), ())), preferred_element_type=F32)
                dp_scr[h] = lax.dot_general(do_ref[:, sl], vv, (((1,), (1,)), ((), ())),
                                            preferred_element_type=F32)
            for h in range(HEADS):
                s = s_scr[h] * scale + b_ref[h]
                s = jnp.where(dead, NEG, s)
                p = jnp.exp(s - lv[:, h:h + 1])
                ds_scr[h] = (p * (dp_scr[h] - dv_[:, h:h + 1]) * scale).astype(BF16)
                p_scr[h] = p.astype(BF16)
            for h in range(HEADS):
                sl = slice(h * HD, (h + 1) * HD)
                kk = jnp.concatenate([kp_ref[:, sl], kc_ref[:, sl]], axis=0)
                ds = ds_scr[h]
                dq_c[:, sl] = jnp.dot(ds, kk, preferred_element_type=F32)
                dkk = lax.dot_general(ds, q_ref[:, sl], (((0,), (0,)), ((), ())), preferred_element_type=F32)
                dvv = lax.dot_general(p_scr[h], do_ref[:, sl], (((0,), (0,)), ((), ())),
                                      preferred_element_type=F32)
                out_ref[:, D + h * HD:D + (h + 1) * HD] = (dk_c[:, sl] + dkk[:QB]).astype(BF16)
                out_ref[:, 2 * D + h * HD:2 * D + (h + 1) * HD] = (dv_c[:, sl] + dvv[:QB]).astype(BF16)
                dk_c[:, sl] = dkk[QB:]
                dv_c[:, sl] = dvv[QB:]

    last = nblk - 1

    def blk(colblk, prev):
        if prev:
            return pl.BlockSpec((QB, D), lambda b: (jnp.maximum(jnp.minimum(b, last) - 1, 0), colblk))
        return pl.BlockSpec((QB, D), lambda b: (jnp.minimum(b, last), colblk))

    stat = pl.BlockSpec((QB, HD), lambda b: (jnp.minimum(b, last), 0))
    return pl.pallas_call(
        kern, grid=(nblk + 1,),
        in_specs=[blk(0, False), blk(1, False), blk(1, True), blk(2, False), blk(2, True),
                  pl.BlockSpec((QB, D), lambda b: (jnp.minimum(b, last), 0)), stat, stat,
                  pl.BlockSpec((HEADS, QB, 2 * QB), lambda b: (0, 0, 0))],
        out_specs=pl.BlockSpec((QB, 3 * D), lambda b: (jnp.maximum(b - 1, 0), 0)),
        out_shape=jax.ShapeDtypeStruct((S, 3 * D), BF16),
        scratch_shapes=[pltpu.VMEM((QB, D), F32), pltpu.VMEM((QB, D), F32), pltpu.VMEM((QB, D), F32),
                        pltpu.VMEM((HEADS, QB, 2 * QB), F32), pltpu.VMEM((HEADS, QB, 2 * QB), F32),
                        pltpu.VMEM((HEADS, QB, 2 * QB), BF16), pltpu.VMEM((HEADS, QB, 2 * QB), BF16)],
        compiler_params=_cparams(1), name=name)(qkv_f, qkv_f, qkv_f, qkv_f, qkv_f, do_f, lse_f, delta_f, bias)


def _local_step(x, tgt, comm, attn_norm, ffn_norm, final_norm, pool_norm, pool_scale):
    S = x.shape[0]
    bias = _bias_table()
    g_attn = attn_norm.reshape(1, D)
    g_f0 = ffn_norm[0:1]
    g_f1 = ffn_norm[1:2]
    g_fin = final_norm.reshape(1, D)
    W = {}

    def ffn_fwd(xin, h, l, next_gain, target=None):
        gu, act = _ffn_up(h, W[f"gu{l}"], f"ffn_up{l}")
        return gu, act, _mm_res_norm(act, W[f"d{l}"], xin, next_gain, K=DFF, tm=512, tgt=target,
                                     name=f"ffn_down{l}")

    def ffn_bwd(dxo, xin, gain, h, gu, act, l, rs_group):
        dgu = _ffn_down_bwd(dxo, W[f"d{l}"], gu, f"ffn_down_bwd{l}")
        gw_d = _mm(act, dxo, mode="tn", M=DFF, N=D, K=S, tm=HCH, tn=D, tk=2048, out_dtype=BF16, name=f"gw_d{l}")
        gw_gu = _mm(dgu, h, mode="tn", M=2 * DFF, N=D, K=S, tm=HCH, tn=D, tk=2048, out_dtype=BF16, name=f"gw_gu{l}")
        token = comm.send_grads(rs_group, {f"d{l}": gw_d, f"gu{l}": gw_gu})
        return _mm_rms_bwd(dgu, W[f"gu{l}"], xin, gain, dxo, mode="nn", M=S, K=2 * DFF, tm=512, deps=(token,),
                           name=f"ffn_up_bwd{l}")

    nbs = [S // QB // dil for dil in DILS]
    hf = _rms_fwd_folded(x, g_attn, "rms_attn", deps=comm.ag_tokens)
    hf = [h.reshape(S, D) for h in hf]
    W.update(comm.weights(0, hf[0]))
    qkv_f, o_f, lse_f = [], [], []
    for g, dil in enumerate(DILS):
        qkv_f.append(_mm(hf[g], W["qkv"], mode="nt", M=S, N=3 * D, K=D, tm=2048, tn=1024, tk=D, out_dtype=BF16,
                         b_off=(3 * g, 0), name=f"qkv_proj{g}"))
        og, lg = _attn_fwd(qkv_f[g], bias[g], nbs[g], f"attn_fwd{g}")
        o_f.append(og if dil == 1 else og.reshape(dil, S // dil, D))
        lse_f.append(lg if dil == 1 else lg.reshape(dil, S // dil, HD))
    o, lse = _attn_merge(o_f, lse_f)
    W.update(comm.weights(1, o))
    x1, h1 = _mm_res_norm(o, W["wo"], x, g_f0, K=D, tm=1024, name="attn_out")
    gu0, act0, (x2, h2) = ffn_fwd(x1, h1, 0, pool_norm)

    W.update(comm.weights(2, x2))
    u = _mm(h2, W["wpi"], mode="nn", M=S, N=D, K=D, tm=1024, tn=D, tk=D, out_dtype=F32, name="pool_in")
    yd = _trail(u, backward=False, name="trail_fwd")
    x3 = _pool_out(yd, W["pg"], pool_scale, x2)
    h3 = _rms_fwd(x3, g_f1, "rms_ffn1")
    gu1, act1, (dx4, d_fin, lossvec) = ffn_fwd(x3, h3, 1, g_fin, target=tgt)

    dx3, d_f1 = ffn_bwd(dx4, x3, g_f1, h3, gu1, act1, 1, 0)
    dyd, d_scale, gw_pg = _pool_out_bwd(dx3, yd, W["pg"], pool_scale)
    du = _trail(dyd, backward=True, name="trail_bwd")
    gw_pi = _mm(h2, du, mode="tn", M=D, N=D, K=S, tm=D, tn=D, tk=2048, out_dtype=BF16, name="gw_pi")
    token = comm.send_grads(1, {"pg": gw_pg, "wpi": gw_pi})
    dx2, d_pool = _mm_rms_bwd(du, W["wpi"], x2, pool_norm, dx3, mode="nt", M=S, K=D, tm=1024, deps=(token,),
                              name="pool_in_bwd")
    dx1, d_f0 = ffn_bwd(dx2, x1, g_f0, h1, gu0, act0, 0, 2)

    gw_o = _mm(o, dx1, mode="tn", M=D, N=D, K=S, tm=D, tn=D, tk=2048, out_dtype=BF16, name="gw_o")
    do = _mm(dx1, W["wo"], mode="nt", M=S, N=D, K=D, tm=1024, tn=D, tk=D, out_dtype=F32, name="attn_out_bwd")
    do_f, lse_ff, delta_f = _attn_bwd_prep(do, o, lse)
    dqkv_f, gw_qkv = [], None
    for g in range(NGROUPS):
        dqkv_f.append(_attn_bwd(qkv_f[g], do_f[g].reshape(S, D), lse_ff[g].reshape(S, HD),
                                delta_f[g].reshape(S, HD), bias[g], nbs[g], f"attn_bwd{g}"))
        gw_qkv = _mm(dqkv_f[g], hf[g], mode="tn", M=3 * D, N=D, K=S, tm=1024, tn=D, tk=2048, out_dtype=BF16,
                     out_rows=NGROUPS * 3 * D, out_off=3 * g, out_prev=gw_qkv, name=f"gw_qkv{g}")
    token = comm.send_grads_pairwise({"wo": gw_o, "qkv": gw_qkv})
    dh0_f = [None] * NGROUPS
    for g in reversed(range(NGROUPS)):
        dh0_f[g] = _mm(dqkv_f[g], W["qkv"], mode="nn", M=S, N=D, K=3 * D, tm=1024, tn=D, tk=3 * D, out_dtype=F32,
                       b_off=(g, 0), deps=(token,), name=f"qkv_proj_bwd{g}")
        if g == NGROUPS - 1:
            token = comm.pass_grads(dh0_f[g])
    folded = [dh0_f[g].reshape(dil, S // dil, D) for g, dil in enumerate(DILS) if dil > 1]
    grad_x, d_attn = _rms_bwd(dh0_f[0], x, g_attn, dx1, "rms_attn_bwd", folded=folded)

    vec = jnp.concatenate([d_attn, d_f0, d_f1, d_fin, d_pool, d_scale, lossvec, jnp.zeros((1, D), F32)], axis=0)
    return grad_x, vec


def _mesh_pos():
    x, y, c = lax.axis_index("x"), lax.axis_index("y"), lax.axis_index("c")
    return x, y, c, 4 * x + 2 * y + c


def _peer(x, y, c, k):
    kx, ky, kc = (k >> 2) & 1, (k >> 1) & 1, k & 1
    px = 1 - x if kx else x
    py = 1 - y if ky else y
    pc = 1 - c if kc else c
    return (px, py, pc), 4 * px + 2 * py + pc


ANY = pl.BlockSpec(memory_space=pl.ANY)


HBM = pl.BlockSpec(memory_space=pltpu.HBM)
SEMS = pl.BlockSpec(memory_space=pltpu.SEMAPHORE)
EFFECT = pltpu.SideEffectType.DATAFLOW_SIDE_EFFECTING
NPEER = NDEV - 1

AG_GROUPS = (("qkv",), ("wo", "gu0", "d0"), ("wpi", "pg", "gu1", "d1"))
AG_ORDER = tuple(n for grp in AG_GROUPS for n in grp)
RS_GROUPS = (("d1", "gu1"), ("pg", "wpi"), ("d0", "gu0"), ("wo", "qkv"))


def _hbm(a):
    return pltpu.with_memory_space_constraint(a, pltpu.HBM)


def _remote(src, dst, send, recv, peer):
    return pltpu.make_async_remote_copy(src_ref=src, dst_ref=dst, send_sem=send, recv_sem=recv, device_id=peer,
                                        device_id_type=pl.DeviceIdType.MESH)


def _bcast_all(v, name, deps=()):
    W = v.shape[1]
    nd = len(deps)

    def kern(v_ref, *rest):
        o_ref, send, recv, lsem = rest[nd:]
        x, y, c, me = _mesh_pos()
        own = pltpu.make_async_copy(v_ref, o_ref.at[me], lsem)
        own.start()
        cps = [_remote(v_ref, o_ref.at[me], send.at[k - 1], recv.at[k - 1], _peer(x, y, c, k)[0])
               for k in range(1, NDEV)]
        for cp in cps:
            cp.start()
        for cp in cps:
            cp.wait_recv()
            cp.wait_send()
        own.wait()

    return pl.pallas_call(
        kern, in_specs=[ANY] * (1 + nd), out_specs=ANY, out_shape=jax.ShapeDtypeStruct((NDEV, 8, W), F32),
        scratch_shapes=[pltpu.SemaphoreType.DMA((NPEER,)), pltpu.SemaphoreType.DMA((NPEER,)),
                        pltpu.SemaphoreType.DMA(())],
        name=name)(v, *deps)


ALL_KS = tuple(range(1, NDEV))
AG_KS1 = (1, 2, 4, 6)
AG_KS2 = (2, 4, 6)
RS_KS_PAIR = (1, 3, 5, 7)
RS_KS_CHIPS = (2, 4, 6)


def _split_start(srcs, src_of, lands, copy_refs, name, deps=(), ks=ALL_KS, to=None):
    ns, n, nd, nk = len(srcs), len(lands), len(deps), len(ks)

    def body(*refs):
        ins, land = refs[:ns], refs[ns:ns + n]
        send, recv = refs[ns + n + nd], refs[ns + n + nd + 1]
        token = refs[-1]
        x, y, c, me = _mesh_pos()
        for j in range(n):
            for i, k in enumerate(ks):
                _, pid = _peer(x, y, c, k)
                dest, _ = _peer(x, y, c, k if to is None else to)
                src, dst = copy_refs(j, (land[j] if src_of[j] is None else ins[src_of[j]]), land[j], me, pid, i)
                _remote(src, dst, send.at[j * nk + i], recv.at[j * nk + i], dest).start()
        token[...] = jnp.zeros_like(token)

    outs = pl.pallas_call(
        body, name=name,
        out_shape=(pltpu.SemaphoreType.DMA((n * nk,)), pltpu.SemaphoreType.DMA((n * nk,)))
        + tuple(pltpu.HBM(a.shape, a.dtype) for a in srcs) + tuple(pltpu.HBM(a.shape, a.dtype) for a in lands)
        + (jax.ShapeDtypeStruct((8, 128), F32),),
        in_specs=(HBM,) * (ns + n) + (ANY,) * nd,
        out_specs=(SEMS, SEMS) + (HBM,) * (ns + n) + (pl.BlockSpec(memory_space=pltpu.VMEM),),
        input_output_aliases={i: 2 + i for i in range(ns + n)},
        compiler_params=pltpu.CompilerParams(has_side_effects=EFFECT),
    )(*[_hbm(a) for a in srcs], *[_hbm(a) for a in lands], *deps)
    return outs[0], outs[1], list(outs[2:2 + ns]), list(outs[2 + ns:2 + ns + n]), outs[-1]


def _split_wait(srcs, src_of, lands, send, recv, sem_rows, wait_refs, after, name, ks=ALL_KS):
    ns, n, nk = len(srcs), len(lands), len(ks)

    def body(*refs):
        ins, land = refs[:ns], refs[ns:ns + n]
        send_ref, recv_ref = refs[ns + n], refs[ns + n + 1]
        x, y, c, me = _mesh_pos()
        for j in range(n):
            for i, k in enumerate(ks):
                peer, _ = _peer(x, y, c, k)
                src, dst = wait_refs(j, (land[j] if src_of[j] is None else ins[src_of[j]]), land[j])
                sem = sem_rows[j] * nk + i
                cp = _remote(src, dst, send_ref.at[sem], recv_ref.at[sem], peer)
                cp.wait_send()
                cp.wait_recv()

    outs = pl.pallas_call(
        body, name=name,
        out_shape=tuple(pltpu.HBM(a.shape, a.dtype) for a in srcs) + tuple(pltpu.HBM(a.shape, a.dtype) for a in lands),
        in_specs=(HBM,) * (ns + n) + (SEMS, SEMS, ANY),
        out_specs=(HBM,) * (ns + n),
        input_output_aliases={i: i for i in range(ns + n)},
        compiler_params=pltpu.CompilerParams(has_side_effects=EFFECT),
    )(*srcs, *lands, send, recv, after)
    return list(outs[:ns]), list(outs[ns:])


class _Comm:
    def __init__(self, shards, me, deps=()):
        self.me = me
        self.ag_land, self.ag_sems, self.ag_tokens = {}, {}, ()
        self.rs = []
        for part, names in enumerate((AG_GROUPS[0], AG_ORDER[len(AG_GROUPS[0]):])):
            rows = [SEC_ROWS[n] for n in names]
            lands = [lax.dynamic_update_slice(lax.empty((NDEV * r, D), BF16), shards[n], (_shard_pos(n, me), 0))
                     for n, r in zip(names, rows)]

            def copy_refs(j, src, land, me, pid, i, names=names, rows=rows):
                own = land.at[pl.ds(pl.multiple_of(_shard_pos(names[j], me), 16), rows[j])]
                return own, own

            send, recv, _, lands, token = _split_start([], [None] * len(names), lands, copy_refs, f"ag_start{part}",
                                                       deps=deps, ks=AG_KS1)
            deps = (token,)
            self.ag_tokens += (token,)
            for j, n in enumerate(names):
                self.ag_land[n] = lands[j]
                self.ag_sems[n] = (send, recv, j)

    def weights(self, group, after):
        names = AG_GROUPS[group]
        send, recv = self.ag_sems[names[0]][:2]
        idx = [self.ag_sems[n][2] for n in names]
        rows = [SEC_ROWS[n] for n in names]
        none = [None] * len(names)

        def wait_refs(j, src, land):
            return land.at[pl.ds(0, rows[j])], land.at[pl.ds(0, rows[j])]

        _, lands = _split_wait([], none, [self.ag_land[n] for n in names], send, recv, idx,
                               wait_refs, after, f"ag_wait{group}", ks=AG_KS1)

        def copy_refs(j, src, land, me, pid, i):
            theirs = land.at[pl.ds(pl.multiple_of(_shard_pos(names[j], pid), 16), rows[j])]
            return theirs, theirs

        send, recv, _, lands, token = _split_start([], none, lands, copy_refs, f"ag_pass{group}", ks=AG_KS2, to=1)
        _, lands = _split_wait([], none, lands, send, recv, list(range(len(names))), wait_refs, token,
                               f"ag_pass_wait{group}", ks=AG_KS2)
        return dict(zip(names, lands))

    def send_grads(self, group, gws):
        names = RS_GROUPS[group]
        rows = [SEC_ROWS[n] for n in names]
        grads = [gws[n] for n in names]
        me = self.me
        lands = [lax.dynamic_update_slice(
            lax.empty((NDEV, r, D), BF16),
            lax.dynamic_slice(g, (_shard_pos(n, me), 0), (r, D))[None], (me, 0, 0))
            for n, r, g in zip(names, rows, grads)]

        def copy_refs(j, src, land, me, pid, i):
            return src.at[pl.ds(pl.multiple_of(_shard_pos(names[j], pid), 16), rows[j])], land.at[me]

        send, recv, srcs, lands, token = _split_start(grads, list(range(len(names))), lands, copy_refs,
                                                      f"rs_start{group}")
        self.rs.append((names, rows, send, recv, srcs, lands, ALL_KS))
        return token

    def send_grads_pairwise(self, gws):
        names = RS_GROUPS[-1]
        rows = [SEC_ROWS[n] for n in names]
        grads = [gws[n] for n in names]
        idx = list(range(len(names)))
        lands = [lax.empty((len(RS_KS_PAIR), r, D), BF16) for r in rows]

        def copy_refs(j, src, land, me, pid, i):
            return src.at[pl.ds(pl.multiple_of(_shard_pos(names[j], pid), 16), rows[j])], land.at[i]

        send, recv, srcs, lands, token = _split_start(grads, idx, lands, copy_refs, "rs_pair_start",
                                                      ks=RS_KS_PAIR, to=1)
        self.pair = (names, rows, send, recv, srcs, lands)
        return token

    def pass_grads(self, after):
        names, rows, send, recv, srcs, lands = self.pair
        idx = list(range(len(names)))
        me = self.me

        def wait_refs(j, src, land):
            return src.at[pl.ds(0, rows[j])], land.at[0]

        srcs, lands = _split_wait(srcs, idx, lands, send, recv, idx, wait_refs, after, "rs_pair_wait", ks=RS_KS_PAIR)
        sums = []
        for n, r, g, got in zip(names, rows, srcs, lands):
            mine = jnp.stack([lax.dynamic_slice(g, (_shard_pos(n, jnp.bitwise_xor(me, k)), 0), (r, D))
                              for k in (0,) + RS_KS_CHIPS])
            sums.append(_pair_sum(mine, got, f"rs_pair_sum_{n}"))
        lands = [lax.dynamic_update_slice(lax.empty(p.shape, BF16), p[0:1], (0, 0, 0)) for p in sums]

        def copy_refs(j, src, land, me, pid, i):
            return src.at[i + 1], land.at[i + 1]

        send, recv, sums, lands, token = _split_start(sums, idx, lands, copy_refs, f"rs_start{len(RS_GROUPS) - 1}",
                                                      ks=RS_KS_CHIPS)
        self.rs.append((names, rows, send, recv, sums, lands, RS_KS_CHIPS))
        return token

    def received(self, group, after):
        names, rows, send, recv, srcs, lands, ks = self.rs[group]
        whole = srcs[0].ndim == 2

        def wait_refs(j, src, land):
            return (src.at[pl.ds(0, rows[j])] if whole else src.at[0]), land.at[0]

        _, lands = _split_wait(srcs, list(range(len(names))), lands, send, recv, list(range(len(names))), wait_refs,
                               after, f"rs_wait{group}", ks=ks)
        return dict(zip(names, lands))


def _pair_sum(a, b, name):
    n, rows, _ = a.shape
    tr = 384 if rows % 384 == 0 else rows

    def kern(a_ref, b_ref, o_ref):
        o_ref[...] = (a_ref[...].astype(F32) + b_ref[...].astype(F32)).astype(BF16)

    blk = pl.BlockSpec((1, tr, D), lambda i, t: (i, t, 0))
    return pl.pallas_call(
        kern, grid=(n, rows // tr), in_specs=[blk, blk], out_specs=blk,
        out_shape=jax.ShapeDtypeStruct(a.shape, BF16), compiler_params=_cparams(2), name=name)(a, b)


def _sum_contributions(r_ref):
    g = r_ref[0].astype(F32)
    for slot in range(1, r_ref.shape[0]):
        g = g + r_ref[slot].astype(F32)
    return g


def _adam_math(g, w, m, v):
    c1 = 1.0 / (1.0 - ADAM_B1 ** ADAM_STEP)
    c2 = 1.0 / (1.0 - ADAM_B2 ** ADAM_STEP)
    mn = ADAM_B1 * m + (1.0 - ADAM_B1) * g
    vn = ADAM_B2 * v + (1.0 - ADAM_B2) * (g * g)
    return -ADAM_LR * ((mn * c1) / (jnp.sqrt(vn * c2) + ADAM_EPS) + ADAM_WD * w), mn, vn


def _adamw(R, w, m, v, *, tr, name, layer=None, prev=None):
    rows, C = w.shape[-2:]
    nprev = 0 if prev is None else 4

    def kern(r_ref, w_ref, m_ref, v_ref, *rest):
        g_out, d_out, m_out, v_out = rest[nprev:]
        g = _sum_contributions(r_ref)
        g_out[...] = g
        d_out[...], m_out[...], v_out[...] = _adam_math(g, w_ref[...], m_ref[...], v_ref[...])

    if layer is None:
        tile = pl.BlockSpec((tr, C), lambda i: (i, 0))
    else:
        tile = pl.BlockSpec((None, tr, C), lambda i: (layer, i, 0))
    shp = jax.ShapeDtypeStruct(w.shape, F32)
    return pl.pallas_call(
        kern, grid=(rows // tr,),
        in_specs=[pl.BlockSpec((R.shape[0], tr, C), lambda i: (0, i, 0)), tile, tile, tile]
        + [pl.BlockSpec(memory_space=pl.ANY)] * nprev,
        out_specs=[tile] * 4, out_shape=[shp] * 4,
        input_output_aliases={4 + k: k for k in range(nprev)},
        compiler_params=_cparams(1), name=name)(R, w, m, v, *(prev or ()))


def _adamw_pool_group(R, w, m, v):
    rows = SEC_ROWS["pg"]

    def kern(r_ref, w_ref, m_ref, v_ref, g_out, d_out, m_out, v_out):
        g = _sum_contributions(r_ref)
        g_out[0] = g
        d_out[0], m_out[0], v_out[0] = _adam_math(g, w_ref[0], m_ref[0], v_ref[0])

    blk = pl.BlockSpec((1, rows, PGD), lambda i: (i, 0, 0))
    shp = jax.ShapeDtypeStruct((POOL_G, rows, PGD), F32)
    return pl.pallas_call(
        kern, grid=(POOL_G,),
        in_specs=[pl.BlockSpec((NDEV, rows, PGD), lambda i: (0, 0, i)), blk, blk, blk],
        out_specs=[blk] * 4, out_shape=[shp] * 4, compiler_params=_cparams(1), name="adamw_pg")(R, w, m, v)


def _grad_sum_t(R, name):
    rows = R.shape[1]
    tr = 128

    def kern(r_ref, o_ref):
        o_ref[...] = _sum_contributions(r_ref).T

    return pl.pallas_call(
        kern, grid=(rows // tr,), in_specs=[pl.BlockSpec((R.shape[0], tr, D), lambda i: (0, i, 0))],
        out_specs=pl.BlockSpec((D, tr), lambda i: (0, i)), out_shape=jax.ShapeDtypeStruct((D, rows), F32),
        compiler_params=_cparams(1), name=name)(R)


def _adam_plain(g, w, m, v, *, tr, name):
    rows, C = w.shape

    def kern(g_ref, w_ref, m_ref, v_ref, d_out, m_out, v_out):
        d_out[...], m_out[...], v_out[...] = _adam_math(g_ref[...], w_ref[...], m_ref[...], v_ref[...])

    tile = pl.BlockSpec((tr, C), lambda i: (i, 0))
    shp = jax.ShapeDtypeStruct((rows, C), F32)
    return pl.pallas_call(
        kern, grid=(rows // tr,), in_specs=[tile] * 4, out_specs=[tile] * 3, out_shape=[shp] * 3,
        compiler_params=_cparams(1), name=name)(g, w, m, v)


def _pack_sections(w_qkv, w_attn_out, w_pool_in, w_pool_group, w_ffn_gate_up, w_ffn_down):
    pg = w_pool_group[0].transpose(1, 0, 2).reshape(SEC_ROWS["pg"], D)
    return {"qkv": w_qkv[0].T, "wo": w_attn_out[0], "wpi": w_pool_in[0], "gu0": w_ffn_gate_up[0].T,
            "gu1": w_ffn_gate_up[1].T, "d0": w_ffn_down[0], "d1": w_ffn_down[1], "pg": pg}


def _vec_pack(attn_norm, ffn_norm, final_norm, pool_norm_sh, pool_scale_sh, me):
    def place(sh):
        return lax.dynamic_update_slice(jnp.zeros((1, D), F32), sh, (0, me * 128))
    return jnp.concatenate([attn_norm, ffn_norm, final_norm.reshape(1, D), place(pool_norm_sh),
                            place(pool_scale_sh), jnp.zeros((2, D), F32)], axis=0)


def _vec_unpack(p, me):
    def take(r):
        return lax.dynamic_slice(p[r:r + 1], (0, me * 128), (1, 128))
    return p[0:1], p[1:3], p[3], take(4), take(5)


def kernel(x, attn_norm, w_qkv, w_attn_out, pool_norm, w_pool_in, w_pool_group, pool_scale, ffn_norm, w_ffn_gate_up, w_ffn_down, final_norm, loss_target, m_attn_norm, m_w_qkv, m_w_attn_out, m_pool_norm, m_w_pool_in, m_w_pool_group, m_pool_scale, m_ffn_norm, m_w_ffn_gate_up, m_w_ffn_down, m_final_norm, v_attn_norm, v_w_qkv, v_w_attn_out, v_pool_norm, v_w_pool_in, v_w_pool_group, v_pool_scale, v_ffn_norm, v_w_ffn_gate_up, v_w_ffn_down, v_final_norm):
    me = 4 * lax.axis_index("x") + 2 * lax.axis_index("y") + lax.axis_index("c")

    pw = _pack_sections(w_qkv, w_attn_out, w_pool_in, w_pool_group, w_ffn_gate_up, w_ffn_down)
    vsh = jnp.concatenate([pool_norm, pool_scale, jnp.zeros((6, 128), F32)], axis=0)

    vg = _bcast_all(vsh, "gather_pool_vectors")
    comm = _Comm({n: pw[n].astype(BF16) for n, _ in SECTIONS}, me, deps=(vg,))
    pool_norm_full = vg[:, 0, :].reshape(1, D)
    pool_scale_full = vg[:, 1, :].reshape(1, D)

    grad_x, vec = _local_step(x[0], loss_target[0], comm, attn_norm, ffn_norm, final_norm,
                              pool_norm_full, pool_scale_full)

    vw = _vec_pack(attn_norm, ffn_norm, final_norm, pool_norm, pool_scale, me)
    vm = _vec_pack(m_attn_norm, m_ffn_norm, m_final_norm, m_pool_norm, m_pool_scale, me)
    vv = _vec_pack(v_attn_norm, v_ffn_norm, v_final_norm, v_pool_norm, v_pool_scale, me)

    gu_t = [jnp.swapaxes(a, 1, 2) for a in (w_ffn_gate_up, m_w_ffn_gate_up, v_w_ffn_gate_up)]
    res = {}
    gu_res, d_res = None, None
    vec_out = None
    after = grad_x
    for group in range(len(RS_GROUPS)):
        if group == len(RS_GROUPS) - 1:
            VR = _bcast_all(vec, "exchange_vector_grads", deps=(after,))
            vec_out = _adamw(VR, vw, vm, vv, tr=8, name="adamw_vec")
            after = vec_out[0]
        for n, R in comm.received(group, after).items():
            if n in ("d0", "d1"):
                d_res = _adamw(R, w_ffn_down, m_w_ffn_down, v_w_ffn_down, tr=352, name=f"adamw_{n}",
                               layer=int(n[1]), prev=d_res)
                after = d_res[0]
            elif n in ("gu0", "gu1"):
                gu_res = _adamw(R, *gu_t, tr=352, name=f"adamw_{n}", layer=int(n[2]), prev=gu_res)
                after = gu_res[0]
            elif n == "pg":
                out = _adamw_pool_group(R, w_pool_group[0], m_w_pool_group[0], v_w_pool_group[0])
                res["pg"] = tuple(a[None] for a in out)
                after = out[0]
            elif n in ("wo", "wpi"):
                w, m, v = ((w_attn_out, m_w_attn_out, v_w_attn_out) if n == "wo"
                           else (w_pool_in, m_w_pool_in, v_w_pool_in))
                res[n] = _adamw(R, w[0], m[0], v[0], tr=128, name=f"adamw_{n}")
                res[n] = tuple(a[None] for a in res[n])
                after = res[n][0]
            else:
                g = _grad_sum_t(R, "grad_sum_qkv")
                out = _adam_plain(g, w_qkv[0], m_w_qkv[0], v_w_qkv[0], tr=256, name="adamw_qkv")
                res["qkv"] = tuple(a[None] for a in (g,) + tuple(out))
                after = out[0]
    res["gu"] = tuple(jnp.swapaxes(a, 1, 2) for a in gu_res)
    res["d"] = tuple(d_res)

    outs = []
    for kind in range(4):
        an, fn, fin, pn, ps = _vec_unpack(vec_out[kind], me)
        outs.append((an, res["qkv"][kind], res["wo"][kind], pn, res["wpi"][kind], res["pg"][kind], ps, fn,
                     res["gu"][kind], res["d"][kind], fin))
    loss = 0.5 * jnp.sum(vec_out[0][6]) / D
    return (loss, grad_x[None]) + outs[0] + outs[1] + outs[2] + outs[3]
```

```python
import jax
import jax.numpy as jnp
from jax import lax
from jax.experimental import pallas as pl
from jax.experimental.pallas import tpu as pltpu

F32 = jnp.float32
BF16 = jnp.bfloat16

D = 1024
NDEV = 8
HEADS = 8
HD = 128
QB = 128
NGROUPS = 3
DILS = (1, 4, 16)
DFF = 2816
HCH = 1408
POOL_G = 4
PGD = 256
RMS_EPS = 1e-6
NEG = -1e30

ADAM_LR = 0.001
ADAM_B1 = 0.9
ADAM_B2 = 0.999
ADAM_EPS = 1e-08
ADAM_WD = 0.01
ADAM_STEP = 10

VMEM_LIMIT = 52 * 1024 * 1024

SECTIONS = (("qkv", 1152), ("wo", 128), ("wpi", 128), ("gu0", 704), ("gu1", 704),
            ("d0", 352), ("d1", 352), ("pg", 32))
LOC_OFF = {}
GLB_OFF = {}
_o = 0
for _n, _r in SECTIONS:
    LOC_OFF[_n] = _o
    GLB_OFF[_n] = _o * NDEV
    _o += _r
PACK_ROWS = _o
GLB_ROWS = PACK_ROWS * NDEV
SEC_ROWS = dict(SECTIONS)


def _cparams(n_grid):
    return pltpu.CompilerParams(dimension_semantics=("arbitrary",) * n_grid, vmem_limit_bytes=VMEM_LIMIT)


def _shard_pos(name, dev):
    n = SEC_ROWS[name]
    if name in ("gu0", "gu1"):
        return ((dev % 4) // 2) * (2 * HCH) + (dev // 4) * HCH + (dev % 2) * n
    return dev * n


def _mm(a, b, *, mode, M, N, K, tm, tn, tk, out_dtype, name, a_off=(0, 0), b_off=(0, 0), res=None,
        out_rows=None, out_off=0, out_prev=None, deps=()):
    nm, nn, nk = M // tm, N // tn, K // tk
    assert nm * tm == M and nn * tn == N and nk * tk == K
    if mode == "nn":
        a_bs, b_bs = (tm, tk), (tk, tn)
        a_ix = lambda i, j, k: (i, k)
        b_ix = lambda i, j, k: (k, j)
        dims = (((1,), (0,)), ((), ()))
    elif mode == "nt":
        a_bs, b_bs = (tm, tk), (tn, tk)
        a_ix = lambda i, j, k: (i, k)
        b_ix = lambda i, j, k: (j, k)
        dims = (((1,), (1,)), ((), ()))
    else:
        a_bs, b_bs = (tk, tm), (tk, tn)
        a_ix = lambda i, j, k: (k, i)
        b_ix = lambda i, j, k: (k, j)
        dims = (((0,), (0,)), ((), ()))

    def spec(bs, ix, off):
        def im(i, j, k):
            r, c = ix(i, j, k)
            return (r + off[0], c + off[1])
        return pl.BlockSpec(bs, im)

    in_specs = [spec(a_bs, a_ix, a_off), spec(b_bs, b_ix, b_off)]
    args = [a, b]
    if res is not None:
        in_specs.append(pl.BlockSpec((tm, tn), lambda i, j, k: (i, j)))
        args.append(res)
    out_shape = jax.ShapeDtypeStruct((M if out_rows is None else out_rows, N), out_dtype)
    out_spec = pl.BlockSpec((tm, tn), lambda i, j, k: (i + out_off, j))
    has_res = res is not None
    extra = list(deps) + ([out_prev] if out_prev is not None else [])
    for dep in extra:
        in_specs.append(pl.BlockSpec(memory_space=pl.ANY))
        args.append(dep)
    o_pos = 2 + int(has_res) + len(extra)
    aliases = {len(args) - 1: 0} if out_prev is not None else {}

    def kern(*refs):
        a_ref, b_ref = refs[0], refs[1]
        res_ref = refs[2] if has_res else None
        o_ref = refs[o_pos]
        av = a_ref[...]
        bv = b_ref[...]
        if av.dtype != BF16:
            av = av.astype(BF16)
        if bv.dtype != BF16:
            bv = bv.astype(BF16)
        part = lax.dot_general(av, bv, dims, preferred_element_type=F32)

        def write(val):
            if has_res:
                val = val + res_ref[...]
            o_ref[...] = val.astype(out_dtype)

        if nk == 1:
            write(part)
        else:
            acc_ref = refs[-1]
            k = pl.program_id(2)

            @pl.when(k == 0)
            def _():
                acc_ref[...] = part

            @pl.when(k > 0)
            def _():
                acc_ref[...] += part

            @pl.when(k == nk - 1)
            def _():
                write(acc_ref[...])

    scratch = [pltpu.VMEM((tm, tn), F32)] if nk > 1 else []
    return pl.pallas_call(
        kern, grid=(nm, nn, nk), in_specs=in_specs, out_specs=out_spec, out_shape=out_shape,
        scratch_shapes=scratch, input_output_aliases=aliases, compiler_params=_cparams(3), name=name)(*args)


def _mm_rms_bwd(a, b, x, g, dres, *, mode, M, K, tm, name, b_off=(0, 0), deps=()):
    nd = len(deps)
    b_bs = (K, D) if mode == "nn" else (D, K)
    dims = (((1,), (0,)), ((), ())) if mode == "nn" else (((1,), (1,)), ((), ()))

    def kern(a_ref, b_ref, x_ref, g_ref, dres_ref, *rest):
        dx_ref, dg_ref = rest[nd:]
        i = pl.program_id(0)
        av = a_ref[...]
        if av.dtype != BF16:
            av = av.astype(BF16)
        dhv = lax.dot_general(av, b_ref[...], dims, preferred_element_type=F32)
        xv = x_ref[...]
        r = lax.rsqrt(jnp.mean(xv * xv, axis=-1, keepdims=True) + RMS_EPS)
        xhat = xv * r
        gy = dhv * g_ref[...]
        dx_ref[...] = dres_ref[...] + r * (gy - xhat * jnp.mean(gy * xhat, axis=-1, keepdims=True))
        part = jnp.sum(dhv * xhat, axis=0, keepdims=True)

        @pl.when(i == 0)
        def _():
            dg_ref[...] = part

        @pl.when(i > 0)
        def _():
            dg_ref[...] += part

    row = pl.BlockSpec((tm, D), lambda i: (i, 0))
    vec = pl.BlockSpec((1, D), lambda i: (0, 0))
    return pl.pallas_call(
        kern, grid=(M // tm,),
        in_specs=[pl.BlockSpec((tm, K), lambda i: (i, 0)),
                  pl.BlockSpec(b_bs, lambda i: b_off, pipeline_mode=pl.Buffered(1)), row, vec, row]
        + [pl.BlockSpec(memory_space=pl.ANY)] * nd,
        out_specs=[row, vec],
        out_shape=[jax.ShapeDtypeStruct((M, D), F32), jax.ShapeDtypeStruct((1, D), F32)],
        compiler_params=_cparams(1), name=name)(a, b, x, g, dres, *deps)


def _mm_res_norm(a, b, res, g, *, K, tm, name, b_off=(0, 0), tgt=None):
    M = a.shape[0]
    head = tgt is not None

    def kern(a_ref, b_ref, res_ref, g_ref, *rest):
        xv = res_ref[...] + jnp.dot(a_ref[...], b_ref[...], preferred_element_type=F32)
        gv = g_ref[...]
        r = lax.rsqrt(jnp.mean(xv * xv, axis=-1, keepdims=True) + RMS_EPS)
        xhat = xv * r
        if not head:
            xo_ref, h_ref = rest
            xo_ref[...] = xv
            h_ref[...] = (xhat * gv).astype(BF16)
            return
        t_ref, dx_ref, dg_ref, ls_ref = rest
        i = pl.program_id(0)
        e = xhat * gv - t_ref[...]
        dy = e * (1.0 / D)
        gy = dy * gv
        dx_ref[...] = r * (gy - xhat * jnp.mean(gy * xhat, axis=-1, keepdims=True))
        dgp = jnp.sum(dy * xhat, axis=0, keepdims=True)
        lsp = jnp.sum(e * e, axis=0, keepdims=True)

        @pl.when(i == 0)
        def _():
            dg_ref[...] = dgp
            ls_ref[...] = lsp

        @pl.when(i > 0)
        def _():
            dg_ref[...] += dgp
            ls_ref[...] += lsp

    row = pl.BlockSpec((tm, D), lambda i: (i, 0))
    vec = pl.BlockSpec((1, D), lambda i: (0, 0))
    in_specs = [pl.BlockSpec((tm, K), lambda i: (i, 0)),
                pl.BlockSpec((K, D), lambda i: b_off, pipeline_mode=pl.Buffered(1)), row, vec]
    if head:
        return pl.pallas_call(
            kern, grid=(M // tm,), in_specs=in_specs + [row], out_specs=[row, vec, vec],
            out_shape=[jax.ShapeDtypeStruct((M, D), F32), jax.ShapeDtypeStruct((1, D), F32),
                       jax.ShapeDtypeStruct((1, D), F32)],
            compiler_params=_cparams(1), name=name)(a, b, res, g, tgt)
    return pl.pallas_call(
        kern, grid=(M // tm,), in_specs=in_specs, out_specs=[row, row],
        out_shape=[jax.ShapeDtypeStruct((M, D), F32), jax.ShapeDtypeStruct((M, D), BF16)],
        compiler_params=_cparams(1), name=name)(a, b, res, g)


def _rms_fwd(x, g, name, deps=()):
    S = x.shape[0]
    tr = 512

    def kern(x_ref, g_ref, *rest):
        h_ref = rest[-1]
        xv = x_ref[...]
        r = lax.rsqrt(jnp.mean(xv * xv, axis=-1, keepdims=True) + RMS_EPS)
        h_ref[...] = (xv * r * g_ref[...]).astype(BF16)

    return pl.pallas_call(
        kern, grid=(S // tr,),
        in_specs=[pl.BlockSpec((tr, D), lambda i: (i, 0)), pl.BlockSpec((1, D), lambda i: (0, 0))]
        + [pl.BlockSpec(memory_space=pl.ANY)] * len(deps),
        out_specs=pl.BlockSpec((tr, D), lambda i: (i, 0)),
        out_shape=jax.ShapeDtypeStruct((S, D), BF16), compiler_params=_cparams(1), name=name)(x, g, *deps)


def _chunks_put(scr, val):
    for c in range(scr.shape[0]):
        scr[c] = val[:, c * 128:(c + 1) * 128]


def _chunks_get(scr):
    return jnp.concatenate([scr[c] for c in range(scr.shape[0])], axis=1)


def _chunks_rows(scr, r, n, dil):
    return jnp.concatenate([scr.at[c][pl.ds(r, n, stride=dil), :] for c in range(scr.shape[0])], axis=1)


def _chunks_add_rows(scr, val, r, n, dil, accumulate):
    for c in range(scr.shape[0]):
        rows = pl.ds(r, n, stride=dil)
        piece = val[:, c * 128:(c + 1) * 128]
        tile = scr.at[c]
        tile[rows, :] = tile[rows, :] + piece if accumulate else piece


def _rms_fwd_folded(x, g, name, deps=()):
    S = x.shape[0]
    tr = 512
    dils = DILS[1:]

    def kern(x_ref, g_ref, *rest):
        outs, scr = rest[len(deps):-1], rest[-1]
        xv = x_ref[...]
        r = lax.rsqrt(jnp.mean(xv * xv, axis=-1, keepdims=True) + RMS_EPS)
        h = (xv * r * g_ref[...]).astype(BF16)
        outs[0][...] = h
        _chunks_put(scr, h.astype(F32))
        for o_ref, dil in zip(outs[1:], dils):
            for res in range(dil):
                o_ref[res] = _chunks_rows(scr, res, tr // dil, dil).astype(BF16)

    return pl.pallas_call(
        kern, grid=(S // tr,),
        in_specs=[pl.BlockSpec((tr, D), lambda i: (i, 0)), pl.BlockSpec((1, D), lambda i: (0, 0))]
        + [pl.BlockSpec(memory_space=pl.ANY)] * len(deps),
        out_specs=[pl.BlockSpec((tr, D), lambda i: (i, 0))]
        + [pl.BlockSpec((dil, tr // dil, D), lambda i: (0, i, 0)) for dil in dils],
        out_shape=[jax.ShapeDtypeStruct((S, D), BF16)]
        + [jax.ShapeDtypeStruct((dil, S // dil, D), BF16) for dil in dils],
        scratch_shapes=[pltpu.VMEM((D // 128, tr, 128), F32)],
        compiler_params=_cparams(1), name=name)(x, g, *deps)


def _rms_bwd(dh, x, g, dres, name, folded=()):
    S = x.shape[0]
    tr = 512
    nf = len(folded)

    def kern(dh_ref, *rest):
        f_refs = rest[:nf]
        x_ref, g_ref, dres_ref, dx_ref, dg_ref = rest[nf:nf + 5]
        i = pl.program_id(0)
        xv = x_ref[...]
        if nf:
            acc_ref = rest[nf + 5]
            _chunks_put(acc_ref, dh_ref[...].astype(F32))
            for f_ref in f_refs:
                dil = f_ref.shape[0]
                for res in range(dil):
                    _chunks_add_rows(acc_ref, f_ref[res], res, tr // dil, dil, True)
            dhv = _chunks_get(acc_ref)
        else:
            dhv = dh_ref[...].astype(F32)
        r = lax.rsqrt(jnp.mean(xv * xv, axis=-1, keepdims=True) + RMS_EPS)
        xhat = xv * r
        gy = dhv * g_ref[...]
        dx_ref[...] = dres_ref[...] + r * (gy - xhat * jnp.mean(gy * xhat, axis=-1, keepdims=True))
        part = jnp.sum(dhv * xhat, axis=0, keepdims=True)

        @pl.when(i == 0)
        def _():
            dg_ref[...] = part

        @pl.when(i > 0)
        def _():
            dg_ref[...] += part

    row = pl.BlockSpec((tr, D), lambda i: (i, 0))
    vec = pl.BlockSpec((1, D), lambda i: (0, 0))
    fspecs = [pl.BlockSpec((f.shape[0], tr // f.shape[0], D), lambda i: (0, i, 0)) for f in folded]
    return pl.pallas_call(
        kern, grid=(S // tr,), in_specs=[row] + fspecs + [row, vec, row], out_specs=[row, vec],
        out_shape=[jax.ShapeDtypeStruct((S, D), F32), jax.ShapeDtypeStruct((1, D), F32)],
        scratch_shapes=[pltpu.VMEM((D // 128, tr, 128), F32)] if nf else [],
        compiler_params=_cparams(1), name=name)(dh, *folded, x, g, dres)


def _ffn_up(h, G, name):
    S = h.shape[0]
    tm = 512
    nj = DFF // HCH

    def kern(h_ref, w_ref, gu_ref, act_ref):
        gu = lax.dot_general(h_ref[...], w_ref[...], (((1,), (1,)), ((), ())), preferred_element_type=F32)
        gu_ref[...] = gu.astype(BF16)
        gate = gu[:, :HCH]
        up = gu[:, HCH:]
        act_ref[...] = (gate * jax.nn.sigmoid(gate) * up).astype(BF16)

    return pl.pallas_call(
        kern, grid=(nj, S // tm),
        in_specs=[pl.BlockSpec((tm, D), lambda j, i: (i, 0)),
                  pl.BlockSpec((2 * HCH, D), lambda j, i: (j, 0))],
        out_specs=[pl.BlockSpec((tm, 2 * HCH), lambda j, i: (i, j)),
                   pl.BlockSpec((tm, HCH), lambda j, i: (i, j))],
        out_shape=[jax.ShapeDtypeStruct((S, 2 * DFF), BF16), jax.ShapeDtypeStruct((S, DFF), BF16)],
        compiler_params=_cparams(2), name=name)(h, G)


def _ffn_down_bwd(dx, G, gu, name):
    S = dx.shape[0]
    tm = 512
    nj = DFF // HCH

    def kern(dx_ref, w_ref, gu_ref, o_ref):
        dact = lax.dot_general(dx_ref[...].astype(BF16), w_ref[...], (((1,), (1,)), ((), ())),
                               preferred_element_type=F32)
        gate = gu_ref[:, :HCH].astype(F32)
        up = gu_ref[:, HCH:].astype(F32)
        sig = jax.nn.sigmoid(gate)
        silu = gate * sig
        o_ref[:, :HCH] = (dact * up * (sig * (1.0 + gate * (1.0 - sig)))).astype(BF16)
        o_ref[:, HCH:] = (dact * silu).astype(BF16)

    return pl.pallas_call(
        kern, grid=(nj, S // tm),
        in_specs=[pl.BlockSpec((tm, D), lambda j, i: (i, 0)),
                  pl.BlockSpec((HCH, D), lambda j, i: (j, 0)),
                  pl.BlockSpec((tm, 2 * HCH), lambda j, i: (i, j))],
        out_specs=pl.BlockSpec((tm, 2 * HCH), lambda j, i: (i, j)),
        out_shape=jax.ShapeDtypeStruct((S, 2 * DFF), BF16),
        compiler_params=_cparams(2), name=name)(dx, G, gu)


def _trail(u, *, backward, name):
    S = u.shape[0]

    def kern(u_ref, o_ref):
        g = pl.program_id(0)
        for grp in range(POOL_G):
            @pl.when(g == grp)
            def _(grp=grp):
                uv = u_ref[...].astype(F32)
                row = lax.broadcasted_iota(jnp.int32, uv.shape, 0)
                cnt = jnp.minimum(row + 1, 2 << grp).astype(F32)
                s = uv / cnt if backward else uv
                for k in (1, 2, 4, 8)[:grp + 1]:
                    if backward:
                        sh = jnp.where(row < S - k, pltpu.roll(s, S - k, 0), 0.0)
                    else:
                        sh = jnp.where(row >= k, pltpu.roll(s, k, 0), 0.0)
                    s = s + sh
                if backward:
                    o_ref[...] = (s - uv).astype(BF16)
                else:
                    o_ref[...] = (s / cnt - uv).astype(BF16)

    blk = pl.BlockSpec((S, PGD), lambda g: (0, g))
    return pl.pallas_call(
        kern, grid=(POOL_G,), in_specs=[blk], out_specs=blk,
        out_shape=jax.ShapeDtypeStruct((S, D), BF16), compiler_params=_cparams(1), name=name)(u)


def _pool_out(yd, G, scale, xres):
    S = yd.shape[0]
    tm = min(S, 4096)

    def kern(y_ref, w_ref, s_ref, x_ref, o_ref):
        z = jnp.dot(y_ref[...], w_ref[...], preferred_element_type=F32)
        o_ref[...] = x_ref[...] + z * s_ref[...]

    tile = pl.BlockSpec((tm, PGD), lambda i, g: (i, g))
    return pl.pallas_call(
        kern, grid=(S // tm, POOL_G),
        in_specs=[tile, pl.BlockSpec((PGD, PGD), lambda i, g: (0, g)),
                  pl.BlockSpec((1, PGD), lambda i, g: (0, g)), tile],
        out_specs=tile, out_shape=jax.ShapeDtypeStruct((S, D), F32),
        compiler_params=_cparams(2), name="pool_out")(yd, G, scale, xres)


def _pool_out_bwd(dz, yd, G, scale):
    S = yd.shape[0]
    tm = min(S, 4096)
    ni = S // tm

    def kern(dz_ref, y_ref, w_ref, s_ref, dy_ref, ds_ref, dw_ref, acc_ref):
        i = pl.program_id(1)
        dzv = dz_ref[...]
        yv = y_ref[...]
        wv = w_ref[...]
        zraw = jnp.dot(yv, wv, preferred_element_type=F32)
        dsp = jnp.sum(dzv * zraw, axis=0, keepdims=True)
        dzr = (dzv * s_ref[...]).astype(BF16)
        dy_ref[...] = lax.dot_general(dzr, wv, (((1,), (1,)), ((), ())), preferred_element_type=F32)
        dwp = lax.dot_general(yv, dzr, (((0,), (0,)), ((), ())), preferred_element_type=F32)

        @pl.when(i == 0)
        def _():
            ds_ref[...] = dsp
            acc_ref[...] = dwp

        @pl.when(i > 0)
        def _():
            ds_ref[...] += dsp
            acc_ref[...] += dwp

        @pl.when(i == ni - 1)
        def _():
            dw_ref[...] = acc_ref[...].astype(BF16)

    tile = pl.BlockSpec((tm, PGD), lambda g, i: (i, g))
    return pl.pallas_call(
        kern, grid=(POOL_G, ni),
        in_specs=[tile, tile, pl.BlockSpec((PGD, PGD), lambda g, i: (0, g)),
                  pl.BlockSpec((1, PGD), lambda g, i: (0, g))],
        out_specs=[tile, pl.BlockSpec((1, PGD), lambda g, i: (0, g)),
                   pl.BlockSpec((PGD, PGD), lambda g, i: (0, g))],
        out_shape=[jax.ShapeDtypeStruct((S, D), F32), jax.ShapeDtypeStruct((1, D), F32),
                   jax.ShapeDtypeStruct((PGD, D), BF16)],
        scratch_shapes=[pltpu.VMEM((PGD, PGD), F32)],
        compiler_params=_cparams(2), name="pool_out_bwd")(dz, yd, G, scale)


def _bias_table():
    qi = jnp.arange(QB)[:, None]
    ki = jnp.arange(2 * QB)[None, :]
    delta = QB + qi - ki
    inband = (delta >= 0) & (delta <= QB)
    n = NGROUPS * HEADS
    slopes = jnp.exp2(-8.0 * jnp.arange(1, n + 1, dtype=F32) / n).reshape(NGROUPS, HEADS)
    dil = jnp.asarray(DILS, F32)
    bias = -slopes[:, :, None, None] * (delta.astype(F32)[None, None] * dil[:, None, None, None])
    return jnp.where(inband[None, None], bias, NEG)


def _attn_fwd(qkv_f, bias, nb, name):
    S = qkv_f.shape[0]
    nblk = S // QB
    scale = HD ** -0.5

    def kern(q_ref, kc_ref, kp_ref, vc_ref, vp_ref, b_ref, o_ref, l_ref, s_scr, p_scr, r_scr):
        b = pl.program_id(0)
        has_prev = jnp.bitwise_and(b, nb - 1) != 0
        col = lax.broadcasted_iota(jnp.int32, (QB, 2 * QB), 1)
        dead = jnp.logical_and(col < QB, jnp.logical_not(has_prev))
        lane = lax.broadcasted_iota(jnp.int32, (QB, HD), 1)
        lse_all = jnp.zeros((QB, HD), F32)
        for h in range(HEADS):
            sl = slice(h * HD, (h + 1) * HD)
            kk = jnp.concatenate([kp_ref[:, sl], kc_ref[:, sl]], axis=0)
            s_scr[h] = lax.dot_general(q_ref[:, sl], kk, (((1,), (1,)), ((), ())), preferred_element_type=F32)
        for h in range(HEADS):
            s = s_scr[h] * scale + b_ref[h]
            s = jnp.where(dead, NEG, s)
            m = jnp.max(s, axis=-1, keepdims=True)
            p = jnp.exp(s - m)
            den = jnp.sum(p, axis=-1, keepdims=True)
            p_scr[h] = p.astype(BF16)
            r_scr[h] = jnp.broadcast_to(1.0 / den, (QB, HD))
            lse_all = jnp.where(lane == h, m + jnp.log(den), lse_all)
        for h in range(HEADS):
            sl = slice(h * HD, (h + 1) * HD)
            vv = jnp.concatenate([vp_ref[:, sl], vc_ref[:, sl]], axis=0)
            o = jnp.dot(p_scr[h], vv, preferred_element_type=F32) * r_scr[h]
            o_ref[:, sl] = o.astype(BF16)
        l_ref[...] = lse_all

    def blk(colblk, prev):
        if prev:
            return pl.BlockSpec((QB, D), lambda b: (jnp.maximum(b - 1, 0), colblk))
        return pl.BlockSpec((QB, D), lambda b: (b, colblk))

    return pl.pallas_call(
        kern, grid=(nblk,),
        in_specs=[blk(0, False), blk(1, False), blk(1, True), blk(2, False), blk(2, True),
                  pl.BlockSpec((HEADS, QB, 2 * QB), lambda b: (0, 0, 0))],
        out_specs=[pl.BlockSpec((QB, D), lambda b: (b, 0)), pl.BlockSpec((QB, HD), lambda b: (b, 0))],
        out_shape=[jax.ShapeDtypeStruct((S, D), BF16), jax.ShapeDtypeStruct((S, HD), F32)],
        scratch_shapes=[pltpu.VMEM((HEADS, QB, 2 * QB), F32), pltpu.VMEM((HEADS, QB, 2 * QB), BF16),
                        pltpu.VMEM((HEADS, QB, HD), F32)],
        compiler_params=_cparams(1), name=name)(qkv_f, qkv_f, qkv_f, qkv_f, qkv_f, bias)


def _natural(ref, scr, tm):
    dil = ref.shape[0]
    for res in range(dil):
        _chunks_add_rows(scr, ref[res].astype(F32), res, tm // dil, dil, False)
    return _chunks_get(scr)


def _attn_merge(os, lses):
    S = os[0].shape[0]
    tm = 512

    def kern(o0, o1, o2, l0, l1, l2, om_ref, lm_ref, ls1, ls2, os1, os2):
        la = l0[...]
        lb = _natural(l1, ls1, tm)
        lc = _natural(l2, ls2, tm)
        m = jnp.maximum(jnp.maximum(la, lb), lc)
        e0, e1, e2 = jnp.exp(la - m), jnp.exp(lb - m), jnp.exp(lc - m)
        tot = e0 + e1 + e2
        lm_ref[...] = m + jnp.log(tot)
        w0, w1, w2 = e0 / tot, e1 / tot, e2 / tot
        for res in range(o1.shape[0]):
            _chunks_add_rows(os1, o1[res].astype(F32), res, tm // o1.shape[0], o1.shape[0], False)
        for res in range(o2.shape[0]):
            _chunks_add_rows(os2, o2[res].astype(F32), res, tm // o2.shape[0], o2.shape[0], False)
        for h in range(HEADS):
            sl = slice(h * HD, (h + 1) * HD)
            acc = w0[:, h:h + 1] * o0[:, sl].astype(F32) + w1[:, h:h + 1] * os1[h] + w2[:, h:h + 1] * os2[h]
            om_ref[:, sl] = acc.astype(BF16)

    def spec(a, c):
        if a.ndim == 2:
            return pl.BlockSpec((tm, c), lambda i: (i, 0))
        return pl.BlockSpec((a.shape[0], tm // a.shape[0], c), lambda i: (0, i, 0))

    return pl.pallas_call(
        kern, grid=(S // tm,),
        in_specs=[spec(a, D) for a in os] + [spec(a, HD) for a in lses],
        out_specs=[pl.BlockSpec((tm, D), lambda i: (i, 0)), pl.BlockSpec((tm, HD), lambda i: (i, 0))],
        out_shape=[jax.ShapeDtypeStruct((S, D), BF16), jax.ShapeDtypeStruct((S, HD), F32)],
        scratch_shapes=[pltpu.VMEM((1, tm, HD), F32), pltpu.VMEM((1, tm, HD), F32),
                        pltpu.VMEM((HEADS, tm, HD), F32), pltpu.VMEM((HEADS, tm, HD), F32)],
        compiler_params=_cparams(1), name="attn_merge")(*os, *lses)


def _attn_bwd_prep(do, o, lse):
    S = o.shape[0]
    tm = 512
    dils = DILS[1:]

    def kern(do_ref, o_ref, l_ref, *rest):
        do_outs, l_outs, d_outs = rest[0:3], rest[3:5], rest[5:8]
        do_scr, l_scr, d_scr = rest[8:11]
        lane = lax.broadcasted_iota(jnp.int32, (tm, HD), 1)
        acc = jnp.zeros((tm, HD), F32)
        for h in range(HEADS):
            sl = slice(h * HD, (h + 1) * HD)
            prod = do_ref[:, sl] * o_ref[:, sl].astype(F32)
            acc = jnp.where(lane == h, jnp.sum(prod, axis=-1, keepdims=True), acc)
        d_scr[0] = acc
        l_scr[0] = l_ref[...]
        _chunks_put(do_scr, do_ref[...])
        do_outs[0][...] = do_ref[...].astype(BF16)
        d_outs[0][...] = acc
        for j, dil in enumerate(dils):
            for res in range(dil):
                n = tm // dil
                do_outs[1 + j][res] = _chunks_rows(do_scr, res, n, dil).astype(BF16)
                l_outs[j][res] = _chunks_rows(l_scr, res, n, dil)
                d_outs[1 + j][res] = _chunks_rows(d_scr, res, n, dil)

    def nat(c):
        return pl.BlockSpec((tm, c), lambda i: (i, 0))

    def fol(dil, c):
        return pl.BlockSpec((dil, tm // dil, c), lambda i: (0, i, 0))

    def shapes(c, dt, with_natural):
        first = [jax.ShapeDtypeStruct((S, c), dt)] if with_natural else []
        return first + [jax.ShapeDtypeStruct((dil, S // dil, c), dt) for dil in dils]

    outs = pl.pallas_call(
        kern, grid=(S // tm,), in_specs=[nat(D), nat(D), nat(HD)],
        out_specs=[nat(D)] + [fol(dil, D) for dil in dils] + [fol(dil, HD) for dil in dils]
        + [nat(HD)] + [fol(dil, HD) for dil in dils],
        out_shape=shapes(D, BF16, True) + shapes(HD, F32, False) + shapes(HD, F32, True),
        scratch_shapes=[pltpu.VMEM((HEADS, tm, HD), F32), pltpu.VMEM((1, tm, HD), F32), pltpu.VMEM((1, tm, HD), F32)],
        compiler_params=_cparams(1), name="attn_bwd_prep")(do, o, lse)
    return outs[0:3], [lse] + list(outs[3:5]), outs[5:8]


def _attn_bwd(qkv_f, do_f, lse_f, delta_f, bias, nb, name):
    S = qkv_f.shape[0]
    nblk = S // QB
    scale = HD ** -0.5

    def kern(q_ref, kc_ref, kp_ref, vc_ref, vp_ref, do_ref, l_ref, d_ref, b_ref, out_ref, dq_c, dk_c, dv_c,
             s_scr, dp_scr, ds_scr, p_scr):
        b = pl.program_id(0)

        @pl.when(b == 0)
        def _():
            dq_c[...] = jnp.zeros_like(dq_c)
            dk_c[...] = jnp.zeros_like(dk_c)
            dv_c[...] = jnp.zeros_like(dv_c)

        @pl.when(b == nblk)
        def _():
            out_ref[:, 0:D] = dq_c[...].astype(BF16)
            out_ref[:, D:2 * D] = dk_c[...].astype(BF16)
            out_ref[:, 2 * D:3 * D] = dv_c[...].astype(BF16)

        @pl.when(b < nblk)
        def _():
            has_prev = jnp.bitwise_and(b, nb - 1) != 0
            col = lax.broadcasted_iota(jnp.int32, (QB, 2 * QB), 1)
            dead = jnp.logical_and(col < QB, jnp.logical_not(has_prev))
            out_ref[:, 0:D] = dq_c[...].astype(BF16)
            lv = l_ref[...]
            dv_ = d_ref[...]
            for h in range(HEADS):
                sl = slice(h * HD, (h + 1) * HD)
                kk = jnp.concatenate([kp_ref[:, sl], kc_ref[:, sl]], axis=0)
                vv = jnp.concatenate([vp_ref[:, sl], vc_ref[:, sl]], axis=0)
                s_scr[h] = lax.dot_general(q_ref[:, sl], kk, (((1,), (1,)), ((), ())), preferred_element_type=F32)
                dp_scr[h] = lax.dot_general(do_ref[:, sl], vv, (((1,), (1,)), ((), ())),
                                            preferred_element_type=F32)
            for h in range(HEADS):
                s = s_scr[h] * scale + b_ref[h]
                s = jnp.where(dead, NEG, s)
                p = jnp.exp(s - lv[:, h:h + 1])
                ds_scr[h] = (p * (dp_scr[h] - dv_[:, h:h + 1]) * scale).astype(BF16)
                p_scr[h] = p.astype(BF16)
            for h in range(HEADS):
                sl = slice(h * HD, (h + 1) * HD)
                kk = jnp.concatenate([kp_ref[:, sl], kc_ref[:, sl]], axis=0)
                ds = ds_scr[h]
                dq_c[:, sl] = jnp.dot(ds, kk, preferred_element_type=F32)
                dkk = lax.dot_general(ds, q_ref[:, sl], (((0,), (0,)), ((), ())), preferred_element_type=F32)
                dvv = lax.dot_general(p_scr[h], do_ref[:, sl], (((0,), (0,)), ((), ())),
                                      preferred_element_type=F32)
                out_ref[:, D + h * HD:D + (h + 1) * HD] = (dk_c[:, sl] + dkk[:QB]).astype(BF16)
                out_ref[:, 2 * D + h * HD:2 * D + (h + 1) * HD] = (dv_c[:, sl] + dvv[:QB]).astype(BF16)
                dk_c[:, sl] = dkk[QB:]
                dv_c[:, sl] = dvv[QB:]

    last = nblk - 1

    def blk(colblk, prev):
        if prev:
            return pl.BlockSpec((QB, D), lambda b: (jnp.maximum(jnp.minimum(b, last) - 1, 0), colblk))
        return pl.BlockSpec((QB, D), lambda b: (jnp.minimum(b, last), colblk))

    stat = pl.BlockSpec((QB, HD), lambda b: (jnp.minimum(b, last), 0))
    return pl.pallas_call(
        kern, grid=(nblk + 1,),
        in_specs=[blk(0, False), blk(1, False), blk(1, True), blk(2, False), blk(2, True),
                  pl.BlockSpec((QB, D), lambda b: (jnp.minimum(b, last), 0)), stat, stat,
                  pl.BlockSpec((HEADS, QB, 2 * QB), lambda b: (0, 0, 0))],
        out_specs=pl.BlockSpec((QB, 3 * D), lambda b: (jnp.maximum(b - 1, 0), 0)),
        out_shape=jax.ShapeDtypeStruct((S, 3 * D), BF16),
        scratch_shapes=[pltpu.VMEM((QB, D), F32), pltpu.VMEM((QB, D), F32), pltpu.VMEM((QB, D), F32),
                        pltpu.VMEM((HEADS, QB, 2 * QB), F32), pltpu.VMEM((HEADS, QB, 2 * QB), F32),
                        pltpu.VMEM((HEADS, QB, 2 * QB), BF16), pltpu.VMEM((HEADS, QB, 2 * QB), BF16)],
        compiler_params=_cparams(1), name=name)(qkv_f, qkv_f, qkv_f, qkv_f, qkv_f, do_f, lse_f, delta_f, bias)


def _local_step(x, tgt, comm, attn_norm, ffn_norm, final_norm, pool_norm, pool_scale):
    S = x.shape[0]
    bias = _bias_table()
    g_attn = attn_norm.reshape(1, D)
    g_f0 = ffn_norm[0:1]
    g_f1 = ffn_norm[1:2]
    g_fin = final_norm.reshape(1, D)
    W = {}

    def ffn_fwd(xin, h, l, next_gain, target=None):
        gu, act = _ffn_up(h, W[f"gu{l}"], f"ffn_up{l}")
        return gu, act, _mm_res_norm(act, W[f"d{l}"], xin, next_gain, K=DFF, tm=512, tgt=target,
                                     name=f"ffn_down{l}")

    def ffn_bwd(dxo, xin, gain, h, gu, act, l, rs_group):
        dgu = _ffn_down_bwd(dxo, W[f"d{l}"], gu, f"ffn_down_bwd{l}")
        gw_d = _mm(act, dxo, mode="tn", M=DFF, N=D, K=S, tm=HCH, tn=D, tk=2048, out_dtype=BF16, name=f"gw_d{l}")
        gw_gu = _mm(dgu, h, mode="tn", M=2 * DFF, N=D, K=S, tm=HCH, tn=D, tk=2048, out_dtype=BF16, name=f"gw_gu{l}")
        token = comm.send_grads(rs_group, {f"d{l}": gw_d, f"gu{l}": gw_gu})
        return _mm_rms_bwd(dgu, W[f"gu{l}"], xin, gain, dxo, mode="nn", M=S, K=2 * DFF, tm=512, deps=(token,),
                           name=f"ffn_up_bwd{l}")

    nbs = [S // QB // dil for dil in DILS]
    hf = _rms_fwd_folded(x, g_attn, "rms_attn", deps=comm.ag_tokens)
    hf = [h.reshape(S, D) for h in hf]
    W.update(comm.weights(0, hf[0]))
    qkv_f, o_f, lse_f = [], [], []
    for g, dil in enumerate(DILS):
        qkv_f.append(_mm(hf[g], W["qkv"], mode="nt", M=S, N=3 * D, K=D, tm=2048, tn=1024, tk=D, out_dtype=BF16,
                         b_off=(3 * g, 0), name=f"qkv_proj{g}"))
        og, lg = _attn_fwd(qkv_f[g], bias[g], nbs[g], f"attn_fwd{g}")
        o_f.append(og if dil == 1 else og.reshape(dil, S // dil, D))
        lse_f.append(lg if dil == 1 else lg.reshape(dil, S // dil, HD))
    o, lse = _attn_merge(o_f, lse_f)
    W.update(comm.weights(1, o))
    x1, h1 = _mm_res_norm(o, W["wo"], x, g_f0, K=D, tm=1024, name="attn_out")
    gu0, act0, (x2, h2) = ffn_fwd(x1, h1, 0, pool_norm)

    W.update(comm.weights(2, x2))
    u = _mm(h2, W["wpi"], mode="nn", M=S, N=D, K=D, tm=1024, tn=D, tk=D, out_dtype=F32, name="pool_in")
    yd = _trail(u, backward=False, name="trail_fwd")
    x3 = _pool_out(yd, W["pg"], pool_scale, x2)
    h3 = _rms_fwd(x3, g_f1, "rms_ffn1")
    gu1, act1, (dx4, d_fin, lossvec) = ffn_fwd(x3, h3, 1, g_fin, target=tgt)

    dx3, d_f1 = ffn_bwd(dx4, x3, g_f1, h3, gu1, act1, 1, 0)
    dyd, d_scale, gw_pg = _pool_out_bwd(dx3, yd, W["pg"], pool_scale)
    du = _trail(dyd, backward=True, name="trail_bwd")
    gw_pi = _mm(h2, du, mode="tn", M=D, N=D, K=S, tm=D, tn=D, tk=2048, out_dtype=BF16, name="gw_pi")
    token = comm.send_grads(1, {"pg": gw_pg, "wpi": gw_pi})
    dx2, d_pool = _mm_rms_bwd(du, W["wpi"], x2, pool_norm, dx3, mode="nt", M=S, K=D, tm=1024, deps=(token,),
                              name="pool_in_bwd")
    dx1, d_f0 = ffn_bwd(dx2, x1, g_f0, h1, gu0, act0, 0, 2)

    gw_o = _mm(o, dx1, mode="tn", M=D, N=D, K=S, tm=D, tn=D, tk=2048, out_dtype=BF16, name="gw_o")
    do = _mm(dx1, W["wo"], mode="nt", M=S, N=D, K=D, tm=1024, tn=D, tk=D, out_dtype=F32, name="attn_out_bwd")
    do_f, lse_ff, delta_f = _attn_bwd_prep(do, o, lse)
    dqkv_f, gw_qkv = [], None
    for g in range(NGROUPS):
        dqkv_f.append(_attn_bwd(qkv_f[g], do_f[g].reshape(S, D), lse_ff[g].reshape(S, HD),
                                delta_f[g].reshape(S, HD), bias[g], nbs[g], f"attn_bwd{g}"))
        gw_qkv = _mm(dqkv_f[g], hf[g], mode="tn", M=3 * D, N=D, K=S, tm=1024, tn=D, tk=2048, out_dtype=BF16,
                     out_rows=NGROUPS * 3 * D, out_off=3 * g, out_prev=gw_qkv, name=f"gw_qkv{g}")
    token = comm.send_grads_pairwise({"wo": gw_o, "qkv": gw_qkv})
    dh0_f = [None] * NGROUPS
    for g in reversed(range(NGROUPS)):
        dh0_f[g] = _mm(dqkv_f[g], W["qkv"], mode="nn", M=S, N=D, K=3 * D, tm=1024, tn=D, tk=3 * D, out_dtype=F32,
                       b_off=(g, 0), deps=(token,), name=f"qkv_proj_bwd{g}")
        if g == NGROUPS - 1:
            token = comm.pass_grads(dh0_f[g])
    folded = [dh0_f[g].reshape(dil, S // dil, D) for g, dil in enumerate(DILS) if dil > 1]
    grad_x, d_attn = _rms_bwd(dh0_f[0], x, g_attn, dx1, "rms_attn_bwd", folded=folded)

    vec = jnp.concatenate([d_attn, d_f0, d_f1, d_fin, d_pool, d_scale, lossvec, jnp.zeros((1, D), F32)], axis=0)
    return grad_x, vec


def _mesh_pos():
    x, y, c = lax.axis_index("x"), lax.axis_index("y"), lax.axis_index("c")
    return x, y, c, 4 * x + 2 * y + c


def _peer(x, y, c, k):
    kx, ky, kc = (k >> 2) & 1, (k >> 1) & 1, k & 1
    px = 1 - x if kx else x
    py = 1 - y if ky else y
    pc = 1 - c if kc else c
    return (px, py, pc), 4 * px + 2 * py + pc


ANY = pl.BlockSpec(memory_space=pl.ANY)


HBM = pl.BlockSpec(memory_space=pltpu.HBM)
SEMS = pl.BlockSpec(memory_space=pltpu.SEMAPHORE)
EFFECT = pltpu.SideEffectType.DATAFLOW_SIDE_EFFECTING
NPEER = NDEV - 1

AG_GROUPS = (("qkv",), ("wo", "gu0", "d0"), ("wpi", "pg", "gu1", "d1"))
AG_ORDER = tuple(n for grp in AG_GROUPS for n in grp)
RS_GROUPS = (("d1", "gu1"), ("pg", "wpi"), ("d0", "gu0"), ("wo", "qkv"))


def _hbm(a):
    return pltpu.with_memory_space_constraint(a, pltpu.HBM)


def _remote(src, dst, send, recv, peer):
    return pltpu.make_async_remote_copy(src_ref=src, dst_ref=dst, send_sem=send, recv_sem=recv, device_id=peer,
                                        device_id_type=pl.DeviceIdType.MESH)


def _bcast_all(v, name, deps=()):
    W = v.shape[1]
    nd = len(deps)

    def kern(v_ref, *rest):
        o_ref, send, recv, lsem = rest[nd:]
        x, y, c, me = _mesh_pos()
        own = pltpu.make_async_copy(v_ref, o_ref.at[me], lsem)
        own.start()
        cps = [_remote(v_ref, o_ref.at[me], send.at[k - 1], recv.at[k - 1], _peer(x, y, c, k)[0])
               for k in range(1, NDEV)]
        for cp in cps:
            cp.start()
        for cp in cps:
            cp.wait_recv()
            cp.wait_send()
        own.wait()

    return pl.pallas_call(
        kern, in_specs=[ANY] * (1 + nd), out_specs=ANY, out_shape=jax.ShapeDtypeStruct((NDEV, 8, W), F32),
        scratch_shapes=[pltpu.SemaphoreType.DMA((NPEER,)), pltpu.SemaphoreType.DMA((NPEER,)),
                        pltpu.SemaphoreType.DMA(())],
        name=name)(v, *deps)


ALL_KS = tuple(range(1, NDEV))
AG_KS1 = (1, 2, 4, 6)
AG_KS2 = (2, 4, 6)
RS_KS_PAIR = (1, 3, 5, 7)
RS_KS_CHIPS = (2, 4, 6)


def _split_start(srcs, src_of, lands, copy_refs, name, deps=(), ks=ALL_KS, to=None):
    ns, n, nd, nk = len(srcs), len(lands), len(deps), len(ks)

    def body(*refs):
        ins, land = refs[:ns], refs[ns:ns + n]
        send, recv = refs[ns + n + nd], refs[ns + n + nd + 1]
        token = refs[-1]
        x, y, c, me = _mesh_pos()
        for j in range(n):
            for i, k in enumerate(ks):
                _, pid = _peer(x, y, c, k)
                dest, _ = _peer(x, y, c, k if to is None else to)
                src, dst = copy_refs(j, (land[j] if src_of[j] is None else ins[src_of[j]]), land[j], me, pid, i)
                _remote(src, dst, send.at[j * nk + i], recv.at[j * nk + i], dest).start()
        token[...] = jnp.zeros_like(token)

    outs = pl.pallas_call(
        body, name=name,
        out_shape=(pltpu.SemaphoreType.DMA((n * nk,)), pltpu.SemaphoreType.DMA((n * nk,)))
        + tuple(pltpu.HBM(a.shape, a.dtype) for a in srcs) + tuple(pltpu.HBM(a.shape, a.dtype) for a in lands)
        + (jax.ShapeDtypeStruct((8, 128), F32),),
        in_specs=(HBM,) * (ns + n) + (ANY,) * nd,
        out_specs=(SEMS, SEMS) + (HBM,) * (ns + n) + (pl.BlockSpec(memory_space=pltpu.VMEM),),
        input_output_aliases={i: 2 + i for i in range(ns + n)},
        compiler_params=pltpu.CompilerParams(has_side_effects=EFFECT),
    )(*[_hbm(a) for a in srcs], *[_hbm(a) for a in lands], *deps)
    return outs[0], outs[1], list(outs[2:2 + ns]), list(outs[2 + ns:2 + ns + n]), outs[-1]


def _split_wait(srcs, src_of, lands, send, recv, sem_rows, wait_refs, after, name, ks=ALL_KS):
    ns, n, nk = len(srcs), len(lands), len(ks)

    def body(*refs):
        ins, land = refs[:ns], refs[ns:ns + n]
        send_ref, recv_ref = refs[ns + n], refs[ns + n + 1]
        x, y, c, me = _mesh_pos()
        for j in range(n):
            for i, k in enumerate(ks):
                peer, _ = _peer(x, y, c, k)
                src, dst = wait_refs(j, (land[j] if src_of[j] is None else ins[src_of[j]]), land[j])
                sem = sem_rows[j] * nk + i
                cp = _remote(src, dst, send_ref.at[sem], recv_ref.at[sem], peer)
                cp.wait_send()
                cp.wait_recv()

    outs = pl.pallas_call(
        body, name=name,
        out_shape=tuple(pltpu.HBM(a.shape, a.dtype) for a in srcs) + tuple(pltpu.HBM(a.shape, a.dtype) for a in lands),
        in_specs=(HBM,) * (ns + n) + (SEMS, SEMS, ANY),
        out_specs=(HBM,) * (ns + n),
        input_output_aliases={i: i for i in range(ns + n)},
        compiler_params=pltpu.CompilerParams(has_side_effects=EFFECT),
    )(*srcs, *lands, send, recv, after)
    return list(outs[:ns]), list(outs[ns:])


class _Comm:
    def __init__(self, shards, me, deps=()):
        self.me = me
        self.ag_land, self.ag_sems, self.ag_tokens = {}, {}, ()
        self.rs = []
        for part, names in enumerate((AG_GROUPS[0], AG_ORDER[len(AG_GROUPS[0]):])):
            rows = [SEC_ROWS[n] for n in names]
            mine = [shards[n] for n in names]
            if part:
                mine, deps = lax.optimization_barrier((mine, deps))
            lands = [lax.dynamic_update_slice(lax.empty((NDEV * r, D), BF16), s.astype(BF16), (_shard_pos(n, me), 0))
                     for n, r, s in zip(names, rows, mine)]

            def copy_refs(j, src, land, me, pid, i, names=names, rows=rows):
                own = land.at[pl.ds(pl.multiple_of(_shard_pos(names[j], me), 16), rows[j])]
                return own, own

            send, recv, _, lands, token = _split_start([], [None] * len(names), lands, copy_refs, f"ag_start{part}",
                                                       deps=deps, ks=AG_KS1)
            deps = (token,)
            self.ag_tokens += (token,)
            for j, n in enumerate(names):
                self.ag_land[n] = lands[j]
                self.ag_sems[n] = (send, recv, j)

    def weights(self, group, after):
        names = AG_GROUPS[group]
        send, recv = self.ag_sems[names[0]][:2]
        idx = [self.ag_sems[n][2] for n in names]
        rows = [SEC_ROWS[n] for n in names]
        none = [None] * len(names)

        def wait_refs(j, src, land):
            return land.at[pl.ds(0, rows[j])], land.at[pl.ds(0, rows[j])]

        _, lands = _split_wait([], none, [self.ag_land[n] for n in names], send, recv, idx,
                               wait_refs, after, f"ag_wait{group}", ks=AG_KS1)

        def copy_refs(j, src, land, me, pid, i):
            theirs = land.at[pl.ds(pl.multiple_of(_shard_pos(names[j], pid), 16), rows[j])]
            return theirs, theirs

        send, recv, _, lands, token = _split_start([], none, lands, copy_refs, f"ag_pass{group}", ks=AG_KS2, to=1)
        _, lands = _split_wait([], none, lands, send, recv, list(range(len(names))), wait_refs, token,
                               f"ag_pass_wait{group}", ks=AG_KS2)
        return dict(zip(names, lands))

    def send_grads(self, group, gws):
        names = RS_GROUPS[group]
        rows = [SEC_ROWS[n] for n in names]
        grads = [gws[n] for n in names]
        me = self.me
        lands = [lax.dynamic_update_slice(
            lax.empty((NDEV, r, D), BF16),
            lax.dynamic_slice(g, (_shard_pos(n, me), 0), (r, D))[None], (me, 0, 0))
            for n, r, g in zip(names, rows, grads)]

        def copy_refs(j, src, land, me, pid, i):
            return src.at[pl.ds(pl.multiple_of(_shard_pos(names[j], pid), 16), rows[j])], land.at[me]

        send, recv, srcs, lands, token = _split_start(grads, list(range(len(names))), lands, copy_refs,
                                                      f"rs_start{group}")
        self.rs.append((names, rows, send, recv, srcs, lands, ALL_KS))
        return token

    def send_grads_pairwise(self, gws):
        names = RS_GROUPS[-1]
        rows = [SEC_ROWS[n] for n in names]
        grads = [gws[n] for n in names]
        idx = list(range(len(names)))
        lands = [lax.empty((len(RS_KS_PAIR), r, D), BF16) for r in rows]

        def copy_refs(j, src, land, me, pid, i):
            return src.at[pl.ds(pl.multiple_of(_shard_pos(names[j], pid), 16), rows[j])], land.at[i]

        send, recv, srcs, lands, token = _split_start(grads, idx, lands, copy_refs, "rs_pair_start",
                                                      ks=RS_KS_PAIR, to=1)
        self.pair = (names, rows, send, recv, srcs, lands)
        return token

    def pass_grads(self, after):
        names, rows, send, recv, srcs, lands = self.pair
        idx = list(range(len(names)))
        me = self.me

        def wait_refs(j, src, land):
            return src.at[pl.ds(0, rows[j])], land.at[0]

        srcs, lands = _split_wait(srcs, idx, lands, send, recv, idx, wait_refs, after, "rs_pair_wait", ks=RS_KS_PAIR)
        sums = []
        for n, r, g, got in zip(names, rows, srcs, lands):
            mine = jnp.stack([lax.dynamic_slice(g, (_shard_pos(n, jnp.bitwise_xor(me, k)), 0), (r, D))
                              for k in (0,) + RS_KS_CHIPS])
            sums.append(_pair_sum(mine, got, f"rs_pair_sum_{n}"))
        lands = [lax.dynamic_update_slice(lax.empty(p.shape, BF16), p[0:1], (0, 0, 0)) for p in sums]

        def copy_refs(j, src, land, me, pid, i):
            return src.at[i + 1], land.at[i + 1]

        send, recv, sums, lands, token = _split_start(sums, idx, lands, copy_refs, f"rs_start{len(RS_GROUPS) - 1}",
                                                      ks=RS_KS_CHIPS)
        self.rs.append((names, rows, send, recv, sums, lands, RS_KS_CHIPS))
        return token

    def received(self, group, after):
        names, rows, send, recv, srcs, lands, ks = self.rs[group]
        whole = srcs[0].ndim == 2

        def wait_refs(j, src, land):
            return (src.at[pl.ds(0, rows[j])] if whole else src.at[0]), land.at[0]

        _, lands = _split_wait(srcs, list(range(len(names))), lands, send, recv, list(range(len(names))), wait_refs,
                               after, f"rs_wait{group}", ks=ks)
        return dict(zip(names, lands))


def _pair_sum(a, b, name):
    n, rows, _ = a.shape
    tr = 384 if rows % 384 == 0 else rows

    def kern(a_ref, b_ref, o_ref):
        o_ref[...] = (a_ref[...].astype(F32) + b_ref[...].astype(F32)).astype(BF16)

    blk = pl.BlockSpec((1, tr, D), lambda i, t: (i, t, 0))
    return pl.pallas_call(
        kern, grid=(n, rows // tr), in_specs=[blk, blk], out_specs=blk,
        out_shape=jax.ShapeDtypeStruct(a.shape, BF16), compiler_params=_cparams(2), name=name)(a, b)


def _sum_contributions(r_ref):
    g = r_ref[0].astype(F32)
    for slot in range(1, r_ref.shape[0]):
        g = g + r_ref[slot].astype(F32)
    return g


def _adam_math(g, w, m, v):
    c1 = 1.0 / (1.0 - ADAM_B1 ** ADAM_STEP)
    c2 = 1.0 / (1.0 - ADAM_B2 ** ADAM_STEP)
    mn = ADAM_B1 * m + (1.0 - ADAM_B1) * g
    vn = ADAM_B2 * v + (1.0 - ADAM_B2) * (g * g)
    return -ADAM_LR * ((mn * c1) / (jnp.sqrt(vn * c2) + ADAM_EPS) + ADAM_WD * w), mn, vn


def _adamw(R, w, m, v, *, tr, name, layer=None, prev=None):
    rows, C = w.shape[-2:]
    nprev = 0 if prev is None else 4

    def kern(r_ref, w_ref, m_ref, v_ref, *rest):
        g_out, d_out, m_out, v_out = rest[nprev:]
        g = _sum_contributions(r_ref)
        g_out[...] = g
        d_out[...], m_out[...], v_out[...] = _adam_math(g, w_ref[...], m_ref[...], v_ref[...])

    if layer is None:
        tile = pl.BlockSpec((tr, C), lambda i: (i, 0))
    else:
        tile = pl.BlockSpec((None, tr, C), lambda i: (layer, i, 0))
    shp = jax.ShapeDtypeStruct(w.shape, F32)
    return pl.pallas_call(
        kern, grid=(rows // tr,),
        in_specs=[pl.BlockSpec((R.shape[0], tr, C), lambda i: (0, i, 0)), tile, tile, tile]
        + [pl.BlockSpec(memory_space=pl.ANY)] * nprev,
        out_specs=[tile] * 4, out_shape=[shp] * 4,
        input_output_aliases={4 + k: k for k in range(nprev)},
        compiler_params=_cparams(1), name=name)(R, w, m, v, *(prev or ()))


def _adamw_pool_group(R, w, m, v):
    rows = SEC_ROWS["pg"]

    def kern(r_ref, w_ref, m_ref, v_ref, g_out, d_out, m_out, v_out):
        g = _sum_contributions(r_ref)
        g_out[0] = g
        d_out[0], m_out[0], v_out[0] = _adam_math(g, w_ref[0], m_ref[0], v_ref[0])

    blk = pl.BlockSpec((1, rows, PGD), lambda i: (i, 0, 0))
    shp = jax.ShapeDtypeStruct((POOL_G, rows, PGD), F32)
    return pl.pallas_call(
        kern, grid=(POOL_G,),
        in_specs=[pl.BlockSpec((NDEV, rows, PGD), lambda i: (0, 0, i)), blk, blk, blk],
        out_specs=[blk] * 4, out_shape=[shp] * 4, compiler_params=_cparams(1), name="adamw_pg")(R, w, m, v)


def _grad_sum_t(R, name):
    rows = R.shape[1]
    tr = 128

    def kern(r_ref, o_ref):
        o_ref[...] = _sum_contributions(r_ref).T

    return pl.pallas_call(
        kern, grid=(rows // tr,), in_specs=[pl.BlockSpec((R.shape[0], tr, D), lambda i: (0, i, 0))],
        out_specs=pl.BlockSpec((D, tr), lambda i: (0, i)), out_shape=jax.ShapeDtypeStruct((D, rows), F32),
        compiler_params=_cparams(1), name=name)(R)


def _adam_plain(g, w, m, v, *, tr, name):
    rows, C = w.shape

    def kern(g_ref, w_ref, m_ref, v_ref, d_out, m_out, v_out):
        d_out[...], m_out[...], v_out[...] = _adam_math(g_ref[...], w_ref[...], m_ref[...], v_ref[...])

    tile = pl.BlockSpec((tr, C), lambda i: (i, 0))
    shp = jax.ShapeDtypeStruct((rows, C), F32)
    return pl.pallas_call(
        kern, grid=(rows // tr,), in_specs=[tile] * 4, out_specs=[tile] * 3, out_shape=[shp] * 3,
        compiler_params=_cparams(1), name=name)(g, w, m, v)


def _pack_sections(w_qkv, w_attn_out, w_pool_in, w_pool_group, w_ffn_gate_up, w_ffn_down):
    pg = w_pool_group[0].transpose(1, 0, 2).reshape(SEC_ROWS["pg"], D)
    return {"qkv": w_qkv[0].T, "wo": w_attn_out[0], "wpi": w_pool_in[0], "gu0": w_ffn_gate_up[0].T,
            "gu1": w_ffn_gate_up[1].T, "d0": w_ffn_down[0], "d1": w_ffn_down[1], "pg": pg}


def _vec_pack(attn_norm, ffn_norm, final_norm, pool_norm_sh, pool_scale_sh, me):
    def place(sh):
        return lax.dynamic_update_slice(jnp.zeros((1, D), F32), sh, (0, me * 128))
    return jnp.concatenate([attn_norm, ffn_norm, final_norm.reshape(1, D), place(pool_norm_sh),
                            place(pool_scale_sh), jnp.zeros((2, D), F32)], axis=0)


def _vec_unpack(p, me):
    def take(r):
        return lax.dynamic_slice(p[r:r + 1], (0, me * 128), (1, 128))
    return p[0:1], p[1:3], p[3], take(4), take(5)


def kernel(x, attn_norm, w_qkv, w_attn_out, pool_norm, w_pool_in, w_pool_group, pool_scale, ffn_norm, w_ffn_gate_up, w_ffn_down, final_norm, loss_target, m_attn_norm, m_w_qkv, m_w_attn_out, m_pool_norm, m_w_pool_in, m_w_pool_group, m_pool_scale, m_ffn_norm, m_w_ffn_gate_up, m_w_ffn_down, m_final_norm, v_attn_norm, v_w_qkv, v_w_attn_out, v_pool_norm, v_w_pool_in, v_w_pool_group, v_pool_scale, v_ffn_norm, v_w_ffn_gate_up, v_w_ffn_down, v_final_norm):
    me = 4 * lax.axis_index("x") + 2 * lax.axis_index("y") + lax.axis_index("c")

    pw = _pack_sections(w_qkv, w_attn_out, w_pool_in, w_pool_group, w_ffn_gate_up, w_ffn_down)
    vsh = jnp.concatenate([pool_norm, pool_scale, jnp.zeros((6, 128), F32)], axis=0)

    vg = _bcast_all(vsh, "gather_pool_vectors")
    comm = _Comm(pw, me, deps=(vg,))
    pool_norm_full = vg[:, 0, :].reshape(1, D)
    pool_scale_full = vg[:, 1, :].reshape(1, D)

    grad_x, vec = _local_step(x[0], loss_target[0], comm, attn_norm, ffn_norm, final_norm,
                              pool_norm_full, pool_scale_full)

    small = ((attn_norm, ffn_norm, final_norm, pool_norm, pool_scale),
             (m_attn_norm, m_ffn_norm, m_final_norm, m_pool_norm, m_pool_scale),
             (v_attn_norm, v_ffn_norm, v_final_norm, v_pool_norm, v_pool_scale))
    small, grad_x = lax.optimization_barrier((small, grad_x))
    vw, vm, vv = (_vec_pack(*s, me) for s in small)

    gu_t = [jnp.swapaxes(a, 1, 2) for a in (w_ffn_gate_up, m_w_ffn_gate_up, v_w_ffn_gate_up)]
    res = {}
    gu_res, d_res = None, None
    vec_out = None
    after = grad_x
    for group in range(len(RS_GROUPS)):
        if group == len(RS_GROUPS) - 1:
            VR = _bcast_all(vec, "exchange_vector_grads", deps=(after,))
            vec_out = _adamw(VR, vw, vm, vv, tr=8, name="adamw_vec")
            after = vec_out[0]
        for n, R in comm.received(group, after).items():
            if n in ("d0", "d1"):
                d_res = _adamw(R, w_ffn_down, m_w_ffn_down, v_w_ffn_down, tr=352, name=f"adamw_{n}",
                               layer=int(n[1]), prev=d_res)
                after = d_res[0]
            elif n in ("gu0", "gu1"):
                gu_res = _adamw(R, *gu_t, tr=352, name=f"adamw_{n}", layer=int(n[2]), prev=gu_res)
                after = gu_res[0]
            elif n == "pg":
                out = _adamw_pool_group(R, w_pool_group[0], m_w_pool_group[0], v_w_pool_group[0])
                res["pg"] = tuple(a[None] for a in out)
                after = out[0]
            elif n in ("wo", "wpi"):
                w, m, v = ((w_attn_out, m_w_attn_out, v_w_attn_out) if n == "wo"
                           else (w_pool_in, m_w_pool_in, v_w_pool_in))
                res[n] = _adamw(R, w[0], m[0], v[0], tr=128, name=f"adamw_{n}")
                res[n] = tuple(a[None] for a in res[n])
                after = res[n][0]
            else:
                g = _grad_sum_t(R, "grad_sum_qkv")
                out = _adam_plain(g, w_qkv[0], m_w_qkv[0], v_w_qkv[0], tr=256, name="adamw_qkv")
                res["qkv"] = tuple(a[None] for a in (g,) + tuple(out))
                after = out[0]
    res["gu"] = tuple(jnp.swapaxes(a, 1, 2) for a in gu_res)
    res["d"] = tuple(d_res)

    outs = []
    for kind in range(4):
        an, fn, fin, pn, ps = _vec_unpack(vec_out[kind], me)
        outs.append((an, res["qkv"][kind], res["wo"][kind], pn, res["wpi"][kind], res["pg"][kind], ps, fn,
                     res["gu"][kind], res["d"][kind], fin))
    loss = 0.5 * jnp.sum(vec_out[0][6]) / D
    return (loss, grad_x[None]) + outs[0] + outs[1] + outs[2] + outs[3]
```

```python
import jax
import jax.numpy as jnp
from jax import lax
from jax.experimental import pallas as pl
from jax.experimental.pallas import tpu as pltpu

F32 = jnp.float32
BF16 = jnp.bfloat16

D = 1024
NDEV = 8
HEADS = 8
HD = 128
QB = 128
NGROUPS = 3
DILS = (1, 4, 16)
DFF = 2816
HCH = 1408
POOL_G = 4
PGD = 256
RMS_EPS = 1e-6
NEG = -1e30

ADAM_LR = 0.001
ADAM_B1 = 0.9
ADAM_B2 = 0.999
ADAM_EPS = 1e-08
ADAM_WD = 0.01
ADAM_STEP = 10

VMEM_LIMIT = 52 * 1024 * 1024

SECTIONS = (("qkv", 1152), ("wo", 128), ("wpi", 128), ("gu0", 704), ("gu1", 704),
            ("d0", 352), ("d1", 352), ("pg", 32))
LOC_OFF = {}
GLB_OFF = {}
_o = 0
for _n, _r in SECTIONS:
    LOC_OFF[_n] = _o
    GLB_OFF[_n] = _o * NDEV
    _o += _r
PACK_ROWS = _o
GLB_ROWS = PACK_ROWS * NDEV
SEC_ROWS = dict(SECTIONS)


def _cparams(n_grid):
    return pltpu.CompilerParams(dimension_semantics=("arbitrary",) * n_grid, vmem_limit_bytes=VMEM_LIMIT)


def _shard_pos(name, dev):
    n = SEC_ROWS[name]
    if name in ("gu0", "gu1"):
        return ((dev % 4) // 2) * (2 * HCH) + (dev // 4) * HCH + (dev % 2) * n
    return dev * n


def _mm(a, b, *, mode, M, N, K, tm, tn, tk, out_dtype, name, a_off=(0, 0), b_off=(0, 0), res=None,
        out_rows=None, out_off=0, out_prev=None, deps=()):
    nm, nn, nk = M // tm, N // tn, K // tk
    assert nm * tm == M and nn * tn == N and nk * tk == K
    if mode == "nn":
        a_bs, b_bs = (tm, tk), (tk, tn)
        a_ix = lambda i, j, k: (i, k)
        b_ix = lambda i, j, k: (k, j)
        dims = (((1,), (0,)), ((), ()))
    elif mode == "nt":
        a_bs, b_bs = (tm, tk), (tn, tk)
        a_ix = lambda i, j, k: (i, k)
        b_ix = lambda i, j, k: (j, k)
        dims = (((1,), (1,)), ((), ()))
    else:
        a_bs, b_bs = (tk, tm), (tk, tn)
        a_ix = lambda i, j, k: (k, i)
        b_ix = lambda i, j, k: (k, j)
        dims = (((0,), (0,)), ((), ()))

    def spec(bs, ix, off):
        def im(i, j, k):
            r, c = ix(i, j, k)
            return (r + off[0], c + off[1])
        return pl.BlockSpec(bs, im)

    in_specs = [spec(a_bs, a_ix, a_off), spec(b_bs, b_ix, b_off)]
    args = [a, b]
    if res is not None:
        in_specs.append(pl.BlockSpec((tm, tn), lambda i, j, k: (i, j)))
        args.append(res)
    out_shape = jax.ShapeDtypeStruct((M if out_rows is None else out_rows, N), out_dtype)
    out_spec = pl.BlockSpec((tm, tn), lambda i, j, k: (i + out_off, j))
    has_res = res is not None
    extra = list(deps) + ([out_prev] if out_prev is not None else [])
    for dep in extra:
        in_specs.append(pl.BlockSpec(memory_space=pl.ANY))
        args.append(dep)
    o_pos = 2 + int(has_res) + len(extra)
    aliases = {len(args) - 1: 0} if out_prev is not None else {}

    def kern(*refs):
        a_ref, b_ref = refs[0], refs[1]
        res_ref = refs[2] if has_res else None
        o_ref = refs[o_pos]
        av = a_ref[...]
        bv = b_ref[...]
        if av.dtype != BF16:
            av = av.astype(BF16)
        if bv.dtype != BF16:
            bv = bv.astype(BF16)
        part = lax.dot_general(av, bv, dims, preferred_element_type=F32)

        def write(val):
            if has_res:
                val = val + res_ref[...]
            o_ref[...] = val.astype(out_dtype)

        if nk == 1:
            write(part)
        else:
            acc_ref = refs[-1]
            k = pl.program_id(2)

            @pl.when(k == 0)
            def _():
                acc_ref[...] = part

            @pl.when(k > 0)
            def _():
                acc_ref[...] += part

            @pl.when(k == nk - 1)
            def _():
                write(acc_ref[...])

    scratch = [pltpu.VMEM((tm, tn), F32)] if nk > 1 else []
    return pl.pallas_call(
        kern, grid=(nm, nn, nk), in_specs=in_specs, out_specs=out_spec, out_shape=out_shape,
        scratch_shapes=scratch, input_output_aliases=aliases, compiler_params=_cparams(3), name=name)(*args)


def _mm_rms_bwd(a, b, x, g, dres, *, mode, M, K, tm, name, b_off=(0, 0), deps=()):
    nd = len(deps)
    b_bs = (K, D) if mode == "nn" else (D, K)
    dims = (((1,), (0,)), ((), ())) if mode == "nn" else (((1,), (1,)), ((), ()))

    def kern(a_ref, b_ref, x_ref, g_ref, dres_ref, *rest):
        dx_ref, dg_ref = rest[nd:]
        i = pl.program_id(0)
        av = a_ref[...]
        if av.dtype != BF16:
            av = av.astype(BF16)
        dhv = lax.dot_general(av, b_ref[...], dims, preferred_element_type=F32)
        xv = x_ref[...]
        r = lax.rsqrt(jnp.mean(xv * xv, axis=-1, keepdims=True) + RMS_EPS)
        xhat = xv * r
        gy = dhv * g_ref[...]
        dx_ref[...] = dres_ref[...] + r * (gy - xhat * jnp.mean(gy * xhat, axis=-1, keepdims=True))
        part = jnp.sum(dhv * xhat, axis=0, keepdims=True)

        @pl.when(i == 0)
        def _():
            dg_ref[...] = part

        @pl.when(i > 0)
        def _():
            dg_ref[...] += part

    row = pl.BlockSpec((tm, D), lambda i: (i, 0))
    vec = pl.BlockSpec((1, D), lambda i: (0, 0))
    return pl.pallas_call(
        kern, grid=(M // tm,),
        in_specs=[pl.BlockSpec((tm, K), lambda i: (i, 0)),
                  pl.BlockSpec(b_bs, lambda i: b_off, pipeline_mode=pl.Buffered(1)), row, vec, row]
        + [pl.BlockSpec(memory_space=pl.ANY)] * nd,
        out_specs=[row, vec],
        out_shape=[jax.ShapeDtypeStruct((M, D), F32), jax.ShapeDtypeStruct((1, D), F32)],
        compiler_params=_cparams(1), name=name)(a, b, x, g, dres, *deps)


def _mm_res_norm(a, b, res, g, *, K, tm, name, b_off=(0, 0), tgt=None):
    M = a.shape[0]
    head = tgt is not None

    def kern(a_ref, b_ref, res_ref, g_ref, *rest):
        xv = res_ref[...] + jnp.dot(a_ref[...], b_ref[...], preferred_element_type=F32)
        gv = g_ref[...]
        r = lax.rsqrt(jnp.mean(xv * xv, axis=-1, keepdims=True) + RMS_EPS)
        xhat = xv * r
        if not head:
            xo_ref, h_ref = rest
            xo_ref[...] = xv
            h_ref[...] = (xhat * gv).astype(BF16)
            return
        t_ref, dx_ref, dg_ref, ls_ref = rest
        i = pl.program_id(0)
        e = xhat * gv - t_ref[...]
        dy = e * (1.0 / D)
        gy = dy * gv
        dx_ref[...] = r * (gy - xhat * jnp.mean(gy * xhat, axis=-1, keepdims=True))
        dgp = jnp.sum(dy * xhat, axis=0, keepdims=True)
        lsp = jnp.sum(e * e, axis=0, keepdims=True)

        @pl.when(i == 0)
        def _():
            dg_ref[...] = dgp
            ls_ref[...] = lsp

        @pl.when(i > 0)
        def _():
            dg_ref[...] += dgp
            ls_ref[...] += lsp

    row = pl.BlockSpec((tm, D), lambda i: (i, 0))
    vec = pl.BlockSpec((1, D), lambda i: (0, 0))
    in_specs = [pl.BlockSpec((tm, K), lambda i: (i, 0)),
                pl.BlockSpec((K, D), lambda i: b_off, pipeline_mode=pl.Buffered(1)), row, vec]
    if head:
        return pl.pallas_call(
            kern, grid=(M // tm,), in_specs=in_specs + [row], out_specs=[row, vec, vec],
            out_shape=[jax.ShapeDtypeStruct((M, D), F32), jax.ShapeDtypeStruct((1, D), F32),
                       jax.ShapeDtypeStruct((1, D), F32)],
            compiler_params=_cparams(1), name=name)(a, b, res, g, tgt)
    return pl.pallas_call(
        kern, grid=(M // tm,), in_specs=in_specs, out_specs=[row, row],
        out_shape=[jax.ShapeDtypeStruct((M, D), F32), jax.ShapeDtypeStruct((M, D), BF16)],
        compiler_params=_cparams(1), name=name)(a, b, res, g)


def _rms_fwd(x, g, name, deps=()):
    S = x.shape[0]
    tr = 512

    def kern(x_ref, g_ref, *rest):
        h_ref = rest[-1]
        xv = x_ref[...]
        r = lax.rsqrt(jnp.mean(xv * xv, axis=-1, keepdims=True) + RMS_EPS)
        h_ref[...] = (xv * r * g_ref[...]).astype(BF16)

    return pl.pallas_call(
        kern, grid=(S // tr,),
        in_specs=[pl.BlockSpec((tr, D), lambda i: (i, 0)), pl.BlockSpec((1, D), lambda i: (0, 0))]
        + [pl.BlockSpec(memory_space=pl.ANY)] * len(deps),
        out_specs=pl.BlockSpec((tr, D), lambda i: (i, 0)),
        out_shape=jax.ShapeDtypeStruct((S, D), BF16), compiler_params=_cparams(1), name=name)(x, g, *deps)


def _chunks_put(scr, val):
    for c in range(scr.shape[0]):
        scr[c] = val[:, c * 128:(c + 1) * 128]


def _chunks_get(scr):
    return jnp.concatenate([scr[c] for c in range(scr.shape[0])], axis=1)


def _chunks_rows(scr, r, n, dil):
    return jnp.concatenate([scr.at[c][pl.ds(r, n, stride=dil), :] for c in range(scr.shape[0])], axis=1)


def _chunks_add_rows(scr, val, r, n, dil, accumulate):
    for c in range(scr.shape[0]):
        rows = pl.ds(r, n, stride=dil)
        piece = val[:, c * 128:(c + 1) * 128]
        tile = scr.at[c]
        tile[rows, :] = tile[rows, :] + piece if accumulate else piece


def _rms_fwd_folded(x, g, name, deps=()):
    S = x.shape[0]
    tr = 512
    dils = DILS[1:]

    def kern(x_ref, g_ref, *rest):
        outs, scr = rest[len(deps):-1], rest[-1]
        xv = x_ref[...]
        r = lax.rsqrt(jnp.mean(xv * xv, axis=-1, keepdims=True) + RMS_EPS)
        h = (xv * r * g_ref[...]).astype(BF16)
        outs[0][...] = h
        _chunks_put(scr, h.astype(F32))
        for o_ref, dil in zip(outs[1:], dils):
            for res in range(dil):
                o_ref[res] = _chunks_rows(scr, res, tr // dil, dil).astype(BF16)

    return pl.pallas_call(
        kern, grid=(S // tr,),
        in_specs=[pl.BlockSpec((tr, D), lambda i: (i, 0)), pl.BlockSpec((1, D), lambda i: (0, 0))]
        + [pl.BlockSpec(memory_space=pl.ANY)] * len(deps),
        out_specs=[pl.BlockSpec((tr, D), lambda i: (i, 0))]
        + [pl.BlockSpec((dil, tr // dil, D), lambda i: (0, i, 0)) for dil in dils],
        out_shape=[jax.ShapeDtypeStruct((S, D), BF16)]
        + [jax.ShapeDtypeStruct((dil, S // dil, D), BF16) for dil in dils],
        scratch_shapes=[pltpu.VMEM((D // 128, tr, 128), F32)],
        compiler_params=_cparams(1), name=name)(x, g, *deps)


def _rms_bwd(dh, x, g, dres, name, folded=()):
    S = x.shape[0]
    tr = 512
    nf = len(folded)

    def kern(dh_ref, *rest):
        f_refs = rest[:nf]
        x_ref, g_ref, dres_ref, dx_ref, dg_ref = rest[nf:nf + 5]
        i = pl.program_id(0)
        xv = x_ref[...]
        if nf:
            acc_ref = rest[nf + 5]
            _chunks_put(acc_ref, dh_ref[...].astype(F32))
            for f_ref in f_refs:
                dil = f_ref.shape[0]
                for res in range(dil):
                    _chunks_add_rows(acc_ref, f_ref[res], res, tr // dil, dil, True)
            dhv = _chunks_get(acc_ref)
        else:
            dhv = dh_ref[...].astype(F32)
        r = lax.rsqrt(jnp.mean(xv * xv, axis=-1, keepdims=True) + RMS_EPS)
        xhat = xv * r
        gy = dhv * g_ref[...]
        dx_ref[...] = dres_ref[...] + r * (gy - xhat * jnp.mean(gy * xhat, axis=-1, keepdims=True))
        part = jnp.sum(dhv * xhat, axis=0, keepdims=True)

        @pl.when(i == 0)
        def _():
            dg_ref[...] = part

        @pl.when(i > 0)
        def _():
            dg_ref[...] += part

    row = pl.BlockSpec((tr, D), lambda i: (i, 0))
    vec = pl.BlockSpec((1, D), lambda i: (0, 0))
    fspecs = [pl.BlockSpec((f.shape[0], tr // f.shape[0], D), lambda i: (0, i, 0)) for f in folded]
    return pl.pallas_call(
        kern, grid=(S // tr,), in_specs=[row] + fspecs + [row, vec, row], out_specs=[row, vec],
        out_shape=[jax.ShapeDtypeStruct((S, D), F32), jax.ShapeDtypeStruct((1, D), F32)],
        scratch_shapes=[pltpu.VMEM((D // 128, tr, 128), F32)] if nf else [],
        compiler_params=_cparams(1), name=name)(dh, *folded, x, g, dres)


def _ffn_up(h, G, name):
    S = h.shape[0]
    tm = 512
    nj = DFF // HCH

    def kern(h_ref, w_ref, gu_ref, act_ref):
        gu = lax.dot_general(h_ref[...], w_ref[...], (((1,), (1,)), ((), ())), preferred_element_type=F32)
        gu_ref[...] = gu.astype(BF16)
        gate = gu[:, :HCH]
        up = gu[:, HCH:]
        act_ref[...] = (gate * jax.nn.sigmoid(gate) * up).astype(BF16)

    return pl.pallas_call(
        kern, grid=(nj, S // tm),
        in_specs=[pl.BlockSpec((tm, D), lambda j, i: (i, 0)),
                  pl.BlockSpec((2 * HCH, D), lambda j, i: (j, 0))],
        out_specs=[pl.BlockSpec((tm, 2 * HCH), lambda j, i: (i, j)),
                   pl.BlockSpec((tm, HCH), lambda j, i: (i, j))],
        out_shape=[jax.ShapeDtypeStruct((S, 2 * DFF), BF16), jax.ShapeDtypeStruct((S, DFF), BF16)],
        compiler_params=_cparams(2), name=name)(h, G)


def _ffn_down_bwd(dx, G, gu, name):
    S = dx.shape[0]
    tm = 512
    nj = DFF // HCH

    def kern(dx_ref, w_ref, gu_ref, o_ref):
        dxv = dx_ref[...].astype(BF16)
        for j in range(nj):
            c0 = 2 * HCH * j
            dact = lax.dot_general(dxv, w_ref[HCH * j:HCH * (j + 1), :], (((1,), (1,)), ((), ())),
                                   preferred_element_type=F32)
            gate = gu_ref[:, c0:c0 + HCH].astype(F32)
            up = gu_ref[:, c0 + HCH:c0 + 2 * HCH].astype(F32)
            sig = jax.nn.sigmoid(gate)
            silu = gate * sig
            o_ref[:, c0:c0 + HCH] = (dact * up * (sig * (1.0 + gate * (1.0 - sig)))).astype(BF16)
            o_ref[:, c0 + HCH:c0 + 2 * HCH] = (dact * silu).astype(BF16)

    row = pl.BlockSpec((tm, 2 * DFF), lambda i: (i, 0))
    return pl.pallas_call(
        kern, grid=(S // tm,),
        in_specs=[pl.BlockSpec((tm, D), lambda i: (i, 0)),
                  pl.BlockSpec((DFF, D), lambda i: (0, 0), pipeline_mode=pl.Buffered(1)), row],
        out_specs=row, out_shape=jax.ShapeDtypeStruct((S, 2 * DFF), BF16),
        compiler_params=_cparams(1), name=name)(dx, G, gu)


def _trail(u, *, backward, name):
    S = u.shape[0]

    def kern(u_ref, o_ref):
        g = pl.program_id(0)
        for grp in range(POOL_G):
            @pl.when(g == grp)
            def _(grp=grp):
                uv = u_ref[...].astype(F32)
                row = lax.broadcasted_iota(jnp.int32, uv.shape, 0)
                cnt = jnp.minimum(row + 1, 2 << grp).astype(F32)
                s = uv / cnt if backward else uv
                for k in (1, 2, 4, 8)[:grp + 1]:
                    if backward:
                        sh = jnp.where(row < S - k, pltpu.roll(s, S - k, 0), 0.0)
                    else:
                        sh = jnp.where(row >= k, pltpu.roll(s, k, 0), 0.0)
                    s = s + sh
                if backward:
                    o_ref[...] = (s - uv).astype(BF16)
                else:
                    o_ref[...] = (s / cnt - uv).astype(BF16)

    blk = pl.BlockSpec((S, PGD), lambda g: (0, g))
    return pl.pallas_call(
        kern, grid=(POOL_G,), in_specs=[blk], out_specs=blk,
        out_shape=jax.ShapeDtypeStruct((S, D), BF16), compiler_params=_cparams(1), name=name)(u)


def _pool_out(yd, G, scale, xres):
    S = yd.shape[0]
    tm = min(S, 4096)

    def kern(y_ref, w_ref, s_ref, x_ref, o_ref):
        z = jnp.dot(y_ref[...], w_ref[...], preferred_element_type=F32)
        o_ref[...] = x_ref[...] + z * s_ref[...]

    tile = pl.BlockSpec((tm, PGD), lambda i, g: (i, g))
    return pl.pallas_call(
        kern, grid=(S // tm, POOL_G),
        in_specs=[tile, pl.BlockSpec((PGD, PGD), lambda i, g: (0, g)),
                  pl.BlockSpec((1, PGD), lambda i, g: (0, g)), tile],
        out_specs=tile, out_shape=jax.ShapeDtypeStruct((S, D), F32),
        compiler_params=_cparams(2), name="pool_out")(yd, G, scale, xres)


def _pool_out_bwd(dz, yd, G, scale):
    S = yd.shape[0]
    tm = min(S, 4096)
    ni = S // tm

    def kern(dz_ref, y_ref, w_ref, s_ref, dy_ref, ds_ref, dw_ref, acc_ref):
        i = pl.program_id(1)
        dzv = dz_ref[...]
        yv = y_ref[...]
        wv = w_ref[...]
        zraw = jnp.dot(yv, wv, preferred_element_type=F32)
        dsp = jnp.sum(dzv * zraw, axis=0, keepdims=True)
        dzr = (dzv * s_ref[...]).astype(BF16)
        dy_ref[...] = lax.dot_general(dzr, wv, (((1,), (1,)), ((), ())), preferred_element_type=F32)
        dwp = lax.dot_general(yv, dzr, (((0,), (0,)), ((), ())), preferred_element_type=F32)

        @pl.when(i == 0)
        def _():
            ds_ref[...] = dsp
            acc_ref[...] = dwp

        @pl.when(i > 0)
        def _():
            ds_ref[...] += dsp
            acc_ref[...] += dwp

        @pl.when(i == ni - 1)
        def _():
            dw_ref[...] = acc_ref[...].astype(BF16)

    tile = pl.BlockSpec((tm, PGD), lambda g, i: (i, g))
    return pl.pallas_call(
        kern, grid=(POOL_G, ni),
        in_specs=[tile, tile, pl.BlockSpec((PGD, PGD), lambda g, i: (0, g)),
                  pl.BlockSpec((1, PGD), lambda g, i: (0, g))],
        out_specs=[tile, pl.BlockSpec((1, PGD), lambda g, i: (0, g)),
                   pl.BlockSpec((PGD, PGD), lambda g, i: (0, g))],
        out_shape=[jax.ShapeDtypeStruct((S, D), F32), jax.ShapeDtypeStruct((1, D), F32),
                   jax.ShapeDtypeStruct((PGD, D), BF16)],
        scratch_shapes=[pltpu.VMEM((PGD, PGD), F32)],
        compiler_params=_cparams(2), name="pool_out_bwd")(dz, yd, G, scale)


def _bias_table():
    qi = jnp.arange(QB)[:, None]
    ki = jnp.arange(2 * QB)[None, :]
    delta = QB + qi - ki
    inband = (delta >= 0) & (delta <= QB)
    n = NGROUPS * HEADS
    slopes = jnp.exp2(-8.0 * jnp.arange(1, n + 1, dtype=F32) / n).reshape(NGROUPS, HEADS)
    dil = jnp.asarray(DILS, F32)
    bias = -slopes[:, :, None, None] * (delta.astype(F32)[None, None] * dil[:, None, None, None])
    return jnp.where(inband[None, None], bias, NEG)


def _attn_fwd(qkv_f, bias, nb, name):
    S = qkv_f.shape[0]
    nblk = S // QB
    scale = HD ** -0.5

    def kern(q_ref, kc_ref, kp_ref, vc_ref, vp_ref, b_ref, o_ref, l_ref, s_scr, p_scr, r_scr):
        b = pl.program_id(0)
        has_prev = jnp.bitwise_and(b, nb - 1) != 0
        col = lax.broadcasted_iota(jnp.int32, (QB, 2 * QB), 1)
        dead = jnp.logical_and(col < QB, jnp.logical_not(has_prev))
        lane = lax.broadcasted_iota(jnp.int32, (QB, HD), 1)
        lse_all = jnp.zeros((QB, HD), F32)
        for h in range(HEADS):
            sl = slice(h * HD, (h + 1) * HD)
            kk = jnp.concatenate([kp_ref[:, sl], kc_ref[:, sl]], axis=0)
            s_scr[h] = lax.dot_general(q_ref[:, sl], kk, (((1,), (1,)), ((), ())), preferred_element_type=F32)
        for h in range(HEADS):
            s = s_scr[h] * scale + b_ref[h]
            s = jnp.where(dead, NEG, s)
            m = jnp.max(s, axis=-1, keepdims=True)
            p = jnp.exp(s - m)
            den = jnp.sum(p, axis=-1, keepdims=True)
            p_scr[h] = p.astype(BF16)
            r_scr[h] = jnp.broadcast_to(1.0 / den, (QB, HD))
            lse_all = jnp.where(lane == h, m + jnp.log(den), lse_all)
        for h in range(HEADS):
            sl = slice(h * HD, (h + 1) * HD)
            vv = jnp.concatenate([vp_ref[:, sl], vc_ref[:, sl]], axis=0)
            o = jnp.dot(p_scr[h], vv, preferred_element_type=F32) * r_scr[h]
            o_ref[:, sl] = o.astype(BF16)
        l_ref[...] = lse_all

    def blk(colblk, prev):
        if prev:
            return pl.BlockSpec((QB, D), lambda b: (jnp.maximum(b - 1, 0), colblk))
        return pl.BlockSpec((QB, D), lambda b: (b, colblk))

    return pl.pallas_call(
        kern, grid=(nblk,),
        in_specs=[blk(0, False), blk(1, False), blk(1, True), blk(2, False), blk(2, True),
                  pl.BlockSpec((HEADS, QB, 2 * QB), lambda b: (0, 0, 0))],
        out_specs=[pl.BlockSpec((QB, D), lambda b: (b, 0)), pl.BlockSpec((QB, HD), lambda b: (b, 0))],
        out_shape=[jax.ShapeDtypeStruct((S, D), BF16), jax.ShapeDtypeStruct((S, HD), F32)],
        scratch_shapes=[pltpu.VMEM((HEADS, QB, 2 * QB), F32), pltpu.VMEM((HEADS, QB, 2 * QB), BF16),
                        pltpu.VMEM((HEADS, QB, HD), F32)],
        compiler_params=_cparams(1), name=name)(qkv_f, qkv_f, qkv_f, qkv_f, qkv_f, bias)


def _natural(ref, scr, tm):
    dil = ref.shape[0]
    for res in range(dil):
        _chunks_add_rows(scr, ref[res].astype(F32), res, tm // dil, dil, False)
    return _chunks_get(scr)


def _attn_merge(os, lses):
    S = os[0].shape[0]
    tm = 512

    def kern(o0, o1, o2, l0, l1, l2, om_ref, lm_ref, ls1, ls2, os1, os2):
        la = l0[...]
        lb = _natural(l1, ls1, tm)
        lc = _natural(l2, ls2, tm)
        m = jnp.maximum(jnp.maximum(la, lb), lc)
        e0, e1, e2 = jnp.exp(la - m), jnp.exp(lb - m), jnp.exp(lc - m)
        tot = e0 + e1 + e2
        lm_ref[...] = m + jnp.log(tot)
        w0, w1, w2 = e0 / tot, e1 / tot, e2 / tot
        for res in range(o1.shape[0]):
            _chunks_add_rows(os1, o1[res].astype(F32), res, tm // o1.shape[0], o1.shape[0], False)
        for res in range(o2.shape[0]):
            _chunks_add_rows(os2, o2[res].astype(F32), res, tm // o2.shape[0], o2.shape[0], False)
        for h in range(HEADS):
            sl = slice(h * HD, (h + 1) * HD)
            acc = w0[:, h:h + 1] * o0[:, sl].astype(F32) + w1[:, h:h + 1] * os1[h] + w2[:, h:h + 1] * os2[h]
            om_ref[:, sl] = acc.astype(BF16)

    def spec(a, c):
        if a.ndim == 2:
            return pl.BlockSpec((tm, c), lambda i: (i, 0))
        return pl.BlockSpec((a.shape[0], tm // a.shape[0], c), lambda i: (0, i, 0))

    return pl.pallas_call(
        kern, grid=(S // tm,),
        in_specs=[spec(a, D) for a in os] + [spec(a, HD) for a in lses],
        out_specs=[pl.BlockSpec((tm, D), lambda i: (i, 0)), pl.BlockSpec((tm, HD), lambda i: (i, 0))],
        out_shape=[jax.ShapeDtypeStruct((S, D), BF16), jax.ShapeDtypeStruct((S, HD), F32)],
        scratch_shapes=[pltpu.VMEM((1, tm, HD), F32), pltpu.VMEM((1, tm, HD), F32),
                        pltpu.VMEM((HEADS, tm, HD), F32), pltpu.VMEM((HEADS, tm, HD), F32)],
        compiler_params=_cparams(1), name="attn_merge")(*os, *lses)


def _attn_bwd_prep(do, o, lse):
    S = o.shape[0]
    tm = 512
    dils = DILS[1:]

    def kern(do_ref, o_ref, l_ref, *rest):
        do_outs, l_outs, d_outs = rest[0:3], rest[3:5], rest[5:8]
        do_scr, l_scr, d_scr = rest[8:11]
        lane = lax.broadcasted_iota(jnp.int32, (tm, HD), 1)
        acc = jnp.zeros((tm, HD), F32)
        for h in range(HEADS):
            sl = slice(h * HD, (h + 1) * HD)
            prod = do_ref[:, sl] * o_ref[:, sl].astype(F32)
            acc = jnp.where(lane == h, jnp.sum(prod, axis=-1, keepdims=True), acc)
        d_scr[0] = acc
        l_scr[0] = l_ref[...]
        _chunks_put(do_scr, do_ref[...])
        do_outs[0][...] = do_ref[...].astype(BF16)
        d_outs[0][...] = acc
        for j, dil in enumerate(dils):
            for res in range(dil):
                n = tm // dil
                do_outs[1 + j][res] = _chunks_rows(do_scr, res, n, dil).astype(BF16)
                l_outs[j][res] = _chunks_rows(l_scr, res, n, dil)
                d_outs[1 + j][res] = _chunks_rows(d_scr, res, n, dil)

    def nat(c):
        return pl.BlockSpec((tm, c), lambda i: (i, 0))

    def fol(dil, c):
        return pl.BlockSpec((dil, tm // dil, c), lambda i: (0, i, 0))

    def shapes(c, dt, with_natural):
        first = [jax.ShapeDtypeStruct((S, c), dt)] if with_natural else []
        return first + [jax.ShapeDtypeStruct((dil, S // dil, c), dt) for dil in dils]

    outs = pl.pallas_call(
        kern, grid=(S // tm,), in_specs=[nat(D), nat(D), nat(HD)],
        out_specs=[nat(D)] + [fol(dil, D) for dil in dils] + [fol(dil, HD) for dil in dils]
        + [nat(HD)] + [fol(dil, HD) for dil in dils],
        out_shape=shapes(D, BF16, True) + shapes(HD, F32, False) + shapes(HD, F32, True),
        scratch_shapes=[pltpu.VMEM((HEADS, tm, HD), F32), pltpu.VMEM((1, tm, HD), F32), pltpu.VMEM((1, tm, HD), F32)],
        compiler_params=_cparams(1), name="attn_bwd_prep")(do, o, lse)
    return outs[0:3], [lse] + list(outs[3:5]), outs[5:8]


def _attn_bwd(qkv_f, do_f, lse_f, delta_f, bias, nb, name):
    S = qkv_f.shape[0]
    nblk = S // QB
    scale = HD ** -0.5

    def kern(q_ref, kc_ref, kp_ref, vc_ref, vp_ref, do_ref, l_ref, d_ref, b_ref, out_ref, dq_c, dk_c, dv_c,
             s_scr, dp_scr, ds_scr, p_scr):
        b = pl.program_id(0)

        @pl.when(b == 0)
        def _():
            dq_c[...] = jnp.zeros_like(dq_c)
            dk_c[...] = jnp.zeros_like(dk_c)
            dv_c[...] = jnp.zeros_like(dv_c)

        @pl.when(b == nblk)
        def _():
            out_ref[:, 0:D] = dq_c[...].astype(BF16)
            out_ref[:, D:2 * D] = dk_c[...].astype(BF16)
            out_ref[:, 2 * D:3 * D] = dv_c[...].astype(BF16)

        @pl.when(b < nblk)
        def _():
            has_prev = jnp.bitwise_and(b, nb - 1) != 0
            col = lax.broadcasted_iota(jnp.int32, (QB, 2 * QB), 1)
            dead = jnp.logical_and(col < QB, jnp.logical_not(has_prev))
            out_ref[:, 0:D] = dq_c[...].astype(BF16)
            lv = l_ref[...]
            dv_ = d_ref[...]
            for h in range(HEADS):
                sl = slice(h * HD, (h + 1) * HD)
                kk = jnp.concatenate([kp_ref[:, sl], kc_ref[:, sl]], axis=0)
                vv = jnp.concatenate([vp_ref[:, sl], vc_ref[:, sl]], axis=0)
                s_scr[h] = lax.dot_general(q_ref[:, sl], kk, (((1,), (1,)), ((), ())), preferred_element_type=F32)
                dp_scr[h] = lax.dot_general(do_ref[:, sl], vv, (((1,), (1,)), ((), ())),
                                            preferred_element_type=F32)
            for h in range(HEADS):
                s = s_scr[h] * scale + b_ref[h]
                s = jnp.where(dead, NEG, s)
                p = jnp.exp(s - lv[:, h:h + 1])
                ds_scr[h] = (p * (dp_scr[h] - dv_[:, h:h + 1]) * scale).astype(BF16)
                p_scr[h] = p.astype(BF16)
            for h in range(HEADS):
                sl = slice(h * HD, (h + 1) * HD)
                kk = jnp.concatenate([kp_ref[:, sl], kc_ref[:, sl]], axis=0)
                ds = ds_scr[h]
                dq_c[:, sl] = jnp.dot(ds, kk, preferred_element_type=F32)
                dkk = lax.dot_general(ds, q_ref[:, sl], (((0,), (0,)), ((), ())), preferred_element_type=F32)
                dvv = lax.dot_general(p_scr[h], do_ref[:, sl], (((0,), (0,)), ((), ())),
                                      preferred_element_type=F32)
                out_ref[:, D + h * HD:D + (h + 1) * HD] = (dk_c[:, sl] + dkk[:QB]).astype(BF16)
                out_ref[:, 2 * D + h * HD:2 * D + (h + 1) * HD] = (dv_c[:, sl] + dvv[:QB]).astype(BF16)
                dk_c[:, sl] = dkk[QB:]
                dv_c[:, sl] = dvv[QB:]

    last = nblk - 1

    def blk(colblk, prev):
        if prev:
            return pl.BlockSpec((QB, D), lambda b: (jnp.maximum(jnp.minimum(b, last) - 1, 0), colblk))
        return pl.BlockSpec((QB, D), lambda b: (jnp.minimum(b, last), colblk))

    stat = pl.BlockSpec((QB, HD), lambda b: (jnp.minimum(b, last), 0))
    return pl.pallas_call(
        kern, grid=(nblk + 1,),
        in_specs=[blk(0, False), blk(1, False), blk(1, True), blk(2, False), blk(2, True),
                  pl.BlockSpec((QB, D), lambda b: (jnp.minimum(b, last), 0)), stat, stat,
                  pl.BlockSpec((HEADS, QB, 2 * QB), lambda b: (0, 0, 0))],
        out_specs=pl.BlockSpec((QB, 3 * D), lambda b: (jnp.maximum(b - 1, 0), 0)),
        out_shape=jax.ShapeDtypeStruct((S, 3 * D), BF16),
        scratch_shapes=[pltpu.VMEM((QB, D), F32), pltpu.VMEM((QB, D), F32), pltpu.VMEM((QB, D), F32),
                        pltpu.VMEM((HEADS, QB, 2 * QB), F32), pltpu.VMEM((HEADS, QB, 2 * QB), F32),
                        pltpu.VMEM((HEADS, QB, 2 * QB), BF16), pltpu.VMEM((HEADS, QB, 2 * QB), BF16)],
        compiler_params=_cparams(1), name=name)(qkv_f, qkv_f, qkv_f, qkv_f, qkv_f, do_f, lse_f, delta_f, bias)


def _local_step(x, tgt, comm, attn_norm, ffn_norm, final_norm, pool_norm, pool_scale):
    S = x.shape[0]
    bias = _bias_table()
    g_attn = attn_norm.reshape(1, D)
    g_f0 = ffn_norm[0:1]
    g_f1 = ffn_norm[1:2]
    g_fin = final_norm.reshape(1, D)
    W = {}

    def ffn_fwd(xin, h, l, next_gain, target=None):
        gu, act = _ffn_up(h, W[f"gu{l}"], f"ffn_up{l}")
        return gu, act, _mm_res_norm(act, W[f"d{l}"], xin, next_gain, K=DFF, tm=512, tgt=target,
                                     name=f"ffn_down{l}")

    def ffn_bwd(dxo, xin, gain, h, gu, act, l, rs_group):
        dgu = _ffn_down_bwd(dxo, W[f"d{l}"], gu, f"ffn_down_bwd{l}")
        gw_d = _mm(act, dxo, mode="tn", M=DFF, N=D, K=S, tm=HCH, tn=D, tk=2048, out_dtype=BF16, name=f"gw_d{l}")
        gw_gu = _mm(dgu, h, mode="tn", M=2 * DFF, N=D, K=S, tm=HCH, tn=D, tk=2048, out_dtype=BF16, name=f"gw_gu{l}")
        token = comm.send_grads(rs_group, {f"d{l}": gw_d, f"gu{l}": gw_gu})
        return _mm_rms_bwd(dgu, W[f"gu{l}"], xin, gain, dxo, mode="nn", M=S, K=2 * DFF, tm=512, deps=(token,),
                           name=f"ffn_up_bwd{l}")

    nbs = [S // QB // dil for dil in DILS]
    hf = _rms_fwd_folded(x, g_attn, "rms_attn", deps=comm.ag_tokens)
    hf = [h.reshape(S, D) for h in hf]
    W.update(comm.weights(0, hf[0]))
    qkv_f, o_f, lse_f = [], [], []
    for g, dil in enumerate(DILS):
        qkv_f.append(_mm(hf[g], W["qkv"], mode="nt", M=S, N=3 * D, K=D, tm=2048, tn=1024, tk=D, out_dtype=BF16,
                         b_off=(3 * g, 0), name=f"qkv_proj{g}"))
        og, lg = _attn_fwd(qkv_f[g], bias[g], nbs[g], f"attn_fwd{g}")
        o_f.append(og if dil == 1 else og.reshape(dil, S // dil, D))
        lse_f.append(lg if dil == 1 else lg.reshape(dil, S // dil, HD))
    o, lse = _attn_merge(o_f, lse_f)
    W.update(comm.weights(1, o))
    x1, h1 = _mm_res_norm(o, W["wo"], x, g_f0, K=D, tm=1024, name="attn_out")
    gu0, act0, (x2, h2) = ffn_fwd(x1, h1, 0, pool_norm)

    W.update(comm.weights(2, x2))
    u = _mm(h2, W["wpi"], mode="nn", M=S, N=D, K=D, tm=1024, tn=D, tk=D, out_dtype=F32, name="pool_in")
    yd = _trail(u, backward=False, name="trail_fwd")
    x3 = _pool_out(yd, W["pg"], pool_scale, x2)
    h3 = _rms_fwd(x3, g_f1, "rms_ffn1")
    gu1, act1, (dx4, d_fin, lossvec) = ffn_fwd(x3, h3, 1, g_fin, target=tgt)

    dx3, d_f1 = ffn_bwd(dx4, x3, g_f1, h3, gu1, act1, 1, 0)
    dyd, d_scale, gw_pg = _pool_out_bwd(dx3, yd, W["pg"], pool_scale)
    du = _trail(dyd, backward=True, name="trail_bwd")
    gw_pi = _mm(h2, du, mode="tn", M=D, N=D, K=S, tm=D, tn=D, tk=2048, out_dtype=BF16, name="gw_pi")
    token = comm.send_grads(1, {"pg": gw_pg, "wpi": gw_pi})
    dx2, d_pool = _mm_rms_bwd(du, W["wpi"], x2, pool_norm, dx3, mode="nt", M=S, K=D, tm=1024, deps=(token,),
                              name="pool_in_bwd")
    dx1, d_f0 = ffn_bwd(dx2, x1, g_f0, h1, gu0, act0, 0, 2)

    gw_o = _mm(o, dx1, mode="tn", M=D, N=D, K=S, tm=D, tn=D, tk=2048, out_dtype=BF16, name="gw_o")
    do = _mm(dx1, W["wo"], mode="nt", M=S, N=D, K=D, tm=1024, tn=D, tk=D, out_dtype=F32, name="attn_out_bwd")
    do_f, lse_ff, delta_f = _attn_bwd_prep(do, o, lse)
    dqkv_f, gw_qkv = [], None
    for g in range(NGROUPS):
        dqkv_f.append(_attn_bwd(qkv_f[g], do_f[g].reshape(S, D), lse_ff[g].reshape(S, HD),
                                delta_f[g].reshape(S, HD), bias[g], nbs[g], f"attn_bwd{g}"))
        gw_qkv = _mm(dqkv_f[g], hf[g], mode="tn", M=3 * D, N=D, K=S, tm=1024, tn=D, tk=2048, out_dtype=BF16,
                     out_rows=NGROUPS * 3 * D, out_off=3 * g, out_prev=gw_qkv, name=f"gw_qkv{g}")
    token = comm.send_grads_pairwise({"wo": gw_o, "qkv": gw_qkv})
    dh0_f = [None] * NGROUPS
    for g in reversed(range(NGROUPS)):
        dh0_f[g] = _mm(dqkv_f[g], W["qkv"], mode="nn", M=S, N=D, K=3 * D, tm=1024, tn=D, tk=3 * D, out_dtype=F32,
                       b_off=(g, 0), deps=(token,), name=f"qkv_proj_bwd{g}")
        if g == NGROUPS - 1:
            token = comm.pass_grads(dh0_f[g])
    folded = [dh0_f[g].reshape(dil, S // dil, D) for g, dil in enumerate(DILS) if dil > 1]
    grad_x, d_attn = _rms_bwd(dh0_f[0], x, g_attn, dx1, "rms_attn_bwd", folded=folded)

    vec = jnp.concatenate([d_attn, d_f0, d_f1, d_fin, d_pool, d_scale, lossvec, jnp.zeros((1, D), F32)], axis=0)
    return grad_x, vec


def _mesh_pos():
    x, y, c = lax.axis_index("x"), lax.axis_index("y"), lax.axis_index("c")
    return x, y, c, 4 * x + 2 * y + c


def _peer(x, y, c, k):
    kx, ky, kc = (k >> 2) & 1, (k >> 1) & 1, k & 1
    px = 1 - x if kx else x
    py = 1 - y if ky else y
    pc = 1 - c if kc else c
    return (px, py, pc), 4 * px + 2 * py + pc


ANY = pl.BlockSpec(memory_space=pl.ANY)


HBM = pl.BlockSpec(memory_space=pltpu.HBM)
SEMS = pl.BlockSpec(memory_space=pltpu.SEMAPHORE)
EFFECT = pltpu.SideEffectType.DATAFLOW_SIDE_EFFECTING
NPEER = NDEV - 1

AG_GROUPS = (("qkv",), ("wo", "gu0", "d0"), ("wpi", "pg", "gu1", "d1"))
AG_ORDER = tuple(n for grp in AG_GROUPS for n in grp)
RS_GROUPS = (("d1", "gu1"), ("pg", "wpi"), ("d0", "gu0"), ("wo", "qkv"))


def _hbm(a):
    return pltpu.with_memory_space_constraint(a, pltpu.HBM)


def _remote(src, dst, send, recv, peer):
    return pltpu.make_async_remote_copy(src_ref=src, dst_ref=dst, send_sem=send, recv_sem=recv, device_id=peer,
                                        device_id_type=pl.DeviceIdType.MESH)


def _bcast_all(v, name, deps=()):
    W = v.shape[1]
    nd = len(deps)

    def kern(v_ref, *rest):
        o_ref, send, recv, lsem = rest[nd:]
        x, y, c, me = _mesh_pos()
        own = pltpu.make_async_copy(v_ref, o_ref.at[me], lsem)
        own.start()
        cps = [_remote(v_ref, o_ref.at[me], send.at[k - 1], recv.at[k - 1], _peer(x, y, c, k)[0])
               for k in range(1, NDEV)]
        for cp in cps:
            cp.start()
        for cp in cps:
            cp.wait_recv()
            cp.wait_send()
        own.wait()

    return pl.pallas_call(
        kern, in_specs=[ANY] * (1 + nd), out_specs=ANY, out_shape=jax.ShapeDtypeStruct((NDEV, 8, W), F32),
        scratch_shapes=[pltpu.SemaphoreType.DMA((NPEER,)), pltpu.SemaphoreType.DMA((NPEER,)),
                        pltpu.SemaphoreType.DMA(())],
        name=name)(v, *deps)


ALL_KS = tuple(range(1, NDEV))
AG_KS1 = (1, 2, 4, 6)
AG_KS2 = (2, 4, 6)
RS_KS_PAIR = (1, 3, 5, 7)
RS_KS_CHIPS = (2, 4, 6)


def _split_start(srcs, src_of, lands, copy_refs, name, deps=(), ks=ALL_KS, to=None):
    ns, n, nd, nk = len(srcs), len(lands), len(deps), len(ks)

    def body(*refs):
        ins, land = refs[:ns], refs[ns:ns + n]
        send, recv = refs[ns + n + nd], refs[ns + n + nd + 1]
        token = refs[-1]
        x, y, c, me = _mesh_pos()
        for j in range(n):
            for i, k in enumerate(ks):
                _, pid = _peer(x, y, c, k)
                dest, _ = _peer(x, y, c, k if to is None else to)
                src, dst = copy_refs(j, (land[j] if src_of[j] is None else ins[src_of[j]]), land[j], me, pid, i)
                _remote(src, dst, send.at[j * nk + i], recv.at[j * nk + i], dest).start()
        token[...] = jnp.zeros_like(token)

    outs = pl.pallas_call(
        body, name=name,
        out_shape=(pltpu.SemaphoreType.DMA((n * nk,)), pltpu.SemaphoreType.DMA((n * nk,)))
        + tuple(pltpu.HBM(a.shape, a.dtype) for a in srcs) + tuple(pltpu.HBM(a.shape, a.dtype) for a in lands)
        + (jax.ShapeDtypeStruct((8, 128), F32),),
        in_specs=(HBM,) * (ns + n) + (ANY,) * nd,
        out_specs=(SEMS, SEMS) + (HBM,) * (ns + n) + (pl.BlockSpec(memory_space=pltpu.VMEM),),
        input_output_aliases={i: 2 + i for i in range(ns + n)},
        compiler_params=pltpu.CompilerParams(has_side_effects=EFFECT),
    )(*[_hbm(a) for a in srcs], *[_hbm(a) for a in lands], *deps)
    return outs[0], outs[1], list(outs[2:2 + ns]), list(outs[2 + ns:2 + ns + n]), outs[-1]


def _split_wait(srcs, src_of, lands, send, recv, sem_rows, wait_refs, after, name, ks=ALL_KS):
    ns, n, nk = len(srcs), len(lands), len(ks)

    def body(*refs):
        ins, land = refs[:ns], refs[ns:ns + n]
        send_ref, recv_ref = refs[ns + n], refs[ns + n + 1]
        x, y, c, me = _mesh_pos()
        for j in range(n):
            for i, k in enumerate(ks):
                peer, _ = _peer(x, y, c, k)
                src, dst = wait_refs(j, (land[j] if src_of[j] is None else ins[src_of[j]]), land[j])
                sem = sem_rows[j] * nk + i
                cp = _remote(src, dst, send_ref.at[sem], recv_ref.at[sem], peer)
                cp.wait_send()
                cp.wait_recv()

    outs = pl.pallas_call(
        body, name=name,
        out_shape=tuple(pltpu.HBM(a.shape, a.dtype) for a in srcs) + tuple(pltpu.HBM(a.shape, a.dtype) for a in lands),
        in_specs=(HBM,) * (ns + n) + (SEMS, SEMS, ANY),
        out_specs=(HBM,) * (ns + n),
        input_output_aliases={i: i for i in range(ns + n)},
        compiler_params=pltpu.CompilerParams(has_side_effects=EFFECT),
    )(*srcs, *lands, send, recv, after)
    return list(outs[:ns]), list(outs[ns:])


def _place_transposed(w, me, name):
    rows = w.shape[1]
    nblk = rows // 128

    def kern(me_ref, w_ref, o_ref):
        o_ref[...] = w_ref[...].T.astype(BF16)

    grid_spec = pltpu.PrefetchScalarGridSpec(
        num_scalar_prefetch=1, grid=(nblk,),
        in_specs=[pl.BlockSpec((D, 128), lambda i, me_ref: (0, i))],
        out_specs=pl.BlockSpec((128, D), lambda i, me_ref: (me_ref[0] * nblk + i, 0)))
    return pl.pallas_call(
        kern, grid_spec=grid_spec, out_shape=jax.ShapeDtypeStruct((NDEV * rows, D), BF16),
        compiler_params=_cparams(1), name=name)(me.reshape(1).astype(jnp.int32), w)


class _Comm:
    def __init__(self, shards, me, deps=(), placed=None):
        self.me = me
        self.ag_land, self.ag_sems, self.ag_tokens = {}, {}, ()
        self.rs = []
        for part, names in enumerate((AG_GROUPS[0], AG_ORDER[len(AG_GROUPS[0]):])):
            rows = [SEC_ROWS[n] for n in names]
            mine = [shards[n] for n in names]
            if part:
                mine, deps = lax.optimization_barrier((mine, deps))
            lands = [placed[n] if n in (placed or {}) else
                     lax.dynamic_update_slice(lax.empty((NDEV * r, D), BF16), s.astype(BF16), (_shard_pos(n, me), 0))
                     for n, r, s in zip(names, rows, mine)]

            def copy_refs(j, src, land, me, pid, i, names=names, rows=rows):
                own = land.at[pl.ds(pl.multiple_of(_shard_pos(names[j], me), 16), rows[j])]
                return own, own

            send, recv, _, lands, token = _split_start([], [None] * len(names), lands, copy_refs, f"ag_start{part}",
                                                       deps=deps, ks=AG_KS1)
            deps = (token,)
            self.ag_tokens += (token,)
            for j, n in enumerate(names):
                self.ag_land[n] = lands[j]
                self.ag_sems[n] = (send, recv, j)

    def weights(self, group, after):
        names = AG_GROUPS[group]
        send, recv = self.ag_sems[names[0]][:2]
        idx = [self.ag_sems[n][2] for n in names]
        rows = [SEC_ROWS[n] for n in names]
        none = [None] * len(names)

        def wait_refs(j, src, land):
            return land.at[pl.ds(0, rows[j])], land.at[pl.ds(0, rows[j])]

        _, lands = _split_wait([], none, [self.ag_land[n] for n in names], send, recv, idx,
                               wait_refs, after, f"ag_wait{group}", ks=AG_KS1)

        def copy_refs(j, src, land, me, pid, i):
            theirs = land.at[pl.ds(pl.multiple_of(_shard_pos(names[j], pid), 16), rows[j])]
            return theirs, theirs

        send, recv, _, lands, token = _split_start([], none, lands, copy_refs, f"ag_pass{group}", ks=AG_KS2, to=1)
        _, lands = _split_wait([], none, lands, send, recv, list(range(len(names))), wait_refs, token,
                               f"ag_pass_wait{group}", ks=AG_KS2)
        return dict(zip(names, lands))

    def send_grads(self, group, gws):
        names = RS_GROUPS[group]
        rows = [SEC_ROWS[n] for n in names]
        grads = [gws[n] for n in names]
        me = self.me
        lands = [lax.dynamic_update_slice(
            lax.empty((NDEV, r, D), BF16),
            lax.dynamic_slice(g, (_shard_pos(n, me), 0), (r, D))[None], (me, 0, 0))
            for n, r, g in zip(names, rows, grads)]

        def copy_refs(j, src, land, me, pid, i):
            return src.at[pl.ds(pl.multiple_of(_shard_pos(names[j], pid), 16), rows[j])], land.at[me]

        send, recv, srcs, lands, token = _split_start(grads, list(range(len(names))), lands, copy_refs,
                                                      f"rs_start{group}")
        self.rs.append((names, rows, send, recv, srcs, lands, ALL_KS))
        return token

    def send_grads_pairwise(self, gws):
        names = RS_GROUPS[-1]
        rows = [SEC_ROWS[n] for n in names]
        grads = [gws[n] for n in names]
        idx = list(range(len(names)))
        lands = [lax.empty((len(RS_KS_PAIR), r, D), BF16) for r in rows]

        def copy_refs(j, src, land, me, pid, i):
            return src.at[pl.ds(pl.multiple_of(_shard_pos(names[j], pid), 16), rows[j])], land.at[i]

        send, recv, srcs, lands, token = _split_start(grads, idx, lands, copy_refs, "rs_pair_start",
                                                      ks=RS_KS_PAIR, to=1)
        self.pair = (names, rows, send, recv, srcs, lands)
        return token

    def pass_grads(self, after):
        names, rows, send, recv, srcs, lands = self.pair
        idx = list(range(len(names)))
        me = self.me

        def wait_refs(j, src, land):
            return src.at[pl.ds(0, rows[j])], land.at[0]

        srcs, lands = _split_wait(srcs, idx, lands, send, recv, idx, wait_refs, after, "rs_pair_wait", ks=RS_KS_PAIR)
        sums = []
        for n, r, g, got in zip(names, rows, srcs, lands):
            mine = jnp.stack([lax.dynamic_slice(g, (_shard_pos(n, jnp.bitwise_xor(me, k)), 0), (r, D))
                              for k in (0,) + RS_KS_CHIPS])
            sums.append(_pair_sum(mine, got, f"rs_pair_sum_{n}"))
        lands = [lax.dynamic_update_slice(lax.empty(p.shape, BF16), p[0:1], (0, 0, 0)) for p in sums]

        def copy_refs(j, src, land, me, pid, i):
            return src.at[i + 1], land.at[i + 1]

        send, recv, sums, lands, token = _split_start(sums, idx, lands, copy_refs, f"rs_start{len(RS_GROUPS) - 1}",
                                                      ks=RS_KS_CHIPS)
        self.rs.append((names, rows, send, recv, sums, lands, RS_KS_CHIPS))
        return token

    def received(self, group, after):
        names, rows, send, recv, srcs, lands, ks = self.rs[group]
        whole = srcs[0].ndim == 2

        def wait_refs(j, src, land):
            return (src.at[pl.ds(0, rows[j])] if whole else src.at[0]), land.at[0]

        _, lands = _split_wait(srcs, list(range(len(names))), lands, send, recv, list(range(len(names))), wait_refs,
                               after, f"rs_wait{group}", ks=ks)
        return dict(zip(names, lands))


def _pair_sum(a, b, name):
    n, rows, _ = a.shape
    tr = 384 if rows % 384 == 0 else rows

    def kern(a_ref, b_ref, o_ref):
        o_ref[...] = (a_ref[...].astype(F32) + b_ref[...].astype(F32)).astype(BF16)

    blk = pl.BlockSpec((1, tr, D), lambda i, t: (i, t, 0))
    return pl.pallas_call(
        kern, grid=(n, rows // tr), in_specs=[blk, blk], out_specs=blk,
        out_shape=jax.ShapeDtypeStruct(a.shape, BF16), compiler_params=_cparams(2), name=name)(a, b)


def _sum_contributions(r_ref):
    g = r_ref[0].astype(F32)
    for slot in range(1, r_ref.shape[0]):
        g = g + r_ref[slot].astype(F32)
    return g


def _adam_math(g, w, m, v):
    c1 = 1.0 / (1.0 - ADAM_B1 ** ADAM_STEP)
    c2 = 1.0 / (1.0 - ADAM_B2 ** ADAM_STEP)
    mn = ADAM_B1 * m + (1.0 - ADAM_B1) * g
    vn = ADAM_B2 * v + (1.0 - ADAM_B2) * (g * g)
    return -ADAM_LR * ((mn * c1) / (jnp.sqrt(vn * c2) + ADAM_EPS) + ADAM_WD * w), mn, vn


def _adamw(R, w, m, v, *, tr, name, layer=None, prev=None):
    rows, C = w.shape[-2:]
    nprev = 0 if prev is None else 4

    def kern(r_ref, w_ref, m_ref, v_ref, *rest):
        g_out, d_out, m_out, v_out = rest[nprev:]
        g = _sum_contributions(r_ref)
        g_out[...] = g
        d_out[...], m_out[...], v_out[...] = _adam_math(g, w_ref[...], m_ref[...], v_ref[...])

    if layer is None:
        tile = pl.BlockSpec((tr, C), lambda i: (i, 0))
    else:
        tile = pl.BlockSpec((None, tr, C), lambda i: (layer, i, 0))
    shp = jax.ShapeDtypeStruct(w.shape, F32)
    return pl.pallas_call(
        kern, grid=(rows // tr,),
        in_specs=[pl.BlockSpec((R.shape[0], tr, C), lambda i: (0, i, 0)), tile, tile, tile]
        + [pl.BlockSpec(memory_space=pl.ANY)] * nprev,
        out_specs=[tile] * 4, out_shape=[shp] * 4,
        input_output_aliases={4 + k: k for k in range(nprev)},
        compiler_params=_cparams(1), name=name)(R, w, m, v, *(prev or ()))


def _adamw_pool_group(R, w, m, v):
    rows = SEC_ROWS["pg"]

    def kern(r_ref, w_ref, m_ref, v_ref, g_out, d_out, m_out, v_out):
        g = _sum_contributions(r_ref)
        g_out[0] = g
        d_out[0], m_out[0], v_out[0] = _adam_math(g, w_ref[0], m_ref[0], v_ref[0])

    blk = pl.BlockSpec((1, rows, PGD), lambda i: (i, 0, 0))
    shp = jax.ShapeDtypeStruct((POOL_G, rows, PGD), F32)
    return pl.pallas_call(
        kern, grid=(POOL_G,),
        in_specs=[pl.BlockSpec((NDEV, rows, PGD), lambda i: (0, 0, i)), blk, blk, blk],
        out_specs=[blk] * 4, out_shape=[shp] * 4, compiler_params=_cparams(1), name="adamw_pg")(R, w, m, v)


def _grad_sum_t(R, name):
    rows = R.shape[1]
    tr = 128

    def kern(r_ref, o_ref):
        o_ref[...] = _sum_contributions(r_ref).T

    return pl.pallas_call(
        kern, grid=(rows // tr,), in_specs=[pl.BlockSpec((R.shape[0], tr, D), lambda i: (0, i, 0))],
        out_specs=pl.BlockSpec((D, tr), lambda i: (0, i)), out_shape=jax.ShapeDtypeStruct((D, rows), F32),
        compiler_params=_cparams(1), name=name)(R)


def _adam_plain(g, w, m, v, *, tr, name):
    rows, C = w.shape

    def kern(g_ref, w_ref, m_ref, v_ref, d_out, m_out, v_out):
        d_out[...], m_out[...], v_out[...] = _adam_math(g_ref[...], w_ref[...], m_ref[...], v_ref[...])

    tile = pl.BlockSpec((tr, C), lambda i: (i, 0))
    shp = jax.ShapeDtypeStruct((rows, C), F32)
    return pl.pallas_call(
        kern, grid=(rows // tr,), in_specs=[tile] * 4, out_specs=[tile] * 3, out_shape=[shp] * 3,
        compiler_params=_cparams(1), name=name)(g, w, m, v)


def _pack_sections(w_qkv, w_attn_out, w_pool_in, w_pool_group, w_ffn_gate_up, w_ffn_down):
    pg = w_pool_group[0].transpose(1, 0, 2).reshape(SEC_ROWS["pg"], D)
    return {"qkv": w_qkv[0].T, "wo": w_attn_out[0], "wpi": w_pool_in[0], "gu0": w_ffn_gate_up[0].T,
            "gu1": w_ffn_gate_up[1].T, "d0": w_ffn_down[0], "d1": w_ffn_down[1], "pg": pg}


def _vec_pack(attn_norm, ffn_norm, final_norm, pool_norm_sh, pool_scale_sh, me):
    def place(sh):
        return lax.dynamic_update_slice(jnp.zeros((1, D), F32), sh, (0, me * 128))
    return jnp.concatenate([attn_norm, ffn_norm, final_norm.reshape(1, D), place(pool_norm_sh),
                            place(pool_scale_sh), jnp.zeros((2, D), F32)], axis=0)


def _vec_unpack(p, me):
    def take(r):
        return lax.dynamic_slice(p[r:r + 1], (0, me * 128), (1, 128))
    return p[0:1], p[1:3], p[3], take(4), take(5)


def kernel(x, attn_norm, w_qkv, w_attn_out, pool_norm, w_pool_in, w_pool_group, pool_scale, ffn_norm, w_ffn_gate_up, w_ffn_down, final_norm, loss_target, m_attn_norm, m_w_qkv, m_w_attn_out, m_pool_norm, m_w_pool_in, m_w_pool_group, m_pool_scale, m_ffn_norm, m_w_ffn_gate_up, m_w_ffn_down, m_final_norm, v_attn_norm, v_w_qkv, v_w_attn_out, v_pool_norm, v_w_pool_in, v_w_pool_group, v_pool_scale, v_ffn_norm, v_w_ffn_gate_up, v_w_ffn_down, v_final_norm):
    me = 4 * lax.axis_index("x") + 2 * lax.axis_index("y") + lax.axis_index("c")

    pw = _pack_sections(w_qkv, w_attn_out, w_pool_in, w_pool_group, w_ffn_gate_up, w_ffn_down)
    vsh = jnp.concatenate([pool_norm, pool_scale, jnp.zeros((6, 128), F32)], axis=0)

    vg = _bcast_all(vsh, "gather_pool_vectors")
    comm = _Comm(pw, me, deps=(vg,), placed={"qkv": _place_transposed(w_qkv[0], me, "place_qkv")})
    pool_norm_full = vg[:, 0, :].reshape(1, D)
    pool_scale_full = vg[:, 1, :].reshape(1, D)

    grad_x, vec = _local_step(x[0], loss_target[0], comm, attn_norm, ffn_norm, final_norm,
                              pool_norm_full, pool_scale_full)

    small = ((attn_norm, ffn_norm, final_norm, pool_norm, pool_scale),
             (m_attn_norm, m_ffn_norm, m_final_norm, m_pool_norm, m_pool_scale),
             (v_attn_norm, v_ffn_norm, v_final_norm, v_pool_norm, v_pool_scale))
    small, grad_x = lax.optimization_barrier((small, grad_x))
    vw, vm, vv = (_vec_pack(*s, me) for s in small)

    gu_t = [jnp.swapaxes(a, 1, 2) for a in (w_ffn_gate_up, m_w_ffn_gate_up, v_w_ffn_gate_up)]
    res = {}
    gu_res, d_res = None, None
    vec_out = None
    after = grad_x
    for group in range(len(RS_GROUPS)):
        if group == len(RS_GROUPS) - 1:
            VR = _bcast_all(vec, "exchange_vector_grads", deps=(after,))
            vec_out = _adamw(VR, vw, vm, vv, tr=8, name="adamw_vec")
            after = vec_out[0]
        for n, R in comm.received(group, after).items():
            if n in ("d0", "d1"):
                d_res = _adamw(R, w_ffn_down, m_w_ffn_down, v_w_ffn_down, tr=352, name=f"adamw_{n}",
                               layer=int(n[1]), prev=d_res)
                after = d_res[0]
            elif n in ("gu0", "gu1"):
                gu_res = _adamw(R, *gu_t, tr=352, name=f"adamw_{n}", layer=int(n[2]), prev=gu_res)
                after = gu_res[0]
            elif n == "pg":
                out = _adamw_pool_group(R, w_pool_group[0], m_w_pool_group[0], v_w_pool_group[0])
                res["pg"] = tuple(a[None] for a in out)
                after = out[0]
            elif n in ("wo", "wpi"):
                w, m, v = ((w_attn_out, m_w_attn_out, v_w_attn_out) if n == "wo"
                           else (w_pool_in, m_w_pool_in, v_w_pool_in))
                res[n] = _adamw(R, w[0], m[0], v[0], tr=128, name=f"adamw_{n}")
                res[n] = tuple(a[None] for a in res[n])
                after = res[n][0]
            else:
                g = _grad_sum_t(R, "grad_sum_qkv")
                out = _adam_plain(g, w_qkv[0], m_w_qkv[0], v_w_qkv[0], tr=256, name="adamw_qkv")
                res["qkv"] = tuple(a[None] for a in (g,) + tuple(out))
                after = out[0]
    res["gu"] = tuple(jnp.swapaxes(a, 1, 2) for a in gu_res)
    res["d"] = tuple(d_res)

    outs = []
    for kind in range(4):
        an, fn, fin, pn, ps = _vec_unpack(vec_out[kind], me)
        outs.append((an, res["qkv"][kind], res["wo"][kind], pn, res["wpi"][kind], res["pg"][kind], ps, fn,
                     res["gu"][kind], res["d"][kind], fin))
    loss = 0.5 * jnp.sum(vec_out[0][6]) / D
    return (loss, grad_x[None]) + outs[0] + outs[1] + outs[2] + outs[3]
```

```python
import jax
import jax.numpy as jnp
from jax import lax
from jax.experimental import pallas as pl
from jax.experimental.pallas import tpu as pltpu

F32 = jnp.float32
BF16 = jnp.bfloat16

D = 1024
NDEV = 8
HEADS = 8
HD = 128
QB = 128
NGROUPS = 3
DILS = (1, 4, 16)
DFF = 2816
HCH = 1408
POOL_G = 4
PGD = 256
RMS_EPS = 1e-6
NEG = -1e30

ADAM_LR = 0.001
ADAM_B1 = 0.9
ADAM_B2 = 0.999
ADAM_EPS = 1e-08
ADAM_WD = 0.01
ADAM_STEP = 10

VMEM_LIMIT = 52 * 1024 * 1024

SECTIONS = (("qkv", 1152), ("wo", 128), ("wpi", 128), ("gu0", 704), ("gu1", 704),
            ("d0", 352), ("d1", 352), ("pg", 32))
LOC_OFF = {}
GLB_OFF = {}
_o = 0
for _n, _r in SECTIONS:
    LOC_OFF[_n] = _o
    GLB_OFF[_n] = _o * NDEV
    _o += _r
PACK_ROWS = _o
GLB_ROWS = PACK_ROWS * NDEV
SEC_ROWS = dict(SECTIONS)


def _cparams(n_grid):
    return pltpu.CompilerParams(dimension_semantics=("arbitrary",) * n_grid, vmem_limit_bytes=VMEM_LIMIT)


def _shard_pos(name, dev):
    n = SEC_ROWS[name]
    if name in ("gu0", "gu1"):
        return ((dev % 4) // 2) * (2 * HCH) + (dev // 4) * HCH + (dev % 2) * n
    return dev * n


def _mm(a, b, *, mode, M, N, K, tm, tn, tk, out_dtype, name, a_off=(0, 0), b_off=(0, 0), res=None,
        out_rows=None, out_off=0, out_prev=None, deps=()):
    nm, nn, nk = M // tm, N // tn, K // tk
    assert nm * tm == M and nn * tn == N and nk * tk == K
    if mode == "nn":
        a_bs, b_bs = (tm, tk), (tk, tn)
        a_ix = lambda i, j, k: (i, k)
        b_ix = lambda i, j, k: (k, j)
        dims = (((1,), (0,)), ((), ()))
    elif mode == "nt":
        a_bs, b_bs = (tm, tk), (tn, tk)
        a_ix = lambda i, j, k: (i, k)
        b_ix = lambda i, j, k: (j, k)
        dims = (((1,), (1,)), ((), ()))
    else:
        a_bs, b_bs = (tk, tm), (tk, tn)
        a_ix = lambda i, j, k: (k, i)
        b_ix = lambda i, j, k: (k, j)
        dims = (((0,), (0,)), ((), ()))

    def spec(bs, ix, off):
        def im(i, j, k):
            r, c = ix(i, j, k)
            return (r + off[0], c + off[1])
        return pl.BlockSpec(bs, im)

    in_specs = [spec(a_bs, a_ix, a_off), spec(b_bs, b_ix, b_off)]
    args = [a, b]
    if res is not None:
        in_specs.append(pl.BlockSpec((tm, tn), lambda i, j, k: (i, j)))
        args.append(res)
    out_shape = jax.ShapeDtypeStruct((M if out_rows is None else out_rows, N), out_dtype)
    out_spec = pl.BlockSpec((tm, tn), lambda i, j, k: (i + out_off, j))
    has_res = res is not None
    extra = list(deps) + ([out_prev] if out_prev is not None else [])
    for dep in extra:
        in_specs.append(pl.BlockSpec(memory_space=pl.ANY))
        args.append(dep)
    o_pos = 2 + int(has_res) + len(extra)
    aliases = {len(args) - 1: 0} if out_prev is not None else {}

    def kern(*refs):
        a_ref, b_ref = refs[0], refs[1]
        res_ref = refs[2] if has_res else None
        o_ref = refs[o_pos]
        av = a_ref[...]
        bv = b_ref[...]
        if av.dtype != BF16:
            av = av.astype(BF16)
        if bv.dtype != BF16:
            bv = bv.astype(BF16)
        part = lax.dot_general(av, bv, dims, preferred_element_type=F32)

        def write(val):
            if has_res:
                val = val + res_ref[...]
            o_ref[...] = val.astype(out_dtype)

        if nk == 1:
            write(part)
        else:
            acc_ref = refs[-1]
            k = pl.program_id(2)

            @pl.when(k == 0)
            def _():
                acc_ref[...] = part

            @pl.when(k > 0)
            def _():
                acc_ref[...] += part

            @pl.when(k == nk - 1)
            def _():
                write(acc_ref[...])

    scratch = [pltpu.VMEM((tm, tn), F32)] if nk > 1 else []
    return pl.pallas_call(
        kern, grid=(nm, nn, nk), in_specs=in_specs, out_specs=out_spec, out_shape=out_shape,
        scratch_shapes=scratch, input_output_aliases=aliases, compiler_params=_cparams(3), name=name)(*args)


def _mm_rms_bwd(a, b, x, g, dres, *, mode, M, K, tm, name, b_off=(0, 0), deps=()):
    nd = len(deps)
    b_bs = (K, D) if mode == "nn" else (D, K)
    dims = (((1,), (0,)), ((), ())) if mode == "nn" else (((1,), (1,)), ((), ()))

    def kern(a_ref, b_ref, x_ref, g_ref, dres_ref, *rest):
        dx_ref, dg_ref = rest[nd:]
        i = pl.program_id(0)
        av = a_ref[...]
        if av.dtype != BF16:
            av = av.astype(BF16)
        dhv = lax.dot_general(av, b_ref[...], dims, preferred_element_type=F32)
        xv = x_ref[...]
        r = lax.rsqrt(jnp.mean(xv * xv, axis=-1, keepdims=True) + RMS_EPS)
        xhat = xv * r
        gy = dhv * g_ref[...]
        dx_ref[...] = dres_ref[...] + r * (gy - xhat * jnp.mean(gy * xhat, axis=-1, keepdims=True))
        part = jnp.sum(dhv * xhat, axis=0, keepdims=True)

        @pl.when(i == 0)
        def _():
            dg_ref[...] = part

        @pl.when(i > 0)
        def _():
            dg_ref[...] += part

    row = pl.BlockSpec((tm, D), lambda i: (i, 0))
    vec = pl.BlockSpec((1, D), lambda i: (0, 0))
    return pl.pallas_call(
        kern, grid=(M // tm,),
        in_specs=[pl.BlockSpec((tm, K), lambda i: (i, 0)),
                  pl.BlockSpec(b_bs, lambda i: b_off, pipeline_mode=pl.Buffered(1)), row, vec, row]
        + [pl.BlockSpec(memory_space=pl.ANY)] * nd,
        out_specs=[row, vec],
        out_shape=[jax.ShapeDtypeStruct((M, D), F32), jax.ShapeDtypeStruct((1, D), F32)],
        compiler_params=_cparams(1), name=name)(a, b, x, g, dres, *deps)


def _mm_res_norm(a, b, res, g, *, K, tm, name, b_off=(0, 0), tgt=None):
    M = a.shape[0]
    head = tgt is not None

    def kern(a_ref, b_ref, res_ref, g_ref, *rest):
        xv = res_ref[...] + jnp.dot(a_ref[...], b_ref[...], preferred_element_type=F32)
        gv = g_ref[...]
        r = lax.rsqrt(jnp.mean(xv * xv, axis=-1, keepdims=True) + RMS_EPS)
        xhat = xv * r
        if not head:
            xo_ref, h_ref = rest
            xo_ref[...] = xv
            h_ref[...] = (xhat * gv).astype(BF16)
            return
        t_ref, dx_ref, dg_ref, ls_ref = rest
        i = pl.program_id(0)
        e = xhat * gv - t_ref[...]
        dy = e * (1.0 / D)
        gy = dy * gv
        dx_ref[...] = r * (gy - xhat * jnp.mean(gy * xhat, axis=-1, keepdims=True))
        dgp = jnp.sum(dy * xhat, axis=0, keepdims=True)
        lsp = jnp.sum(e * e, axis=0, keepdims=True)

        @pl.when(i == 0)
        def _():
            dg_ref[...] = dgp
            ls_ref[...] = lsp

        @pl.when(i > 0)
        def _():
            dg_ref[...] += dgp
            ls_ref[...] += lsp

    row = pl.BlockSpec((tm, D), lambda i: (i, 0))
    vec = pl.BlockSpec((1, D), lambda i: (0, 0))
    in_specs = [pl.BlockSpec((tm, K), lambda i: (i, 0)),
                pl.BlockSpec((K, D), lambda i: b_off, pipeline_mode=pl.Buffered(1)), row, vec]
    if head:
        return pl.pallas_call(
            kern, grid=(M // tm,), in_specs=in_specs + [row], out_specs=[row, vec, vec],
            out_shape=[jax.ShapeDtypeStruct((M, D), F32), jax.ShapeDtypeStruct((1, D), F32),
                       jax.ShapeDtypeStruct((1, D), F32)],
            compiler_params=_cparams(1), name=name)(a, b, res, g, tgt)
    return pl.pallas_call(
        kern, grid=(M // tm,), in_specs=in_specs, out_specs=[row, row],
        out_shape=[jax.ShapeDtypeStruct((M, D), F32), jax.ShapeDtypeStruct((M, D), BF16)],
        compiler_params=_cparams(1), name=name)(a, b, res, g)


def _rms_fwd(x, g, name, deps=()):
    S = x.shape[0]
    tr = 512

    def kern(x_ref, g_ref, *rest):
        h_ref = rest[-1]
        xv = x_ref[...]
        r = lax.rsqrt(jnp.mean(xv * xv, axis=-1, keepdims=True) + RMS_EPS)
        h_ref[...] = (xv * r * g_ref[...]).astype(BF16)

    return pl.pallas_call(
        kern, grid=(S // tr,),
        in_specs=[pl.BlockSpec((tr, D), lambda i: (i, 0)), pl.BlockSpec((1, D), lambda i: (0, 0))]
        + [pl.BlockSpec(memory_space=pl.ANY)] * len(deps),
        out_specs=pl.BlockSpec((tr, D), lambda i: (i, 0)),
        out_shape=jax.ShapeDtypeStruct((S, D), BF16), compiler_params=_cparams(1), name=name)(x, g, *deps)


def _chunks_put(scr, val):
    for c in range(scr.shape[0]):
        scr[c] = val[:, c * 128:(c + 1) * 128]


def _chunks_get(scr):
    return jnp.concatenate([scr[c] for c in range(scr.shape[0])], axis=1)


def _chunks_rows(scr, r, n, dil):
    return jnp.concatenate([scr.at[c][pl.ds(r, n, stride=dil), :] for c in range(scr.shape[0])], axis=1)


def _chunks_add_rows(scr, val, r, n, dil, accumulate):
    for c in range(scr.shape[0]):
        rows = pl.ds(r, n, stride=dil)
        piece = val[:, c * 128:(c + 1) * 128]
        tile = scr.at[c]
        tile[rows, :] = tile[rows, :] + piece if accumulate else piece


def _rms_fwd_folded(x, g, name, deps=()):
    S = x.shape[0]
    tr = 512
    dils = DILS[1:]

    def kern(x_ref, g_ref, *rest):
        outs, scr = rest[len(deps):-1], rest[-1]
        xv = x_ref[...]
        r = lax.rsqrt(jnp.mean(xv * xv, axis=-1, keepdims=True) + RMS_EPS)
        h = (xv * r * g_ref[...]).astype(BF16)
        outs[0][...] = h
        _chunks_put(scr, h.astype(F32))
        for o_ref, dil in zip(outs[1:], dils):
            for res in range(dil):
                o_ref[res] = _chunks_rows(scr, res, tr // dil, dil).astype(BF16)

    return pl.pallas_call(
        kern, grid=(S // tr,),
        in_specs=[pl.BlockSpec((tr, D), lambda i: (i, 0)), pl.BlockSpec((1, D), lambda i: (0, 0))]
        + [pl.BlockSpec(memory_space=pl.ANY)] * len(deps),
        out_specs=[pl.BlockSpec((tr, D), lambda i: (i, 0))]
        + [pl.BlockSpec((dil, tr // dil, D), lambda i: (0, i, 0)) for dil in dils],
        out_shape=[jax.ShapeDtypeStruct((S, D), BF16)]
        + [jax.ShapeDtypeStruct((dil, S // dil, D), BF16) for dil in dils],
        scratch_shapes=[pltpu.VMEM((D // 128, tr, 128), F32)],
        compiler_params=_cparams(1), name=name)(x, g, *deps)


def _rms_bwd(dh, x, g, dres, name, folded=()):
    S = x.shape[0]
    tr = 512
    nf = len(folded)

    def kern(dh_ref, *rest):
        f_refs = rest[:nf]
        x_ref, g_ref, dres_ref, dx_ref, dg_ref = rest[nf:nf + 5]
        i = pl.program_id(0)
        xv = x_ref[...]
        if nf:
            acc_ref = rest[nf + 5]
            _chunks_put(acc_ref, dh_ref[...].astype(F32))
            for f_ref in f_refs:
                dil = f_ref.shape[0]
                for res in range(dil):
                    _chunks_add_rows(acc_ref, f_ref[res], res, tr // dil, dil, True)
            dhv = _chunks_get(acc_ref)
        else:
            dhv = dh_ref[...].astype(F32)
        r = lax.rsqrt(jnp.mean(xv * xv, axis=-1, keepdims=True) + RMS_EPS)
        xhat = xv * r
        gy = dhv * g_ref[...]
        dx_ref[...] = dres_ref[...] + r * (gy - xhat * jnp.mean(gy * xhat, axis=-1, keepdims=True))
        part = jnp.sum(dhv * xhat, axis=0, keepdims=True)

        @pl.when(i == 0)
        def _():
            dg_ref[...] = part

        @pl.when(i > 0)
        def _():
            dg_ref[...] += part

    row = pl.BlockSpec((tr, D), lambda i: (i, 0))
    vec = pl.BlockSpec((1, D), lambda i: (0, 0))
    fspecs = [pl.BlockSpec((f.shape[0], tr // f.shape[0], D), lambda i: (0, i, 0)) for f in folded]
    return pl.pallas_call(
        kern, grid=(S // tr,), in_specs=[row] + fspecs + [row, vec, row], out_specs=[row, vec],
        out_shape=[jax.ShapeDtypeStruct((S, D), F32), jax.ShapeDtypeStruct((1, D), F32)],
        scratch_shapes=[pltpu.VMEM((D // 128, tr, 128), F32)] if nf else [],
        compiler_params=_cparams(1), name=name)(dh, *folded, x, g, dres)


def _ffn_up(h, G, name):
    S = h.shape[0]
    tm = 512
    nj = DFF // HCH

    def kern(h_ref, w_ref, gu_ref, act_ref):
        gu = lax.dot_general(h_ref[...], w_ref[...], (((1,), (1,)), ((), ())), preferred_element_type=F32)
        gu_ref[...] = gu.astype(BF16)
        gate = gu[:, :HCH]
        up = gu[:, HCH:]
        act_ref[...] = (gate * jax.nn.sigmoid(gate) * up).astype(BF16)

    return pl.pallas_call(
        kern, grid=(nj, S // tm),
        in_specs=[pl.BlockSpec((tm, D), lambda j, i: (i, 0)),
                  pl.BlockSpec((2 * HCH, D), lambda j, i: (j, 0))],
        out_specs=[pl.BlockSpec((tm, 2 * HCH), lambda j, i: (i, j)),
                   pl.BlockSpec((tm, HCH), lambda j, i: (i, j))],
        out_shape=[jax.ShapeDtypeStruct((S, 2 * DFF), BF16), jax.ShapeDtypeStruct((S, DFF), BF16)],
        compiler_params=_cparams(2), name=name)(h, G)


def _ffn_down_bwd(dx, G, gu, name):
    S = dx.shape[0]
    tm = 512
    nj = DFF // HCH

    def kern(dx_ref, w_ref, gu_ref, o_ref):
        dxv = dx_ref[...].astype(BF16)
        for j in range(nj):
            c0 = 2 * HCH * j
            dact = lax.dot_general(dxv, w_ref[HCH * j:HCH * (j + 1), :], (((1,), (1,)), ((), ())),
                                   preferred_element_type=F32)
            gate = gu_ref[:, c0:c0 + HCH].astype(F32)
            up = gu_ref[:, c0 + HCH:c0 + 2 * HCH].astype(F32)
            sig = jax.nn.sigmoid(gate)
            silu = gate * sig
            o_ref[:, c0:c0 + HCH] = (dact * up * (sig * (1.0 + gate * (1.0 - sig)))).astype(BF16)
            o_ref[:, c0 + HCH:c0 + 2 * HCH] = (dact * silu).astype(BF16)

    row = pl.BlockSpec((tm, 2 * DFF), lambda i: (i, 0))
    return pl.pallas_call(
        kern, grid=(S // tm,),
        in_specs=[pl.BlockSpec((tm, D), lambda i: (i, 0)),
                  pl.BlockSpec((DFF, D), lambda i: (0, 0), pipeline_mode=pl.Buffered(1)), row],
        out_specs=row, out_shape=jax.ShapeDtypeStruct((S, 2 * DFF), BF16),
        compiler_params=_cparams(1), name=name)(dx, G, gu)


def _trail(u, *, backward, name):
    S = u.shape[0]

    def kern(u_ref, o_ref):
        g = pl.program_id(0)
        for grp in range(POOL_G):
            @pl.when(g == grp)
            def _(grp=grp):
                uv = u_ref[...].astype(F32)
                row = lax.broadcasted_iota(jnp.int32, uv.shape, 0)
                cnt = jnp.minimum(row + 1, 2 << grp).astype(F32)
                s = uv / cnt if backward else uv
                for k in (1, 2, 4, 8)[:grp + 1]:
                    if backward:
                        sh = jnp.where(row < S - k, pltpu.roll(s, S - k, 0), 0.0)
                    else:
                        sh = jnp.where(row >= k, pltpu.roll(s, k, 0), 0.0)
                    s = s + sh
                if backward:
                    o_ref[...] = (s - uv).astype(BF16)
                else:
                    o_ref[...] = (s / cnt - uv).astype(BF16)

    blk = pl.BlockSpec((S, PGD), lambda g: (0, g))
    return pl.pallas_call(
        kern, grid=(POOL_G,), in_specs=[blk], out_specs=blk,
        out_shape=jax.ShapeDtypeStruct((S, D), BF16), compiler_params=_cparams(1), name=name)(u)


def _pool_out(yd, G, scale, xres):
    S = yd.shape[0]
    tm = min(S, 4096)

    def kern(y_ref, w_ref, s_ref, x_ref, o_ref):
        z = jnp.dot(y_ref[...], w_ref[...], preferred_element_type=F32)
        o_ref[...] = x_ref[...] + z * s_ref[...]

    tile = pl.BlockSpec((tm, PGD), lambda i, g: (i, g))
    return pl.pallas_call(
        kern, grid=(S // tm, POOL_G),
        in_specs=[tile, pl.BlockSpec((PGD, PGD), lambda i, g: (0, g)),
                  pl.BlockSpec((1, PGD), lambda i, g: (0, g)), tile],
        out_specs=tile, out_shape=jax.ShapeDtypeStruct((S, D), F32),
        compiler_params=_cparams(2), name="pool_out")(yd, G, scale, xres)


def _pool_out_bwd(dz, yd, G, scale):
    S = yd.shape[0]
    tm = min(S, 4096)
    ni = S // tm

    def kern(dz_ref, y_ref, w_ref, s_ref, dy_ref, ds_ref, dw_ref, acc_ref):
        i = pl.program_id(1)
        dzv = dz_ref[...]
        yv = y_ref[...]
        wv = w_ref[...]
        zraw = jnp.dot(yv, wv, preferred_element_type=F32)
        dsp = jnp.sum(dzv * zraw, axis=0, keepdims=True)
        dzr = (dzv * s_ref[...]).astype(BF16)
        dy_ref[...] = lax.dot_general(dzr, wv, (((1,), (1,)), ((), ())), preferred_element_type=F32)
        dwp = lax.dot_general(yv, dzr, (((0,), (0,)), ((), ())), preferred_element_type=F32)

        @pl.when(i == 0)
        def _():
            ds_ref[...] = dsp
            acc_ref[...] = dwp

        @pl.when(i > 0)
        def _():
            ds_ref[...] += dsp
            acc_ref[...] += dwp

        @pl.when(i == ni - 1)
        def _():
            dw_ref[...] = acc_ref[...].astype(BF16)

    tile = pl.BlockSpec((tm, PGD), lambda g, i: (i, g))
    return pl.pallas_call(
        kern, grid=(POOL_G, ni),
        in_specs=[tile, tile, pl.BlockSpec((PGD, PGD), lambda g, i: (0, g)),
                  pl.BlockSpec((1, PGD), lambda g, i: (0, g))],
        out_specs=[tile, pl.BlockSpec((1, PGD), lambda g, i: (0, g)),
                   pl.BlockSpec((PGD, PGD), lambda g, i: (0, g))],
        out_shape=[jax.ShapeDtypeStruct((S, D), F32), jax.ShapeDtypeStruct((1, D), F32),
                   jax.ShapeDtypeStruct((PGD, D), BF16)],
        scratch_shapes=[pltpu.VMEM((PGD, PGD), F32)],
        compiler_params=_cparams(2), name="pool_out_bwd")(dz, yd, G, scale)


def _bias_table():
    qi = jnp.arange(QB)[:, None]
    ki = jnp.arange(2 * QB)[None, :]
    delta = QB + qi - ki
    inband = (delta >= 0) & (delta <= QB)
    n = NGROUPS * HEADS
    slopes = jnp.exp2(-8.0 * jnp.arange(1, n + 1, dtype=F32) / n).reshape(NGROUPS, HEADS)
    dil = jnp.asarray(DILS, F32)
    bias = -slopes[:, :, None, None] * (delta.astype(F32)[None, None] * dil[:, None, None, None])
    return jnp.where(inband[None, None], bias, NEG)


def _attn_fwd(qkv_f, bias, nb, name):
    S = qkv_f.shape[0]
    nblk = S // QB
    scale = HD ** -0.5

    def kern(q_ref, kc_ref, kp_ref, vc_ref, vp_ref, b_ref, o_ref, l_ref, s_scr, p_scr, r_scr):
        b = pl.program_id(0)
        has_prev = jnp.bitwise_and(b, nb - 1) != 0
        col = lax.broadcasted_iota(jnp.int32, (QB, 2 * QB), 1)
        dead = jnp.logical_and(col < QB, jnp.logical_not(has_prev))
        lane = lax.broadcasted_iota(jnp.int32, (QB, HD), 1)
        lse_all = jnp.zeros((QB, HD), F32)
        for h in range(HEADS):
            sl = slice(h * HD, (h + 1) * HD)
            kk = jnp.concatenate([kp_ref[:, sl], kc_ref[:, sl]], axis=0)
            s_scr[h] = lax.dot_general(q_ref[:, sl], kk, (((1,), (1,)), ((), ())), preferred_element_type=F32)
        for h in range(HEADS):
            s = s_scr[h] * scale + b_ref[h]
            s = jnp.where(dead, NEG, s)
            m = jnp.max(s, axis=-1, keepdims=True)
            p = jnp.exp(s - m)
            den = jnp.sum(p, axis=-1, keepdims=True)
            p_scr[h] = p.astype(BF16)
            r_scr[h] = jnp.broadcast_to(1.0 / den, (QB, HD))
            lse_all = jnp.where(lane == h, m + jnp.log(den), lse_all)
        for h in range(HEADS):
            sl = slice(h * HD, (h + 1) * HD)
            vv = jnp.concatenate([vp_ref[:, sl], vc_ref[:, sl]], axis=0)
            o = jnp.dot(p_scr[h], vv, preferred_element_type=F32) * r_scr[h]
            o_ref[:, sl] = o.astype(BF16)
        l_ref[...] = lse_all

    def blk(colblk, prev):
        if prev:
            return pl.BlockSpec((QB, D), lambda b: (jnp.maximum(b - 1, 0), colblk))
        return pl.BlockSpec((QB, D), lambda b: (b, colblk))

    return pl.pallas_call(
        kern, grid=(nblk,),
        in_specs=[blk(0, False), blk(1, False), blk(1, True), blk(2, False), blk(2, True),
                  pl.BlockSpec((HEADS, QB, 2 * QB), lambda b: (0, 0, 0))],
        out_specs=[pl.BlockSpec((QB, D), lambda b: (b, 0)), pl.BlockSpec((QB, HD), lambda b: (b, 0))],
        out_shape=[jax.ShapeDtypeStruct((S, D), BF16), jax.ShapeDtypeStruct((S, HD), F32)],
        scratch_shapes=[pltpu.VMEM((HEADS, QB, 2 * QB), F32), pltpu.VMEM((HEADS, QB, 2 * QB), BF16),
                        pltpu.VMEM((HEADS, QB, HD), F32)],
        compiler_params=_cparams(1), name=name)(qkv_f, qkv_f, qkv_f, qkv_f, qkv_f, bias)


def _natural(ref, scr, tm):
    dil = ref.shape[0]
    for res in range(dil):
        _chunks_add_rows(scr, ref[res].astype(F32), res, tm // dil, dil, False)
    return _chunks_get(scr)


def _attn_merge(os, lses):
    S = os[0].shape[0]
    tm = 512

    def kern(o0, o1, o2, l0, l1, l2, om_ref, lm_ref, ls1, ls2, os1, os2):
        la = l0[...]
        lb = _natural(l1, ls1, tm)
        lc = _natural(l2, ls2, tm)
        m = jnp.maximum(jnp.maximum(la, lb), lc)
        e0, e1, e2 = jnp.exp(la - m), jnp.exp(lb - m), jnp.exp(lc - m)
        tot = e0 + e1 + e2
        lm_ref[...] = m + jnp.log(tot)
        w0, w1, w2 = e0 / tot, e1 / tot, e2 / tot
        for res in range(o1.shape[0]):
            _chunks_add_rows(os1, o1[res].astype(F32), res, tm // o1.shape[0], o1.shape[0], False)
        for res in range(o2.shape[0]):
            _chunks_add_rows(os2, o2[res].astype(F32), res, tm // o2.shape[0], o2.shape[0], False)
        for h in range(HEADS):
            sl = slice(h * HD, (h + 1) * HD)
            acc = w0[:, h:h + 1] * o0[:, sl].astype(F32) + w1[:, h:h + 1] * os1[h] + w2[:, h:h + 1] * os2[h]
            om_ref[:, sl] = acc.astype(BF16)

    def spec(a, c):
        if a.ndim == 2:
            return pl.BlockSpec((tm, c), lambda i: (i, 0))
        return pl.BlockSpec((a.shape[0], tm // a.shape[0], c), lambda i: (0, i, 0))

    return pl.pallas_call(
        kern, grid=(S // tm,),
        in_specs=[spec(a, D) for a in os] + [spec(a, HD) for a in lses],
        out_specs=[pl.BlockSpec((tm, D), lambda i: (i, 0)), pl.BlockSpec((tm, HD), lambda i: (i, 0))],
        out_shape=[jax.ShapeDtypeStruct((S, D), BF16), jax.ShapeDtypeStruct((S, HD), F32)],
        scratch_shapes=[pltpu.VMEM((1, tm, HD), F32), pltpu.VMEM((1, tm, HD), F32),
                        pltpu.VMEM((HEADS, tm, HD), F32), pltpu.VMEM((HEADS, tm, HD), F32)],
        compiler_params=_cparams(1), name="attn_merge")(*os, *lses)


def _attn_bwd_prep(do, o, lse):
    S = o.shape[0]
    tm = 512
    dils = DILS[1:]

    def kern(do_ref, o_ref, l_ref, *rest):
        do_outs, l_outs, d_outs = rest[0:3], rest[3:5], rest[5:8]
        do_scr, l_scr, d_scr = rest[8:11]
        lane = lax.broadcasted_iota(jnp.int32, (tm, HD), 1)
        acc = jnp.zeros((tm, HD), F32)
        for h in range(HEADS):
            sl = slice(h * HD, (h + 1) * HD)
            prod = do_ref[:, sl] * o_ref[:, sl].astype(F32)
            acc = jnp.where(lane == h, jnp.sum(prod, axis=-1, keepdims=True), acc)
        d_scr[0] = acc
        l_scr[0] = l_ref[...]
        _chunks_put(do_scr, do_ref[...])
        do_outs[0][...] = do_ref[...].astype(BF16)
        d_outs[0][...] = acc
        for j, dil in enumerate(dils):
            for res in range(dil):
                n = tm // dil
                do_outs[1 + j][res] = _chunks_rows(do_scr, res, n, dil).astype(BF16)
                l_outs[j][res] = _chunks_rows(l_scr, res, n, dil)
                d_outs[1 + j][res] = _chunks_rows(d_scr, res, n, dil)

    def nat(c):
        return pl.BlockSpec((tm, c), lambda i: (i, 0))

    def fol(dil, c):
        return pl.BlockSpec((dil, tm // dil, c), lambda i: (0, i, 0))

    def shapes(c, dt, with_natural):
        first = [jax.ShapeDtypeStruct((S, c), dt)] if with_natural else []
        return first + [jax.ShapeDtypeStruct((dil, S // dil, c), dt) for dil in dils]

    outs = pl.pallas_call(
        kern, grid=(S // tm,), in_specs=[nat(D), nat(D), nat(HD)],
        out_specs=[nat(D)] + [fol(dil, D) for dil in dils] + [fol(dil, HD) for dil in dils]
        + [nat(HD)] + [fol(dil, HD) for dil in dils],
        out_shape=shapes(D, BF16, True) + shapes(HD, F32, False) + shapes(HD, F32, True),
        scratch_shapes=[pltpu.VMEM((HEADS, tm, HD), F32), pltpu.VMEM((1, tm, HD), F32), pltpu.VMEM((1, tm, HD), F32)],
        compiler_params=_cparams(1), name="attn_bwd_prep")(do, o, lse)
    return outs[0:3], [lse] + list(outs[3:5]), outs[5:8]


def _attn_bwd(qkv_f, do_f, lse_f, delta_f, bias, nb, name):
    S = qkv_f.shape[0]
    nblk = S // QB
    scale = HD ** -0.5

    def kern(q_ref, kc_ref, kp_ref, vc_ref, vp_ref, do_ref, l_ref, d_ref, b_ref, out_ref, dq_c, dk_c, dv_c,
             s_scr, dp_scr, ds_scr, p_scr):
        b = pl.program_id(0)

        @pl.when(b == 0)
        def _():
            dq_c[...] = jnp.zeros_like(dq_c)
            dk_c[...] = jnp.zeros_like(dk_c)
            dv_c[...] = jnp.zeros_like(dv_c)

        @pl.when(b == nblk)
        def _():
            out_ref[:, 0:D] = dq_c[...].astype(BF16)
            out_ref[:, D:2 * D] = dk_c[...].astype(BF16)
            out_ref[:, 2 * D:3 * D] = dv_c[...].astype(BF16)

        @pl.when(b < nblk)
        def _():
            has_prev = jnp.bitwise_and(b, nb - 1) != 0
            col = lax.broadcasted_iota(jnp.int32, (QB, 2 * QB), 1)
            dead = jnp.logical_and(col < QB, jnp.logical_not(has_prev))
            out_ref[:, 0:D] = dq_c[...].astype(BF16)
            lv = l_ref[...]
            dv_ = d_ref[...]
            for h in range(HEADS):
                sl = slice(h * HD, (h + 1) * HD)
                kk = jnp.concatenate([kp_ref[:, sl], kc_ref[:, sl]], axis=0)
                vv = jnp.concatenate([vp_ref[:, sl], vc_ref[:, sl]], axis=0)
                s_scr[h] = lax.dot_general(q_ref[:, sl], kk, (((1,), (1,)), ((), ())), preferred_element_type=F32)
                dp_scr[h] = lax.dot_general(do_ref[:, sl], vv, (((1,), (1,)), ((), ())),
                                            preferred_element_type=F32)
            for h in range(HEADS):
                s = s_scr[h] * scale + b_ref[h]
                s = jnp.where(dead, NEG, s)
                p = jnp.exp(s - lv[:, h:h + 1])
                ds_scr[h] = (p * (dp_scr[h] - dv_[:, h:h + 1]) * scale).astype(BF16)
                p_scr[h] = p.astype(BF16)
            for h in range(HEADS):
                sl = slice(h * HD, (h + 1) * HD)
                kk = jnp.concatenate([kp_ref[:, sl], kc_ref[:, sl]], axis=0)
                ds = ds_scr[h]
                dq_c[:, sl] = jnp.dot(ds, kk, preferred_element_type=F32)
                dkk = lax.dot_general(ds, q_ref[:, sl], (((0,), (0,)), ((), ())), preferred_element_type=F32)
                dvv = lax.dot_general(p_scr[h], do_ref[:, sl], (((0,), (0,)), ((), ())),
                                      preferred_element_type=F32)
                out_ref[:, D + h * HD:D + (h + 1) * HD] = (dk_c[:, sl] + dkk[:QB]).astype(BF16)
                out_ref[:, 2 * D + h * HD:2 * D + (h + 1) * HD] = (dv_c[:, sl] + dvv[:QB]).astype(BF16)
                dk_c[:, sl] = dkk[QB:]
                dv_c[:, sl] = dvv[QB:]

    last = nblk - 1

    def blk(colblk, prev):
        if prev:
            return pl.BlockSpec((QB, D), lambda b: (jnp.maximum(jnp.minimum(b, last) - 1, 0), colblk))
        return pl.BlockSpec((QB, D), lambda b: (jnp.minimum(b, last), colblk))

    stat = pl.BlockSpec((QB, HD), lambda b: (jnp.minimum(b, last), 0))
    return pl.pallas_call(
        kern, grid=(nblk + 1,),
        in_specs=[blk(0, False), blk(1, False), blk(1, True), blk(2, False), blk(2, True),
                  pl.BlockSpec((QB, D), lambda b: (jnp.minimum(b, last), 0)), stat, stat,
                  pl.BlockSpec((HEADS, QB, 2 * QB), lambda b: (0, 0, 0))],
        out_specs=pl.BlockSpec((QB, 3 * D), lambda b: (jnp.maximum(b - 1, 0), 0)),
        out_shape=jax.ShapeDtypeStruct((S, 3 * D), BF16),
        scratch_shapes=[pltpu.VMEM((QB, D), F32), pltpu.VMEM((QB, D), F32), pltpu.VMEM((QB, D), F32),
                        pltpu.VMEM((HEADS, QB, 2 * QB), F32), pltpu.VMEM((HEADS, QB, 2 * QB), F32),
                        pltpu.VMEM((HEADS, QB, 2 * QB), BF16), pltpu.VMEM((HEADS, QB, 2 * QB), BF16)],
        compiler_params=_cparams(1), name=name)(qkv_f, qkv_f, qkv_f, qkv_f, qkv_f, do_f, lse_f, delta_f, bias)


def _local_step(x, tgt, comm, attn_norm, ffn_norm, final_norm, pool_norm, pool_scale):
    S = x.shape[0]
    bias = _bias_table()
    g_attn = attn_norm.reshape(1, D)
    g_f0 = ffn_norm[0:1]
    g_f1 = ffn_norm[1:2]
    g_fin = final_norm.reshape(1, D)
    W = {}

    def ffn_fwd(xin, h, l, next_gain, target=None):
        gu, act = _ffn_up(h, W[f"gu{l}"], f"ffn_up{l}")
        return gu, act, _mm_res_norm(act, W[f"d{l}"], xin, next_gain, K=DFF, tm=512, tgt=target,
                                     name=f"ffn_down{l}")

    def ffn_bwd(dxo, xin, gain, h, gu, act, l, rs_group):
        dgu = _ffn_down_bwd(dxo, W[f"d{l}"], gu, f"ffn_down_bwd{l}")
        gw_d = _mm(act, dxo, mode="tn", M=DFF, N=D, K=S, tm=HCH, tn=D, tk=2048, out_dtype=BF16, name=f"gw_d{l}")
        gw_gu = _mm(dgu, h, mode="tn", M=2 * DFF, N=D, K=S, tm=HCH, tn=D, tk=2048, out_dtype=BF16, name=f"gw_gu{l}")
        token = comm.send_grads(rs_group, {f"d{l}": gw_d, f"gu{l}": gw_gu})
        return _mm_rms_bwd(dgu, W[f"gu{l}"], xin, gain, dxo, mode="nn", M=S, K=2 * DFF, tm=512, deps=(token,),
                           name=f"ffn_up_bwd{l}")

    nbs = [S // QB // dil for dil in DILS]
    hf = _rms_fwd_folded(x, g_attn, "rms_attn", deps=comm.ag_tokens)
    hf = [h.reshape(S, D) for h in hf]
    W.update(comm.weights(0, hf[0]))
    qkv_f, o_f, lse_f = [], [], []
    for g, dil in enumerate(DILS):
        qkv_f.append(_mm(hf[g], W["qkv"], mode="nt", M=S, N=3 * D, K=D, tm=2048, tn=1024, tk=D, out_dtype=BF16,
                         b_off=(3 * g, 0), name=f"qkv_proj{g}"))
        og, lg = _attn_fwd(qkv_f[g], bias[g], nbs[g], f"attn_fwd{g}")
        o_f.append(og if dil == 1 else og.reshape(dil, S // dil, D))
        lse_f.append(lg if dil == 1 else lg.reshape(dil, S // dil, HD))
    passing = [comm.pass_on(1, tuple(o_f)), comm.pass_on(2, tuple(o_f))]
    (o_f, lse_f), passing = lax.optimization_barrier(((o_f, lse_f), passing))
    o, lse = _attn_merge(o_f, lse_f)
    W.update(comm.weights(1, (o, passing[0])))
    x1, h1 = _mm_res_norm(o, W["wo"], x, g_f0, K=D, tm=1024, name="attn_out")
    gu0, act0, (x2, h2) = ffn_fwd(x1, h1, 0, pool_norm)

    W.update(comm.weights(2, (x2, passing[1])))
    u = _mm(h2, W["wpi"], mode="nn", M=S, N=D, K=D, tm=1024, tn=D, tk=D, out_dtype=F32, name="pool_in")
    yd = _trail(u, backward=False, name="trail_fwd")
    x3 = _pool_out(yd, W["pg"], pool_scale, x2)
    h3 = _rms_fwd(x3, g_f1, "rms_ffn1")
    gu1, act1, (dx4, d_fin, lossvec) = ffn_fwd(x3, h3, 1, g_fin, target=tgt)

    dx3, d_f1 = ffn_bwd(dx4, x3, g_f1, h3, gu1, act1, 1, 0)
    dyd, d_scale, gw_pg = _pool_out_bwd(dx3, yd, W["pg"], pool_scale)
    du = _trail(dyd, backward=True, name="trail_bwd")
    gw_pi = _mm(h2, du, mode="tn", M=D, N=D, K=S, tm=D, tn=D, tk=2048, out_dtype=BF16, name="gw_pi")
    token = comm.send_grads(1, {"pg": gw_pg, "wpi": gw_pi})
    dx2, d_pool = _mm_rms_bwd(du, W["wpi"], x2, pool_norm, dx3, mode="nt", M=S, K=D, tm=1024, deps=(token,),
                              name="pool_in_bwd")
    dx1, d_f0 = ffn_bwd(dx2, x1, g_f0, h1, gu0, act0, 0, 2)

    gw_o = _mm(o, dx1, mode="tn", M=D, N=D, K=S, tm=D, tn=D, tk=2048, out_dtype=BF16, name="gw_o")
    do = _mm(dx1, W["wo"], mode="nt", M=S, N=D, K=D, tm=1024, tn=D, tk=D, out_dtype=F32, name="attn_out_bwd")
    do_f, lse_ff, delta_f = _attn_bwd_prep(do, o, lse)
    dqkv_f, gw_qkv = [], None
    for g in range(NGROUPS):
        dqkv_f.append(_attn_bwd(qkv_f[g], do_f[g].reshape(S, D), lse_ff[g].reshape(S, HD),
                                delta_f[g].reshape(S, HD), bias[g], nbs[g], f"attn_bwd{g}"))
        gw_qkv = _mm(dqkv_f[g], hf[g], mode="tn", M=3 * D, N=D, K=S, tm=1024, tn=D, tk=2048, out_dtype=BF16,
                     out_rows=NGROUPS * 3 * D, out_off=3 * g, out_prev=gw_qkv, name=f"gw_qkv{g}")
    token = comm.send_grads_pairwise({"wo": gw_o, "qkv": gw_qkv})
    dh0_f = [None] * NGROUPS
    for g in reversed(range(NGROUPS)):
        dh0_f[g] = _mm(dqkv_f[g], W["qkv"], mode="nn", M=S, N=D, K=3 * D, tm=1024, tn=D, tk=3 * D, out_dtype=F32,
                       b_off=(g, 0), deps=(token,), name=f"qkv_proj_bwd{g}")
        if g == NGROUPS - 1:
            token = comm.pass_grads(dh0_f[g])
    folded = [dh0_f[g].reshape(dil, S // dil, D) for g, dil in enumerate(DILS) if dil > 1]
    grad_x, d_attn = _rms_bwd(dh0_f[0], x, g_attn, dx1, "rms_attn_bwd", folded=folded)

    vec = jnp.concatenate([d_attn, d_f0, d_f1, d_fin, d_pool, d_scale, lossvec, jnp.zeros((1, D), F32)], axis=0)
    return grad_x, vec


def _mesh_pos():
    x, y, c = lax.axis_index("x"), lax.axis_index("y"), lax.axis_index("c")
    return x, y, c, 4 * x + 2 * y + c


def _peer(x, y, c, k):
    kx, ky, kc = (k >> 2) & 1, (k >> 1) & 1, k & 1
    px = 1 - x if kx else x
    py = 1 - y if ky else y
    pc = 1 - c if kc else c
    return (px, py, pc), 4 * px + 2 * py + pc


ANY = pl.BlockSpec(memory_space=pl.ANY)


HBM = pl.BlockSpec(memory_space=pltpu.HBM)
SEMS = pl.BlockSpec(memory_space=pltpu.SEMAPHORE)
EFFECT = pltpu.SideEffectType.DATAFLOW_SIDE_EFFECTING
NPEER = NDEV - 1

AG_GROUPS = (("qkv",), ("wo", "gu0", "d0"), ("wpi", "pg", "gu1", "d1"))
AG_ORDER = tuple(n for grp in AG_GROUPS for n in grp)
RS_GROUPS = (("d1", "gu1"), ("pg", "wpi"), ("d0", "gu0"), ("wo", "qkv"))


def _hbm(a):
    return pltpu.with_memory_space_constraint(a, pltpu.HBM)


def _remote(src, dst, send, recv, peer):
    return pltpu.make_async_remote_copy(src_ref=src, dst_ref=dst, send_sem=send, recv_sem=recv, device_id=peer,
                                        device_id_type=pl.DeviceIdType.MESH)


def _bcast_all(v, name, deps=()):
    W = v.shape[1]
    nd = len(deps)

    def kern(v_ref, *rest):
        o_ref, send, recv, lsem = rest[nd:]
        x, y, c, me = _mesh_pos()
        own = pltpu.make_async_copy(v_ref, o_ref.at[me], lsem)
        own.start()
        cps = [_remote(v_ref, o_ref.at[me], send.at[k - 1], recv.at[k - 1], _peer(x, y, c, k)[0])
               for k in range(1, NDEV)]
        for cp in cps:
            cp.start()
        for cp in cps:
            cp.wait_recv()
            cp.wait_send()
        own.wait()

    return pl.pallas_call(
        kern, in_specs=[ANY] * (1 + nd), out_specs=ANY, out_shape=jax.ShapeDtypeStruct((NDEV, 8, W), F32),
        scratch_shapes=[pltpu.SemaphoreType.DMA((NPEER,)), pltpu.SemaphoreType.DMA((NPEER,)),
                        pltpu.SemaphoreType.DMA(())],
        name=name)(v, *deps)


ALL_KS = tuple(range(1, NDEV))
AG_KS1 = (1, 2, 4, 6)
AG_KS2 = (2, 4, 6)
RS_KS_PAIR = (1, 3, 5, 7)
RS_KS_CHIPS = (2, 4, 6)


def _split_start(srcs, src_of, lands, copy_refs, name, deps=(), ks=ALL_KS, to=None):
    ns, n, nd, nk = len(srcs), len(lands), len(deps), len(ks)

    def body(*refs):
        ins, land = refs[:ns], refs[ns:ns + n]
        send, recv = refs[ns + n + nd], refs[ns + n + nd + 1]
        token = refs[-1]
        x, y, c, me = _mesh_pos()
        for j in range(n):
            for i, k in enumerate(ks):
                _, pid = _peer(x, y, c, k)
                dest, _ = _peer(x, y, c, k if to is None else to)
                src, dst = copy_refs(j, (land[j] if src_of[j] is None else ins[src_of[j]]), land[j], me, pid, i)
                _remote(src, dst, send.at[j * nk + i], recv.at[j * nk + i], dest).start()
        token[...] = jnp.zeros_like(token)

    outs = pl.pallas_call(
        body, name=name,
        out_shape=(pltpu.SemaphoreType.DMA((n * nk,)), pltpu.SemaphoreType.DMA((n * nk,)))
        + tuple(pltpu.HBM(a.shape, a.dtype) for a in srcs) + tuple(pltpu.HBM(a.shape, a.dtype) for a in lands)
        + (jax.ShapeDtypeStruct((8, 128), F32),),
        in_specs=(HBM,) * (ns + n) + (ANY,) * nd,
        out_specs=(SEMS, SEMS) + (HBM,) * (ns + n) + (pl.BlockSpec(memory_space=pltpu.VMEM),),
        input_output_aliases={i: 2 + i for i in range(ns + n)},
        compiler_params=pltpu.CompilerParams(has_side_effects=EFFECT),
    )(*[_hbm(a) for a in srcs], *[_hbm(a) for a in lands], *deps)
    return outs[0], outs[1], list(outs[2:2 + ns]), list(outs[2 + ns:2 + ns + n]), outs[-1]


def _split_wait(srcs, src_of, lands, send, recv, sem_rows, wait_refs, after, name, ks=ALL_KS):
    ns, n, nk = len(srcs), len(lands), len(ks)
    after = tuple(after) if isinstance(after, (tuple, list)) else (after,)

    def body(*refs):
        ins, land = refs[:ns], refs[ns:ns + n]
        send_ref, recv_ref = refs[ns + n], refs[ns + n + 1]
        x, y, c, me = _mesh_pos()
        for j in range(n):
            for i, k in enumerate(ks):
                peer, _ = _peer(x, y, c, k)
                src, dst = wait_refs(j, (land[j] if src_of[j] is None else ins[src_of[j]]), land[j])
                sem = sem_rows[j] * nk + i
                cp = _remote(src, dst, send_ref.at[sem], recv_ref.at[sem], peer)
                cp.wait_send()
                cp.wait_recv()

    outs = pl.pallas_call(
        body, name=name,
        out_shape=tuple(pltpu.HBM(a.shape, a.dtype) for a in srcs) + tuple(pltpu.HBM(a.shape, a.dtype) for a in lands),
        in_specs=(HBM,) * (ns + n) + (SEMS, SEMS) + (ANY,) * len(after),
        out_specs=(HBM,) * (ns + n),
        input_output_aliases={i: i for i in range(ns + n)},
        compiler_params=pltpu.CompilerParams(has_side_effects=EFFECT),
    )(*srcs, *lands, send, recv, *after)
    return list(outs[:ns]), list(outs[ns:])


def _place_transposed(w, me, name):
    rows = w.shape[1]
    nblk = rows // 128

    def kern(me_ref, w_ref, o_ref):
        o_ref[...] = w_ref[...].T.astype(BF16)

    grid_spec = pltpu.PrefetchScalarGridSpec(
        num_scalar_prefetch=1, grid=(nblk,),
        in_specs=[pl.BlockSpec((D, 128), lambda i, me_ref: (0, i))],
        out_specs=pl.BlockSpec((128, D), lambda i, me_ref: (me_ref[0] * nblk + i, 0)))
    return pl.pallas_call(
        kern, grid_spec=grid_spec, out_shape=jax.ShapeDtypeStruct((NDEV * rows, D), BF16),
        compiler_params=_cparams(1), name=name)(me.reshape(1).astype(jnp.int32), w)


class _Comm:
    def __init__(self, shards, me, deps=(), placed=None):
        self.me = me
        self.ag_land, self.ag_sems, self.ag_tokens, self.ag_passing = {}, {}, (), {}
        self.rs = []
        for part, names in enumerate((AG_GROUPS[0], AG_ORDER[len(AG_GROUPS[0]):])):
            rows = [SEC_ROWS[n] for n in names]
            mine = [shards[n] for n in names]
            if part:
                mine, deps = lax.optimization_barrier((mine, deps))
            lands = [placed[n] if n in (placed or {}) else
                     lax.dynamic_update_slice(lax.empty((NDEV * r, D), BF16), s.astype(BF16), (_shard_pos(n, me), 0))
                     for n, r, s in zip(names, rows, mine)]

            def copy_refs(j, src, land, me, pid, i, names=names, rows=rows):
                own = land.at[pl.ds(pl.multiple_of(_shard_pos(names[j], me), 16), rows[j])]
                return own, own

            send, recv, _, lands, token = _split_start([], [None] * len(names), lands, copy_refs, f"ag_start{part}",
                                                       deps=deps, ks=AG_KS1)
            deps = (token,)
            self.ag_tokens += (token,)
            for j, n in enumerate(names):
                self.ag_land[n] = lands[j]
                self.ag_sems[n] = (send, recv, j)

    def pass_on(self, group, after):
        names = AG_GROUPS[group]
        send, recv = self.ag_sems[names[0]][:2]
        idx = [self.ag_sems[n][2] for n in names]
        rows = [SEC_ROWS[n] for n in names]
        none = [None] * len(names)

        def wait_refs(j, src, land):
            return land.at[pl.ds(0, rows[j])], land.at[pl.ds(0, rows[j])]

        _, lands = _split_wait([], none, [self.ag_land[n] for n in names], send, recv, idx,
                               wait_refs, after, f"ag_wait{group}", ks=AG_KS1)

        def copy_refs(j, src, land, me, pid, i):
            theirs = land.at[pl.ds(pl.multiple_of(_shard_pos(names[j], pid), 16), rows[j])]
            return theirs, theirs

        send, recv, _, lands, token = _split_start([], none, lands, copy_refs, f"ag_pass{group}", ks=AG_KS2, to=1)
        self.ag_passing[group] = (send, recv, lands, wait_refs)
        return token

    def weights(self, group, after):
        names = AG_GROUPS[group]
        if group not in self.ag_passing:
            after = self.pass_on(group, after)
        send, recv, lands, wait_refs = self.ag_passing[group]
        _, lands = _split_wait([], [None] * len(names), lands, send, recv, list(range(len(names))), wait_refs, after,
                               f"ag_pass_wait{group}", ks=AG_KS2)
        return dict(zip(names, lands))

    def send_grads(self, group, gws):
        names = RS_GROUPS[group]
        rows = [SEC_ROWS[n] for n in names]
        grads = [gws[n] for n in names]
        me = self.me
        lands = [lax.dynamic_update_slice(
            lax.empty((NDEV, r, D), BF16),
            lax.dynamic_slice(g, (_shard_pos(n, me), 0), (r, D))[None], (me, 0, 0))
            for n, r, g in zip(names, rows, grads)]

        def copy_refs(j, src, land, me, pid, i):
            return src.at[pl.ds(pl.multiple_of(_shard_pos(names[j], pid), 16), rows[j])], land.at[me]

        send, recv, srcs, lands, token = _split_start(grads, list(range(len(names))), lands, copy_refs,
                                                      f"rs_start{group}")
        self.rs.append((names, rows, send, recv, srcs, lands, ALL_KS))
        return token

    def send_grads_pairwise(self, gws):
        names = RS_GROUPS[-1]
        rows = [SEC_ROWS[n] for n in names]
        grads = [gws[n] for n in names]
        idx = list(range(len(names)))
        lands = [lax.empty((len(RS_KS_PAIR), r, D), BF16) for r in rows]

        def copy_refs(j, src, land, me, pid, i):
            return src.at[pl.ds(pl.multiple_of(_shard_pos(names[j], pid), 16), rows[j])], land.at[i]

        send, recv, srcs, lands, token = _split_start(grads, idx, lands, copy_refs, "rs_pair_start",
                                                      ks=RS_KS_PAIR, to=1)
        self.pair = (names, rows, send, recv, srcs, lands)
        return token

    def pass_grads(self, after):
        names, rows, send, recv, srcs, lands = self.pair
        idx = list(range(len(names)))
        me = self.me

        def wait_refs(j, src, land):
            return src.at[pl.ds(0, rows[j])], land.at[0]

        srcs, lands = _split_wait(srcs, idx, lands, send, recv, idx, wait_refs, after, "rs_pair_wait", ks=RS_KS_PAIR)
        sums = []
        for n, r, g, got in zip(names, rows, srcs, lands):
            mine = jnp.stack([lax.dynamic_slice(g, (_shard_pos(n, jnp.bitwise_xor(me, k)), 0), (r, D))
                              for k in (0,) + RS_KS_CHIPS])
            sums.append(_pair_sum(mine, got, f"rs_pair_sum_{n}"))
        lands = [lax.dynamic_update_slice(lax.empty(p.shape, BF16), p[0:1], (0, 0, 0)) for p in sums]

        def copy_refs(j, src, land, me, pid, i):
            return src.at[i + 1], land.at[i + 1]

        send, recv, sums, lands, token = _split_start(sums, idx, lands, copy_refs, f"rs_start{len(RS_GROUPS) - 1}",
                                                      ks=RS_KS_CHIPS)
        self.rs.append((names, rows, send, recv, sums, lands, RS_KS_CHIPS))
        return token

    def received(self, group, after):
        names, rows, send, recv, srcs, lands, ks = self.rs[group]
        whole = srcs[0].ndim == 2

        def wait_refs(j, src, land):
            return (src.at[pl.ds(0, rows[j])] if whole else src.at[0]), land.at[0]

        _, lands = _split_wait(srcs, list(range(len(names))), lands, send, recv, list(range(len(names))), wait_refs,
                               after, f"rs_wait{group}", ks=ks)
        return dict(zip(names, lands))


def _pair_sum(a, b, name):
    n, rows, _ = a.shape
    tr = 384 if rows % 384 == 0 else rows

    def kern(a_ref, b_ref, o_ref):
        o_ref[...] = (a_ref[...].astype(F32) + b_ref[...].astype(F32)).astype(BF16)

    blk = pl.BlockSpec((1, tr, D), lambda i, t: (i, t, 0))
    return pl.pallas_call(
        kern, grid=(n, rows // tr), in_specs=[blk, blk], out_specs=blk,
        out_shape=jax.ShapeDtypeStruct(a.shape, BF16), compiler_params=_cparams(2), name=name)(a, b)


def _sum_contributions(r_ref):
    g = r_ref[0].astype(F32)
    for slot in range(1, r_ref.shape[0]):
        g = g + r_ref[slot].astype(F32)
    return g


def _adam_math(g, w, m, v):
    c1 = 1.0 / (1.0 - ADAM_B1 ** ADAM_STEP)
    c2 = 1.0 / (1.0 - ADAM_B2 ** ADAM_STEP)
    mn = ADAM_B1 * m + (1.0 - ADAM_B1) * g
    vn = ADAM_B2 * v + (1.0 - ADAM_B2) * (g * g)
    return -ADAM_LR * ((mn * c1) / (jnp.sqrt(vn * c2) + ADAM_EPS) + ADAM_WD * w), mn, vn


def _adamw(R, w, m, v, *, tr, name, layer=None, prev=None):
    rows, C = w.shape[-2:]
    nprev = 0 if prev is None else 4

    def kern(r_ref, w_ref, m_ref, v_ref, *rest):
        g_out, d_out, m_out, v_out = rest[nprev:]
        g = _sum_contributions(r_ref)
        g_out[...] = g
        d_out[...], m_out[...], v_out[...] = _adam_math(g, w_ref[...], m_ref[...], v_ref[...])

    if layer is None:
        tile = pl.BlockSpec((tr, C), lambda i: (i, 0))
    else:
        tile = pl.BlockSpec((None, tr, C), lambda i: (layer, i, 0))
    shp = jax.ShapeDtypeStruct(w.shape, F32)
    return pl.pallas_call(
        kern, grid=(rows // tr,),
        in_specs=[pl.BlockSpec((R.shape[0], tr, C), lambda i: (0, i, 0)), tile, tile, tile]
        + [pl.BlockSpec(memory_space=pl.ANY)] * nprev,
        out_specs=[tile] * 4, out_shape=[shp] * 4,
        input_output_aliases={4 + k: k for k in range(nprev)},
        compiler_params=_cparams(1), name=name)(R, w, m, v, *(prev or ()))


def _adamw_pool_group(R, w, m, v):
    rows = SEC_ROWS["pg"]

    def kern(r_ref, w_ref, m_ref, v_ref, g_out, d_out, m_out, v_out):
        g = _sum_contributions(r_ref)
        g_out[0] = g
        d_out[0], m_out[0], v_out[0] = _adam_math(g, w_ref[0], m_ref[0], v_ref[0])

    blk = pl.BlockSpec((1, rows, PGD), lambda i: (i, 0, 0))
    shp = jax.ShapeDtypeStruct((POOL_G, rows, PGD), F32)
    return pl.pallas_call(
        kern, grid=(POOL_G,),
        in_specs=[pl.BlockSpec((NDEV, rows, PGD), lambda i: (0, 0, i)), blk, blk, blk],
        out_specs=[blk] * 4, out_shape=[shp] * 4, compiler_params=_cparams(1), name="adamw_pg")(R, w, m, v)


def _grad_sum_t(R, name):
    rows = R.shape[1]
    tr = 128

    def kern(r_ref, o_ref):
        o_ref[...] = _sum_contributions(r_ref).T

    return pl.pallas_call(
        kern, grid=(rows // tr,), in_specs=[pl.BlockSpec((R.shape[0], tr, D), lambda i: (0, i, 0))],
        out_specs=pl.BlockSpec((D, tr), lambda i: (0, i)), out_shape=jax.ShapeDtypeStruct((D, rows), F32),
        compiler_params=_cparams(1), name=name)(R)


def _adam_plain(g, w, m, v, *, tr, name):
    rows, C = w.shape

    def kern(g_ref, w_ref, m_ref, v_ref, d_out, m_out, v_out):
        d_out[...], m_out[...], v_out[...] = _adam_math(g_ref[...], w_ref[...], m_ref[...], v_ref[...])

    tile = pl.BlockSpec((tr, C), lambda i: (i, 0))
    shp = jax.ShapeDtypeStruct((rows, C), F32)
    return pl.pallas_call(
        kern, grid=(rows // tr,), in_specs=[tile] * 4, out_specs=[tile] * 3, out_shape=[shp] * 3,
        compiler_params=_cparams(1), name=name)(g, w, m, v)


def _pack_sections(w_qkv, w_attn_out, w_pool_in, w_pool_group, w_ffn_gate_up, w_ffn_down):
    pg = w_pool_group[0].transpose(1, 0, 2).reshape(SEC_ROWS["pg"], D)
    return {"qkv": w_qkv[0].T, "wo": w_attn_out[0], "wpi": w_pool_in[0], "gu0": w_ffn_gate_up[0].T,
            "gu1": w_ffn_gate_up[1].T, "d0": w_ffn_down[0], "d1": w_ffn_down[1], "pg": pg}


def _vec_pack(attn_norm, ffn_norm, final_norm, pool_norm_sh, pool_scale_sh, me):
    def place(sh):
        return lax.dynamic_update_slice(jnp.zeros((1, D), F32), sh, (0, me * 128))
    return jnp.concatenate([attn_norm, ffn_norm, final_norm.reshape(1, D), place(pool_norm_sh),
                            place(pool_scale_sh), jnp.zeros((2, D), F32)], axis=0)


def _vec_unpack(p, me):
    def take(r):
        return lax.dynamic_slice(p[r:r + 1], (0, me * 128), (1, 128))
    return p[0:1], p[1:3], p[3], take(4), take(5)


def kernel(x, attn_norm, w_qkv, w_attn_out, pool_norm, w_pool_in, w_pool_group, pool_scale, ffn_norm, w_ffn_gate_up, w_ffn_down, final_norm, loss_target, m_attn_norm, m_w_qkv, m_w_attn_out, m_pool_norm, m_w_pool_in, m_w_pool_group, m_pool_scale, m_ffn_norm, m_w_ffn_gate_up, m_w_ffn_down, m_final_norm, v_attn_norm, v_w_qkv, v_w_attn_out, v_pool_norm, v_w_pool_in, v_w_pool_group, v_pool_scale, v_ffn_norm, v_w_ffn_gate_up, v_w_ffn_down, v_final_norm):
    me = 4 * lax.axis_index("x") + 2 * lax.axis_index("y") + lax.axis_index("c")

    pw = _pack_sections(w_qkv, w_attn_out, w_pool_in, w_pool_group, w_ffn_gate_up, w_ffn_down)
    vsh = jnp.concatenate([pool_norm, pool_scale, jnp.zeros((6, 128), F32)], axis=0)

    vg = _bcast_all(vsh, "gather_pool_vectors")
    comm = _Comm(pw, me, deps=(vg,), placed={"qkv": _place_transposed(w_qkv[0], me, "place_qkv")})
    pool_norm_full = vg[:, 0, :].reshape(1, D)
    pool_scale_full = vg[:, 1, :].reshape(1, D)

    grad_x, vec = _local_step(x[0], loss_target[0], comm, attn_norm, ffn_norm, final_norm,
                              pool_norm_full, pool_scale_full)

    small = ((attn_norm, ffn_norm, final_norm, pool_norm, pool_scale),
             (m_attn_norm, m_ffn_norm, m_final_norm, m_pool_norm, m_pool_scale),
             (v_attn_norm, v_ffn_norm, v_final_norm, v_pool_norm, v_pool_scale))
    small, grad_x = lax.optimization_barrier((small, grad_x))
    vw, vm, vv = (_vec_pack(*s, me) for s in small)

    gu_t = [jnp.swapaxes(a, 1, 2) for a in (w_ffn_gate_up, m_w_ffn_gate_up, v_w_ffn_gate_up)]
    res = {}
    gu_res, d_res = None, None
    vec_out = None
    after = grad_x
    for group in range(len(RS_GROUPS)):
        if group == len(RS_GROUPS) - 1:
            VR = _bcast_all(vec, "exchange_vector_grads", deps=(after,))
            vec_out = _adamw(VR, vw, vm, vv, tr=8, name="adamw_vec")
            after = vec_out[0]
        for n, R in comm.received(group, after).items():
            if n in ("d0", "d1"):
                d_res = _adamw(R, w_ffn_down, m_w_ffn_down, v_w_ffn_down, tr=352, name=f"adamw_{n}",
                               layer=int(n[1]), prev=d_res)
                after = d_res[0]
            elif n in ("gu0", "gu1"):
                gu_res = _adamw(R, *gu_t, tr=352, name=f"adamw_{n}", layer=int(n[2]), prev=gu_res)
                after = gu_res[0]
            elif n == "pg":
                out = _adamw_pool_group(R, w_pool_group[0], m_w_pool_group[0], v_w_pool_group[0])
                res["pg"] = tuple(a[None] for a in out)
                after = out[0]
            elif n in ("wo", "wpi"):
                w, m, v = ((w_attn_out, m_w_attn_out, v_w_attn_out) if n == "wo"
                           else (w_pool_in, m_w_pool_in, v_w_pool_in))
                res[n] = _adamw(R, w[0], m[0], v[0], tr=128, name=f"adamw_{n}")
                res[n] = tuple(a[None] for a in res[n])
                after = res[n][0]
            else:
                g = _grad_sum_t(R, "grad_sum_qkv")
                out = _adam_plain(g, w_qkv[0], m_w_qkv[0], v_w_qkv[0], tr=256, name="adamw_qkv")
                res["qkv"] = tuple(a[None] for a in (g,) + tuple(out))
                after = out[0]
    res["gu"] = tuple(jnp.swapaxes(a, 1, 2) for a in gu_res)
    res["d"] = tuple(d_res)

    outs = []
    for kind in range(4):
        an, fn, fin, pn, ps = _vec_unpack(vec_out[kind], me)
        outs.append((an, res["qkv"][kind], res["wo"][kind], pn, res["wpi"][kind], res["pg"][kind], ps, fn,
                     res["gu"][kind], res["d"][kind], fin))
    loss = 0.5 * jnp.sum(vec_out[0][6]) / D
    return (loss, grad_x[None]) + outs[0] + outs[1] + outs[2] + outs[3]
```

```python
import jax
import jax.numpy as jnp
from jax import lax
from jax.experimental import pallas as pl
from jax.experimental.pallas import tpu as pltpu

F32 = jnp.float32
BF16 = jnp.bfloat16

D = 1024
NDEV = 8
HEADS = 8
HD = 128
QB = 128
NGROUPS = 3
DILS = (1, 4, 16)
DFF = 2816
HCH = 1408
POOL_G = 4
PGD = 256
RMS_EPS = 1e-6
NEG = -1e30

ADAM_LR = 0.001
ADAM_B1 = 0.9
ADAM_B2 = 0.999
ADAM_EPS = 1e-08
ADAM_WD = 0.01
ADAM_STEP = 10

VMEM_LIMIT = 52 * 1024 * 1024

SECTIONS = (("qkv", 1152), ("wo", 128), ("wpi", 128), ("gu0", 704), ("gu1", 704),
            ("d0", 352), ("d1", 352), ("pg", 32))
LOC_OFF = {}
GLB_OFF = {}
_o = 0
for _n, _r in SECTIONS:
    LOC_OFF[_n] = _o
    GLB_OFF[_n] = _o * NDEV
    _o += _r
PACK_ROWS = _o
GLB_ROWS = PACK_ROWS * NDEV
SEC_ROWS = dict(SECTIONS)
SEC_ROWS["pv"] = 8


def _cparams(n_grid):
    return pltpu.CompilerParams(dimension_semantics=("arbitrary",) * n_grid, vmem_limit_bytes=VMEM_LIMIT)


def _shard_pos(name, dev):
    n = SEC_ROWS[name]
    if name in ("gu0", "gu1"):
        return ((dev % 4) // 2) * (2 * HCH) + (dev // 4) * HCH + (dev % 2) * n
    return dev * n


def _mm(a, b, *, mode, M, N, K, tm, tn, tk, out_dtype, name, a_off=(0, 0), b_off=(0, 0), res=None,
        out_rows=None, out_off=0, out_prev=None, deps=()):
    nm, nn, nk = M // tm, N // tn, K // tk
    assert nm * tm == M and nn * tn == N and nk * tk == K
    if mode == "nn":
        a_bs, b_bs = (tm, tk), (tk, tn)
        a_ix = lambda i, j, k: (i, k)
        b_ix = lambda i, j, k: (k, j)
        dims = (((1,), (0,)), ((), ()))
    elif mode == "nt":
        a_bs, b_bs = (tm, tk), (tn, tk)
        a_ix = lambda i, j, k: (i, k)
        b_ix = lambda i, j, k: (j, k)
        dims = (((1,), (1,)), ((), ()))
    else:
        a_bs, b_bs = (tk, tm), (tk, tn)
        a_ix = lambda i, j, k: (k, i)
        b_ix = lambda i, j, k: (k, j)
        dims = (((0,), (0,)), ((), ()))

    def spec(bs, ix, off):
        def im(i, j, k):
            r, c = ix(i, j, k)
            return (r + off[0], c + off[1])
        return pl.BlockSpec(bs, im)

    in_specs = [spec(a_bs, a_ix, a_off), spec(b_bs, b_ix, b_off)]
    args = [a, b]
    if res is not None:
        in_specs.append(pl.BlockSpec((tm, tn), lambda i, j, k: (i, j)))
        args.append(res)
    out_shape = jax.ShapeDtypeStruct((M if out_rows is None else out_rows, N), out_dtype)
    out_spec = pl.BlockSpec((tm, tn), lambda i, j, k: (i + out_off, j))
    has_res = res is not None
    extra = list(deps) + ([out_prev] if out_prev is not None else [])
    for dep in extra:
        in_specs.append(pl.BlockSpec(memory_space=pl.ANY))
        args.append(dep)
    o_pos = 2 + int(has_res) + len(extra)
    aliases = {len(args) - 1: 0} if out_prev is not None else {}

    def kern(*refs):
        a_ref, b_ref = refs[0], refs[1]
        res_ref = refs[2] if has_res else None
        o_ref = refs[o_pos]
        av = a_ref[...]
        bv = b_ref[...]
        if av.dtype != BF16:
            av = av.astype(BF16)
        if bv.dtype != BF16:
            bv = bv.astype(BF16)
        part = lax.dot_general(av, bv, dims, preferred_element_type=F32)

        def write(val):
            if has_res:
                val = val + res_ref[...]
            o_ref[...] = val.astype(out_dtype)

        if nk == 1:
            write(part)
        else:
            acc_ref = refs[-1]
            k = pl.program_id(2)

            @pl.when(k == 0)
            def _():
                acc_ref[...] = part

            @pl.when(k > 0)
            def _():
                acc_ref[...] += part

            @pl.when(k == nk - 1)
            def _():
                write(acc_ref[...])

    scratch = [pltpu.VMEM((tm, tn), F32)] if nk > 1 else []
    return pl.pallas_call(
        kern, grid=(nm, nn, nk), in_specs=in_specs, out_specs=out_spec, out_shape=out_shape,
        scratch_shapes=scratch, input_output_aliases=aliases, compiler_params=_cparams(3), name=name)(*args)


def _mm_rms_bwd(a, b, x, g, dres, *, mode, M, K, tm, name, b_off=(0, 0), deps=()):
    nd = len(deps)
    b_bs = (K, D) if mode == "nn" else (D, K)
    dims = (((1,), (0,)), ((), ())) if mode == "nn" else (((1,), (1,)), ((), ()))

    def kern(a_ref, b_ref, x_ref, g_ref, dres_ref, *rest):
        dx_ref, dg_ref = rest[nd:]
        i = pl.program_id(0)
        av = a_ref[...]
        if av.dtype != BF16:
            av = av.astype(BF16)
        dhv = lax.dot_general(av, b_ref[...], dims, preferred_element_type=F32)
        xv = x_ref[...]
        r = lax.rsqrt(jnp.mean(xv * xv, axis=-1, keepdims=True) + RMS_EPS)
        xhat = xv * r
        gy = dhv * g_ref[...]
        dx_ref[...] = dres_ref[...] + r * (gy - xhat * jnp.mean(gy * xhat, axis=-1, keepdims=True))
        part = jnp.sum(dhv * xhat, axis=0, keepdims=True)

        @pl.when(i == 0)
        def _():
            dg_ref[...] = part

        @pl.when(i > 0)
        def _():
            dg_ref[...] += part

    row = pl.BlockSpec((tm, D), lambda i: (i, 0))
    vec = pl.BlockSpec((1, D), lambda i: (0, 0))
    return pl.pallas_call(
        kern, grid=(M // tm,),
        in_specs=[pl.BlockSpec((tm, K), lambda i: (i, 0)),
                  pl.BlockSpec(b_bs, lambda i: b_off, pipeline_mode=pl.Buffered(1)), row, vec, row]
        + [pl.BlockSpec(memory_space=pl.ANY)] * nd,
        out_specs=[row, vec],
        out_shape=[jax.ShapeDtypeStruct((M, D), F32), jax.ShapeDtypeStruct((1, D), F32)],
        compiler_params=_cparams(1), name=name)(a, b, x, g, dres, *deps)


def _mm_res_norm(a, b, res, g, *, K, tm, name, b_off=(0, 0), tgt=None):
    M = a.shape[0]
    head = tgt is not None

    def kern(a_ref, b_ref, res_ref, g_ref, *rest):
        xv = res_ref[...] + jnp.dot(a_ref[...], b_ref[...], preferred_element_type=F32)
        gv = g_ref[...]
        r = lax.rsqrt(jnp.mean(xv * xv, axis=-1, keepdims=True) + RMS_EPS)
        xhat = xv * r
        if not head:
            xo_ref, h_ref = rest
            xo_ref[...] = xv
            h_ref[...] = (xhat * gv).astype(BF16)
            return
        t_ref, dx_ref, dg_ref, ls_ref = rest
        i = pl.program_id(0)
        e = xhat * gv - t_ref[...]
        dy = e * (1.0 / D)
        gy = dy * gv
        dx_ref[...] = r * (gy - xhat * jnp.mean(gy * xhat, axis=-1, keepdims=True))
        dgp = jnp.sum(dy * xhat, axis=0, keepdims=True)
        lsp = jnp.sum(e * e, axis=0, keepdims=True)

        @pl.when(i == 0)
        def _():
            dg_ref[...] = dgp
            ls_ref[...] = lsp

        @pl.when(i > 0)
        def _():
            dg_ref[...] += dgp
            ls_ref[...] += lsp

    row = pl.BlockSpec((tm, D), lambda i: (i, 0))
    vec = pl.BlockSpec((1, D), lambda i: (0, 0))
    in_specs = [pl.BlockSpec((tm, K), lambda i: (i, 0)),
                pl.BlockSpec((K, D), lambda i: b_off, pipeline_mode=pl.Buffered(1)), row, vec]
    if head:
        return pl.pallas_call(
            kern, grid=(M // tm,), in_specs=in_specs + [row], out_specs=[row, vec, vec],
            out_shape=[jax.ShapeDtypeStruct((M, D), F32), jax.ShapeDtypeStruct((1, D), F32),
                       jax.ShapeDtypeStruct((1, D), F32)],
            compiler_params=_cparams(1), name=name)(a, b, res, g, tgt)
    return pl.pallas_call(
        kern, grid=(M // tm,), in_specs=in_specs, out_specs=[row, row],
        out_shape=[jax.ShapeDtypeStruct((M, D), F32), jax.ShapeDtypeStruct((M, D), BF16)],
        compiler_params=_cparams(1), name=name)(a, b, res, g)


def _rms_fwd(x, g, name, deps=()):
    S = x.shape[0]
    tr = 512

    def kern(x_ref, g_ref, *rest):
        h_ref = rest[-1]
        xv = x_ref[...]
        r = lax.rsqrt(jnp.mean(xv * xv, axis=-1, keepdims=True) + RMS_EPS)
        h_ref[...] = (xv * r * g_ref[...]).astype(BF16)

    return pl.pallas_call(
        kern, grid=(S // tr,),
        in_specs=[pl.BlockSpec((tr, D), lambda i: (i, 0)), pl.BlockSpec((1, D), lambda i: (0, 0))]
        + [pl.BlockSpec(memory_space=pl.ANY)] * len(deps),
        out_specs=pl.BlockSpec((tr, D), lambda i: (i, 0)),
        out_shape=jax.ShapeDtypeStruct((S, D), BF16), compiler_params=_cparams(1), name=name)(x, g, *deps)


def _chunks_put(scr, val):
    for c in range(scr.shape[0]):
        scr[c] = val[:, c * 128:(c + 1) * 128]


def _chunks_get(scr):
    return jnp.concatenate([scr[c] for c in range(scr.shape[0])], axis=1)


def _chunks_rows(scr, r, n, dil):
    return jnp.concatenate([scr.at[c][pl.ds(r, n, stride=dil), :] for c in range(scr.shape[0])], axis=1)


def _chunks_add_rows(scr, val, r, n, dil, accumulate):
    for c in range(scr.shape[0]):
        rows = pl.ds(r, n, stride=dil)
        piece = val[:, c * 128:(c + 1) * 128]
        tile = scr.at[c]
        tile[rows, :] = tile[rows, :] + piece if accumulate else piece


def _rms_fwd_folded(x, g, name, deps=()):
    S = x.shape[0]
    tr = 512
    dils = DILS[1:]

    def kern(x_ref, g_ref, *rest):
        outs, scr = rest[len(deps):-1], rest[-1]
        xv = x_ref[...]
        r = lax.rsqrt(jnp.mean(xv * xv, axis=-1, keepdims=True) + RMS_EPS)
        h = (xv * r * g_ref[...]).astype(BF16)
        outs[0][...] = h
        _chunks_put(scr, h.astype(F32))
        for o_ref, dil in zip(outs[1:], dils):
            for res in range(dil):
                o_ref[res] = _chunks_rows(scr, res, tr // dil, dil).astype(BF16)

    return pl.pallas_call(
        kern, grid=(S // tr,),
        in_specs=[pl.BlockSpec((tr, D), lambda i: (i, 0)), pl.BlockSpec((1, D), lambda i: (0, 0))]
        + [pl.BlockSpec(memory_space=pl.ANY)] * len(deps),
        out_specs=[pl.BlockSpec((tr, D), lambda i: (i, 0))]
        + [pl.BlockSpec((dil, tr // dil, D), lambda i: (0, i, 0)) for dil in dils],
        out_shape=[jax.ShapeDtypeStruct((S, D), BF16)]
        + [jax.ShapeDtypeStruct((dil, S // dil, D), BF16) for dil in dils],
        scratch_shapes=[pltpu.VMEM((D // 128, tr, 128), F32)],
        compiler_params=_cparams(1), name=name)(x, g, *deps)


def _rms_bwd(dh, x, g, dres, name, folded=()):
    S = x.shape[0]
    tr = 512
    nf = len(folded)

    def kern(dh_ref, *rest):
        f_refs = rest[:nf]
        x_ref, g_ref, dres_ref, dx_ref, dg_ref = rest[nf:nf + 5]
        i = pl.program_id(0)
        xv = x_ref[...]
        if nf:
            acc_ref = rest[nf + 5]
            _chunks_put(acc_ref, dh_ref[...].astype(F32))
            for f_ref in f_refs:
                dil = f_ref.shape[0]
                for res in range(dil):
                    _chunks_add_rows(acc_ref, f_ref[res], res, tr // dil, dil, True)
            dhv = _chunks_get(acc_ref)
        else:
            dhv = dh_ref[...].astype(F32)
        r = lax.rsqrt(jnp.mean(xv * xv, axis=-1, keepdims=True) + RMS_EPS)
        xhat = xv * r
        gy = dhv * g_ref[...]
        dx_ref[...] = dres_ref[...] + r * (gy - xhat * jnp.mean(gy * xhat, axis=-1, keepdims=True))
        part = jnp.sum(dhv * xhat, axis=0, keepdims=True)

        @pl.when(i == 0)
        def _():
            dg_ref[...] = part

        @pl.when(i > 0)
        def _():
            dg_ref[...] += part

    row = pl.BlockSpec((tr, D), lambda i: (i, 0))
    vec = pl.BlockSpec((1, D), lambda i: (0, 0))
    fspecs = [pl.BlockSpec((f.shape[0], tr // f.shape[0], D), lambda i: (0, i, 0)) for f in folded]
    return pl.pallas_call(
        kern, grid=(S // tr,), in_specs=[row] + fspecs + [row, vec, row], out_specs=[row, vec],
        out_shape=[jax.ShapeDtypeStruct((S, D), F32), jax.ShapeDtypeStruct((1, D), F32)],
        scratch_shapes=[pltpu.VMEM((D // 128, tr, 128), F32)] if nf else [],
        compiler_params=_cparams(1), name=name)(dh, *folded, x, g, dres)


def _ffn_up(h, G, name):
    S = h.shape[0]
    tm = 512
    nj = DFF // HCH

    def kern(h_ref, w_ref, gu_ref, act_ref):
        gu = lax.dot_general(h_ref[...], w_ref[...], (((1,), (1,)), ((), ())), preferred_element_type=F32)
        gu_ref[...] = gu.astype(BF16)
        gate = gu[:, :HCH]
        up = gu[:, HCH:]
        act_ref[...] = (gate * jax.nn.sigmoid(gate) * up).astype(BF16)

    return pl.pallas_call(
        kern, grid=(nj, S // tm),
        in_specs=[pl.BlockSpec((tm, D), lambda j, i: (i, 0)),
                  pl.BlockSpec((2 * HCH, D), lambda j, i: (j, 0))],
        out_specs=[pl.BlockSpec((tm, 2 * HCH), lambda j, i: (i, j)),
                   pl.BlockSpec((tm, HCH), lambda j, i: (i, j))],
        out_shape=[jax.ShapeDtypeStruct((S, 2 * DFF), BF16), jax.ShapeDtypeStruct((S, DFF), BF16)],
        compiler_params=_cparams(2), name=name)(h, G)


def _ffn_down_bwd(dx, G, gu, name):
    S = dx.shape[0]
    tm = 512
    nj = DFF // HCH

    def kern(dx_ref, w_ref, gu_ref, o_ref):
        dxv = dx_ref[...].astype(BF16)
        for j in range(nj):
            c0 = 2 * HCH * j
            dact = lax.dot_general(dxv, w_ref[HCH * j:HCH * (j + 1), :], (((1,), (1,)), ((), ())),
                                   preferred_element_type=F32)
            gate = gu_ref[:, c0:c0 + HCH].astype(F32)
            up = gu_ref[:, c0 + HCH:c0 + 2 * HCH].astype(F32)
            sig = jax.nn.sigmoid(gate)
            silu = gate * sig
            o_ref[:, c0:c0 + HCH] = (dact * up * (sig * (1.0 + gate * (1.0 - sig)))).astype(BF16)
            o_ref[:, c0 + HCH:c0 + 2 * HCH] = (dact * silu).astype(BF16)

    row = pl.BlockSpec((tm, 2 * DFF), lambda i: (i, 0))
    return pl.pallas_call(
        kern, grid=(S // tm,),
        in_specs=[pl.BlockSpec((tm, D), lambda i: (i, 0)),
                  pl.BlockSpec((DFF, D), lambda i: (0, 0), pipeline_mode=pl.Buffered(1)), row],
        out_specs=row, out_shape=jax.ShapeDtypeStruct((S, 2 * DFF), BF16),
        compiler_params=_cparams(1), name=name)(dx, G, gu)


def _trail(u, *, backward, name):
    S = u.shape[0]

    def kern(u_ref, o_ref):
        g = pl.program_id(0)
        for grp in range(POOL_G):
            @pl.when(g == grp)
            def _(grp=grp):
                uv = u_ref[...].astype(F32)
                row = lax.broadcasted_iota(jnp.int32, uv.shape, 0)
                cnt = jnp.minimum(row + 1, 2 << grp).astype(F32)
                s = uv / cnt if backward else uv
                for k in (1, 2, 4, 8)[:grp + 1]:
                    if backward:
                        sh = jnp.where(row < S - k, pltpu.roll(s, S - k, 0), 0.0)
                    else:
                        sh = jnp.where(row >= k, pltpu.roll(s, k, 0), 0.0)
                    s = s + sh
                if backward:
                    o_ref[...] = (s - uv).astype(BF16)
                else:
                    o_ref[...] = (s / cnt - uv).astype(BF16)

    blk = pl.BlockSpec((S, PGD), lambda g: (0, g))
    return pl.pallas_call(
        kern, grid=(POOL_G,), in_specs=[blk], out_specs=blk,
        out_shape=jax.ShapeDtypeStruct((S, D), BF16), compiler_params=_cparams(1), name=name)(u)


def _pool_out(yd, G, scale, xres):
    S = yd.shape[0]
    tm = min(S, 4096)

    def kern(y_ref, w_ref, s_ref, x_ref, o_ref):
        z = jnp.dot(y_ref[...], w_ref[...], preferred_element_type=F32)
        o_ref[...] = x_ref[...] + z * s_ref[...]

    tile = pl.BlockSpec((tm, PGD), lambda i, g: (i, g))
    return pl.pallas_call(
        kern, grid=(S // tm, POOL_G),
        in_specs=[tile, pl.BlockSpec((PGD, PGD), lambda i, g: (0, g)),
                  pl.BlockSpec((1, PGD), lambda i, g: (0, g)), tile],
        out_specs=tile, out_shape=jax.ShapeDtypeStruct((S, D), F32),
        compiler_params=_cparams(2), name="pool_out")(yd, G, scale, xres)


def _pool_out_bwd(dz, yd, G, scale):
    S = yd.shape[0]
    tm = min(S, 4096)
    ni = S // tm

    def kern(dz_ref, y_ref, w_ref, s_ref, dy_ref, ds_ref, dw_ref, acc_ref):
        i = pl.program_id(1)
        dzv = dz_ref[...]
        yv = y_ref[...]
        wv = w_ref[...]
        zraw = jnp.dot(yv, wv, preferred_element_type=F32)
        dsp = jnp.sum(dzv * zraw, axis=0, keepdims=True)
        dzr = (dzv * s_ref[...]).astype(BF16)
        dy_ref[...] = lax.dot_general(dzr, wv, (((1,), (1,)), ((), ())), preferred_element_type=F32)
        dwp = lax.dot_general(yv, dzr, (((0,), (0,)), ((), ())), preferred_element_type=F32)

        @pl.when(i == 0)
        def _():
            ds_ref[...] = dsp
            acc_ref[...] = dwp

        @pl.when(i > 0)
        def _():
            ds_ref[...] += dsp
            acc_ref[...] += dwp

        @pl.when(i == ni - 1)
        def _():
            dw_ref[...] = acc_ref[...].astype(BF16)

    tile = pl.BlockSpec((tm, PGD), lambda g, i: (i, g))
    return pl.pallas_call(
        kern, grid=(POOL_G, ni),
        in_specs=[tile, tile, pl.BlockSpec((PGD, PGD), lambda g, i: (0, g)),
                  pl.BlockSpec((1, PGD), lambda g, i: (0, g))],
        out_specs=[tile, pl.BlockSpec((1, PGD), lambda g, i: (0, g)),
                   pl.BlockSpec((PGD, PGD), lambda g, i: (0, g))],
        out_shape=[jax.ShapeDtypeStruct((S, D), F32), jax.ShapeDtypeStruct((1, D), F32),
                   jax.ShapeDtypeStruct((PGD, D), BF16)],
        scratch_shapes=[pltpu.VMEM((PGD, PGD), F32)],
        compiler_params=_cparams(2), name="pool_out_bwd")(dz, yd, G, scale)


def _bias_table():
    qi = jnp.arange(QB)[:, None]
    ki = jnp.arange(2 * QB)[None, :]
    delta = QB + qi - ki
    inband = (delta >= 0) & (delta <= QB)
    n = NGROUPS * HEADS
    slopes = jnp.exp2(-8.0 * jnp.arange(1, n + 1, dtype=F32) / n).reshape(NGROUPS, HEADS)
    dil = jnp.asarray(DILS, F32)
    bias = -slopes[:, :, None, None] * (delta.astype(F32)[None, None] * dil[:, None, None, None])
    return jnp.where(inband[None, None], bias, NEG)


def _attn_fwd(qkv_f, bias, nb, name):
    S = qkv_f.shape[0]
    nblk = S // QB
    scale = HD ** -0.5

    def kern(q_ref, kc_ref, kp_ref, vc_ref, vp_ref, b_ref, o_ref, l_ref, s_scr, p_scr, r_scr):
        b = pl.program_id(0)
        has_prev = jnp.bitwise_and(b, nb - 1) != 0
        col = lax.broadcasted_iota(jnp.int32, (QB, 2 * QB), 1)
        dead = jnp.logical_and(col < QB, jnp.logical_not(has_prev))
        lane = lax.broadcasted_iota(jnp.int32, (QB, HD), 1)
        lse_all = jnp.zeros((QB, HD), F32)
        for h in range(HEADS):
            sl = slice(h * HD, (h + 1) * HD)
            kk = jnp.concatenate([kp_ref[:, sl], kc_ref[:, sl]], axis=0)
            s_scr[h] = lax.dot_general(q_ref[:, sl], kk, (((1,), (1,)), ((), ())), preferred_element_type=F32)
        for h in range(HEADS):
            s = s_scr[h] * scale + b_ref[h]
            s = jnp.where(dead, NEG, s)
            m = jnp.max(s, axis=-1, keepdims=True)
            p = jnp.exp(s - m)
            den = jnp.sum(p, axis=-1, keepdims=True)
            p_scr[h] = p.astype(BF16)
            r_scr[h] = jnp.broadcast_to(1.0 / den, (QB, HD))
            lse_all = jnp.where(lane == h, m + jnp.log(den), lse_all)
        for h in range(HEADS):
            sl = slice(h * HD, (h + 1) * HD)
            vv = jnp.concatenate([vp_ref[:, sl], vc_ref[:, sl]], axis=0)
            o = jnp.dot(p_scr[h], vv, preferred_element_type=F32) * r_scr[h]
            o_ref[:, sl] = o.astype(BF16)
        l_ref[...] = lse_all

    def blk(colblk, prev):
        if prev:
            return pl.BlockSpec((QB, D), lambda b: (jnp.maximum(b - 1, 0), colblk))
        return pl.BlockSpec((QB, D), lambda b: (b, colblk))

    return pl.pallas_call(
        kern, grid=(nblk,),
        in_specs=[blk(0, False), blk(1, False), blk(1, True), blk(2, False), blk(2, True),
                  pl.BlockSpec((HEADS, QB, 2 * QB), lambda b: (0, 0, 0))],
        out_specs=[pl.BlockSpec((QB, D), lambda b: (b, 0)), pl.BlockSpec((QB, HD), lambda b: (b, 0))],
        out_shape=[jax.ShapeDtypeStruct((S, D), BF16), jax.ShapeDtypeStruct((S, HD), F32)],
        scratch_shapes=[pltpu.VMEM((HEADS, QB, 2 * QB), F32), pltpu.VMEM((HEADS, QB, 2 * QB), BF16),
                        pltpu.VMEM((HEADS, QB, HD), F32)],
        compiler_params=_cparams(1), name=name)(qkv_f, qkv_f, qkv_f, qkv_f, qkv_f, bias)


def _natural(ref, scr, tm):
    dil = ref.shape[0]
    for res in range(dil):
        _chunks_add_rows(scr, ref[res].astype(F32), res, tm // dil, dil, False)
    return _chunks_get(scr)


def _attn_merge(os, lses):
    S = os[0].shape[0]
    tm = 512

    def kern(o0, o1, o2, l0, l1, l2, om_ref, lm_ref, ls1, ls2, os1, os2):
        la = l0[...]
        lb = _natural(l1, ls1, tm)
        lc = _natural(l2, ls2, tm)
        m = jnp.maximum(jnp.maximum(la, lb), lc)
        e0, e1, e2 = jnp.exp(la - m), jnp.exp(lb - m), jnp.exp(lc - m)
        tot = e0 + e1 + e2
        lm_ref[...] = m + jnp.log(tot)
        w0, w1, w2 = e0 / tot, e1 / tot, e2 / tot
        for res in range(o1.shape[0]):
            _chunks_add_rows(os1, o1[res].astype(F32), res, tm // o1.shape[0], o1.shape[0], False)
        for res in range(o2.shape[0]):
            _chunks_add_rows(os2, o2[res].astype(F32), res, tm // o2.shape[0], o2.shape[0], False)
        for h in range(HEADS):
            sl = slice(h * HD, (h + 1) * HD)
            acc = w0[:, h:h + 1] * o0[:, sl].astype(F32) + w1[:, h:h + 1] * os1[h] + w2[:, h:h + 1] * os2[h]
            om_ref[:, sl] = acc.astype(BF16)

    def spec(a, c):
        if a.ndim == 2:
            return pl.BlockSpec((tm, c), lambda i: (i, 0))
        return pl.BlockSpec((a.shape[0], tm // a.shape[0], c), lambda i: (0, i, 0))

    return pl.pallas_call(
        kern, grid=(S // tm,),
        in_specs=[spec(a, D) for a in os] + [spec(a, HD) for a in lses],
        out_specs=[pl.BlockSpec((tm, D), lambda i: (i, 0)), pl.BlockSpec((tm, HD), lambda i: (i, 0))],
        out_shape=[jax.ShapeDtypeStruct((S, D), BF16), jax.ShapeDtypeStruct((S, HD), F32)],
        scratch_shapes=[pltpu.VMEM((1, tm, HD), F32), pltpu.VMEM((1, tm, HD), F32),
                        pltpu.VMEM((HEADS, tm, HD), F32), pltpu.VMEM((HEADS, tm, HD), F32)],
        compiler_params=_cparams(1), name="attn_merge")(*os, *lses)


def _attn_bwd_prep(do, o, lse):
    S = o.shape[0]
    tm = 512
    dils = DILS[1:]

    def kern(do_ref, o_ref, l_ref, *rest):
        do_outs, l_outs, d_outs = rest[0:3], rest[3:5], rest[5:8]
        do_scr, l_scr, d_scr = rest[8:11]
        lane = lax.broadcasted_iota(jnp.int32, (tm, HD), 1)
        acc = jnp.zeros((tm, HD), F32)
        for h in range(HEADS):
            sl = slice(h * HD, (h + 1) * HD)
            prod = do_ref[:, sl] * o_ref[:, sl].astype(F32)
            acc = jnp.where(lane == h, jnp.sum(prod, axis=-1, keepdims=True), acc)
        d_scr[0] = acc
        l_scr[0] = l_ref[...]
        _chunks_put(do_scr, do_ref[...])
        do_outs[0][...] = do_ref[...].astype(BF16)
        d_outs[0][...] = acc
        for j, dil in enumerate(dils):
            for res in range(dil):
                n = tm // dil
                do_outs[1 + j][res] = _chunks_rows(do_scr, res, n, dil).astype(BF16)
                l_outs[j][res] = _chunks_rows(l_scr, res, n, dil)
                d_outs[1 + j][res] = _chunks_rows(d_scr, res, n, dil)

    def nat(c):
        return pl.BlockSpec((tm, c), lambda i: (i, 0))

    def fol(dil, c):
        return pl.BlockSpec((dil, tm // dil, c), lambda i: (0, i, 0))

    def shapes(c, dt, with_natural):
        first = [jax.ShapeDtypeStruct((S, c), dt)] if with_natural else []
        return first + [jax.ShapeDtypeStruct((dil, S // dil, c), dt) for dil in dils]

    outs = pl.pallas_call(
        kern, grid=(S // tm,), in_specs=[nat(D), nat(D), nat(HD)],
        out_specs=[nat(D)] + [fol(dil, D) for dil in dils] + [fol(dil, HD) for dil in dils]
        + [nat(HD)] + [fol(dil, HD) for dil in dils],
        out_shape=shapes(D, BF16, True) + shapes(HD, F32, False) + shapes(HD, F32, True),
        scratch_shapes=[pltpu.VMEM((HEADS, tm, HD), F32), pltpu.VMEM((1, tm, HD), F32), pltpu.VMEM((1, tm, HD), F32)],
        compiler_params=_cparams(1), name="attn_bwd_prep")(do, o, lse)
    return outs[0:3], [lse] + list(outs[3:5]), outs[5:8]


def _attn_bwd(qkv_f, do_f, lse_f, delta_f, bias, nb, name):
    S = qkv_f.shape[0]
    nblk = S // QB
    scale = HD ** -0.5

    def kern(q_ref, kc_ref, kp_ref, vc_ref, vp_ref, do_ref, l_ref, d_ref, b_ref, out_ref, dq_c, dk_c, dv_c,
             s_scr, dp_scr, ds_scr, p_scr):
        b = pl.program_id(0)

        @pl.when(b == 0)
        def _():
            dq_c[...] = jnp.zeros_like(dq_c)
            dk_c[...] = jnp.zeros_like(dk_c)
            dv_c[...] = jnp.zeros_like(dv_c)

        @pl.when(b == nblk)
        def _():
            out_ref[:, 0:D] = dq_c[...].astype(BF16)
            out_ref[:, D:2 * D] = dk_c[...].astype(BF16)
            out_ref[:, 2 * D:3 * D] = dv_c[...].astype(BF16)

        @pl.when(b < nblk)
        def _():
            has_prev = jnp.bitwise_and(b, nb - 1) != 0
            col = lax.broadcasted_iota(jnp.int32, (QB, 2 * QB), 1)
            dead = jnp.logical_and(col < QB, jnp.logical_not(has_prev))
            out_ref[:, 0:D] = dq_c[...].astype(BF16)
            lv = l_ref[...]
            dv_ = d_ref[...]
            for h in range(HEADS):
                sl = slice(h * HD, (h + 1) * HD)
                kk = jnp.concatenate([kp_ref[:, sl], kc_ref[:, sl]], axis=0)
                vv = jnp.concatenate([vp_ref[:, sl], vc_ref[:, sl]], axis=0)
                s_scr[h] = lax.dot_general(q_ref[:, sl], kk, (((1,), (1,)), ((), ())), preferred_element_type=F32)
                dp_scr[h] = lax.dot_general(do_ref[:, sl], vv, (((1,), (1,)), ((), ())),
                                            preferred_element_type=F32)
            for h in range(HEADS):
                s = s_scr[h] * scale + b_ref[h]
                s = jnp.where(dead, NEG, s)
                p = jnp.exp(s - lv[:, h:h + 1])
                ds_scr[h] = (p * (dp_scr[h] - dv_[:, h:h + 1]) * scale).astype(BF16)
                p_scr[h] = p.astype(BF16)
            for h in range(HEADS):
                sl = slice(h * HD, (h + 1) * HD)
                kk = jnp.concatenate([kp_ref[:, sl], kc_ref[:, sl]], axis=0)
                ds = ds_scr[h]
                dq_c[:, sl] = jnp.dot(ds, kk, preferred_element_type=F32)
                dkk = lax.dot_general(ds, q_ref[:, sl], (((0,), (0,)), ((), ())), preferred_element_type=F32)
                dvv = lax.dot_general(p_scr[h], do_ref[:, sl], (((0,), (0,)), ((), ())),
                                      preferred_element_type=F32)
                out_ref[:, D + h * HD:D + (h + 1) * HD] = (dk_c[:, sl] + dkk[:QB]).astype(BF16)
                out_ref[:, 2 * D + h * HD:2 * D + (h + 1) * HD] = (dv_c[:, sl] + dvv[:QB]).astype(BF16)
                dk_c[:, sl] = dkk[QB:]
                dv_c[:, sl] = dvv[QB:]

    last = nblk - 1

    def blk(colblk, prev):
        if prev:
            return pl.BlockSpec((QB, D), lambda b: (jnp.maximum(jnp.minimum(b, last) - 1, 0), colblk))
        return pl.BlockSpec((QB, D), lambda b: (jnp.minimum(b, last), colblk))

    stat = pl.BlockSpec((QB, HD), lambda b: (jnp.minimum(b, last), 0))
    return pl.pallas_call(
        kern, grid=(nblk + 1,),
        in_specs=[blk(0, False), blk(1, False), blk(1, True), blk(2, False), blk(2, True),
                  pl.BlockSpec((QB, D), lambda b: (jnp.minimum(b, last), 0)), stat, stat,
                  pl.BlockSpec((HEADS, QB, 2 * QB), lambda b: (0, 0, 0))],
        out_specs=pl.BlockSpec((QB, 3 * D), lambda b: (jnp.maximum(b - 1, 0), 0)),
        out_shape=jax.ShapeDtypeStruct((S, 3 * D), BF16),
        scratch_shapes=[pltpu.VMEM((QB, D), F32), pltpu.VMEM((QB, D), F32), pltpu.VMEM((QB, D), F32),
                        pltpu.VMEM((HEADS, QB, 2 * QB), F32), pltpu.VMEM((HEADS, QB, 2 * QB), F32),
                        pltpu.VMEM((HEADS, QB, 2 * QB), BF16), pltpu.VMEM((HEADS, QB, 2 * QB), BF16)],
        compiler_params=_cparams(1), name=name)(qkv_f, qkv_f, qkv_f, qkv_f, qkv_f, do_f, lse_f, delta_f, bias)


def _local_step(x, tgt, comm, attn_norm, ffn_norm, final_norm):
    S = x.shape[0]
    bias = _bias_table()
    g_attn = attn_norm.reshape(1, D)
    g_f0 = ffn_norm[0:1]
    g_f1 = ffn_norm[1:2]
    g_fin = final_norm.reshape(1, D)
    W = {}

    def ffn_fwd(xin, h, l, next_gain, target=None):
        gu, act = _ffn_up(h, W[f"gu{l}"], f"ffn_up{l}")
        return gu, act, _mm_res_norm(act, W[f"d{l}"], xin, next_gain, K=DFF, tm=512, tgt=target,
                                     name=f"ffn_down{l}")

    def ffn_bwd(dxo, xin, gain, h, gu, act, l, rs_group):
        dgu = _ffn_down_bwd(dxo, W[f"d{l}"], gu, f"ffn_down_bwd{l}")
        gw_d = _mm(act, dxo, mode="tn", M=DFF, N=D, K=S, tm=HCH, tn=D, tk=2048, out_dtype=BF16, name=f"gw_d{l}")
        gw_gu = _mm(dgu, h, mode="tn", M=2 * DFF, N=D, K=S, tm=HCH, tn=D, tk=2048, out_dtype=BF16, name=f"gw_gu{l}")
        token = comm.send_grads(rs_group, {f"d{l}": gw_d, f"gu{l}": gw_gu})
        return _mm_rms_bwd(dgu, W[f"gu{l}"], xin, gain, dxo, mode="nn", M=S, K=2 * DFF, tm=512, deps=(token,),
                           name=f"ffn_up_bwd{l}")

    nbs = [S // QB // dil for dil in DILS]
    hf = _rms_fwd_folded(x, g_attn, "rms_attn", deps=comm.ag_tokens)
    hf = [h.reshape(S, D) for h in hf]
    W.update(comm.weights(0, hf[0]))
    qkv_f, o_f, lse_f = [], [], []
    for g, dil in enumerate(DILS):
        qkv_f.append(_mm(hf[g], W["qkv"], mode="nt", M=S, N=3 * D, K=D, tm=2048, tn=1024, tk=D, out_dtype=BF16,
                         b_off=(3 * g, 0), name=f"qkv_proj{g}"))
        og, lg = _attn_fwd(qkv_f[g], bias[g], nbs[g], f"attn_fwd{g}")
        o_f.append(og if dil == 1 else og.reshape(dil, S // dil, D))
        lse_f.append(lg if dil == 1 else lg.reshape(dil, S // dil, HD))
    passing = [comm.pass_on(1, tuple(o_f)), comm.pass_on(2, tuple(o_f))]
    (o_f, lse_f), passing = lax.optimization_barrier(((o_f, lse_f), passing))
    o, lse = _attn_merge(o_f, lse_f)
    W.update(comm.weights(1, (o, passing[0])))
    x1, h1 = _mm_res_norm(o, W["wo"], x, g_f0, K=D, tm=1024, name="attn_out")
    pv = W["pv"].reshape(NDEV, 8, 128)
    pool_norm, pool_scale = pv[:, 0, :].reshape(1, D), pv[:, 1, :].reshape(1, D)
    gu0, act0, (x2, h2) = ffn_fwd(x1, h1, 0, pool_norm)

    W.update(comm.weights(2, (x2, passing[1])))
    u = _mm(h2, W["wpi"], mode="nn", M=S, N=D, K=D, tm=1024, tn=D, tk=D, out_dtype=F32, name="pool_in")
    yd = _trail(u, backward=False, name="trail_fwd")
    x3 = _pool_out(yd, W["pg"], pool_scale, x2)
    h3 = _rms_fwd(x3, g_f1, "rms_ffn1")
    gu1, act1, (dx4, d_fin, lossvec) = ffn_fwd(x3, h3, 1, g_fin, target=tgt)

    dx3, d_f1 = ffn_bwd(dx4, x3, g_f1, h3, gu1, act1, 1, 0)
    dyd, d_scale, gw_pg = _pool_out_bwd(dx3, yd, W["pg"], pool_scale)
    du = _trail(dyd, backward=True, name="trail_bwd")
    gw_pi = _mm(h2, du, mode="tn", M=D, N=D, K=S, tm=D, tn=D, tk=2048, out_dtype=BF16, name="gw_pi")
    token = comm.send_grads(1, {"pg": gw_pg, "wpi": gw_pi})
    dx2, d_pool = _mm_rms_bwd(du, W["wpi"], x2, pool_norm, dx3, mode="nt", M=S, K=D, tm=1024, deps=(token,),
                              name="pool_in_bwd")
    dx1, d_f0 = ffn_bwd(dx2, x1, g_f0, h1, gu0, act0, 0, 2)

    gw_o = _mm(o, dx1, mode="tn", M=D, N=D, K=S, tm=D, tn=D, tk=2048, out_dtype=BF16, name="gw_o")
    do = _mm(dx1, W["wo"], mode="nt", M=S, N=D, K=D, tm=1024, tn=D, tk=D, out_dtype=F32, name="attn_out_bwd")
    do_f, lse_ff, delta_f = _attn_bwd_prep(do, o, lse)
    dqkv_f, gw_qkv = [], None
    for g in range(NGROUPS):
        dqkv_f.append(_attn_bwd(qkv_f[g], do_f[g].reshape(S, D), lse_ff[g].reshape(S, HD),
                                delta_f[g].reshape(S, HD), bias[g], nbs[g], f"attn_bwd{g}"))
        gw_qkv = _mm(dqkv_f[g], hf[g], mode="tn", M=3 * D, N=D, K=S, tm=1024, tn=D, tk=2048, out_dtype=BF16,
                     out_rows=NGROUPS * 3 * D, out_off=3 * g, out_prev=gw_qkv, name=f"gw_qkv{g}")
    token = comm.send_grads_pairwise({"wo": gw_o, "qkv": gw_qkv})
    dh0_f = [None] * NGROUPS
    for g in reversed(range(NGROUPS)):
        dh0_f[g] = _mm(dqkv_f[g], W["qkv"], mode="nn", M=S, N=D, K=3 * D, tm=1024, tn=D, tk=3 * D, out_dtype=F32,
                       b_off=(g, 0), deps=(token,), name=f"qkv_proj_bwd{g}")
        if g == NGROUPS - 1:
            token = comm.pass_grads(dh0_f[g])
    folded = [dh0_f[g].reshape(dil, S // dil, D) for g, dil in enumerate(DILS) if dil > 1]
    grad_x, d_attn = _rms_bwd(dh0_f[0], x, g_attn, dx1, "rms_attn_bwd", folded=folded)

    vec = jnp.concatenate([d_attn, d_f0, d_f1, d_fin, d_pool, d_scale, lossvec, jnp.zeros((1, D), F32)], axis=0)
    return grad_x, vec


def _mesh_pos():
    x, y, c = lax.axis_index("x"), lax.axis_index("y"), lax.axis_index("c")
    return x, y, c, 4 * x + 2 * y + c


def _peer(x, y, c, k):
    kx, ky, kc = (k >> 2) & 1, (k >> 1) & 1, k & 1
    px = 1 - x if kx else x
    py = 1 - y if ky else y
    pc = 1 - c if kc else c
    return (px, py, pc), 4 * px + 2 * py + pc


ANY = pl.BlockSpec(memory_space=pl.ANY)


HBM = pl.BlockSpec(memory_space=pltpu.HBM)
SEMS = pl.BlockSpec(memory_space=pltpu.SEMAPHORE)
EFFECT = pltpu.SideEffectType.DATAFLOW_SIDE_EFFECTING
NPEER = NDEV - 1

AG_GROUPS = (("qkv",), ("wo", "gu0", "d0", "pv"), ("wpi", "pg", "gu1", "d1"))
AG_ORDER = tuple(n for grp in AG_GROUPS for n in grp)
RS_GROUPS = (("d1", "gu1"), ("pg", "wpi"), ("d0", "gu0"), ("wo", "qkv"))


def _hbm(a):
    return pltpu.with_memory_space_constraint(a, pltpu.HBM)


def _remote(src, dst, send, recv, peer):
    return pltpu.make_async_remote_copy(src_ref=src, dst_ref=dst, send_sem=send, recv_sem=recv, device_id=peer,
                                        device_id_type=pl.DeviceIdType.MESH)


def _bcast_all(v, name, deps=()):
    W = v.shape[1]
    nd = len(deps)

    def kern(v_ref, *rest):
        o_ref, send, recv, lsem = rest[nd:]
        x, y, c, me = _mesh_pos()
        own = pltpu.make_async_copy(v_ref, o_ref.at[me], lsem)
        own.start()
        cps = [_remote(v_ref, o_ref.at[me], send.at[k - 1], recv.at[k - 1], _peer(x, y, c, k)[0])
               for k in range(1, NDEV)]
        for cp in cps:
            cp.start()
        for cp in cps:
            cp.wait_recv()
            cp.wait_send()
        own.wait()

    return pl.pallas_call(
        kern, in_specs=[ANY] * (1 + nd), out_specs=ANY, out_shape=jax.ShapeDtypeStruct((NDEV, 8, W), F32),
        scratch_shapes=[pltpu.SemaphoreType.DMA((NPEER,)), pltpu.SemaphoreType.DMA((NPEER,)),
                        pltpu.SemaphoreType.DMA(())],
        name=name)(v, *deps)


ALL_KS = tuple(range(1, NDEV))
AG_KS1 = (1, 2, 4, 6)
AG_KS2 = (2, 4, 6)
RS_KS_PAIR = (1, 3, 5, 7)
RS_KS_CHIPS = (2, 4, 6)


def _split_start(srcs, src_of, lands, copy_refs, name, deps=(), ks=ALL_KS, to=None):
    ns, n, nd, nk = len(srcs), len(lands), len(deps), len(ks)

    def body(*refs):
        ins, land = refs[:ns], refs[ns:ns + n]
        send, recv = refs[ns + n + nd], refs[ns + n + nd + 1]
        token = refs[-1]
        x, y, c, me = _mesh_pos()
        for j in range(n):
            for i, k in enumerate(ks):
                _, pid = _peer(x, y, c, k)
                dest, _ = _peer(x, y, c, k if to is None else to)
                src, dst = copy_refs(j, (land[j] if src_of[j] is None else ins[src_of[j]]), land[j], me, pid, i)
                _remote(src, dst, send.at[j * nk + i], recv.at[j * nk + i], dest).start()
        token[...] = jnp.zeros_like(token)

    outs = pl.pallas_call(
        body, name=name,
        out_shape=(pltpu.SemaphoreType.DMA((n * nk,)), pltpu.SemaphoreType.DMA((n * nk,)))
        + tuple(pltpu.HBM(a.shape, a.dtype) for a in srcs) + tuple(pltpu.HBM(a.shape, a.dtype) for a in lands)
        + (jax.ShapeDtypeStruct((8, 128), F32),),
        in_specs=(HBM,) * (ns + n) + (ANY,) * nd,
        out_specs=(SEMS, SEMS) + (HBM,) * (ns + n) + (pl.BlockSpec(memory_space=pltpu.VMEM),),
        input_output_aliases={i: 2 + i for i in range(ns + n)},
        compiler_params=pltpu.CompilerParams(has_side_effects=EFFECT),
    )(*[_hbm(a) for a in srcs], *[_hbm(a) for a in lands], *deps)
    return outs[0], outs[1], list(outs[2:2 + ns]), list(outs[2 + ns:2 + ns + n]), outs[-1]


def _split_wait(srcs, src_of, lands, send, recv, sem_rows, wait_refs, after, name, ks=ALL_KS):
    ns, n, nk = len(srcs), len(lands), len(ks)
    after = tuple(after) if isinstance(after, (tuple, list)) else (after,)

    def body(*refs):
        ins, land = refs[:ns], refs[ns:ns + n]
        send_ref, recv_ref = refs[ns + n], refs[ns + n + 1]
        x, y, c, me = _mesh_pos()
        for j in range(n):
            for i, k in enumerate(ks):
                peer, _ = _peer(x, y, c, k)
                src, dst = wait_refs(j, (land[j] if src_of[j] is None else ins[src_of[j]]), land[j])
                sem = sem_rows[j] * nk + i
                cp = _remote(src, dst, send_ref.at[sem], recv_ref.at[sem], peer)
                cp.wait_send()
                cp.wait_recv()

    outs = pl.pallas_call(
        body, name=name,
        out_shape=tuple(pltpu.HBM(a.shape, a.dtype) for a in srcs) + tuple(pltpu.HBM(a.shape, a.dtype) for a in lands),
        in_specs=(HBM,) * (ns + n) + (SEMS, SEMS) + (ANY,) * len(after),
        out_specs=(HBM,) * (ns + n),
        input_output_aliases={i: i for i in range(ns + n)},
        compiler_params=pltpu.CompilerParams(has_side_effects=EFFECT),
    )(*srcs, *lands, send, recv, *after)
    return list(outs[:ns]), list(outs[ns:])


def _ag_dtype(name):
    return F32 if name == "pv" else BF16


def _ag_align(name):
    return 8 if name == "pv" else 16


def _place_transposed(w, me, name):
    rows = w.shape[1]
    nblk = rows // 128

    def kern(me_ref, w_ref, o_ref):
        o_ref[...] = w_ref[...].T.astype(BF16)

    grid_spec = pltpu.PrefetchScalarGridSpec(
        num_scalar_prefetch=1, grid=(nblk,),
        in_specs=[pl.BlockSpec((D, 128), lambda i, me_ref: (0, i))],
        out_specs=pl.BlockSpec((128, D), lambda i, me_ref: (me_ref[0] * nblk + i, 0)))
    return pl.pallas_call(
        kern, grid_spec=grid_spec, out_shape=jax.ShapeDtypeStruct((NDEV * rows, D), BF16),
        compiler_params=_cparams(1), name=name)(me.reshape(1).astype(jnp.int32), w)


class _Comm:
    def __init__(self, shards, me, deps=(), placed=None):
        self.me = me
        self.ag_land, self.ag_sems, self.ag_tokens, self.ag_passing = {}, {}, (), {}
        self.rs = []
        for part, names in enumerate((AG_GROUPS[0], AG_ORDER[len(AG_GROUPS[0]):])):
            rows = [SEC_ROWS[n] for n in names]
            mine = [shards[n] for n in names]
            if part:
                mine, deps = lax.optimization_barrier((mine, deps))
            lands = [placed[n] if n in (placed or {}) else
                     lax.dynamic_update_slice(lax.empty((NDEV * r, s.shape[1]), _ag_dtype(n)), s.astype(_ag_dtype(n)),
                                              (_shard_pos(n, me), 0))
                     for n, r, s in zip(names, rows, mine)]

            def copy_refs(j, src, land, me, pid, i, names=names, rows=rows):
                own = land.at[pl.ds(pl.multiple_of(_shard_pos(names[j], me), _ag_align(names[j])), rows[j])]
                return own, own

            send, recv, _, lands, token = _split_start([], [None] * len(names), lands, copy_refs, f"ag_start{part}",
                                                       deps=deps, ks=AG_KS1)
            deps = (token,)
            self.ag_tokens += (token,)
            for j, n in enumerate(names):
                self.ag_land[n] = lands[j]
                self.ag_sems[n] = (send, recv, j)

    def pass_on(self, group, after):
        names = AG_GROUPS[group]
        send, recv = self.ag_sems[names[0]][:2]
        idx = [self.ag_sems[n][2] for n in names]
        rows = [SEC_ROWS[n] for n in names]
        none = [None] * len(names)

        def wait_refs(j, src, land):
            return land.at[pl.ds(0, rows[j])], land.at[pl.ds(0, rows[j])]

        _, lands = _split_wait([], none, [self.ag_land[n] for n in names], send, recv, idx,
                               wait_refs, after, f"ag_wait{group}", ks=AG_KS1)

        def copy_refs(j, src, land, me, pid, i):
            theirs = land.at[pl.ds(pl.multiple_of(_shard_pos(names[j], pid), _ag_align(names[j])), rows[j])]
            return theirs, theirs

        send, recv, _, lands, token = _split_start([], none, lands, copy_refs, f"ag_pass{group}", ks=AG_KS2, to=1)
        self.ag_passing[group] = (send, recv, lands, wait_refs)
        return token

    def weights(self, group, after):
        names = AG_GROUPS[group]
        if group not in self.ag_passing:
            after = self.pass_on(group, after)
        send, recv, lands, wait_refs = self.ag_passing[group]
        _, lands = _split_wait([], [None] * len(names), lands, send, recv, list(range(len(names))), wait_refs, after,
                               f"ag_pass_wait{group}", ks=AG_KS2)
        return dict(zip(names, lands))

    def send_grads(self, group, gws):
        names = RS_GROUPS[group]
        rows = [SEC_ROWS[n] for n in names]
        grads = [gws[n] for n in names]
        me = self.me
        lands = [lax.dynamic_update_slice(
            lax.empty((NDEV, r, D), BF16),
            lax.dynamic_slice(g, (_shard_pos(n, me), 0), (r, D))[None], (me, 0, 0))
            for n, r, g in zip(names, rows, grads)]

        def copy_refs(j, src, land, me, pid, i):
            return src.at[pl.ds(pl.multiple_of(_shard_pos(names[j], pid), 16), rows[j])], land.at[me]

        send, recv, srcs, lands, token = _split_start(grads, list(range(len(names))), lands, copy_refs,
                                                      f"rs_start{group}")
        self.rs.append((names, rows, send, recv, srcs, lands, ALL_KS))
        return token

    def send_grads_pairwise(self, gws):
        names = RS_GROUPS[-1]
        rows = [SEC_ROWS[n] for n in names]
        grads = [gws[n] for n in names]
        idx = list(range(len(names)))
        lands = [lax.empty((len(RS_KS_PAIR), r, D), BF16) for r in rows]

        def copy_refs(j, src, land, me, pid, i):
            return src.at[pl.ds(pl.multiple_of(_shard_pos(names[j], pid), 16), rows[j])], land.at[i]

        send, recv, srcs, lands, token = _split_start(grads, idx, lands, copy_refs, "rs_pair_start",
                                                      ks=RS_KS_PAIR, to=1)
        self.pair = (names, rows, send, recv, srcs, lands)
        return token

    def pass_grads(self, after):
        names, rows, send, recv, srcs, lands = self.pair
        idx = list(range(len(names)))
        me = self.me

        def wait_refs(j, src, land):
            return src.at[pl.ds(0, rows[j])], land.at[0]

        srcs, lands = _split_wait(srcs, idx, lands, send, recv, idx, wait_refs, after, "rs_pair_wait", ks=RS_KS_PAIR)
        sums = []
        for n, r, g, got in zip(names, rows, srcs, lands):
            mine = jnp.stack([lax.dynamic_slice(g, (_shard_pos(n, jnp.bitwise_xor(me, k)), 0), (r, D))
                              for k in (0,) + RS_KS_CHIPS])
            sums.append(_pair_sum(mine, got, f"rs_pair_sum_{n}"))
        lands = [lax.dynamic_update_slice(lax.empty(p.shape, BF16), p[0:1], (0, 0, 0)) for p in sums]

        def copy_refs(j, src, land, me, pid, i):
            return src.at[i + 1], land.at[i + 1]

        send, recv, sums, lands, token = _split_start(sums, idx, lands, copy_refs, f"rs_start{len(RS_GROUPS) - 1}",
                                                      ks=RS_KS_CHIPS)
        self.rs.append((names, rows, send, recv, sums, lands, RS_KS_CHIPS))
        return token

    def received(self, group, after):
        names, rows, send, recv, srcs, lands, ks = self.rs[group]
        whole = srcs[0].ndim == 2

        def wait_refs(j, src, land):
            return (src.at[pl.ds(0, rows[j])] if whole else src.at[0]), land.at[0]

        _, lands = _split_wait(srcs, list(range(len(names))), lands, send, recv, list(range(len(names))), wait_refs,
                               after, f"rs_wait{group}", ks=ks)
        return dict(zip(names, lands))


def _pair_sum(a, b, name):
    n, rows, _ = a.shape
    tr = 384 if rows % 384 == 0 else rows

    def kern(a_ref, b_ref, o_ref):
        o_ref[...] = (a_ref[...].astype(F32) + b_ref[...].astype(F32)).astype(BF16)

    blk = pl.BlockSpec((1, tr, D), lambda i, t: (i, t, 0))
    return pl.pallas_call(
        kern, grid=(n, rows // tr), in_specs=[blk, blk], out_specs=blk,
        out_shape=jax.ShapeDtypeStruct(a.shape, BF16), compiler_params=_cparams(2), name=name)(a, b)


def _sum_contributions(r_ref):
    g = r_ref[0].astype(F32)
    for slot in range(1, r_ref.shape[0]):
        g = g + r_ref[slot].astype(F32)
    return g


def _adam_math(g, w, m, v):
    c1 = 1.0 / (1.0 - ADAM_B1 ** ADAM_STEP)
    c2 = 1.0 / (1.0 - ADAM_B2 ** ADAM_STEP)
    mn = ADAM_B1 * m + (1.0 - ADAM_B1) * g
    vn = ADAM_B2 * v + (1.0 - ADAM_B2) * (g * g)
    return -ADAM_LR * ((mn * c1) / (jnp.sqrt(vn * c2) + ADAM_EPS) + ADAM_WD * w), mn, vn


def _adamw(R, w, m, v, *, tr, name, layer=None, prev=None):
    rows, C = w.shape[-2:]
    nprev = 0 if prev is None else 4

    def kern(r_ref, w_ref, m_ref, v_ref, *rest):
        g_out, d_out, m_out, v_out = rest[nprev:]
        g = _sum_contributions(r_ref)
        g_out[...] = g
        d_out[...], m_out[...], v_out[...] = _adam_math(g, w_ref[...], m_ref[...], v_ref[...])

    if layer is None:
        tile = pl.BlockSpec((tr, C), lambda i: (i, 0))
    else:
        tile = pl.BlockSpec((None, tr, C), lambda i: (layer, i, 0))
    shp = jax.ShapeDtypeStruct(w.shape, F32)
    return pl.pallas_call(
        kern, grid=(rows // tr,),
        in_specs=[pl.BlockSpec((R.shape[0], tr, C), lambda i: (0, i, 0)), tile, tile, tile]
        + [pl.BlockSpec(memory_space=pl.ANY)] * nprev,
        out_specs=[tile] * 4, out_shape=[shp] * 4,
        input_output_aliases={4 + k: k for k in range(nprev)},
        compiler_params=_cparams(1), name=name)(R, w, m, v, *(prev or ()))


def _adamw_pool_group(R, w, m, v):
    rows = SEC_ROWS["pg"]

    def kern(r_ref, w_ref, m_ref, v_ref, g_out, d_out, m_out, v_out):
        g = _sum_contributions(r_ref)
        g_out[0] = g
        d_out[0], m_out[0], v_out[0] = _adam_math(g, w_ref[0], m_ref[0], v_ref[0])

    blk = pl.BlockSpec((1, rows, PGD), lambda i: (i, 0, 0))
    shp = jax.ShapeDtypeStruct((POOL_G, rows, PGD), F32)
    return pl.pallas_call(
        kern, grid=(POOL_G,),
        in_specs=[pl.BlockSpec((NDEV, rows, PGD), lambda i: (0, 0, i)), blk, blk, blk],
        out_specs=[blk] * 4, out_shape=[shp] * 4, compiler_params=_cparams(1), name="adamw_pg")(R, w, m, v)


def _grad_sum_t(R, name):
    rows = R.shape[1]
    tr = 128

    def kern(r_ref, o_ref):
        o_ref[...] = _sum_contributions(r_ref).T

    return pl.pallas_call(
        kern, grid=(rows // tr,), in_specs=[pl.BlockSpec((R.shape[0], tr, D), lambda i: (0, i, 0))],
        out_specs=pl.BlockSpec((D, tr), lambda i: (0, i)), out_shape=jax.ShapeDtypeStruct((D, rows), F32),
        compiler_params=_cparams(1), name=name)(R)


def _adam_plain(g, w, m, v, *, tr, name):
    rows, C = w.shape

    def kern(g_ref, w_ref, m_ref, v_ref, d_out, m_out, v_out):
        d_out[...], m_out[...], v_out[...] = _adam_math(g_ref[...], w_ref[...], m_ref[...], v_ref[...])

    tile = pl.BlockSpec((tr, C), lambda i: (i, 0))
    shp = jax.ShapeDtypeStruct((rows, C), F32)
    return pl.pallas_call(
        kern, grid=(rows // tr,), in_specs=[tile] * 4, out_specs=[tile] * 3, out_shape=[shp] * 3,
        compiler_params=_cparams(1), name=name)(g, w, m, v)


def _pack_sections(w_qkv, w_attn_out, w_pool_in, w_pool_group, w_ffn_gate_up, w_ffn_down):
    pg = w_pool_group[0].transpose(1, 0, 2).reshape(SEC_ROWS["pg"], D)
    return {"qkv": w_qkv[0].T, "wo": w_attn_out[0], "wpi": w_pool_in[0], "gu0": w_ffn_gate_up[0].T,
            "gu1": w_ffn_gate_up[1].T, "d0": w_ffn_down[0], "d1": w_ffn_down[1], "pg": pg}


def _vec_pack(attn_norm, ffn_norm, final_norm, pool_norm_sh, pool_scale_sh, me):
    def place(sh):
        return lax.dynamic_update_slice(jnp.zeros((1, D), F32), sh, (0, me * 128))
    return jnp.concatenate([attn_norm, ffn_norm, final_norm.reshape(1, D), place(pool_norm_sh),
                            place(pool_scale_sh), jnp.zeros((2, D), F32)], axis=0)


def _vec_unpack(p, me):
    def take(r):
        return lax.dynamic_slice(p[r:r + 1], (0, me * 128), (1, 128))
    return p[0:1], p[1:3], p[3], take(4), take(5)


def kernel(x, attn_norm, w_qkv, w_attn_out, pool_norm, w_pool_in, w_pool_group, pool_scale, ffn_norm, w_ffn_gate_up, w_ffn_down, final_norm, loss_target, m_attn_norm, m_w_qkv, m_w_attn_out, m_pool_norm, m_w_pool_in, m_w_pool_group, m_pool_scale, m_ffn_norm, m_w_ffn_gate_up, m_w_ffn_down, m_final_norm, v_attn_norm, v_w_qkv, v_w_attn_out, v_pool_norm, v_w_pool_in, v_w_pool_group, v_pool_scale, v_ffn_norm, v_w_ffn_gate_up, v_w_ffn_down, v_final_norm):
    me = 4 * lax.axis_index("x") + 2 * lax.axis_index("y") + lax.axis_index("c")

    pw = _pack_sections(w_qkv, w_attn_out, w_pool_in, w_pool_group, w_ffn_gate_up, w_ffn_down)
    vsh = jnp.concatenate([pool_norm, pool_scale, jnp.zeros((6, 128), F32)], axis=0)

    pw["pv"] = vsh
    comm = _Comm(pw, me, placed={"qkv": _place_transposed(w_qkv[0], me, "place_qkv")})

    grad_x, vec = _local_step(x[0], loss_target[0], comm, attn_norm, ffn_norm, final_norm)

    small = ((attn_norm, ffn_norm, final_norm, pool_norm, pool_scale),
             (m_attn_norm, m_ffn_norm, m_final_norm, m_pool_norm, m_pool_scale),
             (v_attn_norm, v_ffn_norm, v_final_norm, v_pool_norm, v_pool_scale))
    small, grad_x = lax.optimization_barrier((small, grad_x))
    vw, vm, vv = (_vec_pack(*s, me) for s in small)

    gu_t = [jnp.swapaxes(a, 1, 2) for a in (w_ffn_gate_up, m_w_ffn_gate_up, v_w_ffn_gate_up)]
    res = {}
    gu_res, d_res = None, None
    vec_out = None
    after = grad_x
    for group in range(len(RS_GROUPS)):
        if group == len(RS_GROUPS) - 1:
            VR = _bcast_all(vec, "exchange_vector_grads", deps=(after,))
            vec_out = _adamw(VR, vw, vm, vv, tr=8, name="adamw_vec")
            after = vec_out[0]
        for n, R in comm.received(group, after).items():
            if n in ("d0", "d1"):
                d_res = _adamw(R, w_ffn_down, m_w_ffn_down, v_w_ffn_down, tr=352, name=f"adamw_{n}",
                               layer=int(n[1]), prev=d_res)
                after = d_res[0]
            elif n in ("gu0", "gu1"):
                gu_res = _adamw(R, *gu_t, tr=352, name=f"adamw_{n}", layer=int(n[2]), prev=gu_res)
                after = gu_res[0]
            elif n == "pg":
                out = _adamw_pool_group(R, w_pool_group[0], m_w_pool_group[0], v_w_pool_group[0])
                res["pg"] = tuple(a[None] for a in out)
                after = out[0]
            elif n in ("wo", "wpi"):
                w, m, v = ((w_attn_out, m_w_attn_out, v_w_attn_out) if n == "wo"
                           else (w_pool_in, m_w_pool_in, v_w_pool_in))
                res[n] = _adamw(R, w[0], m[0], v[0], tr=128, name=f"adamw_{n}")
                res[n] = tuple(a[None] for a in res[n])
                after = res[n][0]
            else:
                g = _grad_sum_t(R, "grad_sum_qkv")
                out = _adam_plain(g, w_qkv[0], m_w_qkv[0], v_w_qkv[0], tr=256, name="adamw_qkv")
                res["qkv"] = tuple(a[None] for a in (g,) + tuple(out))
                after = out[0]
    res["gu"] = tuple(jnp.swapaxes(a, 1, 2) for a in gu_res)
    res["d"] = tuple(d_res)

    outs = []
    for kind in range(4):
        an, fn, fin, pn, ps = _vec_unpack(vec_out[kind], me)
        outs.append((an, res["qkv"][kind], res["wo"][kind], pn, res["wpi"][kind], res["pg"][kind], ps, fn,
                     res["gu"][kind], res["d"][kind], fin))
    loss = 0.5 * jnp.sum(vec_out[0][6]) / D
    return (loss, grad_x[None]) + outs[0] + outs[1] + outs[2] + outs[3]
```

```python
import jax
import jax.numpy as jnp
from jax import lax
from jax.experimental import pallas as pl
from jax.experimental.pallas import tpu as pltpu

F32 = jnp.float32
BF16 = jnp.bfloat16

D = 1024
NDEV = 8
HEADS = 8
HD = 128
QB = 128
NGROUPS = 3
DILS = (1, 4, 16)
DFF = 2816
HCH = 1408
POOL_G = 4
PGD = 256
RMS_EPS = 1e-6
NEG = -1e30

ADAM_LR = 0.001
ADAM_B1 = 0.9
ADAM_B2 = 0.999
ADAM_EPS = 1e-08
ADAM_WD = 0.01
ADAM_STEP = 10

VMEM_LIMIT = 52 * 1024 * 1024

SECTIONS = (("qkv", 1152), ("wo", 128), ("wpi", 128), ("gu0", 704), ("gu1", 704),
            ("d0", 352), ("d1", 352), ("pg", 32))
LOC_OFF = {}
GLB_OFF = {}
_o = 0
for _n, _r in SECTIONS:
    LOC_OFF[_n] = _o
    GLB_OFF[_n] = _o * NDEV
    _o += _r
PACK_ROWS = _o
GLB_ROWS = PACK_ROWS * NDEV
SEC_ROWS = dict(SECTIONS)
SEC_ROWS["pv"] = 8


def _cparams(n_grid):
    return pltpu.CompilerParams(dimension_semantics=("arbitrary",) * n_grid, vmem_limit_bytes=VMEM_LIMIT)


def _shard_pos(name, dev):
    n = SEC_ROWS[name]
    if name in ("gu0", "gu1"):
        return ((dev % 4) // 2) * (2 * HCH) + (dev // 4) * HCH + (dev % 2) * n
    return dev * n


def _mm(a, b, *, mode, M, N, K, tm, tn, tk, out_dtype, name, a_off=(0, 0), b_off=(0, 0), res=None,
        out_rows=None, out_off=0, out_prev=None, deps=()):
    nm, nn, nk = M // tm, N // tn, K // tk
    assert nm * tm == M and nn * tn == N and nk * tk == K
    if mode == "nn":
        a_bs, b_bs = (tm, tk), (tk, tn)
        a_ix = lambda i, j, k: (i, k)
        b_ix = lambda i, j, k: (k, j)
        dims = (((1,), (0,)), ((), ()))
    elif mode == "nt":
        a_bs, b_bs = (tm, tk), (tn, tk)
        a_ix = lambda i, j, k: (i, k)
        b_ix = lambda i, j, k: (j, k)
        dims = (((1,), (1,)), ((), ()))
    else:
        a_bs, b_bs = (tk, tm), (tk, tn)
        a_ix = lambda i, j, k: (k, i)
        b_ix = lambda i, j, k: (k, j)
        dims = (((0,), (0,)), ((), ()))

    def spec(bs, ix, off):
        def im(i, j, k):
            r, c = ix(i, j, k)
            return (r + off[0], c + off[1])
        return pl.BlockSpec(bs, im)

    in_specs = [spec(a_bs, a_ix, a_off), spec(b_bs, b_ix, b_off)]
    args = [a, b]
    if res is not None:
        in_specs.append(pl.BlockSpec((tm, tn), lambda i, j, k: (i, j)))
        args.append(res)
    out_shape = jax.ShapeDtypeStruct((M if out_rows is None else out_rows, N), out_dtype)
    out_spec = pl.BlockSpec((tm, tn), lambda i, j, k: (i + out_off, j))
    has_res = res is not None
    extra = list(deps) + ([out_prev] if out_prev is not None else [])
    for dep in extra:
        in_specs.append(pl.BlockSpec(memory_space=pl.ANY))
        args.append(dep)
    o_pos = 2 + int(has_res) + len(extra)
    aliases = {len(args) - 1: 0} if out_prev is not None else {}

    def kern(*refs):
        a_ref, b_ref = refs[0], refs[1]
        res_ref = refs[2] if has_res else None
        o_ref = refs[o_pos]
        av = a_ref[...]
        bv = b_ref[...]
        if av.dtype != BF16:
            av = av.astype(BF16)
        if bv.dtype != BF16:
            bv = bv.astype(BF16)
        part = lax.dot_general(av, bv, dims, preferred_element_type=F32)

        def write(val):
            if has_res:
                val = val + res_ref[...]
            o_ref[...] = val.astype(out_dtype)

        if nk == 1:
            write(part)
        else:
            acc_ref = refs[-1]
            k = pl.program_id(2)

            @pl.when(k == 0)
            def _():
                acc_ref[...] = part

            @pl.when(k > 0)
            def _():
                acc_ref[...] += part

            @pl.when(k == nk - 1)
            def _():
                write(acc_ref[...])

    scratch = [pltpu.VMEM((tm, tn), F32)] if nk > 1 else []
    return pl.pallas_call(
        kern, grid=(nm, nn, nk), in_specs=in_specs, out_specs=out_spec, out_shape=out_shape,
        scratch_shapes=scratch, input_output_aliases=aliases, compiler_params=_cparams(3), name=name)(*args)


def _mm_rms_bwd(a, b, x, g, dres, *, mode, M, K, tm, name, b_off=(0, 0), deps=()):
    nd = len(deps)
    b_bs = (K, D) if mode == "nn" else (D, K)
    dims = (((1,), (0,)), ((), ())) if mode == "nn" else (((1,), (1,)), ((), ()))

    def kern(a_ref, b_ref, x_ref, g_ref, dres_ref, *rest):
        dx_ref, dg_ref = rest[nd:]
        i = pl.program_id(0)
        av = a_ref[...]
        if av.dtype != BF16:
            av = av.astype(BF16)
        dhv = lax.dot_general(av, b_ref[...], dims, preferred_element_type=F32)
        xv = x_ref[...]
        r = lax.rsqrt(jnp.mean(xv * xv, axis=-1, keepdims=True) + RMS_EPS)
        xhat = xv * r
        gy = dhv * g_ref[...]
        dx_ref[...] = dres_ref[...] + r * (gy - xhat * jnp.mean(gy * xhat, axis=-1, keepdims=True))
        part = jnp.sum(dhv * xhat, axis=0, keepdims=True)

        @pl.when(i == 0)
        def _():
            dg_ref[...] = part

        @pl.when(i > 0)
        def _():
            dg_ref[...] += part

    row = pl.BlockSpec((tm, D), lambda i: (i, 0))
    vec = pl.BlockSpec((1, D), lambda i: (0, 0))
    return pl.pallas_call(
        kern, grid=(M // tm,),
        in_specs=[pl.BlockSpec((tm, K), lambda i: (i, 0)),
                  pl.BlockSpec(b_bs, lambda i: b_off, pipeline_mode=pl.Buffered(1)), row, vec, row]
        + [pl.BlockSpec(memory_space=pl.ANY)] * nd,
        out_specs=[row, vec],
        out_shape=[jax.ShapeDtypeStruct((M, D), F32), jax.ShapeDtypeStruct((1, D), F32)],
        compiler_params=_cparams(1), name=name)(a, b, x, g, dres, *deps)


def _mm_res_norm(a, b, res, g, *, K, tm, name, b_off=(0, 0), tgt=None):
    M = a.shape[0]
    head = tgt is not None

    def kern(a_ref, b_ref, res_ref, g_ref, *rest):
        xv = res_ref[...] + jnp.dot(a_ref[...], b_ref[...], preferred_element_type=F32)
        gv = g_ref[...]
        r = lax.rsqrt(jnp.mean(xv * xv, axis=-1, keepdims=True) + RMS_EPS)
        xhat = xv * r
        if not head:
            xo_ref, h_ref = rest
            xo_ref[...] = xv
            h_ref[...] = (xhat * gv).astype(BF16)
            return
        t_ref, dx_ref, dg_ref, ls_ref = rest
        i = pl.program_id(0)
        e = xhat * gv - t_ref[...]
        dy = e * (1.0 / D)
        gy = dy * gv
        dx_ref[...] = r * (gy - xhat * jnp.mean(gy * xhat, axis=-1, keepdims=True))
        dgp = jnp.sum(dy * xhat, axis=0, keepdims=True)
        lsp = jnp.sum(e * e, axis=0, keepdims=True)

        @pl.when(i == 0)
        def _():
            dg_ref[...] = dgp
            ls_ref[...] = lsp

        @pl.when(i > 0)
        def _():
            dg_ref[...] += dgp
            ls_ref[...] += lsp

    row = pl.BlockSpec((tm, D), lambda i: (i, 0))
    vec = pl.BlockSpec((1, D), lambda i: (0, 0))
    in_specs = [pl.BlockSpec((tm, K), lambda i: (i, 0)),
                pl.BlockSpec((K, D), lambda i: b_off, pipeline_mode=pl.Buffered(1)), row, vec]
    if head:
        return pl.pallas_call(
            kern, grid=(M // tm,), in_specs=in_specs + [row], out_specs=[row, vec, vec],
            out_shape=[jax.ShapeDtypeStruct((M, D), F32), jax.ShapeDtypeStruct((1, D), F32),
                       jax.ShapeDtypeStruct((1, D), F32)],
            compiler_params=_cparams(1), name=name)(a, b, res, g, tgt)
    return pl.pallas_call(
        kern, grid=(M // tm,), in_specs=in_specs, out_specs=[row, row],
        out_shape=[jax.ShapeDtypeStruct((M, D), F32), jax.ShapeDtypeStruct((M, D), BF16)],
        compiler_params=_cparams(1), name=name)(a, b, res, g)


def _rms_fwd(x, g, name, deps=()):
    S = x.shape[0]
    tr = 512

    def kern(x_ref, g_ref, *rest):
        h_ref = rest[-1]
        xv = x_ref[...]
        r = lax.rsqrt(jnp.mean(xv * xv, axis=-1, keepdims=True) + RMS_EPS)
        h_ref[...] = (xv * r * g_ref[...]).astype(BF16)

    return pl.pallas_call(
        kern, grid=(S // tr,),
        in_specs=[pl.BlockSpec((tr, D), lambda i: (i, 0)), pl.BlockSpec((1, D), lambda i: (0, 0))]
        + [pl.BlockSpec(memory_space=pl.ANY)] * len(deps),
        out_specs=pl.BlockSpec((tr, D), lambda i: (i, 0)),
        out_shape=jax.ShapeDtypeStruct((S, D), BF16), compiler_params=_cparams(1), name=name)(x, g, *deps)


def _chunks_put(scr, val):
    for c in range(scr.shape[0]):
        scr[c] = val[:, c * 128:(c + 1) * 128]


def _chunks_get(scr):
    return jnp.concatenate([scr[c] for c in range(scr.shape[0])], axis=1)


def _chunks_rows(scr, r, n, dil):
    return jnp.concatenate([scr.at[c][pl.ds(r, n, stride=dil), :] for c in range(scr.shape[0])], axis=1)


def _chunks_add_rows(scr, val, r, n, dil, accumulate):
    for c in range(scr.shape[0]):
        rows = pl.ds(r, n, stride=dil)
        piece = val[:, c * 128:(c + 1) * 128]
        tile = scr.at[c]
        tile[rows, :] = tile[rows, :] + piece if accumulate else piece


def _rms_fwd_folded(x, g, name, deps=()):
    S = x.shape[0]
    tr = 512
    dils = DILS[1:]

    def kern(x_ref, g_ref, *rest):
        outs, scr = rest[len(deps):-1], rest[-1]
        xv = x_ref[...]
        r = lax.rsqrt(jnp.mean(xv * xv, axis=-1, keepdims=True) + RMS_EPS)
        h = (xv * r * g_ref[...]).astype(BF16)
        outs[0][...] = h
        _chunks_put(scr, h.astype(F32))
        for o_ref, dil in zip(outs[1:], dils):
            for res in range(dil):
                o_ref[res] = _chunks_rows(scr, res, tr // dil, dil).astype(BF16)

    return pl.pallas_call(
        kern, grid=(S // tr,),
        in_specs=[pl.BlockSpec((tr, D), lambda i: (i, 0)), pl.BlockSpec((1, D), lambda i: (0, 0))]
        + [pl.BlockSpec(memory_space=pl.ANY)] * len(deps),
        out_specs=[pl.BlockSpec((tr, D), lambda i: (i, 0))]
        + [pl.BlockSpec((dil, tr // dil, D), lambda i: (0, i, 0)) for dil in dils],
        out_shape=[jax.ShapeDtypeStruct((S, D), BF16)]
        + [jax.ShapeDtypeStruct((dil, S // dil, D), BF16) for dil in dils],
        scratch_shapes=[pltpu.VMEM((D // 128, tr, 128), F32)],
        compiler_params=_cparams(1), name=name)(x, g, *deps)


def _rms_bwd(dh, x, g, dres, name, folded=()):
    S = x.shape[0]
    tr = 512
    nf = len(folded)

    def kern(dh_ref, *rest):
        f_refs = rest[:nf]
        x_ref, g_ref, dres_ref, dx_ref, dg_ref = rest[nf:nf + 5]
        i = pl.program_id(0)
        xv = x_ref[...]
        if nf:
            acc_ref = rest[nf + 5]
            _chunks_put(acc_ref, dh_ref[...].astype(F32))
            for f_ref in f_refs:
                dil = f_ref.shape[0]
                for res in range(dil):
                    _chunks_add_rows(acc_ref, f_ref[res], res, tr // dil, dil, True)
            dhv = _chunks_get(acc_ref)
        else:
            dhv = dh_ref[...].astype(F32)
        r = lax.rsqrt(jnp.mean(xv * xv, axis=-1, keepdims=True) + RMS_EPS)
        xhat = xv * r
        gy = dhv * g_ref[...]
        dx_ref[...] = dres_ref[...] + r * (gy - xhat * jnp.mean(gy * xhat, axis=-1, keepdims=True))
        part = jnp.sum(dhv * xhat, axis=0, keepdims=True)

        @pl.when(i == 0)
        def _():
            dg_ref[...] = part

        @pl.when(i > 0)
        def _():
            dg_ref[...] += part

    row = pl.BlockSpec((tr, D), lambda i: (i, 0))
    vec = pl.BlockSpec((1, D), lambda i: (0, 0))
    fspecs = [pl.BlockSpec((f.shape[0], tr // f.shape[0], D), lambda i: (0, i, 0)) for f in folded]
    return pl.pallas_call(
        kern, grid=(S // tr,), in_specs=[row] + fspecs + [row, vec, row], out_specs=[row, vec],
        out_shape=[jax.ShapeDtypeStruct((S, D), F32), jax.ShapeDtypeStruct((1, D), F32)],
        scratch_shapes=[pltpu.VMEM((D // 128, tr, 128), F32)] if nf else [],
        compiler_params=_cparams(1), name=name)(dh, *folded, x, g, dres)


def _ffn_up(h, G, name):
    S = h.shape[0]
    tm = 512
    nj = DFF // HCH

    def kern(h_ref, w_ref, gu_ref, act_ref):
        gu = lax.dot_general(h_ref[...], w_ref[...], (((1,), (1,)), ((), ())), preferred_element_type=F32)
        gu_ref[...] = gu.astype(BF16)
        gate = gu[:, :HCH]
        up = gu[:, HCH:]
        act_ref[...] = (gate * jax.nn.sigmoid(gate) * up).astype(BF16)

    return pl.pallas_call(
        kern, grid=(nj, S // tm),
        in_specs=[pl.BlockSpec((tm, D), lambda j, i: (i, 0)),
                  pl.BlockSpec((2 * HCH, D), lambda j, i: (j, 0))],
        out_specs=[pl.BlockSpec((tm, 2 * HCH), lambda j, i: (i, j)),
                   pl.BlockSpec((tm, HCH), lambda j, i: (i, j))],
        out_shape=[jax.ShapeDtypeStruct((S, 2 * DFF), BF16), jax.ShapeDtypeStruct((S, DFF), BF16)],
        compiler_params=_cparams(2), name=name)(h, G)


def _ffn_down_bwd(dx, G, gu, name):
    S = dx.shape[0]
    tm = 512
    nj = DFF // HCH

    def kern(dx_ref, w_ref, gu_ref, o_ref):
        dxv = dx_ref[...].astype(BF16)
        for j in range(nj):
            c0 = 2 * HCH * j
            dact = lax.dot_general(dxv, w_ref[HCH * j:HCH * (j + 1), :], (((1,), (1,)), ((), ())),
                                   preferred_element_type=F32)
            gate = gu_ref[:, c0:c0 + HCH].astype(F32)
            up = gu_ref[:, c0 + HCH:c0 + 2 * HCH].astype(F32)
            sig = jax.nn.sigmoid(gate)
            silu = gate * sig
            o_ref[:, c0:c0 + HCH] = (dact * up * (sig * (1.0 + gate * (1.0 - sig)))).astype(BF16)
            o_ref[:, c0 + HCH:c0 + 2 * HCH] = (dact * silu).astype(BF16)

    row = pl.BlockSpec((tm, 2 * DFF), lambda i: (i, 0))
    return pl.pallas_call(
        kern, grid=(S // tm,),
        in_specs=[pl.BlockSpec((tm, D), lambda i: (i, 0)),
                  pl.BlockSpec((DFF, D), lambda i: (0, 0), pipeline_mode=pl.Buffered(1)), row],
        out_specs=row, out_shape=jax.ShapeDtypeStruct((S, 2 * DFF), BF16),
        compiler_params=_cparams(1), name=name)(dx, G, gu)


def _trail(u, *, backward, name):
    S = u.shape[0]

    def kern(u_ref, o_ref):
        g = pl.program_id(0)
        for grp in range(POOL_G):
            @pl.when(g == grp)
            def _(grp=grp):
                uv = u_ref[...].astype(F32)
                row = lax.broadcasted_iota(jnp.int32, uv.shape, 0)
                cnt = jnp.minimum(row + 1, 2 << grp).astype(F32)
                s = uv / cnt if backward else uv
                for k in (1, 2, 4, 8)[:grp + 1]:
                    if backward:
                        sh = jnp.where(row < S - k, pltpu.roll(s, S - k, 0), 0.0)
                    else:
                        sh = jnp.where(row >= k, pltpu.roll(s, k, 0), 0.0)
                    s = s + sh
                if backward:
                    o_ref[...] = (s - uv).astype(BF16)
                else:
                    o_ref[...] = (s / cnt - uv).astype(BF16)

    blk = pl.BlockSpec((S, PGD), lambda g: (0, g))
    return pl.pallas_call(
        kern, grid=(POOL_G,), in_specs=[blk], out_specs=blk,
        out_shape=jax.ShapeDtypeStruct((S, D), BF16), compiler_params=_cparams(1), name=name)(u)


def _pool_out(yd, G, scale, xres):
    S = yd.shape[0]
    tm = min(S, 4096)

    def kern(y_ref, w_ref, s_ref, x_ref, o_ref):
        z = jnp.dot(y_ref[...], w_ref[...], preferred_element_type=F32)
        o_ref[...] = x_ref[...] + z * s_ref[...]

    tile = pl.BlockSpec((tm, PGD), lambda i, g: (i, g))
    return pl.pallas_call(
        kern, grid=(S // tm, POOL_G),
        in_specs=[tile, pl.BlockSpec((PGD, PGD), lambda i, g: (0, g)),
                  pl.BlockSpec((1, PGD), lambda i, g: (0, g)), tile],
        out_specs=tile, out_shape=jax.ShapeDtypeStruct((S, D), F32),
        compiler_params=_cparams(2), name="pool_out")(yd, G, scale, xres)


def _pool_out_bwd(dz, yd, G, scale):
    S = yd.shape[0]
    tm = min(S, 4096)
    ni = S // tm

    def kern(dz_ref, y_ref, w_ref, s_ref, dy_ref, ds_ref, dw_ref, acc_ref):
        i = pl.program_id(1)
        dzv = dz_ref[...]
        yv = y_ref[...]
        wv = w_ref[...]
        zraw = jnp.dot(yv, wv, preferred_element_type=F32)
        dsp = jnp.sum(dzv * zraw, axis=0, keepdims=True)
        dzr = (dzv * s_ref[...]).astype(BF16)
        dy_ref[...] = lax.dot_general(dzr, wv, (((1,), (1,)), ((), ())), preferred_element_type=F32)
        dwp = lax.dot_general(yv, dzr, (((0,), (0,)), ((), ())), preferred_element_type=F32)

        @pl.when(i == 0)
        def _():
            ds_ref[...] = dsp
            acc_ref[...] = dwp

        @pl.when(i > 0)
        def _():
            ds_ref[...] += dsp
            acc_ref[...] += dwp

        @pl.when(i == ni - 1)
        def _():
            dw_ref[...] = acc_ref[...].astype(BF16)

    tile = pl.BlockSpec((tm, PGD), lambda g, i: (i, g))
    return pl.pallas_call(
        kern, grid=(POOL_G, ni),
        in_specs=[tile, tile, pl.BlockSpec((PGD, PGD), lambda g, i: (0, g)),
                  pl.BlockSpec((1, PGD), lambda g, i: (0, g))],
        out_specs=[tile, pl.BlockSpec((1, PGD), lambda g, i: (0, g)),
                   pl.BlockSpec((PGD, PGD), lambda g, i: (0, g))],
        out_shape=[jax.ShapeDtypeStruct((S, D), F32), jax.ShapeDtypeStruct((1, D), F32),
                   jax.ShapeDtypeStruct((PGD, D), BF16)],
        scratch_shapes=[pltpu.VMEM((PGD, PGD), F32)],
        compiler_params=_cparams(2), name="pool_out_bwd")(dz, yd, G, scale)


def _bias_table():
    qi = jnp.arange(QB)[:, None]
    ki = jnp.arange(2 * QB)[None, :]
    delta = QB + qi - ki
    inband = (delta >= 0) & (delta <= QB)
    n = NGROUPS * HEADS
    slopes = jnp.exp2(-8.0 * jnp.arange(1, n + 1, dtype=F32) / n).reshape(NGROUPS, HEADS)
    dil = jnp.asarray(DILS, F32)
    bias = -slopes[:, :, None, None] * (delta.astype(F32)[None, None] * dil[:, None, None, None])
    return jnp.where(inband[None, None], bias, NEG)


def _attn_fwd(qkv_f, bias, nb, name):
    S = qkv_f.shape[0]
    nblk = S // QB
    scale = HD ** -0.5

    def kern(q_ref, k2_ref, kp_ref, v2_ref, vp_ref, b_ref, o_ref, l_ref, s_scr, p_scr, r_scr):
        s_id = pl.program_id(0)
        col = lax.broadcasted_iota(jnp.int32, (QB, 2 * QB), 1)
        lane = lax.broadcasted_iota(jnp.int32, (QB, HD), 1)

        def keys(sub, cur2_ref, prev_ref, sl):
            if sub:
                return cur2_ref[:, sl]
            return jnp.concatenate([prev_ref[:, sl], cur2_ref[0:QB, sl]], axis=0)

        for sub in range(2):
            for h in range(HEADS):
                sl = slice(h * HD, (h + 1) * HD)
                s_scr[sub * HEADS + h] = lax.dot_general(
                    q_ref[sub * QB:(sub + 1) * QB, sl], keys(sub, k2_ref, kp_ref, sl), (((1,), (1,)), ((), ())),
                    preferred_element_type=F32)
        for sub in range(2):
            has_prev = jnp.bitwise_and(2 * s_id + sub, nb - 1) != 0
            dead = jnp.logical_and(col < QB, jnp.logical_not(has_prev))
            lse_all = jnp.zeros((QB, HD), F32)
            for h in range(HEADS):
                u = sub * HEADS + h
                s = s_scr[u] * scale + b_ref[h]
                s = jnp.where(dead, NEG, s)
                m = jnp.max(s, axis=-1, keepdims=True)
                p = jnp.exp(s - m)
                den = jnp.sum(p, axis=-1, keepdims=True)
                p_scr[u] = p.astype(BF16)
                r_scr[u] = jnp.broadcast_to(1.0 / den, (QB, HD))
                lse_all = jnp.where(lane == h, m + jnp.log(den), lse_all)
            l_ref[sub * QB:(sub + 1) * QB, :] = lse_all
        for sub in range(2):
            for h in range(HEADS):
                u = sub * HEADS + h
                sl = slice(h * HD, (h + 1) * HD)
                o = jnp.dot(p_scr[u], keys(sub, v2_ref, vp_ref, sl), preferred_element_type=F32) * r_scr[u]
                o_ref[sub * QB:(sub + 1) * QB, sl] = o.astype(BF16)

    def pair(colblk):
        return pl.BlockSpec((2 * QB, D), lambda s: (s, colblk))

    def prev(colblk):
        return pl.BlockSpec((QB, D), lambda s: (jnp.maximum(2 * s - 1, 0), colblk))

    return pl.pallas_call(
        kern, grid=(nblk // 2,),
        in_specs=[pair(0), pair(1), prev(1), pair(2), prev(2), pl.BlockSpec((HEADS, QB, 2 * QB), lambda s: (0, 0, 0))],
        out_specs=[pl.BlockSpec((2 * QB, D), lambda s: (s, 0)), pl.BlockSpec((2 * QB, HD), lambda s: (s, 0))],
        out_shape=[jax.ShapeDtypeStruct((S, D), BF16), jax.ShapeDtypeStruct((S, HD), F32)],
        scratch_shapes=[pltpu.VMEM((2 * HEADS, QB, 2 * QB), F32), pltpu.VMEM((2 * HEADS, QB, 2 * QB), BF16),
                        pltpu.VMEM((2 * HEADS, QB, HD), F32)],
        compiler_params=_cparams(1), name=name)(qkv_f, qkv_f, qkv_f, qkv_f, qkv_f, bias)


def _natural(ref, scr, tm):
    dil = ref.shape[0]
    for res in range(dil):
        _chunks_add_rows(scr, ref[res].astype(F32), res, tm // dil, dil, False)
    return _chunks_get(scr)


def _attn_merge(os, lses):
    S = os[0].shape[0]
    tm = 512

    def kern(o0, o1, o2, l0, l1, l2, om_ref, lm_ref, ls1, ls2, os1, os2):
        la = l0[...]
        lb = _natural(l1, ls1, tm)
        lc = _natural(l2, ls2, tm)
        m = jnp.maximum(jnp.maximum(la, lb), lc)
        e0, e1, e2 = jnp.exp(la - m), jnp.exp(lb - m), jnp.exp(lc - m)
        tot = e0 + e1 + e2
        lm_ref[...] = m + jnp.log(tot)
        w0, w1, w2 = e0 / tot, e1 / tot, e2 / tot
        for res in range(o1.shape[0]):
            _chunks_add_rows(os1, o1[res].astype(F32), res, tm // o1.shape[0], o1.shape[0], False)
        for res in range(o2.shape[0]):
            _chunks_add_rows(os2, o2[res].astype(F32), res, tm // o2.shape[0], o2.shape[0], False)
        for h in range(HEADS):
            sl = slice(h * HD, (h + 1) * HD)
            acc = w0[:, h:h + 1] * o0[:, sl].astype(F32) + w1[:, h:h + 1] * os1[h] + w2[:, h:h + 1] * os2[h]
            om_ref[:, sl] = acc.astype(BF16)

    def spec(a, c):
        if a.ndim == 2:
            return pl.BlockSpec((tm, c), lambda i: (i, 0))
        return pl.BlockSpec((a.shape[0], tm // a.shape[0], c), lambda i: (0, i, 0))

    return pl.pallas_call(
        kern, grid=(S // tm,),
        in_specs=[spec(a, D) for a in os] + [spec(a, HD) for a in lses],
        out_specs=[pl.BlockSpec((tm, D), lambda i: (i, 0)), pl.BlockSpec((tm, HD), lambda i: (i, 0))],
        out_shape=[jax.ShapeDtypeStruct((S, D), BF16), jax.ShapeDtypeStruct((S, HD), F32)],
        scratch_shapes=[pltpu.VMEM((1, tm, HD), F32), pltpu.VMEM((1, tm, HD), F32),
                        pltpu.VMEM((HEADS, tm, HD), F32), pltpu.VMEM((HEADS, tm, HD), F32)],
        compiler_params=_cparams(1), name="attn_merge")(*os, *lses)


def _attn_bwd_prep(do, o, lse):
    S = o.shape[0]
    tm = 512
    dils = DILS[1:]

    def kern(do_ref, o_ref, l_ref, *rest):
        do_outs, l_outs, d_outs = rest[0:3], rest[3:5], rest[5:8]
        do_scr, l_scr, d_scr = rest[8:11]
        lane = lax.broadcasted_iota(jnp.int32, (tm, HD), 1)
        acc = jnp.zeros((tm, HD), F32)
        for h in range(HEADS):
            sl = slice(h * HD, (h + 1) * HD)
            prod = do_ref[:, sl] * o_ref[:, sl].astype(F32)
            acc = jnp.where(lane == h, jnp.sum(prod, axis=-1, keepdims=True), acc)
        d_scr[0] = acc
        l_scr[0] = l_ref[...]
        _chunks_put(do_scr, do_ref[...])
        do_outs[0][...] = do_ref[...].astype(BF16)
        d_outs[0][...] = acc
        for j, dil in enumerate(dils):
            for res in range(dil):
                n = tm // dil
                do_outs[1 + j][res] = _chunks_rows(do_scr, res, n, dil).astype(BF16)
                l_outs[j][res] = _chunks_rows(l_scr, res, n, dil)
                d_outs[1 + j][res] = _chunks_rows(d_scr, res, n, dil)

    def nat(c):
        return pl.BlockSpec((tm, c), lambda i: (i, 0))

    def fol(dil, c):
        return pl.BlockSpec((dil, tm // dil, c), lambda i: (0, i, 0))

    def shapes(c, dt, with_natural):
        first = [jax.ShapeDtypeStruct((S, c), dt)] if with_natural else []
        return first + [jax.ShapeDtypeStruct((dil, S // dil, c), dt) for dil in dils]

    outs = pl.pallas_call(
        kern, grid=(S // tm,), in_specs=[nat(D), nat(D), nat(HD)],
        out_specs=[nat(D)] + [fol(dil, D) for dil in dils] + [fol(dil, HD) for dil in dils]
        + [nat(HD)] + [fol(dil, HD) for dil in dils],
        out_shape=shapes(D, BF16, True) + shapes(HD, F32, False) + shapes(HD, F32, True),
        scratch_shapes=[pltpu.VMEM((HEADS, tm, HD), F32), pltpu.VMEM((1, tm, HD), F32), pltpu.VMEM((1, tm, HD), F32)],
        compiler_params=_cparams(1), name="attn_bwd_prep")(do, o, lse)
    return outs[0:3], [lse] + list(outs[3:5]), outs[5:8]


def _attn_bwd(qkv_f, do_f, lse_f, delta_f, bias, nb, name):
    S = qkv_f.shape[0]
    nblk = S // QB
    scale = HD ** -0.5

    def kern(q_ref, kc_ref, kp_ref, vc_ref, vp_ref, do_ref, l_ref, d_ref, b_ref, out_ref, dq_c, dk_c, dv_c,
             s_scr, dp_scr, ds_scr, p_scr):
        b = pl.program_id(0)

        @pl.when(b == 0)
        def _():
            dq_c[...] = jnp.zeros_like(dq_c)
            dk_c[...] = jnp.zeros_like(dk_c)
            dv_c[...] = jnp.zeros_like(dv_c)

        @pl.when(b == nblk)
        def _():
            out_ref[:, 0:D] = dq_c[...].astype(BF16)
            out_ref[:, D:2 * D] = dk_c[...].astype(BF16)
            out_ref[:, 2 * D:3 * D] = dv_c[...].astype(BF16)

        @pl.when(b < nblk)
        def _():
            has_prev = jnp.bitwise_and(b, nb - 1) != 0
            col = lax.broadcasted_iota(jnp.int32, (QB, 2 * QB), 1)
            dead = jnp.logical_and(col < QB, jnp.logical_not(has_prev))
            out_ref[:, 0:D] = dq_c[...].astype(BF16)
            lv = l_ref[...]
            dv_ = d_ref[...]
            for h in range(HEADS):
                sl = slice(h * HD, (h + 1) * HD)
                kk = jnp.concatenate([kp_ref[:, sl], kc_ref[:, sl]], axis=0)
                vv = jnp.concatenate([vp_ref[:, sl], vc_ref[:, sl]], axis=0)
                s_scr[h] = lax.dot_general(q_ref[:, sl], kk, (((1,), (1,)), ((), ())), preferred_element_type=F32)
                dp_scr[h] = lax.dot_general(do_ref[:, sl], vv, (((1,), (1,)), ((), ())),
                                            preferred_element_type=F32)
            for h in range(HEADS):
                s = s_scr[h] * scale + b_ref[h]
                s = jnp.where(dead, NEG, s)
                p = jnp.exp(s - lv[:, h:h + 1])
                ds_scr[h] = (p * (dp_scr[h] - dv_[:, h:h + 1]) * scale).astype(BF16)
                p_scr[h] = p.astype(BF16)
            for h in range(HEADS):
                sl = slice(h * HD, (h + 1) * HD)
                kk = jnp.concatenate([kp_ref[:, sl], kc_ref[:, sl]], axis=0)
                ds = ds_scr[h]
                dq_c[:, sl] = jnp.dot(ds, kk, preferred_element_type=F32)
                dkk = lax.dot_general(ds, q_ref[:, sl], (((0,), (0,)), ((), ())), preferred_element_type=F32)
                dvv = lax.dot_general(p_scr[h], do_ref[:, sl], (((0,), (0,)), ((), ())),
                                      preferred_element_type=F32)
                out_ref[:, D + h * HD:D + (h + 1) * HD] = (dk_c[:, sl] + dkk[:QB]).astype(BF16)
                out_ref[:, 2 * D + h * HD:2 * D + (h + 1) * HD] = (dv_c[:, sl] + dvv[:QB]).astype(BF16)
                dk_c[:, sl] = dkk[QB:]
                dv_c[:, sl] = dvv[QB:]

    last = nblk - 1

    def blk(colblk, prev):
        if prev:
            return pl.BlockSpec((QB, D), lambda b: (jnp.maximum(jnp.minimum(b, last) - 1, 0), colblk))
        return pl.BlockSpec((QB, D), lambda b: (jnp.minimum(b, last), colblk))

    stat = pl.BlockSpec((QB, HD), lambda b: (jnp.minimum(b, last), 0))
    return pl.pallas_call(
        kern, grid=(nblk + 1,),
        in_specs=[blk(0, False), blk(1, False), blk(1, True), blk(2, False), blk(2, True),
                  pl.BlockSpec((QB, D), lambda b: (jnp.minimum(b, last), 0)), stat, stat,
                  pl.BlockSpec((HEADS, QB, 2 * QB), lambda b: (0, 0, 0))],
        out_specs=pl.BlockSpec((QB, 3 * D), lambda b: (jnp.maximum(b - 1, 0), 0)),
        out_shape=jax.ShapeDtypeStruct((S, 3 * D), BF16),
        scratch_shapes=[pltpu.VMEM((QB, D), F32), pltpu.VMEM((QB, D), F32), pltpu.VMEM((QB, D), F32),
                        pltpu.VMEM((HEADS, QB, 2 * QB), F32), pltpu.VMEM((HEADS, QB, 2 * QB), F32),
                        pltpu.VMEM((HEADS, QB, 2 * QB), BF16), pltpu.VMEM((HEADS, QB, 2 * QB), BF16)],
        compiler_params=_cparams(1), name=name)(qkv_f, qkv_f, qkv_f, qkv_f, qkv_f, do_f, lse_f, delta_f, bias)


def _local_step(x, tgt, comm, attn_norm, ffn_norm, final_norm):
    S = x.shape[0]
    bias = _bias_table()
    g_attn = attn_norm.reshape(1, D)
    g_f0 = ffn_norm[0:1]
    g_f1 = ffn_norm[1:2]
    g_fin = final_norm.reshape(1, D)
    W = {}

    def ffn_fwd(xin, h, l, next_gain, target=None):
        gu, act = _ffn_up(h, W[f"gu{l}"], f"ffn_up{l}")
        return gu, act, _mm_res_norm(act, W[f"d{l}"], xin, next_gain, K=DFF, tm=512, tgt=target,
                                     name=f"ffn_down{l}")

    def ffn_bwd(dxo, xin, gain, h, gu, act, l, rs_group):
        dgu = _ffn_down_bwd(dxo, W[f"d{l}"], gu, f"ffn_down_bwd{l}")
        gw_d = _mm(act, dxo, mode="tn", M=DFF, N=D, K=S, tm=HCH, tn=D, tk=2048, out_dtype=BF16, name=f"gw_d{l}")
        gw_gu = _mm(dgu, h, mode="tn", M=2 * DFF, N=D, K=S, tm=HCH, tn=D, tk=2048, out_dtype=BF16, name=f"gw_gu{l}")
        token = comm.send_grads(rs_group, {f"d{l}": gw_d, f"gu{l}": gw_gu})
        return _mm_rms_bwd(dgu, W[f"gu{l}"], xin, gain, dxo, mode="nn", M=S, K=2 * DFF, tm=512, deps=(token,),
                           name=f"ffn_up_bwd{l}")

    nbs = [S // QB // dil for dil in DILS]
    hf = _rms_fwd_folded(x, g_attn, "rms_attn", deps=comm.ag_tokens)
    hf = [h.reshape(S, D) for h in hf]
    W.update(comm.weights(0, hf[0]))
    qkv_f, o_f, lse_f = [], [], []
    for g, dil in enumerate(DILS):
        qkv_f.append(_mm(hf[g], W["qkv"], mode="nt", M=S, N=3 * D, K=D, tm=2048, tn=1024, tk=D, out_dtype=BF16,
                         b_off=(3 * g, 0), name=f"qkv_proj{g}"))
        og, lg = _attn_fwd(qkv_f[g], bias[g], nbs[g], f"attn_fwd{g}")
        o_f.append(og if dil == 1 else og.reshape(dil, S // dil, D))
        lse_f.append(lg if dil == 1 else lg.reshape(dil, S // dil, HD))
    passing = [comm.pass_on(1, tuple(o_f)), comm.pass_on(2, tuple(o_f))]
    (o_f, lse_f), passing = lax.optimization_barrier(((o_f, lse_f), passing))
    o, lse = _attn_merge(o_f, lse_f)
    W.update(comm.weights(1, (o, passing[0])))
    x1, h1 = _mm_res_norm(o, W["wo"], x, g_f0, K=D, tm=1024, name="attn_out")
    pv = W["pv"].reshape(NDEV, 8, 128)
    pool_norm, pool_scale = pv[:, 0, :].reshape(1, D), pv[:, 1, :].reshape(1, D)
    gu0, act0, (x2, h2) = ffn_fwd(x1, h1, 0, pool_norm)

    W.update(comm.weights(2, (x2, passing[1])))
    u = _mm(h2, W["wpi"], mode="nn", M=S, N=D, K=D, tm=1024, tn=D, tk=D, out_dtype=F32, name="pool_in")
    yd = _trail(u, backward=False, name="trail_fwd")
    x3 = _pool_out(yd, W["pg"], pool_scale, x2)
    h3 = _rms_fwd(x3, g_f1, "rms_ffn1")
    gu1, act1, (dx4, d_fin, lossvec) = ffn_fwd(x3, h3, 1, g_fin, target=tgt)

    dx3, d_f1 = ffn_bwd(dx4, x3, g_f1, h3, gu1, act1, 1, 0)
    dyd, d_scale, gw_pg = _pool_out_bwd(dx3, yd, W["pg"], pool_scale)
    du = _trail(dyd, backward=True, name="trail_bwd")
    gw_pi = _mm(h2, du, mode="tn", M=D, N=D, K=S, tm=D, tn=D, tk=2048, out_dtype=BF16, name="gw_pi")
    token = comm.send_grads(1, {"pg": gw_pg, "wpi": gw_pi})
    dx2, d_pool = _mm_rms_bwd(du, W["wpi"], x2, pool_norm, dx3, mode="nt", M=S, K=D, tm=1024, deps=(token,),
                              name="pool_in_bwd")
    dx1, d_f0 = ffn_bwd(dx2, x1, g_f0, h1, gu0, act0, 0, 2)

    gw_o = _mm(o, dx1, mode="tn", M=D, N=D, K=S, tm=D, tn=D, tk=2048, out_dtype=BF16, name="gw_o")
    do = _mm(dx1, W["wo"], mode="nt", M=S, N=D, K=D, tm=1024, tn=D, tk=D, out_dtype=F32, name="attn_out_bwd")
    do_f, lse_ff, delta_f = _attn_bwd_prep(do, o, lse)
    dqkv_f, gw_qkv = [], None
    for g in range(NGROUPS):
        dqkv_f.append(_attn_bwd(qkv_f[g], do_f[g].reshape(S, D), lse_ff[g].reshape(S, HD),
                                delta_f[g].reshape(S, HD), bias[g], nbs[g], f"attn_bwd{g}"))
        gw_qkv = _mm(dqkv_f[g], hf[g], mode="tn", M=3 * D, N=D, K=S, tm=1024, tn=D, tk=2048, out_dtype=BF16,
                     out_rows=NGROUPS * 3 * D, out_off=3 * g, out_prev=gw_qkv, name=f"gw_qkv{g}")
    token = comm.send_grads_pairwise({"wo": gw_o, "qkv": gw_qkv})
    dh0_f = [None] * NGROUPS
    for g in reversed(range(NGROUPS)):
        dh0_f[g] = _mm(dqkv_f[g], W["qkv"], mode="nn", M=S, N=D, K=3 * D, tm=1024, tn=D, tk=3 * D, out_dtype=F32,
                       b_off=(g, 0), deps=(token,), name=f"qkv_proj_bwd{g}")
        if g == NGROUPS - 1:
            token = comm.pass_grads(dh0_f[g])
    folded = [dh0_f[g].reshape(dil, S // dil, D) for g, dil in enumerate(DILS) if dil > 1]
    grad_x, d_attn = _rms_bwd(dh0_f[0], x, g_attn, dx1, "rms_attn_bwd", folded=folded)

    vec = jnp.concatenate([d_attn, d_f0, d_f1, d_fin, d_pool, d_scale, lossvec, jnp.zeros((1, D), F32)], axis=0)
    return grad_x, vec


def _mesh_pos():
    x, y, c = lax.axis_index("x"), lax.axis_index("y"), lax.axis_index("c")
    return x, y, c, 4 * x + 2 * y + c


def _peer(x, y, c, k):
    kx, ky, kc = (k >> 2) & 1, (k >> 1) & 1, k & 1
    px = 1 - x if kx else x
    py = 1 - y if ky else y
    pc = 1 - c if kc else c
    return (px, py, pc), 4 * px + 2 * py + pc


ANY = pl.BlockSpec(memory_space=pl.ANY)


HBM = pl.BlockSpec(memory_space=pltpu.HBM)
SEMS = pl.BlockSpec(memory_space=pltpu.SEMAPHORE)
EFFECT = pltpu.SideEffectType.DATAFLOW_SIDE_EFFECTING
NPEER = NDEV - 1

AG_GROUPS = (("qkv",), ("wo", "gu0", "d0", "pv"), ("wpi", "pg", "gu1", "d1"))
AG_ORDER = tuple(n for grp in AG_GROUPS for n in grp)
RS_GROUPS = (("d1", "gu1"), ("pg", "wpi"), ("d0", "gu0"), ("wo", "qkv"))


def _hbm(a):
    return pltpu.with_memory_space_constraint(a, pltpu.HBM)


def _remote(src, dst, send, recv, peer):
    return pltpu.make_async_remote_copy(src_ref=src, dst_ref=dst, send_sem=send, recv_sem=recv, device_id=peer,
                                        device_id_type=pl.DeviceIdType.MESH)


def _bcast_all(v, name, deps=()):
    W = v.shape[1]
    nd = len(deps)

    def kern(v_ref, *rest):
        o_ref, send, recv, lsem = rest[nd:]
        x, y, c, me = _mesh_pos()
        own = pltpu.make_async_copy(v_ref, o_ref.at[me], lsem)
        own.start()
        cps = [_remote(v_ref, o_ref.at[me], send.at[k - 1], recv.at[k - 1], _peer(x, y, c, k)[0])
               for k in range(1, NDEV)]
        for cp in cps:
            cp.start()
        for cp in cps:
            cp.wait_recv()
            cp.wait_send()
        own.wait()

    return pl.pallas_call(
        kern, in_specs=[ANY] * (1 + nd), out_specs=ANY, out_shape=jax.ShapeDtypeStruct((NDEV, 8, W), F32),
        scratch_shapes=[pltpu.SemaphoreType.DMA((NPEER,)), pltpu.SemaphoreType.DMA((NPEER,)),
                        pltpu.SemaphoreType.DMA(())],
        name=name)(v, *deps)


ALL_KS = tuple(range(1, NDEV))
AG_KS1 = (1, 2, 4, 6)
AG_KS2 = (2, 4, 6)
RS_KS_PAIR = (1, 3, 5, 7)
RS_KS_CHIPS = (2, 4, 6)


def _split_start(srcs, src_of, lands, copy_refs, name, deps=(), ks=ALL_KS, to=None):
    ns, n, nd, nk = len(srcs), len(lands), len(deps), len(ks)

    def body(*refs):
        ins, land = refs[:ns], refs[ns:ns + n]
        send, recv = refs[ns + n + nd], refs[ns + n + nd + 1]
        token = refs[-1]
        x, y, c, me = _mesh_pos()
        for j in range(n):
            for i, k in enumerate(ks):
                _, pid = _peer(x, y, c, k)
                dest, _ = _peer(x, y, c, k if to is None else to)
                src, dst = copy_refs(j, (land[j] if src_of[j] is None else ins[src_of[j]]), land[j], me, pid, i)
                _remote(src, dst, send.at[j * nk + i], recv.at[j * nk + i], dest).start()
        token[...] = jnp.zeros_like(token)

    outs = pl.pallas_call(
        body, name=name,
        out_shape=(pltpu.SemaphoreType.DMA((n * nk,)), pltpu.SemaphoreType.DMA((n * nk,)))
        + tuple(pltpu.HBM(a.shape, a.dtype) for a in srcs) + tuple(pltpu.HBM(a.shape, a.dtype) for a in lands)
        + (jax.ShapeDtypeStruct((8, 128), F32),),
        in_specs=(HBM,) * (ns + n) + (ANY,) * nd,
        out_specs=(SEMS, SEMS) + (HBM,) * (ns + n) + (pl.BlockSpec(memory_space=pltpu.VMEM),),
        input_output_aliases={i: 2 + i for i in range(ns + n)},
        compiler_params=pltpu.CompilerParams(has_side_effects=EFFECT),
    )(*[_hbm(a) for a in srcs], *[_hbm(a) for a in lands], *deps)
    return outs[0], outs[1], list(outs[2:2 + ns]), list(outs[2 + ns:2 + ns + n]), outs[-1]


def _split_wait(srcs, src_of, lands, send, recv, sem_rows, wait_refs, after, name, ks=ALL_KS):
    ns, n, nk = len(srcs), len(lands), len(ks)
    after = tuple(after) if isinstance(after, (tuple, list)) else (after,)

    def body(*refs):
        ins, land = refs[:ns], refs[ns:ns + n]
        send_ref, recv_ref = refs[ns + n], refs[ns + n + 1]
        x, y, c, me = _mesh_pos()
        for j in range(n):
            for i, k in enumerate(ks):
                peer, _ = _peer(x, y, c, k)
                src, dst = wait_refs(j, (land[j] if src_of[j] is None else ins[src_of[j]]), land[j])
                sem = sem_rows[j] * nk + i
                cp = _remote(src, dst, send_ref.at[sem], recv_ref.at[sem], peer)
                cp.wait_send()
                cp.wait_recv()

    outs = pl.pallas_call(
        body, name=name,
        out_shape=tuple(pltpu.HBM(a.shape, a.dtype) for a in srcs) + tuple(pltpu.HBM(a.shape, a.dtype) for a in lands),
        in_specs=(HBM,) * (ns + n) + (SEMS, SEMS) + (ANY,) * len(after),
        out_specs=(HBM,) * (ns + n),
        input_output_aliases={i: i for i in range(ns + n)},
        compiler_params=pltpu.CompilerParams(has_side_effects=EFFECT),
    )(*srcs, *lands, send, recv, *after)
    return list(outs[:ns]), list(outs[ns:])


def _ag_dtype(name):
    return F32 if name == "pv" else BF16


def _ag_align(name):
    return 8 if name == "pv" else 16


def _place_transposed(w, me, name):
    rows = w.shape[1]
    nblk = rows // 128

    def kern(me_ref, w_ref, o_ref):
        o_ref[...] = w_ref[...].T.astype(BF16)

    grid_spec = pltpu.PrefetchScalarGridSpec(
        num_scalar_prefetch=1, grid=(nblk,),
        in_specs=[pl.BlockSpec((D, 128), lambda i, me_ref: (0, i))],
        out_specs=pl.BlockSpec((128, D), lambda i, me_ref: (me_ref[0] * nblk + i, 0)))
    return pl.pallas_call(
        kern, grid_spec=grid_spec, out_shape=jax.ShapeDtypeStruct((NDEV * rows, D), BF16),
        compiler_params=_cparams(1), name=name)(me.reshape(1).astype(jnp.int32), w)


class _Comm:
    def __init__(self, params, make_shards, me, placed):
        self.me = me
        self.ag_land, self.ag_sems, self.ag_tokens, self.ag_passing = {}, {}, (), {}
        self.rs = []
        deps = ()
        for part, names in enumerate((AG_GROUPS[0], AG_ORDER[len(AG_GROUPS[0]):])):
            rows = [SEC_ROWS[n] for n in names]
            if part == 0:
                lands = [placed[n] for n in names]
            else:
                params, deps = lax.optimization_barrier((params, deps))
                shards = make_shards(*params)
                lands = [lax.dynamic_update_slice(lax.empty((NDEV * r, shards[n].shape[1]), _ag_dtype(n)),
                                                  shards[n].astype(_ag_dtype(n)), (_shard_pos(n, me), 0))
                         for n, r in zip(names, rows)]

            def copy_refs(j, src, land, me, pid, i, names=names, rows=rows):
                own = land.at[pl.ds(pl.multiple_of(_shard_pos(names[j], me), _ag_align(names[j])), rows[j])]
                return own, own

            send, recv, _, lands, token = _split_start([], [None] * len(names), lands, copy_refs, f"ag_start{part}",
                                                       deps=deps, ks=AG_KS1)
            deps = (token,)
            self.ag_tokens += (token,)
            for j, n in enumerate(names):
                self.ag_land[n] = lands[j]
                self.ag_sems[n] = (send, recv, j)

    def pass_on(self, group, after):
        names = AG_GROUPS[group]
        send, recv = self.ag_sems[names[0]][:2]
        idx = [self.ag_sems[n][2] for n in names]
        rows = [SEC_ROWS[n] for n in names]
        none = [None] * len(names)

        def wait_refs(j, src, land):
            return land.at[pl.ds(0, rows[j])], land.at[pl.ds(0, rows[j])]

        _, lands = _split_wait([], none, [self.ag_land[n] for n in names], send, recv, idx,
                               wait_refs, after, f"ag_wait{group}", ks=AG_KS1)

        def copy_refs(j, src, land, me, pid, i):
            theirs = land.at[pl.ds(pl.multiple_of(_shard_pos(names[j], pid), _ag_align(names[j])), rows[j])]
            return theirs, theirs

        send, recv, _, lands, token = _split_start([], none, lands, copy_refs, f"ag_pass{group}", ks=AG_KS2, to=1)
        self.ag_passing[group] = (send, recv, lands, wait_refs)
        return token

    def weights(self, group, after):
        names = AG_GROUPS[group]
        if group not in self.ag_passing:
            after = self.pass_on(group, after)
        send, recv, lands, wait_refs = self.ag_passing[group]
        _, lands = _split_wait([], [None] * len(names), lands, send, recv, list(range(len(names))), wait_refs, after,
                               f"ag_pass_wait{group}", ks=AG_KS2)
        return dict(zip(names, lands))

    def send_grads(self, group, gws):
        names = RS_GROUPS[group]
        rows = [SEC_ROWS[n] for n in names]
        grads = [gws[n] for n in names]
        me = self.me
        lands = [lax.dynamic_update_slice(
            lax.empty((NDEV, r, D), BF16),
            lax.dynamic_slice(g, (_shard_pos(n, me), 0), (r, D))[None], (me, 0, 0))
            for n, r, g in zip(names, rows, grads)]

        def copy_refs(j, src, land, me, pid, i):
            return src.at[pl.ds(pl.multiple_of(_shard_pos(names[j], pid), 16), rows[j])], land.at[me]

        send, recv, srcs, lands, token = _split_start(grads, list(range(len(names))), lands, copy_refs,
                                                      f"rs_start{group}")
        self.rs.append((names, rows, send, recv, srcs, lands, ALL_KS))
        return token

    def send_grads_pairwise(self, gws):
        names = RS_GROUPS[-1]
        rows = [SEC_ROWS[n] for n in names]
        grads = [gws[n] for n in names]
        idx = list(range(len(names)))
        lands = [lax.empty((len(RS_KS_PAIR), r, D), BF16) for r in rows]

        def copy_refs(j, src, land, me, pid, i):
            return src.at[pl.ds(pl.multiple_of(_shard_pos(names[j], pid), 16), rows[j])], land.at[i]

        send, recv, srcs, lands, token = _split_start(grads, idx, lands, copy_refs, "rs_pair_start",
                                                      ks=RS_KS_PAIR, to=1)
        self.pair = (names, rows, send, recv, srcs, lands)
        return token

    def pass_grads(self, after):
        names, rows, send, recv, srcs, lands = self.pair
        idx = list(range(len(names)))
        me = self.me

        def wait_refs(j, src, land):
            return src.at[pl.ds(0, rows[j])], land.at[0]

        srcs, lands = _split_wait(srcs, idx, lands, send, recv, idx, wait_refs, after, "rs_pair_wait", ks=RS_KS_PAIR)
        sums = []
        for n, r, g, got in zip(names, rows, srcs, lands):
            mine = jnp.stack([lax.dynamic_slice(g, (_shard_pos(n, jnp.bitwise_xor(me, k)), 0), (r, D))
                              for k in (0,) + RS_KS_CHIPS])
            sums.append(_pair_sum(mine, got, f"rs_pair_sum_{n}"))
        lands = [lax.dynamic_update_slice(lax.empty(p.shape, BF16), p[0:1], (0, 0, 0)) for p in sums]

        def copy_refs(j, src, land, me, pid, i):
            return src.at[i + 1], land.at[i + 1]

        send, recv, sums, lands, token = _split_start(sums, idx, lands, copy_refs, f"rs_start{len(RS_GROUPS) - 1}",
                                                      ks=RS_KS_CHIPS)
        self.rs.append((names, rows, send, recv, sums, lands, RS_KS_CHIPS))
        return token

    def received(self, group, after):
        names, rows, send, recv, srcs, lands, ks = self.rs[group]
        whole = srcs[0].ndim == 2

        def wait_refs(j, src, land):
            return (src.at[pl.ds(0, rows[j])] if whole else src.at[0]), land.at[0]

        _, lands = _split_wait(srcs, list(range(len(names))), lands, send, recv, list(range(len(names))), wait_refs,
                               after, f"rs_wait{group}", ks=ks)
        return dict(zip(names, lands))


def _pair_sum(a, b, name):
    n, rows, _ = a.shape
    tr = 384 if rows % 384 == 0 else rows

    def kern(a_ref, b_ref, o_ref):
        o_ref[...] = (a_ref[...].astype(F32) + b_ref[...].astype(F32)).astype(BF16)

    blk = pl.BlockSpec((1, tr, D), lambda i, t: (i, t, 0))
    return pl.pallas_call(
        kern, grid=(n, rows // tr), in_specs=[blk, blk], out_specs=blk,
        out_shape=jax.ShapeDtypeStruct(a.shape, BF16), compiler_params=_cparams(2), name=name)(a, b)


def _sum_contributions(r_ref):
    g = r_ref[0].astype(F32)
    for slot in range(1, r_ref.shape[0]):
        g = g + r_ref[slot].astype(F32)
    return g


def _adam_math(g, w, m, v):
    c1 = 1.0 / (1.0 - ADAM_B1 ** ADAM_STEP)
    c2 = 1.0 / (1.0 - ADAM_B2 ** ADAM_STEP)
    mn = ADAM_B1 * m + (1.0 - ADAM_B1) * g
    vn = ADAM_B2 * v + (1.0 - ADAM_B2) * (g * g)
    return -ADAM_LR * ((mn * c1) / (jnp.sqrt(vn * c2) + ADAM_EPS) + ADAM_WD * w), mn, vn


def _adamw(R, w, m, v, *, tr, name, layer=None, prev=None):
    rows, C = w.shape[-2:]
    nprev = 0 if prev is None else 4

    def kern(r_ref, w_ref, m_ref, v_ref, *rest):
        g_out, d_out, m_out, v_out = rest[nprev:]
        g = _sum_contributions(r_ref)
        g_out[...] = g
        d_out[...], m_out[...], v_out[...] = _adam_math(g, w_ref[...], m_ref[...], v_ref[...])

    if layer is None:
        tile = pl.BlockSpec((tr, C), lambda i: (i, 0))
    else:
        tile = pl.BlockSpec((None, tr, C), lambda i: (layer, i, 0))
    shp = jax.ShapeDtypeStruct(w.shape, F32)
    return pl.pallas_call(
        kern, grid=(rows // tr,),
        in_specs=[pl.BlockSpec((R.shape[0], tr, C), lambda i: (0, i, 0)), tile, tile, tile]
        + [pl.BlockSpec(memory_space=pl.ANY)] * nprev,
        out_specs=[tile] * 4, out_shape=[shp] * 4,
        input_output_aliases={4 + k: k for k in range(nprev)},
        compiler_params=_cparams(1), name=name)(R, w, m, v, *(prev or ()))


def _adamw_pool_group(R, w, m, v):
    rows = SEC_ROWS["pg"]

    def kern(r_ref, w_ref, m_ref, v_ref, g_out, d_out, m_out, v_out):
        g = _sum_contributions(r_ref)
        g_out[0] = g
        d_out[0], m_out[0], v_out[0] = _adam_math(g, w_ref[0], m_ref[0], v_ref[0])

    blk = pl.BlockSpec((1, rows, PGD), lambda i: (i, 0, 0))
    shp = jax.ShapeDtypeStruct((POOL_G, rows, PGD), F32)
    return pl.pallas_call(
        kern, grid=(POOL_G,),
        in_specs=[pl.BlockSpec((NDEV, rows, PGD), lambda i: (0, 0, i)), blk, blk, blk],
        out_specs=[blk] * 4, out_shape=[shp] * 4, compiler_params=_cparams(1), name="adamw_pg")(R, w, m, v)


def _grad_sum_t(R, name):
    rows = R.shape[1]
    tr = 128

    def kern(r_ref, o_ref):
        o_ref[...] = _sum_contributions(r_ref).T

    return pl.pallas_call(
        kern, grid=(rows // tr,), in_specs=[pl.BlockSpec((R.shape[0], tr, D), lambda i: (0, i, 0))],
        out_specs=pl.BlockSpec((D, tr), lambda i: (0, i)), out_shape=jax.ShapeDtypeStruct((D, rows), F32),
        compiler_params=_cparams(1), name=name)(R)


def _adam_plain(g, w, m, v, *, tr, name):
    rows, C = w.shape

    def kern(g_ref, w_ref, m_ref, v_ref, d_out, m_out, v_out):
        d_out[...], m_out[...], v_out[...] = _adam_math(g_ref[...], w_ref[...], m_ref[...], v_ref[...])

    tile = pl.BlockSpec((tr, C), lambda i: (i, 0))
    shp = jax.ShapeDtypeStruct((rows, C), F32)
    return pl.pallas_call(
        kern, grid=(rows // tr,), in_specs=[tile] * 4, out_specs=[tile] * 3, out_shape=[shp] * 3,
        compiler_params=_cparams(1), name=name)(g, w, m, v)


def _pack_sections(w_qkv, w_attn_out, w_pool_in, w_pool_group, w_ffn_gate_up, w_ffn_down):
    pg = w_pool_group[0].transpose(1, 0, 2).reshape(SEC_ROWS["pg"], D)
    return {"qkv": w_qkv[0].T, "wo": w_attn_out[0], "wpi": w_pool_in[0], "gu0": w_ffn_gate_up[0].T,
            "gu1": w_ffn_gate_up[1].T, "d0": w_ffn_down[0], "d1": w_ffn_down[1], "pg": pg}


def _vec_pack(attn_norm, ffn_norm, final_norm, pool_norm_sh, pool_scale_sh, me):
    def place(sh):
        return lax.dynamic_update_slice(jnp.zeros((1, D), F32), sh, (0, me * 128))
    return jnp.concatenate([attn_norm, ffn_norm, final_norm.reshape(1, D), place(pool_norm_sh),
                            place(pool_scale_sh), jnp.zeros((2, D), F32)], axis=0)


def _vec_unpack(p, me):
    def take(r):
        return lax.dynamic_slice(p[r:r + 1], (0, me * 128), (1, 128))
    return p[0:1], p[1:3], p[3], take(4), take(5)


def kernel(x, attn_norm, w_qkv, w_attn_out, pool_norm, w_pool_in, w_pool_group, pool_scale, ffn_norm, w_ffn_gate_up, w_ffn_down, final_norm, loss_target, m_attn_norm, m_w_qkv, m_w_attn_out, m_pool_norm, m_w_pool_in, m_w_pool_group, m_pool_scale, m_ffn_norm, m_w_ffn_gate_up, m_w_ffn_down, m_final_norm, v_attn_norm, v_w_qkv, v_w_attn_out, v_pool_norm, v_w_pool_in, v_w_pool_group, v_pool_scale, v_ffn_norm, v_w_ffn_gate_up, v_w_ffn_down, v_final_norm):
    me = 4 * lax.axis_index("x") + 2 * lax.axis_index("y") + lax.axis_index("c")

    def make_shards(wq, wo, wpi, wpg, wgu, wd, pn, ps):
        shards = _pack_sections(wq, wo, wpi, wpg, wgu, wd)
        shards["pv"] = jnp.concatenate([pn, ps, jnp.zeros((6, 128), F32)], axis=0)
        return shards

    comm = _Comm((w_qkv, w_attn_out, w_pool_in, w_pool_group, w_ffn_gate_up, w_ffn_down, pool_norm, pool_scale),
                 make_shards, me, placed={"qkv": _place_transposed(w_qkv[0], me, "place_qkv")})

    grad_x, vec = _local_step(x[0], loss_target[0], comm, attn_norm, ffn_norm, final_norm)

    small = ((attn_norm, ffn_norm, final_norm, pool_norm, pool_scale),
             (m_attn_norm, m_ffn_norm, m_final_norm, m_pool_norm, m_pool_scale),
             (v_attn_norm, v_ffn_norm, v_final_norm, v_pool_norm, v_pool_scale))
    small, grad_x = lax.optimization_barrier((small, grad_x))
    vw, vm, vv = (_vec_pack(*s, me) for s in small)

    gu_t = [jnp.swapaxes(a, 1, 2) for a in (w_ffn_gate_up, m_w_ffn_gate_up, v_w_ffn_gate_up)]
    res = {}
    gu_res, d_res = None, None
    vec_out = None
    after = grad_x
    for group in range(len(RS_GROUPS)):
        if group == len(RS_GROUPS) - 1:
            VR = _bcast_all(vec, "exchange_vector_grads", deps=(after,))
            vec_out = _adamw(VR, vw, vm, vv, tr=8, name="adamw_vec")
            after = vec_out[0]
        for n, R in comm.received(group, after).items():
            if n in ("d0", "d1"):
                d_res = _adamw(R, w_ffn_down, m_w_ffn_down, v_w_ffn_down, tr=352, name=f"adamw_{n}",
                               layer=int(n[1]), prev=d_res)
                after = d_res[0]
            elif n in ("gu0", "gu1"):
                gu_res = _adamw(R, *gu_t, tr=352, name=f"adamw_{n}", layer=int(n[2]), prev=gu_res)
                after = gu_res[0]
            elif n == "pg":
                out = _adamw_pool_group(R, w_pool_group[0], m_w_pool_group[0], v_w_pool_group[0])
                res["pg"] = tuple(a[None] for a in out)
                after = out[0]
            elif n in ("wo", "wpi"):
                w, m, v = ((w_attn_out, m_w_attn_out, v_w_attn_out) if n == "wo"
                           else (w_pool_in, m_w_pool_in, v_w_pool_in))
                res[n] = _adamw(R, w[0], m[0], v[0], tr=128, name=f"adamw_{n}")
                res[n] = tuple(a[None] for a in res[n])
                after = res[n][0]
            else:
                g = _grad_sum_t(R, "grad_sum_qkv")
                out = _adam_plain(g, w_qkv[0], m_w_qkv[0], v_w_qkv[0], tr=256, name="adamw_qkv")
                res["qkv"] = tuple(a[None] for a in (g,) + tuple(out))
                after = out[0]
    res["gu"] = tuple(jnp.swapaxes(a, 1, 2) for a in gu_res)
    res["d"] = tuple(d_res)

    outs = []
    for kind in range(4):
        an, fn, fin, pn, ps = _vec_unpack(vec_out[kind], me)
        outs.append((an, res["qkv"][kind], res["wo"][kind], pn, res["wpi"][kind], res["pg"][kind], ps, fn,
                     res["gu"][kind], res["d"][kind], fin))
    loss = 0.5 * jnp.sum(vec_out[0][6]) / D
    return (loss, grad_x[None]) + outs[0] + outs[1] + outs[2] + outs[3]
```

```python
import jax
import jax.numpy as jnp
from jax import lax
from jax.experimental import pallas as pl
from jax.experimental.pallas import tpu as pltpu

F32 = jnp.float32
BF16 = jnp.bfloat16

D = 1024
NDEV = 8
HEADS = 8
HD = 128
QB = 128
NGROUPS = 3
DILS = (1, 4, 16)
DFF = 2816
HCH = 1408
POOL_G = 4
PGD = 256
RMS_EPS = 1e-6
NEG = -1e30

ADAM_LR = 0.001
ADAM_B1 = 0.9
ADAM_B2 = 0.999
ADAM_EPS = 1e-08
ADAM_WD = 0.01
ADAM_STEP = 10

VMEM_LIMIT = 52 * 1024 * 1024

SECTIONS = (("qkv", 1152), ("wo", 128), ("wpi", 128), ("gu0", 704), ("gu1", 704),
            ("d0", 352), ("d1", 352), ("pg", 32))
LOC_OFF = {}
GLB_OFF = {}
_o = 0
for _n, _r in SECTIONS:
    LOC_OFF[_n] = _o
    GLB_OFF[_n] = _o * NDEV
    _o += _r
PACK_ROWS = _o
GLB_ROWS = PACK_ROWS * NDEV
SEC_ROWS = dict(SECTIONS)
SEC_ROWS["pv"] = 8


def _cparams(n_grid):
    return pltpu.CompilerParams(dimension_semantics=("arbitrary",) * n_grid, vmem_limit_bytes=VMEM_LIMIT)


def _shard_pos(name, dev):
    n = SEC_ROWS[name]
    if name in ("gu0", "gu1"):
        return ((dev % 4) // 2) * (2 * HCH) + (dev // 4) * HCH + (dev % 2) * n
    return dev * n


def _mm(a, b, *, mode, M, N, K, tm, tn, tk, out_dtype, name, a_off=(0, 0), b_off=(0, 0), res=None,
        out_rows=None, out_off=0, out_prev=None, deps=()):
    nm, nn, nk = M // tm, N // tn, K // tk
    assert nm * tm == M and nn * tn == N and nk * tk == K
    if mode == "nn":
        a_bs, b_bs = (tm, tk), (tk, tn)
        a_ix = lambda i, j, k: (i, k)
        b_ix = lambda i, j, k: (k, j)
        dims = (((1,), (0,)), ((), ()))
    elif mode == "nt":
        a_bs, b_bs = (tm, tk), (tn, tk)
        a_ix = lambda i, j, k: (i, k)
        b_ix = lambda i, j, k: (j, k)
        dims = (((1,), (1,)), ((), ()))
    else:
        a_bs, b_bs = (tk, tm), (tk, tn)
        a_ix = lambda i, j, k: (k, i)
        b_ix = lambda i, j, k: (k, j)
        dims = (((0,), (0,)), ((), ()))

    def spec(bs, ix, off):
        def im(i, j, k):
            r, c = ix(i, j, k)
            return (r + off[0], c + off[1])
        return pl.BlockSpec(bs, im)

    in_specs = [spec(a_bs, a_ix, a_off), spec(b_bs, b_ix, b_off)]
    args = [a, b]
    if res is not None:
        in_specs.append(pl.BlockSpec((tm, tn), lambda i, j, k: (i, j)))
        args.append(res)
    out_shape = jax.ShapeDtypeStruct((M if out_rows is None else out_rows, N), out_dtype)
    out_spec = pl.BlockSpec((tm, tn), lambda i, j, k: (i + out_off, j))
    has_res = res is not None
    extra = list(deps) + ([out_prev] if out_prev is not None else [])
    for dep in extra:
        in_specs.append(pl.BlockSpec(memory_space=pl.ANY))
        args.append(dep)
    o_pos = 2 + int(has_res) + len(extra)
    aliases = {len(args) - 1: 0} if out_prev is not None else {}

    def kern(*refs):
        a_ref, b_ref = refs[0], refs[1]
        res_ref = refs[2] if has_res else None
        o_ref = refs[o_pos]
        av = a_ref[...]
        bv = b_ref[...]
        if av.dtype != BF16:
            av = av.astype(BF16)
        if bv.dtype != BF16:
            bv = bv.astype(BF16)
        part = lax.dot_general(av, bv, dims, preferred_element_type=F32)

        def write(val):
            if has_res:
                val = val + res_ref[...]
            o_ref[...] = val.astype(out_dtype)

        if nk == 1:
            write(part)
        else:
            acc_ref = refs[-1]
            k = pl.program_id(2)

            @pl.when(k == 0)
            def _():
                acc_ref[...] = part

            @pl.when(k > 0)
            def _():
                acc_ref[...] += part

            @pl.when(k == nk - 1)
            def _():
                write(acc_ref[...])

    scratch = [pltpu.VMEM((tm, tn), F32)] if nk > 1 else []
    return pl.pallas_call(
        kern, grid=(nm, nn, nk), in_specs=in_specs, out_specs=out_spec, out_shape=out_shape,
        scratch_shapes=scratch, input_output_aliases=aliases, compiler_params=_cparams(3), name=name)(*args)


def _mm_rms_bwd(a, b, x, g, dres, *, mode, M, K, tm, name, b_off=(0, 0), deps=()):
    nd = len(deps)
    b_bs = (K, D) if mode == "nn" else (D, K)
    dims = (((1,), (0,)), ((), ())) if mode == "nn" else (((1,), (1,)), ((), ()))

    def kern(a_ref, b_ref, x_ref, g_ref, dres_ref, *rest):
        dx_ref, dg_ref = rest[nd:]
        i = pl.program_id(0)
        av = a_ref[...]
        if av.dtype != BF16:
            av = av.astype(BF16)
        dhv = lax.dot_general(av, b_ref[...], dims, preferred_element_type=F32)
        xv = x_ref[...]
        r = lax.rsqrt(jnp.mean(xv * xv, axis=-1, keepdims=True) + RMS_EPS)
        xhat = xv * r
        gy = dhv * g_ref[...]
        dx_ref[...] = dres_ref[...] + r * (gy - xhat * jnp.mean(gy * xhat, axis=-1, keepdims=True))
        part = jnp.sum(dhv * xhat, axis=0, keepdims=True)

        @pl.when(i == 0)
        def _():
            dg_ref[...] = part

        @pl.when(i > 0)
        def _():
            dg_ref[...] += part

    row = pl.BlockSpec((tm, D), lambda i: (i, 0))
    vec = pl.BlockSpec((1, D), lambda i: (0, 0))
    return pl.pallas_call(
        kern, grid=(M // tm,),
        in_specs=[pl.BlockSpec((tm, K), lambda i: (i, 0)),
                  pl.BlockSpec(b_bs, lambda i: b_off, pipeline_mode=pl.Buffered(1)), row, vec, row]
        + [pl.BlockSpec(memory_space=pl.ANY)] * nd,
        out_specs=[row, vec],
        out_shape=[jax.ShapeDtypeStruct((M, D), F32), jax.ShapeDtypeStruct((1, D), F32)],
        compiler_params=_cparams(1), name=name)(a, b, x, g, dres, *deps)


def _mm_res_norm(a, b, res, g, *, K, tm, name, b_off=(0, 0), tgt=None):
    M = a.shape[0]
    head = tgt is not None

    def kern(a_ref, b_ref, res_ref, g_ref, *rest):
        xv = res_ref[...] + jnp.dot(a_ref[...], b_ref[...], preferred_element_type=F32)
        gv = g_ref[...]
        r = lax.rsqrt(jnp.mean(xv * xv, axis=-1, keepdims=True) + RMS_EPS)
        xhat = xv * r
        if not head:
            xo_ref, h_ref = rest
            xo_ref[...] = xv
            h_ref[...] = (xhat * gv).astype(BF16)
            return
        t_ref, dx_ref, dg_ref, ls_ref = rest
        i = pl.program_id(0)
        e = xhat * gv - t_ref[...]
        dy = e * (1.0 / D)
        gy = dy * gv
        dx_ref[...] = r * (gy - xhat * jnp.mean(gy * xhat, axis=-1, keepdims=True))
        dgp = jnp.sum(dy * xhat, axis=0, keepdims=True)
        lsp = jnp.sum(e * e, axis=0, keepdims=True)

        @pl.when(i == 0)
        def _():
            dg_ref[...] = dgp
            ls_ref[...] = lsp

        @pl.when(i > 0)
        def _():
            dg_ref[...] += dgp
            ls_ref[...] += lsp

    row = pl.BlockSpec((tm, D), lambda i: (i, 0))
    vec = pl.BlockSpec((1, D), lambda i: (0, 0))
    in_specs = [pl.BlockSpec((tm, K), lambda i: (i, 0)),
                pl.BlockSpec((K, D), lambda i: b_off, pipeline_mode=pl.Buffered(1)), row, vec]
    if head:
        return pl.pallas_call(
            kern, grid=(M // tm,), in_specs=in_specs + [row], out_specs=[row, vec, vec],
            out_shape=[jax.ShapeDtypeStruct((M, D), F32), jax.ShapeDtypeStruct((1, D), F32),
                       jax.ShapeDtypeStruct((1, D), F32)],
            compiler_params=_cparams(1), name=name)(a, b, res, g, tgt)
    return pl.pallas_call(
        kern, grid=(M // tm,), in_specs=in_specs, out_specs=[row, row],
        out_shape=[jax.ShapeDtypeStruct((M, D), F32), jax.ShapeDtypeStruct((M, D), BF16)],
        compiler_params=_cparams(1), name=name)(a, b, res, g)


def _rms_fwd(x, g, name, deps=()):
    S = x.shape[0]
    tr = 512

    def kern(x_ref, g_ref, *rest):
        h_ref = rest[-1]
        xv = x_ref[...]
        r = lax.rsqrt(jnp.mean(xv * xv, axis=-1, keepdims=True) + RMS_EPS)
        h_ref[...] = (xv * r * g_ref[...]).astype(BF16)

    return pl.pallas_call(
        kern, grid=(S // tr,),
        in_specs=[pl.BlockSpec((tr, D), lambda i: (i, 0)), pl.BlockSpec((1, D), lambda i: (0, 0))]
        + [pl.BlockSpec(memory_space=pl.ANY)] * len(deps),
        out_specs=pl.BlockSpec((tr, D), lambda i: (i, 0)),
        out_shape=jax.ShapeDtypeStruct((S, D), BF16), compiler_params=_cparams(1), name=name)(x, g, *deps)


def _chunks_put(scr, val):
    for c in range(scr.shape[0]):
        scr[c] = val[:, c * 128:(c + 1) * 128]


def _chunks_get(scr):
    return jnp.concatenate([scr[c] for c in range(scr.shape[0])], axis=1)


def _chunks_rows(scr, r, n, dil):
    return jnp.concatenate([scr.at[c][pl.ds(r, n, stride=dil), :] for c in range(scr.shape[0])], axis=1)


def _chunks_add_rows(scr, val, r, n, dil, accumulate):
    for c in range(scr.shape[0]):
        rows = pl.ds(r, n, stride=dil)
        piece = val[:, c * 128:(c + 1) * 128]
        tile = scr.at[c]
        tile[rows, :] = tile[rows, :] + piece if accumulate else piece


def _rms_fwd_folded(x, g, name, deps=()):
    S = x.shape[0]
    tr = 512
    dils = DILS[1:]

    def kern(x_ref, g_ref, *rest):
        outs, scr = rest[len(deps):-1], rest[-1]
        xv = x_ref[...]
        r = lax.rsqrt(jnp.mean(xv * xv, axis=-1, keepdims=True) + RMS_EPS)
        h = (xv * r * g_ref[...]).astype(BF16)
        outs[0][...] = h
        _chunks_put(scr, h.astype(F32))
        for o_ref, dil in zip(outs[1:], dils):
            for res in range(dil):
                o_ref[res] = _chunks_rows(scr, res, tr // dil, dil).astype(BF16)

    return pl.pallas_call(
        kern, grid=(S // tr,),
        in_specs=[pl.BlockSpec((tr, D), lambda i: (i, 0)), pl.BlockSpec((1, D), lambda i: (0, 0))]
        + [pl.BlockSpec(memory_space=pl.ANY)] * len(deps),
        out_specs=[pl.BlockSpec((tr, D), lambda i: (i, 0))]
        + [pl.BlockSpec((dil, tr // dil, D), lambda i: (0, i, 0)) for dil in dils],
        out_shape=[jax.ShapeDtypeStruct((S, D), BF16)]
        + [jax.ShapeDtypeStruct((dil, S // dil, D), BF16) for dil in dils],
        scratch_shapes=[pltpu.VMEM((D // 128, tr, 128), F32)],
        compiler_params=_cparams(1), name=name)(x, g, *deps)


def _rms_bwd(dh, x, g, dres, name, folded=()):
    S = x.shape[0]
    tr = 512
    nf = len(folded)

    def kern(dh_ref, *rest):
        f_refs = rest[:nf]
        x_ref, g_ref, dres_ref, dx_ref, dg_ref = rest[nf:nf + 5]
        i = pl.program_id(0)
        xv = x_ref[...]
        if nf:
            acc_ref = rest[nf + 5]
            _chunks_put(acc_ref, dh_ref[...].astype(F32))
            for f_ref in f_refs:
                dil = f_ref.shape[0]
                for res in range(dil):
                    _chunks_add_rows(acc_ref, f_ref[res], res, tr // dil, dil, True)
            dhv = _chunks_get(acc_ref)
        else:
            dhv = dh_ref[...].astype(F32)
        r = lax.rsqrt(jnp.mean(xv * xv, axis=-1, keepdims=True) + RMS_EPS)
        xhat = xv * r
        gy = dhv * g_ref[...]
        dx_ref[...] = dres_ref[...] + r * (gy - xhat * jnp.mean(gy * xhat, axis=-1, keepdims=True))
        part = jnp.sum(dhv * xhat, axis=0, keepdims=True)

        @pl.when(i == 0)
        def _():
            dg_ref[...] = part

        @pl.when(i > 0)
        def _():
            dg_ref[...] += part

    row = pl.BlockSpec((tr, D), lambda i: (i, 0))
    vec = pl.BlockSpec((1, D), lambda i: (0, 0))
    fspecs = [pl.BlockSpec((f.shape[0], tr // f.shape[0], D), lambda i: (0, i, 0)) for f in folded]
    return pl.pallas_call(
        kern, grid=(S // tr,), in_specs=[row] + fspecs + [row, vec, row], out_specs=[row, vec],
        out_shape=[jax.ShapeDtypeStruct((S, D), F32), jax.ShapeDtypeStruct((1, D), F32)],
        scratch_shapes=[pltpu.VMEM((D // 128, tr, 128), F32)] if nf else [],
        compiler_params=_cparams(1), name=name)(dh, *folded, x, g, dres)


def _ffn_up(h, G, name):
    S = h.shape[0]
    tm = 512
    nj = DFF // HCH

    def kern(h_ref, w_ref, gu_ref, act_ref):
        gu = lax.dot_general(h_ref[...], w_ref[...], (((1,), (1,)), ((), ())), preferred_element_type=F32)
        gu_ref[...] = gu.astype(BF16)
        gate = gu[:, :HCH]
        up = gu[:, HCH:]
        act_ref[...] = (gate * jax.nn.sigmoid(gate) * up).astype(BF16)

    return pl.pallas_call(
        kern, grid=(nj, S // tm),
        in_specs=[pl.BlockSpec((tm, D), lambda j, i: (i, 0)),
                  pl.BlockSpec((2 * HCH, D), lambda j, i: (j, 0))],
        out_specs=[pl.BlockSpec((tm, 2 * HCH), lambda j, i: (i, j)),
                   pl.BlockSpec((tm, HCH), lambda j, i: (i, j))],
        out_shape=[jax.ShapeDtypeStruct((S, 2 * DFF), BF16), jax.ShapeDtypeStruct((S, DFF), BF16)],
        compiler_params=_cparams(2), name=name)(h, G)


def _ffn_down_bwd(dx, G, gu, name):
    S = dx.shape[0]
    tm = 512
    nj = DFF // HCH

    def kern(dx_ref, w_ref, gu_ref, o_ref):
        dxv = dx_ref[...].astype(BF16)
        for j in range(nj):
            c0 = 2 * HCH * j
            dact = lax.dot_general(dxv, w_ref[HCH * j:HCH * (j + 1), :], (((1,), (1,)), ((), ())),
                                   preferred_element_type=F32)
            gate = gu_ref[:, c0:c0 + HCH].astype(F32)
            up = gu_ref[:, c0 + HCH:c0 + 2 * HCH].astype(F32)
            sig = jax.nn.sigmoid(gate)
            silu = gate * sig
            o_ref[:, c0:c0 + HCH] = (dact * up * (sig * (1.0 + gate * (1.0 - sig)))).astype(BF16)
            o_ref[:, c0 + HCH:c0 + 2 * HCH] = (dact * silu).astype(BF16)

    row = pl.BlockSpec((tm, 2 * DFF), lambda i: (i, 0))
    return pl.pallas_call(
        kern, grid=(S // tm,),
        in_specs=[pl.BlockSpec((tm, D), lambda i: (i, 0)),
                  pl.BlockSpec((DFF, D), lambda i: (0, 0), pipeline_mode=pl.Buffered(1)), row],
        out_specs=row, out_shape=jax.ShapeDtypeStruct((S, 2 * DFF), BF16),
        compiler_params=_cparams(1), name=name)(dx, G, gu)


def _trail(u, *, backward, name):
    S = u.shape[0]

    def kern(u_ref, o_ref):
        g = pl.program_id(0)
        for grp in range(POOL_G):
            @pl.when(g == grp)
            def _(grp=grp):
                uv = u_ref[...].astype(F32)
                row = lax.broadcasted_iota(jnp.int32, uv.shape, 0)
                cnt = jnp.minimum(row + 1, 2 << grp).astype(F32)
                s = uv / cnt if backward else uv
                for k in (1, 2, 4, 8)[:grp + 1]:
                    if backward:
                        sh = jnp.where(row < S - k, pltpu.roll(s, S - k, 0), 0.0)
                    else:
                        sh = jnp.where(row >= k, pltpu.roll(s, k, 0), 0.0)
                    s = s + sh
                if backward:
                    o_ref[...] = (s - uv).astype(BF16)
                else:
                    o_ref[...] = (s / cnt - uv).astype(BF16)

    blk = pl.BlockSpec((S, PGD), lambda g: (0, g))
    return pl.pallas_call(
        kern, grid=(POOL_G,), in_specs=[blk], out_specs=blk,
        out_shape=jax.ShapeDtypeStruct((S, D), BF16), compiler_params=_cparams(1), name=name)(u)


def _pool_out(yd, G, scale, xres):
    S = yd.shape[0]
    tm = min(S, 4096)

    def kern(y_ref, w_ref, s_ref, x_ref, o_ref):
        z = jnp.dot(y_ref[...], w_ref[...], preferred_element_type=F32)
        o_ref[...] = x_ref[...] + z * s_ref[...]

    tile = pl.BlockSpec((tm, PGD), lambda i, g: (i, g))
    return pl.pallas_call(
        kern, grid=(S // tm, POOL_G),
        in_specs=[tile, pl.BlockSpec((PGD, PGD), lambda i, g: (0, g)),
                  pl.BlockSpec((1, PGD), lambda i, g: (0, g)), tile],
        out_specs=tile, out_shape=jax.ShapeDtypeStruct((S, D), F32),
        compiler_params=_cparams(2), name="pool_out")(yd, G, scale, xres)


def _pool_out_bwd(dz, yd, G, scale):
    S = yd.shape[0]
    tm = min(S, 4096)
    ni = S // tm

    def kern(dz_ref, y_ref, w_ref, s_ref, dy_ref, ds_ref, dw_ref, acc_ref):
        i = pl.program_id(1)
        dzv = dz_ref[...]
        yv = y_ref[...]
        wv = w_ref[...]
        zraw = jnp.dot(yv, wv, preferred_element_type=F32)
        dsp = jnp.sum(dzv * zraw, axis=0, keepdims=True)
        dzr = (dzv * s_ref[...]).astype(BF16)
        dy_ref[...] = lax.dot_general(dzr, wv, (((1,), (1,)), ((), ())), preferred_element_type=F32)
        dwp = lax.dot_general(yv, dzr, (((0,), (0,)), ((), ())), preferred_element_type=F32)

        @pl.when(i == 0)
        def _():
            ds_ref[...] = dsp
            acc_ref[...] = dwp

        @pl.when(i > 0)
        def _():
            ds_ref[...] += dsp
            acc_ref[...] += dwp

        @pl.when(i == ni - 1)
        def _():
            dw_ref[...] = acc_ref[...].astype(BF16)

    tile = pl.BlockSpec((tm, PGD), lambda g, i: (i, g))
    return pl.pallas_call(
        kern, grid=(POOL_G, ni),
        in_specs=[tile, tile, pl.BlockSpec((PGD, PGD), lambda g, i: (0, g)),
                  pl.BlockSpec((1, PGD), lambda g, i: (0, g))],
        out_specs=[tile, pl.BlockSpec((1, PGD), lambda g, i: (0, g)),
                   pl.BlockSpec((PGD, PGD), lambda g, i: (0, g))],
        out_shape=[jax.ShapeDtypeStruct((S, D), F32), jax.ShapeDtypeStruct((1, D), F32),
                   jax.ShapeDtypeStruct((PGD, D), BF16)],
        scratch_shapes=[pltpu.VMEM((PGD, PGD), F32)],
        compiler_params=_cparams(2), name="pool_out_bwd")(dz, yd, G, scale)


def _bias_table():
    qi = jnp.arange(QB)[:, None]
    ki = jnp.arange(2 * QB)[None, :]
    delta = QB + qi - ki
    inband = (delta >= 0) & (delta <= QB)
    n = NGROUPS * HEADS
    slopes = jnp.exp2(-8.0 * jnp.arange(1, n + 1, dtype=F32) / n).reshape(NGROUPS, HEADS)
    dil = jnp.asarray(DILS, F32)
    bias = -slopes[:, :, None, None] * (delta.astype(F32)[None, None] * dil[:, None, None, None])
    return jnp.where(inband[None, None], bias, NEG)


def _attn_fwd(qkv_f, bias, nb, name):
    S = qkv_f.shape[0]
    nblk = S // QB
    scale = HD ** -0.5

    def kern(q_ref, k2_ref, kp_ref, v2_ref, vp_ref, b_ref, o_ref, l_ref, s_scr, p_scr, r_scr):
        s_id = pl.program_id(0)
        col = lax.broadcasted_iota(jnp.int32, (QB, 2 * QB), 1)
        lane = lax.broadcasted_iota(jnp.int32, (QB, HD), 1)

        def keys(sub, cur2_ref, prev_ref, sl):
            if sub:
                return cur2_ref[:, sl]
            return jnp.concatenate([prev_ref[:, sl], cur2_ref[0:QB, sl]], axis=0)

        for sub in range(2):
            for h in range(HEADS):
                sl = slice(h * HD, (h + 1) * HD)
                s_scr[sub * HEADS + h] = lax.dot_general(
                    q_ref[sub * QB:(sub + 1) * QB, sl], keys(sub, k2_ref, kp_ref, sl), (((1,), (1,)), ((), ())),
                    preferred_element_type=F32)
        for sub in range(2):
            has_prev = jnp.bitwise_and(2 * s_id + sub, nb - 1) != 0
            dead = jnp.logical_and(col < QB, jnp.logical_not(has_prev))
            lse_all = jnp.zeros((QB, HD), F32)
            for h in range(HEADS):
                u = sub * HEADS + h
                s = s_scr[u] * scale + b_ref[h]
                s = jnp.where(dead, NEG, s)
                m = jnp.max(s, axis=-1, keepdims=True)
                p = jnp.exp(s - m)
                den = jnp.sum(p, axis=-1, keepdims=True)
                p_scr[u] = p.astype(BF16)
                r_scr[u] = jnp.broadcast_to(1.0 / den, (QB, HD))
                lse_all = jnp.where(lane == h, m + jnp.log(den), lse_all)
            l_ref[sub * QB:(sub + 1) * QB, :] = lse_all
        for sub in range(2):
            for h in range(HEADS):
                u = sub * HEADS + h
                sl = slice(h * HD, (h + 1) * HD)
                o = jnp.dot(p_scr[u], keys(sub, v2_ref, vp_ref, sl), preferred_element_type=F32) * r_scr[u]
                o_ref[sub * QB:(sub + 1) * QB, sl] = o.astype(BF16)

    def pair(colblk):
        return pl.BlockSpec((2 * QB, D), lambda s: (s, colblk))

    def prev(colblk):
        return pl.BlockSpec((QB, D), lambda s: (jnp.maximum(2 * s - 1, 0), colblk))

    return pl.pallas_call(
        kern, grid=(nblk // 2,),
        in_specs=[pair(0), pair(1), prev(1), pair(2), prev(2), pl.BlockSpec((HEADS, QB, 2 * QB), lambda s: (0, 0, 0))],
        out_specs=[pl.BlockSpec((2 * QB, D), lambda s: (s, 0)), pl.BlockSpec((2 * QB, HD), lambda s: (s, 0))],
        out_shape=[jax.ShapeDtypeStruct((S, D), BF16), jax.ShapeDtypeStruct((S, HD), F32)],
        scratch_shapes=[pltpu.VMEM((2 * HEADS, QB, 2 * QB), F32), pltpu.VMEM((2 * HEADS, QB, 2 * QB), BF16),
                        pltpu.VMEM((2 * HEADS, QB, HD), F32)],
        compiler_params=_cparams(1), name=name)(qkv_f, qkv_f, qkv_f, qkv_f, qkv_f, bias)


def _natural(ref, scr, tm):
    dil = ref.shape[0]
    for res in range(dil):
        _chunks_add_rows(scr, ref[res].astype(F32), res, tm // dil, dil, False)
    return _chunks_get(scr)


def _attn_merge(os, lses):
    S = os[0].shape[0]
    tm = 512

    def kern(o0, o1, o2, l0, l1, l2, om_ref, lm_ref, ls1, ls2, os1, os2):
        la = l0[...]
        lb = _natural(l1, ls1, tm)
        lc = _natural(l2, ls2, tm)
        m = jnp.maximum(jnp.maximum(la, lb), lc)
        e0, e1, e2 = jnp.exp(la - m), jnp.exp(lb - m), jnp.exp(lc - m)
        tot = e0 + e1 + e2
        lm_ref[...] = m + jnp.log(tot)
        w0, w1, w2 = e0 / tot, e1 / tot, e2 / tot
        for res in range(o1.shape[0]):
            _chunks_add_rows(os1, o1[res].astype(F32), res, tm // o1.shape[0], o1.shape[0], False)
        for res in range(o2.shape[0]):
            _chunks_add_rows(os2, o2[res].astype(F32), res, tm // o2.shape[0], o2.shape[0], False)
        for h in range(HEADS):
            sl = slice(h * HD, (h + 1) * HD)
            acc = w0[:, h:h + 1] * o0[:, sl].astype(F32) + w1[:, h:h + 1] * os1[h] + w2[:, h:h + 1] * os2[h]
            om_ref[:, sl] = acc.astype(BF16)

    def spec(a, c):
        if a.ndim == 2:
            return pl.BlockSpec((tm, c), lambda i: (i, 0))
        return pl.BlockSpec((a.shape[0], tm // a.shape[0], c), lambda i: (0, i, 0))

    return pl.pallas_call(
        kern, grid=(S // tm,),
        in_specs=[spec(a, D) for a in os] + [spec(a, HD) for a in lses],
        out_specs=[pl.BlockSpec((tm, D), lambda i: (i, 0)), pl.BlockSpec((tm, HD), lambda i: (i, 0))],
        out_shape=[jax.ShapeDtypeStruct((S, D), BF16), jax.ShapeDtypeStruct((S, HD), F32)],
        scratch_shapes=[pltpu.VMEM((1, tm, HD), F32), pltpu.VMEM((1, tm, HD), F32),
                        pltpu.VMEM((HEADS, tm, HD), F32), pltpu.VMEM((HEADS, tm, HD), F32)],
        compiler_params=_cparams(1), name="attn_merge")(*os, *lses)


def _attn_bwd_prep(do, o, lse):
    S = o.shape[0]
    tm = 512
    dils = DILS[1:]

    def kern(do_ref, o_ref, l_ref, *rest):
        do_outs, l_outs, d_outs = rest[0:3], rest[3:5], rest[5:8]
        do_scr, l_scr, d_scr = rest[8:11]
        lane = lax.broadcasted_iota(jnp.int32, (tm, HD), 1)
        acc = jnp.zeros((tm, HD), F32)
        for h in range(HEADS):
            sl = slice(h * HD, (h + 1) * HD)
            prod = do_ref[:, sl] * o_ref[:, sl].astype(F32)
            acc = jnp.where(lane == h, jnp.sum(prod, axis=-1, keepdims=True), acc)
        d_scr[0] = acc
        l_scr[0] = l_ref[...]
        _chunks_put(do_scr, do_ref[...])
        do_outs[0][...] = do_ref[...].astype(BF16)
        d_outs[0][...] = acc
        for j, dil in enumerate(dils):
            for res in range(dil):
                n = tm // dil
                do_outs[1 + j][res] = _chunks_rows(do_scr, res, n, dil).astype(BF16)
                l_outs[j][res] = _chunks_rows(l_scr, res, n, dil)
                d_outs[1 + j][res] = _chunks_rows(d_scr, res, n, dil)

    def nat(c):
        return pl.BlockSpec((tm, c), lambda i: (i, 0))

    def fol(dil, c):
        return pl.BlockSpec((dil, tm // dil, c), lambda i: (0, i, 0))

    def shapes(c, dt, with_natural):
        first = [jax.ShapeDtypeStruct((S, c), dt)] if with_natural else []
        return first + [jax.ShapeDtypeStruct((dil, S // dil, c), dt) for dil in dils]

    outs = pl.pallas_call(
        kern, grid=(S // tm,), in_specs=[nat(D), nat(D), nat(HD)],
        out_specs=[nat(D)] + [fol(dil, D) for dil in dils] + [fol(dil, HD) for dil in dils]
        + [nat(HD)] + [fol(dil, HD) for dil in dils],
        out_shape=shapes(D, BF16, True) + shapes(HD, F32, False) + shapes(HD, F32, True),
        scratch_shapes=[pltpu.VMEM((HEADS, tm, HD), F32), pltpu.VMEM((1, tm, HD), F32), pltpu.VMEM((1, tm, HD), F32)],
        compiler_params=_cparams(1), name="attn_bwd_prep")(do, o, lse)
    return outs[0:3], [lse] + list(outs[3:5]), outs[5:8]


def _attn_bwd(qkv_f, do_f, lse_f, delta_f, bias, nb, name):
    S = qkv_f.shape[0]
    nblk = S // QB
    scale = HD ** -0.5

    npair = nblk // 2

    def kern(q_ref, k2_ref, kp_ref, v2_ref, vp_ref, do_ref, l_ref, d_ref, b_ref, out_ref, dq_c, dk_c, dv_c,
             s_scr, dp_scr, ds_scr, p_scr):
        s_id = pl.program_id(0)

        @pl.when(s_id == 0)
        def _():
            dq_c[...] = jnp.zeros_like(dq_c)
            dk_c[...] = jnp.zeros_like(dk_c)
            dv_c[...] = jnp.zeros_like(dv_c)

        @pl.when(s_id == npair)
        def _():
            out_ref[:, 0:D] = dq_c[...].astype(BF16)
            out_ref[:, D:2 * D] = dk_c[...].astype(BF16)
            out_ref[:, 2 * D:3 * D] = dv_c[...].astype(BF16)

        def keys(sub, cur2_ref, prev_ref, sl):
            if sub:
                return cur2_ref[:, sl]
            return jnp.concatenate([prev_ref[:, sl], cur2_ref[0:QB, sl]], axis=0)

        @pl.when(s_id < npair)
        def _():
            col = lax.broadcasted_iota(jnp.int32, (QB, 2 * QB), 1)
            out_ref[:, 0:D] = dq_c[...].astype(BF16)
            for sub in range(2):
                rows = slice(sub * QB, (sub + 1) * QB)
                for h in range(HEADS):
                    sl = slice(h * HD, (h + 1) * HD)
                    u = sub * HEADS + h
                    s_scr[u] = lax.dot_general(q_ref[rows, sl], keys(sub, k2_ref, kp_ref, sl),
                                               (((1,), (1,)), ((), ())), preferred_element_type=F32)
                    dp_scr[u] = lax.dot_general(do_ref[rows, sl], keys(sub, v2_ref, vp_ref, sl),
                                                (((1,), (1,)), ((), ())), preferred_element_type=F32)
            for sub in range(2):
                rows = slice(sub * QB, (sub + 1) * QB)
                has_prev = jnp.bitwise_and(2 * s_id + sub, nb - 1) != 0
                dead = jnp.logical_and(col < QB, jnp.logical_not(has_prev))
                lv = l_ref[rows, :]
                dv_ = d_ref[rows, :]
                for h in range(HEADS):
                    u = sub * HEADS + h
                    s = s_scr[u] * scale + b_ref[h]
                    s = jnp.where(dead, NEG, s)
                    p = jnp.exp(s - lv[:, h:h + 1])
                    ds_scr[u] = (p * (dp_scr[u] - dv_[:, h:h + 1]) * scale).astype(BF16)
                    p_scr[u] = p.astype(BF16)
            for h in range(HEADS):
                sl = slice(h * HD, (h + 1) * HD)
                parts = []
                for sub in range(2):
                    rows = slice(sub * QB, (sub + 1) * QB)
                    u = sub * HEADS + h
                    ds = ds_scr[u]
                    dq_c[rows, sl] = jnp.dot(ds, keys(sub, k2_ref, kp_ref, sl), preferred_element_type=F32)
                    dkk = lax.dot_general(ds, q_ref[rows, sl], (((0,), (0,)), ((), ())), preferred_element_type=F32)
                    dvv = lax.dot_general(p_scr[u], do_ref[rows, sl], (((0,), (0,)), ((), ())),
                                          preferred_element_type=F32)
                    parts.append((dkk, dvv))
                for which, carry, base in ((0, dk_c, D), (1, dv_c, 2 * D)):
                    first, second = parts[0][which], parts[1][which]
                    cols = slice(base + h * HD, base + (h + 1) * HD)
                    out_ref[0:QB, cols] = carry[0:QB, sl].astype(BF16)
                    out_ref[QB:2 * QB, cols] = (carry[QB:2 * QB, sl] + first[:QB]).astype(BF16)
                    carry[0:QB, sl] = first[QB:] + second[:QB]
                    carry[QB:2 * QB, sl] = second[QB:]

    last = npair - 1

    def pair(colblk, c):
        return pl.BlockSpec((2 * QB, c), lambda s: (jnp.minimum(s, last), colblk))

    def prev(colblk):
        return pl.BlockSpec((QB, D), lambda s: (jnp.maximum(2 * jnp.minimum(s, last) - 1, 0), colblk))

    return pl.pallas_call(
        kern, grid=(npair + 1,),
        in_specs=[pair(0, D), pair(1, D), prev(1), pair(2, D), prev(2), pair(0, D), pair(0, HD), pair(0, HD),
                  pl.BlockSpec((HEADS, QB, 2 * QB), lambda s: (0, 0, 0))],
        out_specs=pl.BlockSpec((2 * QB, 3 * D), lambda s: (jnp.maximum(s - 1, 0), 0)),
        out_shape=jax.ShapeDtypeStruct((S, 3 * D), BF16),
        scratch_shapes=[pltpu.VMEM((2 * QB, D), F32), pltpu.VMEM((2 * QB, D), F32), pltpu.VMEM((2 * QB, D), F32),
                        pltpu.VMEM((2 * HEADS, QB, 2 * QB), F32), pltpu.VMEM((2 * HEADS, QB, 2 * QB), F32),
                        pltpu.VMEM((2 * HEADS, QB, 2 * QB), BF16), pltpu.VMEM((2 * HEADS, QB, 2 * QB), BF16)],
        compiler_params=_cparams(1), name=name)(qkv_f, qkv_f, qkv_f, qkv_f, qkv_f, do_f, lse_f, delta_f, bias)


def _local_step(x, tgt, comm, attn_norm, ffn_norm, final_norm):
    S = x.shape[0]
    bias = _bias_table()
    g_attn = attn_norm.reshape(1, D)
    g_f0 = ffn_norm[0:1]
    g_f1 = ffn_norm[1:2]
    g_fin = final_norm.reshape(1, D)
    W = {}

    def ffn_fwd(xin, h, l, next_gain, target=None):
        gu, act = _ffn_up(h, W[f"gu{l}"], f"ffn_up{l}")
        return gu, act, _mm_res_norm(act, W[f"d{l}"], xin, next_gain, K=DFF, tm=512, tgt=target,
                                     name=f"ffn_down{l}")

    def ffn_bwd(dxo, xin, gain, h, gu, act, l, rs_group):
        dgu = _ffn_down_bwd(dxo, W[f"d{l}"], gu, f"ffn_down_bwd{l}")
        gw_d = _mm(act, dxo, mode="tn", M=DFF, N=D, K=S, tm=HCH, tn=D, tk=2048, out_dtype=BF16, name=f"gw_d{l}")
        gw_gu = _mm(dgu, h, mode="tn", M=2 * DFF, N=D, K=S, tm=HCH, tn=D, tk=2048, out_dtype=BF16, name=f"gw_gu{l}")
        token = comm.send_grads(rs_group, {f"d{l}": gw_d, f"gu{l}": gw_gu})
        return _mm_rms_bwd(dgu, W[f"gu{l}"], xin, gain, dxo, mode="nn", M=S, K=2 * DFF, tm=512, deps=(token,),
                           name=f"ffn_up_bwd{l}")

    nbs = [S // QB // dil for dil in DILS]
    hf = _rms_fwd_folded(x, g_attn, "rms_attn", deps=comm.ag_tokens)
    hf = [h.reshape(S, D) for h in hf]
    W.update(comm.weights(0, hf[0]))
    qkv_f, o_f, lse_f = [], [], []
    for g, dil in enumerate(DILS):
        qkv_f.append(_mm(hf[g], W["qkv"], mode="nt", M=S, N=3 * D, K=D, tm=2048, tn=1024, tk=D, out_dtype=BF16,
                         b_off=(3 * g, 0), name=f"qkv_proj{g}"))
        og, lg = _attn_fwd(qkv_f[g], bias[g], nbs[g], f"attn_fwd{g}")
        o_f.append(og if dil == 1 else og.reshape(dil, S // dil, D))
        lse_f.append(lg if dil == 1 else lg.reshape(dil, S // dil, HD))
    passing = [comm.pass_on(1, tuple(o_f)), comm.pass_on(2, tuple(o_f))]
    (o_f, lse_f), passing = lax.optimization_barrier(((o_f, lse_f), passing))
    o, lse = _attn_merge(o_f, lse_f)
    W.update(comm.weights(1, (o, passing[0])))
    x1, h1 = _mm_res_norm(o, W["wo"], x, g_f0, K=D, tm=1024, name="attn_out")
    pv = W["pv"].reshape(NDEV, 8, 128)
    pool_norm, pool_scale = pv[:, 0, :].reshape(1, D), pv[:, 1, :].reshape(1, D)
    gu0, act0, (x2, h2) = ffn_fwd(x1, h1, 0, pool_norm)

    W.update(comm.weights(2, (x2, passing[1])))
    u = _mm(h2, W["wpi"], mode="nn", M=S, N=D, K=D, tm=1024, tn=D, tk=D, out_dtype=F32, name="pool_in")
    yd = _trail(u, backward=False, name="trail_fwd")
    x3 = _pool_out(yd, W["pg"], pool_scale, x2)
    h3 = _rms_fwd(x3, g_f1, "rms_ffn1")
    gu1, act1, (dx4, d_fin, lossvec) = ffn_fwd(x3, h3, 1, g_fin, target=tgt)

    dx3, d_f1 = ffn_bwd(dx4, x3, g_f1, h3, gu1, act1, 1, 0)
    dyd, d_scale, gw_pg = _pool_out_bwd(dx3, yd, W["pg"], pool_scale)
    du = _trail(dyd, backward=True, name="trail_bwd")
    gw_pi = _mm(h2, du, mode="tn", M=D, N=D, K=S, tm=D, tn=D, tk=2048, out_dtype=BF16, name="gw_pi")
    token = comm.send_grads(1, {"pg": gw_pg, "wpi": gw_pi})
    dx2, d_pool = _mm_rms_bwd(du, W["wpi"], x2, pool_norm, dx3, mode="nt", M=S, K=D, tm=1024, deps=(token,),
                              name="pool_in_bwd")
    dx1, d_f0 = ffn_bwd(dx2, x1, g_f0, h1, gu0, act0, 0, 2)

    gw_o = _mm(o, dx1, mode="tn", M=D, N=D, K=S, tm=D, tn=D, tk=2048, out_dtype=BF16, name="gw_o")
    do = _mm(dx1, W["wo"], mode="nt", M=S, N=D, K=D, tm=1024, tn=D, tk=D, out_dtype=F32, name="attn_out_bwd")
    do_f, lse_ff, delta_f = _attn_bwd_prep(do, o, lse)
    dqkv_f, gw_qkv = [], None
    for g in range(NGROUPS):
        dqkv_f.append(_attn_bwd(qkv_f[g], do_f[g].reshape(S, D), lse_ff[g].reshape(S, HD),
                                delta_f[g].reshape(S, HD), bias[g], nbs[g], f"attn_bwd{g}"))
        gw_qkv = _mm(dqkv_f[g], hf[g], mode="tn", M=3 * D, N=D, K=S, tm=1024, tn=D, tk=2048, out_dtype=BF16,
                     out_rows=NGROUPS * 3 * D, out_off=3 * g, out_prev=gw_qkv, name=f"gw_qkv{g}")
    token = comm.send_grads_pairwise({"wo": gw_o, "qkv": gw_qkv})
    dh0_f = [None] * NGROUPS
    for g in reversed(range(NGROUPS)):
        dh0_f[g] = _mm(dqkv_f[g], W["qkv"], mode="nn", M=S, N=D, K=3 * D, tm=1024, tn=D, tk=3 * D, out_dtype=F32,
                       b_off=(g, 0), deps=(token,), name=f"qkv_proj_bwd{g}")
        if g == NGROUPS - 1:
            token = comm.pass_grads(dh0_f[g])
    folded = [dh0_f[g].reshape(dil, S // dil, D) for g, dil in enumerate(DILS) if dil > 1]
    grad_x, d_attn = _rms_bwd(dh0_f[0], x, g_attn, dx1, "rms_attn_bwd", folded=folded)

    vec = jnp.concatenate([d_attn, d_f0, d_f1, d_fin, d_pool, d_scale, lossvec, jnp.zeros((1, D), F32)], axis=0)
    return grad_x, vec


def _mesh_pos():
    x, y, c = lax.axis_index("x"), lax.axis_index("y"), lax.axis_index("c")
    return x, y, c, 4 * x + 2 * y + c


def _peer(x, y, c, k):
    kx, ky, kc = (k >> 2) & 1, (k >> 1) & 1, k & 1
    px = 1 - x if kx else x
    py = 1 - y if ky else y
    pc = 1 - c if kc else c
    return (px, py, pc), 4 * px + 2 * py + pc


ANY = pl.BlockSpec(memory_space=pl.ANY)


HBM = pl.BlockSpec(memory_space=pltpu.HBM)
SEMS = pl.BlockSpec(memory_space=pltpu.SEMAPHORE)
EFFECT = pltpu.SideEffectType.DATAFLOW_SIDE_EFFECTING
NPEER = NDEV - 1

AG_GROUPS = (("qkv",), ("wo", "gu0", "d0", "pv"), ("wpi", "pg", "gu1", "d1"))
AG_ORDER = tuple(n for grp in AG_GROUPS for n in grp)
RS_GROUPS = (("d1", "gu1"), ("pg", "wpi"), ("d0", "gu0"), ("wo", "qkv"))


def _hbm(a):
    return pltpu.with_memory_space_constraint(a, pltpu.HBM)


def _remote(src, dst, send, recv, peer):
    return pltpu.make_async_remote_copy(src_ref=src, dst_ref=dst, send_sem=send, recv_sem=recv, device_id=peer,
                                        device_id_type=pl.DeviceIdType.MESH)


def _bcast_all(v, name, deps=()):
    W = v.shape[1]
    nd = len(deps)

    def kern(v_ref, *rest):
        o_ref, send, recv, lsem = rest[nd:]
        x, y, c, me = _mesh_pos()
        own = pltpu.make_async_copy(v_ref, o_ref.at[me], lsem)
        own.start()
        cps = [_remote(v_ref, o_ref.at[me], send.at[k - 1], recv.at[k - 1], _peer(x, y, c, k)[0])
               for k in range(1, NDEV)]
        for cp in cps:
            cp.start()
        for cp in cps:
            cp.wait_recv()
            cp.wait_send()
        own.wait()

    return pl.pallas_call(
        kern, in_specs=[ANY] * (1 + nd), out_specs=ANY, out_shape=jax.ShapeDtypeStruct((NDEV, 8, W), F32),
        scratch_shapes=[pltpu.SemaphoreType.DMA((NPEER,)), pltpu.SemaphoreType.DMA((NPEER,)),
                        pltpu.SemaphoreType.DMA(())],
        name=name)(v, *deps)


ALL_KS = tuple(range(1, NDEV))
AG_KS1 = (1, 2, 4, 6)
AG_KS2 = (2, 4, 6)
RS_KS_PAIR = (1, 3, 5, 7)
RS_KS_CHIPS = (2, 4, 6)


def _split_start(srcs, src_of, lands, copy_refs, name, deps=(), ks=ALL_KS, to=None):
    ns, n, nd, nk = len(srcs), len(lands), len(deps), len(ks)

    def body(*refs):
        ins, land = refs[:ns], refs[ns:ns + n]
        send, recv = refs[ns + n + nd], refs[ns + n + nd + 1]
        token = refs[-1]
        x, y, c, me = _mesh_pos()
        for j in range(n):
            for i, k in enumerate(ks):
                _, pid = _peer(x, y, c, k)
                dest, _ = _peer(x, y, c, k if to is None else to)
                src, dst = copy_refs(j, (land[j] if src_of[j] is None else ins[src_of[j]]), land[j], me, pid, i)
                _remote(src, dst, send.at[j * nk + i], recv.at[j * nk + i], dest).start()
        token[...] = jnp.zeros_like(token)

    outs = pl.pallas_call(
        body, name=name,
        out_shape=(pltpu.SemaphoreType.DMA((n * nk,)), pltpu.SemaphoreType.DMA((n * nk,)))
        + tuple(pltpu.HBM(a.shape, a.dtype) for a in srcs) + tuple(pltpu.HBM(a.shape, a.dtype) for a in lands)
        + (jax.ShapeDtypeStruct((8, 128), F32),),
        in_specs=(HBM,) * (ns + n) + (ANY,) * nd,
        out_specs=(SEMS, SEMS) + (HBM,) * (ns + n) + (pl.BlockSpec(memory_space=pltpu.VMEM),),
        input_output_aliases={i: 2 + i for i in range(ns + n)},
        compiler_params=pltpu.CompilerParams(has_side_effects=EFFECT),
    )(*[_hbm(a) for a in srcs], *[_hbm(a) for a in lands], *deps)
    return outs[0], outs[1], list(outs[2:2 + ns]), list(outs[2 + ns:2 + ns + n]), outs[-1]


def _split_wait(srcs, src_of, lands, send, recv, sem_rows, wait_refs, after, name, ks=ALL_KS):
    ns, n, nk = len(srcs), len(lands), len(ks)
    after = tuple(after) if isinstance(after, (tuple, list)) else (after,)

    def body(*refs):
        ins, land = refs[:ns], refs[ns:ns + n]
        send_ref, recv_ref = refs[ns + n], refs[ns + n + 1]
        x, y, c, me = _mesh_pos()
        for j in range(n):
            for i, k in enumerate(ks):
                peer, _ = _peer(x, y, c, k)
                src, dst = wait_refs(j, (land[j] if src_of[j] is None else ins[src_of[j]]), land[j])
                sem = sem_rows[j] * nk + i
                cp = _remote(src, dst, send_ref.at[sem], recv_ref.at[sem], peer)
                cp.wait_send()
                cp.wait_recv()

    outs = pl.pallas_call(
        body, name=name,
        out_shape=tuple(pltpu.HBM(a.shape, a.dtype) for a in srcs) + tuple(pltpu.HBM(a.shape, a.dtype) for a in lands),
        in_specs=(HBM,) * (ns + n) + (SEMS, SEMS) + (ANY,) * len(after),
        out_specs=(HBM,) * (ns + n),
        input_output_aliases={i: i for i in range(ns + n)},
        compiler_params=pltpu.CompilerParams(has_side_effects=EFFECT),
    )(*srcs, *lands, send, recv, *after)
    return list(outs[:ns]), list(outs[ns:])


def _ag_dtype(name):
    return F32 if name == "pv" else BF16


def _ag_align(name):
    return 8 if name == "pv" else 16


def _place_transposed(w, me, name):
    rows = w.shape[1]
    nblk = rows // 128

    def kern(me_ref, w_ref, o_ref):
        o_ref[...] = w_ref[...].T.astype(BF16)

    grid_spec = pltpu.PrefetchScalarGridSpec(
        num_scalar_prefetch=1, grid=(nblk,),
        in_specs=[pl.BlockSpec((D, 128), lambda i, me_ref: (0, i))],
        out_specs=pl.BlockSpec((128, D), lambda i, me_ref: (me_ref[0] * nblk + i, 0)))
    return pl.pallas_call(
        kern, grid_spec=grid_spec, out_shape=jax.ShapeDtypeStruct((NDEV * rows, D), BF16),
        compiler_params=_cparams(1), name=name)(me.reshape(1).astype(jnp.int32), w)


class _Comm:
    def __init__(self, params, make_shards, me, placed):
        self.me = me
        self.ag_land, self.ag_sems, self.ag_tokens, self.ag_passing = {}, {}, (), {}
        self.rs = []
        deps = ()
        for part, names in enumerate((AG_GROUPS[0], AG_ORDER[len(AG_GROUPS[0]):])):
            rows = [SEC_ROWS[n] for n in names]
            if part == 0:
                lands = [placed[n] for n in names]
            else:
                params, deps = lax.optimization_barrier((params, deps))
                shards = make_shards(*params)
                lands = [lax.dynamic_update_slice(lax.empty((NDEV * r, shards[n].shape[1]), _ag_dtype(n)),
                                                  shards[n].astype(_ag_dtype(n)), (_shard_pos(n, me), 0))
                         for n, r in zip(names, rows)]

            def copy_refs(j, src, land, me, pid, i, names=names, rows=rows):
                own = land.at[pl.ds(pl.multiple_of(_shard_pos(names[j], me), _ag_align(names[j])), rows[j])]
                return own, own

            send, recv, _, lands, token = _split_start([], [None] * len(names), lands, copy_refs, f"ag_start{part}",
                                                       deps=deps, ks=AG_KS1)
            deps = (token,)
            self.ag_tokens += (token,)
            for j, n in enumerate(names):
                self.ag_land[n] = lands[j]
                self.ag_sems[n] = (send, recv, j)

    def pass_on(self, group, after):
        names = AG_GROUPS[group]
        send, recv = self.ag_sems[names[0]][:2]
        idx = [self.ag_sems[n][2] for n in names]
        rows = [SEC_ROWS[n] for n in names]
        none = [None] * len(names)

        def wait_refs(j, src, land):
            return land.at[pl.ds(0, rows[j])], land.at[pl.ds(0, rows[j])]

        _, lands = _split_wait([], none, [self.ag_land[n] for n in names], send, recv, idx,
                               wait_refs, after, f"ag_wait{group}", ks=AG_KS1)

        def copy_refs(j, src, land, me, pid, i):
            theirs = land.at[pl.ds(pl.multiple_of(_shard_pos(names[j], pid), _ag_align(names[j])), rows[j])]
            return theirs, theirs

        send, recv, _, lands, token = _split_start([], none, lands, copy_refs, f"ag_pass{group}", ks=AG_KS2, to=1)
        self.ag_passing[group] = (send, recv, lands, wait_refs)
        return token

    def weights(self, group, after):
        names = AG_GROUPS[group]
        if group not in self.ag_passing:
            after = self.pass_on(group, after)
        send, recv, lands, wait_refs = self.ag_passing[group]
        _, lands = _split_wait([], [None] * len(names), lands, send, recv, list(range(len(names))), wait_refs, after,
                               f"ag_pass_wait{group}", ks=AG_KS2)
        return dict(zip(names, lands))

    def send_grads(self, group, gws):
        names = RS_GROUPS[group]
        rows = [SEC_ROWS[n] for n in names]
        grads = [gws[n] for n in names]
        me = self.me
        lands = [lax.dynamic_update_slice(
            lax.empty((NDEV, r, D), BF16),
            lax.dynamic_slice(g, (_shard_pos(n, me), 0), (r, D))[None], (me, 0, 0))
            for n, r, g in zip(names, rows, grads)]

        def copy_refs(j, src, land, me, pid, i):
            return src.at[pl.ds(pl.multiple_of(_shard_pos(names[j], pid), 16), rows[j])], land.at[me]

        send, recv, srcs, lands, token = _split_start(grads, list(range(len(names))), lands, copy_refs,
                                                      f"rs_start{group}")
        self.rs.append((names, rows, send, recv, srcs, lands, ALL_KS))
        return token

    def send_grads_pairwise(self, gws):
        names = RS_GROUPS[-1]
        rows = [SEC_ROWS[n] for n in names]
        grads = [gws[n] for n in names]
        idx = list(range(len(names)))
        lands = [lax.empty((len(RS_KS_PAIR), r, D), BF16) for r in rows]

        def copy_refs(j, src, land, me, pid, i):
            return src.at[pl.ds(pl.multiple_of(_shard_pos(names[j], pid), 16), rows[j])], land.at[i]

        send, recv, srcs, lands, token = _split_start(grads, idx, lands, copy_refs, "rs_pair_start",
                                                      ks=RS_KS_PAIR, to=1)
        self.pair = (names, rows, send, recv, srcs, lands)
        return token

    def pass_grads(self, after):
        names, rows, send, recv, srcs, lands = self.pair
        idx = list(range(len(names)))
        me = self.me

        def wait_refs(j, src, land):
            return src.at[pl.ds(0, rows[j])], land.at[0]

        srcs, lands = _split_wait(srcs, idx, lands, send, recv, idx, wait_refs, after, "rs_pair_wait", ks=RS_KS_PAIR)
        sums = []
        for n, r, g, got in zip(names, rows, srcs, lands):
            mine = jnp.stack([lax.dynamic_slice(g, (_shard_pos(n, jnp.bitwise_xor(me, k)), 0), (r, D))
                              for k in (0,) + RS_KS_CHIPS])
            sums.append(_pair_sum(mine, got, f"rs_pair_sum_{n}"))
        lands = [lax.dynamic_update_slice(lax.empty(p.shape, BF16), p[0:1], (0, 0, 0)) for p in sums]

        def copy_refs(j, src, land, me, pid, i):
            return src.at[i + 1], land.at[i + 1]

        send, recv, sums, lands, token = _split_start(sums, idx, lands, copy_refs, f"rs_start{len(RS_GROUPS) - 1}",
                                                      ks=RS_KS_CHIPS)
        self.rs.append((names, rows, send, recv, sums, lands, RS_KS_CHIPS))
        return token

    def received(self, group, after):
        names, rows, send, recv, srcs, lands, ks = self.rs[group]
        whole = srcs[0].ndim == 2

        def wait_refs(j, src, land):
            return (src.at[pl.ds(0, rows[j])] if whole else src.at[0]), land.at[0]

        _, lands = _split_wait(srcs, list(range(len(names))), lands, send, recv, list(range(len(names))), wait_refs,
                               after, f"rs_wait{group}", ks=ks)
        return dict(zip(names, lands))


def _pair_sum(a, b, name):
    n, rows, _ = a.shape
    tr = 384 if rows % 384 == 0 else rows

    def kern(a_ref, b_ref, o_ref):
        o_ref[...] = (a_ref[...].astype(F32) + b_ref[...].astype(F32)).astype(BF16)

    blk = pl.BlockSpec((1, tr, D), lambda i, t: (i, t, 0))
    return pl.pallas_call(
        kern, grid=(n, rows // tr), in_specs=[blk, blk], out_specs=blk,
        out_shape=jax.ShapeDtypeStruct(a.shape, BF16), compiler_params=_cparams(2), name=name)(a, b)


def _sum_contributions(r_ref):
    g = r_ref[0].astype(F32)
    for slot in range(1, r_ref.shape[0]):
        g = g + r_ref[slot].astype(F32)
    return g


def _adam_math(g, w, m, v):
    c1 = 1.0 / (1.0 - ADAM_B1 ** ADAM_STEP)
    c2 = 1.0 / (1.0 - ADAM_B2 ** ADAM_STEP)
    mn = ADAM_B1 * m + (1.0 - ADAM_B1) * g
    vn = ADAM_B2 * v + (1.0 - ADAM_B2) * (g * g)
    return -ADAM_LR * ((mn * c1) / (jnp.sqrt(vn * c2) + ADAM_EPS) + ADAM_WD * w), mn, vn


def _adamw(R, w, m, v, *, tr, name, layer=None, prev=None):
    rows, C = w.shape[-2:]
    nprev = 0 if prev is None else 4

    def kern(r_ref, w_ref, m_ref, v_ref, *rest):
        g_out, d_out, m_out, v_out = rest[nprev:]
        g = _sum_contributions(r_ref)
        g_out[...] = g
        d_out[...], m_out[...], v_out[...] = _adam_math(g, w_ref[...], m_ref[...], v_ref[...])

    if layer is None:
        tile = pl.BlockSpec((tr, C), lambda i: (i, 0))
    else:
        tile = pl.BlockSpec((None, tr, C), lambda i: (layer, i, 0))
    shp = jax.ShapeDtypeStruct(w.shape, F32)
    return pl.pallas_call(
        kern, grid=(rows // tr,),
        in_specs=[pl.BlockSpec((R.shape[0], tr, C), lambda i: (0, i, 0)), tile, tile, tile]
        + [pl.BlockSpec(memory_space=pl.ANY)] * nprev,
        out_specs=[tile] * 4, out_shape=[shp] * 4,
        input_output_aliases={4 + k: k for k in range(nprev)},
        compiler_params=_cparams(1), name=name)(R, w, m, v, *(prev or ()))


def _adamw_pool_group(R, w, m, v):
    rows = SEC_ROWS["pg"]

    def kern(r_ref, w_ref, m_ref, v_ref, g_out, d_out, m_out, v_out):
        g = _sum_contributions(r_ref)
        g_out[0] = g
        d_out[0], m_out[0], v_out[0] = _adam_math(g, w_ref[0], m_ref[0], v_ref[0])

    blk = pl.BlockSpec((1, rows, PGD), lambda i: (i, 0, 0))
    shp = jax.ShapeDtypeStruct((POOL_G, rows, PGD), F32)
    return pl.pallas_call(
        kern, grid=(POOL_G,),
        in_specs=[pl.BlockSpec((NDEV, rows, PGD), lambda i: (0, 0, i)), blk, blk, blk],
        out_specs=[blk] * 4, out_shape=[shp] * 4, compiler_params=_cparams(1), name="adamw_pg")(R, w, m, v)


def _grad_sum_t(R, name):
    rows = R.shape[1]
    tr = 128

    def kern(r_ref, o_ref):
        o_ref[...] = _sum_contributions(r_ref).T

    return pl.pallas_call(
        kern, grid=(rows // tr,), in_specs=[pl.BlockSpec((R.shape[0], tr, D), lambda i: (0, i, 0))],
        out_specs=pl.BlockSpec((D, tr), lambda i: (0, i)), out_shape=jax.ShapeDtypeStruct((D, rows), F32),
        compiler_params=_cparams(1), name=name)(R)


def _adam_plain(g, w, m, v, *, tr, name):
    rows, C = w.shape

    def kern(g_ref, w_ref, m_ref, v_ref, d_out, m_out, v_out):
        d_out[...], m_out[...], v_out[...] = _adam_math(g_ref[...], w_ref[...], m_ref[...], v_ref[...])

    tile = pl.BlockSpec((tr, C), lambda i: (i, 0))
    shp = jax.ShapeDtypeStruct((rows, C), F32)
    return pl.pallas_call(
        kern, grid=(rows // tr,), in_specs=[tile] * 4, out_specs=[tile] * 3, out_shape=[shp] * 3,
        compiler_params=_cparams(1), name=name)(g, w, m, v)


def _pack_sections(w_qkv, w_attn_out, w_pool_in, w_pool_group, w_ffn_gate_up, w_ffn_down):
    pg = w_pool_group[0].transpose(1, 0, 2).reshape(SEC_ROWS["pg"], D)
    return {"qkv": w_qkv[0].T, "wo": w_attn_out[0], "wpi": w_pool_in[0], "gu0": w_ffn_gate_up[0].T,
            "gu1": w_ffn_gate_up[1].T, "d0": w_ffn_down[0], "d1": w_ffn_down[1], "pg": pg}


def _vec_pack(attn_norm, ffn_norm, final_norm, pool_norm_sh, pool_scale_sh, me):
    def place(sh):
        return lax.dynamic_update_slice(jnp.zeros((1, D), F32), sh, (0, me * 128))
    return jnp.concatenate([attn_norm, ffn_norm, final_norm.reshape(1, D), place(pool_norm_sh),
                            place(pool_scale_sh), jnp.zeros((2, D), F32)], axis=0)


def _vec_unpack(p, me):
    def take(r):
        return lax.dynamic_slice(p[r:r + 1], (0, me * 128), (1, 128))
    return p[0:1], p[1:3], p[3], take(4), take(5)


def kernel(x, attn_norm, w_qkv, w_attn_out, pool_norm, w_pool_in, w_pool_group, pool_scale, ffn_norm, w_ffn_gate_up, w_ffn_down, final_norm, loss_target, m_attn_norm, m_w_qkv, m_w_attn_out, m_pool_norm, m_w_pool_in, m_w_pool_group, m_pool_scale, m_ffn_norm, m_w_ffn_gate_up, m_w_ffn_down, m_final_norm, v_attn_norm, v_w_qkv, v_w_attn_out, v_pool_norm, v_w_pool_in, v_w_pool_group, v_pool_scale, v_ffn_norm, v_w_ffn_gate_up, v_w_ffn_down, v_final_norm):
    me = 4 * lax.axis_index("x") + 2 * lax.axis_index("y") + lax.axis_index("c")

    def make_shards(wq, wo, wpi, wpg, wgu, wd, pn, ps):
        shards = _pack_sections(wq, wo, wpi, wpg, wgu, wd)
        shards["pv"] = jnp.concatenate([pn, ps, jnp.zeros((6, 128), F32)], axis=0)
        return shards

    comm = _Comm((w_qkv, w_attn_out, w_pool_in, w_pool_group, w_ffn_gate_up, w_ffn_down, pool_norm, pool_scale),
                 make_shards, me, placed={"qkv": _place_transposed(w_qkv[0], me, "place_qkv")})

    grad_x, vec = _local_step(x[0], loss_target[0], comm, attn_norm, ffn_norm, final_norm)

    small = ((attn_norm, ffn_norm, final_norm, pool_norm, pool_scale),
             (m_attn_norm, m_ffn_norm, m_final_norm, m_pool_norm, m_pool_scale),
             (v_attn_norm, v_ffn_norm, v_final_norm, v_pool_norm, v_pool_scale))
    small, grad_x = lax.optimization_barrier((small, grad_x))
    vw, vm, vv = (_vec_pack(*s, me) for s in small)

    gu_t = [jnp.swapaxes(a, 1, 2) for a in (w_ffn_gate_up, m_w_ffn_gate_up, v_w_ffn_gate_up)]
    res = {}
    gu_res, d_res = None, None
    vec_out = None
    after = grad_x
    for group in range(len(RS_GROUPS)):
        if group == len(RS_GROUPS) - 1:
            VR = _bcast_all(vec, "exchange_vector_grads", deps=(after,))
            vec_out = _adamw(VR, vw, vm, vv, tr=8, name="adamw_vec")
            after = vec_out[0]
        for n, R in comm.received(group, after).items():
            if n in ("d0", "d1"):
                d_res = _adamw(R, w_ffn_down, m_w_ffn_down, v_w_ffn_down, tr=352, name=f"adamw_{n}",
                               layer=int(n[1]), prev=d_res)
                after = d_res[0]
            elif n in ("gu0", "gu1"):
                gu_res = _adamw(R, *gu_t, tr=352, name=f"adamw_{n}", layer=int(n[2]), prev=gu_res)
                after = gu_res[0]
            elif n == "pg":
                out = _adamw_pool_group(R, w_pool_group[0], m_w_pool_group[0], v_w_pool_group[0])
                res["pg"] = tuple(a[None] for a in out)
                after = out[0]
            elif n in ("wo", "wpi"):
                w, m, v = ((w_attn_out, m_w_attn_out, v_w_attn_out) if n == "wo"
                           else (w_pool_in, m_w_pool_in, v_w_pool_in))
                res[n] = _adamw(R, w[0], m[0], v[0], tr=128, name=f"adamw_{n}")
                res[n] = tuple(a[None] for a in res[n])
                after = res[n][0]
            else:
                g = _grad_sum_t(R, "grad_sum_qkv")
                out = _adam_plain(g, w_qkv[0], m_w_qkv[0], v_w_qkv[0], tr=256, name="adamw_qkv")
                res["qkv"] = tuple(a[None] for a in (g,) + tuple(out))
                after = out[0]
    res["gu"] = tuple(jnp.swapaxes(a, 1, 2) for a in gu_res)
    res["d"] = tuple(d_res)

    outs = []
    for kind in range(4):
        an, fn, fin, pn, ps = _vec_unpack(vec_out[kind], me)
        outs.append((an, res["qkv"][kind], res["wo"][kind], pn, res["wpi"][kind], res["pg"][kind], ps, fn,
                     res["gu"][kind], res["d"][kind], fin))
    loss = 0.5 * jnp.sum(vec_out[0][6]) / D
    return (loss, grad_x[None]) + outs[0] + outs[1] + outs[2] + outs[3]
```

```python
import jax
import jax.numpy as jnp
from jax import lax
from jax.experimental import pallas as pl
from jax.experimental.pallas import tpu as pltpu

F32 = jnp.float32
BF16 = jnp.bfloat16

D = 1024
NDEV = 8
HEADS = 8
HD = 128
QB = 128
NGROUPS = 3
DILS = (1, 4, 16)
DFF = 2816
HCH = 1408
POOL_G = 4
PGD = 256
RMS_EPS = 1e-6
NEG = -1e30

ADAM_LR = 0.001
ADAM_B1 = 0.9
ADAM_B2 = 0.999
ADAM_EPS = 1e-08
ADAM_WD = 0.01
ADAM_STEP = 10

VMEM_LIMIT = 52 * 1024 * 1024

SECTIONS = (("qkv", 1152), ("wo", 128), ("wpi", 128), ("gu0", 704), ("gu1", 704),
            ("d0", 352), ("d1", 352), ("pg", 32))
LOC_OFF = {}
GLB_OFF = {}
_o = 0
for _n, _r in SECTIONS:
    LOC_OFF[_n] = _o
    GLB_OFF[_n] = _o * NDEV
    _o += _r
PACK_ROWS = _o
GLB_ROWS = PACK_ROWS * NDEV
SEC_ROWS = dict(SECTIONS)
SEC_ROWS["pv"] = 8


def _cparams(n_grid):
    return pltpu.CompilerParams(dimension_semantics=("arbitrary",) * n_grid, vmem_limit_bytes=VMEM_LIMIT)


def _shard_pos(name, dev):
    n = SEC_ROWS[name]
    if name in ("gu0", "gu1"):
        return ((dev % 4) // 2) * (2 * HCH) + (dev // 4) * HCH + (dev % 2) * n
    return dev * n


def _mm(a, b, *, mode, M, N, K, tm, tn, tk, out_dtype, name, a_off=(0, 0), b_off=(0, 0), res=None,
        out_rows=None, out_off=0, out_prev=None, deps=()):
    nm, nn, nk = M // tm, N // tn, K // tk
    assert nm * tm == M and nn * tn == N and nk * tk == K
    if mode == "nn":
        a_bs, b_bs = (tm, tk), (tk, tn)
        a_ix = lambda i, j, k: (i, k)
        b_ix = lambda i, j, k: (k, j)
        dims = (((1,), (0,)), ((), ()))
    elif mode == "nt":
        a_bs, b_bs = (tm, tk), (tn, tk)
        a_ix = lambda i, j, k: (i, k)
        b_ix = lambda i, j, k: (j, k)
        dims = (((1,), (1,)), ((), ()))
    else:
        a_bs, b_bs = (tk, tm), (tk, tn)
        a_ix = lambda i, j, k: (k, i)
        b_ix = lambda i, j, k: (k, j)
        dims = (((0,), (0,)), ((), ()))

    def spec(bs, ix, off):
        def im(i, j, k):
            r, c = ix(i, j, k)
            return (r + off[0], c + off[1])
        return pl.BlockSpec(bs, im)

    in_specs = [spec(a_bs, a_ix, a_off), spec(b_bs, b_ix, b_off)]
    args = [a, b]
    if res is not None:
        in_specs.append(pl.BlockSpec((tm, tn), lambda i, j, k: (i, j)))
        args.append(res)
    out_shape = jax.ShapeDtypeStruct((M if out_rows is None else out_rows, N), out_dtype)
    out_spec = pl.BlockSpec((tm, tn), lambda i, j, k: (i + out_off, j))
    has_res = res is not None
    extra = list(deps) + ([out_prev] if out_prev is not None else [])
    for dep in extra:
        in_specs.append(pl.BlockSpec(memory_space=pl.ANY))
        args.append(dep)
    o_pos = 2 + int(has_res) + len(extra)
    aliases = {len(args) - 1: 0} if out_prev is not None else {}

    def kern(*refs):
        a_ref, b_ref = refs[0], refs[1]
        res_ref = refs[2] if has_res else None
        o_ref = refs[o_pos]
        av = a_ref[...]
        bv = b_ref[...]
        if av.dtype != BF16:
            av = av.astype(BF16)
        if bv.dtype != BF16:
            bv = bv.astype(BF16)
        part = lax.dot_general(av, bv, dims, preferred_element_type=F32)

        def write(val):
            if has_res:
                val = val + res_ref[...]
            o_ref[...] = val.astype(out_dtype)

        if nk == 1:
            write(part)
        else:
            acc_ref = refs[-1]
            k = pl.program_id(2)

            @pl.when(k == 0)
            def _():
                acc_ref[...] = part

            @pl.when(k > 0)
            def _():
                acc_ref[...] += part

            @pl.when(k == nk - 1)
            def _():
                write(acc_ref[...])

    scratch = [pltpu.VMEM((tm, tn), F32)] if nk > 1 else []
    return pl.pallas_call(
        kern, grid=(nm, nn, nk), in_specs=in_specs, out_specs=out_spec, out_shape=out_shape,
        scratch_shapes=scratch, input_output_aliases=aliases, compiler_params=_cparams(3), name=name)(*args)


def _mm_rms_bwd(a, b, x, g, dres, *, mode, M, K, tm, name, b_off=(0, 0), deps=()):
    nd = len(deps)
    b_bs = (K, D) if mode == "nn" else (D, K)
    dims = (((1,), (0,)), ((), ())) if mode == "nn" else (((1,), (1,)), ((), ()))

    def kern(a_ref, b_ref, x_ref, g_ref, dres_ref, *rest):
        dx_ref, dg_ref = rest[nd:]
        i = pl.program_id(0)
        av = a_ref[...]
        if av.dtype != BF16:
            av = av.astype(BF16)
        dhv = lax.dot_general(av, b_ref[...], dims, preferred_element_type=F32)
        xv = x_ref[...]
        r = lax.rsqrt(jnp.mean(xv * xv, axis=-1, keepdims=True) + RMS_EPS)
        xhat = xv * r
        gy = dhv * g_ref[...]
        dx_ref[...] = dres_ref[...] + r * (gy - xhat * jnp.mean(gy * xhat, axis=-1, keepdims=True))
        part = jnp.sum(dhv * xhat, axis=0, keepdims=True)

        @pl.when(i == 0)
        def _():
            dg_ref[...] = part

        @pl.when(i > 0)
        def _():
            dg_ref[...] += part

    row = pl.BlockSpec((tm, D), lambda i: (i, 0))
    vec = pl.BlockSpec((1, D), lambda i: (0, 0))
    return pl.pallas_call(
        kern, grid=(M // tm,),
        in_specs=[pl.BlockSpec((tm, K), lambda i: (i, 0)),
                  pl.BlockSpec(b_bs, lambda i: b_off, pipeline_mode=pl.Buffered(1)), row, vec, row]
        + [pl.BlockSpec(memory_space=pl.ANY)] * nd,
        out_specs=[row, vec],
        out_shape=[jax.ShapeDtypeStruct((M, D), F32), jax.ShapeDtypeStruct((1, D), F32)],
        compiler_params=_cparams(1), name=name)(a, b, x, g, dres, *deps)


def _mm_res_norm(a, b, res, g, *, K, tm, name, b_off=(0, 0), tgt=None):
    M = a.shape[0]
    head = tgt is not None

    def kern(a_ref, b_ref, res_ref, g_ref, *rest):
        xv = res_ref[...] + jnp.dot(a_ref[...], b_ref[...], preferred_element_type=F32)
        gv = g_ref[...]
        r = lax.rsqrt(jnp.mean(xv * xv, axis=-1, keepdims=True) + RMS_EPS)
        xhat = xv * r
        if not head:
            xo_ref, h_ref = rest
            xo_ref[...] = xv
            h_ref[...] = (xhat * gv).astype(BF16)
            return
        t_ref, dx_ref, dg_ref, ls_ref = rest
        i = pl.program_id(0)
        e = xhat * gv - t_ref[...]
        dy = e * (1.0 / D)
        gy = dy * gv
        dx_ref[...] = r * (gy - xhat * jnp.mean(gy * xhat, axis=-1, keepdims=True))
        dgp = jnp.sum(dy * xhat, axis=0, keepdims=True)
        lsp = jnp.sum(e * e, axis=0, keepdims=True)

        @pl.when(i == 0)
        def _():
            dg_ref[...] = dgp
            ls_ref[...] = lsp

        @pl.when(i > 0)
        def _():
            dg_ref[...] += dgp
            ls_ref[...] += lsp

    row = pl.BlockSpec((tm, D), lambda i: (i, 0))
    vec = pl.BlockSpec((1, D), lambda i: (0, 0))
    in_specs = [pl.BlockSpec((tm, K), lambda i: (i, 0)),
                pl.BlockSpec((K, D), lambda i: b_off, pipeline_mode=pl.Buffered(1)), row, vec]
    if head:
        return pl.pallas_call(
            kern, grid=(M // tm,), in_specs=in_specs + [row], out_specs=[row, vec, vec],
            out_shape=[jax.ShapeDtypeStruct((M, D), F32), jax.ShapeDtypeStruct((1, D), F32),
                       jax.ShapeDtypeStruct((1, D), F32)],
            compiler_params=_cparams(1), name=name)(a, b, res, g, tgt)
    return pl.pallas_call(
        kern, grid=(M // tm,), in_specs=in_specs, out_specs=[row, row],
        out_shape=[jax.ShapeDtypeStruct((M, D), F32), jax.ShapeDtypeStruct((M, D), BF16)],
        compiler_params=_cparams(1), name=name)(a, b, res, g)


def _rms_fwd(x, g, name, deps=()):
    S = x.shape[0]
    tr = 512

    def kern(x_ref, g_ref, *rest):
        h_ref = rest[-1]
        xv = x_ref[...]
        r = lax.rsqrt(jnp.mean(xv * xv, axis=-1, keepdims=True) + RMS_EPS)
        h_ref[...] = (xv * r * g_ref[...]).astype(BF16)

    return pl.pallas_call(
        kern, grid=(S // tr,),
        in_specs=[pl.BlockSpec((tr, D), lambda i: (i, 0)), pl.BlockSpec((1, D), lambda i: (0, 0))]
        + [pl.BlockSpec(memory_space=pl.ANY)] * len(deps),
        out_specs=pl.BlockSpec((tr, D), lambda i: (i, 0)),
        out_shape=jax.ShapeDtypeStruct((S, D), BF16), compiler_params=_cparams(1), name=name)(x, g, *deps)


def _chunks_put(scr, val):
    for c in range(scr.shape[0]):
        scr[c] = val[:, c * 128:(c + 1) * 128]


def _chunks_get(scr):
    return jnp.concatenate([scr[c] for c in range(scr.shape[0])], axis=1)


def _chunks_rows(scr, r, n, dil):
    return jnp.concatenate([scr.at[c][pl.ds(r, n, stride=dil), :] for c in range(scr.shape[0])], axis=1)


def _chunks_add_rows(scr, val, r, n, dil, accumulate):
    for c in range(scr.shape[0]):
        rows = pl.ds(r, n, stride=dil)
        piece = val[:, c * 128:(c + 1) * 128]
        tile = scr.at[c]
        tile[rows, :] = tile[rows, :] + piece if accumulate else piece


def _rms_fwd_folded(x, g, name, deps=()):
    S = x.shape[0]
    tr = 512
    dils = DILS[1:]

    def kern(x_ref, g_ref, *rest):
        outs, scr = rest[len(deps):-1], rest[-1]
        xv = x_ref[...]
        r = lax.rsqrt(jnp.mean(xv * xv, axis=-1, keepdims=True) + RMS_EPS)
        h = (xv * r * g_ref[...]).astype(BF16)
        outs[0][...] = h
        _chunks_put(scr, h.astype(F32))
        for o_ref, dil in zip(outs[1:], dils):
            for res in range(dil):
                o_ref[res] = _chunks_rows(scr, res, tr // dil, dil).astype(BF16)

    return pl.pallas_call(
        kern, grid=(S // tr,),
        in_specs=[pl.BlockSpec((tr, D), lambda i: (i, 0)), pl.BlockSpec((1, D), lambda i: (0, 0))]
        + [pl.BlockSpec(memory_space=pl.ANY)] * len(deps),
        out_specs=[pl.BlockSpec((tr, D), lambda i: (i, 0))]
        + [pl.BlockSpec((dil, tr // dil, D), lambda i: (0, i, 0)) for dil in dils],
        out_shape=[jax.ShapeDtypeStruct((S, D), BF16)]
        + [jax.ShapeDtypeStruct((dil, S // dil, D), BF16) for dil in dils],
        scratch_shapes=[pltpu.VMEM((D // 128, tr, 128), F32)],
        compiler_params=_cparams(1), name=name)(x, g, *deps)


def _rms_bwd(dh, x, g, dres, name, folded=()):
    S = x.shape[0]
    tr = 512
    nf = len(folded)

    def kern(dh_ref, *rest):
        f_refs = rest[:nf]
        x_ref, g_ref, dres_ref, dx_ref, dg_ref = rest[nf:nf + 5]
        i = pl.program_id(0)
        xv = x_ref[...]
        if nf:
            acc_ref = rest[nf + 5]
            _chunks_put(acc_ref, dh_ref[...].astype(F32))
            for f_ref in f_refs:
                dil = f_ref.shape[0]
                for res in range(dil):
                    _chunks_add_rows(acc_ref, f_ref[res], res, tr // dil, dil, True)
            dhv = _chunks_get(acc_ref)
        else:
            dhv = dh_ref[...].astype(F32)
        r = lax.rsqrt(jnp.mean(xv * xv, axis=-1, keepdims=True) + RMS_EPS)
        xhat = xv * r
        gy = dhv * g_ref[...]
        dx_ref[...] = dres_ref[...] + r * (gy - xhat * jnp.mean(gy * xhat, axis=-1, keepdims=True))
        part = jnp.sum(dhv * xhat, axis=0, keepdims=True)

        @pl.when(i == 0)
        def _():
            dg_ref[...] = part

        @pl.when(i > 0)
        def _():
            dg_ref[...] += part

    row = pl.BlockSpec((tr, D), lambda i: (i, 0))
    vec = pl.BlockSpec((1, D), lambda i: (0, 0))
    fspecs = [pl.BlockSpec((f.shape[0], tr // f.shape[0], D), lambda i: (0, i, 0)) for f in folded]
    return pl.pallas_call(
        kern, grid=(S // tr,), in_specs=[row] + fspecs + [row, vec, row], out_specs=[row, vec],
        out_shape=[jax.ShapeDtypeStruct((S, D), F32), jax.ShapeDtypeStruct((1, D), F32)],
        scratch_shapes=[pltpu.VMEM((D // 128, tr, 128), F32)] if nf else [],
        compiler_params=_cparams(1), name=name)(dh, *folded, x, g, dres)


def _ffn_up(h, G, name):
    S = h.shape[0]
    tm = 512
    nj = DFF // HCH

    def kern(h_ref, w_ref, gu_ref, act_ref):
        gu = lax.dot_general(h_ref[...], w_ref[...], (((1,), (1,)), ((), ())), preferred_element_type=F32)
        gu_ref[...] = gu.astype(BF16)
        gate = gu[:, :HCH]
        up = gu[:, HCH:]
        act_ref[...] = (gate * jax.nn.sigmoid(gate) * up).astype(BF16)

    return pl.pallas_call(
        kern, grid=(nj, S // tm),
        in_specs=[pl.BlockSpec((tm, D), lambda j, i: (i, 0)),
                  pl.BlockSpec((2 * HCH, D), lambda j, i: (j, 0))],
        out_specs=[pl.BlockSpec((tm, 2 * HCH), lambda j, i: (i, j)),
                   pl.BlockSpec((tm, HCH), lambda j, i: (i, j))],
        out_shape=[jax.ShapeDtypeStruct((S, 2 * DFF), BF16), jax.ShapeDtypeStruct((S, DFF), BF16)],
        compiler_params=_cparams(2), name=name)(h, G)


def _ffn_down_bwd(dx, G, gu, name):
    S = dx.shape[0]
    tm = 512
    nj = DFF // HCH

    def kern(dx_ref, w_ref, gu_ref, o_ref):
        dxv = dx_ref[...].astype(BF16)
        for j in range(nj):
            c0 = 2 * HCH * j
            dact = lax.dot_general(dxv, w_ref[HCH * j:HCH * (j + 1), :], (((1,), (1,)), ((), ())),
                                   preferred_element_type=F32)
            gate = gu_ref[:, c0:c0 + HCH].astype(F32)
            up = gu_ref[:, c0 + HCH:c0 + 2 * HCH].astype(F32)
            sig = jax.nn.sigmoid(gate)
            silu = gate * sig
            o_ref[:, c0:c0 + HCH] = (dact * up * (sig * (1.0 + gate * (1.0 - sig)))).astype(BF16)
            o_ref[:, c0 + HCH:c0 + 2 * HCH] = (dact * silu).astype(BF16)

    row = pl.BlockSpec((tm, 2 * DFF), lambda i: (i, 0))
    return pl.pallas_call(
        kern, grid=(S // tm,),
        in_specs=[pl.BlockSpec((tm, D), lambda i: (i, 0)),
                  pl.BlockSpec((DFF, D), lambda i: (0, 0), pipeline_mode=pl.Buffered(1)), row],
        out_specs=row, out_shape=jax.ShapeDtypeStruct((S, 2 * DFF), BF16),
        compiler_params=_cparams(1), name=name)(dx, G, gu)


def _trail(u, *, backward, name):
    S = u.shape[0]

    def kern(u_ref, o_ref):
        g = pl.program_id(0)
        for grp in range(POOL_G):
            @pl.when(g == grp)
            def _(grp=grp):
                uv = u_ref[...].astype(F32)
                row = lax.broadcasted_iota(jnp.int32, uv.shape, 0)
                cnt = jnp.minimum(row + 1, 2 << grp).astype(F32)
                s = uv / cnt if backward else uv
                for k in (1, 2, 4, 8)[:grp + 1]:
                    if backward:
                        sh = jnp.where(row < S - k, pltpu.roll(s, S - k, 0), 0.0)
                    else:
                        sh = jnp.where(row >= k, pltpu.roll(s, k, 0), 0.0)
                    s = s + sh
                if backward:
                    o_ref[...] = (s - uv).astype(BF16)
                else:
                    o_ref[...] = (s / cnt - uv).astype(BF16)

    blk = pl.BlockSpec((S, PGD), lambda g: (0, g))
    return pl.pallas_call(
        kern, grid=(POOL_G,), in_specs=[blk], out_specs=blk,
        out_shape=jax.ShapeDtypeStruct((S, D), BF16), compiler_params=_cparams(1), name=name)(u)


def _pool_out(yd, G, scale, xres):
    S = yd.shape[0]
    tm = min(S, 4096)

    def kern(y_ref, w_ref, s_ref, x_ref, o_ref):
        z = jnp.dot(y_ref[...], w_ref[...], preferred_element_type=F32)
        o_ref[...] = x_ref[...] + z * s_ref[...]

    tile = pl.BlockSpec((tm, PGD), lambda i, g: (i, g))
    return pl.pallas_call(
        kern, grid=(S // tm, POOL_G),
        in_specs=[tile, pl.BlockSpec((PGD, PGD), lambda i, g: (0, g)),
                  pl.BlockSpec((1, PGD), lambda i, g: (0, g)), tile],
        out_specs=tile, out_shape=jax.ShapeDtypeStruct((S, D), F32),
        compiler_params=_cparams(2), name="pool_out")(yd, G, scale, xres)


def _pool_out_bwd(dz, yd, G, scale):
    S = yd.shape[0]
    tm = min(S, 4096)
    ni = S // tm

    def kern(dz_ref, y_ref, w_ref, s_ref, dy_ref, ds_ref, dw_ref, acc_ref):
        i = pl.program_id(1)
        dzv = dz_ref[...]
        yv = y_ref[...]
        wv = w_ref[...]
        zraw = jnp.dot(yv, wv, preferred_element_type=F32)
        dsp = jnp.sum(dzv * zraw, axis=0, keepdims=True)
        dzr = (dzv * s_ref[...]).astype(BF16)
        dy_ref[...] = lax.dot_general(dzr, wv, (((1,), (1,)), ((), ())), preferred_element_type=F32)
        dwp = lax.dot_general(yv, dzr, (((0,), (0,)), ((), ())), preferred_element_type=F32)

        @pl.when(i == 0)
        def _():
            ds_ref[...] = dsp
            acc_ref[...] = dwp

        @pl.when(i > 0)
        def _():
            ds_ref[...] += dsp
            acc_ref[...] += dwp

        @pl.when(i == ni - 1)
        def _():
            dw_ref[...] = acc_ref[...].astype(BF16)

    tile = pl.BlockSpec((tm, PGD), lambda g, i: (i, g))
    return pl.pallas_call(
        kern, grid=(POOL_G, ni),
        in_specs=[tile, tile, pl.BlockSpec((PGD, PGD), lambda g, i: (0, g)),
                  pl.BlockSpec((1, PGD), lambda g, i: (0, g))],
        out_specs=[tile, pl.BlockSpec((1, PGD), lambda g, i: (0, g)),
                   pl.BlockSpec((PGD, PGD), lambda g, i: (0, g))],
        out_shape=[jax.ShapeDtypeStruct((S, D), F32), jax.ShapeDtypeStruct((1, D), F32),
                   jax.ShapeDtypeStruct((PGD, D), BF16)],
        scratch_shapes=[pltpu.VMEM((PGD, PGD), F32)],
        compiler_params=_cparams(2), name="pool_out_bwd")(dz, yd, G, scale)


def _bias_table():
    qi = jnp.arange(QB)[:, None]
    ki = jnp.arange(2 * QB)[None, :]
    delta = QB + qi - ki
    inband = (delta >= 0) & (delta <= QB)
    n = NGROUPS * HEADS
    slopes = jnp.exp2(-8.0 * jnp.arange(1, n + 1, dtype=F32) / n).reshape(NGROUPS, HEADS)
    dil = jnp.asarray(DILS, F32)
    bias = -slopes[:, :, None, None] * (delta.astype(F32)[None, None] * dil[:, None, None, None])
    return jnp.where(inband[None, None], bias, NEG)


def _attn_fwd(qkv_f, bias, nb, name):
    S = qkv_f.shape[0]
    nblk = S // QB
    scale = HD ** -0.5

    def kern(q_ref, k2_ref, kp_ref, v2_ref, vp_ref, b_ref, o_ref, l_ref, s_scr, p_scr, r_scr):
        s_id = pl.program_id(0)
        col = lax.broadcasted_iota(jnp.int32, (QB, 2 * QB), 1)
        lane = lax.broadcasted_iota(jnp.int32, (QB, HD), 1)

        def keys(sub, cur2_ref, prev_ref, sl):
            if sub:
                return cur2_ref[:, sl]
            return jnp.concatenate([prev_ref[:, sl], cur2_ref[0:QB, sl]], axis=0)

        for sub in range(2):
            for h in range(HEADS):
                sl = slice(h * HD, (h + 1) * HD)
                s_scr[sub * HEADS + h] = lax.dot_general(
                    q_ref[sub * QB:(sub + 1) * QB, sl], keys(sub, k2_ref, kp_ref, sl), (((1,), (1,)), ((), ())),
                    preferred_element_type=F32)
        for sub in range(2):
            has_prev = jnp.bitwise_and(2 * s_id + sub, nb - 1) != 0
            dead = jnp.logical_and(col < QB, jnp.logical_not(has_prev))
            lse_all = jnp.zeros((QB, HD), F32)
            for h in range(HEADS):
                u = sub * HEADS + h
                s = s_scr[u] * scale + b_ref[h]
                s = jnp.where(dead, NEG, s)
                m = jnp.max(s, axis=-1, keepdims=True)
                p = jnp.exp(s - m)
                den = jnp.sum(p, axis=-1, keepdims=True)
                p_scr[u] = p.astype(BF16)
                r_scr[u] = jnp.broadcast_to(1.0 / den, (QB, HD))
                lse_all = jnp.where(lane == h, m + jnp.log(den), lse_all)
            l_ref[sub * QB:(sub + 1) * QB, :] = lse_all
        for sub in range(2):
            for h in range(HEADS):
                u = sub * HEADS + h
                sl = slice(h * HD, (h + 1) * HD)
                o = jnp.dot(p_scr[u], keys(sub, v2_ref, vp_ref, sl), preferred_element_type=F32) * r_scr[u]
                o_ref[sub * QB:(sub + 1) * QB, sl] = o.astype(BF16)

    def pair(colblk):
        return pl.BlockSpec((2 * QB, D), lambda s: (s, colblk))

    def prev(colblk):
        return pl.BlockSpec((QB, D), lambda s: (jnp.maximum(2 * s - 1, 0), colblk))

    return pl.pallas_call(
        kern, grid=(nblk // 2,),
        in_specs=[pair(0), pair(1), prev(1), pair(2), prev(2), pl.BlockSpec((HEADS, QB, 2 * QB), lambda s: (0, 0, 0))],
        out_specs=[pl.BlockSpec((2 * QB, D), lambda s: (s, 0)), pl.BlockSpec((2 * QB, HD), lambda s: (s, 0))],
        out_shape=[jax.ShapeDtypeStruct((S, D), BF16), jax.ShapeDtypeStruct((S, HD), F32)],
        scratch_shapes=[pltpu.VMEM((2 * HEADS, QB, 2 * QB), F32), pltpu.VMEM((2 * HEADS, QB, 2 * QB), BF16),
                        pltpu.VMEM((2 * HEADS, QB, HD), F32)],
        compiler_params=_cparams(1), name=name)(qkv_f, qkv_f, qkv_f, qkv_f, qkv_f, bias)


def _natural(ref, scr, tm):
    dil = ref.shape[0]
    for res in range(dil):
        _chunks_add_rows(scr, ref[res].astype(F32), res, tm // dil, dil, False)
    return _chunks_get(scr)


def _attn_merge(os, lses):
    S = os[0].shape[0]
    tm = 512

    def kern(o0, o1, o2, l0, l1, l2, om_ref, lm_ref, ls1, ls2, os1, os2):
        la = l0[...]
        lb = _natural(l1, ls1, tm)
        lc = _natural(l2, ls2, tm)
        m = jnp.maximum(jnp.maximum(la, lb), lc)
        e0, e1, e2 = jnp.exp(la - m), jnp.exp(lb - m), jnp.exp(lc - m)
        tot = e0 + e1 + e2
        lm_ref[...] = m + jnp.log(tot)
        w0, w1, w2 = e0 / tot, e1 / tot, e2 / tot
        for res in range(o1.shape[0]):
            _chunks_add_rows(os1, o1[res].astype(F32), res, tm // o1.shape[0], o1.shape[0], False)
        for res in range(o2.shape[0]):
            _chunks_add_rows(os2, o2[res].astype(F32), res, tm // o2.shape[0], o2.shape[0], False)
        for h in range(HEADS):
            sl = slice(h * HD, (h + 1) * HD)
            acc = w0[:, h:h + 1] * o0[:, sl].astype(F32) + w1[:, h:h + 1] * os1[h] + w2[:, h:h + 1] * os2[h]
            om_ref[:, sl] = acc.astype(BF16)

    def spec(a, c):
        if a.ndim == 2:
            return pl.BlockSpec((tm, c), lambda i: (i, 0))
        return pl.BlockSpec((a.shape[0], tm // a.shape[0], c), lambda i: (0, i, 0))

    return pl.pallas_call(
        kern, grid=(S // tm,),
        in_specs=[spec(a, D) for a in os] + [spec(a, HD) for a in lses],
        out_specs=[pl.BlockSpec((tm, D), lambda i: (i, 0)), pl.BlockSpec((tm, HD), lambda i: (i, 0))],
        out_shape=[jax.ShapeDtypeStruct((S, D), BF16), jax.ShapeDtypeStruct((S, HD), F32)],
        scratch_shapes=[pltpu.VMEM((1, tm, HD), F32), pltpu.VMEM((1, tm, HD), F32),
                        pltpu.VMEM((HEADS, tm, HD), F32), pltpu.VMEM((HEADS, tm, HD), F32)],
        compiler_params=_cparams(1), name="attn_merge")(*os, *lses)


def _attn_bwd_prep(do, o, lse):
    S = o.shape[0]
    tm = 512
    dils = DILS[1:]

    def kern(do_ref, o_ref, l_ref, *rest):
        do_outs, l_outs, d_outs = rest[0:3], rest[3:5], rest[5:8]
        do_scr, l_scr, d_scr = rest[8:11]
        lane = lax.broadcasted_iota(jnp.int32, (tm, HD), 1)
        acc = jnp.zeros((tm, HD), F32)
        for h in range(HEADS):
            sl = slice(h * HD, (h + 1) * HD)
            prod = do_ref[:, sl] * o_ref[:, sl].astype(F32)
            acc = jnp.where(lane == h, jnp.sum(prod, axis=-1, keepdims=True), acc)
        d_scr[0] = acc
        l_scr[0] = l_ref[...]
        _chunks_put(do_scr, do_ref[...])
        do_outs[0][...] = do_ref[...].astype(BF16)
        d_outs[0][...] = acc
        for j, dil in enumerate(dils):
            for res in range(dil):
                n = tm // dil
                do_outs[1 + j][res] = _chunks_rows(do_scr, res, n, dil).astype(BF16)
                l_outs[j][res] = _chunks_rows(l_scr, res, n, dil)
                d_outs[1 + j][res] = _chunks_rows(d_scr, res, n, dil)

    def nat(c):
        return pl.BlockSpec((tm, c), lambda i: (i, 0))

    def fol(dil, c):
        return pl.BlockSpec((dil, tm // dil, c), lambda i: (0, i, 0))

    def shapes(c, dt, with_natural):
        first = [jax.ShapeDtypeStruct((S, c), dt)] if with_natural else []
        return first + [jax.ShapeDtypeStruct((dil, S // dil, c), dt) for dil in dils]

    outs = pl.pallas_call(
        kern, grid=(S // tm,), in_specs=[nat(D), nat(D), nat(HD)],
        out_specs=[nat(D)] + [fol(dil, D) for dil in dils] + [fol(dil, HD) for dil in dils]
        + [nat(HD)] + [fol(dil, HD) for dil in dils],
        out_shape=shapes(D, BF16, True) + shapes(HD, F32, False) + shapes(HD, F32, True),
        scratch_shapes=[pltpu.VMEM((HEADS, tm, HD), F32), pltpu.VMEM((1, tm, HD), F32), pltpu.VMEM((1, tm, HD), F32)],
        compiler_params=_cparams(1), name="attn_bwd_prep")(do, o, lse)
    return outs[0:3], [lse] + list(outs[3:5]), outs[5:8]


def _attn_bwd(qkv_f, do_f, lse_f, delta_f, bias, nb, name):
    S = qkv_f.shape[0]
    nblk = S // QB
    scale = HD ** -0.5

    npair = nblk // 2

    def kern(q_ref, k2_ref, kp_ref, v2_ref, vp_ref, do_ref, l_ref, d_ref, b_ref, out_ref, dq_c, dk_c, dv_c,
             s_scr, dp_scr, ds_scr, p_scr):
        s_id = pl.program_id(0)

        @pl.when(s_id == 0)
        def _():
            dq_c[...] = jnp.zeros_like(dq_c)
            dk_c[...] = jnp.zeros_like(dk_c)
            dv_c[...] = jnp.zeros_like(dv_c)

        @pl.when(s_id == npair)
        def _():
            out_ref[:, 0:D] = dq_c[...].astype(BF16)
            out_ref[:, D:2 * D] = dk_c[...].astype(BF16)
            out_ref[:, 2 * D:3 * D] = dv_c[...].astype(BF16)

        def keys(sub, cur2_ref, prev_ref, sl):
            if sub:
                return cur2_ref[:, sl]
            return jnp.concatenate([prev_ref[:, sl], cur2_ref[0:QB, sl]], axis=0)

        @pl.when(s_id < npair)
        def _():
            col = lax.broadcasted_iota(jnp.int32, (QB, 2 * QB), 1)
            out_ref[:, 0:D] = dq_c[...].astype(BF16)
            for sub in range(2):
                rows = slice(sub * QB, (sub + 1) * QB)
                for h in range(HEADS):
                    sl = slice(h * HD, (h + 1) * HD)
                    u = sub * HEADS + h
                    s_scr[u] = lax.dot_general(q_ref[rows, sl], keys(sub, k2_ref, kp_ref, sl),
                                               (((1,), (1,)), ((), ())), preferred_element_type=F32)
                    dp_scr[u] = lax.dot_general(do_ref[rows, sl], keys(sub, v2_ref, vp_ref, sl),
                                                (((1,), (1,)), ((), ())), preferred_element_type=F32)
            for sub in range(2):
                rows = slice(sub * QB, (sub + 1) * QB)
                has_prev = jnp.bitwise_and(2 * s_id + sub, nb - 1) != 0
                dead = jnp.logical_and(col < QB, jnp.logical_not(has_prev))
                lv = l_ref[rows, :]
                dv_ = d_ref[rows, :]
                for h in range(HEADS):
                    u = sub * HEADS + h
                    s = s_scr[u] * scale + b_ref[h]
                    s = jnp.where(dead, NEG, s)
                    p = jnp.exp(s - lv[:, h:h + 1])
                    ds_scr[u] = (p * (dp_scr[u] - dv_[:, h:h + 1]) * scale).astype(BF16)
                    p_scr[u] = p.astype(BF16)
            for h in range(HEADS):
                sl = slice(h * HD, (h + 1) * HD)
                parts = []
                for sub in range(2):
                    rows = slice(sub * QB, (sub + 1) * QB)
                    u = sub * HEADS + h
                    ds = ds_scr[u]
                    dq_c[rows, sl] = jnp.dot(ds, keys(sub, k2_ref, kp_ref, sl), preferred_element_type=F32)
                    dkk = lax.dot_general(ds, q_ref[rows, sl], (((0,), (0,)), ((), ())), preferred_element_type=F32)
                    dvv = lax.dot_general(p_scr[u], do_ref[rows, sl], (((0,), (0,)), ((), ())),
                                          preferred_element_type=F32)
                    parts.append((dkk, dvv))
                for which, carry, base in ((0, dk_c, D), (1, dv_c, 2 * D)):
                    first, second = parts[0][which], parts[1][which]
                    cols = slice(base + h * HD, base + (h + 1) * HD)
                    out_ref[0:QB, cols] = carry[0:QB, sl].astype(BF16)
                    out_ref[QB:2 * QB, cols] = (carry[QB:2 * QB, sl] + first[:QB]).astype(BF16)
                    carry[0:QB, sl] = first[QB:] + second[:QB]
                    carry[QB:2 * QB, sl] = second[QB:]

    last = npair - 1

    def pair(colblk, c):
        return pl.BlockSpec((2 * QB, c), lambda s: (jnp.minimum(s, last), colblk))

    def prev(colblk):
        return pl.BlockSpec((QB, D), lambda s: (jnp.maximum(2 * jnp.minimum(s, last) - 1, 0), colblk))

    return pl.pallas_call(
        kern, grid=(npair + 1,),
        in_specs=[pair(0, D), pair(1, D), prev(1), pair(2, D), prev(2), pair(0, D), pair(0, HD), pair(0, HD),
                  pl.BlockSpec((HEADS, QB, 2 * QB), lambda s: (0, 0, 0))],
        out_specs=pl.BlockSpec((2 * QB, 3 * D), lambda s: (jnp.maximum(s - 1, 0), 0)),
        out_shape=jax.ShapeDtypeStruct((S, 3 * D), BF16),
        scratch_shapes=[pltpu.VMEM((2 * QB, D), F32), pltpu.VMEM((2 * QB, D), F32), pltpu.VMEM((2 * QB, D), F32),
                        pltpu.VMEM((2 * HEADS, QB, 2 * QB), F32), pltpu.VMEM((2 * HEADS, QB, 2 * QB), F32),
                        pltpu.VMEM((2 * HEADS, QB, 2 * QB), BF16), pltpu.VMEM((2 * HEADS, QB, 2 * QB), BF16)],
        compiler_params=_cparams(1), name=name)(qkv_f, qkv_f, qkv_f, qkv_f, qkv_f, do_f, lse_f, delta_f, bias)


def _local_step(x, tgt, comm, attn_norm, ffn_norm, final_norm):
    S = x.shape[0]
    bias = _bias_table()
    g_attn = attn_norm.reshape(1, D)
    g_f0 = ffn_norm[0:1]
    g_f1 = ffn_norm[1:2]
    g_fin = final_norm.reshape(1, D)
    W = {}

    def ffn_fwd(xin, h, l, next_gain, target=None):
        gu, act = _ffn_up(h, W[f"gu{l}"], f"ffn_up{l}")
        return gu, act, _mm_res_norm(act, W[f"d{l}"], xin, next_gain, K=DFF, tm=512, tgt=target,
                                     name=f"ffn_down{l}")

    def ffn_bwd(dxo, xin, gain, h, gu, act, l, rs_group):
        dgu = _ffn_down_bwd(dxo, W[f"d{l}"], gu, f"ffn_down_bwd{l}")
        gw_d = _mm(act, dxo, mode="tn", M=DFF, N=D, K=S, tm=HCH, tn=D, tk=2048, out_dtype=BF16, name=f"gw_d{l}")
        gw_gu = _mm(dgu, h, mode="tn", M=2 * DFF, N=D, K=S, tm=HCH, tn=D, tk=2048, out_dtype=BF16, name=f"gw_gu{l}")
        token = comm.send_grads(rs_group, {f"d{l}": gw_d, f"gu{l}": gw_gu})
        return _mm_rms_bwd(dgu, W[f"gu{l}"], xin, gain, dxo, mode="nn", M=S, K=2 * DFF, tm=512, deps=(token,),
                           name=f"ffn_up_bwd{l}")

    nbs = [S // QB // dil for dil in DILS]
    hf = _rms_fwd_folded(x, g_attn, "rms_attn", deps=comm.ag_tokens)
    hf = [h.reshape(S, D) for h in hf]
    W.update(comm.weights(0, hf[0]))
    qkv_f, o_f, lse_f = [], [], []
    for g, dil in enumerate(DILS):
        qkv_f.append(_mm(hf[g], W["qkv"], mode="nt", M=S, N=3 * D, K=D, tm=2048, tn=1024, tk=D, out_dtype=BF16,
                         b_off=(3 * g, 0), name=f"qkv_proj{g}"))
        og, lg = _attn_fwd(qkv_f[g], bias[g], nbs[g], f"attn_fwd{g}")
        o_f.append(og if dil == 1 else og.reshape(dil, S // dil, D))
        lse_f.append(lg if dil == 1 else lg.reshape(dil, S // dil, HD))
    passing = [comm.pass_on(1, tuple(o_f)), comm.pass_on(2, tuple(o_f))]
    (o_f, lse_f), passing = lax.optimization_barrier(((o_f, lse_f), passing))
    o, lse = _attn_merge(o_f, lse_f)
    W.update(comm.weights(1, (o, passing[0])))
    x1, h1 = _mm_res_norm(o, W["wo"], x, g_f0, K=D, tm=1024, name="attn_out")
    pv = W["pv"].reshape(NDEV, 8, 128)
    pool_norm, pool_scale = pv[:, 0, :].reshape(1, D), pv[:, 1, :].reshape(1, D)
    gu0, act0, (x2, h2) = ffn_fwd(x1, h1, 0, pool_norm)

    W.update(comm.weights(2, (x2, passing[1])))
    u = _mm(h2, W["wpi"], mode="nn", M=S, N=D, K=D, tm=1024, tn=D, tk=D, out_dtype=F32, name="pool_in")
    yd = _trail(u, backward=False, name="trail_fwd")
    x3 = _pool_out(yd, W["pg"], pool_scale, x2)
    h3 = _rms_fwd(x3, g_f1, "rms_ffn1")
    gu1, act1, (dx4, d_fin, lossvec) = ffn_fwd(x3, h3, 1, g_fin, target=tgt)

    dx3, d_f1 = ffn_bwd(dx4, x3, g_f1, h3, gu1, act1, 1, 0)
    dyd, d_scale, gw_pg = _pool_out_bwd(dx3, yd, W["pg"], pool_scale)
    du = _trail(dyd, backward=True, name="trail_bwd")
    gw_pi = _mm(h2, du, mode="tn", M=D, N=D, K=S, tm=D, tn=D, tk=S, out_dtype=BF16, name="gw_pi")
    token = comm.send_grads(1, {"pg": gw_pg, "wpi": gw_pi})
    dx2, d_pool = _mm_rms_bwd(du, W["wpi"], x2, pool_norm, dx3, mode="nt", M=S, K=D, tm=1024, deps=(token,),
                              name="pool_in_bwd")
    dx1, d_f0 = ffn_bwd(dx2, x1, g_f0, h1, gu0, act0, 0, 2)

    gw_o = _mm(o, dx1, mode="tn", M=D, N=D, K=S, tm=D, tn=D, tk=2048, out_dtype=BF16, name="gw_o")
    do = _mm(dx1, W["wo"], mode="nt", M=S, N=D, K=D, tm=1024, tn=D, tk=D, out_dtype=F32, name="attn_out_bwd")
    do_f, lse_ff, delta_f = _attn_bwd_prep(do, o, lse)
    dqkv_f, gw_qkv = [], None
    for g in range(NGROUPS):
        dqkv_f.append(_attn_bwd(qkv_f[g], do_f[g].reshape(S, D), lse_ff[g].reshape(S, HD),
                                delta_f[g].reshape(S, HD), bias[g], nbs[g], f"attn_bwd{g}"))
        gw_qkv = _mm(dqkv_f[g], hf[g], mode="tn", M=3 * D, N=D, K=S, tm=1024, tn=D, tk=S, out_dtype=BF16,
                     out_rows=NGROUPS * 3 * D, out_off=3 * g, out_prev=gw_qkv, name=f"gw_qkv{g}")
    token = comm.send_grads_pairwise({"wo": gw_o, "qkv": gw_qkv})
    dh0_f = [None] * NGROUPS
    for g in reversed(range(NGROUPS)):
        dh0_f[g] = _mm(dqkv_f[g], W["qkv"], mode="nn", M=S, N=D, K=3 * D, tm=1024, tn=D, tk=3 * D, out_dtype=F32,
                       b_off=(g, 0), deps=(token,), name=f"qkv_proj_bwd{g}")
        if g == NGROUPS - 1:
            token = comm.pass_grads(dh0_f[g])
    folded = [dh0_f[g].reshape(dil, S // dil, D) for g, dil in enumerate(DILS) if dil > 1]
    grad_x, d_attn = _rms_bwd(dh0_f[0], x, g_attn, dx1, "rms_attn_bwd", folded=folded)

    vec = jnp.concatenate([d_attn, d_f0, d_f1, d_fin, d_pool, d_scale, lossvec, jnp.zeros((1, D), F32)], axis=0)
    return grad_x, vec


def _mesh_pos():
    x, y, c = lax.axis_index("x"), lax.axis_index("y"), lax.axis_index("c")
    return x, y, c, 4 * x + 2 * y + c


def _peer(x, y, c, k):
    kx, ky, kc = (k >> 2) & 1, (k >> 1) & 1, k & 1
    px = 1 - x if kx else x
    py = 1 - y if ky else y
    pc = 1 - c if kc else c
    return (px, py, pc), 4 * px + 2 * py + pc


ANY = pl.BlockSpec(memory_space=pl.ANY)


HBM = pl.BlockSpec(memory_space=pltpu.HBM)
SEMS = pl.BlockSpec(memory_space=pltpu.SEMAPHORE)
EFFECT = pltpu.SideEffectType.DATAFLOW_SIDE_EFFECTING
NPEER = NDEV - 1

AG_GROUPS = (("qkv",), ("wo", "gu0", "d0", "pv"), ("wpi", "pg", "gu1", "d1"))
AG_ORDER = tuple(n for grp in AG_GROUPS for n in grp)
RS_GROUPS = (("d1", "gu1"), ("pg", "wpi"), ("d0", "gu0"), ("wo", "qkv"))


def _hbm(a):
    return pltpu.with_memory_space_constraint(a, pltpu.HBM)


def _remote(src, dst, send, recv, peer):
    return pltpu.make_async_remote_copy(src_ref=src, dst_ref=dst, send_sem=send, recv_sem=recv, device_id=peer,
                                        device_id_type=pl.DeviceIdType.MESH)


def _bcast_all(v, name, deps=()):
    W = v.shape[1]
    nd = len(deps)

    def kern(v_ref, *rest):
        o_ref, send, recv, lsem = rest[nd:]
        x, y, c, me = _mesh_pos()
        own = pltpu.make_async_copy(v_ref, o_ref.at[me], lsem)
        own.start()
        cps = [_remote(v_ref, o_ref.at[me], send.at[k - 1], recv.at[k - 1], _peer(x, y, c, k)[0])
               for k in range(1, NDEV)]
        for cp in cps:
            cp.start()
        for cp in cps:
            cp.wait_recv()
            cp.wait_send()
        own.wait()

    return pl.pallas_call(
        kern, in_specs=[ANY] * (1 + nd), out_specs=ANY, out_shape=jax.ShapeDtypeStruct((NDEV, 8, W), F32),
        scratch_shapes=[pltpu.SemaphoreType.DMA((NPEER,)), pltpu.SemaphoreType.DMA((NPEER,)),
                        pltpu.SemaphoreType.DMA(())],
        name=name)(v, *deps)


ALL_KS = tuple(range(1, NDEV))
AG_KS1 = (1, 2, 4, 6)
AG_KS2 = (2, 4, 6)
RS_KS_PAIR = (1, 3, 5, 7)
RS_KS_CHIPS = (2, 4, 6)


def _split_start(srcs, src_of, lands, copy_refs, name, deps=(), ks=ALL_KS, to=None):
    ns, n, nd, nk = len(srcs), len(lands), len(deps), len(ks)

    def body(*refs):
        ins, land = refs[:ns], refs[ns:ns + n]
        send, recv = refs[ns + n + nd], refs[ns + n + nd + 1]
        token = refs[-1]
        x, y, c, me = _mesh_pos()
        for j in range(n):
            for i, k in enumerate(ks):
                _, pid = _peer(x, y, c, k)
                dest, _ = _peer(x, y, c, k if to is None else to)
                src, dst = copy_refs(j, (land[j] if src_of[j] is None else ins[src_of[j]]), land[j], me, pid, i)
                _remote(src, dst, send.at[j * nk + i], recv.at[j * nk + i], dest).start()
        token[...] = jnp.zeros_like(token)

    outs = pl.pallas_call(
        body, name=name,
        out_shape=(pltpu.SemaphoreType.DMA((n * nk,)), pltpu.SemaphoreType.DMA((n * nk,)))
        + tuple(pltpu.HBM(a.shape, a.dtype) for a in srcs) + tuple(pltpu.HBM(a.shape, a.dtype) for a in lands)
        + (jax.ShapeDtypeStruct((8, 128), F32),),
        in_specs=(HBM,) * (ns + n) + (ANY,) * nd,
        out_specs=(SEMS, SEMS) + (HBM,) * (ns + n) + (pl.BlockSpec(memory_space=pltpu.VMEM),),
        input_output_aliases={i: 2 + i for i in range(ns + n)},
        compiler_params=pltpu.CompilerParams(has_side_effects=EFFECT),
    )(*[_hbm(a) for a in srcs], *[_hbm(a) for a in lands], *deps)
    return outs[0], outs[1], list(outs[2:2 + ns]), list(outs[2 + ns:2 + ns + n]), outs[-1]


def _split_wait(srcs, src_of, lands, send, recv, sem_rows, wait_refs, after, name, ks=ALL_KS):
    ns, n, nk = len(srcs), len(lands), len(ks)
    after = tuple(after) if isinstance(after, (tuple, list)) else (after,)

    def body(*refs):
        ins, land = refs[:ns], refs[ns:ns + n]
        send_ref, recv_ref = refs[ns + n], refs[ns + n + 1]
        x, y, c, me = _mesh_pos()
        for j in range(n):
            for i, k in enumerate(ks):
                peer, _ = _peer(x, y, c, k)
                src, dst = wait_refs(j, (land[j] if src_of[j] is None else ins[src_of[j]]), land[j])
                sem = sem_rows[j] * nk + i
                cp = _remote(src, dst, send_ref.at[sem], recv_ref.at[sem], peer)
                cp.wait_send()
                cp.wait_recv()

    outs = pl.pallas_call(
        body, name=name,
        out_shape=tuple(pltpu.HBM(a.shape, a.dtype) for a in srcs) + tuple(pltpu.HBM(a.shape, a.dtype) for a in lands),
        in_specs=(HBM,) * (ns + n) + (SEMS, SEMS) + (ANY,) * len(after),
        out_specs=(HBM,) * (ns + n),
        input_output_aliases={i: i for i in range(ns + n)},
        compiler_params=pltpu.CompilerParams(has_side_effects=EFFECT),
    )(*srcs, *lands, send, recv, *after)
    return list(outs[:ns]), list(outs[ns:])


def _ag_dtype(name):
    return F32 if name == "pv" else BF16


def _ag_align(name):
    return 8 if name == "pv" else 16


def _place_transposed(w, me, name):
    rows = w.shape[1]
    nblk = rows // 128

    def kern(me_ref, w_ref, o_ref):
        o_ref[...] = w_ref[...].T.astype(BF16)

    grid_spec = pltpu.PrefetchScalarGridSpec(
        num_scalar_prefetch=1, grid=(nblk,),
        in_specs=[pl.BlockSpec((D, 128), lambda i, me_ref: (0, i))],
        out_specs=pl.BlockSpec((128, D), lambda i, me_ref: (me_ref[0] * nblk + i, 0)))
    return pl.pallas_call(
        kern, grid_spec=grid_spec, out_shape=jax.ShapeDtypeStruct((NDEV * rows, D), BF16),
        compiler_params=_cparams(1), name=name)(me.reshape(1).astype(jnp.int32), w)


class _Comm:
    def __init__(self, params, make_shards, me, placed):
        self.me = me
        self.ag_land, self.ag_sems, self.ag_tokens, self.ag_passing = {}, {}, (), {}
        self.rs = []
        deps = ()
        for part, names in enumerate((AG_GROUPS[0], AG_ORDER[len(AG_GROUPS[0]):])):
            rows = [SEC_ROWS[n] for n in names]
            if part == 0:
                lands = [placed[n] for n in names]
            else:
                params, deps = lax.optimization_barrier((params, deps))
                shards = make_shards(*params)
                lands = [lax.dynamic_update_slice(lax.empty((NDEV * r, shards[n].shape[1]), _ag_dtype(n)),
                                                  shards[n].astype(_ag_dtype(n)), (_shard_pos(n, me), 0))
                         for n, r in zip(names, rows)]

            def copy_refs(j, src, land, me, pid, i, names=names, rows=rows):
                own = land.at[pl.ds(pl.multiple_of(_shard_pos(names[j], me), _ag_align(names[j])), rows[j])]
                return own, own

            send, recv, _, lands, token = _split_start([], [None] * len(names), lands, copy_refs, f"ag_start{part}",
                                                       deps=deps, ks=AG_KS1)
            deps = (token,)
            self.ag_tokens += (token,)
            for j, n in enumerate(names):
                self.ag_land[n] = lands[j]
                self.ag_sems[n] = (send, recv, j)

    def pass_on(self, group, after):
        names = AG_GROUPS[group]
        send, recv = self.ag_sems[names[0]][:2]
        idx = [self.ag_sems[n][2] for n in names]
        rows = [SEC_ROWS[n] for n in names]
        none = [None] * len(names)

        def wait_refs(j, src, land):
            return land.at[pl.ds(0, rows[j])], land.at[pl.ds(0, rows[j])]

        _, lands = _split_wait([], none, [self.ag_land[n] for n in names], send, recv, idx,
                               wait_refs, after, f"ag_wait{group}", ks=AG_KS1)

        def copy_refs(j, src, land, me, pid, i):
            theirs = land.at[pl.ds(pl.multiple_of(_shard_pos(names[j], pid), _ag_align(names[j])), rows[j])]
            return theirs, theirs

        send, recv, _, lands, token = _split_start([], none, lands, copy_refs, f"ag_pass{group}", ks=AG_KS2, to=1)
        self.ag_passing[group] = (send, recv, lands, wait_refs)
        return token

    def weights(self, group, after):
        names = AG_GROUPS[group]
        if group not in self.ag_passing:
            after = self.pass_on(group, after)
        send, recv, lands, wait_refs = self.ag_passing[group]
        _, lands = _split_wait([], [None] * len(names), lands, send, recv, list(range(len(names))), wait_refs, after,
                               f"ag_pass_wait{group}", ks=AG_KS2)
        return dict(zip(names, lands))

    def send_grads(self, group, gws):
        names = RS_GROUPS[group]
        rows = [SEC_ROWS[n] for n in names]
        grads = [gws[n] for n in names]
        me = self.me
        lands = [lax.dynamic_update_slice(
            lax.empty((NDEV, r, D), BF16),
            lax.dynamic_slice(g, (_shard_pos(n, me), 0), (r, D))[None], (me, 0, 0))
            for n, r, g in zip(names, rows, grads)]

        def copy_refs(j, src, land, me, pid, i):
            return src.at[pl.ds(pl.multiple_of(_shard_pos(names[j], pid), 16), rows[j])], land.at[me]

        send, recv, srcs, lands, token = _split_start(grads, list(range(len(names))), lands, copy_refs,
                                                      f"rs_start{group}")
        self.rs.append((names, rows, send, recv, srcs, lands, ALL_KS))
        return token

    def send_grads_pairwise(self, gws):
        names = RS_GROUPS[-1]
        rows = [SEC_ROWS[n] for n in names]
        grads = [gws[n] for n in names]
        idx = list(range(len(names)))
        lands = [lax.empty((len(RS_KS_PAIR), r, D), BF16) for r in rows]

        def copy_refs(j, src, land, me, pid, i):
            return src.at[pl.ds(pl.multiple_of(_shard_pos(names[j], pid), 16), rows[j])], land.at[i]

        send, recv, srcs, lands, token = _split_start(grads, idx, lands, copy_refs, "rs_pair_start",
                                                      ks=RS_KS_PAIR, to=1)
        self.pair = (names, rows, send, recv, srcs, lands)
        return token

    def pass_grads(self, after):
        names, rows, send, recv, srcs, lands = self.pair
        idx = list(range(len(names)))
        me = self.me

        def wait_refs(j, src, land):
            return src.at[pl.ds(0, rows[j])], land.at[0]

        srcs, lands = _split_wait(srcs, idx, lands, send, recv, idx, wait_refs, after, "rs_pair_wait", ks=RS_KS_PAIR)
        sums = []
        for n, r, g, got in zip(names, rows, srcs, lands):
            mine = jnp.stack([lax.dynamic_slice(g, (_shard_pos(n, jnp.bitwise_xor(me, k)), 0), (r, D))
                              for k in (0,) + RS_KS_CHIPS])
            sums.append(_pair_sum(mine, got, f"rs_pair_sum_{n}"))
        lands = [lax.dynamic_update_slice(lax.empty(p.shape, BF16), p[0:1], (0, 0, 0)) for p in sums]

        def copy_refs(j, src, land, me, pid, i):
            return src.at[i + 1], land.at[i + 1]

        send, recv, sums, lands, token = _split_start(sums, idx, lands, copy_refs, f"rs_start{len(RS_GROUPS) - 1}",
                                                      ks=RS_KS_CHIPS)
        self.rs.append((names, rows, send, recv, sums, lands, RS_KS_CHIPS))
        return token

    def received(self, group, after):
        names, rows, send, recv, srcs, lands, ks = self.rs[group]
        whole = srcs[0].ndim == 2

        def wait_refs(j, src, land):
            return (src.at[pl.ds(0, rows[j])] if whole else src.at[0]), land.at[0]

        _, lands = _split_wait(srcs, list(range(len(names))), lands, send, recv, list(range(len(names))), wait_refs,
                               after, f"rs_wait{group}", ks=ks)
        return dict(zip(names, lands))


def _pair_sum(a, b, name):
    n, rows, _ = a.shape
    tr = 384 if rows % 384 == 0 else rows

    def kern(a_ref, b_ref, o_ref):
        o_ref[...] = (a_ref[...].astype(F32) + b_ref[...].astype(F32)).astype(BF16)

    blk = pl.BlockSpec((1, tr, D), lambda i, t: (i, t, 0))
    return pl.pallas_call(
        kern, grid=(n, rows // tr), in_specs=[blk, blk], out_specs=blk,
        out_shape=jax.ShapeDtypeStruct(a.shape, BF16), compiler_params=_cparams(2), name=name)(a, b)


def _sum_contributions(r_ref):
    g = r_ref[0].astype(F32)
    for slot in range(1, r_ref.shape[0]):
        g = g + r_ref[slot].astype(F32)
    return g


def _adam_math(g, w, m, v):
    c1 = 1.0 / (1.0 - ADAM_B1 ** ADAM_STEP)
    c2 = 1.0 / (1.0 - ADAM_B2 ** ADAM_STEP)
    mn = ADAM_B1 * m + (1.0 - ADAM_B1) * g
    vn = ADAM_B2 * v + (1.0 - ADAM_B2) * (g * g)
    return -ADAM_LR * ((mn * c1) / (jnp.sqrt(vn * c2) + ADAM_EPS) + ADAM_WD * w), mn, vn


def _adamw(R, w, m, v, *, tr, name, layer=None, prev=None):
    rows, C = w.shape[-2:]
    nprev = 0 if prev is None else 4

    def kern(r_ref, w_ref, m_ref, v_ref, *rest):
        g_out, d_out, m_out, v_out = rest[nprev:]
        g = _sum_contributions(r_ref)
        g_out[...] = g
        d_out[...], m_out[...], v_out[...] = _adam_math(g, w_ref[...], m_ref[...], v_ref[...])

    if layer is None:
        tile = pl.BlockSpec((tr, C), lambda i: (i, 0))
    else:
        tile = pl.BlockSpec((None, tr, C), lambda i: (layer, i, 0))
    shp = jax.ShapeDtypeStruct(w.shape, F32)
    return pl.pallas_call(
        kern, grid=(rows // tr,),
        in_specs=[pl.BlockSpec((R.shape[0], tr, C), lambda i: (0, i, 0)), tile, tile, tile]
        + [pl.BlockSpec(memory_space=pl.ANY)] * nprev,
        out_specs=[tile] * 4, out_shape=[shp] * 4,
        input_output_aliases={4 + k: k for k in range(nprev)},
        compiler_params=_cparams(1), name=name)(R, w, m, v, *(prev or ()))


def _adamw_pool_group(R, w, m, v):
    rows = SEC_ROWS["pg"]

    def kern(r_ref, w_ref, m_ref, v_ref, g_out, d_out, m_out, v_out):
        g = _sum_contributions(r_ref)
        g_out[0] = g
        d_out[0], m_out[0], v_out[0] = _adam_math(g, w_ref[0], m_ref[0], v_ref[0])

    blk = pl.BlockSpec((1, rows, PGD), lambda i: (i, 0, 0))
    shp = jax.ShapeDtypeStruct((POOL_G, rows, PGD), F32)
    return pl.pallas_call(
        kern, grid=(POOL_G,),
        in_specs=[pl.BlockSpec((NDEV, rows, PGD), lambda i: (0, 0, i)), blk, blk, blk],
        out_specs=[blk] * 4, out_shape=[shp] * 4, compiler_params=_cparams(1), name="adamw_pg")(R, w, m, v)


def _adamw_transposed(R, w, m, v, name):
    rows = R.shape[1]
    tr = 128

    def kern(r_ref, w_ref, m_ref, v_ref, g_out, d_out, m_out, v_out):
        g = _sum_contributions(r_ref).T
        g_out[...] = g
        d_out[...], m_out[...], v_out[...] = _adam_math(g, w_ref[...], m_ref[...], v_ref[...])

    tile = pl.BlockSpec((D, tr), lambda i: (0, i))
    shp = jax.ShapeDtypeStruct((D, rows), F32)
    return pl.pallas_call(
        kern, grid=(rows // tr,),
        in_specs=[pl.BlockSpec((R.shape[0], tr, D), lambda i: (0, i, 0)), tile, tile, tile],
        out_specs=[tile] * 4, out_shape=[shp] * 4, compiler_params=_cparams(1), name=name)(R, w, m, v)


def _pack_sections(w_qkv, w_attn_out, w_pool_in, w_pool_group, w_ffn_gate_up, w_ffn_down):
    pg = w_pool_group[0].transpose(1, 0, 2).reshape(SEC_ROWS["pg"], D)
    return {"qkv": w_qkv[0].T, "wo": w_attn_out[0], "wpi": w_pool_in[0], "gu0": w_ffn_gate_up[0].T,
            "gu1": w_ffn_gate_up[1].T, "d0": w_ffn_down[0], "d1": w_ffn_down[1], "pg": pg}


def _vec_pack(attn_norm, ffn_norm, final_norm, pool_norm_sh, pool_scale_sh, me):
    def place(sh):
        return lax.dynamic_update_slice(jnp.zeros((1, D), F32), sh, (0, me * 128))
    return jnp.concatenate([attn_norm, ffn_norm, final_norm.reshape(1, D), place(pool_norm_sh),
                            place(pool_scale_sh), jnp.zeros((2, D), F32)], axis=0)


def _vec_unpack(p, me):
    def take(r):
        return lax.dynamic_slice(p[r:r + 1], (0, me * 128), (1, 128))
    return p[0:1], p[1:3], p[3], take(4), take(5)


def kernel(x, attn_norm, w_qkv, w_attn_out, pool_norm, w_pool_in, w_pool_group, pool_scale, ffn_norm, w_ffn_gate_up, w_ffn_down, final_norm, loss_target, m_attn_norm, m_w_qkv, m_w_attn_out, m_pool_norm, m_w_pool_in, m_w_pool_group, m_pool_scale, m_ffn_norm, m_w_ffn_gate_up, m_w_ffn_down, m_final_norm, v_attn_norm, v_w_qkv, v_w_attn_out, v_pool_norm, v_w_pool_in, v_w_pool_group, v_pool_scale, v_ffn_norm, v_w_ffn_gate_up, v_w_ffn_down, v_final_norm):
    me = 4 * lax.axis_index("x") + 2 * lax.axis_index("y") + lax.axis_index("c")

    def make_shards(wq, wo, wpi, wpg, wgu, wd, pn, ps):
        shards = _pack_sections(wq, wo, wpi, wpg, wgu, wd)
        shards["pv"] = jnp.concatenate([pn, ps, jnp.zeros((6, 128), F32)], axis=0)
        return shards

    comm = _Comm((w_qkv, w_attn_out, w_pool_in, w_pool_group, w_ffn_gate_up, w_ffn_down, pool_norm, pool_scale),
                 make_shards, me, placed={"qkv": _place_transposed(w_qkv[0], me, "place_qkv")})

    grad_x, vec = _local_step(x[0], loss_target[0], comm, attn_norm, ffn_norm, final_norm)

    small = ((attn_norm, ffn_norm, final_norm, pool_norm, pool_scale),
             (m_attn_norm, m_ffn_norm, m_final_norm, m_pool_norm, m_pool_scale),
             (v_attn_norm, v_ffn_norm, v_final_norm, v_pool_norm, v_pool_scale))
    small, grad_x = lax.optimization_barrier((small, grad_x))
    vw, vm, vv = (_vec_pack(*s, me) for s in small)

    gu_t = [jnp.swapaxes(a, 1, 2) for a in (w_ffn_gate_up, m_w_ffn_gate_up, v_w_ffn_gate_up)]
    res = {}
    gu_res, d_res = None, None
    vec_out = None
    after = grad_x
    for group in range(len(RS_GROUPS)):
        if group == len(RS_GROUPS) - 1:
            VR = _bcast_all(vec, "exchange_vector_grads", deps=(after,))
            vec_out = _adamw(VR, vw, vm, vv, tr=8, name="adamw_vec")
            after = vec_out[0]
        for n, R in comm.received(group, after).items():
            if n in ("d0", "d1"):
                d_res = _adamw(R, w_ffn_down, m_w_ffn_down, v_w_ffn_down, tr=352, name=f"adamw_{n}",
                               layer=int(n[1]), prev=d_res)
                after = d_res[0]
            elif n in ("gu0", "gu1"):
                gu_res = _adamw(R, *gu_t, tr=352, name=f"adamw_{n}", layer=int(n[2]), prev=gu_res)
                after = gu_res[0]
            elif n == "pg":
                out = _adamw_pool_group(R, w_pool_group[0], m_w_pool_group[0], v_w_pool_group[0])
                res["pg"] = tuple(a[None] for a in out)
                after = out[0]
            elif n in ("wo", "wpi"):
                w, m, v = ((w_attn_out, m_w_attn_out, v_w_attn_out) if n == "wo"
                           else (w_pool_in, m_w_pool_in, v_w_pool_in))
                res[n] = _adamw(R, w[0], m[0], v[0], tr=128, name=f"adamw_{n}")
                res[n] = tuple(a[None] for a in res[n])
                after = res[n][0]
            else:
                out = _adamw_transposed(R, w_qkv[0], m_w_qkv[0], v_w_qkv[0], "adamw_qkv")
                res["qkv"] = tuple(a[None] for a in out)
                after = out[0]
    res["gu"] = tuple(jnp.swapaxes(a, 1, 2) for a in gu_res)
    res["d"] = tuple(d_res)

    outs = []
    for kind in range(4):
        an, fn, fin, pn, ps = _vec_unpack(vec_out[kind], me)
        outs.append((an, res["qkv"][kind], res["wo"][kind], pn, res["wpi"][kind], res["pg"][kind], ps, fn,
                     res["gu"][kind], res["d"][kind], fin))
    loss = 0.5 * jnp.sum(vec_out[0][6]) / D
    return (loss, grad_x[None]) + outs[0] + outs[1] + outs[2] + outs[3]
```

```python
import jax
import jax.numpy as jnp
from jax import lax
from jax.experimental import pallas as pl
from jax.experimental.pallas import tpu as pltpu

F32 = jnp.float32
BF16 = jnp.bfloat16

D = 1024
NDEV = 8
HEADS = 8
HD = 128
QB = 128
NGROUPS = 3
DILS = (1, 4, 16)
DFF = 2816
HCH = 1408
POOL_G = 4
PGD = 256
RMS_EPS = 1e-6
NEG = -1e30

ADAM_LR = 0.001
ADAM_B1 = 0.9
ADAM_B2 = 0.999
ADAM_EPS = 1e-08
ADAM_WD = 0.01
ADAM_STEP = 10

VMEM_LIMIT = 52 * 1024 * 1024

SECTIONS = (("qkv", 1152), ("wo", 128), ("wpi", 128), ("gu0", 704), ("gu1", 704),
            ("d0", 352), ("d1", 352), ("pg", 32))
LOC_OFF = {}
GLB_OFF = {}
_o = 0
for _n, _r in SECTIONS:
    LOC_OFF[_n] = _o
    GLB_OFF[_n] = _o * NDEV
    _o += _r
PACK_ROWS = _o
GLB_ROWS = PACK_ROWS * NDEV
SEC_ROWS = dict(SECTIONS)
SEC_ROWS["pv"] = 8


def _cparams(n_grid):
    return pltpu.CompilerParams(dimension_semantics=("arbitrary",) * n_grid, vmem_limit_bytes=VMEM_LIMIT)


def _shard_pos(name, dev):
    n = SEC_ROWS[name]
    if name in ("gu0", "gu1"):
        return ((dev % 4) // 2) * (2 * HCH) + (dev // 4) * HCH + (dev % 2) * n
    return dev * n


def _mm(a, b, *, mode, M, N, K, tm, tn, tk, out_dtype, name, a_off=(0, 0), b_off=(0, 0), res=None,
        out_rows=None, out_off=0, out_prev=None, deps=()):
    nm, nn, nk = M // tm, N // tn, K // tk
    assert nm * tm == M and nn * tn == N and nk * tk == K
    if mode == "nn":
        a_bs, b_bs = (tm, tk), (tk, tn)
        a_ix = lambda i, j, k: (i, k)
        b_ix = lambda i, j, k: (k, j)
        dims = (((1,), (0,)), ((), ()))
    elif mode == "nt":
        a_bs, b_bs = (tm, tk), (tn, tk)
        a_ix = lambda i, j, k: (i, k)
        b_ix = lambda i, j, k: (j, k)
        dims = (((1,), (1,)), ((), ()))
    else:
        a_bs, b_bs = (tk, tm), (tk, tn)
        a_ix = lambda i, j, k: (k, i)
        b_ix = lambda i, j, k: (k, j)
        dims = (((0,), (0,)), ((), ()))

    def spec(bs, ix, off):
        def im(i, j, k):
            r, c = ix(i, j, k)
            return (r + off[0], c + off[1])
        return pl.BlockSpec(bs, im)

    in_specs = [spec(a_bs, a_ix, a_off), spec(b_bs, b_ix, b_off)]
    args = [a, b]
    if res is not None:
        in_specs.append(pl.BlockSpec((tm, tn), lambda i, j, k: (i, j)))
        args.append(res)
    out_shape = jax.ShapeDtypeStruct((M if out_rows is None else out_rows, N), out_dtype)
    out_spec = pl.BlockSpec((tm, tn), lambda i, j, k: (i + out_off, j))
    has_res = res is not None
    extra = list(deps) + ([out_prev] if out_prev is not None else [])
    for dep in extra:
        in_specs.append(pl.BlockSpec(memory_space=pl.ANY))
        args.append(dep)
    o_pos = 2 + int(has_res) + len(extra)
    aliases = {len(args) - 1: 0} if out_prev is not None else {}

    def kern(*refs):
        a_ref, b_ref = refs[0], refs[1]
        res_ref = refs[2] if has_res else None
        o_ref = refs[o_pos]
        av = a_ref[...]
        bv = b_ref[...]
        if av.dtype != BF16:
            av = av.astype(BF16)
        if bv.dtype != BF16:
            bv = bv.astype(BF16)
        part = lax.dot_general(av, bv, dims, preferred_element_type=F32)

        def write(val):
            if has_res:
                val = val + res_ref[...]
            o_ref[...] = val.astype(out_dtype)

        if nk == 1:
            write(part)
        else:
            acc_ref = refs[-1]
            k = pl.program_id(2)

            @pl.when(k == 0)
            def _():
                acc_ref[...] = part

            @pl.when(k > 0)
            def _():
                acc_ref[...] += part

            @pl.when(k == nk - 1)
            def _():
                write(acc_ref[...])

    scratch = [pltpu.VMEM((tm, tn), F32)] if nk > 1 else []
    return pl.pallas_call(
        kern, grid=(nm, nn, nk), in_specs=in_specs, out_specs=out_spec, out_shape=out_shape,
        scratch_shapes=scratch, input_output_aliases=aliases, compiler_params=_cparams(3), name=name)(*args)


def _mm_rms_bwd(a, b, x, g, dres, *, mode, M, K, tm, name, b_off=(0, 0), deps=()):
    nd = len(deps)
    b_bs = (K, D) if mode == "nn" else (D, K)
    dims = (((1,), (0,)), ((), ())) if mode == "nn" else (((1,), (1,)), ((), ()))

    def kern(a_ref, b_ref, x_ref, g_ref, dres_ref, *rest):
        dx_ref, dg_ref = rest[nd:]
        i = pl.program_id(0)
        av = a_ref[...]
        if av.dtype != BF16:
            av = av.astype(BF16)
        dhv = lax.dot_general(av, b_ref[...], dims, preferred_element_type=F32)
        xv = x_ref[...]
        r = lax.rsqrt(jnp.mean(xv * xv, axis=-1, keepdims=True) + RMS_EPS)
        xhat = xv * r
        gy = dhv * g_ref[...]
        dx_ref[...] = dres_ref[...] + r * (gy - xhat * jnp.mean(gy * xhat, axis=-1, keepdims=True))
        part = jnp.sum(dhv * xhat, axis=0, keepdims=True)

        @pl.when(i == 0)
        def _():
            dg_ref[...] = part

        @pl.when(i > 0)
        def _():
            dg_ref[...] += part

    row = pl.BlockSpec((tm, D), lambda i: (i, 0))
    vec = pl.BlockSpec((1, D), lambda i: (0, 0))
    return pl.pallas_call(
        kern, grid=(M // tm,),
        in_specs=[pl.BlockSpec((tm, K), lambda i: (i, 0)),
                  pl.BlockSpec(b_bs, lambda i: b_off, pipeline_mode=pl.Buffered(1)), row, vec, row]
        + [pl.BlockSpec(memory_space=pl.ANY)] * nd,
        out_specs=[row, vec],
        out_shape=[jax.ShapeDtypeStruct((M, D), F32), jax.ShapeDtypeStruct((1, D), F32)],
        compiler_params=_cparams(1), name=name)(a, b, x, g, dres, *deps)


def _norm_tail(xv, gv, rest, head):
    r = lax.rsqrt(jnp.mean(xv * xv, axis=-1, keepdims=True) + RMS_EPS)
    xhat = xv * r
    if not head:
        xo_ref, h_ref = rest
        xo_ref[...] = xv
        h_ref[...] = (xhat * gv).astype(BF16)
        return
    t_ref, dx_ref, dg_ref, ls_ref = rest
    i = pl.program_id(0)
    e = xhat * gv - t_ref[...]
    dy = e * (1.0 / D)
    gy = dy * gv
    dx_ref[...] = r * (gy - xhat * jnp.mean(gy * xhat, axis=-1, keepdims=True))
    dgp = jnp.sum(dy * xhat, axis=0, keepdims=True)
    lsp = jnp.sum(e * e, axis=0, keepdims=True)

    @pl.when(i == 0)
    def _():
        dg_ref[...] = dgp
        ls_ref[...] = lsp

    @pl.when(i > 0)
    def _():
        dg_ref[...] += dgp
        ls_ref[...] += lsp


def _ffn_fwd(h, wgu, wd, res, g, *, name, tgt=None):
    S = h.shape[0]
    tm = 256
    nj = DFF // HCH
    head = tgt is not None

    def kern(h_ref, wgu_ref, wd_ref, res_ref, g_ref, *rest):
        t_refs, (gu_ref, act_ref), tail = rest[:int(head)], rest[int(head):int(head) + 2], rest[int(head) + 2:]
        hv = h_ref[...]
        for j in range(nj):
            gu = lax.dot_general(hv, wgu_ref[2 * HCH * j:2 * HCH * (j + 1), :], (((1,), (1,)), ((), ())),
                                 preferred_element_type=F32)
            gu_ref[:, 2 * HCH * j:2 * HCH * (j + 1)] = gu.astype(BF16)
            gate = gu[:, :HCH]
            act_ref[:, HCH * j:HCH * (j + 1)] = (gate * jax.nn.sigmoid(gate) * gu[:, HCH:]).astype(BF16)
        xv = res_ref[...] + jnp.dot(act_ref[...], wd_ref[...], preferred_element_type=F32)
        _norm_tail(xv, g_ref[...], tuple(t_refs) + tuple(tail), head)

    row = pl.BlockSpec((tm, D), lambda i: (i, 0))
    vec = pl.BlockSpec((1, D), lambda i: (0, 0))
    in_specs = [row, pl.BlockSpec((2 * DFF, D), lambda i: (0, 0), pipeline_mode=pl.Buffered(1)),
                pl.BlockSpec((DFF, D), lambda i: (0, 0), pipeline_mode=pl.Buffered(1)), row, vec]
    out_specs = [pl.BlockSpec((tm, 2 * DFF), lambda i: (i, 0)), pl.BlockSpec((tm, DFF), lambda i: (i, 0))]
    out_shape = [jax.ShapeDtypeStruct((S, 2 * DFF), BF16), jax.ShapeDtypeStruct((S, DFF), BF16)]
    args = [h, wgu, wd, res, g]
    if head:
        in_specs, args = in_specs + [row], args + [tgt]
        out_specs += [row, vec, vec]
        out_shape += [jax.ShapeDtypeStruct((S, D), F32), jax.ShapeDtypeStruct((1, D), F32),
                      jax.ShapeDtypeStruct((1, D), F32)]
    else:
        out_specs += [row, row]
        out_shape += [jax.ShapeDtypeStruct((S, D), F32), jax.ShapeDtypeStruct((S, D), BF16)]
    outs = pl.pallas_call(kern, grid=(S // tm,), in_specs=in_specs, out_specs=out_specs, out_shape=out_shape,
                          compiler_params=_cparams(1), name=name)(*args)
    return outs[0], outs[1], tuple(outs[2:])


def _mm_res_norm(a, b, res, g, *, K, tm, name, b_off=(0, 0), tgt=None):
    M = a.shape[0]
    head = tgt is not None

    def kern(a_ref, b_ref, res_ref, g_ref, *rest):
        xv = res_ref[...] + jnp.dot(a_ref[...], b_ref[...], preferred_element_type=F32)
        _norm_tail(xv, g_ref[...], rest, head)

    row = pl.BlockSpec((tm, D), lambda i: (i, 0))
    vec = pl.BlockSpec((1, D), lambda i: (0, 0))
    in_specs = [pl.BlockSpec((tm, K), lambda i: (i, 0)),
                pl.BlockSpec((K, D), lambda i: b_off, pipeline_mode=pl.Buffered(1)), row, vec]
    if head:
        return pl.pallas_call(
            kern, grid=(M // tm,), in_specs=in_specs + [row], out_specs=[row, vec, vec],
            out_shape=[jax.ShapeDtypeStruct((M, D), F32), jax.ShapeDtypeStruct((1, D), F32),
                       jax.ShapeDtypeStruct((1, D), F32)],
            compiler_params=_cparams(1), name=name)(a, b, res, g, tgt)
    return pl.pallas_call(
        kern, grid=(M // tm,), in_specs=in_specs, out_specs=[row, row],
        out_shape=[jax.ShapeDtypeStruct((M, D), F32), jax.ShapeDtypeStruct((M, D), BF16)],
        compiler_params=_cparams(1), name=name)(a, b, res, g)


def _rms_fwd(x, g, name, deps=()):
    S = x.shape[0]
    tr = 512

    def kern(x_ref, g_ref, *rest):
        h_ref = rest[-1]
        xv = x_ref[...]
        r = lax.rsqrt(jnp.mean(xv * xv, axis=-1, keepdims=True) + RMS_EPS)
        h_ref[...] = (xv * r * g_ref[...]).astype(BF16)

    return pl.pallas_call(
        kern, grid=(S // tr,),
        in_specs=[pl.BlockSpec((tr, D), lambda i: (i, 0)), pl.BlockSpec((1, D), lambda i: (0, 0))]
        + [pl.BlockSpec(memory_space=pl.ANY)] * len(deps),
        out_specs=pl.BlockSpec((tr, D), lambda i: (i, 0)),
        out_shape=jax.ShapeDtypeStruct((S, D), BF16), compiler_params=_cparams(1), name=name)(x, g, *deps)


def _chunks_put(scr, val):
    for c in range(scr.shape[0]):
        scr[c] = val[:, c * 128:(c + 1) * 128]


def _chunks_get(scr):
    return jnp.concatenate([scr[c] for c in range(scr.shape[0])], axis=1)


def _chunks_rows(scr, r, n, dil):
    return jnp.concatenate([scr.at[c][pl.ds(r, n, stride=dil), :] for c in range(scr.shape[0])], axis=1)


def _chunks_add_rows(scr, val, r, n, dil, accumulate):
    for c in range(scr.shape[0]):
        rows = pl.ds(r, n, stride=dil)
        piece = val[:, c * 128:(c + 1) * 128]
        tile = scr.at[c]
        tile[rows, :] = tile[rows, :] + piece if accumulate else piece


def _rms_fwd_folded(x, g, name, deps=()):
    S = x.shape[0]
    tr = 512
    dils = DILS[1:]

    def kern(x_ref, g_ref, *rest):
        outs, scr = rest[len(deps):-1], rest[-1]
        xv = x_ref[...]
        r = lax.rsqrt(jnp.mean(xv * xv, axis=-1, keepdims=True) + RMS_EPS)
        h = (xv * r * g_ref[...]).astype(BF16)
        outs[0][...] = h
        _chunks_put(scr, h.astype(F32))
        for o_ref, dil in zip(outs[1:], dils):
            for res in range(dil):
                o_ref[res] = _chunks_rows(scr, res, tr // dil, dil).astype(BF16)

    return pl.pallas_call(
        kern, grid=(S // tr,),
        in_specs=[pl.BlockSpec((tr, D), lambda i: (i, 0)), pl.BlockSpec((1, D), lambda i: (0, 0))]
        + [pl.BlockSpec(memory_space=pl.ANY)] * len(deps),
        out_specs=[pl.BlockSpec((tr, D), lambda i: (i, 0))]
        + [pl.BlockSpec((dil, tr // dil, D), lambda i: (0, i, 0)) for dil in dils],
        out_shape=[jax.ShapeDtypeStruct((S, D), BF16)]
        + [jax.ShapeDtypeStruct((dil, S // dil, D), BF16) for dil in dils],
        scratch_shapes=[pltpu.VMEM((D // 128, tr, 128), F32)],
        compiler_params=_cparams(1), name=name)(x, g, *deps)


def _rms_bwd(dh, x, g, dres, name, folded=()):
    S = x.shape[0]
    tr = 512
    nf = len(folded)

    def kern(dh_ref, *rest):
        f_refs = rest[:nf]
        x_ref, g_ref, dres_ref, dx_ref, dg_ref = rest[nf:nf + 5]
        i = pl.program_id(0)
        xv = x_ref[...]
        if nf:
            acc_ref = rest[nf + 5]
            _chunks_put(acc_ref, dh_ref[...].astype(F32))
            for f_ref in f_refs:
                dil = f_ref.shape[0]
                for res in range(dil):
                    _chunks_add_rows(acc_ref, f_ref[res], res, tr // dil, dil, True)
            dhv = _chunks_get(acc_ref)
        else:
            dhv = dh_ref[...].astype(F32)
        r = lax.rsqrt(jnp.mean(xv * xv, axis=-1, keepdims=True) + RMS_EPS)
        xhat = xv * r
        gy = dhv * g_ref[...]
        dx_ref[...] = dres_ref[...] + r * (gy - xhat * jnp.mean(gy * xhat, axis=-1, keepdims=True))
        part = jnp.sum(dhv * xhat, axis=0, keepdims=True)

        @pl.when(i == 0)
        def _():
            dg_ref[...] = part

        @pl.when(i > 0)
        def _():
            dg_ref[...] += part

    row = pl.BlockSpec((tr, D), lambda i: (i, 0))
    vec = pl.BlockSpec((1, D), lambda i: (0, 0))
    fspecs = [pl.BlockSpec((f.shape[0], tr // f.shape[0], D), lambda i: (0, i, 0)) for f in folded]
    return pl.pallas_call(
        kern, grid=(S // tr,), in_specs=[row] + fspecs + [row, vec, row], out_specs=[row, vec],
        out_shape=[jax.ShapeDtypeStruct((S, D), F32), jax.ShapeDtypeStruct((1, D), F32)],
        scratch_shapes=[pltpu.VMEM((D // 128, tr, 128), F32)] if nf else [],
        compiler_params=_cparams(1), name=name)(dh, *folded, x, g, dres)


def _ffn_down_bwd(dx, G, gu, name):
    S = dx.shape[0]
    tm = 512
    nj = DFF // HCH

    def kern(dx_ref, w_ref, gu_ref, o_ref):
        dxv = dx_ref[...].astype(BF16)
        for j in range(nj):
            c0 = 2 * HCH * j
            dact = lax.dot_general(dxv, w_ref[HCH * j:HCH * (j + 1), :], (((1,), (1,)), ((), ())),
                                   preferred_element_type=F32)
            gate = gu_ref[:, c0:c0 + HCH].astype(F32)
            up = gu_ref[:, c0 + HCH:c0 + 2 * HCH].astype(F32)
            sig = jax.nn.sigmoid(gate)
            silu = gate * sig
            o_ref[:, c0:c0 + HCH] = (dact * up * (sig * (1.0 + gate * (1.0 - sig)))).astype(BF16)
            o_ref[:, c0 + HCH:c0 + 2 * HCH] = (dact * silu).astype(BF16)

    row = pl.BlockSpec((tm, 2 * DFF), lambda i: (i, 0))
    return pl.pallas_call(
        kern, grid=(S // tm,),
        in_specs=[pl.BlockSpec((tm, D), lambda i: (i, 0)),
                  pl.BlockSpec((DFF, D), lambda i: (0, 0), pipeline_mode=pl.Buffered(1)), row],
        out_specs=row, out_shape=jax.ShapeDtypeStruct((S, 2 * DFF), BF16),
        compiler_params=_cparams(1), name=name)(dx, G, gu)


def _trail(u, *, backward, name):
    S = u.shape[0]

    def kern(u_ref, o_ref):
        g = pl.program_id(0)
        for grp in range(POOL_G):
            @pl.when(g == grp)
            def _(grp=grp):
                uv = u_ref[...].astype(F32)
                row = lax.broadcasted_iota(jnp.int32, uv.shape, 0)
                cnt = jnp.minimum(row + 1, 2 << grp).astype(F32)
                s = uv / cnt if backward else uv
                for k in (1, 2, 4, 8)[:grp + 1]:
                    if backward:
                        sh = jnp.where(row < S - k, pltpu.roll(s, S - k, 0), 0.0)
                    else:
                        sh = jnp.where(row >= k, pltpu.roll(s, k, 0), 0.0)
                    s = s + sh
                if backward:
                    o_ref[...] = (s - uv).astype(BF16)
                else:
                    o_ref[...] = (s / cnt - uv).astype(BF16)

    blk = pl.BlockSpec((S, PGD), lambda g: (0, g))
    return pl.pallas_call(
        kern, grid=(POOL_G,), in_specs=[blk], out_specs=blk,
        out_shape=jax.ShapeDtypeStruct((S, D), BF16), compiler_params=_cparams(1), name=name)(u)


def _pool_out(yd, G, scale, xres):
    S = yd.shape[0]
    tm = min(S, 4096)

    def kern(y_ref, w_ref, s_ref, x_ref, o_ref):
        z = jnp.dot(y_ref[...], w_ref[...], preferred_element_type=F32)
        o_ref[...] = x_ref[...] + z * s_ref[...]

    tile = pl.BlockSpec((tm, PGD), lambda i, g: (i, g))
    return pl.pallas_call(
        kern, grid=(S // tm, POOL_G),
        in_specs=[tile, pl.BlockSpec((PGD, PGD), lambda i, g: (0, g)),
                  pl.BlockSpec((1, PGD), lambda i, g: (0, g)), tile],
        out_specs=tile, out_shape=jax.ShapeDtypeStruct((S, D), F32),
        compiler_params=_cparams(2), name="pool_out")(yd, G, scale, xres)


def _pool_out_bwd(dz, yd, G, scale):
    S = yd.shape[0]
    tm = min(S, 4096)
    ni = S // tm

    def kern(dz_ref, y_ref, w_ref, s_ref, dy_ref, ds_ref, dw_ref, acc_ref):
        i = pl.program_id(1)
        dzv = dz_ref[...]
        yv = y_ref[...]
        wv = w_ref[...]
        zraw = jnp.dot(yv, wv, preferred_element_type=F32)
        dsp = jnp.sum(dzv * zraw, axis=0, keepdims=True)
        dzr = (dzv * s_ref[...]).astype(BF16)
        dy_ref[...] = lax.dot_general(dzr, wv, (((1,), (1,)), ((), ())), preferred_element_type=F32)
        dwp = lax.dot_general(yv, dzr, (((0,), (0,)), ((), ())), preferred_element_type=F32)

        @pl.when(i == 0)
        def _():
            ds_ref[...] = dsp
            acc_ref[...] = dwp

        @pl.when(i > 0)
        def _():
            ds_ref[...] += dsp
            acc_ref[...] += dwp

        @pl.when(i == ni - 1)
        def _():
            dw_ref[...] = acc_ref[...].astype(BF16)

    tile = pl.BlockSpec((tm, PGD), lambda g, i: (i, g))
    return pl.pallas_call(
        kern, grid=(POOL_G, ni),
        in_specs=[tile, tile, pl.BlockSpec((PGD, PGD), lambda g, i: (0, g)),
                  pl.BlockSpec((1, PGD), lambda g, i: (0, g))],
        out_specs=[tile, pl.BlockSpec((1, PGD), lambda g, i: (0, g)),
                   pl.BlockSpec((PGD, PGD), lambda g, i: (0, g))],
        out_shape=[jax.ShapeDtypeStruct((S, D), F32), jax.ShapeDtypeStruct((1, D), F32),
                   jax.ShapeDtypeStruct((PGD, D), BF16)],
        scratch_shapes=[pltpu.VMEM((PGD, PGD), F32)],
        compiler_params=_cparams(2), name="pool_out_bwd")(dz, yd, G, scale)


def _bias_table():
    qi = jnp.arange(QB)[:, None]
    ki = jnp.arange(2 * QB)[None, :]
    delta = QB + qi - ki
    inband = (delta >= 0) & (delta <= QB)
    n = NGROUPS * HEADS
    slopes = jnp.exp2(-8.0 * jnp.arange(1, n + 1, dtype=F32) / n).reshape(NGROUPS, HEADS)
    dil = jnp.asarray(DILS, F32)
    bias = -slopes[:, :, None, None] * (delta.astype(F32)[None, None] * dil[:, None, None, None])
    return jnp.where(inband[None, None], bias, NEG)


def _attn_fwd(qkv_f, bias, nb, name):
    S = qkv_f.shape[0]
    nblk = S // QB
    scale = HD ** -0.5

    def kern(q_ref, k2_ref, kp_ref, v2_ref, vp_ref, b_ref, o_ref, l_ref, s_scr, p_scr, r_scr):
        s_id = pl.program_id(0)
        col = lax.broadcasted_iota(jnp.int32, (QB, 2 * QB), 1)
        lane = lax.broadcasted_iota(jnp.int32, (QB, HD), 1)

        def keys(sub, cur2_ref, prev_ref, sl):
            if sub:
                return cur2_ref[:, sl]
            return jnp.concatenate([prev_ref[:, sl], cur2_ref[0:QB, sl]], axis=0)

        for sub in range(2):
            for h in range(HEADS):
                sl = slice(h * HD, (h + 1) * HD)
                s_scr[sub * HEADS + h] = lax.dot_general(
                    q_ref[sub * QB:(sub + 1) * QB, sl], keys(sub, k2_ref, kp_ref, sl), (((1,), (1,)), ((), ())),
                    preferred_element_type=F32)
        for sub in range(2):
            has_prev = jnp.bitwise_and(2 * s_id + sub, nb - 1) != 0
            dead = jnp.logical_and(col < QB, jnp.logical_not(has_prev))
            lse_all = jnp.zeros((QB, HD), F32)
            for h in range(HEADS):
                u = sub * HEADS + h
                s = s_scr[u] * scale + b_ref[h]
                s = jnp.where(dead, NEG, s)
                m = jnp.max(s, axis=-1, keepdims=True)
                p = jnp.exp(s - m)
                den = jnp.sum(p, axis=-1, keepdims=True)
                p_scr[u] = p.astype(BF16)
                r_scr[u] = jnp.broadcast_to(1.0 / den, (QB, HD))
                lse_all = jnp.where(lane == h, m + jnp.log(den), lse_all)
            l_ref[sub * QB:(sub + 1) * QB, :] = lse_all
        for sub in range(2):
            for h in range(HEADS):
                u = sub * HEADS + h
                sl = slice(h * HD, (h + 1) * HD)
                o = jnp.dot(p_scr[u], keys(sub, v2_ref, vp_ref, sl), preferred_element_type=F32) * r_scr[u]
                o_ref[sub * QB:(sub + 1) * QB, sl] = o.astype(BF16)

    def pair(colblk):
        return pl.BlockSpec((2 * QB, D), lambda s: (s, colblk))

    def prev(colblk):
        return pl.BlockSpec((QB, D), lambda s: (jnp.maximum(2 * s - 1, 0), colblk))

    return pl.pallas_call(
        kern, grid=(nblk // 2,),
        in_specs=[pair(0), pair(1), prev(1), pair(2), prev(2), pl.BlockSpec((HEADS, QB, 2 * QB), lambda s: (0, 0, 0))],
        out_specs=[pl.BlockSpec((2 * QB, D), lambda s: (s, 0)), pl.BlockSpec((2 * QB, HD), lambda s: (s, 0))],
        out_shape=[jax.ShapeDtypeStruct((S, D), BF16), jax.ShapeDtypeStruct((S, HD), F32)],
        scratch_shapes=[pltpu.VMEM((2 * HEADS, QB, 2 * QB), F32), pltpu.VMEM((2 * HEADS, QB, 2 * QB), BF16),
                        pltpu.VMEM((2 * HEADS, QB, HD), F32)],
        compiler_params=_cparams(1), name=name)(qkv_f, qkv_f, qkv_f, qkv_f, qkv_f, bias)


def _natural(ref, scr, tm):
    dil = ref.shape[0]
    for res in range(dil):
        _chunks_add_rows(scr, ref[res].astype(F32), res, tm // dil, dil, False)
    return _chunks_get(scr)


def _attn_merge(os, lses):
    S = os[0].shape[0]
    tm = 512

    def kern(o0, o1, o2, l0, l1, l2, om_ref, lm_ref, ls1, ls2, os1, os2):
        la = l0[...]
        lb = _natural(l1, ls1, tm)
        lc = _natural(l2, ls2, tm)
        m = jnp.maximum(jnp.maximum(la, lb), lc)
        e0, e1, e2 = jnp.exp(la - m), jnp.exp(lb - m), jnp.exp(lc - m)
        tot = e0 + e1 + e2
        lm_ref[...] = m + jnp.log(tot)
        w0, w1, w2 = e0 / tot, e1 / tot, e2 / tot
        for res in range(o1.shape[0]):
            _chunks_add_rows(os1, o1[res].astype(F32), res, tm // o1.shape[0], o1.shape[0], False)
        for res in range(o2.shape[0]):
            _chunks_add_rows(os2, o2[res].astype(F32), res, tm // o2.shape[0], o2.shape[0], False)
        for h in range(HEADS):
            sl = slice(h * HD, (h + 1) * HD)
            acc = w0[:, h:h + 1] * o0[:, sl].astype(F32) + w1[:, h:h + 1] * os1[h] + w2[:, h:h + 1] * os2[h]
            om_ref[:, sl] = acc.astype(BF16)

    def spec(a, c):
        if a.ndim == 2:
            return pl.BlockSpec((tm, c), lambda i: (i, 0))
        return pl.BlockSpec((a.shape[0], tm // a.shape[0], c), lambda i: (0, i, 0))

    return pl.pallas_call(
        kern, grid=(S // tm,),
        in_specs=[spec(a, D) for a in os] + [spec(a, HD) for a in lses],
        out_specs=[pl.BlockSpec((tm, D), lambda i: (i, 0)), pl.BlockSpec((tm, HD), lambda i: (i, 0))],
        out_shape=[jax.ShapeDtypeStruct((S, D), BF16), jax.ShapeDtypeStruct((S, HD), F32)],
        scratch_shapes=[pltpu.VMEM((1, tm, HD), F32), pltpu.VMEM((1, tm, HD), F32),
                        pltpu.VMEM((HEADS, tm, HD), F32), pltpu.VMEM((HEADS, tm, HD), F32)],
        compiler_params=_cparams(1), name="attn_merge")(*os, *lses)


def _attn_bwd_prep(do, o, lse):
    S = o.shape[0]
    tm = 512
    dils = DILS[1:]

    def kern(do_ref, o_ref, l_ref, *rest):
        do_outs, l_outs, d_outs = rest[0:3], rest[3:5], rest[5:8]
        do_scr, l_scr, d_scr = rest[8:11]
        lane = lax.broadcasted_iota(jnp.int32, (tm, HD), 1)
        acc = jnp.zeros((tm, HD), F32)
        for h in range(HEADS):
            sl = slice(h * HD, (h + 1) * HD)
            prod = do_ref[:, sl] * o_ref[:, sl].astype(F32)
            acc = jnp.where(lane == h, jnp.sum(prod, axis=-1, keepdims=True), acc)
        d_scr[0] = acc
        l_scr[0] = l_ref[...]
        _chunks_put(do_scr, do_ref[...])
        do_outs[0][...] = do_ref[...].astype(BF16)
        d_outs[0][...] = acc
        for j, dil in enumerate(dils):
            for res in range(dil):
                n = tm // dil
                do_outs[1 + j][res] = _chunks_rows(do_scr, res, n, dil).astype(BF16)
                l_outs[j][res] = _chunks_rows(l_scr, res, n, dil)
                d_outs[1 + j][res] = _chunks_rows(d_scr, res, n, dil)

    def nat(c):
        return pl.BlockSpec((tm, c), lambda i: (i, 0))

    def fol(dil, c):
        return pl.BlockSpec((dil, tm // dil, c), lambda i: (0, i, 0))

    def shapes(c, dt, with_natural):
        first = [jax.ShapeDtypeStruct((S, c), dt)] if with_natural else []
        return first + [jax.ShapeDtypeStruct((dil, S // dil, c), dt) for dil in dils]

    outs = pl.pallas_call(
        kern, grid=(S // tm,), in_specs=[nat(D), nat(D), nat(HD)],
        out_specs=[nat(D)] + [fol(dil, D) for dil in dils] + [fol(dil, HD) for dil in dils]
        + [nat(HD)] + [fol(dil, HD) for dil in dils],
        out_shape=shapes(D, BF16, True) + shapes(HD, F32, False) + shapes(HD, F32, True),
        scratch_shapes=[pltpu.VMEM((HEADS, tm, HD), F32), pltpu.VMEM((1, tm, HD), F32), pltpu.VMEM((1, tm, HD), F32)],
        compiler_params=_cparams(1), name="attn_bwd_prep")(do, o, lse)
    return outs[0:3], [lse] + list(outs[3:5]), outs[5:8]


def _attn_bwd(qkv_f, do_f, lse_f, delta_f, bias, nb, name):
    S = qkv_f.shape[0]
    nblk = S // QB
    scale = HD ** -0.5

    npair = nblk // 2

    def kern(q_ref, k2_ref, kp_ref, v2_ref, vp_ref, do_ref, l_ref, d_ref, b_ref, out_ref, dq_c, dk_c, dv_c,
             s_scr, dp_scr, ds_scr, p_scr):
        s_id = pl.program_id(0)

        @pl.when(s_id == 0)
        def _():
            dq_c[...] = jnp.zeros_like(dq_c)
            dk_c[...] = jnp.zeros_like(dk_c)
            dv_c[...] = jnp.zeros_like(dv_c)

        @pl.when(s_id == npair)
        def _():
            out_ref[:, 0:D] = dq_c[...].astype(BF16)
            out_ref[:, D:2 * D] = dk_c[...].astype(BF16)
            out_ref[:, 2 * D:3 * D] = dv_c[...].astype(BF16)

        def keys(sub, cur2_ref, prev_ref, sl):
            if sub:
                return cur2_ref[:, sl]
            return jnp.concatenate([prev_ref[:, sl], cur2_ref[0:QB, sl]], axis=0)

        @pl.when(s_id < npair)
        def _():
            col = lax.broadcasted_iota(jnp.int32, (QB, 2 * QB), 1)
            out_ref[:, 0:D] = dq_c[...].astype(BF16)
            for sub in range(2):
                rows = slice(sub * QB, (sub + 1) * QB)
                for h in range(HEADS):
                    sl = slice(h * HD, (h + 1) * HD)
                    u = sub * HEADS + h
                    s_scr[u] = lax.dot_general(q_ref[rows, sl], keys(sub, k2_ref, kp_ref, sl),
                                               (((1,), (1,)), ((), ())), preferred_element_type=F32)
                    dp_scr[u] = lax.dot_general(do_ref[rows, sl], keys(sub, v2_ref, vp_ref, sl),
                                                (((1,), (1,)), ((), ())), preferred_element_type=F32)
            for sub in range(2):
                rows = slice(sub * QB, (sub + 1) * QB)
                has_prev = jnp.bitwise_and(2 * s_id + sub, nb - 1) != 0
                dead = jnp.logical_and(col < QB, jnp.logical_not(has_prev))
                lv = l_ref[rows, :]
                dv_ = d_ref[rows, :]
                for h in range(HEADS):
                    u = sub * HEADS + h
                    s = s_scr[u] * scale + b_ref[h]
                    s = jnp.where(dead, NEG, s)
                    p = jnp.exp(s - lv[:, h:h + 1])
                    ds_scr[u] = (p * (dp_scr[u] - dv_[:, h:h + 1]) * scale).astype(BF16)
                    p_scr[u] = p.astype(BF16)
            for h in range(HEADS):
                sl = slice(h * HD, (h + 1) * HD)
                parts = []
                for sub in range(2):
                    rows = slice(sub * QB, (sub + 1) * QB)
                    u = sub * HEADS + h
                    ds = ds_scr[u]
                    dq_c[rows, sl] = jnp.dot(ds, keys(sub, k2_ref, kp_ref, sl), preferred_element_type=F32)
                    dkk = lax.dot_general(ds, q_ref[rows, sl], (((0,), (0,)), ((), ())), preferred_element_type=F32)
                    dvv = lax.dot_general(p_scr[u], do_ref[rows, sl], (((0,), (0,)), ((), ())),
                                          preferred_element_type=F32)
                    parts.append((dkk, dvv))
                for which, carry, base in ((0, dk_c, D), (1, dv_c, 2 * D)):
                    first, second = parts[0][which], parts[1][which]
                    cols = slice(base + h * HD, base + (h + 1) * HD)
                    out_ref[0:QB, cols] = carry[0:QB, sl].astype(BF16)
                    out_ref[QB:2 * QB, cols] = (carry[QB:2 * QB, sl] + first[:QB]).astype(BF16)
                    carry[0:QB, sl] = first[QB:] + second[:QB]
                    carry[QB:2 * QB, sl] = second[QB:]

    last = npair - 1

    def pair(colblk, c):
        return pl.BlockSpec((2 * QB, c), lambda s: (jnp.minimum(s, last), colblk))

    def prev(colblk):
        return pl.BlockSpec((QB, D), lambda s: (jnp.maximum(2 * jnp.minimum(s, last) - 1, 0), colblk))

    return pl.pallas_call(
        kern, grid=(npair + 1,),
        in_specs=[pair(0, D), pair(1, D), prev(1), pair(2, D), prev(2), pair(0, D), pair(0, HD), pair(0, HD),
                  pl.BlockSpec((HEADS, QB, 2 * QB), lambda s: (0, 0, 0))],
        out_specs=pl.BlockSpec((2 * QB, 3 * D), lambda s: (jnp.maximum(s - 1, 0), 0)),
        out_shape=jax.ShapeDtypeStruct((S, 3 * D), BF16),
        scratch_shapes=[pltpu.VMEM((2 * QB, D), F32), pltpu.VMEM((2 * QB, D), F32), pltpu.VMEM((2 * QB, D), F32),
                        pltpu.VMEM((2 * HEADS, QB, 2 * QB), F32), pltpu.VMEM((2 * HEADS, QB, 2 * QB), F32),
                        pltpu.VMEM((2 * HEADS, QB, 2 * QB), BF16), pltpu.VMEM((2 * HEADS, QB, 2 * QB), BF16)],
        compiler_params=_cparams(1), name=name)(qkv_f, qkv_f, qkv_f, qkv_f, qkv_f, do_f, lse_f, delta_f, bias)


def _local_step(x, tgt, comm, attn_norm, ffn_norm, final_norm):
    S = x.shape[0]
    bias = _bias_table()
    g_attn = attn_norm.reshape(1, D)
    g_f0 = ffn_norm[0:1]
    g_f1 = ffn_norm[1:2]
    g_fin = final_norm.reshape(1, D)
    W = {}

    def ffn_fwd(xin, h, l, next_gain, target=None):
        return _ffn_fwd(h, W[f"gu{l}"], W[f"d{l}"], xin, next_gain, tgt=target, name=f"ffn_fwd{l}")

    def ffn_bwd(dxo, xin, gain, h, gu, act, l, rs_group):
        dgu = _ffn_down_bwd(dxo, W[f"d{l}"], gu, f"ffn_down_bwd{l}")
        gw_d = _mm(act, dxo, mode="tn", M=DFF, N=D, K=S, tm=HCH, tn=D, tk=2048, out_dtype=BF16, name=f"gw_d{l}")
        gw_gu = _mm(dgu, h, mode="tn", M=2 * DFF, N=D, K=S, tm=HCH, tn=D, tk=2048, out_dtype=BF16, name=f"gw_gu{l}")
        token = comm.send_grads(rs_group, {f"d{l}": gw_d, f"gu{l}": gw_gu})
        return _mm_rms_bwd(dgu, W[f"gu{l}"], xin, gain, dxo, mode="nn", M=S, K=2 * DFF, tm=512, deps=(token,),
                           name=f"ffn_up_bwd{l}")

    nbs = [S // QB // dil for dil in DILS]
    hf = _rms_fwd_folded(x, g_attn, "rms_attn", deps=comm.ag_tokens)
    hf = [h.reshape(S, D) for h in hf]
    W.update(comm.weights(0, hf[0]))
    qkv_f, o_f, lse_f = [], [], []
    for g, dil in enumerate(DILS):
        qkv_f.append(_mm(hf[g], W["qkv"], mode="nt", M=S, N=3 * D, K=D, tm=2048, tn=1024, tk=D, out_dtype=BF16,
                         b_off=(3 * g, 0), name=f"qkv_proj{g}"))
        og, lg = _attn_fwd(qkv_f[g], bias[g], nbs[g], f"attn_fwd{g}")
        o_f.append(og if dil == 1 else og.reshape(dil, S // dil, D))
        lse_f.append(lg if dil == 1 else lg.reshape(dil, S // dil, HD))
    passing = [comm.pass_on(1, tuple(o_f)), comm.pass_on(2, tuple(o_f))]
    (o_f, lse_f), passing = lax.optimization_barrier(((o_f, lse_f), passing))
    o, lse = _attn_merge(o_f, lse_f)
    W.update(comm.weights(1, (o, passing[0])))
    x1, h1 = _mm_res_norm(o, W["wo"], x, g_f0, K=D, tm=1024, name="attn_out")
    pv = W["pv"].reshape(NDEV, 8, 128)
    pool_norm, pool_scale = pv[:, 0, :].reshape(1, D), pv[:, 1, :].reshape(1, D)
    gu0, act0, (x2, h2) = ffn_fwd(x1, h1, 0, pool_norm)

    W.update(comm.weights(2, (x2, passing[1])))
    u = _mm(h2, W["wpi"], mode="nn", M=S, N=D, K=D, tm=1024, tn=D, tk=D, out_dtype=F32, name="pool_in")
    yd = _trail(u, backward=False, name="trail_fwd")
    x3 = _pool_out(yd, W["pg"], pool_scale, x2)
    h3 = _rms_fwd(x3, g_f1, "rms_ffn1")
    gu1, act1, (dx4, d_fin, lossvec) = ffn_fwd(x3, h3, 1, g_fin, target=tgt)

    dx3, d_f1 = ffn_bwd(dx4, x3, g_f1, h3, gu1, act1, 1, 0)
    dyd, d_scale, gw_pg = _pool_out_bwd(dx3, yd, W["pg"], pool_scale)
    du = _trail(dyd, backward=True, name="trail_bwd")
    gw_pi = _mm(h2, du, mode="tn", M=D, N=D, K=S, tm=D, tn=D, tk=S, out_dtype=BF16, name="gw_pi")
    token = comm.send_grads(1, {"pg": gw_pg, "wpi": gw_pi})
    dx2, d_pool = _mm_rms_bwd(du, W["wpi"], x2, pool_norm, dx3, mode="nt", M=S, K=D, tm=1024, deps=(token,),
                              name="pool_in_bwd")
    dx1, d_f0 = ffn_bwd(dx2, x1, g_f0, h1, gu0, act0, 0, 2)

    gw_o = _mm(o, dx1, mode="tn", M=D, N=D, K=S, tm=D, tn=D, tk=2048, out_dtype=BF16, name="gw_o")
    do = _mm(dx1, W["wo"], mode="nt", M=S, N=D, K=D, tm=1024, tn=D, tk=D, out_dtype=F32, name="attn_out_bwd")
    do_f, lse_ff, delta_f = _attn_bwd_prep(do, o, lse)
    dqkv_f, gw_qkv = [], None
    for g in range(NGROUPS):
        dqkv_f.append(_attn_bwd(qkv_f[g], do_f[g].reshape(S, D), lse_ff[g].reshape(S, HD),
                                delta_f[g].reshape(S, HD), bias[g], nbs[g], f"attn_bwd{g}"))
        gw_qkv = _mm(dqkv_f[g], hf[g], mode="tn", M=3 * D, N=D, K=S, tm=1024, tn=D, tk=S, out_dtype=BF16,
                     out_rows=NGROUPS * 3 * D, out_off=3 * g, out_prev=gw_qkv, name=f"gw_qkv{g}")
    token = comm.send_grads_pairwise({"wo": gw_o, "qkv": gw_qkv})
    dh0_f = [None] * NGROUPS
    for g in reversed(range(NGROUPS)):
        dh0_f[g] = _mm(dqkv_f[g], W["qkv"], mode="nn", M=S, N=D, K=3 * D, tm=1024, tn=D, tk=3 * D, out_dtype=F32,
                       b_off=(g, 0), deps=(token,), name=f"qkv_proj_bwd{g}")
        if g == NGROUPS - 1:
            token = comm.pass_grads(dh0_f[g])
    folded = [dh0_f[g].reshape(dil, S // dil, D) for g, dil in enumerate(DILS) if dil > 1]
    grad_x, d_attn = _rms_bwd(dh0_f[0], x, g_attn, dx1, "rms_attn_bwd", folded=folded)

    vec = jnp.concatenate([d_attn, d_f0, d_f1, d_fin, d_pool, d_scale, lossvec, jnp.zeros((1, D), F32)], axis=0)
    return grad_x, vec


def _mesh_pos():
    x, y, c = lax.axis_index("x"), lax.axis_index("y"), lax.axis_index("c")
    return x, y, c, 4 * x + 2 * y + c


def _peer(x, y, c, k):
    kx, ky, kc = (k >> 2) & 1, (k >> 1) & 1, k & 1
    px = 1 - x if kx else x
    py = 1 - y if ky else y
    pc = 1 - c if kc else c
    return (px, py, pc), 4 * px + 2 * py + pc


ANY = pl.BlockSpec(memory_space=pl.ANY)


HBM = pl.BlockSpec(memory_space=pltpu.HBM)
SEMS = pl.BlockSpec(memory_space=pltpu.SEMAPHORE)
EFFECT = pltpu.SideEffectType.DATAFLOW_SIDE_EFFECTING
NPEER = NDEV - 1

AG_GROUPS = (("qkv",), ("wo", "gu0", "d0", "pv"), ("wpi", "pg", "gu1", "d1"))
AG_ORDER = tuple(n for grp in AG_GROUPS for n in grp)
RS_GROUPS = (("d1", "gu1"), ("pg", "wpi"), ("d0", "gu0"), ("wo", "qkv"))


def _hbm(a):
    return pltpu.with_memory_space_constraint(a, pltpu.HBM)


def _remote(src, dst, send, recv, peer):
    return pltpu.make_async_remote_copy(src_ref=src, dst_ref=dst, send_sem=send, recv_sem=recv, device_id=peer,
                                        device_id_type=pl.DeviceIdType.MESH)


def _bcast_all(v, name, deps=()):
    W = v.shape[1]
    nd = len(deps)

    def kern(v_ref, *rest):
        o_ref, send, recv, lsem = rest[nd:]
        x, y, c, me = _mesh_pos()
        own = pltpu.make_async_copy(v_ref, o_ref.at[me], lsem)
        own.start()
        cps = [_remote(v_ref, o_ref.at[me], send.at[k - 1], recv.at[k - 1], _peer(x, y, c, k)[0])
               for k in range(1, NDEV)]
        for cp in cps:
            cp.start()
        for cp in cps:
            cp.wait_recv()
            cp.wait_send()
        own.wait()

    return pl.pallas_call(
        kern, in_specs=[ANY] * (1 + nd), out_specs=ANY, out_shape=jax.ShapeDtypeStruct((NDEV, 8, W), F32),
        scratch_shapes=[pltpu.SemaphoreType.DMA((NPEER,)), pltpu.SemaphoreType.DMA((NPEER,)),
                        pltpu.SemaphoreType.DMA(())],
        name=name)(v, *deps)


ALL_KS = tuple(range(1, NDEV))
AG_KS1 = (1, 2, 4, 6)
AG_KS2 = (2, 4, 6)
RS_KS_PAIR = (1, 3, 5, 7)
RS_KS_CHIPS = (2, 4, 6)


def _split_start(srcs, src_of, lands, copy_refs, name, deps=(), ks=ALL_KS, to=None):
    ns, n, nd, nk = len(srcs), len(lands), len(deps), len(ks)

    def body(*refs):
        ins, land = refs[:ns], refs[ns:ns + n]
        send, recv = refs[ns + n + nd], refs[ns + n + nd + 1]
        token = refs[-1]
        x, y, c, me = _mesh_pos()
        for j in range(n):
            for i, k in enumerate(ks):
                _, pid = _peer(x, y, c, k)
                dest, _ = _peer(x, y, c, k if to is None else to)
                src, dst = copy_refs(j, (land[j] if src_of[j] is None else ins[src_of[j]]), land[j], me, pid, i)
                _remote(src, dst, send.at[j * nk + i], recv.at[j * nk + i], dest).start()
        token[...] = jnp.zeros_like(token)

    outs = pl.pallas_call(
        body, name=name,
        out_shape=(pltpu.SemaphoreType.DMA((n * nk,)), pltpu.SemaphoreType.DMA((n * nk,)))
        + tuple(pltpu.HBM(a.shape, a.dtype) for a in srcs) + tuple(pltpu.HBM(a.shape, a.dtype) for a in lands)
        + (jax.ShapeDtypeStruct((8, 128), F32),),
        in_specs=(HBM,) * (ns + n) + (ANY,) * nd,
        out_specs=(SEMS, SEMS) + (HBM,) * (ns + n) + (pl.BlockSpec(memory_space=pltpu.VMEM),),
        input_output_aliases={i: 2 + i for i in range(ns + n)},
        compiler_params=pltpu.CompilerParams(has_side_effects=EFFECT),
    )(*[_hbm(a) for a in srcs], *[_hbm(a) for a in lands], *deps)
    return outs[0], outs[1], list(outs[2:2 + ns]), list(outs[2 + ns:2 + ns + n]), outs[-1]


def _split_wait(srcs, src_of, lands, send, recv, sem_rows, wait_refs, after, name, ks=ALL_KS):
    ns, n, nk = len(srcs), len(lands), len(ks)
    after = tuple(after) if isinstance(after, (tuple, list)) else (after,)

    def body(*refs):
        ins, land = refs[:ns], refs[ns:ns + n]
        send_ref, recv_ref = refs[ns + n], refs[ns + n + 1]
        x, y, c, me = _mesh_pos()
        for j in range(n):
            for i, k in enumerate(ks):
                peer, _ = _peer(x, y, c, k)
                src, dst = wait_refs(j, (land[j] if src_of[j] is None else ins[src_of[j]]), land[j])
                sem = sem_rows[j] * nk + i
                cp = _remote(src, dst, send_ref.at[sem], recv_ref.at[sem], peer)
                cp.wait_send()
                cp.wait_recv()

    outs = pl.pallas_call(
        body, name=name,
        out_shape=tuple(pltpu.HBM(a.shape, a.dtype) for a in srcs) + tuple(pltpu.HBM(a.shape, a.dtype) for a in lands),
        in_specs=(HBM,) * (ns + n) + (SEMS, SEMS) + (ANY,) * len(after),
        out_specs=(HBM,) * (ns + n),
        input_output_aliases={i: i for i in range(ns + n)},
        compiler_params=pltpu.CompilerParams(has_side_effects=EFFECT),
    )(*srcs, *lands, send, recv, *after)
    return list(outs[:ns]), list(outs[ns:])


def _ag_dtype(name):
    return F32 if name == "pv" else BF16


def _ag_align(name):
    return 8 if name == "pv" else 16


def _place_transposed(w, me, name):
    rows = w.shape[1]
    nblk = rows // 128

    def kern(me_ref, w_ref, o_ref):
        o_ref[...] = w_ref[...].T.astype(BF16)

    grid_spec = pltpu.PrefetchScalarGridSpec(
        num_scalar_prefetch=1, grid=(nblk,),
        in_specs=[pl.BlockSpec((D, 128), lambda i, me_ref: (0, i))],
        out_specs=pl.BlockSpec((128, D), lambda i, me_ref: (me_ref[0] * nblk + i, 0)))
    return pl.pallas_call(
        kern, grid_spec=grid_spec, out_shape=jax.ShapeDtypeStruct((NDEV * rows, D), BF16),
        compiler_params=_cparams(1), name=name)(me.reshape(1).astype(jnp.int32), w)


class _Comm:
    def __init__(self, params, make_shards, me, placed):
        self.me = me
        self.ag_land, self.ag_sems, self.ag_tokens, self.ag_passing = {}, {}, (), {}
        self.rs = []
        deps = ()
        for part, names in enumerate((AG_GROUPS[0], AG_ORDER[len(AG_GROUPS[0]):])):
            rows = [SEC_ROWS[n] for n in names]
            if part == 0:
                lands = [placed[n] for n in names]
            else:
                params, deps = lax.optimization_barrier((params, deps))
                shards = make_shards(*params)
                lands = [lax.dynamic_update_slice(lax.empty((NDEV * r, shards[n].shape[1]), _ag_dtype(n)),
                                                  shards[n].astype(_ag_dtype(n)), (_shard_pos(n, me), 0))
                         for n, r in zip(names, rows)]

            def copy_refs(j, src, land, me, pid, i, names=names, rows=rows):
                own = land.at[pl.ds(pl.multiple_of(_shard_pos(names[j], me), _ag_align(names[j])), rows[j])]
                return own, own

            send, recv, _, lands, token = _split_start([], [None] * len(names), lands, copy_refs, f"ag_start{part}",
                                                       deps=deps, ks=AG_KS1)
            deps = (token,)
            self.ag_tokens += (token,)
            for j, n in enumerate(names):
                self.ag_land[n] = lands[j]
                self.ag_sems[n] = (send, recv, j)

    def pass_on(self, group, after):
        names = AG_GROUPS[group]
        send, recv = self.ag_sems[names[0]][:2]
        idx = [self.ag_sems[n][2] for n in names]
        rows = [SEC_ROWS[n] for n in names]
        none = [None] * len(names)

        def wait_refs(j, src, land):
            return land.at[pl.ds(0, rows[j])], land.at[pl.ds(0, rows[j])]

        _, lands = _split_wait([], none, [self.ag_land[n] for n in names], send, recv, idx,
                               wait_refs, after, f"ag_wait{group}", ks=AG_KS1)

        def copy_refs(j, src, land, me, pid, i):
            theirs = land.at[pl.ds(pl.multiple_of(_shard_pos(names[j], pid), _ag_align(names[j])), rows[j])]
            return theirs, theirs

        send, recv, _, lands, token = _split_start([], none, lands, copy_refs, f"ag_pass{group}", ks=AG_KS2, to=1)
        self.ag_passing[group] = (send, recv, lands, wait_refs)
        return token

    def weights(self, group, after):
        names = AG_GROUPS[group]
        if group not in self.ag_passing:
            after = self.pass_on(group, after)
        send, recv, lands, wait_refs = self.ag_passing[group]
        _, lands = _split_wait([], [None] * len(names), lands, send, recv, list(range(len(names))), wait_refs, after,
                               f"ag_pass_wait{group}", ks=AG_KS2)
        return dict(zip(names, lands))

    def send_grads(self, group, gws):
        names = RS_GROUPS[group]
        rows = [SEC_ROWS[n] for n in names]
        grads = [gws[n] for n in names]
        me = self.me
        lands = [lax.dynamic_update_slice(
            lax.empty((NDEV, r, D), BF16),
            lax.dynamic_slice(g, (_shard_pos(n, me), 0), (r, D))[None], (me, 0, 0))
            for n, r, g in zip(names, rows, grads)]

        def copy_refs(j, src, land, me, pid, i):
            return src.at[pl.ds(pl.multiple_of(_shard_pos(names[j], pid), 16), rows[j])], land.at[me]

        send, recv, srcs, lands, token = _split_start(grads, list(range(len(names))), lands, copy_refs,
                                                      f"rs_start{group}")
        self.rs.append((names, rows, send, recv, srcs, lands, ALL_KS))
        return token

    def send_grads_pairwise(self, gws):
        names = RS_GROUPS[-1]
        rows = [SEC_ROWS[n] for n in names]
        grads = [gws[n] for n in names]
        idx = list(range(len(names)))
        lands = [lax.empty((len(RS_KS_PAIR), r, D), BF16) for r in rows]

        def copy_refs(j, src, land, me, pid, i):
            return src.at[pl.ds(pl.multiple_of(_shard_pos(names[j], pid), 16), rows[j])], land.at[i]

        send, recv, srcs, lands, token = _split_start(grads, idx, lands, copy_refs, "rs_pair_start",
                                                      ks=RS_KS_PAIR, to=1)
        self.pair = (names, rows, send, recv, srcs, lands)
        return token

    def pass_grads(self, after):
        names, rows, send, recv, srcs, lands = self.pair
        idx = list(range(len(names)))
        me = self.me

        def wait_refs(j, src, land):
            return src.at[pl.ds(0, rows[j])], land.at[0]

        srcs, lands = _split_wait(srcs, idx, lands, send, recv, idx, wait_refs, after, "rs_pair_wait", ks=RS_KS_PAIR)
        sums = []
        for n, r, g, got in zip(names, rows, srcs, lands):
            mine = jnp.stack([lax.dynamic_slice(g, (_shard_pos(n, jnp.bitwise_xor(me, k)), 0), (r, D))
                              for k in (0,) + RS_KS_CHIPS])
            sums.append(_pair_sum(mine, got, f"rs_pair_sum_{n}"))
        lands = [lax.dynamic_update_slice(lax.empty(p.shape, BF16), p[0:1], (0, 0, 0)) for p in sums]

        def copy_refs(j, src, land, me, pid, i):
            return src.at[i + 1], land.at[i + 1]

        send, recv, sums, lands, token = _split_start(sums, idx, lands, copy_refs, f"rs_start{len(RS_GROUPS) - 1}",
                                                      ks=RS_KS_CHIPS)
        self.rs.append((names, rows, send, recv, sums, lands, RS_KS_CHIPS))
        return token

    def received(self, group, after):
        names, rows, send, recv, srcs, lands, ks = self.rs[group]
        whole = srcs[0].ndim == 2

        def wait_refs(j, src, land):
            return (src.at[pl.ds(0, rows[j])] if whole else src.at[0]), land.at[0]

        _, lands = _split_wait(srcs, list(range(len(names))), lands, send, recv, list(range(len(names))), wait_refs,
                               after, f"rs_wait{group}", ks=ks)
        return dict(zip(names, lands))


def _pair_sum(a, b, name):
    n, rows, _ = a.shape
    tr = 384 if rows % 384 == 0 else rows

    def kern(a_ref, b_ref, o_ref):
        o_ref[...] = (a_ref[...].astype(F32) + b_ref[...].astype(F32)).astype(BF16)

    blk = pl.BlockSpec((1, tr, D), lambda i, t: (i, t, 0))
    return pl.pallas_call(
        kern, grid=(n, rows // tr), in_specs=[blk, blk], out_specs=blk,
        out_shape=jax.ShapeDtypeStruct(a.shape, BF16), compiler_params=_cparams(2), name=name)(a, b)


def _sum_contributions(r_ref):
    g = r_ref[0].astype(F32)
    for slot in range(1, r_ref.shape[0]):
        g = g + r_ref[slot].astype(F32)
    return g


def _adam_math(g, w, m, v):
    c1 = 1.0 / (1.0 - ADAM_B1 ** ADAM_STEP)
    c2 = 1.0 / (1.0 - ADAM_B2 ** ADAM_STEP)
    mn = ADAM_B1 * m + (1.0 - ADAM_B1) * g
    vn = ADAM_B2 * v + (1.0 - ADAM_B2) * (g * g)
    return -ADAM_LR * ((mn * c1) / (jnp.sqrt(vn * c2) + ADAM_EPS) + ADAM_WD * w), mn, vn


def _adamw(R, w, m, v, *, tr, name, layer=None, prev=None):
    rows, C = w.shape[-2:]
    nprev = 0 if prev is None else 4

    def kern(r_ref, w_ref, m_ref, v_ref, *rest):
        g_out, d_out, m_out, v_out = rest[nprev:]
        g = _sum_contributions(r_ref)
        g_out[...] = g
        d_out[...], m_out[...], v_out[...] = _adam_math(g, w_ref[...], m_ref[...], v_ref[...])

    if layer is None:
        tile = pl.BlockSpec((tr, C), lambda i: (i, 0))
    else:
        tile = pl.BlockSpec((None, tr, C), lambda i: (layer, i, 0))
    shp = jax.ShapeDtypeStruct(w.shape, F32)
    return pl.pallas_call(
        kern, grid=(rows // tr,),
        in_specs=[pl.BlockSpec((R.shape[0], tr, C), lambda i: (0, i, 0)), tile, tile, tile]
        + [pl.BlockSpec(memory_space=pl.ANY)] * nprev,
        out_specs=[tile] * 4, out_shape=[shp] * 4,
        input_output_aliases={4 + k: k for k in range(nprev)},
        compiler_params=_cparams(1), name=name)(R, w, m, v, *(prev or ()))


def _adamw_pool_group(R, w, m, v):
    rows = SEC_ROWS["pg"]

    def kern(r_ref, w_ref, m_ref, v_ref, g_out, d_out, m_out, v_out):
        g = _sum_contributions(r_ref)
        g_out[0] = g
        d_out[0], m_out[0], v_out[0] = _adam_math(g, w_ref[0], m_ref[0], v_ref[0])

    blk = pl.BlockSpec((1, rows, PGD), lambda i: (i, 0, 0))
    shp = jax.ShapeDtypeStruct((POOL_G, rows, PGD), F32)
    return pl.pallas_call(
        kern, grid=(POOL_G,),
        in_specs=[pl.BlockSpec((NDEV, rows, PGD), lambda i: (0, 0, i)), blk, blk, blk],
        out_specs=[blk] * 4, out_shape=[shp] * 4, compiler_params=_cparams(1), name="adamw_pg")(R, w, m, v)


def _adamw_transposed(R, w, m, v, name):
    rows = R.shape[1]
    tr = 128

    def kern(r_ref, w_ref, m_ref, v_ref, g_out, d_out, m_out, v_out):
        g = _sum_contributions(r_ref).T
        g_out[...] = g
        d_out[...], m_out[...], v_out[...] = _adam_math(g, w_ref[...], m_ref[...], v_ref[...])

    tile = pl.BlockSpec((D, tr), lambda i: (0, i))
    shp = jax.ShapeDtypeStruct((D, rows), F32)
    return pl.pallas_call(
        kern, grid=(rows // tr,),
        in_specs=[pl.BlockSpec((R.shape[0], tr, D), lambda i: (0, i, 0)), tile, tile, tile],
        out_specs=[tile] * 4, out_shape=[shp] * 4, compiler_params=_cparams(1), name=name)(R, w, m, v)


def _pack_sections(w_qkv, w_attn_out, w_pool_in, w_pool_group, w_ffn_gate_up, w_ffn_down):
    pg = w_pool_group[0].transpose(1, 0, 2).reshape(SEC_ROWS["pg"], D)
    return {"qkv": w_qkv[0].T, "wo": w_attn_out[0], "wpi": w_pool_in[0], "gu0": w_ffn_gate_up[0].T,
            "gu1": w_ffn_gate_up[1].T, "d0": w_ffn_down[0], "d1": w_ffn_down[1], "pg": pg}


def _vec_pack(attn_norm, ffn_norm, final_norm, pool_norm_sh, pool_scale_sh, me):
    def place(sh):
        return lax.dynamic_update_slice(jnp.zeros((1, D), F32), sh, (0, me * 128))
    return jnp.concatenate([attn_norm, ffn_norm, final_norm.reshape(1, D), place(pool_norm_sh),
                            place(pool_scale_sh), jnp.zeros((2, D), F32)], axis=0)


def _vec_unpack(p, me):
    def take(r):
        return lax.dynamic_slice(p[r:r + 1], (0, me * 128), (1, 128))
    return p[0:1], p[1:3], p[3], take(4), take(5)


def kernel(x, attn_norm, w_qkv, w_attn_out, pool_norm, w_pool_in, w_pool_group, pool_scale, ffn_norm, w_ffn_gate_up, w_ffn_down, final_norm, loss_target, m_attn_norm, m_w_qkv, m_w_attn_out, m_pool_norm, m_w_pool_in, m_w_pool_group, m_pool_scale, m_ffn_norm, m_w_ffn_gate_up, m_w_ffn_down, m_final_norm, v_attn_norm, v_w_qkv, v_w_attn_out, v_pool_norm, v_w_pool_in, v_w_pool_group, v_pool_scale, v_ffn_norm, v_w_ffn_gate_up, v_w_ffn_down, v_final_norm):
    me = 4 * lax.axis_index("x") + 2 * lax.axis_index("y") + lax.axis_index("c")

    def make_shards(wq, wo, wpi, wpg, wgu, wd, pn, ps):
        shards = _pack_sections(wq, wo, wpi, wpg, wgu, wd)
        shards["pv"] = jnp.concatenate([pn, ps, jnp.zeros((6, 128), F32)], axis=0)
        return shards

    comm = _Comm((w_qkv, w_attn_out, w_pool_in, w_pool_group, w_ffn_gate_up, w_ffn_down, pool_norm, pool_scale),
                 make_shards, me, placed={"qkv": _place_transposed(w_qkv[0], me, "place_qkv")})

    grad_x, vec = _local_step(x[0], loss_target[0], comm, attn_norm, ffn_norm, final_norm)

    small = ((attn_norm, ffn_norm, final_norm, pool_norm, pool_scale),
             (m_attn_norm, m_ffn_norm, m_final_norm, m_pool_norm, m_pool_scale),
             (v_attn_norm, v_ffn_norm, v_final_norm, v_pool_norm, v_pool_scale))
    small, grad_x = lax.optimization_barrier((small, grad_x))
    vw, vm, vv = (_vec_pack(*s, me) for s in small)

    gu_t = [jnp.swapaxes(a, 1, 2) for a in (w_ffn_gate_up, m_w_ffn_gate_up, v_w_ffn_gate_up)]
    res = {}
    gu_res, d_res = None, None
    vec_out = None
    after = grad_x
    for group in range(len(RS_GROUPS)):
        if group == len(RS_GROUPS) - 1:
            VR = _bcast_all(vec, "exchange_vector_grads", deps=(after,))
            vec_out = _adamw(VR, vw, vm, vv, tr=8, name="adamw_vec")
            after = vec_out[0]
        for n, R in comm.received(group, after).items():
            if n in ("d0", "d1"):
                d_res = _adamw(R, w_ffn_down, m_w_ffn_down, v_w_ffn_down, tr=352, name=f"adamw_{n}",
                               layer=int(n[1]), prev=d_res)
                after = d_res[0]
            elif n in ("gu0", "gu1"):
                gu_res = _adamw(R, *gu_t, tr=352, name=f"adamw_{n}", layer=int(n[2]), prev=gu_res)
                after = gu_res[0]
            elif n == "pg":
                out = _adamw_pool_group(R, w_pool_group[0], m_w_pool_group[0], v_w_pool_group[0])
                res["pg"] = tuple(a[None] for a in out)
                after = out[0]
            elif n in ("wo", "wpi"):
                w, m, v = ((w_attn_out, m_w_attn_out, v_w_attn_out) if n == "wo"
                           else (w_pool_in, m_w_pool_in, v_w_pool_in))
                res[n] = _adamw(R, w[0], m[0], v[0], tr=128, name=f"adamw_{n}")
                res[n] = tuple(a[None] for a in res[n])
                after = res[n][0]
            else:
                out = _adamw_transposed(R, w_qkv[0], m_w_qkv[0], v_w_qkv[0], "adamw_qkv")
                res["qkv"] = tuple(a[None] for a in out)
                after = out[0]
    res["gu"] = tuple(jnp.swapaxes(a, 1, 2) for a in gu_res)
    res["d"] = tuple(d_res)

    outs = []
    for kind in range(4):
        an, fn, fin, pn, ps = _vec_unpack(vec_out[kind], me)
        outs.append((an, res["qkv"][kind], res["wo"][kind], pn, res["wpi"][kind], res["pg"][kind], ps, fn,
                     res["gu"][kind], res["d"][kind], fin))
    loss = 0.5 * jnp.sum(vec_out[0][6]) / D
    return (loss, grad_x[None]) + outs[0] + outs[1] + outs[2] + outs[3]
```

```python
import jax
import jax.numpy as jnp
from jax import lax
from jax.experimental import pallas as pl
from jax.experimental.pallas import tpu as pltpu

F32 = jnp.float32
BF16 = jnp.bfloat16

D = 1024
NDEV = 8
HEADS = 8
HD = 128
QB = 128
NGROUPS = 3
DILS = (1, 4, 16)
DFF = 2816
HCH = 1408
POOL_G = 4
PGD = 256
RMS_EPS = 1e-6
NEG = -1e30

ADAM_LR = 0.001
ADAM_B1 = 0.9
ADAM_B2 = 0.999
ADAM_EPS = 1e-08
ADAM_WD = 0.01
ADAM_STEP = 10

VMEM_LIMIT = 52 * 1024 * 1024

SECTIONS = (("qkv", 1152), ("wo", 128), ("wpi", 128), ("gu0", 704), ("gu1", 704),
            ("d0", 352), ("d1", 352), ("pg", 32))
LOC_OFF = {}
GLB_OFF = {}
_o = 0
for _n, _r in SECTIONS:
    LOC_OFF[_n] = _o
    GLB_OFF[_n] = _o * NDEV
    _o += _r
PACK_ROWS = _o
GLB_ROWS = PACK_ROWS * NDEV
SEC_ROWS = dict(SECTIONS)
SEC_ROWS["pv"] = 8


def _cparams(n_grid):
    return pltpu.CompilerParams(dimension_semantics=("arbitrary",) * n_grid, vmem_limit_bytes=VMEM_LIMIT)


def _shard_pos(name, dev):
    n = SEC_ROWS[name]
    if name in ("gu0", "gu1"):
        return ((dev % 4) // 2) * (2 * HCH) + (dev // 4) * HCH + (dev % 2) * n
    return dev * n


def _mm(a, b, *, mode, M, N, K, tm, tn, tk, out_dtype, name, a_off=(0, 0), b_off=(0, 0), res=None,
        out_rows=None, out_off=0, out_prev=None, deps=()):
    nm, nn, nk = M // tm, N // tn, K // tk
    assert nm * tm == M and nn * tn == N and nk * tk == K
    if mode == "nn":
        a_bs, b_bs = (tm, tk), (tk, tn)
        a_ix = lambda i, j, k: (i, k)
        b_ix = lambda i, j, k: (k, j)
        dims = (((1,), (0,)), ((), ()))
    elif mode == "nt":
        a_bs, b_bs = (tm, tk), (tn, tk)
        a_ix = lambda i, j, k: (i, k)
        b_ix = lambda i, j, k: (j, k)
        dims = (((1,), (1,)), ((), ()))
    else:
        a_bs, b_bs = (tk, tm), (tk, tn)
        a_ix = lambda i, j, k: (k, i)
        b_ix = lambda i, j, k: (k, j)
        dims = (((0,), (0,)), ((), ()))

    def spec(bs, ix, off):
        def im(i, j, k):
            r, c = ix(i, j, k)
            return (r + off[0], c + off[1])
        return pl.BlockSpec(bs, im)

    in_specs = [spec(a_bs, a_ix, a_off), spec(b_bs, b_ix, b_off)]
    args = [a, b]
    if res is not None:
        in_specs.append(pl.BlockSpec((tm, tn), lambda i, j, k: (i, j)))
        args.append(res)
    out_shape = jax.ShapeDtypeStruct((M if out_rows is None else out_rows, N), out_dtype)
    out_spec = pl.BlockSpec((tm, tn), lambda i, j, k: (i + out_off, j))
    has_res = res is not None
    extra = list(deps) + ([out_prev] if out_prev is not None else [])
    for dep in extra:
        in_specs.append(pl.BlockSpec(memory_space=pl.ANY))
        args.append(dep)
    o_pos = 2 + int(has_res) + len(extra)
    aliases = {len(args) - 1: 0} if out_prev is not None else {}

    def kern(*refs):
        a_ref, b_ref = refs[0], refs[1]
        res_ref = refs[2] if has_res else None
        o_ref = refs[o_pos]
        av = a_ref[...]
        bv = b_ref[...]
        if av.dtype != BF16:
            av = av.astype(BF16)
        if bv.dtype != BF16:
            bv = bv.astype(BF16)
        part = lax.dot_general(av, bv, dims, preferred_element_type=F32)

        def write(val):
            if has_res:
                val = val + res_ref[...]
            o_ref[...] = val.astype(out_dtype)

        if nk == 1:
            write(part)
        else:
            acc_ref = refs[-1]
            k = pl.program_id(2)

            @pl.when(k == 0)
            def _():
                acc_ref[...] = part

            @pl.when(k > 0)
            def _():
                acc_ref[...] += part

            @pl.when(k == nk - 1)
            def _():
                write(acc_ref[...])

    scratch = [pltpu.VMEM((tm, tn), F32)] if nk > 1 else []
    return pl.pallas_call(
        kern, grid=(nm, nn, nk), in_specs=in_specs, out_specs=out_spec, out_shape=out_shape,
        scratch_shapes=scratch, input_output_aliases=aliases, compiler_params=_cparams(3), name=name)(*args)


def _mm_rms_bwd(a, b, x, g, dres, *, mode, M, K, tm, name, b_off=(0, 0), deps=()):
    nd = len(deps)
    b_bs = (K, D) if mode == "nn" else (D, K)
    dims = (((1,), (0,)), ((), ())) if mode == "nn" else (((1,), (1,)), ((), ()))

    def kern(a_ref, b_ref, x_ref, g_ref, dres_ref, *rest):
        dx_ref, dg_ref = rest[nd:]
        i = pl.program_id(0)
        av = a_ref[...]
        if av.dtype != BF16:
            av = av.astype(BF16)
        dhv = lax.dot_general(av, b_ref[...], dims, preferred_element_type=F32)
        xv = x_ref[...]
        r = lax.rsqrt(jnp.mean(xv * xv, axis=-1, keepdims=True) + RMS_EPS)
        xhat = xv * r
        gy = dhv * g_ref[...]
        dx_ref[...] = dres_ref[...] + r * (gy - xhat * jnp.mean(gy * xhat, axis=-1, keepdims=True))
        part = jnp.sum(dhv * xhat, axis=0, keepdims=True)

        @pl.when(i == 0)
        def _():
            dg_ref[...] = part

        @pl.when(i > 0)
        def _():
            dg_ref[...] += part

    row = pl.BlockSpec((tm, D), lambda i: (i, 0))
    vec = pl.BlockSpec((1, D), lambda i: (0, 0))
    return pl.pallas_call(
        kern, grid=(M // tm,),
        in_specs=[pl.BlockSpec((tm, K), lambda i: (i, 0)),
                  pl.BlockSpec(b_bs, lambda i: b_off, pipeline_mode=pl.Buffered(1)), row, vec, row]
        + [pl.BlockSpec(memory_space=pl.ANY)] * nd,
        out_specs=[row, vec],
        out_shape=[jax.ShapeDtypeStruct((M, D), F32), jax.ShapeDtypeStruct((1, D), F32)],
        compiler_params=_cparams(1), name=name)(a, b, x, g, dres, *deps)


def _norm_tail(xv, gv, rest, head):
    r = lax.rsqrt(jnp.mean(xv * xv, axis=-1, keepdims=True) + RMS_EPS)
    xhat = xv * r
    if not head:
        xo_ref, h_ref = rest
        xo_ref[...] = xv
        h_ref[...] = (xhat * gv).astype(BF16)
        return
    t_ref, dx_ref, dg_ref, ls_ref = rest
    i = pl.program_id(0)
    e = xhat * gv - t_ref[...]
    dy = e * (1.0 / D)
    gy = dy * gv
    dx_ref[...] = r * (gy - xhat * jnp.mean(gy * xhat, axis=-1, keepdims=True))
    dgp = jnp.sum(dy * xhat, axis=0, keepdims=True)
    lsp = jnp.sum(e * e, axis=0, keepdims=True)

    @pl.when(i == 0)
    def _():
        dg_ref[...] = dgp
        ls_ref[...] = lsp

    @pl.when(i > 0)
    def _():
        dg_ref[...] += dgp
        ls_ref[...] += lsp


def _ffn_fwd(h, wgu, wd, res, g, *, name, tgt=None):
    S = h.shape[0]
    tm = 256
    nj = DFF // HCH
    head = tgt is not None

    def kern(h_ref, wgu_ref, wd_ref, res_ref, g_ref, *rest):
        t_refs, (gu_ref, act_ref), tail = rest[:int(head)], rest[int(head):int(head) + 2], rest[int(head) + 2:]
        hv = h_ref[...]
        for j in range(nj):
            gu = lax.dot_general(hv, wgu_ref[2 * HCH * j:2 * HCH * (j + 1), :], (((1,), (1,)), ((), ())),
                                 preferred_element_type=F32)
            gu_ref[:, 2 * HCH * j:2 * HCH * (j + 1)] = gu.astype(BF16)
            gate = gu[:, :HCH]
            act_ref[:, HCH * j:HCH * (j + 1)] = (gate * jax.nn.sigmoid(gate) * gu[:, HCH:]).astype(BF16)
        xv = res_ref[...] + jnp.dot(act_ref[...], wd_ref[...], preferred_element_type=F32)
        _norm_tail(xv, g_ref[...], tuple(t_refs) + tuple(tail), head)

    row = pl.BlockSpec((tm, D), lambda i: (i, 0))
    vec = pl.BlockSpec((1, D), lambda i: (0, 0))
    in_specs = [row, pl.BlockSpec((2 * DFF, D), lambda i: (0, 0), pipeline_mode=pl.Buffered(1)),
                pl.BlockSpec((DFF, D), lambda i: (0, 0), pipeline_mode=pl.Buffered(1)), row, vec]
    out_specs = [pl.BlockSpec((tm, 2 * DFF), lambda i: (i, 0)), pl.BlockSpec((tm, DFF), lambda i: (i, 0))]
    out_shape = [jax.ShapeDtypeStruct((S, 2 * DFF), BF16), jax.ShapeDtypeStruct((S, DFF), BF16)]
    args = [h, wgu, wd, res, g]
    if head:
        in_specs, args = in_specs + [row], args + [tgt]
        out_specs += [row, vec, vec]
        out_shape += [jax.ShapeDtypeStruct((S, D), F32), jax.ShapeDtypeStruct((1, D), F32),
                      jax.ShapeDtypeStruct((1, D), F32)]
    else:
        out_specs += [row, row]
        out_shape += [jax.ShapeDtypeStruct((S, D), F32), jax.ShapeDtypeStruct((S, D), BF16)]
    outs = pl.pallas_call(kern, grid=(S // tm,), in_specs=in_specs, out_specs=out_specs, out_shape=out_shape,
                          compiler_params=_cparams(1), name=name)(*args)
    return outs[0], outs[1], tuple(outs[2:])


def _mm_res_norm(a, b, res, g, *, K, tm, name, b_off=(0, 0), tgt=None):
    M = a.shape[0]
    head = tgt is not None

    def kern(a_ref, b_ref, res_ref, g_ref, *rest):
        xv = res_ref[...] + jnp.dot(a_ref[...], b_ref[...], preferred_element_type=F32)
        _norm_tail(xv, g_ref[...], rest, head)

    row = pl.BlockSpec((tm, D), lambda i: (i, 0))
    vec = pl.BlockSpec((1, D), lambda i: (0, 0))
    in_specs = [pl.BlockSpec((tm, K), lambda i: (i, 0)),
                pl.BlockSpec((K, D), lambda i: b_off, pipeline_mode=pl.Buffered(1)), row, vec]
    if head:
        return pl.pallas_call(
            kern, grid=(M // tm,), in_specs=in_specs + [row], out_specs=[row, vec, vec],
            out_shape=[jax.ShapeDtypeStruct((M, D), F32), jax.ShapeDtypeStruct((1, D), F32),
                       jax.ShapeDtypeStruct((1, D), F32)],
            compiler_params=_cparams(1), name=name)(a, b, res, g, tgt)
    return pl.pallas_call(
        kern, grid=(M // tm,), in_specs=in_specs, out_specs=[row, row],
        out_shape=[jax.ShapeDtypeStruct((M, D), F32), jax.ShapeDtypeStruct((M, D), BF16)],
        compiler_params=_cparams(1), name=name)(a, b, res, g)


def _rms_fwd(x, g, name, deps=()):
    S = x.shape[0]
    tr = 512

    def kern(x_ref, g_ref, *rest):
        h_ref = rest[-1]
        xv = x_ref[...]
        r = lax.rsqrt(jnp.mean(xv * xv, axis=-1, keepdims=True) + RMS_EPS)
        h_ref[...] = (xv * r * g_ref[...]).astype(BF16)

    return pl.pallas_call(
        kern, grid=(S // tr,),
        in_specs=[pl.BlockSpec((tr, D), lambda i: (i, 0)), pl.BlockSpec((1, D), lambda i: (0, 0))]
        + [pl.BlockSpec(memory_space=pl.ANY)] * len(deps),
        out_specs=pl.BlockSpec((tr, D), lambda i: (i, 0)),
        out_shape=jax.ShapeDtypeStruct((S, D), BF16), compiler_params=_cparams(1), name=name)(x, g, *deps)


def _chunks_put(scr, val):
    for c in range(scr.shape[0]):
        scr[c] = val[:, c * 128:(c + 1) * 128]


def _chunks_get(scr):
    return jnp.concatenate([scr[c] for c in range(scr.shape[0])], axis=1)


def _chunks_rows(scr, r, n, dil):
    return jnp.concatenate([scr.at[c][pl.ds(r, n, stride=dil), :] for c in range(scr.shape[0])], axis=1)


def _chunks_add_rows(scr, val, r, n, dil, accumulate):
    for c in range(scr.shape[0]):
        rows = pl.ds(r, n, stride=dil)
        piece = val[:, c * 128:(c + 1) * 128]
        tile = scr.at[c]
        tile[rows, :] = tile[rows, :] + piece if accumulate else piece


def _rms_fwd_folded(x, g, name, deps=()):
    S = x.shape[0]
    tr = 512
    dils = DILS[1:]

    def kern(x_ref, g_ref, *rest):
        outs, scr = rest[len(deps):-1], rest[-1]
        xv = x_ref[...]
        r = lax.rsqrt(jnp.mean(xv * xv, axis=-1, keepdims=True) + RMS_EPS)
        h = (xv * r * g_ref[...]).astype(BF16)
        outs[0][...] = h
        _chunks_put(scr, h.astype(F32))
        for o_ref, dil in zip(outs[1:], dils):
            for res in range(dil):
                o_ref[res] = _chunks_rows(scr, res, tr // dil, dil).astype(BF16)

    return pl.pallas_call(
        kern, grid=(S // tr,),
        in_specs=[pl.BlockSpec((tr, D), lambda i: (i, 0)), pl.BlockSpec((1, D), lambda i: (0, 0))]
        + [pl.BlockSpec(memory_space=pl.ANY)] * len(deps),
        out_specs=[pl.BlockSpec((tr, D), lambda i: (i, 0))]
        + [pl.BlockSpec((dil, tr // dil, D), lambda i: (0, i, 0)) for dil in dils],
        out_shape=[jax.ShapeDtypeStruct((S, D), BF16)]
        + [jax.ShapeDtypeStruct((dil, S // dil, D), BF16) for dil in dils],
        scratch_shapes=[pltpu.VMEM((D // 128, tr, 128), F32)],
        compiler_params=_cparams(1), name=name)(x, g, *deps)


def _rms_bwd(dh, x, g, dres, name, folded=()):
    S = x.shape[0]
    tr = 512
    nf = len(folded)

    def kern(dh_ref, *rest):
        f_refs = rest[:nf]
        x_ref, g_ref, dres_ref, dx_ref, dg_ref = rest[nf:nf + 5]
        i = pl.program_id(0)
        xv = x_ref[...]
        if nf:
            acc_ref = rest[nf + 5]
            _chunks_put(acc_ref, dh_ref[...].astype(F32))
            for f_ref in f_refs:
                dil = f_ref.shape[0]
                for res in range(dil):
                    _chunks_add_rows(acc_ref, f_ref[res], res, tr // dil, dil, True)
            dhv = _chunks_get(acc_ref)
        else:
            dhv = dh_ref[...].astype(F32)
        r = lax.rsqrt(jnp.mean(xv * xv, axis=-1, keepdims=True) + RMS_EPS)
        xhat = xv * r
        gy = dhv * g_ref[...]
        dx_ref[...] = dres_ref[...] + r * (gy - xhat * jnp.mean(gy * xhat, axis=-1, keepdims=True))
        part = jnp.sum(dhv * xhat, axis=0, keepdims=True)

        @pl.when(i == 0)
        def _():
            dg_ref[...] = part

        @pl.when(i > 0)
        def _():
            dg_ref[...] += part

    row = pl.BlockSpec((tr, D), lambda i: (i, 0))
    vec = pl.BlockSpec((1, D), lambda i: (0, 0))
    fspecs = [pl.BlockSpec((f.shape[0], tr // f.shape[0], D), lambda i: (0, i, 0)) for f in folded]
    return pl.pallas_call(
        kern, grid=(S // tr,), in_specs=[row] + fspecs + [row, vec, row], out_specs=[row, vec],
        out_shape=[jax.ShapeDtypeStruct((S, D), F32), jax.ShapeDtypeStruct((1, D), F32)],
        scratch_shapes=[pltpu.VMEM((D // 128, tr, 128), F32)] if nf else [],
        compiler_params=_cparams(1), name=name)(dh, *folded, x, g, dres)


def _ffn_bwd(dxo, wd, wgu, gu, xin, gain, name):
    S = dxo.shape[0]
    tm = 256
    nj = DFF // HCH

    def kern(dx_ref, wd_ref, wgu_ref, gu_ref, x_ref, g_ref, dgu_ref, dxin_ref, dg_ref):
        i = pl.program_id(0)
        dxv = dx_ref[...]
        dxb = dxv.astype(BF16)
        for j in range(nj):
            c0 = 2 * HCH * j
            dact = lax.dot_general(dxb, wd_ref[HCH * j:HCH * (j + 1), :], (((1,), (1,)), ((), ())),
                                   preferred_element_type=F32)
            gate = gu_ref[:, c0:c0 + HCH].astype(F32)
            up = gu_ref[:, c0 + HCH:c0 + 2 * HCH].astype(F32)
            sig = jax.nn.sigmoid(gate)
            silu = gate * sig
            dgu_ref[:, c0:c0 + HCH] = (dact * up * (sig * (1.0 + gate * (1.0 - sig)))).astype(BF16)
            dgu_ref[:, c0 + HCH:c0 + 2 * HCH] = (dact * silu).astype(BF16)
        dhv = jnp.dot(dgu_ref[...], wgu_ref[...], preferred_element_type=F32)
        xv = x_ref[...]
        r = lax.rsqrt(jnp.mean(xv * xv, axis=-1, keepdims=True) + RMS_EPS)
        xhat = xv * r
        gy = dhv * g_ref[...]
        dxin_ref[...] = dxv + r * (gy - xhat * jnp.mean(gy * xhat, axis=-1, keepdims=True))
        part = jnp.sum(dhv * xhat, axis=0, keepdims=True)

        @pl.when(i == 0)
        def _():
            dg_ref[...] = part

        @pl.when(i > 0)
        def _():
            dg_ref[...] += part

    row = pl.BlockSpec((tm, D), lambda i: (i, 0))
    wide = pl.BlockSpec((tm, 2 * DFF), lambda i: (i, 0))
    vec = pl.BlockSpec((1, D), lambda i: (0, 0))
    return pl.pallas_call(
        kern, grid=(S // tm,),
        in_specs=[row, pl.BlockSpec((DFF, D), lambda i: (0, 0), pipeline_mode=pl.Buffered(1)),
                  pl.BlockSpec((2 * DFF, D), lambda i: (0, 0), pipeline_mode=pl.Buffered(1)), wide, row, vec],
        out_specs=[wide, row, vec],
        out_shape=[jax.ShapeDtypeStruct((S, 2 * DFF), BF16), jax.ShapeDtypeStruct((S, D), F32),
                   jax.ShapeDtypeStruct((1, D), F32)],
        compiler_params=_cparams(1), name=name)(dxo, wd, wgu, gu, xin, gain)


def _trail(u, *, backward, name):
    S = u.shape[0]

    def kern(u_ref, o_ref):
        g = pl.program_id(0)
        for grp in range(POOL_G):
            @pl.when(g == grp)
            def _(grp=grp):
                uv = u_ref[...].astype(F32)
                row = lax.broadcasted_iota(jnp.int32, uv.shape, 0)
                cnt = jnp.minimum(row + 1, 2 << grp).astype(F32)
                s = uv / cnt if backward else uv
                for k in (1, 2, 4, 8)[:grp + 1]:
                    if backward:
                        sh = jnp.where(row < S - k, pltpu.roll(s, S - k, 0), 0.0)
                    else:
                        sh = jnp.where(row >= k, pltpu.roll(s, k, 0), 0.0)
                    s = s + sh
                if backward:
                    o_ref[...] = (s - uv).astype(BF16)
                else:
                    o_ref[...] = (s / cnt - uv).astype(BF16)

    blk = pl.BlockSpec((S, PGD), lambda g: (0, g))
    return pl.pallas_call(
        kern, grid=(POOL_G,), in_specs=[blk], out_specs=blk,
        out_shape=jax.ShapeDtypeStruct((S, D), BF16), compiler_params=_cparams(1), name=name)(u)


def _pool_out(yd, G, scale, xres):
    S = yd.shape[0]
    tm = min(S, 4096)

    def kern(y_ref, w_ref, s_ref, x_ref, o_ref):
        z = jnp.dot(y_ref[...], w_ref[...], preferred_element_type=F32)
        o_ref[...] = x_ref[...] + z * s_ref[...]

    tile = pl.BlockSpec((tm, PGD), lambda i, g: (i, g))
    return pl.pallas_call(
        kern, grid=(S // tm, POOL_G),
        in_specs=[tile, pl.BlockSpec((PGD, PGD), lambda i, g: (0, g)),
                  pl.BlockSpec((1, PGD), lambda i, g: (0, g)), tile],
        out_specs=tile, out_shape=jax.ShapeDtypeStruct((S, D), F32),
        compiler_params=_cparams(2), name="pool_out")(yd, G, scale, xres)


def _pool_out_bwd(dz, yd, G, scale, deps=()):
    S = yd.shape[0]
    tm = min(S, 4096)
    ni = S // tm
    nd = len(deps)

    def kern(dz_ref, y_ref, w_ref, s_ref, *rest):
        dy_ref, ds_ref, dw_ref, acc_ref = rest[nd:]
        i = pl.program_id(1)
        dzv = dz_ref[...]
        yv = y_ref[...]
        wv = w_ref[...]
        zraw = jnp.dot(yv, wv, preferred_element_type=F32)
        dsp = jnp.sum(dzv * zraw, axis=0, keepdims=True)
        dzr = (dzv * s_ref[...]).astype(BF16)
        dy_ref[...] = lax.dot_general(dzr, wv, (((1,), (1,)), ((), ())), preferred_element_type=F32)
        dwp = lax.dot_general(yv, dzr, (((0,), (0,)), ((), ())), preferred_element_type=F32)

        @pl.when(i == 0)
        def _():
            ds_ref[...] = dsp
            acc_ref[...] = dwp

        @pl.when(i > 0)
        def _():
            ds_ref[...] += dsp
            acc_ref[...] += dwp

        @pl.when(i == ni - 1)
        def _():
            dw_ref[...] = acc_ref[...].astype(BF16)

    tile = pl.BlockSpec((tm, PGD), lambda g, i: (i, g))
    return pl.pallas_call(
        kern, grid=(POOL_G, ni),
        in_specs=[tile, tile, pl.BlockSpec((PGD, PGD), lambda g, i: (0, g)),
                  pl.BlockSpec((1, PGD), lambda g, i: (0, g))] + [pl.BlockSpec(memory_space=pl.ANY)] * nd,
        out_specs=[tile, pl.BlockSpec((1, PGD), lambda g, i: (0, g)),
                   pl.BlockSpec((PGD, PGD), lambda g, i: (0, g))],
        out_shape=[jax.ShapeDtypeStruct((S, D), F32), jax.ShapeDtypeStruct((1, D), F32),
                   jax.ShapeDtypeStruct((PGD, D), BF16)],
        scratch_shapes=[pltpu.VMEM((PGD, PGD), F32)],
        compiler_params=_cparams(2), name="pool_out_bwd")(dz, yd, G, scale, *deps)


def _bias_table():
    qi = jnp.arange(QB)[:, None]
    ki = jnp.arange(2 * QB)[None, :]
    delta = QB + qi - ki
    inband = (delta >= 0) & (delta <= QB)
    n = NGROUPS * HEADS
    slopes = jnp.exp2(-8.0 * jnp.arange(1, n + 1, dtype=F32) / n).reshape(NGROUPS, HEADS)
    dil = jnp.asarray(DILS, F32)
    bias = -slopes[:, :, None, None] * (delta.astype(F32)[None, None] * dil[:, None, None, None])
    return jnp.where(inband[None, None], bias, NEG)


def _attn_fwd(qkv_f, bias, nb, name):
    S = qkv_f.shape[0]
    nblk = S // QB
    scale = HD ** -0.5

    def kern(q_ref, k2_ref, kp_ref, v2_ref, vp_ref, b_ref, o_ref, l_ref, s_scr, p_scr, r_scr):
        s_id = pl.program_id(0)
        col = lax.broadcasted_iota(jnp.int32, (QB, 2 * QB), 1)
        lane = lax.broadcasted_iota(jnp.int32, (QB, HD), 1)

        def keys(sub, cur2_ref, prev_ref, sl):
            if sub:
                return cur2_ref[:, sl]
            return jnp.concatenate([prev_ref[:, sl], cur2_ref[0:QB, sl]], axis=0)

        for sub in range(2):
            for h in range(HEADS):
                sl = slice(h * HD, (h + 1) * HD)
                s_scr[sub * HEADS + h] = lax.dot_general(
                    q_ref[sub * QB:(sub + 1) * QB, sl], keys(sub, k2_ref, kp_ref, sl), (((1,), (1,)), ((), ())),
                    preferred_element_type=F32)
        for sub in range(2):
            has_prev = jnp.bitwise_and(2 * s_id + sub, nb - 1) != 0
            dead = jnp.logical_and(col < QB, jnp.logical_not(has_prev))
            lse_all = jnp.zeros((QB, HD), F32)
            for h in range(HEADS):
                u = sub * HEADS + h
                s = s_scr[u] * scale + b_ref[h]
                s = jnp.where(dead, NEG, s)
                m = jnp.max(s, axis=-1, keepdims=True)
                p = jnp.exp(s - m)
                den = jnp.sum(p, axis=-1, keepdims=True)
                p_scr[u] = p.astype(BF16)
                r_scr[u] = jnp.broadcast_to(1.0 / den, (QB, HD))
                lse_all = jnp.where(lane == h, m + jnp.log(den), lse_all)
            l_ref[sub * QB:(sub + 1) * QB, :] = lse_all
        for sub in range(2):
            for h in range(HEADS):
                u = sub * HEADS + h
                sl = slice(h * HD, (h + 1) * HD)
                o = jnp.dot(p_scr[u], keys(sub, v2_ref, vp_ref, sl), preferred_element_type=F32) * r_scr[u]
                o_ref[sub * QB:(sub + 1) * QB, sl] = o.astype(BF16)

    def pair(colblk):
        return pl.BlockSpec((2 * QB, D), lambda s: (s, colblk))

    def prev(colblk):
        return pl.BlockSpec((QB, D), lambda s: (jnp.maximum(2 * s - 1, 0), colblk))

    return pl.pallas_call(
        kern, grid=(nblk // 2,),
        in_specs=[pair(0), pair(1), prev(1), pair(2), prev(2), pl.BlockSpec((HEADS, QB, 2 * QB), lambda s: (0, 0, 0))],
        out_specs=[pl.BlockSpec((2 * QB, D), lambda s: (s, 0)), pl.BlockSpec((2 * QB, HD), lambda s: (s, 0))],
        out_shape=[jax.ShapeDtypeStruct((S, D), BF16), jax.ShapeDtypeStruct((S, HD), F32)],
        scratch_shapes=[pltpu.VMEM((2 * HEADS, QB, 2 * QB), F32), pltpu.VMEM((2 * HEADS, QB, 2 * QB), BF16),
                        pltpu.VMEM((2 * HEADS, QB, HD), F32)],
        compiler_params=_cparams(1), name=name)(qkv_f, qkv_f, qkv_f, qkv_f, qkv_f, bias)


def _natural(ref, scr, tm):
    dil = ref.shape[0]
    for res in range(dil):
        _chunks_add_rows(scr, ref[res].astype(F32), res, tm // dil, dil, False)
    return _chunks_get(scr)


def _attn_merge(os, lses):
    S = os[0].shape[0]
    tm = 512

    def kern(o0, o1, o2, l0, l1, l2, om_ref, lm_ref, ls1, ls2, os1, os2):
        la = l0[...]
        lb = _natural(l1, ls1, tm)
        lc = _natural(l2, ls2, tm)
        m = jnp.maximum(jnp.maximum(la, lb), lc)
        e0, e1, e2 = jnp.exp(la - m), jnp.exp(lb - m), jnp.exp(lc - m)
        tot = e0 + e1 + e2
        lm_ref[...] = m + jnp.log(tot)
        w0, w1, w2 = e0 / tot, e1 / tot, e2 / tot
        for res in range(o1.shape[0]):
            _chunks_add_rows(os1, o1[res].astype(F32), res, tm // o1.shape[0], o1.shape[0], False)
        for res in range(o2.shape[0]):
            _chunks_add_rows(os2, o2[res].astype(F32), res, tm // o2.shape[0], o2.shape[0], False)
        for h in range(HEADS):
            sl = slice(h * HD, (h + 1) * HD)
            acc = w0[:, h:h + 1] * o0[:, sl].astype(F32) + w1[:, h:h + 1] * os1[h] + w2[:, h:h + 1] * os2[h]
            om_ref[:, sl] = acc.astype(BF16)

    def spec(a, c):
        if a.ndim == 2:
            return pl.BlockSpec((tm, c), lambda i: (i, 0))
        return pl.BlockSpec((a.shape[0], tm // a.shape[0], c), lambda i: (0, i, 0))

    return pl.pallas_call(
        kern, grid=(S // tm,),
        in_specs=[spec(a, D) for a in os] + [spec(a, HD) for a in lses],
        out_specs=[pl.BlockSpec((tm, D), lambda i: (i, 0)), pl.BlockSpec((tm, HD), lambda i: (i, 0))],
        out_shape=[jax.ShapeDtypeStruct((S, D), BF16), jax.ShapeDtypeStruct((S, HD), F32)],
        scratch_shapes=[pltpu.VMEM((1, tm, HD), F32), pltpu.VMEM((1, tm, HD), F32),
                        pltpu.VMEM((HEADS, tm, HD), F32), pltpu.VMEM((HEADS, tm, HD), F32)],
        compiler_params=_cparams(1), name="attn_merge")(*os, *lses)


def _attn_bwd_prep(do, o, lse):
    S = o.shape[0]
    tm = 512
    dils = DILS[1:]

    def kern(do_ref, o_ref, l_ref, *rest):
        do_outs, l_outs, d_outs = rest[0:3], rest[3:5], rest[5:8]
        do_scr, l_scr, d_scr = rest[8:11]
        lane = lax.broadcasted_iota(jnp.int32, (tm, HD), 1)
        acc = jnp.zeros((tm, HD), F32)
        for h in range(HEADS):
            sl = slice(h * HD, (h + 1) * HD)
            prod = do_ref[:, sl] * o_ref[:, sl].astype(F32)
            acc = jnp.where(lane == h, jnp.sum(prod, axis=-1, keepdims=True), acc)
        d_scr[0] = acc
        l_scr[0] = l_ref[...]
        _chunks_put(do_scr, do_ref[...])
        do_outs[0][...] = do_ref[...].astype(BF16)
        d_outs[0][...] = acc
        for j, dil in enumerate(dils):
            for res in range(dil):
                n = tm // dil
                do_outs[1 + j][res] = _chunks_rows(do_scr, res, n, dil).astype(BF16)
                l_outs[j][res] = _chunks_rows(l_scr, res, n, dil)
                d_outs[1 + j][res] = _chunks_rows(d_scr, res, n, dil)

    def nat(c):
        return pl.BlockSpec((tm, c), lambda i: (i, 0))

    def fol(dil, c):
        return pl.BlockSpec((dil, tm // dil, c), lambda i: (0, i, 0))

    def shapes(c, dt, with_natural):
        first = [jax.ShapeDtypeStruct((S, c), dt)] if with_natural else []
        return first + [jax.ShapeDtypeStruct((dil, S // dil, c), dt) for dil in dils]

    outs = pl.pallas_call(
        kern, grid=(S // tm,), in_specs=[nat(D), nat(D), nat(HD)],
        out_specs=[nat(D)] + [fol(dil, D) for dil in dils] + [fol(dil, HD) for dil in dils]
        + [nat(HD)] + [fol(dil, HD) for dil in dils],
        out_shape=shapes(D, BF16, True) + shapes(HD, F32, False) + shapes(HD, F32, True),
        scratch_shapes=[pltpu.VMEM((HEADS, tm, HD), F32), pltpu.VMEM((1, tm, HD), F32), pltpu.VMEM((1, tm, HD), F32)],
        compiler_params=_cparams(1), name="attn_bwd_prep")(do, o, lse)
    return outs[0:3], [lse] + list(outs[3:5]), outs[5:8]


def _attn_bwd(qkv_f, do_f, lse_f, delta_f, bias, nb, name):
    S = qkv_f.shape[0]
    nblk = S // QB
    scale = HD ** -0.5

    npair = nblk // 2

    def kern(q_ref, k2_ref, kp_ref, v2_ref, vp_ref, do_ref, l_ref, d_ref, b_ref, out_ref, dq_c, dk_c, dv_c,
             s_scr, dp_scr, ds_scr, p_scr):
        s_id = pl.program_id(0)

        @pl.when(s_id == 0)
        def _():
            dq_c[...] = jnp.zeros_like(dq_c)
            dk_c[...] = jnp.zeros_like(dk_c)
            dv_c[...] = jnp.zeros_like(dv_c)

        @pl.when(s_id == npair)
        def _():
            out_ref[:, 0:D] = dq_c[...].astype(BF16)
            out_ref[:, D:2 * D] = dk_c[...].astype(BF16)
            out_ref[:, 2 * D:3 * D] = dv_c[...].astype(BF16)

        def keys(sub, cur2_ref, prev_ref, sl):
            if sub:
                return cur2_ref[:, sl]
            return jnp.concatenate([prev_ref[:, sl], cur2_ref[0:QB, sl]], axis=0)

        @pl.when(s_id < npair)
        def _():
            col = lax.broadcasted_iota(jnp.int32, (QB, 2 * QB), 1)
            out_ref[:, 0:D] = dq_c[...].astype(BF16)
            for sub in range(2):
                rows = slice(sub * QB, (sub + 1) * QB)
                for h in range(HEADS):
                    sl = slice(h * HD, (h + 1) * HD)
                    u = sub * HEADS + h
                    s_scr[u] = lax.dot_general(q_ref[rows, sl], keys(sub, k2_ref, kp_ref, sl),
                                               (((1,), (1,)), ((), ())), preferred_element_type=F32)
                    dp_scr[u] = lax.dot_general(do_ref[rows, sl], keys(sub, v2_ref, vp_ref, sl),
                                                (((1,), (1,)), ((), ())), preferred_element_type=F32)
            for sub in range(2):
                rows = slice(sub * QB, (sub + 1) * QB)
                has_prev = jnp.bitwise_and(2 * s_id + sub, nb - 1) != 0
                dead = jnp.logical_and(col < QB, jnp.logical_not(has_prev))
                lv = l_ref[rows, :]
                dv_ = d_ref[rows, :]
                for h in range(HEADS):
                    u = sub * HEADS + h
                    s = s_scr[u] * scale + b_ref[h]
                    s = jnp.where(dead, NEG, s)
                    p = jnp.exp(s - lv[:, h:h + 1])
                    ds_scr[u] = (p * (dp_scr[u] - dv_[:, h:h + 1]) * scale).astype(BF16)
                    p_scr[u] = p.astype(BF16)
            for h in range(HEADS):
                sl = slice(h * HD, (h + 1) * HD)
                parts = []
                for sub in range(2):
                    rows = slice(sub * QB, (sub + 1) * QB)
                    u = sub * HEADS + h
                    ds = ds_scr[u]
                    dq_c[rows, sl] = jnp.dot(ds, keys(sub, k2_ref, kp_ref, sl), preferred_element_type=F32)
                    dkk = lax.dot_general(ds, q_ref[rows, sl], (((0,), (0,)), ((), ())), preferred_element_type=F32)
                    dvv = lax.dot_general(p_scr[u], do_ref[rows, sl], (((0,), (0,)), ((), ())),
                                          preferred_element_type=F32)
                    parts.append((dkk, dvv))
                for which, carry, base in ((0, dk_c, D), (1, dv_c, 2 * D)):
                    first, second = parts[0][which], parts[1][which]
                    cols = slice(base + h * HD, base + (h + 1) * HD)
                    out_ref[0:QB, cols] = carry[0:QB, sl].astype(BF16)
                    out_ref[QB:2 * QB, cols] = (carry[QB:2 * QB, sl] + first[:QB]).astype(BF16)
                    carry[0:QB, sl] = first[QB:] + second[:QB]
                    carry[QB:2 * QB, sl] = second[QB:]

    last = npair - 1

    def pair(colblk, c):
        return pl.BlockSpec((2 * QB, c), lambda s: (jnp.minimum(s, last), colblk))

    def prev(colblk):
        return pl.BlockSpec((QB, D), lambda s: (jnp.maximum(2 * jnp.minimum(s, last) - 1, 0), colblk))

    return pl.pallas_call(
        kern, grid=(npair + 1,),
        in_specs=[pair(0, D), pair(1, D), prev(1), pair(2, D), prev(2), pair(0, D), pair(0, HD), pair(0, HD),
                  pl.BlockSpec((HEADS, QB, 2 * QB), lambda s: (0, 0, 0))],
        out_specs=pl.BlockSpec((2 * QB, 3 * D), lambda s: (jnp.maximum(s - 1, 0), 0)),
        out_shape=jax.ShapeDtypeStruct((S, 3 * D), BF16),
        scratch_shapes=[pltpu.VMEM((2 * QB, D), F32), pltpu.VMEM((2 * QB, D), F32), pltpu.VMEM((2 * QB, D), F32),
                        pltpu.VMEM((2 * HEADS, QB, 2 * QB), F32), pltpu.VMEM((2 * HEADS, QB, 2 * QB), F32),
                        pltpu.VMEM((2 * HEADS, QB, 2 * QB), BF16), pltpu.VMEM((2 * HEADS, QB, 2 * QB), BF16)],
        compiler_params=_cparams(1), name=name)(qkv_f, qkv_f, qkv_f, qkv_f, qkv_f, do_f, lse_f, delta_f, bias)


def _local_step(x, tgt, comm, attn_norm, ffn_norm, final_norm):
    S = x.shape[0]
    bias = _bias_table()
    g_attn = attn_norm.reshape(1, D)
    g_f0 = ffn_norm[0:1]
    g_f1 = ffn_norm[1:2]
    g_fin = final_norm.reshape(1, D)
    W = {}

    def ffn_fwd(xin, h, l, next_gain, target=None):
        return _ffn_fwd(h, W[f"gu{l}"], W[f"d{l}"], xin, next_gain, tgt=target, name=f"ffn_fwd{l}")

    def ffn_bwd(dxo, xin, gain, h, gu, act, l, rs_group):
        dgu, dxin, dgain = _ffn_bwd(dxo, W[f"d{l}"], W[f"gu{l}"], gu, xin, gain, f"ffn_bwd{l}")
        gw_d = _mm(act, dxo, mode="tn", M=DFF, N=D, K=S, tm=HCH, tn=D, tk=2048, out_dtype=BF16, name=f"gw_d{l}")
        gw_gu = _mm(dgu, h, mode="tn", M=2 * DFF, N=D, K=S, tm=HCH, tn=D, tk=2048, out_dtype=BF16, name=f"gw_gu{l}")
        return dxin, dgain, comm.send_grads(rs_group, {f"d{l}": gw_d, f"gu{l}": gw_gu})

    nbs = [S // QB // dil for dil in DILS]
    hf = _rms_fwd_folded(x, g_attn, "rms_attn", deps=comm.ag_tokens)
    hf = [h.reshape(S, D) for h in hf]
    W.update(comm.weights(0, hf[0]))
    qkv_f, o_f, lse_f = [], [], []
    for g, dil in enumerate(DILS):
        qkv_f.append(_mm(hf[g], W["qkv"], mode="nt", M=S, N=3 * D, K=D, tm=2048, tn=1024, tk=D, out_dtype=BF16,
                         b_off=(3 * g, 0), name=f"qkv_proj{g}"))
        og, lg = _attn_fwd(qkv_f[g], bias[g], nbs[g], f"attn_fwd{g}")
        o_f.append(og if dil == 1 else og.reshape(dil, S // dil, D))
        lse_f.append(lg if dil == 1 else lg.reshape(dil, S // dil, HD))
    passing = [comm.pass_on(1, tuple(o_f)), comm.pass_on(2, tuple(o_f))]
    (o_f, lse_f), passing = lax.optimization_barrier(((o_f, lse_f), passing))
    o, lse = _attn_merge(o_f, lse_f)
    W.update(comm.weights(1, (o, passing[0])))
    x1, h1 = _mm_res_norm(o, W["wo"], x, g_f0, K=D, tm=1024, name="attn_out")
    pv = W["pv"].reshape(NDEV, 8, 128)
    pool_norm, pool_scale = pv[:, 0, :].reshape(1, D), pv[:, 1, :].reshape(1, D)
    gu0, act0, (x2, h2) = ffn_fwd(x1, h1, 0, pool_norm)

    W.update(comm.weights(2, (x2, passing[1])))
    u = _mm(h2, W["wpi"], mode="nn", M=S, N=D, K=D, tm=1024, tn=D, tk=D, out_dtype=F32, name="pool_in")
    yd = _trail(u, backward=False, name="trail_fwd")
    x3 = _pool_out(yd, W["pg"], pool_scale, x2)
    h3 = _rms_fwd(x3, g_f1, "rms_ffn1")
    gu1, act1, (dx4, d_fin, lossvec) = ffn_fwd(x3, h3, 1, g_fin, target=tgt)

    dx3, d_f1, token = ffn_bwd(dx4, x3, g_f1, h3, gu1, act1, 1, 0)
    dyd, d_scale, gw_pg = _pool_out_bwd(dx3, yd, W["pg"], pool_scale, deps=(token,))
    du = _trail(dyd, backward=True, name="trail_bwd")
    gw_pi = _mm(h2, du, mode="tn", M=D, N=D, K=S, tm=D, tn=D, tk=S, out_dtype=BF16, name="gw_pi")
    token = comm.send_grads(1, {"pg": gw_pg, "wpi": gw_pi})
    dx2, d_pool = _mm_rms_bwd(du, W["wpi"], x2, pool_norm, dx3, mode="nt", M=S, K=D, tm=1024, deps=(token,),
                              name="pool_in_bwd")
    dx1, d_f0, token = ffn_bwd(dx2, x1, g_f0, h1, gu0, act0, 0, 2)

    gw_o = _mm(o, dx1, mode="tn", M=D, N=D, K=S, tm=D, tn=D, tk=2048, out_dtype=BF16, deps=(token,), name="gw_o")
    do = _mm(dx1, W["wo"], mode="nt", M=S, N=D, K=D, tm=1024, tn=D, tk=D, out_dtype=F32, deps=(token,),
             name="attn_out_bwd")
    do_f, lse_ff, delta_f = _attn_bwd_prep(do, o, lse)
    dqkv_f, gw_qkv = [], None
    for g in range(NGROUPS):
        dqkv_f.append(_attn_bwd(qkv_f[g], do_f[g].reshape(S, D), lse_ff[g].reshape(S, HD),
                                delta_f[g].reshape(S, HD), bias[g], nbs[g], f"attn_bwd{g}"))
        gw_qkv = _mm(dqkv_f[g], hf[g], mode="tn", M=3 * D, N=D, K=S, tm=1024, tn=D, tk=S, out_dtype=BF16,
                     out_rows=NGROUPS * 3 * D, out_off=3 * g, out_prev=gw_qkv, name=f"gw_qkv{g}")
    token = comm.send_grads_pairwise({"wo": gw_o, "qkv": gw_qkv})
    dh0_f = [None] * NGROUPS
    for g in reversed(range(NGROUPS)):
        dh0_f[g] = _mm(dqkv_f[g], W["qkv"], mode="nn", M=S, N=D, K=3 * D, tm=1024, tn=D, tk=3 * D, out_dtype=F32,
                       b_off=(g, 0), deps=(token,), name=f"qkv_proj_bwd{g}")
        if g == NGROUPS - 1:
            token = comm.pass_grads(dh0_f[g])
    folded = [dh0_f[g].reshape(dil, S // dil, D) for g, dil in enumerate(DILS) if dil > 1]
    grad_x, d_attn = _rms_bwd(dh0_f[0], x, g_attn, dx1, "rms_attn_bwd", folded=folded)

    vec = jnp.concatenate([d_attn, d_f0, d_f1, d_fin, d_pool, d_scale, lossvec, jnp.zeros((1, D), F32)], axis=0)
    return grad_x, vec


def _mesh_pos():
    x, y, c = lax.axis_index("x"), lax.axis_index("y"), lax.axis_index("c")
    return x, y, c, 4 * x + 2 * y + c


def _peer(x, y, c, k):
    kx, ky, kc = (k >> 2) & 1, (k >> 1) & 1, k & 1
    px = 1 - x if kx else x
    py = 1 - y if ky else y
    pc = 1 - c if kc else c
    return (px, py, pc), 4 * px + 2 * py + pc


ANY = pl.BlockSpec(memory_space=pl.ANY)


HBM = pl.BlockSpec(memory_space=pltpu.HBM)
SEMS = pl.BlockSpec(memory_space=pltpu.SEMAPHORE)
EFFECT = pltpu.SideEffectType.DATAFLOW_SIDE_EFFECTING
NPEER = NDEV - 1

AG_GROUPS = (("qkv",), ("wo", "gu0", "d0", "pv"), ("wpi", "pg", "gu1", "d1"))
AG_ORDER = tuple(n for grp in AG_GROUPS for n in grp)
RS_GROUPS = (("d1", "gu1"), ("pg", "wpi"), ("d0", "gu0"), ("wo", "qkv"))


def _hbm(a):
    return pltpu.with_memory_space_constraint(a, pltpu.HBM)


def _remote(src, dst, send, recv, peer):
    return pltpu.make_async_remote_copy(src_ref=src, dst_ref=dst, send_sem=send, recv_sem=recv, device_id=peer,
                                        device_id_type=pl.DeviceIdType.MESH)


def _bcast_all(v, name, deps=()):
    W = v.shape[1]
    nd = len(deps)

    def kern(v_ref, *rest):
        o_ref, send, recv, lsem = rest[nd:]
        x, y, c, me = _mesh_pos()
        own = pltpu.make_async_copy(v_ref, o_ref.at[me], lsem)
        own.start()
        cps = [_remote(v_ref, o_ref.at[me], send.at[k - 1], recv.at[k - 1], _peer(x, y, c, k)[0])
               for k in range(1, NDEV)]
        for cp in cps:
            cp.start()
        for cp in cps:
            cp.wait_recv()
            cp.wait_send()
        own.wait()

    return pl.pallas_call(
        kern, in_specs=[ANY] * (1 + nd), out_specs=ANY, out_shape=jax.ShapeDtypeStruct((NDEV, 8, W), F32),
        scratch_shapes=[pltpu.SemaphoreType.DMA((NPEER,)), pltpu.SemaphoreType.DMA((NPEER,)),
                        pltpu.SemaphoreType.DMA(())],
        name=name)(v, *deps)


ALL_KS = tuple(range(1, NDEV))
AG_KS1 = (1, 2, 4, 6)
AG_KS2 = (2, 4, 6)
RS_KS_PAIR = (1, 3, 5, 7)
RS_KS_CHIPS = (2, 4, 6)


def _split_start(srcs, src_of, lands, copy_refs, name, deps=(), ks=ALL_KS, to=None):
    ns, n, nd, nk = len(srcs), len(lands), len(deps), len(ks)

    def body(*refs):
        ins, land = refs[:ns], refs[ns:ns + n]
        send, recv = refs[ns + n + nd], refs[ns + n + nd + 1]
        token = refs[-1]
        x, y, c, me = _mesh_pos()
        for j in range(n):
            for i, k in enumerate(ks):
                _, pid = _peer(x, y, c, k)
                dest, _ = _peer(x, y, c, k if to is None else to)
                src, dst = copy_refs(j, (land[j] if src_of[j] is None else ins[src_of[j]]), land[j], me, pid, i)
                _remote(src, dst, send.at[j * nk + i], recv.at[j * nk + i], dest).start()
        token[...] = jnp.zeros_like(token)

    outs = pl.pallas_call(
        body, name=name,
        out_shape=(pltpu.SemaphoreType.DMA((n * nk,)), pltpu.SemaphoreType.DMA((n * nk,)))
        + tuple(pltpu.HBM(a.shape, a.dtype) for a in srcs) + tuple(pltpu.HBM(a.shape, a.dtype) for a in lands)
        + (jax.ShapeDtypeStruct((8, 128), F32),),
        in_specs=(HBM,) * (ns + n) + (ANY,) * nd,
        out_specs=(SEMS, SEMS) + (HBM,) * (ns + n) + (pl.BlockSpec(memory_space=pltpu.VMEM),),
        input_output_aliases={i: 2 + i for i in range(ns + n)},
        compiler_params=pltpu.CompilerParams(has_side_effects=EFFECT),
    )(*[_hbm(a) for a in srcs], *[_hbm(a) for a in lands], *deps)
    return outs[0], outs[1], list(outs[2:2 + ns]), list(outs[2 + ns:2 + ns + n]), outs[-1]


def _split_wait(srcs, src_of, lands, send, recv, sem_rows, wait_refs, after, name, ks=ALL_KS):
    ns, n, nk = len(srcs), len(lands), len(ks)
    after = tuple(after) if isinstance(after, (tuple, list)) else (after,)

    def body(*refs):
        ins, land = refs[:ns], refs[ns:ns + n]
        send_ref, recv_ref = refs[ns + n], refs[ns + n + 1]
        x, y, c, me = _mesh_pos()
        for j in range(n):
            for i, k in enumerate(ks):
                peer, _ = _peer(x, y, c, k)
                src, dst = wait_refs(j, (land[j] if src_of[j] is None else ins[src_of[j]]), land[j])
                sem = sem_rows[j] * nk + i
                cp = _remote(src, dst, send_ref.at[sem], recv_ref.at[sem], peer)
                cp.wait_send()
                cp.wait_recv()

    outs = pl.pallas_call(
        body, name=name,
        out_shape=tuple(pltpu.HBM(a.shape, a.dtype) for a in srcs) + tuple(pltpu.HBM(a.shape, a.dtype) for a in lands),
        in_specs=(HBM,) * (ns + n) + (SEMS, SEMS) + (ANY,) * len(after),
        out_specs=(HBM,) * (ns + n),
        input_output_aliases={i: i for i in range(ns + n)},
        compiler_params=pltpu.CompilerParams(has_side_effects=EFFECT),
    )(*srcs, *lands, send, recv, *after)
    return list(outs[:ns]), list(outs[ns:])


def _ag_dtype(name):
    return F32 if name == "pv" else BF16


def _ag_align(name):
    return 8 if name == "pv" else 16


def _place_transposed(w, me, name):
    rows = w.shape[1]
    nblk = rows // 128

    def kern(me_ref, w_ref, o_ref):
        o_ref[...] = w_ref[...].T.astype(BF16)

    grid_spec = pltpu.PrefetchScalarGridSpec(
        num_scalar_prefetch=1, grid=(nblk,),
        in_specs=[pl.BlockSpec((D, 128), lambda i, me_ref: (0, i))],
        out_specs=pl.BlockSpec((128, D), lambda i, me_ref: (me_ref[0] * nblk + i, 0)))
    return pl.pallas_call(
        kern, grid_spec=grid_spec, out_shape=jax.ShapeDtypeStruct((NDEV * rows, D), BF16),
        compiler_params=_cparams(1), name=name)(me.reshape(1).astype(jnp.int32), w)


class _Comm:
    def __init__(self, params, make_shards, me, placed):
        self.me = me
        self.ag_land, self.ag_sems, self.ag_tokens, self.ag_passing = {}, {}, (), {}
        self.rs = []
        deps = ()
        for part, names in enumerate((AG_GROUPS[0], AG_ORDER[len(AG_GROUPS[0]):])):
            rows = [SEC_ROWS[n] for n in names]
            if part == 0:
                lands = [placed[n] for n in names]
            else:
                params, deps = lax.optimization_barrier((params, deps))
                shards = make_shards(*params)
                lands = [lax.dynamic_update_slice(lax.empty((NDEV * r, shards[n].shape[1]), _ag_dtype(n)),
                                                  shards[n].astype(_ag_dtype(n)), (_shard_pos(n, me), 0))
                         for n, r in zip(names, rows)]

            def copy_refs(j, src, land, me, pid, i, names=names, rows=rows):
                own = land.at[pl.ds(pl.multiple_of(_shard_pos(names[j], me), _ag_align(names[j])), rows[j])]
                return own, own

            send, recv, _, lands, token = _split_start([], [None] * len(names), lands, copy_refs, f"ag_start{part}",
                                                       deps=deps, ks=AG_KS1)
            deps = (token,)
            self.ag_tokens += (token,)
            for j, n in enumerate(names):
                self.ag_land[n] = lands[j]
                self.ag_sems[n] = (send, recv, j)

    def pass_on(self, group, after):
        names = AG_GROUPS[group]
        send, recv = self.ag_sems[names[0]][:2]
        idx = [self.ag_sems[n][2] for n in names]
        rows = [SEC_ROWS[n] for n in names]
        none = [None] * len(names)

        def wait_refs(j, src, land):
            return land.at[pl.ds(0, rows[j])], land.at[pl.ds(0, rows[j])]

        _, lands = _split_wait([], none, [self.ag_land[n] for n in names], send, recv, idx,
                               wait_refs, after, f"ag_wait{group}", ks=AG_KS1)

        def copy_refs(j, src, land, me, pid, i):
            theirs = land.at[pl.ds(pl.multiple_of(_shard_pos(names[j], pid), _ag_align(names[j])), rows[j])]
            return theirs, theirs

        send, recv, _, lands, token = _split_start([], none, lands, copy_refs, f"ag_pass{group}", ks=AG_KS2, to=1)
        self.ag_passing[group] = (send, recv, lands, wait_refs)
        return token

    def weights(self, group, after):
        names = AG_GROUPS[group]
        if group not in self.ag_passing:
            after = self.pass_on(group, after)
        send, recv, lands, wait_refs = self.ag_passing[group]
        _, lands = _split_wait([], [None] * len(names), lands, send, recv, list(range(len(names))), wait_refs, after,
                               f"ag_pass_wait{group}", ks=AG_KS2)
        return dict(zip(names, lands))

    def send_grads(self, group, gws):
        names = RS_GROUPS[group]
        rows = [SEC_ROWS[n] for n in names]
        grads = [gws[n] for n in names]
        me = self.me
        lands = [lax.dynamic_update_slice(
            lax.empty((NDEV, r, D), BF16),
            lax.dynamic_slice(g, (_shard_pos(n, me), 0), (r, D))[None], (me, 0, 0))
            for n, r, g in zip(names, rows, grads)]

        def copy_refs(j, src, land, me, pid, i):
            return src.at[pl.ds(pl.multiple_of(_shard_pos(names[j], pid), 16), rows[j])], land.at[me]

        send, recv, srcs, lands, token = _split_start(grads, list(range(len(names))), lands, copy_refs,
                                                      f"rs_start{group}")
        self.rs.append((names, rows, send, recv, srcs, lands, ALL_KS))
        return token

    def send_grads_pairwise(self, gws):
        names = RS_GROUPS[-1]
        rows = [SEC_ROWS[n] for n in names]
        grads = [gws[n] for n in names]
        idx = list(range(len(names)))
        lands = [lax.empty((len(RS_KS_PAIR), r, D), BF16) for r in rows]

        def copy_refs(j, src, land, me, pid, i):
            return src.at[pl.ds(pl.multiple_of(_shard_pos(names[j], pid), 16), rows[j])], land.at[i]

        send, recv, srcs, lands, token = _split_start(grads, idx, lands, copy_refs, "rs_pair_start",
                                                      ks=RS_KS_PAIR, to=1)
        self.pair = (names, rows, send, recv, srcs, lands)
        return token

    def pass_grads(self, after):
        names, rows, send, recv, srcs, lands = self.pair
        idx = list(range(len(names)))
        me = self.me

        def wait_refs(j, src, land):
            return src.at[pl.ds(0, rows[j])], land.at[0]

        srcs, lands = _split_wait(srcs, idx, lands, send, recv, idx, wait_refs, after, "rs_pair_wait", ks=RS_KS_PAIR)
        sums = []
        for n, r, g, got in zip(names, rows, srcs, lands):
            mine = jnp.stack([lax.dynamic_slice(g, (_shard_pos(n, jnp.bitwise_xor(me, k)), 0), (r, D))
                              for k in (0,) + RS_KS_CHIPS])
            sums.append(_pair_sum(mine, got, f"rs_pair_sum_{n}"))
        lands = [lax.dynamic_update_slice(lax.empty(p.shape, BF16), p[0:1], (0, 0, 0)) for p in sums]

        def copy_refs(j, src, land, me, pid, i):
            return src.at[i + 1], land.at[i + 1]

        send, recv, sums, lands, token = _split_start(sums, idx, lands, copy_refs, f"rs_start{len(RS_GROUPS) - 1}",
                                                      ks=RS_KS_CHIPS)
        self.rs.append((names, rows, send, recv, sums, lands, RS_KS_CHIPS))
        return token

    def received(self, group, after):
        names, rows, send, recv, srcs, lands, ks = self.rs[group]
        whole = srcs[0].ndim == 2

        def wait_refs(j, src, land):
            return (src.at[pl.ds(0, rows[j])] if whole else src.at[0]), land.at[0]

        _, lands = _split_wait(srcs, list(range(len(names))), lands, send, recv, list(range(len(names))), wait_refs,
                               after, f"rs_wait{group}", ks=ks)
        return dict(zip(names, lands))


def _pair_sum(a, b, name):
    n, rows, _ = a.shape
    tr = 384 if rows % 384 == 0 else rows

    def kern(a_ref, b_ref, o_ref):
        o_ref[...] = (a_ref[...].astype(F32) + b_ref[...].astype(F32)).astype(BF16)

    blk = pl.BlockSpec((1, tr, D), lambda i, t: (i, t, 0))
    return pl.pallas_call(
        kern, grid=(n, rows // tr), in_specs=[blk, blk], out_specs=blk,
        out_shape=jax.ShapeDtypeStruct(a.shape, BF16), compiler_params=_cparams(2), name=name)(a, b)


def _sum_contributions(r_ref):
    g = r_ref[0].astype(F32)
    for slot in range(1, r_ref.shape[0]):
        g = g + r_ref[slot].astype(F32)
    return g


def _adam_math(g, w, m, v):
    c1 = 1.0 / (1.0 - ADAM_B1 ** ADAM_STEP)
    c2 = 1.0 / (1.0 - ADAM_B2 ** ADAM_STEP)
    mn = ADAM_B1 * m + (1.0 - ADAM_B1) * g
    vn = ADAM_B2 * v + (1.0 - ADAM_B2) * (g * g)
    return -ADAM_LR * ((mn * c1) / (jnp.sqrt(vn * c2) + ADAM_EPS) + ADAM_WD * w), mn, vn


def _adamw(R, w, m, v, *, tr, name, layer=None, prev=None):
    rows, C = w.shape[-2:]
    nprev = 0 if prev is None else 4

    def kern(r_ref, w_ref, m_ref, v_ref, *rest):
        g_out, d_out, m_out, v_out = rest[nprev:]
        g = _sum_contributions(r_ref)
        g_out[...] = g
        d_out[...], m_out[...], v_out[...] = _adam_math(g, w_ref[...], m_ref[...], v_ref[...])

    if layer is None:
        tile = pl.BlockSpec((tr, C), lambda i: (i, 0))
    else:
        tile = pl.BlockSpec((None, tr, C), lambda i: (layer, i, 0))
    shp = jax.ShapeDtypeStruct(w.shape, F32)
    return pl.pallas_call(
        kern, grid=(rows // tr,),
        in_specs=[pl.BlockSpec((R.shape[0], tr, C), lambda i: (0, i, 0)), tile, tile, tile]
        + [pl.BlockSpec(memory_space=pl.ANY)] * nprev,
        out_specs=[tile] * 4, out_shape=[shp] * 4,
        input_output_aliases={4 + k: k for k in range(nprev)},
        compiler_params=_cparams(1), name=name)(R, w, m, v, *(prev or ()))


def _adamw_pool_group(R, w, m, v):
    rows = SEC_ROWS["pg"]

    def kern(r_ref, w_ref, m_ref, v_ref, g_out, d_out, m_out, v_out):
        g = _sum_contributions(r_ref)
        g_out[0] = g
        d_out[0], m_out[0], v_out[0] = _adam_math(g, w_ref[0], m_ref[0], v_ref[0])

    blk = pl.BlockSpec((1, rows, PGD), lambda i: (i, 0, 0))
    shp = jax.ShapeDtypeStruct((POOL_G, rows, PGD), F32)
    return pl.pallas_call(
        kern, grid=(POOL_G,),
        in_specs=[pl.BlockSpec((NDEV, rows, PGD), lambda i: (0, 0, i)), blk, blk, blk],
        out_specs=[blk] * 4, out_shape=[shp] * 4, compiler_params=_cparams(1), name="adamw_pg")(R, w, m, v)


def _adamw_transposed(R, w, m, v, name):
    rows = R.shape[1]
    tr = 128

    def kern(r_ref, w_ref, m_ref, v_ref, g_out, d_out, m_out, v_out):
        g = _sum_contributions(r_ref).T
        g_out[...] = g
        d_out[...], m_out[...], v_out[...] = _adam_math(g, w_ref[...], m_ref[...], v_ref[...])

    tile = pl.BlockSpec((D, tr), lambda i: (0, i))
    shp = jax.ShapeDtypeStruct((D, rows), F32)
    return pl.pallas_call(
        kern, grid=(rows // tr,),
        in_specs=[pl.BlockSpec((R.shape[0], tr, D), lambda i: (0, i, 0)), tile, tile, tile],
        out_specs=[tile] * 4, out_shape=[shp] * 4, compiler_params=_cparams(1), name=name)(R, w, m, v)


def _pack_sections(w_qkv, w_attn_out, w_pool_in, w_pool_group, w_ffn_gate_up, w_ffn_down):
    pg = w_pool_group[0].transpose(1, 0, 2).reshape(SEC_ROWS["pg"], D)
    return {"qkv": w_qkv[0].T, "wo": w_attn_out[0], "wpi": w_pool_in[0], "gu0": w_ffn_gate_up[0].T,
            "gu1": w_ffn_gate_up[1].T, "d0": w_ffn_down[0], "d1": w_ffn_down[1], "pg": pg}


def _vec_pack(attn_norm, ffn_norm, final_norm, pool_norm_sh, pool_scale_sh, me):
    def place(sh):
        return lax.dynamic_update_slice(jnp.zeros((1, D), F32), sh, (0, me * 128))
    return jnp.concatenate([attn_norm, ffn_norm, final_norm.reshape(1, D), place(pool_norm_sh),
                            place(pool_scale_sh), jnp.zeros((2, D), F32)], axis=0)


def _vec_unpack(p, me):
    def take(r):
        return lax.dynamic_slice(p[r:r + 1], (0, me * 128), (1, 128))
    return p[0:1], p[1:3], p[3], take(4), take(5)


def kernel(x, attn_norm, w_qkv, w_attn_out, pool_norm, w_pool_in, w_pool_group, pool_scale, ffn_norm, w_ffn_gate_up, w_ffn_down, final_norm, loss_target, m_attn_norm, m_w_qkv, m_w_attn_out, m_pool_norm, m_w_pool_in, m_w_pool_group, m_pool_scale, m_ffn_norm, m_w_ffn_gate_up, m_w_ffn_down, m_final_norm, v_attn_norm, v_w_qkv, v_w_attn_out, v_pool_norm, v_w_pool_in, v_w_pool_group, v_pool_scale, v_ffn_norm, v_w_ffn_gate_up, v_w_ffn_down, v_final_norm):
    me = 4 * lax.axis_index("x") + 2 * lax.axis_index("y") + lax.axis_index("c")

    def make_shards(wq, wo, wpi, wpg, wgu, wd, pn, ps):
        shards = _pack_sections(wq, wo, wpi, wpg, wgu, wd)
        shards["pv"] = jnp.concatenate([pn, ps, jnp.zeros((6, 128), F32)], axis=0)
        return shards

    comm = _Comm((w_qkv, w_attn_out, w_pool_in, w_pool_group, w_ffn_gate_up, w_ffn_down, pool_norm, pool_scale),
                 make_shards, me, placed={"qkv": _place_transposed(w_qkv[0], me, "place_qkv")})

    grad_x, vec = _local_step(x[0], loss_target[0], comm, attn_norm, ffn_norm, final_norm)

    small = ((attn_norm, ffn_norm, final_norm, pool_norm, pool_scale),
             (m_attn_norm, m_ffn_norm, m_final_norm, m_pool_norm, m_pool_scale),
             (v_attn_norm, v_ffn_norm, v_final_norm, v_pool_norm, v_pool_scale))
    small, grad_x = lax.optimization_barrier((small, grad_x))
    vw, vm, vv = (_vec_pack(*s, me) for s in small)

    gu_t = [jnp.swapaxes(a, 1, 2) for a in (w_ffn_gate_up, m_w_ffn_gate_up, v_w_ffn_gate_up)]
    res = {}
    gu_res, d_res = None, None
    vec_out = None
    after = grad_x
    for group in range(len(RS_GROUPS)):
        if group == len(RS_GROUPS) - 1:
            VR = _bcast_all(vec, "exchange_vector_grads", deps=(after,))
            vec_out = _adamw(VR, vw, vm, vv, tr=8, name="adamw_vec")
            after = vec_out[0]
        for n, R in comm.received(group, after).items():
            if n in ("d0", "d1"):
                d_res = _adamw(R, w_ffn_down, m_w_ffn_down, v_w_ffn_down, tr=352, name=f"adamw_{n}",
                               layer=int(n[1]), prev=d_res)
                after = d_res[0]
            elif n in ("gu0", "gu1"):
                gu_res = _adamw(R, *gu_t, tr=352, name=f"adamw_{n}", layer=int(n[2]), prev=gu_res)
                after = gu_res[0]
            elif n == "pg":
                out = _adamw_pool_group(R, w_pool_group[0], m_w_pool_group[0], v_w_pool_group[0])
                res["pg"] = tuple(a[None] for a in out)
                after = out[0]
            elif n in ("wo", "wpi"):
                w, m, v = ((w_attn_out, m_w_attn_out, v_w_attn_out) if n == "wo"
                           else (w_pool_in, m_w_pool_in, v_w_pool_in))
                res[n] = _adamw(R, w[0], m[0], v[0], tr=128, name=f"adamw_{n}")
                res[n] = tuple(a[None] for a in res[n])
                after = res[n][0]
            else:
                out = _adamw_transposed(R, w_qkv[0], m_w_qkv[0], v_w_qkv[0], "adamw_qkv")
                res["qkv"] = tuple(a[None] for a in out)
                after = out[0]
    res["gu"] = tuple(jnp.swapaxes(a, 1, 2) for a in gu_res)
    res["d"] = tuple(d_res)

    outs = []
    for kind in range(4):
        an, fn, fin, pn, ps = _vec_unpack(vec_out[kind], me)
        outs.append((an, res["qkv"][kind], res["wo"][kind], pn, res["wpi"][kind], res["pg"][kind], ps, fn,
                     res["gu"][kind], res["d"][kind], fin))
    loss = 0.5 * jnp.sum(vec_out[0][6]) / D
    return (loss, grad_x[None]) + outs[0] + outs[1] + outs[2] + outs[3]
```

```python
import jax
import jax.numpy as jnp
from jax import lax
from jax.experimental import pallas as pl
from jax.experimental.pallas import tpu as pltpu

F32 = jnp.float32
BF16 = jnp.bfloat16

D = 1024
NDEV = 8
HEADS = 8
HD = 128
QB = 128
NGROUPS = 3
DILS = (1, 4, 16)
DFF = 2816
HCH = 1408
POOL_G = 4
PGD = 256
RMS_EPS = 1e-6
NEG = -1e30

ADAM_LR = 0.001
ADAM_B1 = 0.9
ADAM_B2 = 0.999
ADAM_EPS = 1e-08
ADAM_WD = 0.01
ADAM_STEP = 10

VMEM_LIMIT = 52 * 1024 * 1024

SECTIONS = (("qkv", 1152), ("wo", 128), ("wpi", 128), ("gu0", 704), ("gu1", 704),
            ("d0", 352), ("d1", 352), ("pg", 32))
SEC_ROWS = dict(SECTIONS)
SEC_ROWS["pv"] = 8


def _cparams(n_grid):
    return pltpu.CompilerParams(dimension_semantics=("arbitrary",) * n_grid, vmem_limit_bytes=VMEM_LIMIT)


def _shard_pos(name, dev):
    n = SEC_ROWS[name]
    if name in ("gu0", "gu1"):
        return ((dev % 4) // 2) * (2 * HCH) + (dev // 4) * HCH + (dev % 2) * n
    return dev * n


def _mm(a, b, *, mode, M, N, K, tm, tn, tk, out_dtype, name, a_off=(0, 0), b_off=(0, 0), res=None,
        out_rows=None, out_off=0, out_prev=None, deps=()):
    nm, nn, nk = M // tm, N // tn, K // tk
    assert nm * tm == M and nn * tn == N and nk * tk == K
    if mode == "nn":
        a_bs, b_bs = (tm, tk), (tk, tn)
        a_ix = lambda i, j, k: (i, k)
        b_ix = lambda i, j, k: (k, j)
        dims = (((1,), (0,)), ((), ()))
    elif mode == "nt":
        a_bs, b_bs = (tm, tk), (tn, tk)
        a_ix = lambda i, j, k: (i, k)
        b_ix = lambda i, j, k: (j, k)
        dims = (((1,), (1,)), ((), ()))
    else:
        a_bs, b_bs = (tk, tm), (tk, tn)
        a_ix = lambda i, j, k: (k, i)
        b_ix = lambda i, j, k: (k, j)
        dims = (((0,), (0,)), ((), ()))

    def spec(bs, ix, off):
        def im(i, j, k):
            r, c = ix(i, j, k)
            return (r + off[0], c + off[1])
        return pl.BlockSpec(bs, im)

    in_specs = [spec(a_bs, a_ix, a_off), spec(b_bs, b_ix, b_off)]
    args = [a, b]
    if res is not None:
        in_specs.append(pl.BlockSpec((tm, tn), lambda i, j, k: (i, j)))
        args.append(res)
    out_shape = jax.ShapeDtypeStruct((M if out_rows is None else out_rows, N), out_dtype)
    out_spec = pl.BlockSpec((tm, tn), lambda i, j, k: (i + out_off, j))
    has_res = res is not None
    extra = list(deps) + ([out_prev] if out_prev is not None else [])
    for dep in extra:
        in_specs.append(pl.BlockSpec(memory_space=pl.ANY))
        args.append(dep)
    o_pos = 2 + int(has_res) + len(extra)
    aliases = {len(args) - 1: 0} if out_prev is not None else {}

    def kern(*refs):
        a_ref, b_ref = refs[0], refs[1]
        res_ref = refs[2] if has_res else None
        o_ref = refs[o_pos]
        av = a_ref[...]
        bv = b_ref[...]
        if av.dtype != BF16:
            av = av.astype(BF16)
        if bv.dtype != BF16:
            bv = bv.astype(BF16)
        part = lax.dot_general(av, bv, dims, preferred_element_type=F32)

        def write(val):
            if has_res:
                val = val + res_ref[...]
            o_ref[...] = val.astype(out_dtype)

        if nk == 1:
            write(part)
        else:
            acc_ref = refs[-1]
            k = pl.program_id(2)

            @pl.when(k == 0)
            def _():
                acc_ref[...] = part

            @pl.when(k > 0)
            def _():
                acc_ref[...] += part

            @pl.when(k == nk - 1)
            def _():
                write(acc_ref[...])

    scratch = [pltpu.VMEM((tm, tn), F32)] if nk > 1 else []
    return pl.pallas_call(
        kern, grid=(nm, nn, nk), in_specs=in_specs, out_specs=out_spec, out_shape=out_shape,
        scratch_shapes=scratch, input_output_aliases=aliases, compiler_params=_cparams(3), name=name)(*args)


def _mm_rms_bwd(a, b, x, g, dres, *, mode, M, K, tm, name, b_off=(0, 0), deps=(), folded=()):
    nd, nf = len(deps), len(folded)
    b_bs = (K, D) if mode == "nn" else (D, K)
    dims = (((1,), (0,)), ((), ())) if mode == "nn" else (((1,), (1,)), ((), ()))

    def kern(a_ref, b_ref, x_ref, g_ref, dres_ref, *rest):
        f_refs = rest[:nf]
        dx_ref, dg_ref = rest[nf + nd:nf + nd + 2]
        i = pl.program_id(0)
        av = a_ref[...]
        if av.dtype != BF16:
            av = av.astype(BF16)
        dhv = lax.dot_general(av, b_ref[...], dims, preferred_element_type=F32)
        if nf:
            acc_ref = rest[-1]
            _chunks_put(acc_ref, dhv)
            for f_ref in f_refs:
                dil = f_ref.shape[0]
                for res in range(dil):
                    _chunks_add_rows(acc_ref, f_ref[res], res, tm // dil, dil, True)
            dhv = _chunks_get(acc_ref)
        xv = x_ref[...]
        r = lax.rsqrt(jnp.mean(xv * xv, axis=-1, keepdims=True) + RMS_EPS)
        xhat = xv * r
        gy = dhv * g_ref[...]
        dx_ref[...] = dres_ref[...] + r * (gy - xhat * jnp.mean(gy * xhat, axis=-1, keepdims=True))
        part = jnp.sum(dhv * xhat, axis=0, keepdims=True)

        @pl.when(i == 0)
        def _():
            dg_ref[...] = part

        @pl.when(i > 0)
        def _():
            dg_ref[...] += part

    row = pl.BlockSpec((tm, D), lambda i: (i, 0))
    vec = pl.BlockSpec((1, D), lambda i: (0, 0))
    return pl.pallas_call(
        kern, grid=(M // tm,),
        in_specs=[pl.BlockSpec((tm, K), lambda i: (i, 0)),
                  pl.BlockSpec(b_bs, lambda i: b_off, pipeline_mode=pl.Buffered(1)), row, vec, row]
        + [pl.BlockSpec((f.shape[0], tm // f.shape[0], D), lambda i: (0, i, 0)) for f in folded]
        + [pl.BlockSpec(memory_space=pl.ANY)] * nd,
        out_specs=[row, vec],
        out_shape=[jax.ShapeDtypeStruct((M, D), F32), jax.ShapeDtypeStruct((1, D), F32)],
        scratch_shapes=[pltpu.VMEM((D // 128, tm, 128), F32)] if nf else [],
        compiler_params=_cparams(1), name=name)(a, b, x, g, dres, *folded, *deps)


def _norm_tail(xv, gv, rest, head):
    r = lax.rsqrt(jnp.mean(xv * xv, axis=-1, keepdims=True) + RMS_EPS)
    xhat = xv * r
    if not head:
        xo_ref, h_ref = rest
        xo_ref[...] = xv
        h_ref[...] = (xhat * gv).astype(BF16)
        return
    t_ref, dx_ref, dg_ref, ls_ref = rest
    i = pl.program_id(0)
    e = xhat * gv - t_ref[...]
    dy = e * (1.0 / D)
    gy = dy * gv
    dx_ref[...] = r * (gy - xhat * jnp.mean(gy * xhat, axis=-1, keepdims=True))
    dgp = jnp.sum(dy * xhat, axis=0, keepdims=True)
    lsp = jnp.sum(e * e, axis=0, keepdims=True)

    @pl.when(i == 0)
    def _():
        dg_ref[...] = dgp
        ls_ref[...] = lsp

    @pl.when(i > 0)
    def _():
        dg_ref[...] += dgp
        ls_ref[...] += lsp


def _ffn_fwd(h, wgu, wd, res, g, *, name, tgt=None):
    S = h.shape[0]
    tm = 256
    nj = DFF // HCH
    head = tgt is not None

    def kern(h_ref, wgu_ref, wd_ref, res_ref, g_ref, *rest):
        t_refs, (gu_ref, act_ref), tail = rest[:int(head)], rest[int(head):int(head) + 2], rest[int(head) + 2:]
        hv = h_ref[...]
        for j in range(nj):
            gu = lax.dot_general(hv, wgu_ref[2 * HCH * j:2 * HCH * (j + 1), :], (((1,), (1,)), ((), ())),
                                 preferred_element_type=F32)
            gu_ref[:, 2 * HCH * j:2 * HCH * (j + 1)] = gu.astype(BF16)
            gate = gu[:, :HCH]
            act_ref[:, HCH * j:HCH * (j + 1)] = (gate * jax.nn.sigmoid(gate) * gu[:, HCH:]).astype(BF16)
        xv = res_ref[...] + jnp.dot(act_ref[...], wd_ref[...], preferred_element_type=F32)
        _norm_tail(xv, g_ref[...], tuple(t_refs) + tuple(tail), head)

    row = pl.BlockSpec((tm, D), lambda i: (i, 0))
    vec = pl.BlockSpec((1, D), lambda i: (0, 0))
    in_specs = [row, pl.BlockSpec((2 * DFF, D), lambda i: (0, 0), pipeline_mode=pl.Buffered(1)),
                pl.BlockSpec((DFF, D), lambda i: (0, 0), pipeline_mode=pl.Buffered(1)), row, vec]
    out_specs = [pl.BlockSpec((tm, 2 * DFF), lambda i: (i, 0)), pl.BlockSpec((tm, DFF), lambda i: (i, 0))]
    out_shape = [jax.ShapeDtypeStruct((S, 2 * DFF), BF16), jax.ShapeDtypeStruct((S, DFF), BF16)]
    args = [h, wgu, wd, res, g]
    if head:
        in_specs, args = in_specs + [row], args + [tgt]
        out_specs += [row, vec, vec]
        out_shape += [jax.ShapeDtypeStruct((S, D), F32), jax.ShapeDtypeStruct((1, D), F32),
                      jax.ShapeDtypeStruct((1, D), F32)]
    else:
        out_specs += [row, row]
        out_shape += [jax.ShapeDtypeStruct((S, D), F32), jax.ShapeDtypeStruct((S, D), BF16)]
    outs = pl.pallas_call(kern, grid=(S // tm,), in_specs=in_specs, out_specs=out_specs, out_shape=out_shape,
                          compiler_params=_cparams(1), name=name)(*args)
    return outs[0], outs[1], tuple(outs[2:])


def _mm_res_norm(a, b, res, g, *, K, tm, name, b_off=(0, 0), tgt=None):
    M = a.shape[0]
    head = tgt is not None

    def kern(a_ref, b_ref, res_ref, g_ref, *rest):
        xv = res_ref[...] + jnp.dot(a_ref[...], b_ref[...], preferred_element_type=F32)
        _norm_tail(xv, g_ref[...], rest, head)

    row = pl.BlockSpec((tm, D), lambda i: (i, 0))
    vec = pl.BlockSpec((1, D), lambda i: (0, 0))
    in_specs = [pl.BlockSpec((tm, K), lambda i: (i, 0)),
                pl.BlockSpec((K, D), lambda i: b_off, pipeline_mode=pl.Buffered(1)), row, vec]
    if head:
        return pl.pallas_call(
            kern, grid=(M // tm,), in_specs=in_specs + [row], out_specs=[row, vec, vec],
            out_shape=[jax.ShapeDtypeStruct((M, D), F32), jax.ShapeDtypeStruct((1, D), F32),
                       jax.ShapeDtypeStruct((1, D), F32)],
            compiler_params=_cparams(1), name=name)(a, b, res, g, tgt)
    return pl.pallas_call(
        kern, grid=(M // tm,), in_specs=in_specs, out_specs=[row, row],
        out_shape=[jax.ShapeDtypeStruct((M, D), F32), jax.ShapeDtypeStruct((M, D), BF16)],
        compiler_params=_cparams(1), name=name)(a, b, res, g)


def _rms_fwd(x, g, name, deps=()):
    S = x.shape[0]
    tr = 512

    def kern(x_ref, g_ref, *rest):
        h_ref = rest[-1]
        xv = x_ref[...]
        r = lax.rsqrt(jnp.mean(xv * xv, axis=-1, keepdims=True) + RMS_EPS)
        h_ref[...] = (xv * r * g_ref[...]).astype(BF16)

    return pl.pallas_call(
        kern, grid=(S // tr,),
        in_specs=[pl.BlockSpec((tr, D), lambda i: (i, 0)), pl.BlockSpec((1, D), lambda i: (0, 0))]
        + [pl.BlockSpec(memory_space=pl.ANY)] * len(deps),
        out_specs=pl.BlockSpec((tr, D), lambda i: (i, 0)),
        out_shape=jax.ShapeDtypeStruct((S, D), BF16), compiler_params=_cparams(1), name=name)(x, g, *deps)


def _chunks_put(scr, val):
    for c in range(scr.shape[0]):
        scr[c] = val[:, c * 128:(c + 1) * 128]


def _chunks_get(scr):
    return jnp.concatenate([scr[c] for c in range(scr.shape[0])], axis=1)


def _chunks_rows(scr, r, n, dil):
    return jnp.concatenate([scr.at[c][pl.ds(r, n, stride=dil), :] for c in range(scr.shape[0])], axis=1)


def _chunks_add_rows(scr, val, r, n, dil, accumulate):
    for c in range(scr.shape[0]):
        rows = pl.ds(r, n, stride=dil)
        piece = val[:, c * 128:(c + 1) * 128]
        tile = scr.at[c]
        tile[rows, :] = tile[rows, :] + piece if accumulate else piece


def _rms_fwd_folded(x, g, name, deps=()):
    S = x.shape[0]
    tr = 512
    dils = DILS[1:]

    def kern(x_ref, g_ref, *rest):
        outs, scr = rest[len(deps):-1], rest[-1]
        xv = x_ref[...]
        r = lax.rsqrt(jnp.mean(xv * xv, axis=-1, keepdims=True) + RMS_EPS)
        h = (xv * r * g_ref[...]).astype(BF16)
        outs[0][...] = h
        _chunks_put(scr, h.astype(F32))
        for o_ref, dil in zip(outs[1:], dils):
            for res in range(dil):
                o_ref[res] = _chunks_rows(scr, res, tr // dil, dil).astype(BF16)

    return pl.pallas_call(
        kern, grid=(S // tr,),
        in_specs=[pl.BlockSpec((tr, D), lambda i: (i, 0)), pl.BlockSpec((1, D), lambda i: (0, 0))]
        + [pl.BlockSpec(memory_space=pl.ANY)] * len(deps),
        out_specs=[pl.BlockSpec((tr, D), lambda i: (i, 0))]
        + [pl.BlockSpec((dil, tr // dil, D), lambda i: (0, i, 0)) for dil in dils],
        out_shape=[jax.ShapeDtypeStruct((S, D), BF16)]
        + [jax.ShapeDtypeStruct((dil, S // dil, D), BF16) for dil in dils],
        scratch_shapes=[pltpu.VMEM((D // 128, tr, 128), F32)],
        compiler_params=_cparams(1), name=name)(x, g, *deps)


def _ffn_bwd(dxo, wd, wgu, gu, xin, gain, name):
    S = dxo.shape[0]
    tm = 256
    nj = DFF // HCH

    def kern(dx_ref, wd_ref, wgu_ref, gu_ref, x_ref, g_ref, dgu_ref, dxin_ref, dg_ref):
        i = pl.program_id(0)
        dxv = dx_ref[...]
        dxb = dxv.astype(BF16)
        for j in range(nj):
            c0 = 2 * HCH * j
            dact = lax.dot_general(dxb, wd_ref[HCH * j:HCH * (j + 1), :], (((1,), (1,)), ((), ())),
                                   preferred_element_type=F32)
            gate = gu_ref[:, c0:c0 + HCH].astype(F32)
            up = gu_ref[:, c0 + HCH:c0 + 2 * HCH].astype(F32)
            sig = jax.nn.sigmoid(gate)
            silu = gate * sig
            dgu_ref[:, c0:c0 + HCH] = (dact * up * (sig * (1.0 + gate * (1.0 - sig)))).astype(BF16)
            dgu_ref[:, c0 + HCH:c0 + 2 * HCH] = (dact * silu).astype(BF16)
        dhv = jnp.dot(dgu_ref[...], wgu_ref[...], preferred_element_type=F32)
        xv = x_ref[...]
        r = lax.rsqrt(jnp.mean(xv * xv, axis=-1, keepdims=True) + RMS_EPS)
        xhat = xv * r
        gy = dhv * g_ref[...]
        dxin_ref[...] = dxv + r * (gy - xhat * jnp.mean(gy * xhat, axis=-1, keepdims=True))
        part = jnp.sum(dhv * xhat, axis=0, keepdims=True)

        @pl.when(i == 0)
        def _():
            dg_ref[...] = part

        @pl.when(i > 0)
        def _():
            dg_ref[...] += part

    row = pl.BlockSpec((tm, D), lambda i: (i, 0))
    wide = pl.BlockSpec((tm, 2 * DFF), lambda i: (i, 0))
    vec = pl.BlockSpec((1, D), lambda i: (0, 0))
    return pl.pallas_call(
        kern, grid=(S // tm,),
        in_specs=[row, pl.BlockSpec((DFF, D), lambda i: (0, 0), pipeline_mode=pl.Buffered(1)),
                  pl.BlockSpec((2 * DFF, D), lambda i: (0, 0), pipeline_mode=pl.Buffered(1)), wide, row, vec],
        out_specs=[wide, row, vec],
        out_shape=[jax.ShapeDtypeStruct((S, 2 * DFF), BF16), jax.ShapeDtypeStruct((S, D), F32),
                   jax.ShapeDtypeStruct((1, D), F32)],
        compiler_params=_cparams(1), name=name)(dxo, wd, wgu, gu, xin, gain)


def _trail(u, *, backward, name):
    S = u.shape[0]

    def kern(u_ref, o_ref):
        g = pl.program_id(0)
        for grp in range(POOL_G):
            @pl.when(g == grp)
            def _(grp=grp):
                uv = u_ref[...].astype(F32)
                row = lax.broadcasted_iota(jnp.int32, uv.shape, 0)
                cnt = jnp.minimum(row + 1, 2 << grp).astype(F32)
                s = uv / cnt if backward else uv
                for k in (1, 2, 4, 8)[:grp + 1]:
                    if backward:
                        sh = jnp.where(row < S - k, pltpu.roll(s, S - k, 0), 0.0)
                    else:
                        sh = jnp.where(row >= k, pltpu.roll(s, k, 0), 0.0)
                    s = s + sh
                if backward:
                    o_ref[...] = (s - uv).astype(BF16)
                else:
                    o_ref[...] = (s / cnt - uv).astype(BF16)

    blk = pl.BlockSpec((S, PGD), lambda g: (0, g))
    return pl.pallas_call(
        kern, grid=(POOL_G,), in_specs=[blk], out_specs=blk,
        out_shape=jax.ShapeDtypeStruct((S, D), BF16), compiler_params=_cparams(1), name=name)(u)


def _pool_out(yd, G, scale, xres):
    S = yd.shape[0]
    tm = min(S, 4096)

    def kern(y_ref, w_ref, s_ref, x_ref, o_ref):
        z = jnp.dot(y_ref[...], w_ref[...], preferred_element_type=F32)
        o_ref[...] = x_ref[...] + z * s_ref[...]

    tile = pl.BlockSpec((tm, PGD), lambda i, g: (i, g))
    return pl.pallas_call(
        kern, grid=(S // tm, POOL_G),
        in_specs=[tile, pl.BlockSpec((PGD, PGD), lambda i, g: (0, g)),
                  pl.BlockSpec((1, PGD), lambda i, g: (0, g)), tile],
        out_specs=tile, out_shape=jax.ShapeDtypeStruct((S, D), F32),
        compiler_params=_cparams(2), name="pool_out")(yd, G, scale, xres)


def _pool_out_bwd(dz, yd, G, scale, deps=()):
    S = yd.shape[0]
    tm = min(S, 4096)
    ni = S // tm
    nd = len(deps)

    def kern(dz_ref, y_ref, w_ref, s_ref, *rest):
        dy_ref, ds_ref, dw_ref, acc_ref = rest[nd:]
        i = pl.program_id(1)
        dzv = dz_ref[...]
        yv = y_ref[...]
        wv = w_ref[...]
        zraw = jnp.dot(yv, wv, preferred_element_type=F32)
        dsp = jnp.sum(dzv * zraw, axis=0, keepdims=True)
        dzr = (dzv * s_ref[...]).astype(BF16)
        dy_ref[...] = lax.dot_general(dzr, wv, (((1,), (1,)), ((), ())), preferred_element_type=F32)
        dwp = lax.dot_general(yv, dzr, (((0,), (0,)), ((), ())), preferred_element_type=F32)

        @pl.when(i == 0)
        def _():
            ds_ref[...] = dsp
            acc_ref[...] = dwp

        @pl.when(i > 0)
        def _():
            ds_ref[...] += dsp
            acc_ref[...] += dwp

        @pl.when(i == ni - 1)
        def _():
            dw_ref[...] = acc_ref[...].astype(BF16)

    tile = pl.BlockSpec((tm, PGD), lambda g, i: (i, g))
    return pl.pallas_call(
        kern, grid=(POOL_G, ni),
        in_specs=[tile, tile, pl.BlockSpec((PGD, PGD), lambda g, i: (0, g)),
                  pl.BlockSpec((1, PGD), lambda g, i: (0, g))] + [pl.BlockSpec(memory_space=pl.ANY)] * nd,
        out_specs=[tile, pl.BlockSpec((1, PGD), lambda g, i: (0, g)),
                   pl.BlockSpec((PGD, PGD), lambda g, i: (0, g))],
        out_shape=[jax.ShapeDtypeStruct((S, D), F32), jax.ShapeDtypeStruct((1, D), F32),
                   jax.ShapeDtypeStruct((PGD, D), BF16)],
        scratch_shapes=[pltpu.VMEM((PGD, PGD), F32)],
        compiler_params=_cparams(2), name="pool_out_bwd")(dz, yd, G, scale, *deps)


def _bias_table():
    qi = jnp.arange(QB)[:, None]
    ki = jnp.arange(2 * QB)[None, :]
    delta = QB + qi - ki
    inband = (delta >= 0) & (delta <= QB)
    n = NGROUPS * HEADS
    slopes = jnp.exp2(-8.0 * jnp.arange(1, n + 1, dtype=F32) / n).reshape(NGROUPS, HEADS)
    dil = jnp.asarray(DILS, F32)
    bias = -slopes[:, :, None, None] * (delta.astype(F32)[None, None] * dil[:, None, None, None])
    return jnp.where(inband[None, None], bias, NEG)


def _attn_fwd(qkv_f, bias, nb, name):
    S = qkv_f.shape[0]
    nblk = S // QB
    scale = HD ** -0.5

    def kern(q_ref, k2_ref, kp_ref, v2_ref, vp_ref, b_ref, o_ref, l_ref, s_scr, p_scr, r_scr):
        s_id = pl.program_id(0)
        col = lax.broadcasted_iota(jnp.int32, (QB, 2 * QB), 1)
        lane = lax.broadcasted_iota(jnp.int32, (QB, HD), 1)

        def keys(sub, cur2_ref, prev_ref, sl):
            if sub:
                return cur2_ref[:, sl]
            return jnp.concatenate([prev_ref[:, sl], cur2_ref[0:QB, sl]], axis=0)

        for sub in range(2):
            for h in range(HEADS):
                sl = slice(h * HD, (h + 1) * HD)
                s_scr[sub * HEADS + h] = lax.dot_general(
                    q_ref[sub * QB:(sub + 1) * QB, sl], keys(sub, k2_ref, kp_ref, sl), (((1,), (1,)), ((), ())),
                    preferred_element_type=F32)
        for sub in range(2):
            has_prev = jnp.bitwise_and(2 * s_id + sub, nb - 1) != 0
            dead = jnp.logical_and(col < QB, jnp.logical_not(has_prev))
            lse_all = jnp.zeros((QB, HD), F32)
            for h in range(HEADS):
                u = sub * HEADS + h
                s = s_scr[u] * scale + b_ref[h]
                s = jnp.where(dead, NEG, s)
                m = jnp.max(s, axis=-1, keepdims=True)
                p = jnp.exp(s - m)
                den = jnp.sum(p, axis=-1, keepdims=True)
                p_scr[u] = p.astype(BF16)
                r_scr[u] = jnp.broadcast_to(1.0 / den, (QB, HD))
                lse_all = jnp.where(lane == h, m + jnp.log(den), lse_all)
            l_ref[sub * QB:(sub + 1) * QB, :] = lse_all
        for sub in range(2):
            for h in range(HEADS):
                u = sub * HEADS + h
                sl = slice(h * HD, (h + 1) * HD)
                o = jnp.dot(p_scr[u], keys(sub, v2_ref, vp_ref, sl), preferred_element_type=F32) * r_scr[u]
                o_ref[sub * QB:(sub + 1) * QB, sl] = o.astype(BF16)

    def pair(colblk):
        return pl.BlockSpec((2 * QB, D), lambda s: (s, colblk))

    def prev(colblk):
        return pl.BlockSpec((QB, D), lambda s: (jnp.maximum(2 * s - 1, 0), colblk))

    return pl.pallas_call(
        kern, grid=(nblk // 2,),
        in_specs=[pair(0), pair(1), prev(1), pair(2), prev(2), pl.BlockSpec((HEADS, QB, 2 * QB), lambda s: (0, 0, 0))],
        out_specs=[pl.BlockSpec((2 * QB, D), lambda s: (s, 0)), pl.BlockSpec((2 * QB, HD), lambda s: (s, 0))],
        out_shape=[jax.ShapeDtypeStruct((S, D), BF16), jax.ShapeDtypeStruct((S, HD), F32)],
        scratch_shapes=[pltpu.VMEM((2 * HEADS, QB, 2 * QB), F32), pltpu.VMEM((2 * HEADS, QB, 2 * QB), BF16),
                        pltpu.VMEM((2 * HEADS, QB, HD), F32)],
        compiler_params=_cparams(1), name=name)(qkv_f, qkv_f, qkv_f, qkv_f, qkv_f, bias)


def _natural(ref, scr, tm):
    dil = ref.shape[0]
    for res in range(dil):
        _chunks_add_rows(scr, ref[res].astype(F32), res, tm // dil, dil, False)
    return _chunks_get(scr)


def _attn_merge(os, lses):
    S = os[0].shape[0]
    tm = 512

    def kern(o0, o1, o2, l0, l1, l2, om_ref, lm_ref, ls1, ls2, os1, os2):
        la = l0[...]
        lb = _natural(l1, ls1, tm)
        lc = _natural(l2, ls2, tm)
        m = jnp.maximum(jnp.maximum(la, lb), lc)
        e0, e1, e2 = jnp.exp(la - m), jnp.exp(lb - m), jnp.exp(lc - m)
        tot = e0 + e1 + e2
        lm_ref[...] = m + jnp.log(tot)
        w0, w1, w2 = e0 / tot, e1 / tot, e2 / tot
        for res in range(o1.shape[0]):
            _chunks_add_rows(os1, o1[res].astype(F32), res, tm // o1.shape[0], o1.shape[0], False)
        for res in range(o2.shape[0]):
            _chunks_add_rows(os2, o2[res].astype(F32), res, tm // o2.shape[0], o2.shape[0], False)
        for h in range(HEADS):
            sl = slice(h * HD, (h + 1) * HD)
            acc = w0[:, h:h + 1] * o0[:, sl].astype(F32) + w1[:, h:h + 1] * os1[h] + w2[:, h:h + 1] * os2[h]
            om_ref[:, sl] = acc.astype(BF16)

    def spec(a, c):
        if a.ndim == 2:
            return pl.BlockSpec((tm, c), lambda i: (i, 0))
        return pl.BlockSpec((a.shape[0], tm // a.shape[0], c), lambda i: (0, i, 0))

    return pl.pallas_call(
        kern, grid=(S // tm,),
        in_specs=[spec(a, D) for a in os] + [spec(a, HD) for a in lses],
        out_specs=[pl.BlockSpec((tm, D), lambda i: (i, 0)), pl.BlockSpec((tm, HD), lambda i: (i, 0))],
        out_shape=[jax.ShapeDtypeStruct((S, D), BF16), jax.ShapeDtypeStruct((S, HD), F32)],
        scratch_shapes=[pltpu.VMEM((1, tm, HD), F32), pltpu.VMEM((1, tm, HD), F32),
                        pltpu.VMEM((HEADS, tm, HD), F32), pltpu.VMEM((HEADS, tm, HD), F32)],
        compiler_params=_cparams(1), name="attn_merge")(*os, *lses)


def _attn_bwd_prep(do, o, lse):
    S = o.shape[0]
    tm = 512
    dils = DILS[1:]

    def kern(do_ref, o_ref, l_ref, *rest):
        do_outs, l_outs, d_outs = rest[0:3], rest[3:5], rest[5:8]
        do_scr, l_scr, d_scr = rest[8:11]
        lane = lax.broadcasted_iota(jnp.int32, (tm, HD), 1)
        acc = jnp.zeros((tm, HD), F32)
        for h in range(HEADS):
            sl = slice(h * HD, (h + 1) * HD)
            prod = do_ref[:, sl] * o_ref[:, sl].astype(F32)
            acc = jnp.where(lane == h, jnp.sum(prod, axis=-1, keepdims=True), acc)
        d_scr[0] = acc
        l_scr[0] = l_ref[...]
        _chunks_put(do_scr, do_ref[...])
        do_outs[0][...] = do_ref[...].astype(BF16)
        d_outs[0][...] = acc
        for j, dil in enumerate(dils):
            for res in range(dil):
                n = tm // dil
                do_outs[1 + j][res] = _chunks_rows(do_scr, res, n, dil).astype(BF16)
                l_outs[j][res] = _chunks_rows(l_scr, res, n, dil)
                d_outs[1 + j][res] = _chunks_rows(d_scr, res, n, dil)

    def nat(c):
        return pl.BlockSpec((tm, c), lambda i: (i, 0))

    def fol(dil, c):
        return pl.BlockSpec((dil, tm // dil, c), lambda i: (0, i, 0))

    def shapes(c, dt, with_natural):
        first = [jax.ShapeDtypeStruct((S, c), dt)] if with_natural else []
        return first + [jax.ShapeDtypeStruct((dil, S // dil, c), dt) for dil in dils]

    outs = pl.pallas_call(
        kern, grid=(S // tm,), in_specs=[nat(D), nat(D), nat(HD)],
        out_specs=[nat(D)] + [fol(dil, D) for dil in dils] + [fol(dil, HD) for dil in dils]
        + [nat(HD)] + [fol(dil, HD) for dil in dils],
        out_shape=shapes(D, BF16, True) + shapes(HD, F32, False) + shapes(HD, F32, True),
        scratch_shapes=[pltpu.VMEM((HEADS, tm, HD), F32), pltpu.VMEM((1, tm, HD), F32), pltpu.VMEM((1, tm, HD), F32)],
        compiler_params=_cparams(1), name="attn_bwd_prep")(do, o, lse)
    return outs[0:3], [lse] + list(outs[3:5]), outs[5:8]


def _attn_bwd(qkv_f, do_f, lse_f, delta_f, bias, nb, name):
    S = qkv_f.shape[0]
    nblk = S // QB
    scale = HD ** -0.5

    npair = nblk // 2

    def kern(q_ref, k2_ref, kp_ref, v2_ref, vp_ref, do_ref, l_ref, d_ref, b_ref, out_ref, dq_c, dk_c, dv_c,
             s_scr, dp_scr, ds_scr, p_scr):
        s_id = pl.program_id(0)

        @pl.when(s_id == 0)
        def _():
            dq_c[...] = jnp.zeros_like(dq_c)
            dk_c[...] = jnp.zeros_like(dk_c)
            dv_c[...] = jnp.zeros_like(dv_c)

        @pl.when(s_id == npair)
        def _():
            out_ref[:, 0:D] = dq_c[...].astype(BF16)
            out_ref[:, D:2 * D] = dk_c[...].astype(BF16)
            out_ref[:, 2 * D:3 * D] = dv_c[...].astype(BF16)

        def keys(sub, cur2_ref, prev_ref, sl):
            if sub:
                return cur2_ref[:, sl]
            return jnp.concatenate([prev_ref[:, sl], cur2_ref[0:QB, sl]], axis=0)

        @pl.when(s_id < npair)
        def _():
            col = lax.broadcasted_iota(jnp.int32, (QB, 2 * QB), 1)
            out_ref[:, 0:D] = dq_c[...].astype(BF16)
            for sub in range(2):
                rows = slice(sub * QB, (sub + 1) * QB)
                for h in range(HEADS):
                    sl = slice(h * HD, (h + 1) * HD)
                    u = sub * HEADS + h
                    s_scr[u] = lax.dot_general(q_ref[rows, sl], keys(sub, k2_ref, kp_ref, sl),
                                               (((1,), (1,)), ((), ())), preferred_element_type=F32)
                    dp_scr[u] = lax.dot_general(do_ref[rows, sl], keys(sub, v2_ref, vp_ref, sl),
                                                (((1,), (1,)), ((), ())), preferred_element_type=F32)
            for sub in range(2):
                rows = slice(sub * QB, (sub + 1) * QB)
                has_prev = jnp.bitwise_and(2 * s_id + sub, nb - 1) != 0
                dead = jnp.logical_and(col < QB, jnp.logical_not(has_prev))
                lv = l_ref[rows, :]
                dv_ = d_ref[rows, :]
                for h in range(HEADS):
                    u = sub * HEADS + h
                    s = s_scr[u] * scale + b_ref[h]
                    s = jnp.where(dead, NEG, s)
                    p = jnp.exp(s - lv[:, h:h + 1])
                    ds_scr[u] = (p * (dp_scr[u] - dv_[:, h:h + 1]) * scale).astype(BF16)
                    p_scr[u] = p.astype(BF16)
            for h in range(HEADS):
                sl = slice(h * HD, (h + 1) * HD)
                parts = []
                for sub in range(2):
                    rows = slice(sub * QB, (sub + 1) * QB)
                    u = sub * HEADS + h
                    ds = ds_scr[u]
                    dq_c[rows, sl] = jnp.dot(ds, keys(sub, k2_ref, kp_ref, sl), preferred_element_type=F32)
                    dkk = lax.dot_general(ds, q_ref[rows, sl], (((0,), (0,)), ((), ())), preferred_element_type=F32)
                    dvv = lax.dot_general(p_scr[u], do_ref[rows, sl], (((0,), (0,)), ((), ())),
                                          preferred_element_type=F32)
                    parts.append((dkk, dvv))
                for which, carry, base in ((0, dk_c, D), (1, dv_c, 2 * D)):
                    first, second = parts[0][which], parts[1][which]
                    cols = slice(base + h * HD, base + (h + 1) * HD)
                    out_ref[0:QB, cols] = carry[0:QB, sl].astype(BF16)
                    out_ref[QB:2 * QB, cols] = (carry[QB:2 * QB, sl] + first[:QB]).astype(BF16)
                    carry[0:QB, sl] = first[QB:] + second[:QB]
                    carry[QB:2 * QB, sl] = second[QB:]

    last = npair - 1

    def pair(colblk, c):
        return pl.BlockSpec((2 * QB, c), lambda s: (jnp.minimum(s, last), colblk))

    def prev(colblk):
        return pl.BlockSpec((QB, D), lambda s: (jnp.maximum(2 * jnp.minimum(s, last) - 1, 0), colblk))

    return pl.pallas_call(
        kern, grid=(npair + 1,),
        in_specs=[pair(0, D), pair(1, D), prev(1), pair(2, D), prev(2), pair(0, D), pair(0, HD), pair(0, HD),
                  pl.BlockSpec((HEADS, QB, 2 * QB), lambda s: (0, 0, 0))],
        out_specs=pl.BlockSpec((2 * QB, 3 * D), lambda s: (jnp.maximum(s - 1, 0), 0)),
        out_shape=jax.ShapeDtypeStruct((S, 3 * D), BF16),
        scratch_shapes=[pltpu.VMEM((2 * QB, D), F32), pltpu.VMEM((2 * QB, D), F32), pltpu.VMEM((2 * QB, D), F32),
                        pltpu.VMEM((2 * HEADS, QB, 2 * QB), F32), pltpu.VMEM((2 * HEADS, QB, 2 * QB), F32),
                        pltpu.VMEM((2 * HEADS, QB, 2 * QB), BF16), pltpu.VMEM((2 * HEADS, QB, 2 * QB), BF16)],
        compiler_params=_cparams(1), name=name)(qkv_f, qkv_f, qkv_f, qkv_f, qkv_f, do_f, lse_f, delta_f, bias)


def _local_step(x, tgt, comm, attn_norm, ffn_norm, final_norm):
    S = x.shape[0]
    bias = _bias_table()
    g_attn = attn_norm.reshape(1, D)
    g_f0 = ffn_norm[0:1]
    g_f1 = ffn_norm[1:2]
    g_fin = final_norm.reshape(1, D)
    W = {}

    def ffn_fwd(xin, h, l, next_gain, target=None):
        return _ffn_fwd(h, W[f"gu{l}"], W[f"d{l}"], xin, next_gain, tgt=target, name=f"ffn_fwd{l}")

    def ffn_bwd(dxo, xin, gain, h, gu, act, l, rs_group):
        dgu, dxin, dgain = _ffn_bwd(dxo, W[f"d{l}"], W[f"gu{l}"], gu, xin, gain, f"ffn_bwd{l}")
        gw_d = _mm(act, dxo, mode="tn", M=DFF, N=D, K=S, tm=HCH, tn=D, tk=2048, out_dtype=BF16, name=f"gw_d{l}")
        gw_gu = _mm(dgu, h, mode="tn", M=2 * DFF, N=D, K=S, tm=HCH, tn=D, tk=2048, out_dtype=BF16, name=f"gw_gu{l}")
        return dxin, dgain, comm.send_grads(rs_group, {f"d{l}": gw_d, f"gu{l}": gw_gu})

    nbs = [S // QB // dil for dil in DILS]
    hf = _rms_fwd_folded(x, g_attn, "rms_attn", deps=comm.ag_tokens)
    hf = [h.reshape(S, D) for h in hf]
    W.update(comm.weights(0, hf[0]))
    qkv_f, o_f, lse_f = [], [], []
    for g, dil in enumerate(DILS):
        qkv_f.append(_mm(hf[g], W["qkv"], mode="nt", M=S, N=3 * D, K=D, tm=2048, tn=1024, tk=D, out_dtype=BF16,
                         b_off=(3 * g, 0), name=f"qkv_proj{g}"))
        og, lg = _attn_fwd(qkv_f[g], bias[g], nbs[g], f"attn_fwd{g}")
        o_f.append(og if dil == 1 else og.reshape(dil, S // dil, D))
        lse_f.append(lg if dil == 1 else lg.reshape(dil, S // dil, HD))
    passing = [comm.pass_on(1, tuple(o_f)), comm.pass_on(2, tuple(o_f))]
    (o_f, lse_f), passing = lax.optimization_barrier(((o_f, lse_f), passing))
    o, lse = _attn_merge(o_f, lse_f)
    W.update(comm.weights(1, (o, passing[0])))
    x1, h1 = _mm_res_norm(o, W["wo"], x, g_f0, K=D, tm=1024, name="attn_out")
    pv = W["pv"].reshape(NDEV, 8, 128)
    pool_norm, pool_scale = pv[:, 0, :].reshape(1, D), pv[:, 1, :].reshape(1, D)
    gu0, act0, (x2, h2) = ffn_fwd(x1, h1, 0, pool_norm)

    W.update(comm.weights(2, (x2, passing[1])))
    u = _mm(h2, W["wpi"], mode="nn", M=S, N=D, K=D, tm=1024, tn=D, tk=D, out_dtype=F32, name="pool_in")
    yd = _trail(u, backward=False, name="trail_fwd")
    x3 = _pool_out(yd, W["pg"], pool_scale, x2)
    h3 = _rms_fwd(x3, g_f1, "rms_ffn1")
    gu1, act1, (dx4, d_fin, lossvec) = ffn_fwd(x3, h3, 1, g_fin, target=tgt)

    dx3, d_f1, token = ffn_bwd(dx4, x3, g_f1, h3, gu1, act1, 1, 0)
    dyd, d_scale, gw_pg = _pool_out_bwd(dx3, yd, W["pg"], pool_scale, deps=(token,))
    du = _trail(dyd, backward=True, name="trail_bwd")
    gw_pi = _mm(h2, du, mode="tn", M=D, N=D, K=S, tm=D, tn=D, tk=S, out_dtype=BF16, name="gw_pi")
    token = comm.send_grads(1, {"pg": gw_pg, "wpi": gw_pi})
    dx2, d_pool = _mm_rms_bwd(du, W["wpi"], x2, pool_norm, dx3, mode="nt", M=S, K=D, tm=1024, deps=(token,),
                              name="pool_in_bwd")
    dx1, d_f0, token = ffn_bwd(dx2, x1, g_f0, h1, gu0, act0, 0, 2)

    gw_o = _mm(o, dx1, mode="tn", M=D, N=D, K=S, tm=D, tn=D, tk=2048, out_dtype=BF16, deps=(token,), name="gw_o")
    do = _mm(dx1, W["wo"], mode="nt", M=S, N=D, K=D, tm=1024, tn=D, tk=D, out_dtype=F32, deps=(token,),
             name="attn_out_bwd")
    do_f, lse_ff, delta_f = _attn_bwd_prep(do, o, lse)
    dqkv_f, gw_qkv = [], None
    for g in range(NGROUPS):
        dqkv_f.append(_attn_bwd(qkv_f[g], do_f[g].reshape(S, D), lse_ff[g].reshape(S, HD),
                                delta_f[g].reshape(S, HD), bias[g], nbs[g], f"attn_bwd{g}"))
        gw_qkv = _mm(dqkv_f[g], hf[g], mode="tn", M=3 * D, N=D, K=S, tm=1024, tn=D, tk=S, out_dtype=BF16,
                     out_rows=NGROUPS * 3 * D, out_off=3 * g, out_prev=gw_qkv, name=f"gw_qkv{g}")
    token = comm.send_grads_pairwise({"wo": gw_o, "qkv": gw_qkv})
    folded = []
    for g in reversed(range(1, NGROUPS)):
        dh0_g = _mm(dqkv_f[g], W["qkv"], mode="nn", M=S, N=D, K=3 * D, tm=1024, tn=D, tk=3 * D, out_dtype=F32,
                    b_off=(g, 0), deps=(token,), name=f"qkv_proj_bwd{g}")
        folded.append(dh0_g.reshape(DILS[g], S // DILS[g], D))
        if g == NGROUPS - 1:
            token = comm.pass_grads(dh0_g)
    grad_x, d_attn = _mm_rms_bwd(dqkv_f[0], W["qkv"], x, g_attn, dx1, mode="nn", M=S, K=3 * D, tm=512,
                                 deps=(token,), folded=folded, name="qkv_proj_bwd0")

    vec = jnp.concatenate([d_attn, d_f0, d_f1, d_fin, d_pool, d_scale, lossvec, jnp.zeros((1, D), F32)], axis=0)
    return grad_x, vec


def _mesh_pos():
    x, y, c = lax.axis_index("x"), lax.axis_index("y"), lax.axis_index("c")
    return x, y, c, 4 * x + 2 * y + c


def _peer(x, y, c, k):
    kx, ky, kc = (k >> 2) & 1, (k >> 1) & 1, k & 1
    px = 1 - x if kx else x
    py = 1 - y if ky else y
    pc = 1 - c if kc else c
    return (px, py, pc), 4 * px + 2 * py + pc


ANY = pl.BlockSpec(memory_space=pl.ANY)


HBM = pl.BlockSpec(memory_space=pltpu.HBM)
SEMS = pl.BlockSpec(memory_space=pltpu.SEMAPHORE)
EFFECT = pltpu.SideEffectType.DATAFLOW_SIDE_EFFECTING

AG_GROUPS = (("qkv",), ("wo", "gu0", "d0", "pv"), ("wpi", "pg", "gu1", "d1"))
AG_ORDER = tuple(n for grp in AG_GROUPS for n in grp)
RS_GROUPS = (("d1", "gu1"), ("pg", "wpi"), ("d0", "gu0"), ("wo", "qkv"))


def _hbm(a):
    return pltpu.with_memory_space_constraint(a, pltpu.HBM)


def _remote(src, dst, send, recv, peer):
    return pltpu.make_async_remote_copy(src_ref=src, dst_ref=dst, send_sem=send, recv_sem=recv, device_id=peer,
                                        device_id_type=pl.DeviceIdType.MESH)


ALL_KS = tuple(range(1, NDEV))
AG_KS1 = (1, 2, 4, 6)
AG_KS2 = (2, 4, 6)
RS_KS_PAIR = (1, 3, 5, 7)
RS_KS_CHIPS = (2, 4, 6)


def _split_start(srcs, src_of, lands, copy_refs, name, deps=(), ks=ALL_KS, to=None):
    ns, n, nd, nk = len(srcs), len(lands), len(deps), len(ks)

    def body(*refs):
        ins, land = refs[:ns], refs[ns:ns + n]
        send, recv = refs[ns + n + nd], refs[ns + n + nd + 1]
        token = refs[-1]
        x, y, c, me = _mesh_pos()
        for j in range(n):
            for i, k in enumerate(ks):
                _, pid = _peer(x, y, c, k)
                dest, _ = _peer(x, y, c, k if to is None else to)
                src, dst = copy_refs(j, (land[j] if src_of[j] is None else ins[src_of[j]]), land[j], me, pid, i)
                _remote(src, dst, send.at[j * nk + i], recv.at[j * nk + i], dest).start()
        token[...] = jnp.zeros_like(token)

    outs = pl.pallas_call(
        body, name=name,
        out_shape=(pltpu.SemaphoreType.DMA((n * nk,)), pltpu.SemaphoreType.DMA((n * nk,)))
        + tuple(pltpu.HBM(a.shape, a.dtype) for a in srcs) + tuple(pltpu.HBM(a.shape, a.dtype) for a in lands)
        + (jax.ShapeDtypeStruct((8, 128), F32),),
        in_specs=(HBM,) * (ns + n) + (ANY,) * nd,
        out_specs=(SEMS, SEMS) + (HBM,) * (ns + n) + (pl.BlockSpec(memory_space=pltpu.VMEM),),
        input_output_aliases={i: 2 + i for i in range(ns + n)},
        compiler_params=pltpu.CompilerParams(has_side_effects=EFFECT),
    )(*[_hbm(a) for a in srcs], *[_hbm(a) for a in lands], *deps)
    return outs[0], outs[1], list(outs[2:2 + ns]), list(outs[2 + ns:2 + ns + n]), outs[-1]


def _split_wait(srcs, src_of, lands, send, recv, sem_rows, wait_refs, after, name, ks=ALL_KS):
    ns, n, nk = len(srcs), len(lands), len(ks)
    after = tuple(after) if isinstance(after, (tuple, list)) else (after,)

    def body(*refs):
        ins, land = refs[:ns], refs[ns:ns + n]
        send_ref, recv_ref = refs[ns + n], refs[ns + n + 1]
        x, y, c, me = _mesh_pos()
        for j in range(n):
            for i, k in enumerate(ks):
                peer, _ = _peer(x, y, c, k)
                src, dst = wait_refs(j, (land[j] if src_of[j] is None else ins[src_of[j]]), land[j])
                sem = sem_rows[j] * nk + i
                cp = _remote(src, dst, send_ref.at[sem], recv_ref.at[sem], peer)
                cp.wait_send()
                cp.wait_recv()

    outs = pl.pallas_call(
        body, name=name,
        out_shape=tuple(pltpu.HBM(a.shape, a.dtype) for a in srcs) + tuple(pltpu.HBM(a.shape, a.dtype) for a in lands),
        in_specs=(HBM,) * (ns + n) + (SEMS, SEMS) + (ANY,) * len(after),
        out_specs=(HBM,) * (ns + n),
        input_output_aliases={i: i for i in range(ns + n)},
        compiler_params=pltpu.CompilerParams(has_side_effects=EFFECT),
    )(*srcs, *lands, send, recv, *after)
    return list(outs[:ns]), list(outs[ns:])


def _ag_dtype(name):
    return F32 if name == "pv" else BF16


def _ag_align(name):
    return 8 if name == "pv" else 16


def _place_transposed(w, me, name):
    rows = w.shape[1]
    nblk = rows // 128

    def kern(me_ref, w_ref, o_ref):
        o_ref[...] = w_ref[...].T.astype(BF16)

    grid_spec = pltpu.PrefetchScalarGridSpec(
        num_scalar_prefetch=1, grid=(nblk,),
        in_specs=[pl.BlockSpec((D, 128), lambda i, me_ref: (0, i))],
        out_specs=pl.BlockSpec((128, D), lambda i, me_ref: (me_ref[0] * nblk + i, 0)))
    return pl.pallas_call(
        kern, grid_spec=grid_spec, out_shape=jax.ShapeDtypeStruct((NDEV * rows, D), BF16),
        compiler_params=_cparams(1), name=name)(me.reshape(1).astype(jnp.int32), w)


class _Comm:
    def __init__(self, params, make_shards, me, placed):
        self.me = me
        self.ag_land, self.ag_sems, self.ag_tokens, self.ag_passing = {}, {}, (), {}
        self.rs = []
        deps = ()
        for part, names in enumerate((AG_GROUPS[0], AG_ORDER[len(AG_GROUPS[0]):])):
            rows = [SEC_ROWS[n] for n in names]
            if part == 0:
                lands = [placed[n] for n in names]
            else:
                params, deps = lax.optimization_barrier((params, deps))
                shards = make_shards(*params)
                lands = [lax.dynamic_update_slice(lax.empty((NDEV * r, shards[n].shape[1]), _ag_dtype(n)),
                                                  shards[n].astype(_ag_dtype(n)), (_shard_pos(n, me), 0))
                         for n, r in zip(names, rows)]

            def copy_refs(j, src, land, me, pid, i, names=names, rows=rows):
                own = land.at[pl.ds(pl.multiple_of(_shard_pos(names[j], me), _ag_align(names[j])), rows[j])]
                return own, own

            send, recv, _, lands, token = _split_start([], [None] * len(names), lands, copy_refs, f"ag_start{part}",
                                                       deps=deps, ks=AG_KS1)
            deps = (token,)
            self.ag_tokens += (token,)
            for j, n in enumerate(names):
                self.ag_land[n] = lands[j]
                self.ag_sems[n] = (send, recv, j)

    def pass_on(self, group, after):
        names = AG_GROUPS[group]
        send, recv = self.ag_sems[names[0]][:2]
        idx = [self.ag_sems[n][2] for n in names]
        rows = [SEC_ROWS[n] for n in names]
        none = [None] * len(names)

        def wait_refs(j, src, land):
            return land.at[pl.ds(0, rows[j])], land.at[pl.ds(0, rows[j])]

        _, lands = _split_wait([], none, [self.ag_land[n] for n in names], send, recv, idx,
                               wait_refs, after, f"ag_wait{group}", ks=AG_KS1)

        def copy_refs(j, src, land, me, pid, i):
            theirs = land.at[pl.ds(pl.multiple_of(_shard_pos(names[j], pid), _ag_align(names[j])), rows[j])]
            return theirs, theirs

        send, recv, _, lands, token = _split_start([], none, lands, copy_refs, f"ag_pass{group}", ks=AG_KS2, to=1)
        self.ag_passing[group] = (send, recv, lands, wait_refs)
        return token

    def weights(self, group, after):
        names = AG_GROUPS[group]
        if group not in self.ag_passing:
            after = self.pass_on(group, after)
        send, recv, lands, wait_refs = self.ag_passing[group]
        _, lands = _split_wait([], [None] * len(names), lands, send, recv, list(range(len(names))), wait_refs, after,
                               f"ag_pass_wait{group}", ks=AG_KS2)
        return dict(zip(names, lands))

    def send_grads(self, group, gws):
        names = RS_GROUPS[group]
        rows = [SEC_ROWS[n] for n in names]
        grads = [gws[n] for n in names]
        me = self.me
        lands = [lax.dynamic_update_slice(
            lax.empty((NDEV, r, D), BF16),
            lax.dynamic_slice(g, (_shard_pos(n, me), 0), (r, D))[None], (me, 0, 0))
            for n, r, g in zip(names, rows, grads)]

        def copy_refs(j, src, land, me, pid, i):
            return src.at[pl.ds(pl.multiple_of(_shard_pos(names[j], pid), 16), rows[j])], land.at[me]

        send, recv, srcs, lands, token = _split_start(grads, list(range(len(names))), lands, copy_refs,
                                                      f"rs_start{group}")
        self.rs.append((names, rows, send, recv, srcs, lands, ALL_KS))
        return token

    def send_grads_pairwise(self, gws):
        names = RS_GROUPS[-1]
        rows = [SEC_ROWS[n] for n in names]
        grads = [gws[n] for n in names]
        idx = list(range(len(names)))
        lands = [lax.empty((len(RS_KS_PAIR), r, D), BF16) for r in rows]

        def copy_refs(j, src, land, me, pid, i):
            return src.at[pl.ds(pl.multiple_of(_shard_pos(names[j], pid), 16), rows[j])], land.at[i]

        send, recv, srcs, lands, token = _split_start(grads, idx, lands, copy_refs, "rs_pair_start",
                                                      ks=RS_KS_PAIR, to=1)
        self.pair = (names, rows, send, recv, srcs, lands)
        return token

    def pass_grads(self, after):
        names, rows, send, recv, srcs, lands = self.pair
        idx = list(range(len(names)))
        me = self.me

        def wait_refs(j, src, land):
            return src.at[pl.ds(0, rows[j])], land.at[0]

        srcs, lands = _split_wait(srcs, idx, lands, send, recv, idx, wait_refs, after, "rs_pair_wait", ks=RS_KS_PAIR)
        sums = []
        for n, r, g, got in zip(names, rows, srcs, lands):
            mine = jnp.stack([lax.dynamic_slice(g, (_shard_pos(n, jnp.bitwise_xor(me, k)), 0), (r, D))
                              for k in (0,) + RS_KS_CHIPS])
            sums.append(_pair_sum(mine, got, f"rs_pair_sum_{n}"))
        lands = [lax.dynamic_update_slice(lax.empty(p.shape, BF16), p[0:1], (0, 0, 0)) for p in sums]

        def copy_refs(j, src, land, me, pid, i):
            return src.at[i + 1], land.at[i + 1]

        send, recv, sums, lands, token = _split_start(sums, idx, lands, copy_refs, f"rs_start{len(RS_GROUPS) - 1}",
                                                      ks=RS_KS_CHIPS)
        self.rs.append((names, rows, send, recv, sums, lands, RS_KS_CHIPS))
        return token

    def received(self, group, after):
        names, rows, send, recv, srcs, lands, ks = self.rs[group]
        whole = srcs[0].ndim == 2

        def wait_refs(j, src, land):
            return (src.at[pl.ds(0, rows[j])] if whole else src.at[0]), land.at[0]

        _, lands = _split_wait(srcs, list(range(len(names))), lands, send, recv, list(range(len(names))), wait_refs,
                               after, f"rs_wait{group}", ks=ks)
        return dict(zip(names, lands))


def _pair_sum(a, b, name):
    n, rows, _ = a.shape
    tr = 384 if rows % 384 == 0 else rows

    def kern(a_ref, b_ref, o_ref):
        o_ref[...] = (a_ref[...].astype(F32) + b_ref[...].astype(F32)).astype(BF16)

    blk = pl.BlockSpec((1, tr, D), lambda i, t: (i, t, 0))
    return pl.pallas_call(
        kern, grid=(n, rows // tr), in_specs=[blk, blk], out_specs=blk,
        out_shape=jax.ShapeDtypeStruct(a.shape, BF16), compiler_params=_cparams(2), name=name)(a, b)


def _sum_contributions(r_ref):
    g = r_ref[0].astype(F32)
    for slot in range(1, r_ref.shape[0]):
        g = g + r_ref[slot].astype(F32)
    return g


def _adam_math(g, w, m, v):
    c1 = 1.0 / (1.0 - ADAM_B1 ** ADAM_STEP)
    c2 = 1.0 / (1.0 - ADAM_B2 ** ADAM_STEP)
    mn = ADAM_B1 * m + (1.0 - ADAM_B1) * g
    vn = ADAM_B2 * v + (1.0 - ADAM_B2) * (g * g)
    return -ADAM_LR * ((mn * c1) / (jnp.sqrt(vn * c2) + ADAM_EPS) + ADAM_WD * w), mn, vn


def _adamw(R, w, m, v, *, tr, name, layer=None, prev=None):
    rows, C = w.shape[-2:]
    nprev = 0 if prev is None else 4

    def kern(r_ref, w_ref, m_ref, v_ref, *rest):
        g_out, d_out, m_out, v_out = rest[nprev:]
        g = _sum_contributions(r_ref)
        g_out[...] = g
        d_out[...], m_out[...], v_out[...] = _adam_math(g, w_ref[...], m_ref[...], v_ref[...])

    if layer is None:
        tile = pl.BlockSpec((tr, C), lambda i: (i, 0))
    else:
        tile = pl.BlockSpec((None, tr, C), lambda i: (layer, i, 0))
    shp = jax.ShapeDtypeStruct(w.shape, F32)
    return pl.pallas_call(
        kern, grid=(rows // tr,),
        in_specs=[pl.BlockSpec((R.shape[0], tr, C), lambda i: (0, i, 0)), tile, tile, tile]
        + [pl.BlockSpec(memory_space=pl.ANY)] * nprev,
        out_specs=[tile] * 4, out_shape=[shp] * 4,
        input_output_aliases={4 + k: k for k in range(nprev)},
        compiler_params=_cparams(1), name=name)(R, w, m, v, *(prev or ()))


def _adamw_pool_group(R, w, m, v):
    rows = SEC_ROWS["pg"]

    def kern(r_ref, w_ref, m_ref, v_ref, g_out, d_out, m_out, v_out):
        g = _sum_contributions(r_ref)
        g_out[0] = g
        d_out[0], m_out[0], v_out[0] = _adam_math(g, w_ref[0], m_ref[0], v_ref[0])

    blk = pl.BlockSpec((1, rows, PGD), lambda i: (i, 0, 0))
    shp = jax.ShapeDtypeStruct((POOL_G, rows, PGD), F32)
    return pl.pallas_call(
        kern, grid=(POOL_G,),
        in_specs=[pl.BlockSpec((NDEV, rows, PGD), lambda i: (0, 0, i)), blk, blk, blk],
        out_specs=[blk] * 4, out_shape=[shp] * 4, compiler_params=_cparams(1), name="adamw_pg")(R, w, m, v)


def _adamw_transposed(R, w, m, v, name):
    rows = R.shape[1]
    tr = 128

    def kern(r_ref, w_ref, m_ref, v_ref, g_out, d_out, m_out, v_out):
        g = _sum_contributions(r_ref).T
        g_out[...] = g
        d_out[...], m_out[...], v_out[...] = _adam_math(g, w_ref[...], m_ref[...], v_ref[...])

    tile = pl.BlockSpec((D, tr), lambda i: (0, i))
    shp = jax.ShapeDtypeStruct((D, rows), F32)
    return pl.pallas_call(
        kern, grid=(rows // tr,),
        in_specs=[pl.BlockSpec((R.shape[0], tr, D), lambda i: (0, i, 0)), tile, tile, tile],
        out_specs=[tile] * 4, out_shape=[shp] * 4, compiler_params=_cparams(1), name=name)(R, w, m, v)


def _pack_sections(w_qkv, w_attn_out, w_pool_in, w_pool_group, w_ffn_gate_up, w_ffn_down):
    pg = w_pool_group[0].transpose(1, 0, 2).reshape(SEC_ROWS["pg"], D)
    return {"qkv": w_qkv[0].T, "wo": w_attn_out[0], "wpi": w_pool_in[0], "gu0": w_ffn_gate_up[0].T,
            "gu1": w_ffn_gate_up[1].T, "d0": w_ffn_down[0], "d1": w_ffn_down[1], "pg": pg}


def _vec_pack(attn_norm, ffn_norm, final_norm, pool_norm_sh, pool_scale_sh, me):
    def place(sh):
        return lax.dynamic_update_slice(jnp.zeros((1, D), F32), sh, (0, me * 128))
    return jnp.concatenate([attn_norm, ffn_norm, final_norm.reshape(1, D), place(pool_norm_sh),
                            place(pool_scale_sh), jnp.zeros((2, D), F32)], axis=0)


def _vec_unpack(p, me):
    def take(r):
        return lax.dynamic_slice(p[r:r + 1], (0, me * 128), (1, 128))
    return p[0:1], p[1:3], p[3], take(4), take(5)


def kernel(x, attn_norm, w_qkv, w_attn_out, pool_norm, w_pool_in, w_pool_group, pool_scale, ffn_norm, w_ffn_gate_up, w_ffn_down, final_norm, loss_target, m_attn_norm, m_w_qkv, m_w_attn_out, m_pool_norm, m_w_pool_in, m_w_pool_group, m_pool_scale, m_ffn_norm, m_w_ffn_gate_up, m_w_ffn_down, m_final_norm, v_attn_norm, v_w_qkv, v_w_attn_out, v_pool_norm, v_w_pool_in, v_w_pool_group, v_pool_scale, v_ffn_norm, v_w_ffn_gate_up, v_w_ffn_down, v_final_norm):
    me = 4 * lax.axis_index("x") + 2 * lax.axis_index("y") + lax.axis_index("c")

    def make_shards(wq, wo, wpi, wpg, wgu, wd, pn, ps):
        shards = _pack_sections(wq, wo, wpi, wpg, wgu, wd)
        shards["pv"] = jnp.concatenate([pn, ps, jnp.zeros((6, 128), F32)], axis=0)
        return shards

    comm = _Comm((w_qkv, w_attn_out, w_pool_in, w_pool_group, w_ffn_gate_up, w_ffn_down, pool_norm, pool_scale),
                 make_shards, me, placed={"qkv": _place_transposed(w_qkv[0], me, "place_qkv")})

    grad_x, vec = _local_step(x[0], loss_target[0], comm, attn_norm, ffn_norm, final_norm)

    small = ((attn_norm, ffn_norm, final_norm, pool_norm, pool_scale),
             (m_attn_norm, m_ffn_norm, m_final_norm, m_pool_norm, m_pool_scale),
             (v_attn_norm, v_ffn_norm, v_final_norm, v_pool_norm, v_pool_scale))
    small, grad_x = lax.optimization_barrier((small, grad_x))
    vw, vm, vv = (_vec_pack(*s, me) for s in small)

    gu_t = [jnp.swapaxes(a, 1, 2) for a in (w_ffn_gate_up, m_w_ffn_gate_up, v_w_ffn_gate_up)]
    res = {}
    gu_res, d_res = None, None
    vec_out = None
    vec_land = lax.dynamic_update_slice(lax.empty((NDEV, 8, D), F32), vec[None], (me, 0, 0))
    vec_sems = _split_start([vec], [0], [vec_land], lambda j, src, land, me_, pid, i: (src, land.at[me_]),
                            "vec_start")
    after = (grad_x, vec_sems[4])
    for group in range(len(RS_GROUPS)):
        if group == len(RS_GROUPS) - 1:
            _, (VR,) = _split_wait(vec_sems[2], [0], vec_sems[3], vec_sems[0], vec_sems[1], [0],
                                   lambda j, src, land: (src, land.at[0]), after, "vec_wait")
            vec_out = _adamw(VR, vw, vm, vv, tr=8, name="adamw_vec")
            after = vec_out[0]
        for n, R in comm.received(group, after).items():
            if n in ("d0", "d1"):
                d_res = _adamw(R, w_ffn_down, m_w_ffn_down, v_w_ffn_down, tr=352, name=f"adamw_{n}",
                               layer=int(n[1]), prev=d_res)
                after = d_res[0]
            elif n in ("gu0", "gu1"):
                gu_res = _adamw(R, *gu_t, tr=352, name=f"adamw_{n}", layer=int(n[2]), prev=gu_res)
                after = gu_res[0]
            elif n == "pg":
                out = _adamw_pool_group(R, w_pool_group[0], m_w_pool_group[0], v_w_pool_group[0])
                res["pg"] = tuple(a[None] for a in out)
                after = out[0]
            elif n in ("wo", "wpi"):
                w, m, v = ((w_attn_out, m_w_attn_out, v_w_attn_out) if n == "wo"
                           else (w_pool_in, m_w_pool_in, v_w_pool_in))
                res[n] = _adamw(R, w[0], m[0], v[0], tr=128, name=f"adamw_{n}")
                res[n] = tuple(a[None] for a in res[n])
                after = res[n][0]
            else:
                out = _adamw_transposed(R, w_qkv[0], m_w_qkv[0], v_w_qkv[0], "adamw_qkv")
                res["qkv"] = tuple(a[None] for a in out)
                after = out[0]
    res["gu"] = tuple(jnp.swapaxes(a, 1, 2) for a in gu_res)
    res["d"] = tuple(d_res)

    outs = []
    for kind in range(4):
        an, fn, fin, pn, ps = _vec_unpack(vec_out[kind], me)
        outs.append((an, res["qkv"][kind], res["wo"][kind], pn, res["wpi"][kind], res["pg"][kind], ps, fn,
                     res["gu"][kind], res["d"][kind], fin))
    loss = 0.5 * jnp.sum(vec_out[0][6]) / D
    return (loss, grad_x[None]) + outs[0] + outs[1] + outs[2] + outs[3]
```

```python
import jax
import jax.numpy as jnp
from jax import lax
from jax.experimental import pallas as pl
from jax.experimental.pallas import tpu as pltpu

F32 = jnp.float32
BF16 = jnp.bfloat16

D = 1024
NDEV = 8
HEADS = 8
HD = 128
QB = 128
NGROUPS = 3
DILS = (1, 4, 16)
DFF = 2816
HCH = 1408
POOL_G = 4
PGD = 256
RMS_EPS = 1e-6
NEG = -1e30

ADAM_LR = 0.001
ADAM_B1 = 0.9
ADAM_B2 = 0.999
ADAM_EPS = 1e-08
ADAM_WD = 0.01
ADAM_STEP = 10

VMEM_LIMIT = 52 * 1024 * 1024

SECTIONS = (("qkv", 1152), ("wo", 128), ("wpi", 128), ("gu0", 704), ("gu1", 704),
            ("d0", 352), ("d1", 352), ("pg", 32))
SEC_ROWS = dict(SECTIONS)
SEC_ROWS["pv"] = 8


def _cparams(n_grid):
    return pltpu.CompilerParams(dimension_semantics=("arbitrary",) * n_grid, vmem_limit_bytes=VMEM_LIMIT)


def _shard_pos(name, dev):
    n = SEC_ROWS[name]
    if name in ("gu0", "gu1"):
        return ((dev % 4) // 2) * (2 * HCH) + (dev // 4) * HCH + (dev % 2) * n
    return dev * n


def _mm(a, b, *, mode, M, N, K, tm, tn, tk, out_dtype, name, a_off=(0, 0), b_off=(0, 0), res=None,
        out_rows=None, out_off=0, out_prev=None, deps=()):
    nm, nn, nk = M // tm, N // tn, K // tk
    assert nm * tm == M and nn * tn == N and nk * tk == K
    if mode == "nn":
        a_bs, b_bs = (tm, tk), (tk, tn)
        a_ix = lambda i, j, k: (i, k)
        b_ix = lambda i, j, k: (k, j)
        dims = (((1,), (0,)), ((), ()))
    elif mode == "nt":
        a_bs, b_bs = (tm, tk), (tn, tk)
        a_ix = lambda i, j, k: (i, k)
        b_ix = lambda i, j, k: (j, k)
        dims = (((1,), (1,)), ((), ()))
    else:
        a_bs, b_bs = (tk, tm), (tk, tn)
        a_ix = lambda i, j, k: (k, i)
        b_ix = lambda i, j, k: (k, j)
        dims = (((0,), (0,)), ((), ()))

    def spec(bs, ix, off):
        def im(i, j, k):
            r, c = ix(i, j, k)
            return (r + off[0], c + off[1])
        return pl.BlockSpec(bs, im)

    in_specs = [spec(a_bs, a_ix, a_off), spec(b_bs, b_ix, b_off)]
    args = [a, b]
    if res is not None:
        in_specs.append(pl.BlockSpec((tm, tn), lambda i, j, k: (i, j)))
        args.append(res)
    out_shape = jax.ShapeDtypeStruct((M if out_rows is None else out_rows, N), out_dtype)
    out_spec = pl.BlockSpec((tm, tn), lambda i, j, k: (i + out_off, j))
    has_res = res is not None
    extra = list(deps) + ([out_prev] if out_prev is not None else [])
    for dep in extra:
        in_specs.append(pl.BlockSpec(memory_space=pl.ANY))
        args.append(dep)
    o_pos = 2 + int(has_res) + len(extra)
    aliases = {len(args) - 1: 0} if out_prev is not None else {}

    def kern(*refs):
        a_ref, b_ref = refs[0], refs[1]
        res_ref = refs[2] if has_res else None
        o_ref = refs[o_pos]
        av = a_ref[...]
        bv = b_ref[...]
        if av.dtype != BF16:
            av = av.astype(BF16)
        if bv.dtype != BF16:
            bv = bv.astype(BF16)
        part = lax.dot_general(av, bv, dims, preferred_element_type=F32)

        def write(val):
            if has_res:
                val = val + res_ref[...]
            o_ref[...] = val.astype(out_dtype)

        if nk == 1:
            write(part)
        else:
            acc_ref = refs[-1]
            k = pl.program_id(2)

            @pl.when(k == 0)
            def _():
                acc_ref[...] = part

            @pl.when(k > 0)
            def _():
                acc_ref[...] += part

            @pl.when(k == nk - 1)
            def _():
                write(acc_ref[...])

    scratch = [pltpu.VMEM((tm, tn), F32)] if nk > 1 else []
    return pl.pallas_call(
        kern, grid=(nm, nn, nk), in_specs=in_specs, out_specs=out_spec, out_shape=out_shape,
        scratch_shapes=scratch, input_output_aliases=aliases, compiler_params=_cparams(3), name=name)(*args)


def _mm_rms_bwd(a, b, x, g, dres, *, mode, M, K, tm, name, b_off=(0, 0), deps=(), folded=()):
    nd, nf = len(deps), len(folded)
    b_bs = (K, D) if mode == "nn" else (D, K)
    dims = (((1,), (0,)), ((), ())) if mode == "nn" else (((1,), (1,)), ((), ()))

    def kern(a_ref, b_ref, x_ref, g_ref, dres_ref, *rest):
        f_refs = rest[:nf]
        dx_ref, dg_ref = rest[nf + nd:nf + nd + 2]
        i = pl.program_id(0)
        av = a_ref[...]
        if av.dtype != BF16:
            av = av.astype(BF16)
        dhv = lax.dot_general(av, b_ref[...], dims, preferred_element_type=F32)
        if nf:
            acc_ref = rest[-1]
            _chunks_put(acc_ref, dhv)
            for f_ref in f_refs:
                dil = f_ref.shape[0]
                for res in range(dil):
                    _chunks_add_rows(acc_ref, f_ref[res], res, tm // dil, dil, True)
            dhv = _chunks_get(acc_ref)
        xv = x_ref[...]
        r = lax.rsqrt(jnp.mean(xv * xv, axis=-1, keepdims=True) + RMS_EPS)
        xhat = xv * r
        gy = dhv * g_ref[...]
        dx_ref[...] = dres_ref[...] + r * (gy - xhat * jnp.mean(gy * xhat, axis=-1, keepdims=True))
        part = jnp.sum(dhv * xhat, axis=0, keepdims=True)

        @pl.when(i == 0)
        def _():
            dg_ref[...] = part

        @pl.when(i > 0)
        def _():
            dg_ref[...] += part

    row = pl.BlockSpec((tm, D), lambda i: (i, 0))
    vec = pl.BlockSpec((1, D), lambda i: (0, 0))
    return pl.pallas_call(
        kern, grid=(M // tm,),
        in_specs=[pl.BlockSpec((tm, K), lambda i: (i, 0)),
                  pl.BlockSpec(b_bs, lambda i: b_off, pipeline_mode=pl.Buffered(1)), row, vec, row]
        + [pl.BlockSpec((f.shape[0], tm // f.shape[0], D), lambda i: (0, i, 0)) for f in folded]
        + [pl.BlockSpec(memory_space=pl.ANY)] * nd,
        out_specs=[row, vec],
        out_shape=[jax.ShapeDtypeStruct((M, D), F32), jax.ShapeDtypeStruct((1, D), F32)],
        scratch_shapes=[pltpu.VMEM((D // 128, tm, 128), F32)] if nf else [],
        compiler_params=_cparams(1), name=name)(a, b, x, g, dres, *folded, *deps)


def _norm_tail(xv, gv, rest, head):
    r = lax.rsqrt(jnp.mean(xv * xv, axis=-1, keepdims=True) + RMS_EPS)
    xhat = xv * r
    if not head:
        xo_ref, h_ref = rest
        xo_ref[...] = xv
        h_ref[...] = (xhat * gv).astype(BF16)
        return
    t_ref, dx_ref, dg_ref, ls_ref = rest
    i = pl.program_id(0)
    e = xhat * gv - t_ref[...]
    dy = e * (1.0 / D)
    gy = dy * gv
    dx_ref[...] = r * (gy - xhat * jnp.mean(gy * xhat, axis=-1, keepdims=True))
    dgp = jnp.sum(dy * xhat, axis=0, keepdims=True)
    lsp = jnp.sum(e * e, axis=0, keepdims=True)

    @pl.when(i == 0)
    def _():
        dg_ref[...] = dgp
        ls_ref[...] = lsp

    @pl.when(i > 0)
    def _():
        dg_ref[...] += dgp
        ls_ref[...] += lsp


def _ffn_fwd(h, wgu, wd, res, g, *, name, tgt=None):
    S = h.shape[0]
    tm = 256
    nj = DFF // HCH
    head = tgt is not None

    def kern(h_ref, wgu_ref, wd_ref, res_ref, g_ref, *rest):
        t_refs, (gu_ref, act_ref), tail = rest[:int(head)], rest[int(head):int(head) + 2], rest[int(head) + 2:]
        hv = h_ref[...]
        for j in range(nj):
            gu = lax.dot_general(hv, wgu_ref[2 * HCH * j:2 * HCH * (j + 1), :], (((1,), (1,)), ((), ())),
                                 preferred_element_type=F32)
            gu_ref[:, 2 * HCH * j:2 * HCH * (j + 1)] = gu.astype(BF16)
            gate = gu[:, :HCH]
            act_ref[:, HCH * j:HCH * (j + 1)] = (gate * jax.nn.sigmoid(gate) * gu[:, HCH:]).astype(BF16)
        xv = res_ref[...] + jnp.dot(act_ref[...], wd_ref[...], preferred_element_type=F32)
        _norm_tail(xv, g_ref[...], tuple(t_refs) + tuple(tail), head)

    row = pl.BlockSpec((tm, D), lambda i: (i, 0))
    vec = pl.BlockSpec((1, D), lambda i: (0, 0))
    in_specs = [row, pl.BlockSpec((2 * DFF, D), lambda i: (0, 0), pipeline_mode=pl.Buffered(1)),
                pl.BlockSpec((DFF, D), lambda i: (0, 0), pipeline_mode=pl.Buffered(1)), row, vec]
    out_specs = [pl.BlockSpec((tm, 2 * DFF), lambda i: (i, 0)), pl.BlockSpec((tm, DFF), lambda i: (i, 0))]
    out_shape = [jax.ShapeDtypeStruct((S, 2 * DFF), BF16), jax.ShapeDtypeStruct((S, DFF), BF16)]
    args = [h, wgu, wd, res, g]
    if head:
        in_specs, args = in_specs + [row], args + [tgt]
        out_specs += [row, vec, vec]
        out_shape += [jax.ShapeDtypeStruct((S, D), F32), jax.ShapeDtypeStruct((1, D), F32),
                      jax.ShapeDtypeStruct((1, D), F32)]
    else:
        out_specs += [row, row]
        out_shape += [jax.ShapeDtypeStruct((S, D), F32), jax.ShapeDtypeStruct((S, D), BF16)]
    outs = pl.pallas_call(kern, grid=(S // tm,), in_specs=in_specs, out_specs=out_specs, out_shape=out_shape,
                          compiler_params=_cparams(1), name=name)(*args)
    return outs[0], outs[1], tuple(outs[2:])


def _mm_res_norm(a, b, res, g, *, K, tm, name, b_off=(0, 0), tgt=None):
    M = a.shape[0]
    head = tgt is not None

    def kern(a_ref, b_ref, res_ref, g_ref, *rest):
        xv = res_ref[...] + jnp.dot(a_ref[...], b_ref[...], preferred_element_type=F32)
        _norm_tail(xv, g_ref[...], rest, head)

    row = pl.BlockSpec((tm, D), lambda i: (i, 0))
    vec = pl.BlockSpec((1, D), lambda i: (0, 0))
    in_specs = [pl.BlockSpec((tm, K), lambda i: (i, 0)),
                pl.BlockSpec((K, D), lambda i: b_off, pipeline_mode=pl.Buffered(1)), row, vec]
    if head:
        return pl.pallas_call(
            kern, grid=(M // tm,), in_specs=in_specs + [row], out_specs=[row, vec, vec],
            out_shape=[jax.ShapeDtypeStruct((M, D), F32), jax.ShapeDtypeStruct((1, D), F32),
                       jax.ShapeDtypeStruct((1, D), F32)],
            compiler_params=_cparams(1), name=name)(a, b, res, g, tgt)
    return pl.pallas_call(
        kern, grid=(M // tm,), in_specs=in_specs, out_specs=[row, row],
        out_shape=[jax.ShapeDtypeStruct((M, D), F32), jax.ShapeDtypeStruct((M, D), BF16)],
        compiler_params=_cparams(1), name=name)(a, b, res, g)


def _rms_fwd(x, g, name, deps=()):
    S = x.shape[0]
    tr = 512

    def kern(x_ref, g_ref, *rest):
        h_ref = rest[-1]
        xv = x_ref[...]
        r = lax.rsqrt(jnp.mean(xv * xv, axis=-1, keepdims=True) + RMS_EPS)
        h_ref[...] = (xv * r * g_ref[...]).astype(BF16)

    return pl.pallas_call(
        kern, grid=(S // tr,),
        in_specs=[pl.BlockSpec((tr, D), lambda i: (i, 0)), pl.BlockSpec((1, D), lambda i: (0, 0))]
        + [pl.BlockSpec(memory_space=pl.ANY)] * len(deps),
        out_specs=pl.BlockSpec((tr, D), lambda i: (i, 0)),
        out_shape=jax.ShapeDtypeStruct((S, D), BF16), compiler_params=_cparams(1), name=name)(x, g, *deps)


def _chunks_put(scr, val):
    for c in range(scr.shape[0]):
        scr[c] = val[:, c * 128:(c + 1) * 128]


def _chunks_get(scr):
    return jnp.concatenate([scr[c] for c in range(scr.shape[0])], axis=1)


def _chunks_rows(scr, r, n, dil):
    return jnp.concatenate([scr.at[c][pl.ds(r, n, stride=dil), :] for c in range(scr.shape[0])], axis=1)


def _chunks_add_rows(scr, val, r, n, dil, accumulate):
    for c in range(scr.shape[0]):
        rows = pl.ds(r, n, stride=dil)
        piece = val[:, c * 128:(c + 1) * 128]
        tile = scr.at[c]
        tile[rows, :] = tile[rows, :] + piece if accumulate else piece


def _rms_fwd_folded(x, g, name, deps=()):
    S = x.shape[0]
    tr = 512
    dils = DILS[1:]

    def kern(x_ref, g_ref, *rest):
        outs, scr = rest[len(deps):-1], rest[-1]
        xv = x_ref[...]
        r = lax.rsqrt(jnp.mean(xv * xv, axis=-1, keepdims=True) + RMS_EPS)
        h = (xv * r * g_ref[...]).astype(BF16)
        outs[0][...] = h
        _chunks_put(scr, h.astype(F32))
        for o_ref, dil in zip(outs[1:], dils):
            for res in range(dil):
                o_ref[res] = _chunks_rows(scr, res, tr // dil, dil).astype(BF16)

    return pl.pallas_call(
        kern, grid=(S // tr,),
        in_specs=[pl.BlockSpec((tr, D), lambda i: (i, 0)), pl.BlockSpec((1, D), lambda i: (0, 0))]
        + [pl.BlockSpec(memory_space=pl.ANY)] * len(deps),
        out_specs=[pl.BlockSpec((tr, D), lambda i: (i, 0))]
        + [pl.BlockSpec((dil, tr // dil, D), lambda i: (0, i, 0)) for dil in dils],
        out_shape=[jax.ShapeDtypeStruct((S, D), BF16)]
        + [jax.ShapeDtypeStruct((dil, S // dil, D), BF16) for dil in dils],
        scratch_shapes=[pltpu.VMEM((D // 128, tr, 128), F32)],
        compiler_params=_cparams(1), name=name)(x, g, *deps)


def _ffn_bwd(dxo, wd, wgu, gu, xin, gain, name):
    S = dxo.shape[0]
    tm = 256
    nj = DFF // HCH

    def kern(dx_ref, wd_ref, wgu_ref, gu_ref, x_ref, g_ref, dgu_ref, dxin_ref, dg_ref):
        i = pl.program_id(0)
        dxv = dx_ref[...]
        dxb = dxv.astype(BF16)
        for j in range(nj):
            c0 = 2 * HCH * j
            dact = lax.dot_general(dxb, wd_ref[HCH * j:HCH * (j + 1), :], (((1,), (1,)), ((), ())),
                                   preferred_element_type=F32)
            gate = gu_ref[:, c0:c0 + HCH].astype(F32)
            up = gu_ref[:, c0 + HCH:c0 + 2 * HCH].astype(F32)
            sig = jax.nn.sigmoid(gate)
            silu = gate * sig
            dgu_ref[:, c0:c0 + HCH] = (dact * up * (sig * (1.0 + gate * (1.0 - sig)))).astype(BF16)
            dgu_ref[:, c0 + HCH:c0 + 2 * HCH] = (dact * silu).astype(BF16)
        dhv = jnp.dot(dgu_ref[...], wgu_ref[...], preferred_element_type=F32)
        xv = x_ref[...]
        r = lax.rsqrt(jnp.mean(xv * xv, axis=-1, keepdims=True) + RMS_EPS)
        xhat = xv * r
        gy = dhv * g_ref[...]
        dxin_ref[...] = dxv + r * (gy - xhat * jnp.mean(gy * xhat, axis=-1, keepdims=True))
        part = jnp.sum(dhv * xhat, axis=0, keepdims=True)

        @pl.when(i == 0)
        def _():
            dg_ref[...] = part

        @pl.when(i > 0)
        def _():
            dg_ref[...] += part

    row = pl.BlockSpec((tm, D), lambda i: (i, 0))
    wide = pl.BlockSpec((tm, 2 * DFF), lambda i: (i, 0))
    vec = pl.BlockSpec((1, D), lambda i: (0, 0))
    return pl.pallas_call(
        kern, grid=(S // tm,),
        in_specs=[row, pl.BlockSpec((DFF, D), lambda i: (0, 0), pipeline_mode=pl.Buffered(1)),
                  pl.BlockSpec((2 * DFF, D), lambda i: (0, 0), pipeline_mode=pl.Buffered(1)), wide, row, vec],
        out_specs=[wide, row, vec],
        out_shape=[jax.ShapeDtypeStruct((S, 2 * DFF), BF16), jax.ShapeDtypeStruct((S, D), F32),
                   jax.ShapeDtypeStruct((1, D), F32)],
        compiler_params=_cparams(1), name=name)(dxo, wd, wgu, gu, xin, gain)


def _trail(u, *, backward, name):
    S = u.shape[0]

    def kern(u_ref, o_ref):
        g = pl.program_id(0)
        for grp in range(POOL_G):
            @pl.when(g == grp)
            def _(grp=grp):
                uv = u_ref[...].astype(F32)
                row = lax.broadcasted_iota(jnp.int32, uv.shape, 0)
                cnt = jnp.minimum(row + 1, 2 << grp).astype(F32)
                s = uv / cnt if backward else uv
                for k in (1, 2, 4, 8)[:grp + 1]:
                    if backward:
                        sh = jnp.where(row < S - k, pltpu.roll(s, S - k, 0), 0.0)
                    else:
                        sh = jnp.where(row >= k, pltpu.roll(s, k, 0), 0.0)
                    s = s + sh
                if backward:
                    o_ref[...] = (s - uv).astype(BF16)
                else:
                    o_ref[...] = (s / cnt - uv).astype(BF16)

    blk = pl.BlockSpec((S, PGD), lambda g: (0, g))
    return pl.pallas_call(
        kern, grid=(POOL_G,), in_specs=[blk], out_specs=blk,
        out_shape=jax.ShapeDtypeStruct((S, D), BF16), compiler_params=_cparams(1), name=name)(u)


def _pool_out(yd, G, scale, xres):
    S = yd.shape[0]
    tm = min(S, 4096)

    def kern(y_ref, w_ref, s_ref, x_ref, o_ref):
        z = jnp.dot(y_ref[...], w_ref[...], preferred_element_type=F32)
        o_ref[...] = x_ref[...] + z * s_ref[...]

    tile = pl.BlockSpec((tm, PGD), lambda i, g: (i, g))
    return pl.pallas_call(
        kern, grid=(S // tm, POOL_G),
        in_specs=[tile, pl.BlockSpec((PGD, PGD), lambda i, g: (0, g)),
                  pl.BlockSpec((1, PGD), lambda i, g: (0, g)), tile],
        out_specs=tile, out_shape=jax.ShapeDtypeStruct((S, D), F32),
        compiler_params=_cparams(2), name="pool_out")(yd, G, scale, xres)


def _pool_out_bwd(dz, yd, G, scale, deps=()):
    S = yd.shape[0]
    tm = min(S, 4096)
    ni = S // tm
    nd = len(deps)

    def kern(dz_ref, y_ref, w_ref, s_ref, *rest):
        dy_ref, ds_ref, dw_ref, acc_ref = rest[nd:]
        i = pl.program_id(1)
        dzv = dz_ref[...]
        yv = y_ref[...]
        wv = w_ref[...]
        zraw = jnp.dot(yv, wv, preferred_element_type=F32)
        dsp = jnp.sum(dzv * zraw, axis=0, keepdims=True)
        dzr = (dzv * s_ref[...]).astype(BF16)
        dy_ref[...] = lax.dot_general(dzr, wv, (((1,), (1,)), ((), ())), preferred_element_type=F32)
        dwp = lax.dot_general(yv, dzr, (((0,), (0,)), ((), ())), preferred_element_type=F32)

        @pl.when(i == 0)
        def _():
            ds_ref[...] = dsp
            acc_ref[...] = dwp

        @pl.when(i > 0)
        def _():
            ds_ref[...] += dsp
            acc_ref[...] += dwp

        @pl.when(i == ni - 1)
        def _():
            dw_ref[...] = acc_ref[...].astype(BF16)

    tile = pl.BlockSpec((tm, PGD), lambda g, i: (i, g))
    return pl.pallas_call(
        kern, grid=(POOL_G, ni),
        in_specs=[tile, tile, pl.BlockSpec((PGD, PGD), lambda g, i: (0, g)),
                  pl.BlockSpec((1, PGD), lambda g, i: (0, g))] + [pl.BlockSpec(memory_space=pl.ANY)] * nd,
        out_specs=[tile, pl.BlockSpec((1, PGD), lambda g, i: (0, g)),
                   pl.BlockSpec((PGD, PGD), lambda g, i: (0, g))],
        out_shape=[jax.ShapeDtypeStruct((S, D), F32), jax.ShapeDtypeStruct((1, D), F32),
                   jax.ShapeDtypeStruct((PGD, D), BF16)],
        scratch_shapes=[pltpu.VMEM((PGD, PGD), F32)],
        compiler_params=_cparams(2), name="pool_out_bwd")(dz, yd, G, scale, *deps)


def _bias_table():
    qi = jnp.arange(QB)[:, None]
    ki = jnp.arange(2 * QB)[None, :]
    delta = QB + qi - ki
    inband = (delta >= 0) & (delta <= QB)
    n = NGROUPS * HEADS
    slopes = jnp.exp2(-8.0 * jnp.arange(1, n + 1, dtype=F32) / n).reshape(NGROUPS, HEADS)
    dil = jnp.asarray(DILS, F32)
    bias = -slopes[:, :, None, None] * (delta.astype(F32)[None, None] * dil[:, None, None, None])
    return jnp.where(inband[None, None], bias, NEG)


def _attn_fwd(qkv_f, bias, nb, name):
    S = qkv_f.shape[0]
    nblk = S // QB
    scale = HD ** -0.5

    def kern(q_ref, k2_ref, kp_ref, v2_ref, vp_ref, b_ref, o_ref, l_ref, s_scr, p_scr, r_scr):
        s_id = pl.program_id(0)
        col = lax.broadcasted_iota(jnp.int32, (QB, 2 * QB), 1)
        lane = lax.broadcasted_iota(jnp.int32, (QB, HD), 1)

        def keys(sub, cur2_ref, prev_ref, sl):
            if sub:
                return cur2_ref[:, sl]
            return jnp.concatenate([prev_ref[:, sl], cur2_ref[0:QB, sl]], axis=0)

        for sub in range(2):
            for h in range(HEADS):
                sl = slice(h * HD, (h + 1) * HD)
                s_scr[sub * HEADS + h] = lax.dot_general(
                    q_ref[sub * QB:(sub + 1) * QB, sl], keys(sub, k2_ref, kp_ref, sl), (((1,), (1,)), ((), ())),
                    preferred_element_type=F32)
        for sub in range(2):
            has_prev = jnp.bitwise_and(2 * s_id + sub, nb - 1) != 0
            dead = jnp.logical_and(col < QB, jnp.logical_not(has_prev))
            lse_all = jnp.zeros((QB, HD), F32)
            for h in range(HEADS):
                u = sub * HEADS + h
                s = s_scr[u] * scale + b_ref[h]
                s = jnp.where(dead, NEG, s)
                m = jnp.max(s, axis=-1, keepdims=True)
                p = jnp.exp(s - m)
                den = jnp.sum(p, axis=-1, keepdims=True)
                p_scr[u] = p.astype(BF16)
                r_scr[u] = jnp.broadcast_to(1.0 / den, (QB, HD))
                lse_all = jnp.where(lane == h, m + jnp.log(den), lse_all)
            l_ref[sub * QB:(sub + 1) * QB, :] = lse_all
        for sub in range(2):
            for h in range(HEADS):
                u = sub * HEADS + h
                sl = slice(h * HD, (h + 1) * HD)
                o = jnp.dot(p_scr[u], keys(sub, v2_ref, vp_ref, sl), preferred_element_type=F32) * r_scr[u]
                o_ref[sub * QB:(sub + 1) * QB, sl] = o.astype(BF16)

    def pair(colblk):
        return pl.BlockSpec((2 * QB, D), lambda s: (s, colblk))

    def prev(colblk):
        return pl.BlockSpec((QB, D), lambda s: (jnp.maximum(2 * s - 1, 0), colblk))

    return pl.pallas_call(
        kern, grid=(nblk // 2,),
        in_specs=[pair(0), pair(1), prev(1), pair(2), prev(2), pl.BlockSpec((HEADS, QB, 2 * QB), lambda s: (0, 0, 0))],
        out_specs=[pl.BlockSpec((2 * QB, D), lambda s: (s, 0)), pl.BlockSpec((2 * QB, HD), lambda s: (s, 0))],
        out_shape=[jax.ShapeDtypeStruct((S, D), BF16), jax.ShapeDtypeStruct((S, HD), F32)],
        scratch_shapes=[pltpu.VMEM((2 * HEADS, QB, 2 * QB), F32), pltpu.VMEM((2 * HEADS, QB, 2 * QB), BF16),
                        pltpu.VMEM((2 * HEADS, QB, HD), F32)],
        compiler_params=_cparams(1), name=name)(qkv_f, qkv_f, qkv_f, qkv_f, qkv_f, bias)


def _natural(ref, scr, tm):
    dil = ref.shape[0]
    for res in range(dil):
        _chunks_add_rows(scr, ref[res].astype(F32), res, tm // dil, dil, False)
    return _chunks_get(scr)


def _attn_merge(os, lses):
    S = os[0].shape[0]
    tm = 512

    def kern(o0, o1, o2, l0, l1, l2, om_ref, lm_ref, ls1, ls2, os1, os2):
        la = l0[...]
        lb = _natural(l1, ls1, tm)
        lc = _natural(l2, ls2, tm)
        m = jnp.maximum(jnp.maximum(la, lb), lc)
        e0, e1, e2 = jnp.exp(la - m), jnp.exp(lb - m), jnp.exp(lc - m)
        tot = e0 + e1 + e2
        lm_ref[...] = m + jnp.log(tot)
        w0, w1, w2 = e0 / tot, e1 / tot, e2 / tot
        for res in range(o1.shape[0]):
            _chunks_add_rows(os1, o1[res].astype(F32), res, tm // o1.shape[0], o1.shape[0], False)
        for res in range(o2.shape[0]):
            _chunks_add_rows(os2, o2[res].astype(F32), res, tm // o2.shape[0], o2.shape[0], False)
        for h in range(HEADS):
            sl = slice(h * HD, (h + 1) * HD)
            acc = w0[:, h:h + 1] * o0[:, sl].astype(F32) + w1[:, h:h + 1] * os1[h] + w2[:, h:h + 1] * os2[h]
            om_ref[:, sl] = acc.astype(BF16)

    def spec(a, c):
        if a.ndim == 2:
            return pl.BlockSpec((tm, c), lambda i: (i, 0))
        return pl.BlockSpec((a.shape[0], tm // a.shape[0], c), lambda i: (0, i, 0))

    return pl.pallas_call(
        kern, grid=(S // tm,),
        in_specs=[spec(a, D) for a in os] + [spec(a, HD) for a in lses],
        out_specs=[pl.BlockSpec((tm, D), lambda i: (i, 0)), pl.BlockSpec((tm, HD), lambda i: (i, 0))],
        out_shape=[jax.ShapeDtypeStruct((S, D), BF16), jax.ShapeDtypeStruct((S, HD), F32)],
        scratch_shapes=[pltpu.VMEM((1, tm, HD), F32), pltpu.VMEM((1, tm, HD), F32),
                        pltpu.VMEM((HEADS, tm, HD), F32), pltpu.VMEM((HEADS, tm, HD), F32)],
        compiler_params=_cparams(1), name="attn_merge")(*os, *lses)


def _attn_bwd_prep(dx, wo, o, lse, deps=()):
    S = o.shape[0]
    tm = 512
    dils = DILS[1:]
    nd = len(deps)

    def kern(dx_ref, w_ref, o_ref, l_ref, *rest):
        rest = rest[nd:]
        do_outs, l_outs, d_outs = rest[0:3], rest[3:5], rest[5:8]
        do_scr, l_scr, d_scr = rest[8:11]
        dov = lax.dot_general(dx_ref[...].astype(BF16), w_ref[...], (((1,), (1,)), ((), ())),
                              preferred_element_type=F32)
        lane = lax.broadcasted_iota(jnp.int32, (tm, HD), 1)
        acc = jnp.zeros((tm, HD), F32)
        for h in range(HEADS):
            sl = slice(h * HD, (h + 1) * HD)
            prod = dov[:, sl] * o_ref[:, sl].astype(F32)
            acc = jnp.where(lane == h, jnp.sum(prod, axis=-1, keepdims=True), acc)
        d_scr[0] = acc
        l_scr[0] = l_ref[...]
        _chunks_put(do_scr, dov)
        do_outs[0][...] = dov.astype(BF16)
        d_outs[0][...] = acc
        for j, dil in enumerate(dils):
            for res in range(dil):
                n = tm // dil
                do_outs[1 + j][res] = _chunks_rows(do_scr, res, n, dil).astype(BF16)
                l_outs[j][res] = _chunks_rows(l_scr, res, n, dil)
                d_outs[1 + j][res] = _chunks_rows(d_scr, res, n, dil)

    def nat(c):
        return pl.BlockSpec((tm, c), lambda i: (i, 0))

    def fol(dil, c):
        return pl.BlockSpec((dil, tm // dil, c), lambda i: (0, i, 0))

    def shapes(c, dt, with_natural):
        first = [jax.ShapeDtypeStruct((S, c), dt)] if with_natural else []
        return first + [jax.ShapeDtypeStruct((dil, S // dil, c), dt) for dil in dils]

    outs = pl.pallas_call(
        kern, grid=(S // tm,),
        in_specs=[nat(D), pl.BlockSpec((D, D), lambda i: (0, 0), pipeline_mode=pl.Buffered(1)), nat(D), nat(HD)]
        + [pl.BlockSpec(memory_space=pl.ANY)] * nd,
        out_specs=[nat(D)] + [fol(dil, D) for dil in dils] + [fol(dil, HD) for dil in dils]
        + [nat(HD)] + [fol(dil, HD) for dil in dils],
        out_shape=shapes(D, BF16, True) + shapes(HD, F32, False) + shapes(HD, F32, True),
        scratch_shapes=[pltpu.VMEM((HEADS, tm, HD), F32), pltpu.VMEM((1, tm, HD), F32), pltpu.VMEM((1, tm, HD), F32)],
        compiler_params=_cparams(1), name="attn_out_bwd")(dx, wo, o, lse, *deps)
    return outs[0:3], [lse] + list(outs[3:5]), outs[5:8]


def _attn_bwd(qkv_f, do_f, lse_f, delta_f, bias, nb, name):
    S = qkv_f.shape[0]
    nblk = S // QB
    scale = HD ** -0.5

    npair = nblk // 2

    def kern(q_ref, k2_ref, kp_ref, v2_ref, vp_ref, do_ref, l_ref, d_ref, b_ref, out_ref, dq_c, dk_c, dv_c,
             s_scr, dp_scr, ds_scr, p_scr):
        s_id = pl.program_id(0)

        @pl.when(s_id == 0)
        def _():
            dq_c[...] = jnp.zeros_like(dq_c)
            dk_c[...] = jnp.zeros_like(dk_c)
            dv_c[...] = jnp.zeros_like(dv_c)

        @pl.when(s_id == npair)
        def _():
            out_ref[:, 0:D] = dq_c[...].astype(BF16)
            out_ref[:, D:2 * D] = dk_c[...].astype(BF16)
            out_ref[:, 2 * D:3 * D] = dv_c[...].astype(BF16)

        def keys(sub, cur2_ref, prev_ref, sl):
            if sub:
                return cur2_ref[:, sl]
            return jnp.concatenate([prev_ref[:, sl], cur2_ref[0:QB, sl]], axis=0)

        @pl.when(s_id < npair)
        def _():
            col = lax.broadcasted_iota(jnp.int32, (QB, 2 * QB), 1)
            out_ref[:, 0:D] = dq_c[...].astype(BF16)
            for sub in range(2):
                rows = slice(sub * QB, (sub + 1) * QB)
                for h in range(HEADS):
                    sl = slice(h * HD, (h + 1) * HD)
                    u = sub * HEADS + h
                    s_scr[u] = lax.dot_general(q_ref[rows, sl], keys(sub, k2_ref, kp_ref, sl),
                                               (((1,), (1,)), ((), ())), preferred_element_type=F32)
                    dp_scr[u] = lax.dot_general(do_ref[rows, sl], keys(sub, v2_ref, vp_ref, sl),
                                                (((1,), (1,)), ((), ())), preferred_element_type=F32)
            for sub in range(2):
                rows = slice(sub * QB, (sub + 1) * QB)
                has_prev = jnp.bitwise_and(2 * s_id + sub, nb - 1) != 0
                dead = jnp.logical_and(col < QB, jnp.logical_not(has_prev))
                lv = l_ref[rows, :]
                dv_ = d_ref[rows, :]
                for h in range(HEADS):
                    u = sub * HEADS + h
                    s = s_scr[u] * scale + b_ref[h]
                    s = jnp.where(dead, NEG, s)
                    p = jnp.exp(s - lv[:, h:h + 1])
                    ds_scr[u] = (p * (dp_scr[u] - dv_[:, h:h + 1]) * scale).astype(BF16)
                    p_scr[u] = p.astype(BF16)
            for h in range(HEADS):
                sl = slice(h * HD, (h + 1) * HD)
                parts = []
                for sub in range(2):
                    rows = slice(sub * QB, (sub + 1) * QB)
                    u = sub * HEADS + h
                    ds = ds_scr[u]
                    dq_c[rows, sl] = jnp.dot(ds, keys(sub, k2_ref, kp_ref, sl), preferred_element_type=F32)
                    dkk = lax.dot_general(ds, q_ref[rows, sl], (((0,), (0,)), ((), ())), preferred_element_type=F32)
                    dvv = lax.dot_general(p_scr[u], do_ref[rows, sl], (((0,), (0,)), ((), ())),
                                          preferred_element_type=F32)
                    parts.append((dkk, dvv))
                for which, carry, base in ((0, dk_c, D), (1, dv_c, 2 * D)):
                    first, second = parts[0][which], parts[1][which]
                    cols = slice(base + h * HD, base + (h + 1) * HD)
                    out_ref[0:QB, cols] = carry[0:QB, sl].astype(BF16)
                    out_ref[QB:2 * QB, cols] = (carry[QB:2 * QB, sl] + first[:QB]).astype(BF16)
                    carry[0:QB, sl] = first[QB:] + second[:QB]
                    carry[QB:2 * QB, sl] = second[QB:]

    last = npair - 1

    def pair(colblk, c):
        return pl.BlockSpec((2 * QB, c), lambda s: (jnp.minimum(s, last), colblk))

    def prev(colblk):
        return pl.BlockSpec((QB, D), lambda s: (jnp.maximum(2 * jnp.minimum(s, last) - 1, 0), colblk))

    return pl.pallas_call(
        kern, grid=(npair + 1,),
        in_specs=[pair(0, D), pair(1, D), prev(1), pair(2, D), prev(2), pair(0, D), pair(0, HD), pair(0, HD),
                  pl.BlockSpec((HEADS, QB, 2 * QB), lambda s: (0, 0, 0))],
        out_specs=pl.BlockSpec((2 * QB, 3 * D), lambda s: (jnp.maximum(s - 1, 0), 0)),
        out_shape=jax.ShapeDtypeStruct((S, 3 * D), BF16),
        scratch_shapes=[pltpu.VMEM((2 * QB, D), F32), pltpu.VMEM((2 * QB, D), F32), pltpu.VMEM((2 * QB, D), F32),
                        pltpu.VMEM((2 * HEADS, QB, 2 * QB), F32), pltpu.VMEM((2 * HEADS, QB, 2 * QB), F32),
                        pltpu.VMEM((2 * HEADS, QB, 2 * QB), BF16), pltpu.VMEM((2 * HEADS, QB, 2 * QB), BF16)],
        compiler_params=_cparams(1), name=name)(qkv_f, qkv_f, qkv_f, qkv_f, qkv_f, do_f, lse_f, delta_f, bias)


def _local_step(x, tgt, comm, attn_norm, ffn_norm, final_norm):
    S = x.shape[0]
    bias = _bias_table()
    g_attn = attn_norm.reshape(1, D)
    g_f0 = ffn_norm[0:1]
    g_f1 = ffn_norm[1:2]
    g_fin = final_norm.reshape(1, D)
    W = {}

    def ffn_fwd(xin, h, l, next_gain, target=None):
        return _ffn_fwd(h, W[f"gu{l}"], W[f"d{l}"], xin, next_gain, tgt=target, name=f"ffn_fwd{l}")

    def ffn_bwd(dxo, xin, gain, h, gu, act, l, rs_group):
        dgu, dxin, dgain = _ffn_bwd(dxo, W[f"d{l}"], W[f"gu{l}"], gu, xin, gain, f"ffn_bwd{l}")
        gw_d = _mm(act, dxo, mode="tn", M=DFF, N=D, K=S, tm=HCH, tn=D, tk=2048, out_dtype=BF16, name=f"gw_d{l}")
        gw_gu = _mm(dgu, h, mode="tn", M=2 * DFF, N=D, K=S, tm=HCH, tn=D, tk=2048, out_dtype=BF16, name=f"gw_gu{l}")
        return dxin, dgain, comm.send_grads(rs_group, {f"d{l}": gw_d, f"gu{l}": gw_gu})

    nbs = [S // QB // dil for dil in DILS]
    hf = _rms_fwd_folded(x, g_attn, "rms_attn", deps=comm.ag_tokens)
    hf = [h.reshape(S, D) for h in hf]
    W.update(comm.weights(0, hf[0]))
    qkv_f, o_f, lse_f = [], [], []
    for g, dil in enumerate(DILS):
        qkv_f.append(_mm(hf[g], W["qkv"], mode="nt", M=S, N=3 * D, K=D, tm=2048, tn=1024, tk=D, out_dtype=BF16,
                         b_off=(3 * g, 0), name=f"qkv_proj{g}"))
        og, lg = _attn_fwd(qkv_f[g], bias[g], nbs[g], f"attn_fwd{g}")
        o_f.append(og if dil == 1 else og.reshape(dil, S // dil, D))
        lse_f.append(lg if dil == 1 else lg.reshape(dil, S // dil, HD))
    passing = [comm.pass_on(1, tuple(o_f)), comm.pass_on(2, tuple(o_f))]
    (o_f, lse_f), passing = lax.optimization_barrier(((o_f, lse_f), passing))
    o, lse = _attn_merge(o_f, lse_f)
    W.update(comm.weights(1, (o, passing[0])))
    x1, h1 = _mm_res_norm(o, W["wo"], x, g_f0, K=D, tm=1024, name="attn_out")
    pv = W["pv"].reshape(NDEV, 8, 128)
    pool_norm, pool_scale = pv[:, 0, :].reshape(1, D), pv[:, 1, :].reshape(1, D)
    gu0, act0, (x2, h2) = ffn_fwd(x1, h1, 0, pool_norm)

    W.update(comm.weights(2, (x2, passing[1])))
    u = _mm(h2, W["wpi"], mode="nn", M=S, N=D, K=D, tm=1024, tn=D, tk=D, out_dtype=F32, name="pool_in")
    yd = _trail(u, backward=False, name="trail_fwd")
    x3 = _pool_out(yd, W["pg"], pool_scale, x2)
    h3 = _rms_fwd(x3, g_f1, "rms_ffn1")
    gu1, act1, (dx4, d_fin, lossvec) = ffn_fwd(x3, h3, 1, g_fin, target=tgt)

    dx3, d_f1, token = ffn_bwd(dx4, x3, g_f1, h3, gu1, act1, 1, 0)
    dyd, d_scale, gw_pg = _pool_out_bwd(dx3, yd, W["pg"], pool_scale, deps=(token,))
    du = _trail(dyd, backward=True, name="trail_bwd")
    gw_pi = _mm(h2, du, mode="tn", M=D, N=D, K=S, tm=D, tn=D, tk=S, out_dtype=BF16, name="gw_pi")
    token = comm.send_grads(1, {"pg": gw_pg, "wpi": gw_pi})
    dx2, d_pool = _mm_rms_bwd(du, W["wpi"], x2, pool_norm, dx3, mode="nt", M=S, K=D, tm=1024, deps=(token,),
                              name="pool_in_bwd")
    dx1, d_f0, token = ffn_bwd(dx2, x1, g_f0, h1, gu0, act0, 0, 2)

    gw_o = _mm(o, dx1, mode="tn", M=D, N=D, K=S, tm=D, tn=D, tk=2048, out_dtype=BF16, deps=(token,), name="gw_o")
    do_f, lse_ff, delta_f = _attn_bwd_prep(dx1, W["wo"], o, lse, deps=(token,))
    dqkv_f, gw_qkv = [], None
    for g in range(NGROUPS):
        dqkv_f.append(_attn_bwd(qkv_f[g], do_f[g].reshape(S, D), lse_ff[g].reshape(S, HD),
                                delta_f[g].reshape(S, HD), bias[g], nbs[g], f"attn_bwd{g}"))
        gw_qkv = _mm(dqkv_f[g], hf[g], mode="tn", M=3 * D, N=D, K=S, tm=1024, tn=D, tk=S, out_dtype=BF16,
                     out_rows=NGROUPS * 3 * D, out_off=3 * g, out_prev=gw_qkv, name=f"gw_qkv{g}")
    token = comm.send_grads_pairwise({"wo": gw_o, "qkv": gw_qkv})
    folded = []
    for g in reversed(range(1, NGROUPS)):
        dh0_g = _mm(dqkv_f[g], W["qkv"], mode="nn", M=S, N=D, K=3 * D, tm=1024, tn=D, tk=3 * D, out_dtype=F32,
                    b_off=(g, 0), deps=(token,), name=f"qkv_proj_bwd{g}")
        folded.append(dh0_g.reshape(DILS[g], S // DILS[g], D))
        if g == NGROUPS - 1:
            token = comm.pass_grads(dh0_g)
    grad_x, d_attn = _mm_rms_bwd(dqkv_f[0], W["qkv"], x, g_attn, dx1, mode="nn", M=S, K=3 * D, tm=512,
                                 deps=(token,), folded=folded, name="qkv_proj_bwd0")

    vec = jnp.concatenate([d_attn, d_f0, d_f1, d_fin, d_pool, d_scale, lossvec, jnp.zeros((1, D), F32)], axis=0)
    return grad_x, vec


def _mesh_pos():
    x, y, c = lax.axis_index("x"), lax.axis_index("y"), lax.axis_index("c")
    return x, y, c, 4 * x + 2 * y + c


def _peer(x, y, c, k):
    kx, ky, kc = (k >> 2) & 1, (k >> 1) & 1, k & 1
    px = 1 - x if kx else x
    py = 1 - y if ky else y
    pc = 1 - c if kc else c
    return (px, py, pc), 4 * px + 2 * py + pc


ANY = pl.BlockSpec(memory_space=pl.ANY)


HBM = pl.BlockSpec(memory_space=pltpu.HBM)
SEMS = pl.BlockSpec(memory_space=pltpu.SEMAPHORE)
EFFECT = pltpu.SideEffectType.DATAFLOW_SIDE_EFFECTING

AG_GROUPS = (("qkv",), ("wo", "gu0", "d0", "pv"), ("wpi", "pg", "gu1", "d1"))
AG_ORDER = tuple(n for grp in AG_GROUPS for n in grp)
RS_GROUPS = (("d1", "gu1"), ("pg", "wpi"), ("d0", "gu0"), ("wo", "qkv"))


def _hbm(a):
    return pltpu.with_memory_space_constraint(a, pltpu.HBM)


def _remote(src, dst, send, recv, peer):
    return pltpu.make_async_remote_copy(src_ref=src, dst_ref=dst, send_sem=send, recv_sem=recv, device_id=peer,
                                        device_id_type=pl.DeviceIdType.MESH)


ALL_KS = tuple(range(1, NDEV))
AG_KS1 = (1, 2, 4, 6)
AG_KS2 = (2, 4, 6)
RS_KS_PAIR = (1, 3, 5, 7)
RS_KS_CHIPS = (2, 4, 6)


def _split_start(srcs, src_of, lands, copy_refs, name, deps=(), ks=ALL_KS, to=None):
    ns, n, nd, nk = len(srcs), len(lands), len(deps), len(ks)

    def body(*refs):
        ins, land = refs[:ns], refs[ns:ns + n]
        send, recv = refs[ns + n + nd], refs[ns + n + nd + 1]
        token = refs[-1]
        x, y, c, me = _mesh_pos()
        for j in range(n):
            for i, k in enumerate(ks):
                _, pid = _peer(x, y, c, k)
                dest, _ = _peer(x, y, c, k if to is None else to)
                src, dst = copy_refs(j, (land[j] if src_of[j] is None else ins[src_of[j]]), land[j], me, pid, i)
                _remote(src, dst, send.at[j * nk + i], recv.at[j * nk + i], dest).start()
        token[...] = jnp.zeros_like(token)

    outs = pl.pallas_call(
        body, name=name,
        out_shape=(pltpu.SemaphoreType.DMA((n * nk,)), pltpu.SemaphoreType.DMA((n * nk,)))
        + tuple(pltpu.HBM(a.shape, a.dtype) for a in srcs) + tuple(pltpu.HBM(a.shape, a.dtype) for a in lands)
        + (jax.ShapeDtypeStruct((8, 128), F32),),
        in_specs=(HBM,) * (ns + n) + (ANY,) * nd,
        out_specs=(SEMS, SEMS) + (HBM,) * (ns + n) + (pl.BlockSpec(memory_space=pltpu.VMEM),),
        input_output_aliases={i: 2 + i for i in range(ns + n)},
        compiler_params=pltpu.CompilerParams(has_side_effects=EFFECT),
    )(*[_hbm(a) for a in srcs], *[_hbm(a) for a in lands], *deps)
    return outs[0], outs[1], list(outs[2:2 + ns]), list(outs[2 + ns:2 + ns + n]), outs[-1]


def _split_wait(srcs, src_of, lands, send, recv, sem_rows, wait_refs, after, name, ks=ALL_KS):
    ns, n, nk = len(srcs), len(lands), len(ks)
    after = tuple(after) if isinstance(after, (tuple, list)) else (after,)

    def body(*refs):
        ins, land = refs[:ns], refs[ns:ns + n]
        send_ref, recv_ref = refs[ns + n], refs[ns + n + 1]
        x, y, c, me = _mesh_pos()
        for j in range(n):
            for i, k in enumerate(ks):
                peer, _ = _peer(x, y, c, k)
                src, dst = wait_refs(j, (land[j] if src_of[j] is None else ins[src_of[j]]), land[j])
                sem = sem_rows[j] * nk + i
                cp = _remote(src, dst, send_ref.at[sem], recv_ref.at[sem], peer)
                cp.wait_send()
                cp.wait_recv()

    outs = pl.pallas_call(
        body, name=name,
        out_shape=tuple(pltpu.HBM(a.shape, a.dtype) for a in srcs) + tuple(pltpu.HBM(a.shape, a.dtype) for a in lands),
        in_specs=(HBM,) * (ns + n) + (SEMS, SEMS) + (ANY,) * len(after),
        out_specs=(HBM,) * (ns + n),
        input_output_aliases={i: i for i in range(ns + n)},
        compiler_params=pltpu.CompilerParams(has_side_effects=EFFECT),
    )(*srcs, *lands, send, recv, *after)
    return list(outs[:ns]), list(outs[ns:])


def _ag_dtype(name):
    return F32 if name == "pv" else BF16


def _ag_align(name):
    return 8 if name == "pv" else 16


def _place_transposed(w, me, name):
    rows = w.shape[1]
    nblk = rows // 128

    def kern(me_ref, w_ref, o_ref):
        o_ref[...] = w_ref[...].T.astype(BF16)

    grid_spec = pltpu.PrefetchScalarGridSpec(
        num_scalar_prefetch=1, grid=(nblk,),
        in_specs=[pl.BlockSpec((D, 128), lambda i, me_ref: (0, i))],
        out_specs=pl.BlockSpec((128, D), lambda i, me_ref: (me_ref[0] * nblk + i, 0)))
    return pl.pallas_call(
        kern, grid_spec=grid_spec, out_shape=jax.ShapeDtypeStruct((NDEV * rows, D), BF16),
        compiler_params=_cparams(1), name=name)(me.reshape(1).astype(jnp.int32), w)


class _Comm:
    def __init__(self, params, make_shards, me, placed):
        self.me = me
        self.ag_land, self.ag_sems, self.ag_tokens, self.ag_passing = {}, {}, (), {}
        self.rs = []
        deps = ()
        for part, names in enumerate((AG_GROUPS[0], AG_ORDER[len(AG_GROUPS[0]):])):
            rows = [SEC_ROWS[n] for n in names]
            if part == 0:
                lands = [placed[n] for n in names]
            else:
                params, deps = lax.optimization_barrier((params, deps))
                shards = make_shards(*params)
                lands = [lax.dynamic_update_slice(lax.empty((NDEV * r, shards[n].shape[1]), _ag_dtype(n)),
                                                  shards[n].astype(_ag_dtype(n)), (_shard_pos(n, me), 0))
                         for n, r in zip(names, rows)]

            def copy_refs(j, src, land, me, pid, i, names=names, rows=rows):
                own = land.at[pl.ds(pl.multiple_of(_shard_pos(names[j], me), _ag_align(names[j])), rows[j])]
                return own, own

            send, recv, _, lands, token = _split_start([], [None] * len(names), lands, copy_refs, f"ag_start{part}",
                                                       deps=deps, ks=AG_KS1)
            deps = (token,)
            self.ag_tokens += (token,)
            for j, n in enumerate(names):
                self.ag_land[n] = lands[j]
                self.ag_sems[n] = (send, recv, j)

    def pass_on(self, group, after):
        names = AG_GROUPS[group]
        send, recv = self.ag_sems[names[0]][:2]
        idx = [self.ag_sems[n][2] for n in names]
        rows = [SEC_ROWS[n] for n in names]
        none = [None] * len(names)

        def wait_refs(j, src, land):
            return land.at[pl.ds(0, rows[j])], land.at[pl.ds(0, rows[j])]

        _, lands = _split_wait([], none, [self.ag_land[n] for n in names], send, recv, idx,
                               wait_refs, after, f"ag_wait{group}", ks=AG_KS1)

        def copy_refs(j, src, land, me, pid, i):
            theirs = land.at[pl.ds(pl.multiple_of(_shard_pos(names[j], pid), _ag_align(names[j])), rows[j])]
            return theirs, theirs

        send, recv, _, lands, token = _split_start([], none, lands, copy_refs, f"ag_pass{group}", ks=AG_KS2, to=1)
        self.ag_passing[group] = (send, recv, lands, wait_refs)
        return token

    def weights(self, group, after):
        names = AG_GROUPS[group]
        if group not in self.ag_passing:
            after = self.pass_on(group, after)
        send, recv, lands, wait_refs = self.ag_passing[group]
        _, lands = _split_wait([], [None] * len(names), lands, send, recv, list(range(len(names))), wait_refs, after,
                               f"ag_pass_wait{group}", ks=AG_KS2)
        return dict(zip(names, lands))

    def send_grads(self, group, gws):
        names = RS_GROUPS[group]
        rows = [SEC_ROWS[n] for n in names]
        grads = [gws[n] for n in names]
        me = self.me
        lands = [lax.dynamic_update_slice(
            lax.empty((NDEV, r, D), BF16),
            lax.dynamic_slice(g, (_shard_pos(n, me), 0), (r, D))[None], (me, 0, 0))
            for n, r, g in zip(names, rows, grads)]

        def copy_refs(j, src, land, me, pid, i):
            return src.at[pl.ds(pl.multiple_of(_shard_pos(names[j], pid), 16), rows[j])], land.at[me]

        send, recv, srcs, lands, token = _split_start(grads, list(range(len(names))), lands, copy_refs,
                                                      f"rs_start{group}")
        self.rs.append((names, rows, send, recv, srcs, lands, ALL_KS))
        return token

    def send_grads_pairwise(self, gws):
        names = RS_GROUPS[-1]
        rows = [SEC_ROWS[n] for n in names]
        grads = [gws[n] for n in names]
        idx = list(range(len(names)))
        lands = [lax.empty((len(RS_KS_PAIR), r, D), BF16) for r in rows]

        def copy_refs(j, src, land, me, pid, i):
            return src.at[pl.ds(pl.multiple_of(_shard_pos(names[j], pid), 16), rows[j])], land.at[i]

        send, recv, srcs, lands, token = _split_start(grads, idx, lands, copy_refs, "rs_pair_start",
                                                      ks=RS_KS_PAIR, to=1)
        self.pair = (names, rows, send, recv, srcs, lands)
        return token

    def pass_grads(self, after):
        names, rows, send, recv, srcs, lands = self.pair
        idx = list(range(len(names)))
        me = self.me

        def wait_refs(j, src, land):
            return src.at[pl.ds(0, rows[j])], land.at[0]

        srcs, lands = _split_wait(srcs, idx, lands, send, recv, idx, wait_refs, after, "rs_pair_wait", ks=RS_KS_PAIR)
        sums = [_pair_sum(g, got, me, f"rs_pair_sum_{n}") for n, g, got in zip(names, srcs, lands)]
        lands = [lax.dynamic_update_slice(lax.empty(p.shape, BF16), p[0:1], (0, 0, 0)) for p in sums]

        def copy_refs(j, src, land, me, pid, i):
            return src.at[i + 1], land.at[i + 1]

        send, recv, sums, lands, token = _split_start(sums, idx, lands, copy_refs, f"rs_start{len(RS_GROUPS) - 1}",
                                                      ks=RS_KS_CHIPS)
        self.rs.append((names, rows, send, recv, sums, lands, RS_KS_CHIPS))
        return token

    def received(self, group, after):
        names, rows, send, recv, srcs, lands, ks = self.rs[group]
        whole = srcs[0].ndim == 2

        def wait_refs(j, src, land):
            return (src.at[pl.ds(0, rows[j])] if whole else src.at[0]), land.at[0]

        _, lands = _split_wait(srcs, list(range(len(names))), lands, send, recv, list(range(len(names))), wait_refs,
                               after, f"rs_wait{group}", ks=ks)
        return dict(zip(names, lands))


def _pair_sum(grad, got, me, name):
    n, rows, _ = got.shape
    tr = 384 if rows % 384 == 0 else rows
    nt = rows // tr

    def kern(me_ref, g_ref, b_ref, o_ref):
        o_ref[0] = (g_ref[...].astype(F32) + b_ref[0].astype(F32)).astype(BF16)

    blk = pl.BlockSpec((1, tr, D), lambda i, t, me_ref: (i, t, 0))
    grid_spec = pltpu.PrefetchScalarGridSpec(
        num_scalar_prefetch=1, grid=(n, nt),
        in_specs=[pl.BlockSpec((tr, D), lambda i, t, me_ref: (jnp.bitwise_xor(me_ref[0], 2 * i) * nt + t, 0)), blk],
        out_specs=blk)
    return pl.pallas_call(
        kern, grid_spec=grid_spec, out_shape=jax.ShapeDtypeStruct(got.shape, BF16),
        compiler_params=_cparams(2), name=name)(me.reshape(1).astype(jnp.int32), grad, got)


def _sum_contributions(r_ref):
    g = r_ref[0].astype(F32)
    for slot in range(1, r_ref.shape[0]):
        g = g + r_ref[slot].astype(F32)
    return g


def _adam_math(g, w, m, v):
    c1 = 1.0 / (1.0 - ADAM_B1 ** ADAM_STEP)
    c2 = 1.0 / (1.0 - ADAM_B2 ** ADAM_STEP)
    mn = ADAM_B1 * m + (1.0 - ADAM_B1) * g
    vn = ADAM_B2 * v + (1.0 - ADAM_B2) * (g * g)
    return -ADAM_LR * ((mn * c1) / (jnp.sqrt(vn * c2) + ADAM_EPS) + ADAM_WD * w), mn, vn


def _adamw(R, w, m, v, *, tr, name, layer=None, prev=None):
    rows, C = w.shape[-2:]
    nprev = 0 if prev is None else 4

    def kern(r_ref, w_ref, m_ref, v_ref, *rest):
        g_out, d_out, m_out, v_out = rest[nprev:]
        g = _sum_contributions(r_ref)
        g_out[...] = g
        d_out[...], m_out[...], v_out[...] = _adam_math(g, w_ref[...], m_ref[...], v_ref[...])

    if layer is None:
        tile = pl.BlockSpec((tr, C), lambda i: (i, 0))
    else:
        tile = pl.BlockSpec((None, tr, C), lambda i: (layer, i, 0))
    shp = jax.ShapeDtypeStruct(w.shape, F32)
    return pl.pallas_call(
        kern, grid=(rows // tr,),
        in_specs=[pl.BlockSpec((R.shape[0], tr, C), lambda i: (0, i, 0)), tile, tile, tile]
        + [pl.BlockSpec(memory_space=pl.ANY)] * nprev,
        out_specs=[tile] * 4, out_shape=[shp] * 4,
        input_output_aliases={4 + k: k for k in range(nprev)},
        compiler_params=_cparams(1), name=name)(R, w, m, v, *(prev or ()))


def _adamw_pool_group(R, w, m, v):
    rows = SEC_ROWS["pg"]

    def kern(r_ref, w_ref, m_ref, v_ref, g_out, d_out, m_out, v_out):
        g = _sum_contributions(r_ref)
        g_out[0] = g
        d_out[0], m_out[0], v_out[0] = _adam_math(g, w_ref[0], m_ref[0], v_ref[0])

    blk = pl.BlockSpec((1, rows, PGD), lambda i: (i, 0, 0))
    shp = jax.ShapeDtypeStruct((POOL_G, rows, PGD), F32)
    return pl.pallas_call(
        kern, grid=(POOL_G,),
        in_specs=[pl.BlockSpec((NDEV, rows, PGD), lambda i: (0, 0, i)), blk, blk, blk],
        out_specs=[blk] * 4, out_shape=[shp] * 4, compiler_params=_cparams(1), name="adamw_pg")(R, w, m, v)


def _adamw_transposed(R, w, m, v, name):
    rows = R.shape[1]
    tr = 128

    def kern(r_ref, w_ref, m_ref, v_ref, g_out, d_out, m_out, v_out):
        g = _sum_contributions(r_ref).T
        g_out[...] = g
        d_out[...], m_out[...], v_out[...] = _adam_math(g, w_ref[...], m_ref[...], v_ref[...])

    tile = pl.BlockSpec((D, tr), lambda i: (0, i))
    shp = jax.ShapeDtypeStruct((D, rows), F32)
    return pl.pallas_call(
        kern, grid=(rows // tr,),
        in_specs=[pl.BlockSpec((R.shape[0], tr, D), lambda i: (0, i, 0)), tile, tile, tile],
        out_specs=[tile] * 4, out_shape=[shp] * 4, compiler_params=_cparams(1), name=name)(R, w, m, v)


def _pack_sections(w_qkv, w_attn_out, w_pool_in, w_pool_group, w_ffn_gate_up, w_ffn_down):
    pg = w_pool_group[0].transpose(1, 0, 2).reshape(SEC_ROWS["pg"], D)
    return {"qkv": w_qkv[0].T, "wo": w_attn_out[0], "wpi": w_pool_in[0], "gu0": w_ffn_gate_up[0].T,
            "gu1": w_ffn_gate_up[1].T, "d0": w_ffn_down[0], "d1": w_ffn_down[1], "pg": pg}


def _vec_pack(attn_norm, ffn_norm, final_norm, pool_norm_sh, pool_scale_sh, me):
    def place(sh):
        return lax.dynamic_update_slice(jnp.zeros((1, D), F32), sh, (0, me * 128))
    return jnp.concatenate([attn_norm, ffn_norm, final_norm.reshape(1, D), place(pool_norm_sh),
                            place(pool_scale_sh), jnp.zeros((2, D), F32)], axis=0)


def _vec_unpack(p, me):
    def take(r):
        return lax.dynamic_slice(p[r:r + 1], (0, me * 128), (1, 128))
    return p[0:1], p[1:3], p[3], take(4), take(5)


def kernel(x, attn_norm, w_qkv, w_attn_out, pool_norm, w_pool_in, w_pool_group, pool_scale, ffn_norm, w_ffn_gate_up, w_ffn_down, final_norm, loss_target, m_attn_norm, m_w_qkv, m_w_attn_out, m_pool_norm, m_w_pool_in, m_w_pool_group, m_pool_scale, m_ffn_norm, m_w_ffn_gate_up, m_w_ffn_down, m_final_norm, v_attn_norm, v_w_qkv, v_w_attn_out, v_pool_norm, v_w_pool_in, v_w_pool_group, v_pool_scale, v_ffn_norm, v_w_ffn_gate_up, v_w_ffn_down, v_final_norm):
    me = 4 * lax.axis_index("x") + 2 * lax.axis_index("y") + lax.axis_index("c")

    def make_shards(wq, wo, wpi, wpg, wgu, wd, pn, ps):
        shards = _pack_sections(wq, wo, wpi, wpg, wgu, wd)
        shards["pv"] = jnp.concatenate([pn, ps, jnp.zeros((6, 128), F32)], axis=0)
        return shards

    comm = _Comm((w_qkv, w_attn_out, w_pool_in, w_pool_group, w_ffn_gate_up, w_ffn_down, pool_norm, pool_scale),
                 make_shards, me, placed={"qkv": _place_transposed(w_qkv[0], me, "place_qkv")})

    grad_x, vec = _local_step(x[0], loss_target[0], comm, attn_norm, ffn_norm, final_norm)

    small = ((attn_norm, ffn_norm, final_norm, pool_norm, pool_scale),
             (m_attn_norm, m_ffn_norm, m_final_norm, m_pool_norm, m_pool_scale),
             (v_attn_norm, v_ffn_norm, v_final_norm, v_pool_norm, v_pool_scale))
    small, grad_x = lax.optimization_barrier((small, grad_x))
    vw, vm, vv = (_vec_pack(*s, me) for s in small)

    gu_t = [jnp.swapaxes(a, 1, 2) for a in (w_ffn_gate_up, m_w_ffn_gate_up, v_w_ffn_gate_up)]
    res = {}
    gu_res, d_res = None, None
    vec_out = None
    vec_land = lax.dynamic_update_slice(lax.empty((NDEV, 8, D), F32), vec[None], (me, 0, 0))
    vec_sems = _split_start([vec], [0], [vec_land], lambda j, src, land, me_, pid, i: (src, land.at[me_]),
                            "vec_start")
    after = (grad_x, vec_sems[4])
    for group in range(len(RS_GROUPS)):
        if group == len(RS_GROUPS) - 1:
            _, (VR,) = _split_wait(vec_sems[2], [0], vec_sems[3], vec_sems[0], vec_sems[1], [0],
                                   lambda j, src, land: (src, land.at[0]), after, "vec_wait")
            vec_out = _adamw(VR, vw, vm, vv, tr=8, name="adamw_vec")
            after = vec_out[0]
        for n, R in comm.received(group, after).items():
            if n in ("d0", "d1"):
                d_res = _adamw(R, w_ffn_down, m_w_ffn_down, v_w_ffn_down, tr=352, name=f"adamw_{n}",
                               layer=int(n[1]), prev=d_res)
                after = d_res[0]
            elif n in ("gu0", "gu1"):
                gu_res = _adamw(R, *gu_t, tr=352, name=f"adamw_{n}", layer=int(n[2]), prev=gu_res)
                after = gu_res[0]
            elif n == "pg":
                out = _adamw_pool_group(R, w_pool_group[0], m_w_pool_group[0], v_w_pool_group[0])
                res["pg"] = tuple(a[None] for a in out)
                after = out[0]
            elif n in ("wo", "wpi"):
                w, m, v = ((w_attn_out, m_w_attn_out, v_w_attn_out) if n == "wo"
                           else (w_pool_in, m_w_pool_in, v_w_pool_in))
                res[n] = _adamw(R, w[0], m[0], v[0], tr=128, name=f"adamw_{n}")
                res[n] = tuple(a[None] for a in res[n])
                after = res[n][0]
            else:
                out = _adamw_transposed(R, w_qkv[0], m_w_qkv[0], v_w_qkv[0], "adamw_qkv")
                res["qkv"] = tuple(a[None] for a in out)
                after = out[0]
    res["gu"] = tuple(jnp.swapaxes(a, 1, 2) for a in gu_res)
    res["d"] = tuple(d_res)

    outs = []
    for kind in range(4):
        an, fn, fin, pn, ps = _vec_unpack(vec_out[kind], me)
        outs.append((an, res["qkv"][kind], res["wo"][kind], pn, res["wpi"][kind], res["pg"][kind], ps, fn,
                     res["gu"][kind], res["d"][kind], fin))
    loss = 0.5 * jnp.sum(vec_out[0][6]) / D
    return (loss, grad_x[None]) + outs[0] + outs[1] + outs[2] + outs[3]
```

```python
import jax
import jax.numpy as jnp
from jax import lax
from jax.experimental import pallas as pl
from jax.experimental.pallas import tpu as pltpu

F32 = jnp.float32
BF16 = jnp.bfloat16

D = 1024
NDEV = 8
HEADS = 8
HD = 128
QB = 128
NGROUPS = 3
DILS = (1, 4, 16)
DFF = 2816
HCH = 1408
POOL_G = 4
PGD = 256
RMS_EPS = 1e-6
NEG = -1e30

ADAM_LR = 0.001
ADAM_B1 = 0.9
ADAM_B2 = 0.999
ADAM_EPS = 1e-08
ADAM_WD = 0.01
ADAM_STEP = 10

VMEM_LIMIT = 52 * 1024 * 1024

SECTIONS = (("qkv", 1152), ("wo", 128), ("wpi", 128), ("gu0", 704), ("gu1", 704),
            ("d0", 352), ("d1", 352), ("pg", 32))
SEC_ROWS = dict(SECTIONS)
SEC_ROWS["pv"] = 8


def _cparams(n_grid):
    return pltpu.CompilerParams(dimension_semantics=("arbitrary",) * n_grid, vmem_limit_bytes=VMEM_LIMIT)


def _shard_pos(name, dev):
    n = SEC_ROWS[name]
    if name in ("gu0", "gu1"):
        return ((dev % 4) // 2) * (2 * HCH) + (dev // 4) * HCH + (dev % 2) * n
    return dev * n


def _mm(a, b, *, mode, M, N, K, tm, tn, tk, out_dtype, name, a_off=(0, 0), b_off=(0, 0), res=None,
        out_rows=None, out_off=0, out_prev=None, deps=()):
    nm, nn, nk = M // tm, N // tn, K // tk
    assert nm * tm == M and nn * tn == N and nk * tk == K
    if mode == "nn":
        a_bs, b_bs = (tm, tk), (tk, tn)
        a_ix = lambda i, j, k: (i, k)
        b_ix = lambda i, j, k: (k, j)
        dims = (((1,), (0,)), ((), ()))
    elif mode == "nt":
        a_bs, b_bs = (tm, tk), (tn, tk)
        a_ix = lambda i, j, k: (i, k)
        b_ix = lambda i, j, k: (j, k)
        dims = (((1,), (1,)), ((), ()))
    else:
        a_bs, b_bs = (tk, tm), (tk, tn)
        a_ix = lambda i, j, k: (k, i)
        b_ix = lambda i, j, k: (k, j)
        dims = (((0,), (0,)), ((), ()))

    def spec(bs, ix, off):
        def im(i, j, k):
            r, c = ix(i, j, k)
            return (r + off[0], c + off[1])
        return pl.BlockSpec(bs, im)

    in_specs = [spec(a_bs, a_ix, a_off), spec(b_bs, b_ix, b_off)]
    args = [a, b]
    if res is not None:
        in_specs.append(pl.BlockSpec((tm, tn), lambda i, j, k: (i, j)))
        args.append(res)
    out_shape = jax.ShapeDtypeStruct((M if out_rows is None else out_rows, N), out_dtype)
    out_spec = pl.BlockSpec((tm, tn), lambda i, j, k: (i + out_off, j))
    has_res = res is not None
    extra = list(deps) + ([out_prev] if out_prev is not None else [])
    for dep in extra:
        in_specs.append(pl.BlockSpec(memory_space=pl.ANY))
        args.append(dep)
    o_pos = 2 + int(has_res) + len(extra)
    aliases = {len(args) - 1: 0} if out_prev is not None else {}

    def kern(*refs):
        a_ref, b_ref = refs[0], refs[1]
        res_ref = refs[2] if has_res else None
        o_ref = refs[o_pos]
        av = a_ref[...]
        bv = b_ref[...]
        if av.dtype != BF16:
            av = av.astype(BF16)
        if bv.dtype != BF16:
            bv = bv.astype(BF16)
        part = lax.dot_general(av, bv, dims, preferred_element_type=F32)

        def write(val):
            if has_res:
                val = val + res_ref[...]
            o_ref[...] = val.astype(out_dtype)

        if nk == 1:
            write(part)
        else:
            acc_ref = refs[-1]
            k = pl.program_id(2)

            @pl.when(k == 0)
            def _():
                acc_ref[...] = part

            @pl.when(k > 0)
            def _():
                acc_ref[...] += part

            @pl.when(k == nk - 1)
            def _():
                write(acc_ref[...])

    scratch = [pltpu.VMEM((tm, tn), F32)] if nk > 1 else []
    return pl.pallas_call(
        kern, grid=(nm, nn, nk), in_specs=in_specs, out_specs=out_spec, out_shape=out_shape,
        scratch_shapes=scratch, input_output_aliases=aliases, compiler_params=_cparams(3), name=name)(*args)


def _mm_rms_bwd(a, b, x, g, dres, *, mode, M, K, tm, name, b_off=(0, 0), deps=(), folded=()):
    nd, nf = len(deps), len(folded)
    b_bs = (K, D) if mode == "nn" else (D, K)
    dims = (((1,), (0,)), ((), ())) if mode == "nn" else (((1,), (1,)), ((), ()))

    def kern(a_ref, b_ref, x_ref, g_ref, dres_ref, *rest):
        f_refs = rest[:nf]
        dx_ref, dg_ref = rest[nf + nd:nf + nd + 2]
        i = pl.program_id(0)
        av = a_ref[...]
        if av.dtype != BF16:
            av = av.astype(BF16)
        dhv = lax.dot_general(av, b_ref[...], dims, preferred_element_type=F32)
        if nf:
            acc_ref = rest[-1]
            _chunks_put(acc_ref, dhv)
            for f_ref in f_refs:
                dil = f_ref.shape[0]
                for res in range(dil):
                    _chunks_add_rows(acc_ref, f_ref[res], res, tm // dil, dil, True)
            dhv = _chunks_get(acc_ref)
        xv = x_ref[...]
        r = lax.rsqrt(jnp.mean(xv * xv, axis=-1, keepdims=True) + RMS_EPS)
        xhat = xv * r
        gy = dhv * g_ref[...]
        dx_ref[...] = dres_ref[...] + r * (gy - xhat * jnp.mean(gy * xhat, axis=-1, keepdims=True))
        part = jnp.sum(dhv * xhat, axis=0, keepdims=True)

        @pl.when(i == 0)
        def _():
            dg_ref[...] = part

        @pl.when(i > 0)
        def _():
            dg_ref[...] += part

    row = pl.BlockSpec((tm, D), lambda i: (i, 0))
    vec = pl.BlockSpec((1, D), lambda i: (0, 0))
    return pl.pallas_call(
        kern, grid=(M // tm,),
        in_specs=[pl.BlockSpec((tm, K), lambda i: (i, 0)),
                  pl.BlockSpec(b_bs, lambda i: b_off, pipeline_mode=pl.Buffered(1)), row, vec, row]
        + [pl.BlockSpec((f.shape[0], tm // f.shape[0], D), lambda i: (0, i, 0)) for f in folded]
        + [pl.BlockSpec(memory_space=pl.ANY)] * nd,
        out_specs=[row, vec],
        out_shape=[jax.ShapeDtypeStruct((M, D), F32), jax.ShapeDtypeStruct((1, D), F32)],
        scratch_shapes=[pltpu.VMEM((D // 128, tm, 128), F32)] if nf else [],
        compiler_params=_cparams(1), name=name)(a, b, x, g, dres, *folded, *deps)


def _norm_tail(xv, gv, rest, head):
    r = lax.rsqrt(jnp.mean(xv * xv, axis=-1, keepdims=True) + RMS_EPS)
    xhat = xv * r
    if not head:
        xo_ref, h_ref = rest
        xo_ref[...] = xv
        h_ref[...] = (xhat * gv).astype(BF16)
        return
    t_ref, dx_ref, dg_ref, ls_ref = rest
    i = pl.program_id(0)
    e = xhat * gv - t_ref[...]
    dy = e * (1.0 / D)
    gy = dy * gv
    dx_ref[...] = r * (gy - xhat * jnp.mean(gy * xhat, axis=-1, keepdims=True))
    dgp = jnp.sum(dy * xhat, axis=0, keepdims=True)
    lsp = jnp.sum(e * e, axis=0, keepdims=True)

    @pl.when(i == 0)
    def _():
        dg_ref[...] = dgp
        ls_ref[...] = lsp

    @pl.when(i > 0)
    def _():
        dg_ref[...] += dgp
        ls_ref[...] += lsp


def _ffn_fwd(h, wgu, wd, res, g, *, name, tgt=None):
    S = h.shape[0]
    tm = 256
    nj = DFF // HCH
    head = tgt is not None

    def kern(h_ref, wgu_ref, wd_ref, res_ref, g_ref, *rest):
        t_refs, (gu_ref, act_ref), tail = rest[:int(head)], rest[int(head):int(head) + 2], rest[int(head) + 2:]
        hv = h_ref[...]
        for j in range(nj):
            gu = lax.dot_general(hv, wgu_ref[2 * HCH * j:2 * HCH * (j + 1), :], (((1,), (1,)), ((), ())),
                                 preferred_element_type=F32)
            gu_ref[:, 2 * HCH * j:2 * HCH * (j + 1)] = gu.astype(BF16)
            gate = gu[:, :HCH]
            act_ref[:, HCH * j:HCH * (j + 1)] = (gate * jax.nn.sigmoid(gate) * gu[:, HCH:]).astype(BF16)
        xv = res_ref[...] + jnp.dot(act_ref[...], wd_ref[...], preferred_element_type=F32)
        _norm_tail(xv, g_ref[...], tuple(t_refs) + tuple(tail), head)

    row = pl.BlockSpec((tm, D), lambda i: (i, 0))
    vec = pl.BlockSpec((1, D), lambda i: (0, 0))
    in_specs = [row, pl.BlockSpec((2 * DFF, D), lambda i: (0, 0), pipeline_mode=pl.Buffered(1)),
                pl.BlockSpec((DFF, D), lambda i: (0, 0), pipeline_mode=pl.Buffered(1)), row, vec]
    out_specs = [pl.BlockSpec((tm, 2 * DFF), lambda i: (i, 0)), pl.BlockSpec((tm, DFF), lambda i: (i, 0))]
    out_shape = [jax.ShapeDtypeStruct((S, 2 * DFF), BF16), jax.ShapeDtypeStruct((S, DFF), BF16)]
    args = [h, wgu, wd, res, g]
    if head:
        in_specs, args = in_specs + [row], args + [tgt]
        out_specs += [row, vec, vec]
        out_shape += [jax.ShapeDtypeStruct((S, D), F32), jax.ShapeDtypeStruct((1, D), F32),
                      jax.ShapeDtypeStruct((1, D), F32)]
    else:
        out_specs += [row, row]
        out_shape += [jax.ShapeDtypeStruct((S, D), F32), jax.ShapeDtypeStruct((S, D), BF16)]
    outs = pl.pallas_call(kern, grid=(S // tm,), in_specs=in_specs, out_specs=out_specs, out_shape=out_shape,
                          compiler_params=_cparams(1), name=name)(*args)
    return outs[0], outs[1], tuple(outs[2:])


def _mm_res_norm(a, b, res, g, *, K, tm, name, b_off=(0, 0), tgt=None):
    M = a.shape[0]
    head = tgt is not None

    def kern(a_ref, b_ref, res_ref, g_ref, *rest):
        xv = res_ref[...] + jnp.dot(a_ref[...], b_ref[...], preferred_element_type=F32)
        _norm_tail(xv, g_ref[...], rest, head)

    row = pl.BlockSpec((tm, D), lambda i: (i, 0))
    vec = pl.BlockSpec((1, D), lambda i: (0, 0))
    in_specs = [pl.BlockSpec((tm, K), lambda i: (i, 0)),
                pl.BlockSpec((K, D), lambda i: b_off, pipeline_mode=pl.Buffered(1)), row, vec]
    if head:
        return pl.pallas_call(
            kern, grid=(M // tm,), in_specs=in_specs + [row], out_specs=[row, vec, vec],
            out_shape=[jax.ShapeDtypeStruct((M, D), F32), jax.ShapeDtypeStruct((1, D), F32),
                       jax.ShapeDtypeStruct((1, D), F32)],
            compiler_params=_cparams(1), name=name)(a, b, res, g, tgt)
    return pl.pallas_call(
        kern, grid=(M // tm,), in_specs=in_specs, out_specs=[row, row],
        out_shape=[jax.ShapeDtypeStruct((M, D), F32), jax.ShapeDtypeStruct((M, D), BF16)],
        compiler_params=_cparams(1), name=name)(a, b, res, g)


def _rms_fwd(x, g, name, deps=()):
    S = x.shape[0]
    tr = 512

    def kern(x_ref, g_ref, *rest):
        h_ref = rest[-1]
        xv = x_ref[...]
        r = lax.rsqrt(jnp.mean(xv * xv, axis=-1, keepdims=True) + RMS_EPS)
        h_ref[...] = (xv * r * g_ref[...]).astype(BF16)

    return pl.pallas_call(
        kern, grid=(S // tr,),
        in_specs=[pl.BlockSpec((tr, D), lambda i: (i, 0)), pl.BlockSpec((1, D), lambda i: (0, 0))]
        + [pl.BlockSpec(memory_space=pl.ANY)] * len(deps),
        out_specs=pl.BlockSpec((tr, D), lambda i: (i, 0)),
        out_shape=jax.ShapeDtypeStruct((S, D), BF16), compiler_params=_cparams(1), name=name)(x, g, *deps)


def _chunks_put(scr, val):
    for c in range(scr.shape[0]):
        scr[c] = val[:, c * 128:(c + 1) * 128]


def _chunks_get(scr):
    return jnp.concatenate([scr[c] for c in range(scr.shape[0])], axis=1)


def _chunks_rows(scr, r, n, dil):
    return jnp.concatenate([scr.at[c][pl.ds(r, n, stride=dil), :] for c in range(scr.shape[0])], axis=1)


def _chunks_add_rows(scr, val, r, n, dil, accumulate):
    for c in range(scr.shape[0]):
        rows = pl.ds(r, n, stride=dil)
        piece = val[:, c * 128:(c + 1) * 128]
        tile = scr.at[c]
        tile[rows, :] = tile[rows, :] + piece if accumulate else piece


def _rms_fwd_folded(x, g, name, deps=()):
    S = x.shape[0]
    tr = 512
    dils = DILS[1:]

    def kern(x_ref, g_ref, *rest):
        outs, scr = rest[len(deps):-1], rest[-1]
        xv = x_ref[...]
        r = lax.rsqrt(jnp.mean(xv * xv, axis=-1, keepdims=True) + RMS_EPS)
        h = (xv * r * g_ref[...]).astype(BF16)
        outs[0][...] = h
        _chunks_put(scr, h.astype(F32))
        for o_ref, dil in zip(outs[1:], dils):
            for res in range(dil):
                o_ref[res] = _chunks_rows(scr, res, tr // dil, dil).astype(BF16)

    return pl.pallas_call(
        kern, grid=(S // tr,),
        in_specs=[pl.BlockSpec((tr, D), lambda i: (i, 0)), pl.BlockSpec((1, D), lambda i: (0, 0))]
        + [pl.BlockSpec(memory_space=pl.ANY)] * len(deps),
        out_specs=[pl.BlockSpec((tr, D), lambda i: (i, 0))]
        + [pl.BlockSpec((dil, tr // dil, D), lambda i: (0, i, 0)) for dil in dils],
        out_shape=[jax.ShapeDtypeStruct((S, D), BF16)]
        + [jax.ShapeDtypeStruct((dil, S // dil, D), BF16) for dil in dils],
        scratch_shapes=[pltpu.VMEM((D // 128, tr, 128), F32)],
        compiler_params=_cparams(1), name=name)(x, g, *deps)


def _ffn_bwd(dxo, wd, wgu, gu, xin, gain, name):
    S = dxo.shape[0]
    tm = 256
    nj = DFF // HCH

    def kern(dx_ref, wd_ref, wgu_ref, gu_ref, x_ref, g_ref, dgu_ref, dxin_ref, dg_ref):
        i = pl.program_id(0)
        dxv = dx_ref[...]
        dxb = dxv.astype(BF16)
        for j in range(nj):
            c0 = 2 * HCH * j
            dact = lax.dot_general(dxb, wd_ref[HCH * j:HCH * (j + 1), :], (((1,), (1,)), ((), ())),
                                   preferred_element_type=F32)
            gate = gu_ref[:, c0:c0 + HCH].astype(F32)
            up = gu_ref[:, c0 + HCH:c0 + 2 * HCH].astype(F32)
            sig = jax.nn.sigmoid(gate)
            silu = gate * sig
            dgu_ref[:, c0:c0 + HCH] = (dact * up * (sig * (1.0 + gate * (1.0 - sig)))).astype(BF16)
            dgu_ref[:, c0 + HCH:c0 + 2 * HCH] = (dact * silu).astype(BF16)
        dhv = jnp.dot(dgu_ref[...], wgu_ref[...], preferred_element_type=F32)
        xv = x_ref[...]
        r = lax.rsqrt(jnp.mean(xv * xv, axis=-1, keepdims=True) + RMS_EPS)
        xhat = xv * r
        gy = dhv * g_ref[...]
        dxin_ref[...] = dxv + r * (gy - xhat * jnp.mean(gy * xhat, axis=-1, keepdims=True))
        part = jnp.sum(dhv * xhat, axis=0, keepdims=True)

        @pl.when(i == 0)
        def _():
            dg_ref[...] = part

        @pl.when(i > 0)
        def _():
            dg_ref[...] += part

    row = pl.BlockSpec((tm, D), lambda i: (i, 0))
    wide = pl.BlockSpec((tm, 2 * DFF), lambda i: (i, 0))
    vec = pl.BlockSpec((1, D), lambda i: (0, 0))
    return pl.pallas_call(
        kern, grid=(S // tm,),
        in_specs=[row, pl.BlockSpec((DFF, D), lambda i: (0, 0), pipeline_mode=pl.Buffered(1)),
                  pl.BlockSpec((2 * DFF, D), lambda i: (0, 0), pipeline_mode=pl.Buffered(1)), wide, row, vec],
        out_specs=[wide, row, vec],
        out_shape=[jax.ShapeDtypeStruct((S, 2 * DFF), BF16), jax.ShapeDtypeStruct((S, D), F32),
                   jax.ShapeDtypeStruct((1, D), F32)],
        compiler_params=_cparams(1), name=name)(dxo, wd, wgu, gu, xin, gain)


def _window(val, grp, backward):
    S = val.shape[0]
    row = lax.broadcasted_iota(jnp.int32, val.shape, 0)
    cnt = jnp.minimum(row + 1, 2 << grp).astype(F32)
    s = val / cnt if backward else val
    for k in (1, 2, 4, 8)[:grp + 1]:
        if backward:
            sh = jnp.where(row < S - k, pltpu.roll(s, S - k, 0), 0.0)
        else:
            sh = jnp.where(row >= k, pltpu.roll(s, k, 0), 0.0)
        s = s + sh
    return s - val if backward else s / cnt - val


def _pool_in_window(h, wpi):
    S = h.shape[0]

    def kern(h_ref, w_ref, o_ref):
        g = pl.program_id(0)
        u = jnp.dot(h_ref[...], w_ref[...], preferred_element_type=F32)
        for grp in range(POOL_G):
            @pl.when(g == grp)
            def _(grp=grp):
                o_ref[...] = _window(u, grp, False).astype(BF16)

    return pl.pallas_call(
        kern, grid=(POOL_G,),
        in_specs=[pl.BlockSpec((S, D), lambda g: (0, 0), pipeline_mode=pl.Buffered(1)),
                  pl.BlockSpec((D, PGD), lambda g: (0, g))],
        out_specs=pl.BlockSpec((S, PGD), lambda g: (0, g)),
        out_shape=jax.ShapeDtypeStruct((S, D), BF16), compiler_params=_cparams(1), name="pool_in")(h, wpi)


def _pool_out(yd, G, scale, xres):
    S = yd.shape[0]
    tm = min(S, 4096)

    def kern(y_ref, w_ref, s_ref, x_ref, o_ref):
        z = jnp.dot(y_ref[...], w_ref[...], preferred_element_type=F32)
        o_ref[...] = x_ref[...] + z * s_ref[...]

    tile = pl.BlockSpec((tm, PGD), lambda i, g: (i, g))
    return pl.pallas_call(
        kern, grid=(S // tm, POOL_G),
        in_specs=[tile, pl.BlockSpec((PGD, PGD), lambda i, g: (0, g)),
                  pl.BlockSpec((1, PGD), lambda i, g: (0, g)), tile],
        out_specs=tile, out_shape=jax.ShapeDtypeStruct((S, D), F32),
        compiler_params=_cparams(2), name="pool_out")(yd, G, scale, xres)


def _pool_out_bwd(dz, yd, G, scale, deps=()):
    S = yd.shape[0]
    nd = len(deps)

    def kern(dz_ref, y_ref, w_ref, s_ref, *rest):
        du_ref, ds_ref, dw_ref = rest[nd:]
        g = pl.program_id(0)
        dzv = dz_ref[...]
        yv = y_ref[...]
        wv = w_ref[...]
        zraw = jnp.dot(yv, wv, preferred_element_type=F32)
        ds_ref[...] = jnp.sum(dzv * zraw, axis=0, keepdims=True)
        dzr = (dzv * s_ref[...]).astype(BF16)
        dw_ref[...] = lax.dot_general(yv, dzr, (((0,), (0,)), ((), ())), preferred_element_type=F32).astype(BF16)
        dyd = lax.dot_general(dzr, wv, (((1,), (1,)), ((), ())), preferred_element_type=F32)
        for grp in range(POOL_G):
            @pl.when(g == grp)
            def _(grp=grp):
                du_ref[...] = _window(dyd, grp, True).astype(BF16)

    tile = pl.BlockSpec((S, PGD), lambda g: (0, g))
    return pl.pallas_call(
        kern, grid=(POOL_G,),
        in_specs=[tile, tile, pl.BlockSpec((PGD, PGD), lambda g: (0, g)),
                  pl.BlockSpec((1, PGD), lambda g: (0, g))] + [pl.BlockSpec(memory_space=pl.ANY)] * nd,
        out_specs=[tile, pl.BlockSpec((1, PGD), lambda g: (0, g)), pl.BlockSpec((PGD, PGD), lambda g: (0, g))],
        out_shape=[jax.ShapeDtypeStruct((S, D), BF16), jax.ShapeDtypeStruct((1, D), F32),
                   jax.ShapeDtypeStruct((PGD, D), BF16)],
        compiler_params=_cparams(1), name="pool_out_bwd")(dz, yd, G, scale, *deps)


def _bias_table():
    qi = jnp.arange(QB)[:, None]
    ki = jnp.arange(2 * QB)[None, :]
    delta = QB + qi - ki
    inband = (delta >= 0) & (delta <= QB)
    n = NGROUPS * HEADS
    slopes = jnp.exp2(-8.0 * jnp.arange(1, n + 1, dtype=F32) / n).reshape(NGROUPS, HEADS)
    dil = jnp.asarray(DILS, F32)
    bias = -slopes[:, :, None, None] * (delta.astype(F32)[None, None] * dil[:, None, None, None])
    return jnp.where(inband[None, None], bias, NEG)


def _attn_fwd(qkv_f, bias, nb, name):
    S = qkv_f.shape[0]
    nblk = S // QB
    scale = HD ** -0.5

    def kern(q_ref, k2_ref, kp_ref, v2_ref, vp_ref, b_ref, o_ref, l_ref, s_scr, p_scr, r_scr):
        s_id = pl.program_id(0)
        col = lax.broadcasted_iota(jnp.int32, (QB, 2 * QB), 1)
        lane = lax.broadcasted_iota(jnp.int32, (QB, HD), 1)

        def keys(sub, cur2_ref, prev_ref, sl):
            if sub:
                return cur2_ref[:, sl]
            return jnp.concatenate([prev_ref[:, sl], cur2_ref[0:QB, sl]], axis=0)

        for sub in range(2):
            for h in range(HEADS):
                sl = slice(h * HD, (h + 1) * HD)
                s_scr[sub * HEADS + h] = lax.dot_general(
                    q_ref[sub * QB:(sub + 1) * QB, sl], keys(sub, k2_ref, kp_ref, sl), (((1,), (1,)), ((), ())),
                    preferred_element_type=F32)
        for sub in range(2):
            has_prev = jnp.bitwise_and(2 * s_id + sub, nb - 1) != 0
            dead = jnp.logical_and(col < QB, jnp.logical_not(has_prev))
            lse_all = jnp.zeros((QB, HD), F32)
            for h in range(HEADS):
                u = sub * HEADS + h
                s = s_scr[u] * scale + b_ref[h]
                s = jnp.where(dead, NEG, s)
                m = jnp.max(s, axis=-1, keepdims=True)
                p = jnp.exp(s - m)
                den = jnp.sum(p, axis=-1, keepdims=True)
                p_scr[u] = p.astype(BF16)
                r_scr[u] = jnp.broadcast_to(1.0 / den, (QB, HD))
                lse_all = jnp.where(lane == h, m + jnp.log(den), lse_all)
            l_ref[sub * QB:(sub + 1) * QB, :] = lse_all
        for sub in range(2):
            for h in range(HEADS):
                u = sub * HEADS + h
                sl = slice(h * HD, (h + 1) * HD)
                o = jnp.dot(p_scr[u], keys(sub, v2_ref, vp_ref, sl), preferred_element_type=F32) * r_scr[u]
                o_ref[sub * QB:(sub + 1) * QB, sl] = o.astype(BF16)

    def pair(colblk):
        return pl.BlockSpec((2 * QB, D), lambda s: (s, colblk))

    def prev(colblk):
        return pl.BlockSpec((QB, D), lambda s: (jnp.maximum(2 * s - 1, 0), colblk))

    return pl.pallas_call(
        kern, grid=(nblk // 2,),
        in_specs=[pair(0), pair(1), prev(1), pair(2), prev(2), pl.BlockSpec((HEADS, QB, 2 * QB), lambda s: (0, 0, 0))],
        out_specs=[pl.BlockSpec((2 * QB, D), lambda s: (s, 0)), pl.BlockSpec((2 * QB, HD), lambda s: (s, 0))],
        out_shape=[jax.ShapeDtypeStruct((S, D), BF16), jax.ShapeDtypeStruct((S, HD), F32)],
        scratch_shapes=[pltpu.VMEM((2 * HEADS, QB, 2 * QB), F32), pltpu.VMEM((2 * HEADS, QB, 2 * QB), BF16),
                        pltpu.VMEM((2 * HEADS, QB, HD), F32)],
        compiler_params=_cparams(1), name=name)(qkv_f, qkv_f, qkv_f, qkv_f, qkv_f, bias)


def _natural(ref, scr, tm):
    dil = ref.shape[0]
    for res in range(dil):
        _chunks_add_rows(scr, ref[res].astype(F32), res, tm // dil, dil, False)
    return _chunks_get(scr)


def _attn_merge(os, lses):
    S = os[0].shape[0]
    tm = 512

    def kern(o0, o1, o2, l0, l1, l2, om_ref, lm_ref, ls1, ls2, os1, os2):
        la = l0[...]
        lb = _natural(l1, ls1, tm)
        lc = _natural(l2, ls2, tm)
        m = jnp.maximum(jnp.maximum(la, lb), lc)
        e0, e1, e2 = jnp.exp(la - m), jnp.exp(lb - m), jnp.exp(lc - m)
        tot = e0 + e1 + e2
        lm_ref[...] = m + jnp.log(tot)
        w0, w1, w2 = e0 / tot, e1 / tot, e2 / tot
        for res in range(o1.shape[0]):
            _chunks_add_rows(os1, o1[res].astype(F32), res, tm // o1.shape[0], o1.shape[0], False)
        for res in range(o2.shape[0]):
            _chunks_add_rows(os2, o2[res].astype(F32), res, tm // o2.shape[0], o2.shape[0], False)
        for h in range(HEADS):
            sl = slice(h * HD, (h + 1) * HD)
            acc = w0[:, h:h + 1] * o0[:, sl].astype(F32) + w1[:, h:h + 1] * os1[h] + w2[:, h:h + 1] * os2[h]
            om_ref[:, sl] = acc.astype(BF16)

    def spec(a, c):
        if a.ndim == 2:
            return pl.BlockSpec((tm, c), lambda i: (i, 0))
        return pl.BlockSpec((a.shape[0], tm // a.shape[0], c), lambda i: (0, i, 0))

    return pl.pallas_call(
        kern, grid=(S // tm,),
        in_specs=[spec(a, D) for a in os] + [spec(a, HD) for a in lses],
        out_specs=[pl.BlockSpec((tm, D), lambda i: (i, 0)), pl.BlockSpec((tm, HD), lambda i: (i, 0))],
        out_shape=[jax.ShapeDtypeStruct((S, D), BF16), jax.ShapeDtypeStruct((S, HD), F32)],
        scratch_shapes=[pltpu.VMEM((1, tm, HD), F32), pltpu.VMEM((1, tm, HD), F32),
                        pltpu.VMEM((HEADS, tm, HD), F32), pltpu.VMEM((HEADS, tm, HD), F32)],
        compiler_params=_cparams(1), name="attn_merge")(*os, *lses)


def _attn_bwd_prep(dx, wo, o, lse, deps=()):
    S = o.shape[0]
    tm = 512
    dils = DILS[1:]
    nd = len(deps)

    def kern(dx_ref, w_ref, o_ref, l_ref, *rest):
        rest = rest[nd:]
        do_outs, l_outs, d_outs = rest[0:3], rest[3:5], rest[5:8]
        do_scr, l_scr, d_scr = rest[8:11]
        dov = lax.dot_general(dx_ref[...].astype(BF16), w_ref[...], (((1,), (1,)), ((), ())),
                              preferred_element_type=F32)
        lane = lax.broadcasted_iota(jnp.int32, (tm, HD), 1)
        acc = jnp.zeros((tm, HD), F32)
        for h in range(HEADS):
            sl = slice(h * HD, (h + 1) * HD)
            prod = dov[:, sl] * o_ref[:, sl].astype(F32)
            acc = jnp.where(lane == h, jnp.sum(prod, axis=-1, keepdims=True), acc)
        d_scr[0] = acc
        l_scr[0] = l_ref[...]
        _chunks_put(do_scr, dov)
        do_outs[0][...] = dov.astype(BF16)
        d_outs[0][...] = acc
        for j, dil in enumerate(dils):
            for res in range(dil):
                n = tm // dil
                do_outs[1 + j][res] = _chunks_rows(do_scr, res, n, dil).astype(BF16)
                l_outs[j][res] = _chunks_rows(l_scr, res, n, dil)
                d_outs[1 + j][res] = _chunks_rows(d_scr, res, n, dil)

    def nat(c):
        return pl.BlockSpec((tm, c), lambda i: (i, 0))

    def fol(dil, c):
        return pl.BlockSpec((dil, tm // dil, c), lambda i: (0, i, 0))

    def shapes(c, dt, with_natural):
        first = [jax.ShapeDtypeStruct((S, c), dt)] if with_natural else []
        return first + [jax.ShapeDtypeStruct((dil, S // dil, c), dt) for dil in dils]

    outs = pl.pallas_call(
        kern, grid=(S // tm,),
        in_specs=[nat(D), pl.BlockSpec((D, D), lambda i: (0, 0), pipeline_mode=pl.Buffered(1)), nat(D), nat(HD)]
        + [pl.BlockSpec(memory_space=pl.ANY)] * nd,
        out_specs=[nat(D)] + [fol(dil, D) for dil in dils] + [fol(dil, HD) for dil in dils]
        + [nat(HD)] + [fol(dil, HD) for dil in dils],
        out_shape=shapes(D, BF16, True) + shapes(HD, F32, False) + shapes(HD, F32, True),
        scratch_shapes=[pltpu.VMEM((HEADS, tm, HD), F32), pltpu.VMEM((1, tm, HD), F32), pltpu.VMEM((1, tm, HD), F32)],
        compiler_params=_cparams(1), name="attn_out_bwd")(dx, wo, o, lse, *deps)
    return outs[0:3], [lse] + list(outs[3:5]), outs[5:8]


def _attn_bwd(qkv_f, do_f, lse_f, delta_f, bias, nb, name):
    S = qkv_f.shape[0]
    nblk = S // QB
    scale = HD ** -0.5

    npair = nblk // 2

    def kern(q_ref, k2_ref, kp_ref, v2_ref, vp_ref, do_ref, l_ref, d_ref, b_ref, out_ref, dq_c, dk_c, dv_c,
             s_scr, dp_scr, ds_scr, p_scr):
        s_id = pl.program_id(0)

        @pl.when(s_id == 0)
        def _():
            dq_c[...] = jnp.zeros_like(dq_c)
            dk_c[...] = jnp.zeros_like(dk_c)
            dv_c[...] = jnp.zeros_like(dv_c)

        @pl.when(s_id == npair)
        def _():
            out_ref[:, 0:D] = dq_c[...].astype(BF16)
            out_ref[:, D:2 * D] = dk_c[...].astype(BF16)
            out_ref[:, 2 * D:3 * D] = dv_c[...].astype(BF16)

        def keys(sub, cur2_ref, prev_ref, sl):
            if sub:
                return cur2_ref[:, sl]
            return jnp.concatenate([prev_ref[:, sl], cur2_ref[0:QB, sl]], axis=0)

        @pl.when(s_id < npair)
        def _():
            col = lax.broadcasted_iota(jnp.int32, (QB, 2 * QB), 1)
            out_ref[:, 0:D] = dq_c[...].astype(BF16)
            for sub in range(2):
                rows = slice(sub * QB, (sub + 1) * QB)
                for h in range(HEADS):
                    sl = slice(h * HD, (h + 1) * HD)
                    u = sub * HEADS + h
                    s_scr[u] = lax.dot_general(q_ref[rows, sl], keys(sub, k2_ref, kp_ref, sl),
                                               (((1,), (1,)), ((), ())), preferred_element_type=F32)
                    dp_scr[u] = lax.dot_general(do_ref[rows, sl], keys(sub, v2_ref, vp_ref, sl),
                                                (((1,), (1,)), ((), ())), preferred_element_type=F32)
            for sub in range(2):
                rows = slice(sub * QB, (sub + 1) * QB)
                has_prev = jnp.bitwise_and(2 * s_id + sub, nb - 1) != 0
                dead = jnp.logical_and(col < QB, jnp.logical_not(has_prev))
                lv = l_ref[rows, :]
                dv_ = d_ref[rows, :]
                for h in range(HEADS):
                    u = sub * HEADS + h
                    s = s_scr[u] * scale + b_ref[h]
                    s = jnp.where(dead, NEG, s)
                    p = jnp.exp(s - lv[:, h:h + 1])
                    ds_scr[u] = (p * (dp_scr[u] - dv_[:, h:h + 1]) * scale).astype(BF16)
                    p_scr[u] = p.astype(BF16)
            for h in range(HEADS):
                sl = slice(h * HD, (h + 1) * HD)
                parts = []
                for sub in range(2):
                    rows = slice(sub * QB, (sub + 1) * QB)
                    u = sub * HEADS + h
                    ds = ds_scr[u]
                    dq_c[rows, sl] = jnp.dot(ds, keys(sub, k2_ref, kp_ref, sl), preferred_element_type=F32)
                    dkk = lax.dot_general(ds, q_ref[rows, sl], (((0,), (0,)), ((), ())), preferred_element_type=F32)
                    dvv = lax.dot_general(p_scr[u], do_ref[rows, sl], (((0,), (0,)), ((), ())),
                                          preferred_element_type=F32)
                    parts.append((dkk, dvv))
                for which, carry, base in ((0, dk_c, D), (1, dv_c, 2 * D)):
                    first, second = parts[0][which], parts[1][which]
                    cols = slice(base + h * HD, base + (h + 1) * HD)
                    out_ref[0:QB, cols] = carry[0:QB, sl].astype(BF16)
                    out_ref[QB:2 * QB, cols] = (carry[QB:2 * QB, sl] + first[:QB]).astype(BF16)
                    carry[0:QB, sl] = first[QB:] + second[:QB]
                    carry[QB:2 * QB, sl] = second[QB:]

    last = npair - 1

    def pair(colblk, c):
        return pl.BlockSpec((2 * QB, c), lambda s: (jnp.minimum(s, last), colblk))

    def prev(colblk):
        return pl.BlockSpec((QB, D), lambda s: (jnp.maximum(2 * jnp.minimum(s, last) - 1, 0), colblk))

    return pl.pallas_call(
        kern, grid=(npair + 1,),
        in_specs=[pair(0, D), pair(1, D), prev(1), pair(2, D), prev(2), pair(0, D), pair(0, HD), pair(0, HD),
                  pl.BlockSpec((HEADS, QB, 2 * QB), lambda s: (0, 0, 0))],
        out_specs=pl.BlockSpec((2 * QB, 3 * D), lambda s: (jnp.maximum(s - 1, 0), 0)),
        out_shape=jax.ShapeDtypeStruct((S, 3 * D), BF16),
        scratch_shapes=[pltpu.VMEM((2 * QB, D), F32), pltpu.VMEM((2 * QB, D), F32), pltpu.VMEM((2 * QB, D), F32),
                        pltpu.VMEM((2 * HEADS, QB, 2 * QB), F32), pltpu.VMEM((2 * HEADS, QB, 2 * QB), F32),
                        pltpu.VMEM((2 * HEADS, QB, 2 * QB), BF16), pltpu.VMEM((2 * HEADS, QB, 2 * QB), BF16)],
        compiler_params=_cparams(1), name=name)(qkv_f, qkv_f, qkv_f, qkv_f, qkv_f, do_f, lse_f, delta_f, bias)


def _local_step(x, tgt, comm, attn_norm, ffn_norm, final_norm):
    S = x.shape[0]
    bias = _bias_table()
    g_attn = attn_norm.reshape(1, D)
    g_f0 = ffn_norm[0:1]
    g_f1 = ffn_norm[1:2]
    g_fin = final_norm.reshape(1, D)
    W = {}

    def ffn_fwd(xin, h, l, next_gain, target=None):
        return _ffn_fwd(h, W[f"gu{l}"], W[f"d{l}"], xin, next_gain, tgt=target, name=f"ffn_fwd{l}")

    def ffn_bwd(dxo, xin, gain, h, gu, act, l, rs_group):
        dgu, dxin, dgain = _ffn_bwd(dxo, W[f"d{l}"], W[f"gu{l}"], gu, xin, gain, f"ffn_bwd{l}")
        gw_d = _mm(act, dxo, mode="tn", M=DFF, N=D, K=S, tm=HCH, tn=D, tk=2048, out_dtype=BF16, name=f"gw_d{l}")
        gw_gu = _mm(dgu, h, mode="tn", M=2 * DFF, N=D, K=S, tm=HCH, tn=D, tk=2048, out_dtype=BF16, name=f"gw_gu{l}")
        return dxin, dgain, comm.send_grads(rs_group, {f"d{l}": gw_d, f"gu{l}": gw_gu})

    nbs = [S // QB // dil for dil in DILS]
    hf = _rms_fwd_folded(x, g_attn, "rms_attn", deps=comm.ag_tokens)
    hf = [h.reshape(S, D) for h in hf]
    W.update(comm.weights(0, hf[0]))
    qkv_f, o_f, lse_f = [], [], []
    for g, dil in enumerate(DILS):
        qkv_f.append(_mm(hf[g], W["qkv"], mode="nt", M=S, N=3 * D, K=D, tm=2048, tn=1024, tk=D, out_dtype=BF16,
                         b_off=(3 * g, 0), name=f"qkv_proj{g}"))
        og, lg = _attn_fwd(qkv_f[g], bias[g], nbs[g], f"attn_fwd{g}")
        o_f.append(og if dil == 1 else og.reshape(dil, S // dil, D))
        lse_f.append(lg if dil == 1 else lg.reshape(dil, S // dil, HD))
    passing = [comm.pass_on(1, tuple(o_f)), comm.pass_on(2, tuple(o_f))]
    (o_f, lse_f), passing = lax.optimization_barrier(((o_f, lse_f), passing))
    o, lse = _attn_merge(o_f, lse_f)
    W.update(comm.weights(1, (o, passing[0])))
    x1, h1 = _mm_res_norm(o, W["wo"], x, g_f0, K=D, tm=1024, name="attn_out")
    pv = W["pv"].reshape(NDEV, 8, 128)
    pool_norm, pool_scale = pv[:, 0, :].reshape(1, D), pv[:, 1, :].reshape(1, D)
    gu0, act0, (x2, h2) = ffn_fwd(x1, h1, 0, pool_norm)

    W.update(comm.weights(2, (x2, passing[1])))
    yd = _pool_in_window(h2, W["wpi"])
    x3 = _pool_out(yd, W["pg"], pool_scale, x2)
    h3 = _rms_fwd(x3, g_f1, "rms_ffn1")
    gu1, act1, (dx4, d_fin, lossvec) = ffn_fwd(x3, h3, 1, g_fin, target=tgt)

    dx3, d_f1, token = ffn_bwd(dx4, x3, g_f1, h3, gu1, act1, 1, 0)
    du, d_scale, gw_pg = _pool_out_bwd(dx3, yd, W["pg"], pool_scale, deps=(token,))
    gw_pi = _mm(h2, du, mode="tn", M=D, N=D, K=S, tm=D, tn=D, tk=S, out_dtype=BF16, name="gw_pi")
    token = comm.send_grads(1, {"pg": gw_pg, "wpi": gw_pi})
    dx2, d_pool = _mm_rms_bwd(du, W["wpi"], x2, pool_norm, dx3, mode="nt", M=S, K=D, tm=1024, deps=(token,),
                              name="pool_in_bwd")
    dx1, d_f0, token = ffn_bwd(dx2, x1, g_f0, h1, gu0, act0, 0, 2)

    gw_o = _mm(o, dx1, mode="tn", M=D, N=D, K=S, tm=D, tn=D, tk=2048, out_dtype=BF16, deps=(token,), name="gw_o")
    do_f, lse_ff, delta_f = _attn_bwd_prep(dx1, W["wo"], o, lse, deps=(token,))
    dqkv_f, gw_qkv = [], None
    for g in range(NGROUPS):
        dqkv_f.append(_attn_bwd(qkv_f[g], do_f[g].reshape(S, D), lse_ff[g].reshape(S, HD),
                                delta_f[g].reshape(S, HD), bias[g], nbs[g], f"attn_bwd{g}"))
        gw_qkv = _mm(dqkv_f[g], hf[g], mode="tn", M=3 * D, N=D, K=S, tm=1024, tn=D, tk=S, out_dtype=BF16,
                     out_rows=NGROUPS * 3 * D, out_off=3 * g, out_prev=gw_qkv, name=f"gw_qkv{g}")
    token = comm.send_grads_pairwise({"wo": gw_o, "qkv": gw_qkv})
    folded = []
    for g in reversed(range(1, NGROUPS)):
        dh0_g = _mm(dqkv_f[g], W["qkv"], mode="nn", M=S, N=D, K=3 * D, tm=1024, tn=D, tk=3 * D, out_dtype=F32,
                    b_off=(g, 0), deps=(token,), name=f"qkv_proj_bwd{g}")
        folded.append(dh0_g.reshape(DILS[g], S // DILS[g], D))
        if g == NGROUPS - 1:
            token = comm.pass_grads(dh0_g)
    grad_x, d_attn = _mm_rms_bwd(dqkv_f[0], W["qkv"], x, g_attn, dx1, mode="nn", M=S, K=3 * D, tm=512,
                                 deps=(token,), folded=folded, name="qkv_proj_bwd0")

    vec = jnp.concatenate([d_attn, d_f0, d_f1, d_fin, d_pool, d_scale, lossvec, jnp.zeros((1, D), F32)], axis=0)
    return grad_x, vec


def _mesh_pos():
    x, y, c = lax.axis_index("x"), lax.axis_index("y"), lax.axis_index("c")
    return x, y, c, 4 * x + 2 * y + c


def _peer(x, y, c, k):
    kx, ky, kc = (k >> 2) & 1, (k >> 1) & 1, k & 1
    px = 1 - x if kx else x
    py = 1 - y if ky else y
    pc = 1 - c if kc else c
    return (px, py, pc), 4 * px + 2 * py + pc


ANY = pl.BlockSpec(memory_space=pl.ANY)


HBM = pl.BlockSpec(memory_space=pltpu.HBM)
SEMS = pl.BlockSpec(memory_space=pltpu.SEMAPHORE)
EFFECT = pltpu.SideEffectType.DATAFLOW_SIDE_EFFECTING

AG_GROUPS = (("qkv",), ("wo", "gu0", "d0", "pv"), ("wpi", "pg", "gu1", "d1"))
AG_ORDER = tuple(n for grp in AG_GROUPS for n in grp)
RS_GROUPS = (("d1", "gu1"), ("pg", "wpi"), ("d0", "gu0"), ("wo", "qkv"))


def _hbm(a):
    return pltpu.with_memory_space_constraint(a, pltpu.HBM)


def _remote(src, dst, send, recv, peer):
    return pltpu.make_async_remote_copy(src_ref=src, dst_ref=dst, send_sem=send, recv_sem=recv, device_id=peer,
                                        device_id_type=pl.DeviceIdType.MESH)


ALL_KS = tuple(range(1, NDEV))
AG_KS1 = (1, 2, 4, 6)
AG_KS2 = (2, 4, 6)
RS_KS_PAIR = (1, 3, 5, 7)
RS_KS_CHIPS = (2, 4, 6)


def _split_start(srcs, src_of, lands, copy_refs, name, deps=(), ks=ALL_KS, to=None):
    ns, n, nd, nk = len(srcs), len(lands), len(deps), len(ks)

    def body(*refs):
        ins, land = refs[:ns], refs[ns:ns + n]
        send, recv = refs[ns + n + nd], refs[ns + n + nd + 1]
        token = refs[-1]
        x, y, c, me = _mesh_pos()
        for j in range(n):
            for i, k in enumerate(ks):
                _, pid = _peer(x, y, c, k)
                dest, _ = _peer(x, y, c, k if to is None else to)
                src, dst = copy_refs(j, (land[j] if src_of[j] is None else ins[src_of[j]]), land[j], me, pid, i)
                _remote(src, dst, send.at[j * nk + i], recv.at[j * nk + i], dest).start()
        token[...] = jnp.zeros_like(token)

    outs = pl.pallas_call(
        body, name=name,
        out_shape=(pltpu.SemaphoreType.DMA((n * nk,)), pltpu.SemaphoreType.DMA((n * nk,)))
        + tuple(pltpu.HBM(a.shape, a.dtype) for a in srcs) + tuple(pltpu.HBM(a.shape, a.dtype) for a in lands)
        + (jax.ShapeDtypeStruct((8, 128), F32),),
        in_specs=(HBM,) * (ns + n) + (ANY,) * nd,
        out_specs=(SEMS, SEMS) + (HBM,) * (ns + n) + (pl.BlockSpec(memory_space=pltpu.VMEM),),
        input_output_aliases={i: 2 + i for i in range(ns + n)},
        compiler_params=pltpu.CompilerParams(has_side_effects=EFFECT),
    )(*[_hbm(a) for a in srcs], *[_hbm(a) for a in lands], *deps)
    return outs[0], outs[1], list(outs[2:2 + ns]), list(outs[2 + ns:2 + ns + n]), outs[-1]


def _split_wait(srcs, src_of, lands, send, recv, sem_rows, wait_refs, after, name, ks=ALL_KS):
    ns, n, nk = len(srcs), len(lands), len(ks)
    after = tuple(after) if isinstance(after, (tuple, list)) else (after,)

    def body(*refs):
        ins, land = refs[:ns], refs[ns:ns + n]
        send_ref, recv_ref = refs[ns + n], refs[ns + n + 1]
        x, y, c, me = _mesh_pos()
        for j in range(n):
            for i, k in enumerate(ks):
                peer, _ = _peer(x, y, c, k)
                src, dst = wait_refs(j, (land[j] if src_of[j] is None else ins[src_of[j]]), land[j])
                sem = sem_rows[j] * nk + i
                cp = _remote(src, dst, send_ref.at[sem], recv_ref.at[sem], peer)
                cp.wait_send()
                cp.wait_recv()

    outs = pl.pallas_call(
        body, name=name,
        out_shape=tuple(pltpu.HBM(a.shape, a.dtype) for a in srcs) + tuple(pltpu.HBM(a.shape, a.dtype) for a in lands),
        in_specs=(HBM,) * (ns + n) + (SEMS, SEMS) + (ANY,) * len(after),
        out_specs=(HBM,) * (ns + n),
        input_output_aliases={i: i for i in range(ns + n)},
        compiler_params=pltpu.CompilerParams(has_side_effects=EFFECT),
    )(*srcs, *lands, send, recv, *after)
    return list(outs[:ns]), list(outs[ns:])


def _ag_dtype(name):
    return F32 if name == "pv" else BF16


def _ag_align(name):
    return 8 if name == "pv" else 16


def _place_transposed(w, me, name):
    rows = w.shape[1]
    nblk = rows // 128

    def kern(me_ref, w_ref, o_ref):
        o_ref[...] = w_ref[...].T.astype(BF16)

    grid_spec = pltpu.PrefetchScalarGridSpec(
        num_scalar_prefetch=1, grid=(nblk,),
        in_specs=[pl.BlockSpec((D, 128), lambda i, me_ref: (0, i))],
        out_specs=pl.BlockSpec((128, D), lambda i, me_ref: (me_ref[0] * nblk + i, 0)))
    return pl.pallas_call(
        kern, grid_spec=grid_spec, out_shape=jax.ShapeDtypeStruct((NDEV * rows, D), BF16),
        compiler_params=_cparams(1), name=name)(me.reshape(1).astype(jnp.int32), w)


class _Comm:
    def __init__(self, params, make_shards, me, placed):
        self.me = me
        self.ag_land, self.ag_sems, self.ag_tokens, self.ag_passing = {}, {}, (), {}
        self.rs = []
        deps = ()
        for part, names in enumerate((AG_GROUPS[0], AG_ORDER[len(AG_GROUPS[0]):])):
            rows = [SEC_ROWS[n] for n in names]
            if part == 0:
                lands = [placed[n] for n in names]
            else:
                params, deps = lax.optimization_barrier((params, deps))
                shards = make_shards(*params)
                lands = [lax.dynamic_update_slice(lax.empty((NDEV * r, shards[n].shape[1]), _ag_dtype(n)),
                                                  shards[n].astype(_ag_dtype(n)), (_shard_pos(n, me), 0))
                         for n, r in zip(names, rows)]

            def copy_refs(j, src, land, me, pid, i, names=names, rows=rows):
                own = land.at[pl.ds(pl.multiple_of(_shard_pos(names[j], me), _ag_align(names[j])), rows[j])]
                return own, own

            send, recv, _, lands, token = _split_start([], [None] * len(names), lands, copy_refs, f"ag_start{part}",
                                                       deps=deps, ks=AG_KS1)
            deps = (token,)
            self.ag_tokens += (token,)
            for j, n in enumerate(names):
                self.ag_land[n] = lands[j]
                self.ag_sems[n] = (send, recv, j)

    def pass_on(self, group, after):
        names = AG_GROUPS[group]
        send, recv = self.ag_sems[names[0]][:2]
        idx = [self.ag_sems[n][2] for n in names]
        rows = [SEC_ROWS[n] for n in names]
        none = [None] * len(names)

        def wait_refs(j, src, land):
            return land.at[pl.ds(0, rows[j])], land.at[pl.ds(0, rows[j])]

        _, lands = _split_wait([], none, [self.ag_land[n] for n in names], send, recv, idx,
                               wait_refs, after, f"ag_wait{group}", ks=AG_KS1)

        def copy_refs(j, src, land, me, pid, i):
            theirs = land.at[pl.ds(pl.multiple_of(_shard_pos(names[j], pid), _ag_align(names[j])), rows[j])]
            return theirs, theirs

        send, recv, _, lands, token = _split_start([], none, lands, copy_refs, f"ag_pass{group}", ks=AG_KS2, to=1)
        self.ag_passing[group] = (send, recv, lands, wait_refs)
        return token

    def weights(self, group, after):
        names = AG_GROUPS[group]
        if group not in self.ag_passing:
            after = self.pass_on(group, after)
        send, recv, lands, wait_refs = self.ag_passing[group]
        _, lands = _split_wait([], [None] * len(names), lands, send, recv, list(range(len(names))), wait_refs, after,
                               f"ag_pass_wait{group}", ks=AG_KS2)
        return dict(zip(names, lands))

    def send_grads(self, group, gws):
        names = RS_GROUPS[group]
        rows = [SEC_ROWS[n] for n in names]
        grads = [gws[n] for n in names]
        me = self.me
        lands = [lax.dynamic_update_slice(
            lax.empty((NDEV, r, D), BF16),
            lax.dynamic_slice(g, (_shard_pos(n, me), 0), (r, D))[None], (me, 0, 0))
            for n, r, g in zip(names, rows, grads)]

        def copy_refs(j, src, land, me, pid, i):
            return src.at[pl.ds(pl.multiple_of(_shard_pos(names[j], pid), 16), rows[j])], land.at[me]

        send, recv, srcs, lands, token = _split_start(grads, list(range(len(names))), lands, copy_refs,
                                                      f"rs_start{group}")
        self.rs.append((names, rows, send, recv, srcs, lands, ALL_KS))
        return token

    def send_grads_pairwise(self, gws):
        names = RS_GROUPS[-1]
        rows = [SEC_ROWS[n] for n in names]
        grads = [gws[n] for n in names]
        idx = list(range(len(names)))
        lands = [lax.empty((len(RS_KS_PAIR), r, D), BF16) for r in rows]

        def copy_refs(j, src, land, me, pid, i):
            return src.at[pl.ds(pl.multiple_of(_shard_pos(names[j], pid), 16), rows[j])], land.at[i]

        send, recv, srcs, lands, token = _split_start(grads, idx, lands, copy_refs, "rs_pair_start",
                                                      ks=RS_KS_PAIR, to=1)
        self.pair = (names, rows, send, recv, srcs, lands)
        return token

    def pass_grads(self, after):
        names, rows, send, recv, srcs, lands = self.pair
        idx = list(range(len(names)))
        me = self.me

        def wait_refs(j, src, land):
            return src.at[pl.ds(0, rows[j])], land.at[0]

        srcs, lands = _split_wait(srcs, idx, lands, send, recv, idx, wait_refs, after, "rs_pair_wait", ks=RS_KS_PAIR)
        sums = [_pair_sum(g, got, me, f"rs_pair_sum_{n}") for n, g, got in zip(names, srcs, lands)]
        lands = [lax.dynamic_update_slice(lax.empty(p.shape, BF16), p[0:1], (0, 0, 0)) for p in sums]

        def copy_refs(j, src, land, me, pid, i):
            return src.at[i + 1], land.at[i + 1]

        send, recv, sums, lands, token = _split_start(sums, idx, lands, copy_refs, f"rs_start{len(RS_GROUPS) - 1}",
                                                      ks=RS_KS_CHIPS)
        self.rs.append((names, rows, send, recv, sums, lands, RS_KS_CHIPS))
        return token

    def received(self, group, after):
        names, rows, send, recv, srcs, lands, ks = self.rs[group]
        whole = srcs[0].ndim == 2

        def wait_refs(j, src, land):
            return (src.at[pl.ds(0, rows[j])] if whole else src.at[0]), land.at[0]

        _, lands = _split_wait(srcs, list(range(len(names))), lands, send, recv, list(range(len(names))), wait_refs,
                               after, f"rs_wait{group}", ks=ks)
        return dict(zip(names, lands))


def _pair_sum(grad, got, me, name):
    n, rows, _ = got.shape
    tr = 384 if rows % 384 == 0 else rows
    nt = rows // tr

    def kern(me_ref, g_ref, b_ref, o_ref):
        o_ref[0] = (g_ref[...].astype(F32) + b_ref[0].astype(F32)).astype(BF16)

    blk = pl.BlockSpec((1, tr, D), lambda i, t, me_ref: (i, t, 0))
    grid_spec = pltpu.PrefetchScalarGridSpec(
        num_scalar_prefetch=1, grid=(n, nt),
        in_specs=[pl.BlockSpec((tr, D), lambda i, t, me_ref: (jnp.bitwise_xor(me_ref[0], 2 * i) * nt + t, 0)), blk],
        out_specs=blk)
    return pl.pallas_call(
        kern, grid_spec=grid_spec, out_shape=jax.ShapeDtypeStruct(got.shape, BF16),
        compiler_params=_cparams(2), name=name)(me.reshape(1).astype(jnp.int32), grad, got)


def _sum_contributions(r_ref):
    g = r_ref[0].astype(F32)
    for slot in range(1, r_ref.shape[0]):
        g = g + r_ref[slot].astype(F32)
    return g


def _adam_math(g, w, m, v):
    c1 = 1.0 / (1.0 - ADAM_B1 ** ADAM_STEP)
    c2 = 1.0 / (1.0 - ADAM_B2 ** ADAM_STEP)
    mn = ADAM_B1 * m + (1.0 - ADAM_B1) * g
    vn = ADAM_B2 * v + (1.0 - ADAM_B2) * (g * g)
    return -ADAM_LR * ((mn * c1) / (jnp.sqrt(vn * c2) + ADAM_EPS) + ADAM_WD * w), mn, vn


def _adamw(R, w, m, v, *, tr, name, layer=None, prev=None):
    rows, C = w.shape[-2:]
    nprev = 0 if prev is None else 4

    def kern(r_ref, w_ref, m_ref, v_ref, *rest):
        g_out, d_out, m_out, v_out = rest[nprev:]
        g = _sum_contributions(r_ref)
        g_out[...] = g
        d_out[...], m_out[...], v_out[...] = _adam_math(g, w_ref[...], m_ref[...], v_ref[...])

    if layer is None:
        tile = pl.BlockSpec((tr, C), lambda i: (i, 0))
    else:
        tile = pl.BlockSpec((None, tr, C), lambda i: (layer, i, 0))
    shp = jax.ShapeDtypeStruct(w.shape, F32)
    return pl.pallas_call(
        kern, grid=(rows // tr,),
        in_specs=[pl.BlockSpec((R.shape[0], tr, C), lambda i: (0, i, 0)), tile, tile, tile]
        + [pl.BlockSpec(memory_space=pl.ANY)] * nprev,
        out_specs=[tile] * 4, out_shape=[shp] * 4,
        input_output_aliases={4 + k: k for k in range(nprev)},
        compiler_params=_cparams(1), name=name)(R, w, m, v, *(prev or ()))


def _adamw_pool_group(R, w, m, v):
    rows = SEC_ROWS["pg"]

    def kern(r_ref, w_ref, m_ref, v_ref, g_out, d_out, m_out, v_out):
        g = _sum_contributions(r_ref)
        g_out[0] = g
        d_out[0], m_out[0], v_out[0] = _adam_math(g, w_ref[0], m_ref[0], v_ref[0])

    blk = pl.BlockSpec((1, rows, PGD), lambda i: (i, 0, 0))
    shp = jax.ShapeDtypeStruct((POOL_G, rows, PGD), F32)
    return pl.pallas_call(
        kern, grid=(POOL_G,),
        in_specs=[pl.BlockSpec((NDEV, rows, PGD), lambda i: (0, 0, i)), blk, blk, blk],
        out_specs=[blk] * 4, out_shape=[shp] * 4, compiler_params=_cparams(1), name="adamw_pg")(R, w, m, v)


def _adamw_transposed(R, w, m, v, name):
    rows = R.shape[1]
    tr = 128

    def kern(r_ref, w_ref, m_ref, v_ref, g_out, d_out, m_out, v_out):
        g = _sum_contributions(r_ref).T
        g_out[...] = g
        d_out[...], m_out[...], v_out[...] = _adam_math(g, w_ref[...], m_ref[...], v_ref[...])

    tile = pl.BlockSpec((D, tr), lambda i: (0, i))
    shp = jax.ShapeDtypeStruct((D, rows), F32)
    return pl.pallas_call(
        kern, grid=(rows // tr,),
        in_specs=[pl.BlockSpec((R.shape[0], tr, D), lambda i: (0, i, 0)), tile, tile, tile],
        out_specs=[tile] * 4, out_shape=[shp] * 4, compiler_params=_cparams(1), name=name)(R, w, m, v)


def _pack_sections(w_qkv, w_attn_out, w_pool_in, w_pool_group, w_ffn_gate_up, w_ffn_down):
    pg = w_pool_group[0].transpose(1, 0, 2).reshape(SEC_ROWS["pg"], D)
    return {"qkv": w_qkv[0].T, "wo": w_attn_out[0], "wpi": w_pool_in[0], "gu0": w_ffn_gate_up[0].T,
            "gu1": w_ffn_gate_up[1].T, "d0": w_ffn_down[0], "d1": w_ffn_down[1], "pg": pg}


def _vec_pack(attn_norm, ffn_norm, final_norm, pool_norm_sh, pool_scale_sh, me):
    def place(sh):
        return lax.dynamic_update_slice(jnp.zeros((1, D), F32), sh, (0, me * 128))
    return jnp.concatenate([attn_norm, ffn_norm, final_norm.reshape(1, D), place(pool_norm_sh),
                            place(pool_scale_sh), jnp.zeros((2, D), F32)], axis=0)


def _vec_unpack(p, me):
    def take(r):
        return lax.dynamic_slice(p[r:r + 1], (0, me * 128), (1, 128))
    return p[0:1], p[1:3], p[3], take(4), take(5)


def kernel(x, attn_norm, w_qkv, w_attn_out, pool_norm, w_pool_in, w_pool_group, pool_scale, ffn_norm, w_ffn_gate_up, w_ffn_down, final_norm, loss_target, m_attn_norm, m_w_qkv, m_w_attn_out, m_pool_norm, m_w_pool_in, m_w_pool_group, m_pool_scale, m_ffn_norm, m_w_ffn_gate_up, m_w_ffn_down, m_final_norm, v_attn_norm, v_w_qkv, v_w_attn_out, v_pool_norm, v_w_pool_in, v_w_pool_group, v_pool_scale, v_ffn_norm, v_w_ffn_gate_up, v_w_ffn_down, v_final_norm):
    me = 4 * lax.axis_index("x") + 2 * lax.axis_index("y") + lax.axis_index("c")

    def make_shards(wq, wo, wpi, wpg, wgu, wd, pn, ps):
        shards = _pack_sections(wq, wo, wpi, wpg, wgu, wd)
        shards["pv"] = jnp.concatenate([pn, ps, jnp.zeros((6, 128), F32)], axis=0)
        return shards

    comm = _Comm((w_qkv, w_attn_out, w_pool_in, w_pool_group, w_ffn_gate_up, w_ffn_down, pool_norm, pool_scale),
                 make_shards, me, placed={"qkv": _place_transposed(w_qkv[0], me, "place_qkv")})

    grad_x, vec = _local_step(x[0], loss_target[0], comm, attn_norm, ffn_norm, final_norm)

    small = ((attn_norm, ffn_norm, final_norm, pool_norm, pool_scale),
             (m_attn_norm, m_ffn_norm, m_final_norm, m_pool_norm, m_pool_scale),
             (v_attn_norm, v_ffn_norm, v_final_norm, v_pool_norm, v_pool_scale))
    small, grad_x = lax.optimization_barrier((small, grad_x))
    vw, vm, vv = (_vec_pack(*s, me) for s in small)

    gu_t = [jnp.swapaxes(a, 1, 2) for a in (w_ffn_gate_up, m_w_ffn_gate_up, v_w_ffn_gate_up)]
    res = {}
    gu_res, d_res = None, None
    vec_out = None
    vec_land = lax.dynamic_update_slice(lax.empty((NDEV, 8, D), F32), vec[None], (me, 0, 0))
    vec_sems = _split_start([vec], [0], [vec_land], lambda j, src, land, me_, pid, i: (src, land.at[me_]),
                            "vec_start")
    after = (grad_x, vec_sems[4])
    for group in range(len(RS_GROUPS)):
        if group == len(RS_GROUPS) - 1:
            _, (VR,) = _split_wait(vec_sems[2], [0], vec_sems[3], vec_sems[0], vec_sems[1], [0],
                                   lambda j, src, land: (src, land.at[0]), after, "vec_wait")
            vec_out = _adamw(VR, vw, vm, vv, tr=8, name="adamw_vec")
            after = vec_out[0]
        for n, R in comm.received(group, after).items():
            if n in ("d0", "d1"):
                d_res = _adamw(R, w_ffn_down, m_w_ffn_down, v_w_ffn_down, tr=352, name=f"adamw_{n}",
                               layer=int(n[1]), prev=d_res)
                after = d_res[0]
            elif n in ("gu0", "gu1"):
                gu_res = _adamw(R, *gu_t, tr=352, name=f"adamw_{n}", layer=int(n[2]), prev=gu_res)
                after = gu_res[0]
            elif n == "pg":
                out = _adamw_pool_group(R, w_pool_group[0], m_w_pool_group[0], v_w_pool_group[0])
                res["pg"] = tuple(a[None] for a in out)
                after = out[0]
            elif n in ("wo", "wpi"):
                w, m, v = ((w_attn_out, m_w_attn_out, v_w_attn_out) if n == "wo"
                           else (w_pool_in, m_w_pool_in, v_w_pool_in))
                res[n] = _adamw(R, w[0], m[0], v[0], tr=128, name=f"adamw_{n}")
                res[n] = tuple(a[None] for a in res[n])
                after = res[n][0]
            else:
                out = _adamw_transposed(R, w_qkv[0], m_w_qkv[0], v_w_qkv[0], "adamw_qkv")
                res["qkv"] = tuple(a[None] for a in out)
                after = out[0]
    res["gu"] = tuple(jnp.swapaxes(a, 1, 2) for a in gu_res)
    res["d"] = tuple(d_res)

    outs = []
    for kind in range(4):
        an, fn, fin, pn, ps = _vec_unpack(vec_out[kind], me)
        outs.append((an, res["qkv"][kind], res["wo"][kind], pn, res["wpi"][kind], res["pg"][kind], ps, fn,
                     res["gu"][kind], res["d"][kind], fin))
    loss = 0.5 * jnp.sum(vec_out[0][6]) / D
    return (loss, grad_x[None]) + outs[0] + outs[1] + outs[2] + outs[3]
```

```python
import jax
import jax.numpy as jnp
from jax import lax
from jax.experimental import pallas as pl
from jax.experimental.pallas import tpu as pltpu

F32 = jnp.float32
BF16 = jnp.bfloat16

D = 1024
NDEV = 8
HEADS = 8
HD = 128
QB = 128
NGROUPS = 3
DILS = (1, 4, 16)
DFF = 2816
HCH = 1408
POOL_G = 4
PGD = 256
RMS_EPS = 1e-6
NEG = -1e30

ADAM_LR = 0.001
ADAM_B1 = 0.9
ADAM_B2 = 0.999
ADAM_EPS = 1e-08
ADAM_WD = 0.01
ADAM_STEP = 10

VMEM_LIMIT = 52 * 1024 * 1024

SECTIONS = (("qkv", 1152), ("wo", 128), ("wpi", 128), ("gu0", 704), ("gu1", 704),
            ("d0", 352), ("d1", 352), ("pg", 32))
SEC_ROWS = dict(SECTIONS)
SEC_ROWS["pv"] = 8


def _cparams(n_grid):
    return pltpu.CompilerParams(dimension_semantics=("arbitrary",) * n_grid, vmem_limit_bytes=VMEM_LIMIT)


def _shard_pos(name, dev):
    n = SEC_ROWS[name]
    if name in ("gu0", "gu1"):
        return ((dev % 4) // 2) * (2 * HCH) + (dev // 4) * HCH + (dev % 2) * n
    return dev * n


def _mm(a, b, *, mode, M, N, K, tm, tn, tk, out_dtype, name, a_off=(0, 0), b_off=(0, 0), res=None,
        out_rows=None, out_off=0, out_prev=None, deps=()):
    nm, nn, nk = M // tm, N // tn, K // tk
    assert nm * tm == M and nn * tn == N and nk * tk == K
    if mode == "nn":
        a_bs, b_bs = (tm, tk), (tk, tn)
        a_ix = lambda i, j, k: (i, k)
        b_ix = lambda i, j, k: (k, j)
        dims = (((1,), (0,)), ((), ()))
    elif mode == "nt":
        a_bs, b_bs = (tm, tk), (tn, tk)
        a_ix = lambda i, j, k: (i, k)
        b_ix = lambda i, j, k: (j, k)
        dims = (((1,), (1,)), ((), ()))
    else:
        a_bs, b_bs = (tk, tm), (tk, tn)
        a_ix = lambda i, j, k: (k, i)
        b_ix = lambda i, j, k: (k, j)
        dims = (((0,), (0,)), ((), ()))

    def spec(bs, ix, off):
        def im(i, j, k):
            r, c = ix(i, j, k)
            return (r + off[0], c + off[1])
        return pl.BlockSpec(bs, im)

    in_specs = [spec(a_bs, a_ix, a_off), spec(b_bs, b_ix, b_off)]
    args = [a, b]
    if res is not None:
        in_specs.append(pl.BlockSpec((tm, tn), lambda i, j, k: (i, j)))
        args.append(res)
    out_shape = jax.ShapeDtypeStruct((M if out_rows is None else out_rows, N), out_dtype)
    out_spec = pl.BlockSpec((tm, tn), lambda i, j, k: (i + out_off, j))
    has_res = res is not None
    extra = list(deps) + ([out_prev] if out_prev is not None else [])
    for dep in extra:
        in_specs.append(pl.BlockSpec(memory_space=pl.ANY))
        args.append(dep)
    o_pos = 2 + int(has_res) + len(extra)
    aliases = {len(args) - 1: 0} if out_prev is not None else {}

    def kern(*refs):
        a_ref, b_ref = refs[0], refs[1]
        res_ref = refs[2] if has_res else None
        o_ref = refs[o_pos]
        av = a_ref[...]
        bv = b_ref[...]
        if av.dtype != BF16:
            av = av.astype(BF16)
        if bv.dtype != BF16:
            bv = bv.astype(BF16)
        part = lax.dot_general(av, bv, dims, preferred_element_type=F32)

        def write(val):
            if has_res:
                val = val + res_ref[...]
            o_ref[...] = val.astype(out_dtype)

        if nk == 1:
            write(part)
        else:
            acc_ref = refs[-1]
            k = pl.program_id(2)

            @pl.when(k == 0)
            def _():
                acc_ref[...] = part

            @pl.when(k > 0)
            def _():
                acc_ref[...] += part

            @pl.when(k == nk - 1)
            def _():
                write(acc_ref[...])

    scratch = [pltpu.VMEM((tm, tn), F32)] if nk > 1 else []
    return pl.pallas_call(
        kern, grid=(nm, nn, nk), in_specs=in_specs, out_specs=out_spec, out_shape=out_shape,
        scratch_shapes=scratch, input_output_aliases=aliases, compiler_params=_cparams(3), name=name)(*args)


def _mm_rms_bwd(a, b, x, g, dres, *, mode, M, K, tm, name, b_off=(0, 0), deps=(), folded=()):
    nd, nf = len(deps), len(folded)
    b_bs = (K, D) if mode == "nn" else (D, K)
    dims = (((1,), (0,)), ((), ())) if mode == "nn" else (((1,), (1,)), ((), ()))

    def kern(a_ref, b_ref, x_ref, g_ref, dres_ref, *rest):
        f_refs = rest[:nf]
        dx_ref, dg_ref = rest[nf + nd:nf + nd + 2]
        i = pl.program_id(0)
        av = a_ref[...]
        if av.dtype != BF16:
            av = av.astype(BF16)
        dhv = lax.dot_general(av, b_ref[...], dims, preferred_element_type=F32)
        if nf:
            acc_ref = rest[-1]
            _chunks_put(acc_ref, dhv)
            for f_ref in f_refs:
                dil = f_ref.shape[0]
                for res in range(dil):
                    _chunks_add_rows(acc_ref, f_ref[res], res, tm // dil, dil, True)
            dhv = _chunks_get(acc_ref)
        xv = x_ref[...]
        r = lax.rsqrt(jnp.mean(xv * xv, axis=-1, keepdims=True) + RMS_EPS)
        xhat = xv * r
        gy = dhv * g_ref[...]
        dx_ref[...] = dres_ref[...] + r * (gy - xhat * jnp.mean(gy * xhat, axis=-1, keepdims=True))
        part = jnp.sum(dhv * xhat, axis=0, keepdims=True)

        @pl.when(i == 0)
        def _():
            dg_ref[...] = part

        @pl.when(i > 0)
        def _():
            dg_ref[...] += part

    row = pl.BlockSpec((tm, D), lambda i: (i, 0))
    vec = pl.BlockSpec((1, D), lambda i: (0, 0))
    return pl.pallas_call(
        kern, grid=(M // tm,),
        in_specs=[pl.BlockSpec((tm, K), lambda i: (i, 0)),
                  pl.BlockSpec(b_bs, lambda i: b_off, pipeline_mode=pl.Buffered(1)), row, vec, row]
        + [pl.BlockSpec((f.shape[0], tm // f.shape[0], D), lambda i: (0, i, 0)) for f in folded]
        + [pl.BlockSpec(memory_space=pl.ANY)] * nd,
        out_specs=[row, vec],
        out_shape=[jax.ShapeDtypeStruct((M, D), F32), jax.ShapeDtypeStruct((1, D), F32)],
        scratch_shapes=[pltpu.VMEM((D // 128, tm, 128), F32)] if nf else [],
        compiler_params=_cparams(1), name=name)(a, b, x, g, dres, *folded, *deps)


def _norm_tail(xv, gv, rest, head):
    r = lax.rsqrt(jnp.mean(xv * xv, axis=-1, keepdims=True) + RMS_EPS)
    xhat = xv * r
    if not head:
        xo_ref, h_ref = rest
        xo_ref[...] = xv
        h_ref[...] = (xhat * gv).astype(BF16)
        return
    t_ref, dx_ref, dg_ref, ls_ref = rest
    i = pl.program_id(0)
    e = xhat * gv - t_ref[...]
    dy = e * (1.0 / D)
    gy = dy * gv
    dx_ref[...] = r * (gy - xhat * jnp.mean(gy * xhat, axis=-1, keepdims=True))
    dgp = jnp.sum(dy * xhat, axis=0, keepdims=True)
    lsp = jnp.sum(e * e, axis=0, keepdims=True)

    @pl.when(i == 0)
    def _():
        dg_ref[...] = dgp
        ls_ref[...] = lsp

    @pl.when(i > 0)
    def _():
        dg_ref[...] += dgp
        ls_ref[...] += lsp


def _ffn_fwd(h, wgu, wd, res, g, *, name, tgt=None):
    S = h.shape[0]
    tm = 256
    nj = DFF // HCH
    head = tgt is not None

    def kern(h_ref, wgu_ref, wd_ref, res_ref, g_ref, *rest):
        t_refs, (gu_ref, act_ref), tail = rest[:int(head)], rest[int(head):int(head) + 2], rest[int(head) + 2:]
        hv = h_ref[...]
        for j in range(nj):
            gu = lax.dot_general(hv, wgu_ref[2 * HCH * j:2 * HCH * (j + 1), :], (((1,), (1,)), ((), ())),
                                 preferred_element_type=F32)
            gu_ref[:, 2 * HCH * j:2 * HCH * (j + 1)] = gu.astype(BF16)
            gate = gu[:, :HCH]
            act_ref[:, HCH * j:HCH * (j + 1)] = (gate * jax.nn.sigmoid(gate) * gu[:, HCH:]).astype(BF16)
        xv = res_ref[...] + jnp.dot(act_ref[...], wd_ref[...], preferred_element_type=F32)
        _norm_tail(xv, g_ref[...], tuple(t_refs) + tuple(tail), head)

    row = pl.BlockSpec((tm, D), lambda i: (i, 0))
    vec = pl.BlockSpec((1, D), lambda i: (0, 0))
    in_specs = [row, pl.BlockSpec((2 * DFF, D), lambda i: (0, 0), pipeline_mode=pl.Buffered(1)),
                pl.BlockSpec((DFF, D), lambda i: (0, 0), pipeline_mode=pl.Buffered(1)), row, vec]
    out_specs = [pl.BlockSpec((tm, 2 * DFF), lambda i: (i, 0)), pl.BlockSpec((tm, DFF), lambda i: (i, 0))]
    out_shape = [jax.ShapeDtypeStruct((S, 2 * DFF), BF16), jax.ShapeDtypeStruct((S, DFF), BF16)]
    args = [h, wgu, wd, res, g]
    if head:
        in_specs, args = in_specs + [row], args + [tgt]
        out_specs += [row, vec, vec]
        out_shape += [jax.ShapeDtypeStruct((S, D), F32), jax.ShapeDtypeStruct((1, D), F32),
                      jax.ShapeDtypeStruct((1, D), F32)]
    else:
        out_specs += [row, row]
        out_shape += [jax.ShapeDtypeStruct((S, D), F32), jax.ShapeDtypeStruct((S, D), BF16)]
    outs = pl.pallas_call(kern, grid=(S // tm,), in_specs=in_specs, out_specs=out_specs, out_shape=out_shape,
                          compiler_params=_cparams(1), name=name)(*args)
    return outs[0], outs[1], tuple(outs[2:])


def _mm_res_norm(a, b, res, g, *, K, tm, name, b_off=(0, 0), tgt=None):
    M = a.shape[0]
    head = tgt is not None

    def kern(a_ref, b_ref, res_ref, g_ref, *rest):
        xv = res_ref[...] + jnp.dot(a_ref[...], b_ref[...], preferred_element_type=F32)
        _norm_tail(xv, g_ref[...], rest, head)

    row = pl.BlockSpec((tm, D), lambda i: (i, 0))
    vec = pl.BlockSpec((1, D), lambda i: (0, 0))
    in_specs = [pl.BlockSpec((tm, K), lambda i: (i, 0)),
                pl.BlockSpec((K, D), lambda i: b_off, pipeline_mode=pl.Buffered(1)), row, vec]
    if head:
        return pl.pallas_call(
            kern, grid=(M // tm,), in_specs=in_specs + [row], out_specs=[row, vec, vec],
            out_shape=[jax.ShapeDtypeStruct((M, D), F32), jax.ShapeDtypeStruct((1, D), F32),
                       jax.ShapeDtypeStruct((1, D), F32)],
            compiler_params=_cparams(1), name=name)(a, b, res, g, tgt)
    return pl.pallas_call(
        kern, grid=(M // tm,), in_specs=in_specs, out_specs=[row, row],
        out_shape=[jax.ShapeDtypeStruct((M, D), F32), jax.ShapeDtypeStruct((M, D), BF16)],
        compiler_params=_cparams(1), name=name)(a, b, res, g)


def _mm_nt_resident(a, b, *, N, tm, b_blk, name):
    M = a.shape[0]

    def kern(a_ref, b_ref, o_ref):
        o_ref[...] = lax.dot_general(a_ref[...].astype(BF16), b_ref[...].astype(BF16), (((1,), (1,)), ((), ())),
                                     preferred_element_type=F32).astype(BF16)

    return pl.pallas_call(
        kern, grid=(M // tm,),
        in_specs=[pl.BlockSpec((tm, D), lambda i: (i, 0)),
                  pl.BlockSpec((N, D), lambda i: (b_blk, 0), pipeline_mode=pl.Buffered(1))],
        out_specs=pl.BlockSpec((tm, N), lambda i: (i, 0)),
        out_shape=jax.ShapeDtypeStruct((M, N), BF16),
        compiler_params=_cparams(1), name=name)(a, b)


def _rms_fwd(x, g, name, deps=()):
    S = x.shape[0]
    tr = 512

    def kern(x_ref, g_ref, *rest):
        h_ref = rest[-1]
        xv = x_ref[...]
        r = lax.rsqrt(jnp.mean(xv * xv, axis=-1, keepdims=True) + RMS_EPS)
        h_ref[...] = (xv * r * g_ref[...]).astype(BF16)

    return pl.pallas_call(
        kern, grid=(S // tr,),
        in_specs=[pl.BlockSpec((tr, D), lambda i: (i, 0)), pl.BlockSpec((1, D), lambda i: (0, 0))]
        + [pl.BlockSpec(memory_space=pl.ANY)] * len(deps),
        out_specs=pl.BlockSpec((tr, D), lambda i: (i, 0)),
        out_shape=jax.ShapeDtypeStruct((S, D), BF16), compiler_params=_cparams(1), name=name)(x, g, *deps)


def _chunks_put(scr, val):
    for c in range(scr.shape[0]):
        scr[c] = val[:, c * 128:(c + 1) * 128]


def _chunks_get(scr):
    return jnp.concatenate([scr[c] for c in range(scr.shape[0])], axis=1)


def _chunks_rows(scr, r, n, dil):
    return jnp.concatenate([scr.at[c][pl.ds(r, n, stride=dil), :] for c in range(scr.shape[0])], axis=1)


def _chunks_add_rows(scr, val, r, n, dil, accumulate):
    for c in range(scr.shape[0]):
        rows = pl.ds(r, n, stride=dil)
        piece = val[:, c * 128:(c + 1) * 128]
        tile = scr.at[c]
        tile[rows, :] = tile[rows, :] + piece if accumulate else piece


def _rms_fwd_folded(x, g, name, deps=()):
    S = x.shape[0]
    tr = 512
    dils = DILS[1:]

    def kern(x_ref, g_ref, *rest):
        outs, scr = rest[len(deps):-1], rest[-1]
        xv = x_ref[...]
        r = lax.rsqrt(jnp.mean(xv * xv, axis=-1, keepdims=True) + RMS_EPS)
        h = (xv * r * g_ref[...]).astype(BF16)
        outs[0][...] = h
        _chunks_put(scr, h.astype(F32))
        for o_ref, dil in zip(outs[1:], dils):
            for res in range(dil):
                o_ref[res] = _chunks_rows(scr, res, tr // dil, dil).astype(BF16)

    return pl.pallas_call(
        kern, grid=(S // tr,),
        in_specs=[pl.BlockSpec((tr, D), lambda i: (i, 0)), pl.BlockSpec((1, D), lambda i: (0, 0))]
        + [pl.BlockSpec(memory_space=pl.ANY)] * len(deps),
        out_specs=[pl.BlockSpec((tr, D), lambda i: (i, 0))]
        + [pl.BlockSpec((dil, tr // dil, D), lambda i: (0, i, 0)) for dil in dils],
        out_shape=[jax.ShapeDtypeStruct((S, D), BF16)]
        + [jax.ShapeDtypeStruct((dil, S // dil, D), BF16) for dil in dils],
        scratch_shapes=[pltpu.VMEM((D // 128, tr, 128), F32)],
        compiler_params=_cparams(1), name=name)(x, g, *deps)


def _ffn_bwd(dxo, wd, wgu, gu, xin, gain, name):
    S = dxo.shape[0]
    tm = 256
    nj = DFF // HCH

    def kern(dx_ref, wd_ref, wgu_ref, gu_ref, x_ref, g_ref, dgu_ref, dxin_ref, dg_ref):
        i = pl.program_id(0)
        dxv = dx_ref[...]
        dxb = dxv.astype(BF16)
        for j in range(nj):
            c0 = 2 * HCH * j
            dact = lax.dot_general(dxb, wd_ref[HCH * j:HCH * (j + 1), :], (((1,), (1,)), ((), ())),
                                   preferred_element_type=F32)
            gate = gu_ref[:, c0:c0 + HCH].astype(F32)
            up = gu_ref[:, c0 + HCH:c0 + 2 * HCH].astype(F32)
            sig = jax.nn.sigmoid(gate)
            silu = gate * sig
            dgu_ref[:, c0:c0 + HCH] = (dact * up * (sig * (1.0 + gate * (1.0 - sig)))).astype(BF16)
            dgu_ref[:, c0 + HCH:c0 + 2 * HCH] = (dact * silu).astype(BF16)
        dhv = jnp.dot(dgu_ref[...], wgu_ref[...], preferred_element_type=F32)
        xv = x_ref[...]
        r = lax.rsqrt(jnp.mean(xv * xv, axis=-1, keepdims=True) + RMS_EPS)
        xhat = xv * r
        gy = dhv * g_ref[...]
        dxin_ref[...] = dxv + r * (gy - xhat * jnp.mean(gy * xhat, axis=-1, keepdims=True))
        part = jnp.sum(dhv * xhat, axis=0, keepdims=True)

        @pl.when(i == 0)
        def _():
            dg_ref[...] = part

        @pl.when(i > 0)
        def _():
            dg_ref[...] += part

    row = pl.BlockSpec((tm, D), lambda i: (i, 0))
    wide = pl.BlockSpec((tm, 2 * DFF), lambda i: (i, 0))
    vec = pl.BlockSpec((1, D), lambda i: (0, 0))
    return pl.pallas_call(
        kern, grid=(S // tm,),
        in_specs=[row, pl.BlockSpec((DFF, D), lambda i: (0, 0), pipeline_mode=pl.Buffered(1)),
                  pl.BlockSpec((2 * DFF, D), lambda i: (0, 0), pipeline_mode=pl.Buffered(1)), wide, row, vec],
        out_specs=[wide, row, vec],
        out_shape=[jax.ShapeDtypeStruct((S, 2 * DFF), BF16), jax.ShapeDtypeStruct((S, D), F32),
                   jax.ShapeDtypeStruct((1, D), F32)],
        compiler_params=_cparams(1), name=name)(dxo, wd, wgu, gu, xin, gain)


def _window(val, grp, backward):
    S = val.shape[0]
    row = lax.broadcasted_iota(jnp.int32, val.shape, 0)
    cnt = jnp.minimum(row + 1, 2 << grp).astype(F32)
    s = val / cnt if backward else val
    for k in (1, 2, 4, 8)[:grp + 1]:
        if backward:
            sh = jnp.where(row < S - k, pltpu.roll(s, S - k, 0), 0.0)
        else:
            sh = jnp.where(row >= k, pltpu.roll(s, k, 0), 0.0)
        s = s + sh
    return s - val if backward else s / cnt - val


def _pool_in_window(h, wpi):
    S = h.shape[0]

    def kern(h_ref, w_ref, o_ref):
        g = pl.program_id(0)
        u = jnp.dot(h_ref[...], w_ref[...], preferred_element_type=F32)
        for grp in range(POOL_G):
            @pl.when(g == grp)
            def _(grp=grp):
                o_ref[...] = _window(u, grp, False).astype(BF16)

    return pl.pallas_call(
        kern, grid=(POOL_G,),
        in_specs=[pl.BlockSpec((S, D), lambda g: (0, 0), pipeline_mode=pl.Buffered(1)),
                  pl.BlockSpec((D, PGD), lambda g: (0, g))],
        out_specs=pl.BlockSpec((S, PGD), lambda g: (0, g)),
        out_shape=jax.ShapeDtypeStruct((S, D), BF16), compiler_params=_cparams(1), name="pool_in")(h, wpi)


def _pool_out(yd, G, scale, xres):
    S = yd.shape[0]
    tm = min(S, 4096)

    def kern(y_ref, w_ref, s_ref, x_ref, o_ref):
        z = jnp.dot(y_ref[...], w_ref[...], preferred_element_type=F32)
        o_ref[...] = x_ref[...] + z * s_ref[...]

    tile = pl.BlockSpec((tm, PGD), lambda i, g: (i, g))
    return pl.pallas_call(
        kern, grid=(S // tm, POOL_G),
        in_specs=[tile, pl.BlockSpec((PGD, PGD), lambda i, g: (0, g)),
                  pl.BlockSpec((1, PGD), lambda i, g: (0, g)), tile],
        out_specs=tile, out_shape=jax.ShapeDtypeStruct((S, D), F32),
        compiler_params=_cparams(2), name="pool_out")(yd, G, scale, xres)


def _pool_out_bwd(dz, yd, G, scale, deps=()):
    S = yd.shape[0]
    nd = len(deps)

    def kern(dz_ref, y_ref, w_ref, s_ref, *rest):
        du_ref, ds_ref, dw_ref = rest[nd:]
        g = pl.program_id(0)
        dzv = dz_ref[...]
        yv = y_ref[...]
        wv = w_ref[...]
        zraw = jnp.dot(yv, wv, preferred_element_type=F32)
        ds_ref[...] = jnp.sum(dzv * zraw, axis=0, keepdims=True)
        dzr = (dzv * s_ref[...]).astype(BF16)
        dw_ref[...] = lax.dot_general(yv, dzr, (((0,), (0,)), ((), ())), preferred_element_type=F32).astype(BF16)
        dyd = lax.dot_general(dzr, wv, (((1,), (1,)), ((), ())), preferred_element_type=F32)
        for grp in range(POOL_G):
            @pl.when(g == grp)
            def _(grp=grp):
                du_ref[...] = _window(dyd, grp, True).astype(BF16)

    tile = pl.BlockSpec((S, PGD), lambda g: (0, g))
    return pl.pallas_call(
        kern, grid=(POOL_G,),
        in_specs=[tile, tile, pl.BlockSpec((PGD, PGD), lambda g: (0, g)),
                  pl.BlockSpec((1, PGD), lambda g: (0, g))] + [pl.BlockSpec(memory_space=pl.ANY)] * nd,
        out_specs=[tile, pl.BlockSpec((1, PGD), lambda g: (0, g)), pl.BlockSpec((PGD, PGD), lambda g: (0, g))],
        out_shape=[jax.ShapeDtypeStruct((S, D), BF16), jax.ShapeDtypeStruct((1, D), F32),
                   jax.ShapeDtypeStruct((PGD, D), BF16)],
        compiler_params=_cparams(1), name="pool_out_bwd")(dz, yd, G, scale, *deps)


def _bias_table():
    qi = jnp.arange(QB)[:, None]
    ki = jnp.arange(2 * QB)[None, :]
    delta = QB + qi - ki
    inband = (delta >= 0) & (delta <= QB)
    n = NGROUPS * HEADS
    slopes = jnp.exp2(-8.0 * jnp.arange(1, n + 1, dtype=F32) / n).reshape(NGROUPS, HEADS)
    dil = jnp.asarray(DILS, F32)
    bias = -slopes[:, :, None, None] * (delta.astype(F32)[None, None] * dil[:, None, None, None])
    return jnp.where(inband[None, None], bias, NEG)


def _attn_fwd(qkv_f, bias, nb, name):
    S = qkv_f.shape[0]
    nblk = S // QB
    scale = HD ** -0.5

    def kern(q_ref, k2_ref, kp_ref, v2_ref, vp_ref, b_ref, o_ref, l_ref, s_scr, p_scr, r_scr):
        s_id = pl.program_id(0)
        col = lax.broadcasted_iota(jnp.int32, (QB, 2 * QB), 1)
        lane = lax.broadcasted_iota(jnp.int32, (QB, HD), 1)

        def keys(sub, cur2_ref, prev_ref, sl):
            if sub:
                return cur2_ref[:, sl]
            return jnp.concatenate([prev_ref[:, sl], cur2_ref[0:QB, sl]], axis=0)

        for sub in range(2):
            for h in range(HEADS):
                sl = slice(h * HD, (h + 1) * HD)
                s_scr[sub * HEADS + h] = lax.dot_general(
                    q_ref[sub * QB:(sub + 1) * QB, sl], keys(sub, k2_ref, kp_ref, sl), (((1,), (1,)), ((), ())),
                    preferred_element_type=F32)
        for sub in range(2):
            has_prev = jnp.bitwise_and(2 * s_id + sub, nb - 1) != 0
            dead = jnp.logical_and(col < QB, jnp.logical_not(has_prev))
            lse_all = jnp.zeros((QB, HD), F32)
            for h in range(HEADS):
                u = sub * HEADS + h
                s = s_scr[u] * scale + b_ref[h]
                s = jnp.where(dead, NEG, s)
                m = jnp.max(s, axis=-1, keepdims=True)
                p = jnp.exp(s - m)
                den = jnp.sum(p, axis=-1, keepdims=True)
                p_scr[u] = p.astype(BF16)
                r_scr[u] = jnp.broadcast_to(1.0 / den, (QB, HD))
                lse_all = jnp.where(lane == h, m + jnp.log(den), lse_all)
            l_ref[sub * QB:(sub + 1) * QB, :] = lse_all
        for sub in range(2):
            for h in range(HEADS):
                u = sub * HEADS + h
                sl = slice(h * HD, (h + 1) * HD)
                o = jnp.dot(p_scr[u], keys(sub, v2_ref, vp_ref, sl), preferred_element_type=F32) * r_scr[u]
                o_ref[sub * QB:(sub + 1) * QB, sl] = o.astype(BF16)

    def pair(colblk):
        return pl.BlockSpec((2 * QB, D), lambda s: (s, colblk))

    def prev(colblk):
        return pl.BlockSpec((QB, D), lambda s: (jnp.maximum(2 * s - 1, 0), colblk))

    return pl.pallas_call(
        kern, grid=(nblk // 2,),
        in_specs=[pair(0), pair(1), prev(1), pair(2), prev(2), pl.BlockSpec((HEADS, QB, 2 * QB), lambda s: (0, 0, 0))],
        out_specs=[pl.BlockSpec((2 * QB, D), lambda s: (s, 0)), pl.BlockSpec((2 * QB, HD), lambda s: (s, 0))],
        out_shape=[jax.ShapeDtypeStruct((S, D), BF16), jax.ShapeDtypeStruct((S, HD), F32)],
        scratch_shapes=[pltpu.VMEM((2 * HEADS, QB, 2 * QB), F32), pltpu.VMEM((2 * HEADS, QB, 2 * QB), BF16),
                        pltpu.VMEM((2 * HEADS, QB, HD), F32)],
        compiler_params=_cparams(1), name=name)(qkv_f, qkv_f, qkv_f, qkv_f, qkv_f, bias)


def _natural(ref, scr, tm):
    dil = ref.shape[0]
    for res in range(dil):
        _chunks_add_rows(scr, ref[res].astype(F32), res, tm // dil, dil, False)
    return _chunks_get(scr)


def _attn_merge(os, lses):
    S = os[0].shape[0]
    tm = 512

    def kern(o0, o1, o2, l0, l1, l2, om_ref, lm_ref, ls1, ls2, os1, os2):
        la = l0[...]
        lb = _natural(l1, ls1, tm)
        lc = _natural(l2, ls2, tm)
        m = jnp.maximum(jnp.maximum(la, lb), lc)
        e0, e1, e2 = jnp.exp(la - m), jnp.exp(lb - m), jnp.exp(lc - m)
        tot = e0 + e1 + e2
        lm_ref[...] = m + jnp.log(tot)
        w0, w1, w2 = e0 / tot, e1 / tot, e2 / tot
        for res in range(o1.shape[0]):
            _chunks_add_rows(os1, o1[res].astype(F32), res, tm // o1.shape[0], o1.shape[0], False)
        for res in range(o2.shape[0]):
            _chunks_add_rows(os2, o2[res].astype(F32), res, tm // o2.shape[0], o2.shape[0], False)
        for h in range(HEADS):
            sl = slice(h * HD, (h + 1) * HD)
            acc = w0[:, h:h + 1] * o0[:, sl].astype(F32) + w1[:, h:h + 1] * os1[h] + w2[:, h:h + 1] * os2[h]
            om_ref[:, sl] = acc.astype(BF16)

    def spec(a, c):
        if a.ndim == 2:
            return pl.BlockSpec((tm, c), lambda i: (i, 0))
        return pl.BlockSpec((a.shape[0], tm // a.shape[0], c), lambda i: (0, i, 0))

    return pl.pallas_call(
        kern, grid=(S // tm,),
        in_specs=[spec(a, D) for a in os] + [spec(a, HD) for a in lses],
        out_specs=[pl.BlockSpec((tm, D), lambda i: (i, 0)), pl.BlockSpec((tm, HD), lambda i: (i, 0))],
        out_shape=[jax.ShapeDtypeStruct((S, D), BF16), jax.ShapeDtypeStruct((S, HD), F32)],
        scratch_shapes=[pltpu.VMEM((1, tm, HD), F32), pltpu.VMEM((1, tm, HD), F32),
                        pltpu.VMEM((HEADS, tm, HD), F32), pltpu.VMEM((HEADS, tm, HD), F32)],
        compiler_params=_cparams(1), name="attn_merge")(*os, *lses)


def _attn_bwd_prep(dx, wo, o, lse, deps=()):
    S = o.shape[0]
    tm = 512
    dils = DILS[1:]
    nd = len(deps)

    def kern(dx_ref, w_ref, o_ref, l_ref, *rest):
        rest = rest[nd:]
        do_outs, l_outs, d_outs = rest[0:3], rest[3:5], rest[5:8]
        do_scr, l_scr, d_scr = rest[8:11]
        dov = lax.dot_general(dx_ref[...].astype(BF16), w_ref[...], (((1,), (1,)), ((), ())),
                              preferred_element_type=F32)
        lane = lax.broadcasted_iota(jnp.int32, (tm, HD), 1)
        acc = jnp.zeros((tm, HD), F32)
        for h in range(HEADS):
            sl = slice(h * HD, (h + 1) * HD)
            prod = dov[:, sl] * o_ref[:, sl].astype(F32)
            acc = jnp.where(lane == h, jnp.sum(prod, axis=-1, keepdims=True), acc)
        d_scr[0] = acc
        l_scr[0] = l_ref[...]
        _chunks_put(do_scr, dov)
        do_outs[0][...] = dov.astype(BF16)
        d_outs[0][...] = acc
        for j, dil in enumerate(dils):
            for res in range(dil):
                n = tm // dil
                do_outs[1 + j][res] = _chunks_rows(do_scr, res, n, dil).astype(BF16)
                l_outs[j][res] = _chunks_rows(l_scr, res, n, dil)
                d_outs[1 + j][res] = _chunks_rows(d_scr, res, n, dil)

    def nat(c):
        return pl.BlockSpec((tm, c), lambda i: (i, 0))

    def fol(dil, c):
        return pl.BlockSpec((dil, tm // dil, c), lambda i: (0, i, 0))

    def shapes(c, dt, with_natural):
        first = [jax.ShapeDtypeStruct((S, c), dt)] if with_natural else []
        return first + [jax.ShapeDtypeStruct((dil, S // dil, c), dt) for dil in dils]

    outs = pl.pallas_call(
        kern, grid=(S // tm,),
        in_specs=[nat(D), pl.BlockSpec((D, D), lambda i: (0, 0), pipeline_mode=pl.Buffered(1)), nat(D), nat(HD)]
        + [pl.BlockSpec(memory_space=pl.ANY)] * nd,
        out_specs=[nat(D)] + [fol(dil, D) for dil in dils] + [fol(dil, HD) for dil in dils]
        + [nat(HD)] + [fol(dil, HD) for dil in dils],
        out_shape=shapes(D, BF16, True) + shapes(HD, F32, False) + shapes(HD, F32, True),
        scratch_shapes=[pltpu.VMEM((HEADS, tm, HD), F32), pltpu.VMEM((1, tm, HD), F32), pltpu.VMEM((1, tm, HD), F32)],
        compiler_params=_cparams(1), name="attn_out_bwd")(dx, wo, o, lse, *deps)
    return outs[0:3], [lse] + list(outs[3:5]), outs[5:8]


def _attn_bwd(qkv_f, do_f, lse_f, delta_f, bias, nb, name):
    S = qkv_f.shape[0]
    nblk = S // QB
    scale = HD ** -0.5

    npair = nblk // 2

    def kern(q_ref, k2_ref, kp_ref, v2_ref, vp_ref, do_ref, l_ref, d_ref, b_ref, out_ref, dq_c, dk_c, dv_c,
             s_scr, dp_scr, ds_scr, p_scr):
        s_id = pl.program_id(0)

        @pl.when(s_id == 0)
        def _():
            dq_c[...] = jnp.zeros_like(dq_c)
            dk_c[...] = jnp.zeros_like(dk_c)
            dv_c[...] = jnp.zeros_like(dv_c)

        @pl.when(s_id == npair)
        def _():
            out_ref[:, 0:D] = dq_c[...].astype(BF16)
            out_ref[:, D:2 * D] = dk_c[...].astype(BF16)
            out_ref[:, 2 * D:3 * D] = dv_c[...].astype(BF16)

        def keys(sub, cur2_ref, prev_ref, sl):
            if sub:
                return cur2_ref[:, sl]
            return jnp.concatenate([prev_ref[:, sl], cur2_ref[0:QB, sl]], axis=0)

        @pl.when(s_id < npair)
        def _():
            col = lax.broadcasted_iota(jnp.int32, (QB, 2 * QB), 1)
            out_ref[:, 0:D] = dq_c[...].astype(BF16)
            for sub in range(2):
                rows = slice(sub * QB, (sub + 1) * QB)
                for h in range(HEADS):
                    sl = slice(h * HD, (h + 1) * HD)
                    u = sub * HEADS + h
                    s_scr[u] = lax.dot_general(q_ref[rows, sl], keys(sub, k2_ref, kp_ref, sl),
                                               (((1,), (1,)), ((), ())), preferred_element_type=F32)
                    dp_scr[u] = lax.dot_general(do_ref[rows, sl], keys(sub, v2_ref, vp_ref, sl),
                                                (((1,), (1,)), ((), ())), preferred_element_type=F32)
            for sub in range(2):
                rows = slice(sub * QB, (sub + 1) * QB)
                has_prev = jnp.bitwise_and(2 * s_id + sub, nb - 1) != 0
                dead = jnp.logical_and(col < QB, jnp.logical_not(has_prev))
                lv = l_ref[rows, :]
                dv_ = d_ref[rows, :]
                for h in range(HEADS):
                    u = sub * HEADS + h
                    s = s_scr[u] * scale + b_ref[h]
                    s = jnp.where(dead, NEG, s)
                    p = jnp.exp(s - lv[:, h:h + 1])
                    ds_scr[u] = (p * (dp_scr[u] - dv_[:, h:h + 1]) * scale).astype(BF16)
                    p_scr[u] = p.astype(BF16)
            for h in range(HEADS):
                sl = slice(h * HD, (h + 1) * HD)
                parts = []
                for sub in range(2):
                    rows = slice(sub * QB, (sub + 1) * QB)
                    u = sub * HEADS + h
                    ds = ds_scr[u]
                    dq_c[rows, sl] = jnp.dot(ds, keys(sub, k2_ref, kp_ref, sl), preferred_element_type=F32)
                    dkk = lax.dot_general(ds, q_ref[rows, sl], (((0,), (0,)), ((), ())), preferred_element_type=F32)
                    dvv = lax.dot_general(p_scr[u], do_ref[rows, sl], (((0,), (0,)), ((), ())),
                                          preferred_element_type=F32)
                    parts.append((dkk, dvv))
                for which, carry, base in ((0, dk_c, D), (1, dv_c, 2 * D)):
                    first, second = parts[0][which], parts[1][which]
                    cols = slice(base + h * HD, base + (h + 1) * HD)
                    out_ref[0:QB, cols] = carry[0:QB, sl].astype(BF16)
                    out_ref[QB:2 * QB, cols] = (carry[QB:2 * QB, sl] + first[:QB]).astype(BF16)
                    carry[0:QB, sl] = first[QB:] + second[:QB]
                    carry[QB:2 * QB, sl] = second[QB:]

    last = npair - 1

    def pair(colblk, c):
        return pl.BlockSpec((2 * QB, c), lambda s: (jnp.minimum(s, last), colblk))

    def prev(colblk):
        return pl.BlockSpec((QB, D), lambda s: (jnp.maximum(2 * jnp.minimum(s, last) - 1, 0), colblk))

    return pl.pallas_call(
        kern, grid=(npair + 1,),
        in_specs=[pair(0, D), pair(1, D), prev(1), pair(2, D), prev(2), pair(0, D), pair(0, HD), pair(0, HD),
                  pl.BlockSpec((HEADS, QB, 2 * QB), lambda s: (0, 0, 0))],
        out_specs=pl.BlockSpec((2 * QB, 3 * D), lambda s: (jnp.maximum(s - 1, 0), 0)),
        out_shape=jax.ShapeDtypeStruct((S, 3 * D), BF16),
        scratch_shapes=[pltpu.VMEM((2 * QB, D), F32), pltpu.VMEM((2 * QB, D), F32), pltpu.VMEM((2 * QB, D), F32),
                        pltpu.VMEM((2 * HEADS, QB, 2 * QB), F32), pltpu.VMEM((2 * HEADS, QB, 2 * QB), F32),
                        pltpu.VMEM((2 * HEADS, QB, 2 * QB), BF16), pltpu.VMEM((2 * HEADS, QB, 2 * QB), BF16)],
        compiler_params=_cparams(1), name=name)(qkv_f, qkv_f, qkv_f, qkv_f, qkv_f, do_f, lse_f, delta_f, bias)


def _local_step(x, tgt, comm, attn_norm, ffn_norm, final_norm):
    S = x.shape[0]
    bias = _bias_table()
    g_attn = attn_norm.reshape(1, D)
    g_f0 = ffn_norm[0:1]
    g_f1 = ffn_norm[1:2]
    g_fin = final_norm.reshape(1, D)
    W = {}

    def ffn_fwd(xin, h, l, next_gain, target=None):
        return _ffn_fwd(h, W[f"gu{l}"], W[f"d{l}"], xin, next_gain, tgt=target, name=f"ffn_fwd{l}")

    def ffn_bwd(dxo, xin, gain, h, gu, act, l, rs_group):
        dgu, dxin, dgain = _ffn_bwd(dxo, W[f"d{l}"], W[f"gu{l}"], gu, xin, gain, f"ffn_bwd{l}")
        gw_d = _mm(act, dxo, mode="tn", M=DFF, N=D, K=S, tm=HCH, tn=D, tk=2048, out_dtype=BF16, name=f"gw_d{l}")
        gw_gu = _mm(dgu, h, mode="tn", M=2 * DFF, N=D, K=S, tm=HCH, tn=D, tk=2048, out_dtype=BF16, name=f"gw_gu{l}")
        return dxin, dgain, comm.send_grads(rs_group, {f"d{l}": gw_d, f"gu{l}": gw_gu})

    nbs = [S // QB // dil for dil in DILS]
    hf = _rms_fwd_folded(x, g_attn, "rms_attn", deps=comm.ag_tokens)
    hf = [h.reshape(S, D) for h in hf]
    W.update(comm.weights(0, hf[0]))
    qkv_f, o_f, lse_f = [], [], []
    for g, dil in enumerate(DILS):
        qkv_f.append(_mm_nt_resident(hf[g], W["qkv"], N=3 * D, tm=1024, b_blk=g, name=f"qkv_proj{g}"))
        og, lg = _attn_fwd(qkv_f[g], bias[g], nbs[g], f"attn_fwd{g}")
        o_f.append(og if dil == 1 else og.reshape(dil, S // dil, D))
        lse_f.append(lg if dil == 1 else lg.reshape(dil, S // dil, HD))
    passing = [comm.pass_on(1, tuple(o_f)), comm.pass_on(2, tuple(o_f))]
    (o_f, lse_f), passing = lax.optimization_barrier(((o_f, lse_f), passing))
    o, lse = _attn_merge(o_f, lse_f)
    W.update(comm.weights(1, (o, passing[0])))
    x1, h1 = _mm_res_norm(o, W["wo"], x, g_f0, K=D, tm=1024, name="attn_out")
    pv = W["pv"].reshape(NDEV, 8, 128)
    pool_norm, pool_scale = pv[:, 0, :].reshape(1, D), pv[:, 1, :].reshape(1, D)
    gu0, act0, (x2, h2) = ffn_fwd(x1, h1, 0, pool_norm)

    W.update(comm.weights(2, (x2, passing[1])))
    yd = _pool_in_window(h2, W["wpi"])
    x3 = _pool_out(yd, W["pg"], pool_scale, x2)
    h3 = _rms_fwd(x3, g_f1, "rms_ffn1")
    gu1, act1, (dx4, d_fin, lossvec) = ffn_fwd(x3, h3, 1, g_fin, target=tgt)

    dx3, d_f1, token = ffn_bwd(dx4, x3, g_f1, h3, gu1, act1, 1, 0)
    du, d_scale, gw_pg = _pool_out_bwd(dx3, yd, W["pg"], pool_scale, deps=(token,))
    gw_pi = _mm(h2, du, mode="tn", M=D, N=D, K=S, tm=D, tn=D, tk=S, out_dtype=BF16, name="gw_pi")
    token = comm.send_grads(1, {"pg": gw_pg, "wpi": gw_pi})
    dx2, d_pool = _mm_rms_bwd(du, W["wpi"], x2, pool_norm, dx3, mode="nt", M=S, K=D, tm=1024, deps=(token,),
                              name="pool_in_bwd")
    dx1, d_f0, token = ffn_bwd(dx2, x1, g_f0, h1, gu0, act0, 0, 2)

    gw_o = _mm(o, dx1, mode="tn", M=D, N=D, K=S, tm=D, tn=D, tk=2048, out_dtype=BF16, deps=(token,), name="gw_o")
    do_f, lse_ff, delta_f = _attn_bwd_prep(dx1, W["wo"], o, lse, deps=(token,))
    dqkv_f, gw_qkv = [], None
    for g in range(NGROUPS):
        dqkv_f.append(_attn_bwd(qkv_f[g], do_f[g].reshape(S, D), lse_ff[g].reshape(S, HD),
                                delta_f[g].reshape(S, HD), bias[g], nbs[g], f"attn_bwd{g}"))
        gw_qkv = _mm(dqkv_f[g], hf[g], mode="tn", M=3 * D, N=D, K=S, tm=1024, tn=D, tk=S, out_dtype=BF16,
                     out_rows=NGROUPS * 3 * D, out_off=3 * g, out_prev=gw_qkv, name=f"gw_qkv{g}")
    token = comm.send_grads_pairwise({"wo": gw_o, "qkv": gw_qkv})
    folded = []
    for g in reversed(range(1, NGROUPS)):
        dh0_g = _mm(dqkv_f[g], W["qkv"], mode="nn", M=S, N=D, K=3 * D, tm=1024, tn=D, tk=3 * D, out_dtype=F32,
                    b_off=(g, 0), deps=(token,), name=f"qkv_proj_bwd{g}")
        folded.append(dh0_g.reshape(DILS[g], S // DILS[g], D))
        if g == NGROUPS - 1:
            token = comm.pass_grads(dh0_g)
    grad_x, d_attn = _mm_rms_bwd(dqkv_f[0], W["qkv"], x, g_attn, dx1, mode="nn", M=S, K=3 * D, tm=512,
                                 deps=(token,), folded=folded, name="qkv_proj_bwd0")

    vec = jnp.concatenate([d_attn, d_f0, d_f1, d_fin, d_pool, d_scale, lossvec, jnp.zeros((1, D), F32)], axis=0)
    return grad_x, vec


def _mesh_pos():
    x, y, c = lax.axis_index("x"), lax.axis_index("y"), lax.axis_index("c")
    return x, y, c, 4 * x + 2 * y + c


def _peer(x, y, c, k):
    kx, ky, kc = (k >> 2) & 1, (k >> 1) & 1, k & 1
    px = 1 - x if kx else x
    py = 1 - y if ky else y
    pc = 1 - c if kc else c
    return (px, py, pc), 4 * px + 2 * py + pc


ANY = pl.BlockSpec(memory_space=pl.ANY)


HBM = pl.BlockSpec(memory_space=pltpu.HBM)
SEMS = pl.BlockSpec(memory_space=pltpu.SEMAPHORE)
EFFECT = pltpu.SideEffectType.DATAFLOW_SIDE_EFFECTING

AG_GROUPS = (("qkv",), ("wo", "gu0", "d0", "pv"), ("wpi", "pg", "gu1", "d1"))
AG_ORDER = tuple(n for grp in AG_GROUPS for n in grp)
RS_GROUPS = (("d1", "gu1"), ("pg", "wpi"), ("d0", "gu0"), ("wo", "qkv"))


def _hbm(a):
    return pltpu.with_memory_space_constraint(a, pltpu.HBM)


def _remote(src, dst, send, recv, peer):
    return pltpu.make_async_remote_copy(src_ref=src, dst_ref=dst, send_sem=send, recv_sem=recv, device_id=peer,
                                        device_id_type=pl.DeviceIdType.MESH)


ALL_KS = tuple(range(1, NDEV))
AG_KS1 = (1, 2, 4, 6)
AG_KS2 = (2, 4, 6)
RS_KS_PAIR = (1, 3, 5, 7)
RS_KS_CHIPS = (2, 4, 6)


def _split_start(srcs, src_of, lands, copy_refs, name, deps=(), ks=ALL_KS, to=None):
    ns, n, nd, nk = len(srcs), len(lands), len(deps), len(ks)

    def body(*refs):
        ins, land = refs[:ns], refs[ns:ns + n]
        send, recv = refs[ns + n + nd], refs[ns + n + nd + 1]
        token = refs[-1]
        x, y, c, me = _mesh_pos()
        for j in range(n):
            for i, k in enumerate(ks):
                _, pid = _peer(x, y, c, k)
                dest, _ = _peer(x, y, c, k if to is None else to)
                src, dst = copy_refs(j, (land[j] if src_of[j] is None else ins[src_of[j]]), land[j], me, pid, i)
                _remote(src, dst, send.at[j * nk + i], recv.at[j * nk + i], dest).start()
        token[...] = jnp.zeros_like(token)

    outs = pl.pallas_call(
        body, name=name,
        out_shape=(pltpu.SemaphoreType.DMA((n * nk,)), pltpu.SemaphoreType.DMA((n * nk,)))
        + tuple(pltpu.HBM(a.shape, a.dtype) for a in srcs) + tuple(pltpu.HBM(a.shape, a.dtype) for a in lands)
        + (jax.ShapeDtypeStruct((8, 128), F32),),
        in_specs=(HBM,) * (ns + n) + (ANY,) * nd,
        out_specs=(SEMS, SEMS) + (HBM,) * (ns + n) + (pl.BlockSpec(memory_space=pltpu.VMEM),),
        input_output_aliases={i: 2 + i for i in range(ns + n)},
        compiler_params=pltpu.CompilerParams(has_side_effects=EFFECT),
    )(*[_hbm(a) for a in srcs], *[_hbm(a) for a in lands], *deps)
    return outs[0], outs[1], list(outs[2:2 + ns]), list(outs[2 + ns:2 + ns + n]), outs[-1]


def _split_wait(srcs, src_of, lands, send, recv, sem_rows, wait_refs, after, name, ks=ALL_KS):
    ns, n, nk = len(srcs), len(lands), len(ks)
    after = tuple(after) if isinstance(after, (tuple, list)) else (after,)

    def body(*refs):
        ins, land = refs[:ns], refs[ns:ns + n]
        send_ref, recv_ref = refs[ns + n], refs[ns + n + 1]
        x, y, c, me = _mesh_pos()
        for j in range(n):
            for i, k in enumerate(ks):
                peer, _ = _peer(x, y, c, k)
                src, dst = wait_refs(j, (land[j] if src_of[j] is None else ins[src_of[j]]), land[j])
                sem = sem_rows[j] * nk + i
                cp = _remote(src, dst, send_ref.at[sem], recv_ref.at[sem], peer)
                cp.wait_send()
                cp.wait_recv()

    outs = pl.pallas_call(
        body, name=name,
        out_shape=tuple(pltpu.HBM(a.shape, a.dtype) for a in srcs) + tuple(pltpu.HBM(a.shape, a.dtype) for a in lands),
        in_specs=(HBM,) * (ns + n) + (SEMS, SEMS) + (ANY,) * len(after),
        out_specs=(HBM,) * (ns + n),
        input_output_aliases={i: i for i in range(ns + n)},
        compiler_params=pltpu.CompilerParams(has_side_effects=EFFECT),
    )(*srcs, *lands, send, recv, *after)
    return list(outs[:ns]), list(outs[ns:])


def _ag_dtype(name):
    return F32 if name == "pv" else BF16


def _ag_align(name):
    return 8 if name == "pv" else 16


def _place_transposed(w, me, name):
    rows = w.shape[1]
    nblk = rows // 128

    def kern(me_ref, w_ref, o_ref):
        o_ref[...] = w_ref[...].T.astype(BF16)

    grid_spec = pltpu.PrefetchScalarGridSpec(
        num_scalar_prefetch=1, grid=(nblk,),
        in_specs=[pl.BlockSpec((D, 128), lambda i, me_ref: (0, i))],
        out_specs=pl.BlockSpec((128, D), lambda i, me_ref: (me_ref[0] * nblk + i, 0)))
    return pl.pallas_call(
        kern, grid_spec=grid_spec, out_shape=jax.ShapeDtypeStruct((NDEV * rows, D), BF16),
        compiler_params=_cparams(1), name=name)(me.reshape(1).astype(jnp.int32), w)


class _Comm:
    def __init__(self, params, make_shards, me, placed):
        self.me = me
        self.ag_land, self.ag_sems, self.ag_tokens, self.ag_passing = {}, {}, (), {}
        self.rs = []
        deps = ()
        for part, names in enumerate((AG_GROUPS[0], AG_ORDER[len(AG_GROUPS[0]):])):
            rows = [SEC_ROWS[n] for n in names]
            if part == 0:
                lands = [placed[n] for n in names]
            else:
                params, deps = lax.optimization_barrier((params, deps))
                shards = make_shards(*params)
                lands = [lax.dynamic_update_slice(lax.empty((NDEV * r, shards[n].shape[1]), _ag_dtype(n)),
                                                  shards[n].astype(_ag_dtype(n)), (_shard_pos(n, me), 0))
                         for n, r in zip(names, rows)]

            def copy_refs(j, src, land, me, pid, i, names=names, rows=rows):
                own = land.at[pl.ds(pl.multiple_of(_shard_pos(names[j], me), _ag_align(names[j])), rows[j])]
                return own, own

            send, recv, _, lands, token = _split_start([], [None] * len(names), lands, copy_refs, f"ag_start{part}",
                                                       deps=deps, ks=AG_KS1)
            deps = (token,)
            self.ag_tokens += (token,)
            for j, n in enumerate(names):
                self.ag_land[n] = lands[j]
                self.ag_sems[n] = (send, recv, j)

    def pass_on(self, group, after):
        names = AG_GROUPS[group]
        send, recv = self.ag_sems[names[0]][:2]
        idx = [self.ag_sems[n][2] for n in names]
        rows = [SEC_ROWS[n] for n in names]
        none = [None] * len(names)

        def wait_refs(j, src, land):
            return land.at[pl.ds(0, rows[j])], land.at[pl.ds(0, rows[j])]

        _, lands = _split_wait([], none, [self.ag_land[n] for n in names], send, recv, idx,
                               wait_refs, after, f"ag_wait{group}", ks=AG_KS1)

        def copy_refs(j, src, land, me, pid, i):
            theirs = land.at[pl.ds(pl.multiple_of(_shard_pos(names[j], pid), _ag_align(names[j])), rows[j])]
            return theirs, theirs

        send, recv, _, lands, token = _split_start([], none, lands, copy_refs, f"ag_pass{group}", ks=AG_KS2, to=1)
        self.ag_passing[group] = (send, recv, lands, wait_refs)
        return token

    def weights(self, group, after):
        names = AG_GROUPS[group]
        if group not in self.ag_passing:
            after = self.pass_on(group, after)
        send, recv, lands, wait_refs = self.ag_passing[group]
        _, lands = _split_wait([], [None] * len(names), lands, send, recv, list(range(len(names))), wait_refs, after,
                               f"ag_pass_wait{group}", ks=AG_KS2)
        return dict(zip(names, lands))

    def send_grads(self, group, gws):
        names = RS_GROUPS[group]
        rows = [SEC_ROWS[n] for n in names]
        grads = [gws[n] for n in names]
        me = self.me
        lands = [lax.dynamic_update_slice(
            lax.empty((NDEV, r, D), BF16),
            lax.dynamic_slice(g, (_shard_pos(n, me), 0), (r, D))[None], (me, 0, 0))
            for n, r, g in zip(names, rows, grads)]

        def copy_refs(j, src, land, me, pid, i):
            return src.at[pl.ds(pl.multiple_of(_shard_pos(names[j], pid), 16), rows[j])], land.at[me]

        send, recv, srcs, lands, token = _split_start(grads, list(range(len(names))), lands, copy_refs,
                                                      f"rs_start{group}")
        self.rs.append((names, rows, send, recv, srcs, lands, ALL_KS))
        return token

    def send_grads_pairwise(self, gws):
        names = RS_GROUPS[-1]
        rows = [SEC_ROWS[n] for n in names]
        grads = [gws[n] for n in names]
        idx = list(range(len(names)))
        lands = [lax.empty((len(RS_KS_PAIR), r, D), BF16) for r in rows]

        def copy_refs(j, src, land, me, pid, i):
            return src.at[pl.ds(pl.multiple_of(_shard_pos(names[j], pid), 16), rows[j])], land.at[i]

        send, recv, srcs, lands, token = _split_start(grads, idx, lands, copy_refs, "rs_pair_start",
                                                      ks=RS_KS_PAIR, to=1)
        self.pair = (names, rows, send, recv, srcs, lands)
        return token

    def pass_grads(self, after):
        names, rows, send, recv, srcs, lands = self.pair
        idx = list(range(len(names)))
        me = self.me

        def wait_refs(j, src, land):
            return src.at[pl.ds(0, rows[j])], land.at[0]

        srcs, lands = _split_wait(srcs, idx, lands, send, recv, idx, wait_refs, after, "rs_pair_wait", ks=RS_KS_PAIR)
        sums = [_pair_sum(g, got, me, f"rs_pair_sum_{n}") for n, g, got in zip(names, srcs, lands)]
        lands = [lax.dynamic_update_slice(lax.empty(p.shape, BF16), p[0:1], (0, 0, 0)) for p in sums]

        def copy_refs(j, src, land, me, pid, i):
            return src.at[i + 1], land.at[i + 1]

        send, recv, sums, lands, token = _split_start(sums, idx, lands, copy_refs, f"rs_start{len(RS_GROUPS) - 1}",
                                                      ks=RS_KS_CHIPS)
        self.rs.append((names, rows, send, recv, sums, lands, RS_KS_CHIPS))
        return token

    def received(self, group, after):
        names, rows, send, recv, srcs, lands, ks = self.rs[group]
        whole = srcs[0].ndim == 2

        def wait_refs(j, src, land):
            return (src.at[pl.ds(0, rows[j])] if whole else src.at[0]), land.at[0]

        _, lands = _split_wait(srcs, list(range(len(names))), lands, send, recv, list(range(len(names))), wait_refs,
                               after, f"rs_wait{group}", ks=ks)
        return dict(zip(names, lands))


def _pair_sum(grad, got, me, name):
    n, rows, _ = got.shape
    tr = 384 if rows % 384 == 0 else rows
    nt = rows // tr

    def kern(me_ref, g_ref, b_ref, o_ref):
        o_ref[0] = (g_ref[...].astype(F32) + b_ref[0].astype(F32)).astype(BF16)

    blk = pl.BlockSpec((1, tr, D), lambda i, t, me_ref: (i, t, 0))
    grid_spec = pltpu.PrefetchScalarGridSpec(
        num_scalar_prefetch=1, grid=(n, nt),
        in_specs=[pl.BlockSpec((tr, D), lambda i, t, me_ref: (jnp.bitwise_xor(me_ref[0], 2 * i) * nt + t, 0)), blk],
        out_specs=blk)
    return pl.pallas_call(
        kern, grid_spec=grid_spec, out_shape=jax.ShapeDtypeStruct(got.shape, BF16),
        compiler_params=_cparams(2), name=name)(me.reshape(1).astype(jnp.int32), grad, got)


def _sum_contributions(r_ref):
    g = r_ref[0].astype(F32)
    for slot in range(1, r_ref.shape[0]):
        g = g + r_ref[slot].astype(F32)
    return g


def _adam_math(g, w, m, v):
    c1 = 1.0 / (1.0 - ADAM_B1 ** ADAM_STEP)
    c2 = 1.0 / (1.0 - ADAM_B2 ** ADAM_STEP)
    mn = ADAM_B1 * m + (1.0 - ADAM_B1) * g
    vn = ADAM_B2 * v + (1.0 - ADAM_B2) * (g * g)
    return -ADAM_LR * ((mn * c1) / (jnp.sqrt(vn * c2) + ADAM_EPS) + ADAM_WD * w), mn, vn


def _adamw(R, w, m, v, *, tr, name, layer=None, prev=None):
    rows, C = w.shape[-2:]
    nprev = 0 if prev is None else 4

    def kern(r_ref, w_ref, m_ref, v_ref, *rest):
        g_out, d_out, m_out, v_out = rest[nprev:]
        g = _sum_contributions(r_ref)
        g_out[...] = g
        d_out[...], m_out[...], v_out[...] = _adam_math(g, w_ref[...], m_ref[...], v_ref[...])

    if layer is None:
        tile = pl.BlockSpec((tr, C), lambda i: (i, 0))
    else:
        tile = pl.BlockSpec((None, tr, C), lambda i: (layer, i, 0))
    shp = jax.ShapeDtypeStruct(w.shape, F32)
    return pl.pallas_call(
        kern, grid=(rows // tr,),
        in_specs=[pl.BlockSpec((R.shape[0], tr, C), lambda i: (0, i, 0)), tile, tile, tile]
        + [pl.BlockSpec(memory_space=pl.ANY)] * nprev,
        out_specs=[tile] * 4, out_shape=[shp] * 4,
        input_output_aliases={4 + k: k for k in range(nprev)},
        compiler_params=_cparams(1), name=name)(R, w, m, v, *(prev or ()))


def _adamw_pool_group(R, w, m, v):
    rows = SEC_ROWS["pg"]

    def kern(r_ref, w_ref, m_ref, v_ref, g_out, d_out, m_out, v_out):
        g = _sum_contributions(r_ref)
        g_out[0] = g
        d_out[0], m_out[0], v_out[0] = _adam_math(g, w_ref[0], m_ref[0], v_ref[0])

    blk = pl.BlockSpec((1, rows, PGD), lambda i: (i, 0, 0))
    shp = jax.ShapeDtypeStruct((POOL_G, rows, PGD), F32)
    return pl.pallas_call(
        kern, grid=(POOL_G,),
        in_specs=[pl.BlockSpec((NDEV, rows, PGD), lambda i: (0, 0, i)), blk, blk, blk],
        out_specs=[blk] * 4, out_shape=[shp] * 4, compiler_params=_cparams(1), name="adamw_pg")(R, w, m, v)


def _adamw_transposed(R, w, m, v, name):
    rows = R.shape[1]
    tr = 128

    def kern(r_ref, w_ref, m_ref, v_ref, g_out, d_out, m_out, v_out):
        g = _sum_contributions(r_ref).T
        g_out[...] = g
        d_out[...], m_out[...], v_out[...] = _adam_math(g, w_ref[...], m_ref[...], v_ref[...])

    tile = pl.BlockSpec((D, tr), lambda i: (0, i))
    shp = jax.ShapeDtypeStruct((D, rows), F32)
    return pl.pallas_call(
        kern, grid=(rows // tr,),
        in_specs=[pl.BlockSpec((R.shape[0], tr, D), lambda i: (0, i, 0)), tile, tile, tile],
        out_specs=[tile] * 4, out_shape=[shp] * 4, compiler_params=_cparams(1), name=name)(R, w, m, v)


def _pack_sections(w_qkv, w_attn_out, w_pool_in, w_pool_group, w_ffn_gate_up, w_ffn_down):
    pg = w_pool_group[0].transpose(1, 0, 2).reshape(SEC_ROWS["pg"], D)
    return {"qkv": w_qkv[0].T, "wo": w_attn_out[0], "wpi": w_pool_in[0], "gu0": w_ffn_gate_up[0].T,
            "gu1": w_ffn_gate_up[1].T, "d0": w_ffn_down[0], "d1": w_ffn_down[1], "pg": pg}


def _vec_pack(attn_norm, ffn_norm, final_norm, pool_norm_sh, pool_scale_sh, me):
    def place(sh):
        return lax.dynamic_update_slice(jnp.zeros((1, D), F32), sh, (0, me * 128))
    return jnp.concatenate([attn_norm, ffn_norm, final_norm.reshape(1, D), place(pool_norm_sh),
                            place(pool_scale_sh), jnp.zeros((2, D), F32)], axis=0)


def _vec_unpack(p, me):
    def take(r):
        return lax.dynamic_slice(p[r:r + 1], (0, me * 128), (1, 128))
    return p[0:1], p[1:3], p[3], take(4), take(5)


def kernel(x, attn_norm, w_qkv, w_attn_out, pool_norm, w_pool_in, w_pool_group, pool_scale, ffn_norm, w_ffn_gate_up, w_ffn_down, final_norm, loss_target, m_attn_norm, m_w_qkv, m_w_attn_out, m_pool_norm, m_w_pool_in, m_w_pool_group, m_pool_scale, m_ffn_norm, m_w_ffn_gate_up, m_w_ffn_down, m_final_norm, v_attn_norm, v_w_qkv, v_w_attn_out, v_pool_norm, v_w_pool_in, v_w_pool_group, v_pool_scale, v_ffn_norm, v_w_ffn_gate_up, v_w_ffn_down, v_final_norm):
    me = 4 * lax.axis_index("x") + 2 * lax.axis_index("y") + lax.axis_index("c")

    def make_shards(wq, wo, wpi, wpg, wgu, wd, pn, ps):
        shards = _pack_sections(wq, wo, wpi, wpg, wgu, wd)
        shards["pv"] = jnp.concatenate([pn, ps, jnp.zeros((6, 128), F32)], axis=0)
        return shards

    comm = _Comm((w_qkv, w_attn_out, w_pool_in, w_pool_group, w_ffn_gate_up, w_ffn_down, pool_norm, pool_scale),
                 make_shards, me, placed={"qkv": _place_transposed(w_qkv[0], me, "place_qkv")})

    grad_x, vec = _local_step(x[0], loss_target[0], comm, attn_norm, ffn_norm, final_norm)

    small = ((attn_norm, ffn_norm, final_norm, pool_norm, pool_scale),
             (m_attn_norm, m_ffn_norm, m_final_norm, m_pool_norm, m_pool_scale),
             (v_attn_norm, v_ffn_norm, v_final_norm, v_pool_norm, v_pool_scale))
    small, grad_x = lax.optimization_barrier((small, grad_x))
    vw, vm, vv = (_vec_pack(*s, me) for s in small)

    gu_t = [jnp.swapaxes(a, 1, 2) for a in (w_ffn_gate_up, m_w_ffn_gate_up, v_w_ffn_gate_up)]
    res = {}
    gu_res, d_res = None, None
    vec_out = None
    vec_land = lax.dynamic_update_slice(lax.empty((NDEV, 8, D), F32), vec[None], (me, 0, 0))
    vec_sems = _split_start([vec], [0], [vec_land], lambda j, src, land, me_, pid, i: (src, land.at[me_]),
                            "vec_start")
    after = (grad_x, vec_sems[4])
    for group in range(len(RS_GROUPS)):
        if group == len(RS_GROUPS) - 1:
            _, (VR,) = _split_wait(vec_sems[2], [0], vec_sems[3], vec_sems[0], vec_sems[1], [0],
                                   lambda j, src, land: (src, land.at[0]), after, "vec_wait")
            vec_out = _adamw(VR, vw, vm, vv, tr=8, name="adamw_vec")
            after = vec_out[0]
        for n, R in comm.received(group, after).items():
            if n in ("d0", "d1"):
                d_res = _adamw(R, w_ffn_down, m_w_ffn_down, v_w_ffn_down, tr=352, name=f"adamw_{n}",
                               layer=int(n[1]), prev=d_res)
                after = d_res[0]
            elif n in ("gu0", "gu1"):
                gu_res = _adamw(R, *gu_t, tr=352, name=f"adamw_{n}", layer=int(n[2]), prev=gu_res)
                after = gu_res[0]
            elif n == "pg":
                out = _adamw_pool_group(R, w_pool_group[0], m_w_pool_group[0], v_w_pool_group[0])
                res["pg"] = tuple(a[None] for a in out)
                after = out[0]
            elif n in ("wo", "wpi"):
                w, m, v = ((w_attn_out, m_w_attn_out, v_w_attn_out) if n == "wo"
                           else (w_pool_in, m_w_pool_in, v_w_pool_in))
                res[n] = _adamw(R, w[0], m[0], v[0], tr=128, name=f"adamw_{n}")
                res[n] = tuple(a[None] for a in res[n])
                after = res[n][0]
            else:
                out = _adamw_transposed(R, w_qkv[0], m_w_qkv[0], v_w_qkv[0], "adamw_qkv")
                res["qkv"] = tuple(a[None] for a in out)
                after = out[0]
    res["gu"] = tuple(jnp.swapaxes(a, 1, 2) for a in gu_res)
    res["d"] = tuple(d_res)

    outs = []
    for kind in range(4):
        an, fn, fin, pn, ps = _vec_unpack(vec_out[kind], me)
        outs.append((an, res["qkv"][kind], res["wo"][kind], pn, res["wpi"][kind], res["pg"][kind], ps, fn,
                     res["gu"][kind], res["d"][kind], fin))
    loss = 0.5 * jnp.sum(vec_out[0][6]) / D
    return (loss, grad_x[None]) + outs[0] + outs[1] + outs[2] + outs[3]
```

```python
import jax
import jax.numpy as jnp
from jax import lax
from jax.experimental import pallas as pl
from jax.experimental.pallas import tpu as pltpu

F32 = jnp.float32
BF16 = jnp.bfloat16

D = 1024
NDEV = 8
HEADS = 8
HD = 128
QB = 128
NGROUPS = 3
DILS = (1, 4, 16)
DFF = 2816
HCH = 1408
POOL_G = 4
PGD = 256
RMS_EPS = 1e-6
NEG = -1e30

ADAM_LR = 0.001
ADAM_B1 = 0.9
ADAM_B2 = 0.999
ADAM_EPS = 1e-08
ADAM_WD = 0.01
ADAM_STEP = 10

VMEM_LIMIT = 52 * 1024 * 1024

SECTIONS = (("qkv", 1152), ("wo", 128), ("wpi", 128), ("gu0", 704), ("gu1", 704),
            ("d0", 352), ("d1", 352), ("pg", 32))
SEC_ROWS = dict(SECTIONS)
SEC_ROWS["pv"] = 8


def _cparams(n_grid):
    return pltpu.CompilerParams(dimension_semantics=("arbitrary",) * n_grid, vmem_limit_bytes=VMEM_LIMIT)


def _shard_pos(name, dev):
    n = SEC_ROWS[name]
    if name in ("gu0", "gu1"):
        return ((dev % 4) // 2) * (2 * HCH) + (dev // 4) * HCH + (dev % 2) * n
    return dev * n


def _mm(a, b, *, mode, M, N, K, tm, tn, tk, out_dtype, name, a_off=(0, 0), b_off=(0, 0), res=None,
        out_rows=None, out_off=0, out_prev=None, deps=(), side=None):
    nm, nn, nk = M // tm, N // tn, K // tk
    assert nm * tm == M and nn * tn == N and nk * tk == K
    if mode == "nn":
        a_bs, b_bs = (tm, tk), (tk, tn)
        a_ix = lambda i, j, k: (i, k)
        b_ix = lambda i, j, k: (k, j)
        dims = (((1,), (0,)), ((), ()))
    elif mode == "nt":
        a_bs, b_bs = (tm, tk), (tn, tk)
        a_ix = lambda i, j, k: (i, k)
        b_ix = lambda i, j, k: (j, k)
        dims = (((1,), (1,)), ((), ()))
    else:
        a_bs, b_bs = (tk, tm), (tk, tn)
        a_ix = lambda i, j, k: (k, i)
        b_ix = lambda i, j, k: (k, j)
        dims = (((0,), (0,)), ((), ()))

    def spec(bs, ix, off):
        def im(i, j, k):
            r, c = ix(i, j, k)
            return (r + off[0], c + off[1])
        return pl.BlockSpec(bs, im)

    in_specs = [spec(a_bs, a_ix, a_off), spec(b_bs, b_ix, b_off)]
    args = [a, b]
    if res is not None:
        in_specs.append(pl.BlockSpec((tm, tn), lambda i, j, k: (i, j)))
        args.append(res)
    out_shape = jax.ShapeDtypeStruct((M if out_rows is None else out_rows, N), out_dtype)
    out_spec = pl.BlockSpec((tm, tn), lambda i, j, k: (i + out_off, j))
    has_res = res is not None
    extra = list(deps) + ([out_prev] if out_prev is not None else [])
    for dep in extra:
        in_specs.append(pl.BlockSpec(memory_space=pl.ANY))
        args.append(dep)
    o_pos = 2 + int(has_res) + len(extra)
    aliases = {len(args) - 1: 0} if out_prev is not None else {}
    out_specs, out_shapes = out_spec, out_shape
    if side is not None:
        R, wmv, layer, s_n = side
        s_rows, s_cols = wmv[0].shape[-2:]
        s_tr = s_rows // s_n
        assert s_tr * s_n == s_rows and s_tr % 8 == 0 and s_n <= nm * nn * nk
        s_pos = o_pos
        o_pos += 4

        def s_ix(i, j, k):
            return jnp.minimum((i * nn + j) * nk + k, s_n - 1)

        in_specs.append(pl.BlockSpec((R.shape[0], s_tr, s_cols), lambda i, j, k: (0, s_ix(i, j, k), 0)))
        s_tile = pl.BlockSpec((None, s_tr, s_cols), lambda i, j, k: (layer, s_ix(i, j, k), 0))
        in_specs += [s_tile] * 3
        args += [R, *wmv]
        out_specs = [out_spec] + [s_tile] * 4
        out_shapes = [out_shape] + [jax.ShapeDtypeStruct(wmv[0].shape, F32)] * 4

    def kern(*refs):
        a_ref, b_ref = refs[0], refs[1]
        res_ref = refs[2] if has_res else None
        o_ref = refs[o_pos]
        if side is not None:
            r_ref, w_ref, m_ref, v_ref = refs[s_pos:s_pos + 4]
            g_out, d_out, m_out, v_out = refs[o_pos + 1:o_pos + 5]

            @pl.when((pl.program_id(0) * nn + pl.program_id(1)) * nk + pl.program_id(2) < s_n)
            def _():
                sg = _sum_contributions(r_ref)
                g_out[...] = sg
                d_out[...], m_out[...], v_out[...] = _adam_math(sg, w_ref[...], m_ref[...], v_ref[...])
        av = a_ref[...]
        bv = b_ref[...]
        if av.dtype != BF16:
            av = av.astype(BF16)
        if bv.dtype != BF16:
            bv = bv.astype(BF16)
        part = lax.dot_general(av, bv, dims, preferred_element_type=F32)

        def write(val):
            if has_res:
                val = val + res_ref[...]
            o_ref[...] = val.astype(out_dtype)

        if nk == 1:
            write(part)
        else:
            acc_ref = refs[-1]
            k = pl.program_id(2)

            @pl.when(k == 0)
            def _():
                acc_ref[...] = part

            @pl.when(k > 0)
            def _():
                acc_ref[...] += part

            @pl.when(k == nk - 1)
            def _():
                write(acc_ref[...])

    scratch = [pltpu.VMEM((tm, tn), F32)] if nk > 1 else []
    outs = pl.pallas_call(
        kern, grid=(nm, nn, nk), in_specs=in_specs, out_specs=out_specs, out_shape=out_shapes,
        scratch_shapes=scratch, input_output_aliases=aliases, compiler_params=_cparams(3), name=name)(*args)
    return outs if side is None else (outs[0], tuple(outs[1:]))


def _mm_rms_bwd(a, b, x, g, dres, *, mode, M, K, tm, name, b_off=(0, 0), deps=(), folded=()):
    nd, nf = len(deps), len(folded)
    b_bs = (K, D) if mode == "nn" else (D, K)
    dims = (((1,), (0,)), ((), ())) if mode == "nn" else (((1,), (1,)), ((), ()))

    def kern(a_ref, b_ref, x_ref, g_ref, dres_ref, *rest):
        f_refs = rest[:nf]
        dx_ref, dg_ref = rest[nf + nd:nf + nd + 2]
        i = pl.program_id(0)
        av = a_ref[...]
        if av.dtype != BF16:
            av = av.astype(BF16)
        dhv = lax.dot_general(av, b_ref[...], dims, preferred_element_type=F32)
        if nf:
            acc_ref = rest[-1]
            _chunks_put(acc_ref, dhv)
            for f_ref in f_refs:
                dil = f_ref.shape[0]
                for res in range(dil):
                    _chunks_add_rows(acc_ref, f_ref[res], res, tm // dil, dil, True)
            dhv = _chunks_get(acc_ref)
        xv = x_ref[...]
        r = lax.rsqrt(jnp.mean(xv * xv, axis=-1, keepdims=True) + RMS_EPS)
        xhat = xv * r
        gy = dhv * g_ref[...]
        dx_ref[...] = dres_ref[...] + r * (gy - xhat * jnp.mean(gy * xhat, axis=-1, keepdims=True))
        part = jnp.sum(dhv * xhat, axis=0, keepdims=True)

        @pl.when(i == 0)
        def _():
            dg_ref[...] = part

        @pl.when(i > 0)
        def _():
            dg_ref[...] += part

    row = pl.BlockSpec((tm, D), lambda i: (i, 0))
    vec = pl.BlockSpec((1, D), lambda i: (0, 0))
    return pl.pallas_call(
        kern, grid=(M // tm,),
        in_specs=[pl.BlockSpec((tm, K), lambda i: (i, 0)),
                  pl.BlockSpec(b_bs, lambda i: b_off, pipeline_mode=pl.Buffered(1)), row, vec, row]
        + [pl.BlockSpec((f.shape[0], tm // f.shape[0], D), lambda i: (0, i, 0)) for f in folded]
        + [pl.BlockSpec(memory_space=pl.ANY)] * nd,
        out_specs=[row, vec],
        out_shape=[jax.ShapeDtypeStruct((M, D), F32), jax.ShapeDtypeStruct((1, D), F32)],
        scratch_shapes=[pltpu.VMEM((D // 128, tm, 128), F32)] if nf else [],
        compiler_params=_cparams(1), name=name)(a, b, x, g, dres, *folded, *deps)


def _norm_tail(xv, gv, rest, head):
    r = lax.rsqrt(jnp.mean(xv * xv, axis=-1, keepdims=True) + RMS_EPS)
    xhat = xv * r
    if not head:
        xo_ref, h_ref = rest
        xo_ref[...] = xv
        h_ref[...] = (xhat * gv).astype(BF16)
        return
    t_ref, dx_ref, dg_ref, ls_ref = rest
    i = pl.program_id(0)
    e = xhat * gv - t_ref[...]
    dy = e * (1.0 / D)
    gy = dy * gv
    dx_ref[...] = r * (gy - xhat * jnp.mean(gy * xhat, axis=-1, keepdims=True))
    dgp = jnp.sum(dy * xhat, axis=0, keepdims=True)
    lsp = jnp.sum(e * e, axis=0, keepdims=True)

    @pl.when(i == 0)
    def _():
        dg_ref[...] = dgp
        ls_ref[...] = lsp

    @pl.when(i > 0)
    def _():
        dg_ref[...] += dgp
        ls_ref[...] += lsp


def _ffn_fwd(h, wgu, wd, res, g, *, name, tgt=None):
    S = h.shape[0]
    tm = 256
    nj = DFF // HCH
    head = tgt is not None

    def kern(h_ref, wgu_ref, wd_ref, res_ref, g_ref, *rest):
        t_refs, (gu_ref, act_ref), tail = rest[:int(head)], rest[int(head):int(head) + 2], rest[int(head) + 2:]
        hv = h_ref[...]
        for j in range(nj):
            gu = lax.dot_general(hv, wgu_ref[2 * HCH * j:2 * HCH * (j + 1), :], (((1,), (1,)), ((), ())),
                                 preferred_element_type=F32)
            gu_ref[:, 2 * HCH * j:2 * HCH * (j + 1)] = gu.astype(BF16)
            gate = gu[:, :HCH]
            act_ref[:, HCH * j:HCH * (j + 1)] = (gate * jax.nn.sigmoid(gate) * gu[:, HCH:]).astype(BF16)
        xv = res_ref[...] + jnp.dot(act_ref[...], wd_ref[...], preferred_element_type=F32)
        _norm_tail(xv, g_ref[...], tuple(t_refs) + tuple(tail), head)

    row = pl.BlockSpec((tm, D), lambda i: (i, 0))
    vec = pl.BlockSpec((1, D), lambda i: (0, 0))
    in_specs = [row, pl.BlockSpec((2 * DFF, D), lambda i: (0, 0), pipeline_mode=pl.Buffered(1)),
                pl.BlockSpec((DFF, D), lambda i: (0, 0), pipeline_mode=pl.Buffered(1)), row, vec]
    out_specs = [pl.BlockSpec((tm, 2 * DFF), lambda i: (i, 0)), pl.BlockSpec((tm, DFF), lambda i: (i, 0))]
    out_shape = [jax.ShapeDtypeStruct((S, 2 * DFF), BF16), jax.ShapeDtypeStruct((S, DFF), BF16)]
    args = [h, wgu, wd, res, g]
    if head:
        in_specs, args = in_specs + [row], args + [tgt]
        out_specs += [row, vec, vec]
        out_shape += [jax.ShapeDtypeStruct((S, D), F32), jax.ShapeDtypeStruct((1, D), F32),
                      jax.ShapeDtypeStruct((1, D), F32)]
    else:
        out_specs += [row, row]
        out_shape += [jax.ShapeDtypeStruct((S, D), F32), jax.ShapeDtypeStruct((S, D), BF16)]
    outs = pl.pallas_call(kern, grid=(S // tm,), in_specs=in_specs, out_specs=out_specs, out_shape=out_shape,
                          compiler_params=_cparams(1), name=name)(*args)
    return outs[0], outs[1], tuple(outs[2:])


def _mm_res_norm(a, b, res, g, *, K, tm, name, b_off=(0, 0), tgt=None):
    M = a.shape[0]
    head = tgt is not None

    def kern(a_ref, b_ref, res_ref, g_ref, *rest):
        xv = res_ref[...] + jnp.dot(a_ref[...], b_ref[...], preferred_element_type=F32)
        _norm_tail(xv, g_ref[...], rest, head)

    row = pl.BlockSpec((tm, D), lambda i: (i, 0))
    vec = pl.BlockSpec((1, D), lambda i: (0, 0))
    in_specs = [pl.BlockSpec((tm, K), lambda i: (i, 0)),
                pl.BlockSpec((K, D), lambda i: b_off, pipeline_mode=pl.Buffered(1)), row, vec]
    if head:
        return pl.pallas_call(
            kern, grid=(M // tm,), in_specs=in_specs + [row], out_specs=[row, vec, vec],
            out_shape=[jax.ShapeDtypeStruct((M, D), F32), jax.ShapeDtypeStruct((1, D), F32),
                       jax.ShapeDtypeStruct((1, D), F32)],
            compiler_params=_cparams(1), name=name)(a, b, res, g, tgt)
    return pl.pallas_call(
        kern, grid=(M // tm,), in_specs=in_specs, out_specs=[row, row],
        out_shape=[jax.ShapeDtypeStruct((M, D), F32), jax.ShapeDtypeStruct((M, D), BF16)],
        compiler_params=_cparams(1), name=name)(a, b, res, g)


def _rms_fwd(x, g, name, deps=()):
    S = x.shape[0]
    tr = 512

    def kern(x_ref, g_ref, *rest):
        h_ref = rest[-1]
        xv = x_ref[...]
        r = lax.rsqrt(jnp.mean(xv * xv, axis=-1, keepdims=True) + RMS_EPS)
        h_ref[...] = (xv * r * g_ref[...]).astype(BF16)

    return pl.pallas_call(
        kern, grid=(S // tr,),
        in_specs=[pl.BlockSpec((tr, D), lambda i: (i, 0)), pl.BlockSpec((1, D), lambda i: (0, 0))]
        + [pl.BlockSpec(memory_space=pl.ANY)] * len(deps),
        out_specs=pl.BlockSpec((tr, D), lambda i: (i, 0)),
        out_shape=jax.ShapeDtypeStruct((S, D), BF16), compiler_params=_cparams(1), name=name)(x, g, *deps)


def _chunks_put(scr, val):
    for c in range(scr.shape[0]):
        scr[c] = val[:, c * 128:(c + 1) * 128]


def _chunks_get(scr):
    return jnp.concatenate([scr[c] for c in range(scr.shape[0])], axis=1)


def _chunks_rows(scr, r, n, dil):
    return jnp.concatenate([scr.at[c][pl.ds(r, n, stride=dil), :] for c in range(scr.shape[0])], axis=1)


def _chunks_add_rows(scr, val, r, n, dil, accumulate):
    for c in range(scr.shape[0]):
        rows = pl.ds(r, n, stride=dil)
        piece = val[:, c * 128:(c + 1) * 128]
        tile = scr.at[c]
        tile[rows, :] = tile[rows, :] + piece if accumulate else piece


def _rms_fwd_folded(x, g, name, deps=()):
    S = x.shape[0]
    tr = 512
    dils = DILS[1:]

    def kern(x_ref, g_ref, *rest):
        outs, scr = rest[len(deps):-1], rest[-1]
        xv = x_ref[...]
        r = lax.rsqrt(jnp.mean(xv * xv, axis=-1, keepdims=True) + RMS_EPS)
        h = (xv * r * g_ref[...]).astype(BF16)
        outs[0][...] = h
        _chunks_put(scr, h.astype(F32))
        for o_ref, dil in zip(outs[1:], dils):
            for res in range(dil):
                o_ref[res] = _chunks_rows(scr, res, tr // dil, dil).astype(BF16)

    return pl.pallas_call(
        kern, grid=(S // tr,),
        in_specs=[pl.BlockSpec((tr, D), lambda i: (i, 0)), pl.BlockSpec((1, D), lambda i: (0, 0))]
        + [pl.BlockSpec(memory_space=pl.ANY)] * len(deps),
        out_specs=[pl.BlockSpec((tr, D), lambda i: (i, 0))]
        + [pl.BlockSpec((dil, tr // dil, D), lambda i: (0, i, 0)) for dil in dils],
        out_shape=[jax.ShapeDtypeStruct((S, D), BF16)]
        + [jax.ShapeDtypeStruct((dil, S // dil, D), BF16) for dil in dils],
        scratch_shapes=[pltpu.VMEM((D // 128, tr, 128), F32)],
        compiler_params=_cparams(1), name=name)(x, g, *deps)


def _ffn_bwd(dxo, wd, wgu, gu, xin, gain, name):
    S = dxo.shape[0]
    tm = 256
    nj = DFF // HCH

    def kern(dx_ref, wd_ref, wgu_ref, gu_ref, x_ref, g_ref, dgu_ref, dxin_ref, dg_ref):
        i = pl.program_id(0)
        dxv = dx_ref[...]
        dxb = dxv.astype(BF16)
        for j in range(nj):
            c0 = 2 * HCH * j
            dact = lax.dot_general(dxb, wd_ref[HCH * j:HCH * (j + 1), :], (((1,), (1,)), ((), ())),
                                   preferred_element_type=F32)
            gate = gu_ref[:, c0:c0 + HCH].astype(F32)
            up = gu_ref[:, c0 + HCH:c0 + 2 * HCH].astype(F32)
            sig = jax.nn.sigmoid(gate)
            silu = gate * sig
            dgu_ref[:, c0:c0 + HCH] = (dact * up * (sig * (1.0 + gate * (1.0 - sig)))).astype(BF16)
            dgu_ref[:, c0 + HCH:c0 + 2 * HCH] = (dact * silu).astype(BF16)
        dhv = jnp.dot(dgu_ref[...], wgu_ref[...], preferred_element_type=F32)
        xv = x_ref[...]
        r = lax.rsqrt(jnp.mean(xv * xv, axis=-1, keepdims=True) + RMS_EPS)
        xhat = xv * r
        gy = dhv * g_ref[...]
        dxin_ref[...] = dxv + r * (gy - xhat * jnp.mean(gy * xhat, axis=-1, keepdims=True))
        part = jnp.sum(dhv * xhat, axis=0, keepdims=True)

        @pl.when(i == 0)
        def _():
            dg_ref[...] = part

        @pl.when(i > 0)
        def _():
            dg_ref[...] += part

    row = pl.BlockSpec((tm, D), lambda i: (i, 0))
    wide = pl.BlockSpec((tm, 2 * DFF), lambda i: (i, 0))
    vec = pl.BlockSpec((1, D), lambda i: (0, 0))
    return pl.pallas_call(
        kern, grid=(S // tm,),
        in_specs=[row, pl.BlockSpec((DFF, D), lambda i: (0, 0), pipeline_mode=pl.Buffered(1)),
                  pl.BlockSpec((2 * DFF, D), lambda i: (0, 0), pipeline_mode=pl.Buffered(1)), wide, row, vec],
        out_specs=[wide, row, vec],
        out_shape=[jax.ShapeDtypeStruct((S, 2 * DFF), BF16), jax.ShapeDtypeStruct((S, D), F32),
                   jax.ShapeDtypeStruct((1, D), F32)],
        compiler_params=_cparams(1), name=name)(dxo, wd, wgu, gu, xin, gain)


def _window(val, grp, backward):
    S = val.shape[0]
    row = lax.broadcasted_iota(jnp.int32, val.shape, 0)
    cnt = jnp.minimum(row + 1, 2 << grp).astype(F32)
    s = val / cnt if backward else val
    for k in (1, 2, 4, 8)[:grp + 1]:
        if backward:
            sh = jnp.where(row < S - k, pltpu.roll(s, S - k, 0), 0.0)
        else:
            sh = jnp.where(row >= k, pltpu.roll(s, k, 0), 0.0)
        s = s + sh
    return s - val if backward else s / cnt - val


def _pool_in_window(h, wpi):
    S = h.shape[0]

    def kern(h_ref, w_ref, o_ref):
        g = pl.program_id(0)
        u = jnp.dot(h_ref[...], w_ref[...], preferred_element_type=F32)
        for grp in range(POOL_G):
            @pl.when(g == grp)
            def _(grp=grp):
                o_ref[...] = _window(u, grp, False).astype(BF16)

    return pl.pallas_call(
        kern, grid=(POOL_G,),
        in_specs=[pl.BlockSpec((S, D), lambda g: (0, 0), pipeline_mode=pl.Buffered(1)),
                  pl.BlockSpec((D, PGD), lambda g: (0, g))],
        out_specs=pl.BlockSpec((S, PGD), lambda g: (0, g)),
        out_shape=jax.ShapeDtypeStruct((S, D), BF16), compiler_params=_cparams(1), name="pool_in")(h, wpi)


def _pool_out(yd, G, scale, xres):
    S = yd.shape[0]
    tm = min(S, 4096)

    def kern(y_ref, w_ref, s_ref, x_ref, o_ref):
        z = jnp.dot(y_ref[...], w_ref[...], preferred_element_type=F32)
        o_ref[...] = x_ref[...] + z * s_ref[...]

    tile = pl.BlockSpec((tm, PGD), lambda i, g: (i, g))
    return pl.pallas_call(
        kern, grid=(S // tm, POOL_G),
        in_specs=[tile, pl.BlockSpec((PGD, PGD), lambda i, g: (0, g)),
                  pl.BlockSpec((1, PGD), lambda i, g: (0, g)), tile],
        out_specs=tile, out_shape=jax.ShapeDtypeStruct((S, D), F32),
        compiler_params=_cparams(2), name="pool_out")(yd, G, scale, xres)


def _pool_out_bwd(dz, yd, G, scale, deps=()):
    S = yd.shape[0]
    nd = len(deps)

    def kern(dz_ref, y_ref, w_ref, s_ref, *rest):
        du_ref, ds_ref, dw_ref = rest[nd:]
        g = pl.program_id(0)
        dzv = dz_ref[...]
        yv = y_ref[...]
        wv = w_ref[...]
        zraw = jnp.dot(yv, wv, preferred_element_type=F32)
        ds_ref[...] = jnp.sum(dzv * zraw, axis=0, keepdims=True)
        dzr = (dzv * s_ref[...]).astype(BF16)
        dw_ref[...] = lax.dot_general(yv, dzr, (((0,), (0,)), ((), ())), preferred_element_type=F32).astype(BF16)
        dyd = lax.dot_general(dzr, wv, (((1,), (1,)), ((), ())), preferred_element_type=F32)
        for grp in range(POOL_G):
            @pl.when(g == grp)
            def _(grp=grp):
                du_ref[...] = _window(dyd, grp, True).astype(BF16)

    tile = pl.BlockSpec((S, PGD), lambda g: (0, g))
    return pl.pallas_call(
        kern, grid=(POOL_G,),
        in_specs=[tile, tile, pl.BlockSpec((PGD, PGD), lambda g: (0, g)),
                  pl.BlockSpec((1, PGD), lambda g: (0, g))] + [pl.BlockSpec(memory_space=pl.ANY)] * nd,
        out_specs=[tile, pl.BlockSpec((1, PGD), lambda g: (0, g)), pl.BlockSpec((PGD, PGD), lambda g: (0, g))],
        out_shape=[jax.ShapeDtypeStruct((S, D), BF16), jax.ShapeDtypeStruct((1, D), F32),
                   jax.ShapeDtypeStruct((PGD, D), BF16)],
        compiler_params=_cparams(1), name="pool_out_bwd")(dz, yd, G, scale, *deps)


def _bias_table():
    qi = jnp.arange(QB)[:, None]
    ki = jnp.arange(2 * QB)[None, :]
    delta = QB + qi - ki
    inband = (delta >= 0) & (delta <= QB)
    n = NGROUPS * HEADS
    slopes = jnp.exp2(-8.0 * jnp.arange(1, n + 1, dtype=F32) / n).reshape(NGROUPS, HEADS)
    dil = jnp.asarray(DILS, F32)
    bias = -slopes[:, :, None, None] * (delta.astype(F32)[None, None] * dil[:, None, None, None])
    return jnp.where(inband[None, None], bias, NEG)


def _attn_fwd(qkv_f, bias, nb, name):
    S = qkv_f.shape[0]
    nblk = S // QB
    scale = HD ** -0.5

    def kern(q_ref, k2_ref, kp_ref, v2_ref, vp_ref, b_ref, o_ref, l_ref, s_scr, p_scr, r_scr):
        s_id = pl.program_id(0)
        col = lax.broadcasted_iota(jnp.int32, (QB, 2 * QB), 1)
        lane = lax.broadcasted_iota(jnp.int32, (QB, HD), 1)

        def keys(sub, cur2_ref, prev_ref, sl):
            if sub:
                return cur2_ref[:, sl]
            return jnp.concatenate([prev_ref[:, sl], cur2_ref[0:QB, sl]], axis=0)

        for sub in range(2):
            for h in range(HEADS):
                sl = slice(h * HD, (h + 1) * HD)
                s_scr[sub * HEADS + h] = lax.dot_general(
                    q_ref[sub * QB:(sub + 1) * QB, sl], keys(sub, k2_ref, kp_ref, sl), (((1,), (1,)), ((), ())),
                    preferred_element_type=F32)
        for sub in range(2):
            has_prev = jnp.bitwise_and(2 * s_id + sub, nb - 1) != 0
            dead = jnp.logical_and(col < QB, jnp.logical_not(has_prev))
            lse_all = jnp.zeros((QB, HD), F32)
            for h in range(HEADS):
                u = sub * HEADS + h
                s = s_scr[u] * scale + b_ref[h]
                s = jnp.where(dead, NEG, s)
                m = jnp.max(s, axis=-1, keepdims=True)
                p = jnp.exp(s - m)
                den = jnp.sum(p, axis=-1, keepdims=True)
                p_scr[u] = p.astype(BF16)
                r_scr[u] = jnp.broadcast_to(1.0 / den, (QB, HD))
                lse_all = jnp.where(lane == h, m + jnp.log(den), lse_all)
            l_ref[sub * QB:(sub + 1) * QB, :] = lse_all
        for sub in range(2):
            for h in range(HEADS):
                u = sub * HEADS + h
                sl = slice(h * HD, (h + 1) * HD)
                o = jnp.dot(p_scr[u], keys(sub, v2_ref, vp_ref, sl), preferred_element_type=F32) * r_scr[u]
                o_ref[sub * QB:(sub + 1) * QB, sl] = o.astype(BF16)

    def pair(colblk):
        return pl.BlockSpec((2 * QB, D), lambda s: (s, colblk))

    def prev(colblk):
        return pl.BlockSpec((QB, D), lambda s: (jnp.maximum(2 * s - 1, 0), colblk))

    return pl.pallas_call(
        kern, grid=(nblk // 2,),
        in_specs=[pair(0), pair(1), prev(1), pair(2), prev(2), pl.BlockSpec((HEADS, QB, 2 * QB), lambda s: (0, 0, 0))],
        out_specs=[pl.BlockSpec((2 * QB, D), lambda s: (s, 0)), pl.BlockSpec((2 * QB, HD), lambda s: (s, 0))],
        out_shape=[jax.ShapeDtypeStruct((S, D), BF16), jax.ShapeDtypeStruct((S, HD), F32)],
        scratch_shapes=[pltpu.VMEM((2 * HEADS, QB, 2 * QB), F32), pltpu.VMEM((2 * HEADS, QB, 2 * QB), BF16),
                        pltpu.VMEM((2 * HEADS, QB, HD), F32)],
        compiler_params=_cparams(1), name=name)(qkv_f, qkv_f, qkv_f, qkv_f, qkv_f, bias)


def _natural(ref, scr, tm):
    dil = ref.shape[0]
    for res in range(dil):
        _chunks_add_rows(scr, ref[res].astype(F32), res, tm // dil, dil, False)
    return _chunks_get(scr)


def _attn_merge(os, lses):
    S = os[0].shape[0]
    tm = 512

    def kern(o0, o1, o2, l0, l1, l2, om_ref, lm_ref, ls1, ls2, os1, os2):
        la = l0[...]
        lb = _natural(l1, ls1, tm)
        lc = _natural(l2, ls2, tm)
        m = jnp.maximum(jnp.maximum(la, lb), lc)
        e0, e1, e2 = jnp.exp(la - m), jnp.exp(lb - m), jnp.exp(lc - m)
        tot = e0 + e1 + e2
        lm_ref[...] = m + jnp.log(tot)
        w0, w1, w2 = e0 / tot, e1 / tot, e2 / tot
        for res in range(o1.shape[0]):
            _chunks_add_rows(os1, o1[res].astype(F32), res, tm // o1.shape[0], o1.shape[0], False)
        for res in range(o2.shape[0]):
            _chunks_add_rows(os2, o2[res].astype(F32), res, tm // o2.shape[0], o2.shape[0], False)
        for h in range(HEADS):
            sl = slice(h * HD, (h + 1) * HD)
            acc = w0[:, h:h + 1] * o0[:, sl].astype(F32) + w1[:, h:h + 1] * os1[h] + w2[:, h:h + 1] * os2[h]
            om_ref[:, sl] = acc.astype(BF16)

    def spec(a, c):
        if a.ndim == 2:
            return pl.BlockSpec((tm, c), lambda i: (i, 0))
        return pl.BlockSpec((a.shape[0], tm // a.shape[0], c), lambda i: (0, i, 0))

    return pl.pallas_call(
        kern, grid=(S // tm,),
        in_specs=[spec(a, D) for a in os] + [spec(a, HD) for a in lses],
        out_specs=[pl.BlockSpec((tm, D), lambda i: (i, 0)), pl.BlockSpec((tm, HD), lambda i: (i, 0))],
        out_shape=[jax.ShapeDtypeStruct((S, D), BF16), jax.ShapeDtypeStruct((S, HD), F32)],
        scratch_shapes=[pltpu.VMEM((1, tm, HD), F32), pltpu.VMEM((1, tm, HD), F32),
                        pltpu.VMEM((HEADS, tm, HD), F32), pltpu.VMEM((HEADS, tm, HD), F32)],
        compiler_params=_cparams(1), name="attn_merge")(*os, *lses)


def _attn_bwd_prep(dx, wo, o, lse, deps=()):
    S = o.shape[0]
    tm = 512
    dils = DILS[1:]
    nd = len(deps)

    def kern(dx_ref, w_ref, o_ref, l_ref, *rest):
        rest = rest[nd:]
        do_outs, l_outs, d_outs = rest[0:3], rest[3:5], rest[5:8]
        do_scr, l_scr, d_scr = rest[8:11]
        dov = lax.dot_general(dx_ref[...].astype(BF16), w_ref[...], (((1,), (1,)), ((), ())),
                              preferred_element_type=F32)
        lane = lax.broadcasted_iota(jnp.int32, (tm, HD), 1)
        acc = jnp.zeros((tm, HD), F32)
        for h in range(HEADS):
            sl = slice(h * HD, (h + 1) * HD)
            prod = dov[:, sl] * o_ref[:, sl].astype(F32)
            acc = jnp.where(lane == h, jnp.sum(prod, axis=-1, keepdims=True), acc)
        d_scr[0] = acc
        l_scr[0] = l_ref[...]
        _chunks_put(do_scr, dov)
        do_outs[0][...] = dov.astype(BF16)
        d_outs[0][...] = acc
        for j, dil in enumerate(dils):
            for res in range(dil):
                n = tm // dil
                do_outs[1 + j][res] = _chunks_rows(do_scr, res, n, dil).astype(BF16)
                l_outs[j][res] = _chunks_rows(l_scr, res, n, dil)
                d_outs[1 + j][res] = _chunks_rows(d_scr, res, n, dil)

    def nat(c):
        return pl.BlockSpec((tm, c), lambda i: (i, 0))

    def fol(dil, c):
        return pl.BlockSpec((dil, tm // dil, c), lambda i: (0, i, 0))

    def shapes(c, dt, with_natural):
        first = [jax.ShapeDtypeStruct((S, c), dt)] if with_natural else []
        return first + [jax.ShapeDtypeStruct((dil, S // dil, c), dt) for dil in dils]

    outs = pl.pallas_call(
        kern, grid=(S // tm,),
        in_specs=[nat(D), pl.BlockSpec((D, D), lambda i: (0, 0), pipeline_mode=pl.Buffered(1)), nat(D), nat(HD)]
        + [pl.BlockSpec(memory_space=pl.ANY)] * nd,
        out_specs=[nat(D)] + [fol(dil, D) for dil in dils] + [fol(dil, HD) for dil in dils]
        + [nat(HD)] + [fol(dil, HD) for dil in dils],
        out_shape=shapes(D, BF16, True) + shapes(HD, F32, False) + shapes(HD, F32, True),
        scratch_shapes=[pltpu.VMEM((HEADS, tm, HD), F32), pltpu.VMEM((1, tm, HD), F32), pltpu.VMEM((1, tm, HD), F32)],
        compiler_params=_cparams(1), name="attn_out_bwd")(dx, wo, o, lse, *deps)
    return outs[0:3], [lse] + list(outs[3:5]), outs[5:8]


def _attn_bwd(qkv_f, do_f, lse_f, delta_f, bias, nb, name):
    S = qkv_f.shape[0]
    nblk = S // QB
    scale = HD ** -0.5

    npair = nblk // 2

    def kern(q_ref, k2_ref, kp_ref, v2_ref, vp_ref, do_ref, l_ref, d_ref, b_ref, out_ref, dq_c, dk_c, dv_c,
             s_scr, dp_scr, ds_scr, p_scr):
        s_id = pl.program_id(0)

        @pl.when(s_id == 0)
        def _():
            dq_c[...] = jnp.zeros_like(dq_c)
            dk_c[...] = jnp.zeros_like(dk_c)
            dv_c[...] = jnp.zeros_like(dv_c)

        @pl.when(s_id == npair)
        def _():
            out_ref[:, 0:D] = dq_c[...].astype(BF16)
            out_ref[:, D:2 * D] = dk_c[...].astype(BF16)
            out_ref[:, 2 * D:3 * D] = dv_c[...].astype(BF16)

        def keys(sub, cur2_ref, prev_ref, sl):
            if sub:
                return cur2_ref[:, sl]
            return jnp.concatenate([prev_ref[:, sl], cur2_ref[0:QB, sl]], axis=0)

        @pl.when(s_id < npair)
        def _():
            col = lax.broadcasted_iota(jnp.int32, (QB, 2 * QB), 1)
            out_ref[:, 0:D] = dq_c[...].astype(BF16)
            for sub in range(2):
                rows = slice(sub * QB, (sub + 1) * QB)
                for h in range(HEADS):
                    sl = slice(h * HD, (h + 1) * HD)
                    u = sub * HEADS + h
                    s_scr[u] = lax.dot_general(q_ref[rows, sl], keys(sub, k2_ref, kp_ref, sl),
                                               (((1,), (1,)), ((), ())), preferred_element_type=F32)
                    dp_scr[u] = lax.dot_general(do_ref[rows, sl], keys(sub, v2_ref, vp_ref, sl),
                                                (((1,), (1,)), ((), ())), preferred_element_type=F32)
            for sub in range(2):
                rows = slice(sub * QB, (sub + 1) * QB)
                has_prev = jnp.bitwise_and(2 * s_id + sub, nb - 1) != 0
                dead = jnp.logical_and(col < QB, jnp.logical_not(has_prev))
                lv = l_ref[rows, :]
                dv_ = d_ref[rows, :]
                for h in range(HEADS):
                    u = sub * HEADS + h
                    s = s_scr[u] * scale + b_ref[h]
                    s = jnp.where(dead, NEG, s)
                    p = jnp.exp(s - lv[:, h:h + 1])
                    ds_scr[u] = (p * (dp_scr[u] - dv_[:, h:h + 1]) * scale).astype(BF16)
                    p_scr[u] = p.astype(BF16)
            for h in range(HEADS):
                sl = slice(h * HD, (h + 1) * HD)
                parts = []
                for sub in range(2):
                    rows = slice(sub * QB, (sub + 1) * QB)
                    u = sub * HEADS + h
                    ds = ds_scr[u]
                    dq_c[rows, sl] = jnp.dot(ds, keys(sub, k2_ref, kp_ref, sl), preferred_element_type=F32)
                    dkk = lax.dot_general(ds, q_ref[rows, sl], (((0,), (0,)), ((), ())), preferred_element_type=F32)
                    dvv = lax.dot_general(p_scr[u], do_ref[rows, sl], (((0,), (0,)), ((), ())),
                                          preferred_element_type=F32)
                    parts.append((dkk, dvv))
                for which, carry, base in ((0, dk_c, D), (1, dv_c, 2 * D)):
                    first, second = parts[0][which], parts[1][which]
                    cols = slice(base + h * HD, base + (h + 1) * HD)
                    out_ref[0:QB, cols] = carry[0:QB, sl].astype(BF16)
                    out_ref[QB:2 * QB, cols] = (carry[QB:2 * QB, sl] + first[:QB]).astype(BF16)
                    carry[0:QB, sl] = first[QB:] + second[:QB]
                    carry[QB:2 * QB, sl] = second[QB:]

    last = npair - 1

    def pair(colblk, c):
        return pl.BlockSpec((2 * QB, c), lambda s: (jnp.minimum(s, last), colblk))

    def prev(colblk):
        return pl.BlockSpec((QB, D), lambda s: (jnp.maximum(2 * jnp.minimum(s, last) - 1, 0), colblk))

    return pl.pallas_call(
        kern, grid=(npair + 1,),
        in_specs=[pair(0, D), pair(1, D), prev(1), pair(2, D), prev(2), pair(0, D), pair(0, HD), pair(0, HD),
                  pl.BlockSpec((HEADS, QB, 2 * QB), lambda s: (0, 0, 0))],
        out_specs=pl.BlockSpec((2 * QB, 3 * D), lambda s: (jnp.maximum(s - 1, 0), 0)),
        out_shape=jax.ShapeDtypeStruct((S, 3 * D), BF16),
        scratch_shapes=[pltpu.VMEM((2 * QB, D), F32), pltpu.VMEM((2 * QB, D), F32), pltpu.VMEM((2 * QB, D), F32),
                        pltpu.VMEM((2 * HEADS, QB, 2 * QB), F32), pltpu.VMEM((2 * HEADS, QB, 2 * QB), F32),
                        pltpu.VMEM((2 * HEADS, QB, 2 * QB), BF16), pltpu.VMEM((2 * HEADS, QB, 2 * QB), BF16)],
        compiler_params=_cparams(1), name=name)(qkv_f, qkv_f, qkv_f, qkv_f, qkv_f, do_f, lse_f, delta_f, bias)


def _local_step(x, tgt, comm, attn_norm, ffn_norm, final_norm, opt=None, early=None):
    S = x.shape[0]
    bias = _bias_table()
    g_attn = attn_norm.reshape(1, D)
    g_f0 = ffn_norm[0:1]
    g_f1 = ffn_norm[1:2]
    g_fin = final_norm.reshape(1, D)
    W = {}

    def ffn_fwd(xin, h, l, next_gain, target=None):
        return _ffn_fwd(h, W[f"gu{l}"], W[f"d{l}"], xin, next_gain, tgt=target, name=f"ffn_fwd{l}")

    def ffn_bwd(dxo, xin, gain, h, gu, act, l, rs_group):
        dgu, dxin, dgain = _ffn_bwd(dxo, W[f"d{l}"], W[f"gu{l}"], gu, xin, gain, f"ffn_bwd{l}")
        side_d = side_gu = None
        if l == 0 and opt is not None:
            got = comm.received(0, dgu)
            side_d, side_gu = (got["d1"], opt["d"], 1, 4), (got["gu1"], opt["gu"], 1, 8)
        gw_d = _mm(act, dxo, mode="tn", M=DFF, N=D, K=S, tm=HCH, tn=D, tk=2048 if side_d is None else 1024,
                   out_dtype=BF16, name=f"gw_d{l}", side=side_d)
        gw_gu = _mm(dgu, h, mode="tn", M=2 * DFF, N=D, K=S, tm=HCH, tn=D, tk=2048, out_dtype=BF16, name=f"gw_gu{l}",
                    side=side_gu)
        if side_d is not None:
            (gw_d, early["d"]), (gw_gu, early["gu"]) = gw_d, gw_gu
        return dxin, dgain, comm.send_grads(rs_group, {f"d{l}": gw_d, f"gu{l}": gw_gu})

    nbs = [S // QB // dil for dil in DILS]
    hf = _rms_fwd_folded(x, g_attn, "rms_attn", deps=comm.ag_tokens)
    hf = [h.reshape(S, D) for h in hf]
    W.update(comm.weights(0, hf[0]))
    qkv_f, o_f, lse_f = [], [], []
    for g, dil in enumerate(DILS):
        qkv_f.append(_mm(hf[g], W["qkv"], mode="nt", M=S, N=3 * D, K=D, tm=2048, tn=1024, tk=D, out_dtype=BF16,
                         b_off=(3 * g, 0), name=f"qkv_proj{g}"))
        og, lg = _attn_fwd(qkv_f[g], bias[g], nbs[g], f"attn_fwd{g}")
        o_f.append(og if dil == 1 else og.reshape(dil, S // dil, D))
        lse_f.append(lg if dil == 1 else lg.reshape(dil, S // dil, HD))
    passing = [comm.pass_on(1, tuple(o_f)), comm.pass_on(2, tuple(o_f))]
    (o_f, lse_f), passing = lax.optimization_barrier(((o_f, lse_f), passing))
    o, lse = _attn_merge(o_f, lse_f)
    W.update(comm.weights(1, (o, passing[0])))
    x1, h1 = _mm_res_norm(o, W["wo"], x, g_f0, K=D, tm=1024, name="attn_out")
    pv = W["pv"].reshape(NDEV, 8, 128)
    pool_norm, pool_scale = pv[:, 0, :].reshape(1, D), pv[:, 1, :].reshape(1, D)
    gu0, act0, (x2, h2) = ffn_fwd(x1, h1, 0, pool_norm)

    W.update(comm.weights(2, (x2, passing[1])))
    yd = _pool_in_window(h2, W["wpi"])
    x3 = _pool_out(yd, W["pg"], pool_scale, x2)
    h3 = _rms_fwd(x3, g_f1, "rms_ffn1")
    gu1, act1, (dx4, d_fin, lossvec) = ffn_fwd(x3, h3, 1, g_fin, target=tgt)

    dx3, d_f1, token = ffn_bwd(dx4, x3, g_f1, h3, gu1, act1, 1, 0)
    du, d_scale, gw_pg = _pool_out_bwd(dx3, yd, W["pg"], pool_scale, deps=(token,))
    gw_pi = _mm(h2, du, mode="tn", M=D, N=D, K=S, tm=D, tn=D, tk=S, out_dtype=BF16, name="gw_pi")
    token = comm.send_grads(1, {"pg": gw_pg, "wpi": gw_pi})
    dx2, d_pool = _mm_rms_bwd(du, W["wpi"], x2, pool_norm, dx3, mode="nt", M=S, K=D, tm=1024, deps=(token,),
                              name="pool_in_bwd")
    dx1, d_f0, token = ffn_bwd(dx2, x1, g_f0, h1, gu0, act0, 0, 2)

    gw_o = _mm(o, dx1, mode="tn", M=D, N=D, K=S, tm=D, tn=D, tk=2048, out_dtype=BF16, deps=(token,), name="gw_o")
    do_f, lse_ff, delta_f = _attn_bwd_prep(dx1, W["wo"], o, lse, deps=(token,))
    dqkv_f, gw_qkv = [], None
    for g in range(NGROUPS):
        dqkv_f.append(_attn_bwd(qkv_f[g], do_f[g].reshape(S, D), lse_ff[g].reshape(S, HD),
                                delta_f[g].reshape(S, HD), bias[g], nbs[g], f"attn_bwd{g}"))
        gw_qkv = _mm(dqkv_f[g], hf[g], mode="tn", M=3 * D, N=D, K=S, tm=1024, tn=D, tk=S, out_dtype=BF16,
                     out_rows=NGROUPS * 3 * D, out_off=3 * g, out_prev=gw_qkv, name=f"gw_qkv{g}")
    token = comm.send_grads_pairwise({"wo": gw_o, "qkv": gw_qkv})
    folded = []
    for g in reversed(range(1, NGROUPS)):
        dh0_g = _mm(dqkv_f[g], W["qkv"], mode="nn", M=S, N=D, K=3 * D, tm=1024, tn=D, tk=3 * D, out_dtype=F32,
                    b_off=(g, 0), deps=(token,), name=f"qkv_proj_bwd{g}")
        folded.append(dh0_g.reshape(DILS[g], S // DILS[g], D))
        if g == NGROUPS - 1:
            token = comm.pass_grads(dh0_g)
    grad_x, d_attn = _mm_rms_bwd(dqkv_f[0], W["qkv"], x, g_attn, dx1, mode="nn", M=S, K=3 * D, tm=512,
                                 deps=(token,), folded=folded, name="qkv_proj_bwd0")

    vec = jnp.concatenate([d_attn, d_f0, d_f1, d_fin, d_pool, d_scale, lossvec, jnp.zeros((1, D), F32)], axis=0)
    return grad_x, vec


def _mesh_pos():
    x, y, c = lax.axis_index("x"), lax.axis_index("y"), lax.axis_index("c")
    return x, y, c, 4 * x + 2 * y + c


def _peer(x, y, c, k):
    kx, ky, kc = (k >> 2) & 1, (k >> 1) & 1, k & 1
    px = 1 - x if kx else x
    py = 1 - y if ky else y
    pc = 1 - c if kc else c
    return (px, py, pc), 4 * px + 2 * py + pc


ANY = pl.BlockSpec(memory_space=pl.ANY)


HBM = pl.BlockSpec(memory_space=pltpu.HBM)
SEMS = pl.BlockSpec(memory_space=pltpu.SEMAPHORE)
EFFECT = pltpu.SideEffectType.DATAFLOW_SIDE_EFFECTING

AG_GROUPS = (("qkv",), ("wo", "gu0", "d0", "pv"), ("wpi", "pg", "gu1", "d1"))
AG_ORDER = tuple(n for grp in AG_GROUPS for n in grp)
RS_GROUPS = (("d1", "gu1"), ("pg", "wpi"), ("d0", "gu0"), ("wo", "qkv"))


def _hbm(a):
    return pltpu.with_memory_space_constraint(a, pltpu.HBM)


def _remote(src, dst, send, recv, peer):
    return pltpu.make_async_remote_copy(src_ref=src, dst_ref=dst, send_sem=send, recv_sem=recv, device_id=peer,
                                        device_id_type=pl.DeviceIdType.MESH)


ALL_KS = tuple(range(1, NDEV))
AG_KS1 = (1, 2, 4, 6)
AG_KS2 = (2, 4, 6)
RS_KS_PAIR = (1, 3, 5, 7)
RS_KS_CHIPS = (2, 4, 6)


def _split_start(srcs, src_of, lands, copy_refs, name, deps=(), ks=ALL_KS, to=None):
    ns, n, nd, nk = len(srcs), len(lands), len(deps), len(ks)

    def body(*refs):
        ins, land = refs[:ns], refs[ns:ns + n]
        send, recv = refs[ns + n + nd], refs[ns + n + nd + 1]
        token = refs[-1]
        x, y, c, me = _mesh_pos()
        for j in range(n):
            for i, k in enumerate(ks):
                _, pid = _peer(x, y, c, k)
                dest, _ = _peer(x, y, c, k if to is None else to)
                src, dst = copy_refs(j, (land[j] if src_of[j] is None else ins[src_of[j]]), land[j], me, pid, i)
                _remote(src, dst, send.at[j * nk + i], recv.at[j * nk + i], dest).start()
        token[...] = jnp.zeros_like(token)

    outs = pl.pallas_call(
        body, name=name,
        out_shape=(pltpu.SemaphoreType.DMA((n * nk,)), pltpu.SemaphoreType.DMA((n * nk,)))
        + tuple(pltpu.HBM(a.shape, a.dtype) for a in srcs) + tuple(pltpu.HBM(a.shape, a.dtype) for a in lands)
        + (jax.ShapeDtypeStruct((8, 128), F32),),
        in_specs=(HBM,) * (ns + n) + (ANY,) * nd,
        out_specs=(SEMS, SEMS) + (HBM,) * (ns + n) + (pl.BlockSpec(memory_space=pltpu.VMEM),),
        input_output_aliases={i: 2 + i for i in range(ns + n)},
        compiler_params=pltpu.CompilerParams(has_side_effects=EFFECT),
    )(*[_hbm(a) for a in srcs], *[_hbm(a) for a in lands], *deps)
    return outs[0], outs[1], list(outs[2:2 + ns]), list(outs[2 + ns:2 + ns + n]), outs[-1]


def _split_wait(srcs, src_of, lands, send, recv, sem_rows, wait_refs, after, name, ks=ALL_KS):
    ns, n, nk = len(srcs), len(lands), len(ks)
    after = tuple(after) if isinstance(after, (tuple, list)) else (after,)

    def body(*refs):
        ins, land = refs[:ns], refs[ns:ns + n]
        send_ref, recv_ref = refs[ns + n], refs[ns + n + 1]
        x, y, c, me = _mesh_pos()
        for j in range(n):
            for i, k in enumerate(ks):
                peer, _ = _peer(x, y, c, k)
                src, dst = wait_refs(j, (land[j] if src_of[j] is None else ins[src_of[j]]), land[j])
                sem = sem_rows[j] * nk + i
                cp = _remote(src, dst, send_ref.at[sem], recv_ref.at[sem], peer)
                cp.wait_send()
                cp.wait_recv()

    outs = pl.pallas_call(
        body, name=name,
        out_shape=tuple(pltpu.HBM(a.shape, a.dtype) for a in srcs) + tuple(pltpu.HBM(a.shape, a.dtype) for a in lands),
        in_specs=(HBM,) * (ns + n) + (SEMS, SEMS) + (ANY,) * len(after),
        out_specs=(HBM,) * (ns + n),
        input_output_aliases={i: i for i in range(ns + n)},
        compiler_params=pltpu.CompilerParams(has_side_effects=EFFECT),
    )(*srcs, *lands, send, recv, *after)
    return list(outs[:ns]), list(outs[ns:])


def _ag_dtype(name):
    return F32 if name == "pv" else BF16


def _ag_align(name):
    return 8 if name == "pv" else 16


def _place_transposed(w, me, name):
    rows = w.shape[1]
    nblk = rows // 128

    def kern(me_ref, w_ref, o_ref):
        o_ref[...] = w_ref[...].T.astype(BF16)

    grid_spec = pltpu.PrefetchScalarGridSpec(
        num_scalar_prefetch=1, grid=(nblk,),
        in_specs=[pl.BlockSpec((D, 128), lambda i, me_ref: (0, i))],
        out_specs=pl.BlockSpec((128, D), lambda i, me_ref: (me_ref[0] * nblk + i, 0)))
    return pl.pallas_call(
        kern, grid_spec=grid_spec, out_shape=jax.ShapeDtypeStruct((NDEV * rows, D), BF16),
        compiler_params=_cparams(1), name=name)(me.reshape(1).astype(jnp.int32), w)


class _Comm:
    def __init__(self, params, make_shards, me, placed):
        self.me = me
        self.ag_land, self.ag_sems, self.ag_tokens, self.ag_passing = {}, {}, (), {}
        self.rs = []
        deps = ()
        for part, names in enumerate((AG_GROUPS[0], AG_ORDER[len(AG_GROUPS[0]):])):
            rows = [SEC_ROWS[n] for n in names]
            if part == 0:
                lands = [placed[n] for n in names]
            else:
                params, deps = lax.optimization_barrier((params, deps))
                shards = make_shards(*params)
                lands = [lax.dynamic_update_slice(lax.empty((NDEV * r, shards[n].shape[1]), _ag_dtype(n)),
                                                  shards[n].astype(_ag_dtype(n)), (_shard_pos(n, me), 0))
                         for n, r in zip(names, rows)]

            def copy_refs(j, src, land, me, pid, i, names=names, rows=rows):
                own = land.at[pl.ds(pl.multiple_of(_shard_pos(names[j], me), _ag_align(names[j])), rows[j])]
                return own, own

            send, recv, _, lands, token = _split_start([], [None] * len(names), lands, copy_refs, f"ag_start{part}",
                                                       deps=deps, ks=AG_KS1)
            deps = (token,)
            self.ag_tokens += (token,)
            for j, n in enumerate(names):
                self.ag_land[n] = lands[j]
                self.ag_sems[n] = (send, recv, j)

    def pass_on(self, group, after):
        names = AG_GROUPS[group]
        send, recv = self.ag_sems[names[0]][:2]
        idx = [self.ag_sems[n][2] for n in names]
        rows = [SEC_ROWS[n] for n in names]
        none = [None] * len(names)

        def wait_refs(j, src, land):
            return land.at[pl.ds(0, rows[j])], land.at[pl.ds(0, rows[j])]

        _, lands = _split_wait([], none, [self.ag_land[n] for n in names], send, recv, idx,
                               wait_refs, after, f"ag_wait{group}", ks=AG_KS1)

        def copy_refs(j, src, land, me, pid, i):
            theirs = land.at[pl.ds(pl.multiple_of(_shard_pos(names[j], pid), _ag_align(names[j])), rows[j])]
            return theirs, theirs

        send, recv, _, lands, token = _split_start([], none, lands, copy_refs, f"ag_pass{group}", ks=AG_KS2, to=1)
        self.ag_passing[group] = (send, recv, lands, wait_refs)
        return token

    def weights(self, group, after):
        names = AG_GROUPS[group]
        if group not in self.ag_passing:
            after = self.pass_on(group, after)
        send, recv, lands, wait_refs = self.ag_passing[group]
        _, lands = _split_wait([], [None] * len(names), lands, send, recv, list(range(len(names))), wait_refs, after,
                               f"ag_pass_wait{group}", ks=AG_KS2)
        return dict(zip(names, lands))

    def send_grads(self, group, gws):
        names = RS_GROUPS[group]
        rows = [SEC_ROWS[n] for n in names]
        grads = [gws[n] for n in names]
        me = self.me
        lands = [lax.dynamic_update_slice(
            lax.empty((NDEV, r, D), BF16),
            lax.dynamic_slice(g, (_shard_pos(n, me), 0), (r, D))[None], (me, 0, 0))
            for n, r, g in zip(names, rows, grads)]

        def copy_refs(j, src, land, me, pid, i):
            return src.at[pl.ds(pl.multiple_of(_shard_pos(names[j], pid), 16), rows[j])], land.at[me]

        send, recv, srcs, lands, token = _split_start(grads, list(range(len(names))), lands, copy_refs,
                                                      f"rs_start{group}")
        self.rs.append((names, rows, send, recv, srcs, lands, ALL_KS))
        return token

    def send_grads_pairwise(self, gws):
        names = RS_GROUPS[-1]
        rows = [SEC_ROWS[n] for n in names]
        grads = [gws[n] for n in names]
        idx = list(range(len(names)))
        lands = [lax.empty((len(RS_KS_PAIR), r, D), BF16) for r in rows]

        def copy_refs(j, src, land, me, pid, i):
            return src.at[pl.ds(pl.multiple_of(_shard_pos(names[j], pid), 16), rows[j])], land.at[i]

        send, recv, srcs, lands, token = _split_start(grads, idx, lands, copy_refs, "rs_pair_start",
                                                      ks=RS_KS_PAIR, to=1)
        self.pair = (names, rows, send, recv, srcs, lands)
        return token

    def pass_grads(self, after):
        names, rows, send, recv, srcs, lands = self.pair
        idx = list(range(len(names)))
        me = self.me

        def wait_refs(j, src, land):
            return src.at[pl.ds(0, rows[j])], land.at[0]

        srcs, lands = _split_wait(srcs, idx, lands, send, recv, idx, wait_refs, after, "rs_pair_wait", ks=RS_KS_PAIR)
        sums = [_pair_sum(g, got, me, f"rs_pair_sum_{n}") for n, g, got in zip(names, srcs, lands)]
        lands = [lax.dynamic_update_slice(lax.empty(p.shape, BF16), p[0:1], (0, 0, 0)) for p in sums]

        def copy_refs(j, src, land, me, pid, i):
            return src.at[i + 1], land.at[i + 1]

        send, recv, sums, lands, token = _split_start(sums, idx, lands, copy_refs, f"rs_start{len(RS_GROUPS) - 1}",
                                                      ks=RS_KS_CHIPS)
        self.rs.append((names, rows, send, recv, sums, lands, RS_KS_CHIPS))
        return token

    def received(self, group, after):
        names, rows, send, recv, srcs, lands, ks = self.rs[group]
        whole = srcs[0].ndim == 2

        def wait_refs(j, src, land):
            return (src.at[pl.ds(0, rows[j])] if whole else src.at[0]), land.at[0]

        _, lands = _split_wait(srcs, list(range(len(names))), lands, send, recv, list(range(len(names))), wait_refs,
                               after, f"rs_wait{group}", ks=ks)
        return dict(zip(names, lands))


def _pair_sum(grad, got, me, name):
    n, rows, _ = got.shape
    tr = 384 if rows % 384 == 0 else rows
    nt = rows // tr

    def kern(me_ref, g_ref, b_ref, o_ref):
        o_ref[0] = (g_ref[...].astype(F32) + b_ref[0].astype(F32)).astype(BF16)

    blk = pl.BlockSpec((1, tr, D), lambda i, t, me_ref: (i, t, 0))
    grid_spec = pltpu.PrefetchScalarGridSpec(
        num_scalar_prefetch=1, grid=(n, nt),
        in_specs=[pl.BlockSpec((tr, D), lambda i, t, me_ref: (jnp.bitwise_xor(me_ref[0], 2 * i) * nt + t, 0)), blk],
        out_specs=blk)
    return pl.pallas_call(
        kern, grid_spec=grid_spec, out_shape=jax.ShapeDtypeStruct(got.shape, BF16),
        compiler_params=_cparams(2), name=name)(me.reshape(1).astype(jnp.int32), grad, got)


def _sum_contributions(r_ref):
    g = r_ref[0].astype(F32)
    for slot in range(1, r_ref.shape[0]):
        g = g + r_ref[slot].astype(F32)
    return g


def _adam_math(g, w, m, v):
    c1 = 1.0 / (1.0 - ADAM_B1 ** ADAM_STEP)
    c2 = 1.0 / (1.0 - ADAM_B2 ** ADAM_STEP)
    mn = ADAM_B1 * m + (1.0 - ADAM_B1) * g
    vn = ADAM_B2 * v + (1.0 - ADAM_B2) * (g * g)
    return -ADAM_LR * ((mn * c1) / (jnp.sqrt(vn * c2) + ADAM_EPS) + ADAM_WD * w), mn, vn


def _adamw(R, w, m, v, *, tr, name, layer=None, prev=None):
    rows, C = w.shape[-2:]
    nprev = 0 if prev is None else 4

    def kern(r_ref, w_ref, m_ref, v_ref, *rest):
        g_out, d_out, m_out, v_out = rest[nprev:]
        g = _sum_contributions(r_ref)
        g_out[...] = g
        d_out[...], m_out[...], v_out[...] = _adam_math(g, w_ref[...], m_ref[...], v_ref[...])

    if layer is None:
        tile = pl.BlockSpec((tr, C), lambda i: (i, 0))
    else:
        tile = pl.BlockSpec((None, tr, C), lambda i: (layer, i, 0))
    shp = jax.ShapeDtypeStruct(w.shape, F32)
    return pl.pallas_call(
        kern, grid=(rows // tr,),
        in_specs=[pl.BlockSpec((R.shape[0], tr, C), lambda i: (0, i, 0)), tile, tile, tile]
        + [pl.BlockSpec(memory_space=pl.ANY)] * nprev,
        out_specs=[tile] * 4, out_shape=[shp] * 4,
        input_output_aliases={4 + k: k for k in range(nprev)},
        compiler_params=_cparams(1), name=name)(R, w, m, v, *(prev or ()))


def _adamw_pool_group(R, w, m, v):
    rows = SEC_ROWS["pg"]

    def kern(r_ref, w_ref, m_ref, v_ref, g_out, d_out, m_out, v_out):
        g = _sum_contributions(r_ref)
        g_out[0] = g
        d_out[0], m_out[0], v_out[0] = _adam_math(g, w_ref[0], m_ref[0], v_ref[0])

    blk = pl.BlockSpec((1, rows, PGD), lambda i: (i, 0, 0))
    shp = jax.ShapeDtypeStruct((POOL_G, rows, PGD), F32)
    return pl.pallas_call(
        kern, grid=(POOL_G,),
        in_specs=[pl.BlockSpec((NDEV, rows, PGD), lambda i: (0, 0, i)), blk, blk, blk],
        out_specs=[blk] * 4, out_shape=[shp] * 4, compiler_params=_cparams(1), name="adamw_pg")(R, w, m, v)


def _adamw_transposed(R, w, m, v, name):
    rows = R.shape[1]
    tr = 128

    def kern(r_ref, w_ref, m_ref, v_ref, g_out, d_out, m_out, v_out):
        g = _sum_contributions(r_ref).T
        g_out[...] = g
        d_out[...], m_out[...], v_out[...] = _adam_math(g, w_ref[...], m_ref[...], v_ref[...])

    tile = pl.BlockSpec((D, tr), lambda i: (0, i))
    shp = jax.ShapeDtypeStruct((D, rows), F32)
    return pl.pallas_call(
        kern, grid=(rows // tr,),
        in_specs=[pl.BlockSpec((R.shape[0], tr, D), lambda i: (0, i, 0)), tile, tile, tile],
        out_specs=[tile] * 4, out_shape=[shp] * 4, compiler_params=_cparams(1), name=name)(R, w, m, v)


def _pack_sections(w_qkv, w_attn_out, w_pool_in, w_pool_group, w_ffn_gate_up, w_ffn_down):
    pg = w_pool_group[0].transpose(1, 0, 2).reshape(SEC_ROWS["pg"], D)
    return {"qkv": w_qkv[0].T, "wo": w_attn_out[0], "wpi": w_pool_in[0], "gu0": w_ffn_gate_up[0].T,
            "gu1": w_ffn_gate_up[1].T, "d0": w_ffn_down[0], "d1": w_ffn_down[1], "pg": pg}


def _vec_pack(attn_norm, ffn_norm, final_norm, pool_norm_sh, pool_scale_sh, me):
    def place(sh):
        return lax.dynamic_update_slice(jnp.zeros((1, D), F32), sh, (0, me * 128))
    return jnp.concatenate([attn_norm, ffn_norm, final_norm.reshape(1, D), place(pool_norm_sh),
                            place(pool_scale_sh), jnp.zeros((2, D), F32)], axis=0)


def _vec_unpack(p, me):
    def take(r):
        return lax.dynamic_slice(p[r:r + 1], (0, me * 128), (1, 128))
    return p[0:1], p[1:3], p[3], take(4), take(5)


def kernel(x, attn_norm, w_qkv, w_attn_out, pool_norm, w_pool_in, w_pool_group, pool_scale, ffn_norm, w_ffn_gate_up, w_ffn_down, final_norm, loss_target, m_attn_norm, m_w_qkv, m_w_attn_out, m_pool_norm, m_w_pool_in, m_w_pool_group, m_pool_scale, m_ffn_norm, m_w_ffn_gate_up, m_w_ffn_down, m_final_norm, v_attn_norm, v_w_qkv, v_w_attn_out, v_pool_norm, v_w_pool_in, v_w_pool_group, v_pool_scale, v_ffn_norm, v_w_ffn_gate_up, v_w_ffn_down, v_final_norm):
    me = 4 * lax.axis_index("x") + 2 * lax.axis_index("y") + lax.axis_index("c")

    def make_shards(wq, wo, wpi, wpg, wgu, wd, pn, ps):
        shards = _pack_sections(wq, wo, wpi, wpg, wgu, wd)
        shards["pv"] = jnp.concatenate([pn, ps, jnp.zeros((6, 128), F32)], axis=0)
        return shards

    comm = _Comm((w_qkv, w_attn_out, w_pool_in, w_pool_group, w_ffn_gate_up, w_ffn_down, pool_norm, pool_scale),
                 make_shards, me, placed={"qkv": _place_transposed(w_qkv[0], me, "place_qkv")})

    gu_t = [jnp.swapaxes(a, 1, 2) for a in (w_ffn_gate_up, m_w_ffn_gate_up, v_w_ffn_gate_up)]
    early = {}
    grad_x, vec = _local_step(x[0], loss_target[0], comm, attn_norm, ffn_norm, final_norm,
                              opt={"d": (w_ffn_down, m_w_ffn_down, v_w_ffn_down), "gu": gu_t}, early=early)

    small = ((attn_norm, ffn_norm, final_norm, pool_norm, pool_scale),
             (m_attn_norm, m_ffn_norm, m_final_norm, m_pool_norm, m_pool_scale),
             (v_attn_norm, v_ffn_norm, v_final_norm, v_pool_norm, v_pool_scale))
    small, grad_x = lax.optimization_barrier((small, grad_x))
    vw, vm, vv = (_vec_pack(*s, me) for s in small)

    res = {}
    gu_res, d_res = early["gu"], early["d"]
    vec_out = None
    vec_land = lax.dynamic_update_slice(lax.empty((NDEV, 8, D), F32), vec[None], (me, 0, 0))
    vec_sems = _split_start([vec], [0], [vec_land], lambda j, src, land, me_, pid, i: (src, land.at[me_]),
                            "vec_start")
    after = (grad_x, vec_sems[4])
    for group in range(1, len(RS_GROUPS)):
        if group == len(RS_GROUPS) - 1:
            _, (VR,) = _split_wait(vec_sems[2], [0], vec_sems[3], vec_sems[0], vec_sems[1], [0],
                                   lambda j, src, land: (src, land.at[0]), after, "vec_wait")
            vec_out = _adamw(VR, vw, vm, vv, tr=8, name="adamw_vec")
            after = vec_out[0]
        for n, R in comm.received(group, after).items():
            if n in ("d0", "d1"):
                d_res = _adamw(R, w_ffn_down, m_w_ffn_down, v_w_ffn_down, tr=352, name=f"adamw_{n}",
                               layer=int(n[1]), prev=d_res)
                after = d_res[0]
            elif n in ("gu0", "gu1"):
                gu_res = _adamw(R, *gu_t, tr=352, name=f"adamw_{n}", layer=int(n[2]), prev=gu_res)
                after = gu_res[0]
            elif n == "pg":
                out = _adamw_pool_group(R, w_pool_group[0], m_w_pool_group[0], v_w_pool_group[0])
                res["pg"] = tuple(a[None] for a in out)
                after = out[0]
            elif n in ("wo", "wpi"):
                w, m, v = ((w_attn_out, m_w_attn_out, v_w_attn_out) if n == "wo"
                           else (w_pool_in, m_w_pool_in, v_w_pool_in))
                res[n] = _adamw(R, w[0], m[0], v[0], tr=128, name=f"adamw_{n}")
                res[n] = tuple(a[None] for a in res[n])
                after = res[n][0]
            else:
                out = _adamw_transposed(R, w_qkv[0], m_w_qkv[0], v_w_qkv[0], "adamw_qkv")
                res["qkv"] = tuple(a[None] for a in out)
                after = out[0]
    res["gu"] = tuple(jnp.swapaxes(a, 1, 2) for a in gu_res)
    res["d"] = tuple(d_res)

    outs = []
    for kind in range(4):
        an, fn, fin, pn, ps = _vec_unpack(vec_out[kind], me)
        outs.append((an, res["qkv"][kind], res["wo"][kind], pn, res["wpi"][kind], res["pg"][kind], ps, fn,
                     res["gu"][kind], res["d"][kind], fin))
    loss = 0.5 * jnp.sum(vec_out[0][6]) / D
    return (loss, grad_x[None]) + outs[0] + outs[1] + outs[2] + outs[3]
```

```python
import jax
import jax.numpy as jnp
from jax import lax
from jax.experimental import pallas as pl
from jax.experimental.pallas import tpu as pltpu

F32 = jnp.float32
BF16 = jnp.bfloat16

D = 1024
NDEV = 8
HEADS = 8
HD = 128
QB = 128
NGROUPS = 3
DILS = (1, 4, 16)
DFF = 2816
HCH = 1408
POOL_G = 4
PGD = 256
RMS_EPS = 1e-6
NEG = -1e30

ADAM_LR = 0.001
ADAM_B1 = 0.9
ADAM_B2 = 0.999
ADAM_EPS = 1e-08
ADAM_WD = 0.01
ADAM_STEP = 10

VMEM_LIMIT = 52 * 1024 * 1024

SECTIONS = (("qkv", 1152), ("wo", 128), ("wpi", 128), ("gu0", 704), ("gu1", 704),
            ("d0", 352), ("d1", 352), ("pg", 32))
SEC_ROWS = dict(SECTIONS)
SEC_ROWS["pv"] = 8


def _cparams(n_grid):
    return pltpu.CompilerParams(dimension_semantics=("arbitrary",) * n_grid, vmem_limit_bytes=VMEM_LIMIT)


def _shard_pos(name, dev):
    n = SEC_ROWS[name]
    if name in ("gu0", "gu1"):
        return ((dev % 4) // 2) * (2 * HCH) + (dev // 4) * HCH + (dev % 2) * n
    return dev * n


def _mm(a, b, *, mode, M, N, K, tm, tn, tk, out_dtype, name, a_off=(0, 0), b_off=(0, 0), res=None,
        out_rows=None, out_off=0, out_prev=None, deps=(), side=None):
    nm, nn, nk = M // tm, N // tn, K // tk
    assert nm * tm == M and nn * tn == N and nk * tk == K
    if mode == "nn":
        a_bs, b_bs = (tm, tk), (tk, tn)
        a_ix = lambda i, j, k: (i, k)
        b_ix = lambda i, j, k: (k, j)
        dims = (((1,), (0,)), ((), ()))
    elif mode == "nt":
        a_bs, b_bs = (tm, tk), (tn, tk)
        a_ix = lambda i, j, k: (i, k)
        b_ix = lambda i, j, k: (j, k)
        dims = (((1,), (1,)), ((), ()))
    else:
        a_bs, b_bs = (tk, tm), (tk, tn)
        a_ix = lambda i, j, k: (k, i)
        b_ix = lambda i, j, k: (k, j)
        dims = (((0,), (0,)), ((), ()))

    def spec(bs, ix, off):
        def im(i, j, k):
            r, c = ix(i, j, k)
            return (r + off[0], c + off[1])
        return pl.BlockSpec(bs, im)

    in_specs = [spec(a_bs, a_ix, a_off), spec(b_bs, b_ix, b_off)]
    args = [a, b]
    if res is not None:
        in_specs.append(pl.BlockSpec((tm, tn), lambda i, j, k: (i, j)))
        args.append(res)
    out_shape = jax.ShapeDtypeStruct((M if out_rows is None else out_rows, N), out_dtype)
    out_spec = pl.BlockSpec((tm, tn), lambda i, j, k: (i + out_off, j))
    has_res = res is not None
    extra = list(deps) + ([out_prev] if out_prev is not None else [])
    for dep in extra:
        in_specs.append(pl.BlockSpec(memory_space=pl.ANY))
        args.append(dep)
    o_pos = 2 + int(has_res) + len(extra)
    aliases = {len(args) - 1: 0} if out_prev is not None else {}
    out_specs, out_shapes = out_spec, out_shape
    if side is not None:
        R, wmv, layer, s_n = side
        s_rows, s_cols = wmv[0].shape[-2:]
        s_tr = s_rows // s_n
        assert s_tr * s_n == s_rows and s_tr % 8 == 0 and s_n <= nm * nn * nk
        s_pos = o_pos
        o_pos += 4

        def s_ix(i, j, k):
            return jnp.minimum((i * nn + j) * nk + k, s_n - 1)

        in_specs.append(pl.BlockSpec((R.shape[0], s_tr, s_cols), lambda i, j, k: (0, s_ix(i, j, k), 0)))
        s_tile = pl.BlockSpec((None, s_tr, s_cols), lambda i, j, k: (layer, s_ix(i, j, k), 0))
        in_specs += [s_tile] * 3
        args += [R, *wmv]
        out_specs = [out_spec] + [s_tile] * 4
        out_shapes = [out_shape] + [jax.ShapeDtypeStruct(wmv[0].shape, F32)] * 4

    def kern(*refs):
        a_ref, b_ref = refs[0], refs[1]
        res_ref = refs[2] if has_res else None
        o_ref = refs[o_pos]
        if side is not None:
            r_ref, w_ref, m_ref, v_ref = refs[s_pos:s_pos + 4]
            g_out, d_out, m_out, v_out = refs[o_pos + 1:o_pos + 5]

            sg = _sum_contributions(r_ref)
            g_out[...] = sg
            d_out[...], m_out[...], v_out[...] = _adam_math(sg, w_ref[...], m_ref[...], v_ref[...])
        av = a_ref[...]
        bv = b_ref[...]
        if av.dtype != BF16:
            av = av.astype(BF16)
        if bv.dtype != BF16:
            bv = bv.astype(BF16)
        part = lax.dot_general(av, bv, dims, preferred_element_type=F32)

        def write(val):
            if has_res:
                val = val + res_ref[...]
            o_ref[...] = val.astype(out_dtype)

        if nk == 1:
            write(part)
        else:
            acc_ref = refs[-1]
            k = pl.program_id(2)

            @pl.when(k == 0)
            def _():
                acc_ref[...] = part

            @pl.when(k > 0)
            def _():
                acc_ref[...] += part

            @pl.when(k == nk - 1)
            def _():
                write(acc_ref[...])

    scratch = [pltpu.VMEM((tm, tn), F32)] if nk > 1 else []
    outs = pl.pallas_call(
        kern, grid=(nm, nn, nk), in_specs=in_specs, out_specs=out_specs, out_shape=out_shapes,
        scratch_shapes=scratch, input_output_aliases=aliases, compiler_params=_cparams(3), name=name)(*args)
    return outs if side is None else (outs[0], tuple(outs[1:]))


def _mm_rms_bwd(a, b, x, g, dres, *, mode, M, K, tm, name, b_off=(0, 0), deps=(), folded=()):
    nd, nf = len(deps), len(folded)
    b_bs = (K, D) if mode == "nn" else (D, K)
    dims = (((1,), (0,)), ((), ())) if mode == "nn" else (((1,), (1,)), ((), ()))

    def kern(a_ref, b_ref, x_ref, g_ref, dres_ref, *rest):
        f_refs = rest[:nf]
        dx_ref, dg_ref = rest[nf + nd:nf + nd + 2]
        i = pl.program_id(0)
        av = a_ref[...]
        if av.dtype != BF16:
            av = av.astype(BF16)
        dhv = lax.dot_general(av, b_ref[...], dims, preferred_element_type=F32)
        if nf:
            acc_ref = rest[-1]
            _chunks_put(acc_ref, dhv)
            for f_ref in f_refs:
                dil = f_ref.shape[0]
                for res in range(dil):
                    _chunks_add_rows(acc_ref, f_ref[res], res, tm // dil, dil, True)
            dhv = _chunks_get(acc_ref)
        xv = x_ref[...]
        r = lax.rsqrt(jnp.mean(xv * xv, axis=-1, keepdims=True) + RMS_EPS)
        xhat = xv * r
        gy = dhv * g_ref[...]
        dx_ref[...] = dres_ref[...] + r * (gy - xhat * jnp.mean(gy * xhat, axis=-1, keepdims=True))
        part = jnp.sum(dhv * xhat, axis=0, keepdims=True)

        @pl.when(i == 0)
        def _():
            dg_ref[...] = part

        @pl.when(i > 0)
        def _():
            dg_ref[...] += part

    row = pl.BlockSpec((tm, D), lambda i: (i, 0))
    vec = pl.BlockSpec((1, D), lambda i: (0, 0))
    return pl.pallas_call(
        kern, grid=(M // tm,),
        in_specs=[pl.BlockSpec((tm, K), lambda i: (i, 0)),
                  pl.BlockSpec(b_bs, lambda i: b_off, pipeline_mode=pl.Buffered(1)), row, vec, row]
        + [pl.BlockSpec((f.shape[0], tm // f.shape[0], D), lambda i: (0, i, 0)) for f in folded]
        + [pl.BlockSpec(memory_space=pl.ANY)] * nd,
        out_specs=[row, vec],
        out_shape=[jax.ShapeDtypeStruct((M, D), F32), jax.ShapeDtypeStruct((1, D), F32)],
        scratch_shapes=[pltpu.VMEM((D // 128, tm, 128), F32)] if nf else [],
        compiler_params=_cparams(1), name=name)(a, b, x, g, dres, *folded, *deps)


def _norm_tail(xv, gv, rest, head):
    r = lax.rsqrt(jnp.mean(xv * xv, axis=-1, keepdims=True) + RMS_EPS)
    xhat = xv * r
    if not head:
        xo_ref, h_ref = rest
        xo_ref[...] = xv
        h_ref[...] = (xhat * gv).astype(BF16)
        return
    t_ref, dx_ref, dg_ref, ls_ref = rest
    i = pl.program_id(0)
    e = xhat * gv - t_ref[...]
    dy = e * (1.0 / D)
    gy = dy * gv
    dx_ref[...] = r * (gy - xhat * jnp.mean(gy * xhat, axis=-1, keepdims=True))
    dgp = jnp.sum(dy * xhat, axis=0, keepdims=True)
    lsp = jnp.sum(e * e, axis=0, keepdims=True)

    @pl.when(i == 0)
    def _():
        dg_ref[...] = dgp
        ls_ref[...] = lsp

    @pl.when(i > 0)
    def _():
        dg_ref[...] += dgp
        ls_ref[...] += lsp


def _ffn_fwd(h, wgu, wd, res, g, *, name, tgt=None):
    S = h.shape[0]
    tm = 256
    nj = DFF // HCH
    head = tgt is not None

    def kern(h_ref, wgu_ref, wd_ref, res_ref, g_ref, *rest):
        t_refs, (gu_ref, act_ref), tail = rest[:int(head)], rest[int(head):int(head) + 2], rest[int(head) + 2:]
        hv = h_ref[...]
        for j in range(nj):
            gu = lax.dot_general(hv, wgu_ref[2 * HCH * j:2 * HCH * (j + 1), :], (((1,), (1,)), ((), ())),
                                 preferred_element_type=F32)
            gu_ref[:, 2 * HCH * j:2 * HCH * (j + 1)] = gu.astype(BF16)
            gate = gu[:, :HCH]
            act_ref[:, HCH * j:HCH * (j + 1)] = (gate * jax.nn.sigmoid(gate) * gu[:, HCH:]).astype(BF16)
        xv = res_ref[...] + jnp.dot(act_ref[...], wd_ref[...], preferred_element_type=F32)
        _norm_tail(xv, g_ref[...], tuple(t_refs) + tuple(tail), head)

    row = pl.BlockSpec((tm, D), lambda i: (i, 0))
    vec = pl.BlockSpec((1, D), lambda i: (0, 0))
    in_specs = [row, pl.BlockSpec((2 * DFF, D), lambda i: (0, 0), pipeline_mode=pl.Buffered(1)),
                pl.BlockSpec((DFF, D), lambda i: (0, 0), pipeline_mode=pl.Buffered(1)), row, vec]
    out_specs = [pl.BlockSpec((tm, 2 * DFF), lambda i: (i, 0)), pl.BlockSpec((tm, DFF), lambda i: (i, 0))]
    out_shape = [jax.ShapeDtypeStruct((S, 2 * DFF), BF16), jax.ShapeDtypeStruct((S, DFF), BF16)]
    args = [h, wgu, wd, res, g]
    if head:
        in_specs, args = in_specs + [row], args + [tgt]
        out_specs += [row, vec, vec]
        out_shape += [jax.ShapeDtypeStruct((S, D), F32), jax.ShapeDtypeStruct((1, D), F32),
                      jax.ShapeDtypeStruct((1, D), F32)]
    else:
        out_specs += [row, row]
        out_shape += [jax.ShapeDtypeStruct((S, D), F32), jax.ShapeDtypeStruct((S, D), BF16)]
    outs = pl.pallas_call(kern, grid=(S // tm,), in_specs=in_specs, out_specs=out_specs, out_shape=out_shape,
                          compiler_params=_cparams(1), name=name)(*args)
    return outs[0], outs[1], tuple(outs[2:])


def _mm_res_norm(a, b, res, g, *, K, tm, name, b_off=(0, 0), tgt=None):
    M = a.shape[0]
    head = tgt is not None

    def kern(a_ref, b_ref, res_ref, g_ref, *rest):
        xv = res_ref[...] + jnp.dot(a_ref[...], b_ref[...], preferred_element_type=F32)
        _norm_tail(xv, g_ref[...], rest, head)

    row = pl.BlockSpec((tm, D), lambda i: (i, 0))
    vec = pl.BlockSpec((1, D), lambda i: (0, 0))
    in_specs = [pl.BlockSpec((tm, K), lambda i: (i, 0)),
                pl.BlockSpec((K, D), lambda i: b_off, pipeline_mode=pl.Buffered(1)), row, vec]
    if head:
        return pl.pallas_call(
            kern, grid=(M // tm,), in_specs=in_specs + [row], out_specs=[row, vec, vec],
            out_shape=[jax.ShapeDtypeStruct((M, D), F32), jax.ShapeDtypeStruct((1, D), F32),
                       jax.ShapeDtypeStruct((1, D), F32)],
            compiler_params=_cparams(1), name=name)(a, b, res, g, tgt)
    return pl.pallas_call(
        kern, grid=(M // tm,), in_specs=in_specs, out_specs=[row, row],
        out_shape=[jax.ShapeDtypeStruct((M, D), F32), jax.ShapeDtypeStruct((M, D), BF16)],
        compiler_params=_cparams(1), name=name)(a, b, res, g)


def _rms_fwd(x, g, name, deps=()):
    S = x.shape[0]
    tr = 512

    def kern(x_ref, g_ref, *rest):
        h_ref = rest[-1]
        xv = x_ref[...]
        r = lax.rsqrt(jnp.mean(xv * xv, axis=-1, keepdims=True) + RMS_EPS)
        h_ref[...] = (xv * r * g_ref[...]).astype(BF16)

    return pl.pallas_call(
        kern, grid=(S // tr,),
        in_specs=[pl.BlockSpec((tr, D), lambda i: (i, 0)), pl.BlockSpec((1, D), lambda i: (0, 0))]
        + [pl.BlockSpec(memory_space=pl.ANY)] * len(deps),
        out_specs=pl.BlockSpec((tr, D), lambda i: (i, 0)),
        out_shape=jax.ShapeDtypeStruct((S, D), BF16), compiler_params=_cparams(1), name=name)(x, g, *deps)


def _chunks_put(scr, val):
    for c in range(scr.shape[0]):
        scr[c] = val[:, c * 128:(c + 1) * 128]


def _chunks_get(scr):
    return jnp.concatenate([scr[c] for c in range(scr.shape[0])], axis=1)


def _chunks_rows(scr, r, n, dil):
    return jnp.concatenate([scr.at[c][pl.ds(r, n, stride=dil), :] for c in range(scr.shape[0])], axis=1)


def _chunks_add_rows(scr, val, r, n, dil, accumulate):
    for c in range(scr.shape[0]):
        rows = pl.ds(r, n, stride=dil)
        piece = val[:, c * 128:(c + 1) * 128]
        tile = scr.at[c]
        tile[rows, :] = tile[rows, :] + piece if accumulate else piece


def _rms_fwd_folded(x, g, name, deps=()):
    S = x.shape[0]
    tr = 512
    dils = DILS[1:]

    def kern(x_ref, g_ref, *rest):
        outs, scr = rest[len(deps):-1], rest[-1]
        xv = x_ref[...]
        r = lax.rsqrt(jnp.mean(xv * xv, axis=-1, keepdims=True) + RMS_EPS)
        h = (xv * r * g_ref[...]).astype(BF16)
        outs[0][...] = h
        _chunks_put(scr, h.astype(F32))
        for o_ref, dil in zip(outs[1:], dils):
            for res in range(dil):
                o_ref[res] = _chunks_rows(scr, res, tr // dil, dil).astype(BF16)

    return pl.pallas_call(
        kern, grid=(S // tr,),
        in_specs=[pl.BlockSpec((tr, D), lambda i: (i, 0)), pl.BlockSpec((1, D), lambda i: (0, 0))]
        + [pl.BlockSpec(memory_space=pl.ANY)] * len(deps),
        out_specs=[pl.BlockSpec((tr, D), lambda i: (i, 0))]
        + [pl.BlockSpec((dil, tr // dil, D), lambda i: (0, i, 0)) for dil in dils],
        out_shape=[jax.ShapeDtypeStruct((S, D), BF16)]
        + [jax.ShapeDtypeStruct((dil, S // dil, D), BF16) for dil in dils],
        scratch_shapes=[pltpu.VMEM((D // 128, tr, 128), F32)],
        compiler_params=_cparams(1), name=name)(x, g, *deps)


def _ffn_bwd(dxo, wd, wgu, gu, xin, gain, name):
    S = dxo.shape[0]
    tm = 256
    nj = DFF // HCH

    def kern(dx_ref, wd_ref, wgu_ref, gu_ref, x_ref, g_ref, dgu_ref, dxin_ref, dg_ref):
        i = pl.program_id(0)
        dxv = dx_ref[...]
        dxb = dxv.astype(BF16)
        for j in range(nj):
            c0 = 2 * HCH * j
            dact = lax.dot_general(dxb, wd_ref[HCH * j:HCH * (j + 1), :], (((1,), (1,)), ((), ())),
                                   preferred_element_type=F32)
            gate = gu_ref[:, c0:c0 + HCH].astype(F32)
            up = gu_ref[:, c0 + HCH:c0 + 2 * HCH].astype(F32)
            sig = jax.nn.sigmoid(gate)
            silu = gate * sig
            dgu_ref[:, c0:c0 + HCH] = (dact * up * (sig * (1.0 + gate * (1.0 - sig)))).astype(BF16)
            dgu_ref[:, c0 + HCH:c0 + 2 * HCH] = (dact * silu).astype(BF16)
        dhv = jnp.dot(dgu_ref[...], wgu_ref[...], preferred_element_type=F32)
        xv = x_ref[...]
        r = lax.rsqrt(jnp.mean(xv * xv, axis=-1, keepdims=True) + RMS_EPS)
        xhat = xv * r
        gy = dhv * g_ref[...]
        dxin_ref[...] = dxv + r * (gy - xhat * jnp.mean(gy * xhat, axis=-1, keepdims=True))
        part = jnp.sum(dhv * xhat, axis=0, keepdims=True)

        @pl.when(i == 0)
        def _():
            dg_ref[...] = part

        @pl.when(i > 0)
        def _():
            dg_ref[...] += part

    row = pl.BlockSpec((tm, D), lambda i: (i, 0))
    wide = pl.BlockSpec((tm, 2 * DFF), lambda i: (i, 0))
    vec = pl.BlockSpec((1, D), lambda i: (0, 0))
    return pl.pallas_call(
        kern, grid=(S // tm,),
        in_specs=[row, pl.BlockSpec((DFF, D), lambda i: (0, 0), pipeline_mode=pl.Buffered(1)),
                  pl.BlockSpec((2 * DFF, D), lambda i: (0, 0), pipeline_mode=pl.Buffered(1)), wide, row, vec],
        out_specs=[wide, row, vec],
        out_shape=[jax.ShapeDtypeStruct((S, 2 * DFF), BF16), jax.ShapeDtypeStruct((S, D), F32),
                   jax.ShapeDtypeStruct((1, D), F32)],
        compiler_params=_cparams(1), name=name)(dxo, wd, wgu, gu, xin, gain)


def _window(val, grp, backward):
    S = val.shape[0]
    row = lax.broadcasted_iota(jnp.int32, val.shape, 0)
    cnt = jnp.minimum(row + 1, 2 << grp).astype(F32)
    s = val / cnt if backward else val
    for k in (1, 2, 4, 8)[:grp + 1]:
        if backward:
            sh = jnp.where(row < S - k, pltpu.roll(s, S - k, 0), 0.0)
        else:
            sh = jnp.where(row >= k, pltpu.roll(s, k, 0), 0.0)
        s = s + sh
    return s - val if backward else s / cnt - val


def _pool_in_window(h, wpi):
    S = h.shape[0]

    def kern(h_ref, w_ref, o_ref):
        g = pl.program_id(0)
        u = jnp.dot(h_ref[...], w_ref[...], preferred_element_type=F32)
        for grp in range(POOL_G):
            @pl.when(g == grp)
            def _(grp=grp):
                o_ref[...] = _window(u, grp, False).astype(BF16)

    return pl.pallas_call(
        kern, grid=(POOL_G,),
        in_specs=[pl.BlockSpec((S, D), lambda g: (0, 0), pipeline_mode=pl.Buffered(1)),
                  pl.BlockSpec((D, PGD), lambda g: (0, g))],
        out_specs=pl.BlockSpec((S, PGD), lambda g: (0, g)),
        out_shape=jax.ShapeDtypeStruct((S, D), BF16), compiler_params=_cparams(1), name="pool_in")(h, wpi)


def _pool_out(yd, G, scale, xres):
    S = yd.shape[0]
    tm = min(S, 4096)

    def kern(y_ref, w_ref, s_ref, x_ref, o_ref):
        z = jnp.dot(y_ref[...], w_ref[...], preferred_element_type=F32)
        o_ref[...] = x_ref[...] + z * s_ref[...]

    tile = pl.BlockSpec((tm, PGD), lambda i, g: (i, g))
    return pl.pallas_call(
        kern, grid=(S // tm, POOL_G),
        in_specs=[tile, pl.BlockSpec((PGD, PGD), lambda i, g: (0, g)),
                  pl.BlockSpec((1, PGD), lambda i, g: (0, g)), tile],
        out_specs=tile, out_shape=jax.ShapeDtypeStruct((S, D), F32),
        compiler_params=_cparams(2), name="pool_out")(yd, G, scale, xres)


def _pool_out_bwd(dz, yd, G, scale, deps=()):
    S = yd.shape[0]
    nd = len(deps)

    def kern(dz_ref, y_ref, w_ref, s_ref, *rest):
        du_ref, ds_ref, dw_ref = rest[nd:]
        g = pl.program_id(0)
        dzv = dz_ref[...]
        yv = y_ref[...]
        wv = w_ref[...]
        zraw = jnp.dot(yv, wv, preferred_element_type=F32)
        ds_ref[...] = jnp.sum(dzv * zraw, axis=0, keepdims=True)
        dzr = (dzv * s_ref[...]).astype(BF16)
        dw_ref[...] = lax.dot_general(yv, dzr, (((0,), (0,)), ((), ())), preferred_element_type=F32).astype(BF16)
        dyd = lax.dot_general(dzr, wv, (((1,), (1,)), ((), ())), preferred_element_type=F32)
        for grp in range(POOL_G):
            @pl.when(g == grp)
            def _(grp=grp):
                du_ref[...] = _window(dyd, grp, True).astype(BF16)

    tile = pl.BlockSpec((S, PGD), lambda g: (0, g))
    return pl.pallas_call(
        kern, grid=(POOL_G,),
        in_specs=[tile, tile, pl.BlockSpec((PGD, PGD), lambda g: (0, g)),
                  pl.BlockSpec((1, PGD), lambda g: (0, g))] + [pl.BlockSpec(memory_space=pl.ANY)] * nd,
        out_specs=[tile, pl.BlockSpec((1, PGD), lambda g: (0, g)), pl.BlockSpec((PGD, PGD), lambda g: (0, g))],
        out_shape=[jax.ShapeDtypeStruct((S, D), BF16), jax.ShapeDtypeStruct((1, D), F32),
                   jax.ShapeDtypeStruct((PGD, D), BF16)],
        compiler_params=_cparams(1), name="pool_out_bwd")(dz, yd, G, scale, *deps)


def _bias_table():
    qi = jnp.arange(QB)[:, None]
    ki = jnp.arange(2 * QB)[None, :]
    delta = QB + qi - ki
    inband = (delta >= 0) & (delta <= QB)
    n = NGROUPS * HEADS
    slopes = jnp.exp2(-8.0 * jnp.arange(1, n + 1, dtype=F32) / n).reshape(NGROUPS, HEADS)
    dil = jnp.asarray(DILS, F32)
    bias = -slopes[:, :, None, None] * (delta.astype(F32)[None, None] * dil[:, None, None, None])
    return jnp.where(inband[None, None], bias, NEG)


def _attn_fwd(qkv_f, bias, nb, name):
    S = qkv_f.shape[0]
    nblk = S // QB
    scale = HD ** -0.5

    def kern(q_ref, k2_ref, kp_ref, v2_ref, vp_ref, b_ref, o_ref, l_ref, s_scr, p_scr, r_scr):
        s_id = pl.program_id(0)
        col = lax.broadcasted_iota(jnp.int32, (QB, 2 * QB), 1)
        lane = lax.broadcasted_iota(jnp.int32, (QB, HD), 1)

        def keys(sub, cur2_ref, prev_ref, sl):
            if sub:
                return cur2_ref[:, sl]
            return jnp.concatenate([prev_ref[:, sl], cur2_ref[0:QB, sl]], axis=0)

        for sub in range(2):
            for h in range(HEADS):
                sl = slice(h * HD, (h + 1) * HD)
                s_scr[sub * HEADS + h] = lax.dot_general(
                    q_ref[sub * QB:(sub + 1) * QB, sl], keys(sub, k2_ref, kp_ref, sl), (((1,), (1,)), ((), ())),
                    preferred_element_type=F32)
        for sub in range(2):
            has_prev = jnp.bitwise_and(2 * s_id + sub, nb - 1) != 0
            dead = jnp.logical_and(col < QB, jnp.logical_not(has_prev))
            lse_all = jnp.zeros((QB, HD), F32)
            for h in range(HEADS):
                u = sub * HEADS + h
                s = s_scr[u] * scale + b_ref[h]
                s = jnp.where(dead, NEG, s)
                m = jnp.max(s, axis=-1, keepdims=True)
                p = jnp.exp(s - m)
                den = jnp.sum(p, axis=-1, keepdims=True)
                p_scr[u] = p.astype(BF16)
                r_scr[u] = jnp.broadcast_to(1.0 / den, (QB, HD))
                lse_all = jnp.where(lane == h, m + jnp.log(den), lse_all)
            l_ref[sub * QB:(sub + 1) * QB, :] = lse_all
        for sub in range(2):
            for h in range(HEADS):
                u = sub * HEADS + h
                sl = slice(h * HD, (h + 1) * HD)
                o = jnp.dot(p_scr[u], keys(sub, v2_ref, vp_ref, sl), preferred_element_type=F32) * r_scr[u]
                o_ref[sub * QB:(sub + 1) * QB, sl] = o.astype(BF16)

    def pair(colblk):
        return pl.BlockSpec((2 * QB, D), lambda s: (s, colblk))

    def prev(colblk):
        return pl.BlockSpec((QB, D), lambda s: (jnp.maximum(2 * s - 1, 0), colblk))

    return pl.pallas_call(
        kern, grid=(nblk // 2,),
        in_specs=[pair(0), pair(1), prev(1), pair(2), prev(2), pl.BlockSpec((HEADS, QB, 2 * QB), lambda s: (0, 0, 0))],
        out_specs=[pl.BlockSpec((2 * QB, D), lambda s: (s, 0)), pl.BlockSpec((2 * QB, HD), lambda s: (s, 0))],
        out_shape=[jax.ShapeDtypeStruct((S, D), BF16), jax.ShapeDtypeStruct((S, HD), F32)],
        scratch_shapes=[pltpu.VMEM((2 * HEADS, QB, 2 * QB), F32), pltpu.VMEM((2 * HEADS, QB, 2 * QB), BF16),
                        pltpu.VMEM((2 * HEADS, QB, HD), F32)],
        compiler_params=_cparams(1), name=name)(qkv_f, qkv_f, qkv_f, qkv_f, qkv_f, bias)


def _natural(ref, scr, tm):
    dil = ref.shape[0]
    for res in range(dil):
        _chunks_add_rows(scr, ref[res].astype(F32), res, tm // dil, dil, False)
    return _chunks_get(scr)


def _attn_merge(os, lses):
    S = os[0].shape[0]
    tm = 512

    def kern(o0, o1, o2, l0, l1, l2, om_ref, lm_ref, ls1, ls2, os1, os2):
        la = l0[...]
        lb = _natural(l1, ls1, tm)
        lc = _natural(l2, ls2, tm)
        m = jnp.maximum(jnp.maximum(la, lb), lc)
        e0, e1, e2 = jnp.exp(la - m), jnp.exp(lb - m), jnp.exp(lc - m)
        tot = e0 + e1 + e2
        lm_ref[...] = m + jnp.log(tot)
        w0, w1, w2 = e0 / tot, e1 / tot, e2 / tot
        for res in range(o1.shape[0]):
            _chunks_add_rows(os1, o1[res].astype(F32), res, tm // o1.shape[0], o1.shape[0], False)
        for res in range(o2.shape[0]):
            _chunks_add_rows(os2, o2[res].astype(F32), res, tm // o2.shape[0], o2.shape[0], False)
        for h in range(HEADS):
            sl = slice(h * HD, (h + 1) * HD)
            acc = w0[:, h:h + 1] * o0[:, sl].astype(F32) + w1[:, h:h + 1] * os1[h] + w2[:, h:h + 1] * os2[h]
            om_ref[:, sl] = acc.astype(BF16)

    def spec(a, c):
        if a.ndim == 2:
            return pl.BlockSpec((tm, c), lambda i: (i, 0))
        return pl.BlockSpec((a.shape[0], tm // a.shape[0], c), lambda i: (0, i, 0))

    return pl.pallas_call(
        kern, grid=(S // tm,),
        in_specs=[spec(a, D) for a in os] + [spec(a, HD) for a in lses],
        out_specs=[pl.BlockSpec((tm, D), lambda i: (i, 0)), pl.BlockSpec((tm, HD), lambda i: (i, 0))],
        out_shape=[jax.ShapeDtypeStruct((S, D), BF16), jax.ShapeDtypeStruct((S, HD), F32)],
        scratch_shapes=[pltpu.VMEM((1, tm, HD), F32), pltpu.VMEM((1, tm, HD), F32),
                        pltpu.VMEM((HEADS, tm, HD), F32), pltpu.VMEM((HEADS, tm, HD), F32)],
        compiler_params=_cparams(1), name="attn_merge")(*os, *lses)


def _attn_bwd_prep(dx, wo, o, lse, deps=()):
    S = o.shape[0]
    tm = 512
    dils = DILS[1:]
    nd = len(deps)

    def kern(dx_ref, w_ref, o_ref, l_ref, *rest):
        rest = rest[nd:]
        do_outs, l_outs, d_outs = rest[0:3], rest[3:5], rest[5:8]
        do_scr, l_scr, d_scr = rest[8:11]
        dov = lax.dot_general(dx_ref[...].astype(BF16), w_ref[...], (((1,), (1,)), ((), ())),
                              preferred_element_type=F32)
        lane = lax.broadcasted_iota(jnp.int32, (tm, HD), 1)
        acc = jnp.zeros((tm, HD), F32)
        for h in range(HEADS):
            sl = slice(h * HD, (h + 1) * HD)
            prod = dov[:, sl] * o_ref[:, sl].astype(F32)
            acc = jnp.where(lane == h, jnp.sum(prod, axis=-1, keepdims=True), acc)
        d_scr[0] = acc
        l_scr[0] = l_ref[...]
        _chunks_put(do_scr, dov)
        do_outs[0][...] = dov.astype(BF16)
        d_outs[0][...] = acc
        for j, dil in enumerate(dils):
            for res in range(dil):
                n = tm // dil
                do_outs[1 + j][res] = _chunks_rows(do_scr, res, n, dil).astype(BF16)
                l_outs[j][res] = _chunks_rows(l_scr, res, n, dil)
                d_outs[1 + j][res] = _chunks_rows(d_scr, res, n, dil)

    def nat(c):
        return pl.BlockSpec((tm, c), lambda i: (i, 0))

    def fol(dil, c):
        return pl.BlockSpec((dil, tm // dil, c), lambda i: (0, i, 0))

    def shapes(c, dt, with_natural):
        first = [jax.ShapeDtypeStruct((S, c), dt)] if with_natural else []
        return first + [jax.ShapeDtypeStruct((dil, S // dil, c), dt) for dil in dils]

    outs = pl.pallas_call(
        kern, grid=(S // tm,),
        in_specs=[nat(D), pl.BlockSpec((D, D), lambda i: (0, 0), pipeline_mode=pl.Buffered(1)), nat(D), nat(HD)]
        + [pl.BlockSpec(memory_space=pl.ANY)] * nd,
        out_specs=[nat(D)] + [fol(dil, D) for dil in dils] + [fol(dil, HD) for dil in dils]
        + [nat(HD)] + [fol(dil, HD) for dil in dils],
        out_shape=shapes(D, BF16, True) + shapes(HD, F32, False) + shapes(HD, F32, True),
        scratch_shapes=[pltpu.VMEM((HEADS, tm, HD), F32), pltpu.VMEM((1, tm, HD), F32), pltpu.VMEM((1, tm, HD), F32)],
        compiler_params=_cparams(1), name="attn_out_bwd")(dx, wo, o, lse, *deps)
    return outs[0:3], [lse] + list(outs[3:5]), outs[5:8]


def _attn_bwd(qkv_f, do_f, lse_f, delta_f, bias, nb, name):
    S = qkv_f.shape[0]
    nblk = S // QB
    scale = HD ** -0.5

    npair = nblk // 2

    def kern(q_ref, k2_ref, kp_ref, v2_ref, vp_ref, do_ref, l_ref, d_ref, b_ref, out_ref, dq_c, dk_c, dv_c,
             s_scr, dp_scr, ds_scr, p_scr):
        s_id = pl.program_id(0)

        @pl.when(s_id == 0)
        def _():
            dq_c[...] = jnp.zeros_like(dq_c)
            dk_c[...] = jnp.zeros_like(dk_c)
            dv_c[...] = jnp.zeros_like(dv_c)

        @pl.when(s_id == npair)
        def _():
            out_ref[:, 0:D] = dq_c[...].astype(BF16)
            out_ref[:, D:2 * D] = dk_c[...].astype(BF16)
            out_ref[:, 2 * D:3 * D] = dv_c[...].astype(BF16)

        def keys(sub, cur2_ref, prev_ref, sl):
            if sub:
                return cur2_ref[:, sl]
            return jnp.concatenate([prev_ref[:, sl], cur2_ref[0:QB, sl]], axis=0)

        @pl.when(s_id < npair)
        def _():
            col = lax.broadcasted_iota(jnp.int32, (QB, 2 * QB), 1)
            out_ref[:, 0:D] = dq_c[...].astype(BF16)
            for sub in range(2):
                rows = slice(sub * QB, (sub + 1) * QB)
                for h in range(HEADS):
                    sl = slice(h * HD, (h + 1) * HD)
                    u = sub * HEADS + h
                    s_scr[u] = lax.dot_general(q_ref[rows, sl], keys(sub, k2_ref, kp_ref, sl),
                                               (((1,), (1,)), ((), ())), preferred_element_type=F32)
                    dp_scr[u] = lax.dot_general(do_ref[rows, sl], keys(sub, v2_ref, vp_ref, sl),
                                                (((1,), (1,)), ((), ())), preferred_element_type=F32)
            for sub in range(2):
                rows = slice(sub * QB, (sub + 1) * QB)
                has_prev = jnp.bitwise_and(2 * s_id + sub, nb - 1) != 0
                dead = jnp.logical_and(col < QB, jnp.logical_not(has_prev))
                lv = l_ref[rows, :]
                dv_ = d_ref[rows, :]
                for h in range(HEADS):
                    u = sub * HEADS + h
                    s = s_scr[u] * scale + b_ref[h]
                    s = jnp.where(dead, NEG, s)
                    p = jnp.exp(s - lv[:, h:h + 1])
                    ds_scr[u] = (p * (dp_scr[u] - dv_[:, h:h + 1]) * scale).astype(BF16)
                    p_scr[u] = p.astype(BF16)
            for h in range(HEADS):
                sl = slice(h * HD, (h + 1) * HD)
                parts = []
                for sub in range(2):
                    rows = slice(sub * QB, (sub + 1) * QB)
                    u = sub * HEADS + h
                    ds = ds_scr[u]
                    dq_c[rows, sl] = jnp.dot(ds, keys(sub, k2_ref, kp_ref, sl), preferred_element_type=F32)
                    dkk = lax.dot_general(ds, q_ref[rows, sl], (((0,), (0,)), ((), ())), preferred_element_type=F32)
                    dvv = lax.dot_general(p_scr[u], do_ref[rows, sl], (((0,), (0,)), ((), ())),
                                          preferred_element_type=F32)
                    parts.append((dkk, dvv))
                for which, carry, base in ((0, dk_c, D), (1, dv_c, 2 * D)):
                    first, second = parts[0][which], parts[1][which]
                    cols = slice(base + h * HD, base + (h + 1) * HD)
                    out_ref[0:QB, cols] = carry[0:QB, sl].astype(BF16)
                    out_ref[QB:2 * QB, cols] = (carry[QB:2 * QB, sl] + first[:QB]).astype(BF16)
                    carry[0:QB, sl] = first[QB:] + second[:QB]
                    carry[QB:2 * QB, sl] = second[QB:]

    last = npair - 1

    def pair(colblk, c):
        return pl.BlockSpec((2 * QB, c), lambda s: (jnp.minimum(s, last), colblk))

    def prev(colblk):
        return pl.BlockSpec((QB, D), lambda s: (jnp.maximum(2 * jnp.minimum(s, last) - 1, 0), colblk))

    return pl.pallas_call(
        kern, grid=(npair + 1,),
        in_specs=[pair(0, D), pair(1, D), prev(1), pair(2, D), prev(2), pair(0, D), pair(0, HD), pair(0, HD),
                  pl.BlockSpec((HEADS, QB, 2 * QB), lambda s: (0, 0, 0))],
        out_specs=pl.BlockSpec((2 * QB, 3 * D), lambda s: (jnp.maximum(s - 1, 0), 0)),
        out_shape=jax.ShapeDtypeStruct((S, 3 * D), BF16),
        scratch_shapes=[pltpu.VMEM((2 * QB, D), F32), pltpu.VMEM((2 * QB, D), F32), pltpu.VMEM((2 * QB, D), F32),
                        pltpu.VMEM((2 * HEADS, QB, 2 * QB), F32), pltpu.VMEM((2 * HEADS, QB, 2 * QB), F32),
                        pltpu.VMEM((2 * HEADS, QB, 2 * QB), BF16), pltpu.VMEM((2 * HEADS, QB, 2 * QB), BF16)],
        compiler_params=_cparams(1), name=name)(qkv_f, qkv_f, qkv_f, qkv_f, qkv_f, do_f, lse_f, delta_f, bias)


def _local_step(x, tgt, comm, attn_norm, ffn_norm, final_norm, opt=None, early=None):
    S = x.shape[0]
    bias = _bias_table()
    g_attn = attn_norm.reshape(1, D)
    g_f0 = ffn_norm[0:1]
    g_f1 = ffn_norm[1:2]
    g_fin = final_norm.reshape(1, D)
    W = {}

    def ffn_fwd(xin, h, l, next_gain, target=None):
        return _ffn_fwd(h, W[f"gu{l}"], W[f"d{l}"], xin, next_gain, tgt=target, name=f"ffn_fwd{l}")

    def ffn_bwd(dxo, xin, gain, h, gu, act, l, rs_group):
        dgu, dxin, dgain = _ffn_bwd(dxo, W[f"d{l}"], W[f"gu{l}"], gu, xin, gain, f"ffn_bwd{l}")
        side_d = side_gu = None
        if l == 0 and opt is not None:
            got = comm.received(0, dgu)
            side_d, side_gu = (got["d1"], opt["d"], 1, 4), (got["gu1"], opt["gu"], 1, 8)
        gw_d = _mm(act, dxo, mode="tn", M=DFF, N=D, K=S, tm=HCH, tn=D, tk=2048 if side_d is None else 1024,
                   out_dtype=BF16, name=f"gw_d{l}", side=side_d)
        gw_gu = _mm(dgu, h, mode="tn", M=2 * DFF, N=D, K=S, tm=HCH, tn=D, tk=2048, out_dtype=BF16, name=f"gw_gu{l}",
                    side=side_gu)
        if side_d is not None:
            (gw_d, early["d"]), (gw_gu, early["gu"]) = gw_d, gw_gu
        return dxin, dgain, comm.send_grads(rs_group, {f"d{l}": gw_d, f"gu{l}": gw_gu})

    nbs = [S // QB // dil for dil in DILS]
    hf = _rms_fwd_folded(x, g_attn, "rms_attn", deps=comm.ag_tokens)
    hf = [h.reshape(S, D) for h in hf]
    W.update(comm.weights(0, hf[0]))
    qkv_f, o_f, lse_f = [], [], []
    for g, dil in enumerate(DILS):
        qkv_f.append(_mm(hf[g], W["qkv"], mode="nt", M=S, N=3 * D, K=D, tm=2048, tn=1024, tk=D, out_dtype=BF16,
                         b_off=(3 * g, 0), name=f"qkv_proj{g}"))
        og, lg = _attn_fwd(qkv_f[g], bias[g], nbs[g], f"attn_fwd{g}")
        o_f.append(og if dil == 1 else og.reshape(dil, S // dil, D))
        lse_f.append(lg if dil == 1 else lg.reshape(dil, S // dil, HD))
    passing = [comm.pass_on(1, tuple(o_f)), comm.pass_on(2, tuple(o_f))]
    (o_f, lse_f), passing = lax.optimization_barrier(((o_f, lse_f), passing))
    o, lse = _attn_merge(o_f, lse_f)
    W.update(comm.weights(1, (o, passing[0])))
    x1, h1 = _mm_res_norm(o, W["wo"], x, g_f0, K=D, tm=1024, name="attn_out")
    pv = W["pv"].reshape(NDEV, 8, 128)
    pool_norm, pool_scale = pv[:, 0, :].reshape(1, D), pv[:, 1, :].reshape(1, D)
    gu0, act0, (x2, h2) = ffn_fwd(x1, h1, 0, pool_norm)

    W.update(comm.weights(2, (x2, passing[1])))
    yd = _pool_in_window(h2, W["wpi"])
    x3 = _pool_out(yd, W["pg"], pool_scale, x2)
    h3 = _rms_fwd(x3, g_f1, "rms_ffn1")
    gu1, act1, (dx4, d_fin, lossvec) = ffn_fwd(x3, h3, 1, g_fin, target=tgt)

    dx3, d_f1, token = ffn_bwd(dx4, x3, g_f1, h3, gu1, act1, 1, 0)
    du, d_scale, gw_pg = _pool_out_bwd(dx3, yd, W["pg"], pool_scale, deps=(token,))
    gw_pi = _mm(h2, du, mode="tn", M=D, N=D, K=S, tm=D, tn=D, tk=S, out_dtype=BF16, name="gw_pi")
    token = comm.send_grads(1, {"pg": gw_pg, "wpi": gw_pi})
    dx2, d_pool = _mm_rms_bwd(du, W["wpi"], x2, pool_norm, dx3, mode="nt", M=S, K=D, tm=1024, deps=(token,),
                              name="pool_in_bwd")
    dx1, d_f0, token = ffn_bwd(dx2, x1, g_f0, h1, gu0, act0, 0, 2)

    gw_o = _mm(o, dx1, mode="tn", M=D, N=D, K=S, tm=D, tn=D, tk=2048, out_dtype=BF16, deps=(token,), name="gw_o")
    do_f, lse_ff, delta_f = _attn_bwd_prep(dx1, W["wo"], o, lse, deps=(token,))
    dqkv_f, gw_qkv = [], None
    for g in range(NGROUPS):
        dqkv_f.append(_attn_bwd(qkv_f[g], do_f[g].reshape(S, D), lse_ff[g].reshape(S, HD),
                                delta_f[g].reshape(S, HD), bias[g], nbs[g], f"attn_bwd{g}"))
        gw_qkv = _mm(dqkv_f[g], hf[g], mode="tn", M=3 * D, N=D, K=S, tm=1024, tn=D, tk=S, out_dtype=BF16,
                     out_rows=NGROUPS * 3 * D, out_off=3 * g, out_prev=gw_qkv, name=f"gw_qkv{g}")
    token = comm.send_grads_pairwise({"wo": gw_o, "qkv": gw_qkv})
    folded = []
    for g in reversed(range(1, NGROUPS)):
        dh0_g = _mm(dqkv_f[g], W["qkv"], mode="nn", M=S, N=D, K=3 * D, tm=1024, tn=D, tk=3 * D, out_dtype=F32,
                    b_off=(g, 0), deps=(token,), name=f"qkv_proj_bwd{g}")
        folded.append(dh0_g.reshape(DILS[g], S // DILS[g], D))
        if g == NGROUPS - 1:
            token = comm.pass_grads(dh0_g)
    grad_x, d_attn = _mm_rms_bwd(dqkv_f[0], W["qkv"], x, g_attn, dx1, mode="nn", M=S, K=3 * D, tm=512,
                                 deps=(token,), folded=folded, name="qkv_proj_bwd0")

    vec = jnp.concatenate([d_attn, d_f0, d_f1, d_fin, d_pool, d_scale, lossvec, jnp.zeros((1, D), F32)], axis=0)
    return grad_x, vec


def _mesh_pos():
    x, y, c = lax.axis_index("x"), lax.axis_index("y"), lax.axis_index("c")
    return x, y, c, 4 * x + 2 * y + c


def _peer(x, y, c, k):
    kx, ky, kc = (k >> 2) & 1, (k >> 1) & 1, k & 1
    px = 1 - x if kx else x
    py = 1 - y if ky else y
    pc = 1 - c if kc else c
    return (px, py, pc), 4 * px + 2 * py + pc


ANY = pl.BlockSpec(memory_space=pl.ANY)


HBM = pl.BlockSpec(memory_space=pltpu.HBM)
SEMS = pl.BlockSpec(memory_space=pltpu.SEMAPHORE)
EFFECT = pltpu.SideEffectType.DATAFLOW_SIDE_EFFECTING

AG_GROUPS = (("qkv",), ("wo", "gu0", "d0", "pv"), ("wpi", "pg", "gu1", "d1"))
AG_ORDER = tuple(n for grp in AG_GROUPS for n in grp)
RS_GROUPS = (("d1", "gu1"), ("pg", "wpi"), ("d0", "gu0"), ("wo", "qkv"))


def _hbm(a):
    return pltpu.with_memory_space_constraint(a, pltpu.HBM)


def _remote(src, dst, send, recv, peer):
    return pltpu.make_async_remote_copy(src_ref=src, dst_ref=dst, send_sem=send, recv_sem=recv, device_id=peer,
                                        device_id_type=pl.DeviceIdType.MESH)


ALL_KS = tuple(range(1, NDEV))
AG_KS1 = (1, 2, 4, 6)
AG_KS2 = (2, 4, 6)
RS_KS_PAIR = (1, 3, 5, 7)
RS_KS_CHIPS = (2, 4, 6)


def _split_start(srcs, src_of, lands, copy_refs, name, deps=(), ks=ALL_KS, to=None):
    ns, n, nd, nk = len(srcs), len(lands), len(deps), len(ks)

    def body(*refs):
        ins, land = refs[:ns], refs[ns:ns + n]
        send, recv = refs[ns + n + nd], refs[ns + n + nd + 1]
        token = refs[-1]
        x, y, c, me = _mesh_pos()
        for j in range(n):
            for i, k in enumerate(ks):
                _, pid = _peer(x, y, c, k)
                dest, _ = _peer(x, y, c, k if to is None else to)
                src, dst = copy_refs(j, (land[j] if src_of[j] is None else ins[src_of[j]]), land[j], me, pid, i)
                _remote(src, dst, send.at[j * nk + i], recv.at[j * nk + i], dest).start()
        token[...] = jnp.zeros_like(token)

    outs = pl.pallas_call(
        body, name=name,
        out_shape=(pltpu.SemaphoreType.DMA((n * nk,)), pltpu.SemaphoreType.DMA((n * nk,)))
        + tuple(pltpu.HBM(a.shape, a.dtype) for a in srcs) + tuple(pltpu.HBM(a.shape, a.dtype) for a in lands)
        + (jax.ShapeDtypeStruct((8, 128), F32),),
        in_specs=(HBM,) * (ns + n) + (ANY,) * nd,
        out_specs=(SEMS, SEMS) + (HBM,) * (ns + n) + (pl.BlockSpec(memory_space=pltpu.VMEM),),
        input_output_aliases={i: 2 + i for i in range(ns + n)},
        compiler_params=pltpu.CompilerParams(has_side_effects=EFFECT),
    )(*[_hbm(a) for a in srcs], *[_hbm(a) for a in lands], *deps)
    return outs[0], outs[1], list(outs[2:2 + ns]), list(outs[2 + ns:2 + ns + n]), outs[-1]


def _split_wait(srcs, src_of, lands, send, recv, sem_rows, wait_refs, after, name, ks=ALL_KS):
    ns, n, nk = len(srcs), len(lands), len(ks)
    after = tuple(after) if isinstance(after, (tuple, list)) else (after,)

    def body(*refs):
        ins, land = refs[:ns], refs[ns:ns + n]
        send_ref, recv_ref = refs[ns + n], refs[ns + n + 1]
        x, y, c, me = _mesh_pos()
        for j in range(n):
            for i, k in enumerate(ks):
                peer, _ = _peer(x, y, c, k)
                src, dst = wait_refs(j, (land[j] if src_of[j] is None else ins[src_of[j]]), land[j])
                sem = sem_rows[j] * nk + i
                cp = _remote(src, dst, send_ref.at[sem], recv_ref.at[sem], peer)
                cp.wait_send()
                cp.wait_recv()

    outs = pl.pallas_call(
        body, name=name,
        out_shape=tuple(pltpu.HBM(a.shape, a.dtype) for a in srcs) + tuple(pltpu.HBM(a.shape, a.dtype) for a in lands),
        in_specs=(HBM,) * (ns + n) + (SEMS, SEMS) + (ANY,) * len(after),
        out_specs=(HBM,) * (ns + n),
        input_output_aliases={i: i for i in range(ns + n)},
        compiler_params=pltpu.CompilerParams(has_side_effects=EFFECT),
    )(*srcs, *lands, send, recv, *after)
    return list(outs[:ns]), list(outs[ns:])


def _ag_dtype(name):
    return F32 if name == "pv" else BF16


def _ag_align(name):
    return 8 if name == "pv" else 16


def _place_transposed(w, me, name):
    rows = w.shape[1]
    nblk = rows // 128

    def kern(me_ref, w_ref, o_ref):
        o_ref[...] = w_ref[...].T.astype(BF16)

    grid_spec = pltpu.PrefetchScalarGridSpec(
        num_scalar_prefetch=1, grid=(nblk,),
        in_specs=[pl.BlockSpec((D, 128), lambda i, me_ref: (0, i))],
        out_specs=pl.BlockSpec((128, D), lambda i, me_ref: (me_ref[0] * nblk + i, 0)))
    return pl.pallas_call(
        kern, grid_spec=grid_spec, out_shape=jax.ShapeDtypeStruct((NDEV * rows, D), BF16),
        compiler_params=_cparams(1), name=name)(me.reshape(1).astype(jnp.int32), w)


class _Comm:
    def __init__(self, params, make_shards, me, placed):
        self.me = me
        self.ag_land, self.ag_sems, self.ag_tokens, self.ag_passing = {}, {}, (), {}
        self.rs = []
        deps = ()
        for part, names in enumerate((AG_GROUPS[0], AG_ORDER[len(AG_GROUPS[0]):])):
            rows = [SEC_ROWS[n] for n in names]
            if part == 0:
                lands = [placed[n] for n in names]
            else:
                params, deps = lax.optimization_barrier((params, deps))
                shards = make_shards(*params)
                lands = [lax.dynamic_update_slice(lax.empty((NDEV * r, shards[n].shape[1]), _ag_dtype(n)),
                                                  shards[n].astype(_ag_dtype(n)), (_shard_pos(n, me), 0))
                         for n, r in zip(names, rows)]

            def copy_refs(j, src, land, me, pid, i, names=names, rows=rows):
                own = land.at[pl.ds(pl.multiple_of(_shard_pos(names[j], me), _ag_align(names[j])), rows[j])]
                return own, own

            send, recv, _, lands, token = _split_start([], [None] * len(names), lands, copy_refs, f"ag_start{part}",
                                                       deps=deps, ks=AG_KS1)
            deps = (token,)
            self.ag_tokens += (token,)
            for j, n in enumerate(names):
                self.ag_land[n] = lands[j]
                self.ag_sems[n] = (send, recv, j)

    def pass_on(self, group, after):
        names = AG_GROUPS[group]
        send, recv = self.ag_sems[names[0]][:2]
        idx = [self.ag_sems[n][2] for n in names]
        rows = [SEC_ROWS[n] for n in names]
        none = [None] * len(names)

        def wait_refs(j, src, land):
            return land.at[pl.ds(0, rows[j])], land.at[pl.ds(0, rows[j])]

        _, lands = _split_wait([], none, [self.ag_land[n] for n in names], send, recv, idx,
                               wait_refs, after, f"ag_wait{group}", ks=AG_KS1)

        def copy_refs(j, src, land, me, pid, i):
            theirs = land.at[pl.ds(pl.multiple_of(_shard_pos(names[j], pid), _ag_align(names[j])), rows[j])]
            return theirs, theirs

        send, recv, _, lands, token = _split_start([], none, lands, copy_refs, f"ag_pass{group}", ks=AG_KS2, to=1)
        self.ag_passing[group] = (send, recv, lands, wait_refs)
        return token

    def weights(self, group, after):
        names = AG_GROUPS[group]
        if group not in self.ag_passing:
            after = self.pass_on(group, after)
        send, recv, lands, wait_refs = self.ag_passing[group]
        _, lands = _split_wait([], [None] * len(names), lands, send, recv, list(range(len(names))), wait_refs, after,
                               f"ag_pass_wait{group}", ks=AG_KS2)
        return dict(zip(names, lands))

    def send_grads(self, group, gws):
        names = RS_GROUPS[group]
        rows = [SEC_ROWS[n] for n in names]
        grads = [gws[n] for n in names]
        me = self.me
        lands = [lax.dynamic_update_slice(
            lax.empty((NDEV, r, D), BF16),
            lax.dynamic_slice(g, (_shard_pos(n, me), 0), (r, D))[None], (me, 0, 0))
            for n, r, g in zip(names, rows, grads)]

        def copy_refs(j, src, land, me, pid, i):
            return src.at[pl.ds(pl.multiple_of(_shard_pos(names[j], pid), 16), rows[j])], land.at[me]

        send, recv, srcs, lands, token = _split_start(grads, list(range(len(names))), lands, copy_refs,
                                                      f"rs_start{group}")
        self.rs.append((names, rows, send, recv, srcs, lands, ALL_KS))
        return token

    def send_grads_pairwise(self, gws):
        names = RS_GROUPS[-1]
        rows = [SEC_ROWS[n] for n in names]
        grads = [gws[n] for n in names]
        idx = list(range(len(names)))
        lands = [lax.empty((len(RS_KS_PAIR), r, D), BF16) for r in rows]

        def copy_refs(j, src, land, me, pid, i):
            return src.at[pl.ds(pl.multiple_of(_shard_pos(names[j], pid), 16), rows[j])], land.at[i]

        send, recv, srcs, lands, token = _split_start(grads, idx, lands, copy_refs, "rs_pair_start",
                                                      ks=RS_KS_PAIR, to=1)
        self.pair = (names, rows, send, recv, srcs, lands)
        return token

    def pass_grads(self, after):
        names, rows, send, recv, srcs, lands = self.pair
        idx = list(range(len(names)))
        me = self.me

        def wait_refs(j, src, land):
            return src.at[pl.ds(0, rows[j])], land.at[0]

        srcs, lands = _split_wait(srcs, idx, lands, send, recv, idx, wait_refs, after, "rs_pair_wait", ks=RS_KS_PAIR)
        sums = [_pair_sum(g, got, me, f"rs_pair_sum_{n}") for n, g, got in zip(names, srcs, lands)]
        lands = [lax.dynamic_update_slice(lax.empty(p.shape, BF16), p[0:1], (0, 0, 0)) for p in sums]

        def copy_refs(j, src, land, me, pid, i):
            return src.at[i + 1], land.at[i + 1]

        send, recv, sums, lands, token = _split_start(sums, idx, lands, copy_refs, f"rs_start{len(RS_GROUPS) - 1}",
                                                      ks=RS_KS_CHIPS)
        self.rs.append((names, rows, send, recv, sums, lands, RS_KS_CHIPS))
        return token

    def received(self, group, after):
        names, rows, send, recv, srcs, lands, ks = self.rs[group]
        whole = srcs[0].ndim == 2

        def wait_refs(j, src, land):
            return (src.at[pl.ds(0, rows[j])] if whole else src.at[0]), land.at[0]

        _, lands = _split_wait(srcs, list(range(len(names))), lands, send, recv, list(range(len(names))), wait_refs,
                               after, f"rs_wait{group}", ks=ks)
        return dict(zip(names, lands))


def _pair_sum(grad, got, me, name):
    n, rows, _ = got.shape
    tr = 384 if rows % 384 == 0 else rows
    nt = rows // tr

    def kern(me_ref, g_ref, b_ref, o_ref):
        o_ref[0] = (g_ref[...].astype(F32) + b_ref[0].astype(F32)).astype(BF16)

    blk = pl.BlockSpec((1, tr, D), lambda i, t, me_ref: (i, t, 0))
    grid_spec = pltpu.PrefetchScalarGridSpec(
        num_scalar_prefetch=1, grid=(n, nt),
        in_specs=[pl.BlockSpec((tr, D), lambda i, t, me_ref: (jnp.bitwise_xor(me_ref[0], 2 * i) * nt + t, 0)), blk],
        out_specs=blk)
    return pl.pallas_call(
        kern, grid_spec=grid_spec, out_shape=jax.ShapeDtypeStruct(got.shape, BF16),
        compiler_params=_cparams(2), name=name)(me.reshape(1).astype(jnp.int32), grad, got)


def _sum_contributions(r_ref):
    g = r_ref[0].astype(F32)
    for slot in range(1, r_ref.shape[0]):
        g = g + r_ref[slot].astype(F32)
    return g


def _adam_math(g, w, m, v):
    c1 = 1.0 / (1.0 - ADAM_B1 ** ADAM_STEP)
    c2 = 1.0 / (1.0 - ADAM_B2 ** ADAM_STEP)
    mn = ADAM_B1 * m + (1.0 - ADAM_B1) * g
    vn = ADAM_B2 * v + (1.0 - ADAM_B2) * (g * g)
    return -ADAM_LR * ((mn * c1) / (jnp.sqrt(vn * c2) + ADAM_EPS) + ADAM_WD * w), mn, vn


def _adamw(R, w, m, v, *, tr, name, layer=None, prev=None):
    rows, C = w.shape[-2:]
    nprev = 0 if prev is None else 4

    def kern(r_ref, w_ref, m_ref, v_ref, *rest):
        g_out, d_out, m_out, v_out = rest[nprev:]
        g = _sum_contributions(r_ref)
        g_out[...] = g
        d_out[...], m_out[...], v_out[...] = _adam_math(g, w_ref[...], m_ref[...], v_ref[...])

    if layer is None:
        tile = pl.BlockSpec((tr, C), lambda i: (i, 0))
    else:
        tile = pl.BlockSpec((None, tr, C), lambda i: (layer, i, 0))
    shp = jax.ShapeDtypeStruct(w.shape, F32)
    return pl.pallas_call(
        kern, grid=(rows // tr,),
        in_specs=[pl.BlockSpec((R.shape[0], tr, C), lambda i: (0, i, 0)), tile, tile, tile]
        + [pl.BlockSpec(memory_space=pl.ANY)] * nprev,
        out_specs=[tile] * 4, out_shape=[shp] * 4,
        input_output_aliases={4 + k: k for k in range(nprev)},
        compiler_params=_cparams(1), name=name)(R, w, m, v, *(prev or ()))


def _adamw_pool_group(R, w, m, v):
    rows = SEC_ROWS["pg"]

    def kern(r_ref, w_ref, m_ref, v_ref, g_out, d_out, m_out, v_out):
        g = _sum_contributions(r_ref)
        g_out[0] = g
        d_out[0], m_out[0], v_out[0] = _adam_math(g, w_ref[0], m_ref[0], v_ref[0])

    blk = pl.BlockSpec((1, rows, PGD), lambda i: (i, 0, 0))
    shp = jax.ShapeDtypeStruct((POOL_G, rows, PGD), F32)
    return pl.pallas_call(
        kern, grid=(POOL_G,),
        in_specs=[pl.BlockSpec((NDEV, rows, PGD), lambda i: (0, 0, i)), blk, blk, blk],
        out_specs=[blk] * 4, out_shape=[shp] * 4, compiler_params=_cparams(1), name="adamw_pg")(R, w, m, v)


def _adamw_transposed(R, w, m, v, name):
    rows = R.shape[1]
    tr = 128

    def kern(r_ref, w_ref, m_ref, v_ref, g_out, d_out, m_out, v_out):
        g = _sum_contributions(r_ref).T
        g_out[...] = g
        d_out[...], m_out[...], v_out[...] = _adam_math(g, w_ref[...], m_ref[...], v_ref[...])

    tile = pl.BlockSpec((D, tr), lambda i: (0, i))
    shp = jax.ShapeDtypeStruct((D, rows), F32)
    return pl.pallas_call(
        kern, grid=(rows // tr,),
        in_specs=[pl.BlockSpec((R.shape[0], tr, D), lambda i: (0, i, 0)), tile, tile, tile],
        out_specs=[tile] * 4, out_shape=[shp] * 4, compiler_params=_cparams(1), name=name)(R, w, m, v)


def _pack_sections(w_qkv, w_attn_out, w_pool_in, w_pool_group, w_ffn_gate_up, w_ffn_down):
    pg = w_pool_group[0].transpose(1, 0, 2).reshape(SEC_ROWS["pg"], D)
    return {"qkv": w_qkv[0].T, "wo": w_attn_out[0], "wpi": w_pool_in[0], "gu0": w_ffn_gate_up[0].T,
            "gu1": w_ffn_gate_up[1].T, "d0": w_ffn_down[0], "d1": w_ffn_down[1], "pg": pg}


def _vec_pack(attn_norm, ffn_norm, final_norm, pool_norm_sh, pool_scale_sh, me):
    def place(sh):
        return lax.dynamic_update_slice(jnp.zeros((1, D), F32), sh, (0, me * 128))
    return jnp.concatenate([attn_norm, ffn_norm, final_norm.reshape(1, D), place(pool_norm_sh),
                            place(pool_scale_sh), jnp.zeros((2, D), F32)], axis=0)


def _vec_unpack(p, me):
    def take(r):
        return lax.dynamic_slice(p[r:r + 1], (0, me * 128), (1, 128))
    return p[0:1], p[1:3], p[3], take(4), take(5)


def kernel(x, attn_norm, w_qkv, w_attn_out, pool_norm, w_pool_in, w_pool_group, pool_scale, ffn_norm, w_ffn_gate_up, w_ffn_down, final_norm, loss_target, m_attn_norm, m_w_qkv, m_w_attn_out, m_pool_norm, m_w_pool_in, m_w_pool_group, m_pool_scale, m_ffn_norm, m_w_ffn_gate_up, m_w_ffn_down, m_final_norm, v_attn_norm, v_w_qkv, v_w_attn_out, v_pool_norm, v_w_pool_in, v_w_pool_group, v_pool_scale, v_ffn_norm, v_w_ffn_gate_up, v_w_ffn_down, v_final_norm):
    me = 4 * lax.axis_index("x") + 2 * lax.axis_index("y") + lax.axis_index("c")

    def make_shards(wq, wo, wpi, wpg, wgu, wd, pn, ps):
        shards = _pack_sections(wq, wo, wpi, wpg, wgu, wd)
        shards["pv"] = jnp.concatenate([pn, ps, jnp.zeros((6, 128), F32)], axis=0)
        return shards

    comm = _Comm((w_qkv, w_attn_out, w_pool_in, w_pool_group, w_ffn_gate_up, w_ffn_down, pool_norm, pool_scale),
                 make_shards, me, placed={"qkv": _place_transposed(w_qkv[0], me, "place_qkv")})

    gu_t = [jnp.swapaxes(a, 1, 2) for a in (w_ffn_gate_up, m_w_ffn_gate_up, v_w_ffn_gate_up)]
    early = {}
    grad_x, vec = _local_step(x[0], loss_target[0], comm, attn_norm, ffn_norm, final_norm,
                              opt={"d": (w_ffn_down, m_w_ffn_down, v_w_ffn_down), "gu": gu_t}, early=early)

    small = ((attn_norm, ffn_norm, final_norm, pool_norm, pool_scale),
             (m_attn_norm, m_ffn_norm, m_final_norm, m_pool_norm, m_pool_scale),
             (v_attn_norm, v_ffn_norm, v_final_norm, v_pool_norm, v_pool_scale))
    small, grad_x = lax.optimization_barrier((small, grad_x))
    vw, vm, vv = (_vec_pack(*s, me) for s in small)

    res = {}
    gu_res, d_res = early["gu"], early["d"]
    vec_out = None
    vec_land = lax.dynamic_update_slice(lax.empty((NDEV, 8, D), F32), vec[None], (me, 0, 0))
    vec_sems = _split_start([vec], [0], [vec_land], lambda j, src, land, me_, pid, i: (src, land.at[me_]),
                            "vec_start")
    after = (grad_x, vec_sems[4])
    for group in range(1, len(RS_GROUPS)):
        if group == len(RS_GROUPS) - 1:
            _, (VR,) = _split_wait(vec_sems[2], [0], vec_sems[3], vec_sems[0], vec_sems[1], [0],
                                   lambda j, src, land: (src, land.at[0]), after, "vec_wait")
            vec_out = _adamw(VR, vw, vm, vv, tr=8, name="adamw_vec")
            after = vec_out[0]
        for n, R in comm.received(group, after).items():
            if n in ("d0", "d1"):
                d_res = _adamw(R, w_ffn_down, m_w_ffn_down, v_w_ffn_down, tr=352, name=f"adamw_{n}",
                               layer=int(n[1]), prev=d_res)
                after = d_res[0]
            elif n in ("gu0", "gu1"):
                gu_res = _adamw(R, *gu_t, tr=352, name=f"adamw_{n}", layer=int(n[2]), prev=gu_res)
                after = gu_res[0]
            elif n == "pg":
                out = _adamw_pool_group(R, w_pool_group[0], m_w_pool_group[0], v_w_pool_group[0])
                res["pg"] = tuple(a[None] for a in out)
                after = out[0]
            elif n in ("wo", "wpi"):
                w, m, v = ((w_attn_out, m_w_attn_out, v_w_attn_out) if n == "wo"
                           else (w_pool_in, m_w_pool_in, v_w_pool_in))
                res[n] = _adamw(R, w[0], m[0], v[0], tr=128, name=f"adamw_{n}")
                res[n] = tuple(a[None] for a in res[n])
                after = res[n][0]
            else:
                out = _adamw_transposed(R, w_qkv[0], m_w_qkv[0], v_w_qkv[0], "adamw_qkv")
                res["qkv"] = tuple(a[None] for a in out)
                after = out[0]
    res["gu"] = tuple(jnp.swapaxes(a, 1, 2) for a in gu_res)
    res["d"] = tuple(d_res)

    outs = []
    for kind in range(4):
        an, fn, fin, pn, ps = _vec_unpack(vec_out[kind], me)
        outs.append((an, res["qkv"][kind], res["wo"][kind], pn, res["wpi"][kind], res["pg"][kind], ps, fn,
                     res["gu"][kind], res["d"][kind], fin))
    loss = 0.5 * jnp.sum(vec_out[0][6]) / D
    return (loss, grad_x[None]) + outs[0] + outs[1] + outs[2] + outs[3]
```

```python
import jax
import jax.numpy as jnp
from jax import lax
from jax.experimental import pallas as pl
from jax.experimental.pallas import tpu as pltpu

F32 = jnp.float32
BF16 = jnp.bfloat16

D = 1024
NDEV = 8
HEADS = 8
HD = 128
QB = 128
NGROUPS = 3
DILS = (1, 4, 16)
DFF = 2816
HCH = 1408
POOL_G = 4
PGD = 256
RMS_EPS = 1e-6
NEG = -1e30

ADAM_LR = 0.001
ADAM_B1 = 0.9
ADAM_B2 = 0.999
ADAM_EPS = 1e-08
ADAM_WD = 0.01
ADAM_STEP = 10

VMEM_LIMIT = 52 * 1024 * 1024

SECTIONS = (("qkv", 1152), ("wo", 128), ("wpi", 128), ("gu0", 704), ("gu1", 704),
            ("d0", 352), ("d1", 352), ("pg", 32))
SEC_ROWS = dict(SECTIONS)
SEC_ROWS["pv"] = 8


def _cparams(n_grid):
    return pltpu.CompilerParams(dimension_semantics=("arbitrary",) * n_grid, vmem_limit_bytes=VMEM_LIMIT)


def _shard_pos(name, dev):
    n = SEC_ROWS[name]
    if name in ("gu0", "gu1"):
        return ((dev % 4) // 2) * (2 * HCH) + (dev // 4) * HCH + (dev % 2) * n
    return dev * n


def _mm(a, b, *, mode, M, N, K, tm, tn, tk, out_dtype, name, a_off=(0, 0), b_off=(0, 0), res=None,
        out_rows=None, out_off=0, out_prev=None, deps=()):
    nm, nn, nk = M // tm, N // tn, K // tk
    assert nm * tm == M and nn * tn == N and nk * tk == K
    if mode == "nn":
        a_bs, b_bs = (tm, tk), (tk, tn)
        a_ix = lambda i, j, k: (i, k)
        b_ix = lambda i, j, k: (k, j)
        dims = (((1,), (0,)), ((), ()))
    elif mode == "nt":
        a_bs, b_bs = (tm, tk), (tn, tk)
        a_ix = lambda i, j, k: (i, k)
        b_ix = lambda i, j, k: (j, k)
        dims = (((1,), (1,)), ((), ()))
    else:
        a_bs, b_bs = (tk, tm), (tk, tn)
        a_ix = lambda i, j, k: (k, i)
        b_ix = lambda i, j, k: (k, j)
        dims = (((0,), (0,)), ((), ()))

    def spec(bs, ix, off):
        def im(i, j, k):
            r, c = ix(i, j, k)
            return (r + off[0], c + off[1])
        return pl.BlockSpec(bs, im)

    in_specs = [spec(a_bs, a_ix, a_off), spec(b_bs, b_ix, b_off)]
    args = [a, b]
    if res is not None:
        in_specs.append(pl.BlockSpec((tm, tn), lambda i, j, k: (i, j)))
        args.append(res)
    out_shape = jax.ShapeDtypeStruct((M if out_rows is None else out_rows, N), out_dtype)
    out_spec = pl.BlockSpec((tm, tn), lambda i, j, k: (i + out_off, j))
    has_res = res is not None
    extra = list(deps) + ([out_prev] if out_prev is not None else [])
    for dep in extra:
        in_specs.append(pl.BlockSpec(memory_space=pl.ANY))
        args.append(dep)
    o_pos = 2 + int(has_res) + len(extra)
    aliases = {len(args) - 1: 0} if out_prev is not None else {}

    def kern(*refs):
        a_ref, b_ref = refs[0], refs[1]
        res_ref = refs[2] if has_res else None
        o_ref = refs[o_pos]
        av = a_ref[...]
        bv = b_ref[...]
        if av.dtype != BF16:
            av = av.astype(BF16)
        if bv.dtype != BF16:
            bv = bv.astype(BF16)
        part = lax.dot_general(av, bv, dims, preferred_element_type=F32)

        def write(val):
            if has_res:
                val = val + res_ref[...]
            o_ref[...] = val.astype(out_dtype)

        if nk == 1:
            write(part)
        else:
            acc_ref = refs[-1]
            k = pl.program_id(2)

            @pl.when(k == 0)
            def _():
                acc_ref[...] = part

            @pl.when(k > 0)
            def _():
                acc_ref[...] += part

            @pl.when(k == nk - 1)
            def _():
                write(acc_ref[...])

    scratch = [pltpu.VMEM((tm, tn), F32)] if nk > 1 else []
    return pl.pallas_call(
        kern, grid=(nm, nn, nk), in_specs=in_specs, out_specs=out_spec, out_shape=out_shape,
        scratch_shapes=scratch, input_output_aliases=aliases, compiler_params=_cparams(3), name=name)(*args)


def _mm_rms_bwd(a, b, x, g, dres, *, mode, M, K, tm, name, b_off=(0, 0), deps=(), folded=()):
    nd, nf = len(deps), len(folded)
    b_bs = (K, D) if mode == "nn" else (D, K)
    dims = (((1,), (0,)), ((), ())) if mode == "nn" else (((1,), (1,)), ((), ()))

    def kern(a_ref, b_ref, x_ref, g_ref, dres_ref, *rest):
        f_refs = rest[:nf]
        dx_ref, dg_ref = rest[nf + nd:nf + nd + 2]
        i = pl.program_id(0)
        av = a_ref[...]
        if av.dtype != BF16:
            av = av.astype(BF16)
        dhv = lax.dot_general(av, b_ref[...], dims, preferred_element_type=F32)
        if nf:
            acc_ref = rest[-1]
            _chunks_put(acc_ref, dhv)
            for f_ref in f_refs:
                dil = f_ref.shape[0]
                for res in range(dil):
                    _chunks_add_rows(acc_ref, f_ref[res], res, tm // dil, dil, True)
            dhv = _chunks_get(acc_ref)
        xv = x_ref[...]
        r = lax.rsqrt(jnp.mean(xv * xv, axis=-1, keepdims=True) + RMS_EPS)
        xhat = xv * r
        gy = dhv * g_ref[...]
        dx_ref[...] = dres_ref[...] + r * (gy - xhat * jnp.mean(gy * xhat, axis=-1, keepdims=True))
        part = jnp.sum(dhv * xhat, axis=0, keepdims=True)

        @pl.when(i == 0)
        def _():
            dg_ref[...] = part

        @pl.when(i > 0)
        def _():
            dg_ref[...] += part

    row = pl.BlockSpec((tm, D), lambda i: (i, 0))
    vec = pl.BlockSpec((1, D), lambda i: (0, 0))
    return pl.pallas_call(
        kern, grid=(M // tm,),
        in_specs=[pl.BlockSpec((tm, K), lambda i: (i, 0)),
                  pl.BlockSpec(b_bs, lambda i: b_off, pipeline_mode=pl.Buffered(1)), row, vec, row]
        + [pl.BlockSpec((f.shape[0], tm // f.shape[0], D), lambda i: (0, i, 0)) for f in folded]
        + [pl.BlockSpec(memory_space=pl.ANY)] * nd,
        out_specs=[row, vec],
        out_shape=[jax.ShapeDtypeStruct((M, D), F32), jax.ShapeDtypeStruct((1, D), F32)],
        scratch_shapes=[pltpu.VMEM((D // 128, tm, 128), F32)] if nf else [],
        compiler_params=_cparams(1), name=name)(a, b, x, g, dres, *folded, *deps)


def _norm_tail(xv, gv, rest, head):
    r = lax.rsqrt(jnp.mean(xv * xv, axis=-1, keepdims=True) + RMS_EPS)
    xhat = xv * r
    if not head:
        xo_ref, h_ref = rest
        xo_ref[...] = xv
        h_ref[...] = (xhat * gv).astype(BF16)
        return
    t_ref, dx_ref, dg_ref, ls_ref = rest
    i = pl.program_id(0)
    e = xhat * gv - t_ref[...]
    dy = e * (1.0 / D)
    gy = dy * gv
    dx_ref[...] = r * (gy - xhat * jnp.mean(gy * xhat, axis=-1, keepdims=True))
    dgp = jnp.sum(dy * xhat, axis=0, keepdims=True)
    lsp = jnp.sum(e * e, axis=0, keepdims=True)

    @pl.when(i == 0)
    def _():
        dg_ref[...] = dgp
        ls_ref[...] = lsp

    @pl.when(i > 0)
    def _():
        dg_ref[...] += dgp
        ls_ref[...] += lsp


def _ffn_fwd(h, wgu, wd, res, g, *, name, tgt=None):
    S = h.shape[0]
    tm = 256
    nj = DFF // HCH
    head = tgt is not None

    def kern(h_ref, wgu_ref, wd_ref, res_ref, g_ref, *rest):
        t_refs, (gu_ref, act_ref), tail = rest[:int(head)], rest[int(head):int(head) + 2], rest[int(head) + 2:]
        hv = h_ref[...]
        for j in range(nj):
            gu = lax.dot_general(hv, wgu_ref[2 * HCH * j:2 * HCH * (j + 1), :], (((1,), (1,)), ((), ())),
                                 preferred_element_type=F32)
            gu_ref[:, 2 * HCH * j:2 * HCH * (j + 1)] = gu.astype(BF16)
            gate = gu[:, :HCH]
            act_ref[:, HCH * j:HCH * (j + 1)] = (gate * jax.nn.sigmoid(gate) * gu[:, HCH:]).astype(BF16)
        xv = res_ref[...] + jnp.dot(act_ref[...], wd_ref[...], preferred_element_type=F32)
        _norm_tail(xv, g_ref[...], tuple(t_refs) + tuple(tail), head)

    row = pl.BlockSpec((tm, D), lambda i: (i, 0))
    vec = pl.BlockSpec((1, D), lambda i: (0, 0))
    in_specs = [row, pl.BlockSpec((2 * DFF, D), lambda i: (0, 0), pipeline_mode=pl.Buffered(1)),
                pl.BlockSpec((DFF, D), lambda i: (0, 0), pipeline_mode=pl.Buffered(1)), row, vec]
    out_specs = [pl.BlockSpec((tm, 2 * DFF), lambda i: (i, 0)), pl.BlockSpec((tm, DFF), lambda i: (i, 0))]
    out_shape = [jax.ShapeDtypeStruct((S, 2 * DFF), BF16), jax.ShapeDtypeStruct((S, DFF), BF16)]
    args = [h, wgu, wd, res, g]
    if head:
        in_specs, args = in_specs + [row], args + [tgt]
        out_specs += [row, vec, vec]
        out_shape += [jax.ShapeDtypeStruct((S, D), F32), jax.ShapeDtypeStruct((1, D), F32),
                      jax.ShapeDtypeStruct((1, D), F32)]
    else:
        out_specs += [row, row]
        out_shape += [jax.ShapeDtypeStruct((S, D), F32), jax.ShapeDtypeStruct((S, D), BF16)]
    outs = pl.pallas_call(kern, grid=(S // tm,), in_specs=in_specs, out_specs=out_specs, out_shape=out_shape,
                          compiler_params=_cparams(1), name=name)(*args)
    return outs[0], outs[1], tuple(outs[2:])


def _mm_res_norm(a, b, res, g, *, K, tm, name, b_off=(0, 0), tgt=None):
    M = a.shape[0]
    head = tgt is not None

    def kern(a_ref, b_ref, res_ref, g_ref, *rest):
        xv = res_ref[...] + jnp.dot(a_ref[...], b_ref[...], preferred_element_type=F32)
        _norm_tail(xv, g_ref[...], rest, head)

    row = pl.BlockSpec((tm, D), lambda i: (i, 0))
    vec = pl.BlockSpec((1, D), lambda i: (0, 0))
    in_specs = [pl.BlockSpec((tm, K), lambda i: (i, 0)),
                pl.BlockSpec((K, D), lambda i: b_off, pipeline_mode=pl.Buffered(1)), row, vec]
    if head:
        return pl.pallas_call(
            kern, grid=(M // tm,), in_specs=in_specs + [row], out_specs=[row, vec, vec],
            out_shape=[jax.ShapeDtypeStruct((M, D), F32), jax.ShapeDtypeStruct((1, D), F32),
                       jax.ShapeDtypeStruct((1, D), F32)],
            compiler_params=_cparams(1), name=name)(a, b, res, g, tgt)
    return pl.pallas_call(
        kern, grid=(M // tm,), in_specs=in_specs, out_specs=[row, row],
        out_shape=[jax.ShapeDtypeStruct((M, D), F32), jax.ShapeDtypeStruct((M, D), BF16)],
        compiler_params=_cparams(1), name=name)(a, b, res, g)


def _rms_fwd(x, g, name, deps=()):
    S = x.shape[0]
    tr = 512

    def kern(x_ref, g_ref, *rest):
        h_ref = rest[-1]
        xv = x_ref[...]
        r = lax.rsqrt(jnp.mean(xv * xv, axis=-1, keepdims=True) + RMS_EPS)
        h_ref[...] = (xv * r * g_ref[...]).astype(BF16)

    return pl.pallas_call(
        kern, grid=(S // tr,),
        in_specs=[pl.BlockSpec((tr, D), lambda i: (i, 0)), pl.BlockSpec((1, D), lambda i: (0, 0))]
        + [pl.BlockSpec(memory_space=pl.ANY)] * len(deps),
        out_specs=pl.BlockSpec((tr, D), lambda i: (i, 0)),
        out_shape=jax.ShapeDtypeStruct((S, D), BF16), compiler_params=_cparams(1), name=name)(x, g, *deps)


def _chunks_put(scr, val):
    for c in range(scr.shape[0]):
        scr[c] = val[:, c * 128:(c + 1) * 128]


def _chunks_get(scr):
    return jnp.concatenate([scr[c] for c in range(scr.shape[0])], axis=1)


def _chunks_rows(scr, r, n, dil):
    return jnp.concatenate([scr.at[c][pl.ds(r, n, stride=dil), :] for c in range(scr.shape[0])], axis=1)


def _chunks_add_rows(scr, val, r, n, dil, accumulate):
    for c in range(scr.shape[0]):
        rows = pl.ds(r, n, stride=dil)
        piece = val[:, c * 128:(c + 1) * 128]
        tile = scr.at[c]
        tile[rows, :] = tile[rows, :] + piece if accumulate else piece


def _rms_fwd_folded(x, g, name, deps=()):
    S = x.shape[0]
    tr = 512
    dils = DILS[1:]

    def kern(x_ref, g_ref, *rest):
        outs, scr = rest[len(deps):-1], rest[-1]
        xv = x_ref[...]
        r = lax.rsqrt(jnp.mean(xv * xv, axis=-1, keepdims=True) + RMS_EPS)
        h = (xv * r * g_ref[...]).astype(BF16)
        outs[0][...] = h
        _chunks_put(scr, h.astype(F32))
        for o_ref, dil in zip(outs[1:], dils):
            for res in range(dil):
                o_ref[res] = _chunks_rows(scr, res, tr // dil, dil).astype(BF16)

    return pl.pallas_call(
        kern, grid=(S // tr,),
        in_specs=[pl.BlockSpec((tr, D), lambda i: (i, 0)), pl.BlockSpec((1, D), lambda i: (0, 0))]
        + [pl.BlockSpec(memory_space=pl.ANY)] * len(deps),
        out_specs=[pl.BlockSpec((tr, D), lambda i: (i, 0))]
        + [pl.BlockSpec((dil, tr // dil, D), lambda i: (0, i, 0)) for dil in dils],
        out_shape=[jax.ShapeDtypeStruct((S, D), BF16)]
        + [jax.ShapeDtypeStruct((dil, S // dil, D), BF16) for dil in dils],
        scratch_shapes=[pltpu.VMEM((D // 128, tr, 128), F32)],
        compiler_params=_cparams(1), name=name)(x, g, *deps)


def _ffn_bwd(dxo, wd, wgu, gu, xin, gain, name):
    S = dxo.shape[0]
    tm = 256
    nj = DFF // HCH

    def kern(dx_ref, wd_ref, wgu_ref, gu_ref, x_ref, g_ref, dgu_ref, dxin_ref, dg_ref):
        i = pl.program_id(0)
        dxv = dx_ref[...]
        dxb = dxv.astype(BF16)
        for j in range(nj):
            c0 = 2 * HCH * j
            dact = lax.dot_general(dxb, wd_ref[HCH * j:HCH * (j + 1), :], (((1,), (1,)), ((), ())),
                                   preferred_element_type=F32)
            gate = gu_ref[:, c0:c0 + HCH].astype(F32)
            up = gu_ref[:, c0 + HCH:c0 + 2 * HCH].astype(F32)
            sig = jax.nn.sigmoid(gate)
            silu = gate * sig
            dgu_ref[:, c0:c0 + HCH] = (dact * up * (sig * (1.0 + gate * (1.0 - sig)))).astype(BF16)
            dgu_ref[:, c0 + HCH:c0 + 2 * HCH] = (dact * silu).astype(BF16)
        dhv = jnp.dot(dgu_ref[...], wgu_ref[...], preferred_element_type=F32)
        xv = x_ref[...]
        r = lax.rsqrt(jnp.mean(xv * xv, axis=-1, keepdims=True) + RMS_EPS)
        xhat = xv * r
        gy = dhv * g_ref[...]
        dxin_ref[...] = dxv + r * (gy - xhat * jnp.mean(gy * xhat, axis=-1, keepdims=True))
        part = jnp.sum(dhv * xhat, axis=0, keepdims=True)

        @pl.when(i == 0)
        def _():
            dg_ref[...] = part

        @pl.when(i > 0)
        def _():
            dg_ref[...] += part

    row = pl.BlockSpec((tm, D), lambda i: (i, 0))
    wide = pl.BlockSpec((tm, 2 * DFF), lambda i: (i, 0))
    vec = pl.BlockSpec((1, D), lambda i: (0, 0))
    return pl.pallas_call(
        kern, grid=(S // tm,),
        in_specs=[row, pl.BlockSpec((DFF, D), lambda i: (0, 0), pipeline_mode=pl.Buffered(1)),
                  pl.BlockSpec((2 * DFF, D), lambda i: (0, 0), pipeline_mode=pl.Buffered(1)), wide, row, vec],
        out_specs=[wide, row, vec],
        out_shape=[jax.ShapeDtypeStruct((S, 2 * DFF), BF16), jax.ShapeDtypeStruct((S, D), F32),
                   jax.ShapeDtypeStruct((1, D), F32)],
        compiler_params=_cparams(1), name=name)(dxo, wd, wgu, gu, xin, gain)


def _window(val, grp, backward):
    S = val.shape[0]
    row = lax.broadcasted_iota(jnp.int32, val.shape, 0)
    cnt = jnp.minimum(row + 1, 2 << grp).astype(F32)
    s = val / cnt if backward else val
    for k in (1, 2, 4, 8)[:grp + 1]:
        if backward:
            sh = jnp.where(row < S - k, pltpu.roll(s, S - k, 0), 0.0)
        else:
            sh = jnp.where(row >= k, pltpu.roll(s, k, 0), 0.0)
        s = s + sh
    return s - val if backward else s / cnt - val


def _pool_in_window(h, wpi):
    S = h.shape[0]

    def kern(h_ref, w_ref, o_ref):
        g = pl.program_id(0)
        u = jnp.dot(h_ref[...], w_ref[...], preferred_element_type=F32)
        for grp in range(POOL_G):
            @pl.when(g == grp)
            def _(grp=grp):
                o_ref[...] = _window(u, grp, False).astype(BF16)

    return pl.pallas_call(
        kern, grid=(POOL_G,),
        in_specs=[pl.BlockSpec((S, D), lambda g: (0, 0), pipeline_mode=pl.Buffered(1)),
                  pl.BlockSpec((D, PGD), lambda g: (0, g))],
        out_specs=pl.BlockSpec((S, PGD), lambda g: (0, g)),
        out_shape=jax.ShapeDtypeStruct((S, D), BF16), compiler_params=_cparams(1), name="pool_in")(h, wpi)


def _pool_out(yd, G, scale, xres):
    S = yd.shape[0]
    tm = min(S, 4096)

    def kern(y_ref, w_ref, s_ref, x_ref, o_ref):
        z = jnp.dot(y_ref[...], w_ref[...], preferred_element_type=F32)
        o_ref[...] = x_ref[...] + z * s_ref[...]

    tile = pl.BlockSpec((tm, PGD), lambda i, g: (i, g))
    return pl.pallas_call(
        kern, grid=(S // tm, POOL_G),
        in_specs=[tile, pl.BlockSpec((PGD, PGD), lambda i, g: (0, g)),
                  pl.BlockSpec((1, PGD), lambda i, g: (0, g)), tile],
        out_specs=tile, out_shape=jax.ShapeDtypeStruct((S, D), F32),
        compiler_params=_cparams(2), name="pool_out")(yd, G, scale, xres)


def _pool_out_bwd(dz, yd, G, scale, deps=()):
    S = yd.shape[0]
    nd = len(deps)

    def kern(dz_ref, y_ref, w_ref, s_ref, *rest):
        du_ref, ds_ref, dw_ref = rest[nd:]
        g = pl.program_id(0)
        dzv = dz_ref[...]
        yv = y_ref[...]
        wv = w_ref[...]
        zraw = jnp.dot(yv, wv, preferred_element_type=F32)
        ds_ref[...] = jnp.sum(dzv * zraw, axis=0, keepdims=True)
        dzr = (dzv * s_ref[...]).astype(BF16)
        dw_ref[...] = lax.dot_general(yv, dzr, (((0,), (0,)), ((), ())), preferred_element_type=F32).astype(BF16)
        dyd = lax.dot_general(dzr, wv, (((1,), (1,)), ((), ())), preferred_element_type=F32)
        for grp in range(POOL_G):
            @pl.when(g == grp)
            def _(grp=grp):
                du_ref[...] = _window(dyd, grp, True).astype(BF16)

    tile = pl.BlockSpec((S, PGD), lambda g: (0, g))
    return pl.pallas_call(
        kern, grid=(POOL_G,),
        in_specs=[tile, tile, pl.BlockSpec((PGD, PGD), lambda g: (0, g)),
                  pl.BlockSpec((1, PGD), lambda g: (0, g))] + [pl.BlockSpec(memory_space=pl.ANY)] * nd,
        out_specs=[tile, pl.BlockSpec((1, PGD), lambda g: (0, g)), pl.BlockSpec((PGD, PGD), lambda g: (0, g))],
        out_shape=[jax.ShapeDtypeStruct((S, D), BF16), jax.ShapeDtypeStruct((1, D), F32),
                   jax.ShapeDtypeStruct((PGD, D), BF16)],
        compiler_params=_cparams(1), name="pool_out_bwd")(dz, yd, G, scale, *deps)


def _bias_table():
    qi = jnp.arange(QB)[:, None]
    ki = jnp.arange(2 * QB)[None, :]
    delta = QB + qi - ki
    inband = (delta >= 0) & (delta <= QB)
    n = NGROUPS * HEADS
    slopes = jnp.exp2(-8.0 * jnp.arange(1, n + 1, dtype=F32) / n).reshape(NGROUPS, HEADS)
    dil = jnp.asarray(DILS, F32)
    bias = -slopes[:, :, None, None] * (delta.astype(F32)[None, None] * dil[:, None, None, None])
    return jnp.where(inband[None, None], bias, NEG)


def _attn_fwd(qkv_f, bias, nb, name):
    S = qkv_f.shape[0]
    nblk = S // QB
    scale = HD ** -0.5

    def kern(q_ref, k2_ref, kp_ref, v2_ref, vp_ref, b_ref, o_ref, l_ref, s_scr, p_scr, r_scr):
        s_id = pl.program_id(0)
        col = lax.broadcasted_iota(jnp.int32, (QB, 2 * QB), 1)
        lane = lax.broadcasted_iota(jnp.int32, (QB, HD), 1)

        def keys(sub, cur2_ref, prev_ref, sl):
            if sub:
                return cur2_ref[:, sl]
            return jnp.concatenate([prev_ref[:, sl], cur2_ref[0:QB, sl]], axis=0)

        for sub in range(2):
            for h in range(HEADS):
                sl = slice(h * HD, (h + 1) * HD)
                s_scr[sub * HEADS + h] = lax.dot_general(
                    q_ref[sub * QB:(sub + 1) * QB, sl], keys(sub, k2_ref, kp_ref, sl), (((1,), (1,)), ((), ())),
                    preferred_element_type=F32)
        for sub in range(2):
            has_prev = jnp.bitwise_and(2 * s_id + sub, nb - 1) != 0
            dead = jnp.logical_and(col < QB, jnp.logical_not(has_prev))
            lse_all = jnp.zeros((QB, HD), F32)
            for h in range(HEADS):
                u = sub * HEADS + h
                s = s_scr[u] * scale + b_ref[h]
                s = jnp.where(dead, NEG, s)
                m = jnp.max(s, axis=-1, keepdims=True)
                p = jnp.exp(s - m)
                den = jnp.sum(p, axis=-1, keepdims=True)
                p_scr[u] = p.astype(BF16)
                r_scr[u] = jnp.broadcast_to(1.0 / den, (QB, HD))
                lse_all = jnp.where(lane == h, m + jnp.log(den), lse_all)
            l_ref[sub * QB:(sub + 1) * QB, :] = lse_all
        for sub in range(2):
            for h in range(HEADS):
                u = sub * HEADS + h
                sl = slice(h * HD, (h + 1) * HD)
                o = jnp.dot(p_scr[u], keys(sub, v2_ref, vp_ref, sl), preferred_element_type=F32) * r_scr[u]
                o_ref[sub * QB:(sub + 1) * QB, sl] = o.astype(BF16)

    def pair(colblk):
        return pl.BlockSpec((2 * QB, D), lambda s: (s, colblk))

    def prev(colblk):
        return pl.BlockSpec((QB, D), lambda s: (jnp.maximum(2 * s - 1, 0), colblk))

    return pl.pallas_call(
        kern, grid=(nblk // 2,),
        in_specs=[pair(0), pair(1), prev(1), pair(2), prev(2), pl.BlockSpec((HEADS, QB, 2 * QB), lambda s: (0, 0, 0))],
        out_specs=[pl.BlockSpec((2 * QB, D), lambda s: (s, 0)), pl.BlockSpec((2 * QB, HD), lambda s: (s, 0))],
        out_shape=[jax.ShapeDtypeStruct((S, D), BF16), jax.ShapeDtypeStruct((S, HD), F32)],
        scratch_shapes=[pltpu.VMEM((2 * HEADS, QB, 2 * QB), F32), pltpu.VMEM((2 * HEADS, QB, 2 * QB), BF16),
                        pltpu.VMEM((2 * HEADS, QB, HD), F32)],
        compiler_params=_cparams(1), name=name)(qkv_f, qkv_f, qkv_f, qkv_f, qkv_f, bias)


def _natural(ref, scr, tm):
    dil = ref.shape[0]
    for res in range(dil):
        _chunks_add_rows(scr, ref[res].astype(F32), res, tm // dil, dil, False)
    return _chunks_get(scr)


def _attn_merge(os, lses):
    S = os[0].shape[0]
    tm = 512

    def kern(o0, o1, o2, l0, l1, l2, om_ref, lm_ref, ls1, ls2, os1, os2):
        la = l0[...]
        lb = _natural(l1, ls1, tm)
        lc = _natural(l2, ls2, tm)
        m = jnp.maximum(jnp.maximum(la, lb), lc)
        e0, e1, e2 = jnp.exp(la - m), jnp.exp(lb - m), jnp.exp(lc - m)
        tot = e0 + e1 + e2
        lm_ref[...] = m + jnp.log(tot)
        w0, w1, w2 = e0 / tot, e1 / tot, e2 / tot
        for res in range(o1.shape[0]):
            _chunks_add_rows(os1, o1[res].astype(F32), res, tm // o1.shape[0], o1.shape[0], False)
        for res in range(o2.shape[0]):
            _chunks_add_rows(os2, o2[res].astype(F32), res, tm // o2.shape[0], o2.shape[0], False)
        for h in range(HEADS):
            sl = slice(h * HD, (h + 1) * HD)
            acc = w0[:, h:h + 1] * o0[:, sl].astype(F32) + w1[:, h:h + 1] * os1[h] + w2[:, h:h + 1] * os2[h]
            om_ref[:, sl] = acc.astype(BF16)

    def spec(a, c):
        if a.ndim == 2:
            return pl.BlockSpec((tm, c), lambda i: (i, 0))
        return pl.BlockSpec((a.shape[0], tm // a.shape[0], c), lambda i: (0, i, 0))

    return pl.pallas_call(
        kern, grid=(S // tm,),
        in_specs=[spec(a, D) for a in os] + [spec(a, HD) for a in lses],
        out_specs=[pl.BlockSpec((tm, D), lambda i: (i, 0)), pl.BlockSpec((tm, HD), lambda i: (i, 0))],
        out_shape=[jax.ShapeDtypeStruct((S, D), BF16), jax.ShapeDtypeStruct((S, HD), F32)],
        scratch_shapes=[pltpu.VMEM((1, tm, HD), F32), pltpu.VMEM((1, tm, HD), F32),
                        pltpu.VMEM((HEADS, tm, HD), F32), pltpu.VMEM((HEADS, tm, HD), F32)],
        compiler_params=_cparams(1), name="attn_merge")(*os, *lses)


def _attn_bwd_prep(dx, wo, o, lse, deps=()):
    S = o.shape[0]
    tm = 512
    dils = DILS[1:]
    nd = len(deps)

    def kern(dx_ref, w_ref, o_ref, l_ref, *rest):
        rest = rest[nd:]
        do_outs, l_outs, d_outs = rest[0:3], rest[3:5], rest[5:8]
        do_scr, l_scr, d_scr = rest[8:11]
        dov = lax.dot_general(dx_ref[...].astype(BF16), w_ref[...], (((1,), (1,)), ((), ())),
                              preferred_element_type=F32)
        lane = lax.broadcasted_iota(jnp.int32, (tm, HD), 1)
        acc = jnp.zeros((tm, HD), F32)
        for h in range(HEADS):
            sl = slice(h * HD, (h + 1) * HD)
            prod = dov[:, sl] * o_ref[:, sl].astype(F32)
            acc = jnp.where(lane == h, jnp.sum(prod, axis=-1, keepdims=True), acc)
        d_scr[0] = acc
        l_scr[0] = l_ref[...]
        _chunks_put(do_scr, dov)
        do_outs[0][...] = dov.astype(BF16)
        d_outs[0][...] = acc
        for j, dil in enumerate(dils):
            for res in range(dil):
                n = tm // dil
                do_outs[1 + j][res] = _chunks_rows(do_scr, res, n, dil).astype(BF16)
                l_outs[j][res] = _chunks_rows(l_scr, res, n, dil)
                d_outs[1 + j][res] = _chunks_rows(d_scr, res, n, dil)

    def nat(c):
        return pl.BlockSpec((tm, c), lambda i: (i, 0))

    def fol(dil, c):
        return pl.BlockSpec((dil, tm // dil, c), lambda i: (0, i, 0))

    def shapes(c, dt, with_natural):
        first = [jax.ShapeDtypeStruct((S, c), dt)] if with_natural else []
        return first + [jax.ShapeDtypeStruct((dil, S // dil, c), dt) for dil in dils]

    outs = pl.pallas_call(
        kern, grid=(S // tm,),
        in_specs=[nat(D), pl.BlockSpec((D, D), lambda i: (0, 0), pipeline_mode=pl.Buffered(1)), nat(D), nat(HD)]
        + [pl.BlockSpec(memory_space=pl.ANY)] * nd,
        out_specs=[nat(D)] + [fol(dil, D) for dil in dils] + [fol(dil, HD) for dil in dils]
        + [nat(HD)] + [fol(dil, HD) for dil in dils],
        out_shape=shapes(D, BF16, True) + shapes(HD, F32, False) + shapes(HD, F32, True),
        scratch_shapes=[pltpu.VMEM((HEADS, tm, HD), F32), pltpu.VMEM((1, tm, HD), F32), pltpu.VMEM((1, tm, HD), F32)],
        compiler_params=_cparams(1), name="attn_out_bwd")(dx, wo, o, lse, *deps)
    return outs[0:3], [lse] + list(outs[3:5]), outs[5:8]


def _attn_bwd(qkv_f, do_f, lse_f, delta_f, bias, nb, name):
    S = qkv_f.shape[0]
    nblk = S // QB
    scale = HD ** -0.5

    npair = nblk // 2

    def kern(q_ref, k2_ref, kp_ref, v2_ref, vp_ref, do_ref, l_ref, d_ref, b_ref, out_ref, dq_c, dk_c, dv_c,
             s_scr, dp_scr, ds_scr, p_scr):
        s_id = pl.program_id(0)

        @pl.when(s_id == 0)
        def _():
            dq_c[...] = jnp.zeros_like(dq_c)
            dk_c[...] = jnp.zeros_like(dk_c)
            dv_c[...] = jnp.zeros_like(dv_c)

        @pl.when(s_id == npair)
        def _():
            out_ref[:, 0:D] = dq_c[...].astype(BF16)
            out_ref[:, D:2 * D] = dk_c[...].astype(BF16)
            out_ref[:, 2 * D:3 * D] = dv_c[...].astype(BF16)

        def keys(sub, cur2_ref, prev_ref, sl):
            if sub:
                return cur2_ref[:, sl]
            return jnp.concatenate([prev_ref[:, sl], cur2_ref[0:QB, sl]], axis=0)

        @pl.when(s_id < npair)
        def _():
            col = lax.broadcasted_iota(jnp.int32, (QB, 2 * QB), 1)
            out_ref[:, 0:D] = dq_c[...].astype(BF16)
            for sub in range(2):
                rows = slice(sub * QB, (sub + 1) * QB)
                for h in range(HEADS):
                    sl = slice(h * HD, (h + 1) * HD)
                    u = sub * HEADS + h
                    s_scr[u] = lax.dot_general(q_ref[rows, sl], keys(sub, k2_ref, kp_ref, sl),
                                               (((1,), (1,)), ((), ())), preferred_element_type=F32)
                    dp_scr[u] = lax.dot_general(do_ref[rows, sl], keys(sub, v2_ref, vp_ref, sl),
                                                (((1,), (1,)), ((), ())), preferred_element_type=F32)
            for sub in range(2):
                rows = slice(sub * QB, (sub + 1) * QB)
                has_prev = jnp.bitwise_and(2 * s_id + sub, nb - 1) != 0
                dead = jnp.logical_and(col < QB, jnp.logical_not(has_prev))
                lv = l_ref[rows, :]
                dv_ = d_ref[rows, :]
                for h in range(HEADS):
                    u = sub * HEADS + h
                    s = s_scr[u] * scale + b_ref[h]
                    s = jnp.where(dead, NEG, s)
                    p = jnp.exp(s - lv[:, h:h + 1])
                    ds_scr[u] = (p * (dp_scr[u] - dv_[:, h:h + 1]) * scale).astype(BF16)
                    p_scr[u] = p.astype(BF16)
            for h in range(HEADS):
                sl = slice(h * HD, (h + 1) * HD)
                parts = []
                for sub in range(2):
                    rows = slice(sub * QB, (sub + 1) * QB)
                    u = sub * HEADS + h
                    ds = ds_scr[u]
                    dq_c[rows, sl] = jnp.dot(ds, keys(sub, k2_ref, kp_ref, sl), preferred_element_type=F32)
                    dkk = lax.dot_general(ds, q_ref[rows, sl], (((0,), (0,)), ((), ())), preferred_element_type=F32)
                    dvv = lax.dot_general(p_scr[u], do_ref[rows, sl], (((0,), (0,)), ((), ())),
                                          preferred_element_type=F32)
                    parts.append((dkk, dvv))
                for which, carry, base in ((0, dk_c, D), (1, dv_c, 2 * D)):
                    first, second = parts[0][which], parts[1][which]
                    cols = slice(base + h * HD, base + (h + 1) * HD)
                    out_ref[0:QB, cols] = carry[0:QB, sl].astype(BF16)
                    out_ref[QB:2 * QB, cols] = (carry[QB:2 * QB, sl] + first[:QB]).astype(BF16)
                    carry[0:QB, sl] = first[QB:] + second[:QB]
                    carry[QB:2 * QB, sl] = second[QB:]

    last = npair - 1

    def pair(colblk, c):
        return pl.BlockSpec((2 * QB, c), lambda s: (jnp.minimum(s, last), colblk))

    def prev(colblk):
        return pl.BlockSpec((QB, D), lambda s: (jnp.maximum(2 * jnp.minimum(s, last) - 1, 0), colblk))

    return pl.pallas_call(
        kern, grid=(npair + 1,),
        in_specs=[pair(0, D), pair(1, D), prev(1), pair(2, D), prev(2), pair(0, D), pair(0, HD), pair(0, HD),
                  pl.BlockSpec((HEADS, QB, 2 * QB), lambda s: (0, 0, 0))],
        out_specs=pl.BlockSpec((2 * QB, 3 * D), lambda s: (jnp.maximum(s - 1, 0), 0)),
        out_shape=jax.ShapeDtypeStruct((S, 3 * D), BF16),
        scratch_shapes=[pltpu.VMEM((2 * QB, D), F32), pltpu.VMEM((2 * QB, D), F32), pltpu.VMEM((2 * QB, D), F32),
                        pltpu.VMEM((2 * HEADS, QB, 2 * QB), F32), pltpu.VMEM((2 * HEADS, QB, 2 * QB), F32),
                        pltpu.VMEM((2 * HEADS, QB, 2 * QB), BF16), pltpu.VMEM((2 * HEADS, QB, 2 * QB), BF16)],
        compiler_params=_cparams(1), name=name)(qkv_f, qkv_f, qkv_f, qkv_f, qkv_f, do_f, lse_f, delta_f, bias)


def _local_step(x, tgt, comm, attn_norm, ffn_norm, final_norm):
    S = x.shape[0]
    bias = _bias_table()
    g_attn = attn_norm.reshape(1, D)
    g_f0 = ffn_norm[0:1]
    g_f1 = ffn_norm[1:2]
    g_fin = final_norm.reshape(1, D)
    W = {}

    def ffn_fwd(xin, h, l, next_gain, target=None):
        return _ffn_fwd(h, W[f"gu{l}"], W[f"d{l}"], xin, next_gain, tgt=target, name=f"ffn_fwd{l}")

    def ffn_bwd(dxo, xin, gain, h, gu, act, l, rs_group):
        dgu, dxin, dgain = _ffn_bwd(dxo, W[f"d{l}"], W[f"gu{l}"], gu, xin, gain, f"ffn_bwd{l}")
        gw_d = _mm(act, dxo, mode="tn", M=DFF, N=D, K=S, tm=HCH, tn=D, tk=2048, out_dtype=BF16, name=f"gw_d{l}")
        gw_gu = _mm(dgu, h, mode="tn", M=2 * DFF, N=D, K=S, tm=HCH, tn=D, tk=2048, out_dtype=BF16, name=f"gw_gu{l}")
        return dxin, dgain, comm.send_grads(rs_group, {f"d{l}": gw_d, f"gu{l}": gw_gu})

    nbs = [S // QB // dil for dil in DILS]
    hf = _rms_fwd_folded(x, g_attn, "rms_attn", deps=comm.ag_tokens)
    hf = [h.reshape(S, D) for h in hf]
    W.update(comm.weights(0, hf[0]))
    qkv_f, o_f, lse_f = [], [], []
    for g, dil in enumerate(DILS):
        qkv_f.append(_mm(hf[g], W["qkv"], mode="nt", M=S, N=3 * D, K=D, tm=2048, tn=1024, tk=D, out_dtype=BF16,
                         b_off=(3 * g, 0), name=f"qkv_proj{g}"))
        og, lg = _attn_fwd(qkv_f[g], bias[g], nbs[g], f"attn_fwd{g}")
        o_f.append(og if dil == 1 else og.reshape(dil, S // dil, D))
        lse_f.append(lg if dil == 1 else lg.reshape(dil, S // dil, HD))
    passing = [comm.pass_on(1, tuple(o_f)), comm.pass_on(2, tuple(o_f))]
    (o_f, lse_f), passing = lax.optimization_barrier(((o_f, lse_f), passing))
    o, lse = _attn_merge(o_f, lse_f)
    W.update(comm.weights(1, (o, passing[0])))
    x1, h1 = _mm_res_norm(o, W["wo"], x, g_f0, K=D, tm=1024, name="attn_out")
    pv = W["pv"].reshape(NDEV, 8, 128)
    pool_norm, pool_scale = pv[:, 0, :].reshape(1, D), pv[:, 1, :].reshape(1, D)
    gu0, act0, (x2, h2) = ffn_fwd(x1, h1, 0, pool_norm)

    W.update(comm.weights(2, (x2, passing[1])))
    yd = _pool_in_window(h2, W["wpi"])
    x3 = _pool_out(yd, W["pg"], pool_scale, x2)
    h3 = _rms_fwd(x3, g_f1, "rms_ffn1")
    gu1, act1, (dx4, d_fin, lossvec) = ffn_fwd(x3, h3, 1, g_fin, target=tgt)

    dx3, d_f1, token = ffn_bwd(dx4, x3, g_f1, h3, gu1, act1, 1, 0)
    du, d_scale, gw_pg = _pool_out_bwd(dx3, yd, W["pg"], pool_scale, deps=(token,))
    gw_pi = _mm(h2, du, mode="tn", M=D, N=D, K=S, tm=D, tn=D, tk=S, out_dtype=BF16, name="gw_pi")
    token = comm.send_grads(1, {"pg": gw_pg, "wpi": gw_pi})
    dx2, d_pool = _mm_rms_bwd(du, W["wpi"], x2, pool_norm, dx3, mode="nt", M=S, K=D, tm=1024, deps=(token,),
                              name="pool_in_bwd")
    dx1, d_f0, token = ffn_bwd(dx2, x1, g_f0, h1, gu0, act0, 0, 2)

    gw_o = _mm(o, dx1, mode="tn", M=D, N=D, K=S, tm=D, tn=D, tk=2048, out_dtype=BF16, deps=(token,), name="gw_o")
    do_f, lse_ff, delta_f = _attn_bwd_prep(dx1, W["wo"], o, lse, deps=(token,))
    dqkv_f, gw_qkv = [], None
    for g in range(NGROUPS):
        dqkv_f.append(_attn_bwd(qkv_f[g], do_f[g].reshape(S, D), lse_ff[g].reshape(S, HD),
                                delta_f[g].reshape(S, HD), bias[g], nbs[g], f"attn_bwd{g}"))
        gw_qkv = _mm(dqkv_f[g], hf[g], mode="tn", M=3 * D, N=D, K=S, tm=1024, tn=D, tk=S, out_dtype=BF16,
                     out_rows=NGROUPS * 3 * D, out_off=3 * g, out_prev=gw_qkv, name=f"gw_qkv{g}")
    token = comm.send_grads_pairwise({"wo": gw_o, "qkv": gw_qkv})
    folded = []
    for g in reversed(range(1, NGROUPS)):
        dh0_g = _mm(dqkv_f[g], W["qkv"], mode="nn", M=S, N=D, K=3 * D, tm=1024, tn=D, tk=3 * D, out_dtype=F32,
                    b_off=(g, 0), deps=(token,), name=f"qkv_proj_bwd{g}")
        folded.append(dh0_g.reshape(DILS[g], S // DILS[g], D))
        if g == NGROUPS - 1:
            token = comm.pass_grads(dh0_g)
    grad_x, d_attn = _mm_rms_bwd(dqkv_f[0], W["qkv"], x, g_attn, dx1, mode="nn", M=S, K=3 * D, tm=512,
                                 deps=(token,), folded=folded, name="qkv_proj_bwd0")

    vec = jnp.concatenate([d_attn, d_f0, d_f1, d_fin, d_pool, d_scale, lossvec, jnp.zeros((1, D), F32)], axis=0)
    return grad_x, vec


def _mesh_pos():
    x, y, c = lax.axis_index("x"), lax.axis_index("y"), lax.axis_index("c")
    return x, y, c, 4 * x + 2 * y + c


def _peer(x, y, c, k):
    kx, ky, kc = (k >> 2) & 1, (k >> 1) & 1, k & 1
    px = 1 - x if kx else x
    py = 1 - y if ky else y
    pc = 1 - c if kc else c
    return (px, py, pc), 4 * px + 2 * py + pc


ANY = pl.BlockSpec(memory_space=pl.ANY)


HBM = pl.BlockSpec(memory_space=pltpu.HBM)
SEMS = pl.BlockSpec(memory_space=pltpu.SEMAPHORE)
EFFECT = pltpu.SideEffectType.DATAFLOW_SIDE_EFFECTING

AG_GROUPS = (("qkv",), ("wo", "gu0", "d0", "pv"), ("wpi", "pg", "gu1", "d1"))
AG_ORDER = tuple(n for grp in AG_GROUPS for n in grp)
RS_GROUPS = (("d1", "gu1"), ("pg", "wpi"), ("d0", "gu0"), ("wo", "qkv"))


def _hbm(a):
    return pltpu.with_memory_space_constraint(a, pltpu.HBM)


def _remote(src, dst, send, recv, peer):
    return pltpu.make_async_remote_copy(src_ref=src, dst_ref=dst, send_sem=send, recv_sem=recv, device_id=peer,
                                        device_id_type=pl.DeviceIdType.MESH)


ALL_KS = tuple(range(1, NDEV))
AG_KS1 = (1, 2, 4, 6)
AG_KS2 = (2, 4, 6)
RS_KS_PAIR = (1, 3, 5, 7)
RS_KS_CHIPS = (2, 4, 6)


def _split_start(srcs, src_of, lands, copy_refs, name, deps=(), ks=ALL_KS, to=None):
    ns, n, nd, nk = len(srcs), len(lands), len(deps), len(ks)

    def body(*refs):
        ins, land = refs[:ns], refs[ns:ns + n]
        send, recv = refs[ns + n + nd], refs[ns + n + nd + 1]
        token = refs[-1]
        x, y, c, me = _mesh_pos()
        for j in range(n):
            for i, k in enumerate(ks):
                _, pid = _peer(x, y, c, k)
                dest, _ = _peer(x, y, c, k if to is None else to)
                src, dst = copy_refs(j, (land[j] if src_of[j] is None else ins[src_of[j]]), land[j], me, pid, i)
                _remote(src, dst, send.at[j * nk + i], recv.at[j * nk + i], dest).start()
        token[...] = jnp.zeros_like(token)

    outs = pl.pallas_call(
        body, name=name,
        out_shape=(pltpu.SemaphoreType.DMA((n * nk,)), pltpu.SemaphoreType.DMA((n * nk,)))
        + tuple(pltpu.HBM(a.shape, a.dtype) for a in srcs) + tuple(pltpu.HBM(a.shape, a.dtype) for a in lands)
        + (jax.ShapeDtypeStruct((8, 128), F32),),
        in_specs=(HBM,) * (ns + n) + (ANY,) * nd,
        out_specs=(SEMS, SEMS) + (HBM,) * (ns + n) + (pl.BlockSpec(memory_space=pltpu.VMEM),),
        input_output_aliases={i: 2 + i for i in range(ns + n)},
        compiler_params=pltpu.CompilerParams(has_side_effects=EFFECT),
    )(*[_hbm(a) for a in srcs], *[_hbm(a) for a in lands], *deps)
    return outs[0], outs[1], list(outs[2:2 + ns]), list(outs[2 + ns:2 + ns + n]), outs[-1]


def _split_wait(srcs, src_of, lands, send, recv, sem_rows, wait_refs, after, name, ks=ALL_KS):
    ns, n, nk = len(srcs), len(lands), len(ks)
    after = tuple(after) if isinstance(after, (tuple, list)) else (after,)

    def body(*refs):
        ins, land = refs[:ns], refs[ns:ns + n]
        send_ref, recv_ref = refs[ns + n], refs[ns + n + 1]
        x, y, c, me = _mesh_pos()
        for j in range(n):
            for i, k in enumerate(ks):
                peer, _ = _peer(x, y, c, k)
                src, dst = wait_refs(j, (land[j] if src_of[j] is None else ins[src_of[j]]), land[j])
                sem = sem_rows[j] * nk + i
                cp = _remote(src, dst, send_ref.at[sem], recv_ref.at[sem], peer)
                cp.wait_send()
                cp.wait_recv()

    outs = pl.pallas_call(
        body, name=name,
        out_shape=tuple(pltpu.HBM(a.shape, a.dtype) for a in srcs) + tuple(pltpu.HBM(a.shape, a.dtype) for a in lands),
        in_specs=(HBM,) * (ns + n) + (SEMS, SEMS) + (ANY,) * len(after),
        out_specs=(HBM,) * (ns + n),
        input_output_aliases={i: i for i in range(ns + n)},
        compiler_params=pltpu.CompilerParams(has_side_effects=EFFECT),
    )(*srcs, *lands, send, recv, *after)
    return list(outs[:ns]), list(outs[ns:])


def _ag_dtype(name):
    return F32 if name == "pv" else BF16


def _ag_align(name):
    return 8 if name == "pv" else 16


def _place_transposed(w, me, name):
    rows = w.shape[1]
    nblk = rows // 128

    def kern(me_ref, w_ref, o_ref):
        o_ref[...] = w_ref[...].T.astype(BF16)

    grid_spec = pltpu.PrefetchScalarGridSpec(
        num_scalar_prefetch=1, grid=(nblk,),
        in_specs=[pl.BlockSpec((D, 128), lambda i, me_ref: (0, i))],
        out_specs=pl.BlockSpec((128, D), lambda i, me_ref: (me_ref[0] * nblk + i, 0)))
    return pl.pallas_call(
        kern, grid_spec=grid_spec, out_shape=jax.ShapeDtypeStruct((NDEV * rows, D), BF16),
        compiler_params=_cparams(1), name=name)(me.reshape(1).astype(jnp.int32), w)


class _Comm:
    def __init__(self, params, make_shards, me, placed):
        self.me = me
        self.ag_land, self.ag_sems, self.ag_tokens, self.ag_passing = {}, {}, (), {}
        self.rs = []
        deps = ()
        for part, names in enumerate((AG_GROUPS[0], AG_ORDER[len(AG_GROUPS[0]):])):
            rows = [SEC_ROWS[n] for n in names]
            if part == 0:
                lands = [placed[n] for n in names]
            else:
                params, deps = lax.optimization_barrier((params, deps))
                shards = make_shards(*params)
                lands = [lax.dynamic_update_slice(lax.empty((NDEV * r, shards[n].shape[1]), _ag_dtype(n)),
                                                  shards[n].astype(_ag_dtype(n)), (_shard_pos(n, me), 0))
                         for n, r in zip(names, rows)]

            def copy_refs(j, src, land, me, pid, i, names=names, rows=rows):
                own = land.at[pl.ds(pl.multiple_of(_shard_pos(names[j], me), _ag_align(names[j])), rows[j])]
                return own, own

            send, recv, _, lands, token = _split_start([], [None] * len(names), lands, copy_refs, f"ag_start{part}",
                                                       deps=deps, ks=AG_KS1)
            deps = (token,)
            self.ag_tokens += (token,)
            for j, n in enumerate(names):
                self.ag_land[n] = lands[j]
                self.ag_sems[n] = (send, recv, j)

    def pass_on(self, group, after):
        names = AG_GROUPS[group]
        send, recv = self.ag_sems[names[0]][:2]
        idx = [self.ag_sems[n][2] for n in names]
        rows = [SEC_ROWS[n] for n in names]
        none = [None] * len(names)

        def wait_refs(j, src, land):
            return land.at[pl.ds(0, rows[j])], land.at[pl.ds(0, rows[j])]

        _, lands = _split_wait([], none, [self.ag_land[n] for n in names], send, recv, idx,
                               wait_refs, after, f"ag_wait{group}", ks=AG_KS1)

        def copy_refs(j, src, land, me, pid, i):
            theirs = land.at[pl.ds(pl.multiple_of(_shard_pos(names[j], pid), _ag_align(names[j])), rows[j])]
            return theirs, theirs

        send, recv, _, lands, token = _split_start([], none, lands, copy_refs, f"ag_pass{group}", ks=AG_KS2, to=1)
        self.ag_passing[group] = (send, recv, lands, wait_refs)
        return token

    def weights(self, group, after):
        names = AG_GROUPS[group]
        if group not in self.ag_passing:
            after = self.pass_on(group, after)
        send, recv, lands, wait_refs = self.ag_passing[group]
        _, lands = _split_wait([], [None] * len(names), lands, send, recv, list(range(len(names))), wait_refs, after,
                               f"ag_pass_wait{group}", ks=AG_KS2)
        return dict(zip(names, lands))

    def send_grads(self, group, gws):
        names = RS_GROUPS[group]
        rows = [SEC_ROWS[n] for n in names]
        grads = [gws[n] for n in names]
        me = self.me
        lands = [lax.dynamic_update_slice(
            lax.empty((NDEV, r, D), BF16),
            lax.dynamic_slice(g, (_shard_pos(n, me), 0), (r, D))[None], (me, 0, 0))
            for n, r, g in zip(names, rows, grads)]

        def copy_refs(j, src, land, me, pid, i):
            return src.at[pl.ds(pl.multiple_of(_shard_pos(names[j], pid), 16), rows[j])], land.at[me]

        send, recv, srcs, lands, token = _split_start(grads, list(range(len(names))), lands, copy_refs,
                                                      f"rs_start{group}")
        self.rs.append((names, rows, send, recv, srcs, lands, ALL_KS))
        return token

    def send_grads_pairwise(self, gws):
        names = RS_GROUPS[-1]
        rows = [SEC_ROWS[n] for n in names]
        grads = [gws[n] for n in names]
        idx = list(range(len(names)))
        lands = [lax.empty((len(RS_KS_PAIR), r, D), BF16) for r in rows]

        def copy_refs(j, src, land, me, pid, i):
            return src.at[pl.ds(pl.multiple_of(_shard_pos(names[j], pid), 16), rows[j])], land.at[i]

        send, recv, srcs, lands, token = _split_start(grads, idx, lands, copy_refs, "rs_pair_start",
                                                      ks=RS_KS_PAIR, to=1)
        self.pair = (names, rows, send, recv, srcs, lands)
        return token

    def pass_grads(self, after):
        names, rows, send, recv, srcs, lands = self.pair
        idx = list(range(len(names)))
        me = self.me

        def wait_refs(j, src, land):
            return src.at[pl.ds(0, rows[j])], land.at[0]

        srcs, lands = _split_wait(srcs, idx, lands, send, recv, idx, wait_refs, after, "rs_pair_wait", ks=RS_KS_PAIR)
        sums = [_pair_sum(g, got, me, f"rs_pair_sum_{n}") for n, g, got in zip(names, srcs, lands)]
        lands = [lax.dynamic_update_slice(lax.empty(p.shape, BF16), p[0:1], (0, 0, 0)) for p in sums]

        def copy_refs(j, src, land, me, pid, i):
            return src.at[i + 1], land.at[i + 1]

        send, recv, sums, lands, token = _split_start(sums, idx, lands, copy_refs, f"rs_start{len(RS_GROUPS) - 1}",
                                                      ks=RS_KS_CHIPS)
        self.rs.append((names, rows, send, recv, sums, lands, RS_KS_CHIPS))
        return token

    def received(self, group, after):
        names, rows, send, recv, srcs, lands, ks = self.rs[group]
        whole = srcs[0].ndim == 2

        def wait_refs(j, src, land):
            return (src.at[pl.ds(0, rows[j])] if whole else src.at[0]), land.at[0]

        _, lands = _split_wait(srcs, list(range(len(names))), lands, send, recv, list(range(len(names))), wait_refs,
                               after, f"rs_wait{group}", ks=ks)
        return dict(zip(names, lands))


def _pair_sum(grad, got, me, name):
    n, rows, _ = got.shape
    tr = 384 if rows % 384 == 0 else rows
    nt = rows // tr

    def kern(me_ref, g_ref, b_ref, o_ref):
        o_ref[0] = (g_ref[...].astype(F32) + b_ref[0].astype(F32)).astype(BF16)

    blk = pl.BlockSpec((1, tr, D), lambda i, t, me_ref: (i, t, 0))
    grid_spec = pltpu.PrefetchScalarGridSpec(
        num_scalar_prefetch=1, grid=(n, nt),
        in_specs=[pl.BlockSpec((tr, D), lambda i, t, me_ref: (jnp.bitwise_xor(me_ref[0], 2 * i) * nt + t, 0)), blk],
        out_specs=blk)
    return pl.pallas_call(
        kern, grid_spec=grid_spec, out_shape=jax.ShapeDtypeStruct(got.shape, BF16),
        compiler_params=_cparams(2), name=name)(me.reshape(1).astype(jnp.int32), grad, got)


def _sum_contributions(r_ref):
    g = r_ref[0].astype(F32)
    for slot in range(1, r_ref.shape[0]):
        g = g + r_ref[slot].astype(F32)
    return g


def _adam_math(g, w, m, v):
    c1 = 1.0 / (1.0 - ADAM_B1 ** ADAM_STEP)
    c2 = 1.0 / (1.0 - ADAM_B2 ** ADAM_STEP)
    mn = ADAM_B1 * m + (1.0 - ADAM_B1) * g
    vn = ADAM_B2 * v + (1.0 - ADAM_B2) * (g * g)
    return -ADAM_LR * ((mn * c1) / (jnp.sqrt(vn * c2) + ADAM_EPS) + ADAM_WD * w), mn, vn


def _adamw(R, w, m, v, *, tr, name, layer=None, prev=None):
    rows, C = w.shape[-2:]
    nprev = 0 if prev is None else 4

    def kern(r_ref, w_ref, m_ref, v_ref, *rest):
        g_out, d_out, m_out, v_out = rest[nprev:]
        g = _sum_contributions(r_ref)
        g_out[...] = g
        d_out[...], m_out[...], v_out[...] = _adam_math(g, w_ref[...], m_ref[...], v_ref[...])

    if layer is None:
        tile = pl.BlockSpec((tr, C), lambda i: (i, 0))
    else:
        tile = pl.BlockSpec((None, tr, C), lambda i: (layer, i, 0))
    shp = jax.ShapeDtypeStruct(w.shape, F32)
    return pl.pallas_call(
        kern, grid=(rows // tr,),
        in_specs=[pl.BlockSpec((R.shape[0], tr, C), lambda i: (0, i, 0)), tile, tile, tile]
        + [pl.BlockSpec(memory_space=pl.ANY)] * nprev,
        out_specs=[tile] * 4, out_shape=[shp] * 4,
        input_output_aliases={4 + k: k for k in range(nprev)},
        compiler_params=_cparams(1), name=name)(R, w, m, v, *(prev or ()))


def _adamw_pool_group(R, w, m, v):
    rows = SEC_ROWS["pg"]

    def kern(r_ref, w_ref, m_ref, v_ref, g_out, d_out, m_out, v_out):
        g = _sum_contributions(r_ref)
        g_out[0] = g
        d_out[0], m_out[0], v_out[0] = _adam_math(g, w_ref[0], m_ref[0], v_ref[0])

    blk = pl.BlockSpec((1, rows, PGD), lambda i: (i, 0, 0))
    shp = jax.ShapeDtypeStruct((POOL_G, rows, PGD), F32)
    return pl.pallas_call(
        kern, grid=(POOL_G,),
        in_specs=[pl.BlockSpec((NDEV, rows, PGD), lambda i: (0, 0, i)), blk, blk, blk],
        out_specs=[blk] * 4, out_shape=[shp] * 4, compiler_params=_cparams(1), name="adamw_pg")(R, w, m, v)


def _adamw_transposed(R, w, m, v, name):
    rows = R.shape[1]
    tr = 128

    def kern(r_ref, w_ref, m_ref, v_ref, g_out, d_out, m_out, v_out):
        g = _sum_contributions(r_ref).T
        g_out[...] = g
        d_out[...], m_out[...], v_out[...] = _adam_math(g, w_ref[...], m_ref[...], v_ref[...])

    tile = pl.BlockSpec((D, tr), lambda i: (0, i))
    shp = jax.ShapeDtypeStruct((D, rows), F32)
    return pl.pallas_call(
        kern, grid=(rows // tr,),
        in_specs=[pl.BlockSpec((R.shape[0], tr, D), lambda i: (0, i, 0)), tile, tile, tile],
        out_specs=[tile] * 4, out_shape=[shp] * 4, compiler_params=_cparams(1), name=name)(R, w, m, v)


def _pack_sections(w_qkv, w_attn_out, w_pool_in, w_pool_group, w_ffn_gate_up, w_ffn_down):
    pg = w_pool_group[0].transpose(1, 0, 2).reshape(SEC_ROWS["pg"], D)
    return {"qkv": w_qkv[0].T, "wo": w_attn_out[0], "wpi": w_pool_in[0], "gu0": w_ffn_gate_up[0].T,
            "gu1": w_ffn_gate_up[1].T, "d0": w_ffn_down[0], "d1": w_ffn_down[1], "pg": pg}


def _vec_pack(attn_norm, ffn_norm, final_norm, pool_norm_sh, pool_scale_sh, me):
    def place(sh):
        return lax.dynamic_update_slice(jnp.zeros((1, D), F32), sh, (0, me * 128))
    return jnp.concatenate([attn_norm, ffn_norm, final_norm.reshape(1, D), place(pool_norm_sh),
                            place(pool_scale_sh), jnp.zeros((2, D), F32)], axis=0)


def _vec_unpack(p, me):
    def take(r):
        return lax.dynamic_slice(p[r:r + 1], (0, me * 128), (1, 128))
    return p[0:1], p[1:3], p[3], take(4), take(5)


def kernel(x, attn_norm, w_qkv, w_attn_out, pool_norm, w_pool_in, w_pool_group, pool_scale, ffn_norm, w_ffn_gate_up, w_ffn_down, final_norm, loss_target, m_attn_norm, m_w_qkv, m_w_attn_out, m_pool_norm, m_w_pool_in, m_w_pool_group, m_pool_scale, m_ffn_norm, m_w_ffn_gate_up, m_w_ffn_down, m_final_norm, v_attn_norm, v_w_qkv, v_w_attn_out, v_pool_norm, v_w_pool_in, v_w_pool_group, v_pool_scale, v_ffn_norm, v_w_ffn_gate_up, v_w_ffn_down, v_final_norm):
    me = 4 * lax.axis_index("x") + 2 * lax.axis_index("y") + lax.axis_index("c")

    def make_shards(wq, wo, wpi, wpg, wgu, wd, pn, ps):
        shards = _pack_sections(wq, wo, wpi, wpg, wgu, wd)
        shards["pv"] = jnp.concatenate([pn, ps, jnp.zeros((6, 128), F32)], axis=0)
        return shards

    comm = _Comm((w_qkv, w_attn_out, w_pool_in, w_pool_group, w_ffn_gate_up, w_ffn_down, pool_norm, pool_scale),
                 make_shards, me, placed={"qkv": _place_transposed(w_qkv[0], me, "place_qkv")})

    grad_x, vec = _local_step(x[0], loss_target[0], comm, attn_norm, ffn_norm, final_norm)

    small = ((attn_norm, ffn_norm, final_norm, pool_norm, pool_scale),
             (m_attn_norm, m_ffn_norm, m_final_norm, m_pool_norm, m_pool_scale),
             (v_attn_norm, v_ffn_norm, v_final_norm, v_pool_norm, v_pool_scale))
    small, grad_x = lax.optimization_barrier((small, grad_x))
    vw, vm, vv = (_vec_pack(*s, me) for s in small)

    gu_t = [jnp.swapaxes(a, 1, 2) for a in (w_ffn_gate_up, m_w_ffn_gate_up, v_w_ffn_gate_up)]
    res = {}
    gu_res, d_res = None, None
    vec_out = None
    vec_land = lax.dynamic_update_slice(lax.empty((NDEV, 8, D), F32), vec[None], (me, 0, 0))
    vec_sems = _split_start([vec], [0], [vec_land], lambda j, src, land, me_, pid, i: (src, land.at[me_]),
                            "vec_start")
    after = (grad_x, vec_sems[4])
    for group in range(len(RS_GROUPS)):
        if group == len(RS_GROUPS) - 1:
            _, (VR,) = _split_wait(vec_sems[2], [0], vec_sems[3], vec_sems[0], vec_sems[1], [0],
                                   lambda j, src, land: (src, land.at[0]), after, "vec_wait")
            vec_out = _adamw(VR, vw, vm, vv, tr=8, name="adamw_vec")
            after = vec_out[0]
        for n, R in comm.received(group, after).items():
            if n in ("d0", "d1"):
                d_res = _adamw(R, w_ffn_down, m_w_ffn_down, v_w_ffn_down, tr=88, name=f"adamw_{n}",
                               layer=int(n[1]), prev=d_res)
                after = d_res[0]
            elif n in ("gu0", "gu1"):
                gu_res = _adamw(R, *gu_t, tr=176, name=f"adamw_{n}", layer=int(n[2]), prev=gu_res)
                after = gu_res[0]
            elif n == "pg":
                out = _adamw_pool_group(R, w_pool_group[0], m_w_pool_group[0], v_w_pool_group[0])
                res["pg"] = tuple(a[None] for a in out)
                after = out[0]
            elif n in ("wo", "wpi"):
                w, m, v = ((w_attn_out, m_w_attn_out, v_w_attn_out) if n == "wo"
                           else (w_pool_in, m_w_pool_in, v_w_pool_in))
                res[n] = _adamw(R, w[0], m[0], v[0], tr=128, name=f"adamw_{n}")
                res[n] = tuple(a[None] for a in res[n])
                after = res[n][0]
            else:
                out = _adamw_transposed(R, w_qkv[0], m_w_qkv[0], v_w_qkv[0], "adamw_qkv")
                res["qkv"] = tuple(a[None] for a in out)
                after = out[0]
    res["gu"] = tuple(jnp.swapaxes(a, 1, 2) for a in gu_res)
    res["d"] = tuple(d_res)

    outs = []
    for kind in range(4):
        an, fn, fin, pn, ps = _vec_unpack(vec_out[kind], me)
        outs.append((an, res["qkv"][kind], res["wo"][kind], pn, res["wpi"][kind], res["pg"][kind], ps, fn,
                     res["gu"][kind], res["d"][kind], fin))
    loss = 0.5 * jnp.sum(vec_out[0][6]) / D
    return (loss, grad_x[None]) + outs[0] + outs[1] + outs[2] + outs[3]
```

```python
import jax
import jax.numpy as jnp
from jax import lax
from jax.experimental import pallas as pl
from jax.experimental.pallas import tpu as pltpu

F32 = jnp.float32
BF16 = jnp.bfloat16

D = 1024
NDEV = 8
HEADS = 8
HD = 128
QB = 128
NGROUPS = 3
DILS = (1, 4, 16)
DFF = 2816
HCH = 1408
POOL_G = 4
PGD = 256
RMS_EPS = 1e-6
NEG = -1e30

ADAM_LR = 0.001
ADAM_B1 = 0.9
ADAM_B2 = 0.999
ADAM_EPS = 1e-08
ADAM_WD = 0.01
ADAM_STEP = 10

VMEM_LIMIT = 52 * 1024 * 1024

SECTIONS = (("qkv", 1152), ("wo", 128), ("wpi", 128), ("gu0", 704), ("gu1", 704),
            ("d0", 352), ("d1", 352), ("pg", 32))
SEC_ROWS = dict(SECTIONS)
SEC_ROWS["pv"] = 8


def _cparams(n_grid):
    return pltpu.CompilerParams(dimension_semantics=("arbitrary",) * n_grid, vmem_limit_bytes=VMEM_LIMIT)


def _shard_pos(name, dev):
    n = SEC_ROWS[name]
    if name in ("gu0", "gu1"):
        return ((dev % 4) // 2) * (2 * HCH) + (dev // 4) * HCH + (dev % 2) * n
    return dev * n


def _mm(a, b, *, mode, M, N, K, tm, tn, tk, out_dtype, name, a_off=(0, 0), b_off=(0, 0), res=None,
        out_rows=None, out_off=0, out_prev=None, deps=()):
    nm, nn, nk = M // tm, N // tn, K // tk
    assert nm * tm == M and nn * tn == N and nk * tk == K
    if mode == "nn":
        a_bs, b_bs = (tm, tk), (tk, tn)
        a_ix = lambda i, j, k: (i, k)
        b_ix = lambda i, j, k: (k, j)
        dims = (((1,), (0,)), ((), ()))
    elif mode == "nt":
        a_bs, b_bs = (tm, tk), (tn, tk)
        a_ix = lambda i, j, k: (i, k)
        b_ix = lambda i, j, k: (j, k)
        dims = (((1,), (1,)), ((), ()))
    else:
        a_bs, b_bs = (tk, tm), (tk, tn)
        a_ix = lambda i, j, k: (k, i)
        b_ix = lambda i, j, k: (k, j)
        dims = (((0,), (0,)), ((), ()))

    def spec(bs, ix, off):
        def im(i, j, k):
            r, c = ix(i, j, k)
            return (r + off[0], c + off[1])
        return pl.BlockSpec(bs, im)

    in_specs = [spec(a_bs, a_ix, a_off), spec(b_bs, b_ix, b_off)]
    args = [a, b]
    if res is not None:
        in_specs.append(pl.BlockSpec((tm, tn), lambda i, j, k: (i, j)))
        args.append(res)
    out_shape = jax.ShapeDtypeStruct((M if out_rows is None else out_rows, N), out_dtype)
    out_spec = pl.BlockSpec((tm, tn), lambda i, j, k: (i + out_off, j))
    has_res = res is not None
    extra = list(deps) + ([out_prev] if out_prev is not None else [])
    for dep in extra:
        in_specs.append(pl.BlockSpec(memory_space=pl.ANY))
        args.append(dep)
    o_pos = 2 + int(has_res) + len(extra)
    aliases = {len(args) - 1: 0} if out_prev is not None else {}

    def kern(*refs):
        a_ref, b_ref = refs[0], refs[1]
        res_ref = refs[2] if has_res else None
        o_ref = refs[o_pos]
        av = a_ref[...]
        bv = b_ref[...]
        if av.dtype != BF16:
            av = av.astype(BF16)
        if bv.dtype != BF16:
            bv = bv.astype(BF16)
        part = lax.dot_general(av, bv, dims, preferred_element_type=F32)

        def write(val):
            if has_res:
                val = val + res_ref[...]
            o_ref[...] = val.astype(out_dtype)

        if nk == 1:
            write(part)
        else:
            acc_ref = refs[-1]
            k = pl.program_id(2)

            @pl.when(k == 0)
            def _():
                acc_ref[...] = part

            @pl.when(k > 0)
            def _():
                acc_ref[...] += part

            @pl.when(k == nk - 1)
            def _():
                write(acc_ref[...])

    scratch = [pltpu.VMEM((tm, tn), F32)] if nk > 1 else []
    return pl.pallas_call(
        kern, grid=(nm, nn, nk), in_specs=in_specs, out_specs=out_spec, out_shape=out_shape,
        scratch_shapes=scratch, input_output_aliases=aliases, compiler_params=_cparams(3), name=name)(*args)


def _mm_rms_bwd(a, b, x, g, dres, *, mode, M, K, tm, name, b_off=(0, 0), deps=(), folded=()):
    nd, nf = len(deps), len(folded)
    b_bs = (K, D) if mode == "nn" else (D, K)
    dims = (((1,), (0,)), ((), ())) if mode == "nn" else (((1,), (1,)), ((), ()))

    def kern(a_ref, b_ref, x_ref, g_ref, dres_ref, *rest):
        f_refs = rest[:nf]
        dx_ref, dg_ref = rest[nf + nd:nf + nd + 2]
        i = pl.program_id(0)
        av = a_ref[...]
        if av.dtype != BF16:
            av = av.astype(BF16)
        dhv = lax.dot_general(av, b_ref[...], dims, preferred_element_type=F32)
        if nf:
            acc_ref = rest[-1]
            _chunks_put(acc_ref, dhv)
            for f_ref in f_refs:
                dil = f_ref.shape[0]
                for res in range(dil):
                    _chunks_add_rows(acc_ref, f_ref[res], res, tm // dil, dil, True)
            dhv = _chunks_get(acc_ref)
        xv = x_ref[...]
        r = lax.rsqrt(jnp.mean(xv * xv, axis=-1, keepdims=True) + RMS_EPS)
        xhat = xv * r
        gy = dhv * g_ref[...]
        dx_ref[...] = dres_ref[...] + r * (gy - xhat * jnp.mean(gy * xhat, axis=-1, keepdims=True))
        part = jnp.sum(dhv * xhat, axis=0, keepdims=True)

        @pl.when(i == 0)
        def _():
            dg_ref[...] = part

        @pl.when(i > 0)
        def _():
            dg_ref[...] += part

    row = pl.BlockSpec((tm, D), lambda i: (i, 0))
    vec = pl.BlockSpec((1, D), lambda i: (0, 0))
    return pl.pallas_call(
        kern, grid=(M // tm,),
        in_specs=[pl.BlockSpec((tm, K), lambda i: (i, 0)),
                  pl.BlockSpec(b_bs, lambda i: b_off, pipeline_mode=pl.Buffered(1)), row, vec, row]
        + [pl.BlockSpec((f.shape[0], tm // f.shape[0], D), lambda i: (0, i, 0)) for f in folded]
        + [pl.BlockSpec(memory_space=pl.ANY)] * nd,
        out_specs=[row, vec],
        out_shape=[jax.ShapeDtypeStruct((M, D), F32), jax.ShapeDtypeStruct((1, D), F32)],
        scratch_shapes=[pltpu.VMEM((D // 128, tm, 128), F32)] if nf else [],
        compiler_params=_cparams(1), name=name)(a, b, x, g, dres, *folded, *deps)


def _norm_tail(xv, gv, rest, head):
    r = lax.rsqrt(jnp.mean(xv * xv, axis=-1, keepdims=True) + RMS_EPS)
    xhat = xv * r
    if not head:
        xo_ref, h_ref = rest
        xo_ref[...] = xv
        h_ref[...] = (xhat * gv).astype(BF16)
        return
    t_ref, dx_ref, dg_ref, ls_ref = rest
    i = pl.program_id(0)
    e = xhat * gv - t_ref[...]
    dy = e * (1.0 / D)
    gy = dy * gv
    dx_ref[...] = r * (gy - xhat * jnp.mean(gy * xhat, axis=-1, keepdims=True))
    dgp = jnp.sum(dy * xhat, axis=0, keepdims=True)
    lsp = jnp.sum(e * e, axis=0, keepdims=True)

    @pl.when(i == 0)
    def _():
        dg_ref[...] = dgp
        ls_ref[...] = lsp

    @pl.when(i > 0)
    def _():
        dg_ref[...] += dgp
        ls_ref[...] += lsp


def _ffn_fwd(h, wgu, wd, res, g, *, name, tgt=None):
    S = h.shape[0]
    tm = 256
    nj = DFF // HCH
    head = tgt is not None

    def kern(h_ref, wgu_ref, wd_ref, res_ref, g_ref, *rest):
        t_refs, (gu_ref, act_ref), tail = rest[:int(head)], rest[int(head):int(head) + 2], rest[int(head) + 2:]
        hv = h_ref[...]
        for j in range(nj):
            gu = lax.dot_general(hv, wgu_ref[2 * HCH * j:2 * HCH * (j + 1), :], (((1,), (1,)), ((), ())),
                                 preferred_element_type=F32)
            gu_ref[:, 2 * HCH * j:2 * HCH * (j + 1)] = gu.astype(BF16)
            gate = gu[:, :HCH]
            act_ref[:, HCH * j:HCH * (j + 1)] = (gate * jax.nn.sigmoid(gate) * gu[:, HCH:]).astype(BF16)
        xv = res_ref[...] + jnp.dot(act_ref[...], wd_ref[...], preferred_element_type=F32)
        _norm_tail(xv, g_ref[...], tuple(t_refs) + tuple(tail), head)

    row = pl.BlockSpec((tm, D), lambda i: (i, 0))
    vec = pl.BlockSpec((1, D), lambda i: (0, 0))
    in_specs = [row, pl.BlockSpec((2 * DFF, D), lambda i: (0, 0), pipeline_mode=pl.Buffered(1)),
                pl.BlockSpec((DFF, D), lambda i: (0, 0), pipeline_mode=pl.Buffered(1)), row, vec]
    out_specs = [pl.BlockSpec((tm, 2 * DFF), lambda i: (i, 0)), pl.BlockSpec((tm, DFF), lambda i: (i, 0))]
    out_shape = [jax.ShapeDtypeStruct((S, 2 * DFF), BF16), jax.ShapeDtypeStruct((S, DFF), BF16)]
    args = [h, wgu, wd, res, g]
    if head:
        in_specs, args = in_specs + [row], args + [tgt]
        out_specs += [row, vec, vec]
        out_shape += [jax.ShapeDtypeStruct((S, D), F32), jax.ShapeDtypeStruct((1, D), F32),
                      jax.ShapeDtypeStruct((1, D), F32)]
    else:
        out_specs += [row, row]
        out_shape += [jax.ShapeDtypeStruct((S, D), F32), jax.ShapeDtypeStruct((S, D), BF16)]
    outs = pl.pallas_call(kern, grid=(S // tm,), in_specs=in_specs, out_specs=out_specs, out_shape=out_shape,
                          compiler_params=_cparams(1), name=name)(*args)
    return outs[0], outs[1], tuple(outs[2:])


def _mm_res_norm(a, b, res, g, *, K, tm, name, b_off=(0, 0), tgt=None):
    M = a.shape[0]
    head = tgt is not None

    def kern(a_ref, b_ref, res_ref, g_ref, *rest):
        xv = res_ref[...] + jnp.dot(a_ref[...], b_ref[...], preferred_element_type=F32)
        _norm_tail(xv, g_ref[...], rest, head)

    row = pl.BlockSpec((tm, D), lambda i: (i, 0))
    vec = pl.BlockSpec((1, D), lambda i: (0, 0))
    in_specs = [pl.BlockSpec((tm, K), lambda i: (i, 0)),
                pl.BlockSpec((K, D), lambda i: b_off, pipeline_mode=pl.Buffered(1)), row, vec]
    if head:
        return pl.pallas_call(
            kern, grid=(M // tm,), in_specs=in_specs + [row], out_specs=[row, vec, vec],
            out_shape=[jax.ShapeDtypeStruct((M, D), F32), jax.ShapeDtypeStruct((1, D), F32),
                       jax.ShapeDtypeStruct((1, D), F32)],
            compiler_params=_cparams(1), name=name)(a, b, res, g, tgt)
    return pl.pallas_call(
        kern, grid=(M // tm,), in_specs=in_specs, out_specs=[row, row],
        out_shape=[jax.ShapeDtypeStruct((M, D), F32), jax.ShapeDtypeStruct((M, D), BF16)],
        compiler_params=_cparams(1), name=name)(a, b, res, g)


def _rms_fwd(x, g, name, deps=()):
    S = x.shape[0]
    tr = 512

    def kern(x_ref, g_ref, *rest):
        h_ref = rest[-1]
        xv = x_ref[...]
        r = lax.rsqrt(jnp.mean(xv * xv, axis=-1, keepdims=True) + RMS_EPS)
        h_ref[...] = (xv * r * g_ref[...]).astype(BF16)

    return pl.pallas_call(
        kern, grid=(S // tr,),
        in_specs=[pl.BlockSpec((tr, D), lambda i: (i, 0)), pl.BlockSpec((1, D), lambda i: (0, 0))]
        + [pl.BlockSpec(memory_space=pl.ANY)] * len(deps),
        out_specs=pl.BlockSpec((tr, D), lambda i: (i, 0)),
        out_shape=jax.ShapeDtypeStruct((S, D), BF16), compiler_params=_cparams(1), name=name)(x, g, *deps)


def _chunks_put(scr, val):
    for c in range(scr.shape[0]):
        scr[c] = val[:, c * 128:(c + 1) * 128]


def _chunks_get(scr):
    return jnp.concatenate([scr[c] for c in range(scr.shape[0])], axis=1)


def _chunks_rows(scr, r, n, dil):
    return jnp.concatenate([scr.at[c][pl.ds(r, n, stride=dil), :] for c in range(scr.shape[0])], axis=1)


def _chunks_add_rows(scr, val, r, n, dil, accumulate):
    for c in range(scr.shape[0]):
        rows = pl.ds(r, n, stride=dil)
        piece = val[:, c * 128:(c + 1) * 128]
        tile = scr.at[c]
        tile[rows, :] = tile[rows, :] + piece if accumulate else piece


def _rms_fwd_folded(x, g, name, deps=()):
    S = x.shape[0]
    tr = 512
    dils = DILS[1:]

    def kern(x_ref, g_ref, *rest):
        outs, scr = rest[len(deps):-1], rest[-1]
        xv = x_ref[...]
        r = lax.rsqrt(jnp.mean(xv * xv, axis=-1, keepdims=True) + RMS_EPS)
        h = (xv * r * g_ref[...]).astype(BF16)
        outs[0][...] = h
        _chunks_put(scr, h.astype(F32))
        for o_ref, dil in zip(outs[1:], dils):
            for res in range(dil):
                o_ref[res] = _chunks_rows(scr, res, tr // dil, dil).astype(BF16)

    return pl.pallas_call(
        kern, grid=(S // tr,),
        in_specs=[pl.BlockSpec((tr, D), lambda i: (i, 0)), pl.BlockSpec((1, D), lambda i: (0, 0))]
        + [pl.BlockSpec(memory_space=pl.ANY)] * len(deps),
        out_specs=[pl.BlockSpec((tr, D), lambda i: (i, 0))]
        + [pl.BlockSpec((dil, tr // dil, D), lambda i: (0, i, 0)) for dil in dils],
        out_shape=[jax.ShapeDtypeStruct((S, D), BF16)]
        + [jax.ShapeDtypeStruct((dil, S // dil, D), BF16) for dil in dils],
        scratch_shapes=[pltpu.VMEM((D // 128, tr, 128), F32)],
        compiler_params=_cparams(1), name=name)(x, g, *deps)


def _ffn_bwd(dxo, wd, wgu, gu, xin, gain, name):
    S = dxo.shape[0]
    tm = 256
    nj = DFF // HCH

    def kern(dx_ref, wd_ref, wgu_ref, gu_ref, x_ref, g_ref, dgu_ref, dxin_ref, dg_ref):
        i = pl.program_id(0)
        dxv = dx_ref[...]
        dxb = dxv.astype(BF16)
        for j in range(nj):
            c0 = 2 * HCH * j
            dact = lax.dot_general(dxb, wd_ref[HCH * j:HCH * (j + 1), :], (((1,), (1,)), ((), ())),
                                   preferred_element_type=F32)
            gate = gu_ref[:, c0:c0 + HCH].astype(F32)
            up = gu_ref[:, c0 + HCH:c0 + 2 * HCH].astype(F32)
            sig = jax.nn.sigmoid(gate)
            silu = gate * sig
            dgu_ref[:, c0:c0 + HCH] = (dact * up * (sig * (1.0 + gate * (1.0 - sig)))).astype(BF16)
            dgu_ref[:, c0 + HCH:c0 + 2 * HCH] = (dact * silu).astype(BF16)
        dhv = jnp.dot(dgu_ref[...], wgu_ref[...], preferred_element_type=F32)
        xv = x_ref[...]
        r = lax.rsqrt(jnp.mean(xv * xv, axis=-1, keepdims=True) + RMS_EPS)
        xhat = xv * r
        gy = dhv * g_ref[...]
        dxin_ref[...] = dxv + r * (gy - xhat * jnp.mean(gy * xhat, axis=-1, keepdims=True))
        part = jnp.sum(dhv * xhat, axis=0, keepdims=True)

        @pl.when(i == 0)
        def _():
            dg_ref[...] = part

        @pl.when(i > 0)
        def _():
            dg_ref[...] += part

    row = pl.BlockSpec((tm, D), lambda i: (i, 0))
    wide = pl.BlockSpec((tm, 2 * DFF), lambda i: (i, 0))
    vec = pl.BlockSpec((1, D), lambda i: (0, 0))
    return pl.pallas_call(
        kern, grid=(S // tm,),
        in_specs=[row, pl.BlockSpec((DFF, D), lambda i: (0, 0), pipeline_mode=pl.Buffered(1)),
                  pl.BlockSpec((2 * DFF, D), lambda i: (0, 0), pipeline_mode=pl.Buffered(1)), wide, row, vec],
        out_specs=[wide, row, vec],
        out_shape=[jax.ShapeDtypeStruct((S, 2 * DFF), BF16), jax.ShapeDtypeStruct((S, D), F32),
                   jax.ShapeDtypeStruct((1, D), F32)],
        compiler_params=_cparams(1), name=name)(dxo, wd, wgu, gu, xin, gain)


def _window(val, grp, backward):
    S = val.shape[0]
    row = lax.broadcasted_iota(jnp.int32, val.shape, 0)
    cnt = jnp.minimum(row + 1, 2 << grp).astype(F32)
    s = val / cnt if backward else val
    for k in (1, 2, 4, 8)[:grp + 1]:
        if backward:
            sh = jnp.where(row < S - k, pltpu.roll(s, S - k, 0), 0.0)
        else:
            sh = jnp.where(row >= k, pltpu.roll(s, k, 0), 0.0)
        s = s + sh
    return s - val if backward else s / cnt - val


def _pool_in_window(h, wpi):
    S = h.shape[0]

    def kern(h_ref, w_ref, o_ref):
        g = pl.program_id(0)
        u = jnp.dot(h_ref[...], w_ref[...], preferred_element_type=F32)
        for grp in range(POOL_G):
            @pl.when(g == grp)
            def _(grp=grp):
                o_ref[...] = _window(u, grp, False).astype(BF16)

    return pl.pallas_call(
        kern, grid=(POOL_G,),
        in_specs=[pl.BlockSpec((S, D), lambda g: (0, 0), pipeline_mode=pl.Buffered(1)),
                  pl.BlockSpec((D, PGD), lambda g: (0, g))],
        out_specs=pl.BlockSpec((S, PGD), lambda g: (0, g)),
        out_shape=jax.ShapeDtypeStruct((S, D), BF16), compiler_params=_cparams(1), name="pool_in")(h, wpi)


def _pool_out(yd, G, scale, xres):
    S = yd.shape[0]
    tm = min(S, 4096)

    def kern(y_ref, w_ref, s_ref, x_ref, o_ref):
        z = jnp.dot(y_ref[...], w_ref[...], preferred_element_type=F32)
        o_ref[...] = x_ref[...] + z * s_ref[...]

    tile = pl.BlockSpec((tm, PGD), lambda i, g: (i, g))
    return pl.pallas_call(
        kern, grid=(S // tm, POOL_G),
        in_specs=[tile, pl.BlockSpec((PGD, PGD), lambda i, g: (0, g)),
                  pl.BlockSpec((1, PGD), lambda i, g: (0, g)), tile],
        out_specs=tile, out_shape=jax.ShapeDtypeStruct((S, D), F32),
        compiler_params=_cparams(2), name="pool_out")(yd, G, scale, xres)


def _pool_out_bwd(dz, yd, G, scale, deps=()):
    S = yd.shape[0]
    nd = len(deps)

    def kern(dz_ref, y_ref, w_ref, s_ref, *rest):
        du_ref, ds_ref, dw_ref = rest[nd:]
        g = pl.program_id(0)
        dzv = dz_ref[...]
        yv = y_ref[...]
        wv = w_ref[...]
        zraw = jnp.dot(yv, wv, preferred_element_type=F32)
        ds_ref[...] = jnp.sum(dzv * zraw, axis=0, keepdims=True)
        dzr = (dzv * s_ref[...]).astype(BF16)
        dw_ref[...] = lax.dot_general(yv, dzr, (((0,), (0,)), ((), ())), preferred_element_type=F32).astype(BF16)
        dyd = lax.dot_general(dzr, wv, (((1,), (1,)), ((), ())), preferred_element_type=F32)
        for grp in range(POOL_G):
            @pl.when(g == grp)
            def _(grp=grp):
                du_ref[...] = _window(dyd, grp, True).astype(BF16)

    tile = pl.BlockSpec((S, PGD), lambda g: (0, g))
    return pl.pallas_call(
        kern, grid=(POOL_G,),
        in_specs=[tile, tile, pl.BlockSpec((PGD, PGD), lambda g: (0, g)),
                  pl.BlockSpec((1, PGD), lambda g: (0, g))] + [pl.BlockSpec(memory_space=pl.ANY)] * nd,
        out_specs=[tile, pl.BlockSpec((1, PGD), lambda g: (0, g)), pl.BlockSpec((PGD, PGD), lambda g: (0, g))],
        out_shape=[jax.ShapeDtypeStruct((S, D), BF16), jax.ShapeDtypeStruct((1, D), F32),
                   jax.ShapeDtypeStruct((PGD, D), BF16)],
        compiler_params=_cparams(1), name="pool_out_bwd")(dz, yd, G, scale, *deps)


def _bias_table():
    qi = jnp.arange(QB)[:, None]
    ki = jnp.arange(2 * QB)[None, :]
    delta = QB + qi - ki
    inband = (delta >= 0) & (delta <= QB)
    n = NGROUPS * HEADS
    slopes = jnp.exp2(-8.0 * jnp.arange(1, n + 1, dtype=F32) / n).reshape(NGROUPS, HEADS)
    dil = jnp.asarray(DILS, F32)
    bias = -slopes[:, :, None, None] * (delta.astype(F32)[None, None] * dil[:, None, None, None])
    return jnp.where(inband[None, None], bias, NEG)


def _attn_fwd(qkv_f, bias, nb, name):
    S = qkv_f.shape[0]
    nblk = S // QB
    scale = HD ** -0.5

    def kern(q_ref, k2_ref, kp_ref, v2_ref, vp_ref, b_ref, o_ref, l_ref, s_scr, p_scr, r_scr):
        s_id = pl.program_id(0)
        col = lax.broadcasted_iota(jnp.int32, (QB, 2 * QB), 1)
        lane = lax.broadcasted_iota(jnp.int32, (QB, HD), 1)

        def keys(sub, cur2_ref, prev_ref, sl):
            if sub:
                return cur2_ref[:, sl]
            return jnp.concatenate([prev_ref[:, sl], cur2_ref[0:QB, sl]], axis=0)

        for sub in range(2):
            for h in range(HEADS):
                sl = slice(h * HD, (h + 1) * HD)
                s_scr[sub * HEADS + h] = lax.dot_general(
                    q_ref[sub * QB:(sub + 1) * QB, sl], keys(sub, k2_ref, kp_ref, sl), (((1,), (1,)), ((), ())),
                    preferred_element_type=F32)
        for sub in range(2):
            has_prev = jnp.bitwise_and(2 * s_id + sub, nb - 1) != 0
            dead = jnp.logical_and(col < QB, jnp.logical_not(has_prev))
            lse_all = jnp.zeros((QB, HD), F32)
            for h in range(HEADS):
                u = sub * HEADS + h
                s = s_scr[u] * scale + b_ref[h]
                s = jnp.where(dead, NEG, s)
                m = jnp.max(s, axis=-1, keepdims=True)
                p = jnp.exp(s - m)
                den = jnp.sum(p, axis=-1, keepdims=True)
                p_scr[u] = p.astype(BF16)
                r_scr[u] = jnp.broadcast_to(1.0 / den, (QB, HD))
                lse_all = jnp.where(lane == h, m + jnp.log(den), lse_all)
            l_ref[sub * QB:(sub + 1) * QB, :] = lse_all
        for sub in range(2):
            for h in range(HEADS):
                u = sub * HEADS + h
                sl = slice(h * HD, (h + 1) * HD)
                o = jnp.dot(p_scr[u], keys(sub, v2_ref, vp_ref, sl), preferred_element_type=F32) * r_scr[u]
                o_ref[sub * QB:(sub + 1) * QB, sl] = o.astype(BF16)

    def pair(colblk):
        return pl.BlockSpec((2 * QB, D), lambda s: (s, colblk))

    def prev(colblk):
        return pl.BlockSpec((QB, D), lambda s: (jnp.maximum(2 * s - 1, 0), colblk))

    return pl.pallas_call(
        kern, grid=(nblk // 2,),
        in_specs=[pair(0), pair(1), prev(1), pair(2), prev(2), pl.BlockSpec((HEADS, QB, 2 * QB), lambda s: (0, 0, 0))],
        out_specs=[pl.BlockSpec((2 * QB, D), lambda s: (s, 0)), pl.BlockSpec((2 * QB, HD), lambda s: (s, 0))],
        out_shape=[jax.ShapeDtypeStruct((S, D), BF16), jax.ShapeDtypeStruct((S, HD), F32)],
        scratch_shapes=[pltpu.VMEM((2 * HEADS, QB, 2 * QB), F32), pltpu.VMEM((2 * HEADS, QB, 2 * QB), BF16),
                        pltpu.VMEM((2 * HEADS, QB, HD), F32)],
        compiler_params=_cparams(1), name=name)(qkv_f, qkv_f, qkv_f, qkv_f, qkv_f, bias)


def _natural(ref, scr, tm):
    dil = ref.shape[0]
    for res in range(dil):
        _chunks_add_rows(scr, ref[res].astype(F32), res, tm // dil, dil, False)
    return _chunks_get(scr)


def _attn_merge(os, lses):
    S = os[0].shape[0]
    tm = 512

    def kern(o0, o1, o2, l0, l1, l2, om_ref, lm_ref, ls1, ls2, os1, os2):
        la = l0[...]
        lb = _natural(l1, ls1, tm)
        lc = _natural(l2, ls2, tm)
        m = jnp.maximum(jnp.maximum(la, lb), lc)
        e0, e1, e2 = jnp.exp(la - m), jnp.exp(lb - m), jnp.exp(lc - m)
        tot = e0 + e1 + e2
        lm_ref[...] = m + jnp.log(tot)
        w0, w1, w2 = e0 / tot, e1 / tot, e2 / tot
        for res in range(o1.shape[0]):
            _chunks_add_rows(os1, o1[res].astype(F32), res, tm // o1.shape[0], o1.shape[0], False)
        for res in range(o2.shape[0]):
            _chunks_add_rows(os2, o2[res].astype(F32), res, tm // o2.shape[0], o2.shape[0], False)
        for h in range(HEADS):
            sl = slice(h * HD, (h + 1) * HD)
            acc = w0[:, h:h + 1] * o0[:, sl].astype(F32) + w1[:, h:h + 1] * os1[h] + w2[:, h:h + 1] * os2[h]
            om_ref[:, sl] = acc.astype(BF16)

    def spec(a, c):
        if a.ndim == 2:
            return pl.BlockSpec((tm, c), lambda i: (i, 0))
        return pl.BlockSpec((a.shape[0], tm // a.shape[0], c), lambda i: (0, i, 0))

    return pl.pallas_call(
        kern, grid=(S // tm,),
        in_specs=[spec(a, D) for a in os] + [spec(a, HD) for a in lses],
        out_specs=[pl.BlockSpec((tm, D), lambda i: (i, 0)), pl.BlockSpec((tm, HD), lambda i: (i, 0))],
        out_shape=[jax.ShapeDtypeStruct((S, D), BF16), jax.ShapeDtypeStruct((S, HD), F32)],
        scratch_shapes=[pltpu.VMEM((1, tm, HD), F32), pltpu.VMEM((1, tm, HD), F32),
                        pltpu.VMEM((HEADS, tm, HD), F32), pltpu.VMEM((HEADS, tm, HD), F32)],
        compiler_params=_cparams(1), name="attn_merge")(*os, *lses)


def _attn_bwd_prep(dx, wo, o, lse, deps=()):
    S = o.shape[0]
    tm = 512
    dils = DILS[1:]
    nd = len(deps)

    def kern(dx_ref, w_ref, o_ref, l_ref, *rest):
        rest = rest[nd:]
        do_outs, l_outs, d_outs = rest[0:3], rest[3:5], rest[5:8]
        do_scr, l_scr, d_scr = rest[8:11]
        dov = lax.dot_general(dx_ref[...].astype(BF16), w_ref[...], (((1,), (1,)), ((), ())),
                              preferred_element_type=F32)
        lane = lax.broadcasted_iota(jnp.int32, (tm, HD), 1)
        acc = jnp.zeros((tm, HD), F32)
        for h in range(HEADS):
            sl = slice(h * HD, (h + 1) * HD)
            prod = dov[:, sl] * o_ref[:, sl].astype(F32)
            acc = jnp.where(lane == h, jnp.sum(prod, axis=-1, keepdims=True), acc)
        d_scr[0] = acc
        l_scr[0] = l_ref[...]
        _chunks_put(do_scr, dov)
        do_outs[0][...] = dov.astype(BF16)
        d_outs[0][...] = acc
        for j, dil in enumerate(dils):
            for res in range(dil):
                n = tm // dil
                do_outs[1 + j][res] = _chunks_rows(do_scr, res, n, dil).astype(BF16)
                l_outs[j][res] = _chunks_rows(l_scr, res, n, dil)
                d_outs[1 + j][res] = _chunks_rows(d_scr, res, n, dil)

    def nat(c):
        return pl.BlockSpec((tm, c), lambda i: (i, 0))

    def fol(dil, c):
        return pl.BlockSpec((dil, tm // dil, c), lambda i: (0, i, 0))

    def shapes(c, dt, with_natural):
        first = [jax.ShapeDtypeStruct((S, c), dt)] if with_natural else []
        return first + [jax.ShapeDtypeStruct((dil, S // dil, c), dt) for dil in dils]

    outs = pl.pallas_call(
        kern, grid=(S // tm,),
        in_specs=[nat(D), pl.BlockSpec((D, D), lambda i: (0, 0), pipeline_mode=pl.Buffered(1)), nat(D), nat(HD)]
        + [pl.BlockSpec(memory_space=pl.ANY)] * nd,
        out_specs=[nat(D)] + [fol(dil, D) for dil in dils] + [fol(dil, HD) for dil in dils]
        + [nat(HD)] + [fol(dil, HD) for dil in dils],
        out_shape=shapes(D, BF16, True) + shapes(HD, F32, False) + shapes(HD, F32, True),
        scratch_shapes=[pltpu.VMEM((HEADS, tm, HD), F32), pltpu.VMEM((1, tm, HD), F32), pltpu.VMEM((1, tm, HD), F32)],
        compiler_params=_cparams(1), name="attn_out_bwd")(dx, wo, o, lse, *deps)
    return outs[0:3], [lse] + list(outs[3:5]), outs[5:8]


def _attn_bwd(qkv_f, do_f, lse_f, delta_f, bias, nb, name):
    S = qkv_f.shape[0]
    nblk = S // QB
    scale = HD ** -0.5

    npair = nblk // 2

    def kern(q_ref, k2_ref, kp_ref, v2_ref, vp_ref, do_ref, l_ref, d_ref, b_ref, out_ref, dq_c, dk_c, dv_c,
             s_scr, dp_scr, ds_scr, p_scr):
        s_id = pl.program_id(0)

        @pl.when(s_id == 0)
        def _():
            dq_c[...] = jnp.zeros_like(dq_c)
            dk_c[...] = jnp.zeros_like(dk_c)
            dv_c[...] = jnp.zeros_like(dv_c)

        @pl.when(s_id == npair)
        def _():
            out_ref[:, 0:D] = dq_c[...].astype(BF16)
            out_ref[:, D:2 * D] = dk_c[...].astype(BF16)
            out_ref[:, 2 * D:3 * D] = dv_c[...].astype(BF16)

        def keys(sub, cur2_ref, prev_ref, sl):
            if sub:
                return cur2_ref[:, sl]
            return jnp.concatenate([prev_ref[:, sl], cur2_ref[0:QB, sl]], axis=0)

        @pl.when(s_id < npair)
        def _():
            col = lax.broadcasted_iota(jnp.int32, (QB, 2 * QB), 1)
            out_ref[:, 0:D] = dq_c[...].astype(BF16)
            for sub in range(2):
                rows = slice(sub * QB, (sub + 1) * QB)
                for h in range(HEADS):
                    sl = slice(h * HD, (h + 1) * HD)
                    u = sub * HEADS + h
                    s_scr[u] = lax.dot_general(q_ref[rows, sl], keys(sub, k2_ref, kp_ref, sl),
                                               (((1,), (1,)), ((), ())), preferred_element_type=F32)
                    dp_scr[u] = lax.dot_general(do_ref[rows, sl], keys(sub, v2_ref, vp_ref, sl),
                                                (((1,), (1,)), ((), ())), preferred_element_type=F32)
            for sub in range(2):
                rows = slice(sub * QB, (sub + 1) * QB)
                has_prev = jnp.bitwise_and(2 * s_id + sub, nb - 1) != 0
                dead = jnp.logical_and(col < QB, jnp.logical_not(has_prev))
                lv = l_ref[rows, :]
                dv_ = d_ref[rows, :]
                for h in range(HEADS):
                    u = sub * HEADS + h
                    s = s_scr[u] * scale + b_ref[h]
                    s = jnp.where(dead, NEG, s)
                    p = jnp.exp(s - lv[:, h:h + 1])
                    ds_scr[u] = (p * (dp_scr[u] - dv_[:, h:h + 1]) * scale).astype(BF16)
                    p_scr[u] = p.astype(BF16)
            for h in range(HEADS):
                sl = slice(h * HD, (h + 1) * HD)
                parts = []
                for sub in range(2):
                    rows = slice(sub * QB, (sub + 1) * QB)
                    u = sub * HEADS + h
                    ds = ds_scr[u]
                    dq_c[rows, sl] = jnp.dot(ds, keys(sub, k2_ref, kp_ref, sl), preferred_element_type=F32)
                    dkk = lax.dot_general(ds, q_ref[rows, sl], (((0,), (0,)), ((), ())), preferred_element_type=F32)
                    dvv = lax.dot_general(p_scr[u], do_ref[rows, sl], (((0,), (0,)), ((), ())),
                                          preferred_element_type=F32)
                    parts.append((dkk, dvv))
                for which, carry, base in ((0, dk_c, D), (1, dv_c, 2 * D)):
                    first, second = parts[0][which], parts[1][which]
                    cols = slice(base + h * HD, base + (h + 1) * HD)
                    out_ref[0:QB, cols] = carry[0:QB, sl].astype(BF16)
                    out_ref[QB:2 * QB, cols] = (carry[QB:2 * QB, sl] + first[:QB]).astype(BF16)
                    carry[0:QB, sl] = first[QB:] + second[:QB]
                    carry[QB:2 * QB, sl] = second[QB:]

    last = npair - 1

    def pair(colblk, c):
        return pl.BlockSpec((2 * QB, c), lambda s: (jnp.minimum(s, last), colblk))

    def prev(colblk):
        return pl.BlockSpec((QB, D), lambda s: (jnp.maximum(2 * jnp.minimum(s, last) - 1, 0), colblk))

    return pl.pallas_call(
        kern, grid=(npair + 1,),
        in_specs=[pair(0, D), pair(1, D), prev(1), pair(2, D), prev(2), pair(0, D), pair(0, HD), pair(0, HD),
                  pl.BlockSpec((HEADS, QB, 2 * QB), lambda s: (0, 0, 0))],
        out_specs=pl.BlockSpec((2 * QB, 3 * D), lambda s: (jnp.maximum(s - 1, 0), 0)),
        out_shape=jax.ShapeDtypeStruct((S, 3 * D), BF16),
        scratch_shapes=[pltpu.VMEM((2 * QB, D), F32), pltpu.VMEM((2 * QB, D), F32), pltpu.VMEM((2 * QB, D), F32),
                        pltpu.VMEM((2 * HEADS, QB, 2 * QB), F32), pltpu.VMEM((2 * HEADS, QB, 2 * QB), F32),
                        pltpu.VMEM((2 * HEADS, QB, 2 * QB), BF16), pltpu.VMEM((2 * HEADS, QB, 2 * QB), BF16)],
        compiler_params=_cparams(1), name=name)(qkv_f, qkv_f, qkv_f, qkv_f, qkv_f, do_f, lse_f, delta_f, bias)


def _local_step(x, tgt, comm, attn_norm, ffn_norm, final_norm):
    S = x.shape[0]
    bias = _bias_table()
    g_attn = attn_norm.reshape(1, D)
    g_f0 = ffn_norm[0:1]
    g_f1 = ffn_norm[1:2]
    g_fin = final_norm.reshape(1, D)
    W = {}

    def ffn_fwd(xin, h, l, next_gain, target=None):
        return _ffn_fwd(h, W[f"gu{l}"], W[f"d{l}"], xin, next_gain, tgt=target, name=f"ffn_fwd{l}")

    def ffn_bwd(dxo, xin, gain, h, gu, act, l, rs_group):
        dgu, dxin, dgain = _ffn_bwd(dxo, W[f"d{l}"], W[f"gu{l}"], gu, xin, gain, f"ffn_bwd{l}")
        gw_d = _mm(act, dxo, mode="tn", M=DFF, N=D, K=S, tm=HCH, tn=D, tk=2048, out_dtype=BF16, name=f"gw_d{l}")
        gw_gu = _mm(dgu, h, mode="tn", M=2 * DFF, N=D, K=S, tm=HCH, tn=D, tk=2048, out_dtype=BF16, name=f"gw_gu{l}")
        return dxin, dgain, comm.send_grads(rs_group, {f"d{l}": gw_d, f"gu{l}": gw_gu})

    nbs = [S // QB // dil for dil in DILS]
    hf = _rms_fwd_folded(x, g_attn, "rms_attn", deps=comm.ag_tokens)
    hf = [h.reshape(S, D) for h in hf]
    W.update(comm.weights(0, hf[0]))
    qkv_f, o_f, lse_f = [], [], []
    for g, dil in enumerate(DILS):
        qkv_f.append(_mm(hf[g], W["qkv"], mode="nt", M=S, N=3 * D, K=D, tm=2048, tn=1024, tk=D, out_dtype=BF16,
                         b_off=(3 * g, 0), name=f"qkv_proj{g}"))
        og, lg = _attn_fwd(qkv_f[g], bias[g], nbs[g], f"attn_fwd{g}")
        o_f.append(og if dil == 1 else og.reshape(dil, S // dil, D))
        lse_f.append(lg if dil == 1 else lg.reshape(dil, S // dil, HD))
    passing = [comm.pass_on(1, tuple(o_f)), comm.pass_on(2, tuple(o_f))]
    (o_f, lse_f), passing = lax.optimization_barrier(((o_f, lse_f), passing))
    o, lse = _attn_merge(o_f, lse_f)
    W.update(comm.weights(1, (o, passing[0])))
    x1, h1 = _mm_res_norm(o, W["wo"], x, g_f0, K=D, tm=1024, name="attn_out")
    pv = W["pv"].reshape(NDEV, 8, 128)
    pool_norm, pool_scale = pv[:, 0, :].reshape(1, D), pv[:, 1, :].reshape(1, D)
    gu0, act0, (x2, h2) = ffn_fwd(x1, h1, 0, pool_norm)

    W.update(comm.weights(2, (x2, passing[1])))
    yd = _pool_in_window(h2, W["wpi"])
    x3 = _pool_out(yd, W["pg"], pool_scale, x2)
    h3 = _rms_fwd(x3, g_f1, "rms_ffn1")
    gu1, act1, (dx4, d_fin, lossvec) = ffn_fwd(x3, h3, 1, g_fin, target=tgt)

    dx3, d_f1, token = ffn_bwd(dx4, x3, g_f1, h3, gu1, act1, 1, 0)
    du, d_scale, gw_pg = _pool_out_bwd(dx3, yd, W["pg"], pool_scale, deps=(token,))
    gw_pi = _mm(h2, du, mode="tn", M=D, N=D, K=S, tm=D, tn=D, tk=1024, out_dtype=BF16, name="gw_pi")
    token = comm.send_grads(1, {"pg": gw_pg, "wpi": gw_pi})
    dx2, d_pool = _mm_rms_bwd(du, W["wpi"], x2, pool_norm, dx3, mode="nt", M=S, K=D, tm=1024, deps=(token,),
                              name="pool_in_bwd")
    dx1, d_f0, token = ffn_bwd(dx2, x1, g_f0, h1, gu0, act0, 0, 2)

    gw_o = _mm(o, dx1, mode="tn", M=D, N=D, K=S, tm=D, tn=D, tk=2048, out_dtype=BF16, deps=(token,), name="gw_o")
    do_f, lse_ff, delta_f = _attn_bwd_prep(dx1, W["wo"], o, lse, deps=(token,))
    dqkv_f, gw_qkv = [], None
    for g in range(NGROUPS):
        dqkv_f.append(_attn_bwd(qkv_f[g], do_f[g].reshape(S, D), lse_ff[g].reshape(S, HD),
                                delta_f[g].reshape(S, HD), bias[g], nbs[g], f"attn_bwd{g}"))
        gw_qkv = _mm(dqkv_f[g], hf[g], mode="tn", M=3 * D, N=D, K=S, tm=1024, tn=D, tk=S, out_dtype=BF16,
                     out_rows=NGROUPS * 3 * D, out_off=3 * g, out_prev=gw_qkv, name=f"gw_qkv{g}")
    token = comm.send_grads_pairwise({"wo": gw_o, "qkv": gw_qkv})
    folded = []
    for g in reversed(range(1, NGROUPS)):
        dh0_g = _mm(dqkv_f[g], W["qkv"], mode="nn", M=S, N=D, K=3 * D, tm=1024, tn=D, tk=3 * D, out_dtype=F32,
                    b_off=(g, 0), deps=(token,), name=f"qkv_proj_bwd{g}")
        folded.append(dh0_g.reshape(DILS[g], S // DILS[g], D))
        if g == NGROUPS - 1:
            token = comm.pass_grads(dh0_g)
    grad_x, d_attn = _mm_rms_bwd(dqkv_f[0], W["qkv"], x, g_attn, dx1, mode="nn", M=S, K=3 * D, tm=512,
                                 deps=(token,), folded=folded, name="qkv_proj_bwd0")

    vec = jnp.concatenate([d_attn, d_f0, d_f1, d_fin, d_pool, d_scale, lossvec, jnp.zeros((1, D), F32)], axis=0)
    return grad_x, vec


def _mesh_pos():
    x, y, c = lax.axis_index("x"), lax.axis_index("y"), lax.axis_index("c")
    return x, y, c, 4 * x + 2 * y + c


def _peer(x, y, c, k):
    kx, ky, kc = (k >> 2) & 1, (k >> 1) & 1, k & 1
    px = 1 - x if kx else x
    py = 1 - y if ky else y
    pc = 1 - c if kc else c
    return (px, py, pc), 4 * px + 2 * py + pc


ANY = pl.BlockSpec(memory_space=pl.ANY)


HBM = pl.BlockSpec(memory_space=pltpu.HBM)
SEMS = pl.BlockSpec(memory_space=pltpu.SEMAPHORE)
EFFECT = pltpu.SideEffectType.DATAFLOW_SIDE_EFFECTING

AG_GROUPS = (("qkv",), ("wo", "gu0", "d0", "pv"), ("wpi", "pg", "gu1", "d1"))
AG_ORDER = tuple(n for grp in AG_GROUPS for n in grp)
RS_GROUPS = (("d1", "gu1"), ("pg", "wpi"), ("d0", "gu0"), ("wo", "qkv"))


def _hbm(a):
    return pltpu.with_memory_space_constraint(a, pltpu.HBM)


def _remote(src, dst, send, recv, peer):
    return pltpu.make_async_remote_copy(src_ref=src, dst_ref=dst, send_sem=send, recv_sem=recv, device_id=peer,
                                        device_id_type=pl.DeviceIdType.MESH)


ALL_KS = tuple(range(1, NDEV))
AG_KS1 = (1, 2, 4, 6)
AG_KS2 = (2, 4, 6)
RS_KS_PAIR = (1, 3, 5, 7)
RS_KS_CHIPS = (2, 4, 6)


def _split_start(srcs, src_of, lands, copy_refs, name, deps=(), ks=ALL_KS, to=None):
    ns, n, nd, nk = len(srcs), len(lands), len(deps), len(ks)

    def body(*refs):
        ins, land = refs[:ns], refs[ns:ns + n]
        send, recv = refs[ns + n + nd], refs[ns + n + nd + 1]
        token = refs[-1]
        x, y, c, me = _mesh_pos()
        for j in range(n):
            for i, k in enumerate(ks):
                _, pid = _peer(x, y, c, k)
                dest, _ = _peer(x, y, c, k if to is None else to)
                src, dst = copy_refs(j, (land[j] if src_of[j] is None else ins[src_of[j]]), land[j], me, pid, i)
                _remote(src, dst, send.at[j * nk + i], recv.at[j * nk + i], dest).start()
        token[...] = jnp.zeros_like(token)

    outs = pl.pallas_call(
        body, name=name,
        out_shape=(pltpu.SemaphoreType.DMA((n * nk,)), pltpu.SemaphoreType.DMA((n * nk,)))
        + tuple(pltpu.HBM(a.shape, a.dtype) for a in srcs) + tuple(pltpu.HBM(a.shape, a.dtype) for a in lands)
        + (jax.ShapeDtypeStruct((8, 128), F32),),
        in_specs=(HBM,) * (ns + n) + (ANY,) * nd,
        out_specs=(SEMS, SEMS) + (HBM,) * (ns + n) + (pl.BlockSpec(memory_space=pltpu.VMEM),),
        input_output_aliases={i: 2 + i for i in range(ns + n)},
        compiler_params=pltpu.CompilerParams(has_side_effects=EFFECT),
    )(*[_hbm(a) for a in srcs], *[_hbm(a) for a in lands], *deps)
    return outs[0], outs[1], list(outs[2:2 + ns]), list(outs[2 + ns:2 + ns + n]), outs[-1]


def _split_wait(srcs, src_of, lands, send, recv, sem_rows, wait_refs, after, name, ks=ALL_KS):
    ns, n, nk = len(srcs), len(lands), len(ks)
    after = tuple(after) if isinstance(after, (tuple, list)) else (after,)

    def body(*refs):
        ins, land = refs[:ns], refs[ns:ns + n]
        send_ref, recv_ref = refs[ns + n], refs[ns + n + 1]
        x, y, c, me = _mesh_pos()
        for j in range(n):
            for i, k in enumerate(ks):
                peer, _ = _peer(x, y, c, k)
                src, dst = wait_refs(j, (land[j] if src_of[j] is None else ins[src_of[j]]), land[j])
                sem = sem_rows[j] * nk + i
                cp = _remote(src, dst, send_ref.at[sem], recv_ref.at[sem], peer)
                cp.wait_send()
                cp.wait_recv()

    outs = pl.pallas_call(
        body, name=name,
        out_shape=tuple(pltpu.HBM(a.shape, a.dtype) for a in srcs) + tuple(pltpu.HBM(a.shape, a.dtype) for a in lands),
        in_specs=(HBM,) * (ns + n) + (SEMS, SEMS) + (ANY,) * len(after),
        out_specs=(HBM,) * (ns + n),
        input_output_aliases={i: i for i in range(ns + n)},
        compiler_params=pltpu.CompilerParams(has_side_effects=EFFECT),
    )(*srcs, *lands, send, recv, *after)
    return list(outs[:ns]), list(outs[ns:])


def _ag_dtype(name):
    return F32 if name == "pv" else BF16


def _ag_align(name):
    return 8 if name == "pv" else 16


def _place_transposed(w, me, name):
    rows = w.shape[1]
    nblk = rows // 128

    def kern(me_ref, w_ref, o_ref):
        o_ref[...] = w_ref[...].T.astype(BF16)

    grid_spec = pltpu.PrefetchScalarGridSpec(
        num_scalar_prefetch=1, grid=(nblk,),
        in_specs=[pl.BlockSpec((D, 128), lambda i, me_ref: (0, i))],
        out_specs=pl.BlockSpec((128, D), lambda i, me_ref: (me_ref[0] * nblk + i, 0)))
    return pl.pallas_call(
        kern, grid_spec=grid_spec, out_shape=jax.ShapeDtypeStruct((NDEV * rows, D), BF16),
        compiler_params=_cparams(1), name=name)(me.reshape(1).astype(jnp.int32), w)


class _Comm:
    def __init__(self, params, make_shards, me, placed):
        self.me = me
        self.ag_land, self.ag_sems, self.ag_tokens, self.ag_passing = {}, {}, (), {}
        self.rs = []
        deps = ()
        for part, names in enumerate((AG_GROUPS[0], AG_ORDER[len(AG_GROUPS[0]):])):
            rows = [SEC_ROWS[n] for n in names]
            if part == 0:
                lands = [placed[n] for n in names]
            else:
                params, deps = lax.optimization_barrier((params, deps))
                shards = make_shards(*params)
                lands = [lax.dynamic_update_slice(lax.empty((NDEV * r, shards[n].shape[1]), _ag_dtype(n)),
                                                  shards[n].astype(_ag_dtype(n)), (_shard_pos(n, me), 0))
                         for n, r in zip(names, rows)]

            def copy_refs(j, src, land, me, pid, i, names=names, rows=rows):
                own = land.at[pl.ds(pl.multiple_of(_shard_pos(names[j], me), _ag_align(names[j])), rows[j])]
                return own, own

            send, recv, _, lands, token = _split_start([], [None] * len(names), lands, copy_refs, f"ag_start{part}",
                                                       deps=deps, ks=AG_KS1)
            deps = (token,)
            self.ag_tokens += (token,)
            for j, n in enumerate(names):
                self.ag_land[n] = lands[j]
                self.ag_sems[n] = (send, recv, j)

    def pass_on(self, group, after):
        names = AG_GROUPS[group]
        send, recv = self.ag_sems[names[0]][:2]
        idx = [self.ag_sems[n][2] for n in names]
        rows = [SEC_ROWS[n] for n in names]
        none = [None] * len(names)

        def wait_refs(j, src, land):
            return land.at[pl.ds(0, rows[j])], land.at[pl.ds(0, rows[j])]

        _, lands = _split_wait([], none, [self.ag_land[n] for n in names], send, recv, idx,
                               wait_refs, after, f"ag_wait{group}", ks=AG_KS1)

        def copy_refs(j, src, land, me, pid, i):
            theirs = land.at[pl.ds(pl.multiple_of(_shard_pos(names[j], pid), _ag_align(names[j])), rows[j])]
            return theirs, theirs

        send, recv, _, lands, token = _split_start([], none, lands, copy_refs, f"ag_pass{group}", ks=AG_KS2, to=1)
        self.ag_passing[group] = (send, recv, lands, wait_refs)
        return token

    def weights(self, group, after):
        names = AG_GROUPS[group]
        if group not in self.ag_passing:
            after = self.pass_on(group, after)
        send, recv, lands, wait_refs = self.ag_passing[group]
        _, lands = _split_wait([], [None] * len(names), lands, send, recv, list(range(len(names))), wait_refs, after,
                               f"ag_pass_wait{group}", ks=AG_KS2)
        return dict(zip(names, lands))

    def send_grads(self, group, gws):
        names = RS_GROUPS[group]
        rows = [SEC_ROWS[n] for n in names]
        grads = [gws[n] for n in names]
        me = self.me
        lands = [lax.dynamic_update_slice(
            lax.empty((NDEV, r, D), BF16),
            lax.dynamic_slice(g, (_shard_pos(n, me), 0), (r, D))[None], (me, 0, 0))
            for n, r, g in zip(names, rows, grads)]

        def copy_refs(j, src, land, me, pid, i):
            return src.at[pl.ds(pl.multiple_of(_shard_pos(names[j], pid), 16), rows[j])], land.at[me]

        send, recv, srcs, lands, token = _split_start(grads, list(range(len(names))), lands, copy_refs,
                                                      f"rs_start{group}")
        self.rs.append((names, rows, send, recv, srcs, lands, ALL_KS))
        return token

    def send_grads_pairwise(self, gws):
        names = RS_GROUPS[-1]
        rows = [SEC_ROWS[n] for n in names]
        grads = [gws[n] for n in names]
        idx = list(range(len(names)))
        lands = [lax.empty((len(RS_KS_PAIR), r, D), BF16) for r in rows]

        def copy_refs(j, src, land, me, pid, i):
            return src.at[pl.ds(pl.multiple_of(_shard_pos(names[j], pid), 16), rows[j])], land.at[i]

        send, recv, srcs, lands, token = _split_start(grads, idx, lands, copy_refs, "rs_pair_start",
                                                      ks=RS_KS_PAIR, to=1)
        self.pair = (names, rows, send, recv, srcs, lands)
        return token

    def pass_grads(self, after):
        names, rows, send, recv, srcs, lands = self.pair
        idx = list(range(len(names)))
        me = self.me

        def wait_refs(j, src, land):
            return src.at[pl.ds(0, rows[j])], land.at[0]

        srcs, lands = _split_wait(srcs, idx, lands, send, recv, idx, wait_refs, after, "rs_pair_wait", ks=RS_KS_PAIR)
        sums = [_pair_sum(g, got, me, f"rs_pair_sum_{n}") for n, g, got in zip(names, srcs, lands)]
        lands = [lax.dynamic_update_slice(lax.empty(p.shape, BF16), p[0:1], (0, 0, 0)) for p in sums]

        def copy_refs(j, src, land, me, pid, i):
            return src.at[i + 1], land.at[i + 1]

        send, recv, sums, lands, token = _split_start(sums, idx, lands, copy_refs, f"rs_start{len(RS_GROUPS) - 1}",
                                                      ks=RS_KS_CHIPS)
        self.rs.append((names, rows, send, recv, sums, lands, RS_KS_CHIPS))
        return token

    def received(self, group, after):
        names, rows, send, recv, srcs, lands, ks = self.rs[group]
        whole = srcs[0].ndim == 2

        def wait_refs(j, src, land):
            return (src.at[pl.ds(0, rows[j])] if whole else src.at[0]), land.at[0]

        _, lands = _split_wait(srcs, list(range(len(names))), lands, send, recv, list(range(len(names))), wait_refs,
                               after, f"rs_wait{group}", ks=ks)
        return dict(zip(names, lands))


def _pair_sum(grad, got, me, name):
    n, rows, _ = got.shape
    tr = 384 if rows % 384 == 0 else rows
    nt = rows // tr

    def kern(me_ref, g_ref, b_ref, o_ref):
        o_ref[0] = (g_ref[...].astype(F32) + b_ref[0].astype(F32)).astype(BF16)

    blk = pl.BlockSpec((1, tr, D), lambda i, t, me_ref: (i, t, 0))
    grid_spec = pltpu.PrefetchScalarGridSpec(
        num_scalar_prefetch=1, grid=(n, nt),
        in_specs=[pl.BlockSpec((tr, D), lambda i, t, me_ref: (jnp.bitwise_xor(me_ref[0], 2 * i) * nt + t, 0)), blk],
        out_specs=blk)
    return pl.pallas_call(
        kern, grid_spec=grid_spec, out_shape=jax.ShapeDtypeStruct(got.shape, BF16),
        compiler_params=_cparams(2), name=name)(me.reshape(1).astype(jnp.int32), grad, got)


def _sum_contributions(r_ref):
    g = r_ref[0].astype(F32)
    for slot in range(1, r_ref.shape[0]):
        g = g + r_ref[slot].astype(F32)
    return g


def _adam_math(g, w, m, v):
    c1 = 1.0 / (1.0 - ADAM_B1 ** ADAM_STEP)
    c2 = 1.0 / (1.0 - ADAM_B2 ** ADAM_STEP)
    mn = ADAM_B1 * m + (1.0 - ADAM_B1) * g
    vn = ADAM_B2 * v + (1.0 - ADAM_B2) * (g * g)
    return -ADAM_LR * ((mn * c1) / (jnp.sqrt(vn * c2) + ADAM_EPS) + ADAM_WD * w), mn, vn


def _adamw(R, w, m, v, *, tr, name, layer=None, prev=None):
    rows, C = w.shape[-2:]
    nprev = 0 if prev is None else 4

    def kern(r_ref, w_ref, m_ref, v_ref, *rest):
        g_out, d_out, m_out, v_out = rest[nprev:]
        g = _sum_contributions(r_ref)
        g_out[...] = g
        d_out[...], m_out[...], v_out[...] = _adam_math(g, w_ref[...], m_ref[...], v_ref[...])

    if layer is None:
        tile = pl.BlockSpec((tr, C), lambda i: (i, 0))
    else:
        tile = pl.BlockSpec((None, tr, C), lambda i: (layer, i, 0))
    shp = jax.ShapeDtypeStruct(w.shape, F32)
    return pl.pallas_call(
        kern, grid=(rows // tr,),
        in_specs=[pl.BlockSpec((R.shape[0], tr, C), lambda i: (0, i, 0)), tile, tile, tile]
        + [pl.BlockSpec(memory_space=pl.ANY)] * nprev,
        out_specs=[tile] * 4, out_shape=[shp] * 4,
        input_output_aliases={4 + k: k for k in range(nprev)},
        compiler_params=_cparams(1), name=name)(R, w, m, v, *(prev or ()))


def _adamw_pool_group(R, w, m, v):
    rows = SEC_ROWS["pg"]

    def kern(r_ref, w_ref, m_ref, v_ref, g_out, d_out, m_out, v_out):
        g = _sum_contributions(r_ref)
        g_out[0] = g
        d_out[0], m_out[0], v_out[0] = _adam_math(g, w_ref[0], m_ref[0], v_ref[0])

    blk = pl.BlockSpec((1, rows, PGD), lambda i: (i, 0, 0))
    shp = jax.ShapeDtypeStruct((POOL_G, rows, PGD), F32)
    return pl.pallas_call(
        kern, grid=(POOL_G,),
        in_specs=[pl.BlockSpec((NDEV, rows, PGD), lambda i: (0, 0, i)), blk, blk, blk],
        out_specs=[blk] * 4, out_shape=[shp] * 4, compiler_params=_cparams(1), name="adamw_pg")(R, w, m, v)


def _adamw_transposed(R, w, m, v, name):
    rows = R.shape[1]
    tr = 128

    def kern(r_ref, w_ref, m_ref, v_ref, g_out, d_out, m_out, v_out):
        g = _sum_contributions(r_ref).T
        g_out[...] = g
        d_out[...], m_out[...], v_out[...] = _adam_math(g, w_ref[...], m_ref[...], v_ref[...])

    tile = pl.BlockSpec((D, tr), lambda i: (0, i))
    shp = jax.ShapeDtypeStruct((D, rows), F32)
    return pl.pallas_call(
        kern, grid=(rows // tr,),
        in_specs=[pl.BlockSpec((R.shape[0], tr, D), lambda i: (0, i, 0)), tile, tile, tile],
        out_specs=[tile] * 4, out_shape=[shp] * 4, compiler_params=_cparams(1), name=name)(R, w, m, v)


def _pack_sections(w_qkv, w_attn_out, w_pool_in, w_pool_group, w_ffn_gate_up, w_ffn_down):
    pg = w_pool_group[0].transpose(1, 0, 2).reshape(SEC_ROWS["pg"], D)
    return {"qkv": w_qkv[0].T, "wo": w_attn_out[0], "wpi": w_pool_in[0], "gu0": w_ffn_gate_up[0].T,
            "gu1": w_ffn_gate_up[1].T, "d0": w_ffn_down[0], "d1": w_ffn_down[1], "pg": pg}


def _vec_pack(attn_norm, ffn_norm, final_norm, pool_norm_sh, pool_scale_sh, me):
    def place(sh):
        return lax.dynamic_update_slice(jnp.zeros((1, D), F32), sh, (0, me * 128))
    return jnp.concatenate([attn_norm, ffn_norm, final_norm.reshape(1, D), place(pool_norm_sh),
                            place(pool_scale_sh), jnp.zeros((2, D), F32)], axis=0)


def _vec_unpack(p, me):
    def take(r):
        return lax.dynamic_slice(p[r:r + 1], (0, me * 128), (1, 128))
    return p[0:1], p[1:3], p[3], take(4), take(5)


def kernel(x, attn_norm, w_qkv, w_attn_out, pool_norm, w_pool_in, w_pool_group, pool_scale, ffn_norm, w_ffn_gate_up, w_ffn_down, final_norm, loss_target, m_attn_norm, m_w_qkv, m_w_attn_out, m_pool_norm, m_w_pool_in, m_w_pool_group, m_pool_scale, m_ffn_norm, m_w_ffn_gate_up, m_w_ffn_down, m_final_norm, v_attn_norm, v_w_qkv, v_w_attn_out, v_pool_norm, v_w_pool_in, v_w_pool_group, v_pool_scale, v_ffn_norm, v_w_ffn_gate_up, v_w_ffn_down, v_final_norm):
    me = 4 * lax.axis_index("x") + 2 * lax.axis_index("y") + lax.axis_index("c")

    def make_shards(wq, wo, wpi, wpg, wgu, wd, pn, ps):
        shards = _pack_sections(wq, wo, wpi, wpg, wgu, wd)
        shards["pv"] = jnp.concatenate([pn, ps, jnp.zeros((6, 128), F32)], axis=0)
        return shards

    comm = _Comm((w_qkv, w_attn_out, w_pool_in, w_pool_group, w_ffn_gate_up, w_ffn_down, pool_norm, pool_scale),
                 make_shards, me, placed={"qkv": _place_transposed(w_qkv[0], me, "place_qkv")})

    grad_x, vec = _local_step(x[0], loss_target[0], comm, attn_norm, ffn_norm, final_norm)

    small = ((attn_norm, ffn_norm, final_norm, pool_norm, pool_scale),
             (m_attn_norm, m_ffn_norm, m_final_norm, m_pool_norm, m_pool_scale),
             (v_attn_norm, v_ffn_norm, v_final_norm, v_pool_norm, v_pool_scale))
    small, grad_x = lax.optimization_barrier((small, grad_x))
    vw, vm, vv = (_vec_pack(*s, me) for s in small)

    gu_t = [jnp.swapaxes(a, 1, 2) for a in (w_ffn_gate_up, m_w_ffn_gate_up, v_w_ffn_gate_up)]
    res = {}
    gu_res, d_res = None, None
    vec_out = None
    vec_land = lax.dynamic_update_slice(lax.empty((NDEV, 8, D), F32), vec[None], (me, 0, 0))
    vec_sems = _split_start([vec], [0], [vec_land], lambda j, src, land, me_, pid, i: (src, land.at[me_]),
                            "vec_start")
    after = (grad_x, vec_sems[4])
    for group in range(len(RS_GROUPS)):
        if group == len(RS_GROUPS) - 1:
            _, (VR,) = _split_wait(vec_sems[2], [0], vec_sems[3], vec_sems[0], vec_sems[1], [0],
                                   lambda j, src, land: (src, land.at[0]), after, "vec_wait")
            vec_out = _adamw(VR, vw, vm, vv, tr=8, name="adamw_vec")
            after = vec_out[0]
        for n, R in comm.received(group, after).items():
            if n in ("d0", "d1"):
                d_res = _adamw(R, w_ffn_down, m_w_ffn_down, v_w_ffn_down, tr=352, name=f"adamw_{n}",
                               layer=int(n[1]), prev=d_res)
                after = d_res[0]
            elif n in ("gu0", "gu1"):
                gu_res = _adamw(R, *gu_t, tr=352, name=f"adamw_{n}", layer=int(n[2]), prev=gu_res)
                after = gu_res[0]
            elif n == "pg":
                out = _adamw_pool_group(R, w_pool_group[0], m_w_pool_group[0], v_w_pool_group[0])
                res["pg"] = tuple(a[None] for a in out)
                after = out[0]
            elif n in ("wo", "wpi"):
                w, m, v = ((w_attn_out, m_w_attn_out, v_w_attn_out) if n == "wo"
                           else (w_pool_in, m_w_pool_in, v_w_pool_in))
                res[n] = _adamw(R, w[0], m[0], v[0], tr=128, name=f"adamw_{n}")
                res[n] = tuple(a[None] for a in res[n])
                after = res[n][0]
            else:
                out = _adamw_transposed(R, w_qkv[0], m_w_qkv[0], v_w_qkv[0], "adamw_qkv")
                res["qkv"] = tuple(a[None] for a in out)
                after = out[0]
    res["gu"] = tuple(jnp.swapaxes(a, 1, 2) for a in gu_res)
    res["d"] = tuple(d_res)

    outs = []
    for kind in range(4):
        an, fn, fin, pn, ps = _vec_unpack(vec_out[kind], me)
        outs.append((an, res["qkv"][kind], res["wo"][kind], pn, res["wpi"][kind], res["pg"][kind], ps, fn,
                     res["gu"][kind], res["d"][kind], fin))
    loss = 0.5 * jnp.sum(vec_out[0][6]) / D
    return (loss, grad_x[None]) + outs[0] + outs[1] + outs[2] + outs[3]
```
